```python
import jax, jax.numpy as jnp
from jax import lax
import numpy as np

D_MODEL = 1024
BATCH = 16
SEQ = 2048
DEPTH = 1

LRU_WIDTH = D_MODEL
LRU_BLOCKS = 16
LRU_BW = LRU_WIDTH // LRU_BLOCKS
CONV_WIDTH = 4
LRU_C = 8.0
RET_HEADS = 4
RET_DK = 256
RET_DV = 256
RET_QK_WIDTH = RET_HEADS * RET_DK
RET_WIDTH = RET_HEADS * RET_DV
CHUNK = 128
ROPE_THETA = 10000.0
EPS = 1e-6
IN_SIZES = (LRU_WIDTH, LRU_WIDTH, RET_QK_WIDTH, RET_QK_WIDTH, RET_WIDTH, RET_WIDTH, D_MODEL, D_MODEL)
IN_COLS = sum(IN_SIZES)
SPLIT_POINTS = tuple(int(c) for c in np.cumsum(IN_SIZES)[:-1])

kernel_name = "hybrid_rglru_retention_gated_block"


def rmsnorm(x, g):
    xf = x.astype(jnp.float32)
    y = xf * lax.rsqrt(jnp.mean(xf * xf, axis=-1, keepdims=True) + EPS)
    return (y * g.astype(jnp.float32)).astype(x.dtype)


def causal_depthwise_conv(x, w, b):
    S = x.shape[1]
    xp = jnp.pad(x, ((0, 0), (CONV_WIDTH - 1, 0), (0, 0)))
    y = b
    for k in range(CONV_WIDTH):
        y = y + xp[:, k:k + S, :] * w[k]
    return y


def block_diag_linear(x, w, b):
    B, S, W = x.shape
    xb = x.reshape(B, S, LRU_BLOCKS, LRU_BW)
    return jnp.einsum('bsnk,nkj->bsnj', xb, w).reshape(B, S, W) + b


def rg_lru(x, wx, bx, wa, ba, lam):
    B, S, W = x.shape
    i_t = jax.nn.sigmoid(block_diag_linear(x, wx, bx))
    r_t = jax.nn.sigmoid(block_diag_linear(x, wa, ba))
    log_a = -LRU_C * r_t.astype(jnp.float32) * jax.nn.softplus(-lam.astype(jnp.float32))
    a = jnp.exp(log_a)
    mult = jnp.sqrt(-jnp.expm1(2.0 * log_a))
    u = mult * (i_t * x).astype(jnp.float32)

    def step(h, au):
        a_t, u_t = au
        h = a_t * h + u_t
        return h, h

    _, hs = lax.scan(step, jnp.zeros((B, W), jnp.float32),
                     (jnp.swapaxes(a, 0, 1), jnp.swapaxes(u, 0, 1)))
    return jnp.swapaxes(hs, 0, 1).astype(x.dtype)


def rotary(x):
    S, D = x.shape[1], x.shape[3]
    half = D // 2
    freqs = ROPE_THETA ** (-jnp.arange(half, dtype=jnp.float32) / half)
    ang = jnp.arange(S, dtype=jnp.float32)[:, None] * freqs[None, :]
    cos = jnp.cos(ang)[None, :, None, :]
    sin = jnp.sin(ang)[None, :, None, :]
    x1, x2 = x[..., :half], x[..., half:]
    return jnp.concatenate([x1 * cos - x2 * sin, x1 * sin + x2 * cos], axis=-1)


def retention_chunkwise(q, k, v):
    B, S, H, DK = q.shape
    DV = v.shape[-1]
    NC = S // CHUNK
    log_g = jnp.log1p(-(2.0 ** (-5.0 - jnp.arange(H, dtype=jnp.float32))))
    idx = jnp.arange(CHUNK, dtype=jnp.float32)
    diff = idx[:, None] - idx[None, :]
    inner_decay = jnp.where(diff >= 0, jnp.exp(jnp.maximum(diff, 0.0)[None] * log_g[:, None, None]), 0.0)
    cross_decay = jnp.exp((idx[:, None] + 1.0) * log_g[None, :])[None, :, :, None]
    state_decay = jnp.exp((CHUNK - 1.0 - idx[:, None]) * log_g[None, :])[None, :, :, None]
    chunk_decay = jnp.exp(CHUNK * log_g)[None, :, None, None]

    def to_chunks(t):
        return jnp.swapaxes(t.reshape(B, NC, CHUNK, H, t.shape[-1]), 0, 1)

    def step(R, qkv):
        qc, kc, vc = qkv
        scores = jnp.einsum('bihd,bjhd->bhij', qc, kc) * inner_decay
        inner = jnp.einsum('bhij,bjhe->bihe', scores, vc)
        cross = jnp.einsum('bihd,bhde->bihe', qc, R) * cross_decay
        R_new = chunk_decay * R + jnp.einsum('bjhd,bjhe->bhde', kc, vc * state_decay)
        return R_new, inner + cross

    R0 = jnp.zeros((B, H, DK, DV), jnp.float32)
    _, out = lax.scan(step, R0, (to_chunks(q), to_chunks(k), to_chunks(v)))
    return jnp.swapaxes(out, 0, 1).reshape(B, S, H, DV)


def head_groupnorm(y, g):
    mu = jnp.mean(y, axis=-1, keepdims=True)
    yc = y - mu
    var = jnp.mean(yc * yc, axis=-1, keepdims=True)
    return yc * lax.rsqrt(var + EPS) * g.astype(jnp.float32)


def _fwd_setup_inputs(seed: int = 0) -> dict:
    key = jax.random.key(seed)
    ks = jax.random.split(key, 16)
    L = DEPTH
    f32 = jnp.float32
    nrm = lambda k, shape, fan: jax.random.normal(k, shape, f32) * (fan ** -0.5)
    a0 = jax.random.uniform(ks[10], (L, LRU_WIDTH), f32, 0.9, 0.999)
    return {
        "x": jax.random.normal(ks[0], (BATCH, SEQ, D_MODEL), f32),
        "norm_in": 1.0 + 0.02 * jax.random.normal(ks[1], (L, D_MODEL), f32),
        "w_in": nrm(ks[2], (L, D_MODEL, IN_COLS), D_MODEL),
        "conv_w": nrm(ks[3], (L, CONV_WIDTH, LRU_WIDTH), CONV_WIDTH),
        "conv_b": 0.02 * jax.random.normal(ks[4], (L, LRU_WIDTH), f32),
        "gate_x_w": nrm(ks[5], (L, LRU_BLOCKS, LRU_BW, LRU_BW), LRU_BW),
        "gate_x_b": 0.02 * jax.random.normal(ks[6], (L, LRU_WIDTH), f32),
        "gate_a_w": nrm(ks[7], (L, LRU_BLOCKS, LRU_BW, LRU_BW), LRU_BW),
        "gate_a_b": 0.02 * jax.random.normal(ks[8], (L, LRU_WIDTH), f32),
        "lru_lambda": jnp.log(a0 / (1.0 - a0)),
        "gn_gain": 1.0 + 0.02 * jax.random.normal(ks[9], (L, RET_HEADS, RET_DV), f32),
        "w_proj_a": nrm(ks[11], (L, LRU_WIDTH, D_MODEL), LRU_WIDTH),
        "w_proj_b": nrm(ks[12], (L, RET_WIDTH, D_MODEL), RET_WIDTH),
        "w_out": nrm(ks[13], (L, D_MODEL, D_MODEL), D_MODEL),
        "norm_final": 1.0 + 0.02 * jax.random.normal(ks[14], (D_MODEL,), f32),
    }


def _fwd_reference(x, norm_in, w_in, conv_w, conv_b, gate_x_w, gate_x_b, gate_a_w, gate_a_b,
              lru_lambda, gn_gain, w_proj_a, w_proj_b, w_out, norm_final):
    B, S, _ = x.shape
    for l in range(DEPTH):
        h = rmsnorm(x, norm_in[l])
        proj = jnp.einsum('bsd,dc->bsc', h, w_in[l])
        xa, ga, q, k, v, gb, ma, mb = jnp.split(proj, SPLIT_POINTS, axis=-1)

        xa = causal_depthwise_conv(xa, conv_w[l], conv_b[l])
        ya = rg_lru(xa, gate_x_w[l], gate_x_b[l], gate_a_w[l], gate_a_b[l], lru_lambda[l])
        ya = jax.nn.silu(ga) * ya
        out_a = jnp.einsum('bsw,wd->bsd', ya, w_proj_a[l])

        qh = rotary(q.reshape(B, S, RET_HEADS, RET_DK).astype(jnp.float32))
        kh = rotary(k.reshape(B, S, RET_HEADS, RET_DK).astype(jnp.float32)) * (RET_DK ** -0.5)
        vh = v.reshape(B, S, RET_HEADS, RET_DV).astype(jnp.float32)
        ret = head_groupnorm(retention_chunkwise(qh, kh, vh), gn_gain[l])
        yb = jax.nn.silu(gb) * ret.reshape(B, S, RET_WIDTH).astype(x.dtype)
        out_b = jnp.einsum('bsw,wd->bsd', yb, w_proj_b[l])

        merged = jax.nn.sigmoid(ma) * out_a + jax.nn.sigmoid(mb) * out_b
        x = x + jnp.einsum('bsd,de->bse', merged, w_out[l])
    return rmsnorm(x, norm_final)


import jax as _jax
import jax.numpy as _jnp

TWIN_FORMAT = 'train_step'
FWD_PARAMS = ['x', 'norm_in', 'w_in', 'conv_w', 'conv_b', 'gate_x_w', 'gate_x_b', 'gate_a_w', 'gate_a_b', 'lru_lambda', 'gn_gain', 'w_proj_a', 'w_proj_b', 'w_out', 'norm_final']
TWIN_WEIGHTS = ['norm_in', 'w_in', 'conv_w', 'conv_b', 'gate_x_w', 'gate_x_b', 'gate_a_w', 'gate_a_b', 'lru_lambda', 'gn_gain', 'w_proj_a', 'w_proj_b', 'w_out', 'norm_final']
TWIN_DIFF_INPUT = 'x'
TWIN_INPUTS = ['x', 'norm_in', 'w_in', 'conv_w', 'conv_b', 'gate_x_w', 'gate_x_b', 'gate_a_w', 'gate_a_b', 'lru_lambda', 'gn_gain', 'w_proj_a', 'w_proj_b', 'w_out', 'norm_final', 'loss_target', 'm_norm_in', 'm_w_in', 'm_conv_w', 'm_conv_b', 'm_gate_x_w', 'm_gate_x_b', 'm_gate_a_w', 'm_gate_a_b', 'm_lru_lambda', 'm_gn_gain', 'm_w_proj_a', 'm_w_proj_b', 'm_w_out', 'm_norm_final', 'v_norm_in', 'v_w_in', 'v_conv_w', 'v_conv_b', 'v_gate_x_w', 'v_gate_x_b', 'v_gate_a_w', 'v_gate_a_b', 'v_lru_lambda', 'v_gn_gain', 'v_w_proj_a', 'v_w_proj_b', 'v_w_out', 'v_norm_final']
TWIN_OUTPUTS = ['loss', 'grad_x', 'grad_norm_in', 'grad_w_in', 'grad_conv_w', 'grad_conv_b', 'grad_gate_x_w', 'grad_gate_x_b', 'grad_gate_a_w', 'grad_gate_a_b', 'grad_lru_lambda', 'grad_gn_gain', 'grad_w_proj_a', 'grad_w_proj_b', 'grad_w_out', 'grad_norm_final', 'delta_norm_in', 'delta_w_in', 'delta_conv_w', 'delta_conv_b', 'delta_gate_x_w', 'delta_gate_x_b', 'delta_gate_a_w', 'delta_gate_a_b', 'delta_lru_lambda', 'delta_gn_gain', 'delta_w_proj_a', 'delta_w_proj_b', 'delta_w_out', 'delta_norm_final', 'new_m_norm_in', 'new_m_w_in', 'new_m_conv_w', 'new_m_conv_b', 'new_m_gate_x_w', 'new_m_gate_x_b', 'new_m_gate_a_w', 'new_m_gate_a_b', 'new_m_lru_lambda', 'new_m_gn_gain', 'new_m_w_proj_a', 'new_m_w_proj_b', 'new_m_w_out', 'new_m_norm_final', 'new_v_norm_in', 'new_v_w_in', 'new_v_conv_w', 'new_v_conv_b', 'new_v_gate_x_w', 'new_v_gate_x_b', 'new_v_gate_a_w', 'new_v_gate_a_b', 'new_v_lru_lambda', 'new_v_gn_gain', 'new_v_w_proj_a', 'new_v_w_proj_b', 'new_v_w_out', 'new_v_norm_final']
TWIN_LEAF_KINDS = {'loss': 'loss', 'grad_x': 'grad_x', 'grad_norm_in': 'grad_w', 'grad_w_in': 'grad_w', 'grad_conv_w': 'grad_w', 'grad_conv_b': 'grad_w', 'grad_gate_x_w': 'grad_w', 'grad_gate_x_b': 'grad_w', 'grad_gate_a_w': 'grad_w', 'grad_gate_a_b': 'grad_w', 'grad_lru_lambda': 'grad_w', 'grad_gn_gain': 'grad_w', 'grad_w_proj_a': 'grad_w', 'grad_w_proj_b': 'grad_w', 'grad_w_out': 'grad_w', 'grad_norm_final': 'grad_w', 'delta_norm_in': 'delta_w', 'delta_w_in': 'delta_w', 'delta_conv_w': 'delta_w', 'delta_conv_b': 'delta_w', 'delta_gate_x_w': 'delta_w', 'delta_gate_x_b': 'delta_w', 'delta_gate_a_w': 'delta_w', 'delta_gate_a_b': 'delta_w', 'delta_lru_lambda': 'delta_w', 'delta_gn_gain': 'delta_w', 'delta_w_proj_a': 'delta_w', 'delta_w_proj_b': 'delta_w', 'delta_w_out': 'delta_w', 'delta_norm_final': 'delta_w', 'new_m_norm_in': 'new_m', 'new_m_w_in': 'new_m', 'new_m_conv_w': 'new_m', 'new_m_conv_b': 'new_m', 'new_m_gate_x_w': 'new_m', 'new_m_gate_x_b': 'new_m', 'new_m_gate_a_w': 'new_m', 'new_m_gate_a_b': 'new_m', 'new_m_lru_lambda': 'new_m', 'new_m_gn_gain': 'new_m', 'new_m_w_proj_a': 'new_m', 'new_m_w_proj_b': 'new_m', 'new_m_w_out': 'new_m', 'new_m_norm_final': 'new_m', 'new_v_norm_in': 'new_v', 'new_v_w_in': 'new_v', 'new_v_conv_w': 'new_v', 'new_v_conv_b': 'new_v', 'new_v_gate_x_w': 'new_v', 'new_v_gate_x_b': 'new_v', 'new_v_gate_a_w': 'new_v', 'new_v_gate_a_b': 'new_v', 'new_v_lru_lambda': 'new_v', 'new_v_gn_gain': 'new_v', 'new_v_w_proj_a': 'new_v', 'new_v_w_proj_b': 'new_v', 'new_v_w_out': 'new_v', 'new_v_norm_final': 'new_v'}


def _forward(args):
    return _fwd_reference(*[args[k] for k in FWD_PARAMS])


def _output_shape():
    out = _jax.eval_shape(lambda: _forward(_fwd_setup_inputs(0)))
    return out.shape, out.dtype

N_MICROBATCH = 1
ADAM_LR = 0.001
ADAM_B1 = 0.9
ADAM_B2 = 0.999
ADAM_EPS = 1e-08
ADAM_WD = 0.01
ADAM_STEP = 10
PER_EXAMPLE_BATCH_AXIS = {'x': 0, 'loss_target': 0}
SHARED_INPUTS = []
_WEIGHT_DTYPES = {'norm_in': _jnp.float32, 'w_in': _jnp.float32, 'conv_w': _jnp.float32, 'conv_b': _jnp.float32, 'gate_x_w': _jnp.float32, 'gate_x_b': _jnp.float32, 'gate_a_w': _jnp.float32, 'gate_a_b': _jnp.float32, 'lru_lambda': _jnp.float32, 'gn_gain': _jnp.float32, 'w_proj_a': _jnp.float32, 'w_proj_b': _jnp.float32, 'w_out': _jnp.float32, 'norm_final': _jnp.float32}
MOMENT_SCALE = {'norm_in': 1.239018e-01, 'w_in': 4.234899e-02, 'conv_w': 3.171335e-02, 'conv_b': 1.407985e-01, 'gate_x_w': 1.557286e-02, 'gate_x_b': 1.136721e-02, 'gate_a_w': 8.840006e-03, 'gate_a_b': 7.477382e-03, 'lru_lambda': 1.416075e-02, 'gn_gain': 5.368489e-02, 'w_proj_a': 3.073906e-02, 'w_proj_b': 5.390613e-02, 'w_out': 6.167857e-02, 'norm_final': 3.202828e+01}


def _to_microbatches(a, axis):
    t = _jnp.moveaxis(a, axis, 0)
    t = t.reshape((N_MICROBATCH, t.shape[0] // N_MICROBATCH) + t.shape[1:])
    return _jnp.moveaxis(t, 1, axis + 1)


def setup_inputs(seed: int = 0) -> dict:
    inp = _fwd_setup_inputs(seed)
    key = _jax.random.fold_in(_jax.random.key(seed), 7919)
    shape, _ = _output_shape()
    out = dict(inp)
    out["loss_target"] = _jax.random.normal(_jax.random.fold_in(key, 0), shape, _jnp.float32)
    for i, name in enumerate(TWIN_WEIGHTS):
        w = inp[name].astype(_jnp.float32)
        if MOMENT_SCALE is None:
            s = _jnp.sqrt(_jnp.mean(_jnp.square(w)) + 1e-30)
        else:
            s = MOMENT_SCALE[name]
        km, kv = _jax.random.split(_jax.random.fold_in(key, i + 1))
        out[name] = w
        out["m_" + name] = s * _jax.random.normal(km, w.shape, _jnp.float32)
        out["v_" + name] = (s * s) * _jax.random.uniform(kv, w.shape, _jnp.float32, 0.5, 1.5)
    if N_MICROBATCH > 1:
        for name, axis in PER_EXAMPLE_BATCH_AXIS.items():
            out[name] = _to_microbatches(out[name], axis)
    return {'x': out['x'], 'norm_in': out['norm_in'], 'w_in': out['w_in'], 'conv_w': out['conv_w'], 'conv_b': out['conv_b'], 'gate_x_w': out['gate_x_w'], 'gate_x_b': out['gate_x_b'], 'gate_a_w': out['gate_a_w'], 'gate_a_b': out['gate_a_b'], 'lru_lambda': out['lru_lambda'], 'gn_gain': out['gn_gain'], 'w_proj_a': out['w_proj_a'], 'w_proj_b': out['w_proj_b'], 'w_out': out['w_out'], 'norm_final': out['norm_final'], 'loss_target': out['loss_target'], 'm_norm_in': out['m_norm_in'], 'm_w_in': out['m_w_in'], 'm_conv_w': out['m_conv_w'], 'm_conv_b': out['m_conv_b'], 'm_gate_x_w': out['m_gate_x_w'], 'm_gate_x_b': out['m_gate_x_b'], 'm_gate_a_w': out['m_gate_a_w'], 'm_gate_a_b': out['m_gate_a_b'], 'm_lru_lambda': out['m_lru_lambda'], 'm_gn_gain': out['m_gn_gain'], 'm_w_proj_a': out['m_w_proj_a'], 'm_w_proj_b': out['m_w_proj_b'], 'm_w_out': out['m_w_out'], 'm_norm_final': out['m_norm_final'], 'v_norm_in': out['v_norm_in'], 'v_w_in': out['v_w_in'], 'v_conv_w': out['v_conv_w'], 'v_conv_b': out['v_conv_b'], 'v_gate_x_w': out['v_gate_x_w'], 'v_gate_x_b': out['v_gate_x_b'], 'v_gate_a_w': out['v_gate_a_w'], 'v_gate_a_b': out['v_gate_a_b'], 'v_lru_lambda': out['v_lru_lambda'], 'v_gn_gain': out['v_gn_gain'], 'v_w_proj_a': out['v_w_proj_a'], 'v_w_proj_b': out['v_w_proj_b'], 'v_w_out': out['v_w_out'], 'v_norm_final': out['v_norm_final']}


def _loss(weights, diff, rest, loss_target):
    with _jax.named_scope("forward"):
        args = {**rest, TWIN_DIFF_INPUT: diff, **{k: w.astype(_WEIGHT_DTYPES[k]) for k, w in weights.items()}}
        y = _forward(args)
    with _jax.named_scope("loss_head"):
        err = _jnp.square(y.astype(_jnp.float32) - loss_target)
        return 0.5 * _jnp.sum(_jnp.mean(err, axis=-1)) if err.ndim else 0.5 * err


def _adamw(w, g, m, v):
    m = ADAM_B1 * m + (1.0 - ADAM_B1) * g
    v = ADAM_B2 * v + (1.0 - ADAM_B2) * _jnp.square(g)
    m_hat = m / (1.0 - ADAM_B1 ** ADAM_STEP)
    v_hat = v / (1.0 - ADAM_B2 ** ADAM_STEP)
    delta = -ADAM_LR * (m_hat / (_jnp.sqrt(v_hat) + ADAM_EPS) + ADAM_WD * w)
    return delta, m, v


def reference(x, norm_in, w_in, conv_w, conv_b, gate_x_w, gate_x_b, gate_a_w, gate_a_b, lru_lambda, gn_gain, w_proj_a, w_proj_b, w_out, norm_final, loss_target, m_norm_in, m_w_in, m_conv_w, m_conv_b, m_gate_x_w, m_gate_x_b, m_gate_a_w, m_gate_a_b, m_lru_lambda, m_gn_gain, m_w_proj_a, m_w_proj_b, m_w_out, m_norm_final, v_norm_in, v_w_in, v_conv_w, v_conv_b, v_gate_x_w, v_gate_x_b, v_gate_a_w, v_gate_a_b, v_lru_lambda, v_gn_gain, v_w_proj_a, v_w_proj_b, v_w_out, v_norm_final):
    given = dict(x=x, norm_in=norm_in, w_in=w_in, conv_w=conv_w, conv_b=conv_b, gate_x_w=gate_x_w, gate_x_b=gate_x_b, gate_a_w=gate_a_w, gate_a_b=gate_a_b, lru_lambda=lru_lambda, gn_gain=gn_gain, w_proj_a=w_proj_a, w_proj_b=w_proj_b, w_out=w_out, norm_final=norm_final, loss_target=loss_target, m_norm_in=m_norm_in, m_w_in=m_w_in, m_conv_w=m_conv_w, m_conv_b=m_conv_b, m_gate_x_w=m_gate_x_w, m_gate_x_b=m_gate_x_b, m_gate_a_w=m_gate_a_w, m_gate_a_b=m_gate_a_b, m_lru_lambda=m_lru_lambda, m_gn_gain=m_gn_gain, m_w_proj_a=m_w_proj_a, m_w_proj_b=m_w_proj_b, m_w_out=m_w_out, m_norm_final=m_norm_final, v_norm_in=v_norm_in, v_w_in=v_w_in, v_conv_w=v_conv_w, v_conv_b=v_conv_b, v_gate_x_w=v_gate_x_w, v_gate_x_b=v_gate_x_b, v_gate_a_w=v_gate_a_w, v_gate_a_b=v_gate_a_b, v_lru_lambda=v_lru_lambda, v_gn_gain=v_gn_gain, v_w_proj_a=v_w_proj_a, v_w_proj_b=v_w_proj_b, v_w_out=v_w_out, v_norm_final=v_norm_final)
    weights = {n: given[n] for n in TWIN_WEIGHTS}
    shared = {n: given[n] for n in SHARED_INPUTS}
    per_example = {n: given[n] for n in ['x']}
    grad_fn = _jax.value_and_grad(_loss, argnums=(0, 1))

    def one_microbatch(ex, loss_target):
        ex = dict(ex)
        diff = ex.pop(TWIN_DIFF_INPUT)
        return grad_fn(weights, diff, {**shared, **ex}, loss_target)

    if N_MICROBATCH == 1:
        loss, (grad_w, grad_x) = one_microbatch(per_example, given["loss_target"])
    else:
        def body(carry, xs):
            loss_sum, grad_sum = carry
            l_k, (gw_k, gx_k) = one_microbatch(xs[0], xs[1])
            with _jax.named_scope("update"):
                return (loss_sum + l_k, _jax.tree.map(_jnp.add, grad_sum, gw_k)), gx_k

        init = (_jnp.zeros((), _jnp.float32), _jax.tree.map(_jnp.zeros_like, weights))
        (loss, grad_w), grad_x = _jax.lax.scan(body, init, (per_example, given["loss_target"]))
    with _jax.named_scope("update"):
        delta_w, new_m, new_v = {}, {}, {}
        for n in TWIN_WEIGHTS:
            delta_w[n], new_m[n], new_v[n] = _adamw(weights[n], grad_w[n], given["m_" + n], given["v_" + n])
    return (loss, grad_x, *[grad_w[n] for n in TWIN_WEIGHTS], *[delta_w[n] for n in TWIN_WEIGHTS],
            *[new_m[n] for n in TWIN_WEIGHTS], *[new_v[n] for n in TWIN_WEIGHTS])
```

```python
import functools

import jax
import jax.numpy as jnp
from jax import lax
from jax.experimental import pallas as pl
from jax.experimental.pallas import tpu as pltpu

F32 = jnp.float32
_MXU = jnp.bfloat16

D_MODEL = 1024
N_GROUPS = 8
HEADS = 4
DK = 256
CHUNK = 128
CONV = 4
LRU_BLOCKS = 16
LRU_BW = 64
LRU_C = 8.0
ROPE_THETA = 10000.0
EPS = 1e-6
CW = 256
N_CT = D_MODEL // CW
N_CHIPS = 4
MESH = pl.DeviceIdType.MESH

ADAM_LR = 0.001
ADAM_B1 = 0.9
ADAM_B2 = 0.999
ADAM_EPS = 1e-08
ADAM_WD = 0.01
ADAM_STEP = 10

VMEM_LIMIT = 56 * 1024 * 1024


def _c(v):
    return v.astype(_MXU)


def _dot(a, b):
    return lax.dot_general(a, b, (((1,), (0,)), ((), ())), preferred_element_type=F32)


def _dot_nt(a, b):
    return lax.dot_general(a, b, (((1,), (1,)), ((), ())), preferred_element_type=F32)


def _dot_tn(a, b):
    return lax.dot_general(a, b, (((0,), (0,)), ((), ())), preferred_element_type=F32)


def _sigmoid(z):
    return 1.0 / (1.0 + jnp.exp(-z))


def _params(sem=None):
    if sem is None:
        return pltpu.CompilerParams(vmem_limit_bytes=VMEM_LIMIT)
    return pltpu.CompilerParams(vmem_limit_bytes=VMEM_LIMIT, dimension_semantics=sem)


def _inproj_fwd(x2d, g_in, w_all):
    T = x2d.shape[0]
    tm = min(512, T)

    def body(x_ref, g_ref, w_ref, proj_ref, hb_ref):
        @pl.when(pl.program_id(1) == 0)
        def _():
            x = x_ref[...]
            r = lax.rsqrt(jnp.mean(x * x, axis=-1, keepdims=True) + EPS)
            hb_ref[...] = (x * r * g_ref[...]).astype(hb_ref.dtype)

        proj_ref[...] = _dot(hb_ref[...], w_ref[0])

    return pl.pallas_call(
        body,
        name="inproj_fwd",
        grid=(T // tm, N_GROUPS),
        in_specs=[
            pl.BlockSpec((tm, D_MODEL), lambda i, j: (i, 0)),
            pl.BlockSpec((1, D_MODEL), lambda i, j: (0, 0)),
            pl.BlockSpec((1, D_MODEL, D_MODEL), lambda i, j: (j // 2, 0, j % 2)),
        ],
        out_specs=[
            pl.BlockSpec((tm, D_MODEL), lambda i, j: (i, j)),
            pl.BlockSpec((tm, D_MODEL), lambda i, j: (i, 0)),
        ],
        out_shape=[
            jax.ShapeDtypeStruct((T, N_GROUPS * D_MODEL), F32),
            jax.ShapeDtypeStruct((T, D_MODEL), _MXU),
        ],
        compiler_params=_params(("parallel", "arbitrary")),
    )(x2d, g_in, w_all)


def _scan_fwd(a, u):
    n = a.shape[0]
    row = lax.broadcasted_iota(jnp.int32, a.shape, 0)
    s = 1
    while s < n:
        m = row >= s
        u = u + a * jnp.where(m, pltpu.roll(u, s, 0), 0.0)
        a = a * jnp.where(m, pltpu.roll(a, s, 0), 1.0)
        s *= 2
    return a, u


def _scan_bwd(b, g):
    n = b.shape[0]
    row = lax.broadcasted_iota(jnp.int32, b.shape, 0)
    s = 1
    while s < n:
        m = row < n - s
        g = g + b * jnp.where(m, pltpu.roll(g, n - s, 0), 0.0)
        b = b * jnp.where(m, pltpu.roll(b, n - s, 0), 1.0)
        s *= 2
    return b, g


def _softplus_neg(lam):
    z = -lam
    return jnp.maximum(z, 0.0) + jnp.log1p(jnp.exp(-jnp.abs(z)))


def _lru_gates(xc, wx_ref, wa_ref, bx_ref, ba_ref, lam_ref):
    xcb = _c(xc)
    i_t = _sigmoid(_dot(xcb, wx_ref[0]) + bx_ref[...])
    r_t = _sigmoid(_dot(xcb, wa_ref[0]) + ba_ref[...])
    sp = _softplus_neg(lam_ref[...])
    log_a = (-LRU_C) * r_t * sp
    a = jnp.exp(log_a)
    mult = jnp.sqrt(1.0 - a * a)
    return xcb, i_t, r_t, sp, a, mult


def _conv_from_ext(ext_ref, xa, cw_ref, cb_ref, tc):
    return (cb_ref[...] + cw_ref[3:4, :] * xa + cw_ref[2:3, :] * ext_ref[7:7 + tc, :]
            + cw_ref[1:2, :] * ext_ref[6:6 + tc, :] + cw_ref[0:1, :] * ext_ref[5:5 + tc, :])


def _lru_fwd(proj, conv_w, conv_b, wx_bd, wa_bd, bx, ba, lam, B, S):
    T = B * S
    tc = min(256, S)
    nt = S // tc
    h8 = tc // 8

    def body(xa_ref, halo_ref, ga_ref, cw_ref, cb_ref, wx_ref, wa_ref, bx_ref, ba_ref, lam_ref,
             h_ref, ya_ref, ext_ref, carry_ref):
        t = pl.program_id(2)

        @pl.when(t == 0)
        def _():
            carry_ref[...] = jnp.zeros_like(carry_ref)

        xa = xa_ref[...]
        ext_ref[0:8, :] = jnp.where(t == 0, 0.0, halo_ref[...])
        ext_ref[8:8 + tc, :] = xa
        xc = _conv_from_ext(ext_ref, xa, cw_ref, cb_ref, tc)
        _, i_t, _, _, a, mult = _lru_gates(xc, wx_ref, wa_ref, bx_ref, ba_ref, lam_ref)
        u = mult * (i_t * xc)
        acum, hloc = _scan_fwd(a, u)
        h = hloc + acum * carry_ref[7:8, :]
        h_ref[...] = h
        carry_ref[...] = h[tc - 8:tc, :]
        ga = ga_ref[...]
        ya_ref[...] = (ga * _sigmoid(ga) * h).astype(ya_ref.dtype)

    row = lambda b, t: b * nt + t
    vec = pl.BlockSpec((1, CW), lambda b, c, t: (0, c))
    mat = pl.BlockSpec((1, CW, CW), lambda b, c, t: (c, 0, 0))
    return pl.pallas_call(
        body,
        name="lru_fwd",
        grid=(B, N_CT, nt),
        in_specs=[
            pl.BlockSpec((tc, CW), lambda b, c, t: (row(b, t), c)),
            pl.BlockSpec((8, CW), lambda b, c, t: (jnp.maximum(row(b, t) * h8 - 1, 0), c)),
            pl.BlockSpec((tc, CW), lambda b, c, t: (row(b, t), N_CT + c)),
            pl.BlockSpec((CONV, CW), lambda b, c, t: (0, c)),
            vec, mat, mat, vec, vec, vec,
        ],
        out_specs=[
            pl.BlockSpec((tc, CW), lambda b, c, t: (row(b, t), c)),
            pl.BlockSpec((tc, CW), lambda b, c, t: (row(b, t), c)),
        ],
        out_shape=[
            jax.ShapeDtypeStruct((T, D_MODEL), F32),
            jax.ShapeDtypeStruct((T, D_MODEL), _MXU),
        ],
        scratch_shapes=[pltpu.VMEM((tc + 8, CW), F32), pltpu.VMEM((8, CW), F32)],
        compiler_params=_params(("parallel", "parallel", "arbitrary")),
    )(proj, proj, proj, conv_w, conv_b, wx_bd, wa_bd, bx, ba, lam)


def _lru_bwd(dya, proj, hlru, conv_w, conv_b, wx_bd, wa_bd, bx, ba, lam, B, S):
    T = B * S
    tc = min(256, S)
    nt = S // tc
    h8 = tc // 8

    def body(dya_ref, xa_ref, xhalo_ref, ga_ref, h_ref, hhalo_ref, cw_ref, cb_ref, wx_ref, wa_ref, bx_ref, ba_ref,
             lam_ref, dxa_ref, dga_ref, dcw_ref, dcb_ref, dwx_ref, dwa_ref, dbx_ref, dba_ref, dlam_ref,
             ext_ref, ext2_ref, carry_ref, dhalo_ref):
        b = pl.program_id(1)
        t = pl.program_id(2)
        tt = nt - 1 - t

        @pl.when(t == 0)
        def _():
            carry_ref[...] = jnp.zeros_like(carry_ref)
            dhalo_ref[...] = jnp.zeros_like(dhalo_ref)

        @pl.when((t == 0) & (b == 0))
        def _():
            for r in (dcw_ref, dcb_ref, dwx_ref, dwa_ref, dbx_ref, dba_ref, dlam_ref):
                r[...] = jnp.zeros_like(r)

        xa = xa_ref[...]
        ext_ref[0:8, :] = jnp.where(tt == 0, 0.0, xhalo_ref[...])
        ext_ref[8:8 + tc, :] = xa
        xc = _conv_from_ext(ext_ref, xa, cw_ref, cb_ref, tc)
        xcb, i_t, r_t, sp, a, mult = _lru_gates(xc, wx_ref, wa_ref, bx_ref, ba_ref, lam_ref)

        h = h_ref[...]
        ga = ga_ref[...]
        dya_t = dya_ref[...]
        sg = _sigmoid(ga)
        dga_ref[...] = (dya_t * h * (sg * (1.0 + ga * (1.0 - sg)))).astype(dga_ref.dtype)
        dlru = dya_t * (ga * sg)

        row = lax.broadcasted_iota(jnp.int32, a.shape, 0)
        coef = jnp.where(row == tc - 1, 1.0, pltpu.roll(a, tc - 1, 0))
        bcum, dloc = _scan_bwd(coef, dlru)
        dh = dloc + bcum * carry_ref[0:1, :]
        ext2_ref[0:tc, :] = a * dh
        carry_ref[...] = ext2_ref[0:8, :]

        ext2_ref[0:8, :] = jnp.where(tt == 0, 0.0, hhalo_ref[...])
        ext2_ref[8:8 + tc, :] = h
        hprev = ext2_ref[7:7 + tc, :]

        da = dh * hprev
        ix = i_t * xc
        dmult = dh * ix
        di = dh * mult * xc
        dxc = dh * mult * i_t
        dlog_a = da * a - dmult * (a * a) / mult
        dr = dlog_a * ((-LRU_C) * sp)
        dlam_ref[...] += jnp.sum(dlog_a * r_t, axis=0, keepdims=True) * (LRU_C * _sigmoid(-lam_ref[...]))
        dza = dr * r_t * (1.0 - r_t)
        dzx = di * i_t * (1.0 - i_t)
        dzab = _c(dza)
        dzxb = _c(dzx)
        dxc = dxc + _dot_nt(dzxb, wx_ref[0]) + _dot_nt(dzab, wa_ref[0])
        dwx_ref[0] += _dot_tn(xcb, dzxb)
        dwa_ref[0] += _dot_tn(xcb, dzab)
        dbx_ref[...] += jnp.sum(dzx, axis=0, keepdims=True)
        dba_ref[...] += jnp.sum(dza, axis=0, keepdims=True)

        dcb_ref[...] += jnp.sum(dxc, axis=0, keepdims=True)
        dcw_ref[3:4, :] += jnp.sum(dxc * xa, axis=0, keepdims=True)
        dcw_ref[2:3, :] += jnp.sum(dxc * ext_ref[7:7 + tc, :], axis=0, keepdims=True)
        dcw_ref[1:2, :] += jnp.sum(dxc * ext_ref[6:6 + tc, :], axis=0, keepdims=True)
        dcw_ref[0:1, :] += jnp.sum(dxc * ext_ref[5:5 + tc, :], axis=0, keepdims=True)
        ext2_ref[0:tc, :] = dxc
        ext2_ref[tc:tc + 8, :] = dhalo_ref[...]
        dxa = (cw_ref[3:4, :] * dxc + cw_ref[2:3, :] * ext2_ref[1:1 + tc, :]
               + cw_ref[1:2, :] * ext2_ref[2:2 + tc, :] + cw_ref[0:1, :] * ext2_ref[3:3 + tc, :])
        dxa_ref[...] = dxa.astype(dxa_ref.dtype)
        dhalo_ref[...] = ext2_ref[0:8, :]

    row_of = lambda b, t: b * nt + (nt - 1 - t)
    tile = lambda off: pl.BlockSpec((tc, CW), lambda c, b, t: (row_of(b, t), off + c))
    halo = pl.BlockSpec((8, CW), lambda c, b, t: (jnp.maximum(row_of(b, t) * h8 - 1, 0), c))
    vec = pl.BlockSpec((1, CW), lambda c, b, t: (0, c))
    mat = pl.BlockSpec((1, CW, CW), lambda c, b, t: (c, 0, 0))
    cwspec = pl.BlockSpec((CONV, CW), lambda c, b, t: (0, c))
    return pl.pallas_call(
        body,
        name="lru_bwd",
        grid=(N_CT, B, nt),
        in_specs=[tile(0), tile(0), halo, tile(N_CT), tile(0), halo, cwspec, vec, mat, mat, vec, vec, vec],
        out_specs=[tile(0), tile(0), cwspec, vec, mat, mat, vec, vec, vec],
        out_shape=[
            jax.ShapeDtypeStruct((T, D_MODEL), _MXU),
            jax.ShapeDtypeStruct((T, D_MODEL), _MXU),
            jax.ShapeDtypeStruct((CONV, D_MODEL), F32),
            jax.ShapeDtypeStruct((1, D_MODEL), F32),
            jax.ShapeDtypeStruct((N_CT, CW, CW), F32),
            jax.ShapeDtypeStruct((N_CT, CW, CW), F32),
            jax.ShapeDtypeStruct((1, D_MODEL), F32),
            jax.ShapeDtypeStruct((1, D_MODEL), F32),
            jax.ShapeDtypeStruct((1, D_MODEL), F32),
        ],
        scratch_shapes=[pltpu.VMEM((tc + 8, CW), F32), pltpu.VMEM((tc + 8, CW), F32),
                        pltpu.VMEM((8, CW), F32), pltpu.VMEM((8, CW), F32)],
        compiler_params=_params(("parallel", "arbitrary", "arbitrary")),
    )(dya, proj, proj, proj, hlru, hlru, conv_w, conv_b, wx_bd, wa_bd, bx, ba, lam)


def _retention_tables(S):
    half = DK // 2
    freqs = ROPE_THETA ** (-jnp.arange(half, dtype=F32) / half)
    ang = jnp.arange(S, dtype=F32)[:, None] * freqs[None, :]
    log_g = jnp.log1p(-(2.0 ** (-5.0 - jnp.arange(HEADS, dtype=F32))))
    idx = jnp.arange(CHUNK, dtype=F32)
    diff = idx[:, None] - idx[None, :]
    inner = jnp.where(diff >= 0, jnp.exp(jnp.maximum(diff, 0.0)[None] * log_g[:, None, None]), 0.0)
    cross = jnp.exp((idx[None, :] + 1.0) * log_g[:, None])[:, :, None]
    state = jnp.exp((CHUNK - 1.0 - idx[None, :]) * log_g[:, None])[:, :, None]
    gam = jnp.broadcast_to(jnp.exp(CHUNK * log_g)[:, None, None], (HEADS, 1, DK))
    return jnp.cos(ang), jnp.sin(ang), inner, cross, state, gam


def _rot(x, cos, sin):
    half = DK // 2
    x1, x2 = x[:, :half], x[:, half:]
    return jnp.concatenate([x1 * cos - x2 * sin, x1 * sin + x2 * cos], axis=-1)


def _rot_t(y, cos, sin):
    half = DK // 2
    y1, y2 = y[:, :half], y[:, half:]
    return jnp.concatenate([y1 * cos + y2 * sin, y2 * cos - y1 * sin], axis=-1)


def _groupnorm(o):
    mu = jnp.mean(o, axis=-1, keepdims=True)
    oc = o - mu
    rs = lax.rsqrt(jnp.mean(oc * oc, axis=-1, keepdims=True) + EPS)
    return oc * rs, rs


def _ret_specs(S, order):
    nc = S // CHUNK
    qkv = lambda g: pl.BlockSpec((CHUNK, DK), lambda *i: (order(*i)[0] * nc + order(*i)[2], g * HEADS + order(*i)[1]))
    act = pl.BlockSpec((CHUNK, DK), lambda *i: (order(*i)[0] * nc + order(*i)[2], order(*i)[1]))
    rope = pl.BlockSpec((CHUNK, DK // 2), lambda *i: (order(*i)[2], 0))
    dmat = pl.BlockSpec((1, CHUNK, CHUNK), lambda *i: (order(*i)[1], 0, 0))
    dvec = pl.BlockSpec((1, CHUNK, 1), lambda *i: (order(*i)[1], 0, 0))
    hrow = pl.BlockSpec((1, 1, DK), lambda *i: (order(*i)[1], 0, 0))
    rst = pl.BlockSpec((1, DK, DK), lambda *i: ((order(*i)[0] * HEADS + order(*i)[1]) * nc + order(*i)[2], 0, 0))
    return qkv, act, rope, dmat, dvec, hrow, rst


def _ret_fwd(proj, tables, gain3, B, S):
    T = B * S
    nc = S // CHUNK
    cos, sin, dmat_t, cd_t, sd_t, gam_t = tables

    def body(q_ref, k_ref, v_ref, gb_ref, cos_ref, sin_ref, dm_ref, cd_ref, sd_ref, gam_ref, gain_ref,
             o_ref, yb_ref, rs_ref, state_ref):
        @pl.when(pl.program_id(2) == 0)
        def _():
            state_ref[...] = jnp.zeros_like(state_ref)

        cos_t, sin_t = cos_ref[...], sin_ref[...]
        qb = _c(_rot(q_ref[...], cos_t, sin_t))
        kb = _c(_rot(k_ref[...], cos_t, sin_t) * (DK ** -0.5))
        v = v_ref[...]
        state = state_ref[...]
        sb = _c(state)
        rs_ref[0] = sb
        scores = _dot_nt(qb, kb) * dm_ref[0]
        o = _dot(_c(scores), _c(v)) + _dot(qb, sb) * cd_ref[0]
        state_ref[...] = gam_ref[0] * state + _dot_tn(kb, _c(v * sd_ref[0]))
        o_ref[...] = o
        n, _ = _groupnorm(o)
        gb = gb_ref[...]
        yb_ref[...] = (gb * _sigmoid(gb) * (n * gain_ref[0])).astype(yb_ref.dtype)

    qkv, act, rope, dmat, dvec, hrow, rst = _ret_specs(S, lambda b, h, c: (b, h, c))
    return pl.pallas_call(
        body,
        name="ret_fwd",
        grid=(B, HEADS, nc),
        in_specs=[qkv(2), qkv(3), qkv(4), qkv(5), rope, rope, dmat, dvec, dvec, hrow, hrow],
        out_specs=[act, act, rst],
        out_shape=[
            jax.ShapeDtypeStruct((T, D_MODEL), F32),
            jax.ShapeDtypeStruct((T, D_MODEL), _MXU),
            jax.ShapeDtypeStruct((B * HEADS * nc, DK, DK), _MXU),
        ],
        scratch_shapes=[pltpu.VMEM((DK, DK), F32)],
        compiler_params=_params(("parallel", "parallel", "arbitrary")),
    )(proj, proj, proj, proj, cos, sin, dmat_t, cd_t, sd_t, gam_t, gain3)


def _ret_bwd(dyb, o_pre, proj, states, tables, gain3, B, S):
    T = B * S
    nc = S // CHUNK
    cos, sin, dmat_t, cd_t, sd_t, gam_t = tables

    def body(dyb_ref, o_ref, q_ref, k_ref, v_ref, gb_ref, rs_ref, cos_ref, sin_ref, dm_ref, cd_ref, sd_ref, gam_ref,
             gain_ref, dq_ref, dk_ref, dv_ref, dgb_ref, dgain_ref, dstate_ref):
        @pl.when(pl.program_id(2) == 0)
        def _():
            dstate_ref[...] = jnp.zeros_like(dstate_ref)

        @pl.when((pl.program_id(2) == 0) & (pl.program_id(1) == 0))
        def _():
            dgain_ref[...] = jnp.zeros_like(dgain_ref)

        gain = gain_ref[0]
        n, rs = _groupnorm(o_ref[...])
        gb = gb_ref[...]
        sg = _sigmoid(gb)
        dy = dyb_ref[...]
        dgb_ref[...] = (dy * (n * gain) * (sg * (1.0 + gb * (1.0 - sg)))).astype(dgb_ref.dtype)
        dgn = dy * (gb * sg)
        dgain_ref[0] += jnp.sum(dgn * n, axis=0, keepdims=True)
        dn = dgn * gain
        do = rs * (dn - jnp.mean(dn, axis=-1, keepdims=True) - n * jnp.mean(dn * n, axis=-1, keepdims=True))

        cos_t, sin_t = cos_ref[...], sin_ref[...]
        qb = _c(_rot(q_ref[...], cos_t, sin_t))
        kb = _c(_rot(k_ref[...], cos_t, sin_t) * (DK ** -0.5))
        v = v_ref[...]
        vb = _c(v)
        vsb = _c(v * sd_ref[0])
        dob = _c(do)
        docb = _c(do * cd_ref[0])
        dmat = dm_ref[0]
        dstate = dstate_ref[...]
        dsb = _c(dstate)
        pb = _c(_dot_nt(qb, kb) * dmat)
        dsc = _c(_dot_nt(dob, vb) * dmat)
        dq = _dot(dsc, kb) + _dot_nt(docb, rs_ref[0])
        dk = _dot_tn(dsc, qb) + _dot_nt(vsb, dsb)
        dv = _dot_tn(pb, dob) + _dot(kb, dsb) * sd_ref[0]
        dstate_ref[...] = gam_ref[0] * dstate + _dot_tn(qb, docb)
        dq_ref[...] = _rot_t(dq, cos_t, sin_t).astype(dq_ref.dtype)
        dk_ref[...] = (_rot_t(dk, cos_t, sin_t) * (DK ** -0.5)).astype(dk_ref.dtype)
        dv_ref[...] = dv.astype(dv_ref.dtype)

    qkv, act, rope, dmat, dvec, hrow, rst = _ret_specs(S, lambda h, b, c: (b, h, nc - 1 - c))
    big = jax.ShapeDtypeStruct((T, D_MODEL), _MXU)
    return pl.pallas_call(
        body,
        name="ret_bwd",
        grid=(HEADS, B, nc),
        in_specs=[act, act, qkv(2), qkv(3), qkv(4), qkv(5), rst, rope, rope, dmat, dvec, dvec, hrow, hrow],
        out_specs=[act, act, act, act, hrow],
        out_shape=[big, big, big, big, jax.ShapeDtypeStruct((HEADS, 1, DK), F32)],
        scratch_shapes=[pltpu.VMEM((DK, DK), F32)],
        compiler_params=_params(("parallel", "arbitrary", "arbitrary")),
    )(dyb, o_pre, proj, proj, proj, proj, states, cos, sin, dmat_t, cd_t, sd_t, gam_t, gain3)


def _mid(ya, yb, proj, x2d, tgt2d, wpa, wpb, wout, g_fin):
    T = x2d.shape[0]
    tm = min(256, T)
    n_steps = T // tm
    rows = D_MODEL // (2 * N_CHIPS)

    def body(ya_ref, yb_ref, ma_ref, mb_ref, x_ref, t_ref, gf_ref, wpa_hbm, wpb_hbm, wout_hbm,
             loss_ref, dx2_ref, dya_ref, dyb_ref, dma_ref, dmb_ref, dgf_ref, gw_hbm, w_ref, acc_ref, sem):
        i = pl.program_id(0)

        @pl.when(i == 0)
        def _():
            loads = [pltpu.make_async_copy(src, w_ref.at[k], sem.at[k]) for k, src in enumerate((wpa_hbm, wpb_hbm, wout_hbm))]
            for cp in loads:
                cp.start()
            for cp in loads:
                cp.wait()
            acc_ref[...] = jnp.zeros_like(acc_ref)
            loss_ref[...] = jnp.zeros_like(loss_ref)
            dgf_ref[...] = jnp.zeros_like(dgf_ref)

        ya_t, yb_t = ya_ref[...], yb_ref[...]
        out_a = _dot(ya_t, w_ref[0])
        out_b = _dot(yb_t, w_ref[1])
        sa = _sigmoid(ma_ref[...])
        sb = _sigmoid(mb_ref[...])
        mgb = _c(sa * out_a + sb * out_b)
        x2 = x_ref[...] + _dot(mgb, w_ref[2])
        r2 = lax.rsqrt(jnp.mean(x2 * x2, axis=-1, keepdims=True) + EPS)
        nx = x2 * r2
        gf = gf_ref[...]
        err = nx * gf - t_ref[...]
        loss_ref[...] += 0.5 * jnp.sum(jnp.mean(err * err, axis=-1, keepdims=True), axis=0, keepdims=True)
        dy = err * (1.0 / D_MODEL)
        dgf_ref[...] += jnp.sum(dy * nx, axis=0, keepdims=True)
        dyg = dy * gf
        dx2 = r2 * (dyg - nx * jnp.mean(dyg * nx, axis=-1, keepdims=True))
        dx2_ref[...] = dx2
        dx2b = _c(dx2)
        dmg = _dot_nt(dx2b, w_ref[2])
        acc_ref[2] += _dot_tn(mgb, dx2b)
        dma_ref[...] = (dmg * out_a * sa * (1.0 - sa)).astype(dma_ref.dtype)
        dmb_ref[...] = (dmg * out_b * sb * (1.0 - sb)).astype(dmb_ref.dtype)
        dab = _c(dmg * sa)
        dbb = _c(dmg * sb)
        dya_ref[...] = _dot_nt(dab, w_ref[0])
        dyb_ref[...] = _dot_nt(dbb, w_ref[1])
        acc_ref[0] += _dot_tn(ya_t, dab)
        acc_ref[1] += _dot_tn(yb_t, dbb)

        @pl.when(i == n_steps - 1)
        def _():
            copies = [pltpu.make_async_copy(acc_ref.at[k, pl.ds((2 * p + hf) * rows, rows), :], gw_hbm.at[p, hf, k],
                                            sem.at[(k * N_CHIPS + p) * 2 + hf])
                      for k in range(3) for p in range(N_CHIPS) for hf in range(2)]
            for cp in copies:
                cp.start()
            for cp in copies:
                cp.wait()

    tile = lambda j: pl.BlockSpec((tm, D_MODEL), lambda i: (i, j))
    one = pl.BlockSpec((1, D_MODEL), lambda i: (0, 0))
    anyspec = pl.BlockSpec(memory_space=pl.ANY)
    return pl.pallas_call(
        body,
        name="mid",
        grid=(n_steps,),
        in_specs=[tile(0), tile(0), tile(6), tile(7), tile(0), tile(0), one, anyspec, anyspec, anyspec],
        out_specs=[pl.BlockSpec((1, 1), lambda i: (0, 0)), tile(0), tile(0), tile(0), tile(0), tile(0), one, anyspec],
        out_shape=[
            jax.ShapeDtypeStruct((1, 1), F32),
            jax.ShapeDtypeStruct((T, D_MODEL), F32),
            jax.ShapeDtypeStruct((T, D_MODEL), F32),
            jax.ShapeDtypeStruct((T, D_MODEL), F32),
            jax.ShapeDtypeStruct((T, D_MODEL), _MXU),
            jax.ShapeDtypeStruct((T, D_MODEL), _MXU),
            jax.ShapeDtypeStruct((1, D_MODEL), F32),
            jax.ShapeDtypeStruct((N_CHIPS, 2, 3, rows, D_MODEL), F32),
        ],
        scratch_shapes=[pltpu.VMEM((3, D_MODEL, D_MODEL), _MXU), pltpu.VMEM((3, D_MODEL, D_MODEL), F32),
                        pltpu.SemaphoreType.DMA((3 * N_CHIPS * 2,))],
        compiler_params=_params(("arbitrary",)),
    )(ya, yb, proj, proj, x2d, tgt2d, g_fin, wpa, wpb, wout)


def _inproj_bwd_dx(dparts, w_all, x2d, dx2, g_in):
    T = x2d.shape[0]
    tm = min(256, T)

    def body(*refs):
        d_refs = refs[:N_GROUPS]
        x_ref, dx2_ref, g_ref, w_hbm, dx_ref, dg_ref, w_ref, sem = refs[N_GROUPS:]

        @pl.when(pl.program_id(0) == 0)
        def _():
            cp = pltpu.make_async_copy(w_hbm, w_ref, sem)
            cp.start()
            cp.wait()
            dg_ref[...] = jnp.zeros_like(dg_ref)

        dh = jnp.zeros((tm, D_MODEL), F32)
        for j in range(N_GROUPS):
            dh = dh + _dot_nt(d_refs[j][...], w_ref[j // 2, :, (j % 2) * D_MODEL:(j % 2 + 1) * D_MODEL])
        x = x_ref[...]
        r = lax.rsqrt(jnp.mean(x * x, axis=-1, keepdims=True) + EPS)
        nx = x * r
        dg_ref[...] += jnp.sum(dh * nx, axis=0, keepdims=True)
        dhg = dh * g_ref[...]
        dx_ref[...] = dx2_ref[...] + r * (dhg - nx * jnp.mean(dhg * nx, axis=-1, keepdims=True))

    tile = pl.BlockSpec((tm, D_MODEL), lambda i: (i, 0))
    one = pl.BlockSpec((1, D_MODEL), lambda i: (0, 0))
    return pl.pallas_call(
        body,
        name="inproj_bwd_dx",
        grid=(T // tm,),
        in_specs=[tile] * N_GROUPS + [tile, tile, one, pl.BlockSpec(memory_space=pl.ANY)],
        out_specs=[tile, one],
        out_shape=[jax.ShapeDtypeStruct((T, D_MODEL), F32), jax.ShapeDtypeStruct((1, D_MODEL), F32)],
        scratch_shapes=[pltpu.VMEM(w_all.shape, w_all.dtype), pltpu.SemaphoreType.DMA],
        compiler_params=_params(("arbitrary",)),
    )(*dparts, x2d, dx2, g_in, w_all)


def _inproj_bwd_dw(hb, dparts):
    T = hb.shape[0]
    tm = min(512, T)
    half = D_MODEL // 2

    def body(*refs):
        hb_ref = refs[0]
        d_refs = refs[1:1 + N_GROUPS]
        out_ref = refs[1 + N_GROUPS]
        j = pl.program_id(0)

        @pl.when(pl.program_id(1) == 0)
        def _():
            out_ref[...] = jnp.zeros_like(out_ref)

        for jj in range(N_GROUPS):
            @pl.when(j == jj)
            def _(jj=jj):
                g = _dot_tn(hb_ref[...], d_refs[jj][...])
                out_ref[0, 0] += g[:half]
                out_ref[0, 1] += g[half:]

    def dspec(jj):
        return pl.BlockSpec((tm, D_MODEL), lambda j, i: (jnp.where(j == jj, i, 0), 0))

    return pl.pallas_call(
        body,
        name="inproj_bwd_dw",
        grid=(N_GROUPS, T // tm),
        in_specs=[pl.BlockSpec((tm, D_MODEL), lambda j, i: (i, 0))] + [dspec(jj) for jj in range(N_GROUPS)],
        out_specs=pl.BlockSpec((1, 2, half, D_MODEL), lambda j, i: (j // 2, 0, 0, j % 2)),
        out_shape=jax.ShapeDtypeStruct((N_CHIPS, 2, half, 2 * D_MODEL), F32),
        compiler_params=_params(("parallel", "arbitrary")),
    )(hb, *dparts)


def _coords():
    return lax.axis_index("x"), lax.axis_index("y"), lax.axis_index("c")


def _other_chips(x, y):
    return [(1 - x, y), (x, 1 - y), (1 - x, 1 - y)]


def _all_gather8(xs, name):
    m_per, n = xs.shape

    def body(x_ref, out_ref, send_sems, recv_sems, local_sem):
        x, y, c = _coords()
        me, sibling = (x, y, c), (x, y, 1 - c)
        chips = _other_chips(x, y)

        def rows(px, py, pc):
            return out_ref.at[pl.ds((4 * px + 2 * py + pc) * m_per, m_per), :]

        def copy(k, block, to, src=None):
            return pltpu.make_async_remote_copy(
                src_ref=rows(*block) if src is None else src, dst_ref=rows(*block),
                send_sem=send_sems.at[k], recv_sem=recv_sems.at[k], device_id=to, device_id_type=MESH)

        mine = pltpu.make_async_copy(x_ref, rows(*me), local_sem)
        mine.start()
        first = [copy(0, me, sibling, src=x_ref)]
        first += [copy(1 + j, me, (*chip, c), src=x_ref) for j, chip in enumerate(chips)]
        for cp in first:
            cp.start()
        passed = [copy(4 + j, (*chip, c), sibling) for j, chip in enumerate(chips)]
        for j, chip in enumerate(chips):
            copy(1 + j, (*chip, c), me).wait_recv()
            passed[j].start()
        copy(0, sibling, me).wait_recv()
        for j, chip in enumerate(chips):
            copy(4 + j, (*chip, 1 - c), me).wait_recv()
        for cp in first + passed:
            cp.wait_send()
        mine.wait()

    return pl.pallas_call(
        body,
        name=name,
        out_shape=jax.ShapeDtypeStruct((8 * m_per, n), xs.dtype),
        in_specs=[pl.BlockSpec(memory_space=pltpu.VMEM)],
        out_specs=pl.BlockSpec(memory_space=pltpu.VMEM),
        scratch_shapes=[pltpu.SemaphoreType.DMA((7,)), pltpu.SemaphoreType.DMA((7,)), pltpu.SemaphoreType.DMA],
        compiler_params=pltpu.CompilerParams(vmem_limit_bytes=VMEM_LIMIT),
    )(xs)


def _gather_chips(shards, name):
    n = len(shards)

    def body(*refs):
        ins, outs = refs[:n], refs[n:2 * n]
        send_sems, recv_sems, local_sems = refs[2 * n:]
        x, y, c = _coords()
        me = 2 * x + y
        chips = _other_chips(x, y)
        local = [pltpu.make_async_copy(ins[a], outs[a].at[me], local_sems.at[a]) for a in range(n)]
        for cp in local:
            cp.start()

        def copy(k, a, slot, chip):
            return pltpu.make_async_remote_copy(
                src_ref=ins[a], dst_ref=outs[a].at[slot], send_sem=send_sems.at[k * n + a],
                recv_sem=recv_sems.at[k * n + a], device_id=(*chip, c), device_id_type=MESH)

        sends = [copy(k, a, me, chip) for k, chip in enumerate(chips) for a in range(n)]
        for cp in sends:
            cp.start()
        for k, (px, py) in enumerate(chips):
            for a in range(n):
                copy(k, a, 2 * px + py, (px, py)).wait_recv()
        for cp in sends:
            cp.wait_send()
        for cp in local:
            cp.wait()

    anyspec = pl.BlockSpec(memory_space=pl.ANY)
    return pl.pallas_call(
        body,
        name=name,
        in_specs=[anyspec] * n,
        out_specs=[anyspec] * n,
        out_shape=[jax.ShapeDtypeStruct((N_CHIPS,) + s.shape, s.dtype) for s in shards],
        scratch_shapes=[pltpu.SemaphoreType.DMA((3 * n,)), pltpu.SemaphoreType.DMA((3 * n,)),
                        pltpu.SemaphoreType.DMA((n,))],
    )(*shards)


def _swap_halves(arrs, name):
    n = len(arrs)

    def body(*refs):
        ins, outs = refs[:n], refs[n:2 * n]
        send_sems, recv_sems = refs[2 * n:]
        x, y, c = _coords()
        copies = [pltpu.make_async_remote_copy(
            src_ref=ins[a].at[p, 1 - c], dst_ref=outs[a].at[p], send_sem=send_sems.at[a * N_CHIPS + p],
            recv_sem=recv_sems.at[a * N_CHIPS + p], device_id=(x, y, 1 - c), device_id_type=MESH)
            for a in range(n) for p in range(N_CHIPS)]
        for cp in copies:
            cp.start()
        for cp in copies:
            cp.wait()

    anyspec = pl.BlockSpec(memory_space=pl.ANY)
    return pl.pallas_call(
        body,
        name=name,
        in_specs=[anyspec] * n,
        out_specs=[anyspec] * n,
        out_shape=[jax.ShapeDtypeStruct((N_CHIPS,) + a.shape[2:], a.dtype) for a in arrs],
        scratch_shapes=[pltpu.SemaphoreType.DMA((N_CHIPS * n,)), pltpu.SemaphoreType.DMA((N_CHIPS * n,))],
    )(*arrs)


def _scatter_chips(arrs, name):
    n = len(arrs)

    def body(*refs):
        ins, outs = refs[:n], refs[n:2 * n]
        send_sems, recv_sems, local_sems = refs[2 * n:]
        x, y, c = _coords()
        me = 2 * x + y
        chips = _other_chips(x, y)
        local = [pltpu.make_async_copy(ins[a].at[me], outs[a].at[me], local_sems.at[a]) for a in range(n)]
        for cp in local:
            cp.start()

        def copy(k, a, src_slot, dst_slot, chip):
            return pltpu.make_async_remote_copy(
                src_ref=ins[a].at[src_slot], dst_ref=outs[a].at[dst_slot], send_sem=send_sems.at[k * n + a],
                recv_sem=recv_sems.at[k * n + a], device_id=(*chip, c), device_id_type=MESH)

        sends = [copy(k, a, 2 * px + py, me, (px, py)) for k, (px, py) in enumerate(chips) for a in range(n)]
        for cp in sends:
            cp.start()
        for k, (px, py) in enumerate(chips):
            for a in range(n):
                copy(k, a, me, 2 * px + py, (px, py)).wait_recv()
        for cp in sends:
            cp.wait_send()
        for cp in local:
            cp.wait()

    anyspec = pl.BlockSpec(memory_space=pl.ANY)
    return pl.pallas_call(
        body,
        name=name,
        in_specs=[anyspec] * n,
        out_specs=[anyspec] * n,
        out_shape=[jax.ShapeDtypeStruct(a.shape, a.dtype) for a in arrs],
        scratch_shapes=[pltpu.SemaphoreType.DMA((3 * n,)), pltpu.SemaphoreType.DMA((3 * n,)),
                        pltpu.SemaphoreType.DMA((n,))],
    )(*arrs)


def _join_halves(halves, name):
    n = len(halves)

    def body(*refs):
        ins, outs = refs[:n], refs[n:2 * n]
        send_sems, recv_sems, local_sems = refs[2 * n:]
        x, y, c = _coords()
        local = [pltpu.make_async_copy(ins[a], outs[a].at[c], local_sems.at[a]) for a in range(n)]
        for cp in local:
            cp.start()
        sends = [pltpu.make_async_remote_copy(
            src_ref=ins[a], dst_ref=outs[a].at[c], send_sem=send_sems.at[a], recv_sem=recv_sems.at[a],
            device_id=(x, y, 1 - c), device_id_type=MESH) for a in range(n)]
        for cp in sends:
            cp.start()
        for a in range(n):
            pltpu.make_async_remote_copy(
                src_ref=ins[a], dst_ref=outs[a].at[1 - c], send_sem=send_sems.at[a], recv_sem=recv_sems.at[a],
                device_id=(x, y, 1 - c), device_id_type=MESH).wait_recv()
        for cp in sends:
            cp.wait_send()
        for cp in local:
            cp.wait()

    anyspec = pl.BlockSpec(memory_space=pl.ANY)
    return pl.pallas_call(
        body,
        name=name,
        in_specs=[anyspec] * n,
        out_specs=[anyspec] * n,
        out_shape=[jax.ShapeDtypeStruct((2,) + h.shape, h.dtype) for h in halves],
        scratch_shapes=[pltpu.SemaphoreType.DMA((n,)), pltpu.SemaphoreType.DMA((n,)), pltpu.SemaphoreType.DMA((n,))],
    )(*halves)


def _row_tile(rows, cap):
    t = cap
    while rows % t:
        t //= 2
    return t


def _add_my_half(g, r, name):
    _, _, R, C = g.shape
    tr = _row_tile(R, 256)

    def body(c_ref, g_ref, r_ref, o_ref):
        o_ref[...] = g_ref[0] + r_ref[...]

    return pl.pallas_call(
        body,
        name=name,
        grid_spec=pltpu.PrefetchScalarGridSpec(
            num_scalar_prefetch=1,
            grid=(N_CHIPS, R // tr),
            in_specs=[pl.BlockSpec((1, 1, tr, C), lambda p, i, c_ref: (p, c_ref[0], i, 0)),
                      pl.BlockSpec((1, tr, C), lambda p, i, c_ref: (p, i, 0))],
            out_specs=pl.BlockSpec((1, tr, C), lambda p, i, c_ref: (p, i, 0)),
        ),
        out_shape=jax.ShapeDtypeStruct(r.shape, F32),
        compiler_params=_params(("parallel", "parallel")),
    )(lax.axis_index("c").reshape(1).astype(jnp.int32), g, r)


def _sum_slabs(r, name):
    _, R, C = r.shape
    tr = _row_tile(R, 256)

    def body(r_ref, o_ref):
        o_ref[...] = ((r_ref[0] + r_ref[1]) + r_ref[2]) + r_ref[3]

    return pl.pallas_call(
        body,
        name=name,
        grid=(R // tr,),
        in_specs=[pl.BlockSpec((N_CHIPS, tr, C), lambda i: (0, i, 0))],
        out_specs=pl.BlockSpec((tr, C), lambda i: (i, 0)),
        out_shape=jax.ShapeDtypeStruct((R, C), F32),
        compiler_params=_params(("parallel",)),
    )(r)


def _sum_rows8(g, m_per, name):
    n = g.shape[1]

    def body(g_ref, o_ref):
        acc = g_ref[0:m_per, :]
        for k in range(1, 8):
            acc = acc + g_ref[k * m_per:(k + 1) * m_per, :]
        o_ref[...] = acc

    return pl.pallas_call(
        body,
        name=name,
        out_shape=jax.ShapeDtypeStruct((m_per, n), F32),
        compiler_params=_params(),
    )(g)


def _adamw_math(w, g, m, v):
    m = ADAM_B1 * m + (1.0 - ADAM_B1) * g
    v = ADAM_B2 * v + (1.0 - ADAM_B2) * (g * g)
    m_hat = m / (1.0 - ADAM_B1 ** ADAM_STEP)
    v_hat = v / (1.0 - ADAM_B2 ** ADAM_STEP)
    delta = -ADAM_LR * (m_hat / (jnp.sqrt(v_hat) + ADAM_EPS) + ADAM_WD * w)
    return delta, m, v


def _adamw_big(w, g, m, v, name):
    R, C = w.shape
    tr = min(128, R)

    def body(w_ref, g_ref, m_ref, v_ref, d_out, m_out, v_out):
        d, mn, vn = _adamw_math(w_ref[...], g_ref[...], m_ref[...], v_ref[...])
        d_out[...] = d
        m_out[...] = mn
        v_out[...] = vn

    spec = pl.BlockSpec((tr, C), lambda i: (i, 0))
    return pl.pallas_call(
        body,
        name=name,
        grid=(R // tr,),
        in_specs=[spec] * 4,
        out_specs=[spec] * 3,
        out_shape=[jax.ShapeDtypeStruct((R, C), F32)] * 3,
        compiler_params=_params(("parallel",)),
    )(w, g, m, v)


def _adamw_small(ws, gs, ms, vs, name):
    n = len(ws)

    def body(*refs):
        for a in range(n):
            d, mn, vn = _adamw_math(refs[a][...], refs[n + a][...], refs[2 * n + a][...], refs[3 * n + a][...])
            refs[4 * n + a][...] = d
            refs[5 * n + a][...] = mn
            refs[6 * n + a][...] = vn

    shapes = [jax.ShapeDtypeStruct(w.shape, F32) for w in ws]
    outs = pl.pallas_call(
        body,
        name=name,
        out_shape=shapes * 3,
        compiler_params=_params(),
    )(*ws, *gs, *ms, *vs)
    return outs[:n], outs[n:2 * n], outs[2 * n:]


def _to_blockdiag(w):
    per = CW // LRU_BW
    w4 = w.reshape(N_CT, per, LRU_BW, LRU_BW)
    eye = jnp.eye(per, dtype=w.dtype)
    return (w4[:, :, :, None, :] * eye[None, :, None, :, None]).reshape(N_CT, CW, CW)


def _from_blockdiag(g):
    per = CW // LRU_BW
    g5 = g.reshape(N_CT, per, LRU_BW, per, LRU_BW)
    return jnp.stack([g5[:, b, :, b, :] for b in range(per)], axis=1).reshape(LRU_BLOCKS, LRU_BW, LRU_BW)


def _local_grads(x2d, tgt2d, B, S, g_in, w_all, conv_w, conv_b, gate_x_w, gate_x_b, gate_a_w, gate_a_b, lam, gain,
                 wpa, wpb, wout, g_fin):
    wx_bd = _c(_to_blockdiag(gate_x_w))
    wa_bd = _c(_to_blockdiag(gate_a_w))
    tables = _retention_tables(S)
    gain3 = gain.reshape(HEADS, 1, DK)

    proj, hb = _inproj_fwd(x2d, g_in, w_all)
    hlru, ya = _lru_fwd(proj, conv_w, conv_b, wx_bd, wa_bd, gate_x_b, gate_a_b, lam, B, S)
    o_pre, yb, states = _ret_fwd(proj, tables, gain3, B, S)
    loss, dx2, dya, dyb, dma, dmb, dgf, gw_proj = _mid(ya, yb, proj, x2d, tgt2d, wpa, wpb, wout, g_fin)
    dxa, dga, dcw, dcb, dwx_bd, dwa_bd, dbx, dba, dlam = _lru_bwd(
        dya, proj, hlru, conv_w, conv_b, wx_bd, wa_bd, gate_x_b, gate_a_b, lam, B, S)
    dq, dk, dv, dgb, dgain = _ret_bwd(dyb, o_pre, proj, states, tables, gain3, B, S)
    dparts = [dxa, dga, dq, dk, dv, dgb, dma, dmb]
    grad_x, dgin = _inproj_bwd_dx(dparts, w_all, x2d, dx2, g_in)
    gw_in = _inproj_bwd_dw(hb, dparts)
    small = dict(norm_in=dgin, conv_w=dcw, conv_b=dcb, gate_x_w=_from_blockdiag(dwx_bd), gate_x_b=dbx,
                 gate_a_w=_from_blockdiag(dwa_bd), gate_a_b=dba, lru_lambda=dlam, gn_gain=dgain.reshape(HEADS, DK),
                 norm_final=dgf)
    return loss[0, 0], grad_x, gw_in, gw_proj, small


_SMALL = ("gate_x_w", "gate_a_w", "norm_in", "conv_w", "conv_b", "gate_x_b", "gate_a_b", "lru_lambda", "gn_gain",
          "norm_final")
_SMALL_SHAPES = dict(gate_x_w=(LRU_BLOCKS, LRU_BW, LRU_BW), gate_a_w=(LRU_BLOCKS, LRU_BW, LRU_BW),
                     norm_in=(1, D_MODEL), conv_w=(CONV, D_MODEL), conv_b=(1, D_MODEL), gate_x_b=(1, D_MODEL),
                     gate_a_b=(1, D_MODEL), lru_lambda=(1, D_MODEL), gn_gain=(HEADS, DK), norm_final=(1, D_MODEL))


def _pack_small(small):
    return jnp.concatenate([small[k].reshape(-1, 128) for k in _SMALL], axis=0)


def _unpack_small(packed):
    out, r = {}, 0
    for k in _SMALL:
        shape = _SMALL_SHAPES[k]
        rows = 1
        for s in shape:
            rows *= s
        rows //= 128
        out[k] = packed[r:r + rows].reshape(shape)
        r += rows
    return out


def kernel(x, norm_in, w_in, conv_w, conv_b, gate_x_w, gate_x_b, gate_a_w, gate_a_b, lru_lambda, gn_gain, w_proj_a, w_proj_b, w_out, norm_final, loss_target, m_norm_in, m_w_in, m_conv_w, m_conv_b, m_gate_x_w, m_gate_x_b, m_gate_a_w, m_gate_a_b, m_lru_lambda, m_gn_gain, m_w_proj_a, m_w_proj_b, m_w_out, m_norm_final, v_norm_in, v_w_in, v_conv_w, v_conv_b, v_gate_x_w, v_gate_x_b, v_gate_a_w, v_gate_a_b, v_lru_lambda, v_gn_gain, v_w_proj_a, v_w_proj_b, v_w_out, v_norm_final):
    B, S, _ = x.shape
    T = B * S
    xi, yi, ci = _coords()
    chip = 2 * xi + yi

    big = _gather_chips([_c(w_in[0]), _c(w_proj_a[0]), _c(w_proj_b[0]), _c(w_out[0])], "gather_weights")
    w_all = big[0]
    wpa, wpb, wout = (b.reshape(D_MODEL, D_MODEL) for b in big[1:])
    cshard = D_MODEL // N_CHIPS
    gshard = DK // N_CHIPS
    tiny = jnp.concatenate([conv_w[0], jnp.zeros((4, cshard), F32), jnp.pad(gn_gain[0], ((0, 4), (0, cshard - gshard)))],
                           axis=0)
    tiny_all = _all_gather8(tiny, "gather_small_weights").reshape(N_CHIPS, 2, 16, cshard)[:, 0]
    conv_w_full = jnp.transpose(tiny_all[:, 0:CONV, :], (1, 0, 2)).reshape(CONV, D_MODEL)
    gain_full = jnp.transpose(tiny_all[:, 8:8 + HEADS, :gshard], (1, 0, 2)).reshape(HEADS, DK)

    loss, grad_x, gw_in, gw_proj, small = _local_grads(
        x.reshape(T, D_MODEL), loss_target.reshape(T, D_MODEL), B, S, norm_in, w_all, conv_w_full, conv_b,
        gate_x_w[0], gate_x_b, gate_a_w[0], gate_a_b, lru_lambda, gain_full, wpa, wpb, wout,
        norm_final.reshape(1, D_MODEL))
    loss = lax.psum(loss, ("x", "y", "c"))

    packed = _pack_small(small)
    m_per = packed.shape[0]
    gsm = _unpack_small(_sum_rows8(_all_gather8(packed, "gather_small_grads"), m_per, "sum_small_grads"))

    rows_p = 3 * D_MODEL // (2 * N_CHIPS)
    gw_proj = gw_proj.reshape(N_CHIPS, 2, rows_p, D_MODEL)
    other = _swap_halves([gw_in, gw_proj], "swap_halves")
    chip_in = _add_my_half(gw_in, other[0], "chip_sum_w_in")
    chip_pr = _add_my_half(gw_proj, other[1], "chip_sum_w_proj")
    got = _scatter_chips([chip_in, chip_pr], "scatter_chips")
    half_in = _sum_slabs(got[0], "sum_w_in")
    half_pr = _sum_slabs(got[1], "sum_w_proj")
    g_in_full, g_pr_full = _join_halves([half_in, half_pr], "join_halves")
    g_w_in = g_in_full.reshape(D_MODEL, 2 * D_MODEL)
    g_pr = g_pr_full.reshape(2, 3, D_MODEL // (2 * N_CHIPS), D_MODEL)
    g_wpa, g_wpb, g_wout = (g_pr[:, k].reshape(cshard, D_MODEL) for k in range(3))

    grads = dict(gsm)
    grads["conv_w"] = lax.dynamic_slice_in_dim(gsm["conv_w"], chip * cshard, cshard, axis=1)
    grads["gn_gain"] = lax.dynamic_slice_in_dim(gsm["gn_gain"], chip * gshard, gshard, axis=1)
    grads.update(w_in=g_w_in, w_proj_a=g_wpa, w_proj_b=g_wpb, w_out=g_wout)

    weights = dict(norm_in=norm_in, w_in=w_in, conv_w=conv_w, conv_b=conv_b, gate_x_w=gate_x_w, gate_x_b=gate_x_b,
                   gate_a_w=gate_a_w, gate_a_b=gate_a_b, lru_lambda=lru_lambda, gn_gain=gn_gain, w_proj_a=w_proj_a,
                   w_proj_b=w_proj_b, w_out=w_out, norm_final=norm_final)
    ms = dict(norm_in=m_norm_in, w_in=m_w_in, conv_w=m_conv_w, conv_b=m_conv_b, gate_x_w=m_gate_x_w,
              gate_x_b=m_gate_x_b, gate_a_w=m_gate_a_w, gate_a_b=m_gate_a_b, lru_lambda=m_lru_lambda, gn_gain=m_gn_gain,
              w_proj_a=m_w_proj_a, w_proj_b=m_w_proj_b, w_out=m_w_out, norm_final=m_norm_final)
    vs = dict(norm_in=v_norm_in, w_in=v_w_in, conv_w=v_conv_w, conv_b=v_conv_b, gate_x_w=v_gate_x_w,
              gate_x_b=v_gate_x_b, gate_a_w=v_gate_a_w, gate_a_b=v_gate_a_b, lru_lambda=v_lru_lambda, gn_gain=v_gn_gain,
              w_proj_a=v_w_proj_a, w_proj_b=v_w_proj_b, w_out=v_w_out, norm_final=v_norm_final)
    names = list(weights)
    grads = {k: grads[k].reshape(weights[k].shape) for k in names}

    delta, new_m, new_v = {}, {}, {}
    for k in ("w_in", "w_proj_a", "w_proj_b", "w_out"):
        shp = weights[k].shape
        two = lambda a: a.reshape(shp[1], shp[2])
        d, mn, vn = _adamw_big(two(weights[k]), two(grads[k]), two(ms[k]), two(vs[k]), "adamw_" + k)
        delta[k], new_m[k], new_v[k] = d.reshape(shp), mn.reshape(shp), vn.reshape(shp)
    smalls = [k for k in names if k not in delta]

    def view(a):
        return a.reshape(1, -1) if a.ndim == 1 else (a.reshape(a.shape[1:]) if a.ndim > 2 else a)

    ds, mns, vns = _adamw_small([view(weights[k]) for k in smalls], [view(grads[k]) for k in smalls],
                                [view(ms[k]) for k in smalls], [view(vs[k]) for k in smalls], "adamw_small")
    for k, d, mn, vn in zip(smalls, ds, mns, vns):
        shp = weights[k].shape
        delta[k], new_m[k], new_v[k] = d.reshape(shp), mn.reshape(shp), vn.reshape(shp)

    return (loss, grad_x.reshape(B, S, D_MODEL), *[grads[k] for k in names], *[delta[k] for k in names],
            *[new_m[k] for k in names], *[new_v[k] for k in names])
```

```python
import functools

import jax
import jax.numpy as jnp
from jax import lax
from jax.experimental import pallas as pl
from jax.experimental.pallas import tpu as pltpu

F32 = jnp.float32
_MXU = jnp.bfloat16

D_MODEL = 1024
N_GROUPS = 8
HEADS = 4
DK = 256
CHUNK = 128
CONV = 4
LRU_BLOCKS = 16
LRU_BW = 64
LRU_C = 8.0
ROPE_THETA = 10000.0
EPS = 1e-6
CW = 256
N_CT = D_MODEL // CW
N_CHIPS = 4
MESH = pl.DeviceIdType.MESH

ADAM_LR = 0.001
ADAM_B1 = 0.9
ADAM_B2 = 0.999
ADAM_EPS = 1e-08
ADAM_WD = 0.01
ADAM_STEP = 10

VMEM_LIMIT = 56 * 1024 * 1024


def _c(v):
    return v.astype(_MXU)


def _dot(a, b):
    return lax.dot_general(a, b, (((1,), (0,)), ((), ())), preferred_element_type=F32)


def _dot_nt(a, b):
    return lax.dot_general(a, b, (((1,), (1,)), ((), ())), preferred_element_type=F32)


def _dot_tn(a, b):
    return lax.dot_general(a, b, (((0,), (0,)), ((), ())), preferred_element_type=F32)


def _sigmoid(z):
    return 1.0 / (1.0 + jnp.exp(-z))


def _params(sem=None):
    if sem is None:
        return pltpu.CompilerParams(vmem_limit_bytes=VMEM_LIMIT)
    return pltpu.CompilerParams(vmem_limit_bytes=VMEM_LIMIT, dimension_semantics=sem)


def _inproj_fwd(x2d, g_in, w_all):
    T = x2d.shape[0]
    tm = min(512, T)

    def body(x_ref, g_ref, w_ref, proj_ref, hb_ref):
        @pl.when(pl.program_id(1) == 0)
        def _():
            x = x_ref[...]
            r = lax.rsqrt(jnp.mean(x * x, axis=-1, keepdims=True) + EPS)
            hb_ref[...] = (x * r * g_ref[...]).astype(hb_ref.dtype)

        proj_ref[...] = _dot(hb_ref[...], w_ref[0])

    return pl.pallas_call(
        body,
        name="inproj_fwd",
        grid=(T // tm, N_GROUPS),
        in_specs=[
            pl.BlockSpec((tm, D_MODEL), lambda i, j: (i, 0)),
            pl.BlockSpec((1, D_MODEL), lambda i, j: (0, 0)),
            pl.BlockSpec((1, D_MODEL, D_MODEL), lambda i, j: (j // 2, 0, j % 2)),
        ],
        out_specs=[
            pl.BlockSpec((tm, D_MODEL), lambda i, j: (i, j)),
            pl.BlockSpec((tm, D_MODEL), lambda i, j: (i, 0)),
        ],
        out_shape=[
            jax.ShapeDtypeStruct((T, N_GROUPS * D_MODEL), F32),
            jax.ShapeDtypeStruct((T, D_MODEL), _MXU),
        ],
        compiler_params=_params(("parallel", "arbitrary")),
    )(x2d, g_in, w_all)


def _scan_fwd(a, u):
    n = a.shape[0]
    row = lax.broadcasted_iota(jnp.int32, a.shape, 0)
    s = 1
    while s < n:
        m = row >= s
        u = u + a * jnp.where(m, pltpu.roll(u, s, 0), 0.0)
        a = a * jnp.where(m, pltpu.roll(a, s, 0), 1.0)
        s *= 2
    return a, u


def _scan_bwd(b, g):
    n = b.shape[0]
    row = lax.broadcasted_iota(jnp.int32, b.shape, 0)
    s = 1
    while s < n:
        m = row < n - s
        g = g + b * jnp.where(m, pltpu.roll(g, n - s, 0), 0.0)
        b = b * jnp.where(m, pltpu.roll(b, n - s, 0), 1.0)
        s *= 2
    return b, g


def _softplus_neg(lam):
    z = -lam
    return jnp.maximum(z, 0.0) + jnp.log1p(jnp.exp(-jnp.abs(z)))


def _lru_gates(xc, wx_ref, wa_ref, bx_ref, ba_ref, lam_ref):
    xcb = _c(xc)
    i_t = _sigmoid(_dot(xcb, wx_ref[0]) + bx_ref[...])
    r_t = _sigmoid(_dot(xcb, wa_ref[0]) + ba_ref[...])
    sp = _softplus_neg(lam_ref[...])
    log_a = (-LRU_C) * r_t * sp
    a = jnp.exp(log_a)
    mult = jnp.sqrt(1.0 - a * a)
    return xcb, i_t, r_t, sp, a, mult


def _conv_from_ext(ext_ref, xa, cw_ref, cb_ref, tc):
    return (cb_ref[...] + cw_ref[3:4, :] * xa + cw_ref[2:3, :] * ext_ref[7:7 + tc, :]
            + cw_ref[1:2, :] * ext_ref[6:6 + tc, :] + cw_ref[0:1, :] * ext_ref[5:5 + tc, :])


def _lru_fwd(proj, conv_w, conv_b, wx_bd, wa_bd, bx, ba, lam, B, S):
    T = B * S
    tc = min(256, S)
    nt = S // tc
    h8 = tc // 8

    def body(xa_ref, halo_ref, ga_ref, cw_ref, cb_ref, wx_ref, wa_ref, bx_ref, ba_ref, lam_ref,
             h_ref, ya_ref, ext_ref, carry_ref):
        t = pl.program_id(2)

        @pl.when(t == 0)
        def _():
            carry_ref[...] = jnp.zeros_like(carry_ref)

        xa = xa_ref[...]
        ext_ref[0:8, :] = jnp.where(t == 0, 0.0, halo_ref[...])
        ext_ref[8:8 + tc, :] = xa
        xc = _conv_from_ext(ext_ref, xa, cw_ref, cb_ref, tc)
        _, i_t, _, _, a, mult = _lru_gates(xc, wx_ref, wa_ref, bx_ref, ba_ref, lam_ref)
        u = mult * (i_t * xc)
        acum, hloc = _scan_fwd(a, u)
        h = hloc + acum * carry_ref[7:8, :]
        h_ref[...] = h
        carry_ref[...] = h[tc - 8:tc, :]
        ga = ga_ref[...]
        ya_ref[...] = (ga * _sigmoid(ga) * h).astype(ya_ref.dtype)

    row = lambda b, t: b * nt + t
    vec = pl.BlockSpec((1, CW), lambda b, c, t: (0, c))
    mat = pl.BlockSpec((1, CW, CW), lambda b, c, t: (c, 0, 0))
    return pl.pallas_call(
        body,
        name="lru_fwd",
        grid=(B, N_CT, nt),
        in_specs=[
            pl.BlockSpec((tc, CW), lambda b, c, t: (row(b, t), c)),
            pl.BlockSpec((8, CW), lambda b, c, t: (jnp.maximum(row(b, t) * h8 - 1, 0), c)),
            pl.BlockSpec((tc, CW), lambda b, c, t: (row(b, t), N_CT + c)),
            pl.BlockSpec((CONV, CW), lambda b, c, t: (0, c)),
            vec, mat, mat, vec, vec, vec,
        ],
        out_specs=[
            pl.BlockSpec((tc, CW), lambda b, c, t: (row(b, t), c)),
            pl.BlockSpec((tc, CW), lambda b, c, t: (row(b, t), c)),
        ],
        out_shape=[
            jax.ShapeDtypeStruct((T, D_MODEL), F32),
            jax.ShapeDtypeStruct((T, D_MODEL), _MXU),
        ],
        scratch_shapes=[pltpu.VMEM((tc + 8, CW), F32), pltpu.VMEM((8, CW), F32)],
        compiler_params=_params(("parallel", "parallel", "arbitrary")),
    )(proj, proj, proj, conv_w, conv_b, wx_bd, wa_bd, bx, ba, lam)


def _lru_bwd(dya, proj, hlru, conv_w, conv_b, wx_bd, wa_bd, bx, ba, lam, B, S):
    T = B * S
    tc = min(256, S)
    nt = S // tc
    h8 = tc // 8

    def body(dya_ref, xa_ref, xhalo_ref, ga_ref, h_ref, hhalo_ref, cw_ref, cb_ref, wx_ref, wa_ref, bx_ref, ba_ref,
             lam_ref, dxa_ref, dga_ref, dcw_ref, dcb_ref, dwx_ref, dwa_ref, dbx_ref, dba_ref, dlam_ref,
             ext_ref, ext2_ref, carry_ref, dhalo_ref):
        b = pl.program_id(1)
        t = pl.program_id(2)
        tt = nt - 1 - t

        @pl.when(t == 0)
        def _():
            carry_ref[...] = jnp.zeros_like(carry_ref)
            dhalo_ref[...] = jnp.zeros_like(dhalo_ref)

        @pl.when((t == 0) & (b == 0))
        def _():
            for r in (dcw_ref, dcb_ref, dwx_ref, dwa_ref, dbx_ref, dba_ref, dlam_ref):
                r[...] = jnp.zeros_like(r)

        xa = xa_ref[...]
        ext_ref[0:8, :] = jnp.where(tt == 0, 0.0, xhalo_ref[...])
        ext_ref[8:8 + tc, :] = xa
        xc = _conv_from_ext(ext_ref, xa, cw_ref, cb_ref, tc)
        xcb, i_t, r_t, sp, a, mult = _lru_gates(xc, wx_ref, wa_ref, bx_ref, ba_ref, lam_ref)

        h = h_ref[...]
        ga = ga_ref[...]
        dya_t = dya_ref[...]
        sg = _sigmoid(ga)
        dga_ref[...] = (dya_t * h * (sg * (1.0 + ga * (1.0 - sg)))).astype(dga_ref.dtype)
        dlru = dya_t * (ga * sg)

        row = lax.broadcasted_iota(jnp.int32, a.shape, 0)
        coef = jnp.where(row == tc - 1, 1.0, pltpu.roll(a, tc - 1, 0))
        bcum, dloc = _scan_bwd(coef, dlru)
        dh = dloc + bcum * carry_ref[0:1, :]
        ext2_ref[0:tc, :] = a * dh
        carry_ref[...] = ext2_ref[0:8, :]

        ext2_ref[0:8, :] = jnp.where(tt == 0, 0.0, hhalo_ref[...])
        ext2_ref[8:8 + tc, :] = h
        hprev = ext2_ref[7:7 + tc, :]

        da = dh * hprev
        ix = i_t * xc
        dmult = dh * ix
        di = dh * mult * xc
        dxc = dh * mult * i_t
        dlog_a = da * a - dmult * (a * a) / mult
        dr = dlog_a * ((-LRU_C) * sp)
        dlam_ref[...] += jnp.sum(dlog_a * r_t, axis=0, keepdims=True) * (LRU_C * _sigmoid(-lam_ref[...]))
        dza = dr * r_t * (1.0 - r_t)
        dzx = di * i_t * (1.0 - i_t)
        dzab = _c(dza)
        dzxb = _c(dzx)
        dxc = dxc + _dot_nt(dzxb, wx_ref[0]) + _dot_nt(dzab, wa_ref[0])
        dwx_ref[0] += _dot_tn(xcb, dzxb)
        dwa_ref[0] += _dot_tn(xcb, dzab)
        dbx_ref[...] += jnp.sum(dzx, axis=0, keepdims=True)
        dba_ref[...] += jnp.sum(dza, axis=0, keepdims=True)

        dcb_ref[...] += jnp.sum(dxc, axis=0, keepdims=True)
        dcw_ref[3:4, :] += jnp.sum(dxc * xa, axis=0, keepdims=True)
        dcw_ref[2:3, :] += jnp.sum(dxc * ext_ref[7:7 + tc, :], axis=0, keepdims=True)
        dcw_ref[1:2, :] += jnp.sum(dxc * ext_ref[6:6 + tc, :], axis=0, keepdims=True)
        dcw_ref[0:1, :] += jnp.sum(dxc * ext_ref[5:5 + tc, :], axis=0, keepdims=True)
        ext2_ref[0:tc, :] = dxc
        ext2_ref[tc:tc + 8, :] = dhalo_ref[...]
        dxa = (cw_ref[3:4, :] * dxc + cw_ref[2:3, :] * ext2_ref[1:1 + tc, :]
               + cw_ref[1:2, :] * ext2_ref[2:2 + tc, :] + cw_ref[0:1, :] * ext2_ref[3:3 + tc, :])
        dxa_ref[...] = dxa.astype(dxa_ref.dtype)
        dhalo_ref[...] = ext2_ref[0:8, :]

    row_of = lambda b, t: b * nt + (nt - 1 - t)
    tile = lambda off: pl.BlockSpec((tc, CW), lambda c, b, t: (row_of(b, t), off + c))
    halo = pl.BlockSpec((8, CW), lambda c, b, t: (jnp.maximum(row_of(b, t) * h8 - 1, 0), c))
    vec = pl.BlockSpec((1, CW), lambda c, b, t: (0, c))
    mat = pl.BlockSpec((1, CW, CW), lambda c, b, t: (c, 0, 0))
    cwspec = pl.BlockSpec((CONV, CW), lambda c, b, t: (0, c))
    return pl.pallas_call(
        body,
        name="lru_bwd",
        grid=(N_CT, B, nt),
        in_specs=[tile(0), tile(0), halo, tile(N_CT), tile(0), halo, cwspec, vec, mat, mat, vec, vec, vec],
        out_specs=[tile(0), tile(0), cwspec, vec, mat, mat, vec, vec, vec],
        out_shape=[
            jax.ShapeDtypeStruct((T, D_MODEL), _MXU),
            jax.ShapeDtypeStruct((T, D_MODEL), _MXU),
            jax.ShapeDtypeStruct((CONV, D_MODEL), F32),
            jax.ShapeDtypeStruct((1, D_MODEL), F32),
            jax.ShapeDtypeStruct((N_CT, CW, CW), F32),
            jax.ShapeDtypeStruct((N_CT, CW, CW), F32),
            jax.ShapeDtypeStruct((1, D_MODEL), F32),
            jax.ShapeDtypeStruct((1, D_MODEL), F32),
            jax.ShapeDtypeStruct((1, D_MODEL), F32),
        ],
        scratch_shapes=[pltpu.VMEM((tc + 8, CW), F32), pltpu.VMEM((tc + 8, CW), F32),
                        pltpu.VMEM((8, CW), F32), pltpu.VMEM((8, CW), F32)],
        compiler_params=_params(("parallel", "arbitrary", "arbitrary")),
    )(dya, proj, proj, proj, hlru, hlru, conv_w, conv_b, wx_bd, wa_bd, bx, ba, lam)


def _retention_tables(S):
    half = DK // 2
    freqs = ROPE_THETA ** (-jnp.arange(half, dtype=F32) / half)
    ang = jnp.arange(S, dtype=F32)[:, None] * freqs[None, :]
    log_g = jnp.log1p(-(2.0 ** (-5.0 - jnp.arange(HEADS, dtype=F32))))
    idx = jnp.arange(CHUNK, dtype=F32)
    diff = idx[:, None] - idx[None, :]
    inner = jnp.where(diff >= 0, jnp.exp(jnp.maximum(diff, 0.0)[None] * log_g[:, None, None]), 0.0)
    cross = jnp.exp((idx[None, :] + 1.0) * log_g[:, None])[:, :, None]
    state = jnp.exp((CHUNK - 1.0 - idx[None, :]) * log_g[:, None])[:, :, None]
    gam = jnp.broadcast_to(jnp.exp(CHUNK * log_g)[:, None, None], (HEADS, 1, DK))
    return jnp.cos(ang), jnp.sin(ang), inner, cross, state, gam


def _rot(x, cos, sin):
    half = DK // 2
    x1, x2 = x[:, :half], x[:, half:]
    return jnp.concatenate([x1 * cos - x2 * sin, x1 * sin + x2 * cos], axis=-1)


def _rot_t(y, cos, sin):
    half = DK // 2
    y1, y2 = y[:, :half], y[:, half:]
    return jnp.concatenate([y1 * cos + y2 * sin, y2 * cos - y1 * sin], axis=-1)


def _groupnorm(o):
    mu = jnp.mean(o, axis=-1, keepdims=True)
    oc = o - mu
    rs = lax.rsqrt(jnp.mean(oc * oc, axis=-1, keepdims=True) + EPS)
    return oc * rs, rs


def _ret_specs(S, order):
    nc = S // CHUNK
    qkv = lambda g: pl.BlockSpec((CHUNK, DK), lambda *i: (order(*i)[0] * nc + order(*i)[2], g * HEADS + order(*i)[1]))
    act = pl.BlockSpec((CHUNK, DK), lambda *i: (order(*i)[0] * nc + order(*i)[2], order(*i)[1]))
    rope = pl.BlockSpec((CHUNK, DK // 2), lambda *i: (order(*i)[2], 0))
    dmat = pl.BlockSpec((1, CHUNK, CHUNK), lambda *i: (order(*i)[1], 0, 0))
    dvec = pl.BlockSpec((1, CHUNK, 1), lambda *i: (order(*i)[1], 0, 0))
    hrow = pl.BlockSpec((1, 1, DK), lambda *i: (order(*i)[1], 0, 0))
    rst = pl.BlockSpec((1, DK, DK), lambda *i: ((order(*i)[0] * HEADS + order(*i)[1]) * nc + order(*i)[2], 0, 0))
    return qkv, act, rope, dmat, dvec, hrow, rst


def _ret_fwd(proj, tables, gain3, B, S):
    T = B * S
    nc = S // CHUNK
    cos, sin, dmat_t, cd_t, sd_t, gam_t = tables

    def body(q_ref, k_ref, v_ref, gb_ref, cos_ref, sin_ref, dm_ref, cd_ref, sd_ref, gam_ref, gain_ref,
             o_ref, yb_ref, rs_ref, state_ref):
        @pl.when(pl.program_id(2) == 0)
        def _():
            state_ref[...] = jnp.zeros_like(state_ref)

        cos_t, sin_t = cos_ref[...], sin_ref[...]
        qb = _c(_rot(q_ref[...], cos_t, sin_t))
        kb = _c(_rot(k_ref[...], cos_t, sin_t) * (DK ** -0.5))
        v = v_ref[...]
        state = state_ref[...]
        sb = _c(state)
        rs_ref[0] = sb
        scores = _dot_nt(qb, kb) * dm_ref[0]
        o = _dot(_c(scores), _c(v)) + _dot(qb, sb) * cd_ref[0]
        state_ref[...] = gam_ref[0] * state + _dot_tn(kb, _c(v * sd_ref[0]))
        o_ref[...] = o
        n, _ = _groupnorm(o)
        gb = gb_ref[...]
        yb_ref[...] = (gb * _sigmoid(gb) * (n * gain_ref[0])).astype(yb_ref.dtype)

    qkv, act, rope, dmat, dvec, hrow, rst = _ret_specs(S, lambda b, h, c: (b, h, c))
    return pl.pallas_call(
        body,
        name="ret_fwd",
        grid=(B, HEADS, nc),
        in_specs=[qkv(2), qkv(3), qkv(4), qkv(5), rope, rope, dmat, dvec, dvec, hrow, hrow],
        out_specs=[act, act, rst],
        out_shape=[
            jax.ShapeDtypeStruct((T, D_MODEL), F32),
            jax.ShapeDtypeStruct((T, D_MODEL), _MXU),
            jax.ShapeDtypeStruct((B * HEADS * nc, DK, DK), _MXU),
        ],
        scratch_shapes=[pltpu.VMEM((DK, DK), F32)],
        compiler_params=_params(("parallel", "parallel", "arbitrary")),
    )(proj, proj, proj, proj, cos, sin, dmat_t, cd_t, sd_t, gam_t, gain3)


def _ret_bwd(dyb, o_pre, proj, states, tables, gain3, B, S):
    T = B * S
    nc = S // CHUNK
    cos, sin, dmat_t, cd_t, sd_t, gam_t = tables

    def body(dyb_ref, o_ref, q_ref, k_ref, v_ref, gb_ref, rs_ref, cos_ref, sin_ref, dm_ref, cd_ref, sd_ref, gam_ref,
             gain_ref, dq_ref, dk_ref, dv_ref, dgb_ref, dgain_ref, dstate_ref):
        @pl.when(pl.program_id(2) == 0)
        def _():
            dstate_ref[...] = jnp.zeros_like(dstate_ref)

        @pl.when((pl.program_id(2) == 0) & (pl.program_id(1) == 0))
        def _():
            dgain_ref[...] = jnp.zeros_like(dgain_ref)

        gain = gain_ref[0]
        n, rs = _groupnorm(o_ref[...])
        gb = gb_ref[...]
        sg = _sigmoid(gb)
        dy = dyb_ref[...]
        dgb_ref[...] = (dy * (n * gain) * (sg * (1.0 + gb * (1.0 - sg)))).astype(dgb_ref.dtype)
        dgn = dy * (gb * sg)
        dgain_ref[0] += jnp.sum(dgn * n, axis=0, keepdims=True)
        dn = dgn * gain
        do = rs * (dn - jnp.mean(dn, axis=-1, keepdims=True) - n * jnp.mean(dn * n, axis=-1, keepdims=True))

        cos_t, sin_t = cos_ref[...], sin_ref[...]
        qb = _c(_rot(q_ref[...], cos_t, sin_t))
        kb = _c(_rot(k_ref[...], cos_t, sin_t) * (DK ** -0.5))
        v = v_ref[...]
        vb = _c(v)
        vsb = _c(v * sd_ref[0])
        dob = _c(do)
        docb = _c(do * cd_ref[0])
        dmat = dm_ref[0]
        dstate = dstate_ref[...]
        dsb = _c(dstate)
        pb = _c(_dot_nt(qb, kb) * dmat)
        dsc = _c(_dot_nt(dob, vb) * dmat)
        dq = _dot(dsc, kb) + _dot_nt(docb, rs_ref[0])
        dk = _dot_tn(dsc, qb) + _dot_nt(vsb, dsb)
        dv = _dot_tn(pb, dob) + _dot(kb, dsb) * sd_ref[0]
        dstate_ref[...] = gam_ref[0] * dstate + _dot_tn(qb, docb)
        dq_ref[...] = _rot_t(dq, cos_t, sin_t).astype(dq_ref.dtype)
        dk_ref[...] = (_rot_t(dk, cos_t, sin_t) * (DK ** -0.5)).astype(dk_ref.dtype)
        dv_ref[...] = dv.astype(dv_ref.dtype)

    qkv, act, rope, dmat, dvec, hrow, rst = _ret_specs(S, lambda h, b, c: (b, h, nc - 1 - c))
    big = jax.ShapeDtypeStruct((T, D_MODEL), _MXU)
    return pl.pallas_call(
        body,
        name="ret_bwd",
        grid=(HEADS, B, nc),
        in_specs=[act, act, qkv(2), qkv(3), qkv(4), qkv(5), rst, rope, rope, dmat, dvec, dvec, hrow, hrow],
        out_specs=[act, act, act, act, hrow],
        out_shape=[big, big, big, big, jax.ShapeDtypeStruct((HEADS, 1, DK), F32)],
        scratch_shapes=[pltpu.VMEM((DK, DK), F32)],
        compiler_params=_params(("parallel", "arbitrary", "arbitrary")),
    )(dyb, o_pre, proj, proj, proj, proj, states, cos, sin, dmat_t, cd_t, sd_t, gam_t, gain3)


def _mid(ya, yb, proj, x2d, tgt2d, wpa, wpb, wout, g_fin):
    T = x2d.shape[0]
    tm = min(256, T)
    n_steps = T // tm
    rows = D_MODEL // (2 * N_CHIPS)

    def body(ya_ref, yb_ref, ma_ref, mb_ref, x_ref, t_ref, gf_ref, wpa_hbm, wpb_hbm, wout_hbm,
             loss_ref, dx2_ref, dya_ref, dyb_ref, dma_ref, dmb_ref, dgf_ref, gw_hbm, w_ref, acc_ref, sem):
        i = pl.program_id(0)

        @pl.when(i == 0)
        def _():
            loads = [pltpu.make_async_copy(src, w_ref.at[k], sem.at[k]) for k, src in enumerate((wpa_hbm, wpb_hbm, wout_hbm))]
            for cp in loads:
                cp.start()
            for cp in loads:
                cp.wait()
            acc_ref[...] = jnp.zeros_like(acc_ref)
            loss_ref[...] = jnp.zeros_like(loss_ref)
            dgf_ref[...] = jnp.zeros_like(dgf_ref)

        ya_t, yb_t = ya_ref[...], yb_ref[...]
        out_a = _dot(ya_t, w_ref[0])
        out_b = _dot(yb_t, w_ref[1])
        sa = _sigmoid(ma_ref[...])
        sb = _sigmoid(mb_ref[...])
        mgb = _c(sa * out_a + sb * out_b)
        x2 = x_ref[...] + _dot(mgb, w_ref[2])
        r2 = lax.rsqrt(jnp.mean(x2 * x2, axis=-1, keepdims=True) + EPS)
        nx = x2 * r2
        gf = gf_ref[...]
        err = nx * gf - t_ref[...]
        loss_ref[...] += 0.5 * jnp.sum(jnp.mean(err * err, axis=-1, keepdims=True), axis=0, keepdims=True)
        dy = err * (1.0 / D_MODEL)
        dgf_ref[...] += jnp.sum(dy * nx, axis=0, keepdims=True)
        dyg = dy * gf
        dx2 = r2 * (dyg - nx * jnp.mean(dyg * nx, axis=-1, keepdims=True))
        dx2_ref[...] = dx2
        dx2b = _c(dx2)
        dmg = _dot_nt(dx2b, w_ref[2])
        acc_ref[2] += _dot_tn(mgb, dx2b)
        dma_ref[...] = (dmg * out_a * sa * (1.0 - sa)).astype(dma_ref.dtype)
        dmb_ref[...] = (dmg * out_b * sb * (1.0 - sb)).astype(dmb_ref.dtype)
        dab = _c(dmg * sa)
        dbb = _c(dmg * sb)
        dya_ref[...] = _dot_nt(dab, w_ref[0])
        dyb_ref[...] = _dot_nt(dbb, w_ref[1])
        acc_ref[0] += _dot_tn(ya_t, dab)
        acc_ref[1] += _dot_tn(yb_t, dbb)

        @pl.when(i == n_steps - 1)
        def _():
            copies = [pltpu.make_async_copy(acc_ref.at[k, pl.ds((2 * p + hf) * rows, rows), :], gw_hbm.at[p, hf, k],
                                            sem.at[(k * N_CHIPS + p) * 2 + hf])
                      for k in range(3) for p in range(N_CHIPS) for hf in range(2)]
            for cp in copies:
                cp.start()
            for cp in copies:
                cp.wait()

    tile = lambda j: pl.BlockSpec((tm, D_MODEL), lambda i: (i, j))
    one = pl.BlockSpec((1, D_MODEL), lambda i: (0, 0))
    anyspec = pl.BlockSpec(memory_space=pl.ANY)
    return pl.pallas_call(
        body,
        name="mid",
        grid=(n_steps,),
        in_specs=[tile(0), tile(0), tile(6), tile(7), tile(0), tile(0), one, anyspec, anyspec, anyspec],
        out_specs=[pl.BlockSpec((1, 1), lambda i: (0, 0)), tile(0), tile(0), tile(0), tile(0), tile(0), one, anyspec],
        out_shape=[
            jax.ShapeDtypeStruct((1, 1), F32),
            jax.ShapeDtypeStruct((T, D_MODEL), F32),
            jax.ShapeDtypeStruct((T, D_MODEL), F32),
            jax.ShapeDtypeStruct((T, D_MODEL), F32),
            jax.ShapeDtypeStruct((T, D_MODEL), _MXU),
            jax.ShapeDtypeStruct((T, D_MODEL), _MXU),
            jax.ShapeDtypeStruct((1, D_MODEL), F32),
            jax.ShapeDtypeStruct((N_CHIPS, 2, 3, rows, D_MODEL), F32),
        ],
        scratch_shapes=[pltpu.VMEM((3, D_MODEL, D_MODEL), _MXU), pltpu.VMEM((3, D_MODEL, D_MODEL), F32),
                        pltpu.SemaphoreType.DMA((3 * N_CHIPS * 2,))],
        compiler_params=_params(("arbitrary",)),
    )(ya, yb, proj, proj, x2d, tgt2d, g_fin, wpa, wpb, wout)


def _inproj_bwd_dx(dparts, w_all, x2d, dx2, g_in):
    T = x2d.shape[0]
    tm = min(256, T)

    def body(*refs):
        d_refs = refs[:N_GROUPS]
        x_ref, dx2_ref, g_ref, w_hbm, dx_ref, dg_ref, w_ref, sem = refs[N_GROUPS:]

        @pl.when(pl.program_id(0) == 0)
        def _():
            cp = pltpu.make_async_copy(w_hbm, w_ref, sem)
            cp.start()
            cp.wait()
            dg_ref[...] = jnp.zeros_like(dg_ref)

        dh = jnp.zeros((tm, D_MODEL), F32)
        for j in range(N_GROUPS):
            dh = dh + _dot_nt(d_refs[j][...], w_ref[j // 2, :, (j % 2) * D_MODEL:(j % 2 + 1) * D_MODEL])
        x = x_ref[...]
        r = lax.rsqrt(jnp.mean(x * x, axis=-1, keepdims=True) + EPS)
        nx = x * r
        dg_ref[...] += jnp.sum(dh * nx, axis=0, keepdims=True)
        dhg = dh * g_ref[...]
        dx_ref[...] = dx2_ref[...] + r * (dhg - nx * jnp.mean(dhg * nx, axis=-1, keepdims=True))

    tile = pl.BlockSpec((tm, D_MODEL), lambda i: (i, 0))
    one = pl.BlockSpec((1, D_MODEL), lambda i: (0, 0))
    return pl.pallas_call(
        body,
        name="inproj_bwd_dx",
        grid=(T // tm,),
        in_specs=[tile] * N_GROUPS + [tile, tile, one, pl.BlockSpec(memory_space=pl.ANY)],
        out_specs=[tile, one],
        out_shape=[jax.ShapeDtypeStruct((T, D_MODEL), F32), jax.ShapeDtypeStruct((1, D_MODEL), F32)],
        scratch_shapes=[pltpu.VMEM(w_all.shape, w_all.dtype), pltpu.SemaphoreType.DMA],
        compiler_params=_params(("arbitrary",)),
    )(*dparts, x2d, dx2, g_in, w_all)


def _inproj_bwd_dw(hb, dparts):
    T = hb.shape[0]
    tm = min(512, T)
    half = D_MODEL // 2

    def body(*refs):
        hb_ref = refs[0]
        d_refs = refs[1:1 + N_GROUPS]
        out_ref = refs[1 + N_GROUPS]
        j = pl.program_id(0)

        @pl.when(pl.program_id(1) == 0)
        def _():
            out_ref[...] = jnp.zeros_like(out_ref)

        for jj in range(N_GROUPS):
            @pl.when(j == jj)
            def _(jj=jj):
                g = _dot_tn(hb_ref[...], d_refs[jj][...])
                out_ref[0, 0] += g[:half]
                out_ref[0, 1] += g[half:]

    def dspec(jj):
        return pl.BlockSpec((tm, D_MODEL), lambda j, i: (jnp.where(j == jj, i, 0), 0))

    return pl.pallas_call(
        body,
        name="inproj_bwd_dw",
        grid=(N_GROUPS, T // tm),
        in_specs=[pl.BlockSpec((tm, D_MODEL), lambda j, i: (i, 0))] + [dspec(jj) for jj in range(N_GROUPS)],
        out_specs=pl.BlockSpec((1, 2, half, D_MODEL), lambda j, i: (j // 2, 0, 0, j % 2)),
        out_shape=jax.ShapeDtypeStruct((N_CHIPS, 2, half, 2 * D_MODEL), F32),
        compiler_params=_params(("parallel", "arbitrary")),
    )(hb, *dparts)


def _coords():
    return lax.axis_index("x"), lax.axis_index("y"), lax.axis_index("c")


def _other_chips(x, y):
    return [(1 - x, y), (x, 1 - y), (1 - x, 1 - y)]


def _all_gather8(xs, name):
    m_per, n = xs.shape

    def body(x_ref, out_ref, send_sems, recv_sems, local_sem):
        x, y, c = _coords()
        me, sibling = (x, y, c), (x, y, 1 - c)
        chips = _other_chips(x, y)

        def rows(px, py, pc):
            return out_ref.at[pl.ds((4 * px + 2 * py + pc) * m_per, m_per), :]

        def copy(k, block, to, src=None):
            return pltpu.make_async_remote_copy(
                src_ref=rows(*block) if src is None else src, dst_ref=rows(*block),
                send_sem=send_sems.at[k], recv_sem=recv_sems.at[k], device_id=to, device_id_type=MESH)

        mine = pltpu.make_async_copy(x_ref, rows(*me), local_sem)
        mine.start()
        first = [copy(0, me, sibling, src=x_ref)]
        first += [copy(1 + j, me, (*chip, c), src=x_ref) for j, chip in enumerate(chips)]
        for cp in first:
            cp.start()
        passed = [copy(4 + j, (*chip, c), sibling) for j, chip in enumerate(chips)]
        for j, chip in enumerate(chips):
            copy(1 + j, (*chip, c), me).wait_recv()
            passed[j].start()
        copy(0, sibling, me).wait_recv()
        for j, chip in enumerate(chips):
            copy(4 + j, (*chip, 1 - c), me).wait_recv()
        for cp in first + passed:
            cp.wait_send()
        mine.wait()

    return pl.pallas_call(
        body,
        name=name,
        out_shape=jax.ShapeDtypeStruct((8 * m_per, n), xs.dtype),
        in_specs=[pl.BlockSpec(memory_space=pltpu.VMEM)],
        out_specs=pl.BlockSpec(memory_space=pltpu.VMEM),
        scratch_shapes=[pltpu.SemaphoreType.DMA((7,)), pltpu.SemaphoreType.DMA((7,)), pltpu.SemaphoreType.DMA],
        compiler_params=pltpu.CompilerParams(vmem_limit_bytes=VMEM_LIMIT),
    )(xs)


def _chunks(rows, n):
    size = rows // n
    return [pl.ds(q * size, size) for q in range(n)]


def _gather_chips(shards, n_chunks, name):
    n = len(shards)
    pieces = [(a, rows) for a in range(n) for rows in _chunks(shards[a].shape[1], n_chunks[a])]
    n_p = len(pieces)

    def body(*refs):
        ins, outs = refs[:n], refs[n:2 * n]
        send_sems, recv_sems, fsend_sems, frecv_sems, local_sems = refs[2 * n:]
        x, y, c = _coords()
        me = 2 * x + y
        chips = _other_chips(x, y)
        local = [pltpu.make_async_copy(ins[a].at[hf], outs[a].at[me, hf], local_sems.at[2 * a + hf])
                 for a in range(n) for hf in range(2)]
        for cp in local:
            cp.start()

        def send(k, i, slot, chip):
            a, rows = pieces[i]
            return pltpu.make_async_remote_copy(
                src_ref=ins[a].at[c, rows], dst_ref=outs[a].at[slot, c, rows], send_sem=send_sems.at[k * n_p + i],
                recv_sem=recv_sems.at[k * n_p + i], device_id=(*chip, c), device_id_type=MESH)

        def forward(k, i, slot, half):
            a, rows = pieces[i]
            return pltpu.make_async_remote_copy(
                src_ref=outs[a].at[slot, half, rows], dst_ref=outs[a].at[slot, half, rows],
                send_sem=fsend_sems.at[k * n_p + i], recv_sem=frecv_sems.at[k * n_p + i],
                device_id=(x, y, 1 - c), device_id_type=MESH)

        sends = [send(k, i, me, chip) for i in range(n_p) for k, chip in enumerate(chips)]
        for cp in sends:
            cp.start()
        forwards = []
        for i in range(n_p):
            for k, (px, py) in enumerate(chips):
                send(k, i, 2 * px + py, (px, py)).wait_recv()
                fw = forward(k, i, 2 * px + py, c)
                fw.start()
                forwards.append(fw)
        for i in range(n_p):
            for k, (px, py) in enumerate(chips):
                forward(k, i, 2 * px + py, 1 - c).wait_recv()
        for cp in sends + forwards:
            cp.wait_send()
        for cp in local:
            cp.wait()

    anyspec = pl.BlockSpec(memory_space=pl.ANY)
    sems = pltpu.SemaphoreType.DMA((3 * n_p,))
    return pl.pallas_call(
        body,
        name=name,
        in_specs=[anyspec] * n,
        out_specs=[anyspec] * n,
        out_shape=[jax.ShapeDtypeStruct((N_CHIPS,) + s.shape, s.dtype) for s in shards],
        scratch_shapes=[sems, sems, sems, sems, pltpu.SemaphoreType.DMA((2 * n,))],
    )(*shards)


def _swap_halves(arrs, name):
    n = len(arrs)

    def body(*refs):
        ins, outs = refs[:n], refs[n:2 * n]
        send_sems, recv_sems = refs[2 * n:]
        x, y, c = _coords()
        copies = [pltpu.make_async_remote_copy(
            src_ref=ins[a].at[p, 1 - c], dst_ref=outs[a].at[p], send_sem=send_sems.at[a * N_CHIPS + p],
            recv_sem=recv_sems.at[a * N_CHIPS + p], device_id=(x, y, 1 - c), device_id_type=MESH)
            for a in range(n) for p in range(N_CHIPS)]
        for cp in copies:
            cp.start()
        for cp in copies:
            cp.wait()

    anyspec = pl.BlockSpec(memory_space=pl.ANY)
    return pl.pallas_call(
        body,
        name=name,
        in_specs=[anyspec] * n,
        out_specs=[anyspec] * n,
        out_shape=[jax.ShapeDtypeStruct((N_CHIPS,) + a.shape[2:], a.dtype) for a in arrs],
        scratch_shapes=[pltpu.SemaphoreType.DMA((N_CHIPS * n,)), pltpu.SemaphoreType.DMA((N_CHIPS * n,))],
    )(*arrs)


def _scatter_chips(arrs, n_chunks, name):
    n = len(arrs)
    pieces = [(a, rows) for a in range(n) for rows in _chunks(arrs[a].shape[1], n_chunks[a])]
    n_p = len(pieces)

    def body(*refs):
        ins, outs = refs[:n], refs[n:2 * n]
        send_sems, recv_sems, local_sems = refs[2 * n:]
        x, y, c = _coords()
        me = 2 * x + y
        chips = _other_chips(x, y)
        local = [pltpu.make_async_copy(ins[a].at[me, rows], outs[a].at[me, rows], local_sems.at[i])
                 for i, (a, rows) in enumerate(pieces)]
        for cp in local:
            cp.start()

        def copy(k, i, src_slot, dst_slot, chip):
            a, rows = pieces[i]
            return pltpu.make_async_remote_copy(
                src_ref=ins[a].at[src_slot, rows], dst_ref=outs[a].at[dst_slot, rows],
                send_sem=send_sems.at[k * n_p + i], recv_sem=recv_sems.at[k * n_p + i],
                device_id=(*chip, c), device_id_type=MESH)

        sends = [copy(k, i, 2 * px + py, me, (px, py)) for i in range(n_p) for k, (px, py) in enumerate(chips)]
        for cp in sends:
            cp.start()
        for i in range(n_p):
            for k, (px, py) in enumerate(chips):
                copy(k, i, me, 2 * px + py, (px, py)).wait_recv()
        for cp in sends:
            cp.wait_send()
        for cp in local:
            cp.wait()

    anyspec = pl.BlockSpec(memory_space=pl.ANY)
    return pl.pallas_call(
        body,
        name=name,
        in_specs=[anyspec] * n,
        out_specs=[anyspec] * n,
        out_shape=[jax.ShapeDtypeStruct(a.shape, a.dtype) for a in arrs],
        scratch_shapes=[pltpu.SemaphoreType.DMA((3 * n_p,)), pltpu.SemaphoreType.DMA((3 * n_p,)),
                        pltpu.SemaphoreType.DMA((n_p,))],
    )(*arrs)


def _join_halves(halves, n_chunks, name):
    n = len(halves)
    pieces = [(a, rows) for a in range(n) for rows in _chunks(halves[a].shape[0], n_chunks[a])]
    n_p = len(pieces)

    def body(*refs):
        ins, outs = refs[:n], refs[n:2 * n]
        send_sems, recv_sems, local_sems = refs[2 * n:]
        x, y, c = _coords()
        local = [pltpu.make_async_copy(ins[a].at[rows], outs[a].at[c, rows], local_sems.at[i])
                 for i, (a, rows) in enumerate(pieces)]
        for cp in local:
            cp.start()

        def copy(i, half):
            a, rows = pieces[i]
            return pltpu.make_async_remote_copy(
                src_ref=ins[a].at[rows], dst_ref=outs[a].at[half, rows], send_sem=send_sems.at[i],
                recv_sem=recv_sems.at[i], device_id=(x, y, 1 - c), device_id_type=MESH)

        sends = [copy(i, c) for i in range(n_p)]
        for cp in sends:
            cp.start()
        for i in range(n_p):
            copy(i, 1 - c).wait_recv()
        for cp in sends:
            cp.wait_send()
        for cp in local:
            cp.wait()

    anyspec = pl.BlockSpec(memory_space=pl.ANY)
    sems = pltpu.SemaphoreType.DMA((n_p,))
    return pl.pallas_call(
        body,
        name=name,
        in_specs=[anyspec] * n,
        out_specs=[anyspec] * n,
        out_shape=[jax.ShapeDtypeStruct((2,) + h.shape, h.dtype) for h in halves],
        scratch_shapes=[sems, sems, sems],
    )(*halves)


def _row_tile(rows, cap):
    t = cap
    while rows % t:
        t //= 2
    return t


def _add_my_half(g, r, name):
    _, _, R, C = g.shape
    tr = _row_tile(R, 256)

    def body(c_ref, g_ref, r_ref, o_ref):
        o_ref[...] = (g_ref[0] + r_ref[...]).astype(o_ref.dtype)

    return pl.pallas_call(
        body,
        name=name,
        grid_spec=pltpu.PrefetchScalarGridSpec(
            num_scalar_prefetch=1,
            grid=(N_CHIPS, R // tr),
            in_specs=[pl.BlockSpec((1, 1, tr, C), lambda p, i, c_ref: (p, c_ref[0], i, 0)),
                      pl.BlockSpec((1, tr, C), lambda p, i, c_ref: (p, i, 0))],
            out_specs=pl.BlockSpec((1, tr, C), lambda p, i, c_ref: (p, i, 0)),
        ),
        out_shape=jax.ShapeDtypeStruct(r.shape, jnp.bfloat16),
        compiler_params=_params(("parallel", "parallel")),
    )(lax.axis_index("c").reshape(1).astype(jnp.int32), g, r)


def _sum_slabs(r, name):
    _, R, C = r.shape
    tr = _row_tile(R, 256)

    def body(r_ref, o_ref):
        o_ref[...] = ((r_ref[0].astype(F32) + r_ref[1].astype(F32)) + r_ref[2].astype(F32)) + r_ref[3].astype(F32)

    return pl.pallas_call(
        body,
        name=name,
        grid=(R // tr,),
        in_specs=[pl.BlockSpec((N_CHIPS, tr, C), lambda i: (0, i, 0))],
        out_specs=pl.BlockSpec((tr, C), lambda i: (i, 0)),
        out_shape=jax.ShapeDtypeStruct((R, C), F32),
        compiler_params=_params(("parallel",)),
    )(r)


def _sum_rows8(g, m_per, name):
    n = g.shape[1]

    def body(g_ref, o_ref):
        acc = g_ref[0:m_per, :]
        for k in range(1, 8):
            acc = acc + g_ref[k * m_per:(k + 1) * m_per, :]
        o_ref[...] = acc

    return pl.pallas_call(
        body,
        name=name,
        out_shape=jax.ShapeDtypeStruct((m_per, n), F32),
        compiler_params=_params(),
    )(g)


def _adamw_math(w, g, m, v):
    m = ADAM_B1 * m + (1.0 - ADAM_B1) * g
    v = ADAM_B2 * v + (1.0 - ADAM_B2) * (g * g)
    m_hat = m / (1.0 - ADAM_B1 ** ADAM_STEP)
    v_hat = v / (1.0 - ADAM_B2 ** ADAM_STEP)
    delta = -ADAM_LR * (m_hat / (jnp.sqrt(v_hat) + ADAM_EPS) + ADAM_WD * w)
    return delta, m, v


def _adamw_big(w, g, m, v, name):
    R, C = w.shape
    tr = min(128, R)

    def body(w_ref, g_ref, m_ref, v_ref, d_out, m_out, v_out):
        d, mn, vn = _adamw_math(w_ref[...], g_ref[...], m_ref[...], v_ref[...])
        d_out[...] = d
        m_out[...] = mn
        v_out[...] = vn

    spec = pl.BlockSpec((tr, C), lambda i: (i, 0))
    return pl.pallas_call(
        body,
        name=name,
        grid=(R // tr,),
        in_specs=[spec] * 4,
        out_specs=[spec] * 3,
        out_shape=[jax.ShapeDtypeStruct((R, C), F32)] * 3,
        compiler_params=_params(("parallel",)),
    )(w, g, m, v)


def _adamw_small(ws, gs, ms, vs, name):
    n = len(ws)

    def body(*refs):
        for a in range(n):
            d, mn, vn = _adamw_math(refs[a][...], refs[n + a][...], refs[2 * n + a][...], refs[3 * n + a][...])
            refs[4 * n + a][...] = d
            refs[5 * n + a][...] = mn
            refs[6 * n + a][...] = vn

    shapes = [jax.ShapeDtypeStruct(w.shape, F32) for w in ws]
    outs = pl.pallas_call(
        body,
        name=name,
        out_shape=shapes * 3,
        compiler_params=_params(),
    )(*ws, *gs, *ms, *vs)
    return outs[:n], outs[n:2 * n], outs[2 * n:]


def _to_blockdiag(w):
    per = CW // LRU_BW
    w4 = w.reshape(N_CT, per, LRU_BW, LRU_BW)
    eye = jnp.eye(per, dtype=w.dtype)
    return (w4[:, :, :, None, :] * eye[None, :, None, :, None]).reshape(N_CT, CW, CW)


def _from_blockdiag(g):
    per = CW // LRU_BW
    g5 = g.reshape(N_CT, per, LRU_BW, per, LRU_BW)
    return jnp.stack([g5[:, b, :, b, :] for b in range(per)], axis=1).reshape(LRU_BLOCKS, LRU_BW, LRU_BW)


def _local_grads(x2d, tgt2d, B, S, g_in, w_all, conv_w, conv_b, gate_x_w, gate_x_b, gate_a_w, gate_a_b, lam, gain,
                 wpa, wpb, wout, g_fin):
    wx_bd = _c(_to_blockdiag(gate_x_w))
    wa_bd = _c(_to_blockdiag(gate_a_w))
    tables = _retention_tables(S)
    gain3 = gain.reshape(HEADS, 1, DK)

    proj, hb = _inproj_fwd(x2d, g_in, w_all)
    hlru, ya = _lru_fwd(proj, conv_w, conv_b, wx_bd, wa_bd, gate_x_b, gate_a_b, lam, B, S)
    o_pre, yb, states = _ret_fwd(proj, tables, gain3, B, S)
    loss, dx2, dya, dyb, dma, dmb, dgf, gw_proj = _mid(ya, yb, proj, x2d, tgt2d, wpa, wpb, wout, g_fin)
    dxa, dga, dcw, dcb, dwx_bd, dwa_bd, dbx, dba, dlam = _lru_bwd(
        dya, proj, hlru, conv_w, conv_b, wx_bd, wa_bd, gate_x_b, gate_a_b, lam, B, S)
    dq, dk, dv, dgb, dgain = _ret_bwd(dyb, o_pre, proj, states, tables, gain3, B, S)
    dparts = [dxa, dga, dq, dk, dv, dgb, dma, dmb]
    grad_x, dgin = _inproj_bwd_dx(dparts, w_all, x2d, dx2, g_in)
    gw_in = _inproj_bwd_dw(hb, dparts)
    small = dict(norm_in=dgin, conv_w=dcw, conv_b=dcb, gate_x_w=_from_blockdiag(dwx_bd), gate_x_b=dbx,
                 gate_a_w=_from_blockdiag(dwa_bd), gate_a_b=dba, lru_lambda=dlam, gn_gain=dgain.reshape(HEADS, DK),
                 norm_final=dgf)
    return loss[0, 0], grad_x, gw_in, gw_proj, small


_SMALL = ("gate_x_w", "gate_a_w", "norm_in", "conv_w", "conv_b", "gate_x_b", "gate_a_b", "lru_lambda", "gn_gain",
          "norm_final")
_SMALL_SHAPES = dict(gate_x_w=(LRU_BLOCKS, LRU_BW, LRU_BW), gate_a_w=(LRU_BLOCKS, LRU_BW, LRU_BW),
                     norm_in=(1, D_MODEL), conv_w=(CONV, D_MODEL), conv_b=(1, D_MODEL), gate_x_b=(1, D_MODEL),
                     gate_a_b=(1, D_MODEL), lru_lambda=(1, D_MODEL), gn_gain=(HEADS, DK), norm_final=(1, D_MODEL))


def _pack_small(small):
    return jnp.concatenate([small[k].reshape(-1, 128) for k in _SMALL], axis=0)


def _unpack_small(packed):
    out, r = {}, 0
    for k in _SMALL:
        shape = _SMALL_SHAPES[k]
        rows = 1
        for s in shape:
            rows *= s
        rows //= 128
        out[k] = packed[r:r + rows].reshape(shape)
        r += rows
    return out


def kernel(x, norm_in, w_in, conv_w, conv_b, gate_x_w, gate_x_b, gate_a_w, gate_a_b, lru_lambda, gn_gain, w_proj_a, w_proj_b, w_out, norm_final, loss_target, m_norm_in, m_w_in, m_conv_w, m_conv_b, m_gate_x_w, m_gate_x_b, m_gate_a_w, m_gate_a_b, m_lru_lambda, m_gn_gain, m_w_proj_a, m_w_proj_b, m_w_out, m_norm_final, v_norm_in, v_w_in, v_conv_w, v_conv_b, v_gate_x_w, v_gate_x_b, v_gate_a_w, v_gate_a_b, v_lru_lambda, v_gn_gain, v_w_proj_a, v_w_proj_b, v_w_out, v_norm_final):
    B, S, _ = x.shape
    T = B * S
    xi, yi, ci = _coords()
    chip = 2 * xi + yi

    cshard = D_MODEL // N_CHIPS
    big = _gather_chips([_c(w_in[0]).reshape(2, D_MODEL // 2, 2 * D_MODEL)]
                        + [_c(w[0]).reshape(2, cshard // 2, D_MODEL) for w in (w_proj_a, w_proj_b, w_out)],
                        [4, 1, 1, 1], "gather_weights")
    w_all = big[0].reshape(N_CHIPS, D_MODEL, 2 * D_MODEL)
    wpa, wpb, wout = (b.reshape(D_MODEL, D_MODEL) for b in big[1:])
    gshard = DK // N_CHIPS
    tiny = jnp.concatenate([conv_w[0], jnp.zeros((4, cshard), F32), jnp.pad(gn_gain[0], ((0, 4), (0, cshard - gshard)))],
                           axis=0)
    tiny_all = _all_gather8(tiny, "gather_small_weights").reshape(N_CHIPS, 2, 16, cshard)[:, 0]
    conv_w_full = jnp.transpose(tiny_all[:, 0:CONV, :], (1, 0, 2)).reshape(CONV, D_MODEL)
    gain_full = jnp.transpose(tiny_all[:, 8:8 + HEADS, :gshard], (1, 0, 2)).reshape(HEADS, DK)

    loss, grad_x, gw_in, gw_proj, small = _local_grads(
        x.reshape(T, D_MODEL), loss_target.reshape(T, D_MODEL), B, S, norm_in, w_all, conv_w_full, conv_b,
        gate_x_w[0], gate_x_b, gate_a_w[0], gate_a_b, lru_lambda, gain_full, wpa, wpb, wout,
        norm_final.reshape(1, D_MODEL))
    loss = lax.psum(loss, ("x", "y", "c"))

    packed = _pack_small(small)
    m_per = packed.shape[0]
    gsm = _unpack_small(_sum_rows8(_all_gather8(packed, "gather_small_grads"), m_per, "sum_small_grads"))

    rows_p = 3 * D_MODEL // (2 * N_CHIPS)
    gw_proj = gw_proj.reshape(N_CHIPS, 2, rows_p, D_MODEL)
    other = _swap_halves([gw_in, gw_proj], "swap_halves")
    chip_in = _add_my_half(gw_in, other[0], "chip_sum_w_in")
    chip_pr = _add_my_half(gw_proj, other[1], "chip_sum_w_proj")
    got = _scatter_chips([chip_in, chip_pr], [4, 2], "scatter_chips")
    half_in = _sum_slabs(got[0], "sum_w_in")
    half_pr = _sum_slabs(got[1], "sum_w_proj")
    g_in_full, g_pr_full = _join_halves([half_in, half_pr], [8, 4], "join_halves")
    g_w_in = g_in_full.reshape(D_MODEL, 2 * D_MODEL)
    g_pr = g_pr_full.reshape(2, 3, D_MODEL // (2 * N_CHIPS), D_MODEL)
    g_wpa, g_wpb, g_wout = (g_pr[:, k].reshape(cshard, D_MODEL) for k in range(3))

    grads = dict(gsm)
    grads["conv_w"] = lax.dynamic_slice_in_dim(gsm["conv_w"], chip * cshard, cshard, axis=1)
    grads["gn_gain"] = lax.dynamic_slice_in_dim(gsm["gn_gain"], chip * gshard, gshard, axis=1)
    grads.update(w_in=g_w_in, w_proj_a=g_wpa, w_proj_b=g_wpb, w_out=g_wout)

    weights = dict(norm_in=norm_in, w_in=w_in, conv_w=conv_w, conv_b=conv_b, gate_x_w=gate_x_w, gate_x_b=gate_x_b,
                   gate_a_w=gate_a_w, gate_a_b=gate_a_b, lru_lambda=lru_lambda, gn_gain=gn_gain, w_proj_a=w_proj_a,
                   w_proj_b=w_proj_b, w_out=w_out, norm_final=norm_final)
    ms = dict(norm_in=m_norm_in, w_in=m_w_in, conv_w=m_conv_w, conv_b=m_conv_b, gate_x_w=m_gate_x_w,
              gate_x_b=m_gate_x_b, gate_a_w=m_gate_a_w, gate_a_b=m_gate_a_b, lru_lambda=m_lru_lambda, gn_gain=m_gn_gain,
              w_proj_a=m_w_proj_a, w_proj_b=m_w_proj_b, w_out=m_w_out, norm_final=m_norm_final)
    vs = dict(norm_in=v_norm_in, w_in=v_w_in, conv_w=v_conv_w, conv_b=v_conv_b, gate_x_w=v_gate_x_w,
              gate_x_b=v_gate_x_b, gate_a_w=v_gate_a_w, gate_a_b=v_gate_a_b, lru_lambda=v_lru_lambda, gn_gain=v_gn_gain,
              w_proj_a=v_w_proj_a, w_proj_b=v_w_proj_b, w_out=v_w_out, norm_final=v_norm_final)
    names = list(weights)
    grads = {k: grads[k].reshape(weights[k].shape) for k in names}

    delta, new_m, new_v = {}, {}, {}
    for k in ("w_in", "w_proj_a", "w_proj_b", "w_out"):
        shp = weights[k].shape
        two = lambda a: a.reshape(shp[1], shp[2])
        d, mn, vn = _adamw_big(two(weights[k]), two(grads[k]), two(ms[k]), two(vs[k]), "adamw_" + k)
        delta[k], new_m[k], new_v[k] = d.reshape(shp), mn.reshape(shp), vn.reshape(shp)
    smalls = [k for k in names if k not in delta]

    def view(a):
        return a.reshape(1, -1) if a.ndim == 1 else (a.reshape(a.shape[1:]) if a.ndim > 2 else a)

    ds, mns, vns = _adamw_small([view(weights[k]) for k in smalls], [view(grads[k]) for k in smalls],
                                [view(ms[k]) for k in smalls], [view(vs[k]) for k in smalls], "adamw_small")
    for k, d, mn, vn in zip(smalls, ds, mns, vns):
        shp = weights[k].shape
        delta[k], new_m[k], new_v[k] = d.reshape(shp), mn.reshape(shp), vn.reshape(shp)

    return (loss, grad_x.reshape(B, S, D_MODEL), *[grads[k] for k in names], *[delta[k] for k in names],
            *[new_m[k] for k in names], *[new_v[k] for k in names])
```

```python
import functools

import jax
import jax.numpy as jnp
from jax import lax
from jax.experimental import pallas as pl
from jax.experimental.pallas import tpu as pltpu

F32 = jnp.float32
_MXU = jnp.bfloat16

D_MODEL = 1024
N_GROUPS = 8
HEADS = 4
DK = 256
CHUNK = 128
CONV = 4
LRU_BLOCKS = 16
LRU_BW = 64
LRU_C = 8.0
ROPE_THETA = 10000.0
EPS = 1e-6
CW = 256
N_CT = D_MODEL // CW
N_CHIPS = 4
MESH = pl.DeviceIdType.MESH

ADAM_LR = 0.001
ADAM_B1 = 0.9
ADAM_B2 = 0.999
ADAM_EPS = 1e-08
ADAM_WD = 0.01
ADAM_STEP = 10

VMEM_LIMIT = 56 * 1024 * 1024


def _c(v):
    return v.astype(_MXU)


def _dot(a, b):
    return lax.dot_general(a, b, (((1,), (0,)), ((), ())), preferred_element_type=F32)


def _dot_nt(a, b):
    return lax.dot_general(a, b, (((1,), (1,)), ((), ())), preferred_element_type=F32)


def _dot_tn(a, b):
    return lax.dot_general(a, b, (((0,), (0,)), ((), ())), preferred_element_type=F32)


def _sigmoid(z):
    return 1.0 / (1.0 + jnp.exp(-z))


def _params(sem=None):
    if sem is None:
        return pltpu.CompilerParams(vmem_limit_bytes=VMEM_LIMIT)
    return pltpu.CompilerParams(vmem_limit_bytes=VMEM_LIMIT, dimension_semantics=sem)


def _inproj_fwd(x2d, g_in, w_all):
    T = x2d.shape[0]
    tm = min(512, T)

    def body(x_ref, g_ref, w_ref, proj_ref, hb_ref):
        @pl.when(pl.program_id(1) == 0)
        def _():
            x = x_ref[...]
            r = lax.rsqrt(jnp.mean(x * x, axis=-1, keepdims=True) + EPS)
            hb_ref[...] = (x * r * g_ref[...]).astype(hb_ref.dtype)

        proj_ref[...] = _dot(hb_ref[...], w_ref[0])

    return pl.pallas_call(
        body,
        name="inproj_fwd",
        grid=(T // tm, N_GROUPS),
        in_specs=[
            pl.BlockSpec((tm, D_MODEL), lambda i, j: (i, 0)),
            pl.BlockSpec((1, D_MODEL), lambda i, j: (0, 0)),
            pl.BlockSpec((1, D_MODEL, D_MODEL), lambda i, j: (j // 2, 0, j % 2)),
        ],
        out_specs=[
            pl.BlockSpec((tm, D_MODEL), lambda i, j: (i, j)),
            pl.BlockSpec((tm, D_MODEL), lambda i, j: (i, 0)),
        ],
        out_shape=[
            jax.ShapeDtypeStruct((T, N_GROUPS * D_MODEL), F32),
            jax.ShapeDtypeStruct((T, D_MODEL), _MXU),
        ],
        compiler_params=_params(("parallel", "arbitrary")),
    )(x2d, g_in, w_all)


def _scan_fwd(a, u):
    n = a.shape[0]
    row = lax.broadcasted_iota(jnp.int32, a.shape, 0)
    s = 1
    while s < n:
        m = row >= s
        u = u + a * jnp.where(m, pltpu.roll(u, s, 0), 0.0)
        a = a * jnp.where(m, pltpu.roll(a, s, 0), 1.0)
        s *= 2
    return a, u


def _scan_bwd(b, g):
    n = b.shape[0]
    row = lax.broadcasted_iota(jnp.int32, b.shape, 0)
    s = 1
    while s < n:
        m = row < n - s
        g = g + b * jnp.where(m, pltpu.roll(g, n - s, 0), 0.0)
        b = b * jnp.where(m, pltpu.roll(b, n - s, 0), 1.0)
        s *= 2
    return b, g


def _softplus_neg(lam):
    z = -lam
    return jnp.maximum(z, 0.0) + jnp.log1p(jnp.exp(-jnp.abs(z)))


def _lru_gates(xc, wx_ref, wa_ref, bx_ref, ba_ref, lam_ref):
    xcb = _c(xc)
    i_t = _sigmoid(_dot(xcb, wx_ref[0]) + bx_ref[...])
    r_t = _sigmoid(_dot(xcb, wa_ref[0]) + ba_ref[...])
    sp = _softplus_neg(lam_ref[...])
    log_a = (-LRU_C) * r_t * sp
    a = jnp.exp(log_a)
    mult = jnp.sqrt(1.0 - a * a)
    return xcb, i_t, r_t, sp, a, mult


def _conv_from_ext(ext_ref, xa, cw_ref, cb_ref, tc):
    return (cb_ref[...] + cw_ref[3:4, :] * xa + cw_ref[2:3, :] * ext_ref[7:7 + tc, :]
            + cw_ref[1:2, :] * ext_ref[6:6 + tc, :] + cw_ref[0:1, :] * ext_ref[5:5 + tc, :])


def _lru_fwd(proj, conv_w, conv_b, wx_bd, wa_bd, bx, ba, lam, B, S):
    T = B * S
    tc = min(256, S)
    nt = S // tc
    h8 = tc // 8

    def body(xa_ref, halo_ref, ga_ref, cw_ref, cb_ref, wx_ref, wa_ref, bx_ref, ba_ref, lam_ref,
             h_ref, ya_ref, ext_ref, carry_ref):
        t = pl.program_id(2)

        @pl.when(t == 0)
        def _():
            carry_ref[...] = jnp.zeros_like(carry_ref)

        xa = xa_ref[...]
        ext_ref[0:8, :] = jnp.where(t == 0, 0.0, halo_ref[...])
        ext_ref[8:8 + tc, :] = xa
        xc = _conv_from_ext(ext_ref, xa, cw_ref, cb_ref, tc)
        _, i_t, _, _, a, mult = _lru_gates(xc, wx_ref, wa_ref, bx_ref, ba_ref, lam_ref)
        u = mult * (i_t * xc)
        acum, hloc = _scan_fwd(a, u)
        h = hloc + acum * carry_ref[7:8, :]
        h_ref[...] = h
        carry_ref[...] = h[tc - 8:tc, :]
        ga = ga_ref[...]
        ya_ref[...] = (ga * _sigmoid(ga) * h).astype(ya_ref.dtype)

    row = lambda b, t: b * nt + t
    vec = pl.BlockSpec((1, CW), lambda b, c, t: (0, c))
    mat = pl.BlockSpec((1, CW, CW), lambda b, c, t: (c, 0, 0))
    return pl.pallas_call(
        body,
        name="lru_fwd",
        grid=(B, N_CT, nt),
        in_specs=[
            pl.BlockSpec((tc, CW), lambda b, c, t: (row(b, t), c)),
            pl.BlockSpec((8, CW), lambda b, c, t: (jnp.maximum(row(b, t) * h8 - 1, 0), c)),
            pl.BlockSpec((tc, CW), lambda b, c, t: (row(b, t), N_CT + c)),
            pl.BlockSpec((CONV, CW), lambda b, c, t: (0, c)),
            vec, mat, mat, vec, vec, vec,
        ],
        out_specs=[
            pl.BlockSpec((tc, CW), lambda b, c, t: (row(b, t), c)),
            pl.BlockSpec((tc, CW), lambda b, c, t: (row(b, t), c)),
        ],
        out_shape=[
            jax.ShapeDtypeStruct((T, D_MODEL), F32),
            jax.ShapeDtypeStruct((T, D_MODEL), _MXU),
        ],
        scratch_shapes=[pltpu.VMEM((tc + 8, CW), F32), pltpu.VMEM((8, CW), F32)],
        compiler_params=_params(("parallel", "parallel", "arbitrary")),
    )(proj, proj, proj, conv_w, conv_b, wx_bd, wa_bd, bx, ba, lam)


def _lru_bwd(dya, proj, hlru, conv_w, conv_b, wx_bd, wa_bd, bx, ba, lam, B, S):
    T = B * S
    tc = min(256, S)
    nt = S // tc
    h8 = tc // 8

    def body(dya_ref, xa_ref, xhalo_ref, ga_ref, h_ref, hhalo_ref, cw_ref, cb_ref, wx_ref, wa_ref, bx_ref, ba_ref,
             lam_ref, dxa_ref, dga_ref, dcw_ref, dcb_ref, dwx_ref, dwa_ref, dbx_ref, dba_ref, dlam_ref,
             ext_ref, ext2_ref, carry_ref, dhalo_ref):
        b = pl.program_id(1)
        t = pl.program_id(2)
        tt = nt - 1 - t

        @pl.when(t == 0)
        def _():
            carry_ref[...] = jnp.zeros_like(carry_ref)
            dhalo_ref[...] = jnp.zeros_like(dhalo_ref)

        @pl.when((t == 0) & (b == 0))
        def _():
            for r in (dcw_ref, dcb_ref, dwx_ref, dwa_ref, dbx_ref, dba_ref, dlam_ref):
                r[...] = jnp.zeros_like(r)

        xa = xa_ref[...]
        ext_ref[0:8, :] = jnp.where(tt == 0, 0.0, xhalo_ref[...])
        ext_ref[8:8 + tc, :] = xa
        xc = _conv_from_ext(ext_ref, xa, cw_ref, cb_ref, tc)
        xcb, i_t, r_t, sp, a, mult = _lru_gates(xc, wx_ref, wa_ref, bx_ref, ba_ref, lam_ref)

        h = h_ref[...]
        ga = ga_ref[...]
        dya_t = dya_ref[...]
        sg = _sigmoid(ga)
        dga_ref[...] = (dya_t * h * (sg * (1.0 + ga * (1.0 - sg)))).astype(dga_ref.dtype)
        dlru = dya_t * (ga * sg)

        row = lax.broadcasted_iota(jnp.int32, a.shape, 0)
        coef = jnp.where(row == tc - 1, 1.0, pltpu.roll(a, tc - 1, 0))
        bcum, dloc = _scan_bwd(coef, dlru)
        dh = dloc + bcum * carry_ref[0:1, :]
        ext2_ref[0:tc, :] = a * dh
        carry_ref[...] = ext2_ref[0:8, :]

        ext2_ref[0:8, :] = jnp.where(tt == 0, 0.0, hhalo_ref[...])
        ext2_ref[8:8 + tc, :] = h
        hprev = ext2_ref[7:7 + tc, :]

        da = dh * hprev
        ix = i_t * xc
        dmult = dh * ix
        di = dh * mult * xc
        dxc = dh * mult * i_t
        dlog_a = da * a - dmult * (a * a) / mult
        dr = dlog_a * ((-LRU_C) * sp)
        dlam_ref[...] += jnp.sum(dlog_a * r_t, axis=0, keepdims=True) * (LRU_C * _sigmoid(-lam_ref[...]))
        dza = dr * r_t * (1.0 - r_t)
        dzx = di * i_t * (1.0 - i_t)
        dzab = _c(dza)
        dzxb = _c(dzx)
        dxc = dxc + _dot_nt(dzxb, wx_ref[0]) + _dot_nt(dzab, wa_ref[0])
        dwx_ref[0] += _dot_tn(xcb, dzxb)
        dwa_ref[0] += _dot_tn(xcb, dzab)
        dbx_ref[...] += jnp.sum(dzx, axis=0, keepdims=True)
        dba_ref[...] += jnp.sum(dza, axis=0, keepdims=True)

        dcb_ref[...] += jnp.sum(dxc, axis=0, keepdims=True)
        dcw_ref[3:4, :] += jnp.sum(dxc * xa, axis=0, keepdims=True)
        dcw_ref[2:3, :] += jnp.sum(dxc * ext_ref[7:7 + tc, :], axis=0, keepdims=True)
        dcw_ref[1:2, :] += jnp.sum(dxc * ext_ref[6:6 + tc, :], axis=0, keepdims=True)
        dcw_ref[0:1, :] += jnp.sum(dxc * ext_ref[5:5 + tc, :], axis=0, keepdims=True)
        ext2_ref[0:tc, :] = dxc
        ext2_ref[tc:tc + 8, :] = dhalo_ref[...]
        dxa = (cw_ref[3:4, :] * dxc + cw_ref[2:3, :] * ext2_ref[1:1 + tc, :]
               + cw_ref[1:2, :] * ext2_ref[2:2 + tc, :] + cw_ref[0:1, :] * ext2_ref[3:3 + tc, :])
        dxa_ref[...] = dxa.astype(dxa_ref.dtype)
        dhalo_ref[...] = ext2_ref[0:8, :]

    row_of = lambda b, t: b * nt + (nt - 1 - t)
    tile = lambda off: pl.BlockSpec((tc, CW), lambda c, b, t: (row_of(b, t), off + c))
    halo = pl.BlockSpec((8, CW), lambda c, b, t: (jnp.maximum(row_of(b, t) * h8 - 1, 0), c))
    vec = pl.BlockSpec((1, CW), lambda c, b, t: (0, c))
    mat = pl.BlockSpec((1, CW, CW), lambda c, b, t: (c, 0, 0))
    cwspec = pl.BlockSpec((CONV, CW), lambda c, b, t: (0, c))
    return pl.pallas_call(
        body,
        name="lru_bwd",
        grid=(N_CT, B, nt),
        in_specs=[tile(0), tile(0), halo, tile(N_CT), tile(0), halo, cwspec, vec, mat, mat, vec, vec, vec],
        out_specs=[tile(0), tile(0), cwspec, vec, mat, mat, vec, vec, vec],
        out_shape=[
            jax.ShapeDtypeStruct((T, D_MODEL), _MXU),
            jax.ShapeDtypeStruct((T, D_MODEL), _MXU),
            jax.ShapeDtypeStruct((CONV, D_MODEL), F32),
            jax.ShapeDtypeStruct((1, D_MODEL), F32),
            jax.ShapeDtypeStruct((N_CT, CW, CW), F32),
            jax.ShapeDtypeStruct((N_CT, CW, CW), F32),
            jax.ShapeDtypeStruct((1, D_MODEL), F32),
            jax.ShapeDtypeStruct((1, D_MODEL), F32),
            jax.ShapeDtypeStruct((1, D_MODEL), F32),
        ],
        scratch_shapes=[pltpu.VMEM((tc + 8, CW), F32), pltpu.VMEM((tc + 8, CW), F32),
                        pltpu.VMEM((8, CW), F32), pltpu.VMEM((8, CW), F32)],
        compiler_params=_params(("parallel", "arbitrary", "arbitrary")),
    )(dya, proj, proj, proj, hlru, hlru, conv_w, conv_b, wx_bd, wa_bd, bx, ba, lam)


def _retention_tables(S):
    half = DK // 2
    freqs = ROPE_THETA ** (-jnp.arange(half, dtype=F32) / half)
    ang = jnp.arange(S, dtype=F32)[:, None] * freqs[None, :]
    log_g = jnp.log1p(-(2.0 ** (-5.0 - jnp.arange(HEADS, dtype=F32))))
    idx = jnp.arange(CHUNK, dtype=F32)
    diff = idx[:, None] - idx[None, :]
    inner = jnp.where(diff >= 0, jnp.exp(jnp.maximum(diff, 0.0)[None] * log_g[:, None, None]), 0.0)
    cross = jnp.exp((idx[None, :] + 1.0) * log_g[:, None])[:, :, None]
    state = jnp.exp((CHUNK - 1.0 - idx[None, :]) * log_g[:, None])[:, :, None]
    gam = jnp.broadcast_to(jnp.exp(CHUNK * log_g)[:, None, None], (HEADS, 1, DK))
    return jnp.cos(ang), jnp.sin(ang), inner, cross, state, gam


def _rot(x, cos, sin):
    half = DK // 2
    x1, x2 = x[:, :half], x[:, half:]
    return jnp.concatenate([x1 * cos - x2 * sin, x1 * sin + x2 * cos], axis=-1)


def _rot_t(y, cos, sin):
    half = DK // 2
    y1, y2 = y[:, :half], y[:, half:]
    return jnp.concatenate([y1 * cos + y2 * sin, y2 * cos - y1 * sin], axis=-1)


def _groupnorm(o):
    mu = jnp.mean(o, axis=-1, keepdims=True)
    oc = o - mu
    rs = lax.rsqrt(jnp.mean(oc * oc, axis=-1, keepdims=True) + EPS)
    return oc * rs, rs


def _ret_specs(S, chunk_of):
    nc = S // CHUNK
    qkv = lambda g: pl.BlockSpec((CHUNK, D_MODEL), lambda b, c: (b * nc + chunk_of(c), g))
    act = pl.BlockSpec((CHUNK, D_MODEL), lambda b, c: (b * nc + chunk_of(c), 0))
    rope = pl.BlockSpec((CHUNK, DK // 2), lambda b, c: (chunk_of(c), 0))
    dmat = pl.BlockSpec((HEADS, CHUNK, CHUNK), lambda b, c: (0, 0, 0))
    dvec = pl.BlockSpec((HEADS, CHUNK, 1), lambda b, c: (0, 0, 0))
    hrow = pl.BlockSpec((HEADS, 1, DK), lambda b, c: (0, 0, 0))
    rst = pl.BlockSpec((1, HEADS, DK, DK), lambda b, c: (b * nc + chunk_of(c), 0, 0, 0))
    return qkv, act, rope, dmat, dvec, hrow, rst


def _ret_fwd(proj, tables, gain3, B, S):
    T = B * S
    nc = S // CHUNK
    cos, sin, dmat_t, cd_t, sd_t, gam_t = tables

    def body(q_ref, k_ref, v_ref, gb_ref, cos_ref, sin_ref, dm_ref, cd_ref, sd_ref, gam_ref, gain_ref,
             o_ref, yb_ref, rs_ref, state_ref):
        @pl.when(pl.program_id(1) == 0)
        def _():
            state_ref[...] = jnp.zeros_like(state_ref)

        cos_t, sin_t = cos_ref[...], sin_ref[...]
        for h in range(HEADS):
            cols = slice(h * DK, (h + 1) * DK)
            qb = _c(_rot(q_ref[:, cols], cos_t, sin_t))
            kb = _c(_rot(k_ref[:, cols], cos_t, sin_t) * (DK ** -0.5))
            v = v_ref[:, cols]
            state = state_ref[h]
            sb = _c(state)
            rs_ref[0, h] = sb
            scores = _dot_nt(qb, kb) * dm_ref[h]
            o = _dot(_c(scores), _c(v)) + _dot(qb, sb) * cd_ref[h]
            state_ref[h] = gam_ref[h] * state + _dot_tn(kb, _c(v * sd_ref[h]))
            o_ref[:, cols] = o
            n, _ = _groupnorm(o)
            gb = gb_ref[:, cols]
            yb_ref[:, cols] = (gb * _sigmoid(gb) * (n * gain_ref[h])).astype(yb_ref.dtype)

    qkv, act, rope, dmat, dvec, hrow, rst = _ret_specs(S, lambda c: c)
    return pl.pallas_call(
        body,
        name="ret_fwd",
        grid=(B, nc),
        in_specs=[qkv(2), qkv(3), qkv(4), qkv(5), rope, rope, dmat, dvec, dvec, hrow, hrow],
        out_specs=[act, act, rst],
        out_shape=[
            jax.ShapeDtypeStruct((T, D_MODEL), F32),
            jax.ShapeDtypeStruct((T, D_MODEL), _MXU),
            jax.ShapeDtypeStruct((B * nc, HEADS, DK, DK), _MXU),
        ],
        scratch_shapes=[pltpu.VMEM((HEADS, DK, DK), F32)],
        compiler_params=_params(("parallel", "arbitrary")),
    )(proj, proj, proj, proj, cos, sin, dmat_t, cd_t, sd_t, gam_t, gain3)


def _ret_bwd(dyb, o_pre, proj, states, tables, gain3, B, S):
    T = B * S
    nc = S // CHUNK
    cos, sin, dmat_t, cd_t, sd_t, gam_t = tables

    def body(dyb_ref, o_ref, q_ref, k_ref, v_ref, gb_ref, rs_ref, cos_ref, sin_ref, dm_ref, cd_ref, sd_ref, gam_ref,
             gain_ref, dq_ref, dk_ref, dv_ref, dgb_ref, dgain_ref, dstate_ref):
        @pl.when(pl.program_id(1) == 0)
        def _():
            dstate_ref[...] = jnp.zeros_like(dstate_ref)

        @pl.when((pl.program_id(1) == 0) & (pl.program_id(0) == 0))
        def _():
            dgain_ref[...] = jnp.zeros_like(dgain_ref)

        cos_t, sin_t = cos_ref[...], sin_ref[...]
        for h in range(HEADS):
            cols = slice(h * DK, (h + 1) * DK)
            gain = gain_ref[h]
            n, rs = _groupnorm(o_ref[:, cols])
            gb = gb_ref[:, cols]
            sg = _sigmoid(gb)
            dy = dyb_ref[:, cols]
            dgb_ref[:, cols] = (dy * (n * gain) * (sg * (1.0 + gb * (1.0 - sg)))).astype(dgb_ref.dtype)
            dgn = dy * (gb * sg)
            dgain_ref[h] += jnp.sum(dgn * n, axis=0, keepdims=True)
            dn = dgn * gain
            do = rs * (dn - jnp.mean(dn, axis=-1, keepdims=True) - n * jnp.mean(dn * n, axis=-1, keepdims=True))

            qb = _c(_rot(q_ref[:, cols], cos_t, sin_t))
            kb = _c(_rot(k_ref[:, cols], cos_t, sin_t) * (DK ** -0.5))
            v = v_ref[:, cols]
            vb = _c(v)
            vsb = _c(v * sd_ref[h])
            dob = _c(do)
            docb = _c(do * cd_ref[h])
            dmat = dm_ref[h]
            dstate = dstate_ref[h]
            dsb = _c(dstate)
            pb = _c(_dot_nt(qb, kb) * dmat)
            dsc = _c(_dot_nt(dob, vb) * dmat)
            dq = _dot(dsc, kb) + _dot_nt(docb, rs_ref[0, h])
            dk = _dot_tn(dsc, qb) + _dot_nt(vsb, dsb)
            dv = _dot_tn(pb, dob) + _dot(kb, dsb) * sd_ref[h]
            dstate_ref[h] = gam_ref[h] * dstate + _dot_tn(qb, docb)
            dq_ref[:, cols] = _rot_t(dq, cos_t, sin_t).astype(dq_ref.dtype)
            dk_ref[:, cols] = (_rot_t(dk, cos_t, sin_t) * (DK ** -0.5)).astype(dk_ref.dtype)
            dv_ref[:, cols] = dv.astype(dv_ref.dtype)

    qkv, act, rope, dmat, dvec, hrow, rst = _ret_specs(S, lambda c: nc - 1 - c)
    big = jax.ShapeDtypeStruct((T, D_MODEL), _MXU)
    return pl.pallas_call(
        body,
        name="ret_bwd",
        grid=(B, nc),
        in_specs=[act, act, qkv(2), qkv(3), qkv(4), qkv(5), rst, rope, rope, dmat, dvec, dvec, hrow, hrow],
        out_specs=[act, act, act, act, hrow],
        out_shape=[big, big, big, big, jax.ShapeDtypeStruct((HEADS, 1, DK), F32)],
        scratch_shapes=[pltpu.VMEM((HEADS, DK, DK), F32)],
        compiler_params=_params(("arbitrary", "arbitrary")),
    )(dyb, o_pre, proj, proj, proj, proj, states, cos, sin, dmat_t, cd_t, sd_t, gam_t, gain3)


def _mid(ya, yb, proj, x2d, tgt2d, wpa, wpb, wout, g_fin):
    T = x2d.shape[0]
    tm = min(256, T)
    n_steps = T // tm
    rows = D_MODEL // (2 * N_CHIPS)

    def body(ya_ref, yb_ref, ma_ref, mb_ref, x_ref, t_ref, gf_ref, wpa_hbm, wpb_hbm, wout_hbm,
             loss_ref, dx2_ref, dya_ref, dyb_ref, dma_ref, dmb_ref, dgf_ref, gw_hbm, w_ref, acc_ref, sem):
        i = pl.program_id(0)

        @pl.when(i == 0)
        def _():
            loads = [pltpu.make_async_copy(src, w_ref.at[k], sem.at[k]) for k, src in enumerate((wpa_hbm, wpb_hbm, wout_hbm))]
            for cp in loads:
                cp.start()
            for cp in loads:
                cp.wait()
            acc_ref[...] = jnp.zeros_like(acc_ref)
            loss_ref[...] = jnp.zeros_like(loss_ref)
            dgf_ref[...] = jnp.zeros_like(dgf_ref)

        ya_t, yb_t = ya_ref[...], yb_ref[...]
        out_a = _dot(ya_t, w_ref[0])
        out_b = _dot(yb_t, w_ref[1])
        sa = _sigmoid(ma_ref[...])
        sb = _sigmoid(mb_ref[...])
        mgb = _c(sa * out_a + sb * out_b)
        x2 = x_ref[...] + _dot(mgb, w_ref[2])
        r2 = lax.rsqrt(jnp.mean(x2 * x2, axis=-1, keepdims=True) + EPS)
        nx = x2 * r2
        gf = gf_ref[...]
        err = nx * gf - t_ref[...]
        loss_ref[...] += 0.5 * jnp.sum(jnp.mean(err * err, axis=-1, keepdims=True), axis=0, keepdims=True)
        dy = err * (1.0 / D_MODEL)
        dgf_ref[...] += jnp.sum(dy * nx, axis=0, keepdims=True)
        dyg = dy * gf
        dx2 = r2 * (dyg - nx * jnp.mean(dyg * nx, axis=-1, keepdims=True))
        dx2_ref[...] = dx2
        dx2b = _c(dx2)
        dmg = _dot_nt(dx2b, w_ref[2])
        acc_ref[2] += _dot_tn(mgb, dx2b)
        dma_ref[...] = (dmg * out_a * sa * (1.0 - sa)).astype(dma_ref.dtype)
        dmb_ref[...] = (dmg * out_b * sb * (1.0 - sb)).astype(dmb_ref.dtype)
        dab = _c(dmg * sa)
        dbb = _c(dmg * sb)
        dya_ref[...] = _dot_nt(dab, w_ref[0])
        dyb_ref[...] = _dot_nt(dbb, w_ref[1])
        acc_ref[0] += _dot_tn(ya_t, dab)
        acc_ref[1] += _dot_tn(yb_t, dbb)

        @pl.when(i == n_steps - 1)
        def _():
            copies = [pltpu.make_async_copy(acc_ref.at[k, pl.ds((2 * p + hf) * rows, rows), :], gw_hbm.at[p, hf, k],
                                            sem.at[(k * N_CHIPS + p) * 2 + hf])
                      for k in range(3) for p in range(N_CHIPS) for hf in range(2)]
            for cp in copies:
                cp.start()
            for cp in copies:
                cp.wait()

    tile = lambda j: pl.BlockSpec((tm, D_MODEL), lambda i: (i, j))
    one = pl.BlockSpec((1, D_MODEL), lambda i: (0, 0))
    anyspec = pl.BlockSpec(memory_space=pl.ANY)
    return pl.pallas_call(
        body,
        name="mid",
        grid=(n_steps,),
        in_specs=[tile(0), tile(0), tile(6), tile(7), tile(0), tile(0), one, anyspec, anyspec, anyspec],
        out_specs=[pl.BlockSpec((1, 1), lambda i: (0, 0)), tile(0), tile(0), tile(0), tile(0), tile(0), one, anyspec],
        out_shape=[
            jax.ShapeDtypeStruct((1, 1), F32),
            jax.ShapeDtypeStruct((T, D_MODEL), F32),
            jax.ShapeDtypeStruct((T, D_MODEL), F32),
            jax.ShapeDtypeStruct((T, D_MODEL), F32),
            jax.ShapeDtypeStruct((T, D_MODEL), _MXU),
            jax.ShapeDtypeStruct((T, D_MODEL), _MXU),
            jax.ShapeDtypeStruct((1, D_MODEL), F32),
            jax.ShapeDtypeStruct((N_CHIPS, 2, 3, rows, D_MODEL), F32),
        ],
        scratch_shapes=[pltpu.VMEM((3, D_MODEL, D_MODEL), _MXU), pltpu.VMEM((3, D_MODEL, D_MODEL), F32),
                        pltpu.SemaphoreType.DMA((3 * N_CHIPS * 2,))],
        compiler_params=_params(("arbitrary",)),
    )(ya, yb, proj, proj, x2d, tgt2d, g_fin, wpa, wpb, wout)


def _inproj_bwd_dx(dparts, w_all, x2d, dx2, g_in):
    T = x2d.shape[0]
    tm = min(256, T)

    def body(*refs):
        d_refs = refs[:N_GROUPS]
        x_ref, dx2_ref, g_ref, w_hbm, dx_ref, dg_ref, w_ref, sem = refs[N_GROUPS:]

        @pl.when(pl.program_id(0) == 0)
        def _():
            cp = pltpu.make_async_copy(w_hbm, w_ref, sem)
            cp.start()
            cp.wait()
            dg_ref[...] = jnp.zeros_like(dg_ref)

        dh = jnp.zeros((tm, D_MODEL), F32)
        for j in range(N_GROUPS):
            dh = dh + _dot_nt(d_refs[j][...], w_ref[j // 2, :, (j % 2) * D_MODEL:(j % 2 + 1) * D_MODEL])
        x = x_ref[...]
        r = lax.rsqrt(jnp.mean(x * x, axis=-1, keepdims=True) + EPS)
        nx = x * r
        dg_ref[...] += jnp.sum(dh * nx, axis=0, keepdims=True)
        dhg = dh * g_ref[...]
        dx_ref[...] = dx2_ref[...] + r * (dhg - nx * jnp.mean(dhg * nx, axis=-1, keepdims=True))

    tile = pl.BlockSpec((tm, D_MODEL), lambda i: (i, 0))
    one = pl.BlockSpec((1, D_MODEL), lambda i: (0, 0))
    return pl.pallas_call(
        body,
        name="inproj_bwd_dx",
        grid=(T // tm,),
        in_specs=[tile] * N_GROUPS + [tile, tile, one, pl.BlockSpec(memory_space=pl.ANY)],
        out_specs=[tile, one],
        out_shape=[jax.ShapeDtypeStruct((T, D_MODEL), F32), jax.ShapeDtypeStruct((1, D_MODEL), F32)],
        scratch_shapes=[pltpu.VMEM(w_all.shape, w_all.dtype), pltpu.SemaphoreType.DMA],
        compiler_params=_params(("arbitrary",)),
    )(*dparts, x2d, dx2, g_in, w_all)


def _inproj_bwd_dw(hb, dparts):
    T = hb.shape[0]
    tm = min(512, T)
    half = D_MODEL // 2

    def body(*refs):
        hb_ref = refs[0]
        d_refs = refs[1:1 + N_GROUPS]
        out_ref = refs[1 + N_GROUPS]
        j = pl.program_id(0)

        @pl.when(pl.program_id(1) == 0)
        def _():
            out_ref[...] = jnp.zeros_like(out_ref)

        for jj in range(N_GROUPS):
            @pl.when(j == jj)
            def _(jj=jj):
                g = _dot_tn(hb_ref[...], d_refs[jj][...])
                out_ref[0, 0] += g[:half]
                out_ref[0, 1] += g[half:]

    def dspec(jj):
        return pl.BlockSpec((tm, D_MODEL), lambda j, i: (jnp.where(j == jj, i, 0), 0))

    return pl.pallas_call(
        body,
        name="inproj_bwd_dw",
        grid=(N_GROUPS, T // tm),
        in_specs=[pl.BlockSpec((tm, D_MODEL), lambda j, i: (i, 0))] + [dspec(jj) for jj in range(N_GROUPS)],
        out_specs=pl.BlockSpec((1, 2, half, D_MODEL), lambda j, i: (j // 2, 0, 0, j % 2)),
        out_shape=jax.ShapeDtypeStruct((N_CHIPS, 2, half, 2 * D_MODEL), F32),
        compiler_params=_params(("parallel", "arbitrary")),
    )(hb, *dparts)


def _coords():
    return lax.axis_index("x"), lax.axis_index("y"), lax.axis_index("c")


def _other_chips(x, y):
    return [(1 - x, y), (x, 1 - y), (1 - x, 1 - y)]


def _all_gather8(xs, name):
    m_per, n = xs.shape

    def body(x_ref, out_ref, send_sems, recv_sems, local_sem):
        x, y, c = _coords()
        me, sibling = (x, y, c), (x, y, 1 - c)
        chips = _other_chips(x, y)

        def rows(px, py, pc):
            return out_ref.at[pl.ds((4 * px + 2 * py + pc) * m_per, m_per), :]

        def copy(k, block, to, src=None):
            return pltpu.make_async_remote_copy(
                src_ref=rows(*block) if src is None else src, dst_ref=rows(*block),
                send_sem=send_sems.at[k], recv_sem=recv_sems.at[k], device_id=to, device_id_type=MESH)

        mine = pltpu.make_async_copy(x_ref, rows(*me), local_sem)
        mine.start()
        first = [copy(0, me, sibling, src=x_ref)]
        first += [copy(1 + j, me, (*chip, c), src=x_ref) for j, chip in enumerate(chips)]
        for cp in first:
            cp.start()
        passed = [copy(4 + j, (*chip, c), sibling) for j, chip in enumerate(chips)]
        for j, chip in enumerate(chips):
            copy(1 + j, (*chip, c), me).wait_recv()
            passed[j].start()
        copy(0, sibling, me).wait_recv()
        for j, chip in enumerate(chips):
            copy(4 + j, (*chip, 1 - c), me).wait_recv()
        for cp in first + passed:
            cp.wait_send()
        mine.wait()

    return pl.pallas_call(
        body,
        name=name,
        out_shape=jax.ShapeDtypeStruct((8 * m_per, n), xs.dtype),
        in_specs=[pl.BlockSpec(memory_space=pltpu.VMEM)],
        out_specs=pl.BlockSpec(memory_space=pltpu.VMEM),
        scratch_shapes=[pltpu.SemaphoreType.DMA((7,)), pltpu.SemaphoreType.DMA((7,)), pltpu.SemaphoreType.DMA],
        compiler_params=pltpu.CompilerParams(vmem_limit_bytes=VMEM_LIMIT),
    )(xs)


def _chunks(rows, n):
    size = rows // n
    return [pl.ds(q * size, size) for q in range(n)]


def _cast_into_slot(ws, name):
    n = len(ws)
    nt = 2

    def body(s_ref, *refs):
        for a in range(n):
            refs[n + a][0] = refs[a][...].astype(refs[n + a].dtype)

    xi, yi, _ = _coords()
    return pl.pallas_call(
        body,
        name=name,
        grid_spec=pltpu.PrefetchScalarGridSpec(
            num_scalar_prefetch=1,
            grid=(2, nt),
            in_specs=[pl.BlockSpec((1, w.shape[1] // nt, w.shape[2]), lambda hf, i, s: (hf, i, 0)) for w in ws],
            out_specs=[pl.BlockSpec((1, 1, w.shape[1] // nt, w.shape[2]), lambda hf, i, s: (s[0], hf, i, 0)) for w in ws],
        ),
        out_shape=[jax.ShapeDtypeStruct((N_CHIPS,) + w.shape, _MXU) for w in ws],
        compiler_params=_params(("parallel", "parallel")),
    )((2 * xi + yi).reshape(1).astype(jnp.int32), *ws)


def _gather_chips(bufs, n_chunks, name):
    n = len(bufs)
    pieces = [(a, rows) for a in range(n) for rows in _chunks(bufs[a].shape[2], n_chunks[a])]
    n_p = len(pieces)

    def body(*refs):
        outs = refs[n:2 * n]
        send_sems, recv_sems, fsend_sems, frecv_sems = refs[2 * n:]
        x, y, c = _coords()
        me = 2 * x + y
        chips = _other_chips(x, y)

        def send(k, i, slot, chip):
            a, rows = pieces[i]
            return pltpu.make_async_remote_copy(
                src_ref=outs[a].at[slot, c, rows], dst_ref=outs[a].at[slot, c, rows], send_sem=send_sems.at[k * n_p + i],
                recv_sem=recv_sems.at[k * n_p + i], device_id=(*chip, c), device_id_type=MESH)

        def forward(k, i, slot, half):
            a, rows = pieces[i]
            return pltpu.make_async_remote_copy(
                src_ref=outs[a].at[slot, half, rows], dst_ref=outs[a].at[slot, half, rows],
                send_sem=fsend_sems.at[k * n_p + i], recv_sem=frecv_sems.at[k * n_p + i],
                device_id=(x, y, 1 - c), device_id_type=MESH)

        sends = [send(k, i, me, chip) for i in range(n_p) for k, chip in enumerate(chips)]
        for cp in sends:
            cp.start()
        forwards = []
        for i in range(n_p):
            for k, (px, py) in enumerate(chips):
                send(k, i, 2 * px + py, (px, py)).wait_recv()
                fw = forward(k, i, 2 * px + py, c)
                fw.start()
                forwards.append(fw)
        for i in range(n_p):
            for k, (px, py) in enumerate(chips):
                forward(k, i, 2 * px + py, 1 - c).wait_recv()
        for cp in sends + forwards:
            cp.wait_send()

    anyspec = pl.BlockSpec(memory_space=pl.ANY)
    sems = pltpu.SemaphoreType.DMA((3 * n_p,))
    return pl.pallas_call(
        body,
        name=name,
        in_specs=[anyspec] * n,
        out_specs=[anyspec] * n,
        out_shape=[jax.ShapeDtypeStruct(b.shape, b.dtype) for b in bufs],
        input_output_aliases={a: a for a in range(n)},
        scratch_shapes=[sems, sems, sems, sems],
    )(*bufs)


def _swap_halves(arrs, name):
    n = len(arrs)

    def body(*refs):
        ins, outs = refs[:n], refs[n:2 * n]
        send_sems, recv_sems = refs[2 * n:]
        x, y, c = _coords()
        copies = [pltpu.make_async_remote_copy(
            src_ref=ins[a].at[p, 1 - c], dst_ref=outs[a].at[p], send_sem=send_sems.at[a * N_CHIPS + p],
            recv_sem=recv_sems.at[a * N_CHIPS + p], device_id=(x, y, 1 - c), device_id_type=MESH)
            for a in range(n) for p in range(N_CHIPS)]
        for cp in copies:
            cp.start()
        for cp in copies:
            cp.wait()

    anyspec = pl.BlockSpec(memory_space=pl.ANY)
    return pl.pallas_call(
        body,
        name=name,
        in_specs=[anyspec] * n,
        out_specs=[anyspec] * n,
        out_shape=[jax.ShapeDtypeStruct((N_CHIPS,) + a.shape[2:], a.dtype) for a in arrs],
        scratch_shapes=[pltpu.SemaphoreType.DMA((N_CHIPS * n,)), pltpu.SemaphoreType.DMA((N_CHIPS * n,))],
    )(*arrs)


def _scatter_chips(arrs, n_chunks, name):
    n = len(arrs)
    pieces = [(a, rows) for a in range(n) for rows in _chunks(arrs[a].shape[1], n_chunks[a])]
    n_p = len(pieces)

    def body(*refs):
        ins, outs = refs[:n], refs[n:2 * n]
        send_sems, recv_sems = refs[2 * n:]
        x, y, c = _coords()
        chips = _other_chips(x, y)

        def copy(k, i, src_slot, chip):
            a, rows = pieces[i]
            return pltpu.make_async_remote_copy(
                src_ref=ins[a].at[src_slot, rows], dst_ref=outs[a].at[k, rows],
                send_sem=send_sems.at[k * n_p + i], recv_sem=recv_sems.at[k * n_p + i],
                device_id=(*chip, c), device_id_type=MESH)

        sends = [copy(k, i, 2 * px + py, (px, py)) for i in range(n_p) for k, (px, py) in enumerate(chips)]
        for cp in sends:
            cp.start()
        for cp in sends:
            cp.wait()

    anyspec = pl.BlockSpec(memory_space=pl.ANY)
    return pl.pallas_call(
        body,
        name=name,
        in_specs=[anyspec] * n,
        out_specs=[anyspec] * n,
        out_shape=[jax.ShapeDtypeStruct((3,) + a.shape[1:], a.dtype) for a in arrs],
        scratch_shapes=[pltpu.SemaphoreType.DMA((3 * n_p,)), pltpu.SemaphoreType.DMA((3 * n_p,))],
    )(*arrs)


def _join_halves(bufs, n_chunks, name):
    n = len(bufs)
    pieces = [(a, rows) for a in range(n) for rows in _chunks(bufs[a].shape[1], n_chunks[a])]
    n_p = len(pieces)

    def body(*refs):
        outs = refs[n:2 * n]
        send_sems, recv_sems = refs[2 * n:]
        x, y, c = _coords()

        def copy(i, half):
            a, rows = pieces[i]
            return pltpu.make_async_remote_copy(
                src_ref=outs[a].at[half, rows], dst_ref=outs[a].at[half, rows], send_sem=send_sems.at[i],
                recv_sem=recv_sems.at[i], device_id=(x, y, 1 - c), device_id_type=MESH)

        sends = [copy(i, c) for i in range(n_p)]
        for cp in sends:
            cp.start()
        for i in range(n_p):
            copy(i, 1 - c).wait_recv()
        for cp in sends:
            cp.wait_send()

    anyspec = pl.BlockSpec(memory_space=pl.ANY)
    sems = pltpu.SemaphoreType.DMA((n_p,))
    return pl.pallas_call(
        body,
        name=name,
        in_specs=[anyspec] * n,
        out_specs=[anyspec] * n,
        out_shape=[jax.ShapeDtypeStruct(b.shape, b.dtype) for b in bufs],
        input_output_aliases={a: a for a in range(n)},
        scratch_shapes=[sems, sems],
    )(*bufs)


def _row_tile(rows, cap):
    t = cap
    while rows % t:
        t //= 2
    return t


def _add_my_half(g, r, name):
    _, _, R, C = g.shape
    tr = _row_tile(R, 256)

    def body(c_ref, g_ref, r_ref, o_ref):
        o_ref[...] = (g_ref[0] + r_ref[...]).astype(o_ref.dtype)

    return pl.pallas_call(
        body,
        name=name,
        grid_spec=pltpu.PrefetchScalarGridSpec(
            num_scalar_prefetch=1,
            grid=(N_CHIPS, R // tr),
            in_specs=[pl.BlockSpec((1, 1, tr, C), lambda p, i, c_ref: (p, c_ref[0], i, 0)),
                      pl.BlockSpec((1, tr, C), lambda p, i, c_ref: (p, i, 0))],
            out_specs=pl.BlockSpec((1, tr, C), lambda p, i, c_ref: (p, i, 0)),
        ),
        out_shape=jax.ShapeDtypeStruct(r.shape, jnp.bfloat16),
        compiler_params=_params(("parallel", "parallel")),
    )(lax.axis_index("c").reshape(1).astype(jnp.int32), g, r)


def _sum_slabs(own, got, name):
    _, R, C = own.shape
    tr = _row_tile(R, 256)

    def body(s_ref, own_ref, got_ref, o_ref):
        o_ref[0] = ((own_ref[0].astype(F32) + got_ref[0].astype(F32)) + got_ref[1].astype(F32)) + got_ref[2].astype(F32)

    xi, yi, ci = _coords()
    return pl.pallas_call(
        body,
        name=name,
        grid_spec=pltpu.PrefetchScalarGridSpec(
            num_scalar_prefetch=1,
            grid=(R // tr,),
            in_specs=[pl.BlockSpec((1, tr, C), lambda i, s: (s[0], i, 0)),
                      pl.BlockSpec((3, tr, C), lambda i, s: (0, i, 0))],
            out_specs=pl.BlockSpec((1, tr, C), lambda i, s: (s[1], i, 0)),
        ),
        out_shape=jax.ShapeDtypeStruct((2, R, C), F32),
        compiler_params=_params(("parallel",)),
    )(jnp.stack([2 * xi + yi, ci]).astype(jnp.int32), own, got)


def _sum_rows8(g, m_per, name):
    n = g.shape[1]

    def body(g_ref, o_ref):
        acc = g_ref[0:m_per, :]
        for k in range(1, 8):
            acc = acc + g_ref[k * m_per:(k + 1) * m_per, :]
        o_ref[...] = acc

    return pl.pallas_call(
        body,
        name=name,
        out_shape=jax.ShapeDtypeStruct((m_per, n), F32),
        compiler_params=_params(),
    )(g)


def _adamw_math(w, g, m, v):
    m = ADAM_B1 * m + (1.0 - ADAM_B1) * g
    v = ADAM_B2 * v + (1.0 - ADAM_B2) * (g * g)
    m_hat = m / (1.0 - ADAM_B1 ** ADAM_STEP)
    v_hat = v / (1.0 - ADAM_B2 ** ADAM_STEP)
    delta = -ADAM_LR * (m_hat / (jnp.sqrt(v_hat) + ADAM_EPS) + ADAM_WD * w)
    return delta, m, v


def _adamw_big(w, g, m, v, name):
    R, C = w.shape
    tr = min(128, R)

    def body(w_ref, g_ref, m_ref, v_ref, d_out, m_out, v_out):
        d, mn, vn = _adamw_math(w_ref[...], g_ref[...], m_ref[...], v_ref[...])
        d_out[...] = d
        m_out[...] = mn
        v_out[...] = vn

    spec = pl.BlockSpec((tr, C), lambda i: (i, 0))
    return pl.pallas_call(
        body,
        name=name,
        grid=(R // tr,),
        in_specs=[spec] * 4,
        out_specs=[spec] * 3,
        out_shape=[jax.ShapeDtypeStruct((R, C), F32)] * 3,
        compiler_params=_params(("parallel",)),
    )(w, g, m, v)


def _adamw_small(ws, gs, ms, vs, name):
    n = len(ws)

    def body(*refs):
        for a in range(n):
            d, mn, vn = _adamw_math(refs[a][...], refs[n + a][...], refs[2 * n + a][...], refs[3 * n + a][...])
            refs[4 * n + a][...] = d
            refs[5 * n + a][...] = mn
            refs[6 * n + a][...] = vn

    shapes = [jax.ShapeDtypeStruct(w.shape, F32) for w in ws]
    outs = pl.pallas_call(
        body,
        name=name,
        out_shape=shapes * 3,
        compiler_params=_params(),
    )(*ws, *gs, *ms, *vs)
    return outs[:n], outs[n:2 * n], outs[2 * n:]


def _to_blockdiag(w):
    per = CW // LRU_BW
    w4 = w.reshape(N_CT, per, LRU_BW, LRU_BW)
    eye = jnp.eye(per, dtype=w.dtype)
    return (w4[:, :, :, None, :] * eye[None, :, None, :, None]).reshape(N_CT, CW, CW)


def _from_blockdiag(g):
    per = CW // LRU_BW
    g5 = g.reshape(N_CT, per, LRU_BW, per, LRU_BW)
    return jnp.stack([g5[:, b, :, b, :] for b in range(per)], axis=1).reshape(LRU_BLOCKS, LRU_BW, LRU_BW)


def _local_grads(x2d, tgt2d, B, S, g_in, w_all, conv_w, conv_b, gate_x_w, gate_x_b, gate_a_w, gate_a_b, lam, gain,
                 wpa, wpb, wout, g_fin):
    wx_bd = _c(_to_blockdiag(gate_x_w))
    wa_bd = _c(_to_blockdiag(gate_a_w))
    tables = _retention_tables(S)
    gain3 = gain.reshape(HEADS, 1, DK)

    proj, hb = _inproj_fwd(x2d, g_in, w_all)
    hlru, ya = _lru_fwd(proj, conv_w, conv_b, wx_bd, wa_bd, gate_x_b, gate_a_b, lam, B, S)
    o_pre, yb, states = _ret_fwd(proj, tables, gain3, B, S)
    loss, dx2, dya, dyb, dma, dmb, dgf, gw_proj = _mid(ya, yb, proj, x2d, tgt2d, wpa, wpb, wout, g_fin)
    dxa, dga, dcw, dcb, dwx_bd, dwa_bd, dbx, dba, dlam = _lru_bwd(
        dya, proj, hlru, conv_w, conv_b, wx_bd, wa_bd, gate_x_b, gate_a_b, lam, B, S)
    dq, dk, dv, dgb, dgain = _ret_bwd(dyb, o_pre, proj, states, tables, gain3, B, S)
    dparts = [dxa, dga, dq, dk, dv, dgb, dma, dmb]
    grad_x, dgin = _inproj_bwd_dx(dparts, w_all, x2d, dx2, g_in)
    gw_in = _inproj_bwd_dw(hb, dparts)
    small = dict(norm_in=dgin, conv_w=dcw, conv_b=dcb, gate_x_w=_from_blockdiag(dwx_bd), gate_x_b=dbx,
                 gate_a_w=_from_blockdiag(dwa_bd), gate_a_b=dba, lru_lambda=dlam, gn_gain=dgain.reshape(HEADS, DK),
                 norm_final=dgf)
    return loss[0, 0], grad_x, gw_in, gw_proj, small


_SMALL = ("gate_x_w", "gate_a_w", "norm_in", "conv_w", "conv_b", "gate_x_b", "gate_a_b", "lru_lambda", "gn_gain",
          "norm_final")
_SMALL_SHAPES = dict(gate_x_w=(LRU_BLOCKS, LRU_BW, LRU_BW), gate_a_w=(LRU_BLOCKS, LRU_BW, LRU_BW),
                     norm_in=(1, D_MODEL), conv_w=(CONV, D_MODEL), conv_b=(1, D_MODEL), gate_x_b=(1, D_MODEL),
                     gate_a_b=(1, D_MODEL), lru_lambda=(1, D_MODEL), gn_gain=(HEADS, DK), norm_final=(1, D_MODEL))


def _pack_small(small):
    return jnp.concatenate([small[k].reshape(-1, 128) for k in _SMALL], axis=0)


def _unpack_small(packed):
    out, r = {}, 0
    for k in _SMALL:
        shape = _SMALL_SHAPES[k]
        rows = 1
        for s in shape:
            rows *= s
        rows //= 128
        out[k] = packed[r:r + rows].reshape(shape)
        r += rows
    return out


def kernel(x, norm_in, w_in, conv_w, conv_b, gate_x_w, gate_x_b, gate_a_w, gate_a_b, lru_lambda, gn_gain, w_proj_a, w_proj_b, w_out, norm_final, loss_target, m_norm_in, m_w_in, m_conv_w, m_conv_b, m_gate_x_w, m_gate_x_b, m_gate_a_w, m_gate_a_b, m_lru_lambda, m_gn_gain, m_w_proj_a, m_w_proj_b, m_w_out, m_norm_final, v_norm_in, v_w_in, v_conv_w, v_conv_b, v_gate_x_w, v_gate_x_b, v_gate_a_w, v_gate_a_b, v_lru_lambda, v_gn_gain, v_w_proj_a, v_w_proj_b, v_w_out, v_norm_final):
    B, S, _ = x.shape
    T = B * S
    xi, yi, ci = _coords()
    chip = 2 * xi + yi

    cshard = D_MODEL // N_CHIPS
    mine = _cast_into_slot([w_in[0].reshape(2, D_MODEL // 2, 2 * D_MODEL)]
                           + [w[0].reshape(2, cshard // 2, D_MODEL) for w in (w_proj_a, w_proj_b, w_out)],
                           "cast_weights")
    big = _gather_chips(mine, [4, 1, 1, 1], "gather_weights")
    w_all = big[0].reshape(N_CHIPS, D_MODEL, 2 * D_MODEL)
    wpa, wpb, wout = (b.reshape(D_MODEL, D_MODEL) for b in big[1:])
    gshard = DK // N_CHIPS
    tiny = jnp.concatenate([conv_w[0], jnp.zeros((4, cshard), F32), jnp.pad(gn_gain[0], ((0, 4), (0, cshard - gshard)))],
                           axis=0)
    tiny_all = _all_gather8(tiny, "gather_small_weights").reshape(N_CHIPS, 2, 16, cshard)[:, 0]
    conv_w_full = jnp.transpose(tiny_all[:, 0:CONV, :], (1, 0, 2)).reshape(CONV, D_MODEL)
    gain_full = jnp.transpose(tiny_all[:, 8:8 + HEADS, :gshard], (1, 0, 2)).reshape(HEADS, DK)

    loss, grad_x, gw_in, gw_proj, small = _local_grads(
        x.reshape(T, D_MODEL), loss_target.reshape(T, D_MODEL), B, S, norm_in, w_all, conv_w_full, conv_b,
        gate_x_w[0], gate_x_b, gate_a_w[0], gate_a_b, lru_lambda, gain_full, wpa, wpb, wout,
        norm_final.reshape(1, D_MODEL))
    loss = lax.psum(loss, ("x", "y", "c"))

    packed = _pack_small(small)
    m_per = packed.shape[0]
    gsm = _unpack_small(_sum_rows8(_all_gather8(packed, "gather_small_grads"), m_per, "sum_small_grads"))

    rows_p = 3 * D_MODEL // (2 * N_CHIPS)
    gw_proj = gw_proj.reshape(N_CHIPS, 2, rows_p, D_MODEL)
    other = _swap_halves([gw_in, gw_proj], "swap_halves")
    chip_in = _add_my_half(gw_in, other[0], "chip_sum_w_in")
    chip_pr = _add_my_half(gw_proj, other[1], "chip_sum_w_proj")
    got = _scatter_chips([chip_in, chip_pr], [4, 2], "scatter_chips")
    half_in = _sum_slabs(chip_in, got[0], "sum_w_in")
    half_pr = _sum_slabs(chip_pr, got[1], "sum_w_proj")
    g_in_full, g_pr_full = _join_halves([half_in, half_pr], [8, 4], "join_halves")
    g_w_in = g_in_full.reshape(D_MODEL, 2 * D_MODEL)
    g_pr = g_pr_full.reshape(2, 3, D_MODEL // (2 * N_CHIPS), D_MODEL)
    g_wpa, g_wpb, g_wout = (g_pr[:, k].reshape(cshard, D_MODEL) for k in range(3))

    grads = dict(gsm)
    grads["conv_w"] = lax.dynamic_slice_in_dim(gsm["conv_w"], chip * cshard, cshard, axis=1)
    grads["gn_gain"] = lax.dynamic_slice_in_dim(gsm["gn_gain"], chip * gshard, gshard, axis=1)
    grads.update(w_in=g_w_in, w_proj_a=g_wpa, w_proj_b=g_wpb, w_out=g_wout)

    weights = dict(norm_in=norm_in, w_in=w_in, conv_w=conv_w, conv_b=conv_b, gate_x_w=gate_x_w, gate_x_b=gate_x_b,
                   gate_a_w=gate_a_w, gate_a_b=gate_a_b, lru_lambda=lru_lambda, gn_gain=gn_gain, w_proj_a=w_proj_a,
                   w_proj_b=w_proj_b, w_out=w_out, norm_final=norm_final)
    ms = dict(norm_in=m_norm_in, w_in=m_w_in, conv_w=m_conv_w, conv_b=m_conv_b, gate_x_w=m_gate_x_w,
              gate_x_b=m_gate_x_b, gate_a_w=m_gate_a_w, gate_a_b=m_gate_a_b, lru_lambda=m_lru_lambda, gn_gain=m_gn_gain,
              w_proj_a=m_w_proj_a, w_proj_b=m_w_proj_b, w_out=m_w_out, norm_final=m_norm_final)
    vs = dict(norm_in=v_norm_in, w_in=v_w_in, conv_w=v_conv_w, conv_b=v_conv_b, gate_x_w=v_gate_x_w,
              gate_x_b=v_gate_x_b, gate_a_w=v_gate_a_w, gate_a_b=v_gate_a_b, lru_lambda=v_lru_lambda, gn_gain=v_gn_gain,
              w_proj_a=v_w_proj_a, w_proj_b=v_w_proj_b, w_out=v_w_out, norm_final=v_norm_final)
    names = list(weights)
    grads = {k: grads[k].reshape(weights[k].shape) for k in names}

    delta, new_m, new_v = {}, {}, {}
    for k in ("w_in", "w_proj_a", "w_proj_b", "w_out"):
        shp = weights[k].shape
        two = lambda a: a.reshape(shp[1], shp[2])
        d, mn, vn = _adamw_big(two(weights[k]), two(grads[k]), two(ms[k]), two(vs[k]), "adamw_" + k)
        delta[k], new_m[k], new_v[k] = d.reshape(shp), mn.reshape(shp), vn.reshape(shp)
    smalls = [k for k in names if k not in delta]

    def view(a):
        return a.reshape(1, -1) if a.ndim == 1 else (a.reshape(a.shape[1:]) if a.ndim > 2 else a)

    ds, mns, vns = _adamw_small([view(weights[k]) for k in smalls], [view(grads[k]) for k in smalls],
                                [view(ms[k]) for k in smalls], [view(vs[k]) for k in smalls], "adamw_small")
    for k, d, mn, vn in zip(smalls, ds, mns, vns):
        shp = weights[k].shape
        delta[k], new_m[k], new_v[k] = d.reshape(shp), mn.reshape(shp), vn.reshape(shp)

    return (loss, grad_x.reshape(B, S, D_MODEL), *[grads[k] for k in names], *[delta[k] for k in names],
            *[new_m[k] for k in names], *[new_v[k] for k in names])
```

```python
import functools

import jax
import jax.numpy as jnp
from jax import lax
from jax.experimental import pallas as pl
from jax.experimental.pallas import tpu as pltpu

F32 = jnp.float32
_MXU = jnp.bfloat16

D_MODEL = 1024
N_GROUPS = 8
HEADS = 4
DK = 256
CHUNK = 128
CONV = 4
LRU_BLOCKS = 16
LRU_BW = 64
LRU_C = 8.0
ROPE_THETA = 10000.0
EPS = 1e-6
CW = 256
N_CT = D_MODEL // CW
N_CHIPS = 4
MESH = pl.DeviceIdType.MESH

ADAM_LR = 0.001
ADAM_B1 = 0.9
ADAM_B2 = 0.999
ADAM_EPS = 1e-08
ADAM_WD = 0.01
ADAM_STEP = 10

VMEM_LIMIT = 56 * 1024 * 1024


def _c(v):
    return v.astype(_MXU)


def _dot(a, b):
    return lax.dot_general(a, b, (((1,), (0,)), ((), ())), preferred_element_type=F32)


def _dot_nt(a, b):
    return lax.dot_general(a, b, (((1,), (1,)), ((), ())), preferred_element_type=F32)


def _dot_tn(a, b):
    return lax.dot_general(a, b, (((0,), (0,)), ((), ())), preferred_element_type=F32)


def _sigmoid(z):
    return 1.0 / (1.0 + jnp.exp(-z))


def _params(sem=None):
    if sem is None:
        return pltpu.CompilerParams(vmem_limit_bytes=VMEM_LIMIT)
    return pltpu.CompilerParams(vmem_limit_bytes=VMEM_LIMIT, dimension_semantics=sem)


def _inproj_fwd(x2d, g_in, w_all):
    T = x2d.shape[0]
    tm = min(512, T)
    n_i = T // tm

    def body(x_ref, g_ref, w_ref, proj_ref, ht_ref, h_all):
        i = pl.program_id(1)
        rows = pl.ds(pl.multiple_of(i * tm, tm), tm)

        @pl.when(pl.program_id(0) == 0)
        def _():
            x = x_ref[...]
            r = lax.rsqrt(jnp.mean(x * x, axis=-1, keepdims=True) + EPS)
            h = x * r * g_ref[...]
            h_all[rows, :] = h.astype(h_all.dtype)
            ht_ref[...] = h.T.astype(ht_ref.dtype)

        proj_ref[...] = _dot(h_all[rows, :], w_ref[0])

    first = lambda j, i: jnp.where(j == 0, i, n_i - 1)
    return pl.pallas_call(
        body,
        name="inproj_fwd",
        grid=(N_GROUPS, n_i),
        in_specs=[
            pl.BlockSpec((tm, D_MODEL), lambda j, i: (first(j, i), 0)),
            pl.BlockSpec((1, D_MODEL), lambda j, i: (0, 0)),
            pl.BlockSpec((1, D_MODEL, D_MODEL), lambda j, i: (j // 2, 0, j % 2)),
        ],
        out_specs=[
            pl.BlockSpec((tm, D_MODEL), lambda j, i: (i, j)),
            pl.BlockSpec((D_MODEL, tm), lambda j, i: (0, first(j, i))),
        ],
        out_shape=[
            jax.ShapeDtypeStruct((T, N_GROUPS * D_MODEL), F32),
            jax.ShapeDtypeStruct((D_MODEL, T), _MXU),
        ],
        scratch_shapes=[pltpu.VMEM((T, D_MODEL), _MXU)],
        compiler_params=_params(("arbitrary", "arbitrary")),
    )(x2d, g_in, w_all)


def _scan_fwd(a, u):
    n = a.shape[0]
    row = lax.broadcasted_iota(jnp.int32, a.shape, 0)
    s = 1
    while s < n:
        m = row >= s
        u = u + a * jnp.where(m, pltpu.roll(u, s, 0), 0.0)
        a = a * jnp.where(m, pltpu.roll(a, s, 0), 1.0)
        s *= 2
    return a, u


def _scan_bwd(b, g):
    n = b.shape[0]
    row = lax.broadcasted_iota(jnp.int32, b.shape, 0)
    s = 1
    while s < n:
        m = row < n - s
        g = g + b * jnp.where(m, pltpu.roll(g, n - s, 0), 0.0)
        b = b * jnp.where(m, pltpu.roll(b, n - s, 0), 1.0)
        s *= 2
    return b, g


def _softplus_neg(lam):
    z = -lam
    return jnp.maximum(z, 0.0) + jnp.log1p(jnp.exp(-jnp.abs(z)))


def _lru_gates(xc, wx_ref, wa_ref, bx_ref, ba_ref, lam_ref):
    xcb = _c(xc)
    i_t = _sigmoid(_dot(xcb, wx_ref[0]) + bx_ref[...])
    r_t = _sigmoid(_dot(xcb, wa_ref[0]) + ba_ref[...])
    sp = _softplus_neg(lam_ref[...])
    log_a = (-LRU_C) * r_t * sp
    a = jnp.exp(log_a)
    mult = jnp.sqrt(1.0 - a * a)
    return xcb, i_t, r_t, sp, a, mult


def _conv_from_ext(ext_ref, xa, cw_ref, cb_ref, tc):
    return (cb_ref[...] + cw_ref[3:4, :] * xa + cw_ref[2:3, :] * ext_ref[7:7 + tc, :]
            + cw_ref[1:2, :] * ext_ref[6:6 + tc, :] + cw_ref[0:1, :] * ext_ref[5:5 + tc, :])


def _lru_fwd(proj, conv_w, conv_b, wx_bd, wa_bd, bx, ba, lam, B, S):
    T = B * S
    tc = min(256, S)
    nt = S // tc
    h8 = tc // 8

    def body(xa_ref, halo_ref, ga_ref, cw_ref, cb_ref, wx_ref, wa_ref, bx_ref, ba_ref, lam_ref,
             h_ref, ya_ref, ext_ref, carry_ref):
        t = pl.program_id(2)

        @pl.when(t == 0)
        def _():
            carry_ref[...] = jnp.zeros_like(carry_ref)

        xa = xa_ref[...]
        ext_ref[0:8, :] = jnp.where(t == 0, 0.0, halo_ref[...])
        ext_ref[8:8 + tc, :] = xa
        xc = _conv_from_ext(ext_ref, xa, cw_ref, cb_ref, tc)
        _, i_t, _, _, a, mult = _lru_gates(xc, wx_ref, wa_ref, bx_ref, ba_ref, lam_ref)
        u = mult * (i_t * xc)
        acum, hloc = _scan_fwd(a, u)
        h = hloc + acum * carry_ref[7:8, :]
        h_ref[...] = h
        carry_ref[...] = h[tc - 8:tc, :]
        ga = ga_ref[...]
        ya_ref[...] = (ga * _sigmoid(ga) * h).astype(ya_ref.dtype)

    row = lambda b, t: b * nt + t
    vec = pl.BlockSpec((1, CW), lambda b, c, t: (0, c))
    mat = pl.BlockSpec((1, CW, CW), lambda b, c, t: (c, 0, 0))
    return pl.pallas_call(
        body,
        name="lru_fwd",
        grid=(B, N_CT, nt),
        in_specs=[
            pl.BlockSpec((tc, CW), lambda b, c, t: (row(b, t), c)),
            pl.BlockSpec((8, CW), lambda b, c, t: (jnp.maximum(row(b, t) * h8 - 1, 0), c)),
            pl.BlockSpec((tc, CW), lambda b, c, t: (row(b, t), N_CT + c)),
            pl.BlockSpec((CONV, CW), lambda b, c, t: (0, c)),
            vec, mat, mat, vec, vec, vec,
        ],
        out_specs=[
            pl.BlockSpec((tc, CW), lambda b, c, t: (row(b, t), c)),
            pl.BlockSpec((tc, CW), lambda b, c, t: (row(b, t), c)),
        ],
        out_shape=[
            jax.ShapeDtypeStruct((T, D_MODEL), F32),
            jax.ShapeDtypeStruct((T, D_MODEL), _MXU),
        ],
        scratch_shapes=[pltpu.VMEM((tc + 8, CW), F32), pltpu.VMEM((8, CW), F32)],
        compiler_params=_params(("parallel", "parallel", "arbitrary")),
    )(proj, proj, proj, conv_w, conv_b, wx_bd, wa_bd, bx, ba, lam)


def _lru_bwd(dya, proj, hlru, conv_w, conv_b, wx_bd, wa_bd, bx, ba, lam, B, S):
    T = B * S
    tc = min(256, S)
    nt = S // tc
    h8 = tc // 8

    def body(dya_ref, xa_ref, xhalo_ref, ga_ref, h_ref, hhalo_ref, cw_ref, cb_ref, wx_ref, wa_ref, bx_ref, ba_ref,
             lam_ref, dxa_ref, dga_ref, dcw_ref, dcb_ref, dwx_ref, dwa_ref, dbx_ref, dba_ref, dlam_ref,
             ext_ref, ext2_ref, carry_ref, dhalo_ref):
        b = pl.program_id(1)
        t = pl.program_id(2)
        tt = nt - 1 - t

        @pl.when(t == 0)
        def _():
            carry_ref[...] = jnp.zeros_like(carry_ref)
            dhalo_ref[...] = jnp.zeros_like(dhalo_ref)

        @pl.when((t == 0) & (b == 0))
        def _():
            for r in (dcw_ref, dcb_ref, dwx_ref, dwa_ref, dbx_ref, dba_ref, dlam_ref):
                r[...] = jnp.zeros_like(r)

        xa = xa_ref[...]
        ext_ref[0:8, :] = jnp.where(tt == 0, 0.0, xhalo_ref[...])
        ext_ref[8:8 + tc, :] = xa
        xc = _conv_from_ext(ext_ref, xa, cw_ref, cb_ref, tc)
        xcb, i_t, r_t, sp, a, mult = _lru_gates(xc, wx_ref, wa_ref, bx_ref, ba_ref, lam_ref)

        h = h_ref[...]
        ga = ga_ref[...]
        dya_t = dya_ref[...]
        sg = _sigmoid(ga)
        dga_ref[...] = (dya_t * h * (sg * (1.0 + ga * (1.0 - sg)))).astype(dga_ref.dtype)
        dlru = dya_t * (ga * sg)

        row = lax.broadcasted_iota(jnp.int32, a.shape, 0)
        coef = jnp.where(row == tc - 1, 1.0, pltpu.roll(a, tc - 1, 0))
        bcum, dloc = _scan_bwd(coef, dlru)
        dh = dloc + bcum * carry_ref[0:1, :]
        ext2_ref[0:tc, :] = a * dh
        carry_ref[...] = ext2_ref[0:8, :]

        ext2_ref[0:8, :] = jnp.where(tt == 0, 0.0, hhalo_ref[...])
        ext2_ref[8:8 + tc, :] = h
        hprev = ext2_ref[7:7 + tc, :]

        da = dh * hprev
        ix = i_t * xc
        dmult = dh * ix
        di = dh * mult * xc
        dxc = dh * mult * i_t
        dlog_a = da * a - dmult * (a * a) / mult
        dr = dlog_a * ((-LRU_C) * sp)
        dlam_ref[...] += jnp.sum(dlog_a * r_t, axis=0, keepdims=True) * (LRU_C * _sigmoid(-lam_ref[...]))
        dza = dr * r_t * (1.0 - r_t)
        dzx = di * i_t * (1.0 - i_t)
        dzab = _c(dza)
        dzxb = _c(dzx)
        dxc = dxc + _dot_nt(dzxb, wx_ref[0]) + _dot_nt(dzab, wa_ref[0])
        dwx_ref[0] += _dot_tn(xcb, dzxb)
        dwa_ref[0] += _dot_tn(xcb, dzab)
        dbx_ref[...] += jnp.sum(dzx, axis=0, keepdims=True)
        dba_ref[...] += jnp.sum(dza, axis=0, keepdims=True)

        dcb_ref[...] += jnp.sum(dxc, axis=0, keepdims=True)
        dcw_ref[3:4, :] += jnp.sum(dxc * xa, axis=0, keepdims=True)
        dcw_ref[2:3, :] += jnp.sum(dxc * ext_ref[7:7 + tc, :], axis=0, keepdims=True)
        dcw_ref[1:2, :] += jnp.sum(dxc * ext_ref[6:6 + tc, :], axis=0, keepdims=True)
        dcw_ref[0:1, :] += jnp.sum(dxc * ext_ref[5:5 + tc, :], axis=0, keepdims=True)
        ext2_ref[0:tc, :] = dxc
        ext2_ref[tc:tc + 8, :] = dhalo_ref[...]
        dxa = (cw_ref[3:4, :] * dxc + cw_ref[2:3, :] * ext2_ref[1:1 + tc, :]
               + cw_ref[1:2, :] * ext2_ref[2:2 + tc, :] + cw_ref[0:1, :] * ext2_ref[3:3 + tc, :])
        dxa_ref[...] = dxa.astype(dxa_ref.dtype)
        dhalo_ref[...] = ext2_ref[0:8, :]

    row_of = lambda b, t: b * nt + (nt - 1 - t)
    tile = lambda off: pl.BlockSpec((tc, CW), lambda c, b, t: (row_of(b, t), off + c))
    halo = pl.BlockSpec((8, CW), lambda c, b, t: (jnp.maximum(row_of(b, t) * h8 - 1, 0), c))
    vec = pl.BlockSpec((1, CW), lambda c, b, t: (0, c))
    mat = pl.BlockSpec((1, CW, CW), lambda c, b, t: (c, 0, 0))
    cwspec = pl.BlockSpec((CONV, CW), lambda c, b, t: (0, c))
    return pl.pallas_call(
        body,
        name="lru_bwd",
        grid=(N_CT, B, nt),
        in_specs=[tile(0), tile(0), halo, tile(N_CT), tile(0), halo, cwspec, vec, mat, mat, vec, vec, vec],
        out_specs=[tile(0), tile(0), cwspec, vec, mat, mat, vec, vec, vec],
        out_shape=[
            jax.ShapeDtypeStruct((T, D_MODEL), _MXU),
            jax.ShapeDtypeStruct((T, D_MODEL), _MXU),
            jax.ShapeDtypeStruct((CONV, D_MODEL), F32),
            jax.ShapeDtypeStruct((1, D_MODEL), F32),
            jax.ShapeDtypeStruct((N_CT, CW, CW), F32),
            jax.ShapeDtypeStruct((N_CT, CW, CW), F32),
            jax.ShapeDtypeStruct((1, D_MODEL), F32),
            jax.ShapeDtypeStruct((1, D_MODEL), F32),
            jax.ShapeDtypeStruct((1, D_MODEL), F32),
        ],
        scratch_shapes=[pltpu.VMEM((tc + 8, CW), F32), pltpu.VMEM((tc + 8, CW), F32),
                        pltpu.VMEM((8, CW), F32), pltpu.VMEM((8, CW), F32)],
        compiler_params=_params(("parallel", "arbitrary", "arbitrary")),
    )(dya, proj, proj, proj, hlru, hlru, conv_w, conv_b, wx_bd, wa_bd, bx, ba, lam)


def _retention_tables(S):
    half = DK // 2
    freqs = ROPE_THETA ** (-jnp.arange(half, dtype=F32) / half)
    ang = jnp.arange(S, dtype=F32)[:, None] * freqs[None, :]
    log_g = jnp.log1p(-(2.0 ** (-5.0 - jnp.arange(HEADS, dtype=F32))))
    idx = jnp.arange(CHUNK, dtype=F32)
    diff = idx[:, None] - idx[None, :]
    inner = jnp.where(diff >= 0, jnp.exp(jnp.maximum(diff, 0.0)[None] * log_g[:, None, None]), 0.0)
    cross = jnp.exp((idx[None, :] + 1.0) * log_g[:, None])[:, :, None]
    state = jnp.exp((CHUNK - 1.0 - idx[None, :]) * log_g[:, None])[:, :, None]
    gam = jnp.broadcast_to(jnp.exp(CHUNK * log_g)[:, None, None], (HEADS, 1, DK))
    return jnp.cos(ang), jnp.sin(ang), inner, cross, state, gam


def _rot(x, cos, sin):
    half = DK // 2
    x1, x2 = x[:, :half], x[:, half:]
    return jnp.concatenate([x1 * cos - x2 * sin, x1 * sin + x2 * cos], axis=-1)


def _rot_t(y, cos, sin):
    half = DK // 2
    y1, y2 = y[:, :half], y[:, half:]
    return jnp.concatenate([y1 * cos + y2 * sin, y2 * cos - y1 * sin], axis=-1)


def _groupnorm(o):
    mu = jnp.mean(o, axis=-1, keepdims=True)
    oc = o - mu
    rs = lax.rsqrt(jnp.mean(oc * oc, axis=-1, keepdims=True) + EPS)
    return oc * rs, rs


def _ret_specs(S, chunk_of):
    nc = S // CHUNK
    qkv = lambda g: pl.BlockSpec((CHUNK, D_MODEL), lambda b, c: (b * nc + chunk_of(c), g))
    act = pl.BlockSpec((CHUNK, D_MODEL), lambda b, c: (b * nc + chunk_of(c), 0))
    rope = pl.BlockSpec((CHUNK, DK // 2), lambda b, c: (chunk_of(c), 0))
    dmat = pl.BlockSpec((HEADS, CHUNK, CHUNK), lambda b, c: (0, 0, 0))
    dvec = pl.BlockSpec((HEADS, CHUNK, 1), lambda b, c: (0, 0, 0))
    hrow = pl.BlockSpec((HEADS, 1, DK), lambda b, c: (0, 0, 0))
    rst = pl.BlockSpec((1, HEADS, DK, DK), lambda b, c: (b * nc + chunk_of(c), 0, 0, 0))
    return qkv, act, rope, dmat, dvec, hrow, rst


def _ret_fwd(proj, tables, gain3, B, S):
    T = B * S
    nc = S // CHUNK
    cos, sin, dmat_t, cd_t, sd_t, gam_t = tables

    def body(q_ref, k_ref, v_ref, gb_ref, cos_ref, sin_ref, dm_ref, cd_ref, sd_ref, gam_ref, gain_ref,
             o_ref, yb_ref, rs_ref, state_ref):
        @pl.when(pl.program_id(1) == 0)
        def _():
            state_ref[...] = jnp.zeros_like(state_ref)

        cos_t, sin_t = cos_ref[...], sin_ref[...]
        for h in range(HEADS):
            cols = slice(h * DK, (h + 1) * DK)
            qb = _c(_rot(q_ref[:, cols], cos_t, sin_t))
            kb = _c(_rot(k_ref[:, cols], cos_t, sin_t) * (DK ** -0.5))
            v = v_ref[:, cols]
            state = state_ref[h]
            sb = _c(state)
            rs_ref[0, h] = sb
            scores = _dot_nt(qb, kb) * dm_ref[h]
            o = _dot(_c(scores), _c(v)) + _dot(qb, sb) * cd_ref[h]
            state_ref[h] = gam_ref[h] * state + _dot_tn(kb, _c(v * sd_ref[h]))
            o_ref[:, cols] = o
            n, _ = _groupnorm(o)
            gb = gb_ref[:, cols]
            yb_ref[:, cols] = (gb * _sigmoid(gb) * (n * gain_ref[h])).astype(yb_ref.dtype)

    qkv, act, rope, dmat, dvec, hrow, rst = _ret_specs(S, lambda c: c)
    return pl.pallas_call(
        body,
        name="ret_fwd",
        grid=(B, nc),
        in_specs=[qkv(2), qkv(3), qkv(4), qkv(5), rope, rope, dmat, dvec, dvec, hrow, hrow],
        out_specs=[act, act, rst],
        out_shape=[
            jax.ShapeDtypeStruct((T, D_MODEL), F32),
            jax.ShapeDtypeStruct((T, D_MODEL), _MXU),
            jax.ShapeDtypeStruct((B * nc, HEADS, DK, DK), _MXU),
        ],
        scratch_shapes=[pltpu.VMEM((HEADS, DK, DK), F32)],
        compiler_params=_params(("parallel", "arbitrary")),
    )(proj, proj, proj, proj, cos, sin, dmat_t, cd_t, sd_t, gam_t, gain3)


def _ret_bwd(dyb, o_pre, proj, states, tables, gain3, B, S):
    T = B * S
    nc = S // CHUNK
    cos, sin, dmat_t, cd_t, sd_t, gam_t = tables

    def body(dyb_ref, o_ref, q_ref, k_ref, v_ref, gb_ref, rs_ref, cos_ref, sin_ref, dm_ref, cd_ref, sd_ref, gam_ref,
             gain_ref, dr_ref, dgain_ref, dstate_ref):
        @pl.when(pl.program_id(1) == 0)
        def _():
            dstate_ref[...] = jnp.zeros_like(dstate_ref)

        @pl.when((pl.program_id(1) == 0) & (pl.program_id(0) == 0))
        def _():
            dgain_ref[...] = jnp.zeros_like(dgain_ref)

        cos_t, sin_t = cos_ref[...], sin_ref[...]
        for h in range(HEADS):
            cols = slice(h * DK, (h + 1) * DK)
            gain = gain_ref[h]
            n, rs = _groupnorm(o_ref[:, cols])
            gb = gb_ref[:, cols]
            sg = _sigmoid(gb)
            dy = dyb_ref[:, cols]
            part = lambda g: slice(g * D_MODEL + h * DK, g * D_MODEL + (h + 1) * DK)
            dr_ref[:, part(3)] = (dy * (n * gain) * (sg * (1.0 + gb * (1.0 - sg)))).astype(dr_ref.dtype)
            dgn = dy * (gb * sg)
            dgain_ref[h] += jnp.sum(dgn * n, axis=0, keepdims=True)
            dn = dgn * gain
            do = rs * (dn - jnp.mean(dn, axis=-1, keepdims=True) - n * jnp.mean(dn * n, axis=-1, keepdims=True))

            qb = _c(_rot(q_ref[:, cols], cos_t, sin_t))
            kb = _c(_rot(k_ref[:, cols], cos_t, sin_t) * (DK ** -0.5))
            v = v_ref[:, cols]
            vb = _c(v)
            vsb = _c(v * sd_ref[h])
            dob = _c(do)
            docb = _c(do * cd_ref[h])
            dmat = dm_ref[h]
            dstate = dstate_ref[h]
            dsb = _c(dstate)
            pb = _c(_dot_nt(qb, kb) * dmat)
            dsc = _c(_dot_nt(dob, vb) * dmat)
            dq = _dot(dsc, kb) + _dot_nt(docb, rs_ref[0, h])
            dk = _dot_tn(dsc, qb) + _dot_nt(vsb, dsb)
            dv = _dot_tn(pb, dob) + _dot(kb, dsb) * sd_ref[h]
            dstate_ref[h] = gam_ref[h] * dstate + _dot_tn(qb, docb)
            dr_ref[:, part(0)] = _rot_t(dq, cos_t, sin_t).astype(dr_ref.dtype)
            dr_ref[:, part(1)] = (_rot_t(dk, cos_t, sin_t) * (DK ** -0.5)).astype(dr_ref.dtype)
            dr_ref[:, part(2)] = dv.astype(dr_ref.dtype)

    qkv, act, rope, dmat, dvec, hrow, rst = _ret_specs(S, lambda c: nc - 1 - c)
    wide = pl.BlockSpec((CHUNK, 4 * D_MODEL), lambda b, c: (b * nc + nc - 1 - c, 0))
    return pl.pallas_call(
        body,
        name="ret_bwd",
        grid=(B, nc),
        in_specs=[act, act, qkv(2), qkv(3), qkv(4), qkv(5), rst, rope, rope, dmat, dvec, dvec, hrow, hrow],
        out_specs=[wide, hrow],
        out_shape=[jax.ShapeDtypeStruct((T, 4 * D_MODEL), _MXU), jax.ShapeDtypeStruct((HEADS, 1, DK), F32)],
        scratch_shapes=[pltpu.VMEM((HEADS, DK, DK), F32)],
        compiler_params=_params(("arbitrary", "arbitrary")),
    )(dyb, o_pre, proj, proj, proj, proj, states, cos, sin, dmat_t, cd_t, sd_t, gam_t, gain3)


def _mid(ya, yb, proj, x2d, tgt2d, wpa, wpb, wout, g_fin):
    T = x2d.shape[0]
    tm = min(256, T)
    n_steps = T // tm
    rows = D_MODEL // (2 * N_CHIPS)

    def body(ya_ref, yb_ref, ma_ref, mb_ref, x_ref, t_ref, gf_ref, wpa_hbm, wpb_hbm, wout_hbm,
             loss_ref, dx2_ref, dya_ref, dyb_ref, dm_ref, dgf_ref, gw_hbm, w_ref, acc_ref, sem):
        i = pl.program_id(0)

        @pl.when(i == 0)
        def _():
            loads = [pltpu.make_async_copy(src, w_ref.at[k], sem.at[k]) for k, src in enumerate((wpa_hbm, wpb_hbm, wout_hbm))]
            for cp in loads:
                cp.start()
            for cp in loads:
                cp.wait()
            acc_ref[...] = jnp.zeros_like(acc_ref)
            loss_ref[...] = jnp.zeros_like(loss_ref)
            dgf_ref[...] = jnp.zeros_like(dgf_ref)

        ya_t, yb_t = ya_ref[...], yb_ref[...]
        out_a = _dot(ya_t, w_ref[0])
        out_b = _dot(yb_t, w_ref[1])
        sa = _sigmoid(ma_ref[...])
        sb = _sigmoid(mb_ref[...])
        mgb = _c(sa * out_a + sb * out_b)
        x2 = x_ref[...] + _dot(mgb, w_ref[2])
        r2 = lax.rsqrt(jnp.mean(x2 * x2, axis=-1, keepdims=True) + EPS)
        nx = x2 * r2
        gf = gf_ref[...]
        err = nx * gf - t_ref[...]
        loss_ref[...] += 0.5 * jnp.sum(jnp.mean(err * err, axis=-1, keepdims=True), axis=0, keepdims=True)
        dy = err * (1.0 / D_MODEL)
        dgf_ref[...] += jnp.sum(dy * nx, axis=0, keepdims=True)
        dyg = dy * gf
        dx2 = r2 * (dyg - nx * jnp.mean(dyg * nx, axis=-1, keepdims=True))
        dx2_ref[...] = dx2
        dx2b = _c(dx2)
        dmg = _dot_nt(dx2b, w_ref[2])
        acc_ref[2] += _dot_tn(mgb, dx2b)
        dm_ref[:, :D_MODEL] = (dmg * out_a * sa * (1.0 - sa)).astype(dm_ref.dtype)
        dm_ref[:, D_MODEL:] = (dmg * out_b * sb * (1.0 - sb)).astype(dm_ref.dtype)
        dab = _c(dmg * sa)
        dbb = _c(dmg * sb)
        dya_ref[...] = _dot_nt(dab, w_ref[0])
        dyb_ref[...] = _dot_nt(dbb, w_ref[1])
        acc_ref[0] += _dot_tn(ya_t, dab)
        acc_ref[1] += _dot_tn(yb_t, dbb)

        @pl.when(i == n_steps - 1)
        def _():
            copies = [pltpu.make_async_copy(acc_ref.at[k, pl.ds((2 * p + hf) * rows, rows), :], gw_hbm.at[p, hf, k],
                                            sem.at[(k * N_CHIPS + p) * 2 + hf])
                      for k in range(3) for p in range(N_CHIPS) for hf in range(2)]
            for cp in copies:
                cp.start()
            for cp in copies:
                cp.wait()

    tile = lambda j: pl.BlockSpec((tm, D_MODEL), lambda i: (i, j))
    one = pl.BlockSpec((1, D_MODEL), lambda i: (0, 0))
    anyspec = pl.BlockSpec(memory_space=pl.ANY)
    return pl.pallas_call(
        body,
        name="mid",
        grid=(n_steps,),
        in_specs=[tile(0), tile(0), tile(6), tile(7), tile(0), tile(0), one, anyspec, anyspec, anyspec],
        out_specs=[pl.BlockSpec((1, 1), lambda i: (0, 0)), tile(0), tile(0), tile(0),
                   pl.BlockSpec((tm, 2 * D_MODEL), lambda i: (i, 0)), one, anyspec],
        out_shape=[
            jax.ShapeDtypeStruct((1, 1), F32),
            jax.ShapeDtypeStruct((T, D_MODEL), F32),
            jax.ShapeDtypeStruct((T, D_MODEL), F32),
            jax.ShapeDtypeStruct((T, D_MODEL), F32),
            jax.ShapeDtypeStruct((T, 2 * D_MODEL), _MXU),
            jax.ShapeDtypeStruct((1, D_MODEL), F32),
            jax.ShapeDtypeStruct((N_CHIPS, 2, 3, rows, D_MODEL), F32),
        ],
        scratch_shapes=[pltpu.VMEM((3, D_MODEL, D_MODEL), _MXU), pltpu.VMEM((3, D_MODEL, D_MODEL), F32),
                        pltpu.SemaphoreType.DMA((3 * N_CHIPS * 2,))],
        compiler_params=_params(("arbitrary",)),
    )(ya, yb, proj, proj, x2d, tgt2d, g_fin, wpa, wpb, wout)


def _inproj_bwd_dx(dparts, w_all, x2d, dx2, g_in):
    T = x2d.shape[0]
    tm = min(512, T)
    n_d = len(dparts)
    groups = [(a, k) for a, d in enumerate(dparts) for k in range(d.shape[1] // D_MODEL)]

    def body(*refs):
        d_refs = refs[:n_d]
        x_ref, dx2_ref, g_ref, w_hbm, dx_ref, dg_ref, w_ref, sem = refs[n_d:]

        @pl.when(pl.program_id(0) == 0)
        def _():
            cp = pltpu.make_async_copy(w_hbm, w_ref, sem)
            cp.start()
            cp.wait()
            dg_ref[...] = jnp.zeros_like(dg_ref)

        dh = jnp.zeros((tm, D_MODEL), F32)
        for j, (a, k) in enumerate(groups):
            dh = dh + _dot_nt(d_refs[a][:, k * D_MODEL:(k + 1) * D_MODEL],
                              w_ref[j // 2, :, (j % 2) * D_MODEL:(j % 2 + 1) * D_MODEL])
        x = x_ref[...]
        r = lax.rsqrt(jnp.mean(x * x, axis=-1, keepdims=True) + EPS)
        nx = x * r
        dg_ref[...] += jnp.sum(dh * nx, axis=0, keepdims=True)
        dhg = dh * g_ref[...]
        dx_ref[...] = dx2_ref[...] + r * (dhg - nx * jnp.mean(dhg * nx, axis=-1, keepdims=True))

    tile = pl.BlockSpec((tm, D_MODEL), lambda i: (i, 0))
    one = pl.BlockSpec((1, D_MODEL), lambda i: (0, 0))
    return pl.pallas_call(
        body,
        name="inproj_bwd_dx",
        grid=(T // tm,),
        in_specs=[pl.BlockSpec((tm, d.shape[1]), lambda i: (i, 0)) for d in dparts]
        + [tile, tile, one, pl.BlockSpec(memory_space=pl.ANY)],
        out_specs=[tile, one],
        out_shape=[jax.ShapeDtypeStruct((T, D_MODEL), F32), jax.ShapeDtypeStruct((1, D_MODEL), F32)],
        scratch_shapes=[pltpu.VMEM(w_all.shape, w_all.dtype), pltpu.SemaphoreType.DMA],
        compiler_params=_params(("arbitrary",)),
    )(*dparts, x2d, dx2, g_in, w_all)


def _inproj_bwd_dw(ht, dparts, first_chip, buf, name):
    T = ht.shape[1]
    tn = 512
    half = D_MODEL // 2
    per_chip = 2 * D_MODEL // tn
    n_d = len(dparts)
    tiles = [(a, t) for a, d in enumerate(dparts) for t in range(d.shape[1] // tn)]
    offs = [sum(d.shape[1] // tn for d in dparts[:a]) for a in range(n_d)]

    def body(*refs):
        ht_ref = refs[0]
        d_refs = refs[1:1 + n_d]
        out_ref = refs[-1]
        t = pl.program_id(0)
        for a in range(n_d):
            lo, hi = offs[a], offs[a] + dparts[a].shape[1] // tn

            @pl.when((t >= lo) & (t < hi))
            def _(a=a):
                g = _dot(ht_ref[...], d_refs[a][...])
                out_ref[0, 0] = g[:half]
                out_ref[0, 1] = g[half:]

    def dspec(a):
        n_a = dparts[a].shape[1] // tn
        return pl.BlockSpec((T, tn), lambda t: (0, jnp.clip(t - offs[a], 0, n_a - 1)))

    in_specs = [pl.BlockSpec((D_MODEL, T), lambda t: (0, 0))] + [dspec(a) for a in range(n_d)]
    args = [ht, *dparts]
    aliases = {}
    if buf is not None:
        in_specs.append(pl.BlockSpec(memory_space=pl.ANY))
        args.append(buf)
        aliases = {len(args) - 1: 0}
    return pl.pallas_call(
        body,
        name=name,
        grid=(len(tiles),),
        in_specs=in_specs,
        out_specs=pl.BlockSpec((1, 2, half, tn), lambda t: (first_chip + t // per_chip, 0, 0, t % per_chip)),
        out_shape=jax.ShapeDtypeStruct((N_CHIPS, 2, half, 2 * D_MODEL), F32),
        input_output_aliases=aliases,
        compiler_params=_params(("parallel",)),
    )(*args)


def _coords():
    return lax.axis_index("x"), lax.axis_index("y"), lax.axis_index("c")


def _other_chips(x, y):
    return [(1 - x, y), (x, 1 - y), (1 - x, 1 - y)]


def _all_gather8(xs, name):
    m_per, n = xs.shape

    def body(x_ref, out_ref, send_sems, recv_sems, local_sem):
        x, y, c = _coords()
        me, sibling = (x, y, c), (x, y, 1 - c)
        chips = _other_chips(x, y)

        def rows(px, py, pc):
            return out_ref.at[pl.ds((4 * px + 2 * py + pc) * m_per, m_per), :]

        def copy(k, block, to, src=None):
            return pltpu.make_async_remote_copy(
                src_ref=rows(*block) if src is None else src, dst_ref=rows(*block),
                send_sem=send_sems.at[k], recv_sem=recv_sems.at[k], device_id=to, device_id_type=MESH)

        mine = pltpu.make_async_copy(x_ref, rows(*me), local_sem)
        mine.start()
        first = [copy(0, me, sibling, src=x_ref)]
        first += [copy(1 + j, me, (*chip, c), src=x_ref) for j, chip in enumerate(chips)]
        for cp in first:
            cp.start()
        passed = [copy(4 + j, (*chip, c), sibling) for j, chip in enumerate(chips)]
        for j, chip in enumerate(chips):
            copy(1 + j, (*chip, c), me).wait_recv()
            passed[j].start()
        copy(0, sibling, me).wait_recv()
        for j, chip in enumerate(chips):
            copy(4 + j, (*chip, 1 - c), me).wait_recv()
        for cp in first + passed:
            cp.wait_send()
        mine.wait()

    return pl.pallas_call(
        body,
        name=name,
        out_shape=jax.ShapeDtypeStruct((8 * m_per, n), xs.dtype),
        in_specs=[pl.BlockSpec(memory_space=pltpu.VMEM)],
        out_specs=pl.BlockSpec(memory_space=pltpu.VMEM),
        scratch_shapes=[pltpu.SemaphoreType.DMA((7,)), pltpu.SemaphoreType.DMA((7,)), pltpu.SemaphoreType.DMA],
        compiler_params=pltpu.CompilerParams(vmem_limit_bytes=VMEM_LIMIT),
    )(xs)


def _chunks(rows, n):
    size = rows // n
    return [pl.ds(q * size, size) for q in range(n)]


def _cast_into_slot(ws, name):
    n = len(ws)
    nt = 2

    def body(s_ref, *refs):
        for a in range(n):
            refs[n + a][0] = refs[a][...].astype(refs[n + a].dtype)

    xi, yi, _ = _coords()
    return pl.pallas_call(
        body,
        name=name,
        grid_spec=pltpu.PrefetchScalarGridSpec(
            num_scalar_prefetch=1,
            grid=(2, nt),
            in_specs=[pl.BlockSpec((1, w.shape[1] // nt, w.shape[2]), lambda hf, i, s: (hf, i, 0)) for w in ws],
            out_specs=[pl.BlockSpec((1, 1, w.shape[1] // nt, w.shape[2]), lambda hf, i, s: (s[0], hf, i, 0)) for w in ws],
        ),
        out_shape=[jax.ShapeDtypeStruct((N_CHIPS,) + w.shape, _MXU) for w in ws],
        compiler_params=_params(("parallel", "parallel")),
    )((2 * xi + yi).reshape(1).astype(jnp.int32), *ws)


def _gather_chips(bufs, n_chunks, name):
    n = len(bufs)
    pieces = [(a, rows) for a in range(n) for rows in _chunks(bufs[a].shape[2], n_chunks[a])]
    n_p = len(pieces)

    def body(*refs):
        outs = refs[n:2 * n]
        send_sems, recv_sems, fsend_sems, frecv_sems = refs[2 * n:]
        x, y, c = _coords()
        me = 2 * x + y
        chips = _other_chips(x, y)

        def send(k, i, slot, chip):
            a, rows = pieces[i]
            return pltpu.make_async_remote_copy(
                src_ref=outs[a].at[slot, c, rows], dst_ref=outs[a].at[slot, c, rows], send_sem=send_sems.at[k * n_p + i],
                recv_sem=recv_sems.at[k * n_p + i], device_id=(*chip, c), device_id_type=MESH)

        def forward(k, i, slot, half):
            a, rows = pieces[i]
            return pltpu.make_async_remote_copy(
                src_ref=outs[a].at[slot, half, rows], dst_ref=outs[a].at[slot, half, rows],
                send_sem=fsend_sems.at[k * n_p + i], recv_sem=frecv_sems.at[k * n_p + i],
                device_id=(x, y, 1 - c), device_id_type=MESH)

        sends = [send(k, i, me, chip) for i in range(n_p) for k, chip in enumerate(chips)]
        for cp in sends:
            cp.start()
        forwards = []
        for i in range(n_p):
            for k, (px, py) in enumerate(chips):
                send(k, i, 2 * px + py, (px, py)).wait_recv()
                fw = forward(k, i, 2 * px + py, c)
                fw.start()
                forwards.append(fw)
        for i in range(n_p):
            for k, (px, py) in enumerate(chips):
                forward(k, i, 2 * px + py, 1 - c).wait_recv()
        for cp in sends + forwards:
            cp.wait_send()

    anyspec = pl.BlockSpec(memory_space=pl.ANY)
    sems = pltpu.SemaphoreType.DMA((3 * n_p,))
    return pl.pallas_call(
        body,
        name=name,
        in_specs=[anyspec] * n,
        out_specs=[anyspec] * n,
        out_shape=[jax.ShapeDtypeStruct(b.shape, b.dtype) for b in bufs],
        input_output_aliases={a: a for a in range(n)},
        scratch_shapes=[sems, sems, sems, sems],
    )(*bufs)


def _swap_halves(arrs, name):
    n = len(arrs)

    def body(*refs):
        ins, outs = refs[:n], refs[n:2 * n]
        send_sems, recv_sems = refs[2 * n:]
        x, y, c = _coords()
        copies = [pltpu.make_async_remote_copy(
            src_ref=ins[a].at[p, 1 - c], dst_ref=outs[a].at[p], send_sem=send_sems.at[a * N_CHIPS + p],
            recv_sem=recv_sems.at[a * N_CHIPS + p], device_id=(x, y, 1 - c), device_id_type=MESH)
            for a in range(n) for p in range(N_CHIPS)]
        for cp in copies:
            cp.start()
        for cp in copies:
            cp.wait()

    anyspec = pl.BlockSpec(memory_space=pl.ANY)
    return pl.pallas_call(
        body,
        name=name,
        in_specs=[anyspec] * n,
        out_specs=[anyspec] * n,
        out_shape=[jax.ShapeDtypeStruct((N_CHIPS,) + a.shape[2:], a.dtype) for a in arrs],
        scratch_shapes=[pltpu.SemaphoreType.DMA((N_CHIPS * n,)), pltpu.SemaphoreType.DMA((N_CHIPS * n,))],
    )(*arrs)


def _scatter_chips(arrs, n_chunks, name):
    n = len(arrs)
    pieces = [(a, rows) for a in range(n) for rows in _chunks(arrs[a].shape[1], n_chunks[a])]
    n_p = len(pieces)

    def body(*refs):
        ins, outs = refs[:n], refs[n:2 * n]
        send_sems, recv_sems = refs[2 * n:]
        x, y, c = _coords()
        chips = _other_chips(x, y)

        def copy(k, i, src_slot, chip):
            a, rows = pieces[i]
            return pltpu.make_async_remote_copy(
                src_ref=ins[a].at[src_slot, rows], dst_ref=outs[a].at[k, rows],
                send_sem=send_sems.at[k * n_p + i], recv_sem=recv_sems.at[k * n_p + i],
                device_id=(*chip, c), device_id_type=MESH)

        sends = [copy(k, i, 2 * px + py, (px, py)) for i in range(n_p) for k, (px, py) in enumerate(chips)]
        for cp in sends:
            cp.start()
        for cp in sends:
            cp.wait()

    anyspec = pl.BlockSpec(memory_space=pl.ANY)
    return pl.pallas_call(
        body,
        name=name,
        in_specs=[anyspec] * n,
        out_specs=[anyspec] * n,
        out_shape=[jax.ShapeDtypeStruct((3,) + a.shape[1:], a.dtype) for a in arrs],
        scratch_shapes=[pltpu.SemaphoreType.DMA((3 * n_p,)), pltpu.SemaphoreType.DMA((3 * n_p,))],
    )(*arrs)


def _join_halves(bufs, n_chunks, name):
    n = len(bufs)
    pieces = [(a, rows) for a in range(n) for rows in _chunks(bufs[a].shape[1], n_chunks[a])]
    n_p = len(pieces)

    def body(*refs):
        outs = refs[n:2 * n]
        send_sems, recv_sems = refs[2 * n:]
        x, y, c = _coords()

        def copy(i, half):
            a, rows = pieces[i]
            return pltpu.make_async_remote_copy(
                src_ref=outs[a].at[half, rows], dst_ref=outs[a].at[half, rows], send_sem=send_sems.at[i],
                recv_sem=recv_sems.at[i], device_id=(x, y, 1 - c), device_id_type=MESH)

        sends = [copy(i, c) for i in range(n_p)]
        for cp in sends:
            cp.start()
        for i in range(n_p):
            copy(i, 1 - c).wait_recv()
        for cp in sends:
            cp.wait_send()

    anyspec = pl.BlockSpec(memory_space=pl.ANY)
    sems = pltpu.SemaphoreType.DMA((n_p,))
    return pl.pallas_call(
        body,
        name=name,
        in_specs=[anyspec] * n,
        out_specs=[anyspec] * n,
        out_shape=[jax.ShapeDtypeStruct(b.shape, b.dtype) for b in bufs],
        input_output_aliases={a: a for a in range(n)},
        scratch_shapes=[sems, sems],
    )(*bufs)


def _row_tile(rows, cap):
    t = cap
    while rows % t:
        t //= 2
    return t


def _add_my_half(g, r, name):
    _, _, R, C = g.shape
    tr = _row_tile(R, 256)

    def body(c_ref, g_ref, r_ref, o_ref):
        o_ref[...] = (g_ref[0] + r_ref[...]).astype(o_ref.dtype)

    return pl.pallas_call(
        body,
        name=name,
        grid_spec=pltpu.PrefetchScalarGridSpec(
            num_scalar_prefetch=1,
            grid=(N_CHIPS, R // tr),
            in_specs=[pl.BlockSpec((1, 1, tr, C), lambda p, i, c_ref: (p, c_ref[0], i, 0)),
                      pl.BlockSpec((1, tr, C), lambda p, i, c_ref: (p, i, 0))],
            out_specs=pl.BlockSpec((1, tr, C), lambda p, i, c_ref: (p, i, 0)),
        ),
        out_shape=jax.ShapeDtypeStruct(r.shape, jnp.bfloat16),
        compiler_params=_params(("parallel", "parallel")),
    )(lax.axis_index("c").reshape(1).astype(jnp.int32), g, r)


def _sum_slabs(own, got, name):
    _, R, C = own.shape
    tr = _row_tile(R, 256)

    def body(s_ref, own_ref, got_ref, o_ref):
        o_ref[0] = ((own_ref[0].astype(F32) + got_ref[0].astype(F32)) + got_ref[1].astype(F32)) + got_ref[2].astype(F32)

    xi, yi, ci = _coords()
    return pl.pallas_call(
        body,
        name=name,
        grid_spec=pltpu.PrefetchScalarGridSpec(
            num_scalar_prefetch=1,
            grid=(R // tr,),
            in_specs=[pl.BlockSpec((1, tr, C), lambda i, s: (s[0], i, 0)),
                      pl.BlockSpec((3, tr, C), lambda i, s: (0, i, 0))],
            out_specs=pl.BlockSpec((1, tr, C), lambda i, s: (s[1], i, 0)),
        ),
        out_shape=jax.ShapeDtypeStruct((2, R, C), F32),
        compiler_params=_params(("parallel",)),
    )(jnp.stack([2 * xi + yi, ci]).astype(jnp.int32), own, got)


def _sum_rows8(g, m_per, name):
    n = g.shape[1]

    def body(g_ref, o_ref):
        acc = g_ref[0:m_per, :]
        for k in range(1, 8):
            acc = acc + g_ref[k * m_per:(k + 1) * m_per, :]
        o_ref[...] = acc

    return pl.pallas_call(
        body,
        name=name,
        out_shape=jax.ShapeDtypeStruct((m_per, n), F32),
        compiler_params=_params(),
    )(g)


def _adamw_math(w, g, m, v):
    m = ADAM_B1 * m + (1.0 - ADAM_B1) * g
    v = ADAM_B2 * v + (1.0 - ADAM_B2) * (g * g)
    m_hat = m / (1.0 - ADAM_B1 ** ADAM_STEP)
    v_hat = v / (1.0 - ADAM_B2 ** ADAM_STEP)
    delta = -ADAM_LR * (m_hat / (jnp.sqrt(v_hat) + ADAM_EPS) + ADAM_WD * w)
    return delta, m, v


def _adamw_big(w, g, m, v, name):
    R, C = w.shape
    tr = min(128, R)

    def body(w_ref, g_ref, m_ref, v_ref, d_out, m_out, v_out):
        d, mn, vn = _adamw_math(w_ref[...], g_ref[...], m_ref[...], v_ref[...])
        d_out[...] = d
        m_out[...] = mn
        v_out[...] = vn

    spec = pl.BlockSpec((tr, C), lambda i: (i, 0))
    return pl.pallas_call(
        body,
        name=name,
        grid=(R // tr,),
        in_specs=[spec] * 4,
        out_specs=[spec] * 3,
        out_shape=[jax.ShapeDtypeStruct((R, C), F32)] * 3,
        compiler_params=_params(("parallel",)),
    )(w, g, m, v)


def _adamw_small(ws, gs, ms, vs, name):
    n = len(ws)

    def body(*refs):
        for a in range(n):
            d, mn, vn = _adamw_math(refs[a][...], refs[n + a][...], refs[2 * n + a][...], refs[3 * n + a][...])
            refs[4 * n + a][...] = d
            refs[5 * n + a][...] = mn
            refs[6 * n + a][...] = vn

    shapes = [jax.ShapeDtypeStruct(w.shape, F32) for w in ws]
    outs = pl.pallas_call(
        body,
        name=name,
        out_shape=shapes * 3,
        compiler_params=_params(),
    )(*ws, *gs, *ms, *vs)
    return outs[:n], outs[n:2 * n], outs[2 * n:]


def _to_blockdiag(w):
    per = CW // LRU_BW
    w4 = w.reshape(N_CT, per, LRU_BW, LRU_BW)
    eye = jnp.eye(per, dtype=w.dtype)
    return (w4[:, :, :, None, :] * eye[None, :, None, :, None]).reshape(N_CT, CW, CW)


def _from_blockdiag(g):
    per = CW // LRU_BW
    g5 = g.reshape(N_CT, per, LRU_BW, per, LRU_BW)
    return jnp.stack([g5[:, b, :, b, :] for b in range(per)], axis=1).reshape(LRU_BLOCKS, LRU_BW, LRU_BW)


def _local_grads(x2d, tgt2d, B, S, g_in, w_all, conv_w, conv_b, gate_x_w, gate_x_b, gate_a_w, gate_a_b, lam, gain,
                 wpa, wpb, wout, g_fin):
    wx_bd = _c(_to_blockdiag(gate_x_w))
    wa_bd = _c(_to_blockdiag(gate_a_w))
    tables = _retention_tables(S)
    gain3 = gain.reshape(HEADS, 1, DK)

    proj, ht = _inproj_fwd(x2d, g_in, w_all)
    hlru, ya = _lru_fwd(proj, conv_w, conv_b, wx_bd, wa_bd, gate_x_b, gate_a_b, lam, B, S)
    o_pre, yb, states = _ret_fwd(proj, tables, gain3, B, S)
    loss, dx2, dya, dyb, dm, dgf, gw_proj = _mid(ya, yb, proj, x2d, tgt2d, wpa, wpb, wout, g_fin)
    gw_in = _inproj_bwd_dw(ht, [dm], 3, None, "inproj_bwd_dw_m")
    dxa, dga, dcw, dcb, dwx_bd, dwa_bd, dbx, dba, dlam = _lru_bwd(
        dya, proj, hlru, conv_w, conv_b, wx_bd, wa_bd, gate_x_b, gate_a_b, lam, B, S)
    gw_in = _inproj_bwd_dw(ht, [dxa, dga], 0, gw_in, "inproj_bwd_dw_a")
    dr, dgain = _ret_bwd(dyb, o_pre, proj, states, tables, gain3, B, S)
    gw_in = _inproj_bwd_dw(ht, [dr], 1, gw_in, "inproj_bwd_dw_r")
    grad_x, dgin = _inproj_bwd_dx([dxa, dga, dr, dm], w_all, x2d, dx2, g_in)
    small = dict(norm_in=dgin, conv_w=dcw, conv_b=dcb, gate_x_w=_from_blockdiag(dwx_bd), gate_x_b=dbx,
                 gate_a_w=_from_blockdiag(dwa_bd), gate_a_b=dba, lru_lambda=dlam, gn_gain=dgain.reshape(HEADS, DK),
                 norm_final=dgf)
    return loss[0, 0], grad_x, gw_in, gw_proj, small


_SMALL = ("gate_x_w", "gate_a_w", "norm_in", "conv_w", "conv_b", "gate_x_b", "gate_a_b", "lru_lambda", "gn_gain",
          "norm_final")
_SMALL_SHAPES = dict(gate_x_w=(LRU_BLOCKS, LRU_BW, LRU_BW), gate_a_w=(LRU_BLOCKS, LRU_BW, LRU_BW),
                     norm_in=(1, D_MODEL), conv_w=(CONV, D_MODEL), conv_b=(1, D_MODEL), gate_x_b=(1, D_MODEL),
                     gate_a_b=(1, D_MODEL), lru_lambda=(1, D_MODEL), gn_gain=(HEADS, DK), norm_final=(1, D_MODEL))


def _pack_small(small):
    return jnp.concatenate([small[k].reshape(-1, 128) for k in _SMALL], axis=0)


def _unpack_small(packed):
    out, r = {}, 0
    for k in _SMALL:
        shape = _SMALL_SHAPES[k]
        rows = 1
        for s in shape:
            rows *= s
        rows //= 128
        out[k] = packed[r:r + rows].reshape(shape)
        r += rows
    return out


def kernel(x, norm_in, w_in, conv_w, conv_b, gate_x_w, gate_x_b, gate_a_w, gate_a_b, lru_lambda, gn_gain, w_proj_a, w_proj_b, w_out, norm_final, loss_target, m_norm_in, m_w_in, m_conv_w, m_conv_b, m_gate_x_w, m_gate_x_b, m_gate_a_w, m_gate_a_b, m_lru_lambda, m_gn_gain, m_w_proj_a, m_w_proj_b, m_w_out, m_norm_final, v_norm_in, v_w_in, v_conv_w, v_conv_b, v_gate_x_w, v_gate_x_b, v_gate_a_w, v_gate_a_b, v_lru_lambda, v_gn_gain, v_w_proj_a, v_w_proj_b, v_w_out, v_norm_final):
    B, S, _ = x.shape
    T = B * S
    xi, yi, ci = _coords()
    chip = 2 * xi + yi

    cshard = D_MODEL // N_CHIPS
    mine = _cast_into_slot([w_in[0].reshape(2, D_MODEL // 2, 2 * D_MODEL)]
                           + [w[0].reshape(2, cshard // 2, D_MODEL) for w in (w_proj_a, w_proj_b, w_out)],
                           "cast_weights")
    big = _gather_chips(mine, [4, 1, 1, 1], "gather_weights")
    w_all = big[0].reshape(N_CHIPS, D_MODEL, 2 * D_MODEL)
    wpa, wpb, wout = (b.reshape(D_MODEL, D_MODEL) for b in big[1:])
    gshard = DK // N_CHIPS
    tiny = jnp.concatenate([conv_w[0], jnp.zeros((4, cshard), F32), jnp.pad(gn_gain[0], ((0, 4), (0, cshard - gshard)))],
                           axis=0)
    tiny_all = _all_gather8(tiny, "gather_small_weights").reshape(N_CHIPS, 2, 16, cshard)[:, 0]
    conv_w_full = jnp.transpose(tiny_all[:, 0:CONV, :], (1, 0, 2)).reshape(CONV, D_MODEL)
    gain_full = jnp.transpose(tiny_all[:, 8:8 + HEADS, :gshard], (1, 0, 2)).reshape(HEADS, DK)

    loss, grad_x, gw_in, gw_proj, small = _local_grads(
        x.reshape(T, D_MODEL), loss_target.reshape(T, D_MODEL), B, S, norm_in, w_all, conv_w_full, conv_b,
        gate_x_w[0], gate_x_b, gate_a_w[0], gate_a_b, lru_lambda, gain_full, wpa, wpb, wout,
        norm_final.reshape(1, D_MODEL))
    loss = lax.psum(loss, ("x", "y", "c"))

    packed = _pack_small(small)
    m_per = packed.shape[0]
    gsm = _unpack_small(_sum_rows8(_all_gather8(packed, "gather_small_grads"), m_per, "sum_small_grads"))

    rows_p = 3 * D_MODEL // (2 * N_CHIPS)
    gw_proj = gw_proj.reshape(N_CHIPS, 2, rows_p, D_MODEL)
    other = _swap_halves([gw_in, gw_proj], "swap_halves")
    chip_in = _add_my_half(gw_in, other[0], "chip_sum_w_in")
    chip_pr = _add_my_half(gw_proj, other[1], "chip_sum_w_proj")
    got = _scatter_chips([chip_in, chip_pr], [4, 2], "scatter_chips")
    half_in = _sum_slabs(chip_in, got[0], "sum_w_in")
    half_pr = _sum_slabs(chip_pr, got[1], "sum_w_proj")
    g_in_full, g_pr_full = _join_halves([half_in, half_pr], [8, 4], "join_halves")
    g_w_in = g_in_full.reshape(D_MODEL, 2 * D_MODEL)
    g_pr = g_pr_full.reshape(2, 3, D_MODEL // (2 * N_CHIPS), D_MODEL)
    g_wpa, g_wpb, g_wout = (g_pr[:, k].reshape(cshard, D_MODEL) for k in range(3))

    grads = dict(gsm)
    grads["conv_w"] = lax.dynamic_slice_in_dim(gsm["conv_w"], chip * cshard, cshard, axis=1)
    grads["gn_gain"] = lax.dynamic_slice_in_dim(gsm["gn_gain"], chip * gshard, gshard, axis=1)
    grads.update(w_in=g_w_in, w_proj_a=g_wpa, w_proj_b=g_wpb, w_out=g_wout)

    weights = dict(norm_in=norm_in, w_in=w_in, conv_w=conv_w, conv_b=conv_b, gate_x_w=gate_x_w, gate_x_b=gate_x_b,
                   gate_a_w=gate_a_w, gate_a_b=gate_a_b, lru_lambda=lru_lambda, gn_gain=gn_gain, w_proj_a=w_proj_a,
                   w_proj_b=w_proj_b, w_out=w_out, norm_final=norm_final)
    ms = dict(norm_in=m_norm_in, w_in=m_w_in, conv_w=m_conv_w, conv_b=m_conv_b, gate_x_w=m_gate_x_w,
              gate_x_b=m_gate_x_b, gate_a_w=m_gate_a_w, gate_a_b=m_gate_a_b, lru_lambda=m_lru_lambda, gn_gain=m_gn_gain,
              w_proj_a=m_w_proj_a, w_proj_b=m_w_proj_b, w_out=m_w_out, norm_final=m_norm_final)
    vs = dict(norm_in=v_norm_in, w_in=v_w_in, conv_w=v_conv_w, conv_b=v_conv_b, gate_x_w=v_gate_x_w,
              gate_x_b=v_gate_x_b, gate_a_w=v_gate_a_w, gate_a_b=v_gate_a_b, lru_lambda=v_lru_lambda, gn_gain=v_gn_gain,
              w_proj_a=v_w_proj_a, w_proj_b=v_w_proj_b, w_out=v_w_out, norm_final=v_norm_final)
    names = list(weights)
    grads = {k: grads[k].reshape(weights[k].shape) for k in names}

    delta, new_m, new_v = {}, {}, {}
    for k in ("w_in", "w_proj_a", "w_proj_b", "w_out"):
        shp = weights[k].shape
        two = lambda a: a.reshape(shp[1], shp[2])
        d, mn, vn = _adamw_big(two(weights[k]), two(grads[k]), two(ms[k]), two(vs[k]), "adamw_" + k)
        delta[k], new_m[k], new_v[k] = d.reshape(shp), mn.reshape(shp), vn.reshape(shp)
    smalls = [k for k in names if k not in delta]

    def view(a):
        return a.reshape(1, -1) if a.ndim == 1 else (a.reshape(a.shape[1:]) if a.ndim > 2 else a)

    ds, mns, vns = _adamw_small([view(weights[k]) for k in smalls], [view(grads[k]) for k in smalls],
                                [view(ms[k]) for k in smalls], [view(vs[k]) for k in smalls], "adamw_small")
    for k, d, mn, vn in zip(smalls, ds, mns, vns):
        shp = weights[k].shape
        delta[k], new_m[k], new_v[k] = d.reshape(shp), mn.reshape(shp), vn.reshape(shp)

    return (loss, grad_x.reshape(B, S, D_MODEL), *[grads[k] for k in names], *[delta[k] for k in names],
            *[new_m[k] for k in names], *[new_v[k] for k in names])
```

```python
import functools

import jax
import jax.numpy as jnp
from jax import lax
from jax.experimental import pallas as pl
from jax.experimental.pallas import tpu as pltpu

F32 = jnp.float32
_MXU = jnp.bfloat16

D_MODEL = 1024
N_GROUPS = 8
HEADS = 4
DK = 256
CHUNK = 128
CONV = 4
LRU_BLOCKS = 16
LRU_BW = 64
LRU_C = 8.0
ROPE_THETA = 10000.0
EPS = 1e-6
CW = 256
N_CT = D_MODEL // CW
N_CHIPS = 4
MESH = pl.DeviceIdType.MESH

ADAM_LR = 0.001
ADAM_B1 = 0.9
ADAM_B2 = 0.999
ADAM_EPS = 1e-08
ADAM_WD = 0.01
ADAM_STEP = 10

VMEM_LIMIT = 56 * 1024 * 1024


def _c(v):
    return v.astype(_MXU)


def _dot(a, b):
    return lax.dot_general(a, b, (((1,), (0,)), ((), ())), preferred_element_type=F32)


def _dot_nt(a, b):
    return lax.dot_general(a, b, (((1,), (1,)), ((), ())), preferred_element_type=F32)


def _dot_tn(a, b):
    return lax.dot_general(a, b, (((0,), (0,)), ((), ())), preferred_element_type=F32)


def _sigmoid(z):
    return 1.0 / (1.0 + jnp.exp(-z))


def _params(sem=None):
    if sem is None:
        return pltpu.CompilerParams(vmem_limit_bytes=VMEM_LIMIT)
    return pltpu.CompilerParams(vmem_limit_bytes=VMEM_LIMIT, dimension_semantics=sem)


def _inproj_fwd(x2d, g_in, w_all, deps=()):
    T = x2d.shape[0]
    tm = min(512, T)
    n_i = T // tm

    def body(*refs):
        x_ref, g_ref, w_ref = refs[:3]
        proj_ref, ht_ref, h_all = refs[-3:]
        i = pl.program_id(1)
        rows = pl.ds(pl.multiple_of(i * tm, tm), tm)

        @pl.when(pl.program_id(0) == 0)
        def _():
            x = x_ref[...]
            r = lax.rsqrt(jnp.mean(x * x, axis=-1, keepdims=True) + EPS)
            h = x * r * g_ref[...]
            h_all[rows, :] = h.astype(h_all.dtype)
            ht_ref[...] = h.T.astype(ht_ref.dtype)

        proj_ref[...] = _dot(h_all[rows, :], w_ref[0])

    first = lambda j, i: jnp.where(j == 0, i, n_i - 1)
    return pl.pallas_call(
        body,
        name="inproj_fwd",
        grid=(N_GROUPS, n_i),
        in_specs=[
            pl.BlockSpec((tm, D_MODEL), lambda j, i: (first(j, i), 0)),
            pl.BlockSpec((1, D_MODEL), lambda j, i: (0, 0)),
            pl.BlockSpec((1, D_MODEL, D_MODEL), lambda j, i: (j // 2, 0, j % 2)),
        ] + [pl.BlockSpec(memory_space=pl.ANY)] * len(deps),
        out_specs=[
            pl.BlockSpec((tm, D_MODEL), lambda j, i: (i, j)),
            pl.BlockSpec((D_MODEL, tm), lambda j, i: (0, first(j, i))),
        ],
        out_shape=[
            jax.ShapeDtypeStruct((T, N_GROUPS * D_MODEL), F32),
            jax.ShapeDtypeStruct((D_MODEL, T), _MXU),
        ],
        scratch_shapes=[pltpu.VMEM((T, D_MODEL), _MXU)],
        compiler_params=_params(("arbitrary", "arbitrary")),
    )(x2d, g_in, w_all, *deps)


def _scan_fwd(a, u):
    n = a.shape[0]
    row = lax.broadcasted_iota(jnp.int32, a.shape, 0)
    s = 1
    while s < n:
        m = row >= s
        u = u + a * jnp.where(m, pltpu.roll(u, s, 0), 0.0)
        a = a * jnp.where(m, pltpu.roll(a, s, 0), 1.0)
        s *= 2
    return a, u


def _scan_bwd(b, g):
    n = b.shape[0]
    row = lax.broadcasted_iota(jnp.int32, b.shape, 0)
    s = 1
    while s < n:
        m = row < n - s
        g = g + b * jnp.where(m, pltpu.roll(g, n - s, 0), 0.0)
        b = b * jnp.where(m, pltpu.roll(b, n - s, 0), 1.0)
        s *= 2
    return b, g


def _softplus_neg(lam):
    z = -lam
    return jnp.maximum(z, 0.0) + jnp.log1p(jnp.exp(-jnp.abs(z)))


def _lru_gates(xc, wx_ref, wa_ref, bx_ref, ba_ref, lam_ref):
    xcb = _c(xc)
    i_t = _sigmoid(_dot(xcb, wx_ref[0]) + bx_ref[...])
    r_t = _sigmoid(_dot(xcb, wa_ref[0]) + ba_ref[...])
    sp = _softplus_neg(lam_ref[...])
    log_a = (-LRU_C) * r_t * sp
    a = jnp.exp(log_a)
    mult = jnp.sqrt(1.0 - a * a)
    return xcb, i_t, r_t, sp, a, mult


def _conv_from_ext(ext_ref, xa, cw_ref, cb_ref, tc):
    return (cb_ref[...] + cw_ref[3:4, :] * xa + cw_ref[2:3, :] * ext_ref[7:7 + tc, :]
            + cw_ref[1:2, :] * ext_ref[6:6 + tc, :] + cw_ref[0:1, :] * ext_ref[5:5 + tc, :])


def _lru_fwd(proj, conv_w, conv_b, wx_bd, wa_bd, bx, ba, lam, B, S):
    T = B * S
    tc = min(256, S)
    nt = S // tc
    h8 = tc // 8

    def body(xa_ref, halo_ref, ga_ref, cw_ref, cb_ref, wx_ref, wa_ref, bx_ref, ba_ref, lam_ref,
             h_ref, ya_ref, ext_ref, carry_ref):
        t = pl.program_id(2)

        @pl.when(t == 0)
        def _():
            carry_ref[...] = jnp.zeros_like(carry_ref)

        xa = xa_ref[...]
        ext_ref[0:8, :] = jnp.where(t == 0, 0.0, halo_ref[...])
        ext_ref[8:8 + tc, :] = xa
        xc = _conv_from_ext(ext_ref, xa, cw_ref, cb_ref, tc)
        _, i_t, _, _, a, mult = _lru_gates(xc, wx_ref, wa_ref, bx_ref, ba_ref, lam_ref)
        u = mult * (i_t * xc)
        acum, hloc = _scan_fwd(a, u)
        h = hloc + acum * carry_ref[7:8, :]
        h_ref[...] = h
        carry_ref[...] = h[tc - 8:tc, :]
        ga = ga_ref[...]
        ya_ref[...] = (ga * _sigmoid(ga) * h).astype(ya_ref.dtype)

    row = lambda b, t: b * nt + t
    vec = pl.BlockSpec((1, CW), lambda b, c, t: (0, c))
    mat = pl.BlockSpec((1, CW, CW), lambda b, c, t: (c, 0, 0))
    return pl.pallas_call(
        body,
        name="lru_fwd",
        grid=(B, N_CT, nt),
        in_specs=[
            pl.BlockSpec((tc, CW), lambda b, c, t: (row(b, t), c)),
            pl.BlockSpec((8, CW), lambda b, c, t: (jnp.maximum(row(b, t) * h8 - 1, 0), c)),
            pl.BlockSpec((tc, CW), lambda b, c, t: (row(b, t), N_CT + c)),
            pl.BlockSpec((CONV, CW), lambda b, c, t: (0, c)),
            vec, mat, mat, vec, vec, vec,
        ],
        out_specs=[
            pl.BlockSpec((tc, CW), lambda b, c, t: (row(b, t), c)),
            pl.BlockSpec((tc, CW), lambda b, c, t: (row(b, t), c)),
        ],
        out_shape=[
            jax.ShapeDtypeStruct((T, D_MODEL), F32),
            jax.ShapeDtypeStruct((T, D_MODEL), _MXU),
        ],
        scratch_shapes=[pltpu.VMEM((tc + 8, CW), F32), pltpu.VMEM((8, CW), F32)],
        compiler_params=_params(("parallel", "parallel", "arbitrary")),
    )(proj, proj, proj, conv_w, conv_b, wx_bd, wa_bd, bx, ba, lam)


def _lru_bwd(dya, proj, hlru, conv_w, conv_b, wx_bd, wa_bd, bx, ba, lam, B, S):
    T = B * S
    tc = min(256, S)
    nt = S // tc
    h8 = tc // 8

    def body(dya_ref, xa_ref, xhalo_ref, ga_ref, h_ref, hhalo_ref, cw_ref, cb_ref, wx_ref, wa_ref, bx_ref, ba_ref,
             lam_ref, dxa_ref, dga_ref, dcw_ref, dcb_ref, dwx_ref, dwa_ref, dbx_ref, dba_ref, dlam_ref,
             ext_ref, ext2_ref, carry_ref, dhalo_ref):
        b = pl.program_id(1)
        t = pl.program_id(2)
        tt = nt - 1 - t

        @pl.when(t == 0)
        def _():
            carry_ref[...] = jnp.zeros_like(carry_ref)
            dhalo_ref[...] = jnp.zeros_like(dhalo_ref)

        @pl.when((t == 0) & (b == 0))
        def _():
            for r in (dcw_ref, dcb_ref, dwx_ref, dwa_ref, dbx_ref, dba_ref, dlam_ref):
                r[...] = jnp.zeros_like(r)

        xa = xa_ref[...]
        ext_ref[0:8, :] = jnp.where(tt == 0, 0.0, xhalo_ref[...])
        ext_ref[8:8 + tc, :] = xa
        xc = _conv_from_ext(ext_ref, xa, cw_ref, cb_ref, tc)
        xcb, i_t, r_t, sp, a, mult = _lru_gates(xc, wx_ref, wa_ref, bx_ref, ba_ref, lam_ref)

        h = h_ref[...]
        ga = ga_ref[...]
        dya_t = dya_ref[...]
        sg = _sigmoid(ga)
        dga_ref[...] = (dya_t * h * (sg * (1.0 + ga * (1.0 - sg)))).astype(dga_ref.dtype)
        dlru = dya_t * (ga * sg)

        row = lax.broadcasted_iota(jnp.int32, a.shape, 0)
        coef = jnp.where(row == tc - 1, 1.0, pltpu.roll(a, tc - 1, 0))
        bcum, dloc = _scan_bwd(coef, dlru)
        dh = dloc + bcum * carry_ref[0:1, :]
        ext2_ref[0:tc, :] = a * dh
        carry_ref[...] = ext2_ref[0:8, :]

        ext2_ref[0:8, :] = jnp.where(tt == 0, 0.0, hhalo_ref[...])
        ext2_ref[8:8 + tc, :] = h
        hprev = ext2_ref[7:7 + tc, :]

        da = dh * hprev
        ix = i_t * xc
        dmult = dh * ix
        di = dh * mult * xc
        dxc = dh * mult * i_t
        dlog_a = da * a - dmult * (a * a) / mult
        dr = dlog_a * ((-LRU_C) * sp)
        dlam_ref[...] += jnp.sum(dlog_a * r_t, axis=0, keepdims=True) * (LRU_C * _sigmoid(-lam_ref[...]))
        dza = dr * r_t * (1.0 - r_t)
        dzx = di * i_t * (1.0 - i_t)
        dzab = _c(dza)
        dzxb = _c(dzx)
        dxc = dxc + _dot_nt(dzxb, wx_ref[0]) + _dot_nt(dzab, wa_ref[0])
        dwx_ref[0] += _dot_tn(xcb, dzxb)
        dwa_ref[0] += _dot_tn(xcb, dzab)
        dbx_ref[...] += jnp.sum(dzx, axis=0, keepdims=True)
        dba_ref[...] += jnp.sum(dza, axis=0, keepdims=True)

        dcb_ref[...] += jnp.sum(dxc, axis=0, keepdims=True)
        dcw_ref[3:4, :] += jnp.sum(dxc * xa, axis=0, keepdims=True)
        dcw_ref[2:3, :] += jnp.sum(dxc * ext_ref[7:7 + tc, :], axis=0, keepdims=True)
        dcw_ref[1:2, :] += jnp.sum(dxc * ext_ref[6:6 + tc, :], axis=0, keepdims=True)
        dcw_ref[0:1, :] += jnp.sum(dxc * ext_ref[5:5 + tc, :], axis=0, keepdims=True)
        ext2_ref[0:tc, :] = dxc
        ext2_ref[tc:tc + 8, :] = dhalo_ref[...]
        dxa = (cw_ref[3:4, :] * dxc + cw_ref[2:3, :] * ext2_ref[1:1 + tc, :]
               + cw_ref[1:2, :] * ext2_ref[2:2 + tc, :] + cw_ref[0:1, :] * ext2_ref[3:3 + tc, :])
        dxa_ref[...] = dxa.astype(dxa_ref.dtype)
        dhalo_ref[...] = ext2_ref[0:8, :]

    row_of = lambda b, t: b * nt + (nt - 1 - t)
    tile = lambda off: pl.BlockSpec((tc, CW), lambda c, b, t: (row_of(b, t), off + c))
    halo = pl.BlockSpec((8, CW), lambda c, b, t: (jnp.maximum(row_of(b, t) * h8 - 1, 0), c))
    vec = pl.BlockSpec((1, CW), lambda c, b, t: (0, c))
    mat = pl.BlockSpec((1, CW, CW), lambda c, b, t: (c, 0, 0))
    cwspec = pl.BlockSpec((CONV, CW), lambda c, b, t: (0, c))
    return pl.pallas_call(
        body,
        name="lru_bwd",
        grid=(N_CT, B, nt),
        in_specs=[tile(0), tile(0), halo, tile(N_CT), tile(0), halo, cwspec, vec, mat, mat, vec, vec, vec],
        out_specs=[tile(0), tile(0), cwspec, vec, mat, mat, vec, vec, vec],
        out_shape=[
            jax.ShapeDtypeStruct((T, D_MODEL), _MXU),
            jax.ShapeDtypeStruct((T, D_MODEL), _MXU),
            jax.ShapeDtypeStruct((CONV, D_MODEL), F32),
            jax.ShapeDtypeStruct((1, D_MODEL), F32),
            jax.ShapeDtypeStruct((N_CT, CW, CW), F32),
            jax.ShapeDtypeStruct((N_CT, CW, CW), F32),
            jax.ShapeDtypeStruct((1, D_MODEL), F32),
            jax.ShapeDtypeStruct((1, D_MODEL), F32),
            jax.ShapeDtypeStruct((1, D_MODEL), F32),
        ],
        scratch_shapes=[pltpu.VMEM((tc + 8, CW), F32), pltpu.VMEM((tc + 8, CW), F32),
                        pltpu.VMEM((8, CW), F32), pltpu.VMEM((8, CW), F32)],
        compiler_params=_params(("parallel", "arbitrary", "arbitrary")),
    )(dya, proj, proj, proj, hlru, hlru, conv_w, conv_b, wx_bd, wa_bd, bx, ba, lam)


def _retention_tables(S):
    half = DK // 2
    freqs = ROPE_THETA ** (-jnp.arange(half, dtype=F32) / half)
    ang = jnp.arange(S, dtype=F32)[:, None] * freqs[None, :]
    log_g = jnp.log1p(-(2.0 ** (-5.0 - jnp.arange(HEADS, dtype=F32))))
    idx = jnp.arange(CHUNK, dtype=F32)
    diff = idx[:, None] - idx[None, :]
    inner = jnp.where(diff >= 0, jnp.exp(jnp.maximum(diff, 0.0)[None] * log_g[:, None, None]), 0.0)
    cross = jnp.exp((idx[None, :] + 1.0) * log_g[:, None])[:, :, None]
    state = jnp.exp((CHUNK - 1.0 - idx[None, :]) * log_g[:, None])[:, :, None]
    gam = jnp.broadcast_to(jnp.exp(CHUNK * log_g)[:, None, None], (HEADS, 1, DK))
    return jnp.cos(ang), jnp.sin(ang), inner, cross, state, gam


def _rot(x, cos, sin):
    half = DK // 2
    x1, x2 = x[:, :half], x[:, half:]
    return jnp.concatenate([x1 * cos - x2 * sin, x1 * sin + x2 * cos], axis=-1)


def _rot_t(y, cos, sin):
    half = DK // 2
    y1, y2 = y[:, :half], y[:, half:]
    return jnp.concatenate([y1 * cos + y2 * sin, y2 * cos - y1 * sin], axis=-1)


def _groupnorm(o):
    mu = jnp.mean(o, axis=-1, keepdims=True)
    oc = o - mu
    rs = lax.rsqrt(jnp.mean(oc * oc, axis=-1, keepdims=True) + EPS)
    return oc * rs, rs


def _ret_specs(S, chunk_of):
    nc = S // CHUNK
    qkv = lambda g: pl.BlockSpec((CHUNK, D_MODEL), lambda b, c: (b * nc + chunk_of(c), g))
    act = pl.BlockSpec((CHUNK, D_MODEL), lambda b, c: (b * nc + chunk_of(c), 0))
    rope = pl.BlockSpec((CHUNK, DK // 2), lambda b, c: (chunk_of(c), 0))
    dmat = pl.BlockSpec((HEADS, CHUNK, CHUNK), lambda b, c: (0, 0, 0))
    dvec = pl.BlockSpec((HEADS, CHUNK, 1), lambda b, c: (0, 0, 0))
    hrow = pl.BlockSpec((HEADS, 1, DK), lambda b, c: (0, 0, 0))
    rst = pl.BlockSpec((1, HEADS, DK, DK), lambda b, c: (b * nc + chunk_of(c), 0, 0, 0))
    return qkv, act, rope, dmat, dvec, hrow, rst


def _ret_fwd(proj, tables, gain3, B, S):
    T = B * S
    nc = S // CHUNK
    cos, sin, dmat_t, cd_t, sd_t, gam_t = tables

    def body(q_ref, k_ref, v_ref, gb_ref, cos_ref, sin_ref, dm_ref, cd_ref, sd_ref, gam_ref, gain_ref,
             o_ref, yb_ref, rs_ref, state_ref):
        @pl.when(pl.program_id(1) == 0)
        def _():
            state_ref[...] = jnp.zeros_like(state_ref)

        cos_t, sin_t = cos_ref[...], sin_ref[...]
        for h in range(HEADS):
            cols = slice(h * DK, (h + 1) * DK)
            qb = _c(_rot(q_ref[:, cols], cos_t, sin_t))
            kb = _c(_rot(k_ref[:, cols], cos_t, sin_t) * (DK ** -0.5))
            v = v_ref[:, cols]
            state = state_ref[h]
            sb = _c(state)
            rs_ref[0, h] = sb
            scores = _dot_nt(qb, kb) * dm_ref[h]
            o = _dot(_c(scores), _c(v)) + _dot(qb, sb) * cd_ref[h]
            state_ref[h] = gam_ref[h] * state + _dot_tn(kb, _c(v * sd_ref[h]))
            o_ref[:, cols] = o
            n, _ = _groupnorm(o)
            gb = gb_ref[:, cols]
            yb_ref[:, cols] = (gb * _sigmoid(gb) * (n * gain_ref[h])).astype(yb_ref.dtype)

    qkv, act, rope, dmat, dvec, hrow, rst = _ret_specs(S, lambda c: c)
    return pl.pallas_call(
        body,
        name="ret_fwd",
        grid=(B, nc),
        in_specs=[qkv(2), qkv(3), qkv(4), qkv(5), rope, rope, dmat, dvec, dvec, hrow, hrow],
        out_specs=[act, act, rst],
        out_shape=[
            jax.ShapeDtypeStruct((T, D_MODEL), F32),
            jax.ShapeDtypeStruct((T, D_MODEL), _MXU),
            jax.ShapeDtypeStruct((B * nc, HEADS, DK, DK), _MXU),
        ],
        scratch_shapes=[pltpu.VMEM((HEADS, DK, DK), F32)],
        compiler_params=_params(("parallel", "arbitrary")),
    )(proj, proj, proj, proj, cos, sin, dmat_t, cd_t, sd_t, gam_t, gain3)


def _ret_bwd(dyb, o_pre, proj, states, tables, gain3, B, S):
    T = B * S
    nc = S // CHUNK
    cos, sin, dmat_t, cd_t, sd_t, gam_t = tables

    def body(dyb_ref, o_ref, q_ref, k_ref, v_ref, gb_ref, rs_ref, cos_ref, sin_ref, dm_ref, cd_ref, sd_ref, gam_ref,
             gain_ref, dr_ref, dgain_ref, dstate_ref):
        @pl.when(pl.program_id(1) == 0)
        def _():
            dstate_ref[...] = jnp.zeros_like(dstate_ref)

        @pl.when((pl.program_id(1) == 0) & (pl.program_id(0) == 0))
        def _():
            dgain_ref[...] = jnp.zeros_like(dgain_ref)

        cos_t, sin_t = cos_ref[...], sin_ref[...]
        for h in range(HEADS):
            cols = slice(h * DK, (h + 1) * DK)
            gain = gain_ref[h]
            n, rs = _groupnorm(o_ref[:, cols])
            gb = gb_ref[:, cols]
            sg = _sigmoid(gb)
            dy = dyb_ref[:, cols]
            part = lambda g: slice(g * D_MODEL + h * DK, g * D_MODEL + (h + 1) * DK)
            dr_ref[:, part(3)] = (dy * (n * gain) * (sg * (1.0 + gb * (1.0 - sg)))).astype(dr_ref.dtype)
            dgn = dy * (gb * sg)
            dgain_ref[h] += jnp.sum(dgn * n, axis=0, keepdims=True)
            dn = dgn * gain
            do = rs * (dn - jnp.mean(dn, axis=-1, keepdims=True) - n * jnp.mean(dn * n, axis=-1, keepdims=True))

            qb = _c(_rot(q_ref[:, cols], cos_t, sin_t))
            kb = _c(_rot(k_ref[:, cols], cos_t, sin_t) * (DK ** -0.5))
            v = v_ref[:, cols]
            vb = _c(v)
            vsb = _c(v * sd_ref[h])
            dob = _c(do)
            docb = _c(do * cd_ref[h])
            dmat = dm_ref[h]
            dstate = dstate_ref[h]
            dsb = _c(dstate)
            pb = _c(_dot_nt(qb, kb) * dmat)
            dsc = _c(_dot_nt(dob, vb) * dmat)
            dq = _dot(dsc, kb) + _dot_nt(docb, rs_ref[0, h])
            dk = _dot_tn(dsc, qb) + _dot_nt(vsb, dsb)
            dv = _dot_tn(pb, dob) + _dot(kb, dsb) * sd_ref[h]
            dstate_ref[h] = gam_ref[h] * dstate + _dot_tn(qb, docb)
            dr_ref[:, part(0)] = _rot_t(dq, cos_t, sin_t).astype(dr_ref.dtype)
            dr_ref[:, part(1)] = (_rot_t(dk, cos_t, sin_t) * (DK ** -0.5)).astype(dr_ref.dtype)
            dr_ref[:, part(2)] = dv.astype(dr_ref.dtype)

    qkv, act, rope, dmat, dvec, hrow, rst = _ret_specs(S, lambda c: nc - 1 - c)
    wide = pl.BlockSpec((CHUNK, 4 * D_MODEL), lambda b, c: (b * nc + nc - 1 - c, 0))
    return pl.pallas_call(
        body,
        name="ret_bwd",
        grid=(B, nc),
        in_specs=[act, act, qkv(2), qkv(3), qkv(4), qkv(5), rst, rope, rope, dmat, dvec, dvec, hrow, hrow],
        out_specs=[wide, hrow],
        out_shape=[jax.ShapeDtypeStruct((T, 4 * D_MODEL), _MXU), jax.ShapeDtypeStruct((HEADS, 1, DK), F32)],
        scratch_shapes=[pltpu.VMEM((HEADS, DK, DK), F32)],
        compiler_params=_params(("arbitrary", "arbitrary")),
    )(dyb, o_pre, proj, proj, proj, proj, states, cos, sin, dmat_t, cd_t, sd_t, gam_t, gain3)


def _mid(ya, yb, proj, x2d, tgt2d, wpa, wpb, wout, g_fin):
    T = x2d.shape[0]
    tm = min(256, T)
    n_steps = T // tm
    rows = D_MODEL // (2 * N_CHIPS)

    def body(ya_ref, yb_ref, ma_ref, mb_ref, x_ref, t_ref, gf_ref, wpa_hbm, wpb_hbm, wout_hbm,
             loss_ref, dx2_ref, dya_ref, dyb_ref, dm_ref, dgf_ref, gw_hbm, w_ref, acc_ref, sem):
        i = pl.program_id(0)

        @pl.when(i == 0)
        def _():
            loads = [pltpu.make_async_copy(src, w_ref.at[k], sem.at[k]) for k, src in enumerate((wpa_hbm, wpb_hbm, wout_hbm))]
            for cp in loads:
                cp.start()
            for cp in loads:
                cp.wait()
            acc_ref[...] = jnp.zeros_like(acc_ref)
            loss_ref[...] = jnp.zeros_like(loss_ref)
            dgf_ref[...] = jnp.zeros_like(dgf_ref)

        ya_t, yb_t = ya_ref[...], yb_ref[...]
        out_a = _dot(ya_t, w_ref[0])
        out_b = _dot(yb_t, w_ref[1])
        sa = _sigmoid(ma_ref[...])
        sb = _sigmoid(mb_ref[...])
        mgb = _c(sa * out_a + sb * out_b)
        x2 = x_ref[...] + _dot(mgb, w_ref[2])
        r2 = lax.rsqrt(jnp.mean(x2 * x2, axis=-1, keepdims=True) + EPS)
        nx = x2 * r2
        gf = gf_ref[...]
        err = nx * gf - t_ref[...]
        loss_ref[...] += 0.5 * jnp.sum(jnp.mean(err * err, axis=-1, keepdims=True), axis=0, keepdims=True)
        dy = err * (1.0 / D_MODEL)
        dgf_ref[...] += jnp.sum(dy * nx, axis=0, keepdims=True)
        dyg = dy * gf
        dx2 = r2 * (dyg - nx * jnp.mean(dyg * nx, axis=-1, keepdims=True))
        dx2_ref[...] = dx2
        dx2b = _c(dx2)
        dmg = _dot_nt(dx2b, w_ref[2])
        acc_ref[2] += _dot_tn(mgb, dx2b)
        dm_ref[:, :D_MODEL] = (dmg * out_a * sa * (1.0 - sa)).astype(dm_ref.dtype)
        dm_ref[:, D_MODEL:] = (dmg * out_b * sb * (1.0 - sb)).astype(dm_ref.dtype)
        dab = _c(dmg * sa)
        dbb = _c(dmg * sb)
        dya_ref[...] = _dot_nt(dab, w_ref[0])
        dyb_ref[...] = _dot_nt(dbb, w_ref[1])
        acc_ref[0] += _dot_tn(ya_t, dab)
        acc_ref[1] += _dot_tn(yb_t, dbb)

        @pl.when(i == n_steps - 1)
        def _():
            copies = [pltpu.make_async_copy(acc_ref.at[k, pl.ds((2 * p + hf) * rows, rows), :], gw_hbm.at[p, hf, k],
                                            sem.at[(k * N_CHIPS + p) * 2 + hf])
                      for k in range(3) for p in range(N_CHIPS) for hf in range(2)]
            for cp in copies:
                cp.start()
            for cp in copies:
                cp.wait()

    tile = lambda j: pl.BlockSpec((tm, D_MODEL), lambda i: (i, j))
    one = pl.BlockSpec((1, D_MODEL), lambda i: (0, 0))
    anyspec = pl.BlockSpec(memory_space=pl.ANY)
    return pl.pallas_call(
        body,
        name="mid",
        grid=(n_steps,),
        in_specs=[tile(0), tile(0), tile(6), tile(7), tile(0), tile(0), one, anyspec, anyspec, anyspec],
        out_specs=[pl.BlockSpec((1, 1), lambda i: (0, 0)), tile(0), tile(0), tile(0),
                   pl.BlockSpec((tm, 2 * D_MODEL), lambda i: (i, 0)), one, anyspec],
        out_shape=[
            jax.ShapeDtypeStruct((1, 1), F32),
            jax.ShapeDtypeStruct((T, D_MODEL), F32),
            jax.ShapeDtypeStruct((T, D_MODEL), F32),
            jax.ShapeDtypeStruct((T, D_MODEL), F32),
            jax.ShapeDtypeStruct((T, 2 * D_MODEL), _MXU),
            jax.ShapeDtypeStruct((1, D_MODEL), F32),
            jax.ShapeDtypeStruct((N_CHIPS, 2, 3, rows, D_MODEL), F32),
        ],
        scratch_shapes=[pltpu.VMEM((3, D_MODEL, D_MODEL), _MXU), pltpu.VMEM((3, D_MODEL, D_MODEL), F32),
                        pltpu.SemaphoreType.DMA((3 * N_CHIPS * 2,))],
        compiler_params=_params(("arbitrary",)),
    )(ya, yb, proj, proj, x2d, tgt2d, g_fin, wpa, wpb, wout)


def _inproj_bwd_dx(dparts, w_all, x2d, dx2, g_in):
    T = x2d.shape[0]
    tm = min(512, T)
    n_d = len(dparts)
    groups = [(a, k) for a, d in enumerate(dparts) for k in range(d.shape[1] // D_MODEL)]

    def body(*refs):
        d_refs = refs[:n_d]
        x_ref, dx2_ref, g_ref, w_hbm, dx_ref, dg_ref, w_ref, sem = refs[n_d:]

        @pl.when(pl.program_id(0) == 0)
        def _():
            cp = pltpu.make_async_copy(w_hbm, w_ref, sem)
            cp.start()
            cp.wait()
            dg_ref[...] = jnp.zeros_like(dg_ref)

        dh = jnp.zeros((tm, D_MODEL), F32)
        for j, (a, k) in enumerate(groups):
            dh = dh + _dot_nt(d_refs[a][:, k * D_MODEL:(k + 1) * D_MODEL],
                              w_ref[j // 2, :, (j % 2) * D_MODEL:(j % 2 + 1) * D_MODEL])
        x = x_ref[...]
        r = lax.rsqrt(jnp.mean(x * x, axis=-1, keepdims=True) + EPS)
        nx = x * r
        dg_ref[...] += jnp.sum(dh * nx, axis=0, keepdims=True)
        dhg = dh * g_ref[...]
        dx_ref[...] = dx2_ref[...] + r * (dhg - nx * jnp.mean(dhg * nx, axis=-1, keepdims=True))

    tile = pl.BlockSpec((tm, D_MODEL), lambda i: (i, 0))
    one = pl.BlockSpec((1, D_MODEL), lambda i: (0, 0))
    return pl.pallas_call(
        body,
        name="inproj_bwd_dx",
        grid=(T // tm,),
        in_specs=[pl.BlockSpec((tm, d.shape[1]), lambda i: (i, 0)) for d in dparts]
        + [tile, tile, one, pl.BlockSpec(memory_space=pl.ANY)],
        out_specs=[tile, one],
        out_shape=[jax.ShapeDtypeStruct((T, D_MODEL), F32), jax.ShapeDtypeStruct((1, D_MODEL), F32)],
        scratch_shapes=[pltpu.VMEM(w_all.shape, w_all.dtype), pltpu.SemaphoreType.DMA],
        compiler_params=_params(("arbitrary",)),
    )(*dparts, x2d, dx2, g_in, w_all)


def _inproj_bwd_dw(ht, dparts, first_chip, buf, name):
    T = ht.shape[1]
    tn = 512
    half = D_MODEL // 2
    per_chip = 2 * D_MODEL // tn
    n_d = len(dparts)
    tiles = [(a, t) for a, d in enumerate(dparts) for t in range(d.shape[1] // tn)]
    offs = [sum(d.shape[1] // tn for d in dparts[:a]) for a in range(n_d)]

    def body(*refs):
        ht_ref = refs[0]
        d_refs = refs[1:1 + n_d]
        out_ref = refs[-1]
        t = pl.program_id(0)
        for a in range(n_d):
            lo, hi = offs[a], offs[a] + dparts[a].shape[1] // tn

            @pl.when((t >= lo) & (t < hi))
            def _(a=a):
                g = _dot(ht_ref[...], d_refs[a][...])
                out_ref[0, 0] = g[:half]
                out_ref[0, 1] = g[half:]

    def dspec(a):
        n_a = dparts[a].shape[1] // tn
        return pl.BlockSpec((T, tn), lambda t: (0, jnp.clip(t - offs[a], 0, n_a - 1)))

    in_specs = [pl.BlockSpec((D_MODEL, T), lambda t: (0, 0))] + [dspec(a) for a in range(n_d)]
    args = [ht, *dparts]
    aliases = {}
    if buf is not None:
        in_specs.append(pl.BlockSpec(memory_space=pl.ANY))
        args.append(buf)
        aliases = {len(args) - 1: 0}
    return pl.pallas_call(
        body,
        name=name,
        grid=(len(tiles),),
        in_specs=in_specs,
        out_specs=pl.BlockSpec((1, 2, half, tn), lambda t: (first_chip + t // per_chip, 0, 0, t % per_chip)),
        out_shape=jax.ShapeDtypeStruct((N_CHIPS, 2, half, 2 * D_MODEL), F32),
        input_output_aliases=aliases,
        compiler_params=_params(("parallel",)),
    )(*args)


def _coords():
    return lax.axis_index("x"), lax.axis_index("y"), lax.axis_index("c")


def _other_chips(x, y):
    return [(1 - x, y), (x, 1 - y), (1 - x, 1 - y)]


def _all_gather8(xs, name):
    m_per, n = xs.shape

    def body(x_ref, out_ref, send_sems, recv_sems, local_sem):
        x, y, c = _coords()
        me, sibling = (x, y, c), (x, y, 1 - c)
        chips = _other_chips(x, y)

        def rows(px, py, pc):
            return out_ref.at[pl.ds((4 * px + 2 * py + pc) * m_per, m_per), :]

        def copy(k, block, to, src=None):
            return pltpu.make_async_remote_copy(
                src_ref=rows(*block) if src is None else src, dst_ref=rows(*block),
                send_sem=send_sems.at[k], recv_sem=recv_sems.at[k], device_id=to, device_id_type=MESH)

        mine = pltpu.make_async_copy(x_ref, rows(*me), local_sem)
        mine.start()
        first = [copy(0, me, sibling, src=x_ref)]
        first += [copy(1 + j, me, (*chip, c), src=x_ref) for j, chip in enumerate(chips)]
        for cp in first:
            cp.start()
        passed = [copy(4 + j, (*chip, c), sibling) for j, chip in enumerate(chips)]
        for j, chip in enumerate(chips):
            copy(1 + j, (*chip, c), me).wait_recv()
            passed[j].start()
        copy(0, sibling, me).wait_recv()
        for j, chip in enumerate(chips):
            copy(4 + j, (*chip, 1 - c), me).wait_recv()
        for cp in first + passed:
            cp.wait_send()
        mine.wait()

    return pl.pallas_call(
        body,
        name=name,
        out_shape=jax.ShapeDtypeStruct((8 * m_per, n), xs.dtype),
        in_specs=[pl.BlockSpec(memory_space=pltpu.VMEM)],
        out_specs=pl.BlockSpec(memory_space=pltpu.VMEM),
        scratch_shapes=[pltpu.SemaphoreType.DMA((7,)), pltpu.SemaphoreType.DMA((7,)), pltpu.SemaphoreType.DMA],
        compiler_params=pltpu.CompilerParams(vmem_limit_bytes=VMEM_LIMIT),
    )(xs)


def _chunks(rows, n):
    size = rows // n
    return [pl.ds(q * size, size) for q in range(n)]


HBM_SPEC = pl.BlockSpec(memory_space=pltpu.HBM)
SEM_SPEC = pl.BlockSpec(memory_space=pltpu.SEMAPHORE)
DATAFLOW = pltpu.SideEffectType.DATAFLOW_SIDE_EFFECTING


def _copies_start(bufs, plan, n_copies, name):
    n = len(bufs)

    def body(*refs):
        ins = refs[:n]
        send_sems, recv_sems = refs[n], refs[n + 1]
        token = refs[-1]
        for k, send, _ in plan(ins):
            if send is not None:
                src, dst, dev, pred = send
                cp = pltpu.make_async_remote_copy(src_ref=src, dst_ref=dst, send_sem=send_sems.at[k],
                                                  recv_sem=recv_sems.at[k], device_id=dev, device_id_type=MESH)
                if pred is None:
                    cp.start()
                else:
                    pl.when(pred)(cp.start)
        token[...] = jnp.zeros_like(token)

    hbm = [pltpu.with_memory_space_constraint(b, pltpu.HBM) for b in bufs]
    outs = pl.pallas_call(
        body,
        name=name,
        in_specs=[HBM_SPEC] * n,
        out_specs=(SEM_SPEC, SEM_SPEC, *([HBM_SPEC] * n), pl.BlockSpec(memory_space=pltpu.VMEM)),
        out_shape=(pltpu.SemaphoreType.DMA((n_copies,)), pltpu.SemaphoreType.DMA((n_copies,)),
                   *[pltpu.HBM(b.shape, b.dtype) for b in bufs], jax.ShapeDtypeStruct((8, 128), F32)),
        input_output_aliases={a: 2 + a for a in range(n)},
        compiler_params=pltpu.CompilerParams(has_side_effects=DATAFLOW),
    )(*hbm)
    return outs[0], outs[1], list(outs[2:2 + n]), outs[-1]


def _copies_wait(send_sems, recv_sems, bufs, after, plan, name):
    n = len(bufs)

    def body(*refs):
        ins = refs[:n]
        s_sems, r_sems = refs[n], refs[n + 1]
        for k, send, recv in plan(ins):
            if send is not None:
                src, dst, dev, pred = send
                cp = pltpu.make_async_remote_copy(src_ref=src, dst_ref=dst, send_sem=s_sems.at[k],
                                                  recv_sem=r_sems.at[k], device_id=dev, device_id_type=MESH)
                if pred is None:
                    cp.wait_send()
                else:
                    pl.when(pred)(cp.wait_send)
            if recv is not None:
                dst, pred = recv
                cp = pltpu.make_async_remote_copy(src_ref=dst, dst_ref=dst, send_sem=s_sems.at[k],
                                                  recv_sem=r_sems.at[k], device_id=_coords(), device_id_type=MESH)
                if pred is None:
                    cp.wait_recv()
                else:
                    pl.when(pred)(cp.wait_recv)

    outs = pl.pallas_call(
        body,
        name=name,
        in_specs=[HBM_SPEC] * n + [SEM_SPEC, SEM_SPEC, pl.BlockSpec(memory_space=pl.ANY)],
        out_specs=[HBM_SPEC] * n,
        out_shape=[pltpu.HBM(b.shape, b.dtype) for b in bufs],
        input_output_aliases={a: a for a in range(n)},
        compiler_params=pltpu.CompilerParams(has_side_effects=DATAFLOW),
    )(*bufs, send_sems, recv_sems, after)
    return list(outs)


def _gather_plan(n_bufs):
    def plan(refs):
        x, y, c = _coords()
        me = 2 * x + y
        out = []
        for k, (px, py) in enumerate(_other_chips(x, y)):
            for a in range(n_bufs):
                out.append((k * n_bufs + a, (refs[a].at[me], refs[a].at[me], (px, py, c), None),
                            (refs[a].at[2 * px + py], None)))
        return out
    return plan


def _cast_into_slot(ws, name):
    n = len(ws)
    nt = 2

    def body(s_ref, *refs):
        for a in range(n):
            refs[n + a][0] = refs[a][...].astype(refs[n + a].dtype)

    xi, yi, _ = _coords()
    return pl.pallas_call(
        body,
        name=name,
        grid_spec=pltpu.PrefetchScalarGridSpec(
            num_scalar_prefetch=1,
            grid=(2, nt),
            in_specs=[pl.BlockSpec((1, w.shape[1] // nt, w.shape[2]), lambda hf, i, s: (hf, i, 0)) for w in ws],
            out_specs=[pl.BlockSpec((1, 1, w.shape[1] // nt, w.shape[2]), lambda hf, i, s: (s[0], hf, i, 0)) for w in ws],
        ),
        out_shape=[jax.ShapeDtypeStruct((N_CHIPS,) + w.shape, _MXU) for w in ws],
        compiler_params=_params(("parallel", "parallel")),
    )((2 * xi + yi).reshape(1).astype(jnp.int32), *ws)


def _gather_chips(bufs, n_chunks, name):
    n = len(bufs)
    pieces = [(a, rows) for a in range(n) for rows in _chunks(bufs[a].shape[2], n_chunks[a])]
    n_p = len(pieces)

    def body(*refs):
        outs = refs[n:2 * n]
        send_sems, recv_sems, fsend_sems, frecv_sems = refs[2 * n:]
        x, y, c = _coords()
        me = 2 * x + y
        chips = _other_chips(x, y)

        def send(k, i, slot, chip):
            a, rows = pieces[i]
            return pltpu.make_async_remote_copy(
                src_ref=outs[a].at[slot, c, rows], dst_ref=outs[a].at[slot, c, rows], send_sem=send_sems.at[k * n_p + i],
                recv_sem=recv_sems.at[k * n_p + i], device_id=(*chip, c), device_id_type=MESH)

        def forward(k, i, slot, half):
            a, rows = pieces[i]
            return pltpu.make_async_remote_copy(
                src_ref=outs[a].at[slot, half, rows], dst_ref=outs[a].at[slot, half, rows],
                send_sem=fsend_sems.at[k * n_p + i], recv_sem=frecv_sems.at[k * n_p + i],
                device_id=(x, y, 1 - c), device_id_type=MESH)

        sends = [send(k, i, me, chip) for i in range(n_p) for k, chip in enumerate(chips)]
        for cp in sends:
            cp.start()
        forwards = []
        for i in range(n_p):
            for k, (px, py) in enumerate(chips):
                send(k, i, 2 * px + py, (px, py)).wait_recv()
                fw = forward(k, i, 2 * px + py, c)
                fw.start()
                forwards.append(fw)
        for i in range(n_p):
            for k, (px, py) in enumerate(chips):
                forward(k, i, 2 * px + py, 1 - c).wait_recv()
        for cp in sends + forwards:
            cp.wait_send()

    anyspec = pl.BlockSpec(memory_space=pl.ANY)
    sems = pltpu.SemaphoreType.DMA((3 * n_p,))
    return pl.pallas_call(
        body,
        name=name,
        in_specs=[anyspec] * n,
        out_specs=[anyspec] * n,
        out_shape=[jax.ShapeDtypeStruct(b.shape, b.dtype) for b in bufs],
        input_output_aliases={a: a for a in range(n)},
        scratch_shapes=[sems, sems, sems, sems],
    )(*bufs)


def _swap_halves(arrs, name):
    n = len(arrs)

    def body(*refs):
        ins, outs = refs[:n], refs[n:2 * n]
        send_sems, recv_sems = refs[2 * n:]
        x, y, c = _coords()
        copies = [pltpu.make_async_remote_copy(
            src_ref=ins[a].at[p, 1 - c], dst_ref=outs[a].at[p], send_sem=send_sems.at[a * N_CHIPS + p],
            recv_sem=recv_sems.at[a * N_CHIPS + p], device_id=(x, y, 1 - c), device_id_type=MESH)
            for a in range(n) for p in range(N_CHIPS)]
        for cp in copies:
            cp.start()
        for cp in copies:
            cp.wait()

    anyspec = pl.BlockSpec(memory_space=pl.ANY)
    return pl.pallas_call(
        body,
        name=name,
        in_specs=[anyspec] * n,
        out_specs=[anyspec] * n,
        out_shape=[jax.ShapeDtypeStruct((N_CHIPS,) + a.shape[2:], a.dtype) for a in arrs],
        scratch_shapes=[pltpu.SemaphoreType.DMA((N_CHIPS * n,)), pltpu.SemaphoreType.DMA((N_CHIPS * n,))],
    )(*arrs)


def _scatter_chips(arrs, n_chunks, name):
    n = len(arrs)
    pieces = [(a, rows) for a in range(n) for rows in _chunks(arrs[a].shape[1], n_chunks[a])]
    n_p = len(pieces)

    def body(*refs):
        ins, outs = refs[:n], refs[n:2 * n]
        send_sems, recv_sems = refs[2 * n:]
        x, y, c = _coords()
        chips = _other_chips(x, y)

        def copy(k, i, src_slot, chip):
            a, rows = pieces[i]
            return pltpu.make_async_remote_copy(
                src_ref=ins[a].at[src_slot, rows], dst_ref=outs[a].at[k, rows],
                send_sem=send_sems.at[k * n_p + i], recv_sem=recv_sems.at[k * n_p + i],
                device_id=(*chip, c), device_id_type=MESH)

        sends = [copy(k, i, 2 * px + py, (px, py)) for i in range(n_p) for k, (px, py) in enumerate(chips)]
        for cp in sends:
            cp.start()
        for cp in sends:
            cp.wait()

    anyspec = pl.BlockSpec(memory_space=pl.ANY)
    return pl.pallas_call(
        body,
        name=name,
        in_specs=[anyspec] * n,
        out_specs=[anyspec] * n,
        out_shape=[jax.ShapeDtypeStruct((3,) + a.shape[1:], a.dtype) for a in arrs],
        scratch_shapes=[pltpu.SemaphoreType.DMA((3 * n_p,)), pltpu.SemaphoreType.DMA((3 * n_p,))],
    )(*arrs)


def _join_halves(bufs, n_chunks, name):
    n = len(bufs)
    pieces = [(a, rows) for a in range(n) for rows in _chunks(bufs[a].shape[1], n_chunks[a])]
    n_p = len(pieces)

    def body(*refs):
        outs = refs[n:2 * n]
        send_sems, recv_sems = refs[2 * n:]
        x, y, c = _coords()

        def copy(i, half):
            a, rows = pieces[i]
            return pltpu.make_async_remote_copy(
                src_ref=outs[a].at[half, rows], dst_ref=outs[a].at[half, rows], send_sem=send_sems.at[i],
                recv_sem=recv_sems.at[i], device_id=(x, y, 1 - c), device_id_type=MESH)

        sends = [copy(i, c) for i in range(n_p)]
        for cp in sends:
            cp.start()
        for i in range(n_p):
            copy(i, 1 - c).wait_recv()
        for cp in sends:
            cp.wait_send()

    anyspec = pl.BlockSpec(memory_space=pl.ANY)
    sems = pltpu.SemaphoreType.DMA((n_p,))
    return pl.pallas_call(
        body,
        name=name,
        in_specs=[anyspec] * n,
        out_specs=[anyspec] * n,
        out_shape=[jax.ShapeDtypeStruct(b.shape, b.dtype) for b in bufs],
        input_output_aliases={a: a for a in range(n)},
        scratch_shapes=[sems, sems],
    )(*bufs)


def _row_tile(rows, cap):
    t = cap
    while rows % t:
        t //= 2
    return t


def _add_my_half(g, r, name):
    _, _, R, C = g.shape
    tr = _row_tile(R, 256)

    def body(c_ref, g_ref, r_ref, o_ref):
        o_ref[...] = (g_ref[0] + r_ref[...]).astype(o_ref.dtype)

    return pl.pallas_call(
        body,
        name=name,
        grid_spec=pltpu.PrefetchScalarGridSpec(
            num_scalar_prefetch=1,
            grid=(N_CHIPS, R // tr),
            in_specs=[pl.BlockSpec((1, 1, tr, C), lambda p, i, c_ref: (p, c_ref[0], i, 0)),
                      pl.BlockSpec((1, tr, C), lambda p, i, c_ref: (p, i, 0))],
            out_specs=pl.BlockSpec((1, tr, C), lambda p, i, c_ref: (p, i, 0)),
        ),
        out_shape=jax.ShapeDtypeStruct(r.shape, jnp.bfloat16),
        compiler_params=_params(("parallel", "parallel")),
    )(lax.axis_index("c").reshape(1).astype(jnp.int32), g, r)


def _sum_slabs(own, got, name):
    _, R, C = own.shape
    tr = _row_tile(R, 256)

    def body(s_ref, own_ref, got_ref, o_ref):
        o_ref[0] = ((own_ref[0].astype(F32) + got_ref[0].astype(F32)) + got_ref[1].astype(F32)) + got_ref[2].astype(F32)

    xi, yi, ci = _coords()
    return pl.pallas_call(
        body,
        name=name,
        grid_spec=pltpu.PrefetchScalarGridSpec(
            num_scalar_prefetch=1,
            grid=(R // tr,),
            in_specs=[pl.BlockSpec((1, tr, C), lambda i, s: (s[0], i, 0)),
                      pl.BlockSpec((3, tr, C), lambda i, s: (0, i, 0))],
            out_specs=pl.BlockSpec((1, tr, C), lambda i, s: (s[1], i, 0)),
        ),
        out_shape=jax.ShapeDtypeStruct((2, R, C), F32),
        compiler_params=_params(("parallel",)),
    )(jnp.stack([2 * xi + yi, ci]).astype(jnp.int32), own, got)


def _sum_rows8(g, m_per, name):
    n = g.shape[1]

    def body(g_ref, o_ref):
        acc = g_ref[0:m_per, :]
        for k in range(1, 8):
            acc = acc + g_ref[k * m_per:(k + 1) * m_per, :]
        o_ref[...] = acc

    return pl.pallas_call(
        body,
        name=name,
        out_shape=jax.ShapeDtypeStruct((m_per, n), F32),
        compiler_params=_params(),
    )(g)


def _adamw_math(w, g, m, v):
    m = ADAM_B1 * m + (1.0 - ADAM_B1) * g
    v = ADAM_B2 * v + (1.0 - ADAM_B2) * (g * g)
    m_hat = m / (1.0 - ADAM_B1 ** ADAM_STEP)
    v_hat = v / (1.0 - ADAM_B2 ** ADAM_STEP)
    delta = -ADAM_LR * (m_hat / (jnp.sqrt(v_hat) + ADAM_EPS) + ADAM_WD * w)
    return delta, m, v


def _adamw_big(w, g, m, v, name):
    R, C = w.shape
    tr = min(128, R)

    def body(w_ref, g_ref, m_ref, v_ref, d_out, m_out, v_out):
        d, mn, vn = _adamw_math(w_ref[...], g_ref[...], m_ref[...], v_ref[...])
        d_out[...] = d
        m_out[...] = mn
        v_out[...] = vn

    spec = pl.BlockSpec((tr, C), lambda i: (i, 0))
    return pl.pallas_call(
        body,
        name=name,
        grid=(R // tr,),
        in_specs=[spec] * 4,
        out_specs=[spec] * 3,
        out_shape=[jax.ShapeDtypeStruct((R, C), F32)] * 3,
        compiler_params=_params(("parallel",)),
    )(w, g, m, v)


def _adamw_small(ws, gs, ms, vs, name):
    n = len(ws)

    def body(*refs):
        for a in range(n):
            d, mn, vn = _adamw_math(refs[a][...], refs[n + a][...], refs[2 * n + a][...], refs[3 * n + a][...])
            refs[4 * n + a][...] = d
            refs[5 * n + a][...] = mn
            refs[6 * n + a][...] = vn

    shapes = [jax.ShapeDtypeStruct(w.shape, F32) for w in ws]
    outs = pl.pallas_call(
        body,
        name=name,
        out_shape=shapes * 3,
        compiler_params=_params(),
    )(*ws, *gs, *ms, *vs)
    return outs[:n], outs[n:2 * n], outs[2 * n:]


def _to_blockdiag(w):
    per = CW // LRU_BW
    w4 = w.reshape(N_CT, per, LRU_BW, LRU_BW)
    eye = jnp.eye(per, dtype=w.dtype)
    return (w4[:, :, :, None, :] * eye[None, :, None, :, None]).reshape(N_CT, CW, CW)


def _from_blockdiag(g):
    per = CW // LRU_BW
    g5 = g.reshape(N_CT, per, LRU_BW, per, LRU_BW)
    return jnp.stack([g5[:, b, :, b, :] for b in range(per)], axis=1).reshape(LRU_BLOCKS, LRU_BW, LRU_BW)


def _local_grads(x2d, tgt2d, B, S, g_in, w_all, conv_w, conv_b, gate_x_w, gate_x_b, gate_a_w, gate_a_b, lam, gain,
                 proj_weights, g_fin, deps=()):
    wx_bd = _c(_to_blockdiag(gate_x_w))
    wa_bd = _c(_to_blockdiag(gate_a_w))
    tables = _retention_tables(S)
    gain3 = gain.reshape(HEADS, 1, DK)

    proj, ht = _inproj_fwd(x2d, g_in, w_all, deps)
    hlru, ya = _lru_fwd(proj, conv_w, conv_b, wx_bd, wa_bd, gate_x_b, gate_a_b, lam, B, S)
    o_pre, yb, states = _ret_fwd(proj, tables, gain3, B, S)
    wpa, wpb, wout = proj_weights(yb)
    loss, dx2, dya, dyb, dm, dgf, gw_proj = _mid(ya, yb, proj, x2d, tgt2d, wpa, wpb, wout, g_fin)
    gw_in = _inproj_bwd_dw(ht, [dm], 3, None, "inproj_bwd_dw_m")
    dxa, dga, dcw, dcb, dwx_bd, dwa_bd, dbx, dba, dlam = _lru_bwd(
        dya, proj, hlru, conv_w, conv_b, wx_bd, wa_bd, gate_x_b, gate_a_b, lam, B, S)
    gw_in = _inproj_bwd_dw(ht, [dxa, dga], 0, gw_in, "inproj_bwd_dw_a")
    dr, dgain = _ret_bwd(dyb, o_pre, proj, states, tables, gain3, B, S)
    gw_in = _inproj_bwd_dw(ht, [dr], 1, gw_in, "inproj_bwd_dw_r")
    grad_x, dgin = _inproj_bwd_dx([dxa, dga, dr, dm], w_all, x2d, dx2, g_in)
    small = dict(norm_in=dgin, conv_w=dcw, conv_b=dcb, gate_x_w=_from_blockdiag(dwx_bd), gate_x_b=dbx,
                 gate_a_w=_from_blockdiag(dwa_bd), gate_a_b=dba, lru_lambda=dlam, gn_gain=dgain.reshape(HEADS, DK),
                 norm_final=dgf)
    return loss[0, 0], grad_x, gw_in, gw_proj, small


_SMALL = ("gate_x_w", "gate_a_w", "norm_in", "conv_w", "conv_b", "gate_x_b", "gate_a_b", "lru_lambda", "gn_gain",
          "norm_final")
_SMALL_SHAPES = dict(gate_x_w=(LRU_BLOCKS, LRU_BW, LRU_BW), gate_a_w=(LRU_BLOCKS, LRU_BW, LRU_BW),
                     norm_in=(1, D_MODEL), conv_w=(CONV, D_MODEL), conv_b=(1, D_MODEL), gate_x_b=(1, D_MODEL),
                     gate_a_b=(1, D_MODEL), lru_lambda=(1, D_MODEL), gn_gain=(HEADS, DK), norm_final=(1, D_MODEL))


def _pack_small(small):
    return jnp.concatenate([small[k].reshape(-1, 128) for k in _SMALL], axis=0)


def _unpack_small(packed):
    out, r = {}, 0
    for k in _SMALL:
        shape = _SMALL_SHAPES[k]
        rows = 1
        for s in shape:
            rows *= s
        rows //= 128
        out[k] = packed[r:r + rows].reshape(shape)
        r += rows
    return out


def kernel(x, norm_in, w_in, conv_w, conv_b, gate_x_w, gate_x_b, gate_a_w, gate_a_b, lru_lambda, gn_gain, w_proj_a, w_proj_b, w_out, norm_final, loss_target, m_norm_in, m_w_in, m_conv_w, m_conv_b, m_gate_x_w, m_gate_x_b, m_gate_a_w, m_gate_a_b, m_lru_lambda, m_gn_gain, m_w_proj_a, m_w_proj_b, m_w_out, m_norm_final, v_norm_in, v_w_in, v_conv_w, v_conv_b, v_gate_x_w, v_gate_x_b, v_gate_a_w, v_gate_a_b, v_lru_lambda, v_gn_gain, v_w_proj_a, v_w_proj_b, v_w_out, v_norm_final):
    B, S, _ = x.shape
    T = B * S
    xi, yi, ci = _coords()
    chip = 2 * xi + yi

    cshard = D_MODEL // N_CHIPS
    mine = _cast_into_slot([w_in[0].reshape(2, D_MODEL // 2, 2 * D_MODEL)]
                           + [w[0].reshape(2, cshard // 2, D_MODEL) for w in (w_proj_a, w_proj_b, w_out)],
                           "cast_weights")
    plan = _gather_plan(3)
    s_sems, r_sems, pbufs, token = _copies_start(mine[1:], plan, 9, "gather_proj_start")
    w_all = _gather_chips(mine[:1], [4], "gather_weights")[0].reshape(N_CHIPS, D_MODEL, 2 * D_MODEL)

    def proj_weights(after):
        got = _copies_wait(s_sems, r_sems, pbufs, after, plan, "gather_proj_wait")
        return [b.reshape(D_MODEL, D_MODEL) for b in got]

    gshard = DK // N_CHIPS
    tiny = jnp.concatenate([conv_w[0], jnp.zeros((4, cshard), F32), jnp.pad(gn_gain[0], ((0, 4), (0, cshard - gshard)))],
                           axis=0)
    tiny_all = _all_gather8(tiny, "gather_small_weights").reshape(N_CHIPS, 2, 16, cshard)[:, 0]
    conv_w_full = jnp.transpose(tiny_all[:, 0:CONV, :], (1, 0, 2)).reshape(CONV, D_MODEL)
    gain_full = jnp.transpose(tiny_all[:, 8:8 + HEADS, :gshard], (1, 0, 2)).reshape(HEADS, DK)

    loss, grad_x, gw_in, gw_proj, small = _local_grads(
        x.reshape(T, D_MODEL), loss_target.reshape(T, D_MODEL), B, S, norm_in, w_all, conv_w_full, conv_b,
        gate_x_w[0], gate_x_b, gate_a_w[0], gate_a_b, lru_lambda, gain_full, proj_weights,
        norm_final.reshape(1, D_MODEL), deps=(token,))
    loss = lax.psum(loss, ("x", "y", "c"))

    packed = _pack_small(small)
    m_per = packed.shape[0]
    gsm = _unpack_small(_sum_rows8(_all_gather8(packed, "gather_small_grads"), m_per, "sum_small_grads"))

    rows_p = 3 * D_MODEL // (2 * N_CHIPS)
    gw_proj = gw_proj.reshape(N_CHIPS, 2, rows_p, D_MODEL)
    other = _swap_halves([gw_in, gw_proj], "swap_halves")
    chip_in = _add_my_half(gw_in, other[0], "chip_sum_w_in")
    chip_pr = _add_my_half(gw_proj, other[1], "chip_sum_w_proj")
    got = _scatter_chips([chip_in, chip_pr], [4, 2], "scatter_chips")
    half_in = _sum_slabs(chip_in, got[0], "sum_w_in")
    half_pr = _sum_slabs(chip_pr, got[1], "sum_w_proj")
    g_in_full, g_pr_full = _join_halves([half_in, half_pr], [8, 4], "join_halves")
    g_w_in = g_in_full.reshape(D_MODEL, 2 * D_MODEL)
    g_pr = g_pr_full.reshape(2, 3, D_MODEL // (2 * N_CHIPS), D_MODEL)
    g_wpa, g_wpb, g_wout = (g_pr[:, k].reshape(cshard, D_MODEL) for k in range(3))

    grads = dict(gsm)
    grads["conv_w"] = lax.dynamic_slice_in_dim(gsm["conv_w"], chip * cshard, cshard, axis=1)
    grads["gn_gain"] = lax.dynamic_slice_in_dim(gsm["gn_gain"], chip * gshard, gshard, axis=1)
    grads.update(w_in=g_w_in, w_proj_a=g_wpa, w_proj_b=g_wpb, w_out=g_wout)

    weights = dict(norm_in=norm_in, w_in=w_in, conv_w=conv_w, conv_b=conv_b, gate_x_w=gate_x_w, gate_x_b=gate_x_b,
                   gate_a_w=gate_a_w, gate_a_b=gate_a_b, lru_lambda=lru_lambda, gn_gain=gn_gain, w_proj_a=w_proj_a,
                   w_proj_b=w_proj_b, w_out=w_out, norm_final=norm_final)
    ms = dict(norm_in=m_norm_in, w_in=m_w_in, conv_w=m_conv_w, conv_b=m_conv_b, gate_x_w=m_gate_x_w,
              gate_x_b=m_gate_x_b, gate_a_w=m_gate_a_w, gate_a_b=m_gate_a_b, lru_lambda=m_lru_lambda, gn_gain=m_gn_gain,
              w_proj_a=m_w_proj_a, w_proj_b=m_w_proj_b, w_out=m_w_out, norm_final=m_norm_final)
    vs = dict(norm_in=v_norm_in, w_in=v_w_in, conv_w=v_conv_w, conv_b=v_conv_b, gate_x_w=v_gate_x_w,
              gate_x_b=v_gate_x_b, gate_a_w=v_gate_a_w, gate_a_b=v_gate_a_b, lru_lambda=v_lru_lambda, gn_gain=v_gn_gain,
              w_proj_a=v_w_proj_a, w_proj_b=v_w_proj_b, w_out=v_w_out, norm_final=v_norm_final)
    names = list(weights)
    grads = {k: grads[k].reshape(weights[k].shape) for k in names}

    delta, new_m, new_v = {}, {}, {}
    for k in ("w_in", "w_proj_a", "w_proj_b", "w_out"):
        shp = weights[k].shape
        two = lambda a: a.reshape(shp[1], shp[2])
        d, mn, vn = _adamw_big(two(weights[k]), two(grads[k]), two(ms[k]), two(vs[k]), "adamw_" + k)
        delta[k], new_m[k], new_v[k] = d.reshape(shp), mn.reshape(shp), vn.reshape(shp)
    smalls = [k for k in names if k not in delta]

    def view(a):
        return a.reshape(1, -1) if a.ndim == 1 else (a.reshape(a.shape[1:]) if a.ndim > 2 else a)

    ds, mns, vns = _adamw_small([view(weights[k]) for k in smalls], [view(grads[k]) for k in smalls],
                                [view(ms[k]) for k in smalls], [view(vs[k]) for k in smalls], "adamw_small")
    for k, d, mn, vn in zip(smalls, ds, mns, vns):
        shp = weights[k].shape
        delta[k], new_m[k], new_v[k] = d.reshape(shp), mn.reshape(shp), vn.reshape(shp)

    return (loss, grad_x.reshape(B, S, D_MODEL), *[grads[k] for k in names], *[delta[k] for k in names],
            *[new_m[k] for k in names], *[new_v[k] for k in names])
```

```python
import functools

import jax
import jax.numpy as jnp
from jax import lax
from jax.experimental import pallas as pl
from jax.experimental.pallas import tpu as pltpu

F32 = jnp.float32
_MXU = jnp.bfloat16

D_MODEL = 1024
N_GROUPS = 8
HEADS = 4
DK = 256
CHUNK = 128
CONV = 4
LRU_BLOCKS = 16
LRU_BW = 64
LRU_C = 8.0
ROPE_THETA = 10000.0
EPS = 1e-6
CW = 256
N_CT = D_MODEL // CW
N_CHIPS = 4
MESH = pl.DeviceIdType.MESH

ADAM_LR = 0.001
ADAM_B1 = 0.9
ADAM_B2 = 0.999
ADAM_EPS = 1e-08
ADAM_WD = 0.01
ADAM_STEP = 10

VMEM_LIMIT = 56 * 1024 * 1024


def _c(v):
    return v.astype(_MXU)


def _dot(a, b):
    return lax.dot_general(a, b, (((1,), (0,)), ((), ())), preferred_element_type=F32)


def _dot_nt(a, b):
    return lax.dot_general(a, b, (((1,), (1,)), ((), ())), preferred_element_type=F32)


def _dot_tn(a, b):
    return lax.dot_general(a, b, (((0,), (0,)), ((), ())), preferred_element_type=F32)


def _sigmoid(z):
    return 1.0 / (1.0 + jnp.exp(-z))


ANY_SPEC = pl.BlockSpec(memory_space=pl.ANY)


def _after(body, n_in, deps):
    n_deps = len(deps)

    def wrapped(*refs):
        return body(*refs[:n_in], *refs[n_in + n_deps:])

    return wrapped


def _params(sem=None):
    if sem is None:
        return pltpu.CompilerParams(vmem_limit_bytes=VMEM_LIMIT)
    return pltpu.CompilerParams(vmem_limit_bytes=VMEM_LIMIT, dimension_semantics=sem)


def _inproj_fwd(x2d, g_in, w_all, deps=()):
    T = x2d.shape[0]
    tm = min(512, T)
    n_i = T // tm

    def body(*refs):
        x_ref, g_ref, w_ref = refs[:3]
        proj_ref, ht_ref, h_all = refs[-3:]
        i = pl.program_id(1)
        rows = pl.ds(pl.multiple_of(i * tm, tm), tm)

        @pl.when(pl.program_id(0) == 0)
        def _():
            x = x_ref[...]
            r = lax.rsqrt(jnp.mean(x * x, axis=-1, keepdims=True) + EPS)
            h = x * r * g_ref[...]
            h_all[rows, :] = h.astype(h_all.dtype)
            ht_ref[...] = h.T.astype(ht_ref.dtype)

        proj_ref[...] = _dot(h_all[rows, :], w_ref[0])

    first = lambda j, i: jnp.where(j == 0, i, n_i - 1)
    return pl.pallas_call(
        body,
        name="inproj_fwd",
        grid=(N_GROUPS, n_i),
        in_specs=[
            pl.BlockSpec((tm, D_MODEL), lambda j, i: (first(j, i), 0)),
            pl.BlockSpec((1, D_MODEL), lambda j, i: (0, 0)),
            pl.BlockSpec((1, D_MODEL, D_MODEL), lambda j, i: (j // 2, 0, j % 2)),
        ] + [pl.BlockSpec(memory_space=pl.ANY)] * len(deps),
        out_specs=[
            pl.BlockSpec((tm, D_MODEL), lambda j, i: (i, j)),
            pl.BlockSpec((D_MODEL, tm), lambda j, i: (0, first(j, i))),
        ],
        out_shape=[
            jax.ShapeDtypeStruct((T, N_GROUPS * D_MODEL), F32),
            jax.ShapeDtypeStruct((D_MODEL, T), _MXU),
        ],
        scratch_shapes=[pltpu.VMEM((T, D_MODEL), _MXU)],
        compiler_params=_params(("arbitrary", "arbitrary")),
    )(x2d, g_in, w_all, *deps)


def _scan_fwd(a, u):
    n = a.shape[0]
    row = lax.broadcasted_iota(jnp.int32, a.shape, 0)
    s = 1
    while s < n:
        m = row >= s
        u = u + a * jnp.where(m, pltpu.roll(u, s, 0), 0.0)
        a = a * jnp.where(m, pltpu.roll(a, s, 0), 1.0)
        s *= 2
    return a, u


def _scan_bwd(b, g):
    n = b.shape[0]
    row = lax.broadcasted_iota(jnp.int32, b.shape, 0)
    s = 1
    while s < n:
        m = row < n - s
        g = g + b * jnp.where(m, pltpu.roll(g, n - s, 0), 0.0)
        b = b * jnp.where(m, pltpu.roll(b, n - s, 0), 1.0)
        s *= 2
    return b, g


def _softplus_neg(lam):
    z = -lam
    return jnp.maximum(z, 0.0) + jnp.log1p(jnp.exp(-jnp.abs(z)))


def _lru_gates(xc, wx_ref, wa_ref, bx_ref, ba_ref, lam_ref):
    xcb = _c(xc)
    i_t = _sigmoid(_dot(xcb, wx_ref[0]) + bx_ref[...])
    r_t = _sigmoid(_dot(xcb, wa_ref[0]) + ba_ref[...])
    sp = _softplus_neg(lam_ref[...])
    log_a = (-LRU_C) * r_t * sp
    a = jnp.exp(log_a)
    mult = jnp.sqrt(1.0 - a * a)
    return xcb, i_t, r_t, sp, a, mult


def _conv_from_ext(ext_ref, xa, cw_ref, cb_ref, tc):
    return (cb_ref[...] + cw_ref[3:4, :] * xa + cw_ref[2:3, :] * ext_ref[7:7 + tc, :]
            + cw_ref[1:2, :] * ext_ref[6:6 + tc, :] + cw_ref[0:1, :] * ext_ref[5:5 + tc, :])


def _lru_fwd(proj, conv_w, conv_b, wx_bd, wa_bd, bx, ba, lam, B, S):
    T = B * S
    tc = min(256, S)
    nt = S // tc
    h8 = tc // 8

    def body(xa_ref, halo_ref, ga_ref, cw_ref, cb_ref, wx_ref, wa_ref, bx_ref, ba_ref, lam_ref,
             h_ref, ya_ref, ext_ref, carry_ref):
        t = pl.program_id(2)

        @pl.when(t == 0)
        def _():
            carry_ref[...] = jnp.zeros_like(carry_ref)

        xa = xa_ref[...]
        ext_ref[0:8, :] = jnp.where(t == 0, 0.0, halo_ref[...])
        ext_ref[8:8 + tc, :] = xa
        xc = _conv_from_ext(ext_ref, xa, cw_ref, cb_ref, tc)
        _, i_t, _, _, a, mult = _lru_gates(xc, wx_ref, wa_ref, bx_ref, ba_ref, lam_ref)
        u = mult * (i_t * xc)
        acum, hloc = _scan_fwd(a, u)
        h = hloc + acum * carry_ref[7:8, :]
        h_ref[...] = h
        carry_ref[...] = h[tc - 8:tc, :]
        ga = ga_ref[...]
        ya_ref[...] = (ga * _sigmoid(ga) * h).astype(ya_ref.dtype)

    row = lambda b, t: b * nt + t
    vec = pl.BlockSpec((1, CW), lambda b, c, t: (0, c))
    mat = pl.BlockSpec((1, CW, CW), lambda b, c, t: (c, 0, 0))
    return pl.pallas_call(
        body,
        name="lru_fwd",
        grid=(B, N_CT, nt),
        in_specs=[
            pl.BlockSpec((tc, CW), lambda b, c, t: (row(b, t), c)),
            pl.BlockSpec((8, CW), lambda b, c, t: (jnp.maximum(row(b, t) * h8 - 1, 0), c)),
            pl.BlockSpec((tc, CW), lambda b, c, t: (row(b, t), N_CT + c)),
            pl.BlockSpec((CONV, CW), lambda b, c, t: (0, c)),
            vec, mat, mat, vec, vec, vec,
        ],
        out_specs=[
            pl.BlockSpec((tc, CW), lambda b, c, t: (row(b, t), c)),
            pl.BlockSpec((tc, CW), lambda b, c, t: (row(b, t), c)),
        ],
        out_shape=[
            jax.ShapeDtypeStruct((T, D_MODEL), F32),
            jax.ShapeDtypeStruct((T, D_MODEL), _MXU),
        ],
        scratch_shapes=[pltpu.VMEM((tc + 8, CW), F32), pltpu.VMEM((8, CW), F32)],
        compiler_params=_params(("parallel", "parallel", "arbitrary")),
    )(proj, proj, proj, conv_w, conv_b, wx_bd, wa_bd, bx, ba, lam)


def _lru_bwd(dya, proj, hlru, conv_w, conv_b, wx_bd, wa_bd, bx, ba, lam, B, S, deps=()):
    T = B * S
    tc = min(256, S)
    nt = S // tc
    h8 = tc // 8

    def body(dya_ref, xa_ref, xhalo_ref, ga_ref, h_ref, hhalo_ref, cw_ref, cb_ref, wx_ref, wa_ref, bx_ref, ba_ref,
             lam_ref, dxa_ref, dga_ref, dcw_ref, dcb_ref, dwx_ref, dwa_ref, dbx_ref, dba_ref, dlam_ref,
             ext_ref, ext2_ref, carry_ref, dhalo_ref):
        b = pl.program_id(1)
        t = pl.program_id(2)
        tt = nt - 1 - t

        @pl.when(t == 0)
        def _():
            carry_ref[...] = jnp.zeros_like(carry_ref)
            dhalo_ref[...] = jnp.zeros_like(dhalo_ref)

        @pl.when((t == 0) & (b == 0))
        def _():
            for r in (dcw_ref, dcb_ref, dwx_ref, dwa_ref, dbx_ref, dba_ref, dlam_ref):
                r[...] = jnp.zeros_like(r)

        xa = xa_ref[...]
        ext_ref[0:8, :] = jnp.where(tt == 0, 0.0, xhalo_ref[...])
        ext_ref[8:8 + tc, :] = xa
        xc = _conv_from_ext(ext_ref, xa, cw_ref, cb_ref, tc)
        xcb, i_t, r_t, sp, a, mult = _lru_gates(xc, wx_ref, wa_ref, bx_ref, ba_ref, lam_ref)

        h = h_ref[...]
        ga = ga_ref[...]
        dya_t = dya_ref[...]
        sg = _sigmoid(ga)
        dga_ref[...] = (dya_t * h * (sg * (1.0 + ga * (1.0 - sg)))).astype(dga_ref.dtype)
        dlru = dya_t * (ga * sg)

        row = lax.broadcasted_iota(jnp.int32, a.shape, 0)
        coef = jnp.where(row == tc - 1, 1.0, pltpu.roll(a, tc - 1, 0))
        bcum, dloc = _scan_bwd(coef, dlru)
        dh = dloc + bcum * carry_ref[0:1, :]
        ext2_ref[0:tc, :] = a * dh
        carry_ref[...] = ext2_ref[0:8, :]

        ext2_ref[0:8, :] = jnp.where(tt == 0, 0.0, hhalo_ref[...])
        ext2_ref[8:8 + tc, :] = h
        hprev = ext2_ref[7:7 + tc, :]

        da = dh * hprev
        ix = i_t * xc
        dmult = dh * ix
        di = dh * mult * xc
        dxc = dh * mult * i_t
        dlog_a = da * a - dmult * (a * a) / mult
        dr = dlog_a * ((-LRU_C) * sp)
        dlam_ref[...] += jnp.sum(dlog_a * r_t, axis=0, keepdims=True) * (LRU_C * _sigmoid(-lam_ref[...]))
        dza = dr * r_t * (1.0 - r_t)
        dzx = di * i_t * (1.0 - i_t)
        dzab = _c(dza)
        dzxb = _c(dzx)
        dxc = dxc + _dot_nt(dzxb, wx_ref[0]) + _dot_nt(dzab, wa_ref[0])
        dwx_ref[0] += _dot_tn(xcb, dzxb)
        dwa_ref[0] += _dot_tn(xcb, dzab)
        dbx_ref[...] += jnp.sum(dzx, axis=0, keepdims=True)
        dba_ref[...] += jnp.sum(dza, axis=0, keepdims=True)

        dcb_ref[...] += jnp.sum(dxc, axis=0, keepdims=True)
        dcw_ref[3:4, :] += jnp.sum(dxc * xa, axis=0, keepdims=True)
        dcw_ref[2:3, :] += jnp.sum(dxc * ext_ref[7:7 + tc, :], axis=0, keepdims=True)
        dcw_ref[1:2, :] += jnp.sum(dxc * ext_ref[6:6 + tc, :], axis=0, keepdims=True)
        dcw_ref[0:1, :] += jnp.sum(dxc * ext_ref[5:5 + tc, :], axis=0, keepdims=True)
        ext2_ref[0:tc, :] = dxc
        ext2_ref[tc:tc + 8, :] = dhalo_ref[...]
        dxa = (cw_ref[3:4, :] * dxc + cw_ref[2:3, :] * ext2_ref[1:1 + tc, :]
               + cw_ref[1:2, :] * ext2_ref[2:2 + tc, :] + cw_ref[0:1, :] * ext2_ref[3:3 + tc, :])
        dxa_ref[...] = dxa.astype(dxa_ref.dtype)
        dhalo_ref[...] = ext2_ref[0:8, :]

    row_of = lambda b, t: b * nt + (nt - 1 - t)
    tile = lambda off: pl.BlockSpec((tc, CW), lambda c, b, t: (row_of(b, t), off + c))
    halo = pl.BlockSpec((8, CW), lambda c, b, t: (jnp.maximum(row_of(b, t) * h8 - 1, 0), c))
    vec = pl.BlockSpec((1, CW), lambda c, b, t: (0, c))
    mat = pl.BlockSpec((1, CW, CW), lambda c, b, t: (c, 0, 0))
    cwspec = pl.BlockSpec((CONV, CW), lambda c, b, t: (0, c))
    return pl.pallas_call(
        _after(body, 13, deps),
        name="lru_bwd",
        grid=(N_CT, B, nt),
        in_specs=[tile(0), tile(0), halo, tile(N_CT), tile(0), halo, cwspec, vec, mat, mat, vec, vec, vec]
        + [ANY_SPEC] * len(deps),
        out_specs=[tile(0), tile(0), cwspec, vec, mat, mat, vec, vec, vec],
        out_shape=[
            jax.ShapeDtypeStruct((T, D_MODEL), _MXU),
            jax.ShapeDtypeStruct((T, D_MODEL), _MXU),
            jax.ShapeDtypeStruct((CONV, D_MODEL), F32),
            jax.ShapeDtypeStruct((1, D_MODEL), F32),
            jax.ShapeDtypeStruct((N_CT, CW, CW), F32),
            jax.ShapeDtypeStruct((N_CT, CW, CW), F32),
            jax.ShapeDtypeStruct((1, D_MODEL), F32),
            jax.ShapeDtypeStruct((1, D_MODEL), F32),
            jax.ShapeDtypeStruct((1, D_MODEL), F32),
        ],
        scratch_shapes=[pltpu.VMEM((tc + 8, CW), F32), pltpu.VMEM((tc + 8, CW), F32),
                        pltpu.VMEM((8, CW), F32), pltpu.VMEM((8, CW), F32)],
        compiler_params=_params(("parallel", "arbitrary", "arbitrary")),
    )(dya, proj, proj, proj, hlru, hlru, conv_w, conv_b, wx_bd, wa_bd, bx, ba, lam, *deps)


def _retention_tables(S):
    half = DK // 2
    freqs = ROPE_THETA ** (-jnp.arange(half, dtype=F32) / half)
    ang = jnp.arange(S, dtype=F32)[:, None] * freqs[None, :]
    log_g = jnp.log1p(-(2.0 ** (-5.0 - jnp.arange(HEADS, dtype=F32))))
    idx = jnp.arange(CHUNK, dtype=F32)
    diff = idx[:, None] - idx[None, :]
    inner = jnp.where(diff >= 0, jnp.exp(jnp.maximum(diff, 0.0)[None] * log_g[:, None, None]), 0.0)
    cross = jnp.exp((idx[None, :] + 1.0) * log_g[:, None])[:, :, None]
    state = jnp.exp((CHUNK - 1.0 - idx[None, :]) * log_g[:, None])[:, :, None]
    gam = jnp.broadcast_to(jnp.exp(CHUNK * log_g)[:, None, None], (HEADS, 1, DK))
    return jnp.cos(ang), jnp.sin(ang), inner, cross, state, gam


def _rot(x, cos, sin):
    half = DK // 2
    x1, x2 = x[:, :half], x[:, half:]
    return jnp.concatenate([x1 * cos - x2 * sin, x1 * sin + x2 * cos], axis=-1)


def _rot_t(y, cos, sin):
    half = DK // 2
    y1, y2 = y[:, :half], y[:, half:]
    return jnp.concatenate([y1 * cos + y2 * sin, y2 * cos - y1 * sin], axis=-1)


def _groupnorm(o):
    mu = jnp.mean(o, axis=-1, keepdims=True)
    oc = o - mu
    rs = lax.rsqrt(jnp.mean(oc * oc, axis=-1, keepdims=True) + EPS)
    return oc * rs, rs


def _ret_specs(S, chunk_of):
    nc = S // CHUNK
    qkv = lambda g: pl.BlockSpec((CHUNK, D_MODEL), lambda b, c: (b * nc + chunk_of(c), g))
    act = pl.BlockSpec((CHUNK, D_MODEL), lambda b, c: (b * nc + chunk_of(c), 0))
    rope = pl.BlockSpec((CHUNK, DK // 2), lambda b, c: (chunk_of(c), 0))
    dmat = pl.BlockSpec((HEADS, CHUNK, CHUNK), lambda b, c: (0, 0, 0))
    dvec = pl.BlockSpec((HEADS, CHUNK, 1), lambda b, c: (0, 0, 0))
    hrow = pl.BlockSpec((HEADS, 1, DK), lambda b, c: (0, 0, 0))
    rst = pl.BlockSpec((1, HEADS, DK, DK), lambda b, c: (b * nc + chunk_of(c), 0, 0, 0))
    return qkv, act, rope, dmat, dvec, hrow, rst


def _ret_fwd(proj, tables, gain3, B, S):
    T = B * S
    nc = S // CHUNK
    cos, sin, dmat_t, cd_t, sd_t, gam_t = tables

    def body(q_ref, k_ref, v_ref, gb_ref, cos_ref, sin_ref, dm_ref, cd_ref, sd_ref, gam_ref, gain_ref,
             o_ref, yb_ref, rs_ref, state_ref):
        @pl.when(pl.program_id(1) == 0)
        def _():
            state_ref[...] = jnp.zeros_like(state_ref)

        cos_t, sin_t = cos_ref[...], sin_ref[...]
        for h in range(HEADS):
            cols = slice(h * DK, (h + 1) * DK)
            qb = _c(_rot(q_ref[:, cols], cos_t, sin_t))
            kb = _c(_rot(k_ref[:, cols], cos_t, sin_t) * (DK ** -0.5))
            v = v_ref[:, cols]
            state = state_ref[h]
            sb = _c(state)
            rs_ref[0, h] = sb
            scores = _dot_nt(qb, kb) * dm_ref[h]
            o = _dot(_c(scores), _c(v)) + _dot(qb, sb) * cd_ref[h]
            state_ref[h] = gam_ref[h] * state + _dot_tn(kb, _c(v * sd_ref[h]))
            o_ref[:, cols] = o
            n, _ = _groupnorm(o)
            gb = gb_ref[:, cols]
            yb_ref[:, cols] = (gb * _sigmoid(gb) * (n * gain_ref[h])).astype(yb_ref.dtype)

    qkv, act, rope, dmat, dvec, hrow, rst = _ret_specs(S, lambda c: c)
    return pl.pallas_call(
        body,
        name="ret_fwd",
        grid=(B, nc),
        in_specs=[qkv(2), qkv(3), qkv(4), qkv(5), rope, rope, dmat, dvec, dvec, hrow, hrow],
        out_specs=[act, act, rst],
        out_shape=[
            jax.ShapeDtypeStruct((T, D_MODEL), F32),
            jax.ShapeDtypeStruct((T, D_MODEL), _MXU),
            jax.ShapeDtypeStruct((B * nc, HEADS, DK, DK), _MXU),
        ],
        scratch_shapes=[pltpu.VMEM((HEADS, DK, DK), F32)],
        compiler_params=_params(("parallel", "arbitrary")),
    )(proj, proj, proj, proj, cos, sin, dmat_t, cd_t, sd_t, gam_t, gain3)


def _ret_bwd(dyb, o_pre, proj, states, tables, gain3, B, S, deps=()):
    T = B * S
    nc = S // CHUNK
    cos, sin, dmat_t, cd_t, sd_t, gam_t = tables

    def body(dyb_ref, o_ref, q_ref, k_ref, v_ref, gb_ref, rs_ref, cos_ref, sin_ref, dm_ref, cd_ref, sd_ref, gam_ref,
             gain_ref, dr_ref, dgain_ref, dstate_ref):
        @pl.when(pl.program_id(1) == 0)
        def _():
            dstate_ref[...] = jnp.zeros_like(dstate_ref)

        @pl.when((pl.program_id(1) == 0) & (pl.program_id(0) == 0))
        def _():
            dgain_ref[...] = jnp.zeros_like(dgain_ref)

        cos_t, sin_t = cos_ref[...], sin_ref[...]
        for h in range(HEADS):
            cols = slice(h * DK, (h + 1) * DK)
            gain = gain_ref[h]
            n, rs = _groupnorm(o_ref[:, cols])
            gb = gb_ref[:, cols]
            sg = _sigmoid(gb)
            dy = dyb_ref[:, cols]
            part = lambda g: slice(g * D_MODEL + h * DK, g * D_MODEL + (h + 1) * DK)
            dr_ref[:, part(3)] = (dy * (n * gain) * (sg * (1.0 + gb * (1.0 - sg)))).astype(dr_ref.dtype)
            dgn = dy * (gb * sg)
            dgain_ref[h] += jnp.sum(dgn * n, axis=0, keepdims=True)
            dn = dgn * gain
            do = rs * (dn - jnp.mean(dn, axis=-1, keepdims=True) - n * jnp.mean(dn * n, axis=-1, keepdims=True))

            qb = _c(_rot(q_ref[:, cols], cos_t, sin_t))
            kb = _c(_rot(k_ref[:, cols], cos_t, sin_t) * (DK ** -0.5))
            v = v_ref[:, cols]
            vb = _c(v)
            vsb = _c(v * sd_ref[h])
            dob = _c(do)
            docb = _c(do * cd_ref[h])
            dmat = dm_ref[h]
            dstate = dstate_ref[h]
            dsb = _c(dstate)
            pb = _c(_dot_nt(qb, kb) * dmat)
            dsc = _c(_dot_nt(dob, vb) * dmat)
            dq = _dot(dsc, kb) + _dot_nt(docb, rs_ref[0, h])
            dk = _dot_tn(dsc, qb) + _dot_nt(vsb, dsb)
            dv = _dot_tn(pb, dob) + _dot(kb, dsb) * sd_ref[h]
            dstate_ref[h] = gam_ref[h] * dstate + _dot_tn(qb, docb)
            dr_ref[:, part(0)] = _rot_t(dq, cos_t, sin_t).astype(dr_ref.dtype)
            dr_ref[:, part(1)] = (_rot_t(dk, cos_t, sin_t) * (DK ** -0.5)).astype(dr_ref.dtype)
            dr_ref[:, part(2)] = dv.astype(dr_ref.dtype)

    qkv, act, rope, dmat, dvec, hrow, rst = _ret_specs(S, lambda c: nc - 1 - c)
    wide = pl.BlockSpec((CHUNK, 4 * D_MODEL), lambda b, c: (b * nc + nc - 1 - c, 0))
    return pl.pallas_call(
        _after(body, 14, deps),
        name="ret_bwd",
        grid=(B, nc),
        in_specs=[act, act, qkv(2), qkv(3), qkv(4), qkv(5), rst, rope, rope, dmat, dvec, dvec, hrow, hrow]
        + [ANY_SPEC] * len(deps),
        out_specs=[wide, hrow],
        out_shape=[jax.ShapeDtypeStruct((T, 4 * D_MODEL), _MXU), jax.ShapeDtypeStruct((HEADS, 1, DK), F32)],
        scratch_shapes=[pltpu.VMEM((HEADS, DK, DK), F32)],
        compiler_params=_params(("arbitrary", "arbitrary")),
    )(dyb, o_pre, proj, proj, proj, proj, states, cos, sin, dmat_t, cd_t, sd_t, gam_t, gain3, *deps)


def _mid(ya, yb, proj, x2d, tgt2d, wpa, wpb, wout, g_fin):
    T = x2d.shape[0]
    tm = min(256, T)
    n_steps = T // tm
    rows = D_MODEL // (2 * N_CHIPS)

    def body(ya_ref, yb_ref, ma_ref, mb_ref, x_ref, t_ref, gf_ref, wpa_hbm, wpb_hbm, wout_hbm,
             loss_ref, dx2_ref, dya_ref, dyb_ref, dm_ref, dgf_ref, gw_hbm, w_ref, acc_ref, sem):
        i = pl.program_id(0)

        @pl.when(i == 0)
        def _():
            loads = [pltpu.make_async_copy(src, w_ref.at[k], sem.at[k]) for k, src in enumerate((wpa_hbm, wpb_hbm, wout_hbm))]
            for cp in loads:
                cp.start()
            for cp in loads:
                cp.wait()
            acc_ref[...] = jnp.zeros_like(acc_ref)
            loss_ref[...] = jnp.zeros_like(loss_ref)
            dgf_ref[...] = jnp.zeros_like(dgf_ref)

        ya_t, yb_t = ya_ref[...], yb_ref[...]
        out_a = _dot(ya_t, w_ref[0])
        out_b = _dot(yb_t, w_ref[1])
        sa = _sigmoid(ma_ref[...])
        sb = _sigmoid(mb_ref[...])
        mgb = _c(sa * out_a + sb * out_b)
        x2 = x_ref[...] + _dot(mgb, w_ref[2])
        r2 = lax.rsqrt(jnp.mean(x2 * x2, axis=-1, keepdims=True) + EPS)
        nx = x2 * r2
        gf = gf_ref[...]
        err = nx * gf - t_ref[...]
        loss_ref[...] += 0.5 * jnp.sum(jnp.mean(err * err, axis=-1, keepdims=True), axis=0, keepdims=True)
        dy = err * (1.0 / D_MODEL)
        dgf_ref[...] += jnp.sum(dy * nx, axis=0, keepdims=True)
        dyg = dy * gf
        dx2 = r2 * (dyg - nx * jnp.mean(dyg * nx, axis=-1, keepdims=True))
        dx2_ref[...] = dx2
        dx2b = _c(dx2)
        dmg = _dot_nt(dx2b, w_ref[2])
        acc_ref[2] += _dot_tn(mgb, dx2b)
        dm_ref[:, :D_MODEL] = (dmg * out_a * sa * (1.0 - sa)).astype(dm_ref.dtype)
        dm_ref[:, D_MODEL:] = (dmg * out_b * sb * (1.0 - sb)).astype(dm_ref.dtype)
        dab = _c(dmg * sa)
        dbb = _c(dmg * sb)
        dya_ref[...] = _dot_nt(dab, w_ref[0])
        dyb_ref[...] = _dot_nt(dbb, w_ref[1])
        acc_ref[0] += _dot_tn(ya_t, dab)
        acc_ref[1] += _dot_tn(yb_t, dbb)

        @pl.when(i == n_steps - 1)
        def _():
            copies = [pltpu.make_async_copy(acc_ref.at[k, pl.ds((2 * p + hf) * rows, rows), :], gw_hbm.at[p, hf, k],
                                            sem.at[(k * N_CHIPS + p) * 2 + hf])
                      for k in range(3) for p in range(N_CHIPS) for hf in range(2)]
            for cp in copies:
                cp.start()
            for cp in copies:
                cp.wait()

    tile = lambda j: pl.BlockSpec((tm, D_MODEL), lambda i: (i, j))
    one = pl.BlockSpec((1, D_MODEL), lambda i: (0, 0))
    anyspec = pl.BlockSpec(memory_space=pl.ANY)
    return pl.pallas_call(
        body,
        name="mid",
        grid=(n_steps,),
        in_specs=[tile(0), tile(0), tile(6), tile(7), tile(0), tile(0), one, anyspec, anyspec, anyspec],
        out_specs=[pl.BlockSpec((1, 1), lambda i: (0, 0)), tile(0), tile(0), tile(0),
                   pl.BlockSpec((tm, 2 * D_MODEL), lambda i: (i, 0)), one, anyspec],
        out_shape=[
            jax.ShapeDtypeStruct((1, 1), F32),
            jax.ShapeDtypeStruct((T, D_MODEL), F32),
            jax.ShapeDtypeStruct((T, D_MODEL), F32),
            jax.ShapeDtypeStruct((T, D_MODEL), F32),
            jax.ShapeDtypeStruct((T, 2 * D_MODEL), _MXU),
            jax.ShapeDtypeStruct((1, D_MODEL), F32),
            jax.ShapeDtypeStruct((N_CHIPS, 2, 3, rows, D_MODEL), F32),
        ],
        scratch_shapes=[pltpu.VMEM((3, D_MODEL, D_MODEL), _MXU), pltpu.VMEM((3, D_MODEL, D_MODEL), F32),
                        pltpu.SemaphoreType.DMA((3 * N_CHIPS * 2,))],
        compiler_params=_params(("arbitrary",)),
    )(ya, yb, proj, proj, x2d, tgt2d, g_fin, wpa, wpb, wout)


def _inproj_bwd_dx(dparts, w_all, x2d, dx2, g_in, deps=()):
    T = x2d.shape[0]
    tm = min(512, T)
    n_d = len(dparts)
    groups = [(a, k) for a, d in enumerate(dparts) for k in range(d.shape[1] // D_MODEL)]

    def body(*refs):
        d_refs = refs[:n_d]
        x_ref, dx2_ref, g_ref, w_hbm = refs[n_d:n_d + 4]
        dx_ref, dg_ref, w_ref, sem = refs[-4:]

        @pl.when(pl.program_id(0) == 0)
        def _():
            cp = pltpu.make_async_copy(w_hbm, w_ref, sem)
            cp.start()
            cp.wait()
            dg_ref[...] = jnp.zeros_like(dg_ref)

        dh = jnp.zeros((tm, D_MODEL), F32)
        for j, (a, k) in enumerate(groups):
            dh = dh + _dot_nt(d_refs[a][:, k * D_MODEL:(k + 1) * D_MODEL],
                              w_ref[j // 2, :, (j % 2) * D_MODEL:(j % 2 + 1) * D_MODEL])
        x = x_ref[...]
        r = lax.rsqrt(jnp.mean(x * x, axis=-1, keepdims=True) + EPS)
        nx = x * r
        dg_ref[...] += jnp.sum(dh * nx, axis=0, keepdims=True)
        dhg = dh * g_ref[...]
        dx_ref[...] = dx2_ref[...] + r * (dhg - nx * jnp.mean(dhg * nx, axis=-1, keepdims=True))

    tile = pl.BlockSpec((tm, D_MODEL), lambda i: (i, 0))
    one = pl.BlockSpec((1, D_MODEL), lambda i: (0, 0))
    return pl.pallas_call(
        body,
        name="inproj_bwd_dx",
        grid=(T // tm,),
        in_specs=[pl.BlockSpec((tm, d.shape[1]), lambda i: (i, 0)) for d in dparts]
        + [tile, tile, one, ANY_SPEC] + [ANY_SPEC] * len(deps),
        out_specs=[tile, one],
        out_shape=[jax.ShapeDtypeStruct((T, D_MODEL), F32), jax.ShapeDtypeStruct((1, D_MODEL), F32)],
        scratch_shapes=[pltpu.VMEM(w_all.shape, w_all.dtype), pltpu.SemaphoreType.DMA],
        compiler_params=_params(("arbitrary",)),
    )(*dparts, x2d, dx2, g_in, w_all, *deps)


def _inproj_bwd_dw(ht, dparts, name, deps=()):
    T = ht.shape[1]
    tn = 512
    half = D_MODEL // 2
    per_chip = 2 * D_MODEL // tn
    n_d = len(dparts)
    tiles = [(a, t) for a, d in enumerate(dparts) for t in range(d.shape[1] // tn)]
    offs = [sum(d.shape[1] // tn for d in dparts[:a]) for a in range(n_d)]

    def body(*refs):
        ht_ref = refs[0]
        d_refs = refs[1:1 + n_d]
        out_ref = refs[-1]
        t = pl.program_id(0)

        for a in range(n_d):
            lo, hi = offs[a], offs[a] + dparts[a].shape[1] // tn

            @pl.when((t >= lo) & (t < hi))
            def _(a=a):
                g = _dot(ht_ref[...], d_refs[a][...])
                out_ref[0, 0] = g[:half]
                out_ref[0, 1] = g[half:]

    def dspec(a):
        n_a = dparts[a].shape[1] // tn
        return pl.BlockSpec((T, tn), lambda t: (0, jnp.clip(t - offs[a], 0, n_a - 1)))

    return pl.pallas_call(
        body,
        name=name,
        grid=(len(tiles),),
        in_specs=[pl.BlockSpec((D_MODEL, T), lambda t: (0, 0))] + [dspec(a) for a in range(n_d)]
        + [ANY_SPEC] * len(deps),
        out_specs=pl.BlockSpec((1, 2, half, tn), lambda t: (t // per_chip, 0, 0, t % per_chip)),
        out_shape=jax.ShapeDtypeStruct((len(tiles) // per_chip, 2, half, 2 * D_MODEL), F32),
        compiler_params=_params(("parallel",)),
    )(ht, *dparts, *deps)


def _coords():
    return lax.axis_index("x"), lax.axis_index("y"), lax.axis_index("c")


def _other_chips(x, y):
    return [(1 - x, y), (x, 1 - y), (1 - x, 1 - y)]


def _all_gather8(xs, name, deps=()):
    m_per, n = xs.shape

    def body(x_ref, *rest):
        out_ref, send_sems, recv_sems, local_sem = rest[-4:]
        x, y, c = _coords()
        me, sibling = (x, y, c), (x, y, 1 - c)
        chips = _other_chips(x, y)

        def rows(px, py, pc):
            return out_ref.at[pl.ds((4 * px + 2 * py + pc) * m_per, m_per), :]

        def copy(k, block, to, src=None):
            return pltpu.make_async_remote_copy(
                src_ref=rows(*block) if src is None else src, dst_ref=rows(*block),
                send_sem=send_sems.at[k], recv_sem=recv_sems.at[k], device_id=to, device_id_type=MESH)

        mine = pltpu.make_async_copy(x_ref, rows(*me), local_sem)
        mine.start()
        first = [copy(0, me, sibling, src=x_ref)]
        first += [copy(1 + j, me, (*chip, c), src=x_ref) for j, chip in enumerate(chips)]
        for cp in first:
            cp.start()
        passed = [copy(4 + j, (*chip, c), sibling) for j, chip in enumerate(chips)]
        for j, chip in enumerate(chips):
            copy(1 + j, (*chip, c), me).wait_recv()
            passed[j].start()
        copy(0, sibling, me).wait_recv()
        for j, chip in enumerate(chips):
            copy(4 + j, (*chip, 1 - c), me).wait_recv()
        for cp in first + passed:
            cp.wait_send()
        mine.wait()

    return pl.pallas_call(
        body,
        name=name,
        out_shape=jax.ShapeDtypeStruct((8 * m_per, n), xs.dtype),
        in_specs=[pl.BlockSpec(memory_space=pltpu.VMEM)] + [ANY_SPEC] * len(deps),
        out_specs=pl.BlockSpec(memory_space=pltpu.VMEM),
        scratch_shapes=[pltpu.SemaphoreType.DMA((7,)), pltpu.SemaphoreType.DMA((7,)), pltpu.SemaphoreType.DMA],
        compiler_params=pltpu.CompilerParams(vmem_limit_bytes=VMEM_LIMIT),
    )(xs, *deps)


def _chunks(rows, n):
    size = rows // n
    return [pl.ds(q * size, size) for q in range(n)]


HBM_SPEC = pl.BlockSpec(memory_space=pltpu.HBM)
SEM_SPEC = pl.BlockSpec(memory_space=pltpu.SEMAPHORE)
DATAFLOW = pltpu.SideEffectType.DATAFLOW_SIDE_EFFECTING


def _copies_start(bufs, plan, n_copies, name):
    n = len(bufs)

    def body(*refs):
        ins = refs[:n]
        send_sems, recv_sems = refs[n], refs[n + 1]
        token = refs[-1]
        for k, send, _ in plan(ins):
            if send is not None:
                src, dst, dev, pred = send
                cp = pltpu.make_async_remote_copy(src_ref=src, dst_ref=dst, send_sem=send_sems.at[k],
                                                  recv_sem=recv_sems.at[k], device_id=dev, device_id_type=MESH)
                if pred is None:
                    cp.start()
                else:
                    pl.when(pred)(cp.start)
        token[...] = jnp.zeros_like(token)

    hbm = [pltpu.with_memory_space_constraint(b, pltpu.HBM) for b in bufs]
    outs = pl.pallas_call(
        body,
        name=name,
        in_specs=[HBM_SPEC] * n,
        out_specs=(SEM_SPEC, SEM_SPEC, *([HBM_SPEC] * n), pl.BlockSpec(memory_space=pltpu.VMEM)),
        out_shape=(pltpu.SemaphoreType.DMA((n_copies,)), pltpu.SemaphoreType.DMA((n_copies,)),
                   *[pltpu.HBM(b.shape, b.dtype) for b in bufs], jax.ShapeDtypeStruct((8, 128), F32)),
        input_output_aliases={a: 2 + a for a in range(n)},
        compiler_params=pltpu.CompilerParams(has_side_effects=DATAFLOW),
    )(*hbm)
    return outs[0], outs[1], list(outs[2:2 + n]), outs[-1]


def _copies_wait(send_sems, recv_sems, bufs, after, plan, name):
    n = len(bufs)

    def body(*refs):
        ins = refs[:n]
        s_sems, r_sems = refs[n], refs[n + 1]
        for k, send, recv in plan(ins):
            if send is not None:
                src, dst, dev, pred = send
                cp = pltpu.make_async_remote_copy(src_ref=src, dst_ref=dst, send_sem=s_sems.at[k],
                                                  recv_sem=r_sems.at[k], device_id=dev, device_id_type=MESH)
                if pred is None:
                    cp.wait_send()
                else:
                    pl.when(pred)(cp.wait_send)
            if recv is not None:
                dst, pred = recv
                cp = pltpu.make_async_remote_copy(src_ref=dst, dst_ref=dst, send_sem=s_sems.at[k],
                                                  recv_sem=r_sems.at[k], device_id=_coords(), device_id_type=MESH)
                if pred is None:
                    cp.wait_recv()
                else:
                    pl.when(pred)(cp.wait_recv)

    outs = pl.pallas_call(
        body,
        name=name,
        in_specs=[HBM_SPEC] * n + [SEM_SPEC, SEM_SPEC, pl.BlockSpec(memory_space=pl.ANY)],
        out_specs=[HBM_SPEC] * n,
        out_shape=[pltpu.HBM(b.shape, b.dtype) for b in bufs],
        input_output_aliases={a: a for a in range(n)},
        compiler_params=pltpu.CompilerParams(has_side_effects=DATAFLOW),
    )(*bufs, send_sems, recv_sems, after)
    return list(outs)


def _gather_plan(n_bufs):
    def plan(refs):
        x, y, c = _coords()
        me = 2 * x + y
        out = []
        for k, (px, py) in enumerate(_other_chips(x, y)):
            for a in range(n_bufs):
                out.append((k * n_bufs + a, (refs[a].at[me], refs[a].at[me], (px, py, c), None),
                            (refs[a].at[2 * px + py], None)))
        return out
    return plan


def _cast_into_slot(ws, name):
    n = len(ws)
    nt = 2

    def body(s_ref, *refs):
        for a in range(n):
            refs[n + a][0] = refs[a][...].astype(refs[n + a].dtype)

    xi, yi, _ = _coords()
    return pl.pallas_call(
        body,
        name=name,
        grid_spec=pltpu.PrefetchScalarGridSpec(
            num_scalar_prefetch=1,
            grid=(2, nt),
            in_specs=[pl.BlockSpec((1, w.shape[1] // nt, w.shape[2]), lambda hf, i, s: (hf, i, 0)) for w in ws],
            out_specs=[pl.BlockSpec((1, 1, w.shape[1] // nt, w.shape[2]), lambda hf, i, s: (s[0], hf, i, 0)) for w in ws],
        ),
        out_shape=[jax.ShapeDtypeStruct((N_CHIPS,) + w.shape, _MXU) for w in ws],
        compiler_params=_params(("parallel", "parallel")),
    )((2 * xi + yi).reshape(1).astype(jnp.int32), *ws)


def _gather_chips(bufs, n_chunks, name):
    n = len(bufs)
    pieces = [(a, rows) for a in range(n) for rows in _chunks(bufs[a].shape[2], n_chunks[a])]
    n_p = len(pieces)

    def body(*refs):
        outs = refs[n:2 * n]
        send_sems, recv_sems, fsend_sems, frecv_sems = refs[2 * n:]
        x, y, c = _coords()
        me = 2 * x + y
        chips = _other_chips(x, y)

        def send(k, i, slot, chip):
            a, rows = pieces[i]
            return pltpu.make_async_remote_copy(
                src_ref=outs[a].at[slot, c, rows], dst_ref=outs[a].at[slot, c, rows], send_sem=send_sems.at[k * n_p + i],
                recv_sem=recv_sems.at[k * n_p + i], device_id=(*chip, c), device_id_type=MESH)

        def forward(k, i, slot, half):
            a, rows = pieces[i]
            return pltpu.make_async_remote_copy(
                src_ref=outs[a].at[slot, half, rows], dst_ref=outs[a].at[slot, half, rows],
                send_sem=fsend_sems.at[k * n_p + i], recv_sem=frecv_sems.at[k * n_p + i],
                device_id=(x, y, 1 - c), device_id_type=MESH)

        sends = [send(k, i, me, chip) for i in range(n_p) for k, chip in enumerate(chips)]
        for cp in sends:
            cp.start()
        forwards = []
        for i in range(n_p):
            for k, (px, py) in enumerate(chips):
                send(k, i, 2 * px + py, (px, py)).wait_recv()
                fw = forward(k, i, 2 * px + py, c)
                fw.start()
                forwards.append(fw)
        for i in range(n_p):
            for k, (px, py) in enumerate(chips):
                forward(k, i, 2 * px + py, 1 - c).wait_recv()
        for cp in sends + forwards:
            cp.wait_send()

    anyspec = pl.BlockSpec(memory_space=pl.ANY)
    sems = pltpu.SemaphoreType.DMA((3 * n_p,))
    return pl.pallas_call(
        body,
        name=name,
        in_specs=[anyspec] * n,
        out_specs=[anyspec] * n,
        out_shape=[jax.ShapeDtypeStruct(b.shape, b.dtype) for b in bufs],
        input_output_aliases={a: a for a in range(n)},
        scratch_shapes=[sems, sems, sems, sems],
    )(*bufs)


def _swap_plan(n_slabs):
    def plan(refs):
        x, y, c = _coords()
        out, k = [], 0
        for i, n in enumerate(n_slabs):
            g, land = refs[2 * i], refs[2 * i + 1]
            for p in range(n):
                out.append((k, (g.at[p, 1 - c], land.at[p], (x, y, 1 - c), None), (land.at[p], None)))
                k += 1
        return out
    return plan


def _is_one_of(chip, dests):
    hit = chip == dests[0]
    for d in dests[1:]:
        hit = hit | (chip == d)
    return hit


def _slab_of(chip, dests):
    return sum(j * (chip == d).astype(jnp.int32) for j, d in enumerate(dests))


def _scatter_plan(dest_sets):
    def plan(refs):
        x, y, c = _coords()
        me = 2 * x + y
        out = []
        for k, (px, py) in enumerate(_other_chips(x, y)):
            peer = 2 * px + py
            for i, dests in enumerate(dest_sets):
                cs, land = refs[2 * i], refs[2 * i + 1]
                everyone = len(dests) == N_CHIPS
                send = (cs.at[_slab_of(peer, dests)], land.at[k], (px, py, c),
                        None if everyone else _is_one_of(peer, dests))
                recv = (land.at[k], None if everyone else _is_one_of(me, dests))
                out.append((k * len(dest_sets) + i, send, recv))
        return out
    return plan


def _join_halves(bufs, n_chunks, name):
    n = len(bufs)
    pieces = [(a, rows) for a in range(n) for rows in _chunks(bufs[a].shape[1], n_chunks[a])]
    n_p = len(pieces)

    def body(*refs):
        outs = refs[n:2 * n]
        send_sems, recv_sems = refs[2 * n:]
        x, y, c = _coords()

        def copy(i, half):
            a, rows = pieces[i]
            return pltpu.make_async_remote_copy(
                src_ref=outs[a].at[half, rows], dst_ref=outs[a].at[half, rows], send_sem=send_sems.at[i],
                recv_sem=recv_sems.at[i], device_id=(x, y, 1 - c), device_id_type=MESH)

        sends = [copy(i, c) for i in range(n_p)]
        for cp in sends:
            cp.start()
        for i in range(n_p):
            copy(i, 1 - c).wait_recv()
        for cp in sends:
            cp.wait_send()

    anyspec = pl.BlockSpec(memory_space=pl.ANY)
    sems = pltpu.SemaphoreType.DMA((n_p,))
    return pl.pallas_call(
        body,
        name=name,
        in_specs=[anyspec] * n,
        out_specs=[anyspec] * n,
        out_shape=[jax.ShapeDtypeStruct(b.shape, b.dtype) for b in bufs],
        input_output_aliases={a: a for a in range(n)},
        scratch_shapes=[sems, sems],
    )(*bufs)


def _row_tile(rows, cap):
    t = cap
    while rows % t:
        t //= 2
    return t


def _add_my_half(g, r, name):
    n_slabs, _, R, C = g.shape
    tr = _row_tile(R, 256)

    def body(c_ref, g_ref, r_ref, o_ref):
        o_ref[...] = (g_ref[0] + r_ref[...]).astype(o_ref.dtype)

    return pl.pallas_call(
        body,
        name=name,
        grid_spec=pltpu.PrefetchScalarGridSpec(
            num_scalar_prefetch=1,
            grid=(n_slabs, R // tr),
            in_specs=[pl.BlockSpec((1, 1, tr, C), lambda p, i, c_ref: (p, c_ref[0], i, 0)),
                      pl.BlockSpec((1, tr, C), lambda p, i, c_ref: (p, i, 0))],
            out_specs=pl.BlockSpec((1, tr, C), lambda p, i, c_ref: (p, i, 0)),
        ),
        out_shape=jax.ShapeDtypeStruct(r.shape, jnp.bfloat16),
        compiler_params=_params(("parallel", "parallel")),
    )(lax.axis_index("c").reshape(1).astype(jnp.int32), g, r)


def _sum_slabs(own, got, name):
    _, R, C = own.shape
    tr = _row_tile(R, 256)

    def body(s_ref, own_ref, got_ref, o_ref):
        o_ref[0] = ((own_ref[0].astype(F32) + got_ref[0].astype(F32)) + got_ref[1].astype(F32)) + got_ref[2].astype(F32)

    xi, yi, ci = _coords()
    return pl.pallas_call(
        body,
        name=name,
        grid_spec=pltpu.PrefetchScalarGridSpec(
            num_scalar_prefetch=1,
            grid=(R // tr,),
            in_specs=[pl.BlockSpec((1, tr, C), lambda i, s: (s[0], i, 0)),
                      pl.BlockSpec((3, tr, C), lambda i, s: (0, i, 0))],
            out_specs=pl.BlockSpec((1, tr, C), lambda i, s: (s[1], i, 0)),
        ),
        out_shape=jax.ShapeDtypeStruct((2, R, C), F32),
        compiler_params=_params(("parallel",)),
    )(jnp.stack([2 * xi + yi, ci]).astype(jnp.int32), own, got)


def _sum_parts(owns, gots, dest_sets, name):
    n = len(owns)
    _, R, C = owns[0].shape
    tr = _row_tile(R, 256)

    def body(s_ref, *refs):
        o_ref = refs[-1]
        total = jnp.zeros((tr, C), F32)
        for i in range(n):
            total = total + jnp.where(s_ref[2 + 2 * i] == 1, refs[i][0].astype(F32), 0.0)
        for i in range(n):
            got = refs[n + i]
            total = ((total + got[0].astype(F32)) + got[1].astype(F32)) + got[2].astype(F32)
        o_ref[0] = total

    xi, yi, ci = _coords()
    me = 2 * xi + yi
    scalars = [ci, ci]
    for dests in dest_sets:
        scalars += [_is_one_of(me, dests).astype(jnp.int32), _slab_of(me, dests)]
    own_spec = lambda i: pl.BlockSpec((1, tr, C), lambda r, s: (s[3 + 2 * i], r, 0))
    return pl.pallas_call(
        body,
        name=name,
        grid_spec=pltpu.PrefetchScalarGridSpec(
            num_scalar_prefetch=1,
            grid=(R // tr,),
            in_specs=[own_spec(i) for i in range(n)] + [pl.BlockSpec((3, tr, C), lambda r, s: (0, r, 0))] * n,
            out_specs=pl.BlockSpec((1, tr, C), lambda r, s: (s[0], r, 0)),
        ),
        out_shape=jax.ShapeDtypeStruct((2, R, C), F32),
        compiler_params=_params(("parallel",)),
    )(jnp.stack(scalars).astype(jnp.int32), *owns, *gots)


def _sum_rows8(g, m_per, name):
    n = g.shape[1]

    def body(g_ref, o_ref):
        acc = g_ref[0:m_per, :]
        for k in range(1, 8):
            acc = acc + g_ref[k * m_per:(k + 1) * m_per, :]
        o_ref[...] = acc

    return pl.pallas_call(
        body,
        name=name,
        out_shape=jax.ShapeDtypeStruct((m_per, n), F32),
        compiler_params=_params(),
    )(g)


def _adamw_math(w, g, m, v):
    m = ADAM_B1 * m + (1.0 - ADAM_B1) * g
    v = ADAM_B2 * v + (1.0 - ADAM_B2) * (g * g)
    m_hat = m / (1.0 - ADAM_B1 ** ADAM_STEP)
    v_hat = v / (1.0 - ADAM_B2 ** ADAM_STEP)
    delta = -ADAM_LR * (m_hat / (jnp.sqrt(v_hat) + ADAM_EPS) + ADAM_WD * w)
    return delta, m, v


def _adamw_big(w, g, m, v, name):
    R, C = w.shape
    tr = min(128, R)

    def body(w_ref, g_ref, m_ref, v_ref, d_out, m_out, v_out):
        d, mn, vn = _adamw_math(w_ref[...], g_ref[...], m_ref[...], v_ref[...])
        d_out[...] = d
        m_out[...] = mn
        v_out[...] = vn

    spec = pl.BlockSpec((tr, C), lambda i: (i, 0))
    return pl.pallas_call(
        body,
        name=name,
        grid=(R // tr,),
        in_specs=[spec] * 4,
        out_specs=[spec] * 3,
        out_shape=[jax.ShapeDtypeStruct((R, C), F32)] * 3,
        compiler_params=_params(("parallel",)),
    )(w, g, m, v)


def _adamw_small(ws, gs, ms, vs, name):
    n = len(ws)

    def body(*refs):
        for a in range(n):
            d, mn, vn = _adamw_math(refs[a][...], refs[n + a][...], refs[2 * n + a][...], refs[3 * n + a][...])
            refs[4 * n + a][...] = d
            refs[5 * n + a][...] = mn
            refs[6 * n + a][...] = vn

    shapes = [jax.ShapeDtypeStruct(w.shape, F32) for w in ws]
    outs = pl.pallas_call(
        body,
        name=name,
        out_shape=shapes * 3,
        compiler_params=_params(),
    )(*ws, *gs, *ms, *vs)
    return outs[:n], outs[n:2 * n], outs[2 * n:]


def _to_blockdiag(w):
    per = CW // LRU_BW
    w4 = w.reshape(N_CT, per, LRU_BW, LRU_BW)
    eye = jnp.eye(per, dtype=w.dtype)
    return (w4[:, :, :, None, :] * eye[None, :, None, :, None]).reshape(N_CT, CW, CW)


def _from_blockdiag(g):
    per = CW // LRU_BW
    g5 = g.reshape(N_CT, per, LRU_BW, per, LRU_BW)
    return jnp.stack([g5[:, b, :, b, :] for b in range(per)], axis=1).reshape(LRU_BLOCKS, LRU_BW, LRU_BW)


def _local_grads(x2d, tgt2d, B, S, g_in, w_all, conv_w, conv_b, gate_x_w, gate_x_b, gate_a_w, gate_a_b, lam, gain,
                 proj_weights, g_fin, reduce, deps=()):
    wx_bd = _c(_to_blockdiag(gate_x_w))
    wa_bd = _c(_to_blockdiag(gate_a_w))
    tables = _retention_tables(S)
    gain3 = gain.reshape(HEADS, 1, DK)

    proj, ht = _inproj_fwd(x2d, g_in, w_all, deps)
    hlru, ya = _lru_fwd(proj, conv_w, conv_b, wx_bd, wa_bd, gate_x_b, gate_a_b, lam, B, S)
    o_pre, yb, states = _ret_fwd(proj, tables, gain3, B, S)
    wpa, wpb, wout = proj_weights(yb)
    loss, dx2, dya, dyb, dm, dgf, gw_proj = _mid(ya, yb, proj, x2d, tgt2d, wpa, wpb, wout, g_fin)
    g3 = _inproj_bwd_dw(ht, [dm], "inproj_bwd_dw_m")
    deps = reduce.m_ready(gw_proj, g3)
    dr, dgain = _ret_bwd(dyb, o_pre, proj, states, tables, gain3, B, S, deps)
    deps = reduce.ret_done(dr)
    g12 = _inproj_bwd_dw(ht, [dr], "inproj_bwd_dw_r", deps)
    deps = reduce.r_ready(g12)
    dxa, dga, dcw, dcb, dwx_bd, dwa_bd, dbx, dba, dlam = _lru_bwd(
        dya, proj, hlru, conv_w, conv_b, wx_bd, wa_bd, gate_x_b, gate_a_b, lam, B, S, deps)
    deps = reduce.lru_done(dxa)
    g0 = _inproj_bwd_dw(ht, [dxa, dga], "inproj_bwd_dw_a", deps)
    deps = reduce.a_ready(g0)
    grad_x, dgin = _inproj_bwd_dx([dxa, dga, dr, dm], w_all, x2d, dx2, g_in, deps)
    small = dict(norm_in=dgin, conv_w=dcw, conv_b=dcb, gate_x_w=_from_blockdiag(dwx_bd), gate_x_b=dbx,
                 gate_a_w=_from_blockdiag(dwa_bd), gate_a_b=dba, lru_lambda=dlam, gn_gain=dgain.reshape(HEADS, DK),
                 norm_final=dgf)
    return loss[0, 0], grad_x, small


ALL_CHIPS = (0, 1, 2, 3)


class _GradReduce:
    def __init__(self):
        self.pending = {}

    def _start(self, key, bufs, plan, n_copies, name):
        send_sems, recv_sems, bufs, token = _copies_start(bufs, plan, n_copies, name + "_start")
        self.pending[key] = (send_sems, recv_sems, bufs, plan, name + "_wait")
        return (token,)

    def _finish(self, key, after):
        send_sems, recv_sems, bufs, plan, name = self.pending.pop(key)
        return _copies_wait(send_sems, recv_sems, bufs, after, plan, name)

    def _swap(self, key, parts):
        bufs = []
        for g in parts:
            bufs += [g, lax.empty((g.shape[0],) + g.shape[2:], F32)]
        n_slabs = [g.shape[0] for g in parts]
        return self._start(key, bufs, _swap_plan(n_slabs), sum(n_slabs), "swap_" + key)

    def _chip_sums(self, key, after):
        bufs = self._finish(key, after)
        return [_add_my_half(bufs[2 * i], bufs[2 * i + 1], "chip_sum_%s%d" % (key, i)) for i in range(len(bufs) // 2)]

    def _scatter(self, key, sums, dest_sets):
        bufs = []
        for cs in sums:
            bufs += [cs, jnp.zeros((3,) + cs.shape[1:], cs.dtype)]
        return self._start(key, bufs, _scatter_plan(dest_sets), 3 * len(sums), "scatter_" + key)

    def m_ready(self, gw_proj, g3):
        rows = gw_proj.shape[2] * gw_proj.shape[3]
        return self._swap("m", [gw_proj.reshape(N_CHIPS, 2, rows, D_MODEL), g3])

    def ret_done(self, after):
        return self._scatter("sm", self._chip_sums("m", after), [ALL_CHIPS, (3,)])

    def r_ready(self, g12):
        return self._swap("r", [g12])

    def lru_done(self, after):
        return self._scatter("sr", self._chip_sums("r", after), [(1, 2)])

    def a_ready(self, g0):
        return self._swap("a", [g0])

    def finish(self, after, meanwhile):
        deps = self._scatter("sa", self._chip_sums("a", after), [(0,)])
        other = meanwhile(deps)
        csp, gotp, cs3, got3 = self._finish("sm", other)
        cs12, got12 = self._finish("sr", other)
        cs0, got0 = self._finish("sa", other)
        half_in = _sum_parts([cs3, cs12, cs0], [got3, got12, got0], [(3,), (1, 2), (0,)], "sum_w_in")
        half_pr = _sum_slabs(csp, gotp, "sum_w_proj")
        return other, _join_halves([half_in, half_pr], [8, 4], "join_halves")


_SMALL = ("gate_x_w", "gate_a_w", "norm_in", "conv_w", "conv_b", "gate_x_b", "gate_a_b", "lru_lambda", "gn_gain",
          "norm_final")
_SMALL_SHAPES = dict(gate_x_w=(LRU_BLOCKS, LRU_BW, LRU_BW), gate_a_w=(LRU_BLOCKS, LRU_BW, LRU_BW),
                     norm_in=(1, D_MODEL), conv_w=(CONV, D_MODEL), conv_b=(1, D_MODEL), gate_x_b=(1, D_MODEL),
                     gate_a_b=(1, D_MODEL), lru_lambda=(1, D_MODEL), gn_gain=(HEADS, DK), norm_final=(1, D_MODEL))


def _pack_small(small):
    return jnp.concatenate([small[k].reshape(-1, 128) for k in _SMALL], axis=0)


def _unpack_small(packed):
    out, r = {}, 0
    for k in _SMALL:
        shape = _SMALL_SHAPES[k]
        rows = 1
        for s in shape:
            rows *= s
        rows //= 128
        out[k] = packed[r:r + rows].reshape(shape)
        r += rows
    return out


def kernel(x, norm_in, w_in, conv_w, conv_b, gate_x_w, gate_x_b, gate_a_w, gate_a_b, lru_lambda, gn_gain, w_proj_a, w_proj_b, w_out, norm_final, loss_target, m_norm_in, m_w_in, m_conv_w, m_conv_b, m_gate_x_w, m_gate_x_b, m_gate_a_w, m_gate_a_b, m_lru_lambda, m_gn_gain, m_w_proj_a, m_w_proj_b, m_w_out, m_norm_final, v_norm_in, v_w_in, v_conv_w, v_conv_b, v_gate_x_w, v_gate_x_b, v_gate_a_w, v_gate_a_b, v_lru_lambda, v_gn_gain, v_w_proj_a, v_w_proj_b, v_w_out, v_norm_final):
    B, S, _ = x.shape
    T = B * S
    xi, yi, ci = _coords()
    chip = 2 * xi + yi

    cshard = D_MODEL // N_CHIPS
    mine = _cast_into_slot([w_in[0].reshape(2, D_MODEL // 2, 2 * D_MODEL)]
                           + [w[0].reshape(2, cshard // 2, D_MODEL) for w in (w_proj_a, w_proj_b, w_out)],
                           "cast_weights")
    plan = _gather_plan(3)
    s_sems, r_sems, pbufs, token = _copies_start(mine[1:], plan, 9, "gather_proj_start")
    w_all = _gather_chips(mine[:1], [4], "gather_weights")[0].reshape(N_CHIPS, D_MODEL, 2 * D_MODEL)

    def proj_weights(after):
        got = _copies_wait(s_sems, r_sems, pbufs, after, plan, "gather_proj_wait")
        return [b.reshape(D_MODEL, D_MODEL) for b in got]

    gshard = DK // N_CHIPS
    tiny = jnp.concatenate([conv_w[0], jnp.zeros((4, cshard), F32), jnp.pad(gn_gain[0], ((0, 4), (0, cshard - gshard)))],
                           axis=0)
    tiny_all = _all_gather8(tiny, "gather_small_weights").reshape(N_CHIPS, 2, 16, cshard)[:, 0]
    conv_w_full = jnp.transpose(tiny_all[:, 0:CONV, :], (1, 0, 2)).reshape(CONV, D_MODEL)
    gain_full = jnp.transpose(tiny_all[:, 8:8 + HEADS, :gshard], (1, 0, 2)).reshape(HEADS, DK)

    reduce = _GradReduce()
    loss, grad_x, small = _local_grads(
        x.reshape(T, D_MODEL), loss_target.reshape(T, D_MODEL), B, S, norm_in, w_all, conv_w_full, conv_b,
        gate_x_w[0], gate_x_b, gate_a_w[0], gate_a_b, lru_lambda, gain_full, proj_weights,
        norm_final.reshape(1, D_MODEL), reduce, deps=(token,))
    loss = lax.psum(loss, ("x", "y", "c"))

    packed = _pack_small(small)
    m_per = packed.shape[0]

    def small_grads(deps):
        return _sum_rows8(_all_gather8(packed, "gather_small_grads", deps), m_per, "sum_small_grads")

    small_sum, (g_in_full, g_pr_full) = reduce.finish(grad_x, small_grads)
    gsm = _unpack_small(small_sum)
    g_w_in = g_in_full.reshape(D_MODEL, 2 * D_MODEL)
    g_pr = g_pr_full.reshape(2, 3, D_MODEL // (2 * N_CHIPS), D_MODEL)
    g_wpa, g_wpb, g_wout = (g_pr[:, k].reshape(cshard, D_MODEL) for k in range(3))

    grads = dict(gsm)
    grads["conv_w"] = lax.dynamic_slice_in_dim(gsm["conv_w"], chip * cshard, cshard, axis=1)
    grads["gn_gain"] = lax.dynamic_slice_in_dim(gsm["gn_gain"], chip * gshard, gshard, axis=1)
    grads.update(w_in=g_w_in, w_proj_a=g_wpa, w_proj_b=g_wpb, w_out=g_wout)

    weights = dict(norm_in=norm_in, w_in=w_in, conv_w=conv_w, conv_b=conv_b, gate_x_w=gate_x_w, gate_x_b=gate_x_b,
                   gate_a_w=gate_a_w, gate_a_b=gate_a_b, lru_lambda=lru_lambda, gn_gain=gn_gain, w_proj_a=w_proj_a,
                   w_proj_b=w_proj_b, w_out=w_out, norm_final=norm_final)
    ms = dict(norm_in=m_norm_in, w_in=m_w_in, conv_w=m_conv_w, conv_b=m_conv_b, gate_x_w=m_gate_x_w,
              gate_x_b=m_gate_x_b, gate_a_w=m_gate_a_w, gate_a_b=m_gate_a_b, lru_lambda=m_lru_lambda, gn_gain=m_gn_gain,
              w_proj_a=m_w_proj_a, w_proj_b=m_w_proj_b, w_out=m_w_out, norm_final=m_norm_final)
    vs = dict(norm_in=v_norm_in, w_in=v_w_in, conv_w=v_conv_w, conv_b=v_conv_b, gate_x_w=v_gate_x_w,
              gate_x_b=v_gate_x_b, gate_a_w=v_gate_a_w, gate_a_b=v_gate_a_b, lru_lambda=v_lru_lambda, gn_gain=v_gn_gain,
              w_proj_a=v_w_proj_a, w_proj_b=v_w_proj_b, w_out=v_w_out, norm_final=v_norm_final)
    names = list(weights)
    grads = {k: grads[k].reshape(weights[k].shape) for k in names}

    delta, new_m, new_v = {}, {}, {}
    for k in ("w_in", "w_proj_a", "w_proj_b", "w_out"):
        shp = weights[k].shape
        two = lambda a: a.reshape(shp[1], shp[2])
        d, mn, vn = _adamw_big(two(weights[k]), two(grads[k]), two(ms[k]), two(vs[k]), "adamw_" + k)
        delta[k], new_m[k], new_v[k] = d.reshape(shp), mn.reshape(shp), vn.reshape(shp)
    smalls = [k for k in names if k not in delta]

    def view(a):
        return a.reshape(1, -1) if a.ndim == 1 else (a.reshape(a.shape[1:]) if a.ndim > 2 else a)

    ds, mns, vns = _adamw_small([view(weights[k]) for k in smalls], [view(grads[k]) for k in smalls],
                                [view(ms[k]) for k in smalls], [view(vs[k]) for k in smalls], "adamw_small")
    for k, d, mn, vn in zip(smalls, ds, mns, vns):
        shp = weights[k].shape
        delta[k], new_m[k], new_v[k] = d.reshape(shp), mn.reshape(shp), vn.reshape(shp)

    return (loss, grad_x.reshape(B, S, D_MODEL), *[grads[k] for k in names], *[delta[k] for k in names],
            *[new_m[k] for k in names], *[new_v[k] for k in names])
```

```python
import functools

import jax
import jax.numpy as jnp
from jax import lax
from jax.experimental import pallas as pl
from jax.experimental.pallas import tpu as pltpu

F32 = jnp.float32
_MXU = jnp.bfloat16

D_MODEL = 1024
N_GROUPS = 8
HEADS = 4
DK = 256
CHUNK = 128
CONV = 4
LRU_BLOCKS = 16
LRU_BW = 64
LRU_C = 8.0
ROPE_THETA = 10000.0
EPS = 1e-6
CW = 256
N_CT = D_MODEL // CW
N_CHIPS = 4
MESH = pl.DeviceIdType.MESH

ADAM_LR = 0.001
ADAM_B1 = 0.9
ADAM_B2 = 0.999
ADAM_EPS = 1e-08
ADAM_WD = 0.01
ADAM_STEP = 10

VMEM_LIMIT = 56 * 1024 * 1024


def _c(v):
    return v.astype(_MXU)


def _dot(a, b):
    return lax.dot_general(a, b, (((1,), (0,)), ((), ())), preferred_element_type=F32)


def _dot_nt(a, b):
    return lax.dot_general(a, b, (((1,), (1,)), ((), ())), preferred_element_type=F32)


def _dot_tn(a, b):
    return lax.dot_general(a, b, (((0,), (0,)), ((), ())), preferred_element_type=F32)


def _sigmoid(z):
    return 1.0 / (1.0 + jnp.exp(-z))


ANY_SPEC = pl.BlockSpec(memory_space=pl.ANY)


def _after(body, n_in, deps):
    n_deps = len(deps)

    def wrapped(*refs):
        return body(*refs[:n_in], *refs[n_in + n_deps:])

    return wrapped


def _params(sem=None):
    if sem is None:
        return pltpu.CompilerParams(vmem_limit_bytes=VMEM_LIMIT)
    return pltpu.CompilerParams(vmem_limit_bytes=VMEM_LIMIT, dimension_semantics=sem)


def _inproj_fwd(x2d, g_in, w_all, deps=()):
    T = x2d.shape[0]
    tm = min(512, T)
    n_i = T // tm

    def body(*refs):
        x_ref, g_ref, w_ref = refs[:3]
        proj_ref, ht_ref, h_all = refs[-3:]
        i = pl.program_id(1)
        rows = pl.ds(pl.multiple_of(i * tm, tm), tm)

        @pl.when(pl.program_id(0) == 0)
        def _():
            x = x_ref[...]
            r = lax.rsqrt(jnp.mean(x * x, axis=-1, keepdims=True) + EPS)
            h = x * r * g_ref[...]
            h_all[rows, :] = h.astype(h_all.dtype)
            ht_ref[...] = h.T.astype(ht_ref.dtype)

        proj_ref[...] = _dot(h_all[rows, :], w_ref[0])

    first = lambda j, i: jnp.where(j == 0, i, n_i - 1)
    return pl.pallas_call(
        body,
        name="inproj_fwd",
        grid=(N_GROUPS, n_i),
        in_specs=[
            pl.BlockSpec((tm, D_MODEL), lambda j, i: (first(j, i), 0)),
            pl.BlockSpec((1, D_MODEL), lambda j, i: (0, 0)),
            pl.BlockSpec((1, D_MODEL, D_MODEL), lambda j, i: (j // 2, 0, j % 2)),
        ] + [pl.BlockSpec(memory_space=pl.ANY)] * len(deps),
        out_specs=[
            pl.BlockSpec((tm, D_MODEL), lambda j, i: (i, j)),
            pl.BlockSpec((D_MODEL, tm), lambda j, i: (0, first(j, i))),
        ],
        out_shape=[
            jax.ShapeDtypeStruct((T, N_GROUPS * D_MODEL), F32),
            jax.ShapeDtypeStruct((D_MODEL, T), _MXU),
        ],
        scratch_shapes=[pltpu.VMEM((T, D_MODEL), _MXU)],
        compiler_params=_params(("arbitrary", "arbitrary")),
    )(x2d, g_in, w_all, *deps)


def _scan_fwd(a, u):
    n = a.shape[0]
    row = lax.broadcasted_iota(jnp.int32, a.shape, 0)
    s = 1
    while s < n:
        m = row >= s
        u = u + a * jnp.where(m, pltpu.roll(u, s, 0), 0.0)
        a = a * jnp.where(m, pltpu.roll(a, s, 0), 1.0)
        s *= 2
    return a, u


def _scan_bwd(b, g):
    n = b.shape[0]
    row = lax.broadcasted_iota(jnp.int32, b.shape, 0)
    s = 1
    while s < n:
        m = row < n - s
        g = g + b * jnp.where(m, pltpu.roll(g, n - s, 0), 0.0)
        b = b * jnp.where(m, pltpu.roll(b, n - s, 0), 1.0)
        s *= 2
    return b, g


def _softplus_neg(lam):
    z = -lam
    return jnp.maximum(z, 0.0) + jnp.log1p(jnp.exp(-jnp.abs(z)))


def _lru_gates(xc, wx_ref, wa_ref, bx_ref, ba_ref, lam_ref):
    xcb = _c(xc)
    i_t = _sigmoid(_dot(xcb, wx_ref[0]) + bx_ref[...])
    r_t = _sigmoid(_dot(xcb, wa_ref[0]) + ba_ref[...])
    sp = _softplus_neg(lam_ref[...])
    log_a = (-LRU_C) * r_t * sp
    a = jnp.exp(log_a)
    mult = jnp.sqrt(1.0 - a * a)
    return xcb, i_t, r_t, sp, a, mult


def _conv_from_ext(ext_ref, xa, cw_ref, cb_ref, tc):
    return (cb_ref[...] + cw_ref[3:4, :] * xa + cw_ref[2:3, :] * ext_ref[7:7 + tc, :]
            + cw_ref[1:2, :] * ext_ref[6:6 + tc, :] + cw_ref[0:1, :] * ext_ref[5:5 + tc, :])


def _lru_fwd(proj, conv_w, conv_b, wx_bd, wa_bd, bx, ba, lam, B, S):
    T = B * S
    tc = min(256, S)
    nt = S // tc
    h8 = tc // 8

    def body(xa_ref, halo_ref, ga_ref, cw_ref, cb_ref, wx_ref, wa_ref, bx_ref, ba_ref, lam_ref,
             h_ref, ya_ref, ext_ref, carry_ref):
        t = pl.program_id(2)

        @pl.when(t == 0)
        def _():
            carry_ref[...] = jnp.zeros_like(carry_ref)

        xa = xa_ref[...]
        ext_ref[0:8, :] = jnp.where(t == 0, 0.0, halo_ref[...])
        ext_ref[8:8 + tc, :] = xa
        xc = _conv_from_ext(ext_ref, xa, cw_ref, cb_ref, tc)
        _, i_t, _, _, a, mult = _lru_gates(xc, wx_ref, wa_ref, bx_ref, ba_ref, lam_ref)
        u = mult * (i_t * xc)
        acum, hloc = _scan_fwd(a, u)
        h = hloc + acum * carry_ref[7:8, :]
        h_ref[...] = h
        carry_ref[...] = h[tc - 8:tc, :]
        ga = ga_ref[...]
        ya_ref[...] = (ga * _sigmoid(ga) * h).astype(ya_ref.dtype)

    row = lambda b, t: b * nt + t
    vec = pl.BlockSpec((1, CW), lambda b, c, t: (0, c))
    mat = pl.BlockSpec((1, CW, CW), lambda b, c, t: (c, 0, 0))
    return pl.pallas_call(
        body,
        name="lru_fwd",
        grid=(B, N_CT, nt),
        in_specs=[
            pl.BlockSpec((tc, CW), lambda b, c, t: (row(b, t), c)),
            pl.BlockSpec((8, CW), lambda b, c, t: (jnp.maximum(row(b, t) * h8 - 1, 0), c)),
            pl.BlockSpec((tc, CW), lambda b, c, t: (row(b, t), N_CT + c)),
            pl.BlockSpec((CONV, CW), lambda b, c, t: (0, c)),
            vec, mat, mat, vec, vec, vec,
        ],
        out_specs=[
            pl.BlockSpec((tc, CW), lambda b, c, t: (row(b, t), c)),
            pl.BlockSpec((tc, CW), lambda b, c, t: (row(b, t), c)),
        ],
        out_shape=[
            jax.ShapeDtypeStruct((T, D_MODEL), F32),
            jax.ShapeDtypeStruct((T, D_MODEL), _MXU),
        ],
        scratch_shapes=[pltpu.VMEM((tc + 8, CW), F32), pltpu.VMEM((8, CW), F32)],
        compiler_params=_params(("parallel", "parallel", "arbitrary")),
    )(proj, proj, proj, conv_w, conv_b, wx_bd, wa_bd, bx, ba, lam)


def _lru_bwd(dya, proj, hlru, conv_w, conv_b, wx_bd, wa_bd, bx, ba, lam, B, S, deps=()):
    T = B * S
    tc = min(256, S)
    nt = S // tc
    h8 = tc // 8

    def body(dya_ref, xa_ref, xhalo_ref, ga_ref, h_ref, hhalo_ref, cw_ref, cb_ref, wx_ref, wa_ref, bx_ref, ba_ref,
             lam_ref, dxa_ref, dga_ref, dcw_ref, dcb_ref, dwx_ref, dwa_ref, dbx_ref, dba_ref, dlam_ref,
             ext_ref, ext2_ref, carry_ref, dhalo_ref):
        b = pl.program_id(1)
        t = pl.program_id(2)
        tt = nt - 1 - t

        @pl.when(t == 0)
        def _():
            carry_ref[...] = jnp.zeros_like(carry_ref)
            dhalo_ref[...] = jnp.zeros_like(dhalo_ref)

        @pl.when((t == 0) & (b == 0))
        def _():
            for r in (dcw_ref, dcb_ref, dwx_ref, dwa_ref, dbx_ref, dba_ref, dlam_ref):
                r[...] = jnp.zeros_like(r)

        xa = xa_ref[...]
        ext_ref[0:8, :] = jnp.where(tt == 0, 0.0, xhalo_ref[...])
        ext_ref[8:8 + tc, :] = xa
        xc = _conv_from_ext(ext_ref, xa, cw_ref, cb_ref, tc)
        xcb, i_t, r_t, sp, a, mult = _lru_gates(xc, wx_ref, wa_ref, bx_ref, ba_ref, lam_ref)

        h = h_ref[...]
        ga = ga_ref[...]
        dya_t = dya_ref[...]
        sg = _sigmoid(ga)
        dga_ref[...] = (dya_t * h * (sg * (1.0 + ga * (1.0 - sg)))).astype(dga_ref.dtype)
        dlru = dya_t * (ga * sg)

        row = lax.broadcasted_iota(jnp.int32, a.shape, 0)
        coef = jnp.where(row == tc - 1, 1.0, pltpu.roll(a, tc - 1, 0))
        bcum, dloc = _scan_bwd(coef, dlru)
        dh = dloc + bcum * carry_ref[0:1, :]
        ext2_ref[0:tc, :] = a * dh
        carry_ref[...] = ext2_ref[0:8, :]

        ext2_ref[0:8, :] = jnp.where(tt == 0, 0.0, hhalo_ref[...])
        ext2_ref[8:8 + tc, :] = h
        hprev = ext2_ref[7:7 + tc, :]

        da = dh * hprev
        ix = i_t * xc
        dmult = dh * ix
        di = dh * mult * xc
        dxc = dh * mult * i_t
        dlog_a = da * a - dmult * (a * a) / mult
        dr = dlog_a * ((-LRU_C) * sp)
        dlam_ref[...] += jnp.sum(dlog_a * r_t, axis=0, keepdims=True) * (LRU_C * _sigmoid(-lam_ref[...]))
        dza = dr * r_t * (1.0 - r_t)
        dzx = di * i_t * (1.0 - i_t)
        dzab = _c(dza)
        dzxb = _c(dzx)
        dxc = dxc + _dot_nt(dzxb, wx_ref[0]) + _dot_nt(dzab, wa_ref[0])
        dwx_ref[0] += _dot_tn(xcb, dzxb)
        dwa_ref[0] += _dot_tn(xcb, dzab)
        dbx_ref[...] += jnp.sum(dzx, axis=0, keepdims=True)
        dba_ref[...] += jnp.sum(dza, axis=0, keepdims=True)

        dcb_ref[...] += jnp.sum(dxc, axis=0, keepdims=True)
        dcw_ref[3:4, :] += jnp.sum(dxc * xa, axis=0, keepdims=True)
        dcw_ref[2:3, :] += jnp.sum(dxc * ext_ref[7:7 + tc, :], axis=0, keepdims=True)
        dcw_ref[1:2, :] += jnp.sum(dxc * ext_ref[6:6 + tc, :], axis=0, keepdims=True)
        dcw_ref[0:1, :] += jnp.sum(dxc * ext_ref[5:5 + tc, :], axis=0, keepdims=True)
        ext2_ref[0:tc, :] = dxc
        ext2_ref[tc:tc + 8, :] = dhalo_ref[...]
        dxa = (cw_ref[3:4, :] * dxc + cw_ref[2:3, :] * ext2_ref[1:1 + tc, :]
               + cw_ref[1:2, :] * ext2_ref[2:2 + tc, :] + cw_ref[0:1, :] * ext2_ref[3:3 + tc, :])
        dxa_ref[...] = dxa.astype(dxa_ref.dtype)
        dhalo_ref[...] = ext2_ref[0:8, :]

    row_of = lambda b, t: b * nt + (nt - 1 - t)
    tile = lambda off: pl.BlockSpec((tc, CW), lambda c, b, t: (row_of(b, t), off + c))
    halo = pl.BlockSpec((8, CW), lambda c, b, t: (jnp.maximum(row_of(b, t) * h8 - 1, 0), c))
    vec = pl.BlockSpec((1, CW), lambda c, b, t: (0, c))
    mat = pl.BlockSpec((1, CW, CW), lambda c, b, t: (c, 0, 0))
    cwspec = pl.BlockSpec((CONV, CW), lambda c, b, t: (0, c))
    return pl.pallas_call(
        _after(body, 13, deps),
        name="lru_bwd",
        grid=(N_CT, B, nt),
        in_specs=[tile(0), tile(0), halo, tile(N_CT), tile(0), halo, cwspec, vec, mat, mat, vec, vec, vec]
        + [ANY_SPEC] * len(deps),
        out_specs=[tile(0), tile(0), cwspec, vec, mat, mat, vec, vec, vec],
        out_shape=[
            jax.ShapeDtypeStruct((T, D_MODEL), _MXU),
            jax.ShapeDtypeStruct((T, D_MODEL), _MXU),
            jax.ShapeDtypeStruct((CONV, D_MODEL), F32),
            jax.ShapeDtypeStruct((1, D_MODEL), F32),
            jax.ShapeDtypeStruct((N_CT, CW, CW), F32),
            jax.ShapeDtypeStruct((N_CT, CW, CW), F32),
            jax.ShapeDtypeStruct((1, D_MODEL), F32),
            jax.ShapeDtypeStruct((1, D_MODEL), F32),
            jax.ShapeDtypeStruct((1, D_MODEL), F32),
        ],
        scratch_shapes=[pltpu.VMEM((tc + 8, CW), F32), pltpu.VMEM((tc + 8, CW), F32),
                        pltpu.VMEM((8, CW), F32), pltpu.VMEM((8, CW), F32)],
        compiler_params=_params(("parallel", "arbitrary", "arbitrary")),
    )(dya, proj, proj, proj, hlru, hlru, conv_w, conv_b, wx_bd, wa_bd, bx, ba, lam, *deps)


def _retention_tables(S):
    half = DK // 2
    freqs = ROPE_THETA ** (-jnp.arange(half, dtype=F32) / half)
    ang = jnp.arange(S, dtype=F32)[:, None] * freqs[None, :]
    log_g = jnp.log1p(-(2.0 ** (-5.0 - jnp.arange(HEADS, dtype=F32))))
    idx = jnp.arange(CHUNK, dtype=F32)
    diff = idx[:, None] - idx[None, :]
    inner = jnp.where(diff >= 0, jnp.exp(jnp.maximum(diff, 0.0)[None] * log_g[:, None, None]), 0.0)
    cross = jnp.exp((idx[None, :] + 1.0) * log_g[:, None])[:, :, None]
    state = jnp.exp((CHUNK - 1.0 - idx[None, :]) * log_g[:, None])[:, :, None]
    gam = jnp.broadcast_to(jnp.exp(CHUNK * log_g)[:, None, None], (HEADS, 1, DK))
    return jnp.cos(ang), jnp.sin(ang), inner, cross, state, gam


def _rot(x, cos, sin):
    half = DK // 2
    x1, x2 = x[:, :half], x[:, half:]
    return jnp.concatenate([x1 * cos - x2 * sin, x1 * sin + x2 * cos], axis=-1)


def _rot_t(y, cos, sin):
    half = DK // 2
    y1, y2 = y[:, :half], y[:, half:]
    return jnp.concatenate([y1 * cos + y2 * sin, y2 * cos - y1 * sin], axis=-1)


def _groupnorm(o):
    mu = jnp.mean(o, axis=-1, keepdims=True)
    oc = o - mu
    rs = lax.rsqrt(jnp.mean(oc * oc, axis=-1, keepdims=True) + EPS)
    return oc * rs, rs


def _ret_specs(S, chunk_of):
    nc = S // CHUNK
    qkv = lambda g: pl.BlockSpec((CHUNK, D_MODEL), lambda b, c: (b * nc + chunk_of(c), g))
    act = pl.BlockSpec((CHUNK, D_MODEL), lambda b, c: (b * nc + chunk_of(c), 0))
    rope = pl.BlockSpec((CHUNK, DK // 2), lambda b, c: (chunk_of(c), 0))
    dmat = pl.BlockSpec((HEADS, CHUNK, CHUNK), lambda b, c: (0, 0, 0))
    dvec = pl.BlockSpec((HEADS, CHUNK, 1), lambda b, c: (0, 0, 0))
    hrow = pl.BlockSpec((HEADS, 1, DK), lambda b, c: (0, 0, 0))
    rst = pl.BlockSpec((1, HEADS, DK, DK), lambda b, c: (b * nc + chunk_of(c), 0, 0, 0))
    return qkv, act, rope, dmat, dvec, hrow, rst


def _ret_fwd(proj, tables, gain3, B, S):
    T = B * S
    nc = S // CHUNK
    cos, sin, dmat_t, cd_t, sd_t, gam_t = tables

    def body(q_ref, k_ref, v_ref, gb_ref, cos_ref, sin_ref, dm_ref, cd_ref, sd_ref, gam_ref, gain_ref,
             o_ref, yb_ref, rs_ref, state_ref):
        @pl.when(pl.program_id(1) == 0)
        def _():
            state_ref[...] = jnp.zeros_like(state_ref)

        cos_t, sin_t = cos_ref[...], sin_ref[...]
        for h in range(HEADS):
            cols = slice(h * DK, (h + 1) * DK)
            qb = _c(_rot(q_ref[:, cols], cos_t, sin_t))
            kb = _c(_rot(k_ref[:, cols], cos_t, sin_t) * (DK ** -0.5))
            v = v_ref[:, cols]
            state = state_ref[h]
            sb = _c(state)
            rs_ref[0, h] = sb
            scores = _dot_nt(qb, kb) * dm_ref[h]
            o = _dot(_c(scores), _c(v)) + _dot(qb, sb) * cd_ref[h]
            state_ref[h] = gam_ref[h] * state + _dot_tn(kb, _c(v * sd_ref[h]))
            o_ref[:, cols] = o
            n, _ = _groupnorm(o)
            gb = gb_ref[:, cols]
            yb_ref[:, cols] = (gb * _sigmoid(gb) * (n * gain_ref[h])).astype(yb_ref.dtype)

    qkv, act, rope, dmat, dvec, hrow, rst = _ret_specs(S, lambda c: c)
    return pl.pallas_call(
        body,
        name="ret_fwd",
        grid=(B, nc),
        in_specs=[qkv(2), qkv(3), qkv(4), qkv(5), rope, rope, dmat, dvec, dvec, hrow, hrow],
        out_specs=[act, act, rst],
        out_shape=[
            jax.ShapeDtypeStruct((T, D_MODEL), F32),
            jax.ShapeDtypeStruct((T, D_MODEL), _MXU),
            jax.ShapeDtypeStruct((B * nc, HEADS, DK, DK), _MXU),
        ],
        scratch_shapes=[pltpu.VMEM((HEADS, DK, DK), F32)],
        compiler_params=_params(("parallel", "arbitrary")),
    )(proj, proj, proj, proj, cos, sin, dmat_t, cd_t, sd_t, gam_t, gain3)


def _ret_bwd(dyb, o_pre, proj, states, tables, gain3, B, S, deps=()):
    T = B * S
    nc = S // CHUNK
    cos, sin, dmat_t, cd_t, sd_t, gam_t = tables

    def body(dyb_ref, o_ref, q_ref, k_ref, v_ref, gb_ref, rs_ref, cos_ref, sin_ref, dm_ref, cd_ref, sd_ref, gam_ref,
             gain_ref, dr_ref, dgain_ref, dstate_ref):
        @pl.when(pl.program_id(1) == 0)
        def _():
            dstate_ref[...] = jnp.zeros_like(dstate_ref)

        @pl.when((pl.program_id(1) == 0) & (pl.program_id(0) == 0))
        def _():
            dgain_ref[...] = jnp.zeros_like(dgain_ref)

        cos_t, sin_t = cos_ref[...], sin_ref[...]
        for h in range(HEADS):
            cols = slice(h * DK, (h + 1) * DK)
            gain = gain_ref[h]
            n, rs = _groupnorm(o_ref[:, cols])
            gb = gb_ref[:, cols]
            sg = _sigmoid(gb)
            dy = dyb_ref[:, cols]
            part = lambda g: slice(g * D_MODEL + h * DK, g * D_MODEL + (h + 1) * DK)
            dr_ref[:, part(3)] = (dy * (n * gain) * (sg * (1.0 + gb * (1.0 - sg)))).astype(dr_ref.dtype)
            dgn = dy * (gb * sg)
            dgain_ref[h] += jnp.sum(dgn * n, axis=0, keepdims=True)
            dn = dgn * gain
            do = rs * (dn - jnp.mean(dn, axis=-1, keepdims=True) - n * jnp.mean(dn * n, axis=-1, keepdims=True))

            qb = _c(_rot(q_ref[:, cols], cos_t, sin_t))
            kb = _c(_rot(k_ref[:, cols], cos_t, sin_t) * (DK ** -0.5))
            v = v_ref[:, cols]
            vb = _c(v)
            vsb = _c(v * sd_ref[h])
            dob = _c(do)
            docb = _c(do * cd_ref[h])
            dmat = dm_ref[h]
            dstate = dstate_ref[h]
            dsb = _c(dstate)
            pb = _c(_dot_nt(qb, kb) * dmat)
            dsc = _c(_dot_nt(dob, vb) * dmat)
            dq = _dot(dsc, kb) + _dot_nt(docb, rs_ref[0, h])
            dk = _dot_tn(dsc, qb) + _dot_nt(vsb, dsb)
            dv = _dot_tn(pb, dob) + _dot(kb, dsb) * sd_ref[h]
            dstate_ref[h] = gam_ref[h] * dstate + _dot_tn(qb, docb)
            dr_ref[:, part(0)] = _rot_t(dq, cos_t, sin_t).astype(dr_ref.dtype)
            dr_ref[:, part(1)] = (_rot_t(dk, cos_t, sin_t) * (DK ** -0.5)).astype(dr_ref.dtype)
            dr_ref[:, part(2)] = dv.astype(dr_ref.dtype)

    qkv, act, rope, dmat, dvec, hrow, rst = _ret_specs(S, lambda c: nc - 1 - c)
    wide = pl.BlockSpec((CHUNK, 4 * D_MODEL), lambda b, c: (b * nc + nc - 1 - c, 0))
    return pl.pallas_call(
        _after(body, 14, deps),
        name="ret_bwd",
        grid=(B, nc),
        in_specs=[act, act, qkv(2), qkv(3), qkv(4), qkv(5), rst, rope, rope, dmat, dvec, dvec, hrow, hrow]
        + [ANY_SPEC] * len(deps),
        out_specs=[wide, hrow],
        out_shape=[jax.ShapeDtypeStruct((T, 4 * D_MODEL), _MXU), jax.ShapeDtypeStruct((HEADS, 1, DK), F32)],
        scratch_shapes=[pltpu.VMEM((HEADS, DK, DK), F32)],
        compiler_params=_params(("arbitrary", "arbitrary")),
    )(dyb, o_pre, proj, proj, proj, proj, states, cos, sin, dmat_t, cd_t, sd_t, gam_t, gain3, *deps)


def _mid(ya, yb, proj, x2d, tgt2d, wpa, wpb, wout, g_fin):
    T = x2d.shape[0]
    tm = min(256, T)
    n_steps = T // tm
    rows = D_MODEL // (2 * N_CHIPS)

    def body(ya_ref, yb_ref, ma_ref, mb_ref, x_ref, t_ref, gf_ref, wpa_hbm, wpb_hbm, wout_hbm,
             loss_ref, dx2_ref, dya_ref, dyb_ref, dm_ref, dgf_ref, gw_hbm, w_ref, acc_ref, sem):
        i = pl.program_id(0)

        @pl.when(i == 0)
        def _():
            loads = [pltpu.make_async_copy(src, w_ref.at[k], sem.at[k]) for k, src in enumerate((wpa_hbm, wpb_hbm, wout_hbm))]
            for cp in loads:
                cp.start()
            for cp in loads:
                cp.wait()
            acc_ref[...] = jnp.zeros_like(acc_ref)
            loss_ref[...] = jnp.zeros_like(loss_ref)
            dgf_ref[...] = jnp.zeros_like(dgf_ref)

        ya_t, yb_t = ya_ref[...], yb_ref[...]
        out_a = _dot(ya_t, w_ref[0])
        out_b = _dot(yb_t, w_ref[1])
        sa = _sigmoid(ma_ref[...])
        sb = _sigmoid(mb_ref[...])
        mgb = _c(sa * out_a + sb * out_b)
        x2 = x_ref[...] + _dot(mgb, w_ref[2])
        r2 = lax.rsqrt(jnp.mean(x2 * x2, axis=-1, keepdims=True) + EPS)
        nx = x2 * r2
        gf = gf_ref[...]
        err = nx * gf - t_ref[...]
        loss_ref[...] += 0.5 * jnp.sum(jnp.mean(err * err, axis=-1, keepdims=True), axis=0, keepdims=True)
        dy = err * (1.0 / D_MODEL)
        dgf_ref[...] += jnp.sum(dy * nx, axis=0, keepdims=True)
        dyg = dy * gf
        dx2 = r2 * (dyg - nx * jnp.mean(dyg * nx, axis=-1, keepdims=True))
        dx2_ref[...] = dx2
        dx2b = _c(dx2)
        dmg = _dot_nt(dx2b, w_ref[2])
        acc_ref[2] += _dot_tn(mgb, dx2b)
        dm_ref[:, :D_MODEL] = (dmg * out_a * sa * (1.0 - sa)).astype(dm_ref.dtype)
        dm_ref[:, D_MODEL:] = (dmg * out_b * sb * (1.0 - sb)).astype(dm_ref.dtype)
        dab = _c(dmg * sa)
        dbb = _c(dmg * sb)
        dya_ref[...] = _dot_nt(dab, w_ref[0])
        dyb_ref[...] = _dot_nt(dbb, w_ref[1])
        acc_ref[0] += _dot_tn(ya_t, dab)
        acc_ref[1] += _dot_tn(yb_t, dbb)

        @pl.when(i == n_steps - 1)
        def _():
            copies = [pltpu.make_async_copy(acc_ref.at[k, pl.ds((2 * p + hf) * rows, rows), :], gw_hbm.at[p, hf, k],
                                            sem.at[(k * N_CHIPS + p) * 2 + hf])
                      for k in range(3) for p in range(N_CHIPS) for hf in range(2)]
            for cp in copies:
                cp.start()
            for cp in copies:
                cp.wait()

    tile = lambda j: pl.BlockSpec((tm, D_MODEL), lambda i: (i, j))
    one = pl.BlockSpec((1, D_MODEL), lambda i: (0, 0))
    anyspec = pl.BlockSpec(memory_space=pl.ANY)
    return pl.pallas_call(
        body,
        name="mid",
        grid=(n_steps,),
        in_specs=[tile(0), tile(0), tile(6), tile(7), tile(0), tile(0), one, anyspec, anyspec, anyspec],
        out_specs=[pl.BlockSpec((1, 1), lambda i: (0, 0)), tile(0), tile(0), tile(0),
                   pl.BlockSpec((tm, 2 * D_MODEL), lambda i: (i, 0)), one, anyspec],
        out_shape=[
            jax.ShapeDtypeStruct((1, 1), F32),
            jax.ShapeDtypeStruct((T, D_MODEL), F32),
            jax.ShapeDtypeStruct((T, D_MODEL), F32),
            jax.ShapeDtypeStruct((T, D_MODEL), F32),
            jax.ShapeDtypeStruct((T, 2 * D_MODEL), _MXU),
            jax.ShapeDtypeStruct((1, D_MODEL), F32),
            jax.ShapeDtypeStruct((N_CHIPS, 2, 3, rows, D_MODEL), F32),
        ],
        scratch_shapes=[pltpu.VMEM((3, D_MODEL, D_MODEL), _MXU), pltpu.VMEM((3, D_MODEL, D_MODEL), F32),
                        pltpu.SemaphoreType.DMA((3 * N_CHIPS * 2,))],
        compiler_params=_params(("arbitrary",)),
    )(ya, yb, proj, proj, x2d, tgt2d, g_fin, wpa, wpb, wout)


DX_TILE = 512


def _inproj_bwd_dx(dparts, w_all, x2d, dx2, g_in, first, count, prev, name, deps=()):
    T = x2d.shape[0]
    tm = min(DX_TILE, T)
    n_d = len(dparts)
    groups = [(a, k) for a, d in enumerate(dparts) for k in range(d.shape[1] // D_MODEL)]
    dg_start = jnp.zeros((1, D_MODEL), F32) if prev is None else prev[1]
    carried = () if prev is None else (prev[0],)

    def body(*refs):
        d_refs = refs[:n_d]
        x_ref, dx2_ref, g_ref, dg0_ref, w_hbm = refs[n_d:n_d + 5]
        dx_ref, dg_ref, w_ref, sem = refs[-4:]

        @pl.when(pl.program_id(0) == 0)
        def _():
            cp = pltpu.make_async_copy(w_hbm, w_ref, sem)
            cp.start()
            cp.wait()
            dg_ref[...] = dg0_ref[...]

        dh = jnp.zeros((tm, D_MODEL), F32)
        for j, (a, k) in enumerate(groups):
            dh = dh + _dot_nt(d_refs[a][:, k * D_MODEL:(k + 1) * D_MODEL],
                              w_ref[j // 2, :, (j % 2) * D_MODEL:(j % 2 + 1) * D_MODEL])
        x = x_ref[...]
        r = lax.rsqrt(jnp.mean(x * x, axis=-1, keepdims=True) + EPS)
        nx = x * r
        dg_ref[...] += jnp.sum(dh * nx, axis=0, keepdims=True)
        dhg = dh * g_ref[...]
        dx_ref[...] = dx2_ref[...] + r * (dhg - nx * jnp.mean(dhg * nx, axis=-1, keepdims=True))

    tile = pl.BlockSpec((tm, D_MODEL), lambda i: (first + i, 0))
    one = pl.BlockSpec((1, D_MODEL), lambda i: (0, 0))
    return pl.pallas_call(
        body,
        name=name,
        grid=(count,),
        in_specs=[pl.BlockSpec((tm, d.shape[1]), lambda i: (first + i, 0)) for d in dparts]
        + [tile, tile, one, one, ANY_SPEC] + [ANY_SPEC] * (len(carried) + len(deps)),
        out_specs=[tile, one],
        out_shape=[jax.ShapeDtypeStruct((T, D_MODEL), F32), jax.ShapeDtypeStruct((1, D_MODEL), F32)],
        input_output_aliases={n_d + 5: 0} if carried else {},
        scratch_shapes=[pltpu.VMEM(w_all.shape, w_all.dtype), pltpu.SemaphoreType.DMA],
        compiler_params=_params(("arbitrary",)),
    )(*dparts, x2d, dx2, g_in, dg_start, w_all, *carried, *deps)


def _inproj_bwd_dw(ht, dparts, name, deps=()):
    T = ht.shape[1]
    tn = 512
    half = D_MODEL // 2
    per_chip = 2 * D_MODEL // tn
    n_d = len(dparts)
    tiles = [(a, t) for a, d in enumerate(dparts) for t in range(d.shape[1] // tn)]
    offs = [sum(d.shape[1] // tn for d in dparts[:a]) for a in range(n_d)]

    def body(*refs):
        ht_ref = refs[0]
        d_refs = refs[1:1 + n_d]
        out_ref = refs[-1]
        t = pl.program_id(0)

        for a in range(n_d):
            lo, hi = offs[a], offs[a] + dparts[a].shape[1] // tn

            @pl.when((t >= lo) & (t < hi))
            def _(a=a):
                g = _dot(ht_ref[...], d_refs[a][...])
                out_ref[0, 0] = g[:half]
                out_ref[0, 1] = g[half:]

    def dspec(a):
        n_a = dparts[a].shape[1] // tn
        return pl.BlockSpec((T, tn), lambda t: (0, jnp.clip(t - offs[a], 0, n_a - 1)))

    return pl.pallas_call(
        body,
        name=name,
        grid=(len(tiles),),
        in_specs=[pl.BlockSpec((D_MODEL, T), lambda t: (0, 0))] + [dspec(a) for a in range(n_d)]
        + [ANY_SPEC] * len(deps),
        out_specs=pl.BlockSpec((1, 2, half, tn), lambda t: (t // per_chip, 0, 0, t % per_chip)),
        out_shape=jax.ShapeDtypeStruct((len(tiles) // per_chip, 2, half, 2 * D_MODEL), F32),
        compiler_params=_params(("parallel",)),
    )(ht, *dparts, *deps)


def _coords():
    return lax.axis_index("x"), lax.axis_index("y"), lax.axis_index("c")


def _other_chips(x, y):
    return [(1 - x, y), (x, 1 - y), (1 - x, 1 - y)]


def _all_gather8(xs, name, deps=()):
    m_per, n = xs.shape

    def body(x_ref, *rest):
        out_ref, send_sems, recv_sems, local_sem = rest[-4:]
        x, y, c = _coords()
        me, sibling = (x, y, c), (x, y, 1 - c)
        chips = _other_chips(x, y)

        def rows(px, py, pc):
            return out_ref.at[pl.ds((4 * px + 2 * py + pc) * m_per, m_per), :]

        def copy(k, block, to, src=None):
            return pltpu.make_async_remote_copy(
                src_ref=rows(*block) if src is None else src, dst_ref=rows(*block),
                send_sem=send_sems.at[k], recv_sem=recv_sems.at[k], device_id=to, device_id_type=MESH)

        mine = pltpu.make_async_copy(x_ref, rows(*me), local_sem)
        mine.start()
        first = [copy(0, me, sibling, src=x_ref)]
        first += [copy(1 + j, me, (*chip, c), src=x_ref) for j, chip in enumerate(chips)]
        for cp in first:
            cp.start()
        passed = [copy(4 + j, (*chip, c), sibling) for j, chip in enumerate(chips)]
        for j, chip in enumerate(chips):
            copy(1 + j, (*chip, c), me).wait_recv()
            passed[j].start()
        copy(0, sibling, me).wait_recv()
        for j, chip in enumerate(chips):
            copy(4 + j, (*chip, 1 - c), me).wait_recv()
        for cp in first + passed:
            cp.wait_send()
        mine.wait()

    return pl.pallas_call(
        body,
        name=name,
        out_shape=jax.ShapeDtypeStruct((8 * m_per, n), xs.dtype),
        in_specs=[pl.BlockSpec(memory_space=pltpu.VMEM)] + [ANY_SPEC] * len(deps),
        out_specs=pl.BlockSpec(memory_space=pltpu.VMEM),
        scratch_shapes=[pltpu.SemaphoreType.DMA((7,)), pltpu.SemaphoreType.DMA((7,)), pltpu.SemaphoreType.DMA],
        compiler_params=pltpu.CompilerParams(vmem_limit_bytes=VMEM_LIMIT),
    )(xs, *deps)


def _chunks(rows, n):
    size = rows // n
    return [pl.ds(q * size, size) for q in range(n)]


HBM_SPEC = pl.BlockSpec(memory_space=pltpu.HBM)
SEM_SPEC = pl.BlockSpec(memory_space=pltpu.SEMAPHORE)
DATAFLOW = pltpu.SideEffectType.DATAFLOW_SIDE_EFFECTING


def _copies_start(bufs, plan, n_copies, name):
    n = len(bufs)

    def body(*refs):
        ins = refs[:n]
        send_sems, recv_sems = refs[n], refs[n + 1]
        token = refs[-1]
        for k, send, _ in plan(ins):
            if send is not None:
                src, dst, dev, pred = send
                cp = pltpu.make_async_remote_copy(src_ref=src, dst_ref=dst, send_sem=send_sems.at[k],
                                                  recv_sem=recv_sems.at[k], device_id=dev, device_id_type=MESH)
                if pred is None:
                    cp.start()
                else:
                    pl.when(pred)(cp.start)
        token[...] = jnp.zeros_like(token)

    hbm = [pltpu.with_memory_space_constraint(b, pltpu.HBM) for b in bufs]
    outs = pl.pallas_call(
        body,
        name=name,
        in_specs=[HBM_SPEC] * n,
        out_specs=(SEM_SPEC, SEM_SPEC, *([HBM_SPEC] * n), pl.BlockSpec(memory_space=pltpu.VMEM)),
        out_shape=(pltpu.SemaphoreType.DMA((n_copies,)), pltpu.SemaphoreType.DMA((n_copies,)),
                   *[pltpu.HBM(b.shape, b.dtype) for b in bufs], jax.ShapeDtypeStruct((8, 128), F32)),
        input_output_aliases={a: 2 + a for a in range(n)},
        compiler_params=pltpu.CompilerParams(has_side_effects=DATAFLOW),
    )(*hbm)
    return outs[0], outs[1], list(outs[2:2 + n]), outs[-1]


def _copies_wait(send_sems, recv_sems, bufs, after, plan, name):
    n = len(bufs)

    def body(*refs):
        ins = refs[:n]
        s_sems, r_sems = refs[n], refs[n + 1]
        for k, send, recv in plan(ins):
            if send is not None:
                src, dst, dev, pred = send
                cp = pltpu.make_async_remote_copy(src_ref=src, dst_ref=dst, send_sem=s_sems.at[k],
                                                  recv_sem=r_sems.at[k], device_id=dev, device_id_type=MESH)
                if pred is None:
                    cp.wait_send()
                else:
                    pl.when(pred)(cp.wait_send)
            if recv is not None:
                dst, pred = recv
                cp = pltpu.make_async_remote_copy(src_ref=dst, dst_ref=dst, send_sem=s_sems.at[k],
                                                  recv_sem=r_sems.at[k], device_id=_coords(), device_id_type=MESH)
                if pred is None:
                    cp.wait_recv()
                else:
                    pl.when(pred)(cp.wait_recv)

    outs = pl.pallas_call(
        body,
        name=name,
        in_specs=[HBM_SPEC] * n + [SEM_SPEC, SEM_SPEC, pl.BlockSpec(memory_space=pl.ANY)],
        out_specs=[HBM_SPEC] * n,
        out_shape=[pltpu.HBM(b.shape, b.dtype) for b in bufs],
        input_output_aliases={a: a for a in range(n)},
        compiler_params=pltpu.CompilerParams(has_side_effects=DATAFLOW),
    )(*bufs, send_sems, recv_sems, after)
    return list(outs)


def _gather_plan(n_bufs):
    def plan(refs):
        x, y, c = _coords()
        me = 2 * x + y
        out = []
        for k, (px, py) in enumerate(_other_chips(x, y)):
            for a in range(n_bufs):
                out.append((k * n_bufs + a, (refs[a].at[me], refs[a].at[me], (px, py, c), None),
                            (refs[a].at[2 * px + py], None)))
        return out
    return plan


def _cast_into_slot(ws, name):
    n = len(ws)
    nt = 2

    def body(s_ref, *refs):
        for a in range(n):
            refs[n + a][0] = refs[a][...].astype(refs[n + a].dtype)

    xi, yi, _ = _coords()
    return pl.pallas_call(
        body,
        name=name,
        grid_spec=pltpu.PrefetchScalarGridSpec(
            num_scalar_prefetch=1,
            grid=(2, nt),
            in_specs=[pl.BlockSpec((1, w.shape[1] // nt, w.shape[2]), lambda hf, i, s: (hf, i, 0)) for w in ws],
            out_specs=[pl.BlockSpec((1, 1, w.shape[1] // nt, w.shape[2]), lambda hf, i, s: (s[0], hf, i, 0)) for w in ws],
        ),
        out_shape=[jax.ShapeDtypeStruct((N_CHIPS,) + w.shape, _MXU) for w in ws],
        compiler_params=_params(("parallel", "parallel")),
    )((2 * xi + yi).reshape(1).astype(jnp.int32), *ws)


def _gather_chips(bufs, n_chunks, name):
    n = len(bufs)
    pieces = [(a, rows) for a in range(n) for rows in _chunks(bufs[a].shape[2], n_chunks[a])]
    n_p = len(pieces)

    def body(*refs):
        outs = refs[n:2 * n]
        send_sems, recv_sems, fsend_sems, frecv_sems = refs[2 * n:]
        x, y, c = _coords()
        me = 2 * x + y
        chips = _other_chips(x, y)

        def send(k, i, slot, chip):
            a, rows = pieces[i]
            return pltpu.make_async_remote_copy(
                src_ref=outs[a].at[slot, c, rows], dst_ref=outs[a].at[slot, c, rows], send_sem=send_sems.at[k * n_p + i],
                recv_sem=recv_sems.at[k * n_p + i], device_id=(*chip, c), device_id_type=MESH)

        def forward(k, i, slot, half):
            a, rows = pieces[i]
            return pltpu.make_async_remote_copy(
                src_ref=outs[a].at[slot, half, rows], dst_ref=outs[a].at[slot, half, rows],
                send_sem=fsend_sems.at[k * n_p + i], recv_sem=frecv_sems.at[k * n_p + i],
                device_id=(x, y, 1 - c), device_id_type=MESH)

        sends = [send(k, i, me, chip) for i in range(n_p) for k, chip in enumerate(chips)]
        for cp in sends:
            cp.start()
        forwards = []
        for i in range(n_p):
            for k, (px, py) in enumerate(chips):
                send(k, i, 2 * px + py, (px, py)).wait_recv()
                fw = forward(k, i, 2 * px + py, c)
                fw.start()
                forwards.append(fw)
        for i in range(n_p):
            for k, (px, py) in enumerate(chips):
                forward(k, i, 2 * px + py, 1 - c).wait_recv()
        for cp in sends + forwards:
            cp.wait_send()

    anyspec = pl.BlockSpec(memory_space=pl.ANY)
    sems = pltpu.SemaphoreType.DMA((3 * n_p,))
    return pl.pallas_call(
        body,
        name=name,
        in_specs=[anyspec] * n,
        out_specs=[anyspec] * n,
        out_shape=[jax.ShapeDtypeStruct(b.shape, b.dtype) for b in bufs],
        input_output_aliases={a: a for a in range(n)},
        scratch_shapes=[sems, sems, sems, sems],
    )(*bufs)


def _swap_plan(n_slabs):
    def plan(refs):
        x, y, c = _coords()
        out, k = [], 0
        for i, n in enumerate(n_slabs):
            g, land = refs[2 * i], refs[2 * i + 1]
            for p in range(n):
                out.append((k, (g.at[p, 1 - c], land.at[p], (x, y, 1 - c), None), (land.at[p], None)))
                k += 1
        return out
    return plan


def _is_one_of(chip, dests):
    hit = chip == dests[0]
    for d in dests[1:]:
        hit = hit | (chip == d)
    return hit


def _slab_of(chip, dests):
    return sum(j * (chip == d).astype(jnp.int32) for j, d in enumerate(dests))


def _scatter_plan(dest_sets):
    def plan(refs):
        x, y, c = _coords()
        me = 2 * x + y
        out = []
        for k, (px, py) in enumerate(_other_chips(x, y)):
            peer = 2 * px + py
            for i, dests in enumerate(dest_sets):
                cs, land = refs[2 * i], refs[2 * i + 1]
                everyone = len(dests) == N_CHIPS
                send = (cs.at[_slab_of(peer, dests)], land.at[k], (px, py, c),
                        None if everyone else _is_one_of(peer, dests))
                recv = (land.at[k], None if everyone else _is_one_of(me, dests))
                out.append((k * len(dest_sets) + i, send, recv))
        return out
    return plan


def _allgather_plan():
    def plan(refs):
        x, y, c = _coords()
        src, land = refs
        me = 4 * x + 2 * y + c
        out = []
        for r in range(1, 8):
            px = 1 - x if r & 4 else x
            py = 1 - y if r & 2 else y
            pc = 1 - c if r & 1 else c
            out.append((r - 1, (src, land.at[me], (px, py, pc), None), (land.at[4 * px + 2 * py + pc], None)))
        return out
    return plan


def _sum_gathered(own, land, name):
    def body(own_ref, land_ref, o_ref):
        x, y, c = _coords()
        me = 4 * x + 2 * y + c
        acc = jnp.zeros(own_ref.shape, F32)
        for d in range(8):
            acc = acc + (land_ref[d] + jnp.where(me == d, own_ref[...], 0.0))
        o_ref[...] = acc

    return pl.pallas_call(
        body,
        name=name,
        out_shape=jax.ShapeDtypeStruct(own.shape, F32),
        compiler_params=_params(),
    )(own, land)


def _join_halves(bufs, n_chunks, name):
    n = len(bufs)
    pieces = [(a, rows) for a in range(n) for rows in _chunks(bufs[a].shape[1], n_chunks[a])]
    n_p = len(pieces)

    def body(*refs):
        outs = refs[n:2 * n]
        send_sems, recv_sems = refs[2 * n:]
        x, y, c = _coords()

        def copy(i, half):
            a, rows = pieces[i]
            return pltpu.make_async_remote_copy(
                src_ref=outs[a].at[half, rows], dst_ref=outs[a].at[half, rows], send_sem=send_sems.at[i],
                recv_sem=recv_sems.at[i], device_id=(x, y, 1 - c), device_id_type=MESH)

        sends = [copy(i, c) for i in range(n_p)]
        for cp in sends:
            cp.start()
        for i in range(n_p):
            copy(i, 1 - c).wait_recv()
        for cp in sends:
            cp.wait_send()

    anyspec = pl.BlockSpec(memory_space=pl.ANY)
    sems = pltpu.SemaphoreType.DMA((n_p,))
    return pl.pallas_call(
        body,
        name=name,
        in_specs=[anyspec] * n,
        out_specs=[anyspec] * n,
        out_shape=[jax.ShapeDtypeStruct(b.shape, b.dtype) for b in bufs],
        input_output_aliases={a: a for a in range(n)},
        scratch_shapes=[sems, sems],
    )(*bufs)


def _row_tile(rows, cap):
    t = cap
    while rows % t:
        t //= 2
    return t


def _add_my_half(g, r, name):
    n_slabs, _, R, C = g.shape
    tr = _row_tile(R, 256)

    def body(c_ref, g_ref, r_ref, o_ref):
        o_ref[...] = (g_ref[0] + r_ref[...]).astype(o_ref.dtype)

    return pl.pallas_call(
        body,
        name=name,
        grid_spec=pltpu.PrefetchScalarGridSpec(
            num_scalar_prefetch=1,
            grid=(n_slabs, R // tr),
            in_specs=[pl.BlockSpec((1, 1, tr, C), lambda p, i, c_ref: (p, c_ref[0], i, 0)),
                      pl.BlockSpec((1, tr, C), lambda p, i, c_ref: (p, i, 0))],
            out_specs=pl.BlockSpec((1, tr, C), lambda p, i, c_ref: (p, i, 0)),
        ),
        out_shape=jax.ShapeDtypeStruct(r.shape, jnp.bfloat16),
        compiler_params=_params(("parallel", "parallel")),
    )(lax.axis_index("c").reshape(1).astype(jnp.int32), g, r)


def _sum_slabs(own, got, name):
    _, R, C = own.shape
    tr = _row_tile(R, 256)

    def body(s_ref, own_ref, got_ref, o_ref):
        o_ref[0] = ((own_ref[0].astype(F32) + got_ref[0].astype(F32)) + got_ref[1].astype(F32)) + got_ref[2].astype(F32)

    xi, yi, ci = _coords()
    return pl.pallas_call(
        body,
        name=name,
        grid_spec=pltpu.PrefetchScalarGridSpec(
            num_scalar_prefetch=1,
            grid=(R // tr,),
            in_specs=[pl.BlockSpec((1, tr, C), lambda i, s: (s[0], i, 0)),
                      pl.BlockSpec((3, tr, C), lambda i, s: (0, i, 0))],
            out_specs=pl.BlockSpec((1, tr, C), lambda i, s: (s[1], i, 0)),
        ),
        out_shape=jax.ShapeDtypeStruct((2, R, C), F32),
        compiler_params=_params(("parallel",)),
    )(jnp.stack([2 * xi + yi, ci]).astype(jnp.int32), own, got)


def _sum_parts(owns, gots, dest_sets, name):
    n = len(owns)
    _, R, C = owns[0].shape
    tr = _row_tile(R, 256)

    def body(s_ref, *refs):
        o_ref = refs[-1]
        total = jnp.zeros((tr, C), F32)
        for i in range(n):
            total = total + jnp.where(s_ref[2 + 2 * i] == 1, refs[i][0].astype(F32), 0.0)
        for i in range(n):
            got = refs[n + i]
            total = ((total + got[0].astype(F32)) + got[1].astype(F32)) + got[2].astype(F32)
        o_ref[0] = total

    xi, yi, ci = _coords()
    me = 2 * xi + yi
    scalars = [ci, ci]
    for dests in dest_sets:
        scalars += [_is_one_of(me, dests).astype(jnp.int32), _slab_of(me, dests)]
    own_spec = lambda i: pl.BlockSpec((1, tr, C), lambda r, s: (s[3 + 2 * i], r, 0))
    return pl.pallas_call(
        body,
        name=name,
        grid_spec=pltpu.PrefetchScalarGridSpec(
            num_scalar_prefetch=1,
            grid=(R // tr,),
            in_specs=[own_spec(i) for i in range(n)] + [pl.BlockSpec((3, tr, C), lambda r, s: (0, r, 0))] * n,
            out_specs=pl.BlockSpec((1, tr, C), lambda r, s: (s[0], r, 0)),
        ),
        out_shape=jax.ShapeDtypeStruct((2, R, C), F32),
        compiler_params=_params(("parallel",)),
    )(jnp.stack(scalars).astype(jnp.int32), *owns, *gots)


def _sum_rows8(g, m_per, name):
    n = g.shape[1]

    def body(g_ref, o_ref):
        acc = g_ref[0:m_per, :]
        for k in range(1, 8):
            acc = acc + g_ref[k * m_per:(k + 1) * m_per, :]
        o_ref[...] = acc

    return pl.pallas_call(
        body,
        name=name,
        out_shape=jax.ShapeDtypeStruct((m_per, n), F32),
        compiler_params=_params(),
    )(g)


def _adamw_math(w, g, m, v):
    m = ADAM_B1 * m + (1.0 - ADAM_B1) * g
    v = ADAM_B2 * v + (1.0 - ADAM_B2) * (g * g)
    m_hat = m / (1.0 - ADAM_B1 ** ADAM_STEP)
    v_hat = v / (1.0 - ADAM_B2 ** ADAM_STEP)
    delta = -ADAM_LR * (m_hat / (jnp.sqrt(v_hat) + ADAM_EPS) + ADAM_WD * w)
    return delta, m, v


def _adamw_big(w, g, m, v, name):
    R, C = w.shape
    tr = min(128, R)

    def body(w_ref, g_ref, m_ref, v_ref, d_out, m_out, v_out):
        d, mn, vn = _adamw_math(w_ref[...], g_ref[...], m_ref[...], v_ref[...])
        d_out[...] = d
        m_out[...] = mn
        v_out[...] = vn

    spec = pl.BlockSpec((tr, C), lambda i: (i, 0))
    return pl.pallas_call(
        body,
        name=name,
        grid=(R // tr,),
        in_specs=[spec] * 4,
        out_specs=[spec] * 3,
        out_shape=[jax.ShapeDtypeStruct((R, C), F32)] * 3,
        compiler_params=_params(("parallel",)),
    )(w, g, m, v)


def _adamw_small(ws, gs, ms, vs, name):
    n = len(ws)

    def body(*refs):
        for a in range(n):
            d, mn, vn = _adamw_math(refs[a][...], refs[n + a][...], refs[2 * n + a][...], refs[3 * n + a][...])
            refs[4 * n + a][...] = d
            refs[5 * n + a][...] = mn
            refs[6 * n + a][...] = vn

    shapes = [jax.ShapeDtypeStruct(w.shape, F32) for w in ws]
    outs = pl.pallas_call(
        body,
        name=name,
        out_shape=shapes * 3,
        compiler_params=_params(),
    )(*ws, *gs, *ms, *vs)
    return outs[:n], outs[n:2 * n], outs[2 * n:]


def _to_blockdiag(w):
    per = CW // LRU_BW
    w4 = w.reshape(N_CT, per, LRU_BW, LRU_BW)
    eye = jnp.eye(per, dtype=w.dtype)
    return (w4[:, :, :, None, :] * eye[None, :, None, :, None]).reshape(N_CT, CW, CW)


def _from_blockdiag(g):
    per = CW // LRU_BW
    g5 = g.reshape(N_CT, per, LRU_BW, per, LRU_BW)
    return jnp.stack([g5[:, b, :, b, :] for b in range(per)], axis=1).reshape(LRU_BLOCKS, LRU_BW, LRU_BW)


def _local_grads(x2d, tgt2d, B, S, g_in, w_all, conv_w, conv_b, gate_x_w, gate_x_b, gate_a_w, gate_a_b, lam, gain,
                 proj_weights, g_fin, reduce, deps=()):
    wx_bd = _c(_to_blockdiag(gate_x_w))
    wa_bd = _c(_to_blockdiag(gate_a_w))
    tables = _retention_tables(S)
    gain3 = gain.reshape(HEADS, 1, DK)

    proj, ht = _inproj_fwd(x2d, g_in, w_all, deps)
    hlru, ya = _lru_fwd(proj, conv_w, conv_b, wx_bd, wa_bd, gate_x_b, gate_a_b, lam, B, S)
    o_pre, yb, states = _ret_fwd(proj, tables, gain3, B, S)
    wpa, wpb, wout = proj_weights(yb)
    loss, dx2, dya, dyb, dm, dgf, gw_proj = _mid(ya, yb, proj, x2d, tgt2d, wpa, wpb, wout, g_fin)
    g3 = _inproj_bwd_dw(ht, [dm], "inproj_bwd_dw_m")
    deps = reduce.m_ready(gw_proj, g3)
    dr, dgain = _ret_bwd(dyb, o_pre, proj, states, tables, gain3, B, S, deps)
    deps = reduce.ret_done(dr)
    g12 = _inproj_bwd_dw(ht, [dr], "inproj_bwd_dw_r", deps)
    deps = reduce.r_ready(g12)
    dxa, dga, dcw, dcb, dwx_bd, dwa_bd, dbx, dba, dlam = _lru_bwd(
        dya, proj, hlru, conv_w, conv_b, wx_bd, wa_bd, gate_x_b, gate_a_b, lam, B, S, deps)
    deps = reduce.lru_done(dxa)
    small = dict(conv_w=dcw, conv_b=dcb, gate_x_w=_from_blockdiag(dwx_bd), gate_x_b=dbx,
                 gate_a_w=_from_blockdiag(dwa_bd), gate_a_b=dba, lru_lambda=dlam, gn_gain=dgain.reshape(HEADS, DK),
                 norm_final=dgf)
    deps = deps + reduce.small_ready(_pack_small(small))
    g0 = _inproj_bwd_dw(ht, [dxa, dga], "inproj_bwd_dw_a", deps)
    deps = reduce.a_ready(g0)
    dparts = [dxa, dga, dr, dm]
    n_tiles = x2d.shape[0] // min(DX_TILE, x2d.shape[0])
    first = n_tiles // 2
    done = None
    if first:
        done = _inproj_bwd_dx(dparts, w_all, x2d, dx2, g_in, 0, first, None, "inproj_bwd_dx_0", deps)
        deps = reduce.dx_half_done(done[0])
    grad_x, dgin = _inproj_bwd_dx(dparts, w_all, x2d, dx2, g_in, first, n_tiles - first, done, "inproj_bwd_dx_1", deps)
    if not first:
        reduce.dx_half_done(grad_x)
    return loss[0, 0], grad_x, dgin


ALL_CHIPS = (0, 1, 2, 3)


class _GradReduce:
    def __init__(self):
        self.pending = {}

    def _start(self, key, bufs, plan, n_copies, name):
        send_sems, recv_sems, bufs, token = _copies_start(bufs, plan, n_copies, name + "_start")
        self.pending[key] = (send_sems, recv_sems, bufs, plan, name + "_wait")
        return (token,)

    def _finish(self, key, after):
        send_sems, recv_sems, bufs, plan, name = self.pending.pop(key)
        return _copies_wait(send_sems, recv_sems, bufs, after, plan, name)

    def _swap(self, key, parts):
        bufs = []
        for g in parts:
            bufs += [g, lax.empty((g.shape[0],) + g.shape[2:], F32)]
        n_slabs = [g.shape[0] for g in parts]
        return self._start(key, bufs, _swap_plan(n_slabs), sum(n_slabs), "swap_" + key)

    def _chip_sums(self, key, after):
        bufs = self._finish(key, after)
        return [_add_my_half(bufs[2 * i], bufs[2 * i + 1], "chip_sum_%s%d" % (key, i)) for i in range(len(bufs) // 2)]

    def _scatter(self, key, sums, dest_sets):
        bufs = []
        for cs in sums:
            bufs += [cs, jnp.zeros((3,) + cs.shape[1:], cs.dtype)]
        return self._start(key, bufs, _scatter_plan(dest_sets), 3 * len(sums), "scatter_" + key)

    def m_ready(self, gw_proj, g3):
        rows = gw_proj.shape[2] * gw_proj.shape[3]
        return self._swap("m", [gw_proj.reshape(N_CHIPS, 2, rows, D_MODEL), g3])

    def ret_done(self, after):
        return self._scatter("sm", self._chip_sums("m", after), [ALL_CHIPS, (3,)])

    def r_ready(self, g12):
        return self._swap("r", [g12])

    def lru_done(self, after):
        return self._scatter("sr", self._chip_sums("r", after), [(1, 2)])

    def small_ready(self, packed):
        land = jnp.zeros((8,) + packed.shape, F32)
        return self._start("small", [packed, land], _allgather_plan(), 7, "gather_small")

    def a_ready(self, g0):
        return self._swap("a", [g0])

    def dx_half_done(self, after):
        return self._scatter("sa", self._chip_sums("a", after), [(0,)])

    def finish(self, after):
        small_sum = _sum_gathered(*self._finish("small", after), "sum_small_grads")
        csp, gotp, cs3, got3 = self._finish("sm", after)
        cs12, got12 = self._finish("sr", after)
        cs0, got0 = self._finish("sa", after)
        half_in = _sum_parts([cs3, cs12, cs0], [got3, got12, got0], [(3,), (1, 2), (0,)], "sum_w_in")
        half_pr = _sum_slabs(csp, gotp, "sum_w_proj")
        return small_sum, _join_halves([half_in, half_pr], [8, 4], "join_halves")


_SMALL = ("gate_x_w", "gate_a_w", "conv_w", "conv_b", "gate_x_b", "gate_a_b", "lru_lambda", "gn_gain", "norm_final")
_SMALL_SHAPES = dict(gate_x_w=(LRU_BLOCKS, LRU_BW, LRU_BW), gate_a_w=(LRU_BLOCKS, LRU_BW, LRU_BW),
                     norm_in=(1, D_MODEL), conv_w=(CONV, D_MODEL), conv_b=(1, D_MODEL), gate_x_b=(1, D_MODEL),
                     gate_a_b=(1, D_MODEL), lru_lambda=(1, D_MODEL), gn_gain=(HEADS, DK), norm_final=(1, D_MODEL))


def _pack_small(small):
    return jnp.concatenate([small[k].reshape(-1, 128) for k in _SMALL], axis=0)


def _unpack_small(packed):
    out, r = {}, 0
    for k in _SMALL:
        shape = _SMALL_SHAPES[k]
        rows = 1
        for s in shape:
            rows *= s
        rows //= 128
        out[k] = packed[r:r + rows].reshape(shape)
        r += rows
    return out


def kernel(x, norm_in, w_in, conv_w, conv_b, gate_x_w, gate_x_b, gate_a_w, gate_a_b, lru_lambda, gn_gain, w_proj_a, w_proj_b, w_out, norm_final, loss_target, m_norm_in, m_w_in, m_conv_w, m_conv_b, m_gate_x_w, m_gate_x_b, m_gate_a_w, m_gate_a_b, m_lru_lambda, m_gn_gain, m_w_proj_a, m_w_proj_b, m_w_out, m_norm_final, v_norm_in, v_w_in, v_conv_w, v_conv_b, v_gate_x_w, v_gate_x_b, v_gate_a_w, v_gate_a_b, v_lru_lambda, v_gn_gain, v_w_proj_a, v_w_proj_b, v_w_out, v_norm_final):
    B, S, _ = x.shape
    T = B * S
    xi, yi, ci = _coords()
    chip = 2 * xi + yi

    cshard = D_MODEL // N_CHIPS
    mine = _cast_into_slot([w_in[0].reshape(2, D_MODEL // 2, 2 * D_MODEL)]
                           + [w[0].reshape(2, cshard // 2, D_MODEL) for w in (w_proj_a, w_proj_b, w_out)],
                           "cast_weights")
    plan = _gather_plan(3)
    s_sems, r_sems, pbufs, token = _copies_start(mine[1:], plan, 9, "gather_proj_start")
    w_all = _gather_chips(mine[:1], [4], "gather_weights")[0].reshape(N_CHIPS, D_MODEL, 2 * D_MODEL)

    def proj_weights(after):
        got = _copies_wait(s_sems, r_sems, pbufs, after, plan, "gather_proj_wait")
        return [b.reshape(D_MODEL, D_MODEL) for b in got]

    gshard = DK // N_CHIPS
    tiny = jnp.concatenate([conv_w[0], jnp.zeros((4, cshard), F32), jnp.pad(gn_gain[0], ((0, 4), (0, cshard - gshard)))],
                           axis=0)
    tiny_all = _all_gather8(tiny, "gather_small_weights").reshape(N_CHIPS, 2, 16, cshard)[:, 0]
    conv_w_full = jnp.transpose(tiny_all[:, 0:CONV, :], (1, 0, 2)).reshape(CONV, D_MODEL)
    gain_full = jnp.transpose(tiny_all[:, 8:8 + HEADS, :gshard], (1, 0, 2)).reshape(HEADS, DK)

    reduce = _GradReduce()
    loss, grad_x, dgin = _local_grads(
        x.reshape(T, D_MODEL), loss_target.reshape(T, D_MODEL), B, S, norm_in, w_all, conv_w_full, conv_b,
        gate_x_w[0], gate_x_b, gate_a_w[0], gate_a_b, lru_lambda, gain_full, proj_weights,
        norm_final.reshape(1, D_MODEL), reduce, deps=(token,))
    loss = lax.psum(loss, ("x", "y", "c"))

    g_norm_in = _sum_rows8(_all_gather8(dgin.reshape(8, 128), "gather_norm_in_grad"), 8, "sum_norm_in_grad")
    small_sum, (g_in_full, g_pr_full) = reduce.finish(g_norm_in)
    gsm = _unpack_small(small_sum)
    gsm["norm_in"] = g_norm_in.reshape(1, D_MODEL)
    g_w_in = g_in_full.reshape(D_MODEL, 2 * D_MODEL)
    g_pr = g_pr_full.reshape(2, 3, D_MODEL // (2 * N_CHIPS), D_MODEL)
    g_wpa, g_wpb, g_wout = (g_pr[:, k].reshape(cshard, D_MODEL) for k in range(3))

    grads = dict(gsm)
    grads["conv_w"] = lax.dynamic_slice_in_dim(gsm["conv_w"], chip * cshard, cshard, axis=1)
    grads["gn_gain"] = lax.dynamic_slice_in_dim(gsm["gn_gain"], chip * gshard, gshard, axis=1)
    grads.update(w_in=g_w_in, w_proj_a=g_wpa, w_proj_b=g_wpb, w_out=g_wout)

    weights = dict(norm_in=norm_in, w_in=w_in, conv_w=conv_w, conv_b=conv_b, gate_x_w=gate_x_w, gate_x_b=gate_x_b,
                   gate_a_w=gate_a_w, gate_a_b=gate_a_b, lru_lambda=lru_lambda, gn_gain=gn_gain, w_proj_a=w_proj_a,
                   w_proj_b=w_proj_b, w_out=w_out, norm_final=norm_final)
    ms = dict(norm_in=m_norm_in, w_in=m_w_in, conv_w=m_conv_w, conv_b=m_conv_b, gate_x_w=m_gate_x_w,
              gate_x_b=m_gate_x_b, gate_a_w=m_gate_a_w, gate_a_b=m_gate_a_b, lru_lambda=m_lru_lambda, gn_gain=m_gn_gain,
              w_proj_a=m_w_proj_a, w_proj_b=m_w_proj_b, w_out=m_w_out, norm_final=m_norm_final)
    vs = dict(norm_in=v_norm_in, w_in=v_w_in, conv_w=v_conv_w, conv_b=v_conv_b, gate_x_w=v_gate_x_w,
              gate_x_b=v_gate_x_b, gate_a_w=v_gate_a_w, gate_a_b=v_gate_a_b, lru_lambda=v_lru_lambda, gn_gain=v_gn_gain,
              w_proj_a=v_w_proj_a, w_proj_b=v_w_proj_b, w_out=v_w_out, norm_final=v_norm_final)
    names = list(weights)
    grads = {k: grads[k].reshape(weights[k].shape) for k in names}

    delta, new_m, new_v = {}, {}, {}
    for k in ("w_in", "w_proj_a", "w_proj_b", "w_out"):
        shp = weights[k].shape
        two = lambda a: a.reshape(shp[1], shp[2])
        d, mn, vn = _adamw_big(two(weights[k]), two(grads[k]), two(ms[k]), two(vs[k]), "adamw_" + k)
        delta[k], new_m[k], new_v[k] = d.reshape(shp), mn.reshape(shp), vn.reshape(shp)
    smalls = [k for k in names if k not in delta]

    def view(a):
        return a.reshape(1, -1) if a.ndim == 1 else (a.reshape(a.shape[1:]) if a.ndim > 2 else a)

    ds, mns, vns = _adamw_small([view(weights[k]) for k in smalls], [view(grads[k]) for k in smalls],
                                [view(ms[k]) for k in smalls], [view(vs[k]) for k in smalls], "adamw_small")
    for k, d, mn, vn in zip(smalls, ds, mns, vns):
        shp = weights[k].shape
        delta[k], new_m[k], new_v[k] = d.reshape(shp), mn.reshape(shp), vn.reshape(shp)

    return (loss, grad_x.reshape(B, S, D_MODEL), *[grads[k] for k in names], *[delta[k] for k in names],
            *[new_m[k] for k in names], *[new_v[k] for k in names])
```

```python
import functools

import jax
import jax.numpy as jnp
from jax import lax
from jax.experimental import pallas as pl
from jax.experimental.pallas import tpu as pltpu

F32 = jnp.float32
_MXU = jnp.bfloat16

D_MODEL = 1024
N_GROUPS = 8
HEADS = 4
DK = 256
CHUNK = 128
CONV = 4
LRU_BLOCKS = 16
LRU_BW = 64
LRU_C = 8.0
ROPE_THETA = 10000.0
EPS = 1e-6
CW = 256
N_CT = D_MODEL // CW
N_CHIPS = 4
MESH = pl.DeviceIdType.MESH

ADAM_LR = 0.001
ADAM_B1 = 0.9
ADAM_B2 = 0.999
ADAM_EPS = 1e-08
ADAM_WD = 0.01
ADAM_STEP = 10

VMEM_LIMIT = 56 * 1024 * 1024


def _c(v):
    return v.astype(_MXU)


def _dot(a, b):
    return lax.dot_general(a, b, (((1,), (0,)), ((), ())), preferred_element_type=F32)


def _dot_nt(a, b):
    return lax.dot_general(a, b, (((1,), (1,)), ((), ())), preferred_element_type=F32)


def _dot_tn(a, b):
    return lax.dot_general(a, b, (((0,), (0,)), ((), ())), preferred_element_type=F32)


def _sigmoid(z):
    return 0.5 * jnp.tanh(0.5 * z) + 0.5


ANY_SPEC = pl.BlockSpec(memory_space=pl.ANY)


def _after(body, n_in, deps):
    n_deps = len(deps)

    def wrapped(*refs):
        return body(*refs[:n_in], *refs[n_in + n_deps:])

    return wrapped


def _params(sem=None):
    if sem is None:
        return pltpu.CompilerParams(vmem_limit_bytes=VMEM_LIMIT)
    return pltpu.CompilerParams(vmem_limit_bytes=VMEM_LIMIT, dimension_semantics=sem)


def _inproj_fwd(x2d, g_in, w_all, deps=()):
    T = x2d.shape[0]
    tm = min(512, T)
    n_i = T // tm

    def body(*refs):
        x_ref, g_ref, w_ref = refs[:3]
        proj_ref, ht_ref, h_all = refs[-3:]
        i = pl.program_id(1)
        rows = pl.ds(pl.multiple_of(i * tm, tm), tm)

        @pl.when(pl.program_id(0) == 0)
        def _():
            x = x_ref[...]
            r = lax.rsqrt(jnp.mean(x * x, axis=-1, keepdims=True) + EPS)
            h = x * r * g_ref[...]
            h_all[rows, :] = h.astype(h_all.dtype)
            ht_ref[...] = h.T.astype(ht_ref.dtype)

        proj_ref[...] = _dot(h_all[rows, :], w_ref[0])

    first = lambda j, i: jnp.where(j == 0, i, n_i - 1)
    return pl.pallas_call(
        body,
        name="inproj_fwd",
        grid=(N_GROUPS, n_i),
        in_specs=[
            pl.BlockSpec((tm, D_MODEL), lambda j, i: (first(j, i), 0)),
            pl.BlockSpec((1, D_MODEL), lambda j, i: (0, 0)),
            pl.BlockSpec((1, D_MODEL, D_MODEL), lambda j, i: (j // 2, 0, j % 2)),
        ] + [pl.BlockSpec(memory_space=pl.ANY)] * len(deps),
        out_specs=[
            pl.BlockSpec((tm, D_MODEL), lambda j, i: (i, j)),
            pl.BlockSpec((D_MODEL, tm), lambda j, i: (0, first(j, i))),
        ],
        out_shape=[
            jax.ShapeDtypeStruct((T, N_GROUPS * D_MODEL), F32),
            jax.ShapeDtypeStruct((D_MODEL, T), _MXU),
        ],
        scratch_shapes=[pltpu.VMEM((T, D_MODEL), _MXU)],
        compiler_params=_params(("arbitrary", "arbitrary")),
    )(x2d, g_in, w_all, *deps)


def _scan_fwd(a, u):
    n = a.shape[0]
    row = lax.broadcasted_iota(jnp.int32, a.shape, 0)
    s = 1
    while s < n:
        m = row >= s
        u = u + a * jnp.where(m, pltpu.roll(u, s, 0), 0.0)
        a = a * jnp.where(m, pltpu.roll(a, s, 0), 1.0)
        s *= 2
    return a, u


def _scan_bwd(b, g):
    n = b.shape[0]
    row = lax.broadcasted_iota(jnp.int32, b.shape, 0)
    s = 1
    while s < n:
        m = row < n - s
        g = g + b * jnp.where(m, pltpu.roll(g, n - s, 0), 0.0)
        b = b * jnp.where(m, pltpu.roll(b, n - s, 0), 1.0)
        s *= 2
    return b, g


LANES = 128
SUBLANES = 8


def _scan_scratch(tc):
    by_lanes = pltpu.VMEM((CW // LANES, tc, LANES), F32)
    return [by_lanes, by_lanes, pltpu.VMEM((tc // SUBLANES, CW), F32), pltpu.VMEM((tc, CW), F32)]


def _scan_tile(a, u, edge, la_ref, lh_ref, c_ref, dst_ref, reverse):
    n, w = a.shape
    groups = n // SUBLANES
    a3 = a.reshape(groups, SUBLANES, w)
    u3 = u.reshape(groups, SUBLANES, w)
    row = lax.broadcasted_iota(jnp.int32, a3.shape, 1)
    for s in (1, 2, 4):
        m = (row < SUBLANES - s) if reverse else (row >= s)
        shift = SUBLANES - s if reverse else s
        u3 = u3 + a3 * jnp.where(m, pltpu.roll(u3, shift, 1), 0.0)
        a3 = a3 * jnp.where(m, pltpu.roll(a3, shift, 1), 1.0)
    al = a3.reshape(n, w)
    hl = u3.reshape(n, w)
    blocks = w // LANES
    for q in range(blocks):
        la_ref[q] = al[:, q * LANES:(q + 1) * LANES]
        lh_ref[q] = hl[:, q * LANES:(q + 1) * LANES]
    ends = pl.ds(0 if reverse else SUBLANES - 1, groups, stride=SUBLANES)
    end_a = jnp.concatenate([la_ref.at[q][ends, :] for q in range(blocks)], axis=-1)
    end_h = jnp.concatenate([lh_ref.at[q][ends, :] for q in range(blocks)], axis=-1)
    prod, part = (_scan_bwd if reverse else _scan_fwd)(end_a, end_h)
    total = part + prod * edge
    g_row = lax.broadcasted_iota(jnp.int32, total.shape, 0)
    if reverse:
        c_ref[...] = jnp.where(g_row == groups - 1, edge, pltpu.roll(total, groups - 1, 0))
    else:
        c_ref[...] = jnp.where(g_row == 0, edge, pltpu.roll(total, 1, 0))
    for g in range(groups):
        rows = slice(g * SUBLANES, (g + 1) * SUBLANES)
        for q in range(blocks):
            cols = slice(q * LANES, (q + 1) * LANES)
            dst_ref[rows, cols] = lh_ref[q, rows, :] + la_ref[q, rows, :] * c_ref[g:g + 1, cols]


def _softplus_neg(lam):
    z = -lam
    return jnp.maximum(z, 0.0) + jnp.log1p(jnp.exp(-jnp.abs(z)))


def _lru_gates(xc, wx_ref, wa_ref, bx_ref, ba_ref, lam_ref):
    xcb = _c(xc)
    i_t = _sigmoid(_dot(xcb, wx_ref[0]) + bx_ref[...])
    r_t = _sigmoid(_dot(xcb, wa_ref[0]) + ba_ref[...])
    sp = _softplus_neg(lam_ref[...])
    log_a = (-LRU_C) * r_t * sp
    a = jnp.exp(log_a)
    mult = jnp.sqrt(1.0 - a * a)
    return xcb, i_t, r_t, sp, a, mult


def _conv_from_ext(ext_ref, xa, cw_ref, cb_ref, tc):
    return (cb_ref[...] + cw_ref[3:4, :] * xa + cw_ref[2:3, :] * ext_ref[7:7 + tc, :]
            + cw_ref[1:2, :] * ext_ref[6:6 + tc, :] + cw_ref[0:1, :] * ext_ref[5:5 + tc, :])


def _lru_fwd(proj, conv_w, conv_b, wx_bd, wa_bd, bx, ba, lam, B, S):
    T = B * S
    tc = min(256, S)
    nt = S // tc
    h8 = tc // 8

    def body(xa_ref, halo_ref, ga_ref, cw_ref, cb_ref, wx_ref, wa_ref, bx_ref, ba_ref, lam_ref,
             h_ref, ya_ref, ext_ref, carry_ref, la_ref, lh_ref, c_ref):
        t = pl.program_id(2)

        @pl.when(t == 0)
        def _():
            carry_ref[...] = jnp.zeros_like(carry_ref)

        xa = xa_ref[...]
        ext_ref[0:8, :] = jnp.where(t == 0, 0.0, halo_ref[...])
        ext_ref[8:8 + tc, :] = xa
        xc = _conv_from_ext(ext_ref, xa, cw_ref, cb_ref, tc)
        _, i_t, _, _, a, mult = _lru_gates(xc, wx_ref, wa_ref, bx_ref, ba_ref, lam_ref)
        u = mult * (i_t * xc)
        _scan_tile(a, u, carry_ref[7:8, :], la_ref, lh_ref, c_ref, h_ref, False)
        h = h_ref[...]
        carry_ref[...] = h[tc - 8:tc, :]
        ga = ga_ref[...]
        ya_ref[...] = (ga * _sigmoid(ga) * h).astype(ya_ref.dtype)

    row = lambda b, t: b * nt + t
    vec = pl.BlockSpec((1, CW), lambda b, c, t: (0, c))
    mat = pl.BlockSpec((1, CW, CW), lambda b, c, t: (c, 0, 0))
    return pl.pallas_call(
        body,
        name="lru_fwd",
        grid=(B, N_CT, nt),
        in_specs=[
            pl.BlockSpec((tc, CW), lambda b, c, t: (row(b, t), c)),
            pl.BlockSpec((8, CW), lambda b, c, t: (jnp.maximum(row(b, t) * h8 - 1, 0), c)),
            pl.BlockSpec((tc, CW), lambda b, c, t: (row(b, t), N_CT + c)),
            pl.BlockSpec((CONV, CW), lambda b, c, t: (0, c)),
            vec, mat, mat, vec, vec, vec,
        ],
        out_specs=[
            pl.BlockSpec((tc, CW), lambda b, c, t: (row(b, t), c)),
            pl.BlockSpec((tc, CW), lambda b, c, t: (row(b, t), c)),
        ],
        out_shape=[
            jax.ShapeDtypeStruct((T, D_MODEL), F32),
            jax.ShapeDtypeStruct((T, D_MODEL), _MXU),
        ],
        scratch_shapes=[pltpu.VMEM((tc + 8, CW), F32), pltpu.VMEM((8, CW), F32)] + _scan_scratch(tc)[:3],
        compiler_params=_params(("parallel", "parallel", "arbitrary")),
    )(proj, proj, proj, conv_w, conv_b, wx_bd, wa_bd, bx, ba, lam)


def _lru_bwd(dya, proj, hlru, conv_w, conv_b, wx_bd, wa_bd, bx, ba, lam, B, S, deps=()):
    T = B * S
    tc = min(256, S)
    nt = S // tc
    h8 = tc // 8

    def body(dya_ref, xa_ref, xhalo_ref, ga_ref, h_ref, hhalo_ref, cw_ref, cb_ref, wx_ref, wa_ref, bx_ref, ba_ref,
             lam_ref, dxa_ref, dga_ref, dcw_ref, dcb_ref, dwx_ref, dwa_ref, dbx_ref, dba_ref, dlam_ref,
             ext_ref, ext2_ref, carry_ref, dhalo_ref, la_ref, lh_ref, c_ref, dh_ref):
        b = pl.program_id(1)
        t = pl.program_id(2)
        tt = nt - 1 - t

        @pl.when(t == 0)
        def _():
            carry_ref[...] = jnp.zeros_like(carry_ref)
            dhalo_ref[...] = jnp.zeros_like(dhalo_ref)

        @pl.when((t == 0) & (b == 0))
        def _():
            for r in (dcw_ref, dcb_ref, dwx_ref, dwa_ref, dbx_ref, dba_ref, dlam_ref):
                r[...] = jnp.zeros_like(r)

        xa = xa_ref[...]
        ext_ref[0:8, :] = jnp.where(tt == 0, 0.0, xhalo_ref[...])
        ext_ref[8:8 + tc, :] = xa
        xc = _conv_from_ext(ext_ref, xa, cw_ref, cb_ref, tc)
        xcb, i_t, r_t, sp, a, mult = _lru_gates(xc, wx_ref, wa_ref, bx_ref, ba_ref, lam_ref)

        h = h_ref[...]
        ga = ga_ref[...]
        dya_t = dya_ref[...]
        sg = _sigmoid(ga)
        dga_ref[...] = (dya_t * h * (sg * (1.0 + ga * (1.0 - sg)))).astype(dga_ref.dtype)
        dlru = dya_t * (ga * sg)

        row = lax.broadcasted_iota(jnp.int32, a.shape, 0)
        coef = jnp.where(row == tc - 1, 1.0, pltpu.roll(a, tc - 1, 0))
        _scan_tile(coef, dlru, carry_ref[0:1, :], la_ref, lh_ref, c_ref, dh_ref, True)
        dh = dh_ref[...]
        ext2_ref[0:tc, :] = a * dh
        carry_ref[...] = ext2_ref[0:8, :]

        ext2_ref[0:8, :] = jnp.where(tt == 0, 0.0, hhalo_ref[...])
        ext2_ref[8:8 + tc, :] = h
        hprev = ext2_ref[7:7 + tc, :]

        da = dh * hprev
        ix = i_t * xc
        dmult = dh * ix
        di = dh * mult * xc
        dxc = dh * mult * i_t
        dlog_a = da * a - dmult * (a * a) / mult
        dr = dlog_a * ((-LRU_C) * sp)
        dlam_ref[...] += jnp.sum(dlog_a * r_t, axis=0, keepdims=True) * (LRU_C * _sigmoid(-lam_ref[...]))
        dza = dr * r_t * (1.0 - r_t)
        dzx = di * i_t * (1.0 - i_t)
        dzab = _c(dza)
        dzxb = _c(dzx)
        dxc = dxc + _dot_nt(dzxb, wx_ref[0]) + _dot_nt(dzab, wa_ref[0])
        dwx_ref[0] += _dot_tn(xcb, dzxb)
        dwa_ref[0] += _dot_tn(xcb, dzab)
        dbx_ref[...] += jnp.sum(dzx, axis=0, keepdims=True)
        dba_ref[...] += jnp.sum(dza, axis=0, keepdims=True)

        dcb_ref[...] += jnp.sum(dxc, axis=0, keepdims=True)
        dcw_ref[3:4, :] += jnp.sum(dxc * xa, axis=0, keepdims=True)
        dcw_ref[2:3, :] += jnp.sum(dxc * ext_ref[7:7 + tc, :], axis=0, keepdims=True)
        dcw_ref[1:2, :] += jnp.sum(dxc * ext_ref[6:6 + tc, :], axis=0, keepdims=True)
        dcw_ref[0:1, :] += jnp.sum(dxc * ext_ref[5:5 + tc, :], axis=0, keepdims=True)
        ext2_ref[0:tc, :] = dxc
        ext2_ref[tc:tc + 8, :] = dhalo_ref[...]
        dxa = (cw_ref[3:4, :] * dxc + cw_ref[2:3, :] * ext2_ref[1:1 + tc, :]
               + cw_ref[1:2, :] * ext2_ref[2:2 + tc, :] + cw_ref[0:1, :] * ext2_ref[3:3 + tc, :])
        dxa_ref[...] = dxa.astype(dxa_ref.dtype)
        dhalo_ref[...] = ext2_ref[0:8, :]

    row_of = lambda b, t: b * nt + (nt - 1 - t)
    tile = lambda off: pl.BlockSpec((tc, CW), lambda c, b, t: (row_of(b, t), off + c))
    halo = pl.BlockSpec((8, CW), lambda c, b, t: (jnp.maximum(row_of(b, t) * h8 - 1, 0), c))
    vec = pl.BlockSpec((1, CW), lambda c, b, t: (0, c))
    mat = pl.BlockSpec((1, CW, CW), lambda c, b, t: (c, 0, 0))
    cwspec = pl.BlockSpec((CONV, CW), lambda c, b, t: (0, c))
    return pl.pallas_call(
        _after(body, 13, deps),
        name="lru_bwd",
        grid=(N_CT, B, nt),
        in_specs=[tile(0), tile(0), halo, tile(N_CT), tile(0), halo, cwspec, vec, mat, mat, vec, vec, vec]
        + [ANY_SPEC] * len(deps),
        out_specs=[tile(0), tile(0), cwspec, vec, mat, mat, vec, vec, vec],
        out_shape=[
            jax.ShapeDtypeStruct((T, D_MODEL), _MXU),
            jax.ShapeDtypeStruct((T, D_MODEL), _MXU),
            jax.ShapeDtypeStruct((CONV, D_MODEL), F32),
            jax.ShapeDtypeStruct((1, D_MODEL), F32),
            jax.ShapeDtypeStruct((N_CT, CW, CW), F32),
            jax.ShapeDtypeStruct((N_CT, CW, CW), F32),
            jax.ShapeDtypeStruct((1, D_MODEL), F32),
            jax.ShapeDtypeStruct((1, D_MODEL), F32),
            jax.ShapeDtypeStruct((1, D_MODEL), F32),
        ],
        scratch_shapes=[pltpu.VMEM((tc + 8, CW), F32), pltpu.VMEM((tc + 8, CW), F32),
                        pltpu.VMEM((8, CW), F32), pltpu.VMEM((8, CW), F32)] + _scan_scratch(tc),
        compiler_params=_params(("parallel", "arbitrary", "arbitrary")),
    )(dya, proj, proj, proj, hlru, hlru, conv_w, conv_b, wx_bd, wa_bd, bx, ba, lam, *deps)


def _retention_tables(S):
    half = DK // 2
    freqs = ROPE_THETA ** (-jnp.arange(half, dtype=F32) / half)
    ang = jnp.arange(S, dtype=F32)[:, None] * freqs[None, :]
    log_g = jnp.log1p(-(2.0 ** (-5.0 - jnp.arange(HEADS, dtype=F32))))
    idx = jnp.arange(CHUNK, dtype=F32)
    diff = idx[:, None] - idx[None, :]
    inner = jnp.where(diff >= 0, jnp.exp(jnp.maximum(diff, 0.0)[None] * log_g[:, None, None]), 0.0)
    cross = jnp.exp((idx[None, :] + 1.0) * log_g[:, None])[:, :, None]
    state = jnp.exp((CHUNK - 1.0 - idx[None, :]) * log_g[:, None])[:, :, None]
    gam = jnp.broadcast_to(jnp.exp(CHUNK * log_g)[:, None, None], (HEADS, 1, DK))
    return jnp.cos(ang), jnp.sin(ang), inner, cross, state, gam


def _rot(x, cos, sin):
    half = DK // 2
    x1, x2 = x[:, :half], x[:, half:]
    return jnp.concatenate([x1 * cos - x2 * sin, x1 * sin + x2 * cos], axis=-1)


def _rot_t(y, cos, sin):
    half = DK // 2
    y1, y2 = y[:, :half], y[:, half:]
    return jnp.concatenate([y1 * cos + y2 * sin, y2 * cos - y1 * sin], axis=-1)


def _groupnorm(o):
    mu = jnp.mean(o, axis=-1, keepdims=True)
    oc = o - mu
    rs = lax.rsqrt(jnp.mean(oc * oc, axis=-1, keepdims=True) + EPS)
    return oc * rs, rs


def _ret_specs(S, chunk_of):
    nc = S // CHUNK
    qkv = lambda g: pl.BlockSpec((CHUNK, D_MODEL), lambda b, c: (b * nc + chunk_of(c), g))
    act = pl.BlockSpec((CHUNK, D_MODEL), lambda b, c: (b * nc + chunk_of(c), 0))
    rope = pl.BlockSpec((CHUNK, DK // 2), lambda b, c: (chunk_of(c), 0))
    dmat = pl.BlockSpec((HEADS, CHUNK, CHUNK), lambda b, c: (0, 0, 0))
    dvec = pl.BlockSpec((HEADS, CHUNK, 1), lambda b, c: (0, 0, 0))
    hrow = pl.BlockSpec((HEADS, 1, DK), lambda b, c: (0, 0, 0))
    rst = pl.BlockSpec((1, HEADS, DK, DK), lambda b, c: (b * nc + chunk_of(c), 0, 0, 0))
    return qkv, act, rope, dmat, dvec, hrow, rst


def _ret_fwd(proj, tables, gain3, B, S):
    T = B * S
    nc = S // CHUNK
    cos, sin, dmat_t, cd_t, sd_t, gam_t = tables

    def body(q_ref, k_ref, v_ref, gb_ref, cos_ref, sin_ref, dm_ref, cd_ref, sd_ref, gam_ref, gain_ref,
             o_ref, yb_ref, rs_ref, state_ref):
        @pl.when(pl.program_id(1) == 0)
        def _():
            state_ref[...] = jnp.zeros_like(state_ref)

        cos_t, sin_t = cos_ref[...], sin_ref[...]
        for h in range(HEADS):
            cols = slice(h * DK, (h + 1) * DK)
            qb = _c(_rot(q_ref[:, cols], cos_t, sin_t))
            kb = _c(_rot(k_ref[:, cols], cos_t, sin_t) * (DK ** -0.5))
            v = v_ref[:, cols]
            state = state_ref[h]
            sb = _c(state)
            rs_ref[0, h] = sb
            scores = _dot_nt(qb, kb) * dm_ref[h]
            o = _dot(_c(scores), _c(v)) + _dot(qb, sb) * cd_ref[h]
            state_ref[h] = gam_ref[h] * state + _dot_tn(kb, _c(v * sd_ref[h]))
            o_ref[:, cols] = o
            n, _ = _groupnorm(o)
            gb = gb_ref[:, cols]
            yb_ref[:, cols] = (gb * _sigmoid(gb) * (n * gain_ref[h])).astype(yb_ref.dtype)

    qkv, act, rope, dmat, dvec, hrow, rst = _ret_specs(S, lambda c: c)
    return pl.pallas_call(
        body,
        name="ret_fwd",
        grid=(B, nc),
        in_specs=[qkv(2), qkv(3), qkv(4), qkv(5), rope, rope, dmat, dvec, dvec, hrow, hrow],
        out_specs=[act, act, rst],
        out_shape=[
            jax.ShapeDtypeStruct((T, D_MODEL), F32),
            jax.ShapeDtypeStruct((T, D_MODEL), _MXU),
            jax.ShapeDtypeStruct((B * nc, HEADS, DK, DK), _MXU),
        ],
        scratch_shapes=[pltpu.VMEM((HEADS, DK, DK), F32)],
        compiler_params=_params(("parallel", "arbitrary")),
    )(proj, proj, proj, proj, cos, sin, dmat_t, cd_t, sd_t, gam_t, gain3)


def _ret_bwd(dyb, o_pre, proj, states, tables, gain3, B, S, deps=()):
    T = B * S
    nc = S // CHUNK
    cos, sin, dmat_t, cd_t, sd_t, gam_t = tables

    def body(dyb_ref, o_ref, q_ref, k_ref, v_ref, gb_ref, rs_ref, cos_ref, sin_ref, dm_ref, cd_ref, sd_ref, gam_ref,
             gain_ref, dr_ref, dgain_ref, dstate_ref):
        @pl.when(pl.program_id(1) == 0)
        def _():
            dstate_ref[...] = jnp.zeros_like(dstate_ref)

        @pl.when((pl.program_id(1) == 0) & (pl.program_id(0) == 0))
        def _():
            dgain_ref[...] = jnp.zeros_like(dgain_ref)

        cos_t, sin_t = cos_ref[...], sin_ref[...]
        for h in range(HEADS):
            cols = slice(h * DK, (h + 1) * DK)
            gain = gain_ref[h]
            n, rs = _groupnorm(o_ref[:, cols])
            gb = gb_ref[:, cols]
            sg = _sigmoid(gb)
            dy = dyb_ref[:, cols]
            part = lambda g: slice(g * D_MODEL + h * DK, g * D_MODEL + (h + 1) * DK)
            dr_ref[:, part(3)] = (dy * (n * gain) * (sg * (1.0 + gb * (1.0 - sg)))).astype(dr_ref.dtype)
            dgn = dy * (gb * sg)
            dgain_ref[h] += jnp.sum(dgn * n, axis=0, keepdims=True)
            dn = dgn * gain
            do = rs * (dn - jnp.mean(dn, axis=-1, keepdims=True) - n * jnp.mean(dn * n, axis=-1, keepdims=True))

            qb = _c(_rot(q_ref[:, cols], cos_t, sin_t))
            kb = _c(_rot(k_ref[:, cols], cos_t, sin_t) * (DK ** -0.5))
            v = v_ref[:, cols]
            vb = _c(v)
            vsb = _c(v * sd_ref[h])
            dob = _c(do)
            docb = _c(do * cd_ref[h])
            dmat = dm_ref[h]
            dstate = dstate_ref[h]
            dsb = _c(dstate)
            pb = _c(_dot_nt(qb, kb) * dmat)
            dsc = _c(_dot_nt(dob, vb) * dmat)
            dq = _dot(dsc, kb) + _dot_nt(docb, rs_ref[0, h])
            dk = _dot_tn(dsc, qb) + _dot_nt(vsb, dsb)
            dv = _dot_tn(pb, dob) + _dot(kb, dsb) * sd_ref[h]
            dstate_ref[h] = gam_ref[h] * dstate + _dot_tn(qb, docb)
            dr_ref[:, part(0)] = _rot_t(dq, cos_t, sin_t).astype(dr_ref.dtype)
            dr_ref[:, part(1)] = (_rot_t(dk, cos_t, sin_t) * (DK ** -0.5)).astype(dr_ref.dtype)
            dr_ref[:, part(2)] = dv.astype(dr_ref.dtype)

    qkv, act, rope, dmat, dvec, hrow, rst = _ret_specs(S, lambda c: nc - 1 - c)
    wide = pl.BlockSpec((CHUNK, 4 * D_MODEL), lambda b, c: (b * nc + nc - 1 - c, 0))
    return pl.pallas_call(
        _after(body, 14, deps),
        name="ret_bwd",
        grid=(B, nc),
        in_specs=[act, act, qkv(2), qkv(3), qkv(4), qkv(5), rst, rope, rope, dmat, dvec, dvec, hrow, hrow]
        + [ANY_SPEC] * len(deps),
        out_specs=[wide, hrow],
        out_shape=[jax.ShapeDtypeStruct((T, 4 * D_MODEL), _MXU), jax.ShapeDtypeStruct((HEADS, 1, DK), F32)],
        scratch_shapes=[pltpu.VMEM((HEADS, DK, DK), F32)],
        compiler_params=_params(("arbitrary", "arbitrary")),
    )(dyb, o_pre, proj, proj, proj, proj, states, cos, sin, dmat_t, cd_t, sd_t, gam_t, gain3, *deps)


def _mid(ya, yb, proj, x2d, tgt2d, wpa, wpb, wout, g_fin):
    T = x2d.shape[0]
    tm = min(256, T)
    n_steps = T // tm
    rows = D_MODEL // (2 * N_CHIPS)

    def body(ya_ref, yb_ref, ma_ref, mb_ref, x_ref, t_ref, gf_ref, wpa_hbm, wpb_hbm, wout_hbm,
             loss_ref, dx2_ref, dya_ref, dyb_ref, dm_ref, dgf_ref, gw_hbm, w_ref, acc_ref, sem):
        i = pl.program_id(0)

        @pl.when(i == 0)
        def _():
            loads = [pltpu.make_async_copy(src, w_ref.at[k], sem.at[k]) for k, src in enumerate((wpa_hbm, wpb_hbm, wout_hbm))]
            for cp in loads:
                cp.start()
            for cp in loads:
                cp.wait()
            acc_ref[...] = jnp.zeros_like(acc_ref)
            loss_ref[...] = jnp.zeros_like(loss_ref)
            dgf_ref[...] = jnp.zeros_like(dgf_ref)

        ya_t, yb_t = ya_ref[...], yb_ref[...]
        out_a = _dot(ya_t, w_ref[0])
        out_b = _dot(yb_t, w_ref[1])
        sa = _sigmoid(ma_ref[...])
        sb = _sigmoid(mb_ref[...])
        mgb = _c(sa * out_a + sb * out_b)
        x2 = x_ref[...] + _dot(mgb, w_ref[2])
        r2 = lax.rsqrt(jnp.mean(x2 * x2, axis=-1, keepdims=True) + EPS)
        nx = x2 * r2
        gf = gf_ref[...]
        err = nx * gf - t_ref[...]
        loss_ref[...] += 0.5 * jnp.sum(jnp.mean(err * err, axis=-1, keepdims=True), axis=0, keepdims=True)
        dy = err * (1.0 / D_MODEL)
        dgf_ref[...] += jnp.sum(dy * nx, axis=0, keepdims=True)
        dyg = dy * gf
        dx2 = r2 * (dyg - nx * jnp.mean(dyg * nx, axis=-1, keepdims=True))
        dx2_ref[...] = dx2
        dx2b = _c(dx2)
        dmg = _dot_nt(dx2b, w_ref[2])
        acc_ref[2] += _dot_tn(mgb, dx2b)
        dm_ref[:, :D_MODEL] = (dmg * out_a * sa * (1.0 - sa)).astype(dm_ref.dtype)
        dm_ref[:, D_MODEL:] = (dmg * out_b * sb * (1.0 - sb)).astype(dm_ref.dtype)
        dab = _c(dmg * sa)
        dbb = _c(dmg * sb)
        dya_ref[...] = _dot_nt(dab, w_ref[0])
        dyb_ref[...] = _dot_nt(dbb, w_ref[1])
        acc_ref[0] += _dot_tn(ya_t, dab)
        acc_ref[1] += _dot_tn(yb_t, dbb)

        @pl.when(i == n_steps - 1)
        def _():
            copies = [pltpu.make_async_copy(acc_ref.at[k, pl.ds((2 * p + hf) * rows, rows), :], gw_hbm.at[p, hf, k],
                                            sem.at[(k * N_CHIPS + p) * 2 + hf])
                      for k in range(3) for p in range(N_CHIPS) for hf in range(2)]
            for cp in copies:
                cp.start()
            for cp in copies:
                cp.wait()

    tile = lambda j: pl.BlockSpec((tm, D_MODEL), lambda i: (i, j))
    one = pl.BlockSpec((1, D_MODEL), lambda i: (0, 0))
    anyspec = pl.BlockSpec(memory_space=pl.ANY)
    return pl.pallas_call(
        body,
        name="mid",
        grid=(n_steps,),
        in_specs=[tile(0), tile(0), tile(6), tile(7), tile(0), tile(0), one, anyspec, anyspec, anyspec],
        out_specs=[pl.BlockSpec((1, 1), lambda i: (0, 0)), tile(0), tile(0), tile(0),
                   pl.BlockSpec((tm, 2 * D_MODEL), lambda i: (i, 0)), one, anyspec],
        out_shape=[
            jax.ShapeDtypeStruct((1, 1), F32),
            jax.ShapeDtypeStruct((T, D_MODEL), F32),
            jax.ShapeDtypeStruct((T, D_MODEL), F32),
            jax.ShapeDtypeStruct((T, D_MODEL), F32),
            jax.ShapeDtypeStruct((T, 2 * D_MODEL), _MXU),
            jax.ShapeDtypeStruct((1, D_MODEL), F32),
            jax.ShapeDtypeStruct((N_CHIPS, 2, 3, rows, D_MODEL), F32),
        ],
        scratch_shapes=[pltpu.VMEM((3, D_MODEL, D_MODEL), _MXU), pltpu.VMEM((3, D_MODEL, D_MODEL), F32),
                        pltpu.SemaphoreType.DMA((3 * N_CHIPS * 2,))],
        compiler_params=_params(("arbitrary",)),
    )(ya, yb, proj, proj, x2d, tgt2d, g_fin, wpa, wpb, wout)


DX_TILE = 512


def _inproj_bwd_dx(dparts, w_all, x2d, dx2, g_in, first, count, prev, name, deps=()):
    T = x2d.shape[0]
    tm = min(DX_TILE, T)
    n_d = len(dparts)
    groups = [(a, k) for a, d in enumerate(dparts) for k in range(d.shape[1] // D_MODEL)]
    dg_start = jnp.zeros((1, D_MODEL), F32) if prev is None else prev[1]
    carried = () if prev is None else (prev[0],)

    def body(*refs):
        d_refs = refs[:n_d]
        x_ref, dx2_ref, g_ref, dg0_ref, w_hbm = refs[n_d:n_d + 5]
        dx_ref, dg_ref, w_ref, sem = refs[-4:]

        @pl.when(pl.program_id(0) == 0)
        def _():
            cp = pltpu.make_async_copy(w_hbm, w_ref, sem)
            cp.start()
            cp.wait()
            dg_ref[...] = dg0_ref[...]

        dh = jnp.zeros((tm, D_MODEL), F32)
        for j, (a, k) in enumerate(groups):
            dh = dh + _dot_nt(d_refs[a][:, k * D_MODEL:(k + 1) * D_MODEL],
                              w_ref[j // 2, :, (j % 2) * D_MODEL:(j % 2 + 1) * D_MODEL])
        x = x_ref[...]
        r = lax.rsqrt(jnp.mean(x * x, axis=-1, keepdims=True) + EPS)
        nx = x * r
        dg_ref[...] += jnp.sum(dh * nx, axis=0, keepdims=True)
        dhg = dh * g_ref[...]
        dx_ref[...] = dx2_ref[...] + r * (dhg - nx * jnp.mean(dhg * nx, axis=-1, keepdims=True))

    tile = pl.BlockSpec((tm, D_MODEL), lambda i: (first + i, 0))
    one = pl.BlockSpec((1, D_MODEL), lambda i: (0, 0))
    return pl.pallas_call(
        body,
        name=name,
        grid=(count,),
        in_specs=[pl.BlockSpec((tm, d.shape[1]), lambda i: (first + i, 0)) for d in dparts]
        + [tile, tile, one, one, ANY_SPEC] + [ANY_SPEC] * (len(carried) + len(deps)),
        out_specs=[tile, one],
        out_shape=[jax.ShapeDtypeStruct((T, D_MODEL), F32), jax.ShapeDtypeStruct((1, D_MODEL), F32)],
        input_output_aliases={n_d + 5: 0} if carried else {},
        scratch_shapes=[pltpu.VMEM(w_all.shape, w_all.dtype), pltpu.SemaphoreType.DMA],
        compiler_params=_params(("arbitrary",)),
    )(*dparts, x2d, dx2, g_in, dg_start, w_all, *carried, *deps)


def _inproj_bwd_dw(ht, dparts, name, deps=()):
    T = ht.shape[1]
    tn = 512
    half = D_MODEL // 2
    per_chip = 2 * D_MODEL // tn
    n_d = len(dparts)
    tiles = [(a, t) for a, d in enumerate(dparts) for t in range(d.shape[1] // tn)]
    offs = [sum(d.shape[1] // tn for d in dparts[:a]) for a in range(n_d)]

    def body(*refs):
        ht_ref = refs[0]
        d_refs = refs[1:1 + n_d]
        out_ref = refs[-1]
        t = pl.program_id(0)

        for a in range(n_d):
            lo, hi = offs[a], offs[a] + dparts[a].shape[1] // tn

            @pl.when((t >= lo) & (t < hi))
            def _(a=a):
                g = _dot(ht_ref[...], d_refs[a][...])
                out_ref[0, 0] = g[:half]
                out_ref[0, 1] = g[half:]

    def dspec(a):
        n_a = dparts[a].shape[1] // tn
        return pl.BlockSpec((T, tn), lambda t: (0, jnp.clip(t - offs[a], 0, n_a - 1)))

    return pl.pallas_call(
        body,
        name=name,
        grid=(len(tiles),),
        in_specs=[pl.BlockSpec((D_MODEL, T), lambda t: (0, 0))] + [dspec(a) for a in range(n_d)]
        + [ANY_SPEC] * len(deps),
        out_specs=pl.BlockSpec((1, 2, half, tn), lambda t: (t // per_chip, 0, 0, t % per_chip)),
        out_shape=jax.ShapeDtypeStruct((len(tiles) // per_chip, 2, half, 2 * D_MODEL), F32),
        compiler_params=_params(("parallel",)),
    )(ht, *dparts, *deps)


def _coords():
    return lax.axis_index("x"), lax.axis_index("y"), lax.axis_index("c")


def _other_chips(x, y):
    return [(1 - x, y), (x, 1 - y), (1 - x, 1 - y)]


def _all_gather8(xs, name, deps=()):
    m_per, n = xs.shape

    def body(x_ref, *rest):
        out_ref, send_sems, recv_sems, local_sem = rest[-4:]
        x, y, c = _coords()
        me, sibling = (x, y, c), (x, y, 1 - c)
        chips = _other_chips(x, y)

        def rows(px, py, pc):
            return out_ref.at[pl.ds((4 * px + 2 * py + pc) * m_per, m_per), :]

        def copy(k, block, to, src=None):
            return pltpu.make_async_remote_copy(
                src_ref=rows(*block) if src is None else src, dst_ref=rows(*block),
                send_sem=send_sems.at[k], recv_sem=recv_sems.at[k], device_id=to, device_id_type=MESH)

        mine = pltpu.make_async_copy(x_ref, rows(*me), local_sem)
        mine.start()
        first = [copy(0, me, sibling, src=x_ref)]
        first += [copy(1 + j, me, (*chip, c), src=x_ref) for j, chip in enumerate(chips)]
        for cp in first:
            cp.start()
        passed = [copy(4 + j, (*chip, c), sibling) for j, chip in enumerate(chips)]
        for j, chip in enumerate(chips):
            copy(1 + j, (*chip, c), me).wait_recv()
            passed[j].start()
        copy(0, sibling, me).wait_recv()
        for j, chip in enumerate(chips):
            copy(4 + j, (*chip, 1 - c), me).wait_recv()
        for cp in first + passed:
            cp.wait_send()
        mine.wait()

    return pl.pallas_call(
        body,
        name=name,
        out_shape=jax.ShapeDtypeStruct((8 * m_per, n), xs.dtype),
        in_specs=[pl.BlockSpec(memory_space=pltpu.VMEM)] + [ANY_SPEC] * len(deps),
        out_specs=pl.BlockSpec(memory_space=pltpu.VMEM),
        scratch_shapes=[pltpu.SemaphoreType.DMA((7,)), pltpu.SemaphoreType.DMA((7,)), pltpu.SemaphoreType.DMA],
        compiler_params=pltpu.CompilerParams(vmem_limit_bytes=VMEM_LIMIT),
    )(xs, *deps)


def _chunks(rows, n):
    size = rows // n
    return [pl.ds(q * size, size) for q in range(n)]


HBM_SPEC = pl.BlockSpec(memory_space=pltpu.HBM)
SEM_SPEC = pl.BlockSpec(memory_space=pltpu.SEMAPHORE)
DATAFLOW = pltpu.SideEffectType.DATAFLOW_SIDE_EFFECTING


def _copies_start(bufs, plan, n_copies, name):
    n = len(bufs)

    def body(*refs):
        ins = refs[:n]
        send_sems, recv_sems = refs[n], refs[n + 1]
        token = refs[-1]
        for k, send, _ in plan(ins):
            if send is not None:
                src, dst, dev, pred = send
                cp = pltpu.make_async_remote_copy(src_ref=src, dst_ref=dst, send_sem=send_sems.at[k],
                                                  recv_sem=recv_sems.at[k], device_id=dev, device_id_type=MESH)
                if pred is None:
                    cp.start()
                else:
                    pl.when(pred)(cp.start)
        token[...] = jnp.zeros_like(token)

    hbm = [pltpu.with_memory_space_constraint(b, pltpu.HBM) for b in bufs]
    outs = pl.pallas_call(
        body,
        name=name,
        in_specs=[HBM_SPEC] * n,
        out_specs=(SEM_SPEC, SEM_SPEC, *([HBM_SPEC] * n), pl.BlockSpec(memory_space=pltpu.VMEM)),
        out_shape=(pltpu.SemaphoreType.DMA((n_copies,)), pltpu.SemaphoreType.DMA((n_copies,)),
                   *[pltpu.HBM(b.shape, b.dtype) for b in bufs], jax.ShapeDtypeStruct((8, 128), F32)),
        input_output_aliases={a: 2 + a for a in range(n)},
        compiler_params=pltpu.CompilerParams(has_side_effects=DATAFLOW),
    )(*hbm)
    return outs[0], outs[1], list(outs[2:2 + n]), outs[-1]


def _copies_wait(send_sems, recv_sems, bufs, after, plan, name):
    n = len(bufs)

    def body(*refs):
        ins = refs[:n]
        s_sems, r_sems = refs[n], refs[n + 1]
        for k, send, recv in plan(ins):
            if send is not None:
                src, dst, dev, pred = send
                cp = pltpu.make_async_remote_copy(src_ref=src, dst_ref=dst, send_sem=s_sems.at[k],
                                                  recv_sem=r_sems.at[k], device_id=dev, device_id_type=MESH)
                if pred is None:
                    cp.wait_send()
                else:
                    pl.when(pred)(cp.wait_send)
            if recv is not None:
                dst, pred = recv
                cp = pltpu.make_async_remote_copy(src_ref=dst, dst_ref=dst, send_sem=s_sems.at[k],
                                                  recv_sem=r_sems.at[k], device_id=_coords(), device_id_type=MESH)
                if pred is None:
                    cp.wait_recv()
                else:
                    pl.when(pred)(cp.wait_recv)

    outs = pl.pallas_call(
        body,
        name=name,
        in_specs=[HBM_SPEC] * n + [SEM_SPEC, SEM_SPEC, pl.BlockSpec(memory_space=pl.ANY)],
        out_specs=[HBM_SPEC] * n,
        out_shape=[pltpu.HBM(b.shape, b.dtype) for b in bufs],
        input_output_aliases={a: a for a in range(n)},
        compiler_params=pltpu.CompilerParams(has_side_effects=DATAFLOW),
    )(*bufs, send_sems, recv_sems, after)
    return list(outs)


def _gather_plan(n_bufs):
    def plan(refs):
        x, y, c = _coords()
        me = 2 * x + y
        out = []
        for k, (px, py) in enumerate(_other_chips(x, y)):
            for a in range(n_bufs):
                out.append((k * n_bufs + a, (refs[a].at[me], refs[a].at[me], (px, py, c), None),
                            (refs[a].at[2 * px + py], None)))
        return out
    return plan


def _cast_into_slot(ws, name):
    n = len(ws)
    nt = 2

    def body(s_ref, *refs):
        for a in range(n):
            refs[n + a][0] = refs[a][...].astype(refs[n + a].dtype)

    xi, yi, _ = _coords()
    return pl.pallas_call(
        body,
        name=name,
        grid_spec=pltpu.PrefetchScalarGridSpec(
            num_scalar_prefetch=1,
            grid=(2, nt),
            in_specs=[pl.BlockSpec((1, w.shape[1] // nt, w.shape[2]), lambda hf, i, s: (hf, i, 0)) for w in ws],
            out_specs=[pl.BlockSpec((1, 1, w.shape[1] // nt, w.shape[2]), lambda hf, i, s: (s[0], hf, i, 0)) for w in ws],
        ),
        out_shape=[jax.ShapeDtypeStruct((N_CHIPS,) + w.shape, _MXU) for w in ws],
        compiler_params=_params(("parallel", "parallel")),
    )((2 * xi + yi).reshape(1).astype(jnp.int32), *ws)


def _gather_chips(bufs, n_chunks, name):
    n = len(bufs)
    pieces = [(a, rows) for a in range(n) for rows in _chunks(bufs[a].shape[2], n_chunks[a])]
    n_p = len(pieces)

    def body(*refs):
        outs = refs[n:2 * n]
        send_sems, recv_sems, fsend_sems, frecv_sems = refs[2 * n:]
        x, y, c = _coords()
        me = 2 * x + y
        chips = _other_chips(x, y)

        def send(k, i, slot, chip):
            a, rows = pieces[i]
            return pltpu.make_async_remote_copy(
                src_ref=outs[a].at[slot, c, rows], dst_ref=outs[a].at[slot, c, rows], send_sem=send_sems.at[k * n_p + i],
                recv_sem=recv_sems.at[k * n_p + i], device_id=(*chip, c), device_id_type=MESH)

        def forward(k, i, slot, half):
            a, rows = pieces[i]
            return pltpu.make_async_remote_copy(
                src_ref=outs[a].at[slot, half, rows], dst_ref=outs[a].at[slot, half, rows],
                send_sem=fsend_sems.at[k * n_p + i], recv_sem=frecv_sems.at[k * n_p + i],
                device_id=(x, y, 1 - c), device_id_type=MESH)

        sends = [send(k, i, me, chip) for i in range(n_p) for k, chip in enumerate(chips)]
        for cp in sends:
            cp.start()
        forwards = []
        for i in range(n_p):
            for k, (px, py) in enumerate(chips):
                send(k, i, 2 * px + py, (px, py)).wait_recv()
                fw = forward(k, i, 2 * px + py, c)
                fw.start()
                forwards.append(fw)
        for i in range(n_p):
            for k, (px, py) in enumerate(chips):
                forward(k, i, 2 * px + py, 1 - c).wait_recv()
        for cp in sends + forwards:
            cp.wait_send()

    anyspec = pl.BlockSpec(memory_space=pl.ANY)
    sems = pltpu.SemaphoreType.DMA((3 * n_p,))
    return pl.pallas_call(
        body,
        name=name,
        in_specs=[anyspec] * n,
        out_specs=[anyspec] * n,
        out_shape=[jax.ShapeDtypeStruct(b.shape, b.dtype) for b in bufs],
        input_output_aliases={a: a for a in range(n)},
        scratch_shapes=[sems, sems, sems, sems],
    )(*bufs)


def _swap_plan(n_slabs):
    def plan(refs):
        x, y, c = _coords()
        out, k = [], 0
        for i, n in enumerate(n_slabs):
            g, land = refs[2 * i], refs[2 * i + 1]
            for p in range(n):
                out.append((k, (g.at[p, 1 - c], land.at[p], (x, y, 1 - c), None), (land.at[p], None)))
                k += 1
        return out
    return plan


def _is_one_of(chip, dests):
    hit = chip == dests[0]
    for d in dests[1:]:
        hit = hit | (chip == d)
    return hit


def _slab_of(chip, dests):
    return sum(j * (chip == d).astype(jnp.int32) for j, d in enumerate(dests))


def _scatter_plan(dest_sets):
    def plan(refs):
        x, y, c = _coords()
        me = 2 * x + y
        out = []
        for k, (px, py) in enumerate(_other_chips(x, y)):
            peer = 2 * px + py
            for i, dests in enumerate(dest_sets):
                cs, land = refs[2 * i], refs[2 * i + 1]
                everyone = len(dests) == N_CHIPS
                send = (cs.at[_slab_of(peer, dests)], land.at[k], (px, py, c),
                        None if everyone else _is_one_of(peer, dests))
                recv = (land.at[k], None if everyone else _is_one_of(me, dests))
                out.append((k * len(dest_sets) + i, send, recv))
        return out
    return plan


def _allgather_plan():
    def plan(refs):
        x, y, c = _coords()
        src, land = refs
        me = 4 * x + 2 * y + c
        out = []
        for r in range(1, 8):
            px = 1 - x if r & 4 else x
            py = 1 - y if r & 2 else y
            pc = 1 - c if r & 1 else c
            out.append((r - 1, (src, land.at[me], (px, py, pc), None), (land.at[4 * px + 2 * py + pc], None)))
        return out
    return plan


def _sum_gathered(own, land, name):
    def body(own_ref, land_ref, o_ref):
        x, y, c = _coords()
        me = 4 * x + 2 * y + c
        acc = jnp.zeros(own_ref.shape, F32)
        for d in range(8):
            acc = acc + (land_ref[d] + jnp.where(me == d, own_ref[...], 0.0))
        o_ref[...] = acc

    return pl.pallas_call(
        body,
        name=name,
        out_shape=jax.ShapeDtypeStruct(own.shape, F32),
        compiler_params=_params(),
    )(own, land)


def _join_halves(bufs, n_chunks, name):
    n = len(bufs)
    pieces = [(a, rows) for a in range(n) for rows in _chunks(bufs[a].shape[1], n_chunks[a])]
    n_p = len(pieces)

    def body(*refs):
        outs = refs[n:2 * n]
        send_sems, recv_sems = refs[2 * n:]
        x, y, c = _coords()

        def copy(i, half):
            a, rows = pieces[i]
            return pltpu.make_async_remote_copy(
                src_ref=outs[a].at[half, rows], dst_ref=outs[a].at[half, rows], send_sem=send_sems.at[i],
                recv_sem=recv_sems.at[i], device_id=(x, y, 1 - c), device_id_type=MESH)

        sends = [copy(i, c) for i in range(n_p)]
        for cp in sends:
            cp.start()
        for i in range(n_p):
            copy(i, 1 - c).wait_recv()
        for cp in sends:
            cp.wait_send()

    anyspec = pl.BlockSpec(memory_space=pl.ANY)
    sems = pltpu.SemaphoreType.DMA((n_p,))
    return pl.pallas_call(
        body,
        name=name,
        in_specs=[anyspec] * n,
        out_specs=[anyspec] * n,
        out_shape=[jax.ShapeDtypeStruct(b.shape, b.dtype) for b in bufs],
        input_output_aliases={a: a for a in range(n)},
        scratch_shapes=[sems, sems],
    )(*bufs)


def _row_tile(rows, cap):
    t = cap
    while rows % t:
        t //= 2
    return t


def _add_my_half(g, r, name):
    n_slabs, _, R, C = g.shape
    tr = _row_tile(R, 256)

    def body(c_ref, g_ref, r_ref, o_ref):
        o_ref[...] = (g_ref[0] + r_ref[...]).astype(o_ref.dtype)

    return pl.pallas_call(
        body,
        name=name,
        grid_spec=pltpu.PrefetchScalarGridSpec(
            num_scalar_prefetch=1,
            grid=(n_slabs, R // tr),
            in_specs=[pl.BlockSpec((1, 1, tr, C), lambda p, i, c_ref: (p, c_ref[0], i, 0)),
                      pl.BlockSpec((1, tr, C), lambda p, i, c_ref: (p, i, 0))],
            out_specs=pl.BlockSpec((1, tr, C), lambda p, i, c_ref: (p, i, 0)),
        ),
        out_shape=jax.ShapeDtypeStruct(r.shape, jnp.bfloat16),
        compiler_params=_params(("parallel", "parallel")),
    )(lax.axis_index("c").reshape(1).astype(jnp.int32), g, r)


def _sum_slabs(own, got, name):
    _, R, C = own.shape
    tr = _row_tile(R, 256)

    def body(s_ref, own_ref, got_ref, o_ref):
        o_ref[0] = ((own_ref[0].astype(F32) + got_ref[0].astype(F32)) + got_ref[1].astype(F32)) + got_ref[2].astype(F32)

    xi, yi, ci = _coords()
    return pl.pallas_call(
        body,
        name=name,
        grid_spec=pltpu.PrefetchScalarGridSpec(
            num_scalar_prefetch=1,
            grid=(R // tr,),
            in_specs=[pl.BlockSpec((1, tr, C), lambda i, s: (s[0], i, 0)),
                      pl.BlockSpec((3, tr, C), lambda i, s: (0, i, 0))],
            out_specs=pl.BlockSpec((1, tr, C), lambda i, s: (s[1], i, 0)),
        ),
        out_shape=jax.ShapeDtypeStruct((2, R, C), F32),
        compiler_params=_params(("parallel",)),
    )(jnp.stack([2 * xi + yi, ci]).astype(jnp.int32), own, got)


def _sum_parts(owns, gots, dest_sets, name):
    n = len(owns)
    _, R, C = owns[0].shape
    tr = _row_tile(R, 256)

    def body(s_ref, *refs):
        o_ref = refs[-1]
        total = jnp.zeros((tr, C), F32)
        for i in range(n):
            total = total + jnp.where(s_ref[2 + 2 * i] == 1, refs[i][0].astype(F32), 0.0)
        for i in range(n):
            got = refs[n + i]
            total = ((total + got[0].astype(F32)) + got[1].astype(F32)) + got[2].astype(F32)
        o_ref[0] = total

    xi, yi, ci = _coords()
    me = 2 * xi + yi
    scalars = [ci, ci]
    for dests in dest_sets:
        scalars += [_is_one_of(me, dests).astype(jnp.int32), _slab_of(me, dests)]
    own_spec = lambda i: pl.BlockSpec((1, tr, C), lambda r, s: (s[3 + 2 * i], r, 0))
    return pl.pallas_call(
        body,
        name=name,
        grid_spec=pltpu.PrefetchScalarGridSpec(
            num_scalar_prefetch=1,
            grid=(R // tr,),
            in_specs=[own_spec(i) for i in range(n)] + [pl.BlockSpec((3, tr, C), lambda r, s: (0, r, 0))] * n,
            out_specs=pl.BlockSpec((1, tr, C), lambda r, s: (s[0], r, 0)),
        ),
        out_shape=jax.ShapeDtypeStruct((2, R, C), F32),
        compiler_params=_params(("parallel",)),
    )(jnp.stack(scalars).astype(jnp.int32), *owns, *gots)


def _sum_rows8(g, m_per, name):
    n = g.shape[1]

    def body(g_ref, o_ref):
        acc = g_ref[0:m_per, :]
        for k in range(1, 8):
            acc = acc + g_ref[k * m_per:(k + 1) * m_per, :]
        o_ref[...] = acc

    return pl.pallas_call(
        body,
        name=name,
        out_shape=jax.ShapeDtypeStruct((m_per, n), F32),
        compiler_params=_params(),
    )(g)


def _adamw_math(w, g, m, v):
    m = ADAM_B1 * m + (1.0 - ADAM_B1) * g
    v = ADAM_B2 * v + (1.0 - ADAM_B2) * (g * g)
    m_hat = m / (1.0 - ADAM_B1 ** ADAM_STEP)
    v_hat = v / (1.0 - ADAM_B2 ** ADAM_STEP)
    delta = -ADAM_LR * (m_hat / (jnp.sqrt(v_hat) + ADAM_EPS) + ADAM_WD * w)
    return delta, m, v


def _adamw_big(w, g, m, v, name):
    R, C = w.shape
    tr = min(128, R)

    def body(w_ref, g_ref, m_ref, v_ref, d_out, m_out, v_out):
        d, mn, vn = _adamw_math(w_ref[...], g_ref[...], m_ref[...], v_ref[...])
        d_out[...] = d
        m_out[...] = mn
        v_out[...] = vn

    spec = pl.BlockSpec((tr, C), lambda i: (i, 0))
    return pl.pallas_call(
        body,
        name=name,
        grid=(R // tr,),
        in_specs=[spec] * 4,
        out_specs=[spec] * 3,
        out_shape=[jax.ShapeDtypeStruct((R, C), F32)] * 3,
        compiler_params=_params(("parallel",)),
    )(w, g, m, v)


def _adamw_small(ws, gs, ms, vs, name):
    n = len(ws)

    def body(*refs):
        for a in range(n):
            d, mn, vn = _adamw_math(refs[a][...], refs[n + a][...], refs[2 * n + a][...], refs[3 * n + a][...])
            refs[4 * n + a][...] = d
            refs[5 * n + a][...] = mn
            refs[6 * n + a][...] = vn

    shapes = [jax.ShapeDtypeStruct(w.shape, F32) for w in ws]
    outs = pl.pallas_call(
        body,
        name=name,
        out_shape=shapes * 3,
        compiler_params=_params(),
    )(*ws, *gs, *ms, *vs)
    return outs[:n], outs[n:2 * n], outs[2 * n:]


def _to_blockdiag(w):
    per = CW // LRU_BW
    w4 = w.reshape(N_CT, per, LRU_BW, LRU_BW)
    eye = jnp.eye(per, dtype=w.dtype)
    return (w4[:, :, :, None, :] * eye[None, :, None, :, None]).reshape(N_CT, CW, CW)


def _from_blockdiag(g):
    per = CW // LRU_BW
    g5 = g.reshape(N_CT, per, LRU_BW, per, LRU_BW)
    return jnp.stack([g5[:, b, :, b, :] for b in range(per)], axis=1).reshape(LRU_BLOCKS, LRU_BW, LRU_BW)


def _local_grads(x2d, tgt2d, B, S, g_in, w_all, conv_w, conv_b, gate_x_w, gate_x_b, gate_a_w, gate_a_b, lam, gain,
                 proj_weights, g_fin, reduce, deps=()):
    wx_bd = _c(_to_blockdiag(gate_x_w))
    wa_bd = _c(_to_blockdiag(gate_a_w))
    tables = _retention_tables(S)
    gain3 = gain.reshape(HEADS, 1, DK)

    proj, ht = _inproj_fwd(x2d, g_in, w_all, deps)
    hlru, ya = _lru_fwd(proj, conv_w, conv_b, wx_bd, wa_bd, gate_x_b, gate_a_b, lam, B, S)
    o_pre, yb, states = _ret_fwd(proj, tables, gain3, B, S)
    wpa, wpb, wout = proj_weights(yb)
    loss, dx2, dya, dyb, dm, dgf, gw_proj = _mid(ya, yb, proj, x2d, tgt2d, wpa, wpb, wout, g_fin)
    g3 = _inproj_bwd_dw(ht, [dm], "inproj_bwd_dw_m")
    deps = reduce.m_ready(gw_proj, g3)
    dr, dgain = _ret_bwd(dyb, o_pre, proj, states, tables, gain3, B, S, deps)
    deps = reduce.ret_done(dr)
    g12 = _inproj_bwd_dw(ht, [dr], "inproj_bwd_dw_r", deps)
    deps = reduce.r_ready(g12)
    dxa, dga, dcw, dcb, dwx_bd, dwa_bd, dbx, dba, dlam = _lru_bwd(
        dya, proj, hlru, conv_w, conv_b, wx_bd, wa_bd, gate_x_b, gate_a_b, lam, B, S, deps)
    deps = reduce.lru_done(dxa)
    small = dict(conv_w=dcw, conv_b=dcb, gate_x_w=_from_blockdiag(dwx_bd), gate_x_b=dbx,
                 gate_a_w=_from_blockdiag(dwa_bd), gate_a_b=dba, lru_lambda=dlam, gn_gain=dgain.reshape(HEADS, DK),
                 norm_final=dgf)
    deps = deps + reduce.small_ready(_pack_small(small))
    g0 = _inproj_bwd_dw(ht, [dxa, dga], "inproj_bwd_dw_a", deps)
    deps = reduce.a_ready(g0)
    dparts = [dxa, dga, dr, dm]
    n_tiles = x2d.shape[0] // min(DX_TILE, x2d.shape[0])
    first = (n_tiles + 3) // 4 if n_tiles > 1 else 0
    done = None
    if first:
        done = _inproj_bwd_dx(dparts, w_all, x2d, dx2, g_in, 0, first, None, "inproj_bwd_dx_0", deps)
        deps = reduce.dx_half_done(done[0])
    grad_x, dgin = _inproj_bwd_dx(dparts, w_all, x2d, dx2, g_in, first, n_tiles - first, done, "inproj_bwd_dx_1", deps)
    if not first:
        reduce.dx_half_done(grad_x)
    return loss[0, 0], grad_x, dgin


ALL_CHIPS = (0, 1, 2, 3)


class _GradReduce:
    def __init__(self):
        self.pending = {}

    def _start(self, key, bufs, plan, n_copies, name):
        send_sems, recv_sems, bufs, token = _copies_start(bufs, plan, n_copies, name + "_start")
        self.pending[key] = (send_sems, recv_sems, bufs, plan, name + "_wait")
        return (token,)

    def _finish(self, key, after):
        send_sems, recv_sems, bufs, plan, name = self.pending.pop(key)
        return _copies_wait(send_sems, recv_sems, bufs, after, plan, name)

    def _swap(self, key, parts):
        bufs = []
        for g in parts:
            bufs += [g, lax.empty((g.shape[0],) + g.shape[2:], F32)]
        n_slabs = [g.shape[0] for g in parts]
        return self._start(key, bufs, _swap_plan(n_slabs), sum(n_slabs), "swap_" + key)

    def _chip_sums(self, key, after):
        bufs = self._finish(key, after)
        return [_add_my_half(bufs[2 * i], bufs[2 * i + 1], "chip_sum_%s%d" % (key, i)) for i in range(len(bufs) // 2)]

    def _scatter(self, key, sums, dest_sets):
        bufs = []
        for cs in sums:
            bufs += [cs, jnp.zeros((3,) + cs.shape[1:], cs.dtype)]
        return self._start(key, bufs, _scatter_plan(dest_sets), 3 * len(sums), "scatter_" + key)

    def m_ready(self, gw_proj, g3):
        rows = gw_proj.shape[2] * gw_proj.shape[3]
        return self._swap("m", [gw_proj.reshape(N_CHIPS, 2, rows, D_MODEL), g3])

    def ret_done(self, after):
        return self._scatter("sm", self._chip_sums("m", after), [ALL_CHIPS, (3,)])

    def r_ready(self, g12):
        return self._swap("r", [g12])

    def lru_done(self, after):
        return self._scatter("sr", self._chip_sums("r", after), [(1, 2)])

    def small_ready(self, packed):
        land = jnp.zeros((8,) + packed.shape, F32)
        return self._start("small", [packed, land], _allgather_plan(), 7, "gather_small")

    def a_ready(self, g0):
        return self._swap("a", [g0])

    def dx_half_done(self, after):
        return self._scatter("sa", self._chip_sums("a", after), [(0,)])

    def finish(self, after):
        small_sum = _sum_gathered(*self._finish("small", after), "sum_small_grads")
        csp, gotp, cs3, got3 = self._finish("sm", after)
        cs12, got12 = self._finish("sr", after)
        cs0, got0 = self._finish("sa", after)
        half_in = _sum_parts([cs3, cs12, cs0], [got3, got12, got0], [(3,), (1, 2), (0,)], "sum_w_in")
        half_pr = _sum_slabs(csp, gotp, "sum_w_proj")
        return small_sum, _join_halves([half_in, half_pr], [8, 4], "join_halves")


_SMALL = ("gate_x_w", "gate_a_w", "conv_w", "conv_b", "gate_x_b", "gate_a_b", "lru_lambda", "gn_gain", "norm_final")
_SMALL_SHAPES = dict(gate_x_w=(LRU_BLOCKS, LRU_BW, LRU_BW), gate_a_w=(LRU_BLOCKS, LRU_BW, LRU_BW),
                     norm_in=(1, D_MODEL), conv_w=(CONV, D_MODEL), conv_b=(1, D_MODEL), gate_x_b=(1, D_MODEL),
                     gate_a_b=(1, D_MODEL), lru_lambda=(1, D_MODEL), gn_gain=(HEADS, DK), norm_final=(1, D_MODEL))


def _pack_small(small):
    return jnp.concatenate([small[k].reshape(-1, 128) for k in _SMALL], axis=0)


def _unpack_small(packed):
    out, r = {}, 0
    for k in _SMALL:
        shape = _SMALL_SHAPES[k]
        rows = 1
        for s in shape:
            rows *= s
        rows //= 128
        out[k] = packed[r:r + rows].reshape(shape)
        r += rows
    return out


def kernel(x, norm_in, w_in, conv_w, conv_b, gate_x_w, gate_x_b, gate_a_w, gate_a_b, lru_lambda, gn_gain, w_proj_a, w_proj_b, w_out, norm_final, loss_target, m_norm_in, m_w_in, m_conv_w, m_conv_b, m_gate_x_w, m_gate_x_b, m_gate_a_w, m_gate_a_b, m_lru_lambda, m_gn_gain, m_w_proj_a, m_w_proj_b, m_w_out, m_norm_final, v_norm_in, v_w_in, v_conv_w, v_conv_b, v_gate_x_w, v_gate_x_b, v_gate_a_w, v_gate_a_b, v_lru_lambda, v_gn_gain, v_w_proj_a, v_w_proj_b, v_w_out, v_norm_final):
    B, S, _ = x.shape
    T = B * S
    xi, yi, ci = _coords()
    chip = 2 * xi + yi

    cshard = D_MODEL // N_CHIPS
    mine = _cast_into_slot([w_in[0].reshape(2, D_MODEL // 2, 2 * D_MODEL)]
                           + [w[0].reshape(2, cshard // 2, D_MODEL) for w in (w_proj_a, w_proj_b, w_out)],
                           "cast_weights")
    plan = _gather_plan(3)
    s_sems, r_sems, pbufs, token = _copies_start(mine[1:], plan, 9, "gather_proj_start")
    w_all = _gather_chips(mine[:1], [4], "gather_weights")[0].reshape(N_CHIPS, D_MODEL, 2 * D_MODEL)

    def proj_weights(after):
        got = _copies_wait(s_sems, r_sems, pbufs, after, plan, "gather_proj_wait")
        return [b.reshape(D_MODEL, D_MODEL) for b in got]

    gshard = DK // N_CHIPS
    tiny = jnp.concatenate([conv_w[0], jnp.zeros((4, cshard), F32), jnp.pad(gn_gain[0], ((0, 4), (0, cshard - gshard)))],
                           axis=0)
    tiny_all = _all_gather8(tiny, "gather_small_weights").reshape(N_CHIPS, 2, 16, cshard)[:, 0]
    conv_w_full = jnp.transpose(tiny_all[:, 0:CONV, :], (1, 0, 2)).reshape(CONV, D_MODEL)
    gain_full = jnp.transpose(tiny_all[:, 8:8 + HEADS, :gshard], (1, 0, 2)).reshape(HEADS, DK)

    reduce = _GradReduce()
    loss, grad_x, dgin = _local_grads(
        x.reshape(T, D_MODEL), loss_target.reshape(T, D_MODEL), B, S, norm_in, w_all, conv_w_full, conv_b,
        gate_x_w[0], gate_x_b, gate_a_w[0], gate_a_b, lru_lambda, gain_full, proj_weights,
        norm_final.reshape(1, D_MODEL), reduce, deps=(token,))
    loss = lax.psum(loss, ("x", "y", "c"))

    g_norm_in = _sum_rows8(_all_gather8(dgin.reshape(8, 128), "gather_norm_in_grad"), 8, "sum_norm_in_grad")
    small_sum, (g_in_full, g_pr_full) = reduce.finish(g_norm_in)
    gsm = _unpack_small(small_sum)
    gsm["norm_in"] = g_norm_in.reshape(1, D_MODEL)
    g_w_in = g_in_full.reshape(D_MODEL, 2 * D_MODEL)
    g_pr = g_pr_full.reshape(2, 3, D_MODEL // (2 * N_CHIPS), D_MODEL)
    g_wpa, g_wpb, g_wout = (g_pr[:, k].reshape(cshard, D_MODEL) for k in range(3))

    grads = dict(gsm)
    grads["conv_w"] = lax.dynamic_slice_in_dim(gsm["conv_w"], chip * cshard, cshard, axis=1)
    grads["gn_gain"] = lax.dynamic_slice_in_dim(gsm["gn_gain"], chip * gshard, gshard, axis=1)
    grads.update(w_in=g_w_in, w_proj_a=g_wpa, w_proj_b=g_wpb, w_out=g_wout)

    weights = dict(norm_in=norm_in, w_in=w_in, conv_w=conv_w, conv_b=conv_b, gate_x_w=gate_x_w, gate_x_b=gate_x_b,
                   gate_a_w=gate_a_w, gate_a_b=gate_a_b, lru_lambda=lru_lambda, gn_gain=gn_gain, w_proj_a=w_proj_a,
                   w_proj_b=w_proj_b, w_out=w_out, norm_final=norm_final)
    ms = dict(norm_in=m_norm_in, w_in=m_w_in, conv_w=m_conv_w, conv_b=m_conv_b, gate_x_w=m_gate_x_w,
              gate_x_b=m_gate_x_b, gate_a_w=m_gate_a_w, gate_a_b=m_gate_a_b, lru_lambda=m_lru_lambda, gn_gain=m_gn_gain,
              w_proj_a=m_w_proj_a, w_proj_b=m_w_proj_b, w_out=m_w_out, norm_final=m_norm_final)
    vs = dict(norm_in=v_norm_in, w_in=v_w_in, conv_w=v_conv_w, conv_b=v_conv_b, gate_x_w=v_gate_x_w,
              gate_x_b=v_gate_x_b, gate_a_w=v_gate_a_w, gate_a_b=v_gate_a_b, lru_lambda=v_lru_lambda, gn_gain=v_gn_gain,
              w_proj_a=v_w_proj_a, w_proj_b=v_w_proj_b, w_out=v_w_out, norm_final=v_norm_final)
    names = list(weights)
    grads = {k: grads[k].reshape(weights[k].shape) for k in names}

    delta, new_m, new_v = {}, {}, {}
    for k in ("w_in", "w_proj_a", "w_proj_b", "w_out"):
        shp = weights[k].shape
        two = lambda a: a.reshape(shp[1], shp[2])
        d, mn, vn = _adamw_big(two(weights[k]), two(grads[k]), two(ms[k]), two(vs[k]), "adamw_" + k)
        delta[k], new_m[k], new_v[k] = d.reshape(shp), mn.reshape(shp), vn.reshape(shp)
    smalls = [k for k in names if k not in delta]

    def view(a):
        return a.reshape(1, -1) if a.ndim == 1 else (a.reshape(a.shape[1:]) if a.ndim > 2 else a)

    ds, mns, vns = _adamw_small([view(weights[k]) for k in smalls], [view(grads[k]) for k in smalls],
                                [view(ms[k]) for k in smalls], [view(vs[k]) for k in smalls], "adamw_small")
    for k, d, mn, vn in zip(smalls, ds, mns, vns):
        shp = weights[k].shape
        delta[k], new_m[k], new_v[k] = d.reshape(shp), mn.reshape(shp), vn.reshape(shp)

    return (loss, grad_x.reshape(B, S, D_MODEL), *[grads[k] for k in names], *[delta[k] for k in names],
            *[new_m[k] for k in names], *[new_v[k] for k in names])
```

```python
import functools

import jax
import jax.numpy as jnp
from jax import lax
from jax.experimental import pallas as pl
from jax.experimental.pallas import tpu as pltpu

F32 = jnp.float32
_MXU = jnp.bfloat16

D_MODEL = 1024
N_GROUPS = 8
HEADS = 4
DK = 256
CHUNK = 128
CONV = 4
LRU_BLOCKS = 16
LRU_BW = 64
LRU_C = 8.0
ROPE_THETA = 10000.0
EPS = 1e-6
CW = 256
N_CT = D_MODEL // CW
N_CHIPS = 4
MESH = pl.DeviceIdType.MESH

ADAM_LR = 0.001
ADAM_B1 = 0.9
ADAM_B2 = 0.999
ADAM_EPS = 1e-08
ADAM_WD = 0.01
ADAM_STEP = 10

VMEM_LIMIT = 56 * 1024 * 1024


def _c(v):
    return v.astype(_MXU)


def _dot(a, b):
    return lax.dot_general(a, b, (((1,), (0,)), ((), ())), preferred_element_type=F32)


def _dot_nt(a, b):
    return lax.dot_general(a, b, (((1,), (1,)), ((), ())), preferred_element_type=F32)


def _dot_tn(a, b):
    return lax.dot_general(a, b, (((0,), (0,)), ((), ())), preferred_element_type=F32)


def _sigmoid(z):
    return 0.5 * jnp.tanh(0.5 * z) + 0.5


ANY_SPEC = pl.BlockSpec(memory_space=pl.ANY)


def _after(body, n_in, deps):
    n_deps = len(deps)

    def wrapped(*refs):
        return body(*refs[:n_in], *refs[n_in + n_deps:])

    return wrapped


def _params(sem=None):
    if sem is None:
        return pltpu.CompilerParams(vmem_limit_bytes=VMEM_LIMIT)
    return pltpu.CompilerParams(vmem_limit_bytes=VMEM_LIMIT, dimension_semantics=sem)


def _inproj_fwd(x2d, g_in, w_all, deps=()):
    T = x2d.shape[0]
    tm = min(1024, T)
    n_i = T // tm

    def body(*refs):
        x_ref, g_ref, w_ref = refs[:3]
        proj_ref, ht_ref, h_all = refs[-3:]
        i = pl.program_id(1)
        rows = pl.ds(pl.multiple_of(i * tm, tm), tm)

        @pl.when(pl.program_id(0) == 0)
        def _():
            x = x_ref[...]
            r = lax.rsqrt(jnp.mean(x * x, axis=-1, keepdims=True) + EPS)
            h = x * r * g_ref[...]
            h_all[rows, :] = h.astype(h_all.dtype)
            ht_ref[...] = h.T.astype(ht_ref.dtype)

        proj_ref[...] = _dot(h_all[rows, :], w_ref[0])

    first = lambda j, i: jnp.where(j == 0, i, n_i - 1)
    return pl.pallas_call(
        body,
        name="inproj_fwd",
        grid=(N_GROUPS, n_i),
        in_specs=[
            pl.BlockSpec((tm, D_MODEL), lambda j, i: (first(j, i), 0)),
            pl.BlockSpec((1, D_MODEL), lambda j, i: (0, 0)),
            pl.BlockSpec((1, D_MODEL, D_MODEL), lambda j, i: (j // 2, 0, j % 2)),
        ] + [pl.BlockSpec(memory_space=pl.ANY)] * len(deps),
        out_specs=[
            pl.BlockSpec((tm, D_MODEL), lambda j, i: (i, j)),
            pl.BlockSpec((D_MODEL, tm), lambda j, i: (0, first(j, i))),
        ],
        out_shape=[
            jax.ShapeDtypeStruct((T, N_GROUPS * D_MODEL), F32),
            jax.ShapeDtypeStruct((D_MODEL, T), _MXU),
        ],
        scratch_shapes=[pltpu.VMEM((T, D_MODEL), _MXU)],
        compiler_params=_params(("arbitrary", "arbitrary")),
    )(x2d, g_in, w_all, *deps)


def _scan_fwd(a, u):
    n = a.shape[0]
    row = lax.broadcasted_iota(jnp.int32, a.shape, 0)
    s = 1
    while s < n:
        m = row >= s
        u = u + a * jnp.where(m, pltpu.roll(u, s, 0), 0.0)
        a = a * jnp.where(m, pltpu.roll(a, s, 0), 1.0)
        s *= 2
    return a, u


def _scan_bwd(b, g):
    n = b.shape[0]
    row = lax.broadcasted_iota(jnp.int32, b.shape, 0)
    s = 1
    while s < n:
        m = row < n - s
        g = g + b * jnp.where(m, pltpu.roll(g, n - s, 0), 0.0)
        b = b * jnp.where(m, pltpu.roll(b, n - s, 0), 1.0)
        s *= 2
    return b, g


LANES = 128
SUBLANES = 8


def _scan_scratch(tc):
    by_lanes = pltpu.VMEM((CW // LANES, tc, LANES), F32)
    return [by_lanes, by_lanes, pltpu.VMEM((tc // SUBLANES, CW), F32), pltpu.VMEM((tc, CW), F32)]


def _scan_tile(a, u, edge, la_ref, lh_ref, c_ref, dst_ref, reverse):
    n, w = a.shape
    groups = n // SUBLANES
    a3 = a.reshape(groups, SUBLANES, w)
    u3 = u.reshape(groups, SUBLANES, w)
    row = lax.broadcasted_iota(jnp.int32, a3.shape, 1)
    for s in (1, 2, 4):
        m = (row < SUBLANES - s) if reverse else (row >= s)
        shift = SUBLANES - s if reverse else s
        u3 = u3 + a3 * jnp.where(m, pltpu.roll(u3, shift, 1), 0.0)
        a3 = a3 * jnp.where(m, pltpu.roll(a3, shift, 1), 1.0)
    al = a3.reshape(n, w)
    hl = u3.reshape(n, w)
    blocks = w // LANES
    for q in range(blocks):
        la_ref[q] = al[:, q * LANES:(q + 1) * LANES]
        lh_ref[q] = hl[:, q * LANES:(q + 1) * LANES]
    ends = pl.ds(0 if reverse else SUBLANES - 1, groups, stride=SUBLANES)
    end_a = jnp.concatenate([la_ref.at[q][ends, :] for q in range(blocks)], axis=-1)
    end_h = jnp.concatenate([lh_ref.at[q][ends, :] for q in range(blocks)], axis=-1)
    prod, part = (_scan_bwd if reverse else _scan_fwd)(end_a, end_h)
    total = part + prod * edge
    g_row = lax.broadcasted_iota(jnp.int32, total.shape, 0)
    if reverse:
        c_ref[...] = jnp.where(g_row == groups - 1, edge, pltpu.roll(total, groups - 1, 0))
    else:
        c_ref[...] = jnp.where(g_row == 0, edge, pltpu.roll(total, 1, 0))
    for g in range(groups):
        rows = slice(g * SUBLANES, (g + 1) * SUBLANES)
        for q in range(blocks):
            cols = slice(q * LANES, (q + 1) * LANES)
            dst_ref[rows, cols] = lh_ref[q, rows, :] + la_ref[q, rows, :] * c_ref[g:g + 1, cols]


def _softplus_neg(lam):
    z = -lam
    return jnp.maximum(z, 0.0) + jnp.log1p(jnp.exp(-jnp.abs(z)))


def _lru_gates(xc, wx_ref, wa_ref, bx_ref, ba_ref, lam_ref):
    xcb = _c(xc)
    i_t = _sigmoid(_dot(xcb, wx_ref[0]) + bx_ref[...])
    r_t = _sigmoid(_dot(xcb, wa_ref[0]) + ba_ref[...])
    sp = _softplus_neg(lam_ref[...])
    log_a = (-LRU_C) * r_t * sp
    a = jnp.exp(log_a)
    mult = jnp.sqrt(1.0 - a * a)
    return xcb, i_t, r_t, sp, a, mult


def _conv_from_ext(ext_ref, xa, cw_ref, cb_ref, tc):
    return (cb_ref[...] + cw_ref[3:4, :] * xa + cw_ref[2:3, :] * ext_ref[7:7 + tc, :]
            + cw_ref[1:2, :] * ext_ref[6:6 + tc, :] + cw_ref[0:1, :] * ext_ref[5:5 + tc, :])


def _lru_fwd(proj, conv_w, conv_b, wx_bd, wa_bd, bx, ba, lam, B, S):
    T = B * S
    tc = min(256, S)
    nt = S // tc
    h8 = tc // 8

    def body(xa_ref, halo_ref, ga_ref, cw_ref, cb_ref, wx_ref, wa_ref, bx_ref, ba_ref, lam_ref,
             h_ref, ya_ref, ext_ref, carry_ref, la_ref, lh_ref, c_ref):
        t = pl.program_id(2)

        @pl.when(t == 0)
        def _():
            carry_ref[...] = jnp.zeros_like(carry_ref)

        xa = xa_ref[...]
        ext_ref[0:8, :] = jnp.where(t == 0, 0.0, halo_ref[...])
        ext_ref[8:8 + tc, :] = xa
        xc = _conv_from_ext(ext_ref, xa, cw_ref, cb_ref, tc)
        _, i_t, _, _, a, mult = _lru_gates(xc, wx_ref, wa_ref, bx_ref, ba_ref, lam_ref)
        u = mult * (i_t * xc)
        _scan_tile(a, u, carry_ref[7:8, :], la_ref, lh_ref, c_ref, h_ref, False)
        h = h_ref[...]
        carry_ref[...] = h[tc - 8:tc, :]
        ga = ga_ref[...]
        ya_ref[...] = (ga * _sigmoid(ga) * h).astype(ya_ref.dtype)

    row = lambda b, t: b * nt + t
    vec = pl.BlockSpec((1, CW), lambda b, c, t: (0, c))
    mat = pl.BlockSpec((1, CW, CW), lambda b, c, t: (c, 0, 0))
    return pl.pallas_call(
        body,
        name="lru_fwd",
        grid=(B, N_CT, nt),
        in_specs=[
            pl.BlockSpec((tc, CW), lambda b, c, t: (row(b, t), c)),
            pl.BlockSpec((8, CW), lambda b, c, t: (jnp.maximum(row(b, t) * h8 - 1, 0), c)),
            pl.BlockSpec((tc, CW), lambda b, c, t: (row(b, t), N_CT + c)),
            pl.BlockSpec((CONV, CW), lambda b, c, t: (0, c)),
            vec, mat, mat, vec, vec, vec,
        ],
        out_specs=[
            pl.BlockSpec((tc, CW), lambda b, c, t: (row(b, t), c)),
            pl.BlockSpec((tc, CW), lambda b, c, t: (row(b, t), c)),
        ],
        out_shape=[
            jax.ShapeDtypeStruct((T, D_MODEL), F32),
            jax.ShapeDtypeStruct((T, D_MODEL), _MXU),
        ],
        scratch_shapes=[pltpu.VMEM((tc + 8, CW), F32), pltpu.VMEM((8, CW), F32)] + _scan_scratch(tc)[:3],
        compiler_params=_params(("parallel", "parallel", "arbitrary")),
    )(proj, proj, proj, conv_w, conv_b, wx_bd, wa_bd, bx, ba, lam)


def _lru_bwd(dya, proj, hlru, conv_w, conv_b, wx_bd, wa_bd, bx, ba, lam, B, S, deps=()):
    T = B * S
    tc = min(256, S)
    nt = S // tc
    h8 = tc // 8

    def body(dya_ref, xa_ref, xhalo_ref, ga_ref, h_ref, hhalo_ref, cw_ref, cb_ref, wx_ref, wa_ref, bx_ref, ba_ref,
             lam_ref, dxa_ref, dga_ref, dcw_ref, dcb_ref, dwx_ref, dwa_ref, dbx_ref, dba_ref, dlam_ref,
             ext_ref, ext2_ref, carry_ref, dhalo_ref, la_ref, lh_ref, c_ref, dh_ref):
        b = pl.program_id(1)
        t = pl.program_id(2)
        tt = nt - 1 - t

        @pl.when(t == 0)
        def _():
            carry_ref[...] = jnp.zeros_like(carry_ref)
            dhalo_ref[...] = jnp.zeros_like(dhalo_ref)

        @pl.when((t == 0) & (b == 0))
        def _():
            for r in (dcw_ref, dcb_ref, dwx_ref, dwa_ref, dbx_ref, dba_ref, dlam_ref):
                r[...] = jnp.zeros_like(r)

        xa = xa_ref[...]
        ext_ref[0:8, :] = jnp.where(tt == 0, 0.0, xhalo_ref[...])
        ext_ref[8:8 + tc, :] = xa
        xc = _conv_from_ext(ext_ref, xa, cw_ref, cb_ref, tc)
        xcb, i_t, r_t, sp, a, mult = _lru_gates(xc, wx_ref, wa_ref, bx_ref, ba_ref, lam_ref)

        h = h_ref[...]
        ga = ga_ref[...]
        dya_t = dya_ref[...]
        sg = _sigmoid(ga)
        dga_ref[...] = (dya_t * h * (sg * (1.0 + ga * (1.0 - sg)))).astype(dga_ref.dtype)
        dlru = dya_t * (ga * sg)

        row = lax.broadcasted_iota(jnp.int32, a.shape, 0)
        coef = jnp.where(row == tc - 1, 1.0, pltpu.roll(a, tc - 1, 0))
        _scan_tile(coef, dlru, carry_ref[0:1, :], la_ref, lh_ref, c_ref, dh_ref, True)
        dh = dh_ref[...]
        ext2_ref[0:tc, :] = a * dh
        carry_ref[...] = ext2_ref[0:8, :]

        ext2_ref[0:8, :] = jnp.where(tt == 0, 0.0, hhalo_ref[...])
        ext2_ref[8:8 + tc, :] = h
        hprev = ext2_ref[7:7 + tc, :]

        da = dh * hprev
        ix = i_t * xc
        dmult = dh * ix
        di = dh * mult * xc
        dxc = dh * mult * i_t
        dlog_a = da * a - dmult * (a * a) / mult
        dr = dlog_a * ((-LRU_C) * sp)
        dlam_ref[...] += jnp.sum(dlog_a * r_t, axis=0, keepdims=True) * (LRU_C * _sigmoid(-lam_ref[...]))
        dza = dr * r_t * (1.0 - r_t)
        dzx = di * i_t * (1.0 - i_t)
        dzab = _c(dza)
        dzxb = _c(dzx)
        dxc = dxc + _dot_nt(dzxb, wx_ref[0]) + _dot_nt(dzab, wa_ref[0])
        dwx_ref[0] += _dot_tn(xcb, dzxb)
        dwa_ref[0] += _dot_tn(xcb, dzab)
        dbx_ref[...] += jnp.sum(dzx, axis=0, keepdims=True)
        dba_ref[...] += jnp.sum(dza, axis=0, keepdims=True)

        dcb_ref[...] += jnp.sum(dxc, axis=0, keepdims=True)
        dcw_ref[3:4, :] += jnp.sum(dxc * xa, axis=0, keepdims=True)
        dcw_ref[2:3, :] += jnp.sum(dxc * ext_ref[7:7 + tc, :], axis=0, keepdims=True)
        dcw_ref[1:2, :] += jnp.sum(dxc * ext_ref[6:6 + tc, :], axis=0, keepdims=True)
        dcw_ref[0:1, :] += jnp.sum(dxc * ext_ref[5:5 + tc, :], axis=0, keepdims=True)
        ext2_ref[0:tc, :] = dxc
        ext2_ref[tc:tc + 8, :] = dhalo_ref[...]
        dxa = (cw_ref[3:4, :] * dxc + cw_ref[2:3, :] * ext2_ref[1:1 + tc, :]
               + cw_ref[1:2, :] * ext2_ref[2:2 + tc, :] + cw_ref[0:1, :] * ext2_ref[3:3 + tc, :])
        dxa_ref[...] = dxa.astype(dxa_ref.dtype)
        dhalo_ref[...] = ext2_ref[0:8, :]

    row_of = lambda b, t: b * nt + (nt - 1 - t)
    tile = lambda off: pl.BlockSpec((tc, CW), lambda c, b, t: (row_of(b, t), off + c))
    halo = pl.BlockSpec((8, CW), lambda c, b, t: (jnp.maximum(row_of(b, t) * h8 - 1, 0), c))
    vec = pl.BlockSpec((1, CW), lambda c, b, t: (0, c))
    mat = pl.BlockSpec((1, CW, CW), lambda c, b, t: (c, 0, 0))
    cwspec = pl.BlockSpec((CONV, CW), lambda c, b, t: (0, c))
    return pl.pallas_call(
        _after(body, 13, deps),
        name="lru_bwd",
        grid=(N_CT, B, nt),
        in_specs=[tile(0), tile(0), halo, tile(N_CT), tile(0), halo, cwspec, vec, mat, mat, vec, vec, vec]
        + [ANY_SPEC] * len(deps),
        out_specs=[tile(0), tile(0), cwspec, vec, mat, mat, vec, vec, vec],
        out_shape=[
            jax.ShapeDtypeStruct((T, D_MODEL), _MXU),
            jax.ShapeDtypeStruct((T, D_MODEL), _MXU),
            jax.ShapeDtypeStruct((CONV, D_MODEL), F32),
            jax.ShapeDtypeStruct((1, D_MODEL), F32),
            jax.ShapeDtypeStruct((N_CT, CW, CW), F32),
            jax.ShapeDtypeStruct((N_CT, CW, CW), F32),
            jax.ShapeDtypeStruct((1, D_MODEL), F32),
            jax.ShapeDtypeStruct((1, D_MODEL), F32),
            jax.ShapeDtypeStruct((1, D_MODEL), F32),
        ],
        scratch_shapes=[pltpu.VMEM((tc + 8, CW), F32), pltpu.VMEM((tc + 8, CW), F32),
                        pltpu.VMEM((8, CW), F32), pltpu.VMEM((8, CW), F32)] + _scan_scratch(tc),
        compiler_params=_params(("parallel", "arbitrary", "arbitrary")),
    )(dya, proj, proj, proj, hlru, hlru, conv_w, conv_b, wx_bd, wa_bd, bx, ba, lam, *deps)


def _retention_tables(S):
    half = DK // 2
    freqs = ROPE_THETA ** (-jnp.arange(half, dtype=F32) / half)
    ang = jnp.arange(S, dtype=F32)[:, None] * freqs[None, :]
    log_g = jnp.log1p(-(2.0 ** (-5.0 - jnp.arange(HEADS, dtype=F32))))
    idx = jnp.arange(CHUNK, dtype=F32)
    diff = idx[:, None] - idx[None, :]
    inner = jnp.where(diff >= 0, jnp.exp(jnp.maximum(diff, 0.0)[None] * log_g[:, None, None]), 0.0)
    cross = jnp.exp((idx[None, :] + 1.0) * log_g[:, None])[:, :, None]
    state = jnp.exp((CHUNK - 1.0 - idx[None, :]) * log_g[:, None])[:, :, None]
    gam = jnp.broadcast_to(jnp.exp(CHUNK * log_g)[:, None, None], (HEADS, 1, DK))
    return jnp.cos(ang), jnp.sin(ang), inner, cross, state, gam


def _rot(x, cos, sin):
    half = DK // 2
    x1, x2 = x[:, :half], x[:, half:]
    return jnp.concatenate([x1 * cos - x2 * sin, x1 * sin + x2 * cos], axis=-1)


def _rot_t(y, cos, sin):
    half = DK // 2
    y1, y2 = y[:, :half], y[:, half:]
    return jnp.concatenate([y1 * cos + y2 * sin, y2 * cos - y1 * sin], axis=-1)


def _groupnorm(o):
    mu = jnp.mean(o, axis=-1, keepdims=True)
    oc = o - mu
    rs = lax.rsqrt(jnp.mean(oc * oc, axis=-1, keepdims=True) + EPS)
    return oc * rs, rs


def _ret_specs(B, chunk_of):
    qkv = lambda g: pl.BlockSpec((B, CHUNK, D_MODEL), lambda c: (0, chunk_of(c), g))
    act = pl.BlockSpec((B, CHUNK, D_MODEL), lambda c: (0, chunk_of(c), 0))
    rope = pl.BlockSpec((CHUNK, DK // 2), lambda c: (chunk_of(c), 0))
    dmat = pl.BlockSpec((HEADS, CHUNK, CHUNK), lambda c: (0, 0, 0))
    dvec = pl.BlockSpec((HEADS, CHUNK, 1), lambda c: (0, 0, 0))
    hrow = pl.BlockSpec((HEADS, 1, DK), lambda c: (0, 0, 0))
    rst = pl.BlockSpec((1, B, HEADS, DK, DK), lambda c: (chunk_of(c), 0, 0, 0, 0))
    return qkv, act, rope, dmat, dvec, hrow, rst


def _ret_fwd(proj, tables, gain3, B, S):
    T = B * S
    nc = S // CHUNK
    cos, sin, dmat_t, cd_t, sd_t, gam_t = tables

    def body(q_ref, k_ref, v_ref, gb_ref, cos_ref, sin_ref, dm_ref, cd_ref, sd_ref, gam_ref, gain_ref,
             o_ref, yb_ref, rs_ref, state_ref):
        @pl.when(pl.program_id(0) == 0)
        def _():
            state_ref[...] = jnp.zeros_like(state_ref)

        cos_t, sin_t = cos_ref[...], sin_ref[...]
        for b, h in [(b, h) for b in range(B) for h in range(HEADS)]:
            cols = slice(h * DK, (h + 1) * DK)
            qb = _c(_rot(q_ref[b, :, cols], cos_t, sin_t))
            kb = _c(_rot(k_ref[b, :, cols], cos_t, sin_t) * (DK ** -0.5))
            v = v_ref[b, :, cols]
            state = state_ref[b, h]
            sb = _c(state)
            rs_ref[0, b, h] = sb
            scores = _dot_nt(qb, kb) * dm_ref[h]
            o = _dot(_c(scores), _c(v)) + _dot(qb, sb) * cd_ref[h]
            state_ref[b, h] = gam_ref[h] * state + _dot_tn(kb, _c(v * sd_ref[h]))
            o_ref[b, :, cols] = o
            n, _ = _groupnorm(o)
            gb = gb_ref[b, :, cols]
            yb_ref[b, :, cols] = (gb * _sigmoid(gb) * (n * gain_ref[h])).astype(yb_ref.dtype)

    qkv, act, rope, dmat, dvec, hrow, rst = _ret_specs(B, lambda c: c)
    proj3 = proj.reshape(B, S, proj.shape[1])
    o_pre, yb, states = pl.pallas_call(
        body,
        name="ret_fwd",
        grid=(nc,),
        in_specs=[qkv(2), qkv(3), qkv(4), qkv(5), rope, rope, dmat, dvec, dvec, hrow, hrow],
        out_specs=[act, act, rst],
        out_shape=[
            jax.ShapeDtypeStruct((B, S, D_MODEL), F32),
            jax.ShapeDtypeStruct((B, S, D_MODEL), _MXU),
            jax.ShapeDtypeStruct((nc, B, HEADS, DK, DK), _MXU),
        ],
        scratch_shapes=[pltpu.VMEM((B, HEADS, DK, DK), F32)],
        compiler_params=_params(("arbitrary",)),
    )(proj3, proj3, proj3, proj3, cos, sin, dmat_t, cd_t, sd_t, gam_t, gain3)
    return o_pre.reshape(T, D_MODEL), yb.reshape(T, D_MODEL), states


def _ret_bwd(dyb, o_pre, proj, states, tables, gain3, B, S, deps=()):
    T = B * S
    nc = S // CHUNK
    cos, sin, dmat_t, cd_t, sd_t, gam_t = tables

    def body(dyb_ref, o_ref, q_ref, k_ref, v_ref, gb_ref, rs_ref, cos_ref, sin_ref, dm_ref, cd_ref, sd_ref, gam_ref,
             gain_ref, dr_ref, dgain_ref, dstate_ref):
        @pl.when(pl.program_id(0) == 0)
        def _():
            dstate_ref[...] = jnp.zeros_like(dstate_ref)
            dgain_ref[...] = jnp.zeros_like(dgain_ref)

        cos_t, sin_t = cos_ref[...], sin_ref[...]
        for b, h in [(b, h) for b in range(B) for h in range(HEADS)]:
            cols = slice(h * DK, (h + 1) * DK)
            gain = gain_ref[h]
            n, rs = _groupnorm(o_ref[b, :, cols])
            gb = gb_ref[b, :, cols]
            sg = _sigmoid(gb)
            dy = dyb_ref[b, :, cols]
            part = lambda g: slice(g * D_MODEL + h * DK, g * D_MODEL + (h + 1) * DK)
            dr_ref[b, :, part(3)] = (dy * (n * gain) * (sg * (1.0 + gb * (1.0 - sg)))).astype(dr_ref.dtype)
            dgn = dy * (gb * sg)
            dgain_ref[h] += jnp.sum(dgn * n, axis=0, keepdims=True)
            dn = dgn * gain
            do = rs * (dn - jnp.mean(dn, axis=-1, keepdims=True) - n * jnp.mean(dn * n, axis=-1, keepdims=True))

            qb = _c(_rot(q_ref[b, :, cols], cos_t, sin_t))
            kb = _c(_rot(k_ref[b, :, cols], cos_t, sin_t) * (DK ** -0.5))
            v = v_ref[b, :, cols]
            vb = _c(v)
            vsb = _c(v * sd_ref[h])
            dob = _c(do)
            docb = _c(do * cd_ref[h])
            dmat = dm_ref[h]
            dstate = dstate_ref[b, h]
            dsb = _c(dstate)
            pb = _c(_dot_nt(qb, kb) * dmat)
            dsc = _c(_dot_nt(dob, vb) * dmat)
            dq = _dot(dsc, kb) + _dot_nt(docb, rs_ref[0, b, h])
            dk = _dot_tn(dsc, qb) + _dot_nt(vsb, dsb)
            dv = _dot_tn(pb, dob) + _dot(kb, dsb) * sd_ref[h]
            dstate_ref[b, h] = gam_ref[h] * dstate + _dot_tn(qb, docb)
            dr_ref[b, :, part(0)] = _rot_t(dq, cos_t, sin_t).astype(dr_ref.dtype)
            dr_ref[b, :, part(1)] = (_rot_t(dk, cos_t, sin_t) * (DK ** -0.5)).astype(dr_ref.dtype)
            dr_ref[b, :, part(2)] = dv.astype(dr_ref.dtype)

    qkv, act, rope, dmat, dvec, hrow, rst = _ret_specs(B, lambda c: nc - 1 - c)
    wide = pl.BlockSpec((B, CHUNK, 4 * D_MODEL), lambda c: (0, nc - 1 - c, 0))
    proj3 = proj.reshape(B, S, proj.shape[1])
    dr, dgain = pl.pallas_call(
        _after(body, 14, deps),
        name="ret_bwd",
        grid=(nc,),
        in_specs=[act, act, qkv(2), qkv(3), qkv(4), qkv(5), rst, rope, rope, dmat, dvec, dvec, hrow, hrow]
        + [ANY_SPEC] * len(deps),
        out_specs=[wide, hrow],
        out_shape=[jax.ShapeDtypeStruct((B, S, 4 * D_MODEL), _MXU), jax.ShapeDtypeStruct((HEADS, 1, DK), F32)],
        scratch_shapes=[pltpu.VMEM((B, HEADS, DK, DK), F32)],
        compiler_params=_params(("arbitrary",)),
    )(dyb.reshape(B, S, D_MODEL), o_pre.reshape(B, S, D_MODEL), proj3, proj3, proj3, proj3, states, cos, sin, dmat_t,
      cd_t, sd_t, gam_t, gain3, *deps)
    return dr.reshape(T, 4 * D_MODEL), dgain


def _mid(ya, yb, proj, x2d, tgt2d, wpa, wpb, wout, g_fin):
    T = x2d.shape[0]
    tm = min(256, T)
    n_steps = T // tm
    rows = D_MODEL // (2 * N_CHIPS)

    def body(ya_ref, yb_ref, ma_ref, mb_ref, x_ref, t_ref, gf_ref, wpa_hbm, wpb_hbm, wout_hbm,
             loss_ref, dx2_ref, dya_ref, dyb_ref, dm_ref, dgf_ref, gw_hbm, w_ref, acc_ref, sem):
        i = pl.program_id(0)

        @pl.when(i == 0)
        def _():
            loads = [pltpu.make_async_copy(src, w_ref.at[k], sem.at[k]) for k, src in enumerate((wpa_hbm, wpb_hbm, wout_hbm))]
            for cp in loads:
                cp.start()
            for cp in loads:
                cp.wait()
            acc_ref[...] = jnp.zeros_like(acc_ref)
            loss_ref[...] = jnp.zeros_like(loss_ref)
            dgf_ref[...] = jnp.zeros_like(dgf_ref)

        ya_t, yb_t = ya_ref[...], yb_ref[...]
        out_a = _dot(ya_t, w_ref[0])
        out_b = _dot(yb_t, w_ref[1])
        sa = _sigmoid(ma_ref[...])
        sb = _sigmoid(mb_ref[...])
        mgb = _c(sa * out_a + sb * out_b)
        x2 = x_ref[...] + _dot(mgb, w_ref[2])
        r2 = lax.rsqrt(jnp.mean(x2 * x2, axis=-1, keepdims=True) + EPS)
        nx = x2 * r2
        gf = gf_ref[...]
        err = nx * gf - t_ref[...]
        loss_ref[...] += 0.5 * jnp.sum(jnp.mean(err * err, axis=-1, keepdims=True), axis=0, keepdims=True)
        dy = err * (1.0 / D_MODEL)
        dgf_ref[...] += jnp.sum(dy * nx, axis=0, keepdims=True)
        dyg = dy * gf
        dx2 = r2 * (dyg - nx * jnp.mean(dyg * nx, axis=-1, keepdims=True))
        dx2_ref[...] = dx2
        dx2b = _c(dx2)
        dmg = _dot_nt(dx2b, w_ref[2])
        acc_ref[2] += _dot_tn(mgb, dx2b)
        dm_ref[:, :D_MODEL] = (dmg * out_a * sa * (1.0 - sa)).astype(dm_ref.dtype)
        dm_ref[:, D_MODEL:] = (dmg * out_b * sb * (1.0 - sb)).astype(dm_ref.dtype)
        dab = _c(dmg * sa)
        dbb = _c(dmg * sb)
        dya_ref[...] = _dot_nt(dab, w_ref[0])
        dyb_ref[...] = _dot_nt(dbb, w_ref[1])
        acc_ref[0] += _dot_tn(ya_t, dab)
        acc_ref[1] += _dot_tn(yb_t, dbb)

        @pl.when(i == n_steps - 1)
        def _():
            copies = [pltpu.make_async_copy(acc_ref.at[k, pl.ds((2 * p + hf) * rows, rows), :], gw_hbm.at[p, hf, k],
                                            sem.at[(k * N_CHIPS + p) * 2 + hf])
                      for k in range(3) for p in range(N_CHIPS) for hf in range(2)]
            for cp in copies:
                cp.start()
            for cp in copies:
                cp.wait()

    tile = lambda j: pl.BlockSpec((tm, D_MODEL), lambda i: (i, j))
    one = pl.BlockSpec((1, D_MODEL), lambda i: (0, 0))
    anyspec = pl.BlockSpec(memory_space=pl.ANY)
    return pl.pallas_call(
        body,
        name="mid",
        grid=(n_steps,),
        in_specs=[tile(0), tile(0), tile(6), tile(7), tile(0), tile(0), one, anyspec, anyspec, anyspec],
        out_specs=[pl.BlockSpec((1, 1), lambda i: (0, 0)), tile(0), tile(0), tile(0),
                   pl.BlockSpec((tm, 2 * D_MODEL), lambda i: (i, 0)), one, anyspec],
        out_shape=[
            jax.ShapeDtypeStruct((1, 1), F32),
            jax.ShapeDtypeStruct((T, D_MODEL), F32),
            jax.ShapeDtypeStruct((T, D_MODEL), F32),
            jax.ShapeDtypeStruct((T, D_MODEL), F32),
            jax.ShapeDtypeStruct((T, 2 * D_MODEL), _MXU),
            jax.ShapeDtypeStruct((1, D_MODEL), F32),
            jax.ShapeDtypeStruct((N_CHIPS, 2, 3, rows, D_MODEL), F32),
        ],
        scratch_shapes=[pltpu.VMEM((3, D_MODEL, D_MODEL), _MXU), pltpu.VMEM((3, D_MODEL, D_MODEL), F32),
                        pltpu.SemaphoreType.DMA((3 * N_CHIPS * 2,))],
        compiler_params=_params(("arbitrary",)),
    )(ya, yb, proj, proj, x2d, tgt2d, g_fin, wpa, wpb, wout)


DX_TILE = 512


def _inproj_bwd_dx(dparts, w_all, x2d, dx2, g_in, first, count, prev, name, deps=()):
    T = x2d.shape[0]
    tm = min(DX_TILE, T)
    n_d = len(dparts)
    groups = [(a, k) for a, d in enumerate(dparts) for k in range(d.shape[1] // D_MODEL)]
    dg_start = jnp.zeros((1, D_MODEL), F32) if prev is None else prev[1]
    carried = () if prev is None else (prev[0],)

    def body(*refs):
        d_refs = refs[:n_d]
        x_ref, dx2_ref, g_ref, dg0_ref, w_hbm = refs[n_d:n_d + 5]
        dx_ref, dg_ref, w_ref, sem = refs[-4:]

        @pl.when(pl.program_id(0) == 0)
        def _():
            cp = pltpu.make_async_copy(w_hbm, w_ref, sem)
            cp.start()
            cp.wait()
            dg_ref[...] = dg0_ref[...]

        dh = jnp.zeros((tm, D_MODEL), F32)
        for j, (a, k) in enumerate(groups):
            dh = dh + _dot_nt(d_refs[a][:, k * D_MODEL:(k + 1) * D_MODEL],
                              w_ref[j // 2, :, (j % 2) * D_MODEL:(j % 2 + 1) * D_MODEL])
        x = x_ref[...]
        r = lax.rsqrt(jnp.mean(x * x, axis=-1, keepdims=True) + EPS)
        nx = x * r
        dg_ref[...] += jnp.sum(dh * nx, axis=0, keepdims=True)
        dhg = dh * g_ref[...]
        dx_ref[...] = dx2_ref[...] + r * (dhg - nx * jnp.mean(dhg * nx, axis=-1, keepdims=True))

    tile = pl.BlockSpec((tm, D_MODEL), lambda i: (first + i, 0))
    one = pl.BlockSpec((1, D_MODEL), lambda i: (0, 0))
    return pl.pallas_call(
        body,
        name=name,
        grid=(count,),
        in_specs=[pl.BlockSpec((tm, d.shape[1]), lambda i: (first + i, 0)) for d in dparts]
        + [tile, tile, one, one, ANY_SPEC] + [ANY_SPEC] * (len(carried) + len(deps)),
        out_specs=[tile, one],
        out_shape=[jax.ShapeDtypeStruct((T, D_MODEL), F32), jax.ShapeDtypeStruct((1, D_MODEL), F32)],
        input_output_aliases={n_d + 5: 0} if carried else {},
        scratch_shapes=[pltpu.VMEM(w_all.shape, w_all.dtype), pltpu.SemaphoreType.DMA],
        compiler_params=_params(("arbitrary",)),
    )(*dparts, x2d, dx2, g_in, dg_start, w_all, *carried, *deps)


def _inproj_bwd_dw(ht, dparts, name, deps=()):
    T = ht.shape[1]
    tn = 512
    half = D_MODEL // 2
    per_chip = 2 * D_MODEL // tn
    n_d = len(dparts)
    tiles = [(a, t) for a, d in enumerate(dparts) for t in range(d.shape[1] // tn)]
    offs = [sum(d.shape[1] // tn for d in dparts[:a]) for a in range(n_d)]

    def body(*refs):
        ht_ref = refs[0]
        d_refs = refs[1:1 + n_d]
        out_ref = refs[-1]
        t = pl.program_id(0)

        for a in range(n_d):
            lo, hi = offs[a], offs[a] + dparts[a].shape[1] // tn

            @pl.when((t >= lo) & (t < hi))
            def _(a=a):
                g = _dot(ht_ref[...], d_refs[a][...])
                out_ref[0, 0] = g[:half]
                out_ref[0, 1] = g[half:]

    def dspec(a):
        n_a = dparts[a].shape[1] // tn
        return pl.BlockSpec((T, tn), lambda t: (0, jnp.clip(t - offs[a], 0, n_a - 1)))

    return pl.pallas_call(
        body,
        name=name,
        grid=(len(tiles),),
        in_specs=[pl.BlockSpec((D_MODEL, T), lambda t: (0, 0))] + [dspec(a) for a in range(n_d)]
        + [ANY_SPEC] * len(deps),
        out_specs=pl.BlockSpec((1, 2, half, tn), lambda t: (t // per_chip, 0, 0, t % per_chip)),
        out_shape=jax.ShapeDtypeStruct((len(tiles) // per_chip, 2, half, 2 * D_MODEL), F32),
        compiler_params=_params(("parallel",)),
    )(ht, *dparts, *deps)


def _coords():
    return lax.axis_index("x"), lax.axis_index("y"), lax.axis_index("c")


def _other_chips(x, y):
    return [(1 - x, y), (x, 1 - y), (1 - x, 1 - y)]


def _all_gather8(xs, name, deps=()):
    m_per, n = xs.shape

    def body(x_ref, *rest):
        out_ref, send_sems, recv_sems, local_sem = rest[-4:]
        x, y, c = _coords()
        me, sibling = (x, y, c), (x, y, 1 - c)
        chips = _other_chips(x, y)

        def rows(px, py, pc):
            return out_ref.at[pl.ds((4 * px + 2 * py + pc) * m_per, m_per), :]

        def copy(k, block, to, src=None):
            return pltpu.make_async_remote_copy(
                src_ref=rows(*block) if src is None else src, dst_ref=rows(*block),
                send_sem=send_sems.at[k], recv_sem=recv_sems.at[k], device_id=to, device_id_type=MESH)

        mine = pltpu.make_async_copy(x_ref, rows(*me), local_sem)
        mine.start()
        first = [copy(0, me, sibling, src=x_ref)]
        first += [copy(1 + j, me, (*chip, c), src=x_ref) for j, chip in enumerate(chips)]
        for cp in first:
            cp.start()
        passed = [copy(4 + j, (*chip, c), sibling) for j, chip in enumerate(chips)]
        for j, chip in enumerate(chips):
            copy(1 + j, (*chip, c), me).wait_recv()
            passed[j].start()
        copy(0, sibling, me).wait_recv()
        for j, chip in enumerate(chips):
            copy(4 + j, (*chip, 1 - c), me).wait_recv()
        for cp in first + passed:
            cp.wait_send()
        mine.wait()

    return pl.pallas_call(
        body,
        name=name,
        out_shape=jax.ShapeDtypeStruct((8 * m_per, n), xs.dtype),
        in_specs=[pl.BlockSpec(memory_space=pltpu.VMEM)] + [ANY_SPEC] * len(deps),
        out_specs=pl.BlockSpec(memory_space=pltpu.VMEM),
        scratch_shapes=[pltpu.SemaphoreType.DMA((7,)), pltpu.SemaphoreType.DMA((7,)), pltpu.SemaphoreType.DMA],
        compiler_params=pltpu.CompilerParams(vmem_limit_bytes=VMEM_LIMIT),
    )(xs, *deps)


def _chunks(rows, n):
    size = rows // n
    return [pl.ds(q * size, size) for q in range(n)]


HBM_SPEC = pl.BlockSpec(memory_space=pltpu.HBM)
SEM_SPEC = pl.BlockSpec(memory_space=pltpu.SEMAPHORE)
DATAFLOW = pltpu.SideEffectType.DATAFLOW_SIDE_EFFECTING


def _copies_start(bufs, plan, n_copies, name):
    n = len(bufs)

    def body(*refs):
        ins = refs[:n]
        send_sems, recv_sems = refs[n], refs[n + 1]
        token = refs[-1]
        for k, send, _ in plan(ins):
            if send is not None:
                src, dst, dev, pred = send
                cp = pltpu.make_async_remote_copy(src_ref=src, dst_ref=dst, send_sem=send_sems.at[k],
                                                  recv_sem=recv_sems.at[k], device_id=dev, device_id_type=MESH)
                if pred is None:
                    cp.start()
                else:
                    pl.when(pred)(cp.start)
        token[...] = jnp.zeros_like(token)

    hbm = [pltpu.with_memory_space_constraint(b, pltpu.HBM) for b in bufs]
    outs = pl.pallas_call(
        body,
        name=name,
        in_specs=[HBM_SPEC] * n,
        out_specs=(SEM_SPEC, SEM_SPEC, *([HBM_SPEC] * n), pl.BlockSpec(memory_space=pltpu.VMEM)),
        out_shape=(pltpu.SemaphoreType.DMA((n_copies,)), pltpu.SemaphoreType.DMA((n_copies,)),
                   *[pltpu.HBM(b.shape, b.dtype) for b in bufs], jax.ShapeDtypeStruct((8, 128), F32)),
        input_output_aliases={a: 2 + a for a in range(n)},
        compiler_params=pltpu.CompilerParams(has_side_effects=DATAFLOW),
    )(*hbm)
    return outs[0], outs[1], list(outs[2:2 + n]), outs[-1]


def _copies_wait(send_sems, recv_sems, bufs, after, plan, name):
    n = len(bufs)

    def body(*refs):
        ins = refs[:n]
        s_sems, r_sems = refs[n], refs[n + 1]
        for k, send, recv in plan(ins):
            if send is not None:
                src, dst, dev, pred = send
                cp = pltpu.make_async_remote_copy(src_ref=src, dst_ref=dst, send_sem=s_sems.at[k],
                                                  recv_sem=r_sems.at[k], device_id=dev, device_id_type=MESH)
                if pred is None:
                    cp.wait_send()
                else:
                    pl.when(pred)(cp.wait_send)
            if recv is not None:
                dst, pred = recv
                cp = pltpu.make_async_remote_copy(src_ref=dst, dst_ref=dst, send_sem=s_sems.at[k],
                                                  recv_sem=r_sems.at[k], device_id=_coords(), device_id_type=MESH)
                if pred is None:
                    cp.wait_recv()
                else:
                    pl.when(pred)(cp.wait_recv)

    outs = pl.pallas_call(
        body,
        name=name,
        in_specs=[HBM_SPEC] * n + [SEM_SPEC, SEM_SPEC, pl.BlockSpec(memory_space=pl.ANY)],
        out_specs=[HBM_SPEC] * n,
        out_shape=[pltpu.HBM(b.shape, b.dtype) for b in bufs],
        input_output_aliases={a: a for a in range(n)},
        compiler_params=pltpu.CompilerParams(has_side_effects=DATAFLOW),
    )(*bufs, send_sems, recv_sems, after)
    return list(outs)


def _gather_plan(n_bufs):
    def plan(refs):
        x, y, c = _coords()
        me = 2 * x + y
        out = []
        for k, (px, py) in enumerate(_other_chips(x, y)):
            for a in range(n_bufs):
                out.append((k * n_bufs + a, (refs[a].at[me], refs[a].at[me], (px, py, c), None),
                            (refs[a].at[2 * px + py], None)))
        return out
    return plan


def _cast_into_slot(ws, name):
    n = len(ws)
    nt = 2

    def body(s_ref, *refs):
        for a in range(n):
            refs[n + a][0] = refs[a][...].astype(refs[n + a].dtype)

    xi, yi, _ = _coords()
    return pl.pallas_call(
        body,
        name=name,
        grid_spec=pltpu.PrefetchScalarGridSpec(
            num_scalar_prefetch=1,
            grid=(2, nt),
            in_specs=[pl.BlockSpec((1, w.shape[1] // nt, w.shape[2]), lambda hf, i, s: (hf, i, 0)) for w in ws],
            out_specs=[pl.BlockSpec((1, 1, w.shape[1] // nt, w.shape[2]), lambda hf, i, s: (s[0], hf, i, 0)) for w in ws],
        ),
        out_shape=[jax.ShapeDtypeStruct((N_CHIPS,) + w.shape, _MXU) for w in ws],
        compiler_params=_params(("parallel", "parallel")),
    )((2 * xi + yi).reshape(1).astype(jnp.int32), *ws)


def _gather_chips(bufs, n_chunks, name):
    n = len(bufs)
    pieces = [(a, rows) for a in range(n) for rows in _chunks(bufs[a].shape[2], n_chunks[a])]
    n_p = len(pieces)

    def body(*refs):
        outs = refs[n:2 * n]
        send_sems, recv_sems, fsend_sems, frecv_sems = refs[2 * n:]
        x, y, c = _coords()
        me = 2 * x + y
        chips = _other_chips(x, y)

        def send(k, i, slot, chip):
            a, rows = pieces[i]
            return pltpu.make_async_remote_copy(
                src_ref=outs[a].at[slot, c, rows], dst_ref=outs[a].at[slot, c, rows], send_sem=send_sems.at[k * n_p + i],
                recv_sem=recv_sems.at[k * n_p + i], device_id=(*chip, c), device_id_type=MESH)

        def forward(k, i, slot, half):
            a, rows = pieces[i]
            return pltpu.make_async_remote_copy(
                src_ref=outs[a].at[slot, half, rows], dst_ref=outs[a].at[slot, half, rows],
                send_sem=fsend_sems.at[k * n_p + i], recv_sem=frecv_sems.at[k * n_p + i],
                device_id=(x, y, 1 - c), device_id_type=MESH)

        sends = [send(k, i, me, chip) for i in range(n_p) for k, chip in enumerate(chips)]
        for cp in sends:
            cp.start()
        forwards = []
        for i in range(n_p):
            for k, (px, py) in enumerate(chips):
                send(k, i, 2 * px + py, (px, py)).wait_recv()
                fw = forward(k, i, 2 * px + py, c)
                fw.start()
                forwards.append(fw)
        for i in range(n_p):
            for k, (px, py) in enumerate(chips):
                forward(k, i, 2 * px + py, 1 - c).wait_recv()
        for cp in sends + forwards:
            cp.wait_send()

    anyspec = pl.BlockSpec(memory_space=pl.ANY)
    sems = pltpu.SemaphoreType.DMA((3 * n_p,))
    return pl.pallas_call(
        body,
        name=name,
        in_specs=[anyspec] * n,
        out_specs=[anyspec] * n,
        out_shape=[jax.ShapeDtypeStruct(b.shape, b.dtype) for b in bufs],
        input_output_aliases={a: a for a in range(n)},
        scratch_shapes=[sems, sems, sems, sems],
    )(*bufs)


def _swap_plan(n_slabs):
    def plan(refs):
        x, y, c = _coords()
        out, k = [], 0
        for i, n in enumerate(n_slabs):
            g, land = refs[2 * i], refs[2 * i + 1]
            for p in range(n):
                out.append((k, (g.at[p, 1 - c], land.at[p], (x, y, 1 - c), None), (land.at[p], None)))
                k += 1
        return out
    return plan


def _is_one_of(chip, dests):
    hit = chip == dests[0]
    for d in dests[1:]:
        hit = hit | (chip == d)
    return hit


def _slab_of(chip, dests):
    return sum(j * (chip == d).astype(jnp.int32) for j, d in enumerate(dests))


def _scatter_plan(dest_sets):
    def plan(refs):
        x, y, c = _coords()
        me = 2 * x + y
        out = []
        for k, (px, py) in enumerate(_other_chips(x, y)):
            peer = 2 * px + py
            for i, dests in enumerate(dest_sets):
                cs, land = refs[2 * i], refs[2 * i + 1]
                everyone = len(dests) == N_CHIPS
                send = (cs.at[_slab_of(peer, dests)], land.at[k], (px, py, c),
                        None if everyone else _is_one_of(peer, dests))
                recv = (land.at[k], None if everyone else _is_one_of(me, dests))
                out.append((k * len(dest_sets) + i, send, recv))
        return out
    return plan


def _allgather_plan():
    def plan(refs):
        x, y, c = _coords()
        src, land = refs
        me = 4 * x + 2 * y + c
        out = []
        for r in range(1, 8):
            px = 1 - x if r & 4 else x
            py = 1 - y if r & 2 else y
            pc = 1 - c if r & 1 else c
            out.append((r - 1, (src, land.at[me], (px, py, pc), None), (land.at[4 * px + 2 * py + pc], None)))
        return out
    return plan


def _sum_gathered(own, land, name):
    def body(own_ref, land_ref, o_ref):
        x, y, c = _coords()
        me = 4 * x + 2 * y + c
        acc = jnp.zeros(own_ref.shape, F32)
        for d in range(8):
            acc = acc + (land_ref[d] + jnp.where(me == d, own_ref[...], 0.0))
        o_ref[...] = acc

    return pl.pallas_call(
        body,
        name=name,
        out_shape=jax.ShapeDtypeStruct(own.shape, F32),
        compiler_params=_params(),
    )(own, land)


def _join_halves(bufs, n_chunks, name):
    n = len(bufs)
    pieces = [(a, rows) for a in range(n) for rows in _chunks(bufs[a].shape[1], n_chunks[a])]
    n_p = len(pieces)

    def body(*refs):
        outs = refs[n:2 * n]
        send_sems, recv_sems = refs[2 * n:]
        x, y, c = _coords()

        def copy(i, half):
            a, rows = pieces[i]
            return pltpu.make_async_remote_copy(
                src_ref=outs[a].at[half, rows], dst_ref=outs[a].at[half, rows], send_sem=send_sems.at[i],
                recv_sem=recv_sems.at[i], device_id=(x, y, 1 - c), device_id_type=MESH)

        sends = [copy(i, c) for i in range(n_p)]
        for cp in sends:
            cp.start()
        for i in range(n_p):
            copy(i, 1 - c).wait_recv()
        for cp in sends:
            cp.wait_send()

    anyspec = pl.BlockSpec(memory_space=pl.ANY)
    sems = pltpu.SemaphoreType.DMA((n_p,))
    return pl.pallas_call(
        body,
        name=name,
        in_specs=[anyspec] * n,
        out_specs=[anyspec] * n,
        out_shape=[jax.ShapeDtypeStruct(b.shape, b.dtype) for b in bufs],
        input_output_aliases={a: a for a in range(n)},
        scratch_shapes=[sems, sems],
    )(*bufs)


def _row_tile(rows, cap):
    t = cap
    while rows % t:
        t //= 2
    return t


def _add_my_half(g, r, name):
    n_slabs, _, R, C = g.shape
    tr = _row_tile(R, 256)

    def body(c_ref, g_ref, r_ref, o_ref):
        o_ref[...] = (g_ref[0] + r_ref[...]).astype(o_ref.dtype)

    return pl.pallas_call(
        body,
        name=name,
        grid_spec=pltpu.PrefetchScalarGridSpec(
            num_scalar_prefetch=1,
            grid=(n_slabs, R // tr),
            in_specs=[pl.BlockSpec((1, 1, tr, C), lambda p, i, c_ref: (p, c_ref[0], i, 0)),
                      pl.BlockSpec((1, tr, C), lambda p, i, c_ref: (p, i, 0))],
            out_specs=pl.BlockSpec((1, tr, C), lambda p, i, c_ref: (p, i, 0)),
        ),
        out_shape=jax.ShapeDtypeStruct(r.shape, jnp.bfloat16),
        compiler_params=_params(("parallel", "parallel")),
    )(lax.axis_index("c").reshape(1).astype(jnp.int32), g, r)


def _sum_slabs(own, got, name):
    _, R, C = own.shape
    tr = _row_tile(R, 256)

    def body(s_ref, own_ref, got_ref, o_ref):
        o_ref[0] = ((own_ref[0].astype(F32) + got_ref[0].astype(F32)) + got_ref[1].astype(F32)) + got_ref[2].astype(F32)

    xi, yi, ci = _coords()
    return pl.pallas_call(
        body,
        name=name,
        grid_spec=pltpu.PrefetchScalarGridSpec(
            num_scalar_prefetch=1,
            grid=(R // tr,),
            in_specs=[pl.BlockSpec((1, tr, C), lambda i, s: (s[0], i, 0)),
                      pl.BlockSpec((3, tr, C), lambda i, s: (0, i, 0))],
            out_specs=pl.BlockSpec((1, tr, C), lambda i, s: (s[1], i, 0)),
        ),
        out_shape=jax.ShapeDtypeStruct((2, R, C), F32),
        compiler_params=_params(("parallel",)),
    )(jnp.stack([2 * xi + yi, ci]).astype(jnp.int32), own, got)


def _sum_parts(owns, gots, dest_sets, name):
    n = len(owns)
    _, R, C = owns[0].shape
    tr = _row_tile(R, 256)

    def body(s_ref, *refs):
        o_ref = refs[-1]
        total = jnp.zeros((tr, C), F32)
        for i in range(n):
            total = total + jnp.where(s_ref[2 + 2 * i] == 1, refs[i][0].astype(F32), 0.0)
        for i in range(n):
            got = refs[n + i]
            total = ((total + got[0].astype(F32)) + got[1].astype(F32)) + got[2].astype(F32)
        o_ref[0] = total

    xi, yi, ci = _coords()
    me = 2 * xi + yi
    scalars = [ci, ci]
    for dests in dest_sets:
        scalars += [_is_one_of(me, dests).astype(jnp.int32), _slab_of(me, dests)]
    own_spec = lambda i: pl.BlockSpec((1, tr, C), lambda r, s: (s[3 + 2 * i], r, 0))
    return pl.pallas_call(
        body,
        name=name,
        grid_spec=pltpu.PrefetchScalarGridSpec(
            num_scalar_prefetch=1,
            grid=(R // tr,),
            in_specs=[own_spec(i) for i in range(n)] + [pl.BlockSpec((3, tr, C), lambda r, s: (0, r, 0))] * n,
            out_specs=pl.BlockSpec((1, tr, C), lambda r, s: (s[0], r, 0)),
        ),
        out_shape=jax.ShapeDtypeStruct((2, R, C), F32),
        compiler_params=_params(("parallel",)),
    )(jnp.stack(scalars).astype(jnp.int32), *owns, *gots)


def _sum_rows8(g, m_per, name):
    n = g.shape[1]

    def body(g_ref, o_ref):
        acc = g_ref[0:m_per, :]
        for k in range(1, 8):
            acc = acc + g_ref[k * m_per:(k + 1) * m_per, :]
        o_ref[...] = acc

    return pl.pallas_call(
        body,
        name=name,
        out_shape=jax.ShapeDtypeStruct((m_per, n), F32),
        compiler_params=_params(),
    )(g)


def _adamw_math(w, g, m, v):
    m = ADAM_B1 * m + (1.0 - ADAM_B1) * g
    v = ADAM_B2 * v + (1.0 - ADAM_B2) * (g * g)
    m_hat = m / (1.0 - ADAM_B1 ** ADAM_STEP)
    v_hat = v / (1.0 - ADAM_B2 ** ADAM_STEP)
    delta = -ADAM_LR * (m_hat / (jnp.sqrt(v_hat) + ADAM_EPS) + ADAM_WD * w)
    return delta, m, v


def _adamw_big(w, g, m, v, name):
    R, C = w.shape
    tr = min(128, R)

    def body(w_ref, g_ref, m_ref, v_ref, d_out, m_out, v_out):
        d, mn, vn = _adamw_math(w_ref[...], g_ref[...], m_ref[...], v_ref[...])
        d_out[...] = d
        m_out[...] = mn
        v_out[...] = vn

    spec = pl.BlockSpec((tr, C), lambda i: (i, 0))
    return pl.pallas_call(
        body,
        name=name,
        grid=(R // tr,),
        in_specs=[spec] * 4,
        out_specs=[spec] * 3,
        out_shape=[jax.ShapeDtypeStruct((R, C), F32)] * 3,
        compiler_params=_params(("parallel",)),
    )(w, g, m, v)


def _adamw_small(ws, gs, ms, vs, name):
    n = len(ws)

    def body(*refs):
        for a in range(n):
            d, mn, vn = _adamw_math(refs[a][...], refs[n + a][...], refs[2 * n + a][...], refs[3 * n + a][...])
            refs[4 * n + a][...] = d
            refs[5 * n + a][...] = mn
            refs[6 * n + a][...] = vn

    shapes = [jax.ShapeDtypeStruct(w.shape, F32) for w in ws]
    outs = pl.pallas_call(
        body,
        name=name,
        out_shape=shapes * 3,
        compiler_params=_params(),
    )(*ws, *gs, *ms, *vs)
    return outs[:n], outs[n:2 * n], outs[2 * n:]


def _to_blockdiag(w):
    per = CW // LRU_BW
    w4 = w.reshape(N_CT, per, LRU_BW, LRU_BW)
    eye = jnp.eye(per, dtype=w.dtype)
    return (w4[:, :, :, None, :] * eye[None, :, None, :, None]).reshape(N_CT, CW, CW)


def _from_blockdiag(g):
    per = CW // LRU_BW
    g5 = g.reshape(N_CT, per, LRU_BW, per, LRU_BW)
    return jnp.stack([g5[:, b, :, b, :] for b in range(per)], axis=1).reshape(LRU_BLOCKS, LRU_BW, LRU_BW)


def _local_grads(x2d, tgt2d, B, S, g_in, w_all, conv_w, conv_b, gate_x_w, gate_x_b, gate_a_w, gate_a_b, lam, gain,
                 proj_weights, g_fin, reduce, deps=()):
    wx_bd = _c(_to_blockdiag(gate_x_w))
    wa_bd = _c(_to_blockdiag(gate_a_w))
    tables = _retention_tables(S)
    gain3 = gain.reshape(HEADS, 1, DK)

    proj, ht = _inproj_fwd(x2d, g_in, w_all, deps)
    hlru, ya = _lru_fwd(proj, conv_w, conv_b, wx_bd, wa_bd, gate_x_b, gate_a_b, lam, B, S)
    o_pre, yb, states = _ret_fwd(proj, tables, gain3, B, S)
    wpa, wpb, wout = proj_weights(yb)
    loss, dx2, dya, dyb, dm, dgf, gw_proj = _mid(ya, yb, proj, x2d, tgt2d, wpa, wpb, wout, g_fin)
    g3 = _inproj_bwd_dw(ht, [dm], "inproj_bwd_dw_m")
    deps = reduce.m_ready(gw_proj, g3)
    dr, dgain = _ret_bwd(dyb, o_pre, proj, states, tables, gain3, B, S, deps)
    deps = reduce.ret_done(dr)
    g12 = _inproj_bwd_dw(ht, [dr], "inproj_bwd_dw_r", deps)
    deps = reduce.r_ready(g12)
    dxa, dga, dcw, dcb, dwx_bd, dwa_bd, dbx, dba, dlam = _lru_bwd(
        dya, proj, hlru, conv_w, conv_b, wx_bd, wa_bd, gate_x_b, gate_a_b, lam, B, S, deps)
    deps = reduce.lru_done(dxa)
    small = dict(conv_w=dcw, conv_b=dcb, gate_x_w=_from_blockdiag(dwx_bd), gate_x_b=dbx,
                 gate_a_w=_from_blockdiag(dwa_bd), gate_a_b=dba, lru_lambda=dlam, gn_gain=dgain.reshape(HEADS, DK),
                 norm_final=dgf)
    deps = deps + reduce.small_ready(_pack_small(small))
    g0 = _inproj_bwd_dw(ht, [dxa, dga], "inproj_bwd_dw_a", deps)
    deps = reduce.a_ready(g0)
    dparts = [dxa, dga, dr, dm]
    n_tiles = x2d.shape[0] // min(DX_TILE, x2d.shape[0])
    first = (n_tiles + 3) // 4 if n_tiles > 1 else 0
    done = None
    if first:
        done = _inproj_bwd_dx(dparts, w_all, x2d, dx2, g_in, 0, first, None, "inproj_bwd_dx_0", deps)
        deps = reduce.dx_half_done(done[0])
    grad_x, dgin = _inproj_bwd_dx(dparts, w_all, x2d, dx2, g_in, first, n_tiles - first, done, "inproj_bwd_dx_1", deps)
    if not first:
        reduce.dx_half_done(grad_x)
    return loss[0, 0], grad_x, dgin


ALL_CHIPS = (0, 1, 2, 3)


class _GradReduce:
    def __init__(self):
        self.pending = {}

    def _start(self, key, bufs, plan, n_copies, name):
        send_sems, recv_sems, bufs, token = _copies_start(bufs, plan, n_copies, name + "_start")
        self.pending[key] = (send_sems, recv_sems, bufs, plan, name + "_wait")
        return (token,)

    def _finish(self, key, after):
        send_sems, recv_sems, bufs, plan, name = self.pending.pop(key)
        return _copies_wait(send_sems, recv_sems, bufs, after, plan, name)

    def _swap(self, key, parts):
        bufs = []
        for g in parts:
            bufs += [g, lax.empty((g.shape[0],) + g.shape[2:], F32)]
        n_slabs = [g.shape[0] for g in parts]
        return self._start(key, bufs, _swap_plan(n_slabs), sum(n_slabs), "swap_" + key)

    def _chip_sums(self, key, after):
        bufs = self._finish(key, after)
        return [_add_my_half(bufs[2 * i], bufs[2 * i + 1], "chip_sum_%s%d" % (key, i)) for i in range(len(bufs) // 2)]

    def _scatter(self, key, sums, dest_sets):
        bufs = []
        for cs in sums:
            bufs += [cs, jnp.zeros((3,) + cs.shape[1:], cs.dtype)]
        return self._start(key, bufs, _scatter_plan(dest_sets), 3 * len(sums), "scatter_" + key)

    def m_ready(self, gw_proj, g3):
        rows = gw_proj.shape[2] * gw_proj.shape[3]
        return self._swap("m", [gw_proj.reshape(N_CHIPS, 2, rows, D_MODEL), g3])

    def ret_done(self, after):
        return self._scatter("sm", self._chip_sums("m", after), [ALL_CHIPS, (3,)])

    def r_ready(self, g12):
        return self._swap("r", [g12])

    def lru_done(self, after):
        return self._scatter("sr", self._chip_sums("r", after), [(1, 2)])

    def small_ready(self, packed):
        land = jnp.zeros((8,) + packed.shape, F32)
        return self._start("small", [packed, land], _allgather_plan(), 7, "gather_small")

    def a_ready(self, g0):
        return self._swap("a", [g0])

    def dx_half_done(self, after):
        return self._scatter("sa", self._chip_sums("a", after), [(0,)])

    def finish(self, after):
        small_sum = _sum_gathered(*self._finish("small", after), "sum_small_grads")
        csp, gotp, cs3, got3 = self._finish("sm", after)
        cs12, got12 = self._finish("sr", after)
        cs0, got0 = self._finish("sa", after)
        half_in = _sum_parts([cs3, cs12, cs0], [got3, got12, got0], [(3,), (1, 2), (0,)], "sum_w_in")
        half_pr = _sum_slabs(csp, gotp, "sum_w_proj")
        return small_sum, _join_halves([half_in, half_pr], [8, 4], "join_halves")


_SMALL = ("gate_x_w", "gate_a_w", "conv_w", "conv_b", "gate_x_b", "gate_a_b", "lru_lambda", "gn_gain", "norm_final")
_SMALL_SHAPES = dict(gate_x_w=(LRU_BLOCKS, LRU_BW, LRU_BW), gate_a_w=(LRU_BLOCKS, LRU_BW, LRU_BW),
                     norm_in=(1, D_MODEL), conv_w=(CONV, D_MODEL), conv_b=(1, D_MODEL), gate_x_b=(1, D_MODEL),
                     gate_a_b=(1, D_MODEL), lru_lambda=(1, D_MODEL), gn_gain=(HEADS, DK), norm_final=(1, D_MODEL))


def _pack_small(small):
    return jnp.concatenate([small[k].reshape(-1, 128) for k in _SMALL], axis=0)


def _unpack_small(packed):
    out, r = {}, 0
    for k in _SMALL:
        shape = _SMALL_SHAPES[k]
        rows = 1
        for s in shape:
            rows *= s
        rows //= 128
        out[k] = packed[r:r + rows].reshape(shape)
        r += rows
    return out


def kernel(x, norm_in, w_in, conv_w, conv_b, gate_x_w, gate_x_b, gate_a_w, gate_a_b, lru_lambda, gn_gain, w_proj_a, w_proj_b, w_out, norm_final, loss_target, m_norm_in, m_w_in, m_conv_w, m_conv_b, m_gate_x_w, m_gate_x_b, m_gate_a_w, m_gate_a_b, m_lru_lambda, m_gn_gain, m_w_proj_a, m_w_proj_b, m_w_out, m_norm_final, v_norm_in, v_w_in, v_conv_w, v_conv_b, v_gate_x_w, v_gate_x_b, v_gate_a_w, v_gate_a_b, v_lru_lambda, v_gn_gain, v_w_proj_a, v_w_proj_b, v_w_out, v_norm_final):
    B, S, _ = x.shape
    T = B * S
    xi, yi, ci = _coords()
    chip = 2 * xi + yi

    cshard = D_MODEL // N_CHIPS
    mine = _cast_into_slot([w_in[0].reshape(2, D_MODEL // 2, 2 * D_MODEL)]
                           + [w[0].reshape(2, cshard // 2, D_MODEL) for w in (w_proj_a, w_proj_b, w_out)],
                           "cast_weights")
    plan = _gather_plan(3)
    s_sems, r_sems, pbufs, token = _copies_start(mine[1:], plan, 9, "gather_proj_start")
    w_all = _gather_chips(mine[:1], [4], "gather_weights")[0].reshape(N_CHIPS, D_MODEL, 2 * D_MODEL)

    def proj_weights(after):
        got = _copies_wait(s_sems, r_sems, pbufs, after, plan, "gather_proj_wait")
        return [b.reshape(D_MODEL, D_MODEL) for b in got]

    gshard = DK // N_CHIPS
    tiny = jnp.concatenate([conv_w[0], jnp.zeros((4, cshard), F32), jnp.pad(gn_gain[0], ((0, 4), (0, cshard - gshard)))],
                           axis=0)
    tiny_all = _all_gather8(tiny, "gather_small_weights").reshape(N_CHIPS, 2, 16, cshard)[:, 0]
    conv_w_full = jnp.transpose(tiny_all[:, 0:CONV, :], (1, 0, 2)).reshape(CONV, D_MODEL)
    gain_full = jnp.transpose(tiny_all[:, 8:8 + HEADS, :gshard], (1, 0, 2)).reshape(HEADS, DK)

    reduce = _GradReduce()
    loss, grad_x, dgin = _local_grads(
        x.reshape(T, D_MODEL), loss_target.reshape(T, D_MODEL), B, S, norm_in, w_all, conv_w_full, conv_b,
        gate_x_w[0], gate_x_b, gate_a_w[0], gate_a_b, lru_lambda, gain_full, proj_weights,
        norm_final.reshape(1, D_MODEL), reduce, deps=(token,))
    loss = lax.psum(loss, ("x", "y", "c"))

    g_norm_in = _sum_rows8(_all_gather8(dgin.reshape(8, 128), "gather_norm_in_grad"), 8, "sum_norm_in_grad")
    small_sum, (g_in_full, g_pr_full) = reduce.finish(g_norm_in)
    gsm = _unpack_small(small_sum)
    gsm["norm_in"] = g_norm_in.reshape(1, D_MODEL)
    g_w_in = g_in_full.reshape(D_MODEL, 2 * D_MODEL)
    g_pr = g_pr_full.reshape(2, 3, D_MODEL // (2 * N_CHIPS), D_MODEL)
    g_wpa, g_wpb, g_wout = (g_pr[:, k].reshape(cshard, D_MODEL) for k in range(3))

    grads = dict(gsm)
    grads["conv_w"] = lax.dynamic_slice_in_dim(gsm["conv_w"], chip * cshard, cshard, axis=1)
    grads["gn_gain"] = lax.dynamic_slice_in_dim(gsm["gn_gain"], chip * gshard, gshard, axis=1)
    grads.update(w_in=g_w_in, w_proj_a=g_wpa, w_proj_b=g_wpb, w_out=g_wout)

    weights = dict(norm_in=norm_in, w_in=w_in, conv_w=conv_w, conv_b=conv_b, gate_x_w=gate_x_w, gate_x_b=gate_x_b,
                   gate_a_w=gate_a_w, gate_a_b=gate_a_b, lru_lambda=lru_lambda, gn_gain=gn_gain, w_proj_a=w_proj_a,
                   w_proj_b=w_proj_b, w_out=w_out, norm_final=norm_final)
    ms = dict(norm_in=m_norm_in, w_in=m_w_in, conv_w=m_conv_w, conv_b=m_conv_b, gate_x_w=m_gate_x_w,
              gate_x_b=m_gate_x_b, gate_a_w=m_gate_a_w, gate_a_b=m_gate_a_b, lru_lambda=m_lru_lambda, gn_gain=m_gn_gain,
              w_proj_a=m_w_proj_a, w_proj_b=m_w_proj_b, w_out=m_w_out, norm_final=m_norm_final)
    vs = dict(norm_in=v_norm_in, w_in=v_w_in, conv_w=v_conv_w, conv_b=v_conv_b, gate_x_w=v_gate_x_w,
              gate_x_b=v_gate_x_b, gate_a_w=v_gate_a_w, gate_a_b=v_gate_a_b, lru_lambda=v_lru_lambda, gn_gain=v_gn_gain,
              w_proj_a=v_w_proj_a, w_proj_b=v_w_proj_b, w_out=v_w_out, norm_final=v_norm_final)
    names = list(weights)
    grads = {k: grads[k].reshape(weights[k].shape) for k in names}

    delta, new_m, new_v = {}, {}, {}
    for k in ("w_in", "w_proj_a", "w_proj_b", "w_out"):
        shp = weights[k].shape
        two = lambda a: a.reshape(shp[1], shp[2])
        d, mn, vn = _adamw_big(two(weights[k]), two(grads[k]), two(ms[k]), two(vs[k]), "adamw_" + k)
        delta[k], new_m[k], new_v[k] = d.reshape(shp), mn.reshape(shp), vn.reshape(shp)
    smalls = [k for k in names if k not in delta]

    def view(a):
        return a.reshape(1, -1) if a.ndim == 1 else (a.reshape(a.shape[1:]) if a.ndim > 2 else a)

    ds, mns, vns = _adamw_small([view(weights[k]) for k in smalls], [view(grads[k]) for k in smalls],
                                [view(ms[k]) for k in smalls], [view(vs[k]) for k in smalls], "adamw_small")
    for k, d, mn, vn in zip(smalls, ds, mns, vns):
        shp = weights[k].shape
        delta[k], new_m[k], new_v[k] = d.reshape(shp), mn.reshape(shp), vn.reshape(shp)

    return (loss, grad_x.reshape(B, S, D_MODEL), *[grads[k] for k in names], *[delta[k] for k in names],
            *[new_m[k] for k in names], *[new_v[k] for k in names])
```

```python
import functools

import jax
import jax.numpy as jnp
from jax import lax
from jax.experimental import pallas as pl
from jax.experimental.pallas import tpu as pltpu

F32 = jnp.float32
_MXU = jnp.bfloat16

D_MODEL = 1024
N_GROUPS = 8
HEADS = 4
DK = 256
CHUNK = 128
CONV = 4
LRU_BLOCKS = 16
LRU_BW = 64
LRU_C = 8.0
ROPE_THETA = 10000.0
EPS = 1e-6
CW = 256
N_CT = D_MODEL // CW
N_CHIPS = 4
MESH = pl.DeviceIdType.MESH

ADAM_LR = 0.001
ADAM_B1 = 0.9
ADAM_B2 = 0.999
ADAM_EPS = 1e-08
ADAM_WD = 0.01
ADAM_STEP = 10

VMEM_LIMIT = 56 * 1024 * 1024


def _c(v):
    return v.astype(_MXU)


def _dot(a, b):
    return lax.dot_general(a, b, (((1,), (0,)), ((), ())), preferred_element_type=F32)


def _dot_nt(a, b):
    return lax.dot_general(a, b, (((1,), (1,)), ((), ())), preferred_element_type=F32)


def _dot_tn(a, b):
    return lax.dot_general(a, b, (((0,), (0,)), ((), ())), preferred_element_type=F32)


def _sigmoid(z):
    return 0.5 * jnp.tanh(0.5 * z) + 0.5


ANY_SPEC = pl.BlockSpec(memory_space=pl.ANY)


def _after(body, n_in, deps):
    n_deps = len(deps)

    def wrapped(*refs):
        return body(*refs[:n_in], *refs[n_in + n_deps:])

    return wrapped


def _params(sem=None):
    if sem is None:
        return pltpu.CompilerParams(vmem_limit_bytes=VMEM_LIMIT)
    return pltpu.CompilerParams(vmem_limit_bytes=VMEM_LIMIT, dimension_semantics=sem)


def _inproj_fwd(x2d, g_in, w_all, deps=()):
    T = x2d.shape[0]
    tm = min(1024, T)
    n_i = T // tm

    def body(*refs):
        x_ref, g_ref, w_ref = refs[:3]
        proj_ref, ht_ref, h_all = refs[-3:]
        i = pl.program_id(1)
        rows = pl.ds(pl.multiple_of(i * tm, tm), tm)

        @pl.when(pl.program_id(0) == 0)
        def _():
            x = x_ref[...]
            r = lax.rsqrt(jnp.mean(x * x, axis=-1, keepdims=True) + EPS)
            h = x * r * g_ref[...]
            h_all[rows, :] = h.astype(h_all.dtype)
            ht_ref[...] = h.T.astype(ht_ref.dtype)

        proj_ref[...] = _dot(h_all[rows, :], w_ref[0])

    first = lambda j, i: jnp.where(j == 0, i, n_i - 1)
    return pl.pallas_call(
        body,
        name="inproj_fwd",
        grid=(N_GROUPS, n_i),
        in_specs=[
            pl.BlockSpec((tm, D_MODEL), lambda j, i: (first(j, i), 0)),
            pl.BlockSpec((1, D_MODEL), lambda j, i: (0, 0)),
            pl.BlockSpec((1, D_MODEL, D_MODEL), lambda j, i: (j // 2, 0, j % 2)),
        ] + [pl.BlockSpec(memory_space=pl.ANY)] * len(deps),
        out_specs=[
            pl.BlockSpec((tm, D_MODEL), lambda j, i: (i, j)),
            pl.BlockSpec((D_MODEL, tm), lambda j, i: (0, first(j, i))),
        ],
        out_shape=[
            jax.ShapeDtypeStruct((T, N_GROUPS * D_MODEL), F32),
            jax.ShapeDtypeStruct((D_MODEL, T), _MXU),
        ],
        scratch_shapes=[pltpu.VMEM((T, D_MODEL), _MXU)],
        compiler_params=_params(("arbitrary", "arbitrary")),
    )(x2d, g_in, w_all, *deps)


def _scan_fwd(a, u):
    n = a.shape[0]
    row = lax.broadcasted_iota(jnp.int32, a.shape, 0)
    s = 1
    while s < n:
        m = row >= s
        u = u + a * jnp.where(m, pltpu.roll(u, s, 0), 0.0)
        a = a * jnp.where(m, pltpu.roll(a, s, 0), 1.0)
        s *= 2
    return a, u


def _scan_bwd(b, g):
    n = b.shape[0]
    row = lax.broadcasted_iota(jnp.int32, b.shape, 0)
    s = 1
    while s < n:
        m = row < n - s
        g = g + b * jnp.where(m, pltpu.roll(g, n - s, 0), 0.0)
        b = b * jnp.where(m, pltpu.roll(b, n - s, 0), 1.0)
        s *= 2
    return b, g


LANES = 128
SUBLANES = 8


def _scan_scratch(tc):
    by_lanes = pltpu.VMEM((CW // LANES, tc, LANES), F32)
    return [by_lanes, by_lanes, pltpu.VMEM((tc // SUBLANES, CW), F32), pltpu.VMEM((tc, CW), F32)]


def _scan_tile(a, u, edge, la_ref, lh_ref, c_ref, dst_ref, reverse):
    n, w = a.shape
    groups = n // SUBLANES
    a3 = a.reshape(groups, SUBLANES, w)
    u3 = u.reshape(groups, SUBLANES, w)
    row = lax.broadcasted_iota(jnp.int32, a3.shape, 1)
    for s in (1, 2, 4):
        m = (row < SUBLANES - s) if reverse else (row >= s)
        shift = SUBLANES - s if reverse else s
        u3 = u3 + a3 * jnp.where(m, pltpu.roll(u3, shift, 1), 0.0)
        a3 = a3 * jnp.where(m, pltpu.roll(a3, shift, 1), 1.0)
    al = a3.reshape(n, w)
    hl = u3.reshape(n, w)
    blocks = w // LANES
    for q in range(blocks):
        la_ref[q] = al[:, q * LANES:(q + 1) * LANES]
        lh_ref[q] = hl[:, q * LANES:(q + 1) * LANES]
    ends = pl.ds(0 if reverse else SUBLANES - 1, groups, stride=SUBLANES)
    end_a = jnp.concatenate([la_ref.at[q][ends, :] for q in range(blocks)], axis=-1)
    end_h = jnp.concatenate([lh_ref.at[q][ends, :] for q in range(blocks)], axis=-1)
    prod, part = (_scan_bwd if reverse else _scan_fwd)(end_a, end_h)
    total = part + prod * edge
    g_row = lax.broadcasted_iota(jnp.int32, total.shape, 0)
    if reverse:
        c_ref[...] = jnp.where(g_row == groups - 1, edge, pltpu.roll(total, groups - 1, 0))
    else:
        c_ref[...] = jnp.where(g_row == 0, edge, pltpu.roll(total, 1, 0))
    for g in range(groups):
        rows = slice(g * SUBLANES, (g + 1) * SUBLANES)
        for q in range(blocks):
            cols = slice(q * LANES, (q + 1) * LANES)
            dst_ref[rows, cols] = lh_ref[q, rows, :] + la_ref[q, rows, :] * c_ref[g:g + 1, cols]


def _softplus_neg(lam):
    z = -lam
    return jnp.maximum(z, 0.0) + jnp.log1p(jnp.exp(-jnp.abs(z)))


def _lru_gates(xc, wx_ref, wa_ref, bx_ref, ba_ref, lam_ref):
    xcb = _c(xc)
    i_t = _sigmoid(_dot(xcb, wx_ref[0]) + bx_ref[...])
    r_t = _sigmoid(_dot(xcb, wa_ref[0]) + ba_ref[...])
    sp = _softplus_neg(lam_ref[...])
    log_a = (-LRU_C) * r_t * sp
    a = jnp.exp(log_a)
    mult = jnp.sqrt(1.0 - a * a)
    return xcb, i_t, r_t, sp, a, mult


def _conv_from_ext(ext_ref, xa, cw_ref, cb_ref, tc):
    return (cb_ref[...] + cw_ref[3:4, :] * xa + cw_ref[2:3, :] * ext_ref[7:7 + tc, :]
            + cw_ref[1:2, :] * ext_ref[6:6 + tc, :] + cw_ref[0:1, :] * ext_ref[5:5 + tc, :])


def _lru_fwd(proj, conv_w, conv_b, wx_bd, wa_bd, bx, ba, lam, B, S):
    T = B * S
    tc = min(256, S)
    nt = S // tc
    h8 = tc // 8

    def body(xa_ref, halo_ref, ga_ref, cw_ref, cb_ref, wx_ref, wa_ref, bx_ref, ba_ref, lam_ref,
             h_ref, ya_ref, ext_ref, carry_ref, la_ref, lh_ref, c_ref):
        t = pl.program_id(2)

        @pl.when(t == 0)
        def _():
            carry_ref[...] = jnp.zeros_like(carry_ref)

        xa = xa_ref[...]
        ext_ref[0:8, :] = jnp.where(t == 0, 0.0, halo_ref[...])
        ext_ref[8:8 + tc, :] = xa
        xc = _conv_from_ext(ext_ref, xa, cw_ref, cb_ref, tc)
        _, i_t, _, _, a, mult = _lru_gates(xc, wx_ref, wa_ref, bx_ref, ba_ref, lam_ref)
        u = mult * (i_t * xc)
        _scan_tile(a, u, carry_ref[7:8, :], la_ref, lh_ref, c_ref, h_ref, False)
        h = h_ref[...]
        carry_ref[...] = h[tc - 8:tc, :]
        ga = ga_ref[...]
        ya_ref[...] = (ga * _sigmoid(ga) * h).astype(ya_ref.dtype)

    row = lambda b, t: b * nt + t
    vec = pl.BlockSpec((1, CW), lambda b, c, t: (0, c))
    mat = pl.BlockSpec((1, CW, CW), lambda b, c, t: (c, 0, 0))
    return pl.pallas_call(
        body,
        name="lru_fwd",
        grid=(B, N_CT, nt),
        in_specs=[
            pl.BlockSpec((tc, CW), lambda b, c, t: (row(b, t), c)),
            pl.BlockSpec((8, CW), lambda b, c, t: (jnp.maximum(row(b, t) * h8 - 1, 0), c)),
            pl.BlockSpec((tc, CW), lambda b, c, t: (row(b, t), N_CT + c)),
            pl.BlockSpec((CONV, CW), lambda b, c, t: (0, c)),
            vec, mat, mat, vec, vec, vec,
        ],
        out_specs=[
            pl.BlockSpec((tc, CW), lambda b, c, t: (row(b, t), c)),
            pl.BlockSpec((tc, CW), lambda b, c, t: (row(b, t), c)),
        ],
        out_shape=[
            jax.ShapeDtypeStruct((T, D_MODEL), F32),
            jax.ShapeDtypeStruct((T, D_MODEL), _MXU),
        ],
        scratch_shapes=[pltpu.VMEM((tc + 8, CW), F32), pltpu.VMEM((8, CW), F32)] + _scan_scratch(tc)[:3],
        compiler_params=_params(("parallel", "parallel", "arbitrary")),
    )(proj, proj, proj, conv_w, conv_b, wx_bd, wa_bd, bx, ba, lam)


def _lru_bwd(dya, proj, hlru, conv_w, conv_b, wx_bd, wa_bd, bx, ba, lam, B, S, deps=()):
    T = B * S
    tc = min(256, S)
    nt = S // tc
    h8 = tc // 8

    def body(dya_ref, xa_ref, xhalo_ref, ga_ref, h_ref, hhalo_ref, cw_ref, cb_ref, wx_ref, wa_ref, bx_ref, ba_ref,
             lam_ref, dxa_ref, dga_ref, dcw_ref, dcb_ref, dwx_ref, dwa_ref, dbx_ref, dba_ref, dlam_ref,
             ext_ref, ext2_ref, carry_ref, dhalo_ref, la_ref, lh_ref, c_ref, dh_ref):
        b = pl.program_id(1)
        t = pl.program_id(2)
        tt = nt - 1 - t

        @pl.when(t == 0)
        def _():
            carry_ref[...] = jnp.zeros_like(carry_ref)
            dhalo_ref[...] = jnp.zeros_like(dhalo_ref)

        @pl.when((t == 0) & (b == 0))
        def _():
            for r in (dcw_ref, dcb_ref, dwx_ref, dwa_ref, dbx_ref, dba_ref, dlam_ref):
                r[...] = jnp.zeros_like(r)

        xa = xa_ref[...]
        ext_ref[0:8, :] = jnp.where(tt == 0, 0.0, xhalo_ref[...])
        ext_ref[8:8 + tc, :] = xa
        xc = _conv_from_ext(ext_ref, xa, cw_ref, cb_ref, tc)
        xcb, i_t, r_t, sp, a, mult = _lru_gates(xc, wx_ref, wa_ref, bx_ref, ba_ref, lam_ref)

        h = h_ref[...]
        ga = ga_ref[...]
        dya_t = dya_ref[...]
        sg = _sigmoid(ga)
        dga_ref[...] = (dya_t * h * (sg * (1.0 + ga * (1.0 - sg)))).astype(dga_ref.dtype)
        dlru = dya_t * (ga * sg)

        row = lax.broadcasted_iota(jnp.int32, a.shape, 0)
        coef = jnp.where(row == tc - 1, 1.0, pltpu.roll(a, tc - 1, 0))
        _scan_tile(coef, dlru, carry_ref[0:1, :], la_ref, lh_ref, c_ref, dh_ref, True)
        dh = dh_ref[...]
        ext2_ref[0:tc, :] = a * dh
        carry_ref[...] = ext2_ref[0:8, :]

        ext2_ref[0:8, :] = jnp.where(tt == 0, 0.0, hhalo_ref[...])
        ext2_ref[8:8 + tc, :] = h
        hprev = ext2_ref[7:7 + tc, :]

        da = dh * hprev
        ix = i_t * xc
        dmult = dh * ix
        di = dh * mult * xc
        dxc = dh * mult * i_t
        dlog_a = da * a - dmult * (a * a) / mult
        dr = dlog_a * ((-LRU_C) * sp)
        dlam_ref[...] += jnp.sum(dlog_a * r_t, axis=0, keepdims=True) * (LRU_C * _sigmoid(-lam_ref[...]))
        dza = dr * r_t * (1.0 - r_t)
        dzx = di * i_t * (1.0 - i_t)
        dzab = _c(dza)
        dzxb = _c(dzx)
        dxc = dxc + _dot_nt(dzxb, wx_ref[0]) + _dot_nt(dzab, wa_ref[0])
        dwx_ref[0] += _dot_tn(xcb, dzxb)
        dwa_ref[0] += _dot_tn(xcb, dzab)
        dbx_ref[...] += jnp.sum(dzx, axis=0, keepdims=True)
        dba_ref[...] += jnp.sum(dza, axis=0, keepdims=True)

        dcb_ref[...] += jnp.sum(dxc, axis=0, keepdims=True)
        dcw_ref[3:4, :] += jnp.sum(dxc * xa, axis=0, keepdims=True)
        dcw_ref[2:3, :] += jnp.sum(dxc * ext_ref[7:7 + tc, :], axis=0, keepdims=True)
        dcw_ref[1:2, :] += jnp.sum(dxc * ext_ref[6:6 + tc, :], axis=0, keepdims=True)
        dcw_ref[0:1, :] += jnp.sum(dxc * ext_ref[5:5 + tc, :], axis=0, keepdims=True)
        ext2_ref[0:tc, :] = dxc
        ext2_ref[tc:tc + 8, :] = dhalo_ref[...]
        dxa = (cw_ref[3:4, :] * dxc + cw_ref[2:3, :] * ext2_ref[1:1 + tc, :]
               + cw_ref[1:2, :] * ext2_ref[2:2 + tc, :] + cw_ref[0:1, :] * ext2_ref[3:3 + tc, :])
        dxa_ref[...] = dxa.astype(dxa_ref.dtype)
        dhalo_ref[...] = ext2_ref[0:8, :]

    row_of = lambda b, t: b * nt + (nt - 1 - t)
    tile = lambda off: pl.BlockSpec((tc, CW), lambda c, b, t: (row_of(b, t), off + c))
    halo = pl.BlockSpec((8, CW), lambda c, b, t: (jnp.maximum(row_of(b, t) * h8 - 1, 0), c))
    vec = pl.BlockSpec((1, CW), lambda c, b, t: (0, c))
    mat = pl.BlockSpec((1, CW, CW), lambda c, b, t: (c, 0, 0))
    cwspec = pl.BlockSpec((CONV, CW), lambda c, b, t: (0, c))
    return pl.pallas_call(
        _after(body, 13, deps),
        name="lru_bwd",
        grid=(N_CT, B, nt),
        in_specs=[tile(0), tile(0), halo, tile(N_CT), tile(0), halo, cwspec, vec, mat, mat, vec, vec, vec]
        + [ANY_SPEC] * len(deps),
        out_specs=[tile(0), tile(0), cwspec, vec, mat, mat, vec, vec, vec],
        out_shape=[
            jax.ShapeDtypeStruct((T, D_MODEL), _MXU),
            jax.ShapeDtypeStruct((T, D_MODEL), _MXU),
            jax.ShapeDtypeStruct((CONV, D_MODEL), F32),
            jax.ShapeDtypeStruct((1, D_MODEL), F32),
            jax.ShapeDtypeStruct((N_CT, CW, CW), F32),
            jax.ShapeDtypeStruct((N_CT, CW, CW), F32),
            jax.ShapeDtypeStruct((1, D_MODEL), F32),
            jax.ShapeDtypeStruct((1, D_MODEL), F32),
            jax.ShapeDtypeStruct((1, D_MODEL), F32),
        ],
        scratch_shapes=[pltpu.VMEM((tc + 8, CW), F32), pltpu.VMEM((tc + 8, CW), F32),
                        pltpu.VMEM((8, CW), F32), pltpu.VMEM((8, CW), F32)] + _scan_scratch(tc),
        compiler_params=_params(("parallel", "arbitrary", "arbitrary")),
    )(dya, proj, proj, proj, hlru, hlru, conv_w, conv_b, wx_bd, wa_bd, bx, ba, lam, *deps)


def _retention_tables(S):
    half = DK // 2
    freqs = ROPE_THETA ** (-jnp.arange(half, dtype=F32) / half)
    ang = jnp.arange(S, dtype=F32)[:, None] * freqs[None, :]
    log_g = jnp.log1p(-(2.0 ** (-5.0 - jnp.arange(HEADS, dtype=F32))))
    idx = jnp.arange(CHUNK, dtype=F32)
    diff = idx[:, None] - idx[None, :]
    inner = jnp.where(diff >= 0, jnp.exp(jnp.maximum(diff, 0.0)[None] * log_g[:, None, None]), 0.0)
    cross = jnp.exp((idx[None, :] + 1.0) * log_g[:, None])[:, :, None]
    state = jnp.exp((CHUNK - 1.0 - idx[None, :]) * log_g[:, None])[:, :, None]
    gam = jnp.broadcast_to(jnp.exp(CHUNK * log_g)[:, None, None], (HEADS, 1, DK))
    return jnp.cos(ang), jnp.sin(ang), inner, cross, state, gam


def _rot(x, cos, sin):
    half = DK // 2
    x1, x2 = x[:, :half], x[:, half:]
    return jnp.concatenate([x1 * cos - x2 * sin, x1 * sin + x2 * cos], axis=-1)


def _rot_t(y, cos, sin):
    half = DK // 2
    y1, y2 = y[:, :half], y[:, half:]
    return jnp.concatenate([y1 * cos + y2 * sin, y2 * cos - y1 * sin], axis=-1)


def _groupnorm(o):
    mu = jnp.mean(o, axis=-1, keepdims=True)
    oc = o - mu
    rs = lax.rsqrt(jnp.mean(oc * oc, axis=-1, keepdims=True) + EPS)
    return oc * rs, rs


def _ret_specs(B, chunk_of):
    qkv = lambda g: pl.BlockSpec((B, CHUNK, D_MODEL), lambda c: (0, chunk_of(c), g))
    act = pl.BlockSpec((B, CHUNK, D_MODEL), lambda c: (0, chunk_of(c), 0))
    rope = pl.BlockSpec((CHUNK, DK // 2), lambda c: (chunk_of(c), 0))
    dmat = pl.BlockSpec((HEADS, CHUNK, CHUNK), lambda c: (0, 0, 0))
    dvec = pl.BlockSpec((HEADS, CHUNK, 1), lambda c: (0, 0, 0))
    hrow = pl.BlockSpec((HEADS, 1, DK), lambda c: (0, 0, 0))
    rst = pl.BlockSpec((1, B, HEADS, DK, DK), lambda c: (chunk_of(c), 0, 0, 0, 0))
    return qkv, act, rope, dmat, dvec, hrow, rst


def _ret_fwd(proj, tables, gain3, B, S):
    T = B * S
    nc = S // CHUNK
    cos, sin, dmat_t, cd_t, sd_t, gam_t = tables

    def body(q_ref, k_ref, v_ref, gb_ref, cos_ref, sin_ref, dm_ref, cd_ref, sd_ref, gam_ref, gain_ref,
             o_ref, yb_ref, rs_ref, state_ref):
        @pl.when(pl.program_id(0) == 0)
        def _():
            state_ref[...] = jnp.zeros_like(state_ref)

        cos_t, sin_t = cos_ref[...], sin_ref[...]
        for b, h in [(b, h) for b in range(B) for h in range(HEADS)]:
            cols = slice(h * DK, (h + 1) * DK)
            qb = _c(_rot(q_ref[b, :, cols], cos_t, sin_t))
            kb = _c(_rot(k_ref[b, :, cols], cos_t, sin_t) * (DK ** -0.5))
            v = v_ref[b, :, cols]
            state = state_ref[b, h]
            sb = _c(state)
            rs_ref[0, b, h] = sb
            scores = _dot_nt(qb, kb) * dm_ref[h]
            o = _dot(_c(scores), _c(v)) + _dot(qb, sb) * cd_ref[h]
            state_ref[b, h] = gam_ref[h] * state + _dot_tn(kb, _c(v * sd_ref[h]))
            o_ref[b, :, cols] = o
            n, _ = _groupnorm(o)
            gb = gb_ref[b, :, cols]
            yb_ref[b, :, cols] = (gb * _sigmoid(gb) * (n * gain_ref[h])).astype(yb_ref.dtype)

    qkv, act, rope, dmat, dvec, hrow, rst = _ret_specs(B, lambda c: c)
    proj3 = proj.reshape(B, S, proj.shape[1])
    o_pre, yb, states = pl.pallas_call(
        body,
        name="ret_fwd",
        grid=(nc,),
        in_specs=[qkv(2), qkv(3), qkv(4), qkv(5), rope, rope, dmat, dvec, dvec, hrow, hrow],
        out_specs=[act, act, rst],
        out_shape=[
            jax.ShapeDtypeStruct((B, S, D_MODEL), F32),
            jax.ShapeDtypeStruct((B, S, D_MODEL), _MXU),
            jax.ShapeDtypeStruct((nc, B, HEADS, DK, DK), _MXU),
        ],
        scratch_shapes=[pltpu.VMEM((B, HEADS, DK, DK), F32)],
        compiler_params=_params(("arbitrary",)),
    )(proj3, proj3, proj3, proj3, cos, sin, dmat_t, cd_t, sd_t, gam_t, gain3)
    return o_pre.reshape(T, D_MODEL), yb.reshape(T, D_MODEL), states


def _ret_bwd(dyb, o_pre, proj, states, tables, gain3, B, S, deps=()):
    T = B * S
    nc = S // CHUNK
    cos, sin, dmat_t, cd_t, sd_t, gam_t = tables

    def body(dyb_ref, o_ref, q_ref, k_ref, v_ref, gb_ref, rs_ref, cos_ref, sin_ref, dm_ref, cd_ref, sd_ref, gam_ref,
             gain_ref, dr_ref, dgain_ref, dstate_ref):
        @pl.when(pl.program_id(0) == 0)
        def _():
            dstate_ref[...] = jnp.zeros_like(dstate_ref)
            dgain_ref[...] = jnp.zeros_like(dgain_ref)

        cos_t, sin_t = cos_ref[...], sin_ref[...]
        for b, h in [(b, h) for b in range(B) for h in range(HEADS)]:
            cols = slice(h * DK, (h + 1) * DK)
            gain = gain_ref[h]
            n, rs = _groupnorm(o_ref[b, :, cols])
            gb = gb_ref[b, :, cols]
            sg = _sigmoid(gb)
            dy = dyb_ref[b, :, cols]
            part = lambda g: slice(g * D_MODEL + h * DK, g * D_MODEL + (h + 1) * DK)
            dr_ref[b, :, part(3)] = (dy * (n * gain) * (sg * (1.0 + gb * (1.0 - sg)))).astype(dr_ref.dtype)
            dgn = dy * (gb * sg)
            dgain_ref[h] += jnp.sum(dgn * n, axis=0, keepdims=True)
            dn = dgn * gain
            do = rs * (dn - jnp.mean(dn, axis=-1, keepdims=True) - n * jnp.mean(dn * n, axis=-1, keepdims=True))

            qb = _c(_rot(q_ref[b, :, cols], cos_t, sin_t))
            kb = _c(_rot(k_ref[b, :, cols], cos_t, sin_t) * (DK ** -0.5))
            v = v_ref[b, :, cols]
            vb = _c(v)
            vsb = _c(v * sd_ref[h])
            dob = _c(do)
            docb = _c(do * cd_ref[h])
            dmat = dm_ref[h]
            dstate = dstate_ref[b, h]
            dsb = _c(dstate)
            pb = _c(_dot_nt(qb, kb) * dmat)
            dsc = _c(_dot_nt(dob, vb) * dmat)
            dq = _dot(dsc, kb) + _dot_nt(docb, rs_ref[0, b, h])
            dk = _dot_tn(dsc, qb) + _dot_nt(vsb, dsb)
            dv = _dot_tn(pb, dob) + _dot(kb, dsb) * sd_ref[h]
            dstate_ref[b, h] = gam_ref[h] * dstate + _dot_tn(qb, docb)
            dr_ref[b, :, part(0)] = _rot_t(dq, cos_t, sin_t).astype(dr_ref.dtype)
            dr_ref[b, :, part(1)] = (_rot_t(dk, cos_t, sin_t) * (DK ** -0.5)).astype(dr_ref.dtype)
            dr_ref[b, :, part(2)] = dv.astype(dr_ref.dtype)

    qkv, act, rope, dmat, dvec, hrow, rst = _ret_specs(B, lambda c: nc - 1 - c)
    wide = pl.BlockSpec((B, CHUNK, 4 * D_MODEL), lambda c: (0, nc - 1 - c, 0))
    proj3 = proj.reshape(B, S, proj.shape[1])
    dr, dgain = pl.pallas_call(
        _after(body, 14, deps),
        name="ret_bwd",
        grid=(nc,),
        in_specs=[act, act, qkv(2), qkv(3), qkv(4), qkv(5), rst, rope, rope, dmat, dvec, dvec, hrow, hrow]
        + [ANY_SPEC] * len(deps),
        out_specs=[wide, hrow],
        out_shape=[jax.ShapeDtypeStruct((B, S, 4 * D_MODEL), _MXU), jax.ShapeDtypeStruct((HEADS, 1, DK), F32)],
        scratch_shapes=[pltpu.VMEM((B, HEADS, DK, DK), F32)],
        compiler_params=_params(("arbitrary",)),
    )(dyb.reshape(B, S, D_MODEL), o_pre.reshape(B, S, D_MODEL), proj3, proj3, proj3, proj3, states, cos, sin, dmat_t,
      cd_t, sd_t, gam_t, gain3, *deps)
    return dr.reshape(T, 4 * D_MODEL), dgain


def _mid(ya, yb, proj, x2d, tgt2d, wpa, wpb, wout, g_fin):
    T = x2d.shape[0]
    tm = min(256, T)
    n_steps = T // tm
    rows = D_MODEL // (2 * N_CHIPS)

    def body(ya_ref, yb_ref, ma_ref, mb_ref, x_ref, t_ref, gf_ref, wpa_hbm, wpb_hbm, wout_hbm,
             loss_ref, dx2_ref, dya_ref, dyb_ref, dm_ref, dgf_ref, gw_hbm, w_ref, acc_ref, sem):
        i = pl.program_id(0)

        @pl.when(i == 0)
        def _():
            loads = [pltpu.make_async_copy(src, w_ref.at[k], sem.at[k]) for k, src in enumerate((wpa_hbm, wpb_hbm, wout_hbm))]
            for cp in loads:
                cp.start()
            for cp in loads:
                cp.wait()
            acc_ref[...] = jnp.zeros_like(acc_ref)
            loss_ref[...] = jnp.zeros_like(loss_ref)
            dgf_ref[...] = jnp.zeros_like(dgf_ref)

        ya_t, yb_t = ya_ref[...], yb_ref[...]
        out_a = _dot(ya_t, w_ref[0])
        out_b = _dot(yb_t, w_ref[1])
        sa = _sigmoid(ma_ref[...])
        sb = _sigmoid(mb_ref[...])
        mgb = _c(sa * out_a + sb * out_b)
        x2 = x_ref[...] + _dot(mgb, w_ref[2])
        r2 = lax.rsqrt(jnp.mean(x2 * x2, axis=-1, keepdims=True) + EPS)
        nx = x2 * r2
        gf = gf_ref[...]
        err = nx * gf - t_ref[...]
        loss_ref[...] += 0.5 * jnp.sum(jnp.mean(err * err, axis=-1, keepdims=True), axis=0, keepdims=True)
        dy = err * (1.0 / D_MODEL)
        dgf_ref[...] += jnp.sum(dy * nx, axis=0, keepdims=True)
        dyg = dy * gf
        dx2 = r2 * (dyg - nx * jnp.mean(dyg * nx, axis=-1, keepdims=True))
        dx2_ref[...] = dx2
        dx2b = _c(dx2)
        dmg = _dot_nt(dx2b, w_ref[2])
        acc_ref[2] += _dot_tn(mgb, dx2b)
        dm_ref[:, :D_MODEL] = (dmg * out_a * sa * (1.0 - sa)).astype(dm_ref.dtype)
        dm_ref[:, D_MODEL:] = (dmg * out_b * sb * (1.0 - sb)).astype(dm_ref.dtype)
        dab = _c(dmg * sa)
        dbb = _c(dmg * sb)
        dya_ref[...] = _dot_nt(dab, w_ref[0])
        dyb_ref[...] = _dot_nt(dbb, w_ref[1])
        acc_ref[0] += _dot_tn(ya_t, dab)
        acc_ref[1] += _dot_tn(yb_t, dbb)

        @pl.when(i == n_steps - 1)
        def _():
            copies = [pltpu.make_async_copy(acc_ref.at[k, pl.ds((2 * p + hf) * rows, rows), :], gw_hbm.at[p, hf, k],
                                            sem.at[(k * N_CHIPS + p) * 2 + hf])
                      for k in range(3) for p in range(N_CHIPS) for hf in range(2)]
            for cp in copies:
                cp.start()
            for cp in copies:
                cp.wait()

    tile = lambda j: pl.BlockSpec((tm, D_MODEL), lambda i: (i, j))
    one = pl.BlockSpec((1, D_MODEL), lambda i: (0, 0))
    anyspec = pl.BlockSpec(memory_space=pl.ANY)
    return pl.pallas_call(
        body,
        name="mid",
        grid=(n_steps,),
        in_specs=[tile(0), tile(0), tile(6), tile(7), tile(0), tile(0), one, anyspec, anyspec, anyspec],
        out_specs=[pl.BlockSpec((1, 1), lambda i: (0, 0)), tile(0), tile(0), tile(0),
                   pl.BlockSpec((tm, 2 * D_MODEL), lambda i: (i, 0)), one, anyspec],
        out_shape=[
            jax.ShapeDtypeStruct((1, 1), F32),
            jax.ShapeDtypeStruct((T, D_MODEL), F32),
            jax.ShapeDtypeStruct((T, D_MODEL), F32),
            jax.ShapeDtypeStruct((T, D_MODEL), F32),
            jax.ShapeDtypeStruct((T, 2 * D_MODEL), _MXU),
            jax.ShapeDtypeStruct((1, D_MODEL), F32),
            jax.ShapeDtypeStruct((N_CHIPS, 2, 3, rows, D_MODEL), F32),
        ],
        scratch_shapes=[pltpu.VMEM((3, D_MODEL, D_MODEL), _MXU), pltpu.VMEM((3, D_MODEL, D_MODEL), F32),
                        pltpu.SemaphoreType.DMA((3 * N_CHIPS * 2,))],
        compiler_params=_params(("arbitrary",)),
    )(ya, yb, proj, proj, x2d, tgt2d, g_fin, wpa, wpb, wout)


DX_TILE = 512


def _inproj_bwd_dx(dparts, w_all, x2d, dx2, g_in, first, count, prev, name, deps=()):
    T = x2d.shape[0]
    tm = min(DX_TILE, T)
    n_d = len(dparts)
    groups = [(a, k) for a, d in enumerate(dparts) for k in range(d.shape[1] // D_MODEL)]
    dg_start = jnp.zeros((1, D_MODEL), F32) if prev is None else prev[1]
    carried = () if prev is None else (prev[0],)

    def body(*refs):
        d_refs = refs[:n_d]
        x_ref, dx2_ref, g_ref, dg0_ref, w_hbm = refs[n_d:n_d + 5]
        dx_ref, dg_ref, w_ref, sem = refs[-4:]

        @pl.when(pl.program_id(0) == 0)
        def _():
            cp = pltpu.make_async_copy(w_hbm, w_ref, sem)
            cp.start()
            cp.wait()
            dg_ref[...] = dg0_ref[...]

        dh = jnp.zeros((tm, D_MODEL), F32)
        for j, (a, k) in enumerate(groups):
            dh = dh + _dot_nt(d_refs[a][:, k * D_MODEL:(k + 1) * D_MODEL],
                              w_ref[j // 2, :, (j % 2) * D_MODEL:(j % 2 + 1) * D_MODEL])
        x = x_ref[...]
        r = lax.rsqrt(jnp.mean(x * x, axis=-1, keepdims=True) + EPS)
        nx = x * r
        dg_ref[...] += jnp.sum(dh * nx, axis=0, keepdims=True)
        dhg = dh * g_ref[...]
        dx_ref[...] = dx2_ref[...] + r * (dhg - nx * jnp.mean(dhg * nx, axis=-1, keepdims=True))

    tile = pl.BlockSpec((tm, D_MODEL), lambda i: (first + i, 0))
    one = pl.BlockSpec((1, D_MODEL), lambda i: (0, 0))
    return pl.pallas_call(
        body,
        name=name,
        grid=(count,),
        in_specs=[pl.BlockSpec((tm, d.shape[1]), lambda i: (first + i, 0)) for d in dparts]
        + [tile, tile, one, one, ANY_SPEC] + [ANY_SPEC] * (len(carried) + len(deps)),
        out_specs=[tile, one],
        out_shape=[jax.ShapeDtypeStruct((T, D_MODEL), F32), jax.ShapeDtypeStruct((1, D_MODEL), F32)],
        input_output_aliases={n_d + 5: 0} if carried else {},
        scratch_shapes=[pltpu.VMEM(w_all.shape, w_all.dtype), pltpu.SemaphoreType.DMA],
        compiler_params=_params(("arbitrary",)),
    )(*dparts, x2d, dx2, g_in, dg_start, w_all, *carried, *deps)


def _inproj_bwd_dw(ht, dparts, name, deps=()):
    T = ht.shape[1]
    tn = 512
    half = D_MODEL // 2
    per_chip = 2 * D_MODEL // tn
    n_d = len(dparts)
    tiles = [(a, t) for a, d in enumerate(dparts) for t in range(d.shape[1] // tn)]
    offs = [sum(d.shape[1] // tn for d in dparts[:a]) for a in range(n_d)]

    def body(*refs):
        ht_ref = refs[0]
        d_refs = refs[1:1 + n_d]
        out_ref = refs[-1]
        t = pl.program_id(0)

        for a in range(n_d):
            lo, hi = offs[a], offs[a] + dparts[a].shape[1] // tn

            @pl.when((t >= lo) & (t < hi))
            def _(a=a):
                g = _dot(ht_ref[...], d_refs[a][...])
                out_ref[0, 0] = g[:half]
                out_ref[0, 1] = g[half:]

    def dspec(a):
        n_a = dparts[a].shape[1] // tn
        return pl.BlockSpec((T, tn), lambda t: (0, jnp.clip(t - offs[a], 0, n_a - 1)))

    return pl.pallas_call(
        body,
        name=name,
        grid=(len(tiles),),
        in_specs=[pl.BlockSpec((D_MODEL, T), lambda t: (0, 0))] + [dspec(a) for a in range(n_d)]
        + [ANY_SPEC] * len(deps),
        out_specs=pl.BlockSpec((1, 2, half, tn), lambda t: (t // per_chip, 0, 0, t % per_chip)),
        out_shape=jax.ShapeDtypeStruct((len(tiles) // per_chip, 2, half, 2 * D_MODEL), F32),
        compiler_params=_params(("parallel",)),
    )(ht, *dparts, *deps)


def _coords():
    return lax.axis_index("x"), lax.axis_index("y"), lax.axis_index("c")


def _other_chips(x, y):
    return [(1 - x, y), (x, 1 - y), (1 - x, 1 - y)]


def _all_gather8(xs, name, deps=()):
    m_per, n = xs.shape

    def body(x_ref, *rest):
        out_ref, send_sems, recv_sems, local_sem = rest[-4:]
        x, y, c = _coords()
        me, sibling = (x, y, c), (x, y, 1 - c)
        chips = _other_chips(x, y)

        def rows(px, py, pc):
            return out_ref.at[pl.ds((4 * px + 2 * py + pc) * m_per, m_per), :]

        def copy(k, block, to, src=None):
            return pltpu.make_async_remote_copy(
                src_ref=rows(*block) if src is None else src, dst_ref=rows(*block),
                send_sem=send_sems.at[k], recv_sem=recv_sems.at[k], device_id=to, device_id_type=MESH)

        mine = pltpu.make_async_copy(x_ref, rows(*me), local_sem)
        mine.start()
        first = [copy(0, me, sibling, src=x_ref)]
        first += [copy(1 + j, me, (*chip, c), src=x_ref) for j, chip in enumerate(chips)]
        for cp in first:
            cp.start()
        passed = [copy(4 + j, (*chip, c), sibling) for j, chip in enumerate(chips)]
        for j, chip in enumerate(chips):
            copy(1 + j, (*chip, c), me).wait_recv()
            passed[j].start()
        copy(0, sibling, me).wait_recv()
        for j, chip in enumerate(chips):
            copy(4 + j, (*chip, 1 - c), me).wait_recv()
        for cp in first + passed:
            cp.wait_send()
        mine.wait()

    return pl.pallas_call(
        body,
        name=name,
        out_shape=jax.ShapeDtypeStruct((8 * m_per, n), xs.dtype),
        in_specs=[pl.BlockSpec(memory_space=pltpu.VMEM)] + [ANY_SPEC] * len(deps),
        out_specs=pl.BlockSpec(memory_space=pltpu.VMEM),
        scratch_shapes=[pltpu.SemaphoreType.DMA((7,)), pltpu.SemaphoreType.DMA((7,)), pltpu.SemaphoreType.DMA],
        compiler_params=pltpu.CompilerParams(vmem_limit_bytes=VMEM_LIMIT),
    )(xs, *deps)


def _chunks(rows, n):
    size = rows // n
    return [pl.ds(q * size, size) for q in range(n)]


HBM_SPEC = pl.BlockSpec(memory_space=pltpu.HBM)
SEM_SPEC = pl.BlockSpec(memory_space=pltpu.SEMAPHORE)
DATAFLOW = pltpu.SideEffectType.DATAFLOW_SIDE_EFFECTING


def _copies_start(bufs, plan, n_copies, name):
    n = len(bufs)

    def body(*refs):
        ins = refs[:n]
        send_sems, recv_sems = refs[n], refs[n + 1]
        token = refs[-1]
        for k, send, _ in plan(ins):
            if send is not None:
                src, dst, dev, pred = send
                cp = pltpu.make_async_remote_copy(src_ref=src, dst_ref=dst, send_sem=send_sems.at[k],
                                                  recv_sem=recv_sems.at[k], device_id=dev, device_id_type=MESH)
                if pred is None:
                    cp.start()
                else:
                    pl.when(pred)(cp.start)
        token[...] = jnp.zeros_like(token)

    hbm = [pltpu.with_memory_space_constraint(b, pltpu.HBM) for b in bufs]
    outs = pl.pallas_call(
        body,
        name=name,
        in_specs=[HBM_SPEC] * n,
        out_specs=(SEM_SPEC, SEM_SPEC, *([HBM_SPEC] * n), pl.BlockSpec(memory_space=pltpu.VMEM)),
        out_shape=(pltpu.SemaphoreType.DMA((n_copies,)), pltpu.SemaphoreType.DMA((n_copies,)),
                   *[pltpu.HBM(b.shape, b.dtype) for b in bufs], jax.ShapeDtypeStruct((8, 128), F32)),
        input_output_aliases={a: 2 + a for a in range(n)},
        compiler_params=pltpu.CompilerParams(has_side_effects=DATAFLOW),
    )(*hbm)
    return outs[0], outs[1], list(outs[2:2 + n]), outs[-1]


def _copies_wait(send_sems, recv_sems, bufs, after, plan, name):
    n = len(bufs)

    def body(*refs):
        ins = refs[:n]
        s_sems, r_sems = refs[n], refs[n + 1]
        for k, send, recv in plan(ins):
            if send is not None:
                src, dst, dev, pred = send
                cp = pltpu.make_async_remote_copy(src_ref=src, dst_ref=dst, send_sem=s_sems.at[k],
                                                  recv_sem=r_sems.at[k], device_id=dev, device_id_type=MESH)
                if pred is None:
                    cp.wait_send()
                else:
                    pl.when(pred)(cp.wait_send)
            if recv is not None:
                dst, pred = recv
                cp = pltpu.make_async_remote_copy(src_ref=dst, dst_ref=dst, send_sem=s_sems.at[k],
                                                  recv_sem=r_sems.at[k], device_id=_coords(), device_id_type=MESH)
                if pred is None:
                    cp.wait_recv()
                else:
                    pl.when(pred)(cp.wait_recv)

    outs = pl.pallas_call(
        body,
        name=name,
        in_specs=[HBM_SPEC] * n + [SEM_SPEC, SEM_SPEC, pl.BlockSpec(memory_space=pl.ANY)],
        out_specs=[HBM_SPEC] * n,
        out_shape=[pltpu.HBM(b.shape, b.dtype) for b in bufs],
        input_output_aliases={a: a for a in range(n)},
        compiler_params=pltpu.CompilerParams(has_side_effects=DATAFLOW),
    )(*bufs, send_sems, recv_sems, after)
    return list(outs)


def _gather_plan(n_bufs):
    def plan(refs):
        x, y, c = _coords()
        me = 2 * x + y
        out = []
        for k, (px, py) in enumerate(_other_chips(x, y)):
            for a in range(n_bufs):
                out.append((k * n_bufs + a, (refs[a].at[me], refs[a].at[me], (px, py, c), None),
                            (refs[a].at[2 * px + py], None)))
        return out
    return plan


def _cast_into_slot(ws, name):
    n = len(ws)
    nt = 2

    def body(s_ref, *refs):
        for a in range(n):
            refs[n + a][0] = refs[a][...].astype(refs[n + a].dtype)

    xi, yi, _ = _coords()
    return pl.pallas_call(
        body,
        name=name,
        grid_spec=pltpu.PrefetchScalarGridSpec(
            num_scalar_prefetch=1,
            grid=(2, nt),
            in_specs=[pl.BlockSpec((1, w.shape[1] // nt, w.shape[2]), lambda hf, i, s: (hf, i, 0)) for w in ws],
            out_specs=[pl.BlockSpec((1, 1, w.shape[1] // nt, w.shape[2]), lambda hf, i, s: (s[0], hf, i, 0)) for w in ws],
        ),
        out_shape=[jax.ShapeDtypeStruct((N_CHIPS,) + w.shape, _MXU) for w in ws],
        compiler_params=_params(("parallel", "parallel")),
    )((2 * xi + yi).reshape(1).astype(jnp.int32), *ws)


def _gather_chips(bufs, n_chunks, name):
    n = len(bufs)
    pieces = [(a, rows) for a in range(n) for rows in _chunks(bufs[a].shape[2], n_chunks[a])]
    n_p = len(pieces)

    def body(*refs):
        outs = refs[n:2 * n]
        send_sems, recv_sems, fsend_sems, frecv_sems = refs[2 * n:]
        x, y, c = _coords()
        me = 2 * x + y
        chips = _other_chips(x, y)

        def send(k, i, slot, chip):
            a, rows = pieces[i]
            return pltpu.make_async_remote_copy(
                src_ref=outs[a].at[slot, c, rows], dst_ref=outs[a].at[slot, c, rows], send_sem=send_sems.at[k * n_p + i],
                recv_sem=recv_sems.at[k * n_p + i], device_id=(*chip, c), device_id_type=MESH)

        def forward(k, i, slot, half):
            a, rows = pieces[i]
            return pltpu.make_async_remote_copy(
                src_ref=outs[a].at[slot, half, rows], dst_ref=outs[a].at[slot, half, rows],
                send_sem=fsend_sems.at[k * n_p + i], recv_sem=frecv_sems.at[k * n_p + i],
                device_id=(x, y, 1 - c), device_id_type=MESH)

        sends = [send(k, i, me, chip) for i in range(n_p) for k, chip in enumerate(chips)]
        for cp in sends:
            cp.start()
        forwards = []
        for i in range(n_p):
            for k, (px, py) in enumerate(chips):
                send(k, i, 2 * px + py, (px, py)).wait_recv()
                fw = forward(k, i, 2 * px + py, c)
                fw.start()
                forwards.append(fw)
        for i in range(n_p):
            for k, (px, py) in enumerate(chips):
                forward(k, i, 2 * px + py, 1 - c).wait_recv()
        for cp in sends + forwards:
            cp.wait_send()

    anyspec = pl.BlockSpec(memory_space=pl.ANY)
    sems = pltpu.SemaphoreType.DMA((3 * n_p,))
    return pl.pallas_call(
        body,
        name=name,
        in_specs=[anyspec] * n,
        out_specs=[anyspec] * n,
        out_shape=[jax.ShapeDtypeStruct(b.shape, b.dtype) for b in bufs],
        input_output_aliases={a: a for a in range(n)},
        scratch_shapes=[sems, sems, sems, sems],
    )(*bufs)


def _swap_plan(n_slabs):
    def plan(refs):
        x, y, c = _coords()
        out, k = [], 0
        for i, n in enumerate(n_slabs):
            g, land = refs[2 * i], refs[2 * i + 1]
            for p in range(n):
                out.append((k, (g.at[p, 1 - c], land.at[p], (x, y, 1 - c), None), (land.at[p], None)))
                k += 1
        return out
    return plan


def _is_one_of(chip, dests):
    hit = chip == dests[0]
    for d in dests[1:]:
        hit = hit | (chip == d)
    return hit


def _slab_of(chip, dests):
    return sum(j * (chip == d).astype(jnp.int32) for j, d in enumerate(dests))


def _scatter_plan(dest_sets):
    def plan(refs):
        x, y, c = _coords()
        me = 2 * x + y
        out = []
        for k, (px, py) in enumerate(_other_chips(x, y)):
            peer = 2 * px + py
            for i, dests in enumerate(dest_sets):
                cs, land = refs[2 * i], refs[2 * i + 1]
                everyone = len(dests) == N_CHIPS
                send = (cs.at[_slab_of(peer, dests)], land.at[k], (px, py, c),
                        None if everyone else _is_one_of(peer, dests))
                recv = (land.at[k], None if everyone else _is_one_of(me, dests))
                out.append((k * len(dest_sets) + i, send, recv))
        return out
    return plan


def _allgather_plan():
    def plan(refs):
        x, y, c = _coords()
        src, land = refs
        me = 4 * x + 2 * y + c
        out = []
        for r in range(1, 8):
            px = 1 - x if r & 4 else x
            py = 1 - y if r & 2 else y
            pc = 1 - c if r & 1 else c
            out.append((r - 1, (src, land.at[me], (px, py, pc), None), (land.at[4 * px + 2 * py + pc], None)))
        return out
    return plan


def _sum_gathered(own, land, name):
    def body(own_ref, land_ref, o_ref):
        x, y, c = _coords()
        me = 4 * x + 2 * y + c
        acc = jnp.zeros(own_ref.shape, F32)
        for d in range(8):
            acc = acc + (land_ref[d] + jnp.where(me == d, own_ref[...], 0.0))
        o_ref[...] = acc

    return pl.pallas_call(
        body,
        name=name,
        out_shape=jax.ShapeDtypeStruct(own.shape, F32),
        compiler_params=_params(),
    )(own, land)


def _join_halves(bufs, n_chunks, name):
    n = len(bufs)
    pieces = [(a, rows) for a in range(n) for rows in _chunks(bufs[a].shape[1], n_chunks[a])]
    n_p = len(pieces)

    def body(*refs):
        outs = refs[n:2 * n]
        send_sems, recv_sems = refs[2 * n:]
        x, y, c = _coords()

        def copy(i, half):
            a, rows = pieces[i]
            return pltpu.make_async_remote_copy(
                src_ref=outs[a].at[half, rows], dst_ref=outs[a].at[half, rows], send_sem=send_sems.at[i],
                recv_sem=recv_sems.at[i], device_id=(x, y, 1 - c), device_id_type=MESH)

        sends = [copy(i, c) for i in range(n_p)]
        for cp in sends:
            cp.start()
        for i in range(n_p):
            copy(i, 1 - c).wait_recv()
        for cp in sends:
            cp.wait_send()

    anyspec = pl.BlockSpec(memory_space=pl.ANY)
    sems = pltpu.SemaphoreType.DMA((n_p,))
    return pl.pallas_call(
        body,
        name=name,
        in_specs=[anyspec] * n,
        out_specs=[anyspec] * n,
        out_shape=[jax.ShapeDtypeStruct(b.shape, b.dtype) for b in bufs],
        input_output_aliases={a: a for a in range(n)},
        scratch_shapes=[sems, sems],
    )(*bufs)


def _row_tile(rows, cap):
    t = cap
    while rows % t:
        t //= 2
    return t


def _add_my_half(g, r, name):
    n_slabs, _, R, C = g.shape
    tr = _row_tile(R, 256)

    def body(c_ref, g_ref, r_ref, o_ref):
        o_ref[...] = (g_ref[0] + r_ref[...]).astype(o_ref.dtype)

    return pl.pallas_call(
        body,
        name=name,
        grid_spec=pltpu.PrefetchScalarGridSpec(
            num_scalar_prefetch=1,
            grid=(n_slabs, R // tr),
            in_specs=[pl.BlockSpec((1, 1, tr, C), lambda p, i, c_ref: (p, c_ref[0], i, 0)),
                      pl.BlockSpec((1, tr, C), lambda p, i, c_ref: (p, i, 0))],
            out_specs=pl.BlockSpec((1, tr, C), lambda p, i, c_ref: (p, i, 0)),
        ),
        out_shape=jax.ShapeDtypeStruct(r.shape, jnp.bfloat16),
        compiler_params=_params(("parallel", "parallel")),
    )(lax.axis_index("c").reshape(1).astype(jnp.int32), g, r)


def _sum_slabs(own, got, name):
    _, R, C = own.shape
    tr = _row_tile(R, 256)

    def body(s_ref, own_ref, got_ref, o_ref):
        o_ref[0] = ((own_ref[0].astype(F32) + got_ref[0].astype(F32)) + got_ref[1].astype(F32)) + got_ref[2].astype(F32)

    xi, yi, ci = _coords()
    return pl.pallas_call(
        body,
        name=name,
        grid_spec=pltpu.PrefetchScalarGridSpec(
            num_scalar_prefetch=1,
            grid=(R // tr,),
            in_specs=[pl.BlockSpec((1, tr, C), lambda i, s: (s[0], i, 0)),
                      pl.BlockSpec((3, tr, C), lambda i, s: (0, i, 0))],
            out_specs=pl.BlockSpec((1, tr, C), lambda i, s: (s[1], i, 0)),
        ),
        out_shape=jax.ShapeDtypeStruct((2, R, C), F32),
        compiler_params=_params(("parallel",)),
    )(jnp.stack([2 * xi + yi, ci]).astype(jnp.int32), own, got)


def _sum_parts(owns, gots, dest_sets, name):
    n = len(owns)
    _, R, C = owns[0].shape
    tr = _row_tile(R, 256)

    def body(s_ref, *refs):
        o_ref = refs[-1]
        total = jnp.zeros((tr, C), F32)
        for i in range(n):
            total = total + jnp.where(s_ref[2 + 2 * i] == 1, refs[i][0].astype(F32), 0.0)
        for i in range(n):
            got = refs[n + i]
            total = ((total + got[0].astype(F32)) + got[1].astype(F32)) + got[2].astype(F32)
        o_ref[0] = total

    xi, yi, ci = _coords()
    me = 2 * xi + yi
    scalars = [ci, ci]
    for dests in dest_sets:
        scalars += [_is_one_of(me, dests).astype(jnp.int32), _slab_of(me, dests)]
    own_spec = lambda i: pl.BlockSpec((1, tr, C), lambda r, s: (s[3 + 2 * i], r, 0))
    return pl.pallas_call(
        body,
        name=name,
        grid_spec=pltpu.PrefetchScalarGridSpec(
            num_scalar_prefetch=1,
            grid=(R // tr,),
            in_specs=[own_spec(i) for i in range(n)] + [pl.BlockSpec((3, tr, C), lambda r, s: (0, r, 0))] * n,
            out_specs=pl.BlockSpec((1, tr, C), lambda r, s: (s[0], r, 0)),
        ),
        out_shape=jax.ShapeDtypeStruct((2, R, C), F32),
        compiler_params=_params(("parallel",)),
    )(jnp.stack(scalars).astype(jnp.int32), *owns, *gots)


def _sum_rows8(g, m_per, name):
    n = g.shape[1]

    def body(g_ref, o_ref):
        acc = g_ref[0:m_per, :]
        for k in range(1, 8):
            acc = acc + g_ref[k * m_per:(k + 1) * m_per, :]
        o_ref[...] = acc

    return pl.pallas_call(
        body,
        name=name,
        out_shape=jax.ShapeDtypeStruct((m_per, n), F32),
        compiler_params=_params(),
    )(g)


def _adamw_math(w, g, m, v):
    m = ADAM_B1 * m + (1.0 - ADAM_B1) * g
    v = ADAM_B2 * v + (1.0 - ADAM_B2) * (g * g)
    m_hat = m / (1.0 - ADAM_B1 ** ADAM_STEP)
    v_hat = v / (1.0 - ADAM_B2 ** ADAM_STEP)
    delta = -ADAM_LR * (m_hat / (jnp.sqrt(v_hat) + ADAM_EPS) + ADAM_WD * w)
    return delta, m, v


def _adamw_big(w, g, m, v, name):
    R, C = w.shape
    tr = min(128, R)

    def body(w_ref, g_ref, m_ref, v_ref, d_out, m_out, v_out):
        d, mn, vn = _adamw_math(w_ref[...], g_ref[...], m_ref[...], v_ref[...])
        d_out[...] = d
        m_out[...] = mn
        v_out[...] = vn

    spec = pl.BlockSpec((tr, C), lambda i: (i, 0))
    return pl.pallas_call(
        body,
        name=name,
        grid=(R // tr,),
        in_specs=[spec] * 4,
        out_specs=[spec] * 3,
        out_shape=[jax.ShapeDtypeStruct((R, C), F32)] * 3,
        compiler_params=_params(("parallel",)),
    )(w, g, m, v)


def _adamw_small(ws, gs, ms, vs, name):
    n = len(ws)

    def body(*refs):
        for a in range(n):
            d, mn, vn = _adamw_math(refs[a][...], refs[n + a][...], refs[2 * n + a][...], refs[3 * n + a][...])
            refs[4 * n + a][...] = d
            refs[5 * n + a][...] = mn
            refs[6 * n + a][...] = vn

    shapes = [jax.ShapeDtypeStruct(w.shape, F32) for w in ws]
    outs = pl.pallas_call(
        body,
        name=name,
        out_shape=shapes * 3,
        compiler_params=_params(),
    )(*ws, *gs, *ms, *vs)
    return outs[:n], outs[n:2 * n], outs[2 * n:]


def _to_blockdiag(w):
    per = CW // LRU_BW
    w4 = w.reshape(N_CT, per, LRU_BW, LRU_BW)
    eye = jnp.eye(per, dtype=w.dtype)
    return (w4[:, :, :, None, :] * eye[None, :, None, :, None]).reshape(N_CT, CW, CW)


def _from_blockdiag(g):
    per = CW // LRU_BW
    g5 = g.reshape(N_CT, per, LRU_BW, per, LRU_BW)
    return jnp.stack([g5[:, b, :, b, :] for b in range(per)], axis=1).reshape(LRU_BLOCKS, LRU_BW, LRU_BW)


def _local_grads(x2d, tgt2d, B, S, g_in, w_all, conv_w, conv_b, gate_x_w, gate_x_b, gate_a_w, gate_a_b, lam, gain,
                 proj_weights, g_fin, reduce, deps=()):
    wx_bd = _c(_to_blockdiag(gate_x_w))
    wa_bd = _c(_to_blockdiag(gate_a_w))
    tables = _retention_tables(S)
    gain3 = gain.reshape(HEADS, 1, DK)

    proj, ht = _inproj_fwd(x2d, g_in, w_all, deps)
    hlru, ya = _lru_fwd(proj, conv_w, conv_b, wx_bd, wa_bd, gate_x_b, gate_a_b, lam, B, S)
    o_pre, yb, states = _ret_fwd(proj, tables, gain3, B, S)
    wpa, wpb, wout = proj_weights(yb)
    loss, dx2, dya, dyb, dm, dgf, gw_proj = _mid(ya, yb, proj, x2d, tgt2d, wpa, wpb, wout, g_fin)
    g3 = _inproj_bwd_dw(ht, [dm], "inproj_bwd_dw_m")
    deps = reduce.m_ready(gw_proj, g3)
    dr, dgain = _ret_bwd(dyb, o_pre, proj, states, tables, gain3, B, S, deps)
    deps = reduce.ret_done(dr)
    g12 = _inproj_bwd_dw(ht, [dr], "inproj_bwd_dw_r", deps)
    deps = reduce.r_ready(g12)
    dxa, dga, dcw, dcb, dwx_bd, dwa_bd, dbx, dba, dlam = _lru_bwd(
        dya, proj, hlru, conv_w, conv_b, wx_bd, wa_bd, gate_x_b, gate_a_b, lam, B, S, deps)
    deps = reduce.lru_done(dxa)
    small = dict(conv_w=dcw, conv_b=dcb, gate_x_w=_from_blockdiag(dwx_bd), gate_x_b=dbx,
                 gate_a_w=_from_blockdiag(dwa_bd), gate_a_b=dba, lru_lambda=dlam, gn_gain=dgain.reshape(HEADS, DK),
                 norm_final=dgf)
    deps = deps + reduce.small_ready(_pack_small(small))
    g0 = _inproj_bwd_dw(ht, [dxa, dga], "inproj_bwd_dw_a", deps)
    deps = reduce.a_ready(g0)
    n_tiles = x2d.shape[0] // min(DX_TILE, x2d.shape[0])
    grad_x, dgin = _inproj_bwd_dx([dxa, dga, dr, dm], w_all, x2d, dx2, g_in, 0, n_tiles, None, "inproj_bwd_dx", deps)
    return loss[0, 0], grad_x, dgin


ALL_CHIPS = (0, 1, 2, 3)


class _GradReduce:
    def __init__(self, proj_done):
        self.pending = {}
        self.proj_done = proj_done

    def _start(self, key, bufs, plan, n_copies, name):
        send_sems, recv_sems, bufs, token = _copies_start(bufs, plan, n_copies, name + "_start")
        self.pending[key] = (send_sems, recv_sems, bufs, plan, name + "_wait")
        return (token,)

    def _finish(self, key, after):
        send_sems, recv_sems, bufs, plan, name = self.pending.pop(key)
        return _copies_wait(send_sems, recv_sems, bufs, after, plan, name)

    def _swap(self, key, parts):
        bufs = []
        for g in parts:
            bufs += [g, lax.empty((g.shape[0],) + g.shape[2:], F32)]
        n_slabs = [g.shape[0] for g in parts]
        return self._start(key, bufs, _swap_plan(n_slabs), sum(n_slabs), "swap_" + key)

    def _chip_sums(self, key, after):
        bufs = self._finish(key, after)
        return [_add_my_half(bufs[2 * i], bufs[2 * i + 1], "chip_sum_%s%d" % (key, i)) for i in range(len(bufs) // 2)]

    def _scatter(self, key, sums, dest_sets):
        bufs = []
        for cs in sums:
            bufs += [cs, jnp.zeros((3,) + cs.shape[1:], cs.dtype)]
        return self._start(key, bufs, _scatter_plan(dest_sets), 3 * len(sums), "scatter_" + key)

    def m_ready(self, gw_proj, g3):
        rows = gw_proj.shape[2] * gw_proj.shape[3]
        return self._swap("m", [gw_proj.reshape(N_CHIPS, 2, rows, D_MODEL), g3])

    def ret_done(self, after):
        return self._scatter("sm", self._chip_sums("m", after), [ALL_CHIPS, (3,)])

    def r_ready(self, g12):
        return self._swap("r", [g12])

    def lru_done(self, after):
        return self._scatter("sr", self._chip_sums("r", after), [(1, 2)])

    def small_ready(self, packed):
        land = jnp.zeros((8,) + packed.shape, F32)
        return self._start("small", [packed, land], _allgather_plan(), 7, "gather_small")

    def a_ready(self, g0):
        (token,) = self._swap("a", [g0])
        csp, gotp, cs3, got3 = self._finish("sm", token)
        self.m_piece = (cs3, got3)
        (g_proj,) = _join_halves([_sum_slabs(csp, gotp, "sum_w_proj")], [4], "join_halves_proj")
        after = self.proj_done(g_proj)
        return self._scatter("sa", self._chip_sums("a", after), [(0,)])

    def finish(self, after):
        small_sum = _sum_gathered(*self._finish("small", after), "sum_small_grads")
        cs3, got3 = self.m_piece
        cs12, got12 = self._finish("sr", after)
        cs0, got0 = self._finish("sa", after)
        half_in = _sum_parts([cs3, cs12, cs0], [got3, got12, got0], [(3,), (1, 2), (0,)], "sum_w_in")
        return small_sum, _join_halves([half_in], [8], "join_halves")[0]


_SMALL = ("gate_x_w", "gate_a_w", "conv_w", "conv_b", "gate_x_b", "gate_a_b", "lru_lambda", "gn_gain", "norm_final")
_SMALL_SHAPES = dict(gate_x_w=(LRU_BLOCKS, LRU_BW, LRU_BW), gate_a_w=(LRU_BLOCKS, LRU_BW, LRU_BW),
                     norm_in=(1, D_MODEL), conv_w=(CONV, D_MODEL), conv_b=(1, D_MODEL), gate_x_b=(1, D_MODEL),
                     gate_a_b=(1, D_MODEL), lru_lambda=(1, D_MODEL), gn_gain=(HEADS, DK), norm_final=(1, D_MODEL))


def _pack_small(small):
    return jnp.concatenate([small[k].reshape(-1, 128) for k in _SMALL], axis=0)


def _unpack_small(packed):
    out, r = {}, 0
    for k in _SMALL:
        shape = _SMALL_SHAPES[k]
        rows = 1
        for s in shape:
            rows *= s
        rows //= 128
        out[k] = packed[r:r + rows].reshape(shape)
        r += rows
    return out


def kernel(x, norm_in, w_in, conv_w, conv_b, gate_x_w, gate_x_b, gate_a_w, gate_a_b, lru_lambda, gn_gain, w_proj_a, w_proj_b, w_out, norm_final, loss_target, m_norm_in, m_w_in, m_conv_w, m_conv_b, m_gate_x_w, m_gate_x_b, m_gate_a_w, m_gate_a_b, m_lru_lambda, m_gn_gain, m_w_proj_a, m_w_proj_b, m_w_out, m_norm_final, v_norm_in, v_w_in, v_conv_w, v_conv_b, v_gate_x_w, v_gate_x_b, v_gate_a_w, v_gate_a_b, v_lru_lambda, v_gn_gain, v_w_proj_a, v_w_proj_b, v_w_out, v_norm_final):
    B, S, _ = x.shape
    T = B * S
    xi, yi, ci = _coords()
    chip = 2 * xi + yi

    cshard = D_MODEL // N_CHIPS
    mine = _cast_into_slot([w_in[0].reshape(2, D_MODEL // 2, 2 * D_MODEL)]
                           + [w[0].reshape(2, cshard // 2, D_MODEL) for w in (w_proj_a, w_proj_b, w_out)],
                           "cast_weights")
    plan = _gather_plan(3)
    s_sems, r_sems, pbufs, token = _copies_start(mine[1:], plan, 9, "gather_proj_start")
    w_all = _gather_chips(mine[:1], [4], "gather_weights")[0].reshape(N_CHIPS, D_MODEL, 2 * D_MODEL)

    def proj_weights(after):
        got = _copies_wait(s_sems, r_sems, pbufs, after, plan, "gather_proj_wait")
        return [b.reshape(D_MODEL, D_MODEL) for b in got]

    gshard = DK // N_CHIPS
    tiny = jnp.concatenate([conv_w[0], jnp.zeros((4, cshard), F32), jnp.pad(gn_gain[0], ((0, 4), (0, cshard - gshard)))],
                           axis=0)
    tiny_all = _all_gather8(tiny, "gather_small_weights").reshape(N_CHIPS, 2, 16, cshard)[:, 0]
    conv_w_full = jnp.transpose(tiny_all[:, 0:CONV, :], (1, 0, 2)).reshape(CONV, D_MODEL)
    gain_full = jnp.transpose(tiny_all[:, 8:8 + HEADS, :gshard], (1, 0, 2)).reshape(HEADS, DK)

    weights = dict(norm_in=norm_in, w_in=w_in, conv_w=conv_w, conv_b=conv_b, gate_x_w=gate_x_w, gate_x_b=gate_x_b,
                   gate_a_w=gate_a_w, gate_a_b=gate_a_b, lru_lambda=lru_lambda, gn_gain=gn_gain, w_proj_a=w_proj_a,
                   w_proj_b=w_proj_b, w_out=w_out, norm_final=norm_final)
    ms = dict(norm_in=m_norm_in, w_in=m_w_in, conv_w=m_conv_w, conv_b=m_conv_b, gate_x_w=m_gate_x_w,
              gate_x_b=m_gate_x_b, gate_a_w=m_gate_a_w, gate_a_b=m_gate_a_b, lru_lambda=m_lru_lambda, gn_gain=m_gn_gain,
              w_proj_a=m_w_proj_a, w_proj_b=m_w_proj_b, w_out=m_w_out, norm_final=m_norm_final)
    vs = dict(norm_in=v_norm_in, w_in=v_w_in, conv_w=v_conv_w, conv_b=v_conv_b, gate_x_w=v_gate_x_w,
              gate_x_b=v_gate_x_b, gate_a_w=v_gate_a_w, gate_a_b=v_gate_a_b, lru_lambda=v_lru_lambda, gn_gain=v_gn_gain,
              w_proj_a=v_w_proj_a, w_proj_b=v_w_proj_b, w_out=v_w_out, norm_final=v_norm_final)
    names = list(weights)
    grads, delta, new_m, new_v = {}, {}, {}, {}

    def update_big(k, g):
        shp = weights[k].shape
        two = lambda a: a.reshape(shp[1], shp[2])
        d, mn, vn = _adamw_big(two(weights[k]), g, two(ms[k]), two(vs[k]), "adamw_" + k)
        grads[k], delta[k], new_m[k], new_v[k] = g.reshape(shp), d.reshape(shp), mn.reshape(shp), vn.reshape(shp)
        return d

    def proj_done(g_proj):
        g_pr = g_proj.reshape(2, 3, D_MODEL // (2 * N_CHIPS), D_MODEL)
        for i, k in enumerate(("w_proj_a", "w_proj_b", "w_out")):
            last = update_big(k, g_pr[:, i].reshape(cshard, D_MODEL))
        return last

    reduce = _GradReduce(proj_done)
    loss, grad_x, dgin = _local_grads(
        x.reshape(T, D_MODEL), loss_target.reshape(T, D_MODEL), B, S, norm_in, w_all, conv_w_full, conv_b,
        gate_x_w[0], gate_x_b, gate_a_w[0], gate_a_b, lru_lambda, gain_full, proj_weights,
        norm_final.reshape(1, D_MODEL), reduce, deps=(token,))
    loss = lax.psum(loss, ("x", "y", "c"))

    g_norm_in = _sum_rows8(_all_gather8(dgin.reshape(8, 128), "gather_norm_in_grad"), 8, "sum_norm_in_grad")
    small_sum, g_in_full = reduce.finish(g_norm_in)
    update_big("w_in", g_in_full.reshape(D_MODEL, 2 * D_MODEL))

    gsm = _unpack_small(small_sum)
    gsm["norm_in"] = g_norm_in
    gsm["conv_w"] = lax.dynamic_slice_in_dim(gsm["conv_w"], chip * cshard, cshard, axis=1)
    gsm["gn_gain"] = lax.dynamic_slice_in_dim(gsm["gn_gain"], chip * gshard, gshard, axis=1)
    smalls = [k for k in names if k not in delta]

    def view(a):
        return a.reshape(1, -1) if a.ndim == 1 else (a.reshape(a.shape[1:]) if a.ndim > 2 else a)

    ds, mns, vns = _adamw_small([view(weights[k]) for k in smalls], [gsm[k].reshape(view(weights[k]).shape) for k in smalls],
                                [view(ms[k]) for k in smalls], [view(vs[k]) for k in smalls], "adamw_small")
    for k, d, mn, vn in zip(smalls, ds, mns, vns):
        shp = weights[k].shape
        grads[k], delta[k], new_m[k], new_v[k] = gsm[k].reshape(shp), d.reshape(shp), mn.reshape(shp), vn.reshape(shp)

    return (loss, grad_x.reshape(B, S, D_MODEL), *[grads[k] for k in names], *[delta[k] for k in names],
            *[new_m[k] for k in names], *[new_v[k] for k in names])
```

```python
import functools

import jax
import jax.numpy as jnp
from jax import lax
from jax.experimental import pallas as pl
from jax.experimental.pallas import tpu as pltpu

F32 = jnp.float32
_MXU = jnp.bfloat16

D_MODEL = 1024
N_GROUPS = 8
HEADS = 4
DK = 256
CHUNK = 128
CONV = 4
LRU_BLOCKS = 16
LRU_BW = 64
LRU_C = 8.0
ROPE_THETA = 10000.0
EPS = 1e-6
CW = 256
N_CT = D_MODEL // CW
N_CHIPS = 4
MESH = pl.DeviceIdType.MESH

ADAM_LR = 0.001
ADAM_B1 = 0.9
ADAM_B2 = 0.999
ADAM_EPS = 1e-08
ADAM_WD = 0.01
ADAM_STEP = 10

VMEM_LIMIT = 56 * 1024 * 1024


def _c(v):
    return v.astype(_MXU)


def _dot(a, b):
    return lax.dot_general(a, b, (((1,), (0,)), ((), ())), preferred_element_type=F32)


def _dot_nt(a, b):
    return lax.dot_general(a, b, (((1,), (1,)), ((), ())), preferred_element_type=F32)


def _dot_tn(a, b):
    return lax.dot_general(a, b, (((0,), (0,)), ((), ())), preferred_element_type=F32)


def _sigmoid(z):
    return 0.5 * jnp.tanh(0.5 * z) + 0.5


ANY_SPEC = pl.BlockSpec(memory_space=pl.ANY)


def _after(body, n_in, deps):
    n_deps = len(deps)

    def wrapped(*refs):
        return body(*refs[:n_in], *refs[n_in + n_deps:])

    return wrapped


def _params(sem=None):
    if sem is None:
        return pltpu.CompilerParams(vmem_limit_bytes=VMEM_LIMIT)
    return pltpu.CompilerParams(vmem_limit_bytes=VMEM_LIMIT, dimension_semantics=sem)


def _inproj_fwd(x2d, g_in, w_all, deps=()):
    T = x2d.shape[0]
    tm = min(1024, T)
    n_i = T // tm

    def body(*refs):
        x_ref, g_ref, w_ref = refs[:3]
        proj_ref, ht_ref, h_all = refs[-3:]
        i = pl.program_id(1)
        rows = pl.ds(pl.multiple_of(i * tm, tm), tm)

        @pl.when(pl.program_id(0) == 0)
        def _():
            x = x_ref[...]
            r = lax.rsqrt(jnp.mean(x * x, axis=-1, keepdims=True) + EPS)
            h = x * r * g_ref[...]
            h_all[rows, :] = h.astype(h_all.dtype)
            ht_ref[...] = h.T.astype(ht_ref.dtype)

        proj_ref[...] = _dot(h_all[rows, :], w_ref[0])

    first = lambda j, i: jnp.where(j == 0, i, n_i - 1)
    return pl.pallas_call(
        body,
        name="inproj_fwd",
        grid=(N_GROUPS, n_i),
        in_specs=[
            pl.BlockSpec((tm, D_MODEL), lambda j, i: (first(j, i), 0)),
            pl.BlockSpec((1, D_MODEL), lambda j, i: (0, 0)),
            pl.BlockSpec((1, D_MODEL, D_MODEL), lambda j, i: (j // 2, 0, j % 2)),
        ] + [pl.BlockSpec(memory_space=pl.ANY)] * len(deps),
        out_specs=[
            pl.BlockSpec((tm, D_MODEL), lambda j, i: (i, j)),
            pl.BlockSpec((D_MODEL, tm), lambda j, i: (0, first(j, i))),
        ],
        out_shape=[
            jax.ShapeDtypeStruct((T, N_GROUPS * D_MODEL), F32),
            jax.ShapeDtypeStruct((D_MODEL, T), _MXU),
        ],
        scratch_shapes=[pltpu.VMEM((T, D_MODEL), _MXU)],
        compiler_params=_params(("arbitrary", "arbitrary")),
    )(x2d, g_in, w_all, *deps)


def _scan_fwd(a, u):
    n = a.shape[0]
    row = lax.broadcasted_iota(jnp.int32, a.shape, 0)
    s = 1
    while s < n:
        m = row >= s
        u = u + a * jnp.where(m, pltpu.roll(u, s, 0), 0.0)
        a = a * jnp.where(m, pltpu.roll(a, s, 0), 1.0)
        s *= 2
    return a, u


def _scan_bwd(b, g):
    n = b.shape[0]
    row = lax.broadcasted_iota(jnp.int32, b.shape, 0)
    s = 1
    while s < n:
        m = row < n - s
        g = g + b * jnp.where(m, pltpu.roll(g, n - s, 0), 0.0)
        b = b * jnp.where(m, pltpu.roll(b, n - s, 0), 1.0)
        s *= 2
    return b, g


LANES = 128
SUBLANES = 8


def _scan_scratch(tc):
    by_lanes = pltpu.VMEM((CW // LANES, tc, LANES), F32)
    return [by_lanes, by_lanes, pltpu.VMEM((tc // SUBLANES, CW), F32), pltpu.VMEM((tc, CW), F32)]


def _scan_tile(a, u, edge, la_ref, lh_ref, c_ref, dst_ref, reverse):
    n, w = a.shape
    groups = n // SUBLANES
    a3 = a.reshape(groups, SUBLANES, w)
    u3 = u.reshape(groups, SUBLANES, w)
    row = lax.broadcasted_iota(jnp.int32, a3.shape, 1)
    for s in (1, 2, 4):
        m = (row < SUBLANES - s) if reverse else (row >= s)
        shift = SUBLANES - s if reverse else s
        u3 = u3 + a3 * jnp.where(m, pltpu.roll(u3, shift, 1), 0.0)
        a3 = a3 * jnp.where(m, pltpu.roll(a3, shift, 1), 1.0)
    al = a3.reshape(n, w)
    hl = u3.reshape(n, w)
    blocks = w // LANES
    for q in range(blocks):
        la_ref[q] = al[:, q * LANES:(q + 1) * LANES]
        lh_ref[q] = hl[:, q * LANES:(q + 1) * LANES]
    ends = pl.ds(0 if reverse else SUBLANES - 1, groups, stride=SUBLANES)
    end_a = jnp.concatenate([la_ref.at[q][ends, :] for q in range(blocks)], axis=-1)
    end_h = jnp.concatenate([lh_ref.at[q][ends, :] for q in range(blocks)], axis=-1)
    prod, part = (_scan_bwd if reverse else _scan_fwd)(end_a, end_h)
    total = part + prod * edge
    g_row = lax.broadcasted_iota(jnp.int32, total.shape, 0)
    if reverse:
        c_ref[...] = jnp.where(g_row == groups - 1, edge, pltpu.roll(total, groups - 1, 0))
    else:
        c_ref[...] = jnp.where(g_row == 0, edge, pltpu.roll(total, 1, 0))
    for g in range(groups):
        rows = slice(g * SUBLANES, (g + 1) * SUBLANES)
        for q in range(blocks):
            cols = slice(q * LANES, (q + 1) * LANES)
            dst_ref[rows, cols] = lh_ref[q, rows, :] + la_ref[q, rows, :] * c_ref[g:g + 1, cols]


def _softplus_neg(lam):
    z = -lam
    return jnp.maximum(z, 0.0) + jnp.log1p(jnp.exp(-jnp.abs(z)))


def _lru_gates(xc, wx_ref, wa_ref, bx_ref, ba_ref, lam_ref):
    xcb = _c(xc)
    i_t = _sigmoid(_dot(xcb, wx_ref[0]) + bx_ref[...])
    r_t = _sigmoid(_dot(xcb, wa_ref[0]) + ba_ref[...])
    sp = _softplus_neg(lam_ref[...])
    log_a = (-LRU_C) * r_t * sp
    a = jnp.exp(log_a)
    mult = jnp.sqrt(1.0 - a * a)
    return xcb, i_t, r_t, sp, a, mult


def _conv_from_ext(ext_ref, xa, cw_ref, cb_ref, tc):
    return (cb_ref[...] + cw_ref[3:4, :] * xa + cw_ref[2:3, :] * ext_ref[7:7 + tc, :]
            + cw_ref[1:2, :] * ext_ref[6:6 + tc, :] + cw_ref[0:1, :] * ext_ref[5:5 + tc, :])


def _lru_fwd(proj, conv_w, conv_b, wx_bd, wa_bd, bx, ba, lam, B, S):
    T = B * S
    tc = min(256, S)
    nt = S // tc
    h8 = tc // 8

    def body(xa_ref, halo_ref, ga_ref, cw_ref, cb_ref, wx_ref, wa_ref, bx_ref, ba_ref, lam_ref,
             h_ref, ya_ref, ext_ref, carry_ref, la_ref, lh_ref, c_ref):
        t = pl.program_id(2)

        @pl.when(t == 0)
        def _():
            carry_ref[...] = jnp.zeros_like(carry_ref)

        xa = xa_ref[...]
        ext_ref[0:8, :] = jnp.where(t == 0, 0.0, halo_ref[...])
        ext_ref[8:8 + tc, :] = xa
        xc = _conv_from_ext(ext_ref, xa, cw_ref, cb_ref, tc)
        _, i_t, _, _, a, mult = _lru_gates(xc, wx_ref, wa_ref, bx_ref, ba_ref, lam_ref)
        u = mult * (i_t * xc)
        _scan_tile(a, u, carry_ref[7:8, :], la_ref, lh_ref, c_ref, h_ref, False)
        h = h_ref[...]
        carry_ref[...] = h[tc - 8:tc, :]
        ga = ga_ref[...]
        ya_ref[...] = (ga * _sigmoid(ga) * h).astype(ya_ref.dtype)

    row = lambda b, t: b * nt + t
    vec = pl.BlockSpec((1, CW), lambda b, c, t: (0, c))
    mat = pl.BlockSpec((1, CW, CW), lambda b, c, t: (c, 0, 0))
    return pl.pallas_call(
        body,
        name="lru_fwd",
        grid=(B, N_CT, nt),
        in_specs=[
            pl.BlockSpec((tc, CW), lambda b, c, t: (row(b, t), c)),
            pl.BlockSpec((8, CW), lambda b, c, t: (jnp.maximum(row(b, t) * h8 - 1, 0), c)),
            pl.BlockSpec((tc, CW), lambda b, c, t: (row(b, t), N_CT + c)),
            pl.BlockSpec((CONV, CW), lambda b, c, t: (0, c)),
            vec, mat, mat, vec, vec, vec,
        ],
        out_specs=[
            pl.BlockSpec((tc, CW), lambda b, c, t: (row(b, t), c)),
            pl.BlockSpec((tc, CW), lambda b, c, t: (row(b, t), c)),
        ],
        out_shape=[
            jax.ShapeDtypeStruct((T, D_MODEL), F32),
            jax.ShapeDtypeStruct((T, D_MODEL), _MXU),
        ],
        scratch_shapes=[pltpu.VMEM((tc + 8, CW), F32), pltpu.VMEM((8, CW), F32)] + _scan_scratch(tc)[:3],
        compiler_params=_params(("parallel", "parallel", "arbitrary")),
    )(proj, proj, proj, conv_w, conv_b, wx_bd, wa_bd, bx, ba, lam)


def _lru_bwd(dya, proj, hlru, conv_w, conv_b, wx_bd, wa_bd, bx, ba, lam, B, S, deps=()):
    T = B * S
    tc = min(256, S)
    nt = S // tc
    h8 = tc // 8

    def body(dya_ref, xa_ref, xhalo_ref, ga_ref, h_ref, hhalo_ref, cw_ref, cb_ref, wx_ref, wa_ref, bx_ref, ba_ref,
             lam_ref, dxa_ref, dga_ref, dcw_ref, dcb_ref, dwx_ref, dwa_ref, dbx_ref, dba_ref, dlam_ref,
             ext_ref, ext2_ref, carry_ref, dhalo_ref, la_ref, lh_ref, c_ref, dh_ref):
        b = pl.program_id(1)
        t = pl.program_id(2)
        tt = nt - 1 - t

        @pl.when(t == 0)
        def _():
            carry_ref[...] = jnp.zeros_like(carry_ref)
            dhalo_ref[...] = jnp.zeros_like(dhalo_ref)

        @pl.when((t == 0) & (b == 0))
        def _():
            for r in (dcw_ref, dcb_ref, dwx_ref, dwa_ref, dbx_ref, dba_ref, dlam_ref):
                r[...] = jnp.zeros_like(r)

        xa = xa_ref[...]
        ext_ref[0:8, :] = jnp.where(tt == 0, 0.0, xhalo_ref[...])
        ext_ref[8:8 + tc, :] = xa
        xc = _conv_from_ext(ext_ref, xa, cw_ref, cb_ref, tc)
        xcb, i_t, r_t, sp, a, mult = _lru_gates(xc, wx_ref, wa_ref, bx_ref, ba_ref, lam_ref)

        h = h_ref[...]
        ga = ga_ref[...]
        dya_t = dya_ref[...]
        sg = _sigmoid(ga)
        dga_ref[...] = (dya_t * h * (sg * (1.0 + ga * (1.0 - sg)))).astype(dga_ref.dtype)
        dlru = dya_t * (ga * sg)

        row = lax.broadcasted_iota(jnp.int32, a.shape, 0)
        coef = jnp.where(row == tc - 1, 1.0, pltpu.roll(a, tc - 1, 0))
        _scan_tile(coef, dlru, carry_ref[0:1, :], la_ref, lh_ref, c_ref, dh_ref, True)
        dh = dh_ref[...]
        ext2_ref[0:tc, :] = a * dh
        carry_ref[...] = ext2_ref[0:8, :]

        ext2_ref[0:8, :] = jnp.where(tt == 0, 0.0, hhalo_ref[...])
        ext2_ref[8:8 + tc, :] = h
        hprev = ext2_ref[7:7 + tc, :]

        da = dh * hprev
        ix = i_t * xc
        dmult = dh * ix
        di = dh * mult * xc
        dxc = dh * mult * i_t
        dlog_a = da * a - dmult * (a * a) / mult
        dr = dlog_a * ((-LRU_C) * sp)
        dlam_ref[...] += jnp.sum(dlog_a * r_t, axis=0, keepdims=True) * (LRU_C * _sigmoid(-lam_ref[...]))
        dza = dr * r_t * (1.0 - r_t)
        dzx = di * i_t * (1.0 - i_t)
        dzab = _c(dza)
        dzxb = _c(dzx)
        dxc = dxc + _dot_nt(dzxb, wx_ref[0]) + _dot_nt(dzab, wa_ref[0])
        dwx_ref[0] += _dot_tn(xcb, dzxb)
        dwa_ref[0] += _dot_tn(xcb, dzab)
        dbx_ref[...] += jnp.sum(dzx, axis=0, keepdims=True)
        dba_ref[...] += jnp.sum(dza, axis=0, keepdims=True)

        dcb_ref[...] += jnp.sum(dxc, axis=0, keepdims=True)
        dcw_ref[3:4, :] += jnp.sum(dxc * xa, axis=0, keepdims=True)
        dcw_ref[2:3, :] += jnp.sum(dxc * ext_ref[7:7 + tc, :], axis=0, keepdims=True)
        dcw_ref[1:2, :] += jnp.sum(dxc * ext_ref[6:6 + tc, :], axis=0, keepdims=True)
        dcw_ref[0:1, :] += jnp.sum(dxc * ext_ref[5:5 + tc, :], axis=0, keepdims=True)
        ext2_ref[0:tc, :] = dxc
        ext2_ref[tc:tc + 8, :] = dhalo_ref[...]
        dxa = (cw_ref[3:4, :] * dxc + cw_ref[2:3, :] * ext2_ref[1:1 + tc, :]
               + cw_ref[1:2, :] * ext2_ref[2:2 + tc, :] + cw_ref[0:1, :] * ext2_ref[3:3 + tc, :])
        dxa_ref[...] = dxa.astype(dxa_ref.dtype)
        dhalo_ref[...] = ext2_ref[0:8, :]

    row_of = lambda b, t: b * nt + (nt - 1 - t)
    tile = lambda off: pl.BlockSpec((tc, CW), lambda c, b, t: (row_of(b, t), off + c))
    halo = pl.BlockSpec((8, CW), lambda c, b, t: (jnp.maximum(row_of(b, t) * h8 - 1, 0), c))
    vec = pl.BlockSpec((1, CW), lambda c, b, t: (0, c))
    mat = pl.BlockSpec((1, CW, CW), lambda c, b, t: (c, 0, 0))
    cwspec = pl.BlockSpec((CONV, CW), lambda c, b, t: (0, c))
    return pl.pallas_call(
        _after(body, 13, deps),
        name="lru_bwd",
        grid=(N_CT, B, nt),
        in_specs=[tile(0), tile(0), halo, tile(N_CT), tile(0), halo, cwspec, vec, mat, mat, vec, vec, vec]
        + [ANY_SPEC] * len(deps),
        out_specs=[tile(0), tile(0), cwspec, vec, mat, mat, vec, vec, vec],
        out_shape=[
            jax.ShapeDtypeStruct((T, D_MODEL), _MXU),
            jax.ShapeDtypeStruct((T, D_MODEL), _MXU),
            jax.ShapeDtypeStruct((CONV, D_MODEL), F32),
            jax.ShapeDtypeStruct((1, D_MODEL), F32),
            jax.ShapeDtypeStruct((N_CT, CW, CW), F32),
            jax.ShapeDtypeStruct((N_CT, CW, CW), F32),
            jax.ShapeDtypeStruct((1, D_MODEL), F32),
            jax.ShapeDtypeStruct((1, D_MODEL), F32),
            jax.ShapeDtypeStruct((1, D_MODEL), F32),
        ],
        scratch_shapes=[pltpu.VMEM((tc + 8, CW), F32), pltpu.VMEM((tc + 8, CW), F32),
                        pltpu.VMEM((8, CW), F32), pltpu.VMEM((8, CW), F32)] + _scan_scratch(tc),
        compiler_params=_params(("parallel", "arbitrary", "arbitrary")),
    )(dya, proj, proj, proj, hlru, hlru, conv_w, conv_b, wx_bd, wa_bd, bx, ba, lam, *deps)


def _retention_tables(S):
    half = DK // 2
    freqs = ROPE_THETA ** (-jnp.arange(half, dtype=F32) / half)
    ang = jnp.arange(S, dtype=F32)[:, None] * freqs[None, :]
    log_g = jnp.log1p(-(2.0 ** (-5.0 - jnp.arange(HEADS, dtype=F32))))
    idx = jnp.arange(CHUNK, dtype=F32)
    diff = idx[:, None] - idx[None, :]
    inner = jnp.where(diff >= 0, jnp.exp(jnp.maximum(diff, 0.0)[None] * log_g[:, None, None]), 0.0)
    cross = jnp.exp((idx[None, :] + 1.0) * log_g[:, None])[:, :, None]
    state = jnp.exp((CHUNK - 1.0 - idx[None, :]) * log_g[:, None])[:, :, None]
    gam = jnp.broadcast_to(jnp.exp(CHUNK * log_g)[:, None, None], (HEADS, 1, DK))
    return jnp.cos(ang), jnp.sin(ang), inner, cross, state, gam


def _rot(x, cos, sin):
    half = DK // 2
    x1, x2 = x[:, :half], x[:, half:]
    return jnp.concatenate([x1 * cos - x2 * sin, x1 * sin + x2 * cos], axis=-1)


def _rot_t(y, cos, sin):
    half = DK // 2
    y1, y2 = y[:, :half], y[:, half:]
    return jnp.concatenate([y1 * cos + y2 * sin, y2 * cos - y1 * sin], axis=-1)


def _groupnorm(o):
    mu = jnp.mean(o, axis=-1, keepdims=True)
    oc = o - mu
    rs = lax.rsqrt(jnp.mean(oc * oc, axis=-1, keepdims=True) + EPS)
    return oc * rs, rs


def _ret_specs(B, chunk_of):
    qkv = lambda g: pl.BlockSpec((B, CHUNK, D_MODEL), lambda c: (0, chunk_of(c), g))
    act = pl.BlockSpec((B, CHUNK, D_MODEL), lambda c: (0, chunk_of(c), 0))
    rope = pl.BlockSpec((CHUNK, DK // 2), lambda c: (chunk_of(c), 0))
    dmat = pl.BlockSpec((HEADS, CHUNK, CHUNK), lambda c: (0, 0, 0))
    dvec = pl.BlockSpec((HEADS, CHUNK, 1), lambda c: (0, 0, 0))
    hrow = pl.BlockSpec((HEADS, 1, DK), lambda c: (0, 0, 0))
    rst = pl.BlockSpec((1, B, HEADS, DK, DK), lambda c: (chunk_of(c), 0, 0, 0, 0))
    return qkv, act, rope, dmat, dvec, hrow, rst


def _ret_fwd(proj, tables, gain3, B, S):
    T = B * S
    nc = S // CHUNK
    cos, sin, dmat_t, cd_t, sd_t, gam_t = tables

    def body(q_ref, k_ref, v_ref, gb_ref, cos_ref, sin_ref, dm_ref, cd_ref, sd_ref, gam_ref, gain_ref,
             o_ref, yb_ref, rs_ref, state_ref):
        @pl.when(pl.program_id(0) == 0)
        def _():
            state_ref[...] = jnp.zeros_like(state_ref)

        cos_t, sin_t = cos_ref[...], sin_ref[...]
        for b, h in [(b, h) for b in range(B) for h in range(HEADS)]:
            cols = slice(h * DK, (h + 1) * DK)
            qb = _c(_rot(q_ref[b, :, cols], cos_t, sin_t))
            kb = _c(_rot(k_ref[b, :, cols], cos_t, sin_t) * (DK ** -0.5))
            v = v_ref[b, :, cols]
            state = state_ref[b, h]
            sb = _c(state)
            rs_ref[0, b, h] = sb
            scores = _dot_nt(qb, kb) * dm_ref[h]
            o = _dot(_c(scores), _c(v)) + _dot(qb, sb) * cd_ref[h]
            state_ref[b, h] = gam_ref[h] * state + _dot_tn(kb, _c(v * sd_ref[h]))
            o_ref[b, :, cols] = o
            n, _ = _groupnorm(o)
            gb = gb_ref[b, :, cols]
            yb_ref[b, :, cols] = (gb * _sigmoid(gb) * (n * gain_ref[h])).astype(yb_ref.dtype)

    qkv, act, rope, dmat, dvec, hrow, rst = _ret_specs(B, lambda c: c)
    proj3 = proj.reshape(B, S, proj.shape[1])
    o_pre, yb, states = pl.pallas_call(
        body,
        name="ret_fwd",
        grid=(nc,),
        in_specs=[qkv(2), qkv(3), qkv(4), qkv(5), rope, rope, dmat, dvec, dvec, hrow, hrow],
        out_specs=[act, act, rst],
        out_shape=[
            jax.ShapeDtypeStruct((B, S, D_MODEL), F32),
            jax.ShapeDtypeStruct((B, S, D_MODEL), _MXU),
            jax.ShapeDtypeStruct((nc, B, HEADS, DK, DK), _MXU),
        ],
        scratch_shapes=[pltpu.VMEM((B, HEADS, DK, DK), F32)],
        compiler_params=_params(("arbitrary",)),
    )(proj3, proj3, proj3, proj3, cos, sin, dmat_t, cd_t, sd_t, gam_t, gain3)
    return o_pre.reshape(T, D_MODEL), yb.reshape(T, D_MODEL), states


def _ret_bwd(dyb, o_pre, proj, states, tables, gain3, B, S, deps=()):
    T = B * S
    nc = S // CHUNK
    cos, sin, dmat_t, cd_t, sd_t, gam_t = tables

    def body(dyb_ref, o_ref, q_ref, k_ref, v_ref, gb_ref, rs_ref, cos_ref, sin_ref, dm_ref, cd_ref, sd_ref, gam_ref,
             gain_ref, dr_ref, dgain_ref, dstate_ref):
        @pl.when(pl.program_id(0) == 0)
        def _():
            dstate_ref[...] = jnp.zeros_like(dstate_ref)
            dgain_ref[...] = jnp.zeros_like(dgain_ref)

        cos_t, sin_t = cos_ref[...], sin_ref[...]
        for b, h in [(b, h) for b in range(B) for h in range(HEADS)]:
            cols = slice(h * DK, (h + 1) * DK)
            gain = gain_ref[h]
            n, rs = _groupnorm(o_ref[b, :, cols])
            gb = gb_ref[b, :, cols]
            sg = _sigmoid(gb)
            dy = dyb_ref[b, :, cols]
            part = lambda g: slice(g * D_MODEL + h * DK, g * D_MODEL + (h + 1) * DK)
            dr_ref[b, :, part(3)] = (dy * (n * gain) * (sg * (1.0 + gb * (1.0 - sg)))).astype(dr_ref.dtype)
            dgn = dy * (gb * sg)
            dgain_ref[h] += jnp.sum(dgn * n, axis=0, keepdims=True)
            dn = dgn * gain
            do = rs * (dn - jnp.mean(dn, axis=-1, keepdims=True) - n * jnp.mean(dn * n, axis=-1, keepdims=True))

            qb = _c(_rot(q_ref[b, :, cols], cos_t, sin_t))
            kb = _c(_rot(k_ref[b, :, cols], cos_t, sin_t) * (DK ** -0.5))
            v = v_ref[b, :, cols]
            vb = _c(v)
            vsb = _c(v * sd_ref[h])
            dob = _c(do)
            docb = _c(do * cd_ref[h])
            dmat = dm_ref[h]
            dstate = dstate_ref[b, h]
            dsb = _c(dstate)
            pb = _c(_dot_nt(qb, kb) * dmat)
            dsc = _c(_dot_nt(dob, vb) * dmat)
            dq = _dot(dsc, kb) + _dot_nt(docb, rs_ref[0, b, h])
            dk = _dot_tn(dsc, qb) + _dot_nt(vsb, dsb)
            dv = _dot_tn(pb, dob) + _dot(kb, dsb) * sd_ref[h]
            dstate_ref[b, h] = gam_ref[h] * dstate + _dot_tn(qb, docb)
            dr_ref[b, :, part(0)] = _rot_t(dq, cos_t, sin_t).astype(dr_ref.dtype)
            dr_ref[b, :, part(1)] = (_rot_t(dk, cos_t, sin_t) * (DK ** -0.5)).astype(dr_ref.dtype)
            dr_ref[b, :, part(2)] = dv.astype(dr_ref.dtype)

    qkv, act, rope, dmat, dvec, hrow, rst = _ret_specs(B, lambda c: nc - 1 - c)
    wide = pl.BlockSpec((B, CHUNK, 4 * D_MODEL), lambda c: (0, nc - 1 - c, 0))
    proj3 = proj.reshape(B, S, proj.shape[1])
    dr, dgain = pl.pallas_call(
        _after(body, 14, deps),
        name="ret_bwd",
        grid=(nc,),
        in_specs=[act, act, qkv(2), qkv(3), qkv(4), qkv(5), rst, rope, rope, dmat, dvec, dvec, hrow, hrow]
        + [ANY_SPEC] * len(deps),
        out_specs=[wide, hrow],
        out_shape=[jax.ShapeDtypeStruct((B, S, 4 * D_MODEL), _MXU), jax.ShapeDtypeStruct((HEADS, 1, DK), F32)],
        scratch_shapes=[pltpu.VMEM((B, HEADS, DK, DK), F32)],
        compiler_params=_params(("arbitrary",)),
    )(dyb.reshape(B, S, D_MODEL), o_pre.reshape(B, S, D_MODEL), proj3, proj3, proj3, proj3, states, cos, sin, dmat_t,
      cd_t, sd_t, gam_t, gain3, *deps)
    return dr.reshape(T, 4 * D_MODEL), dgain


def _mid(ya, yb, proj, x2d, tgt2d, wpa, wpb, wout, g_fin):
    T = x2d.shape[0]
    tm = min(256, T)
    n_steps = T // tm
    rows = D_MODEL // (2 * N_CHIPS)

    def body(ya_ref, yb_ref, ma_ref, mb_ref, x_ref, t_ref, gf_ref, wpa_hbm, wpb_hbm, wout_hbm,
             loss_ref, dx2_ref, dya_ref, dyb_ref, dm_ref, dgf_ref, gw_hbm, w_ref, acc_ref, sem):
        i = pl.program_id(0)

        @pl.when(i == 0)
        def _():
            loads = [pltpu.make_async_copy(src, w_ref.at[k], sem.at[k]) for k, src in enumerate((wpa_hbm, wpb_hbm, wout_hbm))]
            for cp in loads:
                cp.start()
            for cp in loads:
                cp.wait()
            acc_ref[...] = jnp.zeros_like(acc_ref)
            loss_ref[...] = jnp.zeros_like(loss_ref)
            dgf_ref[...] = jnp.zeros_like(dgf_ref)

        ya_t, yb_t = ya_ref[...], yb_ref[...]
        out_a = _dot(ya_t, w_ref[0])
        out_b = _dot(yb_t, w_ref[1])
        sa = _sigmoid(ma_ref[...])
        sb = _sigmoid(mb_ref[...])
        mgb = _c(sa * out_a + sb * out_b)
        x2 = x_ref[...] + _dot(mgb, w_ref[2])
        r2 = lax.rsqrt(jnp.mean(x2 * x2, axis=-1, keepdims=True) + EPS)
        nx = x2 * r2
        gf = gf_ref[...]
        err = nx * gf - t_ref[...]
        loss_ref[...] += 0.5 * jnp.sum(jnp.mean(err * err, axis=-1, keepdims=True), axis=0, keepdims=True)
        dy = err * (1.0 / D_MODEL)
        dgf_ref[...] += jnp.sum(dy * nx, axis=0, keepdims=True)
        dyg = dy * gf
        dx2 = r2 * (dyg - nx * jnp.mean(dyg * nx, axis=-1, keepdims=True))
        dx2_ref[...] = dx2
        dx2b = _c(dx2)
        dmg = _dot_nt(dx2b, w_ref[2])
        acc_ref[2] += _dot_tn(mgb, dx2b)
        dm_ref[:, :D_MODEL] = (dmg * out_a * sa * (1.0 - sa)).astype(dm_ref.dtype)
        dm_ref[:, D_MODEL:] = (dmg * out_b * sb * (1.0 - sb)).astype(dm_ref.dtype)
        dab = _c(dmg * sa)
        dbb = _c(dmg * sb)
        dya_ref[...] = _dot_nt(dab, w_ref[0])
        dyb_ref[...] = _dot_nt(dbb, w_ref[1])
        acc_ref[0] += _dot_tn(ya_t, dab)
        acc_ref[1] += _dot_tn(yb_t, dbb)

        @pl.when(i == n_steps - 1)
        def _():
            copies = [pltpu.make_async_copy(acc_ref.at[k, pl.ds((2 * p + hf) * rows, rows), :], gw_hbm.at[p, hf, k],
                                            sem.at[(k * N_CHIPS + p) * 2 + hf])
                      for k in range(3) for p in range(N_CHIPS) for hf in range(2)]
            for cp in copies:
                cp.start()
            for cp in copies:
                cp.wait()

    tile = lambda j: pl.BlockSpec((tm, D_MODEL), lambda i: (i, j))
    one = pl.BlockSpec((1, D_MODEL), lambda i: (0, 0))
    anyspec = pl.BlockSpec(memory_space=pl.ANY)
    return pl.pallas_call(
        body,
        name="mid",
        grid=(n_steps,),
        in_specs=[tile(0), tile(0), tile(6), tile(7), tile(0), tile(0), one, anyspec, anyspec, anyspec],
        out_specs=[pl.BlockSpec((1, 1), lambda i: (0, 0)), tile(0), tile(0), tile(0),
                   pl.BlockSpec((tm, 2 * D_MODEL), lambda i: (i, 0)), one, anyspec],
        out_shape=[
            jax.ShapeDtypeStruct((1, 1), F32),
            jax.ShapeDtypeStruct((T, D_MODEL), F32),
            jax.ShapeDtypeStruct((T, D_MODEL), F32),
            jax.ShapeDtypeStruct((T, D_MODEL), F32),
            jax.ShapeDtypeStruct((T, 2 * D_MODEL), _MXU),
            jax.ShapeDtypeStruct((1, D_MODEL), F32),
            jax.ShapeDtypeStruct((N_CHIPS, 2, 3, rows, D_MODEL), F32),
        ],
        scratch_shapes=[pltpu.VMEM((3, D_MODEL, D_MODEL), _MXU), pltpu.VMEM((3, D_MODEL, D_MODEL), F32),
                        pltpu.SemaphoreType.DMA((3 * N_CHIPS * 2,))],
        compiler_params=_params(("arbitrary",)),
    )(ya, yb, proj, proj, x2d, tgt2d, g_fin, wpa, wpb, wout)


DX_TILE = 512


def _inproj_bwd_dx(dparts, w_all, x2d, dx2, g_in, first, count, prev, name, deps=()):
    T = x2d.shape[0]
    tm = min(DX_TILE, T)
    n_d = len(dparts)
    groups = [(a, k) for a, d in enumerate(dparts) for k in range(d.shape[1] // D_MODEL)]
    dg_start = jnp.zeros((1, D_MODEL), F32) if prev is None else prev[1]
    carried = () if prev is None else (prev[0],)

    def body(*refs):
        d_refs = refs[:n_d]
        x_ref, dx2_ref, g_ref, dg0_ref, w_hbm = refs[n_d:n_d + 5]
        dx_ref, dg_ref, w_ref, sem = refs[-4:]

        @pl.when(pl.program_id(0) == 0)
        def _():
            cp = pltpu.make_async_copy(w_hbm, w_ref, sem)
            cp.start()
            cp.wait()
            dg_ref[...] = dg0_ref[...]

        dh = jnp.zeros((tm, D_MODEL), F32)
        for j, (a, k) in enumerate(groups):
            dh = dh + _dot_nt(d_refs[a][:, k * D_MODEL:(k + 1) * D_MODEL],
                              w_ref[j // 2, :, (j % 2) * D_MODEL:(j % 2 + 1) * D_MODEL])
        x = x_ref[...]
        r = lax.rsqrt(jnp.mean(x * x, axis=-1, keepdims=True) + EPS)
        nx = x * r
        dg_ref[...] += jnp.sum(dh * nx, axis=0, keepdims=True)
        dhg = dh * g_ref[...]
        dx_ref[...] = dx2_ref[...] + r * (dhg - nx * jnp.mean(dhg * nx, axis=-1, keepdims=True))

    tile = pl.BlockSpec((tm, D_MODEL), lambda i: (first + i, 0))
    one = pl.BlockSpec((1, D_MODEL), lambda i: (0, 0))
    return pl.pallas_call(
        body,
        name=name,
        grid=(count,),
        in_specs=[pl.BlockSpec((tm, d.shape[1]), lambda i: (first + i, 0)) for d in dparts]
        + [tile, tile, one, one, ANY_SPEC] + [ANY_SPEC] * (len(carried) + len(deps)),
        out_specs=[tile, one],
        out_shape=[jax.ShapeDtypeStruct((T, D_MODEL), F32), jax.ShapeDtypeStruct((1, D_MODEL), F32)],
        input_output_aliases={n_d + 5: 0} if carried else {},
        scratch_shapes=[pltpu.VMEM(w_all.shape, w_all.dtype), pltpu.SemaphoreType.DMA],
        compiler_params=_params(("arbitrary",)),
    )(*dparts, x2d, dx2, g_in, dg_start, w_all, *carried, *deps)


def _inproj_bwd_dw(ht, dparts, name, deps=()):
    T = ht.shape[1]
    tn = 512
    half = D_MODEL // 2
    per_chip = 2 * D_MODEL // tn
    n_d = len(dparts)
    tiles = [(a, t) for a, d in enumerate(dparts) for t in range(d.shape[1] // tn)]
    offs = [sum(d.shape[1] // tn for d in dparts[:a]) for a in range(n_d)]

    def body(*refs):
        ht_ref = refs[0]
        d_refs = refs[1:1 + n_d]
        out_ref = refs[-1]
        t = pl.program_id(0)

        for a in range(n_d):
            lo, hi = offs[a], offs[a] + dparts[a].shape[1] // tn

            @pl.when((t >= lo) & (t < hi))
            def _(a=a):
                g = _dot(ht_ref[...], d_refs[a][...])
                out_ref[0, 0] = g[:half]
                out_ref[0, 1] = g[half:]

    def dspec(a):
        n_a = dparts[a].shape[1] // tn
        return pl.BlockSpec((T, tn), lambda t: (0, jnp.clip(t - offs[a], 0, n_a - 1)))

    return pl.pallas_call(
        body,
        name=name,
        grid=(len(tiles),),
        in_specs=[pl.BlockSpec((D_MODEL, T), lambda t: (0, 0))] + [dspec(a) for a in range(n_d)]
        + [ANY_SPEC] * len(deps),
        out_specs=pl.BlockSpec((1, 2, half, tn), lambda t: (t // per_chip, 0, 0, t % per_chip)),
        out_shape=jax.ShapeDtypeStruct((len(tiles) // per_chip, 2, half, 2 * D_MODEL), F32),
        compiler_params=_params(("parallel",)),
    )(ht, *dparts, *deps)


def _coords():
    return lax.axis_index("x"), lax.axis_index("y"), lax.axis_index("c")


def _other_chips(x, y):
    return [(1 - x, y), (x, 1 - y), (1 - x, 1 - y)]


def _all_gather8(xs, name, deps=()):
    m_per, n = xs.shape

    def body(x_ref, *rest):
        out_ref, send_sems, recv_sems, local_sem = rest[-4:]
        x, y, c = _coords()
        me, sibling = (x, y, c), (x, y, 1 - c)
        chips = _other_chips(x, y)

        def rows(px, py, pc):
            return out_ref.at[pl.ds((4 * px + 2 * py + pc) * m_per, m_per), :]

        def copy(k, block, to, src=None):
            return pltpu.make_async_remote_copy(
                src_ref=rows(*block) if src is None else src, dst_ref=rows(*block),
                send_sem=send_sems.at[k], recv_sem=recv_sems.at[k], device_id=to, device_id_type=MESH)

        mine = pltpu.make_async_copy(x_ref, rows(*me), local_sem)
        mine.start()
        first = [copy(0, me, sibling, src=x_ref)]
        first += [copy(1 + j, me, (*chip, c), src=x_ref) for j, chip in enumerate(chips)]
        for cp in first:
            cp.start()
        passed = [copy(4 + j, (*chip, c), sibling) for j, chip in enumerate(chips)]
        for j, chip in enumerate(chips):
            copy(1 + j, (*chip, c), me).wait_recv()
            passed[j].start()
        copy(0, sibling, me).wait_recv()
        for j, chip in enumerate(chips):
            copy(4 + j, (*chip, 1 - c), me).wait_recv()
        for cp in first + passed:
            cp.wait_send()
        mine.wait()

    return pl.pallas_call(
        body,
        name=name,
        out_shape=jax.ShapeDtypeStruct((8 * m_per, n), xs.dtype),
        in_specs=[pl.BlockSpec(memory_space=pltpu.VMEM)] + [ANY_SPEC] * len(deps),
        out_specs=pl.BlockSpec(memory_space=pltpu.VMEM),
        scratch_shapes=[pltpu.SemaphoreType.DMA((7,)), pltpu.SemaphoreType.DMA((7,)), pltpu.SemaphoreType.DMA],
        compiler_params=pltpu.CompilerParams(vmem_limit_bytes=VMEM_LIMIT),
    )(xs, *deps)


def _chunks(rows, n):
    size = rows // n
    return [pl.ds(q * size, size) for q in range(n)]


HBM_SPEC = pl.BlockSpec(memory_space=pltpu.HBM)
SEM_SPEC = pl.BlockSpec(memory_space=pltpu.SEMAPHORE)
DATAFLOW = pltpu.SideEffectType.DATAFLOW_SIDE_EFFECTING


def _copies_start(bufs, plan, n_copies, name):
    n = len(bufs)

    def body(*refs):
        ins = refs[:n]
        send_sems, recv_sems = refs[n], refs[n + 1]
        token = refs[-1]
        for k, send, _ in plan(ins):
            if send is not None:
                src, dst, dev, pred = send
                cp = pltpu.make_async_remote_copy(src_ref=src, dst_ref=dst, send_sem=send_sems.at[k],
                                                  recv_sem=recv_sems.at[k], device_id=dev, device_id_type=MESH)
                if pred is None:
                    cp.start()
                else:
                    pl.when(pred)(cp.start)
        token[...] = jnp.zeros_like(token)

    hbm = [pltpu.with_memory_space_constraint(b, pltpu.HBM) for b in bufs]
    outs = pl.pallas_call(
        body,
        name=name,
        in_specs=[HBM_SPEC] * n,
        out_specs=(SEM_SPEC, SEM_SPEC, *([HBM_SPEC] * n), pl.BlockSpec(memory_space=pltpu.VMEM)),
        out_shape=(pltpu.SemaphoreType.DMA((n_copies,)), pltpu.SemaphoreType.DMA((n_copies,)),
                   *[pltpu.HBM(b.shape, b.dtype) for b in bufs], jax.ShapeDtypeStruct((8, 128), F32)),
        input_output_aliases={a: 2 + a for a in range(n)},
        compiler_params=pltpu.CompilerParams(has_side_effects=DATAFLOW),
    )(*hbm)
    return outs[0], outs[1], list(outs[2:2 + n]), outs[-1]


def _copies_wait(send_sems, recv_sems, bufs, after, plan, name):
    n = len(bufs)

    def body(*refs):
        ins = refs[:n]
        s_sems, r_sems = refs[n], refs[n + 1]
        for k, send, recv in plan(ins):
            if send is not None:
                src, dst, dev, pred = send
                cp = pltpu.make_async_remote_copy(src_ref=src, dst_ref=dst, send_sem=s_sems.at[k],
                                                  recv_sem=r_sems.at[k], device_id=dev, device_id_type=MESH)
                if pred is None:
                    cp.wait_send()
                else:
                    pl.when(pred)(cp.wait_send)
            if recv is not None:
                dst, pred = recv
                cp = pltpu.make_async_remote_copy(src_ref=dst, dst_ref=dst, send_sem=s_sems.at[k],
                                                  recv_sem=r_sems.at[k], device_id=_coords(), device_id_type=MESH)
                if pred is None:
                    cp.wait_recv()
                else:
                    pl.when(pred)(cp.wait_recv)

    outs = pl.pallas_call(
        body,
        name=name,
        in_specs=[HBM_SPEC] * n + [SEM_SPEC, SEM_SPEC, pl.BlockSpec(memory_space=pl.ANY)],
        out_specs=[HBM_SPEC] * n,
        out_shape=[pltpu.HBM(b.shape, b.dtype) for b in bufs],
        input_output_aliases={a: a for a in range(n)},
        compiler_params=pltpu.CompilerParams(has_side_effects=DATAFLOW),
    )(*bufs, send_sems, recv_sems, after)
    return list(outs)


def _gather_plan(n_bufs):
    def plan(refs):
        x, y, c = _coords()
        me = 2 * x + y
        out = []
        for k, (px, py) in enumerate(_other_chips(x, y)):
            for a in range(n_bufs):
                out.append((k * n_bufs + a, (refs[a].at[me], refs[a].at[me], (px, py, c), None),
                            (refs[a].at[2 * px + py], None)))
        return out
    return plan


def _cast_into_slot(ws, name):
    n = len(ws)
    nt = 2

    def body(s_ref, *refs):
        for a in range(n):
            refs[n + a][0] = refs[a][...].astype(refs[n + a].dtype)

    xi, yi, _ = _coords()
    return pl.pallas_call(
        body,
        name=name,
        grid_spec=pltpu.PrefetchScalarGridSpec(
            num_scalar_prefetch=1,
            grid=(2, nt),
            in_specs=[pl.BlockSpec((1, w.shape[1] // nt, w.shape[2]), lambda hf, i, s: (hf, i, 0)) for w in ws],
            out_specs=[pl.BlockSpec((1, 1, w.shape[1] // nt, w.shape[2]), lambda hf, i, s: (s[0], hf, i, 0)) for w in ws],
        ),
        out_shape=[jax.ShapeDtypeStruct((N_CHIPS,) + w.shape, _MXU) for w in ws],
        compiler_params=_params(("parallel", "parallel")),
    )((2 * xi + yi).reshape(1).astype(jnp.int32), *ws)


def _gather_chips(bufs, n_chunks, name):
    n = len(bufs)
    pieces = [(a, rows) for a in range(n) for rows in _chunks(bufs[a].shape[2], n_chunks[a])]
    n_p = len(pieces)

    def body(*refs):
        outs = refs[n:2 * n]
        send_sems, recv_sems, fsend_sems, frecv_sems = refs[2 * n:]
        x, y, c = _coords()
        me = 2 * x + y
        near = [(1 - x, y), (x, 1 - y)]
        slots = [2 * (1 - x) + y, 2 * x + (1 - y), 2 * (1 - x) + (1 - y)]
        pass_to = (jnp.where(c == 0, x, 1 - x), jnp.where(c == 0, 1 - y, y))
        pass_slot = jnp.where(c == 0, slots[0], slots[1])

        def send(k, i, slot, chip):
            a, rows = pieces[i]
            return pltpu.make_async_remote_copy(
                src_ref=outs[a].at[slot, c, rows], dst_ref=outs[a].at[slot, c, rows], send_sem=send_sems.at[k * n_p + i],
                recv_sem=recv_sems.at[k * n_p + i], device_id=(*chip, c), device_id_type=MESH)

        def forward(k, i, half):
            a, rows = pieces[i]
            return pltpu.make_async_remote_copy(
                src_ref=outs[a].at[slots[k], half, rows], dst_ref=outs[a].at[slots[k], half, rows],
                send_sem=fsend_sems.at[k * n_p + i], recv_sem=frecv_sems.at[k * n_p + i],
                device_id=(x, y, 1 - c), device_id_type=MESH)

        started = [send(k, i, me, chip) for i in range(n_p) for k, chip in enumerate(near)]
        for cp in started:
            cp.start()
        for i in range(n_p):
            for k, chip in enumerate(near):
                send(k, i, slots[k], chip).wait_recv()
            later = [send(2, i, pass_slot, pass_to), forward(0, i, c), forward(1, i, c)]
            for cp in later:
                cp.start()
            started += later
        for i in range(n_p):
            send(2, i, slots[2], pass_to).wait_recv()
            fw = forward(2, i, c)
            fw.start()
            started.append(fw)
        for i in range(n_p):
            for k in range(3):
                forward(k, i, 1 - c).wait_recv()
        for cp in started:
            cp.wait_send()

    anyspec = pl.BlockSpec(memory_space=pl.ANY)
    sems = pltpu.SemaphoreType.DMA((3 * n_p,))
    return pl.pallas_call(
        body,
        name=name,
        in_specs=[anyspec] * n,
        out_specs=[anyspec] * n,
        out_shape=[jax.ShapeDtypeStruct(b.shape, b.dtype) for b in bufs],
        input_output_aliases={a: a for a in range(n)},
        scratch_shapes=[sems, sems, sems, sems],
    )(*bufs)


def _swap_plan(n_slabs):
    def plan(refs):
        x, y, c = _coords()
        out, k = [], 0
        for i, n in enumerate(n_slabs):
            g, land = refs[2 * i], refs[2 * i + 1]
            for p in range(n):
                out.append((k, (g.at[p, 1 - c], land.at[p], (x, y, 1 - c), None), (land.at[p], None)))
                k += 1
        return out
    return plan


def _is_one_of(chip, dests):
    hit = chip == dests[0]
    for d in dests[1:]:
        hit = hit | (chip == d)
    return hit


def _slab_of(chip, dests):
    return sum(j * (chip == d).astype(jnp.int32) for j, d in enumerate(dests))


def _scatter_plan(dest_sets):
    def plan(refs):
        x, y, c = _coords()
        me = 2 * x + y
        out = []
        for k, (px, py) in enumerate(_other_chips(x, y)):
            peer = 2 * px + py
            for i, dests in enumerate(dest_sets):
                cs, land = refs[2 * i], refs[2 * i + 1]
                everyone = len(dests) == N_CHIPS
                send = (cs.at[_slab_of(peer, dests)], land.at[k], (px, py, c),
                        None if everyone else _is_one_of(peer, dests))
                recv = (land.at[k], None if everyone else _is_one_of(me, dests))
                out.append((k * len(dest_sets) + i, send, recv))
        return out
    return plan


def _allgather_plan():
    def plan(refs):
        x, y, c = _coords()
        src, land = refs
        me = 4 * x + 2 * y + c
        out = []
        for r in range(1, 8):
            px = 1 - x if r & 4 else x
            py = 1 - y if r & 2 else y
            pc = 1 - c if r & 1 else c
            out.append((r - 1, (src, land.at[me], (px, py, pc), None), (land.at[4 * px + 2 * py + pc], None)))
        return out
    return plan


def _sum_gathered(own, land, name):
    def body(own_ref, land_ref, o_ref):
        x, y, c = _coords()
        me = 4 * x + 2 * y + c
        acc = jnp.zeros(own_ref.shape, F32)
        for d in range(8):
            acc = acc + (land_ref[d] + jnp.where(me == d, own_ref[...], 0.0))
        o_ref[...] = acc

    return pl.pallas_call(
        body,
        name=name,
        out_shape=jax.ShapeDtypeStruct(own.shape, F32),
        compiler_params=_params(),
    )(own, land)


def _join_halves(bufs, n_chunks, name):
    n = len(bufs)
    pieces = [(a, rows) for a in range(n) for rows in _chunks(bufs[a].shape[1], n_chunks[a])]
    n_p = len(pieces)

    def body(*refs):
        outs = refs[n:2 * n]
        send_sems, recv_sems = refs[2 * n:]
        x, y, c = _coords()

        def copy(i, half):
            a, rows = pieces[i]
            return pltpu.make_async_remote_copy(
                src_ref=outs[a].at[half, rows], dst_ref=outs[a].at[half, rows], send_sem=send_sems.at[i],
                recv_sem=recv_sems.at[i], device_id=(x, y, 1 - c), device_id_type=MESH)

        sends = [copy(i, c) for i in range(n_p)]
        for cp in sends:
            cp.start()
        for i in range(n_p):
            copy(i, 1 - c).wait_recv()
        for cp in sends:
            cp.wait_send()

    anyspec = pl.BlockSpec(memory_space=pl.ANY)
    sems = pltpu.SemaphoreType.DMA((n_p,))
    return pl.pallas_call(
        body,
        name=name,
        in_specs=[anyspec] * n,
        out_specs=[anyspec] * n,
        out_shape=[jax.ShapeDtypeStruct(b.shape, b.dtype) for b in bufs],
        input_output_aliases={a: a for a in range(n)},
        scratch_shapes=[sems, sems],
    )(*bufs)


def _row_tile(rows, cap):
    t = cap
    while rows % t:
        t //= 2
    return t


def _add_my_half(g, r, name):
    n_slabs, _, R, C = g.shape
    tr = _row_tile(R, 256)

    def body(c_ref, g_ref, r_ref, o_ref):
        o_ref[...] = (g_ref[0] + r_ref[...]).astype(o_ref.dtype)

    return pl.pallas_call(
        body,
        name=name,
        grid_spec=pltpu.PrefetchScalarGridSpec(
            num_scalar_prefetch=1,
            grid=(n_slabs, R // tr),
            in_specs=[pl.BlockSpec((1, 1, tr, C), lambda p, i, c_ref: (p, c_ref[0], i, 0)),
                      pl.BlockSpec((1, tr, C), lambda p, i, c_ref: (p, i, 0))],
            out_specs=pl.BlockSpec((1, tr, C), lambda p, i, c_ref: (p, i, 0)),
        ),
        out_shape=jax.ShapeDtypeStruct(r.shape, jnp.bfloat16),
        compiler_params=_params(("parallel", "parallel")),
    )(lax.axis_index("c").reshape(1).astype(jnp.int32), g, r)


def _sum_slabs(own, got, name):
    _, R, C = own.shape
    tr = _row_tile(R, 256)

    def body(s_ref, own_ref, got_ref, o_ref):
        o_ref[0] = ((own_ref[0].astype(F32) + got_ref[0].astype(F32)) + got_ref[1].astype(F32)) + got_ref[2].astype(F32)

    xi, yi, ci = _coords()
    return pl.pallas_call(
        body,
        name=name,
        grid_spec=pltpu.PrefetchScalarGridSpec(
            num_scalar_prefetch=1,
            grid=(R // tr,),
            in_specs=[pl.BlockSpec((1, tr, C), lambda i, s: (s[0], i, 0)),
                      pl.BlockSpec((3, tr, C), lambda i, s: (0, i, 0))],
            out_specs=pl.BlockSpec((1, tr, C), lambda i, s: (s[1], i, 0)),
        ),
        out_shape=jax.ShapeDtypeStruct((2, R, C), F32),
        compiler_params=_params(("parallel",)),
    )(jnp.stack([2 * xi + yi, ci]).astype(jnp.int32), own, got)


def _sum_parts(owns, gots, dest_sets, name):
    n = len(owns)
    _, R, C = owns[0].shape
    tr = _row_tile(R, 256)

    def body(s_ref, *refs):
        o_ref = refs[-1]
        total = jnp.zeros((tr, C), F32)
        for i in range(n):
            total = total + jnp.where(s_ref[2 + 2 * i] == 1, refs[i][0].astype(F32), 0.0)
        for i in range(n):
            got = refs[n + i]
            total = ((total + got[0].astype(F32)) + got[1].astype(F32)) + got[2].astype(F32)
        o_ref[0] = total

    xi, yi, ci = _coords()
    me = 2 * xi + yi
    scalars = [ci, ci]
    for dests in dest_sets:
        scalars += [_is_one_of(me, dests).astype(jnp.int32), _slab_of(me, dests)]
    own_spec = lambda i: pl.BlockSpec((1, tr, C), lambda r, s: (s[3 + 2 * i], r, 0))
    return pl.pallas_call(
        body,
        name=name,
        grid_spec=pltpu.PrefetchScalarGridSpec(
            num_scalar_prefetch=1,
            grid=(R // tr,),
            in_specs=[own_spec(i) for i in range(n)] + [pl.BlockSpec((3, tr, C), lambda r, s: (0, r, 0))] * n,
            out_specs=pl.BlockSpec((1, tr, C), lambda r, s: (s[0], r, 0)),
        ),
        out_shape=jax.ShapeDtypeStruct((2, R, C), F32),
        compiler_params=_params(("parallel",)),
    )(jnp.stack(scalars).astype(jnp.int32), *owns, *gots)


def _sum_rows8(g, m_per, name):
    n = g.shape[1]

    def body(g_ref, o_ref):
        acc = g_ref[0:m_per, :]
        for k in range(1, 8):
            acc = acc + g_ref[k * m_per:(k + 1) * m_per, :]
        o_ref[...] = acc

    return pl.pallas_call(
        body,
        name=name,
        out_shape=jax.ShapeDtypeStruct((m_per, n), F32),
        compiler_params=_params(),
    )(g)


def _adamw_math(w, g, m, v):
    m = ADAM_B1 * m + (1.0 - ADAM_B1) * g
    v = ADAM_B2 * v + (1.0 - ADAM_B2) * (g * g)
    m_hat = m / (1.0 - ADAM_B1 ** ADAM_STEP)
    v_hat = v / (1.0 - ADAM_B2 ** ADAM_STEP)
    delta = -ADAM_LR * (m_hat / (jnp.sqrt(v_hat) + ADAM_EPS) + ADAM_WD * w)
    return delta, m, v


def _adamw_big(w, g, m, v, name):
    R, C = w.shape
    tr = min(128, R)

    def body(w_ref, g_ref, m_ref, v_ref, g_out, d_out, m_out, v_out):
        g = g_ref[...]
        d, mn, vn = _adamw_math(w_ref[...], g, m_ref[...], v_ref[...])
        g_out[...] = g
        d_out[...] = d
        m_out[...] = mn
        v_out[...] = vn

    spec = pl.BlockSpec((tr, C), lambda i: (i, 0))
    return pl.pallas_call(
        body,
        name=name,
        grid=(R // tr,),
        in_specs=[spec] * 4,
        out_specs=[spec] * 4,
        out_shape=[jax.ShapeDtypeStruct((R, C), F32)] * 4,
        compiler_params=_params(("parallel",)),
    )(w, g, m, v)


def _adamw_small(ws, gs, ms, vs, name):
    n = len(ws)

    def body(*refs):
        for a in range(n):
            d, mn, vn = _adamw_math(refs[a][...], refs[n + a][...], refs[2 * n + a][...], refs[3 * n + a][...])
            refs[4 * n + a][...] = d
            refs[5 * n + a][...] = mn
            refs[6 * n + a][...] = vn

    shapes = [jax.ShapeDtypeStruct(w.shape, F32) for w in ws]
    outs = pl.pallas_call(
        body,
        name=name,
        out_shape=shapes * 3,
        compiler_params=_params(),
    )(*ws, *gs, *ms, *vs)
    return outs[:n], outs[n:2 * n], outs[2 * n:]


def _to_blockdiag(w):
    per = CW // LRU_BW
    w4 = w.reshape(N_CT, per, LRU_BW, LRU_BW)
    eye = jnp.eye(per, dtype=w.dtype)
    return (w4[:, :, :, None, :] * eye[None, :, None, :, None]).reshape(N_CT, CW, CW)


def _from_blockdiag(g):
    per = CW // LRU_BW
    g5 = g.reshape(N_CT, per, LRU_BW, per, LRU_BW)
    return jnp.stack([g5[:, b, :, b, :] for b in range(per)], axis=1).reshape(LRU_BLOCKS, LRU_BW, LRU_BW)


def _local_grads(x2d, tgt2d, B, S, g_in, w_all, conv_w, conv_b, gate_x_w, gate_x_b, gate_a_w, gate_a_b, lam, gain,
                 proj_weights, g_fin, reduce, deps=()):
    wx_bd = _c(_to_blockdiag(gate_x_w))
    wa_bd = _c(_to_blockdiag(gate_a_w))
    tables = _retention_tables(S)
    gain3 = gain.reshape(HEADS, 1, DK)

    proj, ht = _inproj_fwd(x2d, g_in, w_all, deps)
    hlru, ya = _lru_fwd(proj, conv_w, conv_b, wx_bd, wa_bd, gate_x_b, gate_a_b, lam, B, S)
    o_pre, yb, states = _ret_fwd(proj, tables, gain3, B, S)
    wpa, wpb, wout = proj_weights(yb)
    loss, dx2, dya, dyb, dm, dgf, gw_proj = _mid(ya, yb, proj, x2d, tgt2d, wpa, wpb, wout, g_fin)
    g3 = _inproj_bwd_dw(ht, [dm], "inproj_bwd_dw_m")
    deps = reduce.m_ready(gw_proj, g3)
    dr, dgain = _ret_bwd(dyb, o_pre, proj, states, tables, gain3, B, S, deps)
    deps = reduce.ret_done(dr)
    g12 = _inproj_bwd_dw(ht, [dr], "inproj_bwd_dw_r", deps)
    deps = reduce.r_ready(g12)
    dxa, dga, dcw, dcb, dwx_bd, dwa_bd, dbx, dba, dlam = _lru_bwd(
        dya, proj, hlru, conv_w, conv_b, wx_bd, wa_bd, gate_x_b, gate_a_b, lam, B, S, deps)
    deps = reduce.lru_done(dxa)
    small = dict(conv_w=dcw, conv_b=dcb, gate_x_w=_from_blockdiag(dwx_bd), gate_x_b=dbx,
                 gate_a_w=_from_blockdiag(dwa_bd), gate_a_b=dba, lru_lambda=dlam, gn_gain=dgain.reshape(HEADS, DK),
                 norm_final=dgf)
    loss_rows = jnp.broadcast_to(loss, (SUBLANES, LANES))
    deps = deps + reduce.small_ready(jnp.concatenate([_pack_small(small), loss_rows], axis=0))
    g0 = _inproj_bwd_dw(ht, [dxa, dga], "inproj_bwd_dw_a", deps)
    deps = reduce.a_ready(g0)
    n_tiles = x2d.shape[0] // min(DX_TILE, x2d.shape[0])
    grad_x, dgin = _inproj_bwd_dx([dxa, dga, dr, dm], w_all, x2d, dx2, g_in, 0, n_tiles, None, "inproj_bwd_dx", deps)
    return grad_x, dgin


ALL_CHIPS = (0, 1, 2, 3)


class _GradReduce:
    def __init__(self, proj_done):
        self.pending = {}
        self.proj_done = proj_done

    def _start(self, key, bufs, plan, n_copies, name):
        send_sems, recv_sems, bufs, token = _copies_start(bufs, plan, n_copies, name + "_start")
        self.pending[key] = (send_sems, recv_sems, bufs, plan, name + "_wait")
        return (token,)

    def _finish(self, key, after):
        send_sems, recv_sems, bufs, plan, name = self.pending.pop(key)
        return _copies_wait(send_sems, recv_sems, bufs, after, plan, name)

    def _swap(self, key, parts):
        bufs = []
        for g in parts:
            bufs += [g, lax.empty((g.shape[0],) + g.shape[2:], F32)]
        n_slabs = [g.shape[0] for g in parts]
        return self._start(key, bufs, _swap_plan(n_slabs), sum(n_slabs), "swap_" + key)

    def _chip_sums(self, key, after):
        bufs = self._finish(key, after)
        return [_add_my_half(bufs[2 * i], bufs[2 * i + 1], "chip_sum_%s%d" % (key, i)) for i in range(len(bufs) // 2)]

    def _scatter(self, key, sums, dest_sets):
        bufs = []
        for cs in sums:
            bufs += [cs, jnp.zeros((3,) + cs.shape[1:], cs.dtype)]
        return self._start(key, bufs, _scatter_plan(dest_sets), 3 * len(sums), "scatter_" + key)

    def m_ready(self, gw_proj, g3):
        rows = gw_proj.shape[2] * gw_proj.shape[3]
        return self._swap("m", [gw_proj.reshape(N_CHIPS, 2, rows, D_MODEL), g3])

    def ret_done(self, after):
        return self._scatter("sm", self._chip_sums("m", after), [ALL_CHIPS, (3,)])

    def r_ready(self, g12):
        return self._swap("r", [g12])

    def lru_done(self, after):
        return self._scatter("sr", self._chip_sums("r", after), [(1, 2)])

    def small_ready(self, packed):
        land = jnp.zeros((8,) + packed.shape, F32)
        return self._start("small", [packed, land], _allgather_plan(), 7, "gather_small")

    def a_ready(self, g0):
        (token,) = self._swap("a", [g0])
        csp, gotp, cs3, got3 = self._finish("sm", token)
        self.m_piece = (cs3, got3)
        (g_proj,) = _join_halves([_sum_slabs(csp, gotp, "sum_w_proj")], [4], "join_halves_proj")
        after = self.proj_done(g_proj)
        return self._scatter("sa", self._chip_sums("a", after), [(0,)])

    def finish(self, after):
        small_sum = _sum_gathered(*self._finish("small", after), "sum_small_grads")
        cs3, got3 = self.m_piece
        cs12, got12 = self._finish("sr", after)
        cs0, got0 = self._finish("sa", after)
        half_in = _sum_parts([cs3, cs12, cs0], [got3, got12, got0], [(3,), (1, 2), (0,)], "sum_w_in")
        return small_sum, _join_halves([half_in], [8], "join_halves")[0]


_SMALL = ("gate_x_w", "gate_a_w", "conv_w", "conv_b", "gate_x_b", "gate_a_b", "lru_lambda", "gn_gain", "norm_final")
_SMALL_SHAPES = dict(gate_x_w=(LRU_BLOCKS, LRU_BW, LRU_BW), gate_a_w=(LRU_BLOCKS, LRU_BW, LRU_BW),
                     norm_in=(1, D_MODEL), conv_w=(CONV, D_MODEL), conv_b=(1, D_MODEL), gate_x_b=(1, D_MODEL),
                     gate_a_b=(1, D_MODEL), lru_lambda=(1, D_MODEL), gn_gain=(HEADS, DK), norm_final=(1, D_MODEL))


def _pack_small(small):
    return jnp.concatenate([small[k].reshape(-1, 128) for k in _SMALL], axis=0)


def _unpack_small(packed):
    out, r = {}, 0
    for k in _SMALL:
        shape = _SMALL_SHAPES[k]
        rows = 1
        for s in shape:
            rows *= s
        rows //= 128
        out[k] = packed[r:r + rows].reshape(shape)
        r += rows
    return out


def kernel(x, norm_in, w_in, conv_w, conv_b, gate_x_w, gate_x_b, gate_a_w, gate_a_b, lru_lambda, gn_gain, w_proj_a, w_proj_b, w_out, norm_final, loss_target, m_norm_in, m_w_in, m_conv_w, m_conv_b, m_gate_x_w, m_gate_x_b, m_gate_a_w, m_gate_a_b, m_lru_lambda, m_gn_gain, m_w_proj_a, m_w_proj_b, m_w_out, m_norm_final, v_norm_in, v_w_in, v_conv_w, v_conv_b, v_gate_x_w, v_gate_x_b, v_gate_a_w, v_gate_a_b, v_lru_lambda, v_gn_gain, v_w_proj_a, v_w_proj_b, v_w_out, v_norm_final):
    B, S, _ = x.shape
    T = B * S
    xi, yi, ci = _coords()
    chip = 2 * xi + yi

    cshard = D_MODEL // N_CHIPS
    mine = _cast_into_slot([w_in[0].reshape(2, D_MODEL // 2, 2 * D_MODEL)]
                           + [w[0].reshape(2, cshard // 2, D_MODEL) for w in (w_proj_a, w_proj_b, w_out)],
                           "cast_weights")
    plan = _gather_plan(3)
    s_sems, r_sems, pbufs, token = _copies_start(mine[1:], plan, 9, "gather_proj_start")
    w_all = _gather_chips(mine[:1], [8], "gather_weights")[0].reshape(N_CHIPS, D_MODEL, 2 * D_MODEL)

    def proj_weights(after):
        got = _copies_wait(s_sems, r_sems, pbufs, after, plan, "gather_proj_wait")
        return [b.reshape(D_MODEL, D_MODEL) for b in got]

    gshard = DK // N_CHIPS
    tiny = jnp.concatenate([conv_w[0], jnp.zeros((4, cshard), F32), jnp.pad(gn_gain[0], ((0, 4), (0, cshard - gshard)))],
                           axis=0)
    tiny_all = _all_gather8(tiny, "gather_small_weights").reshape(N_CHIPS, 2, 16, cshard)[:, 0]
    conv_w_full = jnp.transpose(tiny_all[:, 0:CONV, :], (1, 0, 2)).reshape(CONV, D_MODEL)
    gain_full = jnp.transpose(tiny_all[:, 8:8 + HEADS, :gshard], (1, 0, 2)).reshape(HEADS, DK)

    weights = dict(norm_in=norm_in, w_in=w_in, conv_w=conv_w, conv_b=conv_b, gate_x_w=gate_x_w, gate_x_b=gate_x_b,
                   gate_a_w=gate_a_w, gate_a_b=gate_a_b, lru_lambda=lru_lambda, gn_gain=gn_gain, w_proj_a=w_proj_a,
                   w_proj_b=w_proj_b, w_out=w_out, norm_final=norm_final)
    ms = dict(norm_in=m_norm_in, w_in=m_w_in, conv_w=m_conv_w, conv_b=m_conv_b, gate_x_w=m_gate_x_w,
              gate_x_b=m_gate_x_b, gate_a_w=m_gate_a_w, gate_a_b=m_gate_a_b, lru_lambda=m_lru_lambda, gn_gain=m_gn_gain,
              w_proj_a=m_w_proj_a, w_proj_b=m_w_proj_b, w_out=m_w_out, norm_final=m_norm_final)
    vs = dict(norm_in=v_norm_in, w_in=v_w_in, conv_w=v_conv_w, conv_b=v_conv_b, gate_x_w=v_gate_x_w,
              gate_x_b=v_gate_x_b, gate_a_w=v_gate_a_w, gate_a_b=v_gate_a_b, lru_lambda=v_lru_lambda, gn_gain=v_gn_gain,
              w_proj_a=v_w_proj_a, w_proj_b=v_w_proj_b, w_out=v_w_out, norm_final=v_norm_final)
    names = list(weights)
    grads, delta, new_m, new_v = {}, {}, {}, {}

    def update_big(k, g):
        shp = weights[k].shape
        two = lambda a: a.reshape(shp[1], shp[2])
        g, d, mn, vn = _adamw_big(two(weights[k]), g, two(ms[k]), two(vs[k]), "adamw_" + k)
        grads[k], delta[k], new_m[k], new_v[k] = g.reshape(shp), d.reshape(shp), mn.reshape(shp), vn.reshape(shp)
        return d

    def proj_done(g_proj):
        g_pr = g_proj.reshape(2, 3, D_MODEL // (2 * N_CHIPS), D_MODEL)
        for i, k in enumerate(("w_proj_a", "w_proj_b", "w_out")):
            last = update_big(k, g_pr[:, i].reshape(cshard, D_MODEL))
        return last

    reduce = _GradReduce(proj_done)
    grad_x, dgin = _local_grads(
        x.reshape(T, D_MODEL), loss_target.reshape(T, D_MODEL), B, S, norm_in, w_all, conv_w_full, conv_b,
        gate_x_w[0], gate_x_b, gate_a_w[0], gate_a_b, lru_lambda, gain_full, proj_weights,
        norm_final.reshape(1, D_MODEL), reduce, deps=(token,))

    g_norm_in = _sum_rows8(_all_gather8(dgin.reshape(8, 128), "gather_norm_in_grad"), 8, "sum_norm_in_grad")
    small_sum, g_in_full = reduce.finish(g_norm_in)
    update_big("w_in", g_in_full.reshape(D_MODEL, 2 * D_MODEL))
    loss = small_sum[small_sum.shape[0] - SUBLANES, 0]

    gsm = _unpack_small(small_sum)
    gsm["norm_in"] = g_norm_in
    gsm["conv_w"] = lax.dynamic_slice_in_dim(gsm["conv_w"], chip * cshard, cshard, axis=1)
    gsm["gn_gain"] = lax.dynamic_slice_in_dim(gsm["gn_gain"], chip * gshard, gshard, axis=1)
    smalls = [k for k in names if k not in delta]

    def view(a):
        return a.reshape(1, -1) if a.ndim == 1 else (a.reshape(a.shape[1:]) if a.ndim > 2 else a)

    ds, mns, vns = _adamw_small([view(weights[k]) for k in smalls], [gsm[k].reshape(view(weights[k]).shape) for k in smalls],
                                [view(ms[k]) for k in smalls], [view(vs[k]) for k in smalls], "adamw_small")
    for k, d, mn, vn in zip(smalls, ds, mns, vns):
        shp = weights[k].shape
        grads[k], delta[k], new_m[k], new_v[k] = gsm[k].reshape(shp), d.reshape(shp), mn.reshape(shp), vn.reshape(shp)

    return (loss, grad_x.reshape(B, S, D_MODEL), *[grads[k] for k in names], *[delta[k] for k in names],
            *[new_m[k] for k in names], *[new_v[k] for k in names])
```

```python
import functools

import jax
import jax.numpy as jnp
from jax import lax
from jax.experimental import pallas as pl
from jax.experimental.pallas import tpu as pltpu

F32 = jnp.float32
_MXU = jnp.bfloat16

D_MODEL = 1024
N_GROUPS = 8
HEADS = 4
DK = 256
CHUNK = 128
CONV = 4
LRU_BLOCKS = 16
LRU_BW = 64
LRU_C = 8.0
ROPE_THETA = 10000.0
EPS = 1e-6
CW = 256
N_CT = D_MODEL // CW
N_CHIPS = 4
MESH = pl.DeviceIdType.MESH

ADAM_LR = 0.001
ADAM_B1 = 0.9
ADAM_B2 = 0.999
ADAM_EPS = 1e-08
ADAM_WD = 0.01
ADAM_STEP = 10

VMEM_LIMIT = 56 * 1024 * 1024


def _c(v):
    return v.astype(_MXU)


def _dot(a, b):
    return lax.dot_general(a, b, (((1,), (0,)), ((), ())), preferred_element_type=F32)


def _dot_nt(a, b):
    return lax.dot_general(a, b, (((1,), (1,)), ((), ())), preferred_element_type=F32)


def _dot_tn(a, b):
    return lax.dot_general(a, b, (((0,), (0,)), ((), ())), preferred_element_type=F32)


def _sigmoid(z):
    return 0.5 * jnp.tanh(0.5 * z) + 0.5


ANY_SPEC = pl.BlockSpec(memory_space=pl.ANY)


def _after(body, n_in, deps):
    n_deps = len(deps)

    def wrapped(*refs):
        return body(*refs[:n_in], *refs[n_in + n_deps:])

    return wrapped


def _params(sem=None):
    if sem is None:
        return pltpu.CompilerParams(vmem_limit_bytes=VMEM_LIMIT)
    return pltpu.CompilerParams(vmem_limit_bytes=VMEM_LIMIT, dimension_semantics=sem)


def _inproj_fwd(x2d, g_in, w_all, deps=()):
    T = x2d.shape[0]
    tm = min(1024, T)
    n_i = T // tm

    def body(*refs):
        x_ref, g_ref, w_ref = refs[:3]
        proj_ref, ht_ref, h_all = refs[-3:]
        i = pl.program_id(1)
        rows = pl.ds(pl.multiple_of(i * tm, tm), tm)

        @pl.when(pl.program_id(0) == 0)
        def _():
            x = x_ref[...]
            r = lax.rsqrt(jnp.mean(x * x, axis=-1, keepdims=True) + EPS)
            h = x * r * g_ref[...]
            h_all[rows, :] = h.astype(h_all.dtype)
            ht_ref[...] = h.T.astype(ht_ref.dtype)

        proj_ref[...] = _dot(h_all[rows, :], w_ref[0])

    first = lambda j, i: jnp.where(j == 0, i, n_i - 1)
    return pl.pallas_call(
        body,
        name="inproj_fwd",
        grid=(N_GROUPS, n_i),
        in_specs=[
            pl.BlockSpec((tm, D_MODEL), lambda j, i: (first(j, i), 0)),
            pl.BlockSpec((1, D_MODEL), lambda j, i: (0, 0)),
            pl.BlockSpec((1, D_MODEL, D_MODEL), lambda j, i: (j // 2, 0, j % 2)),
        ] + [pl.BlockSpec(memory_space=pl.ANY)] * len(deps),
        out_specs=[
            pl.BlockSpec((tm, D_MODEL), lambda j, i: (i, j)),
            pl.BlockSpec((D_MODEL, tm), lambda j, i: (0, first(j, i))),
        ],
        out_shape=[
            jax.ShapeDtypeStruct((T, N_GROUPS * D_MODEL), F32),
            jax.ShapeDtypeStruct((D_MODEL, T), _MXU),
        ],
        scratch_shapes=[pltpu.VMEM((T, D_MODEL), _MXU)],
        compiler_params=_params(("arbitrary", "arbitrary")),
    )(x2d, g_in, w_all, *deps)


def _scan_fwd(a, u):
    n = a.shape[0]
    row = lax.broadcasted_iota(jnp.int32, a.shape, 0)
    s = 1
    while s < n:
        m = row >= s
        u = u + a * jnp.where(m, pltpu.roll(u, s, 0), 0.0)
        a = a * jnp.where(m, pltpu.roll(a, s, 0), 1.0)
        s *= 2
    return a, u


def _scan_bwd(b, g):
    n = b.shape[0]
    row = lax.broadcasted_iota(jnp.int32, b.shape, 0)
    s = 1
    while s < n:
        m = row < n - s
        g = g + b * jnp.where(m, pltpu.roll(g, n - s, 0), 0.0)
        b = b * jnp.where(m, pltpu.roll(b, n - s, 0), 1.0)
        s *= 2
    return b, g


LANES = 128
SUBLANES = 8


def _scan_scratch(tc):
    by_lanes = pltpu.VMEM((CW // LANES, tc, LANES), F32)
    return [by_lanes, by_lanes, pltpu.VMEM((tc // SUBLANES, CW), F32), pltpu.VMEM((tc, CW), F32)]


def _scan_tile(a, u, edge, la_ref, lh_ref, c_ref, dst_ref, reverse):
    n, w = a.shape
    groups = n // SUBLANES
    a3 = a.reshape(groups, SUBLANES, w)
    u3 = u.reshape(groups, SUBLANES, w)
    row = lax.broadcasted_iota(jnp.int32, a3.shape, 1)
    for s in (1, 2, 4):
        m = (row < SUBLANES - s) if reverse else (row >= s)
        shift = SUBLANES - s if reverse else s
        u3 = u3 + a3 * jnp.where(m, pltpu.roll(u3, shift, 1), 0.0)
        a3 = a3 * jnp.where(m, pltpu.roll(a3, shift, 1), 1.0)
    al = a3.reshape(n, w)
    hl = u3.reshape(n, w)
    blocks = w // LANES
    for q in range(blocks):
        la_ref[q] = al[:, q * LANES:(q + 1) * LANES]
        lh_ref[q] = hl[:, q * LANES:(q + 1) * LANES]
    ends = pl.ds(0 if reverse else SUBLANES - 1, groups, stride=SUBLANES)
    end_a = jnp.concatenate([la_ref.at[q][ends, :] for q in range(blocks)], axis=-1)
    end_h = jnp.concatenate([lh_ref.at[q][ends, :] for q in range(blocks)], axis=-1)
    prod, part = (_scan_bwd if reverse else _scan_fwd)(end_a, end_h)
    total = part + prod * edge
    g_row = lax.broadcasted_iota(jnp.int32, total.shape, 0)
    if reverse:
        c_ref[...] = jnp.where(g_row == groups - 1, edge, pltpu.roll(total, groups - 1, 0))
    else:
        c_ref[...] = jnp.where(g_row == 0, edge, pltpu.roll(total, 1, 0))
    for g in range(groups):
        rows = slice(g * SUBLANES, (g + 1) * SUBLANES)
        for q in range(blocks):
            cols = slice(q * LANES, (q + 1) * LANES)
            dst_ref[rows, cols] = lh_ref[q, rows, :] + la_ref[q, rows, :] * c_ref[g:g + 1, cols]


def _softplus_neg(lam):
    z = -lam
    return jnp.maximum(z, 0.0) + jnp.log1p(jnp.exp(-jnp.abs(z)))


def _lru_gates(xc, wx_ref, wa_ref, bx_ref, ba_ref, lam_ref):
    xcb = _c(xc)
    i_t = _sigmoid(_dot(xcb, wx_ref[0]) + bx_ref[...])
    r_t = _sigmoid(_dot(xcb, wa_ref[0]) + ba_ref[...])
    sp = _softplus_neg(lam_ref[...])
    log_a = (-LRU_C) * r_t * sp
    a = jnp.exp(log_a)
    mult = jnp.sqrt(1.0 - a * a)
    return xcb, i_t, r_t, sp, a, mult


def _conv_from_ext(ext_ref, xa, cw_ref, cb_ref, tc):
    return (cb_ref[...] + cw_ref[3:4, :] * xa + cw_ref[2:3, :] * ext_ref[7:7 + tc, :]
            + cw_ref[1:2, :] * ext_ref[6:6 + tc, :] + cw_ref[0:1, :] * ext_ref[5:5 + tc, :])


def _lru_fwd(proj, conv_w, conv_b, wx_bd, wa_bd, bx, ba, lam, B, S):
    T = B * S
    tc = min(256, S)
    nt = S // tc
    h8 = tc // 8

    def body(xa_ref, halo_ref, ga_ref, cw_ref, cb_ref, wx_ref, wa_ref, bx_ref, ba_ref, lam_ref,
             h_ref, ya_ref, ext_ref, carry_ref, la_ref, lh_ref, c_ref):
        t = pl.program_id(2)

        @pl.when(t == 0)
        def _():
            carry_ref[...] = jnp.zeros_like(carry_ref)

        xa = xa_ref[...]
        ext_ref[0:8, :] = jnp.where(t == 0, 0.0, halo_ref[...])
        ext_ref[8:8 + tc, :] = xa
        xc = _conv_from_ext(ext_ref, xa, cw_ref, cb_ref, tc)
        _, i_t, _, _, a, mult = _lru_gates(xc, wx_ref, wa_ref, bx_ref, ba_ref, lam_ref)
        u = mult * (i_t * xc)
        _scan_tile(a, u, carry_ref[7:8, :], la_ref, lh_ref, c_ref, h_ref, False)
        h = h_ref[...]
        carry_ref[...] = h[tc - 8:tc, :]
        ga = ga_ref[...]
        ya_ref[...] = (ga * _sigmoid(ga) * h).astype(ya_ref.dtype)

    row = lambda b, t: b * nt + t
    vec = pl.BlockSpec((1, CW), lambda b, c, t: (0, c))
    mat = pl.BlockSpec((1, CW, CW), lambda b, c, t: (c, 0, 0))
    return pl.pallas_call(
        body,
        name="lru_fwd",
        grid=(B, N_CT, nt),
        in_specs=[
            pl.BlockSpec((tc, CW), lambda b, c, t: (row(b, t), c)),
            pl.BlockSpec((8, CW), lambda b, c, t: (jnp.maximum(row(b, t) * h8 - 1, 0), c)),
            pl.BlockSpec((tc, CW), lambda b, c, t: (row(b, t), N_CT + c)),
            pl.BlockSpec((CONV, CW), lambda b, c, t: (0, c)),
            vec, mat, mat, vec, vec, vec,
        ],
        out_specs=[
            pl.BlockSpec((tc, CW), lambda b, c, t: (row(b, t), c)),
            pl.BlockSpec((tc, CW), lambda b, c, t: (row(b, t), c)),
        ],
        out_shape=[
            jax.ShapeDtypeStruct((T, D_MODEL), F32),
            jax.ShapeDtypeStruct((T, D_MODEL), _MXU),
        ],
        scratch_shapes=[pltpu.VMEM((tc + 8, CW), F32), pltpu.VMEM((8, CW), F32)] + _scan_scratch(tc)[:3],
        compiler_params=_params(("parallel", "parallel", "arbitrary")),
    )(proj, proj, proj, conv_w, conv_b, wx_bd, wa_bd, bx, ba, lam)


def _lru_bwd(dya, proj, hlru, conv_w, conv_b, wx_bd, wa_bd, bx, ba, lam, B, S, deps=()):
    T = B * S
    tc = min(256, S)
    nt = S // tc
    h8 = tc // 8

    def body(dya_ref, xa_ref, xhalo_ref, ga_ref, h_ref, hhalo_ref, cw_ref, cb_ref, wx_ref, wa_ref, bx_ref, ba_ref,
             lam_ref, dxa_ref, dga_ref, dcw_ref, dcb_ref, dwx_ref, dwa_ref, dbx_ref, dba_ref, dlam_ref,
             ext_ref, ext2_ref, carry_ref, dhalo_ref, la_ref, lh_ref, c_ref, dh_ref):
        b = pl.program_id(1)
        t = pl.program_id(2)
        tt = nt - 1 - t

        @pl.when(t == 0)
        def _():
            carry_ref[...] = jnp.zeros_like(carry_ref)
            dhalo_ref[...] = jnp.zeros_like(dhalo_ref)

        @pl.when((t == 0) & (b == 0))
        def _():
            for r in (dcw_ref, dcb_ref, dwx_ref, dwa_ref, dbx_ref, dba_ref, dlam_ref):
                r[...] = jnp.zeros_like(r)

        xa = xa_ref[...]
        ext_ref[0:8, :] = jnp.where(tt == 0, 0.0, xhalo_ref[...])
        ext_ref[8:8 + tc, :] = xa
        xc = _conv_from_ext(ext_ref, xa, cw_ref, cb_ref, tc)
        xcb, i_t, r_t, sp, a, mult = _lru_gates(xc, wx_ref, wa_ref, bx_ref, ba_ref, lam_ref)

        h = h_ref[...]
        ga = ga_ref[...]
        dya_t = dya_ref[...]
        sg = _sigmoid(ga)
        dga_ref[...] = (dya_t * h * (sg * (1.0 + ga * (1.0 - sg)))).astype(dga_ref.dtype)
        dlru = dya_t * (ga * sg)

        row = lax.broadcasted_iota(jnp.int32, a.shape, 0)
        coef = jnp.where(row == tc - 1, 1.0, pltpu.roll(a, tc - 1, 0))
        _scan_tile(coef, dlru, carry_ref[0:1, :], la_ref, lh_ref, c_ref, dh_ref, True)
        dh = dh_ref[...]
        ext2_ref[0:tc, :] = a * dh
        carry_ref[...] = ext2_ref[0:8, :]

        ext2_ref[0:8, :] = jnp.where(tt == 0, 0.0, hhalo_ref[...])
        ext2_ref[8:8 + tc, :] = h
        hprev = ext2_ref[7:7 + tc, :]

        da = dh * hprev
        ix = i_t * xc
        dmult = dh * ix
        di = dh * mult * xc
        dxc = dh * mult * i_t
        dlog_a = da * a - dmult * (a * a) / mult
        dr = dlog_a * ((-LRU_C) * sp)
        dlam_ref[...] += jnp.sum(dlog_a * r_t, axis=0, keepdims=True) * (LRU_C * _sigmoid(-lam_ref[...]))
        dza = dr * r_t * (1.0 - r_t)
        dzx = di * i_t * (1.0 - i_t)
        dzab = _c(dza)
        dzxb = _c(dzx)
        dxc = dxc + _dot_nt(dzxb, wx_ref[0]) + _dot_nt(dzab, wa_ref[0])
        dwx_ref[0] += _dot_tn(xcb, dzxb)
        dwa_ref[0] += _dot_tn(xcb, dzab)
        dbx_ref[...] += jnp.sum(dzx, axis=0, keepdims=True)
        dba_ref[...] += jnp.sum(dza, axis=0, keepdims=True)

        dcb_ref[...] += jnp.sum(dxc, axis=0, keepdims=True)
        dcw_ref[3:4, :] += jnp.sum(dxc * xa, axis=0, keepdims=True)
        dcw_ref[2:3, :] += jnp.sum(dxc * ext_ref[7:7 + tc, :], axis=0, keepdims=True)
        dcw_ref[1:2, :] += jnp.sum(dxc * ext_ref[6:6 + tc, :], axis=0, keepdims=True)
        dcw_ref[0:1, :] += jnp.sum(dxc * ext_ref[5:5 + tc, :], axis=0, keepdims=True)
        ext2_ref[0:tc, :] = dxc
        ext2_ref[tc:tc + 8, :] = dhalo_ref[...]
        dxa = (cw_ref[3:4, :] * dxc + cw_ref[2:3, :] * ext2_ref[1:1 + tc, :]
               + cw_ref[1:2, :] * ext2_ref[2:2 + tc, :] + cw_ref[0:1, :] * ext2_ref[3:3 + tc, :])
        dxa_ref[...] = dxa.astype(dxa_ref.dtype)
        dhalo_ref[...] = ext2_ref[0:8, :]

    row_of = lambda b, t: b * nt + (nt - 1 - t)
    tile = lambda off: pl.BlockSpec((tc, CW), lambda c, b, t: (row_of(b, t), off + c))
    halo = pl.BlockSpec((8, CW), lambda c, b, t: (jnp.maximum(row_of(b, t) * h8 - 1, 0), c))
    vec = pl.BlockSpec((1, CW), lambda c, b, t: (0, c))
    mat = pl.BlockSpec((1, CW, CW), lambda c, b, t: (c, 0, 0))
    cwspec = pl.BlockSpec((CONV, CW), lambda c, b, t: (0, c))
    return pl.pallas_call(
        _after(body, 13, deps),
        name="lru_bwd",
        grid=(N_CT, B, nt),
        in_specs=[tile(0), tile(0), halo, tile(N_CT), tile(0), halo, cwspec, vec, mat, mat, vec, vec, vec]
        + [ANY_SPEC] * len(deps),
        out_specs=[tile(0), tile(0), cwspec, vec, mat, mat, vec, vec, vec],
        out_shape=[
            jax.ShapeDtypeStruct((T, D_MODEL), _MXU),
            jax.ShapeDtypeStruct((T, D_MODEL), _MXU),
            jax.ShapeDtypeStruct((CONV, D_MODEL), F32),
            jax.ShapeDtypeStruct((1, D_MODEL), F32),
            jax.ShapeDtypeStruct((N_CT, CW, CW), F32),
            jax.ShapeDtypeStruct((N_CT, CW, CW), F32),
            jax.ShapeDtypeStruct((1, D_MODEL), F32),
            jax.ShapeDtypeStruct((1, D_MODEL), F32),
            jax.ShapeDtypeStruct((1, D_MODEL), F32),
        ],
        scratch_shapes=[pltpu.VMEM((tc + 8, CW), F32), pltpu.VMEM((tc + 8, CW), F32),
                        pltpu.VMEM((8, CW), F32), pltpu.VMEM((8, CW), F32)] + _scan_scratch(tc),
        compiler_params=_params(("parallel", "arbitrary", "arbitrary")),
    )(dya, proj, proj, proj, hlru, hlru, conv_w, conv_b, wx_bd, wa_bd, bx, ba, lam, *deps)


def _retention_tables(S):
    half = DK // 2
    freqs = ROPE_THETA ** (-jnp.arange(half, dtype=F32) / half)
    ang = jnp.arange(S, dtype=F32)[:, None] * freqs[None, :]
    log_g = jnp.log1p(-(2.0 ** (-5.0 - jnp.arange(HEADS, dtype=F32))))
    idx = jnp.arange(CHUNK, dtype=F32)
    diff = idx[:, None] - idx[None, :]
    inner = jnp.where(diff >= 0, jnp.exp(jnp.maximum(diff, 0.0)[None] * log_g[:, None, None]), 0.0)
    cross = jnp.exp((idx[None, :] + 1.0) * log_g[:, None])[:, :, None]
    state = jnp.exp((CHUNK - 1.0 - idx[None, :]) * log_g[:, None])[:, :, None]
    gam = jnp.broadcast_to(jnp.exp(CHUNK * log_g)[:, None, None], (HEADS, 1, DK))
    return jnp.cos(ang), jnp.sin(ang), inner, cross, state, gam


def _rot(x, cos, sin):
    half = DK // 2
    x1, x2 = x[:, :half], x[:, half:]
    return jnp.concatenate([x1 * cos - x2 * sin, x1 * sin + x2 * cos], axis=-1)


def _rot_t(y, cos, sin):
    half = DK // 2
    y1, y2 = y[:, :half], y[:, half:]
    return jnp.concatenate([y1 * cos + y2 * sin, y2 * cos - y1 * sin], axis=-1)


def _groupnorm(o):
    mu = jnp.mean(o, axis=-1, keepdims=True)
    oc = o - mu
    rs = lax.rsqrt(jnp.mean(oc * oc, axis=-1, keepdims=True) + EPS)
    return oc * rs, rs


def _ret_specs(B, chunk_of):
    qkv = lambda g: pl.BlockSpec((B, CHUNK, D_MODEL), lambda c: (0, chunk_of(c), g))
    act = pl.BlockSpec((B, CHUNK, D_MODEL), lambda c: (0, chunk_of(c), 0))
    rope = pl.BlockSpec((CHUNK, DK // 2), lambda c: (chunk_of(c), 0))
    dmat = pl.BlockSpec((HEADS, CHUNK, CHUNK), lambda c: (0, 0, 0))
    dvec = pl.BlockSpec((HEADS, CHUNK, 1), lambda c: (0, 0, 0))
    hrow = pl.BlockSpec((HEADS, 1, DK), lambda c: (0, 0, 0))
    rst = pl.BlockSpec((1, B, HEADS, DK, DK), lambda c: (chunk_of(c), 0, 0, 0, 0))
    return qkv, act, rope, dmat, dvec, hrow, rst


def _ret_fwd(proj, tables, gain3, B, S):
    T = B * S
    nc = S // CHUNK
    cos, sin, dmat_t, cd_t, sd_t, gam_t = tables

    def body(q_ref, k_ref, v_ref, gb_ref, cos_ref, sin_ref, dm_ref, cd_ref, sd_ref, gam_ref, gain_ref,
             o_ref, yb_ref, rs_ref, state_ref):
        @pl.when(pl.program_id(0) == 0)
        def _():
            state_ref[...] = jnp.zeros_like(state_ref)

        cos_t, sin_t = cos_ref[...], sin_ref[...]
        for b, h in [(b, h) for b in range(B) for h in range(HEADS)]:
            cols = slice(h * DK, (h + 1) * DK)
            qb = _c(_rot(q_ref[b, :, cols], cos_t, sin_t))
            kb = _c(_rot(k_ref[b, :, cols], cos_t, sin_t) * (DK ** -0.5))
            v = v_ref[b, :, cols]
            state = state_ref[b, h]
            sb = _c(state)
            rs_ref[0, b, h] = sb
            scores = _dot_nt(qb, kb) * dm_ref[h]
            o = _dot(_c(scores), _c(v)) + _dot(qb, sb) * cd_ref[h]
            state_ref[b, h] = gam_ref[h] * state + _dot_tn(kb, _c(v * sd_ref[h]))
            o_ref[b, :, cols] = o
            n, _ = _groupnorm(o)
            gb = gb_ref[b, :, cols]
            yb_ref[b, :, cols] = (gb * _sigmoid(gb) * (n * gain_ref[h])).astype(yb_ref.dtype)

    qkv, act, rope, dmat, dvec, hrow, rst = _ret_specs(B, lambda c: c)
    proj3 = proj.reshape(B, S, proj.shape[1])
    o_pre, yb, states = pl.pallas_call(
        body,
        name="ret_fwd",
        grid=(nc,),
        in_specs=[qkv(2), qkv(3), qkv(4), qkv(5), rope, rope, dmat, dvec, dvec, hrow, hrow],
        out_specs=[act, act, rst],
        out_shape=[
            jax.ShapeDtypeStruct((B, S, D_MODEL), F32),
            jax.ShapeDtypeStruct((B, S, D_MODEL), _MXU),
            jax.ShapeDtypeStruct((nc, B, HEADS, DK, DK), _MXU),
        ],
        scratch_shapes=[pltpu.VMEM((B, HEADS, DK, DK), F32)],
        compiler_params=_params(("arbitrary",)),
    )(proj3, proj3, proj3, proj3, cos, sin, dmat_t, cd_t, sd_t, gam_t, gain3)
    return o_pre.reshape(T, D_MODEL), yb.reshape(T, D_MODEL), states


def _ret_bwd(dyb, o_pre, proj, states, tables, gain3, B, S, deps=()):
    T = B * S
    nc = S // CHUNK
    cos, sin, dmat_t, cd_t, sd_t, gam_t = tables

    def body(dyb_ref, o_ref, q_ref, k_ref, v_ref, gb_ref, rs_ref, cos_ref, sin_ref, dm_ref, cd_ref, sd_ref, gam_ref,
             gain_ref, dr_ref, dgain_ref, dstate_ref):
        @pl.when(pl.program_id(0) == 0)
        def _():
            dstate_ref[...] = jnp.zeros_like(dstate_ref)
            dgain_ref[...] = jnp.zeros_like(dgain_ref)

        cos_t, sin_t = cos_ref[...], sin_ref[...]
        for b, h in [(b, h) for b in range(B) for h in range(HEADS)]:
            cols = slice(h * DK, (h + 1) * DK)
            gain = gain_ref[h]
            n, rs = _groupnorm(o_ref[b, :, cols])
            gb = gb_ref[b, :, cols]
            sg = _sigmoid(gb)
            dy = dyb_ref[b, :, cols]
            part = lambda g: slice(g * D_MODEL + h * DK, g * D_MODEL + (h + 1) * DK)
            dr_ref[b, :, part(3)] = (dy * (n * gain) * (sg * (1.0 + gb * (1.0 - sg)))).astype(dr_ref.dtype)
            dgn = dy * (gb * sg)
            dgain_ref[h] += jnp.sum(dgn * n, axis=0, keepdims=True)
            dn = dgn * gain
            do = rs * (dn - jnp.mean(dn, axis=-1, keepdims=True) - n * jnp.mean(dn * n, axis=-1, keepdims=True))

            qb = _c(_rot(q_ref[b, :, cols], cos_t, sin_t))
            kb = _c(_rot(k_ref[b, :, cols], cos_t, sin_t) * (DK ** -0.5))
            v = v_ref[b, :, cols]
            vb = _c(v)
            vsb = _c(v * sd_ref[h])
            dob = _c(do)
            docb = _c(do * cd_ref[h])
            dmat = dm_ref[h]
            dstate = dstate_ref[b, h]
            dsb = _c(dstate)
            pb = _c(_dot_nt(qb, kb) * dmat)
            dsc = _c(_dot_nt(dob, vb) * dmat)
            dq = _dot(dsc, kb) + _dot_nt(docb, rs_ref[0, b, h])
            dk = _dot_tn(dsc, qb) + _dot_nt(vsb, dsb)
            dv = _dot_tn(pb, dob) + _dot(kb, dsb) * sd_ref[h]
            dstate_ref[b, h] = gam_ref[h] * dstate + _dot_tn(qb, docb)
            dr_ref[b, :, part(0)] = _rot_t(dq, cos_t, sin_t).astype(dr_ref.dtype)
            dr_ref[b, :, part(1)] = (_rot_t(dk, cos_t, sin_t) * (DK ** -0.5)).astype(dr_ref.dtype)
            dr_ref[b, :, part(2)] = dv.astype(dr_ref.dtype)

    qkv, act, rope, dmat, dvec, hrow, rst = _ret_specs(B, lambda c: nc - 1 - c)
    wide = pl.BlockSpec((B, CHUNK, 4 * D_MODEL), lambda c: (0, nc - 1 - c, 0))
    proj3 = proj.reshape(B, S, proj.shape[1])
    dr, dgain = pl.pallas_call(
        _after(body, 14, deps),
        name="ret_bwd",
        grid=(nc,),
        in_specs=[act, act, qkv(2), qkv(3), qkv(4), qkv(5), rst, rope, rope, dmat, dvec, dvec, hrow, hrow]
        + [ANY_SPEC] * len(deps),
        out_specs=[wide, hrow],
        out_shape=[jax.ShapeDtypeStruct((B, S, 4 * D_MODEL), _MXU), jax.ShapeDtypeStruct((HEADS, 1, DK), F32)],
        scratch_shapes=[pltpu.VMEM((B, HEADS, DK, DK), F32)],
        compiler_params=_params(("arbitrary",)),
    )(dyb.reshape(B, S, D_MODEL), o_pre.reshape(B, S, D_MODEL), proj3, proj3, proj3, proj3, states, cos, sin, dmat_t,
      cd_t, sd_t, gam_t, gain3, *deps)
    return dr.reshape(T, 4 * D_MODEL), dgain


def _mid(ya, yb, proj, x2d, tgt2d, wpa, wpb, wout, g_fin):
    T = x2d.shape[0]
    tm = min(256, T)
    n_steps = T // tm
    rows = D_MODEL // (2 * N_CHIPS)

    def body(ya_ref, yb_ref, ma_ref, mb_ref, x_ref, t_ref, gf_ref, wpa_hbm, wpb_hbm, wout_hbm,
             loss_ref, dx2_ref, dya_ref, dyb_ref, dm_ref, dgf_ref, gw_hbm, w_ref, acc_ref, sem):
        i = pl.program_id(0)

        @pl.when(i == 0)
        def _():
            loads = [pltpu.make_async_copy(src, w_ref.at[k], sem.at[k]) for k, src in enumerate((wpa_hbm, wpb_hbm, wout_hbm))]
            for cp in loads:
                cp.start()
            for cp in loads:
                cp.wait()
            acc_ref[...] = jnp.zeros_like(acc_ref)
            loss_ref[...] = jnp.zeros_like(loss_ref)
            dgf_ref[...] = jnp.zeros_like(dgf_ref)

        ya_t, yb_t = ya_ref[...], yb_ref[...]
        out_a = _dot(ya_t, w_ref[0])
        out_b = _dot(yb_t, w_ref[1])
        sa = _sigmoid(ma_ref[...])
        sb = _sigmoid(mb_ref[...])
        mgb = _c(sa * out_a + sb * out_b)
        x2 = x_ref[...] + _dot(mgb, w_ref[2])
        r2 = lax.rsqrt(jnp.mean(x2 * x2, axis=-1, keepdims=True) + EPS)
        nx = x2 * r2
        gf = gf_ref[...]
        err = nx * gf - t_ref[...]
        loss_ref[...] += 0.5 * jnp.sum(jnp.mean(err * err, axis=-1, keepdims=True), axis=0, keepdims=True)
        dy = err * (1.0 / D_MODEL)
        dgf_ref[...] += jnp.sum(dy * nx, axis=0, keepdims=True)
        dyg = dy * gf
        dx2 = r2 * (dyg - nx * jnp.mean(dyg * nx, axis=-1, keepdims=True))
        dx2_ref[...] = dx2
        dx2b = _c(dx2)
        dmg = _dot_nt(dx2b, w_ref[2])
        acc_ref[2] += _dot_tn(mgb, dx2b)
        dm_ref[:, :D_MODEL] = (dmg * out_a * sa * (1.0 - sa)).astype(dm_ref.dtype)
        dm_ref[:, D_MODEL:] = (dmg * out_b * sb * (1.0 - sb)).astype(dm_ref.dtype)
        dab = _c(dmg * sa)
        dbb = _c(dmg * sb)
        dya_ref[...] = _dot_nt(dab, w_ref[0])
        dyb_ref[...] = _dot_nt(dbb, w_ref[1])
        acc_ref[0] += _dot_tn(ya_t, dab)
        acc_ref[1] += _dot_tn(yb_t, dbb)

        @pl.when(i == n_steps - 1)
        def _():
            copies = [pltpu.make_async_copy(acc_ref.at[k, pl.ds((2 * p + hf) * rows, rows), :], gw_hbm.at[p, hf, k],
                                            sem.at[(k * N_CHIPS + p) * 2 + hf])
                      for k in range(3) for p in range(N_CHIPS) for hf in range(2)]
            for cp in copies:
                cp.start()
            for cp in copies:
                cp.wait()

    tile = lambda j: pl.BlockSpec((tm, D_MODEL), lambda i: (i, j))
    one = pl.BlockSpec((1, D_MODEL), lambda i: (0, 0))
    anyspec = pl.BlockSpec(memory_space=pl.ANY)
    return pl.pallas_call(
        body,
        name="mid",
        grid=(n_steps,),
        in_specs=[tile(0), tile(0), tile(6), tile(7), tile(0), tile(0), one, anyspec, anyspec, anyspec],
        out_specs=[pl.BlockSpec((1, 1), lambda i: (0, 0)), tile(0), tile(0), tile(0),
                   pl.BlockSpec((tm, 2 * D_MODEL), lambda i: (i, 0)), one, anyspec],
        out_shape=[
            jax.ShapeDtypeStruct((1, 1), F32),
            jax.ShapeDtypeStruct((T, D_MODEL), F32),
            jax.ShapeDtypeStruct((T, D_MODEL), F32),
            jax.ShapeDtypeStruct((T, D_MODEL), F32),
            jax.ShapeDtypeStruct((T, 2 * D_MODEL), _MXU),
            jax.ShapeDtypeStruct((1, D_MODEL), F32),
            jax.ShapeDtypeStruct((N_CHIPS, 2, 3, rows, D_MODEL), F32),
        ],
        scratch_shapes=[pltpu.VMEM((3, D_MODEL, D_MODEL), _MXU), pltpu.VMEM((3, D_MODEL, D_MODEL), F32),
                        pltpu.SemaphoreType.DMA((3 * N_CHIPS * 2,))],
        compiler_params=_params(("arbitrary",)),
    )(ya, yb, proj, proj, x2d, tgt2d, g_fin, wpa, wpb, wout)


DX_TILE = 512


def _inproj_bwd_dx(dparts, w_all, x2d, dx2, g_in, first, count, prev, name, deps=()):
    T = x2d.shape[0]
    tm = min(DX_TILE, T)
    n_d = len(dparts)
    groups = [(a, k) for a, d in enumerate(dparts) for k in range(d.shape[1] // D_MODEL)]
    dg_start = jnp.zeros((1, D_MODEL), F32) if prev is None else prev[1]
    carried = () if prev is None else (prev[0],)

    def body(*refs):
        d_refs = refs[:n_d]
        x_ref, dx2_ref, g_ref, dg0_ref, w_hbm = refs[n_d:n_d + 5]
        dx_ref, dg_ref, w_ref, sem = refs[-4:]

        @pl.when(pl.program_id(0) == 0)
        def _():
            cp = pltpu.make_async_copy(w_hbm, w_ref, sem)
            cp.start()
            cp.wait()
            dg_ref[...] = dg0_ref[...]

        dh = jnp.zeros((tm, D_MODEL), F32)
        for j, (a, k) in enumerate(groups):
            dh = dh + _dot_nt(d_refs[a][:, k * D_MODEL:(k + 1) * D_MODEL],
                              w_ref[j // 2, :, (j % 2) * D_MODEL:(j % 2 + 1) * D_MODEL])
        x = x_ref[...]
        r = lax.rsqrt(jnp.mean(x * x, axis=-1, keepdims=True) + EPS)
        nx = x * r
        dg_ref[...] += jnp.sum(dh * nx, axis=0, keepdims=True)
        dhg = dh * g_ref[...]
        dx_ref[...] = dx2_ref[...] + r * (dhg - nx * jnp.mean(dhg * nx, axis=-1, keepdims=True))

    tile = pl.BlockSpec((tm, D_MODEL), lambda i: (first + i, 0))
    one = pl.BlockSpec((1, D_MODEL), lambda i: (0, 0))
    return pl.pallas_call(
        body,
        name=name,
        grid=(count,),
        in_specs=[pl.BlockSpec((tm, d.shape[1]), lambda i: (first + i, 0)) for d in dparts]
        + [tile, tile, one, one, ANY_SPEC] + [ANY_SPEC] * (len(carried) + len(deps)),
        out_specs=[tile, one],
        out_shape=[jax.ShapeDtypeStruct((T, D_MODEL), F32), jax.ShapeDtypeStruct((1, D_MODEL), F32)],
        input_output_aliases={n_d + 5: 0} if carried else {},
        scratch_shapes=[pltpu.VMEM(w_all.shape, w_all.dtype), pltpu.SemaphoreType.DMA],
        compiler_params=_params(("arbitrary",)),
    )(*dparts, x2d, dx2, g_in, dg_start, w_all, *carried, *deps)


def _inproj_bwd_dw(ht, dparts, name, deps=()):
    T = ht.shape[1]
    tn = 512
    half = D_MODEL // 2
    per_chip = 2 * D_MODEL // tn
    n_d = len(dparts)
    tiles = [(a, t) for a, d in enumerate(dparts) for t in range(d.shape[1] // tn)]
    offs = [sum(d.shape[1] // tn for d in dparts[:a]) for a in range(n_d)]

    def body(*refs):
        ht_ref = refs[0]
        d_refs = refs[1:1 + n_d]
        out_ref = refs[-1]
        t = pl.program_id(0)

        for a in range(n_d):
            lo, hi = offs[a], offs[a] + dparts[a].shape[1] // tn

            @pl.when((t >= lo) & (t < hi))
            def _(a=a):
                g = _dot(ht_ref[...], d_refs[a][...])
                out_ref[0, 0] = g[:half]
                out_ref[0, 1] = g[half:]

    def dspec(a):
        n_a = dparts[a].shape[1] // tn
        return pl.BlockSpec((T, tn), lambda t: (0, jnp.clip(t - offs[a], 0, n_a - 1)))

    return pl.pallas_call(
        body,
        name=name,
        grid=(len(tiles),),
        in_specs=[pl.BlockSpec((D_MODEL, T), lambda t: (0, 0))] + [dspec(a) for a in range(n_d)]
        + [ANY_SPEC] * len(deps),
        out_specs=pl.BlockSpec((1, 2, half, tn), lambda t: (t // per_chip, 0, 0, t % per_chip)),
        out_shape=jax.ShapeDtypeStruct((len(tiles) // per_chip, 2, half, 2 * D_MODEL), F32),
        compiler_params=_params(("parallel",)),
    )(ht, *dparts, *deps)


def _coords():
    return lax.axis_index("x"), lax.axis_index("y"), lax.axis_index("c")


def _other_chips(x, y):
    return [(1 - x, y), (x, 1 - y), (1 - x, 1 - y)]


def _all_gather8(xs, name, deps=()):
    m_per, n = xs.shape

    def body(x_ref, *rest):
        out_ref, send_sems, recv_sems, local_sem = rest[-4:]
        x, y, c = _coords()
        me, sibling = (x, y, c), (x, y, 1 - c)
        chips = _other_chips(x, y)

        def rows(px, py, pc):
            return out_ref.at[pl.ds((4 * px + 2 * py + pc) * m_per, m_per), :]

        def copy(k, block, to, src=None):
            return pltpu.make_async_remote_copy(
                src_ref=rows(*block) if src is None else src, dst_ref=rows(*block),
                send_sem=send_sems.at[k], recv_sem=recv_sems.at[k], device_id=to, device_id_type=MESH)

        mine = pltpu.make_async_copy(x_ref, rows(*me), local_sem)
        mine.start()
        first = [copy(0, me, sibling, src=x_ref)]
        first += [copy(1 + j, me, (*chip, c), src=x_ref) for j, chip in enumerate(chips)]
        for cp in first:
            cp.start()
        passed = [copy(4 + j, (*chip, c), sibling) for j, chip in enumerate(chips)]
        for j, chip in enumerate(chips):
            copy(1 + j, (*chip, c), me).wait_recv()
            passed[j].start()
        copy(0, sibling, me).wait_recv()
        for j, chip in enumerate(chips):
            copy(4 + j, (*chip, 1 - c), me).wait_recv()
        for cp in first + passed:
            cp.wait_send()
        mine.wait()

    return pl.pallas_call(
        body,
        name=name,
        out_shape=jax.ShapeDtypeStruct((8 * m_per, n), xs.dtype),
        in_specs=[pl.BlockSpec(memory_space=pltpu.VMEM)] + [ANY_SPEC] * len(deps),
        out_specs=pl.BlockSpec(memory_space=pltpu.VMEM),
        scratch_shapes=[pltpu.SemaphoreType.DMA((7,)), pltpu.SemaphoreType.DMA((7,)), pltpu.SemaphoreType.DMA],
        compiler_params=pltpu.CompilerParams(vmem_limit_bytes=VMEM_LIMIT),
    )(xs, *deps)


def _chunks(rows, n):
    size = rows // n
    return [pl.ds(q * size, size) for q in range(n)]


HBM_SPEC = pl.BlockSpec(memory_space=pltpu.HBM)
SEM_SPEC = pl.BlockSpec(memory_space=pltpu.SEMAPHORE)
DATAFLOW = pltpu.SideEffectType.DATAFLOW_SIDE_EFFECTING


def _copies_start(bufs, plan, n_copies, name):
    n = len(bufs)

    def body(*refs):
        ins = refs[:n]
        send_sems, recv_sems = refs[n], refs[n + 1]
        token = refs[-1]
        for k, send, _ in plan(ins):
            if send is not None:
                src, dst, dev, pred = send
                cp = pltpu.make_async_remote_copy(src_ref=src, dst_ref=dst, send_sem=send_sems.at[k],
                                                  recv_sem=recv_sems.at[k], device_id=dev, device_id_type=MESH)
                if pred is None:
                    cp.start()
                else:
                    pl.when(pred)(cp.start)
        token[...] = jnp.zeros_like(token)

    hbm = [pltpu.with_memory_space_constraint(b, pltpu.HBM) for b in bufs]
    outs = pl.pallas_call(
        body,
        name=name,
        in_specs=[HBM_SPEC] * n,
        out_specs=(SEM_SPEC, SEM_SPEC, *([HBM_SPEC] * n), pl.BlockSpec(memory_space=pltpu.VMEM)),
        out_shape=(pltpu.SemaphoreType.DMA((n_copies,)), pltpu.SemaphoreType.DMA((n_copies,)),
                   *[pltpu.HBM(b.shape, b.dtype) for b in bufs], jax.ShapeDtypeStruct((8, 128), F32)),
        input_output_aliases={a: 2 + a for a in range(n)},
        compiler_params=pltpu.CompilerParams(has_side_effects=DATAFLOW),
    )(*hbm)
    return outs[0], outs[1], list(outs[2:2 + n]), outs[-1]


def _copies_wait(send_sems, recv_sems, bufs, after, plan, name):
    n = len(bufs)

    def body(*refs):
        ins = refs[:n]
        s_sems, r_sems = refs[n], refs[n + 1]
        for k, send, recv in plan(ins):
            if send is not None:
                src, dst, dev, pred = send
                cp = pltpu.make_async_remote_copy(src_ref=src, dst_ref=dst, send_sem=s_sems.at[k],
                                                  recv_sem=r_sems.at[k], device_id=dev, device_id_type=MESH)
                if pred is None:
                    cp.wait_send()
                else:
                    pl.when(pred)(cp.wait_send)
            if recv is not None:
                dst, pred = recv
                cp = pltpu.make_async_remote_copy(src_ref=dst, dst_ref=dst, send_sem=s_sems.at[k],
                                                  recv_sem=r_sems.at[k], device_id=_coords(), device_id_type=MESH)
                if pred is None:
                    cp.wait_recv()
                else:
                    pl.when(pred)(cp.wait_recv)

    outs = pl.pallas_call(
        body,
        name=name,
        in_specs=[HBM_SPEC] * n + [SEM_SPEC, SEM_SPEC, pl.BlockSpec(memory_space=pl.ANY)],
        out_specs=[HBM_SPEC] * n,
        out_shape=[pltpu.HBM(b.shape, b.dtype) for b in bufs],
        input_output_aliases={a: a for a in range(n)},
        compiler_params=pltpu.CompilerParams(has_side_effects=DATAFLOW),
    )(*bufs, send_sems, recv_sems, after)
    return list(outs)


def _gather_plan(n_bufs):
    def plan(refs):
        x, y, c = _coords()
        me = 2 * x + y
        out = []
        for k, (px, py) in enumerate(_other_chips(x, y)):
            for a in range(n_bufs):
                out.append((k * n_bufs + a, (refs[a].at[me], refs[a].at[me], (px, py, c), None),
                            (refs[a].at[2 * px + py], None)))
        return out
    return plan


def _cast_into_slot(ws, name):
    n = len(ws)
    nt = 2

    def body(s_ref, *refs):
        for a in range(n):
            refs[n + a][0] = refs[a][...].astype(refs[n + a].dtype)

    xi, yi, _ = _coords()
    return pl.pallas_call(
        body,
        name=name,
        grid_spec=pltpu.PrefetchScalarGridSpec(
            num_scalar_prefetch=1,
            grid=(2, nt),
            in_specs=[pl.BlockSpec((1, w.shape[1] // nt, w.shape[2]), lambda hf, i, s: (hf, i, 0)) for w in ws],
            out_specs=[pl.BlockSpec((1, 1, w.shape[1] // nt, w.shape[2]), lambda hf, i, s: (s[0], hf, i, 0)) for w in ws],
        ),
        out_shape=[jax.ShapeDtypeStruct((N_CHIPS,) + w.shape, _MXU) for w in ws],
        compiler_params=_params(("parallel", "parallel")),
    )((2 * xi + yi).reshape(1).astype(jnp.int32), *ws)


def _gather_chips(bufs, n_chunks, name):
    n = len(bufs)
    pieces = [(a, rows) for a in range(n) for rows in _chunks(bufs[a].shape[2], n_chunks[a])]
    n_p = len(pieces)

    def body(*refs):
        outs = refs[n:2 * n]
        send_sems, recv_sems, fsend_sems, frecv_sems = refs[2 * n:]
        x, y, c = _coords()
        me = 2 * x + y
        near = [(1 - x, y), (x, 1 - y)]
        slots = [2 * (1 - x) + y, 2 * x + (1 - y), 2 * (1 - x) + (1 - y)]
        pass_to = (jnp.where(c == 0, x, 1 - x), jnp.where(c == 0, 1 - y, y))
        pass_slot = jnp.where(c == 0, slots[0], slots[1])

        def send(k, i, slot, chip):
            a, rows = pieces[i]
            return pltpu.make_async_remote_copy(
                src_ref=outs[a].at[slot, c, rows], dst_ref=outs[a].at[slot, c, rows], send_sem=send_sems.at[k * n_p + i],
                recv_sem=recv_sems.at[k * n_p + i], device_id=(*chip, c), device_id_type=MESH)

        def forward(k, i, half):
            a, rows = pieces[i]
            return pltpu.make_async_remote_copy(
                src_ref=outs[a].at[slots[k], half, rows], dst_ref=outs[a].at[slots[k], half, rows],
                send_sem=fsend_sems.at[k * n_p + i], recv_sem=frecv_sems.at[k * n_p + i],
                device_id=(x, y, 1 - c), device_id_type=MESH)

        started = [send(k, i, me, chip) for i in range(n_p) for k, chip in enumerate(near)]
        for cp in started:
            cp.start()
        for i in range(n_p):
            for k, chip in enumerate(near):
                send(k, i, slots[k], chip).wait_recv()
            later = [send(2, i, pass_slot, pass_to), forward(0, i, c), forward(1, i, c)]
            for cp in later:
                cp.start()
            started += later
        for i in range(n_p):
            send(2, i, slots[2], pass_to).wait_recv()
            fw = forward(2, i, c)
            fw.start()
            started.append(fw)
        for i in range(n_p):
            for k in range(3):
                forward(k, i, 1 - c).wait_recv()
        for cp in started:
            cp.wait_send()

    anyspec = pl.BlockSpec(memory_space=pl.ANY)
    sems = pltpu.SemaphoreType.DMA((3 * n_p,))
    return pl.pallas_call(
        body,
        name=name,
        in_specs=[anyspec] * n,
        out_specs=[anyspec] * n,
        out_shape=[jax.ShapeDtypeStruct(b.shape, b.dtype) for b in bufs],
        input_output_aliases={a: a for a in range(n)},
        scratch_shapes=[sems, sems, sems, sems],
    )(*bufs)


def _swap_plan(n_slabs):
    def plan(refs):
        x, y, c = _coords()
        out, k = [], 0
        for i, n in enumerate(n_slabs):
            g, land = refs[2 * i], refs[2 * i + 1]
            for p in range(n):
                out.append((k, (g.at[p, 1 - c], land.at[p], (x, y, 1 - c), None), (land.at[p], None)))
                k += 1
        return out
    return plan


def _is_one_of(chip, dests):
    hit = chip == dests[0]
    for d in dests[1:]:
        hit = hit | (chip == d)
    return hit


def _slab_of(chip, dests):
    return sum(j * (chip == d).astype(jnp.int32) for j, d in enumerate(dests))


def _scatter_plan(dest_sets):
    def plan(refs):
        x, y, c = _coords()
        me = 2 * x + y
        out = []
        for k, (px, py) in enumerate(_other_chips(x, y)):
            peer = 2 * px + py
            for i, dests in enumerate(dest_sets):
                cs, land = refs[2 * i], refs[2 * i + 1]
                everyone = len(dests) == N_CHIPS
                send = (cs.at[_slab_of(peer, dests)], land.at[k], (px, py, c),
                        None if everyone else _is_one_of(peer, dests))
                recv = (land.at[k], None if everyone else _is_one_of(me, dests))
                out.append((k * len(dest_sets) + i, send, recv))
        return out
    return plan


def _allgather_plan():
    def plan(refs):
        x, y, c = _coords()
        (land,) = refs
        me = 4 * x + 2 * y + c
        out = []
        for r in range(1, 8):
            px = 1 - x if r & 4 else x
            py = 1 - y if r & 2 else y
            pc = 1 - c if r & 1 else c
            out.append((r - 1, (land.at[me], land.at[me], (px, py, pc), None), (land.at[4 * px + 2 * py + pc], None)))
        return out
    return plan


def _sum_gathered(land, name):
    def body(land_ref, o_ref):
        acc = land_ref[0]
        for d in range(1, 8):
            acc = acc + land_ref[d]
        o_ref[...] = acc

    return pl.pallas_call(
        body,
        name=name,
        out_shape=jax.ShapeDtypeStruct(land.shape[1:], F32),
        compiler_params=_params(),
    )(land)


def _join_halves(bufs, n_chunks, name):
    n = len(bufs)
    pieces = [(a, rows) for a in range(n) for rows in _chunks(bufs[a].shape[1], n_chunks[a])]
    n_p = len(pieces)

    def body(*refs):
        outs = refs[n:2 * n]
        send_sems, recv_sems = refs[2 * n:]
        x, y, c = _coords()

        def copy(i, half):
            a, rows = pieces[i]
            return pltpu.make_async_remote_copy(
                src_ref=outs[a].at[half, rows], dst_ref=outs[a].at[half, rows], send_sem=send_sems.at[i],
                recv_sem=recv_sems.at[i], device_id=(x, y, 1 - c), device_id_type=MESH)

        sends = [copy(i, c) for i in range(n_p)]
        for cp in sends:
            cp.start()
        for i in range(n_p):
            copy(i, 1 - c).wait_recv()
        for cp in sends:
            cp.wait_send()

    anyspec = pl.BlockSpec(memory_space=pl.ANY)
    sems = pltpu.SemaphoreType.DMA((n_p,))
    return pl.pallas_call(
        body,
        name=name,
        in_specs=[anyspec] * n,
        out_specs=[anyspec] * n,
        out_shape=[jax.ShapeDtypeStruct(b.shape, b.dtype) for b in bufs],
        input_output_aliases={a: a for a in range(n)},
        scratch_shapes=[sems, sems],
    )(*bufs)


def _row_tile(rows, cap):
    t = cap
    while rows % t:
        t //= 2
    return t


def _add_my_half(g, r, name):
    n_slabs, _, R, C = g.shape
    tr = R if n_slabs > 1 else _row_tile(R, 256)

    def body(c_ref, g_ref, r_ref, o_ref):
        o_ref[...] = (g_ref[0] + r_ref[...]).astype(o_ref.dtype)

    return pl.pallas_call(
        body,
        name=name,
        grid_spec=pltpu.PrefetchScalarGridSpec(
            num_scalar_prefetch=1,
            grid=(n_slabs, R // tr),
            in_specs=[pl.BlockSpec((1, 1, tr, C), lambda p, i, c_ref: (p, c_ref[0], i, 0)),
                      pl.BlockSpec((1, tr, C), lambda p, i, c_ref: (p, i, 0))],
            out_specs=pl.BlockSpec((1, tr, C), lambda p, i, c_ref: (p, i, 0)),
        ),
        out_shape=jax.ShapeDtypeStruct(r.shape, jnp.bfloat16),
        compiler_params=_params(("parallel", "parallel")),
    )(lax.axis_index("c").reshape(1).astype(jnp.int32), g, r)


def _sum_slabs(own, got, name):
    _, R, C = own.shape
    tr = _row_tile(R, 256)

    def body(s_ref, own_ref, got_ref, o_ref):
        o_ref[0] = ((own_ref[0].astype(F32) + got_ref[0].astype(F32)) + got_ref[1].astype(F32)) + got_ref[2].astype(F32)

    xi, yi, ci = _coords()
    return pl.pallas_call(
        body,
        name=name,
        grid_spec=pltpu.PrefetchScalarGridSpec(
            num_scalar_prefetch=1,
            grid=(R // tr,),
            in_specs=[pl.BlockSpec((1, tr, C), lambda i, s: (s[0], i, 0)),
                      pl.BlockSpec((3, tr, C), lambda i, s: (0, i, 0))],
            out_specs=pl.BlockSpec((1, tr, C), lambda i, s: (s[1], i, 0)),
        ),
        out_shape=jax.ShapeDtypeStruct((2, R, C), F32),
        compiler_params=_params(("parallel",)),
    )(jnp.stack([2 * xi + yi, ci]).astype(jnp.int32), own, got)


def _sum_parts(owns, got, dest_sets, name):
    n = len(owns)
    _, R, C = owns[0].shape
    tr = _row_tile(R, 256)

    def body(s_ref, *refs):
        got_ref, o_ref = refs[n], refs[-1]
        total = jnp.zeros((tr, C), F32)
        for i in range(n):
            total = total + jnp.where(s_ref[2 + 2 * i] == 1, refs[i][0].astype(F32), 0.0)
        o_ref[0] = ((total + got_ref[0].astype(F32)) + got_ref[1].astype(F32)) + got_ref[2].astype(F32)

    xi, yi, ci = _coords()
    me = 2 * xi + yi
    scalars = [ci, ci]
    for dests in dest_sets:
        scalars += [_is_one_of(me, dests).astype(jnp.int32), _slab_of(me, dests)]
    own_spec = lambda i: pl.BlockSpec((1, tr, C), lambda r, s: (s[3 + 2 * i], r, 0))
    return pl.pallas_call(
        body,
        name=name,
        grid_spec=pltpu.PrefetchScalarGridSpec(
            num_scalar_prefetch=1,
            grid=(R // tr,),
            in_specs=[own_spec(i) for i in range(n)] + [pl.BlockSpec((3, tr, C), lambda r, s: (0, r, 0))],
            out_specs=pl.BlockSpec((1, tr, C), lambda r, s: (s[0], r, 0)),
        ),
        out_shape=jax.ShapeDtypeStruct((2, R, C), F32),
        compiler_params=_params(("parallel",)),
    )(jnp.stack(scalars).astype(jnp.int32), *owns, got)


def _sum_rows8(g, m_per, name):
    n = g.shape[1]

    def body(g_ref, o_ref):
        acc = g_ref[0:m_per, :]
        for k in range(1, 8):
            acc = acc + g_ref[k * m_per:(k + 1) * m_per, :]
        o_ref[...] = acc

    return pl.pallas_call(
        body,
        name=name,
        out_shape=jax.ShapeDtypeStruct((m_per, n), F32),
        compiler_params=_params(),
    )(g)


def _adamw_math(w, g, m, v):
    m = ADAM_B1 * m + (1.0 - ADAM_B1) * g
    v = ADAM_B2 * v + (1.0 - ADAM_B2) * (g * g)
    m_hat = m / (1.0 - ADAM_B1 ** ADAM_STEP)
    v_hat = v / (1.0 - ADAM_B2 ** ADAM_STEP)
    delta = -ADAM_LR * (m_hat / (jnp.sqrt(v_hat) + ADAM_EPS) + ADAM_WD * w)
    return delta, m, v


def _adamw_big(w, g, m, v, name):
    R, C = w.shape
    tr = min(128, R)

    def body(w_ref, g_ref, m_ref, v_ref, g_out, d_out, m_out, v_out):
        g = g_ref[...]
        d, mn, vn = _adamw_math(w_ref[...], g, m_ref[...], v_ref[...])
        g_out[...] = g
        d_out[...] = d
        m_out[...] = mn
        v_out[...] = vn

    spec = pl.BlockSpec((tr, C), lambda i: (i, 0))
    return pl.pallas_call(
        body,
        name=name,
        grid=(R // tr,),
        in_specs=[spec] * 4,
        out_specs=[spec] * 4,
        out_shape=[jax.ShapeDtypeStruct((R, C), F32)] * 4,
        compiler_params=_params(("parallel",)),
    )(w, g, m, v)


def _adamw_small(ws, gs, ms, vs, name):
    n = len(ws)

    def body(*refs):
        for a in range(n):
            d, mn, vn = _adamw_math(refs[a][...], refs[n + a][...], refs[2 * n + a][...], refs[3 * n + a][...])
            refs[4 * n + a][...] = d
            refs[5 * n + a][...] = mn
            refs[6 * n + a][...] = vn

    shapes = [jax.ShapeDtypeStruct(w.shape, F32) for w in ws]
    outs = pl.pallas_call(
        body,
        name=name,
        out_shape=shapes * 3,
        compiler_params=_params(),
    )(*ws, *gs, *ms, *vs)
    return outs[:n], outs[n:2 * n], outs[2 * n:]


def _to_blockdiag(w):
    per = CW // LRU_BW
    w4 = w.reshape(N_CT, per, LRU_BW, LRU_BW)
    eye = jnp.eye(per, dtype=w.dtype)
    return (w4[:, :, :, None, :] * eye[None, :, None, :, None]).reshape(N_CT, CW, CW)


def _from_blockdiag(g):
    per = CW // LRU_BW
    g5 = g.reshape(N_CT, per, LRU_BW, per, LRU_BW)
    return jnp.stack([g5[:, b, :, b, :] for b in range(per)], axis=1).reshape(LRU_BLOCKS, LRU_BW, LRU_BW)


def _local_grads(x2d, tgt2d, B, S, g_in, w_all, conv_w, conv_b, gate_x_w, gate_x_b, gate_a_w, gate_a_b, lam, gain,
                 proj_weights, g_fin, reduce, deps=()):
    wx_bd = _c(_to_blockdiag(gate_x_w))
    wa_bd = _c(_to_blockdiag(gate_a_w))
    tables = _retention_tables(S)
    gain3 = gain.reshape(HEADS, 1, DK)

    proj, ht = _inproj_fwd(x2d, g_in, w_all, deps)
    hlru, ya = _lru_fwd(proj, conv_w, conv_b, wx_bd, wa_bd, gate_x_b, gate_a_b, lam, B, S)
    o_pre, yb, states = _ret_fwd(proj, tables, gain3, B, S)
    wpa, wpb, wout = proj_weights(yb)
    loss, dx2, dya, dyb, dm, dgf, gw_proj = _mid(ya, yb, proj, x2d, tgt2d, wpa, wpb, wout, g_fin)
    g3 = _inproj_bwd_dw(ht, [dm], "inproj_bwd_dw_m")
    deps = reduce.m_ready(gw_proj, g3)
    dr, dgain = _ret_bwd(dyb, o_pre, proj, states, tables, gain3, B, S, deps)
    deps = reduce.ret_done(dr)
    g12 = _inproj_bwd_dw(ht, [dr], "inproj_bwd_dw_r", deps)
    deps = reduce.r_ready(g12)
    dxa, dga, dcw, dcb, dwx_bd, dwa_bd, dbx, dba, dlam = _lru_bwd(
        dya, proj, hlru, conv_w, conv_b, wx_bd, wa_bd, gate_x_b, gate_a_b, lam, B, S, deps)
    deps = reduce.lru_done(dxa)
    small = dict(conv_w=dcw, conv_b=dcb, gate_x_w=_from_blockdiag(dwx_bd), gate_x_b=dbx,
                 gate_a_w=_from_blockdiag(dwa_bd), gate_a_b=dba, lru_lambda=dlam, gn_gain=dgain.reshape(HEADS, DK),
                 norm_final=dgf)
    loss_rows = jnp.broadcast_to(loss, (SUBLANES, LANES))
    deps = deps + reduce.small_ready(jnp.concatenate([_pack_small(small), loss_rows], axis=0))
    g0 = _inproj_bwd_dw(ht, [dxa, dga], "inproj_bwd_dw_a", deps)
    deps = reduce.a_ready(g0)
    n_tiles = x2d.shape[0] // min(DX_TILE, x2d.shape[0])
    grad_x, dgin = _inproj_bwd_dx([dxa, dga, dr, dm], w_all, x2d, dx2, g_in, 0, n_tiles, None, "inproj_bwd_dx", deps)
    return grad_x, dgin


ALL_CHIPS = (0, 1, 2, 3)


class _GradReduce:
    def __init__(self, proj_done):
        self.pending = {}
        self.proj_done = proj_done
        self.land_in = None

    def _start(self, key, bufs, plan, n_copies, name, shared=None):
        send_sems, recv_sems, bufs, token = _copies_start(bufs, plan, n_copies, name + "_start")
        if shared is not None:
            self.land_in = bufs[shared]
        self.pending[key] = (send_sems, recv_sems, bufs, plan, name + "_wait", shared)
        return (token,)

    def _finish(self, key, after):
        send_sems, recv_sems, bufs, plan, name, shared = self.pending.pop(key)
        if shared is not None:
            bufs[shared] = self.land_in
        bufs = _copies_wait(send_sems, recv_sems, bufs, after, plan, name)
        if shared is not None:
            self.land_in = bufs[shared]
        return bufs

    def _swap(self, key, parts):
        bufs = []
        for g in parts:
            bufs += [g, lax.empty((g.shape[0],) + g.shape[2:], F32)]
        n_slabs = [g.shape[0] for g in parts]
        return self._start(key, bufs, _swap_plan(n_slabs), sum(n_slabs), "swap_" + key)

    def _chip_sums(self, key, after):
        bufs = self._finish(key, after)
        return [_add_my_half(bufs[2 * i], bufs[2 * i + 1], "chip_sum_%s%d" % (key, i)) for i in range(len(bufs) // 2)]

    def _scatter(self, key, sums, dest_sets):
        bufs = []
        for cs in sums:
            bufs += [cs, lax.empty((3,) + cs.shape[1:], cs.dtype)]
        if self.land_in is not None:
            bufs[-1] = self.land_in
        return self._start(key, bufs, _scatter_plan(dest_sets), 3 * len(sums), "scatter_" + key, shared=len(bufs) - 1)

    def m_ready(self, gw_proj, g3):
        rows = gw_proj.shape[2] * gw_proj.shape[3]
        return self._swap("m", [gw_proj.reshape(N_CHIPS, 2, rows, D_MODEL), g3])

    def ret_done(self, after):
        return self._scatter("sm", self._chip_sums("m", after), [ALL_CHIPS, (3,)])

    def r_ready(self, g12):
        return self._swap("r", [g12])

    def lru_done(self, after):
        return self._scatter("sr", self._chip_sums("r", after), [(1, 2)])

    def small_ready(self, packed):
        x, y, c = _coords()
        land = lax.dynamic_update_slice(lax.empty((8,) + packed.shape, F32), packed[None], (4 * x + 2 * y + c, 0, 0))
        return self._start("small", [land], _allgather_plan(), 7, "gather_small")

    def a_ready(self, g0):
        (token,) = self._swap("a", [g0])
        csp, gotp, self.cs3, _ = self._finish("sm", token)
        (g_proj,) = _join_halves([_sum_slabs(csp, gotp, "sum_w_proj")], [4], "join_halves_proj")
        after = self.proj_done(g_proj)
        return self._scatter("sa", self._chip_sums("a", after), [(0,)])

    def finish(self, after):
        small_sum = _sum_gathered(self._finish("small", after)[0], "sum_small_grads")
        cs12, _ = self._finish("sr", after)
        cs0, _ = self._finish("sa", after)
        half_in = _sum_parts([self.cs3, cs12, cs0], self.land_in, [(3,), (1, 2), (0,)], "sum_w_in")
        return small_sum, _join_halves([half_in], [8], "join_halves")[0]


_SMALL = ("gate_x_w", "gate_a_w", "conv_w", "conv_b", "gate_x_b", "gate_a_b", "lru_lambda", "gn_gain", "norm_final")
_SMALL_SHAPES = dict(gate_x_w=(LRU_BLOCKS, LRU_BW, LRU_BW), gate_a_w=(LRU_BLOCKS, LRU_BW, LRU_BW),
                     norm_in=(1, D_MODEL), conv_w=(CONV, D_MODEL), conv_b=(1, D_MODEL), gate_x_b=(1, D_MODEL),
                     gate_a_b=(1, D_MODEL), lru_lambda=(1, D_MODEL), gn_gain=(HEADS, DK), norm_final=(1, D_MODEL))


def _pack_small(small):
    return jnp.concatenate([small[k].reshape(-1, 128) for k in _SMALL], axis=0)


def _unpack_small(packed):
    out, r = {}, 0
    for k in _SMALL:
        shape = _SMALL_SHAPES[k]
        rows = 1
        for s in shape:
            rows *= s
        rows //= 128
        out[k] = packed[r:r + rows].reshape(shape)
        r += rows
    return out


def kernel(x, norm_in, w_in, conv_w, conv_b, gate_x_w, gate_x_b, gate_a_w, gate_a_b, lru_lambda, gn_gain, w_proj_a, w_proj_b, w_out, norm_final, loss_target, m_norm_in, m_w_in, m_conv_w, m_conv_b, m_gate_x_w, m_gate_x_b, m_gate_a_w, m_gate_a_b, m_lru_lambda, m_gn_gain, m_w_proj_a, m_w_proj_b, m_w_out, m_norm_final, v_norm_in, v_w_in, v_conv_w, v_conv_b, v_gate_x_w, v_gate_x_b, v_gate_a_w, v_gate_a_b, v_lru_lambda, v_gn_gain, v_w_proj_a, v_w_proj_b, v_w_out, v_norm_final):
    B, S, _ = x.shape
    T = B * S
    xi, yi, ci = _coords()
    chip = 2 * xi + yi

    cshard = D_MODEL // N_CHIPS
    mine = _cast_into_slot([w_in[0].reshape(2, D_MODEL // 2, 2 * D_MODEL)]
                           + [w[0].reshape(2, cshard // 2, D_MODEL) for w in (w_proj_a, w_proj_b, w_out)],
                           "cast_weights")
    plan = _gather_plan(3)
    s_sems, r_sems, pbufs, token = _copies_start(mine[1:], plan, 9, "gather_proj_start")
    gshard = DK // N_CHIPS
    tiny = jnp.concatenate([conv_w[0], jnp.zeros((4, cshard), F32), jnp.pad(gn_gain[0], ((0, 4), (0, cshard - gshard)))],
                           axis=0).reshape(1, 2, SUBLANES, cshard)
    tiny_buf = lax.dynamic_update_slice(lax.empty((N_CHIPS, 2, SUBLANES, cshard), F32), tiny, (chip, 0, 0, 0))
    w_buf, tiny_buf = _gather_chips([mine[0], tiny_buf], [8, 1], "gather_weights")
    w_all = w_buf.reshape(N_CHIPS, D_MODEL, 2 * D_MODEL)
    tiny_all = tiny_buf.reshape(N_CHIPS, 2 * SUBLANES, cshard)

    def proj_weights(after):
        got = _copies_wait(s_sems, r_sems, pbufs, after, plan, "gather_proj_wait")
        return [b.reshape(D_MODEL, D_MODEL) for b in got]

    conv_w_full = jnp.transpose(tiny_all[:, 0:CONV, :], (1, 0, 2)).reshape(CONV, D_MODEL)
    gain_full = jnp.transpose(tiny_all[:, 8:8 + HEADS, :gshard], (1, 0, 2)).reshape(HEADS, DK)

    weights = dict(norm_in=norm_in, w_in=w_in, conv_w=conv_w, conv_b=conv_b, gate_x_w=gate_x_w, gate_x_b=gate_x_b,
                   gate_a_w=gate_a_w, gate_a_b=gate_a_b, lru_lambda=lru_lambda, gn_gain=gn_gain, w_proj_a=w_proj_a,
                   w_proj_b=w_proj_b, w_out=w_out, norm_final=norm_final)
    ms = dict(norm_in=m_norm_in, w_in=m_w_in, conv_w=m_conv_w, conv_b=m_conv_b, gate_x_w=m_gate_x_w,
              gate_x_b=m_gate_x_b, gate_a_w=m_gate_a_w, gate_a_b=m_gate_a_b, lru_lambda=m_lru_lambda, gn_gain=m_gn_gain,
              w_proj_a=m_w_proj_a, w_proj_b=m_w_proj_b, w_out=m_w_out, norm_final=m_norm_final)
    vs = dict(norm_in=v_norm_in, w_in=v_w_in, conv_w=v_conv_w, conv_b=v_conv_b, gate_x_w=v_gate_x_w,
              gate_x_b=v_gate_x_b, gate_a_w=v_gate_a_w, gate_a_b=v_gate_a_b, lru_lambda=v_lru_lambda, gn_gain=v_gn_gain,
              w_proj_a=v_w_proj_a, w_proj_b=v_w_proj_b, w_out=v_w_out, norm_final=v_norm_final)
    names = list(weights)
    grads, delta, new_m, new_v = {}, {}, {}, {}

    def update_big(k, g):
        shp = weights[k].shape
        two = lambda a: a.reshape(shp[1], shp[2])
        g, d, mn, vn = _adamw_big(two(weights[k]), g, two(ms[k]), two(vs[k]), "adamw_" + k)
        grads[k], delta[k], new_m[k], new_v[k] = g.reshape(shp), d.reshape(shp), mn.reshape(shp), vn.reshape(shp)
        return d

    def proj_done(g_proj):
        g_pr = g_proj.reshape(2, 3, D_MODEL // (2 * N_CHIPS), D_MODEL)
        for i, k in enumerate(("w_proj_a", "w_proj_b", "w_out")):
            last = update_big(k, g_pr[:, i].reshape(cshard, D_MODEL))
        return last

    reduce = _GradReduce(proj_done)
    grad_x, dgin = _local_grads(
        x.reshape(T, D_MODEL), loss_target.reshape(T, D_MODEL), B, S, norm_in, w_all, conv_w_full, conv_b,
        gate_x_w[0], gate_x_b, gate_a_w[0], gate_a_b, lru_lambda, gain_full, proj_weights,
        norm_final.reshape(1, D_MODEL), reduce, deps=(token,))

    g_norm_in = _sum_rows8(_all_gather8(dgin.reshape(8, 128), "gather_norm_in_grad"), 8, "sum_norm_in_grad")
    small_sum, g_in_full = reduce.finish(g_norm_in)
    update_big("w_in", g_in_full.reshape(D_MODEL, 2 * D_MODEL))
    loss = small_sum[small_sum.shape[0] - SUBLANES, 0]

    gsm = _unpack_small(small_sum)
    gsm["norm_in"] = g_norm_in
    gsm["conv_w"] = lax.dynamic_slice_in_dim(gsm["conv_w"], chip * cshard, cshard, axis=1)
    gsm["gn_gain"] = lax.dynamic_slice_in_dim(gsm["gn_gain"], chip * gshard, gshard, axis=1)
    smalls = [k for k in names if k not in delta]

    def view(a):
        return a.reshape(1, -1) if a.ndim == 1 else (a.reshape(a.shape[1:]) if a.ndim > 2 else a)

    ds, mns, vns = _adamw_small([view(weights[k]) for k in smalls], [gsm[k].reshape(view(weights[k]).shape) for k in smalls],
                                [view(ms[k]) for k in smalls], [view(vs[k]) for k in smalls], "adamw_small")
    for k, d, mn, vn in zip(smalls, ds, mns, vns):
        shp = weights[k].shape
        grads[k], delta[k], new_m[k], new_v[k] = gsm[k].reshape(shp), d.reshape(shp), mn.reshape(shp), vn.reshape(shp)

    return (loss, grad_x.reshape(B, S, D_MODEL), *[grads[k] for k in names], *[delta[k] for k in names],
            *[new_m[k] for k in names], *[new_v[k] for k in names])
```

```python
import jax
import jax.numpy as jnp
from jax import lax
from jax.experimental import pallas as pl
from jax.experimental.pallas import tpu as pltpu

F32 = jnp.float32
_MXU = jnp.bfloat16

D_MODEL = 1024
N_GROUPS = 8
HEADS = 4
DK = 256
CHUNK = 128
CONV = 4
LRU_BLOCKS = 16
LRU_BW = 64
LRU_C = 8.0
ROPE_THETA = 10000.0
EPS = 1e-6
CW = 256
N_CT = D_MODEL // CW
N_CHIPS = 4
MESH = pl.DeviceIdType.MESH

ADAM_LR = 0.001
ADAM_B1 = 0.9
ADAM_B2 = 0.999
ADAM_EPS = 1e-08
ADAM_WD = 0.01
ADAM_STEP = 10

VMEM_LIMIT = 56 * 1024 * 1024


def _c(v):
    return v.astype(_MXU)


def _dot(a, b):
    return lax.dot_general(a, b, (((1,), (0,)), ((), ())), preferred_element_type=F32)


def _dot_nt(a, b):
    return lax.dot_general(a, b, (((1,), (1,)), ((), ())), preferred_element_type=F32)


def _dot_tn(a, b):
    return lax.dot_general(a, b, (((0,), (0,)), ((), ())), preferred_element_type=F32)


def _sigmoid(z):
    return 0.5 * jnp.tanh(0.5 * z) + 0.5


ANY_SPEC = pl.BlockSpec(memory_space=pl.ANY)


def _after(body, n_in, deps):
    n_deps = len(deps)

    def wrapped(*refs):
        return body(*refs[:n_in], *refs[n_in + n_deps:])

    return wrapped


def _params(sem=None):
    if sem is None:
        return pltpu.CompilerParams(vmem_limit_bytes=VMEM_LIMIT)
    return pltpu.CompilerParams(vmem_limit_bytes=VMEM_LIMIT, dimension_semantics=sem)


def _inproj_fwd(x2d, g_in, w_all, deps=()):
    T = x2d.shape[0]
    tm = min(1024, T)
    n_i = T // tm

    def body(*refs):
        x_ref, g_ref, w_ref = refs[:3]
        proj_ref, ht_ref, h_all = refs[-3:]
        i = pl.program_id(1)
        rows = pl.ds(pl.multiple_of(i * tm, tm), tm)

        @pl.when(pl.program_id(0) == 0)
        def _():
            x = x_ref[...]
            r = lax.rsqrt(jnp.mean(x * x, axis=-1, keepdims=True) + EPS)
            h = x * r * g_ref[...]
            h_all[rows, :] = h.astype(h_all.dtype)
            ht_ref[...] = h.T.astype(ht_ref.dtype)

        proj_ref[...] = _dot(h_all[rows, :], w_ref[0])

    first = lambda j, i: jnp.where(j == 0, i, n_i - 1)
    return pl.pallas_call(
        body,
        name="inproj_fwd",
        grid=(N_GROUPS, n_i),
        in_specs=[
            pl.BlockSpec((tm, D_MODEL), lambda j, i: (first(j, i), 0)),
            pl.BlockSpec((1, D_MODEL), lambda j, i: (0, 0)),
            pl.BlockSpec((1, D_MODEL, D_MODEL), lambda j, i: (j // 2, 0, j % 2)),
        ] + [pl.BlockSpec(memory_space=pl.ANY)] * len(deps),
        out_specs=[
            pl.BlockSpec((tm, D_MODEL), lambda j, i: (i, j)),
            pl.BlockSpec((D_MODEL, tm), lambda j, i: (0, first(j, i))),
        ],
        out_shape=[
            jax.ShapeDtypeStruct((T, N_GROUPS * D_MODEL), F32),
            jax.ShapeDtypeStruct((D_MODEL, T), _MXU),
        ],
        scratch_shapes=[pltpu.VMEM((T, D_MODEL), _MXU)],
        compiler_params=_params(("arbitrary", "arbitrary")),
    )(x2d, g_in, w_all, *deps)


def _scan_fwd(a, u):
    n = a.shape[0]
    row = lax.broadcasted_iota(jnp.int32, a.shape, 0)
    s = 1
    while s < n:
        m = row >= s
        u = u + a * jnp.where(m, pltpu.roll(u, s, 0), 0.0)
        a = a * jnp.where(m, pltpu.roll(a, s, 0), 1.0)
        s *= 2
    return a, u


def _scan_bwd(b, g):
    n = b.shape[0]
    row = lax.broadcasted_iota(jnp.int32, b.shape, 0)
    s = 1
    while s < n:
        m = row < n - s
        g = g + b * jnp.where(m, pltpu.roll(g, n - s, 0), 0.0)
        b = b * jnp.where(m, pltpu.roll(b, n - s, 0), 1.0)
        s *= 2
    return b, g


LANES = 128
SUBLANES = 8


def _scan_scratch(tc):
    by_lanes = pltpu.VMEM((CW // LANES, tc, LANES), F32)
    return [by_lanes, by_lanes, pltpu.VMEM((tc // SUBLANES, CW), F32), pltpu.VMEM((tc, CW), F32)]


def _scan_tile(a, u, edge, la_ref, lh_ref, c_ref, dst_ref, reverse):
    n, w = a.shape
    groups = n // SUBLANES
    a3 = a.reshape(groups, SUBLANES, w)
    u3 = u.reshape(groups, SUBLANES, w)
    row = lax.broadcasted_iota(jnp.int32, a3.shape, 1)
    for s in (1, 2, 4):
        m = (row < SUBLANES - s) if reverse else (row >= s)
        shift = SUBLANES - s if reverse else s
        u3 = u3 + a3 * jnp.where(m, pltpu.roll(u3, shift, 1), 0.0)
        a3 = a3 * jnp.where(m, pltpu.roll(a3, shift, 1), 1.0)
    al = a3.reshape(n, w)
    hl = u3.reshape(n, w)
    blocks = w // LANES
    for q in range(blocks):
        la_ref[q] = al[:, q * LANES:(q + 1) * LANES]
        lh_ref[q] = hl[:, q * LANES:(q + 1) * LANES]
    ends = pl.ds(0 if reverse else SUBLANES - 1, groups, stride=SUBLANES)
    end_a = jnp.concatenate([la_ref.at[q][ends, :] for q in range(blocks)], axis=-1)
    end_h = jnp.concatenate([lh_ref.at[q][ends, :] for q in range(blocks)], axis=-1)
    prod, part = (_scan_bwd if reverse else _scan_fwd)(end_a, end_h)
    total = part + prod * edge
    g_row = lax.broadcasted_iota(jnp.int32, total.shape, 0)
    if reverse:
        c_ref[...] = jnp.where(g_row == groups - 1, edge, pltpu.roll(total, groups - 1, 0))
    else:
        c_ref[...] = jnp.where(g_row == 0, edge, pltpu.roll(total, 1, 0))
    for g in range(groups):
        rows = slice(g * SUBLANES, (g + 1) * SUBLANES)
        for q in range(blocks):
            cols = slice(q * LANES, (q + 1) * LANES)
            dst_ref[rows, cols] = lh_ref[q, rows, :] + la_ref[q, rows, :] * c_ref[g:g + 1, cols]


def _softplus_neg(lam):
    z = -lam
    return jnp.maximum(z, 0.0) + jnp.log1p(jnp.exp(-jnp.abs(z)))


def _lru_gates(xc, wx_ref, wa_ref, bx_ref, ba_ref, lam_ref):
    xcb = _c(xc)
    i_t = _sigmoid(_dot(xcb, wx_ref[0]) + bx_ref[...])
    r_t = _sigmoid(_dot(xcb, wa_ref[0]) + ba_ref[...])
    sp = _softplus_neg(lam_ref[...])
    log_a = (-LRU_C) * r_t * sp
    a = jnp.exp(log_a)
    mult = jnp.sqrt(1.0 - a * a)
    return xcb, i_t, r_t, sp, a, mult


def _conv_from_ext(ext_ref, xa, cw_ref, cb_ref, tc):
    return (cb_ref[...] + cw_ref[3:4, :] * xa + cw_ref[2:3, :] * ext_ref[7:7 + tc, :]
            + cw_ref[1:2, :] * ext_ref[6:6 + tc, :] + cw_ref[0:1, :] * ext_ref[5:5 + tc, :])


def _lru_fwd(proj, conv_w, conv_b, wx_bd, wa_bd, bx, ba, lam, B, S):
    T = B * S
    tc = min(256, S)
    nt = S // tc
    h8 = tc // 8

    def body(xa_ref, halo_ref, ga_ref, cw_ref, cb_ref, wx_ref, wa_ref, bx_ref, ba_ref, lam_ref,
             h_ref, ya_ref, ext_ref, carry_ref, la_ref, lh_ref, c_ref):
        t = pl.program_id(2)

        @pl.when(t == 0)
        def _():
            carry_ref[...] = jnp.zeros_like(carry_ref)

        xa = xa_ref[...]
        ext_ref[0:8, :] = jnp.where(t == 0, 0.0, halo_ref[...])
        ext_ref[8:8 + tc, :] = xa
        xc = _conv_from_ext(ext_ref, xa, cw_ref, cb_ref, tc)
        _, i_t, _, _, a, mult = _lru_gates(xc, wx_ref, wa_ref, bx_ref, ba_ref, lam_ref)
        u = mult * (i_t * xc)
        _scan_tile(a, u, carry_ref[7:8, :], la_ref, lh_ref, c_ref, h_ref, False)
        h = h_ref[...]
        carry_ref[...] = h[tc - 8:tc, :]
        ga = ga_ref[...]
        ya_ref[...] = (ga * _sigmoid(ga) * h).astype(ya_ref.dtype)

    row = lambda b, t: b * nt + t
    vec = pl.BlockSpec((1, CW), lambda b, c, t: (0, c))
    mat = pl.BlockSpec((1, CW, CW), lambda b, c, t: (c, 0, 0))
    return pl.pallas_call(
        body,
        name="lru_fwd",
        grid=(B, N_CT, nt),
        in_specs=[
            pl.BlockSpec((tc, CW), lambda b, c, t: (row(b, t), c)),
            pl.BlockSpec((8, CW), lambda b, c, t: (jnp.maximum(row(b, t) * h8 - 1, 0), c)),
            pl.BlockSpec((tc, CW), lambda b, c, t: (row(b, t), N_CT + c)),
            pl.BlockSpec((CONV, CW), lambda b, c, t: (0, c)),
            vec, mat, mat, vec, vec, vec,
        ],
        out_specs=[
            pl.BlockSpec((tc, CW), lambda b, c, t: (row(b, t), c)),
            pl.BlockSpec((tc, CW), lambda b, c, t: (row(b, t), c)),
        ],
        out_shape=[
            jax.ShapeDtypeStruct((T, D_MODEL), F32),
            jax.ShapeDtypeStruct((T, D_MODEL), _MXU),
        ],
        scratch_shapes=[pltpu.VMEM((tc + 8, CW), F32), pltpu.VMEM((8, CW), F32)] + _scan_scratch(tc)[:3],
        compiler_params=_params(("parallel", "parallel", "arbitrary")),
    )(proj, proj, proj, conv_w, conv_b, wx_bd, wa_bd, bx, ba, lam)


def _lru_bwd(dya, proj, hlru, conv_w, conv_b, wx_bd, wa_bd, bx, ba, lam, B, S, deps=()):
    T = B * S
    tc = min(256, S)
    nt = S // tc
    h8 = tc // 8

    def body(dya_ref, xa_ref, xhalo_ref, ga_ref, h_ref, hhalo_ref, cw_ref, cb_ref, wx_ref, wa_ref, bx_ref, ba_ref,
             lam_ref, dxa_ref, dga_ref, dcw_ref, dcb_ref, dwx_ref, dwa_ref, dbx_ref, dba_ref, dlam_ref,
             ext_ref, ext2_ref, carry_ref, dhalo_ref, la_ref, lh_ref, c_ref, dh_ref):
        b = pl.program_id(1)
        t = pl.program_id(2)
        tt = nt - 1 - t

        @pl.when(t == 0)
        def _():
            carry_ref[...] = jnp.zeros_like(carry_ref)
            dhalo_ref[...] = jnp.zeros_like(dhalo_ref)

        @pl.when((t == 0) & (b == 0))
        def _():
            for r in (dcw_ref, dcb_ref, dwx_ref, dwa_ref, dbx_ref, dba_ref, dlam_ref):
                r[...] = jnp.zeros_like(r)

        xa = xa_ref[...]
        ext_ref[0:8, :] = jnp.where(tt == 0, 0.0, xhalo_ref[...])
        ext_ref[8:8 + tc, :] = xa
        xc = _conv_from_ext(ext_ref, xa, cw_ref, cb_ref, tc)
        xcb, i_t, r_t, sp, a, mult = _lru_gates(xc, wx_ref, wa_ref, bx_ref, ba_ref, lam_ref)

        h = h_ref[...]
        ga = ga_ref[...]
        dya_t = dya_ref[...]
        sg = _sigmoid(ga)
        dga_ref[...] = (dya_t * h * (sg * (1.0 + ga * (1.0 - sg)))).astype(dga_ref.dtype)
        dlru = dya_t * (ga * sg)

        row = lax.broadcasted_iota(jnp.int32, a.shape, 0)
        coef = jnp.where(row == tc - 1, 1.0, pltpu.roll(a, tc - 1, 0))
        _scan_tile(coef, dlru, carry_ref[0:1, :], la_ref, lh_ref, c_ref, dh_ref, True)
        dh = dh_ref[...]
        ext2_ref[0:tc, :] = a * dh
        carry_ref[...] = ext2_ref[0:8, :]

        ext2_ref[0:8, :] = jnp.where(tt == 0, 0.0, hhalo_ref[...])
        ext2_ref[8:8 + tc, :] = h
        hprev = ext2_ref[7:7 + tc, :]

        da = dh * hprev
        ix = i_t * xc
        dmult = dh * ix
        di = dh * mult * xc
        dxc = dh * mult * i_t
        dlog_a = da * a - dmult * (a * a) / mult
        dr = dlog_a * ((-LRU_C) * sp)
        dlam_ref[...] += jnp.sum(dlog_a * r_t, axis=0, keepdims=True) * (LRU_C * _sigmoid(-lam_ref[...]))
        dza = dr * r_t * (1.0 - r_t)
        dzx = di * i_t * (1.0 - i_t)
        dzab = _c(dza)
        dzxb = _c(dzx)
        dxc = dxc + _dot_nt(dzxb, wx_ref[0]) + _dot_nt(dzab, wa_ref[0])
        dwx_ref[0] += _dot_tn(xcb, dzxb)
        dwa_ref[0] += _dot_tn(xcb, dzab)
        dbx_ref[...] += jnp.sum(dzx, axis=0, keepdims=True)
        dba_ref[...] += jnp.sum(dza, axis=0, keepdims=True)

        dcb_ref[...] += jnp.sum(dxc, axis=0, keepdims=True)
        dcw_ref[3:4, :] += jnp.sum(dxc * xa, axis=0, keepdims=True)
        dcw_ref[2:3, :] += jnp.sum(dxc * ext_ref[7:7 + tc, :], axis=0, keepdims=True)
        dcw_ref[1:2, :] += jnp.sum(dxc * ext_ref[6:6 + tc, :], axis=0, keepdims=True)
        dcw_ref[0:1, :] += jnp.sum(dxc * ext_ref[5:5 + tc, :], axis=0, keepdims=True)
        ext2_ref[0:tc, :] = dxc
        ext2_ref[tc:tc + 8, :] = dhalo_ref[...]
        dxa = (cw_ref[3:4, :] * dxc + cw_ref[2:3, :] * ext2_ref[1:1 + tc, :]
               + cw_ref[1:2, :] * ext2_ref[2:2 + tc, :] + cw_ref[0:1, :] * ext2_ref[3:3 + tc, :])
        dxa_ref[...] = dxa.astype(dxa_ref.dtype)
        dhalo_ref[...] = ext2_ref[0:8, :]

    row_of = lambda b, t: b * nt + (nt - 1 - t)
    tile = lambda off: pl.BlockSpec((tc, CW), lambda c, b, t: (row_of(b, t), off + c))
    halo = pl.BlockSpec((8, CW), lambda c, b, t: (jnp.maximum(row_of(b, t) * h8 - 1, 0), c))
    vec = pl.BlockSpec((1, CW), lambda c, b, t: (0, c))
    mat = pl.BlockSpec((1, CW, CW), lambda c, b, t: (c, 0, 0))
    cwspec = pl.BlockSpec((CONV, CW), lambda c, b, t: (0, c))
    return pl.pallas_call(
        _after(body, 13, deps),
        name="lru_bwd",
        grid=(N_CT, B, nt),
        in_specs=[tile(0), tile(0), halo, tile(N_CT), tile(0), halo, cwspec, vec, mat, mat, vec, vec, vec]
        + [ANY_SPEC] * len(deps),
        out_specs=[tile(0), tile(0), cwspec, vec, mat, mat, vec, vec, vec],
        out_shape=[
            jax.ShapeDtypeStruct((T, D_MODEL), _MXU),
            jax.ShapeDtypeStruct((T, D_MODEL), _MXU),
            jax.ShapeDtypeStruct((CONV, D_MODEL), F32),
            jax.ShapeDtypeStruct((1, D_MODEL), F32),
            jax.ShapeDtypeStruct((N_CT, CW, CW), F32),
            jax.ShapeDtypeStruct((N_CT, CW, CW), F32),
            jax.ShapeDtypeStruct((1, D_MODEL), F32),
            jax.ShapeDtypeStruct((1, D_MODEL), F32),
            jax.ShapeDtypeStruct((1, D_MODEL), F32),
        ],
        scratch_shapes=[pltpu.VMEM((tc + 8, CW), F32), pltpu.VMEM((tc + 8, CW), F32),
                        pltpu.VMEM((8, CW), F32), pltpu.VMEM((8, CW), F32)] + _scan_scratch(tc),
        compiler_params=_params(("parallel", "arbitrary", "arbitrary")),
    )(dya, proj, proj, proj, hlru, hlru, conv_w, conv_b, wx_bd, wa_bd, bx, ba, lam, *deps)


def _retention_tables(S):
    half = DK // 2
    freqs = ROPE_THETA ** (-jnp.arange(half, dtype=F32) / half)
    ang = jnp.arange(S, dtype=F32)[:, None] * freqs[None, :]
    log_g = jnp.log1p(-(2.0 ** (-5.0 - jnp.arange(HEADS, dtype=F32))))
    idx = jnp.arange(CHUNK, dtype=F32)
    diff = idx[:, None] - idx[None, :]
    inner = jnp.where(diff >= 0, jnp.exp(jnp.maximum(diff, 0.0)[None] * log_g[:, None, None]), 0.0)
    cross = jnp.exp((idx[None, :] + 1.0) * log_g[:, None])[:, :, None]
    state = jnp.exp((CHUNK - 1.0 - idx[None, :]) * log_g[:, None])[:, :, None]
    gam = jnp.broadcast_to(jnp.exp(CHUNK * log_g)[:, None, None], (HEADS, 1, DK))
    return jnp.cos(ang), jnp.sin(ang), inner, cross, state, gam


def _rot(x, cos, sin):
    half = DK // 2
    x1, x2 = x[:, :half], x[:, half:]
    return jnp.concatenate([x1 * cos - x2 * sin, x1 * sin + x2 * cos], axis=-1)


def _rot_t(y, cos, sin):
    half = DK // 2
    y1, y2 = y[:, :half], y[:, half:]
    return jnp.concatenate([y1 * cos + y2 * sin, y2 * cos - y1 * sin], axis=-1)


def _groupnorm(o):
    mu = jnp.mean(o, axis=-1, keepdims=True)
    oc = o - mu
    rs = lax.rsqrt(jnp.mean(oc * oc, axis=-1, keepdims=True) + EPS)
    return oc * rs, rs


def _ret_specs(B, chunk_of):
    qkv = lambda g: pl.BlockSpec((B, CHUNK, D_MODEL), lambda c: (0, chunk_of(c), g))
    act = pl.BlockSpec((B, CHUNK, D_MODEL), lambda c: (0, chunk_of(c), 0))
    rope = pl.BlockSpec((CHUNK, DK // 2), lambda c: (chunk_of(c), 0))
    dmat = pl.BlockSpec((HEADS, CHUNK, CHUNK), lambda c: (0, 0, 0))
    dvec = pl.BlockSpec((HEADS, CHUNK, 1), lambda c: (0, 0, 0))
    hrow = pl.BlockSpec((HEADS, 1, DK), lambda c: (0, 0, 0))
    rst = pl.BlockSpec((1, B, HEADS, DK, DK), lambda c: (chunk_of(c), 0, 0, 0, 0))
    return qkv, act, rope, dmat, dvec, hrow, rst


def _ret_fwd(proj, tables, gain3, B, S):
    T = B * S
    nc = S // CHUNK
    cos, sin, dmat_t, cd_t, sd_t, gam_t = tables

    def body(q_ref, k_ref, v_ref, gb_ref, cos_ref, sin_ref, dm_ref, cd_ref, sd_ref, gam_ref, gain_ref,
             o_ref, yb_ref, rs_ref, state_ref):
        @pl.when(pl.program_id(0) == 0)
        def _():
            state_ref[...] = jnp.zeros_like(state_ref)

        cos_t, sin_t = cos_ref[...], sin_ref[...]
        for b, h in [(b, h) for b in range(B) for h in range(HEADS)]:
            cols = slice(h * DK, (h + 1) * DK)
            qb = _c(_rot(q_ref[b, :, cols], cos_t, sin_t))
            kb = _c(_rot(k_ref[b, :, cols], cos_t, sin_t) * (DK ** -0.5))
            v = v_ref[b, :, cols]
            state = state_ref[b, h]
            sb = _c(state)
            rs_ref[0, b, h] = sb
            scores = _dot_nt(qb, kb) * dm_ref[h]
            o = _dot(_c(scores), _c(v)) + _dot(qb, sb) * cd_ref[h]
            state_ref[b, h] = gam_ref[h] * state + _dot_tn(kb, _c(v * sd_ref[h]))
            o_ref[b, :, cols] = o
            n, _ = _groupnorm(o)
            gb = gb_ref[b, :, cols]
            yb_ref[b, :, cols] = (gb * _sigmoid(gb) * (n * gain_ref[h])).astype(yb_ref.dtype)

    qkv, act, rope, dmat, dvec, hrow, rst = _ret_specs(B, lambda c: c)
    proj3 = proj.reshape(B, S, proj.shape[1])
    o_pre, yb, states = pl.pallas_call(
        body,
        name="ret_fwd",
        grid=(nc,),
        in_specs=[qkv(2), qkv(3), qkv(4), qkv(5), rope, rope, dmat, dvec, dvec, hrow, hrow],
        out_specs=[act, act, rst],
        out_shape=[
            jax.ShapeDtypeStruct((B, S, D_MODEL), F32),
            jax.ShapeDtypeStruct((B, S, D_MODEL), _MXU),
            jax.ShapeDtypeStruct((nc, B, HEADS, DK, DK), _MXU),
        ],
        scratch_shapes=[pltpu.VMEM((B, HEADS, DK, DK), F32)],
        compiler_params=_params(("arbitrary",)),
    )(proj3, proj3, proj3, proj3, cos, sin, dmat_t, cd_t, sd_t, gam_t, gain3)
    return o_pre.reshape(T, D_MODEL), yb.reshape(T, D_MODEL), states


def _ret_bwd(dyb, o_pre, proj, states, tables, gain3, B, S, deps=()):
    T = B * S
    nc = S // CHUNK
    cos, sin, dmat_t, cd_t, sd_t, gam_t = tables

    def body(dyb_ref, o_ref, q_ref, k_ref, v_ref, gb_ref, rs_ref, cos_ref, sin_ref, dm_ref, cd_ref, sd_ref, gam_ref,
             gain_ref, dr_ref, dgain_ref, dstate_ref):
        @pl.when(pl.program_id(0) == 0)
        def _():
            dstate_ref[...] = jnp.zeros_like(dstate_ref)
            dgain_ref[...] = jnp.zeros_like(dgain_ref)

        cos_t, sin_t = cos_ref[...], sin_ref[...]
        for b, h in [(b, h) for b in range(B) for h in range(HEADS)]:
            cols = slice(h * DK, (h + 1) * DK)
            gain = gain_ref[h]
            n, rs = _groupnorm(o_ref[b, :, cols])
            gb = gb_ref[b, :, cols]
            sg = _sigmoid(gb)
            dy = dyb_ref[b, :, cols]
            part = lambda g: slice(g * D_MODEL + h * DK, g * D_MODEL + (h + 1) * DK)
            dr_ref[b, :, part(3)] = (dy * (n * gain) * (sg * (1.0 + gb * (1.0 - sg)))).astype(dr_ref.dtype)
            dgn = dy * (gb * sg)
            dgain_ref[h] += jnp.sum(dgn * n, axis=0, keepdims=True)
            dn = dgn * gain
            do = rs * (dn - jnp.mean(dn, axis=-1, keepdims=True) - n * jnp.mean(dn * n, axis=-1, keepdims=True))

            qb = _c(_rot(q_ref[b, :, cols], cos_t, sin_t))
            kb = _c(_rot(k_ref[b, :, cols], cos_t, sin_t) * (DK ** -0.5))
            v = v_ref[b, :, cols]
            vb = _c(v)
            vsb = _c(v * sd_ref[h])
            dob = _c(do)
            docb = _c(do * cd_ref[h])
            dmat = dm_ref[h]
            dstate = dstate_ref[b, h]
            dsb = _c(dstate)
            pb = _c(_dot_nt(qb, kb) * dmat)
            dsc = _c(_dot_nt(dob, vb) * dmat)
            dq = _dot(dsc, kb) + _dot_nt(docb, rs_ref[0, b, h])
            dk = _dot_tn(dsc, qb) + _dot_nt(vsb, dsb)
            dv = _dot_tn(pb, dob) + _dot(kb, dsb) * sd_ref[h]
            dstate_ref[b, h] = gam_ref[h] * dstate + _dot_tn(qb, docb)
            dr_ref[b, :, part(0)] = _rot_t(dq, cos_t, sin_t).astype(dr_ref.dtype)
            dr_ref[b, :, part(1)] = (_rot_t(dk, cos_t, sin_t) * (DK ** -0.5)).astype(dr_ref.dtype)
            dr_ref[b, :, part(2)] = dv.astype(dr_ref.dtype)

    qkv, act, rope, dmat, dvec, hrow, rst = _ret_specs(B, lambda c: nc - 1 - c)
    wide = pl.BlockSpec((B, CHUNK, 4 * D_MODEL), lambda c: (0, nc - 1 - c, 0))
    proj3 = proj.reshape(B, S, proj.shape[1])
    dr, dgain = pl.pallas_call(
        _after(body, 14, deps),
        name="ret_bwd",
        grid=(nc,),
        in_specs=[act, act, qkv(2), qkv(3), qkv(4), qkv(5), rst, rope, rope, dmat, dvec, dvec, hrow, hrow]
        + [ANY_SPEC] * len(deps),
        out_specs=[wide, hrow],
        out_shape=[jax.ShapeDtypeStruct((B, S, 4 * D_MODEL), _MXU), jax.ShapeDtypeStruct((HEADS, 1, DK), F32)],
        scratch_shapes=[pltpu.VMEM((B, HEADS, DK, DK), F32)],
        compiler_params=_params(("arbitrary",)),
    )(dyb.reshape(B, S, D_MODEL), o_pre.reshape(B, S, D_MODEL), proj3, proj3, proj3, proj3, states, cos, sin, dmat_t,
      cd_t, sd_t, gam_t, gain3, *deps)
    return dr.reshape(T, 4 * D_MODEL), dgain


def _mid(ya, yb, proj, x2d, tgt2d, wpa, wpb, wout, g_fin):
    T = x2d.shape[0]
    tm = min(256, T)
    n_steps = T // tm
    rows = D_MODEL // (2 * N_CHIPS)

    def body(ya_ref, yb_ref, ma_ref, mb_ref, x_ref, t_ref, gf_ref, wpa_hbm, wpb_hbm, wout_hbm,
             loss_ref, dx2_ref, dya_ref, dyb_ref, dm_ref, dgf_ref, gw_hbm, w_ref, acc_ref, sem):
        i = pl.program_id(0)

        @pl.when(i == 0)
        def _():
            loads = [pltpu.make_async_copy(src, w_ref.at[k], sem.at[k]) for k, src in enumerate((wpa_hbm, wpb_hbm, wout_hbm))]
            for cp in loads:
                cp.start()
            for cp in loads:
                cp.wait()
            acc_ref[...] = jnp.zeros_like(acc_ref)
            loss_ref[...] = jnp.zeros_like(loss_ref)
            dgf_ref[...] = jnp.zeros_like(dgf_ref)

        ya_t, yb_t = ya_ref[...], yb_ref[...]
        out_a = _dot(ya_t, w_ref[0])
        out_b = _dot(yb_t, w_ref[1])
        sa = _sigmoid(ma_ref[...])
        sb = _sigmoid(mb_ref[...])
        mgb = _c(sa * out_a + sb * out_b)
        x2 = x_ref[...] + _dot(mgb, w_ref[2])
        r2 = lax.rsqrt(jnp.mean(x2 * x2, axis=-1, keepdims=True) + EPS)
        nx = x2 * r2
        gf = gf_ref[...]
        err = nx * gf - t_ref[...]
        loss_ref[...] += 0.5 * jnp.sum(jnp.mean(err * err, axis=-1, keepdims=True), axis=0, keepdims=True)
        dy = err * (1.0 / D_MODEL)
        dgf_ref[...] += jnp.sum(dy * nx, axis=0, keepdims=True)
        dyg = dy * gf
        dx2 = r2 * (dyg - nx * jnp.mean(dyg * nx, axis=-1, keepdims=True))
        dx2_ref[...] = dx2
        dx2b = _c(dx2)
        dmg = _dot_nt(dx2b, w_ref[2])
        acc_ref[2] += _dot_tn(mgb, dx2b)
        dm_ref[:, :D_MODEL] = (dmg * out_a * sa * (1.0 - sa)).astype(dm_ref.dtype)
        dm_ref[:, D_MODEL:] = (dmg * out_b * sb * (1.0 - sb)).astype(dm_ref.dtype)
        dab = _c(dmg * sa)
        dbb = _c(dmg * sb)
        dya_ref[...] = _dot_nt(dab, w_ref[0])
        dyb_ref[...] = _dot_nt(dbb, w_ref[1])
        acc_ref[0] += _dot_tn(ya_t, dab)
        acc_ref[1] += _dot_tn(yb_t, dbb)

        @pl.when(i == n_steps - 1)
        def _():
            copies = [pltpu.make_async_copy(acc_ref.at[k, pl.ds((2 * p + hf) * rows, rows), :], gw_hbm.at[p, hf, k],
                                            sem.at[(k * N_CHIPS + p) * 2 + hf])
                      for k in range(3) for p in range(N_CHIPS) for hf in range(2)]
            for cp in copies:
                cp.start()
            for cp in copies:
                cp.wait()

    tile = lambda j: pl.BlockSpec((tm, D_MODEL), lambda i: (i, j))
    one = pl.BlockSpec((1, D_MODEL), lambda i: (0, 0))
    anyspec = pl.BlockSpec(memory_space=pl.ANY)
    return pl.pallas_call(
        body,
        name="mid",
        grid=(n_steps,),
        in_specs=[tile(0), tile(0), tile(6), tile(7), tile(0), tile(0), one, anyspec, anyspec, anyspec],
        out_specs=[pl.BlockSpec((1, 1), lambda i: (0, 0)), tile(0), tile(0), tile(0),
                   pl.BlockSpec((tm, 2 * D_MODEL), lambda i: (i, 0)), one, anyspec],
        out_shape=[
            jax.ShapeDtypeStruct((1, 1), F32),
            jax.ShapeDtypeStruct((T, D_MODEL), F32),
            jax.ShapeDtypeStruct((T, D_MODEL), F32),
            jax.ShapeDtypeStruct((T, D_MODEL), F32),
            jax.ShapeDtypeStruct((T, 2 * D_MODEL), _MXU),
            jax.ShapeDtypeStruct((1, D_MODEL), F32),
            jax.ShapeDtypeStruct((N_CHIPS, 2, 3, rows, D_MODEL), F32),
        ],
        scratch_shapes=[pltpu.VMEM((3, D_MODEL, D_MODEL), _MXU), pltpu.VMEM((3, D_MODEL, D_MODEL), F32),
                        pltpu.SemaphoreType.DMA((3 * N_CHIPS * 2,))],
        compiler_params=_params(("arbitrary",)),
    )(ya, yb, proj, proj, x2d, tgt2d, g_fin, wpa, wpb, wout)


DX_TILE = 512


def _inproj_bwd_dx(dparts, w_all, x2d, dx2, g_in, first, count, prev, name, deps=()):
    T = x2d.shape[0]
    tm = min(DX_TILE, T)
    n_d = len(dparts)
    groups = [(a, k) for a, d in enumerate(dparts) for k in range(d.shape[1] // D_MODEL)]
    dg_start = jnp.zeros((1, D_MODEL), F32) if prev is None else prev[1]
    carried = () if prev is None else (prev[0],)

    def body(*refs):
        d_refs = refs[:n_d]
        x_ref, dx2_ref, g_ref, dg0_ref, w_hbm = refs[n_d:n_d + 5]
        dx_ref, dg_ref, w_ref, sem = refs[-4:]

        @pl.when(pl.program_id(0) == 0)
        def _():
            cp = pltpu.make_async_copy(w_hbm, w_ref, sem)
            cp.start()
            cp.wait()
            dg_ref[...] = dg0_ref[...]

        dh = jnp.zeros((tm, D_MODEL), F32)
        for j, (a, k) in enumerate(groups):
            dh = dh + _dot_nt(d_refs[a][:, k * D_MODEL:(k + 1) * D_MODEL],
                              w_ref[j // 2, :, (j % 2) * D_MODEL:(j % 2 + 1) * D_MODEL])
        x = x_ref[...]
        r = lax.rsqrt(jnp.mean(x * x, axis=-1, keepdims=True) + EPS)
        nx = x * r
        dg_ref[...] += jnp.sum(dh * nx, axis=0, keepdims=True)
        dhg = dh * g_ref[...]
        dx_ref[...] = dx2_ref[...] + r * (dhg - nx * jnp.mean(dhg * nx, axis=-1, keepdims=True))

    tile = pl.BlockSpec((tm, D_MODEL), lambda i: (first + i, 0))
    one = pl.BlockSpec((1, D_MODEL), lambda i: (0, 0))
    return pl.pallas_call(
        body,
        name=name,
        grid=(count,),
        in_specs=[pl.BlockSpec((tm, d.shape[1]), lambda i: (first + i, 0)) for d in dparts]
        + [tile, tile, one, one, ANY_SPEC] + [ANY_SPEC] * (len(carried) + len(deps)),
        out_specs=[tile, one],
        out_shape=[jax.ShapeDtypeStruct((T, D_MODEL), F32), jax.ShapeDtypeStruct((1, D_MODEL), F32)],
        input_output_aliases={n_d + 5: 0} if carried else {},
        scratch_shapes=[pltpu.VMEM(w_all.shape, w_all.dtype), pltpu.SemaphoreType.DMA],
        compiler_params=_params(("arbitrary",)),
    )(*dparts, x2d, dx2, g_in, dg_start, w_all, *carried, *deps)


def _inproj_bwd_dw(ht, dparts, name, deps=()):
    T = ht.shape[1]
    tn = 512
    half = D_MODEL // 2
    per_chip = 2 * D_MODEL // tn
    n_d = len(dparts)
    tiles = [(a, t) for a, d in enumerate(dparts) for t in range(d.shape[1] // tn)]
    offs = [sum(d.shape[1] // tn for d in dparts[:a]) for a in range(n_d)]

    def body(*refs):
        ht_ref = refs[0]
        d_refs = refs[1:1 + n_d]
        out_ref = refs[-1]
        t = pl.program_id(0)

        for a in range(n_d):
            lo, hi = offs[a], offs[a] + dparts[a].shape[1] // tn

            @pl.when((t >= lo) & (t < hi))
            def _(a=a):
                g = _dot(ht_ref[...], d_refs[a][...])
                out_ref[0, 0] = g[:half]
                out_ref[0, 1] = g[half:]

    def dspec(a):
        n_a = dparts[a].shape[1] // tn
        return pl.BlockSpec((T, tn), lambda t: (0, jnp.clip(t - offs[a], 0, n_a - 1)))

    return pl.pallas_call(
        body,
        name=name,
        grid=(len(tiles),),
        in_specs=[pl.BlockSpec((D_MODEL, T), lambda t: (0, 0))] + [dspec(a) for a in range(n_d)]
        + [ANY_SPEC] * len(deps),
        out_specs=pl.BlockSpec((1, 2, half, tn), lambda t: (t // per_chip, 0, 0, t % per_chip)),
        out_shape=jax.ShapeDtypeStruct((len(tiles) // per_chip, 2, half, 2 * D_MODEL), F32),
        compiler_params=_params(("parallel",)),
    )(ht, *dparts, *deps)


def _coords():
    return lax.axis_index("x"), lax.axis_index("y"), lax.axis_index("c")


def _other_chips(x, y):
    return [(1 - x, y), (x, 1 - y), (1 - x, 1 - y)]


def _chunks(rows, n):
    size = rows // n
    return [pl.ds(q * size, size) for q in range(n)]


HBM_SPEC = pl.BlockSpec(memory_space=pltpu.HBM)
SEM_SPEC = pl.BlockSpec(memory_space=pltpu.SEMAPHORE)
DATAFLOW = pltpu.SideEffectType.DATAFLOW_SIDE_EFFECTING


def _copies_start(bufs, plan, n_copies, name):
    n = len(bufs)

    def body(*refs):
        ins = refs[:n]
        send_sems, recv_sems = refs[n], refs[n + 1]
        token = refs[-1]
        for k, send, _ in plan(ins):
            if send is not None:
                src, dst, dev, pred = send
                cp = pltpu.make_async_remote_copy(src_ref=src, dst_ref=dst, send_sem=send_sems.at[k],
                                                  recv_sem=recv_sems.at[k], device_id=dev, device_id_type=MESH)
                if pred is None:
                    cp.start()
                else:
                    pl.when(pred)(cp.start)
        token[...] = jnp.zeros_like(token)

    hbm = [pltpu.with_memory_space_constraint(b, pltpu.HBM) for b in bufs]
    outs = pl.pallas_call(
        body,
        name=name,
        in_specs=[HBM_SPEC] * n,
        out_specs=(SEM_SPEC, SEM_SPEC, *([HBM_SPEC] * n), pl.BlockSpec(memory_space=pltpu.VMEM)),
        out_shape=(pltpu.SemaphoreType.DMA((n_copies,)), pltpu.SemaphoreType.DMA((n_copies,)),
                   *[pltpu.HBM(b.shape, b.dtype) for b in bufs], jax.ShapeDtypeStruct((8, 128), F32)),
        input_output_aliases={a: 2 + a for a in range(n)},
        compiler_params=pltpu.CompilerParams(has_side_effects=DATAFLOW),
    )(*hbm)
    return outs[0], outs[1], list(outs[2:2 + n]), outs[-1]


def _copies_wait(send_sems, recv_sems, bufs, after, plan, name):
    n = len(bufs)

    def body(*refs):
        ins = refs[:n]
        s_sems, r_sems = refs[n], refs[n + 1]
        for k, send, recv in plan(ins):
            if send is not None:
                src, dst, dev, pred = send
                cp = pltpu.make_async_remote_copy(src_ref=src, dst_ref=dst, send_sem=s_sems.at[k],
                                                  recv_sem=r_sems.at[k], device_id=dev, device_id_type=MESH)
                if pred is None:
                    cp.wait_send()
                else:
                    pl.when(pred)(cp.wait_send)
            if recv is not None:
                dst, pred = recv
                cp = pltpu.make_async_remote_copy(src_ref=dst, dst_ref=dst, send_sem=s_sems.at[k],
                                                  recv_sem=r_sems.at[k], device_id=_coords(), device_id_type=MESH)
                if pred is None:
                    cp.wait_recv()
                else:
                    pl.when(pred)(cp.wait_recv)

    outs = pl.pallas_call(
        body,
        name=name,
        in_specs=[HBM_SPEC] * n + [SEM_SPEC, SEM_SPEC, pl.BlockSpec(memory_space=pl.ANY)],
        out_specs=[HBM_SPEC] * n,
        out_shape=[pltpu.HBM(b.shape, b.dtype) for b in bufs],
        input_output_aliases={a: a for a in range(n)},
        compiler_params=pltpu.CompilerParams(has_side_effects=DATAFLOW),
    )(*bufs, send_sems, recv_sems, after)
    return list(outs)


def _gather_plan(n_bufs):
    def plan(refs):
        x, y, c = _coords()
        me = 2 * x + y
        out = []
        for k, (px, py) in enumerate(_other_chips(x, y)):
            for a in range(n_bufs):
                out.append((k * n_bufs + a, (refs[a].at[me], refs[a].at[me], (px, py, c), None),
                            (refs[a].at[2 * px + py], None)))
        return out
    return plan


def _cast_into_slot(ws, name):
    n = len(ws)
    nt = 2

    def body(s_ref, *refs):
        for a in range(n):
            refs[n + a][0] = refs[a][...].astype(refs[n + a].dtype)

    xi, yi, _ = _coords()
    return pl.pallas_call(
        body,
        name=name,
        grid_spec=pltpu.PrefetchScalarGridSpec(
            num_scalar_prefetch=1,
            grid=(2, nt),
            in_specs=[pl.BlockSpec((1, w.shape[1] // nt, w.shape[2]), lambda hf, i, s: (hf, i, 0)) for w in ws],
            out_specs=[pl.BlockSpec((1, 1, w.shape[1] // nt, w.shape[2]), lambda hf, i, s: (s[0], hf, i, 0)) for w in ws],
        ),
        out_shape=[jax.ShapeDtypeStruct((N_CHIPS,) + w.shape, _MXU) for w in ws],
        compiler_params=_params(("parallel", "parallel")),
    )((2 * xi + yi).reshape(1).astype(jnp.int32), *ws)


def _gather_chips(bufs, n_chunks, name):
    n = len(bufs)
    pieces = [(a, rows) for a in range(n) for rows in _chunks(bufs[a].shape[2], n_chunks[a])]
    n_p = len(pieces)

    def body(*refs):
        outs = refs[n:2 * n]
        send_sems, recv_sems, fsend_sems, frecv_sems = refs[2 * n:]
        x, y, c = _coords()
        me = 2 * x + y
        near = [(1 - x, y), (x, 1 - y)]
        slots = [2 * (1 - x) + y, 2 * x + (1 - y), 2 * (1 - x) + (1 - y)]
        pass_to = (jnp.where(c == 0, x, 1 - x), jnp.where(c == 0, 1 - y, y))
        pass_slot = jnp.where(c == 0, slots[0], slots[1])

        def send(k, i, slot, chip):
            a, rows = pieces[i]
            return pltpu.make_async_remote_copy(
                src_ref=outs[a].at[slot, c, rows], dst_ref=outs[a].at[slot, c, rows], send_sem=send_sems.at[k * n_p + i],
                recv_sem=recv_sems.at[k * n_p + i], device_id=(*chip, c), device_id_type=MESH)

        def forward(k, i, half):
            a, rows = pieces[i]
            return pltpu.make_async_remote_copy(
                src_ref=outs[a].at[slots[k], half, rows], dst_ref=outs[a].at[slots[k], half, rows],
                send_sem=fsend_sems.at[k * n_p + i], recv_sem=frecv_sems.at[k * n_p + i],
                device_id=(x, y, 1 - c), device_id_type=MESH)

        started = [send(k, i, me, chip) for i in range(n_p) for k, chip in enumerate(near)]
        for cp in started:
            cp.start()
        for i in range(n_p):
            for k, chip in enumerate(near):
                send(k, i, slots[k], chip).wait_recv()
            later = [send(2, i, pass_slot, pass_to), forward(0, i, c), forward(1, i, c)]
            for cp in later:
                cp.start()
            started += later
        for i in range(n_p):
            send(2, i, slots[2], pass_to).wait_recv()
            fw = forward(2, i, c)
            fw.start()
            started.append(fw)
        for i in range(n_p):
            for k in range(3):
                forward(k, i, 1 - c).wait_recv()
        for cp in started:
            cp.wait_send()

    anyspec = pl.BlockSpec(memory_space=pl.ANY)
    sems = pltpu.SemaphoreType.DMA((3 * n_p,))
    return pl.pallas_call(
        body,
        name=name,
        in_specs=[anyspec] * n,
        out_specs=[anyspec] * n,
        out_shape=[jax.ShapeDtypeStruct(b.shape, b.dtype) for b in bufs],
        input_output_aliases={a: a for a in range(n)},
        scratch_shapes=[sems, sems, sems, sems],
    )(*bufs)


def _swap_plan(n_slabs):
    def plan(refs):
        x, y, c = _coords()
        out, k = [], 0
        for i, n in enumerate(n_slabs):
            g, land = refs[2 * i], refs[2 * i + 1]
            for p in range(n):
                out.append((k, (g.at[p, 1 - c], land.at[p], (x, y, 1 - c), None), (land.at[p], None)))
                k += 1
        return out
    return plan


def _is_one_of(chip, dests):
    hit = chip == dests[0]
    for d in dests[1:]:
        hit = hit | (chip == d)
    return hit


def _slab_of(chip, dests):
    return sum(j * (chip == d).astype(jnp.int32) for j, d in enumerate(dests))


def _scatter_plan(dest_sets):
    def plan(refs):
        x, y, c = _coords()
        me = 2 * x + y
        out = []
        for k, (px, py) in enumerate(_other_chips(x, y)):
            peer = 2 * px + py
            for i, dests in enumerate(dest_sets):
                cs, land = refs[2 * i], refs[2 * i + 1]
                everyone = len(dests) == N_CHIPS
                send = (cs.at[_slab_of(peer, dests)], land.at[k], (px, py, c),
                        None if everyone else _is_one_of(peer, dests))
                recv = (land.at[k], None if everyone else _is_one_of(me, dests))
                out.append((k * len(dest_sets) + i, send, recv))
        return out
    return plan


def _join_plans(parts):
    def plan(refs):
        out, b0, k0 = [], 0, 0
        for part_plan, n_bufs, n_copies in parts:
            out += [(k0 + k, send, recv) for k, send, recv in part_plan(refs[b0:b0 + n_bufs])]
            b0 += n_bufs
            k0 += n_copies
        return out
    return plan


def _allgather_plan():
    def plan(refs):
        x, y, c = _coords()
        (land,) = refs
        me = 4 * x + 2 * y + c
        out = []
        for r in range(1, 8):
            px = 1 - x if r & 4 else x
            py = 1 - y if r & 2 else y
            pc = 1 - c if r & 1 else c
            out.append((r - 1, (land.at[me], land.at[me], (px, py, pc), None), (land.at[4 * px + 2 * py + pc], None)))
        return out
    return plan


def _sum_gathered(land, name):
    def body(land_ref, o_ref):
        acc = land_ref[0]
        for d in range(1, 8):
            acc = acc + land_ref[d]
        o_ref[...] = acc

    return pl.pallas_call(
        body,
        name=name,
        out_shape=jax.ShapeDtypeStruct(land.shape[1:], F32),
        compiler_params=_params(),
    )(land)


def _join_halves(bufs, n_chunks, name):
    n = len(bufs)
    pieces = [(a, rows) for a in range(n) for rows in _chunks(bufs[a].shape[1], n_chunks[a])]
    n_p = len(pieces)

    def body(*refs):
        outs = refs[n:2 * n]
        send_sems, recv_sems = refs[2 * n:]
        x, y, c = _coords()

        def copy(i, half):
            a, rows = pieces[i]
            return pltpu.make_async_remote_copy(
                src_ref=outs[a].at[half, rows], dst_ref=outs[a].at[half, rows], send_sem=send_sems.at[i],
                recv_sem=recv_sems.at[i], device_id=(x, y, 1 - c), device_id_type=MESH)

        sends = [copy(i, c) for i in range(n_p)]
        for cp in sends:
            cp.start()
        for i in range(n_p):
            copy(i, 1 - c).wait_recv()
        for cp in sends:
            cp.wait_send()

    anyspec = pl.BlockSpec(memory_space=pl.ANY)
    sems = pltpu.SemaphoreType.DMA((n_p,))
    return pl.pallas_call(
        body,
        name=name,
        in_specs=[anyspec] * n,
        out_specs=[anyspec] * n,
        out_shape=[jax.ShapeDtypeStruct(b.shape, b.dtype) for b in bufs],
        input_output_aliases={a: a for a in range(n)},
        scratch_shapes=[sems, sems],
    )(*bufs)


def _row_tile(rows, cap):
    t = cap
    while rows % t:
        t //= 2
    return t


def _add_my_half(g, r, name):
    n_slabs, _, R, C = g.shape
    tr = R if n_slabs > 1 else _row_tile(R, 256)

    def body(c_ref, g_ref, r_ref, o_ref):
        o_ref[...] = (g_ref[0] + r_ref[...]).astype(o_ref.dtype)

    return pl.pallas_call(
        body,
        name=name,
        grid_spec=pltpu.PrefetchScalarGridSpec(
            num_scalar_prefetch=1,
            grid=(n_slabs, R // tr),
            in_specs=[pl.BlockSpec((1, 1, tr, C), lambda p, i, c_ref: (p, c_ref[0], i, 0)),
                      pl.BlockSpec((1, tr, C), lambda p, i, c_ref: (p, i, 0))],
            out_specs=pl.BlockSpec((1, tr, C), lambda p, i, c_ref: (p, i, 0)),
        ),
        out_shape=jax.ShapeDtypeStruct(r.shape, jnp.bfloat16),
        compiler_params=_params(("parallel", "parallel")),
    )(lax.axis_index("c").reshape(1).astype(jnp.int32), g, r)


def _sum_slabs(own, got, name, deps=()):
    _, R, C = own.shape
    tr = _row_tile(R, 256)

    def body(s_ref, own_ref, got_ref, *rest):
        rest[-1][0] = ((own_ref[0].astype(F32) + got_ref[0].astype(F32)) + got_ref[1].astype(F32)) + got_ref[2].astype(F32)

    xi, yi, ci = _coords()
    return pl.pallas_call(
        body,
        name=name,
        grid_spec=pltpu.PrefetchScalarGridSpec(
            num_scalar_prefetch=1,
            grid=(R // tr,),
            in_specs=[pl.BlockSpec((1, tr, C), lambda i, s: (s[0], i, 0)),
                      pl.BlockSpec((3, tr, C), lambda i, s: (0, i, 0))] + [ANY_SPEC] * len(deps),
            out_specs=pl.BlockSpec((1, tr, C), lambda i, s: (s[1], i, 0)),
        ),
        out_shape=jax.ShapeDtypeStruct((2, R, C), F32),
        compiler_params=_params(("parallel",)),
    )(jnp.stack([2 * xi + yi, ci]).astype(jnp.int32), own, got, *deps)


def _sum_parts(owns, got, dest_sets, name):
    n = len(owns)
    _, R, C = owns[0].shape
    tr = _row_tile(R, 256)

    def body(s_ref, *refs):
        got_ref, o_ref = refs[n], refs[-1]
        total = jnp.zeros((tr, C), F32)
        for i in range(n):
            total = total + jnp.where(s_ref[2 + 2 * i] == 1, refs[i][0].astype(F32), 0.0)
        o_ref[0] = ((total + got_ref[0].astype(F32)) + got_ref[1].astype(F32)) + got_ref[2].astype(F32)

    xi, yi, ci = _coords()
    me = 2 * xi + yi
    scalars = [ci, ci]
    for dests in dest_sets:
        scalars += [_is_one_of(me, dests).astype(jnp.int32), _slab_of(me, dests)]
    own_spec = lambda i: pl.BlockSpec((1, tr, C), lambda r, s: (s[3 + 2 * i], r, 0))
    return pl.pallas_call(
        body,
        name=name,
        grid_spec=pltpu.PrefetchScalarGridSpec(
            num_scalar_prefetch=1,
            grid=(R // tr,),
            in_specs=[own_spec(i) for i in range(n)] + [pl.BlockSpec((3, tr, C), lambda r, s: (0, r, 0))],
            out_specs=pl.BlockSpec((1, tr, C), lambda r, s: (s[0], r, 0)),
        ),
        out_shape=jax.ShapeDtypeStruct((2, R, C), F32),
        compiler_params=_params(("parallel",)),
    )(jnp.stack(scalars).astype(jnp.int32), *owns, got)


def _adamw_math(w, g, m, v):
    m = ADAM_B1 * m + (1.0 - ADAM_B1) * g
    v = ADAM_B2 * v + (1.0 - ADAM_B2) * (g * g)
    m_hat = m / (1.0 - ADAM_B1 ** ADAM_STEP)
    v_hat = v / (1.0 - ADAM_B2 ** ADAM_STEP)
    delta = -ADAM_LR * (m_hat / (jnp.sqrt(v_hat) + ADAM_EPS) + ADAM_WD * w)
    return delta, m, v


def _adamw_big(w, g, m, v, name):
    R, C = w.shape
    tr = min(128, R)

    def body(w_ref, g_ref, m_ref, v_ref, g_out, d_out, m_out, v_out):
        g = g_ref[...]
        d, mn, vn = _adamw_math(w_ref[...], g, m_ref[...], v_ref[...])
        g_out[...] = g
        d_out[...] = d
        m_out[...] = mn
        v_out[...] = vn

    spec = pl.BlockSpec((tr, C), lambda i: (i, 0))
    return pl.pallas_call(
        body,
        name=name,
        grid=(R // tr,),
        in_specs=[spec] * 4,
        out_specs=[spec] * 4,
        out_shape=[jax.ShapeDtypeStruct((R, C), F32)] * 4,
        compiler_params=_params(("parallel",)),
    )(w, g, m, v)


def _adamw_small(ws, gs, ms, vs, name):
    n = len(ws)

    def body(*refs):
        for a in range(n):
            d, mn, vn = _adamw_math(refs[a][...], refs[n + a][...], refs[2 * n + a][...], refs[3 * n + a][...])
            refs[4 * n + a][...] = d
            refs[5 * n + a][...] = mn
            refs[6 * n + a][...] = vn

    shapes = [jax.ShapeDtypeStruct(w.shape, F32) for w in ws]
    outs = pl.pallas_call(
        body,
        name=name,
        out_shape=shapes * 3,
        compiler_params=_params(),
    )(*ws, *gs, *ms, *vs)
    return outs[:n], outs[n:2 * n], outs[2 * n:]


def _to_blockdiag(w):
    per = CW // LRU_BW
    w4 = w.reshape(N_CT, per, LRU_BW, LRU_BW)
    eye = jnp.eye(per, dtype=w.dtype)
    return (w4[:, :, :, None, :] * eye[None, :, None, :, None]).reshape(N_CT, CW, CW)


def _from_blockdiag(g):
    per = CW // LRU_BW
    g5 = g.reshape(N_CT, per, LRU_BW, per, LRU_BW)
    return jnp.stack([g5[:, b, :, b, :] for b in range(per)], axis=1).reshape(LRU_BLOCKS, LRU_BW, LRU_BW)


def _local_grads(x2d, tgt2d, B, S, g_in, w_all, conv_w, conv_b, gate_x_w, gate_x_b, gate_a_w, gate_a_b, lam, gain,
                 proj_weights, g_fin, reduce, deps=()):
    wx_bd = _c(_to_blockdiag(gate_x_w))
    wa_bd = _c(_to_blockdiag(gate_a_w))
    tables = _retention_tables(S)
    gain3 = gain.reshape(HEADS, 1, DK)

    proj, ht = _inproj_fwd(x2d, g_in, w_all, deps)
    hlru, ya = _lru_fwd(proj, conv_w, conv_b, wx_bd, wa_bd, gate_x_b, gate_a_b, lam, B, S)
    o_pre, yb, states = _ret_fwd(proj, tables, gain3, B, S)
    wpa, wpb, wout = proj_weights(yb)
    loss, dx2, dya, dyb, dm, dgf, gw_proj = _mid(ya, yb, proj, x2d, tgt2d, wpa, wpb, wout, g_fin)
    g3 = _inproj_bwd_dw(ht, [dm], "inproj_bwd_dw_m")
    deps = reduce.m_ready(gw_proj, g3)
    dr, dgain = _ret_bwd(dyb, o_pre, proj, states, tables, gain3, B, S, deps)
    deps = reduce.ret_done(dr)
    g12 = _inproj_bwd_dw(ht, [dr], "inproj_bwd_dw_r", deps)
    deps = reduce.r_ready(g12)
    dxa, dga, dcw, dcb, dwx_bd, dwa_bd, dbx, dba, dlam = _lru_bwd(
        dya, proj, hlru, conv_w, conv_b, wx_bd, wa_bd, gate_x_b, gate_a_b, lam, B, S, deps)
    small = dict(conv_w=dcw, conv_b=dcb, gate_x_w=_from_blockdiag(dwx_bd), gate_x_b=dbx,
                 gate_a_w=_from_blockdiag(dwa_bd), gate_a_b=dba, lru_lambda=dlam, gn_gain=dgain.reshape(HEADS, DK),
                 norm_final=dgf)
    loss_rows = jnp.broadcast_to(loss, (SUBLANES, LANES))
    deps = reduce.lru_done(dxa, jnp.concatenate([_pack_small(small), loss_rows], axis=0))
    g0 = _inproj_bwd_dw(ht, [dxa, dga], "inproj_bwd_dw_a", deps)
    deps = reduce.a_ready(g0)
    n_tiles = x2d.shape[0] // min(DX_TILE, x2d.shape[0])
    grad_x, dgin = _inproj_bwd_dx([dxa, dga, dr, dm], w_all, x2d, dx2, g_in, 0, n_tiles, None, "inproj_bwd_dx", deps)
    return grad_x, dgin


ALL_CHIPS = (0, 1, 2, 3)


class _GradReduce:
    def __init__(self, proj_done):
        self.pending = {}
        self.proj_done = proj_done
        self.land_in = None

    def _start(self, key, parts, name):
        bufs, plans, shared = [], [], None
        for part_bufs, plan, n_copies, part_shared in parts:
            if part_shared is not None:
                shared = len(bufs) + part_shared
            plans.append((plan, len(part_bufs), n_copies))
            bufs += part_bufs
        plan = _join_plans(plans)
        send_sems, recv_sems, bufs, token = _copies_start(bufs, plan, sum(p[2] for p in plans), name + "_start")
        if shared is not None:
            self.land_in = bufs[shared]
        self.pending[key] = (send_sems, recv_sems, bufs, plan, name + "_wait", shared)
        return (token,)

    def _finish(self, key, after):
        send_sems, recv_sems, bufs, plan, name, shared = self.pending.pop(key)
        if shared is not None:
            bufs[shared] = self.land_in
        bufs = _copies_wait(send_sems, recv_sems, bufs, after, plan, name)
        if shared is not None:
            self.land_in = bufs[shared]
        return bufs

    @staticmethod
    def _swap(pieces):
        bufs = []
        for g in pieces:
            bufs += [g, lax.empty((g.shape[0],) + g.shape[2:], F32)]
        n_slabs = [g.shape[0] for g in pieces]
        return bufs, _swap_plan(n_slabs), sum(n_slabs), None

    def _scatter(self, sums, dest_sets):
        bufs = []
        for cs in sums:
            bufs += [cs, lax.empty((3,) + cs.shape[1:], cs.dtype)]
        if self.land_in is not None:
            bufs[-1] = self.land_in
        return bufs, _scatter_plan(dest_sets), 3 * len(sums), len(bufs) - 1

    @staticmethod
    def _gather8(block):
        x, y, c = _coords()
        land = lax.dynamic_update_slice(lax.empty((8,) + block.shape, F32), block[None], (4 * x + 2 * y + c, 0, 0))
        return [land], _allgather_plan(), 7, None

    def m_ready(self, gw_proj, g3):
        rows = gw_proj.shape[2] * gw_proj.shape[3]
        return self._start("m", [self._swap([gw_proj.reshape(N_CHIPS, 2, rows, D_MODEL), g3])], "swap_m")

    def ret_done(self, after):
        proj, land_p, g3, land_3 = self._finish("m", after)
        self.sums_m = [_add_my_half(proj, land_p, "chip_sum_proj"), _add_my_half(g3, land_3, "chip_sum_m")]
        return ()

    def r_ready(self, g12):
        return self._start("r", [self._scatter(self.sums_m, [ALL_CHIPS, (3,)]), self._swap([g12])], "scatter_m_swap_r")

    def lru_done(self, after, packed):
        self.csp, self.gotp, self.cs3, _, g12, land_12 = self._finish("r", after)
        sums_r = [_add_my_half(g12, land_12, "chip_sum_r")]
        return self._start("s", [self._scatter(sums_r, [(1, 2)]), self._gather8(packed)], "scatter_r_gather_small")

    def a_ready(self, g0):
        deps = self._start("a", [self._swap([g0])], "swap_a")
        (g_proj,) = _join_halves([_sum_slabs(self.csp, self.gotp, "sum_w_proj", deps)], [4], "join_halves_proj")
        g0, land_0 = self._finish("a", self.proj_done(g_proj))
        return self._start("sa", [self._scatter([_add_my_half(g0, land_0, "chip_sum_a")], [(0,)])], "scatter_a")

    def finish(self, dgin, w_in_done):
        (token,) = self._start("n", [self._gather8(dgin)], "gather_norm_in")
        cs12, _, small = self._finish("s", token)
        cs0, _ = self._finish("sa", token)
        half_in = _sum_parts([self.cs3, cs12, cs0], self.land_in, [(3,), (1, 2), (0,)], "sum_w_in")
        after = w_in_done(_join_halves([half_in], [8], "join_halves")[0])
        (norm_in,) = self._finish("n", after)
        return _sum_gathered(small, "sum_small_grads"), _sum_gathered(norm_in, "sum_norm_in_grad")


_SMALL = ("gate_x_w", "gate_a_w", "conv_w", "conv_b", "gate_x_b", "gate_a_b", "lru_lambda", "gn_gain", "norm_final")
_SMALL_SHAPES = dict(gate_x_w=(LRU_BLOCKS, LRU_BW, LRU_BW), gate_a_w=(LRU_BLOCKS, LRU_BW, LRU_BW),
                     norm_in=(1, D_MODEL), conv_w=(CONV, D_MODEL), conv_b=(1, D_MODEL), gate_x_b=(1, D_MODEL),
                     gate_a_b=(1, D_MODEL), lru_lambda=(1, D_MODEL), gn_gain=(HEADS, DK), norm_final=(1, D_MODEL))


def _pack_small(small):
    return jnp.concatenate([small[k].reshape(-1, 128) for k in _SMALL], axis=0)


def _unpack_small(packed):
    out, r = {}, 0
    for k in _SMALL:
        shape = _SMALL_SHAPES[k]
        rows = 1
        for s in shape:
            rows *= s
        rows //= 128
        out[k] = packed[r:r + rows].reshape(shape)
        r += rows
    return out


def kernel(x, norm_in, w_in, conv_w, conv_b, gate_x_w, gate_x_b, gate_a_w, gate_a_b, lru_lambda, gn_gain, w_proj_a, w_proj_b, w_out, norm_final, loss_target, m_norm_in, m_w_in, m_conv_w, m_conv_b, m_gate_x_w, m_gate_x_b, m_gate_a_w, m_gate_a_b, m_lru_lambda, m_gn_gain, m_w_proj_a, m_w_proj_b, m_w_out, m_norm_final, v_norm_in, v_w_in, v_conv_w, v_conv_b, v_gate_x_w, v_gate_x_b, v_gate_a_w, v_gate_a_b, v_lru_lambda, v_gn_gain, v_w_proj_a, v_w_proj_b, v_w_out, v_norm_final):
    B, S, _ = x.shape
    T = B * S
    xi, yi, ci = _coords()
    chip = 2 * xi + yi

    cshard = D_MODEL // N_CHIPS
    mine = _cast_into_slot([w_in[0].reshape(2, D_MODEL // 2, 2 * D_MODEL)]
                           + [w[0].reshape(2, cshard // 2, D_MODEL) for w in (w_proj_a, w_proj_b, w_out)],
                           "cast_weights")
    plan = _gather_plan(3)
    s_sems, r_sems, pbufs, token = _copies_start(mine[1:], plan, 9, "gather_proj_start")
    gshard = DK // N_CHIPS
    tiny = jnp.concatenate([conv_w[0], jnp.zeros((4, cshard), F32), jnp.pad(gn_gain[0], ((0, 4), (0, cshard - gshard)))],
                           axis=0).reshape(1, 2, SUBLANES, cshard)
    tiny_buf = lax.dynamic_update_slice(lax.empty((N_CHIPS, 2, SUBLANES, cshard), F32), tiny, (chip, 0, 0, 0))
    w_buf, tiny_buf = _gather_chips([mine[0], tiny_buf], [8, 1], "gather_weights")
    w_all = w_buf.reshape(N_CHIPS, D_MODEL, 2 * D_MODEL)
    tiny_all = tiny_buf.reshape(N_CHIPS, 2 * SUBLANES, cshard)

    def proj_weights(after):
        got = _copies_wait(s_sems, r_sems, pbufs, after, plan, "gather_proj_wait")
        return [b.reshape(D_MODEL, D_MODEL) for b in got]

    conv_w_full = jnp.transpose(tiny_all[:, 0:CONV, :], (1, 0, 2)).reshape(CONV, D_MODEL)
    gain_full = jnp.transpose(tiny_all[:, 8:8 + HEADS, :gshard], (1, 0, 2)).reshape(HEADS, DK)

    weights = dict(norm_in=norm_in, w_in=w_in, conv_w=conv_w, conv_b=conv_b, gate_x_w=gate_x_w, gate_x_b=gate_x_b,
                   gate_a_w=gate_a_w, gate_a_b=gate_a_b, lru_lambda=lru_lambda, gn_gain=gn_gain, w_proj_a=w_proj_a,
                   w_proj_b=w_proj_b, w_out=w_out, norm_final=norm_final)
    ms = dict(norm_in=m_norm_in, w_in=m_w_in, conv_w=m_conv_w, conv_b=m_conv_b, gate_x_w=m_gate_x_w,
              gate_x_b=m_gate_x_b, gate_a_w=m_gate_a_w, gate_a_b=m_gate_a_b, lru_lambda=m_lru_lambda, gn_gain=m_gn_gain,
              w_proj_a=m_w_proj_a, w_proj_b=m_w_proj_b, w_out=m_w_out, norm_final=m_norm_final)
    vs = dict(norm_in=v_norm_in, w_in=v_w_in, conv_w=v_conv_w, conv_b=v_conv_b, gate_x_w=v_gate_x_w,
              gate_x_b=v_gate_x_b, gate_a_w=v_gate_a_w, gate_a_b=v_gate_a_b, lru_lambda=v_lru_lambda, gn_gain=v_gn_gain,
              w_proj_a=v_w_proj_a, w_proj_b=v_w_proj_b, w_out=v_w_out, norm_final=v_norm_final)
    names = list(weights)
    grads, delta, new_m, new_v = {}, {}, {}, {}

    def update_big(k, g):
        shp = weights[k].shape
        two = lambda a: a.reshape(shp[1], shp[2])
        g, d, mn, vn = _adamw_big(two(weights[k]), g, two(ms[k]), two(vs[k]), "adamw_" + k)
        grads[k], delta[k], new_m[k], new_v[k] = g.reshape(shp), d.reshape(shp), mn.reshape(shp), vn.reshape(shp)
        return d

    def proj_done(g_proj):
        g_pr = g_proj.reshape(2, 3, D_MODEL // (2 * N_CHIPS), D_MODEL)
        for i, k in enumerate(("w_proj_a", "w_proj_b", "w_out")):
            last = update_big(k, g_pr[:, i].reshape(cshard, D_MODEL))
        return last

    reduce = _GradReduce(proj_done)
    grad_x, dgin = _local_grads(
        x.reshape(T, D_MODEL), loss_target.reshape(T, D_MODEL), B, S, norm_in, w_all, conv_w_full, conv_b,
        gate_x_w[0], gate_x_b, gate_a_w[0], gate_a_b, lru_lambda, gain_full, proj_weights,
        norm_final.reshape(1, D_MODEL), reduce, deps=(token,))

    small_sum, g_norm_in = reduce.finish(dgin.reshape(SUBLANES, LANES),
                                         lambda g: update_big("w_in", g.reshape(D_MODEL, 2 * D_MODEL)))
    loss = small_sum[small_sum.shape[0] - SUBLANES, 0]

    gsm = _unpack_small(small_sum)
    gsm["norm_in"] = g_norm_in
    gsm["conv_w"] = lax.dynamic_slice_in_dim(gsm["conv_w"], chip * cshard, cshard, axis=1)
    gsm["gn_gain"] = lax.dynamic_slice_in_dim(gsm["gn_gain"], chip * gshard, gshard, axis=1)
    smalls = [k for k in names if k not in delta]

    def view(a):
        return a.reshape(1, -1) if a.ndim == 1 else (a.reshape(a.shape[1:]) if a.ndim > 2 else a)

    ds, mns, vns = _adamw_small([view(weights[k]) for k in smalls], [gsm[k].reshape(view(weights[k]).shape) for k in smalls],
                                [view(ms[k]) for k in smalls], [view(vs[k]) for k in smalls], "adamw_small")
    for k, d, mn, vn in zip(smalls, ds, mns, vns):
        shp = weights[k].shape
        grads[k], delta[k], new_m[k], new_v[k] = gsm[k].reshape(shp), d.reshape(shp), mn.reshape(shp), vn.reshape(shp)

    return (loss, grad_x.reshape(B, S, D_MODEL), *[grads[k] for k in names], *[delta[k] for k in names],
            *[new_m[k] for k in names], *[new_v[k] for k in names])
```

```python
import jax
import jax.numpy as jnp
from jax import lax
from jax.experimental import pallas as pl
from jax.experimental.pallas import tpu as pltpu

F32 = jnp.float32
_MXU = jnp.bfloat16

D_MODEL = 1024
N_GROUPS = 8
HEADS = 4
DK = 256
CHUNK = 128
CONV = 4
LRU_BLOCKS = 16
LRU_BW = 64
LRU_C = 8.0
ROPE_THETA = 10000.0
EPS = 1e-6
CW = 256
N_CT = D_MODEL // CW
N_CHIPS = 4
MESH = pl.DeviceIdType.MESH

ADAM_LR = 0.001
ADAM_B1 = 0.9
ADAM_B2 = 0.999
ADAM_EPS = 1e-08
ADAM_WD = 0.01
ADAM_STEP = 10

VMEM_LIMIT = 56 * 1024 * 1024


def _c(v):
    return v.astype(_MXU)


def _dot(a, b):
    return lax.dot_general(a, b, (((1,), (0,)), ((), ())), preferred_element_type=F32)


def _dot_nt(a, b):
    return lax.dot_general(a, b, (((1,), (1,)), ((), ())), preferred_element_type=F32)


def _dot_tn(a, b):
    return lax.dot_general(a, b, (((0,), (0,)), ((), ())), preferred_element_type=F32)


def _sigmoid(z):
    return 0.5 * jnp.tanh(0.5 * z) + 0.5


ANY_SPEC = pl.BlockSpec(memory_space=pl.ANY)


def _after(body, n_in, deps):
    n_deps = len(deps)

    def wrapped(*refs):
        return body(*refs[:n_in], *refs[n_in + n_deps:])

    return wrapped


def _params(sem=None):
    if sem is None:
        return pltpu.CompilerParams(vmem_limit_bytes=VMEM_LIMIT)
    return pltpu.CompilerParams(vmem_limit_bytes=VMEM_LIMIT, dimension_semantics=sem)


def _inproj_fwd(x2d, g_in, w_all, deps=()):
    T = x2d.shape[0]
    tm = min(1024, T)
    n_i = T // tm

    def body(*refs):
        x_ref, g_ref, w_ref = refs[:3]
        proj_ref, ht_ref, h_all = refs[-3:]
        i = pl.program_id(1)
        rows = pl.ds(pl.multiple_of(i * tm, tm), tm)

        @pl.when(pl.program_id(0) == 0)
        def _():
            x = x_ref[...]
            r = lax.rsqrt(jnp.mean(x * x, axis=-1, keepdims=True) + EPS)
            h = x * r * g_ref[...]
            h_all[rows, :] = h.astype(h_all.dtype)
            ht_ref[...] = h.T.astype(ht_ref.dtype)

        proj_ref[...] = _dot(h_all[rows, :], w_ref[0])

    first = lambda j, i: jnp.where(j == 0, i, n_i - 1)
    return pl.pallas_call(
        body,
        name="inproj_fwd",
        grid=(N_GROUPS, n_i),
        in_specs=[
            pl.BlockSpec((tm, D_MODEL), lambda j, i: (first(j, i), 0)),
            pl.BlockSpec((1, D_MODEL), lambda j, i: (0, 0)),
            pl.BlockSpec((1, D_MODEL, D_MODEL), lambda j, i: (j // 2, 0, j % 2)),
        ] + [pl.BlockSpec(memory_space=pl.ANY)] * len(deps),
        out_specs=[
            pl.BlockSpec((tm, D_MODEL), lambda j, i: (i, j)),
            pl.BlockSpec((D_MODEL, tm), lambda j, i: (0, first(j, i))),
        ],
        out_shape=[
            jax.ShapeDtypeStruct((T, N_GROUPS * D_MODEL), F32),
            jax.ShapeDtypeStruct((D_MODEL, T), _MXU),
        ],
        scratch_shapes=[pltpu.VMEM((T, D_MODEL), _MXU)],
        compiler_params=_params(("arbitrary", "arbitrary")),
    )(x2d, g_in, w_all, *deps)


def _scan_fwd(a, u):
    n = a.shape[0]
    row = lax.broadcasted_iota(jnp.int32, a.shape, 0)
    s = 1
    while s < n:
        m = row >= s
        u = u + a * jnp.where(m, pltpu.roll(u, s, 0), 0.0)
        a = a * jnp.where(m, pltpu.roll(a, s, 0), 1.0)
        s *= 2
    return a, u


def _scan_bwd(b, g):
    n = b.shape[0]
    row = lax.broadcasted_iota(jnp.int32, b.shape, 0)
    s = 1
    while s < n:
        m = row < n - s
        g = g + b * jnp.where(m, pltpu.roll(g, n - s, 0), 0.0)
        b = b * jnp.where(m, pltpu.roll(b, n - s, 0), 1.0)
        s *= 2
    return b, g


LANES = 128
SUBLANES = 8


def _scan_scratch(tc):
    by_lanes = pltpu.VMEM((CW // LANES, tc, LANES), F32)
    return [by_lanes, by_lanes, pltpu.VMEM((tc // SUBLANES, CW), F32), pltpu.VMEM((tc, CW), F32)]


def _scan_tile(a, u, edge, la_ref, lh_ref, c_ref, dst_ref, reverse):
    n, w = a.shape
    groups = n // SUBLANES
    a3 = a.reshape(groups, SUBLANES, w)
    u3 = u.reshape(groups, SUBLANES, w)
    row = lax.broadcasted_iota(jnp.int32, a3.shape, 1)
    for s in (1, 2, 4):
        m = (row < SUBLANES - s) if reverse else (row >= s)
        shift = SUBLANES - s if reverse else s
        u3 = u3 + a3 * jnp.where(m, pltpu.roll(u3, shift, 1), 0.0)
        a3 = a3 * jnp.where(m, pltpu.roll(a3, shift, 1), 1.0)
    al = a3.reshape(n, w)
    hl = u3.reshape(n, w)
    blocks = w // LANES
    for q in range(blocks):
        la_ref[q] = al[:, q * LANES:(q + 1) * LANES]
        lh_ref[q] = hl[:, q * LANES:(q + 1) * LANES]
    ends = pl.ds(0 if reverse else SUBLANES - 1, groups, stride=SUBLANES)
    end_a = jnp.concatenate([la_ref.at[q][ends, :] for q in range(blocks)], axis=-1)
    end_h = jnp.concatenate([lh_ref.at[q][ends, :] for q in range(blocks)], axis=-1)
    prod, part = (_scan_bwd if reverse else _scan_fwd)(end_a, end_h)
    total = part + prod * edge
    g_row = lax.broadcasted_iota(jnp.int32, total.shape, 0)
    if reverse:
        c_ref[...] = jnp.where(g_row == groups - 1, edge, pltpu.roll(total, groups - 1, 0))
    else:
        c_ref[...] = jnp.where(g_row == 0, edge, pltpu.roll(total, 1, 0))
    for g in range(groups):
        rows = slice(g * SUBLANES, (g + 1) * SUBLANES)
        for q in range(blocks):
            cols = slice(q * LANES, (q + 1) * LANES)
            dst_ref[rows, cols] = lh_ref[q, rows, :] + la_ref[q, rows, :] * c_ref[g:g + 1, cols]


def _softplus_neg(lam):
    z = -lam
    return jnp.maximum(z, 0.0) + jnp.log1p(jnp.exp(-jnp.abs(z)))


def _lru_gates(xc, wx_ref, wa_ref, bx_ref, ba_ref, lam_ref):
    xcb = _c(xc)
    i_t = _sigmoid(_dot(xcb, wx_ref[0]) + bx_ref[...])
    r_t = _sigmoid(_dot(xcb, wa_ref[0]) + ba_ref[...])
    sp = _softplus_neg(lam_ref[...])
    log_a = (-LRU_C) * r_t * sp
    a = jnp.exp(log_a)
    mult = jnp.sqrt(1.0 - a * a)
    return xcb, i_t, r_t, sp, a, mult


def _conv_from_ext(ext_ref, xa, cw_ref, cb_ref, tc):
    return (cb_ref[...] + cw_ref[3:4, :] * xa + cw_ref[2:3, :] * ext_ref[7:7 + tc, :]
            + cw_ref[1:2, :] * ext_ref[6:6 + tc, :] + cw_ref[0:1, :] * ext_ref[5:5 + tc, :])


def _lru_fwd(proj, conv_w, conv_b, wx_bd, wa_bd, bx, ba, lam, B, S):
    T = B * S
    tc = min(256, S)
    nt = S // tc
    h8 = tc // 8

    def body(xa_ref, halo_ref, ga_ref, cw_ref, cb_ref, wx_ref, wa_ref, bx_ref, ba_ref, lam_ref,
             h_ref, ya_ref, ext_ref, carry_ref, la_ref, lh_ref, c_ref):
        t = pl.program_id(2)

        @pl.when(t == 0)
        def _():
            carry_ref[...] = jnp.zeros_like(carry_ref)

        xa = xa_ref[...]
        ext_ref[0:8, :] = jnp.where(t == 0, 0.0, halo_ref[...])
        ext_ref[8:8 + tc, :] = xa
        xc = _conv_from_ext(ext_ref, xa, cw_ref, cb_ref, tc)
        _, i_t, _, _, a, mult = _lru_gates(xc, wx_ref, wa_ref, bx_ref, ba_ref, lam_ref)
        u = mult * (i_t * xc)
        _scan_tile(a, u, carry_ref[7:8, :], la_ref, lh_ref, c_ref, h_ref, False)
        h = h_ref[...]
        carry_ref[...] = h[tc - 8:tc, :]
        ga = ga_ref[...]
        ya_ref[...] = (ga * _sigmoid(ga) * h).astype(ya_ref.dtype)

    row = lambda b, t: b * nt + t
    vec = pl.BlockSpec((1, CW), lambda b, c, t: (0, c))
    mat = pl.BlockSpec((1, CW, CW), lambda b, c, t: (c, 0, 0))
    return pl.pallas_call(
        body,
        name="lru_fwd",
        grid=(B, N_CT, nt),
        in_specs=[
            pl.BlockSpec((tc, CW), lambda b, c, t: (row(b, t), c)),
            pl.BlockSpec((8, CW), lambda b, c, t: (jnp.maximum(row(b, t) * h8 - 1, 0), c)),
            pl.BlockSpec((tc, CW), lambda b, c, t: (row(b, t), N_CT + c)),
            pl.BlockSpec((CONV, CW), lambda b, c, t: (0, c)),
            vec, mat, mat, vec, vec, vec,
        ],
        out_specs=[
            pl.BlockSpec((tc, CW), lambda b, c, t: (row(b, t), c)),
            pl.BlockSpec((tc, CW), lambda b, c, t: (row(b, t), c)),
        ],
        out_shape=[
            jax.ShapeDtypeStruct((T, D_MODEL), F32),
            jax.ShapeDtypeStruct((T, D_MODEL), _MXU),
        ],
        scratch_shapes=[pltpu.VMEM((tc + 8, CW), F32), pltpu.VMEM((8, CW), F32)] + _scan_scratch(tc)[:3],
        compiler_params=_params(("parallel", "parallel", "arbitrary")),
    )(proj, proj, proj, conv_w, conv_b, wx_bd, wa_bd, bx, ba, lam)


def _lru_bwd(dya, proj, hlru, conv_w, conv_b, wx_bd, wa_bd, bx, ba, lam, B, S, deps=()):
    T = B * S
    tc = min(256, S)
    nt = S // tc
    h8 = tc // 8

    def body(dya_ref, xa_ref, xhalo_ref, ga_ref, h_ref, hhalo_ref, cw_ref, cb_ref, wx_ref, wa_ref, bx_ref, ba_ref,
             lam_ref, dxa_ref, dga_ref, dcw_ref, dcb_ref, dwx_ref, dwa_ref, dbx_ref, dba_ref, dlam_ref,
             ext_ref, ext2_ref, carry_ref, dhalo_ref, la_ref, lh_ref, c_ref, dh_ref):
        b = pl.program_id(1)
        t = pl.program_id(2)
        tt = nt - 1 - t

        @pl.when(t == 0)
        def _():
            carry_ref[...] = jnp.zeros_like(carry_ref)
            dhalo_ref[...] = jnp.zeros_like(dhalo_ref)

        @pl.when((t == 0) & (b == 0))
        def _():
            for r in (dcw_ref, dcb_ref, dwx_ref, dwa_ref, dbx_ref, dba_ref, dlam_ref):
                r[...] = jnp.zeros_like(r)

        xa = xa_ref[...]
        ext_ref[0:8, :] = jnp.where(tt == 0, 0.0, xhalo_ref[...])
        ext_ref[8:8 + tc, :] = xa
        xc = _conv_from_ext(ext_ref, xa, cw_ref, cb_ref, tc)
        xcb, i_t, r_t, sp, a, mult = _lru_gates(xc, wx_ref, wa_ref, bx_ref, ba_ref, lam_ref)

        h = h_ref[...]
        ga = ga_ref[...]
        dya_t = dya_ref[...]
        sg = _sigmoid(ga)
        dga_ref[...] = (dya_t * h * (sg * (1.0 + ga * (1.0 - sg)))).astype(dga_ref.dtype)
        dlru = dya_t * (ga * sg)

        row = lax.broadcasted_iota(jnp.int32, a.shape, 0)
        coef = jnp.where(row == tc - 1, 1.0, pltpu.roll(a, tc - 1, 0))
        _scan_tile(coef, dlru, carry_ref[0:1, :], la_ref, lh_ref, c_ref, dh_ref, True)
        dh = dh_ref[...]
        ext2_ref[0:tc, :] = a * dh
        carry_ref[...] = ext2_ref[0:8, :]

        ext2_ref[0:8, :] = jnp.where(tt == 0, 0.0, hhalo_ref[...])
        ext2_ref[8:8 + tc, :] = h
        hprev = ext2_ref[7:7 + tc, :]

        da = dh * hprev
        ix = i_t * xc
        dmult = dh * ix
        di = dh * mult * xc
        dxc = dh * mult * i_t
        dlog_a = da * a - dmult * (a * a) / mult
        dr = dlog_a * ((-LRU_C) * sp)
        dlam_ref[...] += jnp.sum(dlog_a * r_t, axis=0, keepdims=True) * (LRU_C * _sigmoid(-lam_ref[...]))
        dza = dr * r_t * (1.0 - r_t)
        dzx = di * i_t * (1.0 - i_t)
        dzab = _c(dza)
        dzxb = _c(dzx)
        dxc = dxc + _dot_nt(dzxb, wx_ref[0]) + _dot_nt(dzab, wa_ref[0])
        dwx_ref[0] += _dot_tn(xcb, dzxb)
        dwa_ref[0] += _dot_tn(xcb, dzab)
        dbx_ref[...] += jnp.sum(dzx, axis=0, keepdims=True)
        dba_ref[...] += jnp.sum(dza, axis=0, keepdims=True)

        dcb_ref[...] += jnp.sum(dxc, axis=0, keepdims=True)
        dcw_ref[3:4, :] += jnp.sum(dxc * xa, axis=0, keepdims=True)
        dcw_ref[2:3, :] += jnp.sum(dxc * ext_ref[7:7 + tc, :], axis=0, keepdims=True)
        dcw_ref[1:2, :] += jnp.sum(dxc * ext_ref[6:6 + tc, :], axis=0, keepdims=True)
        dcw_ref[0:1, :] += jnp.sum(dxc * ext_ref[5:5 + tc, :], axis=0, keepdims=True)
        ext2_ref[0:tc, :] = dxc
        ext2_ref[tc:tc + 8, :] = dhalo_ref[...]
        dxa = (cw_ref[3:4, :] * dxc + cw_ref[2:3, :] * ext2_ref[1:1 + tc, :]
               + cw_ref[1:2, :] * ext2_ref[2:2 + tc, :] + cw_ref[0:1, :] * ext2_ref[3:3 + tc, :])
        dxa_ref[...] = dxa.astype(dxa_ref.dtype)
        dhalo_ref[...] = ext2_ref[0:8, :]

    row_of = lambda b, t: b * nt + (nt - 1 - t)
    tile = lambda off: pl.BlockSpec((tc, CW), lambda c, b, t: (row_of(b, t), off + c))
    halo = pl.BlockSpec((8, CW), lambda c, b, t: (jnp.maximum(row_of(b, t) * h8 - 1, 0), c))
    vec = pl.BlockSpec((1, CW), lambda c, b, t: (0, c))
    mat = pl.BlockSpec((1, CW, CW), lambda c, b, t: (c, 0, 0))
    cwspec = pl.BlockSpec((CONV, CW), lambda c, b, t: (0, c))
    return pl.pallas_call(
        _after(body, 13, deps),
        name="lru_bwd",
        grid=(N_CT, B, nt),
        in_specs=[tile(0), tile(0), halo, tile(N_CT), tile(0), halo, cwspec, vec, mat, mat, vec, vec, vec]
        + [ANY_SPEC] * len(deps),
        out_specs=[tile(0), tile(0), cwspec, vec, mat, mat, vec, vec, vec],
        out_shape=[
            jax.ShapeDtypeStruct((T, D_MODEL), _MXU),
            jax.ShapeDtypeStruct((T, D_MODEL), _MXU),
            jax.ShapeDtypeStruct((CONV, D_MODEL), F32),
            jax.ShapeDtypeStruct((1, D_MODEL), F32),
            jax.ShapeDtypeStruct((N_CT, CW, CW), F32),
            jax.ShapeDtypeStruct((N_CT, CW, CW), F32),
            jax.ShapeDtypeStruct((1, D_MODEL), F32),
            jax.ShapeDtypeStruct((1, D_MODEL), F32),
            jax.ShapeDtypeStruct((1, D_MODEL), F32),
        ],
        scratch_shapes=[pltpu.VMEM((tc + 8, CW), F32), pltpu.VMEM((tc + 8, CW), F32),
                        pltpu.VMEM((8, CW), F32), pltpu.VMEM((8, CW), F32)] + _scan_scratch(tc),
        compiler_params=_params(("parallel", "arbitrary", "arbitrary")),
    )(dya, proj, proj, proj, hlru, hlru, conv_w, conv_b, wx_bd, wa_bd, bx, ba, lam, *deps)


def _retention_tables(S):
    half = DK // 2
    freqs = ROPE_THETA ** (-jnp.arange(half, dtype=F32) / half)
    ang = jnp.arange(S, dtype=F32)[:, None] * freqs[None, :]
    log_g = jnp.log1p(-(2.0 ** (-5.0 - jnp.arange(HEADS, dtype=F32))))
    idx = jnp.arange(CHUNK, dtype=F32)
    diff = idx[:, None] - idx[None, :]
    inner = jnp.where(diff >= 0, jnp.exp(jnp.maximum(diff, 0.0)[None] * log_g[:, None, None]), 0.0)
    cross = jnp.exp((idx[None, :] + 1.0) * log_g[:, None])[:, :, None]
    state = jnp.exp((CHUNK - 1.0 - idx[None, :]) * log_g[:, None])[:, :, None]
    gam = jnp.broadcast_to(jnp.exp(CHUNK * log_g)[:, None, None], (HEADS, 1, DK))
    return jnp.cos(ang), jnp.sin(ang), inner, cross, state, gam


def _rot(x, cos, sin):
    half = DK // 2
    x1, x2 = x[:, :half], x[:, half:]
    return jnp.concatenate([x1 * cos - x2 * sin, x1 * sin + x2 * cos], axis=-1)


def _rot_t(y, cos, sin):
    half = DK // 2
    y1, y2 = y[:, :half], y[:, half:]
    return jnp.concatenate([y1 * cos + y2 * sin, y2 * cos - y1 * sin], axis=-1)


def _groupnorm(o):
    mu = jnp.mean(o, axis=-1, keepdims=True)
    oc = o - mu
    rs = lax.rsqrt(jnp.mean(oc * oc, axis=-1, keepdims=True) + EPS)
    return oc * rs, rs


def _ret_specs(B, chunk_of):
    qkv = lambda g: pl.BlockSpec((B, CHUNK, D_MODEL), lambda c: (0, chunk_of(c), g))
    act = pl.BlockSpec((B, CHUNK, D_MODEL), lambda c: (0, chunk_of(c), 0))
    rope = pl.BlockSpec((CHUNK, DK // 2), lambda c: (chunk_of(c), 0))
    dmat = pl.BlockSpec((HEADS, CHUNK, CHUNK), lambda c: (0, 0, 0))
    dvec = pl.BlockSpec((HEADS, CHUNK, 1), lambda c: (0, 0, 0))
    hrow = pl.BlockSpec((HEADS, 1, DK), lambda c: (0, 0, 0))
    rst = pl.BlockSpec((1, B, HEADS, DK, DK), lambda c: (chunk_of(c), 0, 0, 0, 0))
    return qkv, act, rope, dmat, dvec, hrow, rst


def _ret_fwd(proj, tables, gain3, B, S):
    T = B * S
    nc = S // CHUNK
    cos, sin, dmat_t, cd_t, sd_t, gam_t = tables

    def body(q_ref, k_ref, v_ref, gb_ref, cos_ref, sin_ref, dm_ref, cd_ref, sd_ref, gam_ref, gain_ref,
             o_ref, yb_ref, rs_ref, state_ref):
        @pl.when(pl.program_id(0) == 0)
        def _():
            state_ref[...] = jnp.zeros_like(state_ref)

        cos_t, sin_t = cos_ref[...], sin_ref[...]
        for b, h in [(b, h) for b in range(B) for h in range(HEADS)]:
            cols = slice(h * DK, (h + 1) * DK)
            qb = _c(_rot(q_ref[b, :, cols], cos_t, sin_t))
            kb = _c(_rot(k_ref[b, :, cols], cos_t, sin_t) * (DK ** -0.5))
            v = v_ref[b, :, cols]
            state = state_ref[b, h]
            sb = _c(state)
            rs_ref[0, b, h] = sb
            scores = _dot_nt(qb, kb) * dm_ref[h]
            o = _dot(_c(scores), _c(v)) + _dot(qb, sb) * cd_ref[h]
            state_ref[b, h] = gam_ref[h] * state + _dot_tn(kb, _c(v * sd_ref[h]))
            o_ref[b, :, cols] = o
            n, _ = _groupnorm(o)
            gb = gb_ref[b, :, cols]
            yb_ref[b, :, cols] = (gb * _sigmoid(gb) * (n * gain_ref[h])).astype(yb_ref.dtype)

    qkv, act, rope, dmat, dvec, hrow, rst = _ret_specs(B, lambda c: c)
    proj3 = proj.reshape(B, S, proj.shape[1])
    o_pre, yb, states = pl.pallas_call(
        body,
        name="ret_fwd",
        grid=(nc,),
        in_specs=[qkv(2), qkv(3), qkv(4), qkv(5), rope, rope, dmat, dvec, dvec, hrow, hrow],
        out_specs=[act, act, rst],
        out_shape=[
            jax.ShapeDtypeStruct((B, S, D_MODEL), F32),
            jax.ShapeDtypeStruct((B, S, D_MODEL), _MXU),
            jax.ShapeDtypeStruct((nc, B, HEADS, DK, DK), _MXU),
        ],
        scratch_shapes=[pltpu.VMEM((B, HEADS, DK, DK), F32)],
        compiler_params=_params(("arbitrary",)),
    )(proj3, proj3, proj3, proj3, cos, sin, dmat_t, cd_t, sd_t, gam_t, gain3)
    return o_pre.reshape(T, D_MODEL), yb.reshape(T, D_MODEL), states


def _ret_bwd(dyb, o_pre, proj, states, tables, gain3, B, S, deps=()):
    T = B * S
    nc = S // CHUNK
    cos, sin, dmat_t, cd_t, sd_t, gam_t = tables

    def body(dyb_ref, o_ref, q_ref, k_ref, v_ref, gb_ref, rs_ref, cos_ref, sin_ref, dm_ref, cd_ref, sd_ref, gam_ref,
             gain_ref, dr_ref, dgain_ref, dstate_ref):
        @pl.when(pl.program_id(0) == 0)
        def _():
            dstate_ref[...] = jnp.zeros_like(dstate_ref)
            dgain_ref[...] = jnp.zeros_like(dgain_ref)

        cos_t, sin_t = cos_ref[...], sin_ref[...]
        for b, h in [(b, h) for b in range(B) for h in range(HEADS)]:
            cols = slice(h * DK, (h + 1) * DK)
            gain = gain_ref[h]
            n, rs = _groupnorm(o_ref[b, :, cols])
            gb = gb_ref[b, :, cols]
            sg = _sigmoid(gb)
            dy = dyb_ref[b, :, cols]
            part = lambda g: slice(g * D_MODEL + h * DK, g * D_MODEL + (h + 1) * DK)
            dr_ref[b, :, part(3)] = (dy * (n * gain) * (sg * (1.0 + gb * (1.0 - sg)))).astype(dr_ref.dtype)
            dgn = dy * (gb * sg)
            dgain_ref[h] += jnp.sum(dgn * n, axis=0, keepdims=True)
            dn = dgn * gain
            do = rs * (dn - jnp.mean(dn, axis=-1, keepdims=True) - n * jnp.mean(dn * n, axis=-1, keepdims=True))

            qb = _c(_rot(q_ref[b, :, cols], cos_t, sin_t))
            kb = _c(_rot(k_ref[b, :, cols], cos_t, sin_t) * (DK ** -0.5))
            v = v_ref[b, :, cols]
            vb = _c(v)
            vsb = _c(v * sd_ref[h])
            dob = _c(do)
            docb = _c(do * cd_ref[h])
            dmat = dm_ref[h]
            dstate = dstate_ref[b, h]
            dsb = _c(dstate)
            pb = _c(_dot_nt(qb, kb) * dmat)
            dsc = _c(_dot_nt(dob, vb) * dmat)
            dq = _dot(dsc, kb) + _dot_nt(docb, rs_ref[0, b, h])
            dk = _dot_tn(dsc, qb) + _dot_nt(vsb, dsb)
            dv = _dot_tn(pb, dob) + _dot(kb, dsb) * sd_ref[h]
            dstate_ref[b, h] = gam_ref[h] * dstate + _dot_tn(qb, docb)
            dr_ref[b, :, part(0)] = _rot_t(dq, cos_t, sin_t).astype(dr_ref.dtype)
            dr_ref[b, :, part(1)] = (_rot_t(dk, cos_t, sin_t) * (DK ** -0.5)).astype(dr_ref.dtype)
            dr_ref[b, :, part(2)] = dv.astype(dr_ref.dtype)

    qkv, act, rope, dmat, dvec, hrow, rst = _ret_specs(B, lambda c: nc - 1 - c)
    wide = pl.BlockSpec((B, CHUNK, 4 * D_MODEL), lambda c: (0, nc - 1 - c, 0))
    proj3 = proj.reshape(B, S, proj.shape[1])
    dr, dgain = pl.pallas_call(
        _after(body, 14, deps),
        name="ret_bwd",
        grid=(nc,),
        in_specs=[act, act, qkv(2), qkv(3), qkv(4), qkv(5), rst, rope, rope, dmat, dvec, dvec, hrow, hrow]
        + [ANY_SPEC] * len(deps),
        out_specs=[wide, hrow],
        out_shape=[jax.ShapeDtypeStruct((B, S, 4 * D_MODEL), _MXU), jax.ShapeDtypeStruct((HEADS, 1, DK), F32)],
        scratch_shapes=[pltpu.VMEM((B, HEADS, DK, DK), F32)],
        compiler_params=_params(("arbitrary",)),
    )(dyb.reshape(B, S, D_MODEL), o_pre.reshape(B, S, D_MODEL), proj3, proj3, proj3, proj3, states, cos, sin, dmat_t,
      cd_t, sd_t, gam_t, gain3, *deps)
    return dr.reshape(T, 4 * D_MODEL), dgain


def _mid(ya, yb, proj, x2d, tgt2d, wpa, wpb, wout, g_fin):
    T = x2d.shape[0]
    tm = min(256, T)
    n_steps = T // tm
    rows = D_MODEL // (2 * N_CHIPS)

    def body(ya_ref, yb_ref, ma_ref, mb_ref, x_ref, t_ref, gf_ref, wpa_hbm, wpb_hbm, wout_hbm,
             loss_ref, dx2_ref, dya_ref, dyb_ref, dm_ref, dgf_ref, gw_hbm, w_ref, acc_ref, sem):
        i = pl.program_id(0)

        @pl.when(i == 0)
        def _():
            loads = [pltpu.make_async_copy(src, w_ref.at[k], sem.at[k]) for k, src in enumerate((wpa_hbm, wpb_hbm, wout_hbm))]
            for cp in loads:
                cp.start()
            for cp in loads:
                cp.wait()
            acc_ref[...] = jnp.zeros_like(acc_ref)
            loss_ref[...] = jnp.zeros_like(loss_ref)
            dgf_ref[...] = jnp.zeros_like(dgf_ref)

        ya_t, yb_t = ya_ref[...], yb_ref[...]
        out_a = _dot(ya_t, w_ref[0])
        out_b = _dot(yb_t, w_ref[1])
        sa = _sigmoid(ma_ref[...])
        sb = _sigmoid(mb_ref[...])
        mgb = _c(sa * out_a + sb * out_b)
        x2 = x_ref[...] + _dot(mgb, w_ref[2])
        r2 = lax.rsqrt(jnp.mean(x2 * x2, axis=-1, keepdims=True) + EPS)
        nx = x2 * r2
        gf = gf_ref[...]
        err = nx * gf - t_ref[...]
        loss_ref[...] += 0.5 * jnp.sum(jnp.mean(err * err, axis=-1, keepdims=True), axis=0, keepdims=True)
        dy = err * (1.0 / D_MODEL)
        dgf_ref[...] += jnp.sum(dy * nx, axis=0, keepdims=True)
        dyg = dy * gf
        dx2 = r2 * (dyg - nx * jnp.mean(dyg * nx, axis=-1, keepdims=True))
        dx2_ref[...] = dx2
        dx2b = _c(dx2)
        dmg = _dot_nt(dx2b, w_ref[2])
        acc_ref[2] += _dot_tn(mgb, dx2b)
        dm_ref[:, :D_MODEL] = (dmg * out_a * sa * (1.0 - sa)).astype(dm_ref.dtype)
        dm_ref[:, D_MODEL:] = (dmg * out_b * sb * (1.0 - sb)).astype(dm_ref.dtype)
        dab = _c(dmg * sa)
        dbb = _c(dmg * sb)
        dya_ref[...] = _dot_nt(dab, w_ref[0])
        dyb_ref[...] = _dot_nt(dbb, w_ref[1])
        acc_ref[0] += _dot_tn(ya_t, dab)
        acc_ref[1] += _dot_tn(yb_t, dbb)

        @pl.when(i == n_steps - 1)
        def _():
            copies = [pltpu.make_async_copy(acc_ref.at[k, pl.ds((2 * p + hf) * rows, rows), :], gw_hbm.at[p, hf, k],
                                            sem.at[(k * N_CHIPS + p) * 2 + hf])
                      for k in range(3) for p in range(N_CHIPS) for hf in range(2)]
            for cp in copies:
                cp.start()
            for cp in copies:
                cp.wait()

    tile = lambda j: pl.BlockSpec((tm, D_MODEL), lambda i: (i, j))
    one = pl.BlockSpec((1, D_MODEL), lambda i: (0, 0))
    anyspec = pl.BlockSpec(memory_space=pl.ANY)
    return pl.pallas_call(
        body,
        name="mid",
        grid=(n_steps,),
        in_specs=[tile(0), tile(0), tile(6), tile(7), tile(0), tile(0), one, anyspec, anyspec, anyspec],
        out_specs=[pl.BlockSpec((1, 1), lambda i: (0, 0)), tile(0), tile(0), tile(0),
                   pl.BlockSpec((tm, 2 * D_MODEL), lambda i: (i, 0)), one, anyspec],
        out_shape=[
            jax.ShapeDtypeStruct((1, 1), F32),
            jax.ShapeDtypeStruct((T, D_MODEL), F32),
            jax.ShapeDtypeStruct((T, D_MODEL), F32),
            jax.ShapeDtypeStruct((T, D_MODEL), F32),
            jax.ShapeDtypeStruct((T, 2 * D_MODEL), _MXU),
            jax.ShapeDtypeStruct((1, D_MODEL), F32),
            jax.ShapeDtypeStruct((N_CHIPS, 2, 3, rows, D_MODEL), F32),
        ],
        scratch_shapes=[pltpu.VMEM((3, D_MODEL, D_MODEL), _MXU), pltpu.VMEM((3, D_MODEL, D_MODEL), F32),
                        pltpu.SemaphoreType.DMA((3 * N_CHIPS * 2,))],
        compiler_params=_params(("arbitrary",)),
    )(ya, yb, proj, proj, x2d, tgt2d, g_fin, wpa, wpb, wout)


DX_TILE = 512


def _inproj_bwd_dx(dparts, w_all, x2d, dx2, g_in, first, count, prev, name, deps=()):
    T = x2d.shape[0]
    tm = min(DX_TILE, T)
    n_d = len(dparts)
    groups = [(a, k) for a, d in enumerate(dparts) for k in range(d.shape[1] // D_MODEL)]
    dg_start = jnp.zeros((1, D_MODEL), F32) if prev is None else prev[1]
    carried = () if prev is None else (prev[0],)

    def body(*refs):
        d_refs = refs[:n_d]
        x_ref, dx2_ref, g_ref, dg0_ref, w_hbm = refs[n_d:n_d + 5]
        dx_ref, dg_ref, w_ref, sem = refs[-4:]

        @pl.when(pl.program_id(0) == 0)
        def _():
            cp = pltpu.make_async_copy(w_hbm, w_ref, sem)
            cp.start()
            cp.wait()
            dg_ref[...] = dg0_ref[...]

        dh = jnp.zeros((tm, D_MODEL), F32)
        for j, (a, k) in enumerate(groups):
            dh = dh + _dot_nt(d_refs[a][:, k * D_MODEL:(k + 1) * D_MODEL],
                              w_ref[j // 2, :, (j % 2) * D_MODEL:(j % 2 + 1) * D_MODEL])
        x = x_ref[...]
        r = lax.rsqrt(jnp.mean(x * x, axis=-1, keepdims=True) + EPS)
        nx = x * r
        dg_ref[...] += jnp.sum(dh * nx, axis=0, keepdims=True)
        dhg = dh * g_ref[...]
        dx_ref[...] = dx2_ref[...] + r * (dhg - nx * jnp.mean(dhg * nx, axis=-1, keepdims=True))

    tile = pl.BlockSpec((tm, D_MODEL), lambda i: (first + i, 0))
    one = pl.BlockSpec((1, D_MODEL), lambda i: (0, 0))
    return pl.pallas_call(
        body,
        name=name,
        grid=(count,),
        in_specs=[pl.BlockSpec((tm, d.shape[1]), lambda i: (first + i, 0)) for d in dparts]
        + [tile, tile, one, one, ANY_SPEC] + [ANY_SPEC] * (len(carried) + len(deps)),
        out_specs=[tile, one],
        out_shape=[jax.ShapeDtypeStruct((T, D_MODEL), F32), jax.ShapeDtypeStruct((1, D_MODEL), F32)],
        input_output_aliases={n_d + 5: 0} if carried else {},
        scratch_shapes=[pltpu.VMEM(w_all.shape, w_all.dtype), pltpu.SemaphoreType.DMA],
        compiler_params=_params(("arbitrary",)),
    )(*dparts, x2d, dx2, g_in, dg_start, w_all, *carried, *deps)


def _inproj_bwd_dw(ht, dparts, name, deps=()):
    T = ht.shape[1]
    tn = 512
    half = D_MODEL // 2
    per_chip = 2 * D_MODEL // tn
    n_d = len(dparts)
    tiles = [(a, t) for a, d in enumerate(dparts) for t in range(d.shape[1] // tn)]
    offs = [sum(d.shape[1] // tn for d in dparts[:a]) for a in range(n_d)]

    def body(*refs):
        ht_ref = refs[0]
        d_refs = refs[1:1 + n_d]
        out_ref = refs[-1]
        t = pl.program_id(0)

        for a in range(n_d):
            lo, hi = offs[a], offs[a] + dparts[a].shape[1] // tn

            @pl.when((t >= lo) & (t < hi))
            def _(a=a):
                g = _dot(ht_ref[...], d_refs[a][...])
                out_ref[0, 0] = g[:half]
                out_ref[0, 1] = g[half:]

    def dspec(a):
        n_a = dparts[a].shape[1] // tn
        return pl.BlockSpec((T, tn), lambda t: (0, jnp.clip(t - offs[a], 0, n_a - 1)))

    return pl.pallas_call(
        body,
        name=name,
        grid=(len(tiles),),
        in_specs=[pl.BlockSpec((D_MODEL, T), lambda t: (0, 0))] + [dspec(a) for a in range(n_d)]
        + [ANY_SPEC] * len(deps),
        out_specs=pl.BlockSpec((1, 2, half, tn), lambda t: (t // per_chip, 0, 0, t % per_chip)),
        out_shape=jax.ShapeDtypeStruct((len(tiles) // per_chip, 2, half, 2 * D_MODEL), F32),
        compiler_params=_params(("parallel",)),
    )(ht, *dparts, *deps)


def _coords():
    return lax.axis_index("x"), lax.axis_index("y"), lax.axis_index("c")


def _other_chips(x, y):
    return [(1 - x, y), (x, 1 - y), (1 - x, 1 - y)]


def _chunks(rows, n):
    size = rows // n
    return [pl.ds(q * size, size) for q in range(n)]


HBM_SPEC = pl.BlockSpec(memory_space=pltpu.HBM)
SEM_SPEC = pl.BlockSpec(memory_space=pltpu.SEMAPHORE)
DATAFLOW = pltpu.SideEffectType.DATAFLOW_SIDE_EFFECTING


def _copies_start(bufs, plan, n_copies, name):
    n = len(bufs)

    def body(*refs):
        ins = refs[:n]
        send_sems, recv_sems = refs[n], refs[n + 1]
        token = refs[-1]
        for k, send, _ in plan(ins):
            if send is not None:
                src, dst, dev, pred = send
                cp = pltpu.make_async_remote_copy(src_ref=src, dst_ref=dst, send_sem=send_sems.at[k],
                                                  recv_sem=recv_sems.at[k], device_id=dev, device_id_type=MESH)
                if pred is None:
                    cp.start()
                else:
                    pl.when(pred)(cp.start)
        token[...] = jnp.zeros_like(token)

    hbm = [pltpu.with_memory_space_constraint(b, pltpu.HBM) for b in bufs]
    outs = pl.pallas_call(
        body,
        name=name,
        in_specs=[HBM_SPEC] * n,
        out_specs=(SEM_SPEC, SEM_SPEC, *([HBM_SPEC] * n), pl.BlockSpec(memory_space=pltpu.VMEM)),
        out_shape=(pltpu.SemaphoreType.DMA((n_copies,)), pltpu.SemaphoreType.DMA((n_copies,)),
                   *[pltpu.HBM(b.shape, b.dtype) for b in bufs], jax.ShapeDtypeStruct((8, 128), F32)),
        input_output_aliases={a: 2 + a for a in range(n)},
        compiler_params=pltpu.CompilerParams(has_side_effects=DATAFLOW),
    )(*hbm)
    return outs[0], outs[1], list(outs[2:2 + n]), outs[-1]


def _copies_wait(send_sems, recv_sems, bufs, after, plan, name):
    n = len(bufs)

    def body(*refs):
        ins = refs[:n]
        s_sems, r_sems = refs[n], refs[n + 1]
        for k, send, recv in plan(ins):
            if send is not None:
                src, dst, dev, pred = send
                cp = pltpu.make_async_remote_copy(src_ref=src, dst_ref=dst, send_sem=s_sems.at[k],
                                                  recv_sem=r_sems.at[k], device_id=dev, device_id_type=MESH)
                if pred is None:
                    cp.wait_send()
                else:
                    pl.when(pred)(cp.wait_send)
            if recv is not None:
                dst, pred = recv
                cp = pltpu.make_async_remote_copy(src_ref=dst, dst_ref=dst, send_sem=s_sems.at[k],
                                                  recv_sem=r_sems.at[k], device_id=_coords(), device_id_type=MESH)
                if pred is None:
                    cp.wait_recv()
                else:
                    pl.when(pred)(cp.wait_recv)

    outs = pl.pallas_call(
        body,
        name=name,
        in_specs=[HBM_SPEC] * n + [SEM_SPEC, SEM_SPEC, pl.BlockSpec(memory_space=pl.ANY)],
        out_specs=[HBM_SPEC] * n,
        out_shape=[pltpu.HBM(b.shape, b.dtype) for b in bufs],
        input_output_aliases={a: a for a in range(n)},
        compiler_params=pltpu.CompilerParams(has_side_effects=DATAFLOW),
    )(*bufs, send_sems, recv_sems, after)
    return list(outs)


def _gather_plan(n_bufs):
    def plan(refs):
        x, y, c = _coords()
        me = 2 * x + y
        out = []
        for k, (px, py) in enumerate(_other_chips(x, y)):
            for a in range(n_bufs):
                out.append((k * n_bufs + a, (refs[a].at[me], refs[a].at[me], (px, py, c), None),
                            (refs[a].at[2 * px + py], None)))
        return out
    return plan


def _cast_into_slot(ws, name):
    n = len(ws)
    nt = 2

    def body(s_ref, *refs):
        for a in range(n):
            refs[n + a][0] = refs[a][...].astype(refs[n + a].dtype)

    xi, yi, _ = _coords()
    return pl.pallas_call(
        body,
        name=name,
        grid_spec=pltpu.PrefetchScalarGridSpec(
            num_scalar_prefetch=1,
            grid=(2, nt),
            in_specs=[pl.BlockSpec((1, w.shape[1] // nt, w.shape[2]), lambda hf, i, s: (hf, i, 0)) for w in ws],
            out_specs=[pl.BlockSpec((1, 1, w.shape[1] // nt, w.shape[2]), lambda hf, i, s: (s[0], hf, i, 0)) for w in ws],
        ),
        out_shape=[jax.ShapeDtypeStruct((N_CHIPS,) + w.shape, _MXU) for w in ws],
        compiler_params=_params(("parallel", "parallel")),
    )((2 * xi + yi).reshape(1).astype(jnp.int32), *ws)


def _gather_chips(bufs, n_chunks, name):
    n = len(bufs)
    pieces = [(a, rows) for a in range(n) for rows in _chunks(bufs[a].shape[2], n_chunks[a])]
    n_p = len(pieces)

    def body(*refs):
        outs = refs[n:2 * n]
        send_sems, recv_sems, fsend_sems, frecv_sems = refs[2 * n:]
        x, y, c = _coords()
        me = 2 * x + y
        near = [(1 - x, y), (x, 1 - y)]
        slots = [2 * (1 - x) + y, 2 * x + (1 - y), 2 * (1 - x) + (1 - y)]
        pass_to = (jnp.where(c == 0, x, 1 - x), jnp.where(c == 0, 1 - y, y))
        pass_slot = jnp.where(c == 0, slots[0], slots[1])

        def send(k, i, slot, chip):
            a, rows = pieces[i]
            return pltpu.make_async_remote_copy(
                src_ref=outs[a].at[slot, c, rows], dst_ref=outs[a].at[slot, c, rows], send_sem=send_sems.at[k * n_p + i],
                recv_sem=recv_sems.at[k * n_p + i], device_id=(*chip, c), device_id_type=MESH)

        def forward(k, i, half):
            a, rows = pieces[i]
            return pltpu.make_async_remote_copy(
                src_ref=outs[a].at[slots[k], half, rows], dst_ref=outs[a].at[slots[k], half, rows],
                send_sem=fsend_sems.at[k * n_p + i], recv_sem=frecv_sems.at[k * n_p + i],
                device_id=(x, y, 1 - c), device_id_type=MESH)

        started = [send(k, i, me, chip) for i in range(n_p) for k, chip in enumerate(near)]
        for cp in started:
            cp.start()
        for i in range(n_p):
            for k, chip in enumerate(near):
                send(k, i, slots[k], chip).wait_recv()
            later = [send(2, i, pass_slot, pass_to), forward(0, i, c), forward(1, i, c)]
            for cp in later:
                cp.start()
            started += later
        for i in range(n_p):
            send(2, i, slots[2], pass_to).wait_recv()
            fw = forward(2, i, c)
            fw.start()
            started.append(fw)
        for i in range(n_p):
            for k in range(3):
                forward(k, i, 1 - c).wait_recv()
        for cp in started:
            cp.wait_send()

    anyspec = pl.BlockSpec(memory_space=pl.ANY)
    sems = pltpu.SemaphoreType.DMA((3 * n_p,))
    return pl.pallas_call(
        body,
        name=name,
        in_specs=[anyspec] * n,
        out_specs=[anyspec] * n,
        out_shape=[jax.ShapeDtypeStruct(b.shape, b.dtype) for b in bufs],
        input_output_aliases={a: a for a in range(n)},
        scratch_shapes=[sems, sems, sems, sems],
    )(*bufs)


def _swap_plan(n_slabs):
    def plan(refs):
        x, y, c = _coords()
        out, k = [], 0
        for i, n in enumerate(n_slabs):
            g, land = refs[2 * i], refs[2 * i + 1]
            for p in range(n):
                out.append((k, (g.at[p, 1 - c], land.at[p], (x, y, 1 - c), None), (land.at[p], None)))
                k += 1
        return out
    return plan


def _is_one_of(chip, dests):
    hit = chip == dests[0]
    for d in dests[1:]:
        hit = hit | (chip == d)
    return hit


def _slab_of(chip, dests):
    return sum(j * (chip == d).astype(jnp.int32) for j, d in enumerate(dests))


def _scatter_plan(dest_sets):
    def plan(refs):
        x, y, c = _coords()
        me = 2 * x + y
        out = []
        for k, (px, py) in enumerate(_other_chips(x, y)):
            peer = 2 * px + py
            for i, dests in enumerate(dest_sets):
                cs, land = refs[2 * i], refs[2 * i + 1]
                everyone = len(dests) == N_CHIPS
                send = (cs.at[_slab_of(peer, dests)], land.at[k], (px, py, c),
                        None if everyone else _is_one_of(peer, dests))
                recv = (land.at[k], None if everyone else _is_one_of(me, dests))
                out.append((k * len(dest_sets) + i, send, recv))
        return out
    return plan


def _join_plans(parts):
    def plan(refs):
        out, b0, k0 = [], 0, 0
        for part_plan, n_bufs, n_copies in parts:
            out += [(k0 + k, send, recv) for k, send, recv in part_plan(refs[b0:b0 + n_bufs])]
            b0 += n_bufs
            k0 += n_copies
        return out
    return plan


def _allgather_plan():
    def plan(refs):
        x, y, c = _coords()
        (land,) = refs
        me = 4 * x + 2 * y + c
        out = []
        for r in range(1, 8):
            px = 1 - x if r & 4 else x
            py = 1 - y if r & 2 else y
            pc = 1 - c if r & 1 else c
            out.append((r - 1, (land.at[me], land.at[me], (px, py, pc), None), (land.at[4 * px + 2 * py + pc], None)))
        return out
    return plan


def _sum_gathered(land, name):
    def body(land_ref, o_ref):
        acc = land_ref[0]
        for d in range(1, 8):
            acc = acc + land_ref[d]
        o_ref[...] = acc

    return pl.pallas_call(
        body,
        name=name,
        out_shape=jax.ShapeDtypeStruct(land.shape[1:], F32),
        compiler_params=_params(),
    )(land)


def _join_halves(bufs, n_chunks, name):
    n = len(bufs)
    pieces = [(a, rows) for a in range(n) for rows in _chunks(bufs[a].shape[1], n_chunks[a])]
    n_p = len(pieces)

    def body(*refs):
        outs = refs[n:2 * n]
        send_sems, recv_sems = refs[2 * n:]
        x, y, c = _coords()

        def copy(i, half):
            a, rows = pieces[i]
            return pltpu.make_async_remote_copy(
                src_ref=outs[a].at[half, rows], dst_ref=outs[a].at[half, rows], send_sem=send_sems.at[i],
                recv_sem=recv_sems.at[i], device_id=(x, y, 1 - c), device_id_type=MESH)

        sends = [copy(i, c) for i in range(n_p)]
        for cp in sends:
            cp.start()
        for i in range(n_p):
            copy(i, 1 - c).wait_recv()
        for cp in sends:
            cp.wait_send()

    anyspec = pl.BlockSpec(memory_space=pl.ANY)
    sems = pltpu.SemaphoreType.DMA((n_p,))
    return pl.pallas_call(
        body,
        name=name,
        in_specs=[anyspec] * n,
        out_specs=[anyspec] * n,
        out_shape=[jax.ShapeDtypeStruct(b.shape, b.dtype) for b in bufs],
        input_output_aliases={a: a for a in range(n)},
        scratch_shapes=[sems, sems],
    )(*bufs)


def _row_tile(rows, cap):
    t = cap
    while rows % t:
        t //= 2
    return t


def _add_my_half(g, r, name):
    n_slabs, _, R, C = g.shape
    tr = R if n_slabs > 1 else _row_tile(R, 256)

    def body(c_ref, g_ref, r_ref, o_ref):
        o_ref[...] = (g_ref[0] + r_ref[...]).astype(o_ref.dtype)

    return pl.pallas_call(
        body,
        name=name,
        grid_spec=pltpu.PrefetchScalarGridSpec(
            num_scalar_prefetch=1,
            grid=(n_slabs, R // tr),
            in_specs=[pl.BlockSpec((1, 1, tr, C), lambda p, i, c_ref: (p, c_ref[0], i, 0)),
                      pl.BlockSpec((1, tr, C), lambda p, i, c_ref: (p, i, 0))],
            out_specs=pl.BlockSpec((1, tr, C), lambda p, i, c_ref: (p, i, 0)),
        ),
        out_shape=jax.ShapeDtypeStruct(r.shape, jnp.bfloat16),
        compiler_params=_params(("parallel", "parallel")),
    )(lax.axis_index("c").reshape(1).astype(jnp.int32), g, r)


def _sum_slabs(own, got, name, deps=()):
    _, R, C = own.shape
    tr = _row_tile(R, 256)

    def body(s_ref, own_ref, got_ref, *rest):
        rest[-1][0] = ((own_ref[0].astype(F32) + got_ref[0].astype(F32)) + got_ref[1].astype(F32)) + got_ref[2].astype(F32)

    xi, yi, ci = _coords()
    return pl.pallas_call(
        body,
        name=name,
        grid_spec=pltpu.PrefetchScalarGridSpec(
            num_scalar_prefetch=1,
            grid=(R // tr,),
            in_specs=[pl.BlockSpec((1, tr, C), lambda i, s: (s[0], i, 0)),
                      pl.BlockSpec((3, tr, C), lambda i, s: (0, i, 0))] + [ANY_SPEC] * len(deps),
            out_specs=pl.BlockSpec((1, tr, C), lambda i, s: (s[1], i, 0)),
        ),
        out_shape=jax.ShapeDtypeStruct((2, R, C), F32),
        compiler_params=_params(("parallel",)),
    )(jnp.stack([2 * xi + yi, ci]).astype(jnp.int32), own, got, *deps)


def _sum_parts(owns, got, dest_sets, name):
    n = len(owns)
    _, R, C = owns[0].shape
    tr = _row_tile(R, 256)

    def body(s_ref, *refs):
        got_ref, o_ref = refs[n], refs[-1]
        total = jnp.zeros((tr, C), F32)
        for i in range(n):
            total = total + jnp.where(s_ref[2 + 2 * i] == 1, refs[i][0].astype(F32), 0.0)
        o_ref[0] = ((total + got_ref[0].astype(F32)) + got_ref[1].astype(F32)) + got_ref[2].astype(F32)

    xi, yi, ci = _coords()
    me = 2 * xi + yi
    scalars = [ci, ci]
    for dests in dest_sets:
        scalars += [_is_one_of(me, dests).astype(jnp.int32), _slab_of(me, dests)]
    own_spec = lambda i: pl.BlockSpec((1, tr, C), lambda r, s: (s[3 + 2 * i], r, 0))
    return pl.pallas_call(
        body,
        name=name,
        grid_spec=pltpu.PrefetchScalarGridSpec(
            num_scalar_prefetch=1,
            grid=(R // tr,),
            in_specs=[own_spec(i) for i in range(n)] + [pl.BlockSpec((3, tr, C), lambda r, s: (0, r, 0))],
            out_specs=pl.BlockSpec((1, tr, C), lambda r, s: (s[0], r, 0)),
        ),
        out_shape=jax.ShapeDtypeStruct((2, R, C), F32),
        compiler_params=_params(("parallel",)),
    )(jnp.stack(scalars).astype(jnp.int32), *owns, got)


def _adamw_math(w, g, m, v):
    m = ADAM_B1 * m + (1.0 - ADAM_B1) * g
    v = ADAM_B2 * v + (1.0 - ADAM_B2) * (g * g)
    m_hat = m / (1.0 - ADAM_B1 ** ADAM_STEP)
    v_hat = v / (1.0 - ADAM_B2 ** ADAM_STEP)
    delta = -ADAM_LR * (m_hat / (jnp.sqrt(v_hat) + ADAM_EPS) + ADAM_WD * w)
    return delta, m, v


def _adamw_big(w, g, m, v, name):
    R, C = w.shape
    tr = min(128, R)

    def body(w_ref, g_ref, m_ref, v_ref, g_out, d_out, m_out, v_out):
        g = g_ref[...]
        d, mn, vn = _adamw_math(w_ref[...], g, m_ref[...], v_ref[...])
        g_out[...] = g
        d_out[...] = d
        m_out[...] = mn
        v_out[...] = vn

    spec = pl.BlockSpec((tr, C), lambda i: (i, 0))
    return pl.pallas_call(
        body,
        name=name,
        grid=(R // tr,),
        in_specs=[spec] * 4,
        out_specs=[spec] * 4,
        out_shape=[jax.ShapeDtypeStruct((R, C), F32)] * 4,
        compiler_params=_params(("parallel",)),
    )(w, g, m, v)


def _adamw_small(ws, gs, ms, vs, name):
    n = len(ws)

    def body(*refs):
        for a in range(n):
            d, mn, vn = _adamw_math(refs[a][...], refs[n + a][...], refs[2 * n + a][...], refs[3 * n + a][...])
            refs[4 * n + a][...] = d
            refs[5 * n + a][...] = mn
            refs[6 * n + a][...] = vn

    shapes = [jax.ShapeDtypeStruct(w.shape, F32) for w in ws]
    outs = pl.pallas_call(
        body,
        name=name,
        out_shape=shapes * 3,
        compiler_params=_params(),
    )(*ws, *gs, *ms, *vs)
    return outs[:n], outs[n:2 * n], outs[2 * n:]


def _to_blockdiag(w):
    per = CW // LRU_BW
    w4 = w.reshape(N_CT, per, LRU_BW, LRU_BW)
    eye = jnp.eye(per, dtype=w.dtype)
    return (w4[:, :, :, None, :] * eye[None, :, None, :, None]).reshape(N_CT, CW, CW)


def _from_blockdiag(g):
    per = CW // LRU_BW
    g5 = g.reshape(N_CT, per, LRU_BW, per, LRU_BW)
    return jnp.stack([g5[:, b, :, b, :] for b in range(per)], axis=1).reshape(LRU_BLOCKS, LRU_BW, LRU_BW)


def _local_grads(x2d, tgt2d, B, S, g_in, w_all, conv_w, conv_b, gate_x_w, gate_x_b, gate_a_w, gate_a_b, lam, gain,
                 proj_weights, g_fin, reduce, deps=()):
    wx_bd = _c(_to_blockdiag(gate_x_w))
    wa_bd = _c(_to_blockdiag(gate_a_w))
    tables = _retention_tables(S)
    gain3 = gain.reshape(HEADS, 1, DK)

    proj, ht = _inproj_fwd(x2d, g_in, w_all, deps)
    hlru, ya = _lru_fwd(proj, conv_w, conv_b, wx_bd, wa_bd, gate_x_b, gate_a_b, lam, B, S)
    o_pre, yb, states = _ret_fwd(proj, tables, gain3, B, S)
    wpa, wpb, wout = proj_weights(yb)
    loss, dx2, dya, dyb, dm, dgf, gw_proj = _mid(ya, yb, proj, x2d, tgt2d, wpa, wpb, wout, g_fin)
    g3 = _inproj_bwd_dw(ht, [dm], "inproj_bwd_dw_m")
    deps = reduce.m_ready(gw_proj, g3)
    dr, dgain = _ret_bwd(dyb, o_pre, proj, states, tables, gain3, B, S, deps)
    deps = reduce.ret_done(dr)
    g12 = _inproj_bwd_dw(ht, [dr], "inproj_bwd_dw_r", deps)
    deps = reduce.r_ready(g12)
    dxa, dga, dcw, dcb, dwx_bd, dwa_bd, dbx, dba, dlam = _lru_bwd(
        dya, proj, hlru, conv_w, conv_b, wx_bd, wa_bd, gate_x_b, gate_a_b, lam, B, S, deps)
    small = dict(conv_w=dcw, conv_b=dcb, gate_x_w=_from_blockdiag(dwx_bd), gate_x_b=dbx,
                 gate_a_w=_from_blockdiag(dwa_bd), gate_a_b=dba, lru_lambda=dlam, gn_gain=dgain.reshape(HEADS, DK),
                 norm_final=dgf)
    loss_rows = jnp.broadcast_to(loss, (SUBLANES, LANES))
    deps = reduce.lru_done(dxa, jnp.concatenate([_pack_small(small), loss_rows], axis=0))
    g0 = _inproj_bwd_dw(ht, [dxa, dga], "inproj_bwd_dw_a", deps)
    deps = reduce.a_ready(g0)
    n_tiles = x2d.shape[0] // min(DX_TILE, x2d.shape[0])
    grad_x, dgin = _inproj_bwd_dx([dxa, dga, dr, dm], w_all, x2d, dx2, g_in, 0, n_tiles, None, "inproj_bwd_dx", deps)
    return grad_x, dgin


ALL_CHIPS = (0, 1, 2, 3)


class _GradReduce:
    def __init__(self, proj_done):
        self.pending = {}
        self.proj_done = proj_done
        self.land_in = None

    def _start(self, key, parts, name):
        bufs, plans, shared = [], [], None
        for part_bufs, plan, n_copies, part_shared in parts:
            if part_shared is not None:
                shared = len(bufs) + part_shared
            plans.append((plan, len(part_bufs), n_copies))
            bufs += part_bufs
        plan = _join_plans(plans)
        send_sems, recv_sems, bufs, token = _copies_start(bufs, plan, sum(p[2] for p in plans), name + "_start")
        if shared is not None:
            self.land_in = bufs[shared]
        self.pending[key] = (send_sems, recv_sems, bufs, plan, name + "_wait", shared)
        return (token,)

    def _finish(self, key, after):
        send_sems, recv_sems, bufs, plan, name, shared = self.pending.pop(key)
        if shared is not None:
            bufs[shared] = self.land_in
        bufs = _copies_wait(send_sems, recv_sems, bufs, after, plan, name)
        if shared is not None:
            self.land_in = bufs[shared]
        return bufs

    @staticmethod
    def _swap(pieces):
        bufs = []
        for g in pieces:
            bufs += [g, lax.empty((g.shape[0],) + g.shape[2:], F32)]
        n_slabs = [g.shape[0] for g in pieces]
        return bufs, _swap_plan(n_slabs), sum(n_slabs), None

    def _scatter(self, sums, dest_sets):
        bufs = []
        for cs in sums:
            bufs += [cs, lax.empty((3,) + cs.shape[1:], cs.dtype)]
        if self.land_in is not None:
            bufs[-1] = self.land_in
        return bufs, _scatter_plan(dest_sets), 3 * len(sums), len(bufs) - 1

    @staticmethod
    def _gather8(block):
        x, y, c = _coords()
        land = lax.dynamic_update_slice(lax.empty((8,) + block.shape, F32), block[None], (4 * x + 2 * y + c, 0, 0))
        return [land], _allgather_plan(), 7, None

    def m_ready(self, gw_proj, g3):
        rows = gw_proj.shape[2] * gw_proj.shape[3]
        return self._start("m", [self._swap([gw_proj.reshape(N_CHIPS, 2, rows, D_MODEL), g3])], "swap_m")

    def ret_done(self, after):
        proj, land_p, g3, land_3 = self._finish("m", after)
        self.sums_m = [_add_my_half(proj, land_p, "chip_sum_proj"), _add_my_half(g3, land_3, "chip_sum_m")]
        return ()

    def r_ready(self, g12):
        return self._start("r", [self._scatter(self.sums_m, [ALL_CHIPS, (3,)]), self._swap([g12])], "scatter_m_swap_r")

    def lru_done(self, after, packed):
        self.csp, self.gotp, self.cs3, _, g12, land_12 = self._finish("r", after)
        sums_r = [_add_my_half(g12, land_12, "chip_sum_r")]
        return (self._start("sr", [self._scatter(sums_r, [(1, 2)])], "scatter_r")
                + self._start("small", [self._gather8(packed)], "gather_small"))

    def a_ready(self, g0):
        deps = self._start("a", [self._swap([g0])], "swap_a")
        (g_proj,) = _join_halves([_sum_slabs(self.csp, self.gotp, "sum_w_proj", deps)], [4], "join_halves_proj")
        g0, land_0 = self._finish("a", self.proj_done(g_proj))
        return self._start("sa", [self._scatter([_add_my_half(g0, land_0, "chip_sum_a")], [(0,)])], "scatter_a")

    def finish(self, dgin, w_in_done):
        (token,) = self._start("n", [self._gather8(dgin)], "gather_norm_in")
        (small,) = self._finish("small", token)
        cs12, _ = self._finish("sr", token)
        cs0, _ = self._finish("sa", token)
        half_in = _sum_parts([self.cs3, cs12, cs0], self.land_in, [(3,), (1, 2), (0,)], "sum_w_in")
        after = w_in_done(_join_halves([half_in], [8], "join_halves")[0])
        (norm_in,) = self._finish("n", after)
        return _sum_gathered(small, "sum_small_grads"), _sum_gathered(norm_in, "sum_norm_in_grad")


_SMALL = ("gate_x_w", "gate_a_w", "conv_w", "conv_b", "gate_x_b", "gate_a_b", "lru_lambda", "gn_gain", "norm_final")
_SMALL_SHAPES = dict(gate_x_w=(LRU_BLOCKS, LRU_BW, LRU_BW), gate_a_w=(LRU_BLOCKS, LRU_BW, LRU_BW),
                     norm_in=(1, D_MODEL), conv_w=(CONV, D_MODEL), conv_b=(1, D_MODEL), gate_x_b=(1, D_MODEL),
                     gate_a_b=(1, D_MODEL), lru_lambda=(1, D_MODEL), gn_gain=(HEADS, DK), norm_final=(1, D_MODEL))


def _pack_small(small):
    return jnp.concatenate([small[k].reshape(-1, 128) for k in _SMALL], axis=0)


def _unpack_small(packed):
    out, r = {}, 0
    for k in _SMALL:
        shape = _SMALL_SHAPES[k]
        rows = 1
        for s in shape:
            rows *= s
        rows //= 128
        out[k] = packed[r:r + rows].reshape(shape)
        r += rows
    return out


def kernel(x, norm_in, w_in, conv_w, conv_b, gate_x_w, gate_x_b, gate_a_w, gate_a_b, lru_lambda, gn_gain, w_proj_a, w_proj_b, w_out, norm_final, loss_target, m_norm_in, m_w_in, m_conv_w, m_conv_b, m_gate_x_w, m_gate_x_b, m_gate_a_w, m_gate_a_b, m_lru_lambda, m_gn_gain, m_w_proj_a, m_w_proj_b, m_w_out, m_norm_final, v_norm_in, v_w_in, v_conv_w, v_conv_b, v_gate_x_w, v_gate_x_b, v_gate_a_w, v_gate_a_b, v_lru_lambda, v_gn_gain, v_w_proj_a, v_w_proj_b, v_w_out, v_norm_final):
    B, S, _ = x.shape
    T = B * S
    xi, yi, ci = _coords()
    chip = 2 * xi + yi

    cshard = D_MODEL // N_CHIPS
    mine = _cast_into_slot([w_in[0].reshape(2, D_MODEL // 2, 2 * D_MODEL)]
                           + [w[0].reshape(2, cshard // 2, D_MODEL) for w in (w_proj_a, w_proj_b, w_out)],
                           "cast_weights")
    plan = _gather_plan(3)
    s_sems, r_sems, pbufs, token = _copies_start(mine[1:], plan, 9, "gather_proj_start")
    gshard = DK // N_CHIPS
    tiny = jnp.concatenate([conv_w[0], jnp.zeros((4, cshard), F32), jnp.pad(gn_gain[0], ((0, 4), (0, cshard - gshard)))],
                           axis=0).reshape(1, 2, SUBLANES, cshard)
    tiny_buf = lax.dynamic_update_slice(lax.empty((N_CHIPS, 2, SUBLANES, cshard), F32), tiny, (chip, 0, 0, 0))
    w_buf, tiny_buf = _gather_chips([mine[0], tiny_buf], [8, 1], "gather_weights")
    w_all = w_buf.reshape(N_CHIPS, D_MODEL, 2 * D_MODEL)
    tiny_all = tiny_buf.reshape(N_CHIPS, 2 * SUBLANES, cshard)

    def proj_weights(after):
        got = _copies_wait(s_sems, r_sems, pbufs, after, plan, "gather_proj_wait")
        return [b.reshape(D_MODEL, D_MODEL) for b in got]

    conv_w_full = jnp.transpose(tiny_all[:, 0:CONV, :], (1, 0, 2)).reshape(CONV, D_MODEL)
    gain_full = jnp.transpose(tiny_all[:, 8:8 + HEADS, :gshard], (1, 0, 2)).reshape(HEADS, DK)

    weights = dict(norm_in=norm_in, w_in=w_in, conv_w=conv_w, conv_b=conv_b, gate_x_w=gate_x_w, gate_x_b=gate_x_b,
                   gate_a_w=gate_a_w, gate_a_b=gate_a_b, lru_lambda=lru_lambda, gn_gain=gn_gain, w_proj_a=w_proj_a,
                   w_proj_b=w_proj_b, w_out=w_out, norm_final=norm_final)
    ms = dict(norm_in=m_norm_in, w_in=m_w_in, conv_w=m_conv_w, conv_b=m_conv_b, gate_x_w=m_gate_x_w,
              gate_x_b=m_gate_x_b, gate_a_w=m_gate_a_w, gate_a_b=m_gate_a_b, lru_lambda=m_lru_lambda, gn_gain=m_gn_gain,
              w_proj_a=m_w_proj_a, w_proj_b=m_w_proj_b, w_out=m_w_out, norm_final=m_norm_final)
    vs = dict(norm_in=v_norm_in, w_in=v_w_in, conv_w=v_conv_w, conv_b=v_conv_b, gate_x_w=v_gate_x_w,
              gate_x_b=v_gate_x_b, gate_a_w=v_gate_a_w, gate_a_b=v_gate_a_b, lru_lambda=v_lru_lambda, gn_gain=v_gn_gain,
              w_proj_a=v_w_proj_a, w_proj_b=v_w_proj_b, w_out=v_w_out, norm_final=v_norm_final)
    names = list(weights)
    grads, delta, new_m, new_v = {}, {}, {}, {}

    def update_big(k, g):
        shp = weights[k].shape
        two = lambda a: a.reshape(shp[1], shp[2])
        g, d, mn, vn = _adamw_big(two(weights[k]), g, two(ms[k]), two(vs[k]), "adamw_" + k)
        grads[k], delta[k], new_m[k], new_v[k] = g.reshape(shp), d.reshape(shp), mn.reshape(shp), vn.reshape(shp)
        return d

    def proj_done(g_proj):
        g_pr = g_proj.reshape(2, 3, D_MODEL // (2 * N_CHIPS), D_MODEL)
        for i, k in enumerate(("w_proj_a", "w_proj_b", "w_out")):
            last = update_big(k, g_pr[:, i].reshape(cshard, D_MODEL))
        return last

    reduce = _GradReduce(proj_done)
    grad_x, dgin = _local_grads(
        x.reshape(T, D_MODEL), loss_target.reshape(T, D_MODEL), B, S, norm_in, w_all, conv_w_full, conv_b,
        gate_x_w[0], gate_x_b, gate_a_w[0], gate_a_b, lru_lambda, gain_full, proj_weights,
        norm_final.reshape(1, D_MODEL), reduce, deps=(token,))

    small_sum, g_norm_in = reduce.finish(dgin.reshape(SUBLANES, LANES),
                                         lambda g: update_big("w_in", g.reshape(D_MODEL, 2 * D_MODEL)))
    loss = small_sum[small_sum.shape[0] - SUBLANES, 0]

    gsm = _unpack_small(small_sum)
    gsm["norm_in"] = g_norm_in
    gsm["conv_w"] = lax.dynamic_slice_in_dim(gsm["conv_w"], chip * cshard, cshard, axis=1)
    gsm["gn_gain"] = lax.dynamic_slice_in_dim(gsm["gn_gain"], chip * gshard, gshard, axis=1)
    smalls = [k for k in names if k not in delta]

    def view(a):
        return a.reshape(1, -1) if a.ndim == 1 else (a.reshape(a.shape[1:]) if a.ndim > 2 else a)

    ds, mns, vns = _adamw_small([view(weights[k]) for k in smalls], [gsm[k].reshape(view(weights[k]).shape) for k in smalls],
                                [view(ms[k]) for k in smalls], [view(vs[k]) for k in smalls], "adamw_small")
    for k, d, mn, vn in zip(smalls, ds, mns, vns):
        shp = weights[k].shape
        grads[k], delta[k], new_m[k], new_v[k] = gsm[k].reshape(shp), d.reshape(shp), mn.reshape(shp), vn.reshape(shp)

    return (loss, grad_x.reshape(B, S, D_MODEL), *[grads[k] for k in names], *[delta[k] for k in names],
            *[new_m[k] for k in names], *[new_v[k] for k in names])
```

```python
import jax
import jax.numpy as jnp
from jax import lax
from jax.experimental import pallas as pl
from jax.experimental.pallas import tpu as pltpu

F32 = jnp.float32
_MXU = jnp.bfloat16

D_MODEL = 1024
N_GROUPS = 8
HEADS = 4
DK = 256
CHUNK = 128
CONV = 4
LRU_BLOCKS = 16
LRU_BW = 64
LRU_C = 8.0
ROPE_THETA = 10000.0
EPS = 1e-6
CW = 256
N_CT = D_MODEL // CW
N_CHIPS = 4
MESH = pl.DeviceIdType.MESH

ADAM_LR = 0.001
ADAM_B1 = 0.9
ADAM_B2 = 0.999
ADAM_EPS = 1e-08
ADAM_WD = 0.01
ADAM_STEP = 10

VMEM_LIMIT = 56 * 1024 * 1024


def _c(v):
    return v.astype(_MXU)


def _dot(a, b):
    return lax.dot_general(a, b, (((1,), (0,)), ((), ())), preferred_element_type=F32)


def _dot_nt(a, b):
    return lax.dot_general(a, b, (((1,), (1,)), ((), ())), preferred_element_type=F32)


def _dot_tn(a, b):
    return lax.dot_general(a, b, (((0,), (0,)), ((), ())), preferred_element_type=F32)


def _sigmoid(z):
    return 0.5 * jnp.tanh(0.5 * z) + 0.5


ANY_SPEC = pl.BlockSpec(memory_space=pl.ANY)


def _after(body, n_in, deps):
    n_deps = len(deps)

    def wrapped(*refs):
        return body(*refs[:n_in], *refs[n_in + n_deps:])

    return wrapped


def _params(sem=None):
    if sem is None:
        return pltpu.CompilerParams(vmem_limit_bytes=VMEM_LIMIT)
    return pltpu.CompilerParams(vmem_limit_bytes=VMEM_LIMIT, dimension_semantics=sem)


def _inproj_fwd(x2d, g_in, w_all, deps=()):
    T = x2d.shape[0]
    tm = min(1024, T)
    n_i = T // tm

    def body(*refs):
        x_ref, g_ref, w_ref = refs[:3]
        proj_ref, ht_ref, h_all = refs[-3:]
        i = pl.program_id(1)
        rows = pl.ds(pl.multiple_of(i * tm, tm), tm)

        @pl.when(pl.program_id(0) == 0)
        def _():
            x = x_ref[...]
            r = lax.rsqrt(jnp.mean(x * x, axis=-1, keepdims=True) + EPS)
            h = x * r * g_ref[...]
            h_all[rows, :] = h.astype(h_all.dtype)
            ht_ref[...] = h.T.astype(ht_ref.dtype)

        proj_ref[...] = _dot(h_all[rows, :], w_ref[0])

    first = lambda j, i: jnp.where(j == 0, i, n_i - 1)
    return pl.pallas_call(
        body,
        name="inproj_fwd",
        grid=(N_GROUPS, n_i),
        in_specs=[
            pl.BlockSpec((tm, D_MODEL), lambda j, i: (first(j, i), 0)),
            pl.BlockSpec((1, D_MODEL), lambda j, i: (0, 0)),
            pl.BlockSpec((1, D_MODEL, D_MODEL), lambda j, i: (j // 2, 0, j % 2)),
        ] + [pl.BlockSpec(memory_space=pl.ANY)] * len(deps),
        out_specs=[
            pl.BlockSpec((tm, D_MODEL), lambda j, i: (i, j)),
            pl.BlockSpec((D_MODEL, tm), lambda j, i: (0, first(j, i))),
        ],
        out_shape=[
            jax.ShapeDtypeStruct((T, N_GROUPS * D_MODEL), F32),
            jax.ShapeDtypeStruct((D_MODEL, T), _MXU),
        ],
        scratch_shapes=[pltpu.VMEM((T, D_MODEL), _MXU)],
        compiler_params=_params(("arbitrary", "arbitrary")),
    )(x2d, g_in, w_all, *deps)


def _scan_fwd(a, u):
    n = a.shape[0]
    row = lax.broadcasted_iota(jnp.int32, a.shape, 0)
    s = 1
    while s < n:
        m = row >= s
        u = u + a * jnp.where(m, pltpu.roll(u, s, 0), 0.0)
        a = a * jnp.where(m, pltpu.roll(a, s, 0), 1.0)
        s *= 2
    return a, u


def _scan_bwd(b, g):
    n = b.shape[0]
    row = lax.broadcasted_iota(jnp.int32, b.shape, 0)
    s = 1
    while s < n:
        m = row < n - s
        g = g + b * jnp.where(m, pltpu.roll(g, n - s, 0), 0.0)
        b = b * jnp.where(m, pltpu.roll(b, n - s, 0), 1.0)
        s *= 2
    return b, g


LANES = 128
SUBLANES = 8


def _scan_scratch(tc):
    by_lanes = pltpu.VMEM((CW // LANES, tc, LANES), F32)
    return [by_lanes, by_lanes, pltpu.VMEM((tc // SUBLANES, CW), F32), pltpu.VMEM((tc, CW), F32)]


def _scan_tile(a, u, edge, la_ref, lh_ref, c_ref, dst_ref, reverse):
    n, w = a.shape
    groups = n // SUBLANES
    a3 = a.reshape(groups, SUBLANES, w)
    u3 = u.reshape(groups, SUBLANES, w)
    row = lax.broadcasted_iota(jnp.int32, a3.shape, 1)
    for s in (1, 2, 4):
        m = (row < SUBLANES - s) if reverse else (row >= s)
        shift = SUBLANES - s if reverse else s
        u3 = u3 + a3 * jnp.where(m, pltpu.roll(u3, shift, 1), 0.0)
        a3 = a3 * jnp.where(m, pltpu.roll(a3, shift, 1), 1.0)
    al = a3.reshape(n, w)
    hl = u3.reshape(n, w)
    blocks = w // LANES
    for q in range(blocks):
        la_ref[q] = al[:, q * LANES:(q + 1) * LANES]
        lh_ref[q] = hl[:, q * LANES:(q + 1) * LANES]
    ends = pl.ds(0 if reverse else SUBLANES - 1, groups, stride=SUBLANES)
    end_a = jnp.concatenate([la_ref.at[q][ends, :] for q in range(blocks)], axis=-1)
    end_h = jnp.concatenate([lh_ref.at[q][ends, :] for q in range(blocks)], axis=-1)
    prod, part = (_scan_bwd if reverse else _scan_fwd)(end_a, end_h)
    total = part + prod * edge
    g_row = lax.broadcasted_iota(jnp.int32, total.shape, 0)
    if reverse:
        c_ref[...] = jnp.where(g_row == groups - 1, edge, pltpu.roll(total, groups - 1, 0))
    else:
        c_ref[...] = jnp.where(g_row == 0, edge, pltpu.roll(total, 1, 0))
    for g in range(groups):
        rows = slice(g * SUBLANES, (g + 1) * SUBLANES)
        for q in range(blocks):
            cols = slice(q * LANES, (q + 1) * LANES)
            dst_ref[rows, cols] = lh_ref[q, rows, :] + la_ref[q, rows, :] * c_ref[g:g + 1, cols]


def _softplus_neg(lam):
    z = -lam
    return jnp.maximum(z, 0.0) + jnp.log1p(jnp.exp(-jnp.abs(z)))


def _lru_gates(xc, wx_ref, wa_ref, bx_ref, ba_ref, lam_ref):
    xcb = _c(xc)
    i_t = _sigmoid(_dot(xcb, wx_ref[0]) + bx_ref[...])
    r_t = _sigmoid(_dot(xcb, wa_ref[0]) + ba_ref[...])
    sp = _softplus_neg(lam_ref[...])
    log_a = (-LRU_C) * r_t * sp
    a = jnp.exp(log_a)
    mult = jnp.sqrt(1.0 - a * a)
    return xcb, i_t, r_t, sp, a, mult


def _conv_from_ext(ext_ref, xa, cw_ref, cb_ref, tc):
    return (cb_ref[...] + cw_ref[3:4, :] * xa + cw_ref[2:3, :] * ext_ref[7:7 + tc, :]
            + cw_ref[1:2, :] * ext_ref[6:6 + tc, :] + cw_ref[0:1, :] * ext_ref[5:5 + tc, :])


def _lru_fwd(proj, conv_w, conv_b, wx_bd, wa_bd, bx, ba, lam, B, S):
    T = B * S
    tc = min(256, S)
    nt = S // tc
    h8 = tc // 8

    def body(xa_ref, halo_ref, ga_ref, cw_ref, cb_ref, wx_ref, wa_ref, bx_ref, ba_ref, lam_ref,
             h_ref, ya_ref, ext_ref, carry_ref, la_ref, lh_ref, c_ref):
        t = pl.program_id(2)

        @pl.when(t == 0)
        def _():
            carry_ref[...] = jnp.zeros_like(carry_ref)

        xa = xa_ref[...]
        ext_ref[0:8, :] = jnp.where(t == 0, 0.0, halo_ref[...])
        ext_ref[8:8 + tc, :] = xa
        xc = _conv_from_ext(ext_ref, xa, cw_ref, cb_ref, tc)
        _, i_t, _, _, a, mult = _lru_gates(xc, wx_ref, wa_ref, bx_ref, ba_ref, lam_ref)
        u = mult * (i_t * xc)
        _scan_tile(a, u, carry_ref[7:8, :], la_ref, lh_ref, c_ref, h_ref, False)
        h = h_ref[...]
        carry_ref[...] = h[tc - 8:tc, :]
        ga = ga_ref[...]
        ya_ref[...] = (ga * _sigmoid(ga) * h).astype(ya_ref.dtype)

    row = lambda b, t: b * nt + t
    vec = pl.BlockSpec((1, CW), lambda b, c, t: (0, c))
    mat = pl.BlockSpec((1, CW, CW), lambda b, c, t: (c, 0, 0))
    return pl.pallas_call(
        body,
        name="lru_fwd",
        grid=(B, N_CT, nt),
        in_specs=[
            pl.BlockSpec((tc, CW), lambda b, c, t: (row(b, t), c)),
            pl.BlockSpec((8, CW), lambda b, c, t: (jnp.maximum(row(b, t) * h8 - 1, 0), c)),
            pl.BlockSpec((tc, CW), lambda b, c, t: (row(b, t), N_CT + c)),
            pl.BlockSpec((CONV, CW), lambda b, c, t: (0, c)),
            vec, mat, mat, vec, vec, vec,
        ],
        out_specs=[
            pl.BlockSpec((tc, CW), lambda b, c, t: (row(b, t), c)),
            pl.BlockSpec((tc, CW), lambda b, c, t: (row(b, t), c)),
        ],
        out_shape=[
            jax.ShapeDtypeStruct((T, D_MODEL), F32),
            jax.ShapeDtypeStruct((T, D_MODEL), _MXU),
        ],
        scratch_shapes=[pltpu.VMEM((tc + 8, CW), F32), pltpu.VMEM((8, CW), F32)] + _scan_scratch(tc)[:3],
        compiler_params=_params(("parallel", "parallel", "arbitrary")),
    )(proj, proj, proj, conv_w, conv_b, wx_bd, wa_bd, bx, ba, lam)


def _lru_bwd(dya, proj, hlru, conv_w, conv_b, wx_bd, wa_bd, bx, ba, lam, B, S, deps=()):
    T = B * S
    tc = min(256, S)
    nt = S // tc
    h8 = tc // 8

    def body(dya_ref, xa_ref, xhalo_ref, ga_ref, h_ref, hhalo_ref, cw_ref, cb_ref, wx_ref, wa_ref, bx_ref, ba_ref,
             lam_ref, dxa_ref, dga_ref, dcw_ref, dcb_ref, dwx_ref, dwa_ref, dbx_ref, dba_ref, dlam_ref,
             ext_ref, ext2_ref, carry_ref, dhalo_ref, la_ref, lh_ref, c_ref, dh_ref):
        b = pl.program_id(1)
        t = pl.program_id(2)
        tt = nt - 1 - t

        @pl.when(t == 0)
        def _():
            carry_ref[...] = jnp.zeros_like(carry_ref)
            dhalo_ref[...] = jnp.zeros_like(dhalo_ref)

        @pl.when((t == 0) & (b == 0))
        def _():
            for r in (dcw_ref, dcb_ref, dwx_ref, dwa_ref, dbx_ref, dba_ref, dlam_ref):
                r[...] = jnp.zeros_like(r)

        xa = xa_ref[...]
        ext_ref[0:8, :] = jnp.where(tt == 0, 0.0, xhalo_ref[...])
        ext_ref[8:8 + tc, :] = xa
        xc = _conv_from_ext(ext_ref, xa, cw_ref, cb_ref, tc)
        xcb, i_t, r_t, sp, a, mult = _lru_gates(xc, wx_ref, wa_ref, bx_ref, ba_ref, lam_ref)

        h = h_ref[...]
        ga = ga_ref[...]
        dya_t = dya_ref[...]
        sg = _sigmoid(ga)
        dga_ref[...] = (dya_t * h * (sg * (1.0 + ga * (1.0 - sg)))).astype(dga_ref.dtype)
        dlru = dya_t * (ga * sg)

        row = lax.broadcasted_iota(jnp.int32, a.shape, 0)
        coef = jnp.where(row == tc - 1, 1.0, pltpu.roll(a, tc - 1, 0))
        _scan_tile(coef, dlru, carry_ref[0:1, :], la_ref, lh_ref, c_ref, dh_ref, True)
        dh = dh_ref[...]
        ext2_ref[0:tc, :] = a * dh
        carry_ref[...] = ext2_ref[0:8, :]

        ext2_ref[0:8, :] = jnp.where(tt == 0, 0.0, hhalo_ref[...])
        ext2_ref[8:8 + tc, :] = h
        hprev = ext2_ref[7:7 + tc, :]

        da = dh * hprev
        ix = i_t * xc
        dmult = dh * ix
        di = dh * mult * xc
        dxc = dh * mult * i_t
        dlog_a = da * a - dmult * (a * a) / mult
        dr = dlog_a * ((-LRU_C) * sp)
        dlam_ref[...] += jnp.sum(dlog_a * r_t, axis=0, keepdims=True) * (LRU_C * _sigmoid(-lam_ref[...]))
        dza = dr * r_t * (1.0 - r_t)
        dzx = di * i_t * (1.0 - i_t)
        dzab = _c(dza)
        dzxb = _c(dzx)
        dxc = dxc + _dot_nt(dzxb, wx_ref[0]) + _dot_nt(dzab, wa_ref[0])
        dwx_ref[0] += _dot_tn(xcb, dzxb)
        dwa_ref[0] += _dot_tn(xcb, dzab)
        dbx_ref[...] += jnp.sum(dzx, axis=0, keepdims=True)
        dba_ref[...] += jnp.sum(dza, axis=0, keepdims=True)

        dcb_ref[...] += jnp.sum(dxc, axis=0, keepdims=True)
        dcw_ref[3:4, :] += jnp.sum(dxc * xa, axis=0, keepdims=True)
        dcw_ref[2:3, :] += jnp.sum(dxc * ext_ref[7:7 + tc, :], axis=0, keepdims=True)
        dcw_ref[1:2, :] += jnp.sum(dxc * ext_ref[6:6 + tc, :], axis=0, keepdims=True)
        dcw_ref[0:1, :] += jnp.sum(dxc * ext_ref[5:5 + tc, :], axis=0, keepdims=True)
        ext2_ref[0:tc, :] = dxc
        ext2_ref[tc:tc + 8, :] = dhalo_ref[...]
        dxa = (cw_ref[3:4, :] * dxc + cw_ref[2:3, :] * ext2_ref[1:1 + tc, :]
               + cw_ref[1:2, :] * ext2_ref[2:2 + tc, :] + cw_ref[0:1, :] * ext2_ref[3:3 + tc, :])
        dxa_ref[...] = dxa.astype(dxa_ref.dtype)
        dhalo_ref[...] = ext2_ref[0:8, :]

    row_of = lambda b, t: b * nt + (nt - 1 - t)
    tile = lambda off: pl.BlockSpec((tc, CW), lambda c, b, t: (row_of(b, t), off + c))
    halo = pl.BlockSpec((8, CW), lambda c, b, t: (jnp.maximum(row_of(b, t) * h8 - 1, 0), c))
    vec = pl.BlockSpec((1, CW), lambda c, b, t: (0, c))
    mat = pl.BlockSpec((1, CW, CW), lambda c, b, t: (c, 0, 0))
    cwspec = pl.BlockSpec((CONV, CW), lambda c, b, t: (0, c))
    return pl.pallas_call(
        _after(body, 13, deps),
        name="lru_bwd",
        grid=(N_CT, B, nt),
        in_specs=[tile(0), tile(0), halo, tile(N_CT), tile(0), halo, cwspec, vec, mat, mat, vec, vec, vec]
        + [ANY_SPEC] * len(deps),
        out_specs=[tile(0), tile(0), cwspec, vec, mat, mat, vec, vec, vec],
        out_shape=[
            jax.ShapeDtypeStruct((T, D_MODEL), _MXU),
            jax.ShapeDtypeStruct((T, D_MODEL), _MXU),
            jax.ShapeDtypeStruct((CONV, D_MODEL), F32),
            jax.ShapeDtypeStruct((1, D_MODEL), F32),
            jax.ShapeDtypeStruct((N_CT, CW, CW), F32),
            jax.ShapeDtypeStruct((N_CT, CW, CW), F32),
            jax.ShapeDtypeStruct((1, D_MODEL), F32),
            jax.ShapeDtypeStruct((1, D_MODEL), F32),
            jax.ShapeDtypeStruct((1, D_MODEL), F32),
        ],
        scratch_shapes=[pltpu.VMEM((tc + 8, CW), F32), pltpu.VMEM((tc + 8, CW), F32),
                        pltpu.VMEM((8, CW), F32), pltpu.VMEM((8, CW), F32)] + _scan_scratch(tc),
        compiler_params=_params(("parallel", "arbitrary", "arbitrary")),
    )(dya, proj, proj, proj, hlru, hlru, conv_w, conv_b, wx_bd, wa_bd, bx, ba, lam, *deps)


def _retention_tables(S):
    half = DK // 2
    freqs = ROPE_THETA ** (-jnp.arange(half, dtype=F32) / half)
    ang = jnp.arange(S, dtype=F32)[:, None] * freqs[None, :]
    log_g = jnp.log1p(-(2.0 ** (-5.0 - jnp.arange(HEADS, dtype=F32))))
    idx = jnp.arange(CHUNK, dtype=F32)
    diff = idx[:, None] - idx[None, :]
    inner = jnp.where(diff >= 0, jnp.exp(jnp.maximum(diff, 0.0)[None] * log_g[:, None, None]), 0.0)
    cross = jnp.exp((idx[None, :] + 1.0) * log_g[:, None])[:, :, None]
    state = jnp.exp((CHUNK - 1.0 - idx[None, :]) * log_g[:, None])[:, :, None]
    gam = jnp.broadcast_to(jnp.exp(CHUNK * log_g)[:, None, None], (HEADS, 1, DK))
    return jnp.cos(ang), jnp.sin(ang), inner, cross, state, gam


def _rot(x, cos, sin):
    half = DK // 2
    x1, x2 = x[:, :half], x[:, half:]
    return jnp.concatenate([x1 * cos - x2 * sin, x1 * sin + x2 * cos], axis=-1)


def _rot_t(y, cos, sin):
    half = DK // 2
    y1, y2 = y[:, :half], y[:, half:]
    return jnp.concatenate([y1 * cos + y2 * sin, y2 * cos - y1 * sin], axis=-1)


def _groupnorm(o):
    mu = jnp.mean(o, axis=-1, keepdims=True)
    oc = o - mu
    rs = lax.rsqrt(jnp.mean(oc * oc, axis=-1, keepdims=True) + EPS)
    return oc * rs, rs


def _ret_specs(B, chunk_of):
    qkv = lambda g: pl.BlockSpec((B, CHUNK, D_MODEL), lambda c: (0, chunk_of(c), g))
    act = pl.BlockSpec((B, CHUNK, D_MODEL), lambda c: (0, chunk_of(c), 0))
    rope = pl.BlockSpec((CHUNK, DK // 2), lambda c: (chunk_of(c), 0))
    dmat = pl.BlockSpec((HEADS, CHUNK, CHUNK), lambda c: (0, 0, 0))
    dvec = pl.BlockSpec((HEADS, CHUNK, 1), lambda c: (0, 0, 0))
    hrow = pl.BlockSpec((HEADS, 1, DK), lambda c: (0, 0, 0))
    rst = pl.BlockSpec((1, B, HEADS, DK, DK), lambda c: (chunk_of(c), 0, 0, 0, 0))
    return qkv, act, rope, dmat, dvec, hrow, rst


def _ret_fwd(proj, tables, gain3, B, S):
    T = B * S
    nc = S // CHUNK
    cos, sin, dmat_t, cd_t, sd_t, gam_t = tables

    def body(q_ref, k_ref, v_ref, gb_ref, cos_ref, sin_ref, dm_ref, cd_ref, sd_ref, gam_ref, gain_ref,
             o_ref, yb_ref, rs_ref, state_ref):
        @pl.when(pl.program_id(0) == 0)
        def _():
            state_ref[...] = jnp.zeros_like(state_ref)

        cos_t, sin_t = cos_ref[...], sin_ref[...]
        for b, h in [(b, h) for b in range(B) for h in range(HEADS)]:
            cols = slice(h * DK, (h + 1) * DK)
            qb = _c(_rot(q_ref[b, :, cols], cos_t, sin_t))
            kb = _c(_rot(k_ref[b, :, cols], cos_t, sin_t) * (DK ** -0.5))
            v = v_ref[b, :, cols]
            state = state_ref[b, h]
            sb = _c(state)
            rs_ref[0, b, h] = sb
            scores = _dot_nt(qb, kb) * dm_ref[h]
            o = _dot(_c(scores), _c(v)) + _dot(qb, sb) * cd_ref[h]
            state_ref[b, h] = gam_ref[h] * state + _dot_tn(kb, _c(v * sd_ref[h]))
            o_ref[b, :, cols] = o
            n, _ = _groupnorm(o)
            gb = gb_ref[b, :, cols]
            yb_ref[b, :, cols] = (gb * _sigmoid(gb) * (n * gain_ref[h])).astype(yb_ref.dtype)

    qkv, act, rope, dmat, dvec, hrow, rst = _ret_specs(B, lambda c: c)
    proj3 = proj.reshape(B, S, proj.shape[1])
    o_pre, yb, states = pl.pallas_call(
        body,
        name="ret_fwd",
        grid=(nc,),
        in_specs=[qkv(2), qkv(3), qkv(4), qkv(5), rope, rope, dmat, dvec, dvec, hrow, hrow],
        out_specs=[act, act, rst],
        out_shape=[
            jax.ShapeDtypeStruct((B, S, D_MODEL), F32),
            jax.ShapeDtypeStruct((B, S, D_MODEL), _MXU),
            jax.ShapeDtypeStruct((nc, B, HEADS, DK, DK), _MXU),
        ],
        scratch_shapes=[pltpu.VMEM((B, HEADS, DK, DK), F32)],
        compiler_params=_params(("arbitrary",)),
    )(proj3, proj3, proj3, proj3, cos, sin, dmat_t, cd_t, sd_t, gam_t, gain3)
    return o_pre.reshape(T, D_MODEL), yb.reshape(T, D_MODEL), states


def _ret_bwd(dyb, o_pre, proj, states, tables, gain3, B, S, deps=()):
    T = B * S
    nc = S // CHUNK
    cos, sin, dmat_t, cd_t, sd_t, gam_t = tables

    def body(dyb_ref, o_ref, q_ref, k_ref, v_ref, gb_ref, rs_ref, cos_ref, sin_ref, dm_ref, cd_ref, sd_ref, gam_ref,
             gain_ref, dr_ref, dgain_ref, dstate_ref):
        @pl.when(pl.program_id(0) == 0)
        def _():
            dstate_ref[...] = jnp.zeros_like(dstate_ref)
            dgain_ref[...] = jnp.zeros_like(dgain_ref)

        cos_t, sin_t = cos_ref[...], sin_ref[...]
        for b, h in [(b, h) for b in range(B) for h in range(HEADS)]:
            cols = slice(h * DK, (h + 1) * DK)
            gain = gain_ref[h]
            n, rs = _groupnorm(o_ref[b, :, cols])
            gb = gb_ref[b, :, cols]
            sg = _sigmoid(gb)
            dy = dyb_ref[b, :, cols]
            part = lambda g: slice(g * D_MODEL + h * DK, g * D_MODEL + (h + 1) * DK)
            dr_ref[b, :, part(3)] = (dy * (n * gain) * (sg * (1.0 + gb * (1.0 - sg)))).astype(dr_ref.dtype)
            dgn = dy * (gb * sg)
            dgain_ref[h] += jnp.sum(dgn * n, axis=0, keepdims=True)
            dn = dgn * gain
            do = rs * (dn - jnp.mean(dn, axis=-1, keepdims=True) - n * jnp.mean(dn * n, axis=-1, keepdims=True))

            qb = _c(_rot(q_ref[b, :, cols], cos_t, sin_t))
            kb = _c(_rot(k_ref[b, :, cols], cos_t, sin_t) * (DK ** -0.5))
            v = v_ref[b, :, cols]
            vb = _c(v)
            vsb = _c(v * sd_ref[h])
            dob = _c(do)
            docb = _c(do * cd_ref[h])
            dmat = dm_ref[h]
            dstate = dstate_ref[b, h]
            dsb = _c(dstate)
            pb = _c(_dot_nt(qb, kb) * dmat)
            dsc = _c(_dot_nt(dob, vb) * dmat)
            dq = _dot(dsc, kb) + _dot_nt(docb, rs_ref[0, b, h])
            dk = _dot_tn(dsc, qb) + _dot_nt(vsb, dsb)
            dv = _dot_tn(pb, dob) + _dot(kb, dsb) * sd_ref[h]
            dstate_ref[b, h] = gam_ref[h] * dstate + _dot_tn(qb, docb)
            dr_ref[b, :, part(0)] = _rot_t(dq, cos_t, sin_t).astype(dr_ref.dtype)
            dr_ref[b, :, part(1)] = (_rot_t(dk, cos_t, sin_t) * (DK ** -0.5)).astype(dr_ref.dtype)
            dr_ref[b, :, part(2)] = dv.astype(dr_ref.dtype)

    qkv, act, rope, dmat, dvec, hrow, rst = _ret_specs(B, lambda c: nc - 1 - c)
    wide = pl.BlockSpec((B, CHUNK, 4 * D_MODEL), lambda c: (0, nc - 1 - c, 0))
    proj3 = proj.reshape(B, S, proj.shape[1])
    dr, dgain = pl.pallas_call(
        _after(body, 14, deps),
        name="ret_bwd",
        grid=(nc,),
        in_specs=[act, act, qkv(2), qkv(3), qkv(4), qkv(5), rst, rope, rope, dmat, dvec, dvec, hrow, hrow]
        + [ANY_SPEC] * len(deps),
        out_specs=[wide, hrow],
        out_shape=[jax.ShapeDtypeStruct((B, S, 4 * D_MODEL), _MXU), jax.ShapeDtypeStruct((HEADS, 1, DK), F32)],
        scratch_shapes=[pltpu.VMEM((B, HEADS, DK, DK), F32)],
        compiler_params=_params(("arbitrary",)),
    )(dyb.reshape(B, S, D_MODEL), o_pre.reshape(B, S, D_MODEL), proj3, proj3, proj3, proj3, states, cos, sin, dmat_t,
      cd_t, sd_t, gam_t, gain3, *deps)
    return dr.reshape(T, 4 * D_MODEL), dgain


def _mid(ya, yb, proj, x2d, tgt2d, wpa, wpb, wout, g_fin):
    T = x2d.shape[0]
    tm = min(256, T)
    n_steps = T // tm
    rows = D_MODEL // (2 * N_CHIPS)

    def body(ya_ref, yb_ref, ma_ref, mb_ref, x_ref, t_ref, gf_ref, wpa_hbm, wpb_hbm, wout_hbm,
             loss_ref, dx2_ref, dya_ref, dyb_ref, dm_ref, dgf_ref, gw_hbm, w_ref, acc_ref, sem):
        i = pl.program_id(0)

        @pl.when(i == 0)
        def _():
            loads = [pltpu.make_async_copy(src, w_ref.at[k], sem.at[k]) for k, src in enumerate((wpa_hbm, wpb_hbm, wout_hbm))]
            for cp in loads:
                cp.start()
            for cp in loads:
                cp.wait()
            acc_ref[...] = jnp.zeros_like(acc_ref)
            loss_ref[...] = jnp.zeros_like(loss_ref)
            dgf_ref[...] = jnp.zeros_like(dgf_ref)

        ya_t, yb_t = ya_ref[...], yb_ref[...]
        out_a = _dot(ya_t, w_ref[0])
        out_b = _dot(yb_t, w_ref[1])
        sa = _sigmoid(ma_ref[...])
        sb = _sigmoid(mb_ref[...])
        mgb = _c(sa * out_a + sb * out_b)
        x2 = x_ref[...] + _dot(mgb, w_ref[2])
        r2 = lax.rsqrt(jnp.mean(x2 * x2, axis=-1, keepdims=True) + EPS)
        nx = x2 * r2
        gf = gf_ref[...]
        err = nx * gf - t_ref[...]
        loss_ref[...] += 0.5 * jnp.sum(jnp.mean(err * err, axis=-1, keepdims=True), axis=0, keepdims=True)
        dy = err * (1.0 / D_MODEL)
        dgf_ref[...] += jnp.sum(dy * nx, axis=0, keepdims=True)
        dyg = dy * gf
        dx2 = r2 * (dyg - nx * jnp.mean(dyg * nx, axis=-1, keepdims=True))
        dx2_ref[...] = dx2
        dx2b = _c(dx2)
        dmg = _dot_nt(dx2b, w_ref[2])
        acc_ref[2] += _dot_tn(mgb, dx2b)
        dm_ref[:, :D_MODEL] = (dmg * out_a * sa * (1.0 - sa)).astype(dm_ref.dtype)
        dm_ref[:, D_MODEL:] = (dmg * out_b * sb * (1.0 - sb)).astype(dm_ref.dtype)
        dab = _c(dmg * sa)
        dbb = _c(dmg * sb)
        dya_ref[...] = _dot_nt(dab, w_ref[0])
        dyb_ref[...] = _dot_nt(dbb, w_ref[1])
        acc_ref[0] += _dot_tn(ya_t, dab)
        acc_ref[1] += _dot_tn(yb_t, dbb)

        @pl.when(i == n_steps - 1)
        def _():
            copies = [pltpu.make_async_copy(acc_ref.at[k, pl.ds((2 * p + hf) * rows, rows), :], gw_hbm.at[p, hf, k],
                                            sem.at[(k * N_CHIPS + p) * 2 + hf])
                      for k in range(3) for p in range(N_CHIPS) for hf in range(2)]
            for cp in copies:
                cp.start()
            for cp in copies:
                cp.wait()

    tile = lambda j: pl.BlockSpec((tm, D_MODEL), lambda i: (i, j))
    one = pl.BlockSpec((1, D_MODEL), lambda i: (0, 0))
    anyspec = pl.BlockSpec(memory_space=pl.ANY)
    return pl.pallas_call(
        body,
        name="mid",
        grid=(n_steps,),
        in_specs=[tile(0), tile(0), tile(6), tile(7), tile(0), tile(0), one, anyspec, anyspec, anyspec],
        out_specs=[pl.BlockSpec((1, 1), lambda i: (0, 0)), tile(0), tile(0), tile(0),
                   pl.BlockSpec((tm, 2 * D_MODEL), lambda i: (i, 0)), one, anyspec],
        out_shape=[
            jax.ShapeDtypeStruct((1, 1), F32),
            jax.ShapeDtypeStruct((T, D_MODEL), F32),
            jax.ShapeDtypeStruct((T, D_MODEL), F32),
            jax.ShapeDtypeStruct((T, D_MODEL), F32),
            jax.ShapeDtypeStruct((T, 2 * D_MODEL), _MXU),
            jax.ShapeDtypeStruct((1, D_MODEL), F32),
            jax.ShapeDtypeStruct((N_CHIPS, 2, 3, rows, D_MODEL), F32),
        ],
        scratch_shapes=[pltpu.VMEM((3, D_MODEL, D_MODEL), _MXU), pltpu.VMEM((3, D_MODEL, D_MODEL), F32),
                        pltpu.SemaphoreType.DMA((3 * N_CHIPS * 2,))],
        compiler_params=_params(("arbitrary",)),
    )(ya, yb, proj, proj, x2d, tgt2d, g_fin, wpa, wpb, wout)


DX_TILE = 512


def _inproj_bwd_dx(dparts, w_all, x2d, dx2, g_in, first, count, prev, name, deps=()):
    T = x2d.shape[0]
    tm = min(DX_TILE, T)
    n_d = len(dparts)
    groups = [(a, k) for a, d in enumerate(dparts) for k in range(d.shape[1] // D_MODEL)]
    dg_start = jnp.zeros((1, D_MODEL), F32) if prev is None else prev[1]
    carried = () if prev is None else (prev[0],)

    def body(*refs):
        d_refs = refs[:n_d]
        x_ref, dx2_ref, g_ref, dg0_ref, w_hbm = refs[n_d:n_d + 5]
        dx_ref, dg_ref, w_ref, sem = refs[-4:]

        @pl.when(pl.program_id(0) == 0)
        def _():
            cp = pltpu.make_async_copy(w_hbm, w_ref, sem)
            cp.start()
            cp.wait()
            dg_ref[...] = dg0_ref[...]

        dh = jnp.zeros((tm, D_MODEL), F32)
        for j, (a, k) in enumerate(groups):
            dh = dh + _dot_nt(d_refs[a][:, k * D_MODEL:(k + 1) * D_MODEL],
                              w_ref[j // 2, :, (j % 2) * D_MODEL:(j % 2 + 1) * D_MODEL])
        x = x_ref[...]
        r = lax.rsqrt(jnp.mean(x * x, axis=-1, keepdims=True) + EPS)
        nx = x * r
        dg_ref[...] += jnp.sum(dh * nx, axis=0, keepdims=True)
        dhg = dh * g_ref[...]
        dx_ref[...] = dx2_ref[...] + r * (dhg - nx * jnp.mean(dhg * nx, axis=-1, keepdims=True))

    tile = pl.BlockSpec((tm, D_MODEL), lambda i: (first + i, 0))
    one = pl.BlockSpec((1, D_MODEL), lambda i: (0, 0))
    return pl.pallas_call(
        body,
        name=name,
        grid=(count,),
        in_specs=[pl.BlockSpec((tm, d.shape[1]), lambda i: (first + i, 0)) for d in dparts]
        + [tile, tile, one, one, ANY_SPEC] + [ANY_SPEC] * (len(carried) + len(deps)),
        out_specs=[tile, one],
        out_shape=[jax.ShapeDtypeStruct((T, D_MODEL), F32), jax.ShapeDtypeStruct((1, D_MODEL), F32)],
        input_output_aliases={n_d + 5: 0} if carried else {},
        scratch_shapes=[pltpu.VMEM(w_all.shape, w_all.dtype), pltpu.SemaphoreType.DMA],
        compiler_params=_params(("arbitrary",)),
    )(*dparts, x2d, dx2, g_in, dg_start, w_all, *carried, *deps)


def _inproj_bwd_dw(ht, dparts, name, deps=()):
    T = ht.shape[1]
    tn = 512
    half = D_MODEL // 2
    per_chip = 2 * D_MODEL // tn
    n_d = len(dparts)
    tiles = [(a, t) for a, d in enumerate(dparts) for t in range(d.shape[1] // tn)]
    offs = [sum(d.shape[1] // tn for d in dparts[:a]) for a in range(n_d)]

    def body(*refs):
        ht_ref = refs[0]
        d_refs = refs[1:1 + n_d]
        out_ref = refs[-1]
        t = pl.program_id(0)

        for a in range(n_d):
            lo, hi = offs[a], offs[a] + dparts[a].shape[1] // tn

            @pl.when((t >= lo) & (t < hi))
            def _(a=a):
                g = _dot(ht_ref[...], d_refs[a][...])
                out_ref[0, 0] = g[:half]
                out_ref[0, 1] = g[half:]

    def dspec(a):
        n_a = dparts[a].shape[1] // tn
        return pl.BlockSpec((T, tn), lambda t: (0, jnp.clip(t - offs[a], 0, n_a - 1)))

    return pl.pallas_call(
        body,
        name=name,
        grid=(len(tiles),),
        in_specs=[pl.BlockSpec((D_MODEL, T), lambda t: (0, 0))] + [dspec(a) for a in range(n_d)]
        + [ANY_SPEC] * len(deps),
        out_specs=pl.BlockSpec((1, 2, half, tn), lambda t: (t // per_chip, 0, 0, t % per_chip)),
        out_shape=jax.ShapeDtypeStruct((len(tiles) // per_chip, 2, half, 2 * D_MODEL), F32),
        compiler_params=_params(("parallel",)),
    )(ht, *dparts, *deps)


def _coords():
    return lax.axis_index("x"), lax.axis_index("y"), lax.axis_index("c")


def _other_chips(x, y):
    return [(1 - x, y), (x, 1 - y), (1 - x, 1 - y)]


def _chunks(rows, n):
    size = rows // n
    return [pl.ds(q * size, size) for q in range(n)]


HBM_SPEC = pl.BlockSpec(memory_space=pltpu.HBM)
SEM_SPEC = pl.BlockSpec(memory_space=pltpu.SEMAPHORE)
DATAFLOW = pltpu.SideEffectType.DATAFLOW_SIDE_EFFECTING


def _copies_start(bufs, plan, n_copies, name):
    n = len(bufs)

    def body(*refs):
        ins = refs[:n]
        send_sems, recv_sems = refs[n], refs[n + 1]
        token = refs[-1]
        for k, send, _ in plan(ins):
            if send is not None:
                src, dst, dev, pred = send
                cp = pltpu.make_async_remote_copy(src_ref=src, dst_ref=dst, send_sem=send_sems.at[k],
                                                  recv_sem=recv_sems.at[k], device_id=dev, device_id_type=MESH)
                if pred is None:
                    cp.start()
                else:
                    pl.when(pred)(cp.start)
        token[...] = jnp.zeros_like(token)

    hbm = [pltpu.with_memory_space_constraint(b, pltpu.HBM) for b in bufs]
    outs = pl.pallas_call(
        body,
        name=name,
        in_specs=[HBM_SPEC] * n,
        out_specs=(SEM_SPEC, SEM_SPEC, *([HBM_SPEC] * n), pl.BlockSpec(memory_space=pltpu.VMEM)),
        out_shape=(pltpu.SemaphoreType.DMA((n_copies,)), pltpu.SemaphoreType.DMA((n_copies,)),
                   *[pltpu.HBM(b.shape, b.dtype) for b in bufs], jax.ShapeDtypeStruct((8, 128), F32)),
        input_output_aliases={a: 2 + a for a in range(n)},
        compiler_params=pltpu.CompilerParams(has_side_effects=DATAFLOW),
    )(*hbm)
    return outs[0], outs[1], list(outs[2:2 + n]), outs[-1]


def _copies_wait(send_sems, recv_sems, bufs, after, plan, name):
    n = len(bufs)

    def body(*refs):
        ins = refs[:n]
        s_sems, r_sems = refs[n], refs[n + 1]
        for k, send, recv in plan(ins):
            if send is not None:
                src, dst, dev, pred = send
                cp = pltpu.make_async_remote_copy(src_ref=src, dst_ref=dst, send_sem=s_sems.at[k],
                                                  recv_sem=r_sems.at[k], device_id=dev, device_id_type=MESH)
                if pred is None:
                    cp.wait_send()
                else:
                    pl.when(pred)(cp.wait_send)
            if recv is not None:
                dst, pred = recv
                cp = pltpu.make_async_remote_copy(src_ref=dst, dst_ref=dst, send_sem=s_sems.at[k],
                                                  recv_sem=r_sems.at[k], device_id=_coords(), device_id_type=MESH)
                if pred is None:
                    cp.wait_recv()
                else:
                    pl.when(pred)(cp.wait_recv)

    outs = pl.pallas_call(
        body,
        name=name,
        in_specs=[HBM_SPEC] * n + [SEM_SPEC, SEM_SPEC, pl.BlockSpec(memory_space=pl.ANY)],
        out_specs=[HBM_SPEC] * n,
        out_shape=[pltpu.HBM(b.shape, b.dtype) for b in bufs],
        input_output_aliases={a: a for a in range(n)},
        compiler_params=pltpu.CompilerParams(has_side_effects=DATAFLOW),
    )(*bufs, send_sems, recv_sems, after)
    return list(outs)


def _gather_plan(n_bufs):
    def plan(refs):
        x, y, c = _coords()
        me = 2 * x + y
        out = []
        for k, (px, py) in enumerate(_other_chips(x, y)):
            for a in range(n_bufs):
                out.append((k * n_bufs + a, (refs[a].at[me], refs[a].at[me], (px, py, c), None),
                            (refs[a].at[2 * px + py], None)))
        return out
    return plan


def _cast_into_slot(ws, name):
    n = len(ws)
    nt = 2

    def body(s_ref, *refs):
        for a in range(n):
            refs[n + a][0] = refs[a][...].astype(refs[n + a].dtype)

    xi, yi, _ = _coords()
    return pl.pallas_call(
        body,
        name=name,
        grid_spec=pltpu.PrefetchScalarGridSpec(
            num_scalar_prefetch=1,
            grid=(2, nt),
            in_specs=[pl.BlockSpec((1, w.shape[1] // nt, w.shape[2]), lambda hf, i, s: (hf, i, 0)) for w in ws],
            out_specs=[pl.BlockSpec((1, 1, w.shape[1] // nt, w.shape[2]), lambda hf, i, s: (s[0], hf, i, 0)) for w in ws],
        ),
        out_shape=[jax.ShapeDtypeStruct((N_CHIPS,) + w.shape, _MXU) for w in ws],
        compiler_params=_params(("parallel", "parallel")),
    )((2 * xi + yi).reshape(1).astype(jnp.int32), *ws)


def _gather_chips(bufs, n_chunks, name):
    n = len(bufs)
    pieces = [(a, rows) for a in range(n) for rows in _chunks(bufs[a].shape[2], n_chunks[a])]
    n_p = len(pieces)

    def body(*refs):
        outs = refs[n:2 * n]
        send_sems, recv_sems, fsend_sems, frecv_sems = refs[2 * n:]
        x, y, c = _coords()
        me = 2 * x + y
        near = [(1 - x, y), (x, 1 - y)]
        slots = [2 * (1 - x) + y, 2 * x + (1 - y), 2 * (1 - x) + (1 - y)]
        pass_to = (jnp.where(c == 0, x, 1 - x), jnp.where(c == 0, 1 - y, y))
        pass_slot = jnp.where(c == 0, slots[0], slots[1])

        def send(k, i, slot, chip):
            a, rows = pieces[i]
            return pltpu.make_async_remote_copy(
                src_ref=outs[a].at[slot, c, rows], dst_ref=outs[a].at[slot, c, rows], send_sem=send_sems.at[k * n_p + i],
                recv_sem=recv_sems.at[k * n_p + i], device_id=(*chip, c), device_id_type=MESH)

        def forward(k, i, half):
            a, rows = pieces[i]
            return pltpu.make_async_remote_copy(
                src_ref=outs[a].at[slots[k], half, rows], dst_ref=outs[a].at[slots[k], half, rows],
                send_sem=fsend_sems.at[k * n_p + i], recv_sem=frecv_sems.at[k * n_p + i],
                device_id=(x, y, 1 - c), device_id_type=MESH)

        started = [send(k, i, me, chip) for i in range(n_p) for k, chip in enumerate(near)]
        for cp in started:
            cp.start()
        for i in range(n_p):
            for k, chip in enumerate(near):
                send(k, i, slots[k], chip).wait_recv()
            later = [send(2, i, pass_slot, pass_to), forward(0, i, c), forward(1, i, c)]
            for cp in later:
                cp.start()
            started += later
        for i in range(n_p):
            send(2, i, slots[2], pass_to).wait_recv()
            fw = forward(2, i, c)
            fw.start()
            started.append(fw)
        for i in range(n_p):
            for k in range(3):
                forward(k, i, 1 - c).wait_recv()
        for cp in started:
            cp.wait_send()

    anyspec = pl.BlockSpec(memory_space=pl.ANY)
    sems = pltpu.SemaphoreType.DMA((3 * n_p,))
    return pl.pallas_call(
        body,
        name=name,
        in_specs=[anyspec] * n,
        out_specs=[anyspec] * n,
        out_shape=[jax.ShapeDtypeStruct(b.shape, b.dtype) for b in bufs],
        input_output_aliases={a: a for a in range(n)},
        scratch_shapes=[sems, sems, sems, sems],
    )(*bufs)


def _swap_plan(n_slabs):
    def plan(refs):
        x, y, c = _coords()
        out, k = [], 0
        for i, n in enumerate(n_slabs):
            g, land = refs[2 * i], refs[2 * i + 1]
            for p in range(n):
                out.append((k, (g.at[p, 1 - c], land.at[p], (x, y, 1 - c), None), (land.at[p], None)))
                k += 1
        return out
    return plan


def _is_one_of(chip, dests):
    hit = chip == dests[0]
    for d in dests[1:]:
        hit = hit | (chip == d)
    return hit


def _slab_of(chip, dests):
    return sum(j * (chip == d).astype(jnp.int32) for j, d in enumerate(dests))


def _scatter_plan(dest_sets):
    def plan(refs):
        x, y, c = _coords()
        me = 2 * x + y
        out = []
        for k, (px, py) in enumerate(_other_chips(x, y)):
            peer = 2 * px + py
            for i, dests in enumerate(dest_sets):
                cs, land = refs[2 * i], refs[2 * i + 1]
                everyone = len(dests) == N_CHIPS
                send = (cs.at[_slab_of(peer, dests)], land.at[k], (px, py, c),
                        None if everyone else _is_one_of(peer, dests))
                recv = (land.at[k], None if everyone else _is_one_of(me, dests))
                out.append((k * len(dest_sets) + i, send, recv))
        return out
    return plan


def _join_plans(parts):
    def plan(refs):
        out, b0, k0 = [], 0, 0
        for part_plan, n_bufs, n_copies in parts:
            out += [(k0 + k, send, recv) for k, send, recv in part_plan(refs[b0:b0 + n_bufs])]
            b0 += n_bufs
            k0 += n_copies
        return out
    return plan


def _allgather_plan():
    def plan(refs):
        x, y, c = _coords()
        (land,) = refs
        me = 4 * x + 2 * y + c
        out = []
        for r in range(1, 8):
            px = 1 - x if r & 4 else x
            py = 1 - y if r & 2 else y
            pc = 1 - c if r & 1 else c
            out.append((r - 1, (land.at[me], land.at[me], (px, py, pc), None), (land.at[4 * px + 2 * py + pc], None)))
        return out
    return plan


def _sum_gathered(land, name):
    def body(land_ref, o_ref):
        acc = land_ref[0]
        for d in range(1, 8):
            acc = acc + land_ref[d]
        o_ref[...] = acc

    return pl.pallas_call(
        body,
        name=name,
        out_shape=jax.ShapeDtypeStruct(land.shape[1:], F32),
        compiler_params=_params(),
    )(land)


def _join_halves(bufs, n_chunks, name, deps=()):
    n = len(bufs)
    pieces = [(a, rows) for a in range(n) for rows in _chunks(bufs[a].shape[1], n_chunks[a])]
    n_p = len(pieces)

    def body(*refs):
        outs = refs[-n - 2:-2]
        send_sems, recv_sems = refs[-2:]
        x, y, c = _coords()

        def copy(i, half):
            a, rows = pieces[i]
            return pltpu.make_async_remote_copy(
                src_ref=outs[a].at[half, rows], dst_ref=outs[a].at[half, rows], send_sem=send_sems.at[i],
                recv_sem=recv_sems.at[i], device_id=(x, y, 1 - c), device_id_type=MESH)

        sends = [copy(i, c) for i in range(n_p)]
        for cp in sends:
            cp.start()
        for i in range(n_p):
            copy(i, 1 - c).wait_recv()
        for cp in sends:
            cp.wait_send()

    anyspec = pl.BlockSpec(memory_space=pl.ANY)
    sems = pltpu.SemaphoreType.DMA((n_p,))
    return pl.pallas_call(
        body,
        name=name,
        in_specs=[anyspec] * (n + len(deps)),
        out_specs=[anyspec] * n,
        out_shape=[jax.ShapeDtypeStruct(b.shape, b.dtype) for b in bufs],
        input_output_aliases={a: a for a in range(n)},
        scratch_shapes=[sems, sems],
    )(*bufs, *deps)


def _row_tile(rows, cap):
    t = cap
    while rows % t:
        t //= 2
    return t


def _add_my_half(g, r, name):
    n_slabs, _, R, C = g.shape
    tr = R if n_slabs > 1 else _row_tile(R, 256)

    def body(c_ref, g_ref, r_ref, o_ref):
        o_ref[...] = (g_ref[0] + r_ref[...]).astype(o_ref.dtype)

    return pl.pallas_call(
        body,
        name=name,
        grid_spec=pltpu.PrefetchScalarGridSpec(
            num_scalar_prefetch=1,
            grid=(n_slabs, R // tr),
            in_specs=[pl.BlockSpec((1, 1, tr, C), lambda p, i, c_ref: (p, c_ref[0], i, 0)),
                      pl.BlockSpec((1, tr, C), lambda p, i, c_ref: (p, i, 0))],
            out_specs=pl.BlockSpec((1, tr, C), lambda p, i, c_ref: (p, i, 0)),
        ),
        out_shape=jax.ShapeDtypeStruct(r.shape, jnp.bfloat16),
        compiler_params=_params(("parallel", "parallel")),
    )(lax.axis_index("c").reshape(1).astype(jnp.int32), g, r)


def _sum_slabs(own, got, name, deps=()):
    _, R, C = own.shape
    tr = _row_tile(R, 256)

    def body(s_ref, own_ref, got_ref, *rest):
        rest[-1][0] = ((own_ref[0].astype(F32) + got_ref[0].astype(F32)) + got_ref[1].astype(F32)) + got_ref[2].astype(F32)

    xi, yi, ci = _coords()
    return pl.pallas_call(
        body,
        name=name,
        grid_spec=pltpu.PrefetchScalarGridSpec(
            num_scalar_prefetch=1,
            grid=(R // tr,),
            in_specs=[pl.BlockSpec((1, tr, C), lambda i, s: (s[0], i, 0)),
                      pl.BlockSpec((3, tr, C), lambda i, s: (0, i, 0))] + [ANY_SPEC] * len(deps),
            out_specs=pl.BlockSpec((1, tr, C), lambda i, s: (s[1], i, 0)),
        ),
        out_shape=jax.ShapeDtypeStruct((2, R, C), F32),
        compiler_params=_params(("parallel",)),
    )(jnp.stack([2 * xi + yi, ci]).astype(jnp.int32), own, got, *deps)


def _sum_parts(owns, got, dest_sets, name):
    n = len(owns)
    _, R, C = owns[0].shape
    tr = _row_tile(R, 256)

    def body(s_ref, *refs):
        got_ref, o_ref = refs[n], refs[-1]
        total = jnp.zeros((tr, C), F32)
        for i in range(n):
            total = total + jnp.where(s_ref[2 + 2 * i] == 1, refs[i][0].astype(F32), 0.0)
        o_ref[0] = ((total + got_ref[0].astype(F32)) + got_ref[1].astype(F32)) + got_ref[2].astype(F32)

    xi, yi, ci = _coords()
    me = 2 * xi + yi
    scalars = [ci, ci]
    for dests in dest_sets:
        scalars += [_is_one_of(me, dests).astype(jnp.int32), _slab_of(me, dests)]
    own_spec = lambda i: pl.BlockSpec((1, tr, C), lambda r, s: (s[3 + 2 * i], r, 0))
    return pl.pallas_call(
        body,
        name=name,
        grid_spec=pltpu.PrefetchScalarGridSpec(
            num_scalar_prefetch=1,
            grid=(R // tr,),
            in_specs=[own_spec(i) for i in range(n)] + [pl.BlockSpec((3, tr, C), lambda r, s: (0, r, 0))],
            out_specs=pl.BlockSpec((1, tr, C), lambda r, s: (s[0], r, 0)),
        ),
        out_shape=jax.ShapeDtypeStruct((2, R, C), F32),
        compiler_params=_params(("parallel",)),
    )(jnp.stack(scalars).astype(jnp.int32), *owns, got)


def _adamw_math(w, g, m, v):
    m = ADAM_B1 * m + (1.0 - ADAM_B1) * g
    v = ADAM_B2 * v + (1.0 - ADAM_B2) * (g * g)
    m_hat = m / (1.0 - ADAM_B1 ** ADAM_STEP)
    v_hat = v / (1.0 - ADAM_B2 ** ADAM_STEP)
    delta = -ADAM_LR * (m_hat / (jnp.sqrt(v_hat) + ADAM_EPS) + ADAM_WD * w)
    return delta, m, v


def _adamw_big(w, g, m, v, name):
    R, C = w.shape
    tr = min(128, R)

    def body(w_ref, g_ref, m_ref, v_ref, g_out, d_out, m_out, v_out):
        g = g_ref[...]
        d, mn, vn = _adamw_math(w_ref[...], g, m_ref[...], v_ref[...])
        g_out[...] = g
        d_out[...] = d
        m_out[...] = mn
        v_out[...] = vn

    spec = pl.BlockSpec((tr, C), lambda i: (i, 0))
    return pl.pallas_call(
        body,
        name=name,
        grid=(R // tr,),
        in_specs=[spec] * 4,
        out_specs=[spec] * 4,
        out_shape=[jax.ShapeDtypeStruct((R, C), F32)] * 4,
        compiler_params=_params(("parallel",)),
    )(w, g, m, v)


def _adamw_small(ws, gs, ms, vs, name):
    n = len(ws)

    def body(*refs):
        for a in range(n):
            d, mn, vn = _adamw_math(refs[a][...], refs[n + a][...], refs[2 * n + a][...], refs[3 * n + a][...])
            refs[4 * n + a][...] = d
            refs[5 * n + a][...] = mn
            refs[6 * n + a][...] = vn

    shapes = [jax.ShapeDtypeStruct(w.shape, F32) for w in ws]
    outs = pl.pallas_call(
        body,
        name=name,
        out_shape=shapes * 3,
        compiler_params=_params(),
    )(*ws, *gs, *ms, *vs)
    return outs[:n], outs[n:2 * n], outs[2 * n:]


def _to_blockdiag(w):
    per = CW // LRU_BW
    w4 = w.reshape(N_CT, per, LRU_BW, LRU_BW)
    eye = jnp.eye(per, dtype=w.dtype)
    return (w4[:, :, :, None, :] * eye[None, :, None, :, None]).reshape(N_CT, CW, CW)


def _from_blockdiag(g):
    per = CW // LRU_BW
    g5 = g.reshape(N_CT, per, LRU_BW, per, LRU_BW)
    return jnp.stack([g5[:, b, :, b, :] for b in range(per)], axis=1).reshape(LRU_BLOCKS, LRU_BW, LRU_BW)


def _local_grads(x2d, tgt2d, B, S, g_in, w_all, conv_w, conv_b, gate_x_w, gate_x_b, gate_a_w, gate_a_b, lam, gain,
                 proj_weights, g_fin, reduce, deps=()):
    wx_bd = _c(_to_blockdiag(gate_x_w))
    wa_bd = _c(_to_blockdiag(gate_a_w))
    tables = _retention_tables(S)
    gain3 = gain.reshape(HEADS, 1, DK)

    proj, ht = _inproj_fwd(x2d, g_in, w_all, deps)
    hlru, ya = _lru_fwd(proj, conv_w, conv_b, wx_bd, wa_bd, gate_x_b, gate_a_b, lam, B, S)
    o_pre, yb, states = _ret_fwd(proj, tables, gain3, B, S)
    wpa, wpb, wout = proj_weights(yb)
    loss, dx2, dya, dyb, dm, dgf, gw_proj = _mid(ya, yb, proj, x2d, tgt2d, wpa, wpb, wout, g_fin)
    g3 = _inproj_bwd_dw(ht, [dm], "inproj_bwd_dw_m")
    deps = reduce.m_ready(gw_proj, g3)
    dr, dgain = _ret_bwd(dyb, o_pre, proj, states, tables, gain3, B, S, deps)
    deps = reduce.ret_done(dr)
    g12 = _inproj_bwd_dw(ht, [dr], "inproj_bwd_dw_r", deps)
    deps = reduce.r_ready(g12)
    dxa, dga, dcw, dcb, dwx_bd, dwa_bd, dbx, dba, dlam = _lru_bwd(
        dya, proj, hlru, conv_w, conv_b, wx_bd, wa_bd, gate_x_b, gate_a_b, lam, B, S, deps)
    small = dict(conv_w=dcw, conv_b=dcb, gate_x_w=_from_blockdiag(dwx_bd), gate_x_b=dbx,
                 gate_a_w=_from_blockdiag(dwa_bd), gate_a_b=dba, lru_lambda=dlam, gn_gain=dgain.reshape(HEADS, DK),
                 norm_final=dgf)
    loss_rows = jnp.broadcast_to(loss, (SUBLANES, LANES))
    deps = reduce.lru_done(dxa, jnp.concatenate([_pack_small(small), loss_rows], axis=0))
    g0 = _inproj_bwd_dw(ht, [dxa, dga], "inproj_bwd_dw_a", deps)
    deps = reduce.a_ready(g0)
    n_tiles = x2d.shape[0] // min(DX_TILE, x2d.shape[0])
    grad_x, dgin = _inproj_bwd_dx([dxa, dga, dr, dm], w_all, x2d, dx2, g_in, 0, n_tiles, None, "inproj_bwd_dx", deps)
    return grad_x, dgin


ALL_CHIPS = (0, 1, 2, 3)


class _GradReduce:
    def __init__(self, proj_done):
        self.pending = {}
        self.proj_done = proj_done
        self.land_in = None

    def _start(self, key, parts, name):
        bufs, plans, shared = [], [], None
        for part_bufs, plan, n_copies, part_shared in parts:
            if part_shared is not None:
                shared = len(bufs) + part_shared
            plans.append((plan, len(part_bufs), n_copies))
            bufs += part_bufs
        plan = _join_plans(plans)
        send_sems, recv_sems, bufs, token = _copies_start(bufs, plan, sum(p[2] for p in plans), name + "_start")
        if shared is not None:
            self.land_in = bufs[shared]
        self.pending[key] = (send_sems, recv_sems, bufs, plan, name + "_wait", shared)
        return (token,)

    def _finish(self, key, after):
        send_sems, recv_sems, bufs, plan, name, shared = self.pending.pop(key)
        if shared is not None:
            bufs[shared] = self.land_in
        bufs = _copies_wait(send_sems, recv_sems, bufs, after, plan, name)
        if shared is not None:
            self.land_in = bufs[shared]
        return bufs

    @staticmethod
    def _swap(pieces):
        bufs = []
        for g in pieces:
            bufs += [g, lax.empty((g.shape[0],) + g.shape[2:], F32)]
        n_slabs = [g.shape[0] for g in pieces]
        return bufs, _swap_plan(n_slabs), sum(n_slabs), None

    def _scatter(self, sums, dest_sets):
        bufs = []
        for cs in sums:
            bufs += [cs, lax.empty((3,) + cs.shape[1:], cs.dtype)]
        if self.land_in is not None:
            bufs[-1] = self.land_in
        return bufs, _scatter_plan(dest_sets), 3 * len(sums), len(bufs) - 1

    @staticmethod
    def _gather8(block):
        x, y, c = _coords()
        land = lax.dynamic_update_slice(lax.empty((8,) + block.shape, F32), block[None], (4 * x + 2 * y + c, 0, 0))
        return [land], _allgather_plan(), 7, None

    def m_ready(self, gw_proj, g3):
        rows = gw_proj.shape[2] * gw_proj.shape[3]
        return self._start("m", [self._swap([gw_proj.reshape(N_CHIPS, 2, rows, D_MODEL), g3])], "swap_m")

    def ret_done(self, after):
        proj, land_p, g3, land_3 = self._finish("m", after)
        sums_m = [_add_my_half(proj, land_p, "chip_sum_proj"), _add_my_half(g3, land_3, "chip_sum_m")]
        return self._start("sm", [self._scatter(sums_m, [ALL_CHIPS, (3,)])], "scatter_m")

    def r_ready(self, g12):
        return self._start("r", [self._swap([g12])], "swap_r")

    def lru_done(self, after, packed):
        g12, land_12 = self._finish("r", after)
        sums_r = [_add_my_half(g12, land_12, "chip_sum_r")]
        return (self._start("sr", [self._scatter(sums_r, [(1, 2)])], "scatter_r")
                + self._start("small", [self._gather8(packed)], "gather_small"))

    def a_ready(self, g0):
        (token,) = self._start("a", [self._swap([g0])], "swap_a")
        csp, gotp, self.cs3, _ = self._finish("sm", token)
        half_proj = _sum_slabs(csp, gotp, "sum_w_proj")
        g0, land_0 = self._finish("a", half_proj)
        deps = self._start("sa", [self._scatter([_add_my_half(g0, land_0, "chip_sum_a")], [(0,)])], "scatter_a")
        self.proj_done(_join_halves([half_proj], [4], "join_halves_proj", deps)[0])
        return deps

    def finish(self, dgin, w_in_done):
        (token,) = self._start("n", [self._gather8(dgin)], "gather_norm_in")
        (small,) = self._finish("small", token)
        cs12, _ = self._finish("sr", token)
        cs0, _ = self._finish("sa", token)
        half_in = _sum_parts([self.cs3, cs12, cs0], self.land_in, [(3,), (1, 2), (0,)], "sum_w_in")
        after = w_in_done(_join_halves([half_in], [8], "join_halves")[0])
        (norm_in,) = self._finish("n", after)
        return _sum_gathered(small, "sum_small_grads"), _sum_gathered(norm_in, "sum_norm_in_grad")


_SMALL = ("gate_x_w", "gate_a_w", "conv_w", "conv_b", "gate_x_b", "gate_a_b", "lru_lambda", "gn_gain", "norm_final")
_SMALL_SHAPES = dict(gate_x_w=(LRU_BLOCKS, LRU_BW, LRU_BW), gate_a_w=(LRU_BLOCKS, LRU_BW, LRU_BW),
                     norm_in=(1, D_MODEL), conv_w=(CONV, D_MODEL), conv_b=(1, D_MODEL), gate_x_b=(1, D_MODEL),
                     gate_a_b=(1, D_MODEL), lru_lambda=(1, D_MODEL), gn_gain=(HEADS, DK), norm_final=(1, D_MODEL))


def _pack_small(small):
    return jnp.concatenate([small[k].reshape(-1, 128) for k in _SMALL], axis=0)


def _unpack_small(packed):
    out, r = {}, 0
    for k in _SMALL:
        shape = _SMALL_SHAPES[k]
        rows = 1
        for s in shape:
            rows *= s
        rows //= 128
        out[k] = packed[r:r + rows].reshape(shape)
        r += rows
    return out


def kernel(x, norm_in, w_in, conv_w, conv_b, gate_x_w, gate_x_b, gate_a_w, gate_a_b, lru_lambda, gn_gain, w_proj_a, w_proj_b, w_out, norm_final, loss_target, m_norm_in, m_w_in, m_conv_w, m_conv_b, m_gate_x_w, m_gate_x_b, m_gate_a_w, m_gate_a_b, m_lru_lambda, m_gn_gain, m_w_proj_a, m_w_proj_b, m_w_out, m_norm_final, v_norm_in, v_w_in, v_conv_w, v_conv_b, v_gate_x_w, v_gate_x_b, v_gate_a_w, v_gate_a_b, v_lru_lambda, v_gn_gain, v_w_proj_a, v_w_proj_b, v_w_out, v_norm_final):
    B, S, _ = x.shape
    T = B * S
    xi, yi, ci = _coords()
    chip = 2 * xi + yi

    cshard = D_MODEL // N_CHIPS
    mine = _cast_into_slot([w_in[0].reshape(2, D_MODEL // 2, 2 * D_MODEL)]
                           + [w[0].reshape(2, cshard // 2, D_MODEL) for w in (w_proj_a, w_proj_b, w_out)],
                           "cast_weights")
    plan = _gather_plan(3)
    s_sems, r_sems, pbufs, token = _copies_start(mine[1:], plan, 9, "gather_proj_start")
    gshard = DK // N_CHIPS
    tiny = jnp.concatenate([conv_w[0], jnp.zeros((4, cshard), F32), jnp.pad(gn_gain[0], ((0, 4), (0, cshard - gshard)))],
                           axis=0).reshape(1, 2, SUBLANES, cshard)
    tiny_buf = lax.dynamic_update_slice(lax.empty((N_CHIPS, 2, SUBLANES, cshard), F32), tiny, (chip, 0, 0, 0))
    w_buf, tiny_buf = _gather_chips([mine[0], tiny_buf], [8, 1], "gather_weights")
    w_all = w_buf.reshape(N_CHIPS, D_MODEL, 2 * D_MODEL)
    tiny_all = tiny_buf.reshape(N_CHIPS, 2 * SUBLANES, cshard)

    def proj_weights(after):
        got = _copies_wait(s_sems, r_sems, pbufs, after, plan, "gather_proj_wait")
        return [b.reshape(D_MODEL, D_MODEL) for b in got]

    conv_w_full = jnp.transpose(tiny_all[:, 0:CONV, :], (1, 0, 2)).reshape(CONV, D_MODEL)
    gain_full = jnp.transpose(tiny_all[:, 8:8 + HEADS, :gshard], (1, 0, 2)).reshape(HEADS, DK)

    weights = dict(norm_in=norm_in, w_in=w_in, conv_w=conv_w, conv_b=conv_b, gate_x_w=gate_x_w, gate_x_b=gate_x_b,
                   gate_a_w=gate_a_w, gate_a_b=gate_a_b, lru_lambda=lru_lambda, gn_gain=gn_gain, w_proj_a=w_proj_a,
                   w_proj_b=w_proj_b, w_out=w_out, norm_final=norm_final)
    ms = dict(norm_in=m_norm_in, w_in=m_w_in, conv_w=m_conv_w, conv_b=m_conv_b, gate_x_w=m_gate_x_w,
              gate_x_b=m_gate_x_b, gate_a_w=m_gate_a_w, gate_a_b=m_gate_a_b, lru_lambda=m_lru_lambda, gn_gain=m_gn_gain,
              w_proj_a=m_w_proj_a, w_proj_b=m_w_proj_b, w_out=m_w_out, norm_final=m_norm_final)
    vs = dict(norm_in=v_norm_in, w_in=v_w_in, conv_w=v_conv_w, conv_b=v_conv_b, gate_x_w=v_gate_x_w,
              gate_x_b=v_gate_x_b, gate_a_w=v_gate_a_w, gate_a_b=v_gate_a_b, lru_lambda=v_lru_lambda, gn_gain=v_gn_gain,
              w_proj_a=v_w_proj_a, w_proj_b=v_w_proj_b, w_out=v_w_out, norm_final=v_norm_final)
    names = list(weights)
    grads, delta, new_m, new_v = {}, {}, {}, {}

    def update_big(k, g):
        shp = weights[k].shape
        two = lambda a: a.reshape(shp[1], shp[2])
        g, d, mn, vn = _adamw_big(two(weights[k]), g, two(ms[k]), two(vs[k]), "adamw_" + k)
        grads[k], delta[k], new_m[k], new_v[k] = g.reshape(shp), d.reshape(shp), mn.reshape(shp), vn.reshape(shp)
        return d

    def proj_done(g_proj):
        g_pr = g_proj.reshape(2, 3, D_MODEL // (2 * N_CHIPS), D_MODEL)
        for i, k in enumerate(("w_proj_a", "w_proj_b", "w_out")):
            last = update_big(k, g_pr[:, i].reshape(cshard, D_MODEL))
        return last

    reduce = _GradReduce(proj_done)
    grad_x, dgin = _local_grads(
        x.reshape(T, D_MODEL), loss_target.reshape(T, D_MODEL), B, S, norm_in, w_all, conv_w_full, conv_b,
        gate_x_w[0], gate_x_b, gate_a_w[0], gate_a_b, lru_lambda, gain_full, proj_weights,
        norm_final.reshape(1, D_MODEL), reduce, deps=(token,))

    small_sum, g_norm_in = reduce.finish(dgin.reshape(SUBLANES, LANES),
                                         lambda g: update_big("w_in", g.reshape(D_MODEL, 2 * D_MODEL)))
    loss = small_sum[small_sum.shape[0] - SUBLANES, 0]

    gsm = _unpack_small(small_sum)
    gsm["norm_in"] = g_norm_in
    gsm["conv_w"] = lax.dynamic_slice_in_dim(gsm["conv_w"], chip * cshard, cshard, axis=1)
    gsm["gn_gain"] = lax.dynamic_slice_in_dim(gsm["gn_gain"], chip * gshard, gshard, axis=1)
    smalls = [k for k in names if k not in delta]

    def view(a):
        return a.reshape(1, -1) if a.ndim == 1 else (a.reshape(a.shape[1:]) if a.ndim > 2 else a)

    ds, mns, vns = _adamw_small([view(weights[k]) for k in smalls], [gsm[k].reshape(view(weights[k]).shape) for k in smalls],
                                [view(ms[k]) for k in smalls], [view(vs[k]) for k in smalls], "adamw_small")
    for k, d, mn, vn in zip(smalls, ds, mns, vns):
        shp = weights[k].shape
        grads[k], delta[k], new_m[k], new_v[k] = gsm[k].reshape(shp), d.reshape(shp), mn.reshape(shp), vn.reshape(shp)

    return (loss, grad_x.reshape(B, S, D_MODEL), *[grads[k] for k in names], *[delta[k] for k in names],
            *[new_m[k] for k in names], *[new_v[k] for k in names])
```

```python
import jax
import jax.numpy as jnp
from jax import lax
from jax.experimental import pallas as pl
from jax.experimental.pallas import tpu as pltpu

F32 = jnp.float32
_MXU = jnp.bfloat16

D_MODEL = 1024
N_GROUPS = 8
HEADS = 4
DK = 256
CHUNK = 128
CONV = 4
LRU_BLOCKS = 16
LRU_BW = 64
LRU_C = 8.0
ROPE_THETA = 10000.0
EPS = 1e-6
CW = 256
N_CT = D_MODEL // CW
N_CHIPS = 4
MESH = pl.DeviceIdType.MESH

ADAM_LR = 0.001
ADAM_B1 = 0.9
ADAM_B2 = 0.999
ADAM_EPS = 1e-08
ADAM_WD = 0.01
ADAM_STEP = 10

VMEM_LIMIT = 56 * 1024 * 1024


def _c(v):
    return v.astype(_MXU)


def _dot(a, b):
    return lax.dot_general(a, b, (((1,), (0,)), ((), ())), preferred_element_type=F32)


def _dot_nt(a, b):
    return lax.dot_general(a, b, (((1,), (1,)), ((), ())), preferred_element_type=F32)


def _dot_tn(a, b):
    return lax.dot_general(a, b, (((0,), (0,)), ((), ())), preferred_element_type=F32)


def _sigmoid(z):
    return 0.5 * jnp.tanh(0.5 * z) + 0.5


ANY_SPEC = pl.BlockSpec(memory_space=pl.ANY)


def _after(body, n_in, deps):
    n_deps = len(deps)

    def wrapped(*refs):
        return body(*refs[:n_in], *refs[n_in + n_deps:])

    return wrapped


def _params(sem=None):
    if sem is None:
        return pltpu.CompilerParams(vmem_limit_bytes=VMEM_LIMIT)
    return pltpu.CompilerParams(vmem_limit_bytes=VMEM_LIMIT, dimension_semantics=sem)


def _inproj_fwd(x2d, g_in, w_all, deps=()):
    T = x2d.shape[0]
    tm = min(1024, T)
    n_i = T // tm

    def body(*refs):
        x_ref, g_ref, w_ref = refs[:3]
        proj_ref, ht_ref, h_all = refs[-3:]
        i = pl.program_id(1)
        rows = pl.ds(pl.multiple_of(i * tm, tm), tm)

        @pl.when(pl.program_id(0) == 0)
        def _():
            x = x_ref[...]
            r = lax.rsqrt(jnp.mean(x * x, axis=-1, keepdims=True) + EPS)
            h = x * r * g_ref[...]
            h_all[rows, :] = h.astype(h_all.dtype)
            ht_ref[...] = h.T.astype(ht_ref.dtype)

        proj_ref[...] = _dot(h_all[rows, :], w_ref[0])

    first = lambda j, i: jnp.where(j == 0, i, n_i - 1)
    return pl.pallas_call(
        body,
        name="inproj_fwd",
        grid=(N_GROUPS, n_i),
        in_specs=[
            pl.BlockSpec((tm, D_MODEL), lambda j, i: (first(j, i), 0)),
            pl.BlockSpec((1, D_MODEL), lambda j, i: (0, 0)),
            pl.BlockSpec((1, D_MODEL, D_MODEL), lambda j, i: (j // 2, 0, j % 2)),
        ] + [pl.BlockSpec(memory_space=pl.ANY)] * len(deps),
        out_specs=[
            pl.BlockSpec((tm, D_MODEL), lambda j, i: (i, j)),
            pl.BlockSpec((D_MODEL, tm), lambda j, i: (0, first(j, i))),
        ],
        out_shape=[
            jax.ShapeDtypeStruct((T, N_GROUPS * D_MODEL), F32),
            jax.ShapeDtypeStruct((D_MODEL, T), _MXU),
        ],
        scratch_shapes=[pltpu.VMEM((T, D_MODEL), _MXU)],
        compiler_params=_params(("arbitrary", "arbitrary")),
    )(x2d, g_in, w_all, *deps)


def _scan_fwd(a, u):
    n = a.shape[0]
    row = lax.broadcasted_iota(jnp.int32, a.shape, 0)
    s = 1
    while s < n:
        m = row >= s
        u = u + a * jnp.where(m, pltpu.roll(u, s, 0), 0.0)
        a = a * jnp.where(m, pltpu.roll(a, s, 0), 1.0)
        s *= 2
    return a, u


def _scan_bwd(b, g):
    n = b.shape[0]
    row = lax.broadcasted_iota(jnp.int32, b.shape, 0)
    s = 1
    while s < n:
        m = row < n - s
        g = g + b * jnp.where(m, pltpu.roll(g, n - s, 0), 0.0)
        b = b * jnp.where(m, pltpu.roll(b, n - s, 0), 1.0)
        s *= 2
    return b, g


LANES = 128
SUBLANES = 8


def _scan_scratch(tc):
    by_lanes = pltpu.VMEM((CW // LANES, tc, LANES), F32)
    return [by_lanes, by_lanes, pltpu.VMEM((tc // SUBLANES, CW), F32), pltpu.VMEM((tc, CW), F32)]


def _scan_tile(a, u, edge, la_ref, lh_ref, c_ref, dst_ref, reverse):
    n, w = a.shape
    groups = n // SUBLANES
    a3 = a.reshape(groups, SUBLANES, w)
    u3 = u.reshape(groups, SUBLANES, w)
    row = lax.broadcasted_iota(jnp.int32, a3.shape, 1)
    for s in (1, 2, 4):
        m = (row < SUBLANES - s) if reverse else (row >= s)
        shift = SUBLANES - s if reverse else s
        u3 = u3 + a3 * jnp.where(m, pltpu.roll(u3, shift, 1), 0.0)
        a3 = a3 * jnp.where(m, pltpu.roll(a3, shift, 1), 1.0)
    al = a3.reshape(n, w)
    hl = u3.reshape(n, w)
    blocks = w // LANES
    for q in range(blocks):
        la_ref[q] = al[:, q * LANES:(q + 1) * LANES]
        lh_ref[q] = hl[:, q * LANES:(q + 1) * LANES]
    ends = pl.ds(0 if reverse else SUBLANES - 1, groups, stride=SUBLANES)
    end_a = jnp.concatenate([la_ref.at[q][ends, :] for q in range(blocks)], axis=-1)
    end_h = jnp.concatenate([lh_ref.at[q][ends, :] for q in range(blocks)], axis=-1)
    prod, part = (_scan_bwd if reverse else _scan_fwd)(end_a, end_h)
    total = part + prod * edge
    g_row = lax.broadcasted_iota(jnp.int32, total.shape, 0)
    if reverse:
        c_ref[...] = jnp.where(g_row == groups - 1, edge, pltpu.roll(total, groups - 1, 0))
    else:
        c_ref[...] = jnp.where(g_row == 0, edge, pltpu.roll(total, 1, 0))
    for g in range(groups):
        rows = slice(g * SUBLANES, (g + 1) * SUBLANES)
        for q in range(blocks):
            cols = slice(q * LANES, (q + 1) * LANES)
            dst_ref[rows, cols] = lh_ref[q, rows, :] + la_ref[q, rows, :] * c_ref[g:g + 1, cols]


def _softplus_neg(lam):
    z = -lam
    return jnp.maximum(z, 0.0) + jnp.log1p(jnp.exp(-jnp.abs(z)))


def _lru_gates(xc, wx_ref, wa_ref, bx_ref, ba_ref, lam_ref):
    xcb = _c(xc)
    i_t = _sigmoid(_dot(xcb, wx_ref[0]) + bx_ref[...])
    r_t = _sigmoid(_dot(xcb, wa_ref[0]) + ba_ref[...])
    sp = _softplus_neg(lam_ref[...])
    log_a = (-LRU_C) * r_t * sp
    a = jnp.exp(log_a)
    mult = jnp.sqrt(1.0 - a * a)
    return xcb, i_t, r_t, sp, a, mult


def _conv_from_ext(ext_ref, xa, cw_ref, cb_ref, tc):
    return (cb_ref[...] + cw_ref[3:4, :] * xa + cw_ref[2:3, :] * ext_ref[7:7 + tc, :]
            + cw_ref[1:2, :] * ext_ref[6:6 + tc, :] + cw_ref[0:1, :] * ext_ref[5:5 + tc, :])


def _lru_fwd(proj, conv_w, conv_b, wx_bd, wa_bd, bx, ba, lam, B, S):
    T = B * S
    tc = min(256, S)
    nt = S // tc
    h8 = tc // 8

    def body(xa_ref, halo_ref, ga_ref, cw_ref, cb_ref, wx_ref, wa_ref, bx_ref, ba_ref, lam_ref,
             h_ref, ya_ref, ext_ref, carry_ref, la_ref, lh_ref, c_ref):
        t = pl.program_id(2)

        @pl.when(t == 0)
        def _():
            carry_ref[...] = jnp.zeros_like(carry_ref)

        xa = xa_ref[...]
        ext_ref[0:8, :] = jnp.where(t == 0, 0.0, halo_ref[...])
        ext_ref[8:8 + tc, :] = xa
        xc = _conv_from_ext(ext_ref, xa, cw_ref, cb_ref, tc)
        _, i_t, _, _, a, mult = _lru_gates(xc, wx_ref, wa_ref, bx_ref, ba_ref, lam_ref)
        u = mult * (i_t * xc)
        _scan_tile(a, u, carry_ref[7:8, :], la_ref, lh_ref, c_ref, h_ref, False)
        h = h_ref[...]
        carry_ref[...] = h[tc - 8:tc, :]
        ga = ga_ref[...]
        ya_ref[...] = (ga * _sigmoid(ga) * h).astype(ya_ref.dtype)

    row = lambda b, t: b * nt + t
    vec = pl.BlockSpec((1, CW), lambda b, c, t: (0, c))
    mat = pl.BlockSpec((1, CW, CW), lambda b, c, t: (c, 0, 0))
    return pl.pallas_call(
        body,
        name="lru_fwd",
        grid=(B, N_CT, nt),
        in_specs=[
            pl.BlockSpec((tc, CW), lambda b, c, t: (row(b, t), c)),
            pl.BlockSpec((8, CW), lambda b, c, t: (jnp.maximum(row(b, t) * h8 - 1, 0), c)),
            pl.BlockSpec((tc, CW), lambda b, c, t: (row(b, t), N_CT + c)),
            pl.BlockSpec((CONV, CW), lambda b, c, t: (0, c)),
            vec, mat, mat, vec, vec, vec,
        ],
        out_specs=[
            pl.BlockSpec((tc, CW), lambda b, c, t: (row(b, t), c)),
            pl.BlockSpec((tc, CW), lambda b, c, t: (row(b, t), c)),
        ],
        out_shape=[
            jax.ShapeDtypeStruct((T, D_MODEL), F32),
            jax.ShapeDtypeStruct((T, D_MODEL), _MXU),
        ],
        scratch_shapes=[pltpu.VMEM((tc + 8, CW), F32), pltpu.VMEM((8, CW), F32)] + _scan_scratch(tc)[:3],
        compiler_params=_params(("parallel", "parallel", "arbitrary")),
    )(proj, proj, proj, conv_w, conv_b, wx_bd, wa_bd, bx, ba, lam)


def _lru_bwd(dya, proj, hlru, conv_w, conv_b, wx_bd, wa_bd, bx, ba, lam, B, S, deps=()):
    T = B * S
    tc = min(256, S)
    nt = S // tc
    h8 = tc // 8

    def body(dya_ref, xa_ref, xhalo_ref, ga_ref, h_ref, hhalo_ref, cw_ref, cb_ref, wx_ref, wa_ref, bx_ref, ba_ref,
             lam_ref, dxa_ref, dga_ref, dcw_ref, dcb_ref, dwx_ref, dwa_ref, dbx_ref, dba_ref, dlam_ref,
             ext_ref, ext2_ref, carry_ref, dhalo_ref, la_ref, lh_ref, c_ref, dh_ref):
        b = pl.program_id(1)
        t = pl.program_id(2)
        tt = nt - 1 - t

        @pl.when(t == 0)
        def _():
            carry_ref[...] = jnp.zeros_like(carry_ref)
            dhalo_ref[...] = jnp.zeros_like(dhalo_ref)

        @pl.when((t == 0) & (b == 0))
        def _():
            for r in (dcw_ref, dcb_ref, dwx_ref, dwa_ref, dbx_ref, dba_ref, dlam_ref):
                r[...] = jnp.zeros_like(r)

        xa = xa_ref[...]
        ext_ref[0:8, :] = jnp.where(tt == 0, 0.0, xhalo_ref[...])
        ext_ref[8:8 + tc, :] = xa
        xc = _conv_from_ext(ext_ref, xa, cw_ref, cb_ref, tc)
        xcb, i_t, r_t, sp, a, mult = _lru_gates(xc, wx_ref, wa_ref, bx_ref, ba_ref, lam_ref)

        h = h_ref[...]
        ga = ga_ref[...]
        dya_t = dya_ref[...]
        sg = _sigmoid(ga)
        dga_ref[...] = (dya_t * h * (sg * (1.0 + ga * (1.0 - sg)))).astype(dga_ref.dtype)
        dlru = dya_t * (ga * sg)

        row = lax.broadcasted_iota(jnp.int32, a.shape, 0)
        coef = jnp.where(row == tc - 1, 1.0, pltpu.roll(a, tc - 1, 0))
        _scan_tile(coef, dlru, carry_ref[0:1, :], la_ref, lh_ref, c_ref, dh_ref, True)
        dh = dh_ref[...]
        ext2_ref[0:tc, :] = a * dh
        carry_ref[...] = ext2_ref[0:8, :]

        ext2_ref[0:8, :] = jnp.where(tt == 0, 0.0, hhalo_ref[...])
        ext2_ref[8:8 + tc, :] = h
        hprev = ext2_ref[7:7 + tc, :]

        da = dh * hprev
        ix = i_t * xc
        dmult = dh * ix
        di = dh * mult * xc
        dxc = dh * mult * i_t
        dlog_a = da * a - dmult * (a * a) / mult
        dr = dlog_a * ((-LRU_C) * sp)
        dlam_ref[...] += jnp.sum(dlog_a * r_t, axis=0, keepdims=True) * (LRU_C * _sigmoid(-lam_ref[...]))
        dza = dr * r_t * (1.0 - r_t)
        dzx = di * i_t * (1.0 - i_t)
        dzab = _c(dza)
        dzxb = _c(dzx)
        dxc = dxc + _dot_nt(dzxb, wx_ref[0]) + _dot_nt(dzab, wa_ref[0])
        dwx_ref[0] += _dot_tn(xcb, dzxb)
        dwa_ref[0] += _dot_tn(xcb, dzab)
        dbx_ref[...] += jnp.sum(dzx, axis=0, keepdims=True)
        dba_ref[...] += jnp.sum(dza, axis=0, keepdims=True)

        dcb_ref[...] += jnp.sum(dxc, axis=0, keepdims=True)
        dcw_ref[3:4, :] += jnp.sum(dxc * xa, axis=0, keepdims=True)
        dcw_ref[2:3, :] += jnp.sum(dxc * ext_ref[7:7 + tc, :], axis=0, keepdims=True)
        dcw_ref[1:2, :] += jnp.sum(dxc * ext_ref[6:6 + tc, :], axis=0, keepdims=True)
        dcw_ref[0:1, :] += jnp.sum(dxc * ext_ref[5:5 + tc, :], axis=0, keepdims=True)
        ext2_ref[0:tc, :] = dxc
        ext2_ref[tc:tc + 8, :] = dhalo_ref[...]
        dxa = (cw_ref[3:4, :] * dxc + cw_ref[2:3, :] * ext2_ref[1:1 + tc, :]
               + cw_ref[1:2, :] * ext2_ref[2:2 + tc, :] + cw_ref[0:1, :] * ext2_ref[3:3 + tc, :])
        dxa_ref[...] = dxa.astype(dxa_ref.dtype)
        dhalo_ref[...] = ext2_ref[0:8, :]

    row_of = lambda b, t: b * nt + (nt - 1 - t)
    tile = lambda off: pl.BlockSpec((tc, CW), lambda c, b, t: (row_of(b, t), off + c))
    halo = pl.BlockSpec((8, CW), lambda c, b, t: (jnp.maximum(row_of(b, t) * h8 - 1, 0), c))
    vec = pl.BlockSpec((1, CW), lambda c, b, t: (0, c))
    mat = pl.BlockSpec((1, CW, CW), lambda c, b, t: (c, 0, 0))
    cwspec = pl.BlockSpec((CONV, CW), lambda c, b, t: (0, c))
    return pl.pallas_call(
        _after(body, 13, deps),
        name="lru_bwd",
        grid=(N_CT, B, nt),
        in_specs=[tile(0), tile(0), halo, tile(N_CT), tile(0), halo, cwspec, vec, mat, mat, vec, vec, vec]
        + [ANY_SPEC] * len(deps),
        out_specs=[tile(0), tile(0), cwspec, vec, mat, mat, vec, vec, vec],
        out_shape=[
            jax.ShapeDtypeStruct((T, D_MODEL), _MXU),
            jax.ShapeDtypeStruct((T, D_MODEL), _MXU),
            jax.ShapeDtypeStruct((CONV, D_MODEL), F32),
            jax.ShapeDtypeStruct((1, D_MODEL), F32),
            jax.ShapeDtypeStruct((N_CT, CW, CW), F32),
            jax.ShapeDtypeStruct((N_CT, CW, CW), F32),
            jax.ShapeDtypeStruct((1, D_MODEL), F32),
            jax.ShapeDtypeStruct((1, D_MODEL), F32),
            jax.ShapeDtypeStruct((1, D_MODEL), F32),
        ],
        scratch_shapes=[pltpu.VMEM((tc + 8, CW), F32), pltpu.VMEM((tc + 8, CW), F32),
                        pltpu.VMEM((8, CW), F32), pltpu.VMEM((8, CW), F32)] + _scan_scratch(tc),
        compiler_params=_params(("parallel", "arbitrary", "arbitrary")),
    )(dya, proj, proj, proj, hlru, hlru, conv_w, conv_b, wx_bd, wa_bd, bx, ba, lam, *deps)


def _retention_tables(S):
    half = DK // 2
    freqs = ROPE_THETA ** (-jnp.arange(half, dtype=F32) / half)
    ang = jnp.arange(S, dtype=F32)[:, None] * freqs[None, :]
    log_g = jnp.log1p(-(2.0 ** (-5.0 - jnp.arange(HEADS, dtype=F32))))
    idx = jnp.arange(CHUNK, dtype=F32)
    diff = idx[:, None] - idx[None, :]
    inner = jnp.where(diff >= 0, jnp.exp(jnp.maximum(diff, 0.0)[None] * log_g[:, None, None]), 0.0)
    cross = jnp.exp((idx[None, :] + 1.0) * log_g[:, None])[:, :, None]
    state = jnp.exp((CHUNK - 1.0 - idx[None, :]) * log_g[:, None])[:, :, None]
    gam = jnp.broadcast_to(jnp.exp(CHUNK * log_g)[:, None, None], (HEADS, 1, DK))
    return jnp.cos(ang), jnp.sin(ang), inner, cross, state, gam


def _rot(x, cos, sin):
    half = DK // 2
    x1, x2 = x[:, :half], x[:, half:]
    return jnp.concatenate([x1 * cos - x2 * sin, x1 * sin + x2 * cos], axis=-1)


def _rot_t(y, cos, sin):
    half = DK // 2
    y1, y2 = y[:, :half], y[:, half:]
    return jnp.concatenate([y1 * cos + y2 * sin, y2 * cos - y1 * sin], axis=-1)


def _groupnorm(o):
    mu = jnp.mean(o, axis=-1, keepdims=True)
    oc = o - mu
    rs = lax.rsqrt(jnp.mean(oc * oc, axis=-1, keepdims=True) + EPS)
    return oc * rs, rs


def _ret_specs(B, chunk_of):
    qkv = lambda g: pl.BlockSpec((B, CHUNK, D_MODEL), lambda c: (0, chunk_of(c), g))
    act = pl.BlockSpec((B, CHUNK, D_MODEL), lambda c: (0, chunk_of(c), 0))
    rope = pl.BlockSpec((CHUNK, DK // 2), lambda c: (chunk_of(c), 0))
    dmat = pl.BlockSpec((HEADS, CHUNK, CHUNK), lambda c: (0, 0, 0))
    dvec = pl.BlockSpec((HEADS, CHUNK, 1), lambda c: (0, 0, 0))
    hrow = pl.BlockSpec((HEADS, 1, DK), lambda c: (0, 0, 0))
    rst = pl.BlockSpec((1, B, HEADS, DK, DK), lambda c: (chunk_of(c), 0, 0, 0, 0))
    return qkv, act, rope, dmat, dvec, hrow, rst


def _ret_fwd(proj, tables, gain3, B, S):
    T = B * S
    nc = S // CHUNK
    cos, sin, dmat_t, cd_t, sd_t, gam_t = tables

    def body(q_ref, k_ref, v_ref, gb_ref, cos_ref, sin_ref, dm_ref, cd_ref, sd_ref, gam_ref, gain_ref,
             o_ref, yb_ref, rs_ref, state_ref):
        @pl.when(pl.program_id(0) == 0)
        def _():
            state_ref[...] = jnp.zeros_like(state_ref)

        cos_t, sin_t = cos_ref[...], sin_ref[...]
        for b, h in [(b, h) for b in range(B) for h in range(HEADS)]:
            cols = slice(h * DK, (h + 1) * DK)
            qb = _c(_rot(q_ref[b, :, cols], cos_t, sin_t))
            kb = _c(_rot(k_ref[b, :, cols], cos_t, sin_t) * (DK ** -0.5))
            v = v_ref[b, :, cols]
            state = state_ref[b, h]
            sb = _c(state)
            rs_ref[0, b, h] = sb
            scores = _dot_nt(qb, kb) * dm_ref[h]
            o = _dot(_c(scores), _c(v)) + _dot(qb, sb) * cd_ref[h]
            state_ref[b, h] = gam_ref[h] * state + _dot_tn(kb, _c(v * sd_ref[h]))
            o_ref[b, :, cols] = o
            n, _ = _groupnorm(o)
            gb = gb_ref[b, :, cols]
            yb_ref[b, :, cols] = (gb * _sigmoid(gb) * (n * gain_ref[h])).astype(yb_ref.dtype)

    qkv, act, rope, dmat, dvec, hrow, rst = _ret_specs(B, lambda c: c)
    proj3 = proj.reshape(B, S, proj.shape[1])
    o_pre, yb, states = pl.pallas_call(
        body,
        name="ret_fwd",
        grid=(nc,),
        in_specs=[qkv(2), qkv(3), qkv(4), qkv(5), rope, rope, dmat, dvec, dvec, hrow, hrow],
        out_specs=[act, act, rst],
        out_shape=[
            jax.ShapeDtypeStruct((B, S, D_MODEL), F32),
            jax.ShapeDtypeStruct((B, S, D_MODEL), _MXU),
            jax.ShapeDtypeStruct((nc, B, HEADS, DK, DK), _MXU),
        ],
        scratch_shapes=[pltpu.VMEM((B, HEADS, DK, DK), F32)],
        compiler_params=_params(("arbitrary",)),
    )(proj3, proj3, proj3, proj3, cos, sin, dmat_t, cd_t, sd_t, gam_t, gain3)
    return o_pre.reshape(T, D_MODEL), yb.reshape(T, D_MODEL), states


def _ret_bwd(dyb, o_pre, proj, states, tables, gain3, B, S, deps=()):
    T = B * S
    nc = S // CHUNK
    cos, sin, dmat_t, cd_t, sd_t, gam_t = tables

    def body(dyb_ref, o_ref, q_ref, k_ref, v_ref, gb_ref, rs_ref, cos_ref, sin_ref, dm_ref, cd_ref, sd_ref, gam_ref,
             gain_ref, dr_ref, dgain_ref, dstate_ref):
        @pl.when(pl.program_id(0) == 0)
        def _():
            dstate_ref[...] = jnp.zeros_like(dstate_ref)
            dgain_ref[...] = jnp.zeros_like(dgain_ref)

        cos_t, sin_t = cos_ref[...], sin_ref[...]
        for b, h in [(b, h) for b in range(B) for h in range(HEADS)]:
            cols = slice(h * DK, (h + 1) * DK)
            gain = gain_ref[h]
            n, rs = _groupnorm(o_ref[b, :, cols])
            gb = gb_ref[b, :, cols]
            sg = _sigmoid(gb)
            dy = dyb_ref[b, :, cols]
            part = lambda g: slice(g * D_MODEL + h * DK, g * D_MODEL + (h + 1) * DK)
            dr_ref[b, :, part(3)] = (dy * (n * gain) * (sg * (1.0 + gb * (1.0 - sg)))).astype(dr_ref.dtype)
            dgn = dy * (gb * sg)
            dgain_ref[h] += jnp.sum(dgn * n, axis=0, keepdims=True)
            dn = dgn * gain
            do = rs * (dn - jnp.mean(dn, axis=-1, keepdims=True) - n * jnp.mean(dn * n, axis=-1, keepdims=True))

            qb = _c(_rot(q_ref[b, :, cols], cos_t, sin_t))
            kb = _c(_rot(k_ref[b, :, cols], cos_t, sin_t) * (DK ** -0.5))
            v = v_ref[b, :, cols]
            vb = _c(v)
            vsb = _c(v * sd_ref[h])
            dob = _c(do)
            docb = _c(do * cd_ref[h])
            dmat = dm_ref[h]
            dstate = dstate_ref[b, h]
            dsb = _c(dstate)
            pb = _c(_dot_nt(qb, kb) * dmat)
            dsc = _c(_dot_nt(dob, vb) * dmat)
            dq = _dot(dsc, kb) + _dot_nt(docb, rs_ref[0, b, h])
            dk = _dot_tn(dsc, qb) + _dot_nt(vsb, dsb)
            dv = _dot_tn(pb, dob) + _dot(kb, dsb) * sd_ref[h]
            dstate_ref[b, h] = gam_ref[h] * dstate + _dot_tn(qb, docb)
            dr_ref[b, :, part(0)] = _rot_t(dq, cos_t, sin_t).astype(dr_ref.dtype)
            dr_ref[b, :, part(1)] = (_rot_t(dk, cos_t, sin_t) * (DK ** -0.5)).astype(dr_ref.dtype)
            dr_ref[b, :, part(2)] = dv.astype(dr_ref.dtype)

    qkv, act, rope, dmat, dvec, hrow, rst = _ret_specs(B, lambda c: nc - 1 - c)
    wide = pl.BlockSpec((B, CHUNK, 4 * D_MODEL), lambda c: (0, nc - 1 - c, 0))
    proj3 = proj.reshape(B, S, proj.shape[1])
    dr, dgain = pl.pallas_call(
        _after(body, 14, deps),
        name="ret_bwd",
        grid=(nc,),
        in_specs=[act, act, qkv(2), qkv(3), qkv(4), qkv(5), rst, rope, rope, dmat, dvec, dvec, hrow, hrow]
        + [ANY_SPEC] * len(deps),
        out_specs=[wide, hrow],
        out_shape=[jax.ShapeDtypeStruct((B, S, 4 * D_MODEL), _MXU), jax.ShapeDtypeStruct((HEADS, 1, DK), F32)],
        scratch_shapes=[pltpu.VMEM((B, HEADS, DK, DK), F32)],
        compiler_params=_params(("arbitrary",)),
    )(dyb.reshape(B, S, D_MODEL), o_pre.reshape(B, S, D_MODEL), proj3, proj3, proj3, proj3, states, cos, sin, dmat_t,
      cd_t, sd_t, gam_t, gain3, *deps)
    return dr.reshape(T, 4 * D_MODEL), dgain


def _mid(ya, yb, proj, x2d, tgt2d, wpa, wpb, wout, g_fin):
    T = x2d.shape[0]
    tm = min(256, T)
    n_steps = T // tm
    rows = D_MODEL // (2 * N_CHIPS)

    def body(ya_ref, yb_ref, ma_ref, mb_ref, x_ref, t_ref, gf_ref, wpa_hbm, wpb_hbm, wout_hbm,
             loss_ref, dx2_ref, dya_ref, dyb_ref, dm_ref, dgf_ref, gw_hbm, w_ref, acc_ref, sem):
        i = pl.program_id(0)

        @pl.when(i == 0)
        def _():
            loads = [pltpu.make_async_copy(src, w_ref.at[k], sem.at[k]) for k, src in enumerate((wpa_hbm, wpb_hbm, wout_hbm))]
            for cp in loads:
                cp.start()
            for cp in loads:
                cp.wait()
            acc_ref[...] = jnp.zeros_like(acc_ref)
            loss_ref[...] = jnp.zeros_like(loss_ref)
            dgf_ref[...] = jnp.zeros_like(dgf_ref)

        ya_t, yb_t = ya_ref[...], yb_ref[...]
        out_a = _dot(ya_t, w_ref[0])
        out_b = _dot(yb_t, w_ref[1])
        sa = _sigmoid(ma_ref[...])
        sb = _sigmoid(mb_ref[...])
        mgb = _c(sa * out_a + sb * out_b)
        x2 = x_ref[...] + _dot(mgb, w_ref[2])
        r2 = lax.rsqrt(jnp.mean(x2 * x2, axis=-1, keepdims=True) + EPS)
        nx = x2 * r2
        gf = gf_ref[...]
        err = nx * gf - t_ref[...]
        loss_ref[...] += 0.5 * jnp.sum(jnp.mean(err * err, axis=-1, keepdims=True), axis=0, keepdims=True)
        dy = err * (1.0 / D_MODEL)
        dgf_ref[...] += jnp.sum(dy * nx, axis=0, keepdims=True)
        dyg = dy * gf
        dx2 = r2 * (dyg - nx * jnp.mean(dyg * nx, axis=-1, keepdims=True))
        dx2_ref[...] = dx2
        dx2b = _c(dx2)
        dmg = _dot_nt(dx2b, w_ref[2])
        acc_ref[2] += _dot_tn(mgb, dx2b)
        dm_ref[:, :D_MODEL] = (dmg * out_a * sa * (1.0 - sa)).astype(dm_ref.dtype)
        dm_ref[:, D_MODEL:] = (dmg * out_b * sb * (1.0 - sb)).astype(dm_ref.dtype)
        dab = _c(dmg * sa)
        dbb = _c(dmg * sb)
        dya_ref[...] = _dot_nt(dab, w_ref[0])
        dyb_ref[...] = _dot_nt(dbb, w_ref[1])
        acc_ref[0] += _dot_tn(ya_t, dab)
        acc_ref[1] += _dot_tn(yb_t, dbb)

        @pl.when(i == n_steps - 1)
        def _():
            copies = [pltpu.make_async_copy(acc_ref.at[k, pl.ds((2 * p + hf) * rows, rows), :], gw_hbm.at[p, hf, k],
                                            sem.at[(k * N_CHIPS + p) * 2 + hf])
                      for k in range(3) for p in range(N_CHIPS) for hf in range(2)]
            for cp in copies:
                cp.start()
            for cp in copies:
                cp.wait()

    tile = lambda j: pl.BlockSpec((tm, D_MODEL), lambda i: (i, j))
    one = pl.BlockSpec((1, D_MODEL), lambda i: (0, 0))
    anyspec = pl.BlockSpec(memory_space=pl.ANY)
    return pl.pallas_call(
        body,
        name="mid",
        grid=(n_steps,),
        in_specs=[tile(0), tile(0), tile(6), tile(7), tile(0), tile(0), one, anyspec, anyspec, anyspec],
        out_specs=[pl.BlockSpec((1, 1), lambda i: (0, 0)), tile(0), tile(0), tile(0),
                   pl.BlockSpec((tm, 2 * D_MODEL), lambda i: (i, 0)), one, anyspec],
        out_shape=[
            jax.ShapeDtypeStruct((1, 1), F32),
            jax.ShapeDtypeStruct((T, D_MODEL), F32),
            jax.ShapeDtypeStruct((T, D_MODEL), F32),
            jax.ShapeDtypeStruct((T, D_MODEL), F32),
            jax.ShapeDtypeStruct((T, 2 * D_MODEL), _MXU),
            jax.ShapeDtypeStruct((1, D_MODEL), F32),
            jax.ShapeDtypeStruct((N_CHIPS, 2, 3, rows, D_MODEL), F32),
        ],
        scratch_shapes=[pltpu.VMEM((3, D_MODEL, D_MODEL), _MXU), pltpu.VMEM((3, D_MODEL, D_MODEL), F32),
                        pltpu.SemaphoreType.DMA((3 * N_CHIPS * 2,))],
        compiler_params=_params(("arbitrary",)),
    )(ya, yb, proj, proj, x2d, tgt2d, g_fin, wpa, wpb, wout)


DX_TILE = 512


def _inproj_bwd_dx(dparts, w_all, x2d, dx2, g_in, first, count, prev, name, deps=()):
    T = x2d.shape[0]
    tm = min(DX_TILE, T)
    n_d = len(dparts)
    groups = [(a, k) for a, d in enumerate(dparts) for k in range(d.shape[1] // D_MODEL)]
    dg_start = jnp.zeros((1, D_MODEL), F32) if prev is None else prev[1]
    carried = () if prev is None else (prev[0],)

    def body(*refs):
        d_refs = refs[:n_d]
        x_ref, dx2_ref, g_ref, dg0_ref, w_hbm = refs[n_d:n_d + 5]
        dx_ref, dg_ref, w_ref, sem = refs[-4:]

        @pl.when(pl.program_id(0) == 0)
        def _():
            cp = pltpu.make_async_copy(w_hbm, w_ref, sem)
            cp.start()
            cp.wait()
            dg_ref[...] = dg0_ref[...]

        dh = jnp.zeros((tm, D_MODEL), F32)
        for j, (a, k) in enumerate(groups):
            dh = dh + _dot_nt(d_refs[a][:, k * D_MODEL:(k + 1) * D_MODEL],
                              w_ref[j // 2, :, (j % 2) * D_MODEL:(j % 2 + 1) * D_MODEL])
        x = x_ref[...]
        r = lax.rsqrt(jnp.mean(x * x, axis=-1, keepdims=True) + EPS)
        nx = x * r
        dg_ref[...] += jnp.sum(dh * nx, axis=0, keepdims=True)
        dhg = dh * g_ref[...]
        dx_ref[...] = dx2_ref[...] + r * (dhg - nx * jnp.mean(dhg * nx, axis=-1, keepdims=True))

    tile = pl.BlockSpec((tm, D_MODEL), lambda i: (first + i, 0))
    one = pl.BlockSpec((1, D_MODEL), lambda i: (0, 0))
    return pl.pallas_call(
        body,
        name=name,
        grid=(count,),
        in_specs=[pl.BlockSpec((tm, d.shape[1]), lambda i: (first + i, 0)) for d in dparts]
        + [tile, tile, one, one, ANY_SPEC] + [ANY_SPEC] * (len(carried) + len(deps)),
        out_specs=[tile, one],
        out_shape=[jax.ShapeDtypeStruct((T, D_MODEL), F32), jax.ShapeDtypeStruct((1, D_MODEL), F32)],
        input_output_aliases={n_d + 5: 0} if carried else {},
        scratch_shapes=[pltpu.VMEM(w_all.shape, w_all.dtype), pltpu.SemaphoreType.DMA],
        compiler_params=_params(("arbitrary",)),
    )(*dparts, x2d, dx2, g_in, dg_start, w_all, *carried, *deps)


def _inproj_bwd_dw(ht, dparts, name, deps=()):
    T = ht.shape[1]
    tn = 512
    half = D_MODEL // 2
    per_chip = 2 * D_MODEL // tn
    n_d = len(dparts)
    tiles = [(a, t) for a, d in enumerate(dparts) for t in range(d.shape[1] // tn)]
    offs = [sum(d.shape[1] // tn for d in dparts[:a]) for a in range(n_d)]

    def body(*refs):
        ht_ref = refs[0]
        d_refs = refs[1:1 + n_d]
        out_ref = refs[-1]
        t = pl.program_id(0)

        for a in range(n_d):
            lo, hi = offs[a], offs[a] + dparts[a].shape[1] // tn

            @pl.when((t >= lo) & (t < hi))
            def _(a=a):
                g = _dot(ht_ref[...], d_refs[a][...])
                out_ref[0, 0] = g[:half]
                out_ref[0, 1] = g[half:]

    def dspec(a):
        n_a = dparts[a].shape[1] // tn
        return pl.BlockSpec((T, tn), lambda t: (0, jnp.clip(t - offs[a], 0, n_a - 1)))

    return pl.pallas_call(
        body,
        name=name,
        grid=(len(tiles),),
        in_specs=[pl.BlockSpec((D_MODEL, T), lambda t: (0, 0))] + [dspec(a) for a in range(n_d)]
        + [ANY_SPEC] * len(deps),
        out_specs=pl.BlockSpec((1, 2, half, tn), lambda t: (t // per_chip, 0, 0, t % per_chip)),
        out_shape=jax.ShapeDtypeStruct((len(tiles) // per_chip, 2, half, 2 * D_MODEL), F32),
        compiler_params=_params(("parallel",)),
    )(ht, *dparts, *deps)


def _coords():
    return lax.axis_index("x"), lax.axis_index("y"), lax.axis_index("c")


def _other_chips(x, y):
    return [(1 - x, y), (x, 1 - y), (1 - x, 1 - y)]


def _chunks(rows, n):
    size = rows // n
    return [pl.ds(q * size, size) for q in range(n)]


HBM_SPEC = pl.BlockSpec(memory_space=pltpu.HBM)
SEM_SPEC = pl.BlockSpec(memory_space=pltpu.SEMAPHORE)
DATAFLOW = pltpu.SideEffectType.DATAFLOW_SIDE_EFFECTING


def _copies_start(bufs, plan, n_copies, name):
    n = len(bufs)

    def body(*refs):
        ins = refs[:n]
        send_sems, recv_sems = refs[n], refs[n + 1]
        token = refs[-1]
        for k, send, _ in plan(ins):
            if send is not None:
                src, dst, dev, pred = send
                cp = pltpu.make_async_remote_copy(src_ref=src, dst_ref=dst, send_sem=send_sems.at[k],
                                                  recv_sem=recv_sems.at[k], device_id=dev, device_id_type=MESH)
                if pred is None:
                    cp.start()
                else:
                    pl.when(pred)(cp.start)
        token[...] = jnp.zeros_like(token)

    hbm = [pltpu.with_memory_space_constraint(b, pltpu.HBM) for b in bufs]
    outs = pl.pallas_call(
        body,
        name=name,
        in_specs=[HBM_SPEC] * n,
        out_specs=(SEM_SPEC, SEM_SPEC, *([HBM_SPEC] * n), pl.BlockSpec(memory_space=pltpu.VMEM)),
        out_shape=(pltpu.SemaphoreType.DMA((n_copies,)), pltpu.SemaphoreType.DMA((n_copies,)),
                   *[pltpu.HBM(b.shape, b.dtype) for b in bufs], jax.ShapeDtypeStruct((8, 128), F32)),
        input_output_aliases={a: 2 + a for a in range(n)},
        compiler_params=pltpu.CompilerParams(has_side_effects=DATAFLOW),
    )(*hbm)
    return outs[0], outs[1], list(outs[2:2 + n]), outs[-1]


def _copies_wait(send_sems, recv_sems, bufs, after, plan, name):
    n = len(bufs)

    def body(*refs):
        ins = refs[:n]
        s_sems, r_sems = refs[n], refs[n + 1]
        for k, send, recv in plan(ins):
            if send is not None:
                src, dst, dev, pred = send
                cp = pltpu.make_async_remote_copy(src_ref=src, dst_ref=dst, send_sem=s_sems.at[k],
                                                  recv_sem=r_sems.at[k], device_id=dev, device_id_type=MESH)
                if pred is None:
                    cp.wait_send()
                else:
                    pl.when(pred)(cp.wait_send)
            if recv is not None:
                dst, pred = recv
                cp = pltpu.make_async_remote_copy(src_ref=dst, dst_ref=dst, send_sem=s_sems.at[k],
                                                  recv_sem=r_sems.at[k], device_id=_coords(), device_id_type=MESH)
                if pred is None:
                    cp.wait_recv()
                else:
                    pl.when(pred)(cp.wait_recv)

    outs = pl.pallas_call(
        body,
        name=name,
        in_specs=[HBM_SPEC] * n + [SEM_SPEC, SEM_SPEC, pl.BlockSpec(memory_space=pl.ANY)],
        out_specs=[HBM_SPEC] * n,
        out_shape=[pltpu.HBM(b.shape, b.dtype) for b in bufs],
        input_output_aliases={a: a for a in range(n)},
        compiler_params=pltpu.CompilerParams(has_side_effects=DATAFLOW),
    )(*bufs, send_sems, recv_sems, after)
    return list(outs)


def _gather_plan(n_bufs):
    def plan(refs):
        x, y, c = _coords()
        me = 2 * x + y
        out = []
        for k, (px, py) in enumerate(_other_chips(x, y)):
            for a in range(n_bufs):
                out.append((k * n_bufs + a, (refs[a].at[me], refs[a].at[me], (px, py, c), None),
                            (refs[a].at[2 * px + py], None)))
        return out
    return plan


def _cast_into_slot(ws, name):
    n = len(ws)
    nt = 2

    def body(s_ref, *refs):
        for a in range(n):
            refs[n + a][0] = refs[a][...].astype(refs[n + a].dtype)

    xi, yi, _ = _coords()
    return pl.pallas_call(
        body,
        name=name,
        grid_spec=pltpu.PrefetchScalarGridSpec(
            num_scalar_prefetch=1,
            grid=(2, nt),
            in_specs=[pl.BlockSpec((1, w.shape[1] // nt, w.shape[2]), lambda hf, i, s: (hf, i, 0)) for w in ws],
            out_specs=[pl.BlockSpec((1, 1, w.shape[1] // nt, w.shape[2]), lambda hf, i, s: (s[0], hf, i, 0)) for w in ws],
        ),
        out_shape=[jax.ShapeDtypeStruct((N_CHIPS,) + w.shape, _MXU) for w in ws],
        compiler_params=_params(("parallel", "parallel")),
    )((2 * xi + yi).reshape(1).astype(jnp.int32), *ws)


def _gather_chips(bufs, n_chunks, name):
    n = len(bufs)
    pieces = [(a, rows) for a in range(n) for rows in _chunks(bufs[a].shape[2], n_chunks[a])]
    n_p = len(pieces)

    def body(*refs):
        outs = refs[n:2 * n]
        send_sems, recv_sems, fsend_sems, frecv_sems = refs[2 * n:]
        x, y, c = _coords()
        me = 2 * x + y
        near = [(1 - x, y), (x, 1 - y)]
        slots = [2 * (1 - x) + y, 2 * x + (1 - y), 2 * (1 - x) + (1 - y)]
        pass_to = (jnp.where(c == 0, x, 1 - x), jnp.where(c == 0, 1 - y, y))
        pass_slot = jnp.where(c == 0, slots[0], slots[1])

        def send(k, i, slot, chip):
            a, rows = pieces[i]
            return pltpu.make_async_remote_copy(
                src_ref=outs[a].at[slot, c, rows], dst_ref=outs[a].at[slot, c, rows], send_sem=send_sems.at[k * n_p + i],
                recv_sem=recv_sems.at[k * n_p + i], device_id=(*chip, c), device_id_type=MESH)

        def forward(k, i, half):
            a, rows = pieces[i]
            return pltpu.make_async_remote_copy(
                src_ref=outs[a].at[slots[k], half, rows], dst_ref=outs[a].at[slots[k], half, rows],
                send_sem=fsend_sems.at[k * n_p + i], recv_sem=frecv_sems.at[k * n_p + i],
                device_id=(x, y, 1 - c), device_id_type=MESH)

        started = [send(k, i, me, chip) for i in range(n_p) for k, chip in enumerate(near)]
        for cp in started:
            cp.start()
        for i in range(n_p):
            for k, chip in enumerate(near):
                send(k, i, slots[k], chip).wait_recv()
            later = [send(2, i, pass_slot, pass_to), forward(0, i, c), forward(1, i, c)]
            for cp in later:
                cp.start()
            started += later
        for i in range(n_p):
            send(2, i, slots[2], pass_to).wait_recv()
            fw = forward(2, i, c)
            fw.start()
            started.append(fw)
        for i in range(n_p):
            for k in range(3):
                forward(k, i, 1 - c).wait_recv()
        for cp in started:
            cp.wait_send()

    anyspec = pl.BlockSpec(memory_space=pl.ANY)
    sems = pltpu.SemaphoreType.DMA((3 * n_p,))
    return pl.pallas_call(
        body,
        name=name,
        in_specs=[anyspec] * n,
        out_specs=[anyspec] * n,
        out_shape=[jax.ShapeDtypeStruct(b.shape, b.dtype) for b in bufs],
        input_output_aliases={a: a for a in range(n)},
        scratch_shapes=[sems, sems, sems, sems],
    )(*bufs)


def _swap_plan(n_slabs):
    def plan(refs):
        x, y, c = _coords()
        out, k = [], 0
        for i, n in enumerate(n_slabs):
            g, land = refs[2 * i], refs[2 * i + 1]
            for p in range(n):
                out.append((k, (g.at[p, 1 - c], land.at[p], (x, y, 1 - c), None), (land.at[p], None)))
                k += 1
        return out
    return plan


def _is_one_of(chip, dests):
    hit = chip == dests[0]
    for d in dests[1:]:
        hit = hit | (chip == d)
    return hit


def _slab_of(chip, dests):
    return sum(j * (chip == d).astype(jnp.int32) for j, d in enumerate(dests))


def _scatter_plan(dest_sets):
    def plan(refs):
        x, y, c = _coords()
        me = 2 * x + y
        out = []
        for k, (px, py) in enumerate(_other_chips(x, y)):
            peer = 2 * px + py
            for i, dests in enumerate(dest_sets):
                cs, land = refs[2 * i], refs[2 * i + 1]
                everyone = len(dests) == N_CHIPS
                send = (cs.at[_slab_of(peer, dests)], land.at[k], (px, py, c),
                        None if everyone else _is_one_of(peer, dests))
                recv = (land.at[k], None if everyone else _is_one_of(me, dests))
                out.append((k * len(dest_sets) + i, send, recv))
        return out
    return plan


def _join_plan(rows, n_pieces):
    def plan(refs):
        x, y, c = _coords()
        (buf,) = refs
        return [(i, (buf.at[c, piece], buf.at[c, piece], (x, y, 1 - c), None), (buf.at[1 - c, piece], None))
                for i, piece in enumerate(_chunks(rows, n_pieces))]
    return plan


def _join_plans(parts):
    def plan(refs):
        out, b0, k0 = [], 0, 0
        for part_plan, n_bufs, n_copies in parts:
            out += [(k0 + k, send, recv) for k, send, recv in part_plan(refs[b0:b0 + n_bufs])]
            b0 += n_bufs
            k0 += n_copies
        return out
    return plan


def _allgather_plan():
    def plan(refs):
        x, y, c = _coords()
        (land,) = refs
        me = 4 * x + 2 * y + c
        out = []
        for r in range(1, 8):
            px = 1 - x if r & 4 else x
            py = 1 - y if r & 2 else y
            pc = 1 - c if r & 1 else c
            out.append((r - 1, (land.at[me], land.at[me], (px, py, pc), None), (land.at[4 * px + 2 * py + pc], None)))
        return out
    return plan


def _sum_gathered(land, name):
    def body(land_ref, o_ref):
        acc = land_ref[0]
        for d in range(1, 8):
            acc = acc + land_ref[d]
        o_ref[...] = acc

    return pl.pallas_call(
        body,
        name=name,
        out_shape=jax.ShapeDtypeStruct(land.shape[1:], F32),
        compiler_params=_params(),
    )(land)


def _join_halves(bufs, n_chunks, name, deps=()):
    n = len(bufs)
    pieces = [(a, rows) for a in range(n) for rows in _chunks(bufs[a].shape[1], n_chunks[a])]
    n_p = len(pieces)

    def body(*refs):
        outs = refs[-n - 2:-2]
        send_sems, recv_sems = refs[-2:]
        x, y, c = _coords()

        def copy(i, half):
            a, rows = pieces[i]
            return pltpu.make_async_remote_copy(
                src_ref=outs[a].at[half, rows], dst_ref=outs[a].at[half, rows], send_sem=send_sems.at[i],
                recv_sem=recv_sems.at[i], device_id=(x, y, 1 - c), device_id_type=MESH)

        sends = [copy(i, c) for i in range(n_p)]
        for cp in sends:
            cp.start()
        for i in range(n_p):
            copy(i, 1 - c).wait_recv()
        for cp in sends:
            cp.wait_send()

    anyspec = pl.BlockSpec(memory_space=pl.ANY)
    sems = pltpu.SemaphoreType.DMA((n_p,))
    return pl.pallas_call(
        body,
        name=name,
        in_specs=[anyspec] * (n + len(deps)),
        out_specs=[anyspec] * n,
        out_shape=[jax.ShapeDtypeStruct(b.shape, b.dtype) for b in bufs],
        input_output_aliases={a: a for a in range(n)},
        scratch_shapes=[sems, sems],
    )(*bufs, *deps)


def _row_tile(rows, cap):
    t = cap
    while rows % t:
        t //= 2
    return t


def _add_my_half(g, r, name):
    n_slabs, _, R, C = g.shape
    tr = R if n_slabs > 1 else _row_tile(R, 256)

    def body(c_ref, g_ref, r_ref, o_ref):
        o_ref[...] = (g_ref[0] + r_ref[...]).astype(o_ref.dtype)

    return pl.pallas_call(
        body,
        name=name,
        grid_spec=pltpu.PrefetchScalarGridSpec(
            num_scalar_prefetch=1,
            grid=(n_slabs, R // tr),
            in_specs=[pl.BlockSpec((1, 1, tr, C), lambda p, i, c_ref: (p, c_ref[0], i, 0)),
                      pl.BlockSpec((1, tr, C), lambda p, i, c_ref: (p, i, 0))],
            out_specs=pl.BlockSpec((1, tr, C), lambda p, i, c_ref: (p, i, 0)),
        ),
        out_shape=jax.ShapeDtypeStruct(r.shape, jnp.bfloat16),
        compiler_params=_params(("parallel", "parallel")),
    )(lax.axis_index("c").reshape(1).astype(jnp.int32), g, r)


def _sum_slabs(own, got, name, deps=()):
    _, R, C = own.shape
    tr = _row_tile(R, 256)

    def body(s_ref, own_ref, got_ref, *rest):
        rest[-1][0] = ((own_ref[0].astype(F32) + got_ref[0].astype(F32)) + got_ref[1].astype(F32)) + got_ref[2].astype(F32)

    xi, yi, ci = _coords()
    return pl.pallas_call(
        body,
        name=name,
        grid_spec=pltpu.PrefetchScalarGridSpec(
            num_scalar_prefetch=1,
            grid=(R // tr,),
            in_specs=[pl.BlockSpec((1, tr, C), lambda i, s: (s[0], i, 0)),
                      pl.BlockSpec((3, tr, C), lambda i, s: (0, i, 0))] + [ANY_SPEC] * len(deps),
            out_specs=pl.BlockSpec((1, tr, C), lambda i, s: (s[1], i, 0)),
        ),
        out_shape=jax.ShapeDtypeStruct((2, R, C), F32),
        compiler_params=_params(("parallel",)),
    )(jnp.stack([2 * xi + yi, ci]).astype(jnp.int32), own, got, *deps)


def _sum_parts(owns, got, dest_sets, name):
    n = len(owns)
    _, R, C = owns[0].shape
    tr = _row_tile(R, 256)

    def body(s_ref, *refs):
        got_ref, o_ref = refs[n], refs[-1]
        total = jnp.zeros((tr, C), F32)
        for i in range(n):
            total = total + jnp.where(s_ref[2 + 2 * i] == 1, refs[i][0].astype(F32), 0.0)
        o_ref[0] = ((total + got_ref[0].astype(F32)) + got_ref[1].astype(F32)) + got_ref[2].astype(F32)

    xi, yi, ci = _coords()
    me = 2 * xi + yi
    scalars = [ci, ci]
    for dests in dest_sets:
        scalars += [_is_one_of(me, dests).astype(jnp.int32), _slab_of(me, dests)]
    own_spec = lambda i: pl.BlockSpec((1, tr, C), lambda r, s: (s[3 + 2 * i], r, 0))
    return pl.pallas_call(
        body,
        name=name,
        grid_spec=pltpu.PrefetchScalarGridSpec(
            num_scalar_prefetch=1,
            grid=(R // tr,),
            in_specs=[own_spec(i) for i in range(n)] + [pl.BlockSpec((3, tr, C), lambda r, s: (0, r, 0))],
            out_specs=pl.BlockSpec((1, tr, C), lambda r, s: (s[0], r, 0)),
        ),
        out_shape=jax.ShapeDtypeStruct((2, R, C), F32),
        compiler_params=_params(("parallel",)),
    )(jnp.stack(scalars).astype(jnp.int32), *owns, got)


def _adamw_math(w, g, m, v):
    m = ADAM_B1 * m + (1.0 - ADAM_B1) * g
    v = ADAM_B2 * v + (1.0 - ADAM_B2) * (g * g)
    m_hat = m / (1.0 - ADAM_B1 ** ADAM_STEP)
    v_hat = v / (1.0 - ADAM_B2 ** ADAM_STEP)
    delta = -ADAM_LR * (m_hat / (jnp.sqrt(v_hat) + ADAM_EPS) + ADAM_WD * w)
    return delta, m, v


def _adamw_halves(ws, g, ms, vs, half, prev, name, deps=()):
    n = len(ws)
    _, _, R, C = g.shape
    tr = _row_tile(R, 128)
    steps = R // tr
    carried = [] if prev is None else [a for four in prev for a in four]
    both = half is None
    which = (lambda i, s: i // steps) if both else (lambda i, s: s[0])
    half = 0 if both else half

    def body(s_ref, *refs):
        w_refs, g_refs, m_refs, v_refs = (refs[k * n:(k + 1) * n] for k in range(4))
        outs = refs[len(refs) - 4 * n:]
        for a in range(n):
            grad = g_refs[a][0, 0]
            d, mn, vn = _adamw_math(w_refs[a][...], grad, m_refs[a][...], v_refs[a][...])
            for o, val in zip(outs[4 * a:4 * a + 4], (grad, d, mn, vn)):
                o[...] = val

    rows = pl.BlockSpec((tr, C), lambda i, s: (which(i, s) * steps + i % steps, 0))
    grad_spec = lambda a: pl.BlockSpec((1, 1, tr, C), lambda i, s: (which(i, s), a, i % steps, 0))
    n_in = 4 * n
    outs = pl.pallas_call(
        body,
        name=name,
        grid_spec=pltpu.PrefetchScalarGridSpec(
            num_scalar_prefetch=1,
            grid=(2 * steps if both else steps,),
            in_specs=[rows] * n + [grad_spec(a) for a in range(n)] + [rows] * (2 * n)
            + [ANY_SPEC] * (len(carried) + len(deps)),
            out_specs=[rows] * (4 * n),
        ),
        out_shape=[jax.ShapeDtypeStruct((2 * R, C), F32)] * (4 * n),
        input_output_aliases={1 + n_in + k: k for k in range(len(carried))},
        compiler_params=_params(("parallel",)),
    )(jnp.reshape(half, (1,)).astype(jnp.int32), *ws, *([g] * n), *ms, *vs, *carried, *deps)
    return [outs[4 * a:4 * a + 4] for a in range(n)]


def _adamw_small(ws, gs, ms, vs, name):
    n = len(ws)

    def body(*refs):
        for a in range(n):
            d, mn, vn = _adamw_math(refs[a][...], refs[n + a][...], refs[2 * n + a][...], refs[3 * n + a][...])
            refs[4 * n + a][...] = d
            refs[5 * n + a][...] = mn
            refs[6 * n + a][...] = vn

    shapes = [jax.ShapeDtypeStruct(w.shape, F32) for w in ws]
    outs = pl.pallas_call(
        body,
        name=name,
        out_shape=shapes * 3,
        compiler_params=_params(),
    )(*ws, *gs, *ms, *vs)
    return outs[:n], outs[n:2 * n], outs[2 * n:]


def _to_blockdiag(w):
    per = CW // LRU_BW
    w4 = w.reshape(N_CT, per, LRU_BW, LRU_BW)
    eye = jnp.eye(per, dtype=w.dtype)
    return (w4[:, :, :, None, :] * eye[None, :, None, :, None]).reshape(N_CT, CW, CW)


def _from_blockdiag(g):
    per = CW // LRU_BW
    g5 = g.reshape(N_CT, per, LRU_BW, per, LRU_BW)
    return jnp.stack([g5[:, b, :, b, :] for b in range(per)], axis=1).reshape(LRU_BLOCKS, LRU_BW, LRU_BW)


def _local_grads(x2d, tgt2d, B, S, g_in, w_all, conv_w, conv_b, gate_x_w, gate_x_b, gate_a_w, gate_a_b, lam, gain,
                 proj_weights, g_fin, reduce, deps=()):
    wx_bd = _c(_to_blockdiag(gate_x_w))
    wa_bd = _c(_to_blockdiag(gate_a_w))
    tables = _retention_tables(S)
    gain3 = gain.reshape(HEADS, 1, DK)

    proj, ht = _inproj_fwd(x2d, g_in, w_all, deps)
    hlru, ya = _lru_fwd(proj, conv_w, conv_b, wx_bd, wa_bd, gate_x_b, gate_a_b, lam, B, S)
    o_pre, yb, states = _ret_fwd(proj, tables, gain3, B, S)
    wpa, wpb, wout = proj_weights(yb)
    loss, dx2, dya, dyb, dm, dgf, gw_proj = _mid(ya, yb, proj, x2d, tgt2d, wpa, wpb, wout, g_fin)
    g3 = _inproj_bwd_dw(ht, [dm], "inproj_bwd_dw_m")
    deps = reduce.m_ready(gw_proj, g3)
    dr, dgain = _ret_bwd(dyb, o_pre, proj, states, tables, gain3, B, S, deps)
    deps = reduce.ret_done(dr)
    g12 = _inproj_bwd_dw(ht, [dr], "inproj_bwd_dw_r", deps)
    deps = reduce.r_ready(g12)
    dxa, dga, dcw, dcb, dwx_bd, dwa_bd, dbx, dba, dlam = _lru_bwd(
        dya, proj, hlru, conv_w, conv_b, wx_bd, wa_bd, gate_x_b, gate_a_b, lam, B, S, deps)
    small = dict(conv_w=dcw, conv_b=dcb, gate_x_w=_from_blockdiag(dwx_bd), gate_x_b=dbx,
                 gate_a_w=_from_blockdiag(dwa_bd), gate_a_b=dba, lru_lambda=dlam, gn_gain=dgain.reshape(HEADS, DK),
                 norm_final=dgf)
    loss_rows = jnp.broadcast_to(loss, (SUBLANES, LANES))
    deps = reduce.lru_done(dxa, jnp.concatenate([_pack_small(small), loss_rows], axis=0))
    g0 = _inproj_bwd_dw(ht, [dxa, dga], "inproj_bwd_dw_a", deps)
    deps = reduce.a_ready(g0)
    n_tiles = x2d.shape[0] // min(DX_TILE, x2d.shape[0])
    grad_x, dgin = _inproj_bwd_dx([dxa, dga, dr, dm], w_all, x2d, dx2, g_in, 0, n_tiles, None, "inproj_bwd_dx", deps)
    return grad_x, dgin


ALL_CHIPS = (0, 1, 2, 3)


class _GradReduce:
    def __init__(self, proj_done):
        self.pending = {}
        self.proj_done = proj_done
        self.land_in = None

    def _start(self, key, parts, name):
        bufs, plans, shared = [], [], None
        for part_bufs, plan, n_copies, part_shared in parts:
            if part_shared is not None:
                shared = len(bufs) + part_shared
            plans.append((plan, len(part_bufs), n_copies))
            bufs += part_bufs
        plan = _join_plans(plans)
        send_sems, recv_sems, bufs, token = _copies_start(bufs, plan, sum(p[2] for p in plans), name + "_start")
        if shared is not None:
            self.land_in = bufs[shared]
        self.pending[key] = (send_sems, recv_sems, bufs, plan, name + "_wait", shared)
        return (token,)

    def _finish(self, key, after):
        send_sems, recv_sems, bufs, plan, name, shared = self.pending.pop(key)
        if shared is not None:
            bufs[shared] = self.land_in
        bufs = _copies_wait(send_sems, recv_sems, bufs, after, plan, name)
        if shared is not None:
            self.land_in = bufs[shared]
        return bufs

    @staticmethod
    def _swap(pieces):
        bufs = []
        for g in pieces:
            bufs += [g, lax.empty((g.shape[0],) + g.shape[2:], F32)]
        n_slabs = [g.shape[0] for g in pieces]
        return bufs, _swap_plan(n_slabs), sum(n_slabs), None

    def _scatter(self, sums, dest_sets):
        bufs = []
        for cs in sums:
            bufs += [cs, lax.empty((3,) + cs.shape[1:], cs.dtype)]
        if self.land_in is not None:
            bufs[-1] = self.land_in
        return bufs, _scatter_plan(dest_sets), 3 * len(sums), len(bufs) - 1

    @staticmethod
    def _gather8(block):
        x, y, c = _coords()
        land = lax.dynamic_update_slice(lax.empty((8,) + block.shape, F32), block[None], (4 * x + 2 * y + c, 0, 0))
        return [land], _allgather_plan(), 7, None

    def m_ready(self, gw_proj, g3):
        rows = gw_proj.shape[2] * gw_proj.shape[3]
        return self._start("m", [self._swap([gw_proj.reshape(N_CHIPS, 2, rows, D_MODEL), g3])], "swap_m")

    def ret_done(self, after):
        proj, land_p, g3, land_3 = self._finish("m", after)
        sums_m = [_add_my_half(proj, land_p, "chip_sum_proj"), _add_my_half(g3, land_3, "chip_sum_m")]
        return self._start("sm", [self._scatter(sums_m, [ALL_CHIPS, (3,)])], "scatter_m")

    def r_ready(self, g12):
        return self._start("r", [self._swap([g12])], "swap_r")

    def lru_done(self, after, packed):
        g12, land_12 = self._finish("r", after)
        sums_r = [_add_my_half(g12, land_12, "chip_sum_r")]
        return (self._start("sr", [self._scatter(sums_r, [(1, 2)])], "scatter_r")
                + self._start("small", [self._gather8(packed)], "gather_small"))

    def a_ready(self, g0):
        (token,) = self._start("a", [self._swap([g0])], "swap_a")
        csp, gotp, self.cs3, _ = self._finish("sm", token)
        half_proj = _sum_slabs(csp, gotp, "sum_w_proj")
        g0, land_0 = self._finish("a", half_proj)
        deps = self._start("sa", [self._scatter([_add_my_half(g0, land_0, "chip_sum_a")], [(0,)])], "scatter_a")
        self.proj_done(_join_halves([half_proj], [4], "join_halves_proj", deps)[0])
        return deps

    def finish(self, dgin, w_in_done):
        (token,) = self._start("n", [self._gather8(dgin)], "gather_norm_in")
        (small,) = self._finish("small", token)
        cs12, _ = self._finish("sr", token)
        cs0, _ = self._finish("sa", token)
        half_in = _sum_parts([self.cs3, cs12, cs0], self.land_in, [(3,), (1, 2), (0,)], "sum_w_in")
        deps = self._start("j", [([half_in], _join_plan(half_in.shape[1], 8), 8, None)], "join_w_in")
        first = w_in_done(self.pending["j"][2][0], True, None, deps)
        (g_in,) = self._finish("j", first[1])
        done = w_in_done(g_in, False, first, ())
        (norm_in,) = self._finish("n", done[1])
        return _sum_gathered(small, "sum_small_grads"), _sum_gathered(norm_in, "sum_norm_in_grad")


_SMALL = ("gate_x_w", "gate_a_w", "conv_w", "conv_b", "gate_x_b", "gate_a_b", "lru_lambda", "gn_gain", "norm_final")
_SMALL_SHAPES = dict(gate_x_w=(LRU_BLOCKS, LRU_BW, LRU_BW), gate_a_w=(LRU_BLOCKS, LRU_BW, LRU_BW),
                     norm_in=(1, D_MODEL), conv_w=(CONV, D_MODEL), conv_b=(1, D_MODEL), gate_x_b=(1, D_MODEL),
                     gate_a_b=(1, D_MODEL), lru_lambda=(1, D_MODEL), gn_gain=(HEADS, DK), norm_final=(1, D_MODEL))


def _pack_small(small):
    return jnp.concatenate([small[k].reshape(-1, 128) for k in _SMALL], axis=0)


def _unpack_small(packed):
    out, r = {}, 0
    for k in _SMALL:
        shape = _SMALL_SHAPES[k]
        rows = 1
        for s in shape:
            rows *= s
        rows //= 128
        out[k] = packed[r:r + rows].reshape(shape)
        r += rows
    return out


def kernel(x, norm_in, w_in, conv_w, conv_b, gate_x_w, gate_x_b, gate_a_w, gate_a_b, lru_lambda, gn_gain, w_proj_a, w_proj_b, w_out, norm_final, loss_target, m_norm_in, m_w_in, m_conv_w, m_conv_b, m_gate_x_w, m_gate_x_b, m_gate_a_w, m_gate_a_b, m_lru_lambda, m_gn_gain, m_w_proj_a, m_w_proj_b, m_w_out, m_norm_final, v_norm_in, v_w_in, v_conv_w, v_conv_b, v_gate_x_w, v_gate_x_b, v_gate_a_w, v_gate_a_b, v_lru_lambda, v_gn_gain, v_w_proj_a, v_w_proj_b, v_w_out, v_norm_final):
    B, S, _ = x.shape
    T = B * S
    xi, yi, ci = _coords()
    chip = 2 * xi + yi

    cshard = D_MODEL // N_CHIPS
    mine = _cast_into_slot([w_in[0].reshape(2, D_MODEL // 2, 2 * D_MODEL)]
                           + [w[0].reshape(2, cshard // 2, D_MODEL) for w in (w_proj_a, w_proj_b, w_out)],
                           "cast_weights")
    plan = _gather_plan(3)
    s_sems, r_sems, pbufs, token = _copies_start(mine[1:], plan, 9, "gather_proj_start")
    gshard = DK // N_CHIPS
    tiny = jnp.concatenate([conv_w[0], jnp.zeros((4, cshard), F32), jnp.pad(gn_gain[0], ((0, 4), (0, cshard - gshard)))],
                           axis=0).reshape(1, 2, SUBLANES, cshard)
    tiny_buf = lax.dynamic_update_slice(lax.empty((N_CHIPS, 2, SUBLANES, cshard), F32), tiny, (chip, 0, 0, 0))
    w_buf, tiny_buf = _gather_chips([mine[0], tiny_buf], [8, 1], "gather_weights")
    w_all = w_buf.reshape(N_CHIPS, D_MODEL, 2 * D_MODEL)
    tiny_all = tiny_buf.reshape(N_CHIPS, 2 * SUBLANES, cshard)

    def proj_weights(after):
        got = _copies_wait(s_sems, r_sems, pbufs, after, plan, "gather_proj_wait")
        return [b.reshape(D_MODEL, D_MODEL) for b in got]

    conv_w_full = jnp.transpose(tiny_all[:, 0:CONV, :], (1, 0, 2)).reshape(CONV, D_MODEL)
    gain_full = jnp.transpose(tiny_all[:, 8:8 + HEADS, :gshard], (1, 0, 2)).reshape(HEADS, DK)

    weights = dict(norm_in=norm_in, w_in=w_in, conv_w=conv_w, conv_b=conv_b, gate_x_w=gate_x_w, gate_x_b=gate_x_b,
                   gate_a_w=gate_a_w, gate_a_b=gate_a_b, lru_lambda=lru_lambda, gn_gain=gn_gain, w_proj_a=w_proj_a,
                   w_proj_b=w_proj_b, w_out=w_out, norm_final=norm_final)
    ms = dict(norm_in=m_norm_in, w_in=m_w_in, conv_w=m_conv_w, conv_b=m_conv_b, gate_x_w=m_gate_x_w,
              gate_x_b=m_gate_x_b, gate_a_w=m_gate_a_w, gate_a_b=m_gate_a_b, lru_lambda=m_lru_lambda, gn_gain=m_gn_gain,
              w_proj_a=m_w_proj_a, w_proj_b=m_w_proj_b, w_out=m_w_out, norm_final=m_norm_final)
    vs = dict(norm_in=v_norm_in, w_in=v_w_in, conv_w=v_conv_w, conv_b=v_conv_b, gate_x_w=v_gate_x_w,
              gate_x_b=v_gate_x_b, gate_a_w=v_gate_a_w, gate_a_b=v_gate_a_b, lru_lambda=v_lru_lambda, gn_gain=v_gn_gain,
              w_proj_a=v_w_proj_a, w_proj_b=v_w_proj_b, w_out=v_w_out, norm_final=v_norm_final)
    names = list(weights)
    grads, delta, new_m, new_v = {}, {}, {}, {}

    def update_big(keys, g, half, prev, name, deps=()):
        two = lambda a: a.reshape(a.shape[1], a.shape[2])
        res = _adamw_halves([two(weights[k]) for k in keys], g, [two(ms[k]) for k in keys], [two(vs[k]) for k in keys],
                            half, prev, name, deps)
        for k, (gk, d, mn, vn) in zip(keys, res):
            shp = weights[k].shape
            grads[k], delta[k], new_m[k], new_v[k] = gk.reshape(shp), d.reshape(shp), mn.reshape(shp), vn.reshape(shp)
        return res

    def proj_done(g_proj):
        g4 = g_proj.reshape(2, 3, D_MODEL // (2 * N_CHIPS), D_MODEL)
        return update_big(("w_proj_a", "w_proj_b", "w_out"), g4, None, None, "adamw_proj")[-1][1]

    def w_in_done(g_in, own, prev, deps):
        g4 = g_in.reshape(2, 1, D_MODEL // 2, 2 * D_MODEL)
        return update_big(("w_in",), g4, ci if own else 1 - ci, None if prev is None else [prev],
                          "adamw_w_in_own" if own else "adamw_w_in_other", deps)[0]

    reduce = _GradReduce(proj_done)
    grad_x, dgin = _local_grads(
        x.reshape(T, D_MODEL), loss_target.reshape(T, D_MODEL), B, S, norm_in, w_all, conv_w_full, conv_b,
        gate_x_w[0], gate_x_b, gate_a_w[0], gate_a_b, lru_lambda, gain_full, proj_weights,
        norm_final.reshape(1, D_MODEL), reduce, deps=(token,))

    small_sum, g_norm_in = reduce.finish(dgin.reshape(SUBLANES, LANES), w_in_done)
    loss = small_sum[small_sum.shape[0] - SUBLANES, 0]

    gsm = _unpack_small(small_sum)
    gsm["norm_in"] = g_norm_in
    gsm["conv_w"] = lax.dynamic_slice_in_dim(gsm["conv_w"], chip * cshard, cshard, axis=1)
    gsm["gn_gain"] = lax.dynamic_slice_in_dim(gsm["gn_gain"], chip * gshard, gshard, axis=1)
    smalls = [k for k in names if k not in delta]

    def view(a):
        return a.reshape(1, -1) if a.ndim == 1 else (a.reshape(a.shape[1:]) if a.ndim > 2 else a)

    ds, mns, vns = _adamw_small([view(weights[k]) for k in smalls], [gsm[k].reshape(view(weights[k]).shape) for k in smalls],
                                [view(ms[k]) for k in smalls], [view(vs[k]) for k in smalls], "adamw_small")
    for k, d, mn, vn in zip(smalls, ds, mns, vns):
        shp = weights[k].shape
        grads[k], delta[k], new_m[k], new_v[k] = gsm[k].reshape(shp), d.reshape(shp), mn.reshape(shp), vn.reshape(shp)

    return (loss, grad_x.reshape(B, S, D_MODEL), *[grads[k] for k in names], *[delta[k] for k in names],
            *[new_m[k] for k in names], *[new_v[k] for k in names])
```

```python
import jax
import jax.numpy as jnp
from jax import lax
from jax.experimental import pallas as pl
from jax.experimental.pallas import tpu as pltpu

F32 = jnp.float32
_MXU = jnp.bfloat16

D_MODEL = 1024
N_GROUPS = 8
HEADS = 4
DK = 256
CHUNK = 128
CONV = 4
LRU_BLOCKS = 16
LRU_BW = 64
LRU_C = 8.0
ROPE_THETA = 10000.0
EPS = 1e-6
CW = 256
N_CT = D_MODEL // CW
N_CHIPS = 4
MESH = pl.DeviceIdType.MESH

ADAM_LR = 0.001
ADAM_B1 = 0.9
ADAM_B2 = 0.999
ADAM_EPS = 1e-08
ADAM_WD = 0.01
ADAM_STEP = 10

VMEM_LIMIT = 56 * 1024 * 1024


def _c(v):
    return v.astype(_MXU)


def _dot(a, b):
    return lax.dot_general(a, b, (((1,), (0,)), ((), ())), preferred_element_type=F32)


def _dot_nt(a, b):
    return lax.dot_general(a, b, (((1,), (1,)), ((), ())), preferred_element_type=F32)


def _dot_tn(a, b):
    return lax.dot_general(a, b, (((0,), (0,)), ((), ())), preferred_element_type=F32)


def _sigmoid(z):
    return 0.5 * jnp.tanh(0.5 * z) + 0.5


ANY_SPEC = pl.BlockSpec(memory_space=pl.ANY)


def _after(body, n_in, deps):
    n_deps = len(deps)

    def wrapped(*refs):
        return body(*refs[:n_in], *refs[n_in + n_deps:])

    return wrapped


def _params(sem=None):
    if sem is None:
        return pltpu.CompilerParams(vmem_limit_bytes=VMEM_LIMIT)
    return pltpu.CompilerParams(vmem_limit_bytes=VMEM_LIMIT, dimension_semantics=sem)


def _inproj_fwd(x2d, g_in, w_all, deps=()):
    T = x2d.shape[0]
    tm = min(1024, T)
    n_i = T // tm

    def body(*refs):
        x_ref, g_ref, w_ref = refs[:3]
        proj_ref, ht_ref, h_all = refs[-3:]
        i = pl.program_id(1)
        rows = pl.ds(pl.multiple_of(i * tm, tm), tm)

        @pl.when(pl.program_id(0) == 0)
        def _():
            x = x_ref[...]
            r = lax.rsqrt(jnp.mean(x * x, axis=-1, keepdims=True) + EPS)
            h = x * r * g_ref[...]
            h_all[rows, :] = h.astype(h_all.dtype)
            ht_ref[...] = h.T.astype(ht_ref.dtype)

        proj_ref[...] = _dot(h_all[rows, :], w_ref[0])

    first = lambda j, i: jnp.where(j == 0, i, n_i - 1)
    return pl.pallas_call(
        body,
        name="inproj_fwd",
        grid=(N_GROUPS, n_i),
        in_specs=[
            pl.BlockSpec((tm, D_MODEL), lambda j, i: (first(j, i), 0)),
            pl.BlockSpec((1, D_MODEL), lambda j, i: (0, 0)),
            pl.BlockSpec((1, D_MODEL, D_MODEL), lambda j, i: (j // 2, 0, j % 2)),
        ] + [pl.BlockSpec(memory_space=pl.ANY)] * len(deps),
        out_specs=[
            pl.BlockSpec((tm, D_MODEL), lambda j, i: (i, j)),
            pl.BlockSpec((D_MODEL, tm), lambda j, i: (0, first(j, i))),
        ],
        out_shape=[
            jax.ShapeDtypeStruct((T, N_GROUPS * D_MODEL), F32),
            jax.ShapeDtypeStruct((D_MODEL, T), _MXU),
        ],
        scratch_shapes=[pltpu.VMEM((T, D_MODEL), _MXU)],
        compiler_params=_params(("arbitrary", "arbitrary")),
    )(x2d, g_in, w_all, *deps)


def _scan_fwd(a, u):
    n = a.shape[0]
    row = lax.broadcasted_iota(jnp.int32, a.shape, 0)
    s = 1
    while s < n:
        m = row >= s
        u = u + a * jnp.where(m, pltpu.roll(u, s, 0), 0.0)
        a = a * jnp.where(m, pltpu.roll(a, s, 0), 1.0)
        s *= 2
    return a, u


def _scan_bwd(b, g):
    n = b.shape[0]
    row = lax.broadcasted_iota(jnp.int32, b.shape, 0)
    s = 1
    while s < n:
        m = row < n - s
        g = g + b * jnp.where(m, pltpu.roll(g, n - s, 0), 0.0)
        b = b * jnp.where(m, pltpu.roll(b, n - s, 0), 1.0)
        s *= 2
    return b, g


LANES = 128
SUBLANES = 8


def _scan_scratch(tc):
    by_lanes = pltpu.VMEM((CW // LANES, tc, LANES), F32)
    return [by_lanes, by_lanes, pltpu.VMEM((tc // SUBLANES, CW), F32), pltpu.VMEM((tc, CW), F32)]


def _scan_tile(a, u, edge, la_ref, lh_ref, c_ref, dst_ref, reverse):
    n, w = a.shape
    groups = n // SUBLANES
    a3 = a.reshape(groups, SUBLANES, w)
    u3 = u.reshape(groups, SUBLANES, w)
    row = lax.broadcasted_iota(jnp.int32, a3.shape, 1)
    for s in (1, 2, 4):
        m = (row < SUBLANES - s) if reverse else (row >= s)
        shift = SUBLANES - s if reverse else s
        u3 = u3 + a3 * jnp.where(m, pltpu.roll(u3, shift, 1), 0.0)
        a3 = a3 * jnp.where(m, pltpu.roll(a3, shift, 1), 1.0)
    al = a3.reshape(n, w)
    hl = u3.reshape(n, w)
    blocks = w // LANES
    for q in range(blocks):
        la_ref[q] = al[:, q * LANES:(q + 1) * LANES]
        lh_ref[q] = hl[:, q * LANES:(q + 1) * LANES]
    ends = pl.ds(0 if reverse else SUBLANES - 1, groups, stride=SUBLANES)
    end_a = jnp.concatenate([la_ref.at[q][ends, :] for q in range(blocks)], axis=-1)
    end_h = jnp.concatenate([lh_ref.at[q][ends, :] for q in range(blocks)], axis=-1)
    prod, part = (_scan_bwd if reverse else _scan_fwd)(end_a, end_h)
    total = part + prod * edge
    g_row = lax.broadcasted_iota(jnp.int32, total.shape, 0)
    if reverse:
        c_ref[...] = jnp.where(g_row == groups - 1, edge, pltpu.roll(total, groups - 1, 0))
    else:
        c_ref[...] = jnp.where(g_row == 0, edge, pltpu.roll(total, 1, 0))
    for g in range(groups):
        rows = slice(g * SUBLANES, (g + 1) * SUBLANES)
        for q in range(blocks):
            cols = slice(q * LANES, (q + 1) * LANES)
            dst_ref[rows, cols] = lh_ref[q, rows, :] + la_ref[q, rows, :] * c_ref[g:g + 1, cols]


def _softplus_neg(lam):
    z = -lam
    return jnp.maximum(z, 0.0) + jnp.log1p(jnp.exp(-jnp.abs(z)))


def _lru_gates(xc, wx_ref, wa_ref, bx_ref, ba_ref, lam_ref):
    xcb = _c(xc)
    i_t = _sigmoid(_dot(xcb, wx_ref[0]) + bx_ref[...])
    r_t = _sigmoid(_dot(xcb, wa_ref[0]) + ba_ref[...])
    sp = _softplus_neg(lam_ref[...])
    log_a = (-LRU_C) * r_t * sp
    a = jnp.exp(log_a)
    mult = jnp.sqrt(1.0 - a * a)
    return xcb, i_t, r_t, sp, a, mult


def _conv_from_ext(ext_ref, xa, cw_ref, cb_ref, tc):
    return (cb_ref[...] + cw_ref[3:4, :] * xa + cw_ref[2:3, :] * ext_ref[7:7 + tc, :]
            + cw_ref[1:2, :] * ext_ref[6:6 + tc, :] + cw_ref[0:1, :] * ext_ref[5:5 + tc, :])


def _lru_fwd(proj, conv_w, conv_b, wx_bd, wa_bd, bx, ba, lam, B, S):
    T = B * S
    tc = min(256, S)
    nt = S // tc
    h8 = tc // 8

    def body(xa_ref, halo_ref, ga_ref, cw_ref, cb_ref, wx_ref, wa_ref, bx_ref, ba_ref, lam_ref,
             h_ref, ya_ref, xc_ref, i_ref, r_ref, ext_ref, carry_ref, la_ref, lh_ref, c_ref):
        t = pl.program_id(2)

        @pl.when(t == 0)
        def _():
            carry_ref[...] = jnp.zeros_like(carry_ref)

        xa = xa_ref[...]
        ext_ref[0:8, :] = jnp.where(t == 0, 0.0, halo_ref[...])
        ext_ref[8:8 + tc, :] = xa
        xc = _conv_from_ext(ext_ref, xa, cw_ref, cb_ref, tc)
        _, i_t, r_t, _, a, mult = _lru_gates(xc, wx_ref, wa_ref, bx_ref, ba_ref, lam_ref)
        xc_ref[...] = xc
        i_ref[...] = i_t
        r_ref[...] = r_t
        u = mult * (i_t * xc)
        _scan_tile(a, u, carry_ref[7:8, :], la_ref, lh_ref, c_ref, h_ref, False)
        h = h_ref[...]
        carry_ref[...] = h[tc - 8:tc, :]
        ga = ga_ref[...]
        ya_ref[...] = (ga * _sigmoid(ga) * h).astype(ya_ref.dtype)

    row = lambda b, t: b * nt + t
    vec = pl.BlockSpec((1, CW), lambda b, c, t: (0, c))
    mat = pl.BlockSpec((1, CW, CW), lambda b, c, t: (c, 0, 0))
    return pl.pallas_call(
        body,
        name="lru_fwd",
        grid=(B, N_CT, nt),
        in_specs=[
            pl.BlockSpec((tc, CW), lambda b, c, t: (row(b, t), c)),
            pl.BlockSpec((8, CW), lambda b, c, t: (jnp.maximum(row(b, t) * h8 - 1, 0), c)),
            pl.BlockSpec((tc, CW), lambda b, c, t: (row(b, t), N_CT + c)),
            pl.BlockSpec((CONV, CW), lambda b, c, t: (0, c)),
            vec, mat, mat, vec, vec, vec,
        ],
        out_specs=[pl.BlockSpec((tc, CW), lambda b, c, t: (row(b, t), c))] * 5,
        out_shape=[
            jax.ShapeDtypeStruct((T, D_MODEL), F32),
            jax.ShapeDtypeStruct((T, D_MODEL), _MXU),
            jax.ShapeDtypeStruct((T, D_MODEL), F32),
            jax.ShapeDtypeStruct((T, D_MODEL), F32),
            jax.ShapeDtypeStruct((T, D_MODEL), F32),
        ],
        scratch_shapes=[pltpu.VMEM((tc + 8, CW), F32), pltpu.VMEM((8, CW), F32)] + _scan_scratch(tc)[:3],
        compiler_params=_params(("parallel", "parallel", "arbitrary")),
    )(proj, proj, proj, conv_w, conv_b, wx_bd, wa_bd, bx, ba, lam)


def _lru_bwd(dya, proj, hlru, xc_all, i_all, r_all, conv_w, wx_bd, wa_bd, lam, B, S, deps=()):
    T = B * S
    tc = min(256, S)
    nt = S // tc
    h8 = tc // 8

    def body(dya_ref, xa_ref, ga_ref, h_ref, hhalo_ref, xc_ref, i_ref, r_ref, cw_ref, wx_ref, wa_ref, lam_ref,
             dxa_ref, dga_ref, dcw_ref, dcb_ref, dwx_ref, dwa_ref, dbx_ref, dba_ref, dlam_ref,
             ext2_ref, carry_ref, dhalo_ref, la_ref, lh_ref, c_ref, dh_ref):
        b = pl.program_id(1)
        t = pl.program_id(2)
        tt = nt - 1 - t

        @pl.when(t == 0)
        def _():
            carry_ref[...] = jnp.zeros_like(carry_ref)
            dhalo_ref[...] = jnp.zeros_like(dhalo_ref)

        @pl.when((t == 0) & (b == 0))
        def _():
            for r in (dcw_ref, dcb_ref, dwx_ref, dwa_ref, dbx_ref, dba_ref, dlam_ref):
                r[...] = jnp.zeros_like(r)

        xc, i_t, r_t = xc_ref[...], i_ref[...], r_ref[...]
        xcb = _c(xc)
        sp = _softplus_neg(lam_ref[...])
        a = jnp.exp((-LRU_C) * r_t * sp)
        mult = jnp.sqrt(1.0 - a * a)

        h = h_ref[...]
        ga = ga_ref[...]
        dya_t = dya_ref[...]
        sg = _sigmoid(ga)
        dga_ref[...] = (dya_t * h * (sg * (1.0 + ga * (1.0 - sg)))).astype(dga_ref.dtype)
        dlru = dya_t * (ga * sg)

        row = lax.broadcasted_iota(jnp.int32, a.shape, 0)
        coef = jnp.where(row == tc - 1, 1.0, pltpu.roll(a, tc - 1, 0))
        _scan_tile(coef, dlru, carry_ref[0:1, :], la_ref, lh_ref, c_ref, dh_ref, True)
        dh = dh_ref[...]
        ext2_ref[0:tc, :] = a * dh
        carry_ref[...] = ext2_ref[0:8, :]

        ext2_ref[0:8, :] = jnp.where(tt == 0, 0.0, hhalo_ref[...])
        ext2_ref[8:8 + tc, :] = h
        hprev = ext2_ref[7:7 + tc, :]

        da = dh * hprev
        ix = i_t * xc
        dmult = dh * ix
        di = dh * mult * xc
        dxc = dh * mult * i_t
        dlog_a = da * a - dmult * (a * a) / mult
        dr = dlog_a * ((-LRU_C) * sp)
        dlam_ref[...] += jnp.sum(dlog_a * r_t, axis=0, keepdims=True) * (LRU_C * _sigmoid(-lam_ref[...]))
        dza = dr * r_t * (1.0 - r_t)
        dzx = di * i_t * (1.0 - i_t)
        dzab = _c(dza)
        dzxb = _c(dzx)
        dxc = dxc + _dot_nt(dzxb, wx_ref[0]) + _dot_nt(dzab, wa_ref[0])
        dwx_ref[0] += _dot_tn(xcb, dzxb)
        dwa_ref[0] += _dot_tn(xcb, dzab)
        dbx_ref[...] += jnp.sum(dzx, axis=0, keepdims=True)
        dba_ref[...] += jnp.sum(dza, axis=0, keepdims=True)

        xa = xa_ref[...]
        ext2_ref[0:tc, :] = dxc
        ext2_ref[tc:tc + 8, :] = dhalo_ref[...]
        later = [dxc] + [ext2_ref[j:j + tc, :] for j in (1, 2, 3)]
        dcb_ref[...] += jnp.sum(dxc, axis=0, keepdims=True)
        dxa = jnp.zeros_like(dxc)
        for j, shifted in enumerate(later):
            k = CONV - 1 - j
            dcw_ref[k:k + 1, :] += jnp.sum(shifted * xa, axis=0, keepdims=True)
            dxa = dxa + cw_ref[k:k + 1, :] * shifted
        dxa_ref[...] = dxa.astype(dxa_ref.dtype)
        dhalo_ref[...] = ext2_ref[0:8, :]

    row_of = lambda b, t: b * nt + (nt - 1 - t)
    tile = lambda off: pl.BlockSpec((tc, CW), lambda c, b, t: (row_of(b, t), off + c))
    halo = pl.BlockSpec((8, CW), lambda c, b, t: (jnp.maximum(row_of(b, t) * h8 - 1, 0), c))
    vec = pl.BlockSpec((1, CW), lambda c, b, t: (0, c))
    mat = pl.BlockSpec((1, CW, CW), lambda c, b, t: (c, 0, 0))
    cwspec = pl.BlockSpec((CONV, CW), lambda c, b, t: (0, c))
    return pl.pallas_call(
        _after(body, 12, deps),
        name="lru_bwd",
        grid=(N_CT, B, nt),
        in_specs=[tile(0), tile(0), tile(N_CT), tile(0), halo, tile(0), tile(0), tile(0), cwspec, mat, mat, vec]
        + [ANY_SPEC] * len(deps),
        out_specs=[tile(0), tile(0), cwspec, vec, mat, mat, vec, vec, vec],
        out_shape=[
            jax.ShapeDtypeStruct((T, D_MODEL), _MXU),
            jax.ShapeDtypeStruct((T, D_MODEL), _MXU),
            jax.ShapeDtypeStruct((CONV, D_MODEL), F32),
            jax.ShapeDtypeStruct((1, D_MODEL), F32),
            jax.ShapeDtypeStruct((N_CT, CW, CW), F32),
            jax.ShapeDtypeStruct((N_CT, CW, CW), F32),
            jax.ShapeDtypeStruct((1, D_MODEL), F32),
            jax.ShapeDtypeStruct((1, D_MODEL), F32),
            jax.ShapeDtypeStruct((1, D_MODEL), F32),
        ],
        scratch_shapes=[pltpu.VMEM((tc + 8, CW), F32), pltpu.VMEM((8, CW), F32), pltpu.VMEM((8, CW), F32)]
        + _scan_scratch(tc),
        compiler_params=_params(("parallel", "arbitrary", "arbitrary")),
    )(dya, proj, proj, hlru, hlru, xc_all, i_all, r_all, conv_w, wx_bd, wa_bd, lam, *deps)


def _retention_tables(S):
    half = DK // 2
    freqs = ROPE_THETA ** (-jnp.arange(half, dtype=F32) / half)
    ang = jnp.arange(S, dtype=F32)[:, None] * freqs[None, :]
    log_g = jnp.log1p(-(2.0 ** (-5.0 - jnp.arange(HEADS, dtype=F32))))
    idx = jnp.arange(CHUNK, dtype=F32)
    diff = idx[:, None] - idx[None, :]
    inner = jnp.where(diff >= 0, jnp.exp(jnp.maximum(diff, 0.0)[None] * log_g[:, None, None]), 0.0)
    cross = jnp.exp((idx[None, :] + 1.0) * log_g[:, None])[:, :, None]
    state = jnp.exp((CHUNK - 1.0 - idx[None, :]) * log_g[:, None])[:, :, None]
    gam = jnp.broadcast_to(jnp.exp(CHUNK * log_g)[:, None, None], (HEADS, 1, DK))
    return jnp.cos(ang), jnp.sin(ang), inner, cross, state, gam


def _rot(x, cos, sin):
    half = DK // 2
    x1, x2 = x[:, :half], x[:, half:]
    return jnp.concatenate([x1 * cos - x2 * sin, x1 * sin + x2 * cos], axis=-1)


def _rot_t(y, cos, sin):
    half = DK // 2
    y1, y2 = y[:, :half], y[:, half:]
    return jnp.concatenate([y1 * cos + y2 * sin, y2 * cos - y1 * sin], axis=-1)


def _groupnorm(o):
    mu = jnp.mean(o, axis=-1, keepdims=True)
    oc = o - mu
    rs = lax.rsqrt(jnp.mean(oc * oc, axis=-1, keepdims=True) + EPS)
    return oc * rs, rs


def _ret_specs(B, chunk_of):
    qkv = lambda g: pl.BlockSpec((B, CHUNK, D_MODEL), lambda c: (0, chunk_of(c), g))
    act = pl.BlockSpec((B, CHUNK, D_MODEL), lambda c: (0, chunk_of(c), 0))
    rope = pl.BlockSpec((CHUNK, DK // 2), lambda c: (chunk_of(c), 0))
    dmat = pl.BlockSpec((HEADS, CHUNK, CHUNK), lambda c: (0, 0, 0))
    dvec = pl.BlockSpec((HEADS, CHUNK, 1), lambda c: (0, 0, 0))
    hrow = pl.BlockSpec((HEADS, 1, DK), lambda c: (0, 0, 0))
    rst = pl.BlockSpec((1, B, HEADS, DK, DK), lambda c: (chunk_of(c), 0, 0, 0, 0))
    return qkv, act, rope, dmat, dvec, hrow, rst


def _ret_fwd(proj, tables, gain3, B, S):
    T = B * S
    nc = S // CHUNK
    cos, sin, dmat_t, cd_t, sd_t, gam_t = tables

    def body(q_ref, k_ref, v_ref, gb_ref, cos_ref, sin_ref, dm_ref, cd_ref, sd_ref, gam_ref, gain_ref,
             o_ref, yb_ref, rs_ref, state_ref):
        @pl.when(pl.program_id(0) == 0)
        def _():
            state_ref[...] = jnp.zeros_like(state_ref)

        cos_t, sin_t = cos_ref[...], sin_ref[...]
        for b, h in [(b, h) for b in range(B) for h in range(HEADS)]:
            cols = slice(h * DK, (h + 1) * DK)
            qb = _c(_rot(q_ref[b, :, cols], cos_t, sin_t))
            kb = _c(_rot(k_ref[b, :, cols], cos_t, sin_t) * (DK ** -0.5))
            v = v_ref[b, :, cols]
            state = state_ref[b, h]
            sb = _c(state)
            rs_ref[0, b, h] = sb
            scores = _dot_nt(qb, kb) * dm_ref[h]
            o = _dot(_c(scores), _c(v)) + _dot(qb, sb) * cd_ref[h]
            state_ref[b, h] = gam_ref[h] * state + _dot_tn(kb, _c(v * sd_ref[h]))
            o_ref[b, :, cols] = o
            n, _ = _groupnorm(o)
            gb = gb_ref[b, :, cols]
            yb_ref[b, :, cols] = (gb * _sigmoid(gb) * (n * gain_ref[h])).astype(yb_ref.dtype)

    qkv, act, rope, dmat, dvec, hrow, rst = _ret_specs(B, lambda c: c)
    proj3 = proj.reshape(B, S, proj.shape[1])
    o_pre, yb, states = pl.pallas_call(
        body,
        name="ret_fwd",
        grid=(nc,),
        in_specs=[qkv(2), qkv(3), qkv(4), qkv(5), rope, rope, dmat, dvec, dvec, hrow, hrow],
        out_specs=[act, act, rst],
        out_shape=[
            jax.ShapeDtypeStruct((B, S, D_MODEL), F32),
            jax.ShapeDtypeStruct((B, S, D_MODEL), _MXU),
            jax.ShapeDtypeStruct((nc, B, HEADS, DK, DK), _MXU),
        ],
        scratch_shapes=[pltpu.VMEM((B, HEADS, DK, DK), F32)],
        compiler_params=_params(("arbitrary",)),
    )(proj3, proj3, proj3, proj3, cos, sin, dmat_t, cd_t, sd_t, gam_t, gain3)
    return o_pre.reshape(T, D_MODEL), yb.reshape(T, D_MODEL), states


def _ret_bwd(dyb, o_pre, proj, states, tables, gain3, B, S, deps=()):
    T = B * S
    nc = S // CHUNK
    cos, sin, dmat_t, cd_t, sd_t, gam_t = tables

    def body(dyb_ref, o_ref, q_ref, k_ref, v_ref, gb_ref, rs_ref, cos_ref, sin_ref, dm_ref, cd_ref, sd_ref, gam_ref,
             gain_ref, dr_ref, dgain_ref, dstate_ref):
        @pl.when(pl.program_id(0) == 0)
        def _():
            dstate_ref[...] = jnp.zeros_like(dstate_ref)
            dgain_ref[...] = jnp.zeros_like(dgain_ref)

        cos_t, sin_t = cos_ref[...], sin_ref[...]
        for b, h in [(b, h) for b in range(B) for h in range(HEADS)]:
            cols = slice(h * DK, (h + 1) * DK)
            gain = gain_ref[h]
            n, rs = _groupnorm(o_ref[b, :, cols])
            gb = gb_ref[b, :, cols]
            sg = _sigmoid(gb)
            dy = dyb_ref[b, :, cols]
            part = lambda g: slice(g * D_MODEL + h * DK, g * D_MODEL + (h + 1) * DK)
            dr_ref[b, :, part(3)] = (dy * (n * gain) * (sg * (1.0 + gb * (1.0 - sg)))).astype(dr_ref.dtype)
            dgn = dy * (gb * sg)
            dgain_ref[h] += jnp.sum(dgn * n, axis=0, keepdims=True)
            dn = dgn * gain
            do = rs * (dn - jnp.mean(dn, axis=-1, keepdims=True) - n * jnp.mean(dn * n, axis=-1, keepdims=True))

            qb = _c(_rot(q_ref[b, :, cols], cos_t, sin_t))
            kb = _c(_rot(k_ref[b, :, cols], cos_t, sin_t) * (DK ** -0.5))
            v = v_ref[b, :, cols]
            vb = _c(v)
            vsb = _c(v * sd_ref[h])
            dob = _c(do)
            docb = _c(do * cd_ref[h])
            dmat = dm_ref[h]
            dstate = dstate_ref[b, h]
            dsb = _c(dstate)
            pb = _c(_dot_nt(qb, kb) * dmat)
            dsc = _c(_dot_nt(dob, vb) * dmat)
            dq = _dot(dsc, kb) + _dot_nt(docb, rs_ref[0, b, h])
            dk = _dot_tn(dsc, qb) + _dot_nt(vsb, dsb)
            dv = _dot_tn(pb, dob) + _dot(kb, dsb) * sd_ref[h]
            dstate_ref[b, h] = gam_ref[h] * dstate + _dot_tn(qb, docb)
            dr_ref[b, :, part(0)] = _rot_t(dq, cos_t, sin_t).astype(dr_ref.dtype)
            dr_ref[b, :, part(1)] = (_rot_t(dk, cos_t, sin_t) * (DK ** -0.5)).astype(dr_ref.dtype)
            dr_ref[b, :, part(2)] = dv.astype(dr_ref.dtype)

    qkv, act, rope, dmat, dvec, hrow, rst = _ret_specs(B, lambda c: nc - 1 - c)
    wide = pl.BlockSpec((B, CHUNK, 4 * D_MODEL), lambda c: (0, nc - 1 - c, 0))
    proj3 = proj.reshape(B, S, proj.shape[1])
    dr, dgain = pl.pallas_call(
        _after(body, 14, deps),
        name="ret_bwd",
        grid=(nc,),
        in_specs=[act, act, qkv(2), qkv(3), qkv(4), qkv(5), rst, rope, rope, dmat, dvec, dvec, hrow, hrow]
        + [ANY_SPEC] * len(deps),
        out_specs=[wide, hrow],
        out_shape=[jax.ShapeDtypeStruct((B, S, 4 * D_MODEL), _MXU), jax.ShapeDtypeStruct((HEADS, 1, DK), F32)],
        scratch_shapes=[pltpu.VMEM((B, HEADS, DK, DK), F32)],
        compiler_params=_params(("arbitrary",)),
    )(dyb.reshape(B, S, D_MODEL), o_pre.reshape(B, S, D_MODEL), proj3, proj3, proj3, proj3, states, cos, sin, dmat_t,
      cd_t, sd_t, gam_t, gain3, *deps)
    return dr.reshape(T, 4 * D_MODEL), dgain


def _mid(ya, yb, proj, x2d, tgt2d, wpa, wpb, wout, g_fin):
    T = x2d.shape[0]
    tm = min(256, T)
    n_steps = T // tm
    rows = D_MODEL // (2 * N_CHIPS)

    def body(ya_ref, yb_ref, ma_ref, mb_ref, x_ref, t_ref, gf_ref, wpa_hbm, wpb_hbm, wout_hbm,
             loss_ref, dx2_ref, dya_ref, dyb_ref, dm_ref, dgf_ref, gw_hbm, w_ref, acc_ref, sem):
        i = pl.program_id(0)

        @pl.when(i == 0)
        def _():
            loads = [pltpu.make_async_copy(src, w_ref.at[k], sem.at[k]) for k, src in enumerate((wpa_hbm, wpb_hbm, wout_hbm))]
            for cp in loads:
                cp.start()
            for cp in loads:
                cp.wait()
            acc_ref[...] = jnp.zeros_like(acc_ref)
            loss_ref[...] = jnp.zeros_like(loss_ref)
            dgf_ref[...] = jnp.zeros_like(dgf_ref)

        ya_t, yb_t = ya_ref[...], yb_ref[...]
        out_a = _dot(ya_t, w_ref[0])
        out_b = _dot(yb_t, w_ref[1])
        sa = _sigmoid(ma_ref[...])
        sb = _sigmoid(mb_ref[...])
        mgb = _c(sa * out_a + sb * out_b)
        x2 = x_ref[...] + _dot(mgb, w_ref[2])
        r2 = lax.rsqrt(jnp.mean(x2 * x2, axis=-1, keepdims=True) + EPS)
        nx = x2 * r2
        gf = gf_ref[...]
        err = nx * gf - t_ref[...]
        loss_ref[...] += 0.5 * jnp.sum(jnp.mean(err * err, axis=-1, keepdims=True), axis=0, keepdims=True)
        dy = err * (1.0 / D_MODEL)
        dgf_ref[...] += jnp.sum(dy * nx, axis=0, keepdims=True)
        dyg = dy * gf
        dx2 = r2 * (dyg - nx * jnp.mean(dyg * nx, axis=-1, keepdims=True))
        dx2_ref[...] = dx2
        dx2b = _c(dx2)
        dmg = _dot_nt(dx2b, w_ref[2])
        acc_ref[2] += _dot_tn(mgb, dx2b)
        dm_ref[:, :D_MODEL] = (dmg * out_a * sa * (1.0 - sa)).astype(dm_ref.dtype)
        dm_ref[:, D_MODEL:] = (dmg * out_b * sb * (1.0 - sb)).astype(dm_ref.dtype)
        dab = _c(dmg * sa)
        dbb = _c(dmg * sb)
        dya_ref[...] = _dot_nt(dab, w_ref[0])
        dyb_ref[...] = _dot_nt(dbb, w_ref[1])
        acc_ref[0] += _dot_tn(ya_t, dab)
        acc_ref[1] += _dot_tn(yb_t, dbb)

        @pl.when(i == n_steps - 1)
        def _():
            copies = [pltpu.make_async_copy(acc_ref.at[k, pl.ds((2 * p + hf) * rows, rows), :], gw_hbm.at[p, hf, k],
                                            sem.at[(k * N_CHIPS + p) * 2 + hf])
                      for k in range(3) for p in range(N_CHIPS) for hf in range(2)]
            for cp in copies:
                cp.start()
            for cp in copies:
                cp.wait()

    tile = lambda j: pl.BlockSpec((tm, D_MODEL), lambda i: (i, j))
    one = pl.BlockSpec((1, D_MODEL), lambda i: (0, 0))
    anyspec = pl.BlockSpec(memory_space=pl.ANY)
    return pl.pallas_call(
        body,
        name="mid",
        grid=(n_steps,),
        in_specs=[tile(0), tile(0), tile(6), tile(7), tile(0), tile(0), one, anyspec, anyspec, anyspec],
        out_specs=[pl.BlockSpec((1, 1), lambda i: (0, 0)), tile(0), tile(0), tile(0),
                   pl.BlockSpec((tm, 2 * D_MODEL), lambda i: (i, 0)), one, anyspec],
        out_shape=[
            jax.ShapeDtypeStruct((1, 1), F32),
            jax.ShapeDtypeStruct((T, D_MODEL), F32),
            jax.ShapeDtypeStruct((T, D_MODEL), F32),
            jax.ShapeDtypeStruct((T, D_MODEL), F32),
            jax.ShapeDtypeStruct((T, 2 * D_MODEL), _MXU),
            jax.ShapeDtypeStruct((1, D_MODEL), F32),
            jax.ShapeDtypeStruct((N_CHIPS, 2, 3, rows, D_MODEL), F32),
        ],
        scratch_shapes=[pltpu.VMEM((3, D_MODEL, D_MODEL), _MXU), pltpu.VMEM((3, D_MODEL, D_MODEL), F32),
                        pltpu.SemaphoreType.DMA((3 * N_CHIPS * 2,))],
        compiler_params=_params(("arbitrary",)),
    )(ya, yb, proj, proj, x2d, tgt2d, g_fin, wpa, wpb, wout)


DX_TILE = 512


def _inproj_bwd_dx(dparts, w_all, x2d, dx2, g_in, first, count, prev, name, deps=()):
    T = x2d.shape[0]
    tm = min(DX_TILE, T)
    n_d = len(dparts)
    groups = [(a, k) for a, d in enumerate(dparts) for k in range(d.shape[1] // D_MODEL)]
    dg_start = jnp.zeros((1, D_MODEL), F32) if prev is None else prev[1]
    carried = () if prev is None else (prev[0],)

    def body(*refs):
        d_refs = refs[:n_d]
        x_ref, dx2_ref, g_ref, dg0_ref, w_hbm = refs[n_d:n_d + 5]
        dx_ref, dg_ref, w_ref, sem = refs[-4:]

        @pl.when(pl.program_id(0) == 0)
        def _():
            cp = pltpu.make_async_copy(w_hbm, w_ref, sem)
            cp.start()
            cp.wait()
            dg_ref[...] = dg0_ref[...]

        dh = jnp.zeros((tm, D_MODEL), F32)
        for j, (a, k) in enumerate(groups):
            dh = dh + _dot_nt(d_refs[a][:, k * D_MODEL:(k + 1) * D_MODEL],
                              w_ref[j // 2, :, (j % 2) * D_MODEL:(j % 2 + 1) * D_MODEL])
        x = x_ref[...]
        r = lax.rsqrt(jnp.mean(x * x, axis=-1, keepdims=True) + EPS)
        nx = x * r
        dg_ref[...] += jnp.sum(dh * nx, axis=0, keepdims=True)
        dhg = dh * g_ref[...]
        dx_ref[...] = dx2_ref[...] + r * (dhg - nx * jnp.mean(dhg * nx, axis=-1, keepdims=True))

    tile = pl.BlockSpec((tm, D_MODEL), lambda i: (first + i, 0))
    one = pl.BlockSpec((1, D_MODEL), lambda i: (0, 0))
    return pl.pallas_call(
        body,
        name=name,
        grid=(count,),
        in_specs=[pl.BlockSpec((tm, d.shape[1]), lambda i: (first + i, 0)) for d in dparts]
        + [tile, tile, one, one, ANY_SPEC] + [ANY_SPEC] * (len(carried) + len(deps)),
        out_specs=[tile, one],
        out_shape=[jax.ShapeDtypeStruct((T, D_MODEL), F32), jax.ShapeDtypeStruct((1, D_MODEL), F32)],
        input_output_aliases={n_d + 5: 0} if carried else {},
        scratch_shapes=[pltpu.VMEM(w_all.shape, w_all.dtype), pltpu.SemaphoreType.DMA],
        compiler_params=_params(("arbitrary",)),
    )(*dparts, x2d, dx2, g_in, dg_start, w_all, *carried, *deps)


def _inproj_bwd_dw(ht, dparts, name, deps=()):
    T = ht.shape[1]
    tn = 512
    half = D_MODEL // 2
    per_chip = 2 * D_MODEL // tn
    n_d = len(dparts)
    tiles = [(a, t) for a, d in enumerate(dparts) for t in range(d.shape[1] // tn)]
    offs = [sum(d.shape[1] // tn for d in dparts[:a]) for a in range(n_d)]

    def body(*refs):
        ht_ref = refs[0]
        d_refs = refs[1:1 + n_d]
        out_ref = refs[-1]
        t = pl.program_id(0)

        for a in range(n_d):
            lo, hi = offs[a], offs[a] + dparts[a].shape[1] // tn

            @pl.when((t >= lo) & (t < hi))
            def _(a=a):
                g = _dot(ht_ref[...], d_refs[a][...])
                out_ref[0, 0] = g[:half]
                out_ref[0, 1] = g[half:]

    def dspec(a):
        n_a = dparts[a].shape[1] // tn
        return pl.BlockSpec((T, tn), lambda t: (0, jnp.clip(t - offs[a], 0, n_a - 1)))

    return pl.pallas_call(
        body,
        name=name,
        grid=(len(tiles),),
        in_specs=[pl.BlockSpec((D_MODEL, T), lambda t: (0, 0))] + [dspec(a) for a in range(n_d)]
        + [ANY_SPEC] * len(deps),
        out_specs=pl.BlockSpec((1, 2, half, tn), lambda t: (t // per_chip, 0, 0, t % per_chip)),
        out_shape=jax.ShapeDtypeStruct((len(tiles) // per_chip, 2, half, 2 * D_MODEL), F32),
        compiler_params=_params(("parallel",)),
    )(ht, *dparts, *deps)


def _coords():
    return lax.axis_index("x"), lax.axis_index("y"), lax.axis_index("c")


def _other_chips(x, y):
    return [(1 - x, y), (x, 1 - y), (1 - x, 1 - y)]


def _chunks(rows, n):
    size = rows // n
    return [pl.ds(q * size, size) for q in range(n)]


HBM_SPEC = pl.BlockSpec(memory_space=pltpu.HBM)
SEM_SPEC = pl.BlockSpec(memory_space=pltpu.SEMAPHORE)
DATAFLOW = pltpu.SideEffectType.DATAFLOW_SIDE_EFFECTING


def _copies_start(bufs, plan, n_copies, name):
    n = len(bufs)

    def body(*refs):
        ins = refs[:n]
        send_sems, recv_sems = refs[n], refs[n + 1]
        token = refs[-1]
        for k, send, _ in plan(ins):
            if send is not None:
                src, dst, dev, pred = send
                cp = pltpu.make_async_remote_copy(src_ref=src, dst_ref=dst, send_sem=send_sems.at[k],
                                                  recv_sem=recv_sems.at[k], device_id=dev, device_id_type=MESH)
                if pred is None:
                    cp.start()
                else:
                    pl.when(pred)(cp.start)
        token[...] = jnp.zeros_like(token)

    hbm = [pltpu.with_memory_space_constraint(b, pltpu.HBM) for b in bufs]
    outs = pl.pallas_call(
        body,
        name=name,
        in_specs=[HBM_SPEC] * n,
        out_specs=(SEM_SPEC, SEM_SPEC, *([HBM_SPEC] * n), pl.BlockSpec(memory_space=pltpu.VMEM)),
        out_shape=(pltpu.SemaphoreType.DMA((n_copies,)), pltpu.SemaphoreType.DMA((n_copies,)),
                   *[pltpu.HBM(b.shape, b.dtype) for b in bufs], jax.ShapeDtypeStruct((8, 128), F32)),
        input_output_aliases={a: 2 + a for a in range(n)},
        compiler_params=pltpu.CompilerParams(has_side_effects=DATAFLOW),
    )(*hbm)
    return outs[0], outs[1], list(outs[2:2 + n]), outs[-1]


def _copies_wait(send_sems, recv_sems, bufs, after, plan, name):
    n = len(bufs)

    def body(*refs):
        ins = refs[:n]
        s_sems, r_sems = refs[n], refs[n + 1]
        for k, send, recv in plan(ins):
            if send is not None:
                src, dst, dev, pred = send
                cp = pltpu.make_async_remote_copy(src_ref=src, dst_ref=dst, send_sem=s_sems.at[k],
                                                  recv_sem=r_sems.at[k], device_id=dev, device_id_type=MESH)
                if pred is None:
                    cp.wait_send()
                else:
                    pl.when(pred)(cp.wait_send)
            if recv is not None:
                dst, pred = recv
                cp = pltpu.make_async_remote_copy(src_ref=dst, dst_ref=dst, send_sem=s_sems.at[k],
                                                  recv_sem=r_sems.at[k], device_id=_coords(), device_id_type=MESH)
                if pred is None:
                    cp.wait_recv()
                else:
                    pl.when(pred)(cp.wait_recv)

    outs = pl.pallas_call(
        body,
        name=name,
        in_specs=[HBM_SPEC] * n + [SEM_SPEC, SEM_SPEC, pl.BlockSpec(memory_space=pl.ANY)],
        out_specs=[HBM_SPEC] * n,
        out_shape=[pltpu.HBM(b.shape, b.dtype) for b in bufs],
        input_output_aliases={a: a for a in range(n)},
        compiler_params=pltpu.CompilerParams(has_side_effects=DATAFLOW),
    )(*bufs, send_sems, recv_sems, after)
    return list(outs)


def _gather_plan(n_bufs):
    def plan(refs):
        x, y, c = _coords()
        me = 2 * x + y
        out = []
        for k, (px, py) in enumerate(_other_chips(x, y)):
            for a in range(n_bufs):
                out.append((k * n_bufs + a, (refs[a].at[me], refs[a].at[me], (px, py, c), None),
                            (refs[a].at[2 * px + py], None)))
        return out
    return plan


def _cast_into_slot(ws, name):
    n = len(ws)
    nt = 2

    def body(s_ref, *refs):
        for a in range(n):
            refs[n + a][0] = refs[a][...].astype(refs[n + a].dtype)

    xi, yi, _ = _coords()
    return pl.pallas_call(
        body,
        name=name,
        grid_spec=pltpu.PrefetchScalarGridSpec(
            num_scalar_prefetch=1,
            grid=(2, nt),
            in_specs=[pl.BlockSpec((1, w.shape[1] // nt, w.shape[2]), lambda hf, i, s: (hf, i, 0)) for w in ws],
            out_specs=[pl.BlockSpec((1, 1, w.shape[1] // nt, w.shape[2]), lambda hf, i, s: (s[0], hf, i, 0)) for w in ws],
        ),
        out_shape=[jax.ShapeDtypeStruct((N_CHIPS,) + w.shape, _MXU) for w in ws],
        compiler_params=_params(("parallel", "parallel")),
    )((2 * xi + yi).reshape(1).astype(jnp.int32), *ws)


def _gather_chips(bufs, n_chunks, name):
    n = len(bufs)
    pieces = [(a, rows) for a in range(n) for rows in _chunks(bufs[a].shape[2], n_chunks[a])]
    n_p = len(pieces)

    def body(*refs):
        outs = refs[n:2 * n]
        send_sems, recv_sems, fsend_sems, frecv_sems = refs[2 * n:]
        x, y, c = _coords()
        me = 2 * x + y
        near = [(1 - x, y), (x, 1 - y)]
        slots = [2 * (1 - x) + y, 2 * x + (1 - y), 2 * (1 - x) + (1 - y)]
        pass_to = (jnp.where(c == 0, x, 1 - x), jnp.where(c == 0, 1 - y, y))
        pass_slot = jnp.where(c == 0, slots[0], slots[1])

        def send(k, i, slot, chip):
            a, rows = pieces[i]
            return pltpu.make_async_remote_copy(
                src_ref=outs[a].at[slot, c, rows], dst_ref=outs[a].at[slot, c, rows], send_sem=send_sems.at[k * n_p + i],
                recv_sem=recv_sems.at[k * n_p + i], device_id=(*chip, c), device_id_type=MESH)

        def forward(k, i, half):
            a, rows = pieces[i]
            return pltpu.make_async_remote_copy(
                src_ref=outs[a].at[slots[k], half, rows], dst_ref=outs[a].at[slots[k], half, rows],
                send_sem=fsend_sems.at[k * n_p + i], recv_sem=frecv_sems.at[k * n_p + i],
                device_id=(x, y, 1 - c), device_id_type=MESH)

        started = [send(k, i, me, chip) for i in range(n_p) for k, chip in enumerate(near)]
        for cp in started:
            cp.start()
        for i in range(n_p):
            for k, chip in enumerate(near):
                send(k, i, slots[k], chip).wait_recv()
            later = [send(2, i, pass_slot, pass_to), forward(0, i, c), forward(1, i, c)]
            for cp in later:
                cp.start()
            started += later
        for i in range(n_p):
            send(2, i, slots[2], pass_to).wait_recv()
            fw = forward(2, i, c)
            fw.start()
            started.append(fw)
        for i in range(n_p):
            for k in range(3):
                forward(k, i, 1 - c).wait_recv()
        for cp in started:
            cp.wait_send()

    anyspec = pl.BlockSpec(memory_space=pl.ANY)
    sems = pltpu.SemaphoreType.DMA((3 * n_p,))
    return pl.pallas_call(
        body,
        name=name,
        in_specs=[anyspec] * n,
        out_specs=[anyspec] * n,
        out_shape=[jax.ShapeDtypeStruct(b.shape, b.dtype) for b in bufs],
        input_output_aliases={a: a for a in range(n)},
        scratch_shapes=[sems, sems, sems, sems],
    )(*bufs)


def _swap_plan(n_slabs):
    def plan(refs):
        x, y, c = _coords()
        out, k = [], 0
        for i, n in enumerate(n_slabs):
            g, land = refs[2 * i], refs[2 * i + 1]
            for p in range(n):
                out.append((k, (g.at[p, 1 - c], land.at[p], (x, y, 1 - c), None), (land.at[p], None)))
                k += 1
        return out
    return plan


def _is_one_of(chip, dests):
    hit = chip == dests[0]
    for d in dests[1:]:
        hit = hit | (chip == d)
    return hit


def _slab_of(chip, dests):
    return sum(j * (chip == d).astype(jnp.int32) for j, d in enumerate(dests))


def _scatter_plan(dest_sets):
    def plan(refs):
        x, y, c = _coords()
        me = 2 * x + y
        out = []
        for k, (px, py) in enumerate(_other_chips(x, y)):
            peer = 2 * px + py
            for i, dests in enumerate(dest_sets):
                cs, land = refs[2 * i], refs[2 * i + 1]
                everyone = len(dests) == N_CHIPS
                send = (cs.at[_slab_of(peer, dests)], land.at[k], (px, py, c),
                        None if everyone else _is_one_of(peer, dests))
                recv = (land.at[k], None if everyone else _is_one_of(me, dests))
                out.append((k * len(dest_sets) + i, send, recv))
        return out
    return plan


def _join_plan(rows, n_pieces):
    def plan(refs):
        x, y, c = _coords()
        (buf,) = refs
        return [(i, (buf.at[c, piece], buf.at[c, piece], (x, y, 1 - c), None), (buf.at[1 - c, piece], None))
                for i, piece in enumerate(_chunks(rows, n_pieces))]
    return plan


def _join_plans(parts):
    def plan(refs):
        out, b0, k0 = [], 0, 0
        for part_plan, n_bufs, n_copies in parts:
            out += [(k0 + k, send, recv) for k, send, recv in part_plan(refs[b0:b0 + n_bufs])]
            b0 += n_bufs
            k0 += n_copies
        return out
    return plan


def _allgather_plan():
    def plan(refs):
        x, y, c = _coords()
        (land,) = refs
        me = 4 * x + 2 * y + c
        out = []
        for r in range(1, 8):
            px = 1 - x if r & 4 else x
            py = 1 - y if r & 2 else y
            pc = 1 - c if r & 1 else c
            out.append((r - 1, (land.at[me], land.at[me], (px, py, pc), None), (land.at[4 * px + 2 * py + pc], None)))
        return out
    return plan


def _sum_gathered(land, name):
    def body(land_ref, o_ref):
        acc = land_ref[0]
        for d in range(1, 8):
            acc = acc + land_ref[d]
        o_ref[...] = acc

    return pl.pallas_call(
        body,
        name=name,
        out_shape=jax.ShapeDtypeStruct(land.shape[1:], F32),
        compiler_params=_params(),
    )(land)


def _join_halves(bufs, n_chunks, name, deps=()):
    n = len(bufs)
    pieces = [(a, rows) for a in range(n) for rows in _chunks(bufs[a].shape[1], n_chunks[a])]
    n_p = len(pieces)

    def body(*refs):
        outs = refs[-n - 2:-2]
        send_sems, recv_sems = refs[-2:]
        x, y, c = _coords()

        def copy(i, half):
            a, rows = pieces[i]
            return pltpu.make_async_remote_copy(
                src_ref=outs[a].at[half, rows], dst_ref=outs[a].at[half, rows], send_sem=send_sems.at[i],
                recv_sem=recv_sems.at[i], device_id=(x, y, 1 - c), device_id_type=MESH)

        sends = [copy(i, c) for i in range(n_p)]
        for cp in sends:
            cp.start()
        for i in range(n_p):
            copy(i, 1 - c).wait_recv()
        for cp in sends:
            cp.wait_send()

    anyspec = pl.BlockSpec(memory_space=pl.ANY)
    sems = pltpu.SemaphoreType.DMA((n_p,))
    return pl.pallas_call(
        body,
        name=name,
        in_specs=[anyspec] * (n + len(deps)),
        out_specs=[anyspec] * n,
        out_shape=[jax.ShapeDtypeStruct(b.shape, b.dtype) for b in bufs],
        input_output_aliases={a: a for a in range(n)},
        scratch_shapes=[sems, sems],
    )(*bufs, *deps)


def _row_tile(rows, cap):
    t = cap
    while rows % t:
        t //= 2
    return t


def _add_my_half(g, r, name):
    n_slabs, _, R, C = g.shape
    tr = R if n_slabs > 1 else _row_tile(R, 256)

    def body(c_ref, g_ref, r_ref, o_ref):
        o_ref[...] = (g_ref[0] + r_ref[...]).astype(o_ref.dtype)

    return pl.pallas_call(
        body,
        name=name,
        grid_spec=pltpu.PrefetchScalarGridSpec(
            num_scalar_prefetch=1,
            grid=(n_slabs, R // tr),
            in_specs=[pl.BlockSpec((1, 1, tr, C), lambda p, i, c_ref: (p, c_ref[0], i, 0)),
                      pl.BlockSpec((1, tr, C), lambda p, i, c_ref: (p, i, 0))],
            out_specs=pl.BlockSpec((1, tr, C), lambda p, i, c_ref: (p, i, 0)),
        ),
        out_shape=jax.ShapeDtypeStruct(r.shape, jnp.bfloat16),
        compiler_params=_params(("parallel", "parallel")),
    )(lax.axis_index("c").reshape(1).astype(jnp.int32), g, r)


def _sum_slabs(own, got, name, deps=()):
    _, R, C = own.shape
    tr = _row_tile(R, 256)

    def body(s_ref, own_ref, got_ref, *rest):
        rest[-1][0] = ((own_ref[0].astype(F32) + got_ref[0].astype(F32)) + got_ref[1].astype(F32)) + got_ref[2].astype(F32)

    xi, yi, ci = _coords()
    return pl.pallas_call(
        body,
        name=name,
        grid_spec=pltpu.PrefetchScalarGridSpec(
            num_scalar_prefetch=1,
            grid=(R // tr,),
            in_specs=[pl.BlockSpec((1, tr, C), lambda i, s: (s[0], i, 0)),
                      pl.BlockSpec((3, tr, C), lambda i, s: (0, i, 0))] + [ANY_SPEC] * len(deps),
            out_specs=pl.BlockSpec((1, tr, C), lambda i, s: (s[1], i, 0)),
        ),
        out_shape=jax.ShapeDtypeStruct((2, R, C), F32),
        compiler_params=_params(("parallel",)),
    )(jnp.stack([2 * xi + yi, ci]).astype(jnp.int32), own, got, *deps)


def _sum_parts(owns, got, dest_sets, name):
    n = len(owns)
    _, R, C = owns[0].shape
    tr = _row_tile(R, 256)

    def body(s_ref, *refs):
        got_ref, o_ref = refs[n], refs[-1]
        total = jnp.zeros((tr, C), F32)
        for i in range(n):
            total = total + jnp.where(s_ref[2 + 2 * i] == 1, refs[i][0].astype(F32), 0.0)
        o_ref[0] = ((total + got_ref[0].astype(F32)) + got_ref[1].astype(F32)) + got_ref[2].astype(F32)

    xi, yi, ci = _coords()
    me = 2 * xi + yi
    scalars = [ci, ci]
    for dests in dest_sets:
        scalars += [_is_one_of(me, dests).astype(jnp.int32), _slab_of(me, dests)]
    own_spec = lambda i: pl.BlockSpec((1, tr, C), lambda r, s: (s[3 + 2 * i], r, 0))
    return pl.pallas_call(
        body,
        name=name,
        grid_spec=pltpu.PrefetchScalarGridSpec(
            num_scalar_prefetch=1,
            grid=(R // tr,),
            in_specs=[own_spec(i) for i in range(n)] + [pl.BlockSpec((3, tr, C), lambda r, s: (0, r, 0))],
            out_specs=pl.BlockSpec((1, tr, C), lambda r, s: (s[0], r, 0)),
        ),
        out_shape=jax.ShapeDtypeStruct((2, R, C), F32),
        compiler_params=_params(("parallel",)),
    )(jnp.stack(scalars).astype(jnp.int32), *owns, got)


def _adamw_math(w, g, m, v):
    m = ADAM_B1 * m + (1.0 - ADAM_B1) * g
    v = ADAM_B2 * v + (1.0 - ADAM_B2) * (g * g)
    m_hat = m / (1.0 - ADAM_B1 ** ADAM_STEP)
    v_hat = v / (1.0 - ADAM_B2 ** ADAM_STEP)
    delta = -ADAM_LR * (m_hat / (jnp.sqrt(v_hat) + ADAM_EPS) + ADAM_WD * w)
    return delta, m, v


def _adamw_halves(ws, g, ms, vs, half, prev, name, deps=()):
    n = len(ws)
    _, _, R, C = g.shape
    tr = _row_tile(R, 128)
    steps = R // tr
    carried = [] if prev is None else [a for four in prev for a in four]
    both = half is None
    which = (lambda i, s: i // steps) if both else (lambda i, s: s[0])
    half = 0 if both else half

    def body(s_ref, *refs):
        w_refs, g_refs, m_refs, v_refs = (refs[k * n:(k + 1) * n] for k in range(4))
        outs = refs[len(refs) - 4 * n:]
        for a in range(n):
            grad = g_refs[a][0, 0]
            d, mn, vn = _adamw_math(w_refs[a][...], grad, m_refs[a][...], v_refs[a][...])
            for o, val in zip(outs[4 * a:4 * a + 4], (grad, d, mn, vn)):
                o[...] = val

    rows = pl.BlockSpec((tr, C), lambda i, s: (which(i, s) * steps + i % steps, 0))
    grad_spec = lambda a: pl.BlockSpec((1, 1, tr, C), lambda i, s: (which(i, s), a, i % steps, 0))
    n_in = 4 * n
    outs = pl.pallas_call(
        body,
        name=name,
        grid_spec=pltpu.PrefetchScalarGridSpec(
            num_scalar_prefetch=1,
            grid=(2 * steps if both else steps,),
            in_specs=[rows] * n + [grad_spec(a) for a in range(n)] + [rows] * (2 * n)
            + [ANY_SPEC] * (len(carried) + len(deps)),
            out_specs=[rows] * (4 * n),
        ),
        out_shape=[jax.ShapeDtypeStruct((2 * R, C), F32)] * (4 * n),
        input_output_aliases={1 + n_in + k: k for k in range(len(carried))},
        compiler_params=_params(("parallel",)),
    )(jnp.reshape(half, (1,)).astype(jnp.int32), *ws, *([g] * n), *ms, *vs, *carried, *deps)
    return [outs[4 * a:4 * a + 4] for a in range(n)]


def _adamw_small(ws, gs, ms, vs, name):
    n = len(ws)

    def body(*refs):
        for a in range(n):
            d, mn, vn = _adamw_math(refs[a][...], refs[n + a][...], refs[2 * n + a][...], refs[3 * n + a][...])
            refs[4 * n + a][...] = d
            refs[5 * n + a][...] = mn
            refs[6 * n + a][...] = vn

    shapes = [jax.ShapeDtypeStruct(w.shape, F32) for w in ws]
    outs = pl.pallas_call(
        body,
        name=name,
        out_shape=shapes * 3,
        compiler_params=_params(),
    )(*ws, *gs, *ms, *vs)
    return outs[:n], outs[n:2 * n], outs[2 * n:]


def _to_blockdiag(w):
    per = CW // LRU_BW
    w4 = w.reshape(N_CT, per, LRU_BW, LRU_BW)
    eye = jnp.eye(per, dtype=w.dtype)
    return (w4[:, :, :, None, :] * eye[None, :, None, :, None]).reshape(N_CT, CW, CW)


def _from_blockdiag(g):
    per = CW // LRU_BW
    g5 = g.reshape(N_CT, per, LRU_BW, per, LRU_BW)
    return jnp.stack([g5[:, b, :, b, :] for b in range(per)], axis=1).reshape(LRU_BLOCKS, LRU_BW, LRU_BW)


def _local_grads(x2d, tgt2d, B, S, g_in, w_all, conv_w, conv_b, gate_x_w, gate_x_b, gate_a_w, gate_a_b, lam, gain,
                 proj_weights, g_fin, reduce, deps=()):
    wx_bd = _c(_to_blockdiag(gate_x_w))
    wa_bd = _c(_to_blockdiag(gate_a_w))
    tables = _retention_tables(S)
    gain3 = gain.reshape(HEADS, 1, DK)

    proj, ht = _inproj_fwd(x2d, g_in, w_all, deps)
    hlru, ya, xc, gate_i, gate_r = _lru_fwd(proj, conv_w, conv_b, wx_bd, wa_bd, gate_x_b, gate_a_b, lam, B, S)
    o_pre, yb, states = _ret_fwd(proj, tables, gain3, B, S)
    wpa, wpb, wout = proj_weights(yb)
    loss, dx2, dya, dyb, dm, dgf, gw_proj = _mid(ya, yb, proj, x2d, tgt2d, wpa, wpb, wout, g_fin)
    g3 = _inproj_bwd_dw(ht, [dm], "inproj_bwd_dw_m")
    deps = reduce.m_ready(gw_proj, g3)
    dr, dgain = _ret_bwd(dyb, o_pre, proj, states, tables, gain3, B, S, deps)
    deps = reduce.ret_done(dr)
    g12 = _inproj_bwd_dw(ht, [dr], "inproj_bwd_dw_r", deps)
    deps = reduce.r_ready(g12)
    dxa, dga, dcw, dcb, dwx_bd, dwa_bd, dbx, dba, dlam = _lru_bwd(
        dya, proj, hlru, xc, gate_i, gate_r, conv_w, wx_bd, wa_bd, lam, B, S, deps)
    small = dict(conv_w=dcw, conv_b=dcb, gate_x_w=_from_blockdiag(dwx_bd), gate_x_b=dbx,
                 gate_a_w=_from_blockdiag(dwa_bd), gate_a_b=dba, lru_lambda=dlam, gn_gain=dgain.reshape(HEADS, DK),
                 norm_final=dgf)
    loss_rows = jnp.broadcast_to(loss, (SUBLANES, LANES))
    deps = reduce.lru_done(dxa, jnp.concatenate([_pack_small(small), loss_rows], axis=0))
    g0 = _inproj_bwd_dw(ht, [dxa, dga], "inproj_bwd_dw_a", deps)
    deps = reduce.a_ready(g0)
    n_tiles = x2d.shape[0] // min(DX_TILE, x2d.shape[0])
    grad_x, dgin = _inproj_bwd_dx([dxa, dga, dr, dm], w_all, x2d, dx2, g_in, 0, n_tiles, None, "inproj_bwd_dx", deps)
    return grad_x, dgin


ALL_CHIPS = (0, 1, 2, 3)


class _GradReduce:
    def __init__(self, proj_done):
        self.pending = {}
        self.proj_done = proj_done
        self.land_in = None

    def _start(self, key, parts, name):
        bufs, plans, shared = [], [], None
        for part_bufs, plan, n_copies, part_shared in parts:
            if part_shared is not None:
                shared = len(bufs) + part_shared
            plans.append((plan, len(part_bufs), n_copies))
            bufs += part_bufs
        plan = _join_plans(plans)
        send_sems, recv_sems, bufs, token = _copies_start(bufs, plan, sum(p[2] for p in plans), name + "_start")
        if shared is not None:
            self.land_in = bufs[shared]
        self.pending[key] = (send_sems, recv_sems, bufs, plan, name + "_wait", shared)
        return (token,)

    def _finish(self, key, after):
        send_sems, recv_sems, bufs, plan, name, shared = self.pending.pop(key)
        if shared is not None:
            bufs[shared] = self.land_in
        bufs = _copies_wait(send_sems, recv_sems, bufs, after, plan, name)
        if shared is not None:
            self.land_in = bufs[shared]
        return bufs

    @staticmethod
    def _swap(pieces):
        bufs = []
        for g in pieces:
            bufs += [g, lax.empty((g.shape[0],) + g.shape[2:], F32)]
        n_slabs = [g.shape[0] for g in pieces]
        return bufs, _swap_plan(n_slabs), sum(n_slabs), None

    def _scatter(self, sums, dest_sets):
        bufs = []
        for cs in sums:
            bufs += [cs, lax.empty((3,) + cs.shape[1:], cs.dtype)]
        if self.land_in is not None:
            bufs[-1] = self.land_in
        return bufs, _scatter_plan(dest_sets), 3 * len(sums), len(bufs) - 1

    @staticmethod
    def _gather8(block):
        x, y, c = _coords()
        land = lax.dynamic_update_slice(lax.empty((8,) + block.shape, F32), block[None], (4 * x + 2 * y + c, 0, 0))
        return [land], _allgather_plan(), 7, None

    def m_ready(self, gw_proj, g3):
        rows = gw_proj.shape[2] * gw_proj.shape[3]
        return self._start("m", [self._swap([gw_proj.reshape(N_CHIPS, 2, rows, D_MODEL), g3])], "swap_m")

    def ret_done(self, after):
        proj, land_p, g3, land_3 = self._finish("m", after)
        sums_m = [_add_my_half(proj, land_p, "chip_sum_proj"), _add_my_half(g3, land_3, "chip_sum_m")]
        return self._start("sm", [self._scatter(sums_m, [ALL_CHIPS, (3,)])], "scatter_m")

    def r_ready(self, g12):
        return self._start("r", [self._swap([g12])], "swap_r")

    def lru_done(self, after, packed):
        g12, land_12 = self._finish("r", after)
        sums_r = [_add_my_half(g12, land_12, "chip_sum_r")]
        return (self._start("sr", [self._scatter(sums_r, [(1, 2)])], "scatter_r")
                + self._start("small", [self._gather8(packed)], "gather_small"))

    def a_ready(self, g0):
        (token,) = self._start("a", [self._swap([g0])], "swap_a")
        csp, gotp, self.cs3, _ = self._finish("sm", token)
        half_proj = _sum_slabs(csp, gotp, "sum_w_proj")
        g0, land_0 = self._finish("a", half_proj)
        deps = self._start("sa", [self._scatter([_add_my_half(g0, land_0, "chip_sum_a")], [(0,)])], "scatter_a")
        self.proj_done(_join_halves([half_proj], [4], "join_halves_proj", deps)[0])
        return deps

    def finish(self, dgin, w_in_done):
        (token,) = self._start("n", [self._gather8(dgin)], "gather_norm_in")
        (small,) = self._finish("small", token)
        cs12, _ = self._finish("sr", token)
        cs0, _ = self._finish("sa", token)
        half_in = _sum_parts([self.cs3, cs12, cs0], self.land_in, [(3,), (1, 2), (0,)], "sum_w_in")
        deps = self._start("j", [([half_in], _join_plan(half_in.shape[1], 8), 8, None)], "join_w_in")
        first = w_in_done(self.pending["j"][2][0], True, None, deps)
        (g_in,) = self._finish("j", first[1])
        done = w_in_done(g_in, False, first, ())
        (norm_in,) = self._finish("n", done[1])
        return _sum_gathered(small, "sum_small_grads"), _sum_gathered(norm_in, "sum_norm_in_grad")


_SMALL = ("gate_x_w", "gate_a_w", "conv_w", "conv_b", "gate_x_b", "gate_a_b", "lru_lambda", "gn_gain", "norm_final")
_SMALL_SHAPES = dict(gate_x_w=(LRU_BLOCKS, LRU_BW, LRU_BW), gate_a_w=(LRU_BLOCKS, LRU_BW, LRU_BW),
                     norm_in=(1, D_MODEL), conv_w=(CONV, D_MODEL), conv_b=(1, D_MODEL), gate_x_b=(1, D_MODEL),
                     gate_a_b=(1, D_MODEL), lru_lambda=(1, D_MODEL), gn_gain=(HEADS, DK), norm_final=(1, D_MODEL))


def _pack_small(small):
    return jnp.concatenate([small[k].reshape(-1, 128) for k in _SMALL], axis=0)


def _unpack_small(packed):
    out, r = {}, 0
    for k in _SMALL:
        shape = _SMALL_SHAPES[k]
        rows = 1
        for s in shape:
            rows *= s
        rows //= 128
        out[k] = packed[r:r + rows].reshape(shape)
        r += rows
    return out


def kernel(x, norm_in, w_in, conv_w, conv_b, gate_x_w, gate_x_b, gate_a_w, gate_a_b, lru_lambda, gn_gain, w_proj_a, w_proj_b, w_out, norm_final, loss_target, m_norm_in, m_w_in, m_conv_w, m_conv_b, m_gate_x_w, m_gate_x_b, m_gate_a_w, m_gate_a_b, m_lru_lambda, m_gn_gain, m_w_proj_a, m_w_proj_b, m_w_out, m_norm_final, v_norm_in, v_w_in, v_conv_w, v_conv_b, v_gate_x_w, v_gate_x_b, v_gate_a_w, v_gate_a_b, v_lru_lambda, v_gn_gain, v_w_proj_a, v_w_proj_b, v_w_out, v_norm_final):
    B, S, _ = x.shape
    T = B * S
    xi, yi, ci = _coords()
    chip = 2 * xi + yi

    cshard = D_MODEL // N_CHIPS
    mine = _cast_into_slot([w_in[0].reshape(2, D_MODEL // 2, 2 * D_MODEL)]
                           + [w[0].reshape(2, cshard // 2, D_MODEL) for w in (w_proj_a, w_proj_b, w_out)],
                           "cast_weights")
    plan = _gather_plan(3)
    s_sems, r_sems, pbufs, token = _copies_start(mine[1:], plan, 9, "gather_proj_start")
    gshard = DK // N_CHIPS
    tiny = jnp.concatenate([conv_w[0], jnp.zeros((4, cshard), F32), jnp.pad(gn_gain[0], ((0, 4), (0, cshard - gshard)))],
                           axis=0).reshape(1, 2, SUBLANES, cshard)
    tiny_buf = lax.dynamic_update_slice(lax.empty((N_CHIPS, 2, SUBLANES, cshard), F32), tiny, (chip, 0, 0, 0))
    w_buf, tiny_buf = _gather_chips([mine[0], tiny_buf], [8, 1], "gather_weights")
    w_all = w_buf.reshape(N_CHIPS, D_MODEL, 2 * D_MODEL)
    tiny_all = tiny_buf.reshape(N_CHIPS, 2 * SUBLANES, cshard)

    def proj_weights(after):
        got = _copies_wait(s_sems, r_sems, pbufs, after, plan, "gather_proj_wait")
        return [b.reshape(D_MODEL, D_MODEL) for b in got]

    conv_w_full = jnp.transpose(tiny_all[:, 0:CONV, :], (1, 0, 2)).reshape(CONV, D_MODEL)
    gain_full = jnp.transpose(tiny_all[:, 8:8 + HEADS, :gshard], (1, 0, 2)).reshape(HEADS, DK)

    weights = dict(norm_in=norm_in, w_in=w_in, conv_w=conv_w, conv_b=conv_b, gate_x_w=gate_x_w, gate_x_b=gate_x_b,
                   gate_a_w=gate_a_w, gate_a_b=gate_a_b, lru_lambda=lru_lambda, gn_gain=gn_gain, w_proj_a=w_proj_a,
                   w_proj_b=w_proj_b, w_out=w_out, norm_final=norm_final)
    ms = dict(norm_in=m_norm_in, w_in=m_w_in, conv_w=m_conv_w, conv_b=m_conv_b, gate_x_w=m_gate_x_w,
              gate_x_b=m_gate_x_b, gate_a_w=m_gate_a_w, gate_a_b=m_gate_a_b, lru_lambda=m_lru_lambda, gn_gain=m_gn_gain,
              w_proj_a=m_w_proj_a, w_proj_b=m_w_proj_b, w_out=m_w_out, norm_final=m_norm_final)
    vs = dict(norm_in=v_norm_in, w_in=v_w_in, conv_w=v_conv_w, conv_b=v_conv_b, gate_x_w=v_gate_x_w,
              gate_x_b=v_gate_x_b, gate_a_w=v_gate_a_w, gate_a_b=v_gate_a_b, lru_lambda=v_lru_lambda, gn_gain=v_gn_gain,
              w_proj_a=v_w_proj_a, w_proj_b=v_w_proj_b, w_out=v_w_out, norm_final=v_norm_final)
    names = list(weights)
    grads, delta, new_m, new_v = {}, {}, {}, {}

    def update_big(keys, g, half, prev, name, deps=()):
        two = lambda a: a.reshape(a.shape[1], a.shape[2])
        res = _adamw_halves([two(weights[k]) for k in keys], g, [two(ms[k]) for k in keys], [two(vs[k]) for k in keys],
                            half, prev, name, deps)
        for k, (gk, d, mn, vn) in zip(keys, res):
            shp = weights[k].shape
            grads[k], delta[k], new_m[k], new_v[k] = gk.reshape(shp), d.reshape(shp), mn.reshape(shp), vn.reshape(shp)
        return res

    def proj_done(g_proj):
        g4 = g_proj.reshape(2, 3, D_MODEL // (2 * N_CHIPS), D_MODEL)
        return update_big(("w_proj_a", "w_proj_b", "w_out"), g4, None, None, "adamw_proj")[-1][1]

    def w_in_done(g_in, own, prev, deps):
        g4 = g_in.reshape(2, 1, D_MODEL // 2, 2 * D_MODEL)
        return update_big(("w_in",), g4, ci if own else 1 - ci, None if prev is None else [prev],
                          "adamw_w_in_own" if own else "adamw_w_in_other", deps)[0]

    reduce = _GradReduce(proj_done)
    grad_x, dgin = _local_grads(
        x.reshape(T, D_MODEL), loss_target.reshape(T, D_MODEL), B, S, norm_in, w_all, conv_w_full, conv_b,
        gate_x_w[0], gate_x_b, gate_a_w[0], gate_a_b, lru_lambda, gain_full, proj_weights,
        norm_final.reshape(1, D_MODEL), reduce, deps=(token,))

    small_sum, g_norm_in = reduce.finish(dgin.reshape(SUBLANES, LANES), w_in_done)
    loss = small_sum[small_sum.shape[0] - SUBLANES, 0]

    gsm = _unpack_small(small_sum)
    gsm["norm_in"] = g_norm_in
    gsm["conv_w"] = lax.dynamic_slice_in_dim(gsm["conv_w"], chip * cshard, cshard, axis=1)
    gsm["gn_gain"] = lax.dynamic_slice_in_dim(gsm["gn_gain"], chip * gshard, gshard, axis=1)
    smalls = [k for k in names if k not in delta]

    def view(a):
        return a.reshape(1, -1) if a.ndim == 1 else (a.reshape(a.shape[1:]) if a.ndim > 2 else a)

    ds, mns, vns = _adamw_small([view(weights[k]) for k in smalls], [gsm[k].reshape(view(weights[k]).shape) for k in smalls],
                                [view(ms[k]) for k in smalls], [view(vs[k]) for k in smalls], "adamw_small")
    for k, d, mn, vn in zip(smalls, ds, mns, vns):
        shp = weights[k].shape
        grads[k], delta[k], new_m[k], new_v[k] = gsm[k].reshape(shp), d.reshape(shp), mn.reshape(shp), vn.reshape(shp)

    return (loss, grad_x.reshape(B, S, D_MODEL), *[grads[k] for k in names], *[delta[k] for k in names],
            *[new_m[k] for k in names], *[new_v[k] for k in names])
```

```python
import jax
import jax.numpy as jnp
from jax import lax
from jax.experimental import pallas as pl
from jax.experimental.pallas import tpu as pltpu

F32 = jnp.float32
_MXU = jnp.bfloat16

D_MODEL = 1024
N_GROUPS = 8
HEADS = 4
DK = 256
CHUNK = 128
CONV = 4
LRU_BLOCKS = 16
LRU_BW = 64
LRU_C = 8.0
ROPE_THETA = 10000.0
EPS = 1e-6
CW = 256
N_CT = D_MODEL // CW
N_CHIPS = 4
MESH = pl.DeviceIdType.MESH

ADAM_LR = 0.001
ADAM_B1 = 0.9
ADAM_B2 = 0.999
ADAM_EPS = 1e-08
ADAM_WD = 0.01
ADAM_STEP = 10

VMEM_LIMIT = 56 * 1024 * 1024


def _c(v):
    return v.astype(_MXU)


def _dot(a, b):
    return lax.dot_general(a, b, (((1,), (0,)), ((), ())), preferred_element_type=F32)


def _dot_nt(a, b):
    return lax.dot_general(a, b, (((1,), (1,)), ((), ())), preferred_element_type=F32)


def _dot_tn(a, b):
    return lax.dot_general(a, b, (((0,), (0,)), ((), ())), preferred_element_type=F32)


def _sigmoid(z):
    return 0.5 * jnp.tanh(0.5 * z) + 0.5


ANY_SPEC = pl.BlockSpec(memory_space=pl.ANY)


def _after(body, n_in, deps):
    n_deps = len(deps)

    def wrapped(*refs):
        return body(*refs[:n_in], *refs[n_in + n_deps:])

    return wrapped


def _params(sem=None):
    if sem is None:
        return pltpu.CompilerParams(vmem_limit_bytes=VMEM_LIMIT)
    return pltpu.CompilerParams(vmem_limit_bytes=VMEM_LIMIT, dimension_semantics=sem)


def _inproj_first(x2d, g_in, w_all, chips, name, deps=()):
    T = x2d.shape[0]
    tm = min(1024, T)
    n_i = T // tm

    def body(s_ref, *refs):
        x_ref, g_ref, w_ref = refs[:3]
        proj_ref, hb_ref, ht_ref, h_all = refs[-4:]
        i = pl.program_id(1)
        rows = pl.ds(pl.multiple_of(i * tm, tm), tm)

        @pl.when(pl.program_id(0) == 0)
        def _():
            x = x_ref[...]
            r = lax.rsqrt(jnp.mean(x * x, axis=-1, keepdims=True) + EPS)
            h = x * r * g_ref[...]
            hb = h.astype(h_all.dtype)
            h_all[rows, :] = hb
            hb_ref[...] = hb
            ht_ref[...] = h.T.astype(ht_ref.dtype)

        proj_ref[...] = _dot(h_all[rows, :], w_ref[0])

    first = lambda j, i: jnp.where(j == 0, i, n_i - 1)
    return pl.pallas_call(
        body,
        name=name,
        grid_spec=pltpu.PrefetchScalarGridSpec(
            num_scalar_prefetch=1,
            grid=(2 * chips.shape[0], n_i),
            in_specs=[
                pl.BlockSpec((tm, D_MODEL), lambda j, i, s: (first(j, i), 0)),
                pl.BlockSpec((1, D_MODEL), lambda j, i, s: (0, 0)),
                pl.BlockSpec((1, D_MODEL, D_MODEL), lambda j, i, s: (s[j // 2], 0, j % 2)),
            ] + [ANY_SPEC] * len(deps),
            out_specs=[
                pl.BlockSpec((tm, D_MODEL), lambda j, i, s: (i, 2 * s[j // 2] + j % 2)),
                pl.BlockSpec((tm, D_MODEL), lambda j, i, s: (first(j, i), 0)),
                pl.BlockSpec((D_MODEL, tm), lambda j, i, s: (0, first(j, i))),
            ],
            scratch_shapes=[pltpu.VMEM((T, D_MODEL), _MXU)],
        ),
        out_shape=[
            jax.ShapeDtypeStruct((T, N_GROUPS * D_MODEL), F32),
            jax.ShapeDtypeStruct((T, D_MODEL), _MXU),
            jax.ShapeDtypeStruct((D_MODEL, T), _MXU),
        ],
        compiler_params=_params(("arbitrary", "arbitrary")),
    )(chips, x2d, g_in, w_all, *deps)


def _inproj_more(hb, w_all, chips, proj, name):
    T = hb.shape[0]
    tm = min(1024, T)

    def body(s_ref, hb_ref, w_ref, prev_ref, proj_ref):
        proj_ref[...] = _dot(hb_ref[...], w_ref[0])

    return pl.pallas_call(
        body,
        name=name,
        grid_spec=pltpu.PrefetchScalarGridSpec(
            num_scalar_prefetch=1,
            grid=(2 * chips.shape[0], T // tm),
            in_specs=[
                pl.BlockSpec((tm, D_MODEL), lambda j, i, s: (i, 0)),
                pl.BlockSpec((1, D_MODEL, D_MODEL), lambda j, i, s: (s[j // 2], 0, j % 2)),
                ANY_SPEC,
            ],
            out_specs=pl.BlockSpec((tm, D_MODEL), lambda j, i, s: (i, 2 * s[j // 2] + j % 2)),
        ),
        out_shape=jax.ShapeDtypeStruct(proj.shape, F32),
        input_output_aliases={3: 0},
        compiler_params=_params(("arbitrary", "arbitrary")),
    )(chips, hb, w_all, proj)


def _scan_fwd(a, u):
    n = a.shape[0]
    row = lax.broadcasted_iota(jnp.int32, a.shape, 0)
    s = 1
    while s < n:
        m = row >= s
        u = u + a * jnp.where(m, pltpu.roll(u, s, 0), 0.0)
        a = a * jnp.where(m, pltpu.roll(a, s, 0), 1.0)
        s *= 2
    return a, u


def _scan_bwd(b, g):
    n = b.shape[0]
    row = lax.broadcasted_iota(jnp.int32, b.shape, 0)
    s = 1
    while s < n:
        m = row < n - s
        g = g + b * jnp.where(m, pltpu.roll(g, n - s, 0), 0.0)
        b = b * jnp.where(m, pltpu.roll(b, n - s, 0), 1.0)
        s *= 2
    return b, g


LANES = 128
SUBLANES = 8


def _scan_scratch(tc):
    by_lanes = pltpu.VMEM((CW // LANES, tc, LANES), F32)
    return [by_lanes, by_lanes, pltpu.VMEM((tc // SUBLANES, CW), F32), pltpu.VMEM((tc, CW), F32)]


def _scan_tile(a, u, edge, la_ref, lh_ref, c_ref, dst_ref, reverse):
    n, w = a.shape
    groups = n // SUBLANES
    a3 = a.reshape(groups, SUBLANES, w)
    u3 = u.reshape(groups, SUBLANES, w)
    row = lax.broadcasted_iota(jnp.int32, a3.shape, 1)
    for s in (1, 2, 4):
        m = (row < SUBLANES - s) if reverse else (row >= s)
        shift = SUBLANES - s if reverse else s
        u3 = u3 + a3 * jnp.where(m, pltpu.roll(u3, shift, 1), 0.0)
        a3 = a3 * jnp.where(m, pltpu.roll(a3, shift, 1), 1.0)
    al = a3.reshape(n, w)
    hl = u3.reshape(n, w)
    blocks = w // LANES
    for q in range(blocks):
        la_ref[q] = al[:, q * LANES:(q + 1) * LANES]
        lh_ref[q] = hl[:, q * LANES:(q + 1) * LANES]
    ends = pl.ds(0 if reverse else SUBLANES - 1, groups, stride=SUBLANES)
    end_a = jnp.concatenate([la_ref.at[q][ends, :] for q in range(blocks)], axis=-1)
    end_h = jnp.concatenate([lh_ref.at[q][ends, :] for q in range(blocks)], axis=-1)
    prod, part = (_scan_bwd if reverse else _scan_fwd)(end_a, end_h)
    total = part + prod * edge
    g_row = lax.broadcasted_iota(jnp.int32, total.shape, 0)
    if reverse:
        c_ref[...] = jnp.where(g_row == groups - 1, edge, pltpu.roll(total, groups - 1, 0))
    else:
        c_ref[...] = jnp.where(g_row == 0, edge, pltpu.roll(total, 1, 0))
    for g in range(groups):
        rows = slice(g * SUBLANES, (g + 1) * SUBLANES)
        for q in range(blocks):
            cols = slice(q * LANES, (q + 1) * LANES)
            dst_ref[rows, cols] = lh_ref[q, rows, :] + la_ref[q, rows, :] * c_ref[g:g + 1, cols]


def _softplus_neg(lam):
    z = -lam
    return jnp.maximum(z, 0.0) + jnp.log1p(jnp.exp(-jnp.abs(z)))


def _lru_gates(xc, wx_ref, wa_ref, bx_ref, ba_ref, lam_ref):
    xcb = _c(xc)
    i_t = _sigmoid(_dot(xcb, wx_ref[0]) + bx_ref[...])
    r_t = _sigmoid(_dot(xcb, wa_ref[0]) + ba_ref[...])
    sp = _softplus_neg(lam_ref[...])
    log_a = (-LRU_C) * r_t * sp
    a = jnp.exp(log_a)
    mult = jnp.sqrt(1.0 - a * a)
    return xcb, i_t, r_t, sp, a, mult


def _conv_from_ext(ext_ref, xa, cw_ref, cb_ref, tc):
    return (cb_ref[...] + cw_ref[3:4, :] * xa + cw_ref[2:3, :] * ext_ref[7:7 + tc, :]
            + cw_ref[1:2, :] * ext_ref[6:6 + tc, :] + cw_ref[0:1, :] * ext_ref[5:5 + tc, :])


def _lru_fwd(proj, conv_w, conv_b, wx_bd, wa_bd, bx, ba, lam, B, S):
    T = B * S
    tc = min(256, S)
    nt = S // tc
    h8 = tc // 8

    def body(xa_ref, halo_ref, ga_ref, cw_ref, cb_ref, wx_ref, wa_ref, bx_ref, ba_ref, lam_ref,
             h_ref, ya_ref, ext_ref, carry_ref, la_ref, lh_ref, c_ref):
        t = pl.program_id(2)

        @pl.when(t == 0)
        def _():
            carry_ref[...] = jnp.zeros_like(carry_ref)

        xa = xa_ref[...]
        ext_ref[0:8, :] = jnp.where(t == 0, 0.0, halo_ref[...])
        ext_ref[8:8 + tc, :] = xa
        xc = _conv_from_ext(ext_ref, xa, cw_ref, cb_ref, tc)
        _, i_t, _, _, a, mult = _lru_gates(xc, wx_ref, wa_ref, bx_ref, ba_ref, lam_ref)
        u = mult * (i_t * xc)
        _scan_tile(a, u, carry_ref[7:8, :], la_ref, lh_ref, c_ref, h_ref, False)
        h = h_ref[...]
        carry_ref[...] = h[tc - 8:tc, :]
        ga = ga_ref[...]
        ya_ref[...] = (ga * _sigmoid(ga) * h).astype(ya_ref.dtype)

    row = lambda b, t: b * nt + t
    vec = pl.BlockSpec((1, CW), lambda b, c, t: (0, c))
    mat = pl.BlockSpec((1, CW, CW), lambda b, c, t: (c, 0, 0))
    return pl.pallas_call(
        body,
        name="lru_fwd",
        grid=(B, N_CT, nt),
        in_specs=[
            pl.BlockSpec((tc, CW), lambda b, c, t: (row(b, t), c)),
            pl.BlockSpec((8, CW), lambda b, c, t: (jnp.maximum(row(b, t) * h8 - 1, 0), c)),
            pl.BlockSpec((tc, CW), lambda b, c, t: (row(b, t), N_CT + c)),
            pl.BlockSpec((CONV, CW), lambda b, c, t: (0, c)),
            vec, mat, mat, vec, vec, vec,
        ],
        out_specs=[
            pl.BlockSpec((tc, CW), lambda b, c, t: (row(b, t), c)),
            pl.BlockSpec((tc, CW), lambda b, c, t: (row(b, t), c)),
        ],
        out_shape=[
            jax.ShapeDtypeStruct((T, D_MODEL), F32),
            jax.ShapeDtypeStruct((T, D_MODEL), _MXU),
        ],
        scratch_shapes=[pltpu.VMEM((tc + 8, CW), F32), pltpu.VMEM((8, CW), F32)] + _scan_scratch(tc)[:3],
        compiler_params=_params(("parallel", "parallel", "arbitrary")),
    )(proj, proj, proj, conv_w, conv_b, wx_bd, wa_bd, bx, ba, lam)


def _lru_bwd(dya, proj, hlru, conv_w, conv_b, wx_bd, wa_bd, bx, ba, lam, B, S, deps=()):
    T = B * S
    tc = min(256, S)
    nt = S // tc
    h8 = tc // 8

    def body(dya_ref, xa_ref, xhalo_ref, ga_ref, h_ref, hhalo_ref, cw_ref, cb_ref, wx_ref, wa_ref, bx_ref, ba_ref,
             lam_ref, dxa_ref, dga_ref, dcw_ref, dcb_ref, dwx_ref, dwa_ref, dbx_ref, dba_ref, dlam_ref,
             ext_ref, ext2_ref, carry_ref, dhalo_ref, la_ref, lh_ref, c_ref, dh_ref):
        b = pl.program_id(1)
        t = pl.program_id(2)
        tt = nt - 1 - t

        @pl.when(t == 0)
        def _():
            carry_ref[...] = jnp.zeros_like(carry_ref)
            dhalo_ref[...] = jnp.zeros_like(dhalo_ref)

        @pl.when((t == 0) & (b == 0))
        def _():
            for r in (dcw_ref, dcb_ref, dwx_ref, dwa_ref, dbx_ref, dba_ref, dlam_ref):
                r[...] = jnp.zeros_like(r)

        xa = xa_ref[...]
        ext_ref[0:8, :] = jnp.where(tt == 0, 0.0, xhalo_ref[...])
        ext_ref[8:8 + tc, :] = xa
        xc = _conv_from_ext(ext_ref, xa, cw_ref, cb_ref, tc)
        xcb, i_t, r_t, sp, a, mult = _lru_gates(xc, wx_ref, wa_ref, bx_ref, ba_ref, lam_ref)

        h = h_ref[...]
        ga = ga_ref[...]
        dya_t = dya_ref[...]
        sg = _sigmoid(ga)
        dga_ref[...] = (dya_t * h * (sg * (1.0 + ga * (1.0 - sg)))).astype(dga_ref.dtype)
        dlru = dya_t * (ga * sg)

        row = lax.broadcasted_iota(jnp.int32, a.shape, 0)
        coef = jnp.where(row == tc - 1, 1.0, pltpu.roll(a, tc - 1, 0))
        _scan_tile(coef, dlru, carry_ref[0:1, :], la_ref, lh_ref, c_ref, dh_ref, True)
        dh = dh_ref[...]
        ext2_ref[0:tc, :] = a * dh
        carry_ref[...] = ext2_ref[0:8, :]

        ext2_ref[0:8, :] = jnp.where(tt == 0, 0.0, hhalo_ref[...])
        ext2_ref[8:8 + tc, :] = h
        hprev = ext2_ref[7:7 + tc, :]

        da = dh * hprev
        ix = i_t * xc
        dmult = dh * ix
        di = dh * mult * xc
        dxc = dh * mult * i_t
        dlog_a = da * a - dmult * (a * a) / mult
        dr = dlog_a * ((-LRU_C) * sp)
        dlam_ref[...] += jnp.sum(dlog_a * r_t, axis=0, keepdims=True) * (LRU_C * _sigmoid(-lam_ref[...]))
        dza = dr * r_t * (1.0 - r_t)
        dzx = di * i_t * (1.0 - i_t)
        dzab = _c(dza)
        dzxb = _c(dzx)
        dxc = dxc + _dot_nt(dzxb, wx_ref[0]) + _dot_nt(dzab, wa_ref[0])
        dwx_ref[0] += _dot_tn(xcb, dzxb)
        dwa_ref[0] += _dot_tn(xcb, dzab)
        dbx_ref[...] += jnp.sum(dzx, axis=0, keepdims=True)
        dba_ref[...] += jnp.sum(dza, axis=0, keepdims=True)

        dcb_ref[...] += jnp.sum(dxc, axis=0, keepdims=True)
        dcw_ref[3:4, :] += jnp.sum(dxc * xa, axis=0, keepdims=True)
        dcw_ref[2:3, :] += jnp.sum(dxc * ext_ref[7:7 + tc, :], axis=0, keepdims=True)
        dcw_ref[1:2, :] += jnp.sum(dxc * ext_ref[6:6 + tc, :], axis=0, keepdims=True)
        dcw_ref[0:1, :] += jnp.sum(dxc * ext_ref[5:5 + tc, :], axis=0, keepdims=True)
        ext2_ref[0:tc, :] = dxc
        ext2_ref[tc:tc + 8, :] = dhalo_ref[...]
        dxa = (cw_ref[3:4, :] * dxc + cw_ref[2:3, :] * ext2_ref[1:1 + tc, :]
               + cw_ref[1:2, :] * ext2_ref[2:2 + tc, :] + cw_ref[0:1, :] * ext2_ref[3:3 + tc, :])
        dxa_ref[...] = dxa.astype(dxa_ref.dtype)
        dhalo_ref[...] = ext2_ref[0:8, :]

    row_of = lambda b, t: b * nt + (nt - 1 - t)
    tile = lambda off: pl.BlockSpec((tc, CW), lambda c, b, t: (row_of(b, t), off + c))
    halo = pl.BlockSpec((8, CW), lambda c, b, t: (jnp.maximum(row_of(b, t) * h8 - 1, 0), c))
    vec = pl.BlockSpec((1, CW), lambda c, b, t: (0, c))
    mat = pl.BlockSpec((1, CW, CW), lambda c, b, t: (c, 0, 0))
    cwspec = pl.BlockSpec((CONV, CW), lambda c, b, t: (0, c))
    return pl.pallas_call(
        _after(body, 13, deps),
        name="lru_bwd",
        grid=(N_CT, B, nt),
        in_specs=[tile(0), tile(0), halo, tile(N_CT), tile(0), halo, cwspec, vec, mat, mat, vec, vec, vec]
        + [ANY_SPEC] * len(deps),
        out_specs=[tile(0), tile(0), cwspec, vec, mat, mat, vec, vec, vec],
        out_shape=[
            jax.ShapeDtypeStruct((T, D_MODEL), _MXU),
            jax.ShapeDtypeStruct((T, D_MODEL), _MXU),
            jax.ShapeDtypeStruct((CONV, D_MODEL), F32),
            jax.ShapeDtypeStruct((1, D_MODEL), F32),
            jax.ShapeDtypeStruct((N_CT, CW, CW), F32),
            jax.ShapeDtypeStruct((N_CT, CW, CW), F32),
            jax.ShapeDtypeStruct((1, D_MODEL), F32),
            jax.ShapeDtypeStruct((1, D_MODEL), F32),
            jax.ShapeDtypeStruct((1, D_MODEL), F32),
        ],
        scratch_shapes=[pltpu.VMEM((tc + 8, CW), F32), pltpu.VMEM((tc + 8, CW), F32),
                        pltpu.VMEM((8, CW), F32), pltpu.VMEM((8, CW), F32)] + _scan_scratch(tc),
        compiler_params=_params(("parallel", "arbitrary", "arbitrary")),
    )(dya, proj, proj, proj, hlru, hlru, conv_w, conv_b, wx_bd, wa_bd, bx, ba, lam, *deps)


def _retention_tables(S):
    half = DK // 2
    freqs = ROPE_THETA ** (-jnp.arange(half, dtype=F32) / half)
    ang = jnp.arange(S, dtype=F32)[:, None] * freqs[None, :]
    log_g = jnp.log1p(-(2.0 ** (-5.0 - jnp.arange(HEADS, dtype=F32))))
    idx = jnp.arange(CHUNK, dtype=F32)
    diff = idx[:, None] - idx[None, :]
    inner = jnp.where(diff >= 0, jnp.exp(jnp.maximum(diff, 0.0)[None] * log_g[:, None, None]), 0.0)
    cross = jnp.exp((idx[None, :] + 1.0) * log_g[:, None])[:, :, None]
    state = jnp.exp((CHUNK - 1.0 - idx[None, :]) * log_g[:, None])[:, :, None]
    gam = jnp.broadcast_to(jnp.exp(CHUNK * log_g)[:, None, None], (HEADS, 1, DK))
    return jnp.cos(ang), jnp.sin(ang), inner, cross, state, gam


def _rot(x, cos, sin):
    half = DK // 2
    x1, x2 = x[:, :half], x[:, half:]
    return jnp.concatenate([x1 * cos - x2 * sin, x1 * sin + x2 * cos], axis=-1)


def _rot_t(y, cos, sin):
    half = DK // 2
    y1, y2 = y[:, :half], y[:, half:]
    return jnp.concatenate([y1 * cos + y2 * sin, y2 * cos - y1 * sin], axis=-1)


def _groupnorm(o):
    mu = jnp.mean(o, axis=-1, keepdims=True)
    oc = o - mu
    rs = lax.rsqrt(jnp.mean(oc * oc, axis=-1, keepdims=True) + EPS)
    return oc * rs, rs


def _ret_specs(B, chunk_of):
    qkv = lambda g: pl.BlockSpec((B, CHUNK, D_MODEL), lambda c: (0, chunk_of(c), g))
    act = pl.BlockSpec((B, CHUNK, D_MODEL), lambda c: (0, chunk_of(c), 0))
    rope = pl.BlockSpec((CHUNK, DK // 2), lambda c: (chunk_of(c), 0))
    dmat = pl.BlockSpec((HEADS, CHUNK, CHUNK), lambda c: (0, 0, 0))
    dvec = pl.BlockSpec((HEADS, CHUNK, 1), lambda c: (0, 0, 0))
    hrow = pl.BlockSpec((HEADS, 1, DK), lambda c: (0, 0, 0))
    rst = pl.BlockSpec((1, B, HEADS, DK, DK), lambda c: (chunk_of(c), 0, 0, 0, 0))
    return qkv, act, rope, dmat, dvec, hrow, rst


def _ret_fwd(proj, tables, gain3, B, S):
    T = B * S
    nc = S // CHUNK
    cos, sin, dmat_t, cd_t, sd_t, gam_t = tables

    def body(q_ref, k_ref, v_ref, gb_ref, cos_ref, sin_ref, dm_ref, cd_ref, sd_ref, gam_ref, gain_ref,
             o_ref, yb_ref, rs_ref, state_ref):
        @pl.when(pl.program_id(0) == 0)
        def _():
            state_ref[...] = jnp.zeros_like(state_ref)

        cos_t, sin_t = cos_ref[...], sin_ref[...]
        for b, h in [(b, h) for b in range(B) for h in range(HEADS)]:
            cols = slice(h * DK, (h + 1) * DK)
            qb = _c(_rot(q_ref[b, :, cols], cos_t, sin_t))
            kb = _c(_rot(k_ref[b, :, cols], cos_t, sin_t) * (DK ** -0.5))
            v = v_ref[b, :, cols]
            state = state_ref[b, h]
            sb = _c(state)
            rs_ref[0, b, h] = sb
            scores = _dot_nt(qb, kb) * dm_ref[h]
            o = _dot(_c(scores), _c(v)) + _dot(qb, sb) * cd_ref[h]
            state_ref[b, h] = gam_ref[h] * state + _dot_tn(kb, _c(v * sd_ref[h]))
            o_ref[b, :, cols] = o
            n, _ = _groupnorm(o)
            gb = gb_ref[b, :, cols]
            yb_ref[b, :, cols] = (gb * _sigmoid(gb) * (n * gain_ref[h])).astype(yb_ref.dtype)

    qkv, act, rope, dmat, dvec, hrow, rst = _ret_specs(B, lambda c: c)
    proj3 = proj.reshape(B, S, proj.shape[1])
    o_pre, yb, states = pl.pallas_call(
        body,
        name="ret_fwd",
        grid=(nc,),
        in_specs=[qkv(2), qkv(3), qkv(4), qkv(5), rope, rope, dmat, dvec, dvec, hrow, hrow],
        out_specs=[act, act, rst],
        out_shape=[
            jax.ShapeDtypeStruct((B, S, D_MODEL), F32),
            jax.ShapeDtypeStruct((B, S, D_MODEL), _MXU),
            jax.ShapeDtypeStruct((nc, B, HEADS, DK, DK), _MXU),
        ],
        scratch_shapes=[pltpu.VMEM((B, HEADS, DK, DK), F32)],
        compiler_params=_params(("arbitrary",)),
    )(proj3, proj3, proj3, proj3, cos, sin, dmat_t, cd_t, sd_t, gam_t, gain3)
    return o_pre.reshape(T, D_MODEL), yb.reshape(T, D_MODEL), states


def _ret_bwd(dyb, o_pre, proj, states, tables, gain3, B, S, deps=()):
    T = B * S
    nc = S // CHUNK
    cos, sin, dmat_t, cd_t, sd_t, gam_t = tables

    def body(dyb_ref, o_ref, q_ref, k_ref, v_ref, gb_ref, rs_ref, cos_ref, sin_ref, dm_ref, cd_ref, sd_ref, gam_ref,
             gain_ref, dr_ref, dgain_ref, dstate_ref):
        @pl.when(pl.program_id(0) == 0)
        def _():
            dstate_ref[...] = jnp.zeros_like(dstate_ref)
            dgain_ref[...] = jnp.zeros_like(dgain_ref)

        cos_t, sin_t = cos_ref[...], sin_ref[...]
        for b, h in [(b, h) for b in range(B) for h in range(HEADS)]:
            cols = slice(h * DK, (h + 1) * DK)
            gain = gain_ref[h]
            n, rs = _groupnorm(o_ref[b, :, cols])
            gb = gb_ref[b, :, cols]
            sg = _sigmoid(gb)
            dy = dyb_ref[b, :, cols]
            part = lambda g: slice(g * D_MODEL + h * DK, g * D_MODEL + (h + 1) * DK)
            dr_ref[b, :, part(3)] = (dy * (n * gain) * (sg * (1.0 + gb * (1.0 - sg)))).astype(dr_ref.dtype)
            dgn = dy * (gb * sg)
            dgain_ref[h] += jnp.sum(dgn * n, axis=0, keepdims=True)
            dn = dgn * gain
            do = rs * (dn - jnp.mean(dn, axis=-1, keepdims=True) - n * jnp.mean(dn * n, axis=-1, keepdims=True))

            qb = _c(_rot(q_ref[b, :, cols], cos_t, sin_t))
            kb = _c(_rot(k_ref[b, :, cols], cos_t, sin_t) * (DK ** -0.5))
            v = v_ref[b, :, cols]
            vb = _c(v)
            vsb = _c(v * sd_ref[h])
            dob = _c(do)
            docb = _c(do * cd_ref[h])
            dmat = dm_ref[h]
            dstate = dstate_ref[b, h]
            dsb = _c(dstate)
            pb = _c(_dot_nt(qb, kb) * dmat)
            dsc = _c(_dot_nt(dob, vb) * dmat)
            dq = _dot(dsc, kb) + _dot_nt(docb, rs_ref[0, b, h])
            dk = _dot_tn(dsc, qb) + _dot_nt(vsb, dsb)
            dv = _dot_tn(pb, dob) + _dot(kb, dsb) * sd_ref[h]
            dstate_ref[b, h] = gam_ref[h] * dstate + _dot_tn(qb, docb)
            dr_ref[b, :, part(0)] = _rot_t(dq, cos_t, sin_t).astype(dr_ref.dtype)
            dr_ref[b, :, part(1)] = (_rot_t(dk, cos_t, sin_t) * (DK ** -0.5)).astype(dr_ref.dtype)
            dr_ref[b, :, part(2)] = dv.astype(dr_ref.dtype)

    qkv, act, rope, dmat, dvec, hrow, rst = _ret_specs(B, lambda c: nc - 1 - c)
    wide = pl.BlockSpec((B, CHUNK, 4 * D_MODEL), lambda c: (0, nc - 1 - c, 0))
    proj3 = proj.reshape(B, S, proj.shape[1])
    dr, dgain = pl.pallas_call(
        _after(body, 14, deps),
        name="ret_bwd",
        grid=(nc,),
        in_specs=[act, act, qkv(2), qkv(3), qkv(4), qkv(5), rst, rope, rope, dmat, dvec, dvec, hrow, hrow]
        + [ANY_SPEC] * len(deps),
        out_specs=[wide, hrow],
        out_shape=[jax.ShapeDtypeStruct((B, S, 4 * D_MODEL), _MXU), jax.ShapeDtypeStruct((HEADS, 1, DK), F32)],
        scratch_shapes=[pltpu.VMEM((B, HEADS, DK, DK), F32)],
        compiler_params=_params(("arbitrary",)),
    )(dyb.reshape(B, S, D_MODEL), o_pre.reshape(B, S, D_MODEL), proj3, proj3, proj3, proj3, states, cos, sin, dmat_t,
      cd_t, sd_t, gam_t, gain3, *deps)
    return dr.reshape(T, 4 * D_MODEL), dgain


def _mid(ya, yb, proj, x2d, tgt2d, wpa, wpb, wout, g_fin):
    T = x2d.shape[0]
    tm = min(256, T)
    n_steps = T // tm
    rows = D_MODEL // (2 * N_CHIPS)

    def body(ya_ref, yb_ref, ma_ref, mb_ref, x_ref, t_ref, gf_ref, wpa_hbm, wpb_hbm, wout_hbm,
             loss_ref, dx2_ref, dya_ref, dyb_ref, dm_ref, dgf_ref, gw_hbm, w_ref, acc_ref, sem):
        i = pl.program_id(0)

        @pl.when(i == 0)
        def _():
            loads = [pltpu.make_async_copy(src, w_ref.at[k], sem.at[k]) for k, src in enumerate((wpa_hbm, wpb_hbm, wout_hbm))]
            for cp in loads:
                cp.start()
            for cp in loads:
                cp.wait()
            acc_ref[...] = jnp.zeros_like(acc_ref)
            loss_ref[...] = jnp.zeros_like(loss_ref)
            dgf_ref[...] = jnp.zeros_like(dgf_ref)

        ya_t, yb_t = ya_ref[...], yb_ref[...]
        out_a = _dot(ya_t, w_ref[0])
        out_b = _dot(yb_t, w_ref[1])
        sa = _sigmoid(ma_ref[...])
        sb = _sigmoid(mb_ref[...])
        mgb = _c(sa * out_a + sb * out_b)
        x2 = x_ref[...] + _dot(mgb, w_ref[2])
        r2 = lax.rsqrt(jnp.mean(x2 * x2, axis=-1, keepdims=True) + EPS)
        nx = x2 * r2
        gf = gf_ref[...]
        err = nx * gf - t_ref[...]
        loss_ref[...] += 0.5 * jnp.sum(jnp.mean(err * err, axis=-1, keepdims=True), axis=0, keepdims=True)
        dy = err * (1.0 / D_MODEL)
        dgf_ref[...] += jnp.sum(dy * nx, axis=0, keepdims=True)
        dyg = dy * gf
        dx2 = r2 * (dyg - nx * jnp.mean(dyg * nx, axis=-1, keepdims=True))
        dx2_ref[...] = dx2
        dx2b = _c(dx2)
        dmg = _dot_nt(dx2b, w_ref[2])
        acc_ref[2] += _dot_tn(mgb, dx2b)
        dm_ref[:, :D_MODEL] = (dmg * out_a * sa * (1.0 - sa)).astype(dm_ref.dtype)
        dm_ref[:, D_MODEL:] = (dmg * out_b * sb * (1.0 - sb)).astype(dm_ref.dtype)
        dab = _c(dmg * sa)
        dbb = _c(dmg * sb)
        dya_ref[...] = _dot_nt(dab, w_ref[0])
        dyb_ref[...] = _dot_nt(dbb, w_ref[1])
        acc_ref[0] += _dot_tn(ya_t, dab)
        acc_ref[1] += _dot_tn(yb_t, dbb)

        @pl.when(i == n_steps - 1)
        def _():
            copies = [pltpu.make_async_copy(acc_ref.at[k, pl.ds((2 * p + hf) * rows, rows), :], gw_hbm.at[p, hf, k],
                                            sem.at[(k * N_CHIPS + p) * 2 + hf])
                      for k in range(3) for p in range(N_CHIPS) for hf in range(2)]
            for cp in copies:
                cp.start()
            for cp in copies:
                cp.wait()

    tile = lambda j: pl.BlockSpec((tm, D_MODEL), lambda i: (i, j))
    one = pl.BlockSpec((1, D_MODEL), lambda i: (0, 0))
    anyspec = pl.BlockSpec(memory_space=pl.ANY)
    return pl.pallas_call(
        body,
        name="mid",
        grid=(n_steps,),
        in_specs=[tile(0), tile(0), tile(6), tile(7), tile(0), tile(0), one, anyspec, anyspec, anyspec],
        out_specs=[pl.BlockSpec((1, 1), lambda i: (0, 0)), tile(0), tile(0), tile(0),
                   pl.BlockSpec((tm, 2 * D_MODEL), lambda i: (i, 0)), one, anyspec],
        out_shape=[
            jax.ShapeDtypeStruct((1, 1), F32),
            jax.ShapeDtypeStruct((T, D_MODEL), F32),
            jax.ShapeDtypeStruct((T, D_MODEL), F32),
            jax.ShapeDtypeStruct((T, D_MODEL), F32),
            jax.ShapeDtypeStruct((T, 2 * D_MODEL), _MXU),
            jax.ShapeDtypeStruct((1, D_MODEL), F32),
            jax.ShapeDtypeStruct((N_CHIPS, 2, 3, rows, D_MODEL), F32),
        ],
        scratch_shapes=[pltpu.VMEM((3, D_MODEL, D_MODEL), _MXU), pltpu.VMEM((3, D_MODEL, D_MODEL), F32),
                        pltpu.SemaphoreType.DMA((3 * N_CHIPS * 2,))],
        compiler_params=_params(("arbitrary",)),
    )(ya, yb, proj, proj, x2d, tgt2d, g_fin, wpa, wpb, wout)


DX_TILE = 512


def _inproj_bwd_dx(dparts, w_all, x2d, dx2, g_in, first, count, prev, name, deps=()):
    T = x2d.shape[0]
    tm = min(DX_TILE, T)
    n_d = len(dparts)
    groups = [(a, k) for a, d in enumerate(dparts) for k in range(d.shape[1] // D_MODEL)]
    dg_start = jnp.zeros((1, D_MODEL), F32) if prev is None else prev[1]
    carried = () if prev is None else (prev[0],)

    def body(*refs):
        d_refs = refs[:n_d]
        x_ref, dx2_ref, g_ref, dg0_ref, w_hbm = refs[n_d:n_d + 5]
        dx_ref, dg_ref, w_ref, sem = refs[-4:]

        @pl.when(pl.program_id(0) == 0)
        def _():
            cp = pltpu.make_async_copy(w_hbm, w_ref, sem)
            cp.start()
            cp.wait()
            dg_ref[...] = dg0_ref[...]

        dh = jnp.zeros((tm, D_MODEL), F32)
        for j, (a, k) in enumerate(groups):
            dh = dh + _dot_nt(d_refs[a][:, k * D_MODEL:(k + 1) * D_MODEL],
                              w_ref[j // 2, :, (j % 2) * D_MODEL:(j % 2 + 1) * D_MODEL])
        x = x_ref[...]
        r = lax.rsqrt(jnp.mean(x * x, axis=-1, keepdims=True) + EPS)
        nx = x * r
        dg_ref[...] += jnp.sum(dh * nx, axis=0, keepdims=True)
        dhg = dh * g_ref[...]
        dx_ref[...] = dx2_ref[...] + r * (dhg - nx * jnp.mean(dhg * nx, axis=-1, keepdims=True))

    tile = pl.BlockSpec((tm, D_MODEL), lambda i: (first + i, 0))
    one = pl.BlockSpec((1, D_MODEL), lambda i: (0, 0))
    return pl.pallas_call(
        body,
        name=name,
        grid=(count,),
        in_specs=[pl.BlockSpec((tm, d.shape[1]), lambda i: (first + i, 0)) for d in dparts]
        + [tile, tile, one, one, ANY_SPEC] + [ANY_SPEC] * (len(carried) + len(deps)),
        out_specs=[tile, one],
        out_shape=[jax.ShapeDtypeStruct((T, D_MODEL), F32), jax.ShapeDtypeStruct((1, D_MODEL), F32)],
        input_output_aliases={n_d + 5: 0} if carried else {},
        scratch_shapes=[pltpu.VMEM(w_all.shape, w_all.dtype), pltpu.SemaphoreType.DMA],
        compiler_params=_params(("arbitrary",)),
    )(*dparts, x2d, dx2, g_in, dg_start, w_all, *carried, *deps)


def _inproj_bwd_dw(ht, dparts, name, deps=()):
    T = ht.shape[1]
    tn = 512
    half = D_MODEL // 2
    per_chip = 2 * D_MODEL // tn
    n_d = len(dparts)
    tiles = [(a, t) for a, d in enumerate(dparts) for t in range(d.shape[1] // tn)]
    offs = [sum(d.shape[1] // tn for d in dparts[:a]) for a in range(n_d)]

    def body(*refs):
        ht_ref = refs[0]
        d_refs = refs[1:1 + n_d]
        out_ref = refs[-1]
        t = pl.program_id(0)

        for a in range(n_d):
            lo, hi = offs[a], offs[a] + dparts[a].shape[1] // tn

            @pl.when((t >= lo) & (t < hi))
            def _(a=a):
                g = _dot(ht_ref[...], d_refs[a][...])
                out_ref[0, 0] = g[:half]
                out_ref[0, 1] = g[half:]

    def dspec(a):
        n_a = dparts[a].shape[1] // tn
        return pl.BlockSpec((T, tn), lambda t: (0, jnp.clip(t - offs[a], 0, n_a - 1)))

    return pl.pallas_call(
        body,
        name=name,
        grid=(len(tiles),),
        in_specs=[pl.BlockSpec((D_MODEL, T), lambda t: (0, 0))] + [dspec(a) for a in range(n_d)]
        + [ANY_SPEC] * len(deps),
        out_specs=pl.BlockSpec((1, 2, half, tn), lambda t: (t // per_chip, 0, 0, t % per_chip)),
        out_shape=jax.ShapeDtypeStruct((len(tiles) // per_chip, 2, half, 2 * D_MODEL), F32),
        compiler_params=_params(("parallel",)),
    )(ht, *dparts, *deps)


def _coords():
    return lax.axis_index("x"), lax.axis_index("y"), lax.axis_index("c")


def _other_chips(x, y):
    return [(1 - x, y), (x, 1 - y), (1 - x, 1 - y)]


def _chunks(rows, n):
    size = rows // n
    return [pl.ds(q * size, size) for q in range(n)]


HBM_SPEC = pl.BlockSpec(memory_space=pltpu.HBM)
SEM_SPEC = pl.BlockSpec(memory_space=pltpu.SEMAPHORE)
DATAFLOW = pltpu.SideEffectType.DATAFLOW_SIDE_EFFECTING


def _copies_start(bufs, plan, n_copies, name):
    n = len(bufs)

    def body(*refs):
        ins = refs[:n]
        send_sems, recv_sems = refs[n], refs[n + 1]
        token = refs[-1]
        for k, send, _ in plan(ins):
            if send is not None:
                src, dst, dev, pred = send
                cp = pltpu.make_async_remote_copy(src_ref=src, dst_ref=dst, send_sem=send_sems.at[k],
                                                  recv_sem=recv_sems.at[k], device_id=dev, device_id_type=MESH)
                if pred is None:
                    cp.start()
                else:
                    pl.when(pred)(cp.start)
        token[...] = jnp.zeros_like(token)

    hbm = [pltpu.with_memory_space_constraint(b, pltpu.HBM) for b in bufs]
    outs = pl.pallas_call(
        body,
        name=name,
        in_specs=[HBM_SPEC] * n,
        out_specs=(SEM_SPEC, SEM_SPEC, *([HBM_SPEC] * n), pl.BlockSpec(memory_space=pltpu.VMEM)),
        out_shape=(pltpu.SemaphoreType.DMA((n_copies,)), pltpu.SemaphoreType.DMA((n_copies,)),
                   *[pltpu.HBM(b.shape, b.dtype) for b in bufs], jax.ShapeDtypeStruct((8, 128), F32)),
        input_output_aliases={a: 2 + a for a in range(n)},
        compiler_params=pltpu.CompilerParams(has_side_effects=DATAFLOW),
    )(*hbm)
    return outs[0], outs[1], list(outs[2:2 + n]), outs[-1]


def _copies_wait(send_sems, recv_sems, bufs, after, plan, name, only=None):
    n = len(bufs)

    def body(*refs):
        ins = refs[:n]
        s_sems, r_sems = refs[n], refs[n + 1]
        for k, send, recv in plan(ins):
            if only is not None and k not in only:
                continue
            if send is not None:
                src, dst, dev, pred = send
                cp = pltpu.make_async_remote_copy(src_ref=src, dst_ref=dst, send_sem=s_sems.at[k],
                                                  recv_sem=r_sems.at[k], device_id=dev, device_id_type=MESH)
                if pred is None:
                    cp.wait_send()
                else:
                    pl.when(pred)(cp.wait_send)
            if recv is not None:
                dst, pred = recv
                cp = pltpu.make_async_remote_copy(src_ref=dst, dst_ref=dst, send_sem=s_sems.at[k],
                                                  recv_sem=r_sems.at[k], device_id=_coords(), device_id_type=MESH)
                if pred is None:
                    cp.wait_recv()
                else:
                    pl.when(pred)(cp.wait_recv)

    outs = pl.pallas_call(
        body,
        name=name,
        in_specs=[HBM_SPEC] * n + [SEM_SPEC, SEM_SPEC, pl.BlockSpec(memory_space=pl.ANY)],
        out_specs=[HBM_SPEC] * n,
        out_shape=[pltpu.HBM(b.shape, b.dtype) for b in bufs],
        input_output_aliases={a: a for a in range(n)},
        compiler_params=pltpu.CompilerParams(has_side_effects=DATAFLOW),
    )(*bufs, send_sems, recv_sems, after)
    return list(outs)


def _gather_plan(n_bufs):
    def plan(refs):
        x, y, c = _coords()
        me = 2 * x + y
        out = []
        for k, (px, py) in enumerate(_other_chips(x, y)):
            for a in range(n_bufs):
                out.append((k * n_bufs + a, (refs[a].at[me], refs[a].at[me], (px, py, c), None),
                            (refs[a].at[2 * px + py], None)))
        return out
    return plan


def _cast_into_slot(ws, name):
    n = len(ws)
    nt = 2

    def body(s_ref, *refs):
        for a in range(n):
            refs[n + a][0] = refs[a][...].astype(refs[n + a].dtype)

    xi, yi, _ = _coords()
    return pl.pallas_call(
        body,
        name=name,
        grid_spec=pltpu.PrefetchScalarGridSpec(
            num_scalar_prefetch=1,
            grid=(2, nt),
            in_specs=[pl.BlockSpec((1, w.shape[1] // nt, w.shape[2]), lambda hf, i, s: (hf, i, 0)) for w in ws],
            out_specs=[pl.BlockSpec((1, 1, w.shape[1] // nt, w.shape[2]), lambda hf, i, s: (s[0], hf, i, 0)) for w in ws],
        ),
        out_shape=[jax.ShapeDtypeStruct((N_CHIPS,) + w.shape, _MXU) for w in ws],
        compiler_params=_params(("parallel", "parallel")),
    )((2 * xi + yi).reshape(1).astype(jnp.int32), *ws)


def _chip_gather_plan(stage, n_bufs):
    def plan(refs):
        x, y, c = _coords()
        me = 2 * x + y
        near = [(1 - x, y), (x, 1 - y)]
        slots = [2 * (1 - x) + y, 2 * x + (1 - y), 2 * (1 - x) + (1 - y)]
        sibling = (x, y, 1 - c)
        out = []
        for a, buf in enumerate(refs):
            half = lambda slot, core: buf.at[slot, core]
            if stage == "near":
                for k, chip in enumerate(near):
                    out.append((k * n_bufs + a, (half(me, c), half(me, c), (*chip, c), None), (half(slots[k], c), None)))
            elif stage == "pass":
                pass_to = (jnp.where(c == 0, x, 1 - x), jnp.where(c == 0, 1 - y, y), c)
                pass_slot = jnp.where(c == 0, slots[0], slots[1])
                out.append((a, (half(pass_slot, c), half(pass_slot, c), pass_to, None), (half(slots[2], c), None)))
                for k in range(2):
                    out.append(((1 + k) * n_bufs + a, (half(slots[k], c), half(slots[k], c), sibling, None),
                                (half(slots[k], 1 - c), None)))
            else:
                out.append((a, (half(slots[2], c), half(slots[2], c), sibling, None), (half(slots[2], 1 - c), None)))
        return out
    return plan


def _swap_plan(n_slabs):
    def plan(refs):
        x, y, c = _coords()
        out, k = [], 0
        for i, n in enumerate(n_slabs):
            g, land = refs[2 * i], refs[2 * i + 1]
            for p in range(n):
                out.append((k, (g.at[p, 1 - c], land.at[p], (x, y, 1 - c), None), (land.at[p], None)))
                k += 1
        return out
    return plan


def _is_one_of(chip, dests):
    hit = chip == dests[0]
    for d in dests[1:]:
        hit = hit | (chip == d)
    return hit


def _slab_of(chip, dests):
    return sum(j * (chip == d).astype(jnp.int32) for j, d in enumerate(dests))


def _scatter_plan(dest_sets):
    def plan(refs):
        x, y, c = _coords()
        me = 2 * x + y
        out = []
        for k, (px, py) in enumerate(_other_chips(x, y)):
            peer = 2 * px + py
            for i, dests in enumerate(dest_sets):
                cs, land = refs[2 * i], refs[2 * i + 1]
                everyone = len(dests) == N_CHIPS
                send = (cs.at[_slab_of(peer, dests)], land.at[k], (px, py, c),
                        None if everyone else _is_one_of(peer, dests))
                recv = (land.at[k], None if everyone else _is_one_of(me, dests))
                out.append((k * len(dest_sets) + i, send, recv))
        return out
    return plan


def _join_plan(rows, n_pieces):
    def plan(refs):
        x, y, c = _coords()
        (buf,) = refs
        return [(i, (buf.at[c, piece], buf.at[c, piece], (x, y, 1 - c), None), (buf.at[1 - c, piece], None))
                for i, piece in enumerate(_chunks(rows, n_pieces))]
    return plan


def _join_plans(parts):
    def plan(refs):
        out, b0, k0 = [], 0, 0
        for part_plan, n_bufs, n_copies in parts:
            out += [(k0 + k, send, recv) for k, send, recv in part_plan(refs[b0:b0 + n_bufs])]
            b0 += n_bufs
            k0 += n_copies
        return out
    return plan


def _allgather_plan():
    def plan(refs):
        x, y, c = _coords()
        (land,) = refs
        me = 4 * x + 2 * y + c
        out = []
        for r in range(1, 8):
            px = 1 - x if r & 4 else x
            py = 1 - y if r & 2 else y
            pc = 1 - c if r & 1 else c
            out.append((r - 1, (land.at[me], land.at[me], (px, py, pc), None), (land.at[4 * px + 2 * py + pc], None)))
        return out
    return plan


def _sum_gathered(land, name):
    def body(land_ref, o_ref):
        acc = land_ref[0]
        for d in range(1, 8):
            acc = acc + land_ref[d]
        o_ref[...] = acc

    return pl.pallas_call(
        body,
        name=name,
        out_shape=jax.ShapeDtypeStruct(land.shape[1:], F32),
        compiler_params=_params(),
    )(land)


def _join_halves(bufs, n_chunks, name, deps=()):
    n = len(bufs)
    pieces = [(a, rows) for a in range(n) for rows in _chunks(bufs[a].shape[1], n_chunks[a])]
    n_p = len(pieces)

    def body(*refs):
        outs = refs[-n - 2:-2]
        send_sems, recv_sems = refs[-2:]
        x, y, c = _coords()

        def copy(i, half):
            a, rows = pieces[i]
            return pltpu.make_async_remote_copy(
                src_ref=outs[a].at[half, rows], dst_ref=outs[a].at[half, rows], send_sem=send_sems.at[i],
                recv_sem=recv_sems.at[i], device_id=(x, y, 1 - c), device_id_type=MESH)

        sends = [copy(i, c) for i in range(n_p)]
        for cp in sends:
            cp.start()
        for i in range(n_p):
            copy(i, 1 - c).wait_recv()
        for cp in sends:
            cp.wait_send()

    anyspec = pl.BlockSpec(memory_space=pl.ANY)
    sems = pltpu.SemaphoreType.DMA((n_p,))
    return pl.pallas_call(
        body,
        name=name,
        in_specs=[anyspec] * (n + len(deps)),
        out_specs=[anyspec] * n,
        out_shape=[jax.ShapeDtypeStruct(b.shape, b.dtype) for b in bufs],
        input_output_aliases={a: a for a in range(n)},
        scratch_shapes=[sems, sems],
    )(*bufs, *deps)


def _row_tile(rows, cap):
    t = cap
    while rows % t:
        t //= 2
    return t


def _add_my_half(g, r, name):
    n_slabs, _, R, C = g.shape
    tr = R if n_slabs > 1 else _row_tile(R, 256)

    def body(c_ref, g_ref, r_ref, o_ref):
        o_ref[...] = (g_ref[0] + r_ref[...]).astype(o_ref.dtype)

    return pl.pallas_call(
        body,
        name=name,
        grid_spec=pltpu.PrefetchScalarGridSpec(
            num_scalar_prefetch=1,
            grid=(n_slabs, R // tr),
            in_specs=[pl.BlockSpec((1, 1, tr, C), lambda p, i, c_ref: (p, c_ref[0], i, 0)),
                      pl.BlockSpec((1, tr, C), lambda p, i, c_ref: (p, i, 0))],
            out_specs=pl.BlockSpec((1, tr, C), lambda p, i, c_ref: (p, i, 0)),
        ),
        out_shape=jax.ShapeDtypeStruct(r.shape, jnp.bfloat16),
        compiler_params=_params(("parallel", "parallel")),
    )(lax.axis_index("c").reshape(1).astype(jnp.int32), g, r)


def _sum_slabs(own, got, name, deps=()):
    _, R, C = own.shape
    tr = _row_tile(R, 256)

    def body(s_ref, own_ref, got_ref, *rest):
        rest[-1][0] = ((own_ref[0].astype(F32) + got_ref[0].astype(F32)) + got_ref[1].astype(F32)) + got_ref[2].astype(F32)

    xi, yi, ci = _coords()
    return pl.pallas_call(
        body,
        name=name,
        grid_spec=pltpu.PrefetchScalarGridSpec(
            num_scalar_prefetch=1,
            grid=(R // tr,),
            in_specs=[pl.BlockSpec((1, tr, C), lambda i, s: (s[0], i, 0)),
                      pl.BlockSpec((3, tr, C), lambda i, s: (0, i, 0))] + [ANY_SPEC] * len(deps),
            out_specs=pl.BlockSpec((1, tr, C), lambda i, s: (s[1], i, 0)),
        ),
        out_shape=jax.ShapeDtypeStruct((2, R, C), F32),
        compiler_params=_params(("parallel",)),
    )(jnp.stack([2 * xi + yi, ci]).astype(jnp.int32), own, got, *deps)


def _sum_parts(owns, got, dest_sets, name):
    n = len(owns)
    _, R, C = owns[0].shape
    tr = _row_tile(R, 256)

    def body(s_ref, *refs):
        got_ref, o_ref = refs[n], refs[-1]
        total = jnp.zeros((tr, C), F32)
        for i in range(n):
            total = total + jnp.where(s_ref[2 + 2 * i] == 1, refs[i][0].astype(F32), 0.0)
        o_ref[0] = ((total + got_ref[0].astype(F32)) + got_ref[1].astype(F32)) + got_ref[2].astype(F32)

    xi, yi, ci = _coords()
    me = 2 * xi + yi
    scalars = [ci, ci]
    for dests in dest_sets:
        scalars += [_is_one_of(me, dests).astype(jnp.int32), _slab_of(me, dests)]
    own_spec = lambda i: pl.BlockSpec((1, tr, C), lambda r, s: (s[3 + 2 * i], r, 0))
    return pl.pallas_call(
        body,
        name=name,
        grid_spec=pltpu.PrefetchScalarGridSpec(
            num_scalar_prefetch=1,
            grid=(R // tr,),
            in_specs=[own_spec(i) for i in range(n)] + [pl.BlockSpec((3, tr, C), lambda r, s: (0, r, 0))],
            out_specs=pl.BlockSpec((1, tr, C), lambda r, s: (s[0], r, 0)),
        ),
        out_shape=jax.ShapeDtypeStruct((2, R, C), F32),
        compiler_params=_params(("parallel",)),
    )(jnp.stack(scalars).astype(jnp.int32), *owns, got)


def _adamw_math(w, g, m, v):
    m = ADAM_B1 * m + (1.0 - ADAM_B1) * g
    v = ADAM_B2 * v + (1.0 - ADAM_B2) * (g * g)
    m_hat = m / (1.0 - ADAM_B1 ** ADAM_STEP)
    v_hat = v / (1.0 - ADAM_B2 ** ADAM_STEP)
    delta = -ADAM_LR * (m_hat / (jnp.sqrt(v_hat) + ADAM_EPS) + ADAM_WD * w)
    return delta, m, v


def _adamw_halves(ws, g, ms, vs, half, prev, name, deps=()):
    n = len(ws)
    _, _, R, C = g.shape
    tr = _row_tile(R, 128)
    steps = R // tr
    carried = [] if prev is None else [a for four in prev for a in four]
    both = half is None
    which = (lambda i, s: i // steps) if both else (lambda i, s: s[0])
    half = 0 if both else half

    def body(s_ref, *refs):
        w_refs, g_refs, m_refs, v_refs = (refs[k * n:(k + 1) * n] for k in range(4))
        outs = refs[len(refs) - 4 * n:]
        for a in range(n):
            grad = g_refs[a][0, 0]
            d, mn, vn = _adamw_math(w_refs[a][...], grad, m_refs[a][...], v_refs[a][...])
            for o, val in zip(outs[4 * a:4 * a + 4], (grad, d, mn, vn)):
                o[...] = val

    rows = pl.BlockSpec((tr, C), lambda i, s: (which(i, s) * steps + i % steps, 0))
    grad_spec = lambda a: pl.BlockSpec((1, 1, tr, C), lambda i, s: (which(i, s), a, i % steps, 0))
    n_in = 4 * n
    outs = pl.pallas_call(
        body,
        name=name,
        grid_spec=pltpu.PrefetchScalarGridSpec(
            num_scalar_prefetch=1,
            grid=(2 * steps if both else steps,),
            in_specs=[rows] * n + [grad_spec(a) for a in range(n)] + [rows] * (2 * n)
            + [ANY_SPEC] * (len(carried) + len(deps)),
            out_specs=[rows] * (4 * n),
        ),
        out_shape=[jax.ShapeDtypeStruct((2 * R, C), F32)] * (4 * n),
        input_output_aliases={1 + n_in + k: k for k in range(len(carried))},
        compiler_params=_params(("parallel",)),
    )(jnp.reshape(half, (1,)).astype(jnp.int32), *ws, *([g] * n), *ms, *vs, *carried, *deps)
    return [outs[4 * a:4 * a + 4] for a in range(n)]


def _adamw_small(ws, gs, ms, vs, name):
    n = len(ws)

    def body(*refs):
        for a in range(n):
            d, mn, vn = _adamw_math(refs[a][...], refs[n + a][...], refs[2 * n + a][...], refs[3 * n + a][...])
            refs[4 * n + a][...] = d
            refs[5 * n + a][...] = mn
            refs[6 * n + a][...] = vn

    shapes = [jax.ShapeDtypeStruct(w.shape, F32) for w in ws]
    outs = pl.pallas_call(
        body,
        name=name,
        out_shape=shapes * 3,
        compiler_params=_params(),
    )(*ws, *gs, *ms, *vs)
    return outs[:n], outs[n:2 * n], outs[2 * n:]


def _to_blockdiag(w):
    per = CW // LRU_BW
    w4 = w.reshape(N_CT, per, LRU_BW, LRU_BW)
    eye = jnp.eye(per, dtype=w.dtype)
    return (w4[:, :, :, None, :] * eye[None, :, None, :, None]).reshape(N_CT, CW, CW)


def _from_blockdiag(g):
    per = CW // LRU_BW
    g5 = g.reshape(N_CT, per, LRU_BW, per, LRU_BW)
    return jnp.stack([g5[:, b, :, b, :] for b in range(per)], axis=1).reshape(LRU_BLOCKS, LRU_BW, LRU_BW)


def _local_grads(x2d, tgt2d, B, S, g_in, in_proj, conv_b, gate_x_w, gate_x_b, gate_a_w, gate_a_b, lam,
                 proj_weights, g_fin, reduce):
    wx_bd = _c(_to_blockdiag(gate_x_w))
    wa_bd = _c(_to_blockdiag(gate_a_w))
    tables = _retention_tables(S)

    proj, ht, w_all, conv_w, gain = in_proj(x2d, g_in)
    gain3 = gain.reshape(HEADS, 1, DK)
    hlru, ya = _lru_fwd(proj, conv_w, conv_b, wx_bd, wa_bd, gate_x_b, gate_a_b, lam, B, S)
    o_pre, yb, states = _ret_fwd(proj, tables, gain3, B, S)
    wpa, wpb, wout = proj_weights(yb)
    loss, dx2, dya, dyb, dm, dgf, gw_proj = _mid(ya, yb, proj, x2d, tgt2d, wpa, wpb, wout, g_fin)
    g3 = _inproj_bwd_dw(ht, [dm], "inproj_bwd_dw_m")
    deps = reduce.m_ready(gw_proj, g3)
    dr, dgain = _ret_bwd(dyb, o_pre, proj, states, tables, gain3, B, S, deps)
    deps = reduce.ret_done(dr)
    g12 = _inproj_bwd_dw(ht, [dr], "inproj_bwd_dw_r", deps)
    deps = reduce.r_ready(g12)
    dxa, dga, dcw, dcb, dwx_bd, dwa_bd, dbx, dba, dlam = _lru_bwd(
        dya, proj, hlru, conv_w, conv_b, wx_bd, wa_bd, gate_x_b, gate_a_b, lam, B, S, deps)
    small = dict(conv_w=dcw, conv_b=dcb, gate_x_w=_from_blockdiag(dwx_bd), gate_x_b=dbx,
                 gate_a_w=_from_blockdiag(dwa_bd), gate_a_b=dba, lru_lambda=dlam, gn_gain=dgain.reshape(HEADS, DK),
                 norm_final=dgf)
    loss_rows = jnp.broadcast_to(loss, (SUBLANES, LANES))
    deps = reduce.lru_done(dxa, jnp.concatenate([_pack_small(small), loss_rows], axis=0))
    g0 = _inproj_bwd_dw(ht, [dxa, dga], "inproj_bwd_dw_a", deps)
    deps = reduce.a_ready(g0)
    n_tiles = x2d.shape[0] // min(DX_TILE, x2d.shape[0])
    grad_x, dgin = _inproj_bwd_dx([dxa, dga, dr, dm], w_all, x2d, dx2, g_in, 0, n_tiles, None, "inproj_bwd_dx", deps)
    return grad_x, dgin


ALL_CHIPS = (0, 1, 2, 3)


class _GradReduce:
    def __init__(self, proj_done):
        self.pending = {}
        self.proj_done = proj_done
        self.land_in = None

    def _start(self, key, parts, name):
        bufs, plans, shared = [], [], None
        for part_bufs, plan, n_copies, part_shared in parts:
            if part_shared is not None:
                shared = len(bufs) + part_shared
            plans.append((plan, len(part_bufs), n_copies))
            bufs += part_bufs
        plan = _join_plans(plans)
        send_sems, recv_sems, bufs, token = _copies_start(bufs, plan, sum(p[2] for p in plans), name + "_start")
        if shared is not None:
            self.land_in = bufs[shared]
        self.pending[key] = (send_sems, recv_sems, bufs, plan, name + "_wait", shared)
        return (token,)

    def _finish(self, key, after):
        send_sems, recv_sems, bufs, plan, name, shared = self.pending.pop(key)
        if shared is not None:
            bufs[shared] = self.land_in
        bufs = _copies_wait(send_sems, recv_sems, bufs, after, plan, name)
        if shared is not None:
            self.land_in = bufs[shared]
        return bufs

    @staticmethod
    def _swap(pieces):
        bufs = []
        for g in pieces:
            bufs += [g, lax.empty((g.shape[0],) + g.shape[2:], F32)]
        n_slabs = [g.shape[0] for g in pieces]
        return bufs, _swap_plan(n_slabs), sum(n_slabs), None

    def _scatter(self, sums, dest_sets):
        bufs = []
        for cs in sums:
            bufs += [cs, lax.empty((3,) + cs.shape[1:], cs.dtype)]
        if self.land_in is not None:
            bufs[-1] = self.land_in
        return bufs, _scatter_plan(dest_sets), 3 * len(sums), len(bufs) - 1

    @staticmethod
    def _gather8(block):
        x, y, c = _coords()
        land = lax.dynamic_update_slice(lax.empty((8,) + block.shape, F32), block[None], (4 * x + 2 * y + c, 0, 0))
        return [land], _allgather_plan(), 7, None

    def m_ready(self, gw_proj, g3):
        rows = gw_proj.shape[2] * gw_proj.shape[3]
        return self._start("m", [self._swap([gw_proj.reshape(N_CHIPS, 2, rows, D_MODEL), g3])], "swap_m")

    def ret_done(self, after):
        proj, land_p, g3, land_3 = self._finish("m", after)
        sums_m = [_add_my_half(proj, land_p, "chip_sum_proj"), _add_my_half(g3, land_3, "chip_sum_m")]
        return self._start("sm", [self._scatter(sums_m, [ALL_CHIPS, (3,)])], "scatter_m")

    def r_ready(self, g12):
        return self._start("r", [self._swap([g12])], "swap_r")

    def lru_done(self, after, packed):
        g12, land_12 = self._finish("r", after)
        sums_r = [_add_my_half(g12, land_12, "chip_sum_r")]
        return (self._start("sr", [self._scatter(sums_r, [(1, 2)])], "scatter_r")
                + self._start("small", [self._gather8(packed)], "gather_small"))

    def a_ready(self, g0):
        (token,) = self._start("a", [self._swap([g0])], "swap_a")
        csp, gotp, self.cs3, _ = self._finish("sm", token)
        half_proj = _sum_slabs(csp, gotp, "sum_w_proj")
        g0, land_0 = self._finish("a", half_proj)
        deps = self._start("sa", [self._scatter([_add_my_half(g0, land_0, "chip_sum_a")], [(0,)])], "scatter_a")
        self.proj_done(_join_halves([half_proj], [4], "join_halves_proj", deps)[0])
        return deps

    def finish(self, dgin, w_in_done):
        (token,) = self._start("n", [self._gather8(dgin)], "gather_norm_in")
        (small,) = self._finish("small", token)
        cs12, _ = self._finish("sr", token)
        cs0, _ = self._finish("sa", token)
        half_in = _sum_parts([self.cs3, cs12, cs0], self.land_in, [(3,), (1, 2), (0,)], "sum_w_in")
        deps = self._start("j", [([half_in], _join_plan(half_in.shape[1], 8), 8, None)], "join_w_in")
        first = w_in_done(self.pending["j"][2][0], True, None, deps)
        (g_in,) = self._finish("j", first[1])
        done = w_in_done(g_in, False, first, ())
        (norm_in,) = self._finish("n", done[1])
        return _sum_gathered(small, "sum_small_grads"), _sum_gathered(norm_in, "sum_norm_in_grad")


_SMALL = ("gate_x_w", "gate_a_w", "conv_w", "conv_b", "gate_x_b", "gate_a_b", "lru_lambda", "gn_gain", "norm_final")
_SMALL_SHAPES = dict(gate_x_w=(LRU_BLOCKS, LRU_BW, LRU_BW), gate_a_w=(LRU_BLOCKS, LRU_BW, LRU_BW),
                     norm_in=(1, D_MODEL), conv_w=(CONV, D_MODEL), conv_b=(1, D_MODEL), gate_x_b=(1, D_MODEL),
                     gate_a_b=(1, D_MODEL), lru_lambda=(1, D_MODEL), gn_gain=(HEADS, DK), norm_final=(1, D_MODEL))


def _pack_small(small):
    return jnp.concatenate([small[k].reshape(-1, 128) for k in _SMALL], axis=0)


def _unpack_small(packed):
    out, r = {}, 0
    for k in _SMALL:
        shape = _SMALL_SHAPES[k]
        rows = 1
        for s in shape:
            rows *= s
        rows //= 128
        out[k] = packed[r:r + rows].reshape(shape)
        r += rows
    return out


def kernel(x, norm_in, w_in, conv_w, conv_b, gate_x_w, gate_x_b, gate_a_w, gate_a_b, lru_lambda, gn_gain, w_proj_a, w_proj_b, w_out, norm_final, loss_target, m_norm_in, m_w_in, m_conv_w, m_conv_b, m_gate_x_w, m_gate_x_b, m_gate_a_w, m_gate_a_b, m_lru_lambda, m_gn_gain, m_w_proj_a, m_w_proj_b, m_w_out, m_norm_final, v_norm_in, v_w_in, v_conv_w, v_conv_b, v_gate_x_w, v_gate_x_b, v_gate_a_w, v_gate_a_b, v_lru_lambda, v_gn_gain, v_w_proj_a, v_w_proj_b, v_w_out, v_norm_final):
    B, S, _ = x.shape
    T = B * S
    xi, yi, ci = _coords()
    chip = 2 * xi + yi

    cshard = D_MODEL // N_CHIPS
    mine = _cast_into_slot([w_in[0].reshape(2, D_MODEL // 2, 2 * D_MODEL)]
                           + [w[0].reshape(2, cshard // 2, D_MODEL) for w in (w_proj_a, w_proj_b, w_out)],
                           "cast_weights")
    plan = _gather_plan(3)
    s_sems, r_sems, pbufs, token = _copies_start(mine[1:], plan, 9, "gather_proj_start")
    gshard = DK // N_CHIPS
    tiny = jnp.concatenate([conv_w[0], jnp.zeros((4, cshard), F32), jnp.pad(gn_gain[0], ((0, 4), (0, cshard - gshard)))],
                           axis=0).reshape(1, 2, SUBLANES, cshard)
    tiny_buf = lax.dynamic_update_slice(lax.empty((N_CHIPS, 2, SUBLANES, cshard), F32), tiny, (chip, 0, 0, 0))
    near_plan, pass_plan, far_plan = (_chip_gather_plan(stage, 2) for stage in ("near", "pass", "far"))
    near_s, near_r, bufs, near_token = _copies_start([mine[0], tiny_buf], near_plan, 4, "gather_near_start")

    def in_proj(x2d, g_in):
        as_w = lambda b: b[0].reshape(N_CHIPS, D_MODEL, 2 * D_MODEL)
        slot_x, slot_y, slot_d = 2 * (1 - xi) + yi, 2 * xi + (1 - yi), 2 * (1 - xi) + (1 - yi)
        ids = lambda *chips: jnp.stack(chips).astype(jnp.int32)
        proj, hb, ht = _inproj_first(x2d, g_in, as_w(bufs), ids(chip), "inproj_own", (near_token, token))
        got = _copies_wait(near_s, near_r, bufs, proj, near_plan, "gather_near_wait")
        pass_s, pass_r, got, pass_token = _copies_start(got, pass_plan, 6, "gather_pass_start")
        got = _copies_wait(pass_s, pass_r, got, pass_token, pass_plan, "gather_pass_wait_halves", only={2, 3, 4, 5})
        proj = _inproj_more(hb, as_w(got), ids(slot_x, slot_y), proj, "inproj_near")
        got = _copies_wait(pass_s, pass_r, got, proj, pass_plan, "gather_pass_wait_far", only={0, 1})
        far_s, far_r, got, far_token = _copies_start(got, far_plan, 2, "gather_far_start")
        got = _copies_wait(far_s, far_r, got, far_token, far_plan, "gather_far_wait")
        proj = _inproj_more(hb, as_w(got), ids(slot_d), proj, "inproj_far")
        tiny_all = got[1].reshape(N_CHIPS, 2 * SUBLANES, cshard)
        conv_w_full = jnp.transpose(tiny_all[:, 0:CONV, :], (1, 0, 2)).reshape(CONV, D_MODEL)
        gain_full = jnp.transpose(tiny_all[:, 8:8 + HEADS, :gshard], (1, 0, 2)).reshape(HEADS, DK)
        return proj, ht, as_w(got), conv_w_full, gain_full

    def proj_weights(after):
        got = _copies_wait(s_sems, r_sems, pbufs, after, plan, "gather_proj_wait")
        return [b.reshape(D_MODEL, D_MODEL) for b in got]

    weights = dict(norm_in=norm_in, w_in=w_in, conv_w=conv_w, conv_b=conv_b, gate_x_w=gate_x_w, gate_x_b=gate_x_b,
                   gate_a_w=gate_a_w, gate_a_b=gate_a_b, lru_lambda=lru_lambda, gn_gain=gn_gain, w_proj_a=w_proj_a,
                   w_proj_b=w_proj_b, w_out=w_out, norm_final=norm_final)
    ms = dict(norm_in=m_norm_in, w_in=m_w_in, conv_w=m_conv_w, conv_b=m_conv_b, gate_x_w=m_gate_x_w,
              gate_x_b=m_gate_x_b, gate_a_w=m_gate_a_w, gate_a_b=m_gate_a_b, lru_lambda=m_lru_lambda, gn_gain=m_gn_gain,
              w_proj_a=m_w_proj_a, w_proj_b=m_w_proj_b, w_out=m_w_out, norm_final=m_norm_final)
    vs = dict(norm_in=v_norm_in, w_in=v_w_in, conv_w=v_conv_w, conv_b=v_conv_b, gate_x_w=v_gate_x_w,
              gate_x_b=v_gate_x_b, gate_a_w=v_gate_a_w, gate_a_b=v_gate_a_b, lru_lambda=v_lru_lambda, gn_gain=v_gn_gain,
              w_proj_a=v_w_proj_a, w_proj_b=v_w_proj_b, w_out=v_w_out, norm_final=v_norm_final)
    names = list(weights)
    grads, delta, new_m, new_v = {}, {}, {}, {}

    def update_big(keys, g, half, prev, name, deps=()):
        two = lambda a: a.reshape(a.shape[1], a.shape[2])
        res = _adamw_halves([two(weights[k]) for k in keys], g, [two(ms[k]) for k in keys], [two(vs[k]) for k in keys],
                            half, prev, name, deps)
        for k, (gk, d, mn, vn) in zip(keys, res):
            shp = weights[k].shape
            grads[k], delta[k], new_m[k], new_v[k] = gk.reshape(shp), d.reshape(shp), mn.reshape(shp), vn.reshape(shp)
        return res

    def proj_done(g_proj):
        g4 = g_proj.reshape(2, 3, D_MODEL // (2 * N_CHIPS), D_MODEL)
        return update_big(("w_proj_a", "w_proj_b", "w_out"), g4, None, None, "adamw_proj")[-1][1]

    def w_in_done(g_in, own, prev, deps):
        g4 = g_in.reshape(2, 1, D_MODEL // 2, 2 * D_MODEL)
        return update_big(("w_in",), g4, ci if own else 1 - ci, None if prev is None else [prev],
                          "adamw_w_in_own" if own else "adamw_w_in_other", deps)[0]

    reduce = _GradReduce(proj_done)
    grad_x, dgin = _local_grads(
        x.reshape(T, D_MODEL), loss_target.reshape(T, D_MODEL), B, S, norm_in, in_proj, conv_b,
        gate_x_w[0], gate_x_b, gate_a_w[0], gate_a_b, lru_lambda, proj_weights,
        norm_final.reshape(1, D_MODEL), reduce)

    small_sum, g_norm_in = reduce.finish(dgin.reshape(SUBLANES, LANES), w_in_done)
    loss = small_sum[small_sum.shape[0] - SUBLANES, 0]

    gsm = _unpack_small(small_sum)
    gsm["norm_in"] = g_norm_in
    gsm["conv_w"] = lax.dynamic_slice_in_dim(gsm["conv_w"], chip * cshard, cshard, axis=1)
    gsm["gn_gain"] = lax.dynamic_slice_in_dim(gsm["gn_gain"], chip * gshard, gshard, axis=1)
    smalls = [k for k in names if k not in delta]

    def view(a):
        return a.reshape(1, -1) if a.ndim == 1 else (a.reshape(a.shape[1:]) if a.ndim > 2 else a)

    ds, mns, vns = _adamw_small([view(weights[k]) for k in smalls], [gsm[k].reshape(view(weights[k]).shape) for k in smalls],
                                [view(ms[k]) for k in smalls], [view(vs[k]) for k in smalls], "adamw_small")
    for k, d, mn, vn in zip(smalls, ds, mns, vns):
        shp = weights[k].shape
        grads[k], delta[k], new_m[k], new_v[k] = gsm[k].reshape(shp), d.reshape(shp), mn.reshape(shp), vn.reshape(shp)

    return (loss, grad_x.reshape(B, S, D_MODEL), *[grads[k] for k in names], *[delta[k] for k in names],
            *[new_m[k] for k in names], *[new_v[k] for k in names])
```

```python
import jax
import jax.numpy as jnp
from jax import lax
from jax.experimental import pallas as pl
from jax.experimental.pallas import tpu as pltpu

F32 = jnp.float32
_MXU = jnp.bfloat16

D_MODEL = 1024
N_GROUPS = 8
HEADS = 4
DK = 256
CHUNK = 128
CONV = 4
LRU_BLOCKS = 16
LRU_BW = 64
LRU_C = 8.0
ROPE_THETA = 10000.0
EPS = 1e-6
CW = 256
N_CT = D_MODEL // CW
N_CHIPS = 4
MESH = pl.DeviceIdType.MESH

ADAM_LR = 0.001
ADAM_B1 = 0.9
ADAM_B2 = 0.999
ADAM_EPS = 1e-08
ADAM_WD = 0.01
ADAM_STEP = 10

VMEM_LIMIT = 56 * 1024 * 1024


def _c(v):
    return v.astype(_MXU)


def _dot(a, b):
    return lax.dot_general(a, b, (((1,), (0,)), ((), ())), preferred_element_type=F32)


def _dot_nt(a, b):
    return lax.dot_general(a, b, (((1,), (1,)), ((), ())), preferred_element_type=F32)


def _dot_tn(a, b):
    return lax.dot_general(a, b, (((0,), (0,)), ((), ())), preferred_element_type=F32)


def _sigmoid(z):
    return 0.5 * jnp.tanh(0.5 * z) + 0.5


ANY_SPEC = pl.BlockSpec(memory_space=pl.ANY)


def _after(body, n_in, deps):
    n_deps = len(deps)

    def wrapped(*refs):
        return body(*refs[:n_in], *refs[n_in + n_deps:])

    return wrapped


def _params(sem=None):
    if sem is None:
        return pltpu.CompilerParams(vmem_limit_bytes=VMEM_LIMIT)
    return pltpu.CompilerParams(vmem_limit_bytes=VMEM_LIMIT, dimension_semantics=sem)


def _inproj_first(x2d, g_in, w_all, chips, name, deps=()):
    T = x2d.shape[0]
    tm = min(1024, T)
    n_i = T // tm

    def body(s_ref, *refs):
        x_ref, g_ref, w_ref = refs[:3]
        proj_ref, hb_ref, ht_ref, h_all = refs[-4:]
        i = pl.program_id(1)
        rows = pl.ds(pl.multiple_of(i * tm, tm), tm)

        @pl.when(pl.program_id(0) == 0)
        def _():
            x = x_ref[...]
            r = lax.rsqrt(jnp.mean(x * x, axis=-1, keepdims=True) + EPS)
            h = x * r * g_ref[...]
            hb = h.astype(h_all.dtype)
            h_all[rows, :] = hb
            hb_ref[...] = hb
            ht_ref[...] = h.T.astype(ht_ref.dtype)

        proj_ref[...] = _dot(h_all[rows, :], w_ref[0])

    first = lambda j, i: jnp.where(j == 0, i, n_i - 1)
    return pl.pallas_call(
        body,
        name=name,
        grid_spec=pltpu.PrefetchScalarGridSpec(
            num_scalar_prefetch=1,
            grid=(2 * chips.shape[0], n_i),
            in_specs=[
                pl.BlockSpec((tm, D_MODEL), lambda j, i, s: (first(j, i), 0)),
                pl.BlockSpec((1, D_MODEL), lambda j, i, s: (0, 0)),
                pl.BlockSpec((1, D_MODEL, D_MODEL), lambda j, i, s: (s[j // 2], 0, j % 2)),
            ] + [ANY_SPEC] * len(deps),
            out_specs=[
                pl.BlockSpec((tm, D_MODEL), lambda j, i, s: (i, 2 * s[j // 2] + j % 2)),
                pl.BlockSpec((tm, D_MODEL), lambda j, i, s: (first(j, i), 0)),
                pl.BlockSpec((D_MODEL, tm), lambda j, i, s: (0, first(j, i))),
            ],
            scratch_shapes=[pltpu.VMEM((T, D_MODEL), _MXU)],
        ),
        out_shape=[
            jax.ShapeDtypeStruct((T, N_GROUPS * D_MODEL), F32),
            jax.ShapeDtypeStruct((T, D_MODEL), _MXU),
            jax.ShapeDtypeStruct((D_MODEL, T), _MXU),
        ],
        compiler_params=_params(("arbitrary", "arbitrary")),
    )(chips, x2d, g_in, w_all, *deps)


def _inproj_more(hb, w_all, chips, proj, name):
    T = hb.shape[0]
    tm = min(1024, T)

    def body(s_ref, hb_ref, w_ref, prev_ref, proj_ref):
        proj_ref[...] = _dot(hb_ref[...], w_ref[0])

    return pl.pallas_call(
        body,
        name=name,
        grid_spec=pltpu.PrefetchScalarGridSpec(
            num_scalar_prefetch=1,
            grid=(2 * chips.shape[0], T // tm),
            in_specs=[
                pl.BlockSpec((tm, D_MODEL), lambda j, i, s: (i, 0)),
                pl.BlockSpec((1, D_MODEL, D_MODEL), lambda j, i, s: (s[j // 2], 0, j % 2)),
                ANY_SPEC,
            ],
            out_specs=pl.BlockSpec((tm, D_MODEL), lambda j, i, s: (i, 2 * s[j // 2] + j % 2)),
        ),
        out_shape=jax.ShapeDtypeStruct(proj.shape, F32),
        input_output_aliases={3: 0},
        compiler_params=_params(("arbitrary", "arbitrary")),
    )(chips, hb, w_all, proj)


def _scan_fwd(a, u):
    n = a.shape[0]
    row = lax.broadcasted_iota(jnp.int32, a.shape, 0)
    s = 1
    while s < n:
        m = row >= s
        u = u + a * jnp.where(m, pltpu.roll(u, s, 0), 0.0)
        a = a * jnp.where(m, pltpu.roll(a, s, 0), 1.0)
        s *= 2
    return a, u


def _scan_bwd(b, g):
    n = b.shape[0]
    row = lax.broadcasted_iota(jnp.int32, b.shape, 0)
    s = 1
    while s < n:
        m = row < n - s
        g = g + b * jnp.where(m, pltpu.roll(g, n - s, 0), 0.0)
        b = b * jnp.where(m, pltpu.roll(b, n - s, 0), 1.0)
        s *= 2
    return b, g


LANES = 128
SUBLANES = 8


def _scan_scratch(tc):
    by_lanes = pltpu.VMEM((CW // LANES, tc, LANES), F32)
    return [by_lanes, by_lanes, pltpu.VMEM((tc // SUBLANES, CW), F32), pltpu.VMEM((tc, CW), F32)]


def _scan_tile(a, u, edge, la_ref, lh_ref, c_ref, dst_ref, reverse):
    n, w = a.shape
    groups = n // SUBLANES
    a3 = a.reshape(groups, SUBLANES, w)
    u3 = u.reshape(groups, SUBLANES, w)
    row = lax.broadcasted_iota(jnp.int32, a3.shape, 1)
    for s in (1, 2, 4):
        m = (row < SUBLANES - s) if reverse else (row >= s)
        shift = SUBLANES - s if reverse else s
        u3 = u3 + a3 * jnp.where(m, pltpu.roll(u3, shift, 1), 0.0)
        a3 = a3 * jnp.where(m, pltpu.roll(a3, shift, 1), 1.0)
    al = a3.reshape(n, w)
    hl = u3.reshape(n, w)
    blocks = w // LANES
    for q in range(blocks):
        la_ref[q] = al[:, q * LANES:(q + 1) * LANES]
        lh_ref[q] = hl[:, q * LANES:(q + 1) * LANES]
    ends = pl.ds(0 if reverse else SUBLANES - 1, groups, stride=SUBLANES)
    end_a = jnp.concatenate([la_ref.at[q][ends, :] for q in range(blocks)], axis=-1)
    end_h = jnp.concatenate([lh_ref.at[q][ends, :] for q in range(blocks)], axis=-1)
    prod, part = (_scan_bwd if reverse else _scan_fwd)(end_a, end_h)
    total = part + prod * edge
    g_row = lax.broadcasted_iota(jnp.int32, total.shape, 0)
    if reverse:
        c_ref[...] = jnp.where(g_row == groups - 1, edge, pltpu.roll(total, groups - 1, 0))
    else:
        c_ref[...] = jnp.where(g_row == 0, edge, pltpu.roll(total, 1, 0))
    for g in range(groups):
        rows = slice(g * SUBLANES, (g + 1) * SUBLANES)
        for q in range(blocks):
            cols = slice(q * LANES, (q + 1) * LANES)
            dst_ref[rows, cols] = lh_ref[q, rows, :] + la_ref[q, rows, :] * c_ref[g:g + 1, cols]


def _softplus_neg(lam):
    z = -lam
    return jnp.maximum(z, 0.0) + jnp.log1p(jnp.exp(-jnp.abs(z)))


def _lru_gates(xc, wx_ref, wa_ref, bx_ref, ba_ref, lam_ref):
    xcb = _c(xc)
    i_t = _sigmoid(_dot(xcb, wx_ref[0]) + bx_ref[...])
    r_t = _sigmoid(_dot(xcb, wa_ref[0]) + ba_ref[...])
    sp = _softplus_neg(lam_ref[...])
    log_a = (-LRU_C) * r_t * sp
    a = jnp.exp(log_a)
    mult = jnp.sqrt(1.0 - a * a)
    return xcb, i_t, r_t, sp, a, mult


def _conv_from_ext(ext_ref, xa, cw_ref, cb_ref, tc):
    return (cb_ref[...] + cw_ref[3:4, :] * xa + cw_ref[2:3, :] * ext_ref[7:7 + tc, :]
            + cw_ref[1:2, :] * ext_ref[6:6 + tc, :] + cw_ref[0:1, :] * ext_ref[5:5 + tc, :])


def _lru_fwd(proj, conv_w, conv_b, wx_bd, wa_bd, bx, ba, lam, B, S):
    T = B * S
    tc = min(256, S)
    nt = S // tc
    h8 = tc // 8

    def body(xa_ref, halo_ref, ga_ref, cw_ref, cb_ref, wx_ref, wa_ref, bx_ref, ba_ref, lam_ref,
             h_ref, ya_ref, ext_ref, carry_ref, la_ref, lh_ref, c_ref):
        t = pl.program_id(2)

        @pl.when(t == 0)
        def _():
            carry_ref[...] = jnp.zeros_like(carry_ref)

        xa = xa_ref[...]
        ext_ref[0:8, :] = jnp.where(t == 0, 0.0, halo_ref[...])
        ext_ref[8:8 + tc, :] = xa
        xc = _conv_from_ext(ext_ref, xa, cw_ref, cb_ref, tc)
        _, i_t, _, _, a, mult = _lru_gates(xc, wx_ref, wa_ref, bx_ref, ba_ref, lam_ref)
        u = mult * (i_t * xc)
        _scan_tile(a, u, carry_ref[7:8, :], la_ref, lh_ref, c_ref, h_ref, False)
        h = h_ref[...]
        carry_ref[...] = h[tc - 8:tc, :]
        ga = ga_ref[...]
        ya_ref[...] = (ga * _sigmoid(ga) * h).astype(ya_ref.dtype)

    row = lambda b, t: b * nt + t
    vec = pl.BlockSpec((1, CW), lambda b, c, t: (0, c))
    mat = pl.BlockSpec((1, CW, CW), lambda b, c, t: (c, 0, 0))
    return pl.pallas_call(
        body,
        name="lru_fwd",
        grid=(B, N_CT, nt),
        in_specs=[
            pl.BlockSpec((tc, CW), lambda b, c, t: (row(b, t), c)),
            pl.BlockSpec((8, CW), lambda b, c, t: (jnp.maximum(row(b, t) * h8 - 1, 0), c)),
            pl.BlockSpec((tc, CW), lambda b, c, t: (row(b, t), N_CT + c)),
            pl.BlockSpec((CONV, CW), lambda b, c, t: (0, c)),
            vec, mat, mat, vec, vec, vec,
        ],
        out_specs=[
            pl.BlockSpec((tc, CW), lambda b, c, t: (row(b, t), c)),
            pl.BlockSpec((tc, CW), lambda b, c, t: (row(b, t), c)),
        ],
        out_shape=[
            jax.ShapeDtypeStruct((T, D_MODEL), F32),
            jax.ShapeDtypeStruct((T, D_MODEL), _MXU),
        ],
        scratch_shapes=[pltpu.VMEM((tc + 8, CW), F32), pltpu.VMEM((8, CW), F32)] + _scan_scratch(tc)[:3],
        compiler_params=_params(("parallel", "parallel", "arbitrary")),
    )(proj, proj, proj, conv_w, conv_b, wx_bd, wa_bd, bx, ba, lam)


def _lru_bwd(dya, proj, hlru, conv_w, conv_b, wx_bd, wa_bd, bx, ba, lam, B, S, deps=()):
    T = B * S
    tc = min(256, S)
    nt = S // tc
    h8 = tc // 8

    def body(dya_ref, xa_ref, xhalo_ref, ga_ref, h_ref, hhalo_ref, cw_ref, cb_ref, wx_ref, wa_ref, bx_ref, ba_ref,
             lam_ref, dxa_ref, dga_ref, dcw_ref, dcb_ref, dwx_ref, dwa_ref, dbx_ref, dba_ref, dlam_ref,
             ext_ref, ext2_ref, carry_ref, dhalo_ref, la_ref, lh_ref, c_ref, dh_ref):
        b = pl.program_id(1)
        t = pl.program_id(2)
        tt = nt - 1 - t

        @pl.when(t == 0)
        def _():
            carry_ref[...] = jnp.zeros_like(carry_ref)
            dhalo_ref[...] = jnp.zeros_like(dhalo_ref)

        @pl.when((t == 0) & (b == 0))
        def _():
            for r in (dcw_ref, dcb_ref, dwx_ref, dwa_ref, dbx_ref, dba_ref, dlam_ref):
                r[...] = jnp.zeros_like(r)

        xa = xa_ref[...]
        ext_ref[0:8, :] = jnp.where(tt == 0, 0.0, xhalo_ref[...])
        ext_ref[8:8 + tc, :] = xa
        xc = _conv_from_ext(ext_ref, xa, cw_ref, cb_ref, tc)
        xcb, i_t, r_t, sp, a, mult = _lru_gates(xc, wx_ref, wa_ref, bx_ref, ba_ref, lam_ref)

        h = h_ref[...]
        ga = ga_ref[...]
        dya_t = dya_ref[...]
        sg = _sigmoid(ga)
        dga_ref[...] = (dya_t * h * (sg * (1.0 + ga * (1.0 - sg)))).astype(dga_ref.dtype)
        dlru = dya_t * (ga * sg)

        row = lax.broadcasted_iota(jnp.int32, a.shape, 0)
        coef = jnp.where(row == tc - 1, 1.0, pltpu.roll(a, tc - 1, 0))
        _scan_tile(coef, dlru, carry_ref[0:1, :], la_ref, lh_ref, c_ref, dh_ref, True)
        dh = dh_ref[...]
        ext2_ref[0:tc, :] = a * dh
        carry_ref[...] = ext2_ref[0:8, :]

        ext2_ref[0:8, :] = jnp.where(tt == 0, 0.0, hhalo_ref[...])
        ext2_ref[8:8 + tc, :] = h
        hprev = ext2_ref[7:7 + tc, :]

        da = dh * hprev
        ix = i_t * xc
        dmult = dh * ix
        di = dh * mult * xc
        dxc = dh * mult * i_t
        dlog_a = da * a - dmult * (a * a) / mult
        dr = dlog_a * ((-LRU_C) * sp)
        dlam_ref[...] += jnp.sum(dlog_a * r_t, axis=0, keepdims=True) * (LRU_C * _sigmoid(-lam_ref[...]))
        dza = dr * r_t * (1.0 - r_t)
        dzx = di * i_t * (1.0 - i_t)
        dzab = _c(dza)
        dzxb = _c(dzx)
        dxc = dxc + _dot_nt(dzxb, wx_ref[0]) + _dot_nt(dzab, wa_ref[0])
        dwx_ref[0] += _dot_tn(xcb, dzxb)
        dwa_ref[0] += _dot_tn(xcb, dzab)
        dbx_ref[...] += jnp.sum(dzx, axis=0, keepdims=True)
        dba_ref[...] += jnp.sum(dza, axis=0, keepdims=True)

        dcb_ref[...] += jnp.sum(dxc, axis=0, keepdims=True)
        dcw_ref[3:4, :] += jnp.sum(dxc * xa, axis=0, keepdims=True)
        dcw_ref[2:3, :] += jnp.sum(dxc * ext_ref[7:7 + tc, :], axis=0, keepdims=True)
        dcw_ref[1:2, :] += jnp.sum(dxc * ext_ref[6:6 + tc, :], axis=0, keepdims=True)
        dcw_ref[0:1, :] += jnp.sum(dxc * ext_ref[5:5 + tc, :], axis=0, keepdims=True)
        ext2_ref[0:tc, :] = dxc
        ext2_ref[tc:tc + 8, :] = dhalo_ref[...]
        dxa = (cw_ref[3:4, :] * dxc + cw_ref[2:3, :] * ext2_ref[1:1 + tc, :]
               + cw_ref[1:2, :] * ext2_ref[2:2 + tc, :] + cw_ref[0:1, :] * ext2_ref[3:3 + tc, :])
        dxa_ref[...] = dxa.astype(dxa_ref.dtype)
        dhalo_ref[...] = ext2_ref[0:8, :]

    row_of = lambda b, t: b * nt + (nt - 1 - t)
    tile = lambda off: pl.BlockSpec((tc, CW), lambda c, b, t: (row_of(b, t), off + c))
    halo = pl.BlockSpec((8, CW), lambda c, b, t: (jnp.maximum(row_of(b, t) * h8 - 1, 0), c))
    vec = pl.BlockSpec((1, CW), lambda c, b, t: (0, c))
    mat = pl.BlockSpec((1, CW, CW), lambda c, b, t: (c, 0, 0))
    cwspec = pl.BlockSpec((CONV, CW), lambda c, b, t: (0, c))
    return pl.pallas_call(
        _after(body, 13, deps),
        name="lru_bwd",
        grid=(N_CT, B, nt),
        in_specs=[tile(0), tile(0), halo, tile(N_CT), tile(0), halo, cwspec, vec, mat, mat, vec, vec, vec]
        + [ANY_SPEC] * len(deps),
        out_specs=[tile(0), tile(0), cwspec, vec, mat, mat, vec, vec, vec],
        out_shape=[
            jax.ShapeDtypeStruct((T, D_MODEL), _MXU),
            jax.ShapeDtypeStruct((T, D_MODEL), _MXU),
            jax.ShapeDtypeStruct((CONV, D_MODEL), F32),
            jax.ShapeDtypeStruct((1, D_MODEL), F32),
            jax.ShapeDtypeStruct((N_CT, CW, CW), F32),
            jax.ShapeDtypeStruct((N_CT, CW, CW), F32),
            jax.ShapeDtypeStruct((1, D_MODEL), F32),
            jax.ShapeDtypeStruct((1, D_MODEL), F32),
            jax.ShapeDtypeStruct((1, D_MODEL), F32),
        ],
        scratch_shapes=[pltpu.VMEM((tc + 8, CW), F32), pltpu.VMEM((tc + 8, CW), F32),
                        pltpu.VMEM((8, CW), F32), pltpu.VMEM((8, CW), F32)] + _scan_scratch(tc),
        compiler_params=_params(("parallel", "arbitrary", "arbitrary")),
    )(dya, proj, proj, proj, hlru, hlru, conv_w, conv_b, wx_bd, wa_bd, bx, ba, lam, *deps)


def _retention_tables(S):
    half = DK // 2
    freqs = ROPE_THETA ** (-jnp.arange(half, dtype=F32) / half)
    ang = jnp.arange(S, dtype=F32)[:, None] * freqs[None, :]
    log_g = jnp.log1p(-(2.0 ** (-5.0 - jnp.arange(HEADS, dtype=F32))))
    idx = jnp.arange(CHUNK, dtype=F32)
    diff = idx[:, None] - idx[None, :]
    inner = jnp.where(diff >= 0, jnp.exp(jnp.maximum(diff, 0.0)[None] * log_g[:, None, None]), 0.0)
    cross = jnp.exp((idx[None, :] + 1.0) * log_g[:, None])[:, :, None]
    state = jnp.exp((CHUNK - 1.0 - idx[None, :]) * log_g[:, None])[:, :, None]
    gam = jnp.broadcast_to(jnp.exp(CHUNK * log_g)[:, None, None], (HEADS, 1, DK))
    return jnp.cos(ang), jnp.sin(ang), inner, cross, state, gam


def _rot(x, cos, sin):
    half = DK // 2
    x1, x2 = x[:, :half], x[:, half:]
    return jnp.concatenate([x1 * cos - x2 * sin, x1 * sin + x2 * cos], axis=-1)


def _rot_t(y, cos, sin):
    half = DK // 2
    y1, y2 = y[:, :half], y[:, half:]
    return jnp.concatenate([y1 * cos + y2 * sin, y2 * cos - y1 * sin], axis=-1)


def _groupnorm(o):
    mu = jnp.mean(o, axis=-1, keepdims=True)
    oc = o - mu
    rs = lax.rsqrt(jnp.mean(oc * oc, axis=-1, keepdims=True) + EPS)
    return oc * rs, rs


def _ret_specs(B, chunk_of):
    qkv = lambda g: pl.BlockSpec((B, CHUNK, D_MODEL), lambda c: (0, chunk_of(c), g))
    act = pl.BlockSpec((B, CHUNK, D_MODEL), lambda c: (0, chunk_of(c), 0))
    rope = pl.BlockSpec((CHUNK, DK // 2), lambda c: (chunk_of(c), 0))
    dmat = pl.BlockSpec((HEADS, CHUNK, CHUNK), lambda c: (0, 0, 0))
    dvec = pl.BlockSpec((HEADS, CHUNK, 1), lambda c: (0, 0, 0))
    hrow = pl.BlockSpec((HEADS, 1, DK), lambda c: (0, 0, 0))
    rst = pl.BlockSpec((1, B, HEADS, DK, DK), lambda c: (chunk_of(c), 0, 0, 0, 0))
    return qkv, act, rope, dmat, dvec, hrow, rst


def _ret_fwd(proj, tables, gain3, B, S):
    T = B * S
    nc = S // CHUNK
    cos, sin, dmat_t, cd_t, sd_t, gam_t = tables

    def body(q_ref, k_ref, v_ref, gb_ref, cos_ref, sin_ref, dm_ref, cd_ref, sd_ref, gam_ref, gain_ref,
             o_ref, yb_ref, rs_ref, state_ref):
        @pl.when(pl.program_id(0) == 0)
        def _():
            state_ref[...] = jnp.zeros_like(state_ref)

        cos_t, sin_t = cos_ref[...], sin_ref[...]
        for b, h in [(b, h) for b in range(B) for h in range(HEADS)]:
            cols = slice(h * DK, (h + 1) * DK)
            qb = _c(_rot(q_ref[b, :, cols], cos_t, sin_t))
            kb = _c(_rot(k_ref[b, :, cols], cos_t, sin_t) * (DK ** -0.5))
            v = v_ref[b, :, cols]
            state = state_ref[b, h]
            sb = _c(state)
            rs_ref[0, b, h] = sb
            scores = _dot_nt(qb, kb) * dm_ref[h]
            o = _dot(_c(scores), _c(v)) + _dot(qb, sb) * cd_ref[h]
            state_ref[b, h] = gam_ref[h] * state + _dot_tn(kb, _c(v * sd_ref[h]))
            o_ref[b, :, cols] = o
            n, _ = _groupnorm(o)
            gb = gb_ref[b, :, cols]
            yb_ref[b, :, cols] = (gb * _sigmoid(gb) * (n * gain_ref[h])).astype(yb_ref.dtype)

    qkv, act, rope, dmat, dvec, hrow, rst = _ret_specs(B, lambda c: c)
    proj3 = proj.reshape(B, S, proj.shape[1])
    o_pre, yb, states = pl.pallas_call(
        body,
        name="ret_fwd",
        grid=(nc,),
        in_specs=[qkv(2), qkv(3), qkv(4), qkv(5), rope, rope, dmat, dvec, dvec, hrow, hrow],
        out_specs=[act, act, rst],
        out_shape=[
            jax.ShapeDtypeStruct((B, S, D_MODEL), F32),
            jax.ShapeDtypeStruct((B, S, D_MODEL), _MXU),
            jax.ShapeDtypeStruct((nc, B, HEADS, DK, DK), _MXU),
        ],
        scratch_shapes=[pltpu.VMEM((B, HEADS, DK, DK), F32)],
        compiler_params=_params(("arbitrary",)),
    )(proj3, proj3, proj3, proj3, cos, sin, dmat_t, cd_t, sd_t, gam_t, gain3)
    return o_pre.reshape(T, D_MODEL), yb.reshape(T, D_MODEL), states


def _ret_bwd(dyb, o_pre, proj, states, tables, gain3, B, S, deps=()):
    T = B * S
    nc = S // CHUNK
    cos, sin, dmat_t, cd_t, sd_t, gam_t = tables

    def body(dyb_ref, o_ref, q_ref, k_ref, v_ref, gb_ref, rs_ref, cos_ref, sin_ref, dm_ref, cd_ref, sd_ref, gam_ref,
             gain_ref, dr_ref, dgain_ref, dstate_ref):
        @pl.when(pl.program_id(0) == 0)
        def _():
            dstate_ref[...] = jnp.zeros_like(dstate_ref)
            dgain_ref[...] = jnp.zeros_like(dgain_ref)

        cos_t, sin_t = cos_ref[...], sin_ref[...]
        for b, h in [(b, h) for b in range(B) for h in range(HEADS)]:
            cols = slice(h * DK, (h + 1) * DK)
            gain = gain_ref[h]
            n, rs = _groupnorm(o_ref[b, :, cols])
            gb = gb_ref[b, :, cols]
            sg = _sigmoid(gb)
            dy = dyb_ref[b, :, cols]
            part = lambda g: slice(g * D_MODEL + h * DK, g * D_MODEL + (h + 1) * DK)
            dr_ref[b, :, part(3)] = (dy * (n * gain) * (sg * (1.0 + gb * (1.0 - sg)))).astype(dr_ref.dtype)
            dgn = dy * (gb * sg)
            dgain_ref[h] += jnp.sum(dgn * n, axis=0, keepdims=True)
            dn = dgn * gain
            do = rs * (dn - jnp.mean(dn, axis=-1, keepdims=True) - n * jnp.mean(dn * n, axis=-1, keepdims=True))

            qb = _c(_rot(q_ref[b, :, cols], cos_t, sin_t))
            kb = _c(_rot(k_ref[b, :, cols], cos_t, sin_t) * (DK ** -0.5))
            v = v_ref[b, :, cols]
            vb = _c(v)
            vsb = _c(v * sd_ref[h])
            dob = _c(do)
            docb = _c(do * cd_ref[h])
            dmat = dm_ref[h]
            dstate = dstate_ref[b, h]
            dsb = _c(dstate)
            pb = _c(_dot_nt(qb, kb) * dmat)
            dsc = _c(_dot_nt(dob, vb) * dmat)
            dq = _dot(dsc, kb) + _dot_nt(docb, rs_ref[0, b, h])
            dk = _dot_tn(dsc, qb) + _dot_nt(vsb, dsb)
            dv = _dot_tn(pb, dob) + _dot(kb, dsb) * sd_ref[h]
            dstate_ref[b, h] = gam_ref[h] * dstate + _dot_tn(qb, docb)
            dr_ref[b, :, part(0)] = _rot_t(dq, cos_t, sin_t).astype(dr_ref.dtype)
            dr_ref[b, :, part(1)] = (_rot_t(dk, cos_t, sin_t) * (DK ** -0.5)).astype(dr_ref.dtype)
            dr_ref[b, :, part(2)] = dv.astype(dr_ref.dtype)

    qkv, act, rope, dmat, dvec, hrow, rst = _ret_specs(B, lambda c: nc - 1 - c)
    wide = pl.BlockSpec((B, CHUNK, 4 * D_MODEL), lambda c: (0, nc - 1 - c, 0))
    proj3 = proj.reshape(B, S, proj.shape[1])
    dr, dgain = pl.pallas_call(
        _after(body, 14, deps),
        name="ret_bwd",
        grid=(nc,),
        in_specs=[act, act, qkv(2), qkv(3), qkv(4), qkv(5), rst, rope, rope, dmat, dvec, dvec, hrow, hrow]
        + [ANY_SPEC] * len(deps),
        out_specs=[wide, hrow],
        out_shape=[jax.ShapeDtypeStruct((B, S, 4 * D_MODEL), _MXU), jax.ShapeDtypeStruct((HEADS, 1, DK), F32)],
        scratch_shapes=[pltpu.VMEM((B, HEADS, DK, DK), F32)],
        compiler_params=_params(("arbitrary",)),
    )(dyb.reshape(B, S, D_MODEL), o_pre.reshape(B, S, D_MODEL), proj3, proj3, proj3, proj3, states, cos, sin, dmat_t,
      cd_t, sd_t, gam_t, gain3, *deps)
    return dr.reshape(T, 4 * D_MODEL), dgain


def _mid(ya, yb, proj, x2d, tgt2d, wpa, wpb, wout, g_fin):
    T = x2d.shape[0]
    tm = min(256, T)
    n_steps = T // tm
    rows = D_MODEL // (2 * N_CHIPS)

    def body(ya_ref, yb_ref, ma_ref, mb_ref, x_ref, t_ref, gf_ref, wpa_hbm, wpb_hbm, wout_hbm,
             loss_ref, dx2_ref, dya_ref, dyb_ref, dm_ref, dgf_ref, gw_hbm, w_ref, acc_ref, sem):
        i = pl.program_id(0)

        @pl.when(i == 0)
        def _():
            loads = [pltpu.make_async_copy(src, w_ref.at[k], sem.at[k]) for k, src in enumerate((wpa_hbm, wpb_hbm, wout_hbm))]
            for cp in loads:
                cp.start()
            for cp in loads:
                cp.wait()
            acc_ref[...] = jnp.zeros_like(acc_ref)
            loss_ref[...] = jnp.zeros_like(loss_ref)
            dgf_ref[...] = jnp.zeros_like(dgf_ref)

        ya_t, yb_t = ya_ref[...], yb_ref[...]
        out_a = _dot(ya_t, w_ref[0])
        out_b = _dot(yb_t, w_ref[1])
        sa = _sigmoid(ma_ref[...])
        sb = _sigmoid(mb_ref[...])
        mgb = _c(sa * out_a + sb * out_b)
        x2 = x_ref[...] + _dot(mgb, w_ref[2])
        r2 = lax.rsqrt(jnp.mean(x2 * x2, axis=-1, keepdims=True) + EPS)
        nx = x2 * r2
        gf = gf_ref[...]
        err = nx * gf - t_ref[...]
        loss_ref[...] += 0.5 * jnp.sum(jnp.mean(err * err, axis=-1, keepdims=True), axis=0, keepdims=True)
        dy = err * (1.0 / D_MODEL)
        dgf_ref[...] += jnp.sum(dy * nx, axis=0, keepdims=True)
        dyg = dy * gf
        dx2 = r2 * (dyg - nx * jnp.mean(dyg * nx, axis=-1, keepdims=True))
        dx2_ref[...] = dx2
        dx2b = _c(dx2)
        dmg = _dot_nt(dx2b, w_ref[2])
        acc_ref[2] += _dot_tn(mgb, dx2b)
        dm_ref[:, :D_MODEL] = (dmg * out_a * sa * (1.0 - sa)).astype(dm_ref.dtype)
        dm_ref[:, D_MODEL:] = (dmg * out_b * sb * (1.0 - sb)).astype(dm_ref.dtype)
        dab = _c(dmg * sa)
        dbb = _c(dmg * sb)
        dya_ref[...] = _dot_nt(dab, w_ref[0])
        dyb_ref[...] = _dot_nt(dbb, w_ref[1])
        acc_ref[0] += _dot_tn(ya_t, dab)
        acc_ref[1] += _dot_tn(yb_t, dbb)

        @pl.when(i == n_steps - 1)
        def _():
            copies = [pltpu.make_async_copy(acc_ref.at[k, pl.ds((2 * p + hf) * rows, rows), :], gw_hbm.at[p, hf, k],
                                            sem.at[(k * N_CHIPS + p) * 2 + hf])
                      for k in range(3) for p in range(N_CHIPS) for hf in range(2)]
            for cp in copies:
                cp.start()
            for cp in copies:
                cp.wait()

    tile = lambda j: pl.BlockSpec((tm, D_MODEL), lambda i: (i, j))
    one = pl.BlockSpec((1, D_MODEL), lambda i: (0, 0))
    anyspec = pl.BlockSpec(memory_space=pl.ANY)
    return pl.pallas_call(
        body,
        name="mid",
        grid=(n_steps,),
        in_specs=[tile(0), tile(0), tile(6), tile(7), tile(0), tile(0), one, anyspec, anyspec, anyspec],
        out_specs=[pl.BlockSpec((1, 1), lambda i: (0, 0)), tile(0), tile(0), tile(0),
                   pl.BlockSpec((tm, 2 * D_MODEL), lambda i: (i, 0)), one, anyspec],
        out_shape=[
            jax.ShapeDtypeStruct((1, 1), F32),
            jax.ShapeDtypeStruct((T, D_MODEL), F32),
            jax.ShapeDtypeStruct((T, D_MODEL), F32),
            jax.ShapeDtypeStruct((T, D_MODEL), F32),
            jax.ShapeDtypeStruct((T, 2 * D_MODEL), _MXU),
            jax.ShapeDtypeStruct((1, D_MODEL), F32),
            jax.ShapeDtypeStruct((N_CHIPS, 2, 3, rows, D_MODEL), F32),
        ],
        scratch_shapes=[pltpu.VMEM((3, D_MODEL, D_MODEL), _MXU), pltpu.VMEM((3, D_MODEL, D_MODEL), F32),
                        pltpu.SemaphoreType.DMA((3 * N_CHIPS * 2,))],
        compiler_params=_params(("arbitrary",)),
    )(ya, yb, proj, proj, x2d, tgt2d, g_fin, wpa, wpb, wout)


DX_TILE = 512


def _inproj_bwd_dx(dparts, w_all, x2d, dx2, g_in, first, count, prev, name, deps=()):
    T = x2d.shape[0]
    tm = min(DX_TILE, T)
    n_d = len(dparts)
    groups = [(a, k) for a, d in enumerate(dparts) for k in range(d.shape[1] // D_MODEL)]
    dg_start = jnp.zeros((1, D_MODEL), F32) if prev is None else prev[1]
    carried = () if prev is None else (prev[0],)

    def body(*refs):
        d_refs = refs[:n_d]
        x_ref, dx2_ref, g_ref, dg0_ref, w_hbm = refs[n_d:n_d + 5]
        dx_ref, dg_ref, w_ref, sem = refs[-4:]

        @pl.when(pl.program_id(0) == 0)
        def _():
            cp = pltpu.make_async_copy(w_hbm, w_ref, sem)
            cp.start()
            cp.wait()
            dg_ref[...] = dg0_ref[...]

        dh = jnp.zeros((tm, D_MODEL), F32)
        for j, (a, k) in enumerate(groups):
            dh = dh + _dot_nt(d_refs[a][:, k * D_MODEL:(k + 1) * D_MODEL],
                              w_ref[j // 2, :, (j % 2) * D_MODEL:(j % 2 + 1) * D_MODEL])
        x = x_ref[...]
        r = lax.rsqrt(jnp.mean(x * x, axis=-1, keepdims=True) + EPS)
        nx = x * r
        dg_ref[...] += jnp.sum(dh * nx, axis=0, keepdims=True)
        dhg = dh * g_ref[...]
        dx_ref[...] = dx2_ref[...] + r * (dhg - nx * jnp.mean(dhg * nx, axis=-1, keepdims=True))

    tile = pl.BlockSpec((tm, D_MODEL), lambda i: (first + i, 0))
    one = pl.BlockSpec((1, D_MODEL), lambda i: (0, 0))
    return pl.pallas_call(
        body,
        name=name,
        grid=(count,),
        in_specs=[pl.BlockSpec((tm, d.shape[1]), lambda i: (first + i, 0)) for d in dparts]
        + [tile, tile, one, one, ANY_SPEC] + [ANY_SPEC] * (len(carried) + len(deps)),
        out_specs=[tile, one],
        out_shape=[jax.ShapeDtypeStruct((T, D_MODEL), F32), jax.ShapeDtypeStruct((1, D_MODEL), F32)],
        input_output_aliases={n_d + 5: 0} if carried else {},
        scratch_shapes=[pltpu.VMEM(w_all.shape, w_all.dtype), pltpu.SemaphoreType.DMA],
        compiler_params=_params(("arbitrary",)),
    )(*dparts, x2d, dx2, g_in, dg_start, w_all, *carried, *deps)


def _inproj_bwd_dw(ht, dparts, name, deps=()):
    T = ht.shape[1]
    tn = 512
    half = D_MODEL // 2
    per_chip = 2 * D_MODEL // tn
    n_d = len(dparts)
    tiles = [(a, t) for a, d in enumerate(dparts) for t in range(d.shape[1] // tn)]
    offs = [sum(d.shape[1] // tn for d in dparts[:a]) for a in range(n_d)]

    def body(*refs):
        ht_ref = refs[0]
        d_refs = refs[1:1 + n_d]
        out_ref = refs[-1]
        t = pl.program_id(0)

        for a in range(n_d):
            lo, hi = offs[a], offs[a] + dparts[a].shape[1] // tn

            @pl.when((t >= lo) & (t < hi))
            def _(a=a):
                g = _dot(ht_ref[...], d_refs[a][...])
                out_ref[0, 0] = g[:half]
                out_ref[0, 1] = g[half:]

    def dspec(a):
        n_a = dparts[a].shape[1] // tn
        return pl.BlockSpec((T, tn), lambda t: (0, jnp.clip(t - offs[a], 0, n_a - 1)))

    return pl.pallas_call(
        body,
        name=name,
        grid=(len(tiles),),
        in_specs=[pl.BlockSpec((D_MODEL, T), lambda t: (0, 0))] + [dspec(a) for a in range(n_d)]
        + [ANY_SPEC] * len(deps),
        out_specs=pl.BlockSpec((1, 2, half, tn), lambda t: (t // per_chip, 0, 0, t % per_chip)),
        out_shape=jax.ShapeDtypeStruct((len(tiles) // per_chip, 2, half, 2 * D_MODEL), F32),
        compiler_params=_params(("parallel",)),
    )(ht, *dparts, *deps)


def _coords():
    return lax.axis_index("x"), lax.axis_index("y"), lax.axis_index("c")


def _other_chips(x, y):
    return [(1 - x, y), (x, 1 - y), (1 - x, 1 - y)]


def _chunks(rows, n):
    size = rows // n
    return [pl.ds(q * size, size) for q in range(n)]


HBM_SPEC = pl.BlockSpec(memory_space=pltpu.HBM)
SEM_SPEC = pl.BlockSpec(memory_space=pltpu.SEMAPHORE)
DATAFLOW = pltpu.SideEffectType.DATAFLOW_SIDE_EFFECTING


def _copies_start(bufs, plan, n_copies, name, deps=()):
    n = len(bufs)
    n_deps = len(deps)

    def body(*refs):
        ins = refs[:n]
        send_sems, recv_sems = refs[n + n_deps], refs[n + n_deps + 1]
        token = refs[-1]
        for k, send, _ in plan(ins):
            if send is not None:
                src, dst, dev, pred = send
                cp = pltpu.make_async_remote_copy(src_ref=src, dst_ref=dst, send_sem=send_sems.at[k],
                                                  recv_sem=recv_sems.at[k], device_id=dev, device_id_type=MESH)
                if pred is None:
                    cp.start()
                else:
                    pl.when(pred)(cp.start)
        token[...] = jnp.zeros_like(token)

    hbm = [pltpu.with_memory_space_constraint(b, pltpu.HBM) for b in bufs]
    outs = pl.pallas_call(
        body,
        name=name,
        in_specs=[HBM_SPEC] * n + [ANY_SPEC] * n_deps,
        out_specs=(SEM_SPEC, SEM_SPEC, *([HBM_SPEC] * n), pl.BlockSpec(memory_space=pltpu.VMEM)),
        out_shape=(pltpu.SemaphoreType.DMA((n_copies,)), pltpu.SemaphoreType.DMA((n_copies,)),
                   *[pltpu.HBM(b.shape, b.dtype) for b in bufs], jax.ShapeDtypeStruct((8, 128), F32)),
        input_output_aliases={a: 2 + a for a in range(n)},
        compiler_params=pltpu.CompilerParams(has_side_effects=DATAFLOW),
    )(*hbm, *deps)
    return outs[0], outs[1], list(outs[2:2 + n]), outs[-1]


def _copies_wait(send_sems, recv_sems, bufs, after, plan, name, only=None):
    n = len(bufs)

    def body(*refs):
        ins = refs[:n]
        s_sems, r_sems = refs[n], refs[n + 1]
        for k, send, recv in plan(ins):
            if only is not None and k not in only:
                continue
            if send is not None:
                src, dst, dev, pred = send
                cp = pltpu.make_async_remote_copy(src_ref=src, dst_ref=dst, send_sem=s_sems.at[k],
                                                  recv_sem=r_sems.at[k], device_id=dev, device_id_type=MESH)
                if pred is None:
                    cp.wait_send()
                else:
                    pl.when(pred)(cp.wait_send)
            if recv is not None:
                dst, pred = recv
                cp = pltpu.make_async_remote_copy(src_ref=dst, dst_ref=dst, send_sem=s_sems.at[k],
                                                  recv_sem=r_sems.at[k], device_id=_coords(), device_id_type=MESH)
                if pred is None:
                    cp.wait_recv()
                else:
                    pl.when(pred)(cp.wait_recv)

    outs = pl.pallas_call(
        body,
        name=name,
        in_specs=[HBM_SPEC] * n + [SEM_SPEC, SEM_SPEC, pl.BlockSpec(memory_space=pl.ANY)],
        out_specs=[HBM_SPEC] * n,
        out_shape=[pltpu.HBM(b.shape, b.dtype) for b in bufs],
        input_output_aliases={a: a for a in range(n)},
        compiler_params=pltpu.CompilerParams(has_side_effects=DATAFLOW),
    )(*bufs, send_sems, recv_sems, after)
    return list(outs)


def _gather_plan(n_bufs):
    def plan(refs):
        x, y, c = _coords()
        me = 2 * x + y
        out = []
        for k, (px, py) in enumerate(_other_chips(x, y)):
            for a in range(n_bufs):
                out.append((k * n_bufs + a, (refs[a].at[me], refs[a].at[me], (px, py, c), None),
                            (refs[a].at[2 * px + py], None)))
        return out
    return plan


def _cast_into_slot(ws, name):
    n = len(ws)
    nt = 2

    def body(s_ref, *refs):
        for a in range(n):
            refs[n + a][0] = refs[a][...].astype(refs[n + a].dtype)

    xi, yi, _ = _coords()
    return pl.pallas_call(
        body,
        name=name,
        grid_spec=pltpu.PrefetchScalarGridSpec(
            num_scalar_prefetch=1,
            grid=(2, nt),
            in_specs=[pl.BlockSpec((1, w.shape[1] // nt, w.shape[2]), lambda hf, i, s: (hf, i, 0)) for w in ws],
            out_specs=[pl.BlockSpec((1, 1, w.shape[1] // nt, w.shape[2]), lambda hf, i, s: (s[0], hf, i, 0)) for w in ws],
        ),
        out_shape=[jax.ShapeDtypeStruct((N_CHIPS,) + w.shape, _MXU) for w in ws],
        compiler_params=_params(("parallel", "parallel")),
    )((2 * xi + yi).reshape(1).astype(jnp.int32), *ws)


def _chip_gather_plan(stage, n_bufs):
    def plan(refs):
        x, y, c = _coords()
        me = 2 * x + y
        near = [(1 - x, y), (x, 1 - y)]
        slots = [2 * (1 - x) + y, 2 * x + (1 - y), 2 * (1 - x) + (1 - y)]
        sibling = (x, y, 1 - c)
        pass_to = (jnp.where(c == 0, x, 1 - x), jnp.where(c == 0, 1 - y, y), c)
        pass_slot = jnp.where(c == 0, slots[0], slots[1])
        out = []

        def move(src_slot, to, land_slot, land_core, pieces):
            for a, buf in enumerate(refs):
                for rows in _chunks(buf.shape[2], pieces[a]):
                    out.append((len(out), (buf.at[src_slot, c, rows], buf.at[src_slot, c, rows], to, None),
                                (buf.at[land_slot, land_core, rows], None)))

        if stage == "near":
            for k, chip in enumerate(near):
                move(me, (*chip, c), slots[k], c, NEAR_PIECES[:n_bufs])
        elif stage == "pass":
            move(pass_slot, pass_to, slots[2], c, PASS_PIECES[:n_bufs])
            for k in range(2):
                move(slots[k], sibling, slots[k], 1 - c, [1] * n_bufs)
        else:
            move(slots[2], sibling, slots[2], 1 - c, [1] * n_bufs)
        return out
    return plan


NEAR_PIECES = (2, 1)
PASS_PIECES = (2, 1)


def _chip_gather_copies(stage, n_bufs):
    if stage == "near":
        return 2 * sum(NEAR_PIECES[:n_bufs]), None
    if stage == "pass":
        n_pass = sum(PASS_PIECES[:n_bufs])
        return n_pass + 2 * n_bufs, set(range(n_pass))
    return n_bufs, None


def _swap_plan(n_slabs):
    def plan(refs):
        x, y, c = _coords()
        out, k = [], 0
        for i, n in enumerate(n_slabs):
            g, land = refs[2 * i], refs[2 * i + 1]
            for p in range(n):
                out.append((k, (g.at[p, 1 - c], land.at[p], (x, y, 1 - c), None), (land.at[p], None)))
                k += 1
        return out
    return plan


def _is_one_of(chip, dests):
    hit = chip == dests[0]
    for d in dests[1:]:
        hit = hit | (chip == d)
    return hit


def _slab_of(chip, dests):
    return sum(j * (chip == d).astype(jnp.int32) for j, d in enumerate(dests))


def _scatter_plan(dest_sets):
    def plan(refs):
        x, y, c = _coords()
        me = 2 * x + y
        out = []
        for k, (px, py) in enumerate(_other_chips(x, y)):
            peer = 2 * px + py
            for i, dests in enumerate(dest_sets):
                cs, land = refs[2 * i], refs[2 * i + 1]
                everyone = len(dests) == N_CHIPS
                send = (cs.at[_slab_of(peer, dests)], land.at[k], (px, py, c),
                        None if everyone else _is_one_of(peer, dests))
                recv = (land.at[k], None if everyone else _is_one_of(me, dests))
                out.append((k * len(dest_sets) + i, send, recv))
        return out
    return plan


def _join_plan(rows, n_pieces):
    def plan(refs):
        x, y, c = _coords()
        (buf,) = refs
        return [(i, (buf.at[c, piece], buf.at[c, piece], (x, y, 1 - c), None), (buf.at[1 - c, piece], None))
                for i, piece in enumerate(_chunks(rows, n_pieces))]
    return plan


def _join_plans(parts):
    def plan(refs):
        out, b0, k0 = [], 0, 0
        for part_plan, n_bufs, n_copies in parts:
            out += [(k0 + k, send, recv) for k, send, recv in part_plan(refs[b0:b0 + n_bufs])]
            b0 += n_bufs
            k0 += n_copies
        return out
    return plan


def _allgather_plan():
    def plan(refs):
        x, y, c = _coords()
        (land,) = refs
        me = 4 * x + 2 * y + c
        out = []
        for r in range(1, 8):
            px = 1 - x if r & 4 else x
            py = 1 - y if r & 2 else y
            pc = 1 - c if r & 1 else c
            out.append((r - 1, (land.at[me], land.at[me], (px, py, pc), None), (land.at[4 * px + 2 * py + pc], None)))
        return out
    return plan


def _sum_gathered(land, name):
    def body(land_ref, o_ref):
        acc = land_ref[0]
        for d in range(1, 8):
            acc = acc + land_ref[d]
        o_ref[...] = acc

    return pl.pallas_call(
        body,
        name=name,
        out_shape=jax.ShapeDtypeStruct(land.shape[1:], F32),
        compiler_params=_params(),
    )(land)


def _join_halves(bufs, n_chunks, name, deps=()):
    n = len(bufs)
    pieces = [(a, rows) for a in range(n) for rows in _chunks(bufs[a].shape[1], n_chunks[a])]
    n_p = len(pieces)

    def body(*refs):
        outs = refs[-n - 2:-2]
        send_sems, recv_sems = refs[-2:]
        x, y, c = _coords()

        def copy(i, half):
            a, rows = pieces[i]
            return pltpu.make_async_remote_copy(
                src_ref=outs[a].at[half, rows], dst_ref=outs[a].at[half, rows], send_sem=send_sems.at[i],
                recv_sem=recv_sems.at[i], device_id=(x, y, 1 - c), device_id_type=MESH)

        sends = [copy(i, c) for i in range(n_p)]
        for cp in sends:
            cp.start()
        for i in range(n_p):
            copy(i, 1 - c).wait_recv()
        for cp in sends:
            cp.wait_send()

    anyspec = pl.BlockSpec(memory_space=pl.ANY)
    sems = pltpu.SemaphoreType.DMA((n_p,))
    return pl.pallas_call(
        body,
        name=name,
        in_specs=[anyspec] * (n + len(deps)),
        out_specs=[anyspec] * n,
        out_shape=[jax.ShapeDtypeStruct(b.shape, b.dtype) for b in bufs],
        input_output_aliases={a: a for a in range(n)},
        scratch_shapes=[sems, sems],
    )(*bufs, *deps)


def _row_tile(rows, cap):
    t = cap
    while rows % t:
        t //= 2
    return t


def _add_my_half(g, r, name):
    n_slabs, _, R, C = g.shape
    tr = R if n_slabs > 1 else _row_tile(R, 256)

    def body(c_ref, g_ref, r_ref, o_ref):
        o_ref[...] = (g_ref[0] + r_ref[...]).astype(o_ref.dtype)

    return pl.pallas_call(
        body,
        name=name,
        grid_spec=pltpu.PrefetchScalarGridSpec(
            num_scalar_prefetch=1,
            grid=(n_slabs, R // tr),
            in_specs=[pl.BlockSpec((1, 1, tr, C), lambda p, i, c_ref: (p, c_ref[0], i, 0)),
                      pl.BlockSpec((1, tr, C), lambda p, i, c_ref: (p, i, 0))],
            out_specs=pl.BlockSpec((1, tr, C), lambda p, i, c_ref: (p, i, 0)),
        ),
        out_shape=jax.ShapeDtypeStruct(r.shape, jnp.bfloat16),
        compiler_params=_params(("parallel", "parallel")),
    )(lax.axis_index("c").reshape(1).astype(jnp.int32), g, r)


def _sum_slabs(own, got, name, deps=()):
    _, R, C = own.shape
    tr = _row_tile(R, 256)

    def body(s_ref, own_ref, got_ref, *rest):
        rest[-1][0] = ((own_ref[0].astype(F32) + got_ref[0].astype(F32)) + got_ref[1].astype(F32)) + got_ref[2].astype(F32)

    xi, yi, ci = _coords()
    return pl.pallas_call(
        body,
        name=name,
        grid_spec=pltpu.PrefetchScalarGridSpec(
            num_scalar_prefetch=1,
            grid=(R // tr,),
            in_specs=[pl.BlockSpec((1, tr, C), lambda i, s: (s[0], i, 0)),
                      pl.BlockSpec((3, tr, C), lambda i, s: (0, i, 0))] + [ANY_SPEC] * len(deps),
            out_specs=pl.BlockSpec((1, tr, C), lambda i, s: (s[1], i, 0)),
        ),
        out_shape=jax.ShapeDtypeStruct((2, R, C), F32),
        compiler_params=_params(("parallel",)),
    )(jnp.stack([2 * xi + yi, ci]).astype(jnp.int32), own, got, *deps)


def _sum_parts(owns, got, dest_sets, name):
    n = len(owns)
    _, R, C = owns[0].shape
    tr = _row_tile(R, 256)

    def body(s_ref, *refs):
        got_ref, o_ref = refs[n], refs[-1]
        total = jnp.zeros((tr, C), F32)
        for i in range(n):
            total = total + jnp.where(s_ref[2 + 2 * i] == 1, refs[i][0].astype(F32), 0.0)
        o_ref[0] = ((total + got_ref[0].astype(F32)) + got_ref[1].astype(F32)) + got_ref[2].astype(F32)

    xi, yi, ci = _coords()
    me = 2 * xi + yi
    scalars = [ci, ci]
    for dests in dest_sets:
        scalars += [_is_one_of(me, dests).astype(jnp.int32), _slab_of(me, dests)]
    own_spec = lambda i: pl.BlockSpec((1, tr, C), lambda r, s: (s[3 + 2 * i], r, 0))
    return pl.pallas_call(
        body,
        name=name,
        grid_spec=pltpu.PrefetchScalarGridSpec(
            num_scalar_prefetch=1,
            grid=(R // tr,),
            in_specs=[own_spec(i) for i in range(n)] + [pl.BlockSpec((3, tr, C), lambda r, s: (0, r, 0))],
            out_specs=pl.BlockSpec((1, tr, C), lambda r, s: (s[0], r, 0)),
        ),
        out_shape=jax.ShapeDtypeStruct((2, R, C), F32),
        compiler_params=_params(("parallel",)),
    )(jnp.stack(scalars).astype(jnp.int32), *owns, got)


def _adamw_math(w, g, m, v):
    m = ADAM_B1 * m + (1.0 - ADAM_B1) * g
    v = ADAM_B2 * v + (1.0 - ADAM_B2) * (g * g)
    m_hat = m / (1.0 - ADAM_B1 ** ADAM_STEP)
    v_hat = v / (1.0 - ADAM_B2 ** ADAM_STEP)
    delta = -ADAM_LR * (m_hat / (jnp.sqrt(v_hat) + ADAM_EPS) + ADAM_WD * w)
    return delta, m, v


def _adamw_halves(ws, g, ms, vs, half, prev, name, deps=()):
    n = len(ws)
    _, _, R, C = g.shape
    tr = _row_tile(R, 128)
    steps = R // tr
    carried = [] if prev is None else [a for four in prev for a in four]
    both = half is None
    which = (lambda i, s: i // steps) if both else (lambda i, s: s[0])
    half = 0 if both else half

    def body(s_ref, *refs):
        w_refs, g_refs, m_refs, v_refs = (refs[k * n:(k + 1) * n] for k in range(4))
        outs = refs[len(refs) - 4 * n:]
        for a in range(n):
            grad = g_refs[a][0, 0]
            d, mn, vn = _adamw_math(w_refs[a][...], grad, m_refs[a][...], v_refs[a][...])
            for o, val in zip(outs[4 * a:4 * a + 4], (grad, d, mn, vn)):
                o[...] = val

    rows = pl.BlockSpec((tr, C), lambda i, s: (which(i, s) * steps + i % steps, 0))
    grad_spec = lambda a: pl.BlockSpec((1, 1, tr, C), lambda i, s: (which(i, s), a, i % steps, 0))
    n_in = 4 * n
    outs = pl.pallas_call(
        body,
        name=name,
        grid_spec=pltpu.PrefetchScalarGridSpec(
            num_scalar_prefetch=1,
            grid=(2 * steps if both else steps,),
            in_specs=[rows] * n + [grad_spec(a) for a in range(n)] + [rows] * (2 * n)
            + [ANY_SPEC] * (len(carried) + len(deps)),
            out_specs=[rows] * (4 * n),
        ),
        out_shape=[jax.ShapeDtypeStruct((2 * R, C), F32)] * (4 * n),
        input_output_aliases={1 + n_in + k: k for k in range(len(carried))},
        compiler_params=_params(("parallel",)),
    )(jnp.reshape(half, (1,)).astype(jnp.int32), *ws, *([g] * n), *ms, *vs, *carried, *deps)
    return [outs[4 * a:4 * a + 4] for a in range(n)]


def _adamw_small(ws, gs, ms, vs, name):
    n = len(ws)

    def body(*refs):
        for a in range(n):
            d, mn, vn = _adamw_math(refs[a][...], refs[n + a][...], refs[2 * n + a][...], refs[3 * n + a][...])
            refs[4 * n + a][...] = d
            refs[5 * n + a][...] = mn
            refs[6 * n + a][...] = vn

    shapes = [jax.ShapeDtypeStruct(w.shape, F32) for w in ws]
    outs = pl.pallas_call(
        body,
        name=name,
        out_shape=shapes * 3,
        compiler_params=_params(),
    )(*ws, *gs, *ms, *vs)
    return outs[:n], outs[n:2 * n], outs[2 * n:]


def _to_blockdiag(w):
    per = CW // LRU_BW
    w4 = w.reshape(N_CT, per, LRU_BW, LRU_BW)
    eye = jnp.eye(per, dtype=w.dtype)
    return (w4[:, :, :, None, :] * eye[None, :, None, :, None]).reshape(N_CT, CW, CW)


def _from_blockdiag(g):
    per = CW // LRU_BW
    g5 = g.reshape(N_CT, per, LRU_BW, per, LRU_BW)
    return jnp.stack([g5[:, b, :, b, :] for b in range(per)], axis=1).reshape(LRU_BLOCKS, LRU_BW, LRU_BW)


def _local_grads(x2d, tgt2d, B, S, g_in, in_proj, conv_b, gate_x_w, gate_x_b, gate_a_w, gate_a_b, lam,
                 proj_weights, g_fin, reduce):
    wx_bd = _c(_to_blockdiag(gate_x_w))
    wa_bd = _c(_to_blockdiag(gate_a_w))
    tables = _retention_tables(S)

    proj, ht, w_all, conv_w, gain = in_proj(x2d, g_in)
    gain3 = gain.reshape(HEADS, 1, DK)
    hlru, ya = _lru_fwd(proj, conv_w, conv_b, wx_bd, wa_bd, gate_x_b, gate_a_b, lam, B, S)
    o_pre, yb, states = _ret_fwd(proj, tables, gain3, B, S)
    wpa, wpb, wout = proj_weights(yb)
    loss, dx2, dya, dyb, dm, dgf, gw_proj = _mid(ya, yb, proj, x2d, tgt2d, wpa, wpb, wout, g_fin)
    g3 = _inproj_bwd_dw(ht, [dm], "inproj_bwd_dw_m")
    deps = reduce.m_ready(gw_proj, g3)
    dr, dgain = _ret_bwd(dyb, o_pre, proj, states, tables, gain3, B, S, deps)
    deps = reduce.ret_done(dr)
    g12 = _inproj_bwd_dw(ht, [dr], "inproj_bwd_dw_r", deps)
    deps = reduce.r_ready(g12)
    dxa, dga, dcw, dcb, dwx_bd, dwa_bd, dbx, dba, dlam = _lru_bwd(
        dya, proj, hlru, conv_w, conv_b, wx_bd, wa_bd, gate_x_b, gate_a_b, lam, B, S, deps)
    small = dict(conv_w=dcw, conv_b=dcb, gate_x_w=_from_blockdiag(dwx_bd), gate_x_b=dbx,
                 gate_a_w=_from_blockdiag(dwa_bd), gate_a_b=dba, lru_lambda=dlam, gn_gain=dgain.reshape(HEADS, DK),
                 norm_final=dgf)
    loss_rows = jnp.broadcast_to(loss, (SUBLANES, LANES))
    deps = reduce.lru_done(dxa, jnp.concatenate([_pack_small(small), loss_rows], axis=0))
    g0 = _inproj_bwd_dw(ht, [dxa, dga], "inproj_bwd_dw_a", deps)
    deps = reduce.a_ready(g0)
    n_tiles = x2d.shape[0] // min(DX_TILE, x2d.shape[0])
    grad_x, dgin = _inproj_bwd_dx([dxa, dga, dr, dm], w_all, x2d, dx2, g_in, 0, n_tiles, None, "inproj_bwd_dx", deps)
    return grad_x, dgin


ALL_CHIPS = (0, 1, 2, 3)


class _GradReduce:
    def __init__(self, proj_done):
        self.pending = {}
        self.proj_done = proj_done
        self.land_in = None

    def _start(self, key, parts, name):
        bufs, plans, shared = [], [], None
        for part_bufs, plan, n_copies, part_shared in parts:
            if part_shared is not None:
                shared = len(bufs) + part_shared
            plans.append((plan, len(part_bufs), n_copies))
            bufs += part_bufs
        plan = _join_plans(plans)
        send_sems, recv_sems, bufs, token = _copies_start(bufs, plan, sum(p[2] for p in plans), name + "_start")
        if shared is not None:
            self.land_in = bufs[shared]
        self.pending[key] = (send_sems, recv_sems, bufs, plan, name + "_wait", shared)
        return (token,)

    def _finish(self, key, after):
        send_sems, recv_sems, bufs, plan, name, shared = self.pending.pop(key)
        if shared is not None:
            bufs[shared] = self.land_in
        bufs = _copies_wait(send_sems, recv_sems, bufs, after, plan, name)
        if shared is not None:
            self.land_in = bufs[shared]
        return bufs

    @staticmethod
    def _swap(pieces):
        bufs = []
        for g in pieces:
            bufs += [g, lax.empty((g.shape[0],) + g.shape[2:], F32)]
        n_slabs = [g.shape[0] for g in pieces]
        return bufs, _swap_plan(n_slabs), sum(n_slabs), None

    def _scatter(self, sums, dest_sets):
        bufs = []
        for cs in sums:
            bufs += [cs, lax.empty((3,) + cs.shape[1:], cs.dtype)]
        if self.land_in is not None:
            bufs[-1] = self.land_in
        return bufs, _scatter_plan(dest_sets), 3 * len(sums), len(bufs) - 1

    @staticmethod
    def _gather8(block):
        x, y, c = _coords()
        land = lax.dynamic_update_slice(lax.empty((8,) + block.shape, F32), block[None], (4 * x + 2 * y + c, 0, 0))
        return [land], _allgather_plan(), 7, None

    def m_ready(self, gw_proj, g3):
        rows = gw_proj.shape[2] * gw_proj.shape[3]
        return self._start("m", [self._swap([gw_proj.reshape(N_CHIPS, 2, rows, D_MODEL), g3])], "swap_m")

    def ret_done(self, after):
        proj, land_p, g3, land_3 = self._finish("m", after)
        sums_m = [_add_my_half(proj, land_p, "chip_sum_proj"), _add_my_half(g3, land_3, "chip_sum_m")]
        return self._start("sm", [self._scatter(sums_m, [ALL_CHIPS, (3,)])], "scatter_m")

    def r_ready(self, g12):
        return self._start("r", [self._swap([g12])], "swap_r")

    def lru_done(self, after, packed):
        g12, land_12 = self._finish("r", after)
        sums_r = [_add_my_half(g12, land_12, "chip_sum_r")]
        return (self._start("sr", [self._scatter(sums_r, [(1, 2)])], "scatter_r")
                + self._start("small", [self._gather8(packed)], "gather_small"))

    def a_ready(self, g0):
        (token,) = self._start("a", [self._swap([g0])], "swap_a")
        csp, gotp, self.cs3, _ = self._finish("sm", token)
        half_proj = _sum_slabs(csp, gotp, "sum_w_proj")
        g0, land_0 = self._finish("a", half_proj)
        deps = self._start("sa", [self._scatter([_add_my_half(g0, land_0, "chip_sum_a")], [(0,)])], "scatter_a")
        self.proj_done(_join_halves([half_proj], [4], "join_halves_proj", deps)[0])
        return deps

    def finish(self, dgin, w_in_done):
        (token,) = self._start("n", [self._gather8(dgin)], "gather_norm_in")
        (small,) = self._finish("small", token)
        cs12, _ = self._finish("sr", token)
        cs0, _ = self._finish("sa", token)
        half_in = _sum_parts([self.cs3, cs12, cs0], self.land_in, [(3,), (1, 2), (0,)], "sum_w_in")
        deps = self._start("j", [([half_in], _join_plan(half_in.shape[1], 8), 8, None)], "join_w_in")
        first = w_in_done(self.pending["j"][2][0], True, None, deps)
        (g_in,) = self._finish("j", first[1])
        done = w_in_done(g_in, False, first, ())
        (norm_in,) = self._finish("n", done[1])
        return _sum_gathered(small, "sum_small_grads"), _sum_gathered(norm_in, "sum_norm_in_grad")


_SMALL = ("gate_x_w", "gate_a_w", "conv_w", "conv_b", "gate_x_b", "gate_a_b", "lru_lambda", "gn_gain", "norm_final")
_SMALL_SHAPES = dict(gate_x_w=(LRU_BLOCKS, LRU_BW, LRU_BW), gate_a_w=(LRU_BLOCKS, LRU_BW, LRU_BW),
                     norm_in=(1, D_MODEL), conv_w=(CONV, D_MODEL), conv_b=(1, D_MODEL), gate_x_b=(1, D_MODEL),
                     gate_a_b=(1, D_MODEL), lru_lambda=(1, D_MODEL), gn_gain=(HEADS, DK), norm_final=(1, D_MODEL))


def _pack_small(small):
    return jnp.concatenate([small[k].reshape(-1, 128) for k in _SMALL], axis=0)


def _unpack_small(packed):
    out, r = {}, 0
    for k in _SMALL:
        shape = _SMALL_SHAPES[k]
        rows = 1
        for s in shape:
            rows *= s
        rows //= 128
        out[k] = packed[r:r + rows].reshape(shape)
        r += rows
    return out


def kernel(x, norm_in, w_in, conv_w, conv_b, gate_x_w, gate_x_b, gate_a_w, gate_a_b, lru_lambda, gn_gain, w_proj_a, w_proj_b, w_out, norm_final, loss_target, m_norm_in, m_w_in, m_conv_w, m_conv_b, m_gate_x_w, m_gate_x_b, m_gate_a_w, m_gate_a_b, m_lru_lambda, m_gn_gain, m_w_proj_a, m_w_proj_b, m_w_out, m_norm_final, v_norm_in, v_w_in, v_conv_w, v_conv_b, v_gate_x_w, v_gate_x_b, v_gate_a_w, v_gate_a_b, v_lru_lambda, v_gn_gain, v_w_proj_a, v_w_proj_b, v_w_out, v_norm_final):
    B, S, _ = x.shape
    T = B * S
    xi, yi, ci = _coords()
    chip = 2 * xi + yi

    cshard = D_MODEL // N_CHIPS
    mine = _cast_into_slot([w_in[0].reshape(2, D_MODEL // 2, 2 * D_MODEL)]
                           + [w[0].reshape(2, cshard // 2, D_MODEL) for w in (w_proj_a, w_proj_b, w_out)],
                           "cast_weights")
    plan = _gather_plan(3)
    pending_proj = []
    gshard = DK // N_CHIPS
    tiny = jnp.concatenate([conv_w[0], jnp.zeros((4, cshard), F32), jnp.pad(gn_gain[0], ((0, 4), (0, cshard - gshard)))],
                           axis=0).reshape(1, 2, SUBLANES, cshard)
    tiny_buf = lax.dynamic_update_slice(lax.empty((N_CHIPS, 2, SUBLANES, cshard), F32), tiny, (chip, 0, 0, 0))
    near_plan, pass_plan, far_plan = (_chip_gather_plan(stage, 2) for stage in ("near", "pass", "far"))
    (n_near, _), (n_pass, passed_on), (n_far, _) = (_chip_gather_copies(stage, 2) for stage in ("near", "pass", "far"))
    halves = set(range(n_pass)) - passed_on
    near_s, near_r, bufs, near_token = _copies_start([mine[0], tiny_buf], near_plan, n_near, "gather_near_start")

    def in_proj(x2d, g_in):
        as_w = lambda b: b[0].reshape(N_CHIPS, D_MODEL, 2 * D_MODEL)
        slot_x, slot_y, slot_d = 2 * (1 - xi) + yi, 2 * xi + (1 - yi), 2 * (1 - xi) + (1 - yi)
        ids = lambda *chips: jnp.stack(chips).astype(jnp.int32)
        proj, hb, ht = _inproj_first(x2d, g_in, as_w(bufs), ids(chip), "inproj_own", (near_token,))
        got = _copies_wait(near_s, near_r, bufs, proj, near_plan, "gather_near_wait")
        pass_s, pass_r, got, pass_token = _copies_start(got, pass_plan, n_pass, "gather_pass_start")
        got = _copies_wait(pass_s, pass_r, got, pass_token, pass_plan, "gather_pass_wait_halves", only=halves)
        proj = _inproj_more(hb, as_w(got), ids(slot_x, slot_y), proj, "inproj_near")
        got = _copies_wait(pass_s, pass_r, got, proj, pass_plan, "gather_pass_wait_far", only=passed_on)
        pending_proj.append(_copies_start(mine[1:], plan, 9, "gather_proj_start", (got[0],)))
        far_s, far_r, got, far_token = _copies_start(got, far_plan, n_far, "gather_far_start")
        got = _copies_wait(far_s, far_r, got, far_token, far_plan, "gather_far_wait")
        proj = _inproj_more(hb, as_w(got), ids(slot_d), proj, "inproj_far")
        tiny_all = got[1].reshape(N_CHIPS, 2 * SUBLANES, cshard)
        conv_w_full = jnp.transpose(tiny_all[:, 0:CONV, :], (1, 0, 2)).reshape(CONV, D_MODEL)
        gain_full = jnp.transpose(tiny_all[:, 8:8 + HEADS, :gshard], (1, 0, 2)).reshape(HEADS, DK)
        return proj, ht, as_w(got), conv_w_full, gain_full

    def proj_weights(after):
        s_sems, r_sems, pbufs, _ = pending_proj[0]
        got = _copies_wait(s_sems, r_sems, pbufs, after, plan, "gather_proj_wait")
        return [b.reshape(D_MODEL, D_MODEL) for b in got]

    weights = dict(norm_in=norm_in, w_in=w_in, conv_w=conv_w, conv_b=conv_b, gate_x_w=gate_x_w, gate_x_b=gate_x_b,
                   gate_a_w=gate_a_w, gate_a_b=gate_a_b, lru_lambda=lru_lambda, gn_gain=gn_gain, w_proj_a=w_proj_a,
                   w_proj_b=w_proj_b, w_out=w_out, norm_final=norm_final)
    ms = dict(norm_in=m_norm_in, w_in=m_w_in, conv_w=m_conv_w, conv_b=m_conv_b, gate_x_w=m_gate_x_w,
              gate_x_b=m_gate_x_b, gate_a_w=m_gate_a_w, gate_a_b=m_gate_a_b, lru_lambda=m_lru_lambda, gn_gain=m_gn_gain,
              w_proj_a=m_w_proj_a, w_proj_b=m_w_proj_b, w_out=m_w_out, norm_final=m_norm_final)
    vs = dict(norm_in=v_norm_in, w_in=v_w_in, conv_w=v_conv_w, conv_b=v_conv_b, gate_x_w=v_gate_x_w,
              gate_x_b=v_gate_x_b, gate_a_w=v_gate_a_w, gate_a_b=v_gate_a_b, lru_lambda=v_lru_lambda, gn_gain=v_gn_gain,
              w_proj_a=v_w_proj_a, w_proj_b=v_w_proj_b, w_out=v_w_out, norm_final=v_norm_final)
    names = list(weights)
    grads, delta, new_m, new_v = {}, {}, {}, {}

    def update_big(keys, g, half, prev, name, deps=()):
        two = lambda a: a.reshape(a.shape[1], a.shape[2])
        res = _adamw_halves([two(weights[k]) for k in keys], g, [two(ms[k]) for k in keys], [two(vs[k]) for k in keys],
                            half, prev, name, deps)
        for k, (gk, d, mn, vn) in zip(keys, res):
            shp = weights[k].shape
            grads[k], delta[k], new_m[k], new_v[k] = gk.reshape(shp), d.reshape(shp), mn.reshape(shp), vn.reshape(shp)
        return res

    def proj_done(g_proj):
        g4 = g_proj.reshape(2, 3, D_MODEL // (2 * N_CHIPS), D_MODEL)
        return update_big(("w_proj_a", "w_proj_b", "w_out"), g4, None, None, "adamw_proj")[-1][1]

    def w_in_done(g_in, own, prev, deps):
        g4 = g_in.reshape(2, 1, D_MODEL // 2, 2 * D_MODEL)
        return update_big(("w_in",), g4, ci if own else 1 - ci, None if prev is None else [prev],
                          "adamw_w_in_own" if own else "adamw_w_in_other", deps)[0]

    reduce = _GradReduce(proj_done)
    grad_x, dgin = _local_grads(
        x.reshape(T, D_MODEL), loss_target.reshape(T, D_MODEL), B, S, norm_in, in_proj, conv_b,
        gate_x_w[0], gate_x_b, gate_a_w[0], gate_a_b, lru_lambda, proj_weights,
        norm_final.reshape(1, D_MODEL), reduce)

    small_sum, g_norm_in = reduce.finish(dgin.reshape(SUBLANES, LANES), w_in_done)
    loss = small_sum[small_sum.shape[0] - SUBLANES, 0]

    gsm = _unpack_small(small_sum)
    gsm["norm_in"] = g_norm_in
    gsm["conv_w"] = lax.dynamic_slice_in_dim(gsm["conv_w"], chip * cshard, cshard, axis=1)
    gsm["gn_gain"] = lax.dynamic_slice_in_dim(gsm["gn_gain"], chip * gshard, gshard, axis=1)
    smalls = [k for k in names if k not in delta]

    def view(a):
        return a.reshape(1, -1) if a.ndim == 1 else (a.reshape(a.shape[1:]) if a.ndim > 2 else a)

    ds, mns, vns = _adamw_small([view(weights[k]) for k in smalls], [gsm[k].reshape(view(weights[k]).shape) for k in smalls],
                                [view(ms[k]) for k in smalls], [view(vs[k]) for k in smalls], "adamw_small")
    for k, d, mn, vn in zip(smalls, ds, mns, vns):
        shp = weights[k].shape
        grads[k], delta[k], new_m[k], new_v[k] = gsm[k].reshape(shp), d.reshape(shp), mn.reshape(shp), vn.reshape(shp)

    return (loss, grad_x.reshape(B, S, D_MODEL), *[grads[k] for k in names], *[delta[k] for k in names],
            *[new_m[k] for k in names], *[new_v[k] for k in names])
```

```python
import jax
import jax.numpy as jnp
from jax import lax
from jax.experimental import pallas as pl
from jax.experimental.pallas import tpu as pltpu

F32 = jnp.float32
_MXU = jnp.bfloat16

D_MODEL = 1024
N_GROUPS = 8
HEADS = 4
DK = 256
CHUNK = 128
CONV = 4
LRU_BLOCKS = 16
LRU_BW = 64
LRU_C = 8.0
ROPE_THETA = 10000.0
EPS = 1e-6
CW = 256
N_CT = D_MODEL // CW
N_CHIPS = 4
MESH = pl.DeviceIdType.MESH

ADAM_LR = 0.001
ADAM_B1 = 0.9
ADAM_B2 = 0.999
ADAM_EPS = 1e-08
ADAM_WD = 0.01
ADAM_STEP = 10

VMEM_LIMIT = 56 * 1024 * 1024


def _c(v):
    return v.astype(_MXU)


def _dot(a, b):
    return lax.dot_general(a, b, (((1,), (0,)), ((), ())), preferred_element_type=F32)


def _dot_nt(a, b):
    return lax.dot_general(a, b, (((1,), (1,)), ((), ())), preferred_element_type=F32)


def _dot_tn(a, b):
    return lax.dot_general(a, b, (((0,), (0,)), ((), ())), preferred_element_type=F32)


def _sigmoid(z):
    return 0.5 * jnp.tanh(0.5 * z) + 0.5


ANY_SPEC = pl.BlockSpec(memory_space=pl.ANY)


def _after(body, n_in, deps):
    n_deps = len(deps)

    def wrapped(*refs):
        return body(*refs[:n_in], *refs[n_in + n_deps:])

    return wrapped


def _params(sem=None):
    if sem is None:
        return pltpu.CompilerParams(vmem_limit_bytes=VMEM_LIMIT)
    return pltpu.CompilerParams(vmem_limit_bytes=VMEM_LIMIT, dimension_semantics=sem)


def _inproj_first(x2d, g_in, w_all, chips, name, deps=()):
    T = x2d.shape[0]
    tm = min(1024, T)
    n_i = T // tm

    def body(s_ref, *refs):
        x_ref, g_ref, w_ref = refs[:3]
        proj_ref, hb_ref, ht_ref, h_all = refs[-4:]
        i = pl.program_id(1)
        rows = pl.ds(pl.multiple_of(i * tm, tm), tm)

        @pl.when(pl.program_id(0) == 0)
        def _():
            x = x_ref[...]
            r = lax.rsqrt(jnp.mean(x * x, axis=-1, keepdims=True) + EPS)
            h = x * r * g_ref[...]
            hb = h.astype(h_all.dtype)
            h_all[rows, :] = hb
            hb_ref[...] = hb
            ht_ref[...] = h.T.astype(ht_ref.dtype)

        proj_ref[...] = _dot(h_all[rows, :], w_ref[0])

    first = lambda j, i: jnp.where(j == 0, i, n_i - 1)
    return pl.pallas_call(
        body,
        name=name,
        grid_spec=pltpu.PrefetchScalarGridSpec(
            num_scalar_prefetch=1,
            grid=(2 * chips.shape[0], n_i),
            in_specs=[
                pl.BlockSpec((tm, D_MODEL), lambda j, i, s: (first(j, i), 0)),
                pl.BlockSpec((1, D_MODEL), lambda j, i, s: (0, 0)),
                pl.BlockSpec((1, D_MODEL, D_MODEL), lambda j, i, s: (s[j // 2], 0, j % 2)),
            ] + [ANY_SPEC] * len(deps),
            out_specs=[
                pl.BlockSpec((tm, D_MODEL), lambda j, i, s: (i, 2 * s[j // 2] + j % 2)),
                pl.BlockSpec((tm, D_MODEL), lambda j, i, s: (first(j, i), 0)),
                pl.BlockSpec((D_MODEL, tm), lambda j, i, s: (0, first(j, i))),
            ],
            scratch_shapes=[pltpu.VMEM((T, D_MODEL), _MXU)],
        ),
        out_shape=[
            jax.ShapeDtypeStruct((T, N_GROUPS * D_MODEL), F32),
            jax.ShapeDtypeStruct((T, D_MODEL), _MXU),
            jax.ShapeDtypeStruct((D_MODEL, T), _MXU),
        ],
        compiler_params=_params(("arbitrary", "arbitrary")),
    )(chips, x2d, g_in, w_all, *deps)


def _inproj_more(hb, w_all, chips, proj, name):
    T = hb.shape[0]
    tm = min(1024, T)

    def body(s_ref, hb_hbm, w_ref, prev_ref, proj_ref, h_all, sem):
        @pl.when((pl.program_id(0) == 0) & (pl.program_id(1) == 0))
        def _():
            cp = pltpu.make_async_copy(hb_hbm, h_all, sem)
            cp.start()
            cp.wait()

        rows = pl.ds(pl.multiple_of(pl.program_id(1) * tm, tm), tm)
        proj_ref[...] = _dot(h_all[rows, :], w_ref[0])

    return pl.pallas_call(
        body,
        name=name,
        grid_spec=pltpu.PrefetchScalarGridSpec(
            num_scalar_prefetch=1,
            grid=(2 * chips.shape[0], T // tm),
            in_specs=[
                ANY_SPEC,
                pl.BlockSpec((1, D_MODEL, D_MODEL), lambda j, i, s: (s[j // 2], 0, j % 2)),
                ANY_SPEC,
            ],
            out_specs=pl.BlockSpec((tm, D_MODEL), lambda j, i, s: (i, 2 * s[j // 2] + j % 2)),
            scratch_shapes=[pltpu.VMEM((T, D_MODEL), hb.dtype), pltpu.SemaphoreType.DMA],
        ),
        out_shape=jax.ShapeDtypeStruct(proj.shape, F32),
        input_output_aliases={3: 0},
        compiler_params=_params(("arbitrary", "arbitrary")),
    )(chips, hb, w_all, proj)


def _scan_fwd(a, u):
    n = a.shape[0]
    row = lax.broadcasted_iota(jnp.int32, a.shape, 0)
    s = 1
    while s < n:
        m = row >= s
        u = u + a * jnp.where(m, pltpu.roll(u, s, 0), 0.0)
        a = a * jnp.where(m, pltpu.roll(a, s, 0), 1.0)
        s *= 2
    return a, u


def _scan_bwd(b, g):
    n = b.shape[0]
    row = lax.broadcasted_iota(jnp.int32, b.shape, 0)
    s = 1
    while s < n:
        m = row < n - s
        g = g + b * jnp.where(m, pltpu.roll(g, n - s, 0), 0.0)
        b = b * jnp.where(m, pltpu.roll(b, n - s, 0), 1.0)
        s *= 2
    return b, g


LANES = 128
SUBLANES = 8


def _scan_scratch(tc):
    by_lanes = pltpu.VMEM((CW // LANES, tc, LANES), F32)
    return [by_lanes, by_lanes, pltpu.VMEM((tc // SUBLANES, CW), F32), pltpu.VMEM((tc, CW), F32)]


def _scan_tile(a, u, edge, la_ref, lh_ref, c_ref, dst_ref, reverse):
    n, w = a.shape
    groups = n // SUBLANES
    a3 = a.reshape(groups, SUBLANES, w)
    u3 = u.reshape(groups, SUBLANES, w)
    row = lax.broadcasted_iota(jnp.int32, a3.shape, 1)
    for s in (1, 2, 4):
        m = (row < SUBLANES - s) if reverse else (row >= s)
        shift = SUBLANES - s if reverse else s
        u3 = u3 + a3 * jnp.where(m, pltpu.roll(u3, shift, 1), 0.0)
        a3 = a3 * jnp.where(m, pltpu.roll(a3, shift, 1), 1.0)
    al = a3.reshape(n, w)
    hl = u3.reshape(n, w)
    blocks = w // LANES
    for q in range(blocks):
        la_ref[q] = al[:, q * LANES:(q + 1) * LANES]
        lh_ref[q] = hl[:, q * LANES:(q + 1) * LANES]
    ends = pl.ds(0 if reverse else SUBLANES - 1, groups, stride=SUBLANES)
    end_a = jnp.concatenate([la_ref.at[q][ends, :] for q in range(blocks)], axis=-1)
    end_h = jnp.concatenate([lh_ref.at[q][ends, :] for q in range(blocks)], axis=-1)
    prod, part = (_scan_bwd if reverse else _scan_fwd)(end_a, end_h)
    total = part + prod * edge
    g_row = lax.broadcasted_iota(jnp.int32, total.shape, 0)
    if reverse:
        c_ref[...] = jnp.where(g_row == groups - 1, edge, pltpu.roll(total, groups - 1, 0))
    else:
        c_ref[...] = jnp.where(g_row == 0, edge, pltpu.roll(total, 1, 0))
    for g in range(groups):
        rows = slice(g * SUBLANES, (g + 1) * SUBLANES)
        for q in range(blocks):
            cols = slice(q * LANES, (q + 1) * LANES)
            dst_ref[rows, cols] = lh_ref[q, rows, :] + la_ref[q, rows, :] * c_ref[g:g + 1, cols]


def _softplus_neg(lam):
    z = -lam
    return jnp.maximum(z, 0.0) + jnp.log1p(jnp.exp(-jnp.abs(z)))


def _lru_gates(xc, wx_ref, wa_ref, bx_ref, ba_ref, lam_ref):
    xcb = _c(xc)
    i_t = _sigmoid(_dot(xcb, wx_ref[0]) + bx_ref[...])
    r_t = _sigmoid(_dot(xcb, wa_ref[0]) + ba_ref[...])
    sp = _softplus_neg(lam_ref[...])
    log_a = (-LRU_C) * r_t * sp
    a = jnp.exp(log_a)
    mult = jnp.sqrt(1.0 - a * a)
    return xcb, i_t, r_t, sp, a, mult


def _conv_from_ext(ext_ref, xa, cw_ref, cb_ref, tc):
    return (cb_ref[...] + cw_ref[3:4, :] * xa + cw_ref[2:3, :] * ext_ref[7:7 + tc, :]
            + cw_ref[1:2, :] * ext_ref[6:6 + tc, :] + cw_ref[0:1, :] * ext_ref[5:5 + tc, :])


def _lru_fwd(proj, conv_w, conv_b, wx_bd, wa_bd, bx, ba, lam, B, S):
    T = B * S
    tc = min(256, S)
    nt = S // tc
    h8 = tc // 8

    def body(xa_ref, halo_ref, ga_ref, cw_ref, cb_ref, wx_ref, wa_ref, bx_ref, ba_ref, lam_ref,
             h_ref, ya_ref, ext_ref, carry_ref, la_ref, lh_ref, c_ref):
        t = pl.program_id(2)

        @pl.when(t == 0)
        def _():
            carry_ref[...] = jnp.zeros_like(carry_ref)

        xa = xa_ref[...]
        ext_ref[0:8, :] = jnp.where(t == 0, 0.0, halo_ref[...])
        ext_ref[8:8 + tc, :] = xa
        xc = _conv_from_ext(ext_ref, xa, cw_ref, cb_ref, tc)
        _, i_t, _, _, a, mult = _lru_gates(xc, wx_ref, wa_ref, bx_ref, ba_ref, lam_ref)
        u = mult * (i_t * xc)
        _scan_tile(a, u, carry_ref[7:8, :], la_ref, lh_ref, c_ref, h_ref, False)
        h = h_ref[...]
        carry_ref[...] = h[tc - 8:tc, :]
        ga = ga_ref[...]
        ya_ref[...] = (ga * _sigmoid(ga) * h).astype(ya_ref.dtype)

    row = lambda b, t: b * nt + t
    vec = pl.BlockSpec((1, CW), lambda b, c, t: (0, c))
    mat = pl.BlockSpec((1, CW, CW), lambda b, c, t: (c, 0, 0))
    return pl.pallas_call(
        body,
        name="lru_fwd",
        grid=(B, N_CT, nt),
        in_specs=[
            pl.BlockSpec((tc, CW), lambda b, c, t: (row(b, t), c)),
            pl.BlockSpec((8, CW), lambda b, c, t: (jnp.maximum(row(b, t) * h8 - 1, 0), c)),
            pl.BlockSpec((tc, CW), lambda b, c, t: (row(b, t), N_CT + c)),
            pl.BlockSpec((CONV, CW), lambda b, c, t: (0, c)),
            vec, mat, mat, vec, vec, vec,
        ],
        out_specs=[
            pl.BlockSpec((tc, CW), lambda b, c, t: (row(b, t), c)),
            pl.BlockSpec((tc, CW), lambda b, c, t: (row(b, t), c)),
        ],
        out_shape=[
            jax.ShapeDtypeStruct((T, D_MODEL), F32),
            jax.ShapeDtypeStruct((T, D_MODEL), _MXU),
        ],
        scratch_shapes=[pltpu.VMEM((tc + 8, CW), F32), pltpu.VMEM((8, CW), F32)] + _scan_scratch(tc)[:3],
        compiler_params=_params(("parallel", "parallel", "arbitrary")),
    )(proj, proj, proj, conv_w, conv_b, wx_bd, wa_bd, bx, ba, lam)


def _lru_bwd(dya, proj, hlru, conv_w, conv_b, wx_bd, wa_bd, bx, ba, lam, B, S, deps=()):
    T = B * S
    tc = min(256, S)
    nt = S // tc
    h8 = tc // 8

    def body(dya_ref, xa_ref, xhalo_ref, ga_ref, h_ref, hhalo_ref, cw_ref, cb_ref, wx_ref, wa_ref, bx_ref, ba_ref,
             lam_ref, dxa_ref, dga_ref, dcw_ref, dcb_ref, dwx_ref, dwa_ref, dbx_ref, dba_ref, dlam_ref,
             ext_ref, ext2_ref, carry_ref, dhalo_ref, la_ref, lh_ref, c_ref, dh_ref):
        b = pl.program_id(1)
        t = pl.program_id(2)
        tt = nt - 1 - t

        @pl.when(t == 0)
        def _():
            carry_ref[...] = jnp.zeros_like(carry_ref)
            dhalo_ref[...] = jnp.zeros_like(dhalo_ref)

        @pl.when((t == 0) & (b == 0))
        def _():
            for r in (dcw_ref, dcb_ref, dwx_ref, dwa_ref, dbx_ref, dba_ref, dlam_ref):
                r[...] = jnp.zeros_like(r)

        xa = xa_ref[...]
        ext_ref[0:8, :] = jnp.where(tt == 0, 0.0, xhalo_ref[...])
        ext_ref[8:8 + tc, :] = xa
        xc = _conv_from_ext(ext_ref, xa, cw_ref, cb_ref, tc)
        xcb, i_t, r_t, sp, a, mult = _lru_gates(xc, wx_ref, wa_ref, bx_ref, ba_ref, lam_ref)

        h = h_ref[...]
        ga = ga_ref[...]
        dya_t = dya_ref[...]
        sg = _sigmoid(ga)
        dga_ref[...] = (dya_t * h * (sg * (1.0 + ga * (1.0 - sg)))).astype(dga_ref.dtype)
        dlru = dya_t * (ga * sg)

        row = lax.broadcasted_iota(jnp.int32, a.shape, 0)
        coef = jnp.where(row == tc - 1, 1.0, pltpu.roll(a, tc - 1, 0))
        _scan_tile(coef, dlru, carry_ref[0:1, :], la_ref, lh_ref, c_ref, dh_ref, True)
        dh = dh_ref[...]
        ext2_ref[0:tc, :] = a * dh
        carry_ref[...] = ext2_ref[0:8, :]

        ext2_ref[0:8, :] = jnp.where(tt == 0, 0.0, hhalo_ref[...])
        ext2_ref[8:8 + tc, :] = h
        hprev = ext2_ref[7:7 + tc, :]

        da = dh * hprev
        ix = i_t * xc
        dmult = dh * ix
        di = dh * mult * xc
        dxc = dh * mult * i_t
        dlog_a = da * a - dmult * (a * a) / mult
        dr = dlog_a * ((-LRU_C) * sp)
        dlam_ref[...] += jnp.sum(dlog_a * r_t, axis=0, keepdims=True) * (LRU_C * _sigmoid(-lam_ref[...]))
        dza = dr * r_t * (1.0 - r_t)
        dzx = di * i_t * (1.0 - i_t)
        dzab = _c(dza)
        dzxb = _c(dzx)
        dxc = dxc + _dot_nt(dzxb, wx_ref[0]) + _dot_nt(dzab, wa_ref[0])
        dwx_ref[0] += _dot_tn(xcb, dzxb)
        dwa_ref[0] += _dot_tn(xcb, dzab)
        dbx_ref[...] += jnp.sum(dzx, axis=0, keepdims=True)
        dba_ref[...] += jnp.sum(dza, axis=0, keepdims=True)

        dcb_ref[...] += jnp.sum(dxc, axis=0, keepdims=True)
        dcw_ref[3:4, :] += jnp.sum(dxc * xa, axis=0, keepdims=True)
        dcw_ref[2:3, :] += jnp.sum(dxc * ext_ref[7:7 + tc, :], axis=0, keepdims=True)
        dcw_ref[1:2, :] += jnp.sum(dxc * ext_ref[6:6 + tc, :], axis=0, keepdims=True)
        dcw_ref[0:1, :] += jnp.sum(dxc * ext_ref[5:5 + tc, :], axis=0, keepdims=True)
        ext2_ref[0:tc, :] = dxc
        ext2_ref[tc:tc + 8, :] = dhalo_ref[...]
        dxa = (cw_ref[3:4, :] * dxc + cw_ref[2:3, :] * ext2_ref[1:1 + tc, :]
               + cw_ref[1:2, :] * ext2_ref[2:2 + tc, :] + cw_ref[0:1, :] * ext2_ref[3:3 + tc, :])
        dxa_ref[...] = dxa.astype(dxa_ref.dtype)
        dhalo_ref[...] = ext2_ref[0:8, :]

    row_of = lambda b, t: b * nt + (nt - 1 - t)
    tile = lambda off: pl.BlockSpec((tc, CW), lambda c, b, t: (row_of(b, t), off + c))
    halo = pl.BlockSpec((8, CW), lambda c, b, t: (jnp.maximum(row_of(b, t) * h8 - 1, 0), c))
    vec = pl.BlockSpec((1, CW), lambda c, b, t: (0, c))
    mat = pl.BlockSpec((1, CW, CW), lambda c, b, t: (c, 0, 0))
    cwspec = pl.BlockSpec((CONV, CW), lambda c, b, t: (0, c))
    return pl.pallas_call(
        _after(body, 13, deps),
        name="lru_bwd",
        grid=(N_CT, B, nt),
        in_specs=[tile(0), tile(0), halo, tile(N_CT), tile(0), halo, cwspec, vec, mat, mat, vec, vec, vec]
        + [ANY_SPEC] * len(deps),
        out_specs=[tile(0), tile(0), cwspec, vec, mat, mat, vec, vec, vec],
        out_shape=[
            jax.ShapeDtypeStruct((T, D_MODEL), _MXU),
            jax.ShapeDtypeStruct((T, D_MODEL), _MXU),
            jax.ShapeDtypeStruct((CONV, D_MODEL), F32),
            jax.ShapeDtypeStruct((1, D_MODEL), F32),
            jax.ShapeDtypeStruct((N_CT, CW, CW), F32),
            jax.ShapeDtypeStruct((N_CT, CW, CW), F32),
            jax.ShapeDtypeStruct((1, D_MODEL), F32),
            jax.ShapeDtypeStruct((1, D_MODEL), F32),
            jax.ShapeDtypeStruct((1, D_MODEL), F32),
        ],
        scratch_shapes=[pltpu.VMEM((tc + 8, CW), F32), pltpu.VMEM((tc + 8, CW), F32),
                        pltpu.VMEM((8, CW), F32), pltpu.VMEM((8, CW), F32)] + _scan_scratch(tc),
        compiler_params=_params(("parallel", "arbitrary", "arbitrary")),
    )(dya, proj, proj, proj, hlru, hlru, conv_w, conv_b, wx_bd, wa_bd, bx, ba, lam, *deps)


def _retention_tables(S):
    half = DK // 2
    freqs = ROPE_THETA ** (-jnp.arange(half, dtype=F32) / half)
    ang = jnp.arange(S, dtype=F32)[:, None] * freqs[None, :]
    log_g = jnp.log1p(-(2.0 ** (-5.0 - jnp.arange(HEADS, dtype=F32))))
    idx = jnp.arange(CHUNK, dtype=F32)
    diff = idx[:, None] - idx[None, :]
    inner = jnp.where(diff >= 0, jnp.exp(jnp.maximum(diff, 0.0)[None] * log_g[:, None, None]), 0.0)
    cross = jnp.exp((idx[None, :] + 1.0) * log_g[:, None])[:, :, None]
    state = jnp.exp((CHUNK - 1.0 - idx[None, :]) * log_g[:, None])[:, :, None]
    gam = jnp.broadcast_to(jnp.exp(CHUNK * log_g)[:, None, None], (HEADS, 1, DK))
    return jnp.cos(ang), jnp.sin(ang), inner, cross, state, gam


def _rot(x, cos, sin):
    half = DK // 2
    x1, x2 = x[:, :half], x[:, half:]
    return jnp.concatenate([x1 * cos - x2 * sin, x1 * sin + x2 * cos], axis=-1)


def _rot_t(y, cos, sin):
    half = DK // 2
    y1, y2 = y[:, :half], y[:, half:]
    return jnp.concatenate([y1 * cos + y2 * sin, y2 * cos - y1 * sin], axis=-1)


def _groupnorm(o):
    mu = jnp.mean(o, axis=-1, keepdims=True)
    oc = o - mu
    rs = lax.rsqrt(jnp.mean(oc * oc, axis=-1, keepdims=True) + EPS)
    return oc * rs, rs


def _ret_specs(B, chunk_of):
    qkv = lambda g: pl.BlockSpec((B, CHUNK, D_MODEL), lambda c: (0, chunk_of(c), g))
    act = pl.BlockSpec((B, CHUNK, D_MODEL), lambda c: (0, chunk_of(c), 0))
    rope = pl.BlockSpec((CHUNK, DK // 2), lambda c: (chunk_of(c), 0))
    dmat = pl.BlockSpec((HEADS, CHUNK, CHUNK), lambda c: (0, 0, 0))
    dvec = pl.BlockSpec((HEADS, CHUNK, 1), lambda c: (0, 0, 0))
    hrow = pl.BlockSpec((HEADS, 1, DK), lambda c: (0, 0, 0))
    rst = pl.BlockSpec((1, B, HEADS, DK, DK), lambda c: (chunk_of(c), 0, 0, 0, 0))
    return qkv, act, rope, dmat, dvec, hrow, rst


def _ret_fwd(proj, tables, gain3, B, S):
    T = B * S
    nc = S // CHUNK
    cos, sin, dmat_t, cd_t, sd_t, gam_t = tables

    def body(q_ref, k_ref, v_ref, gb_ref, cos_ref, sin_ref, dm_ref, cd_ref, sd_ref, gam_ref, gain_ref,
             o_ref, yb_ref, rs_ref, state_ref):
        @pl.when(pl.program_id(0) == 0)
        def _():
            state_ref[...] = jnp.zeros_like(state_ref)

        cos_t, sin_t = cos_ref[...], sin_ref[...]
        for b, h in [(b, h) for b in range(B) for h in range(HEADS)]:
            cols = slice(h * DK, (h + 1) * DK)
            qb = _c(_rot(q_ref[b, :, cols], cos_t, sin_t))
            kb = _c(_rot(k_ref[b, :, cols], cos_t, sin_t) * (DK ** -0.5))
            v = v_ref[b, :, cols]
            state = state_ref[b, h]
            sb = _c(state)
            rs_ref[0, b, h] = sb
            scores = _dot_nt(qb, kb) * dm_ref[h]
            o = _dot(_c(scores), _c(v)) + _dot(qb, sb) * cd_ref[h]
            state_ref[b, h] = gam_ref[h] * state + _dot_tn(kb, _c(v * sd_ref[h]))
            o_ref[b, :, cols] = o
            n, _ = _groupnorm(o)
            gb = gb_ref[b, :, cols]
            yb_ref[b, :, cols] = (gb * _sigmoid(gb) * (n * gain_ref[h])).astype(yb_ref.dtype)

    qkv, act, rope, dmat, dvec, hrow, rst = _ret_specs(B, lambda c: c)
    proj3 = proj.reshape(B, S, proj.shape[1])
    o_pre, yb, states = pl.pallas_call(
        body,
        name="ret_fwd",
        grid=(nc,),
        in_specs=[qkv(2), qkv(3), qkv(4), qkv(5), rope, rope, dmat, dvec, dvec, hrow, hrow],
        out_specs=[act, act, rst],
        out_shape=[
            jax.ShapeDtypeStruct((B, S, D_MODEL), F32),
            jax.ShapeDtypeStruct((B, S, D_MODEL), _MXU),
            jax.ShapeDtypeStruct((nc, B, HEADS, DK, DK), _MXU),
        ],
        scratch_shapes=[pltpu.VMEM((B, HEADS, DK, DK), F32)],
        compiler_params=_params(("arbitrary",)),
    )(proj3, proj3, proj3, proj3, cos, sin, dmat_t, cd_t, sd_t, gam_t, gain3)
    return o_pre.reshape(T, D_MODEL), yb.reshape(T, D_MODEL), states


def _ret_bwd(dyb, o_pre, proj, states, tables, gain3, B, S, deps=()):
    T = B * S
    nc = S // CHUNK
    cos, sin, dmat_t, cd_t, sd_t, gam_t = tables

    def body(dyb_ref, o_ref, q_ref, k_ref, v_ref, gb_ref, rs_ref, cos_ref, sin_ref, dm_ref, cd_ref, sd_ref, gam_ref,
             gain_ref, dr_ref, dgain_ref, dstate_ref):
        @pl.when(pl.program_id(0) == 0)
        def _():
            dstate_ref[...] = jnp.zeros_like(dstate_ref)
            dgain_ref[...] = jnp.zeros_like(dgain_ref)

        cos_t, sin_t = cos_ref[...], sin_ref[...]
        for b, h in [(b, h) for b in range(B) for h in range(HEADS)]:
            cols = slice(h * DK, (h + 1) * DK)
            gain = gain_ref[h]
            n, rs = _groupnorm(o_ref[b, :, cols])
            gb = gb_ref[b, :, cols]
            sg = _sigmoid(gb)
            dy = dyb_ref[b, :, cols]
            part = lambda g: slice(g * D_MODEL + h * DK, g * D_MODEL + (h + 1) * DK)
            dr_ref[b, :, part(3)] = (dy * (n * gain) * (sg * (1.0 + gb * (1.0 - sg)))).astype(dr_ref.dtype)
            dgn = dy * (gb * sg)
            dgain_ref[h] += jnp.sum(dgn * n, axis=0, keepdims=True)
            dn = dgn * gain
            do = rs * (dn - jnp.mean(dn, axis=-1, keepdims=True) - n * jnp.mean(dn * n, axis=-1, keepdims=True))

            qb = _c(_rot(q_ref[b, :, cols], cos_t, sin_t))
            kb = _c(_rot(k_ref[b, :, cols], cos_t, sin_t) * (DK ** -0.5))
            v = v_ref[b, :, cols]
            vb = _c(v)
            vsb = _c(v * sd_ref[h])
            dob = _c(do)
            docb = _c(do * cd_ref[h])
            dmat = dm_ref[h]
            dstate = dstate_ref[b, h]
            dsb = _c(dstate)
            pb = _c(_dot_nt(qb, kb) * dmat)
            dsc = _c(_dot_nt(dob, vb) * dmat)
            dq = _dot(dsc, kb) + _dot_nt(docb, rs_ref[0, b, h])
            dk = _dot_tn(dsc, qb) + _dot_nt(vsb, dsb)
            dv = _dot_tn(pb, dob) + _dot(kb, dsb) * sd_ref[h]
            dstate_ref[b, h] = gam_ref[h] * dstate + _dot_tn(qb, docb)
            dr_ref[b, :, part(0)] = _rot_t(dq, cos_t, sin_t).astype(dr_ref.dtype)
            dr_ref[b, :, part(1)] = (_rot_t(dk, cos_t, sin_t) * (DK ** -0.5)).astype(dr_ref.dtype)
            dr_ref[b, :, part(2)] = dv.astype(dr_ref.dtype)

    qkv, act, rope, dmat, dvec, hrow, rst = _ret_specs(B, lambda c: nc - 1 - c)
    wide = pl.BlockSpec((B, CHUNK, 4 * D_MODEL), lambda c: (0, nc - 1 - c, 0))
    proj3 = proj.reshape(B, S, proj.shape[1])
    dr, dgain = pl.pallas_call(
        _after(body, 14, deps),
        name="ret_bwd",
        grid=(nc,),
        in_specs=[act, act, qkv(2), qkv(3), qkv(4), qkv(5), rst, rope, rope, dmat, dvec, dvec, hrow, hrow]
        + [ANY_SPEC] * len(deps),
        out_specs=[wide, hrow],
        out_shape=[jax.ShapeDtypeStruct((B, S, 4 * D_MODEL), _MXU), jax.ShapeDtypeStruct((HEADS, 1, DK), F32)],
        scratch_shapes=[pltpu.VMEM((B, HEADS, DK, DK), F32)],
        compiler_params=_params(("arbitrary",)),
    )(dyb.reshape(B, S, D_MODEL), o_pre.reshape(B, S, D_MODEL), proj3, proj3, proj3, proj3, states, cos, sin, dmat_t,
      cd_t, sd_t, gam_t, gain3, *deps)
    return dr.reshape(T, 4 * D_MODEL), dgain


def _mid(ya, yb, proj, x2d, tgt2d, wpa, wpb, wout, g_fin):
    T = x2d.shape[0]
    tm = min(256, T)
    n_steps = T // tm
    rows = D_MODEL // (2 * N_CHIPS)

    def body(ya_ref, yb_ref, ma_ref, mb_ref, x_ref, t_ref, gf_ref, wpa_hbm, wpb_hbm, wout_hbm,
             loss_ref, dx2_ref, dya_ref, dyb_ref, dm_ref, dgf_ref, gw_hbm, w_ref, acc_ref, sem):
        i = pl.program_id(0)

        @pl.when(i == 0)
        def _():
            loads = [pltpu.make_async_copy(src, w_ref.at[k], sem.at[k]) for k, src in enumerate((wpa_hbm, wpb_hbm, wout_hbm))]
            for cp in loads:
                cp.start()
            for cp in loads:
                cp.wait()
            acc_ref[...] = jnp.zeros_like(acc_ref)
            loss_ref[...] = jnp.zeros_like(loss_ref)
            dgf_ref[...] = jnp.zeros_like(dgf_ref)

        ya_t, yb_t = ya_ref[...], yb_ref[...]
        out_a = _dot(ya_t, w_ref[0])
        out_b = _dot(yb_t, w_ref[1])
        sa = _sigmoid(ma_ref[...])
        sb = _sigmoid(mb_ref[...])
        mgb = _c(sa * out_a + sb * out_b)
        x2 = x_ref[...] + _dot(mgb, w_ref[2])
        r2 = lax.rsqrt(jnp.mean(x2 * x2, axis=-1, keepdims=True) + EPS)
        nx = x2 * r2
        gf = gf_ref[...]
        err = nx * gf - t_ref[...]
        loss_ref[...] += 0.5 * jnp.sum(jnp.mean(err * err, axis=-1, keepdims=True), axis=0, keepdims=True)
        dy = err * (1.0 / D_MODEL)
        dgf_ref[...] += jnp.sum(dy * nx, axis=0, keepdims=True)
        dyg = dy * gf
        dx2 = r2 * (dyg - nx * jnp.mean(dyg * nx, axis=-1, keepdims=True))
        dx2_ref[...] = dx2
        dx2b = _c(dx2)
        dmg = _dot_nt(dx2b, w_ref[2])
        acc_ref[2] += _dot_tn(mgb, dx2b)
        dm_ref[:, :D_MODEL] = (dmg * out_a * sa * (1.0 - sa)).astype(dm_ref.dtype)
        dm_ref[:, D_MODEL:] = (dmg * out_b * sb * (1.0 - sb)).astype(dm_ref.dtype)
        dab = _c(dmg * sa)
        dbb = _c(dmg * sb)
        dya_ref[...] = _dot_nt(dab, w_ref[0])
        dyb_ref[...] = _dot_nt(dbb, w_ref[1])
        acc_ref[0] += _dot_tn(ya_t, dab)
        acc_ref[1] += _dot_tn(yb_t, dbb)

        @pl.when(i == n_steps - 1)
        def _():
            copies = [pltpu.make_async_copy(acc_ref.at[k, pl.ds((2 * p + hf) * rows, rows), :], gw_hbm.at[p, hf, k],
                                            sem.at[(k * N_CHIPS + p) * 2 + hf])
                      for k in range(3) for p in range(N_CHIPS) for hf in range(2)]
            for cp in copies:
                cp.start()
            for cp in copies:
                cp.wait()

    tile = lambda j: pl.BlockSpec((tm, D_MODEL), lambda i: (i, j))
    one = pl.BlockSpec((1, D_MODEL), lambda i: (0, 0))
    anyspec = pl.BlockSpec(memory_space=pl.ANY)
    return pl.pallas_call(
        body,
        name="mid",
        grid=(n_steps,),
        in_specs=[tile(0), tile(0), tile(6), tile(7), tile(0), tile(0), one, anyspec, anyspec, anyspec],
        out_specs=[pl.BlockSpec((1, 1), lambda i: (0, 0)), tile(0), tile(0), tile(0),
                   pl.BlockSpec((tm, 2 * D_MODEL), lambda i: (i, 0)), one, anyspec],
        out_shape=[
            jax.ShapeDtypeStruct((1, 1), F32),
            jax.ShapeDtypeStruct((T, D_MODEL), F32),
            jax.ShapeDtypeStruct((T, D_MODEL), F32),
            jax.ShapeDtypeStruct((T, D_MODEL), F32),
            jax.ShapeDtypeStruct((T, 2 * D_MODEL), _MXU),
            jax.ShapeDtypeStruct((1, D_MODEL), F32),
            jax.ShapeDtypeStruct((N_CHIPS, 2, 3, rows, D_MODEL), F32),
        ],
        scratch_shapes=[pltpu.VMEM((3, D_MODEL, D_MODEL), _MXU), pltpu.VMEM((3, D_MODEL, D_MODEL), F32),
                        pltpu.SemaphoreType.DMA((3 * N_CHIPS * 2,))],
        compiler_params=_params(("arbitrary",)),
    )(ya, yb, proj, proj, x2d, tgt2d, g_fin, wpa, wpb, wout)


DX_TILE = 512


def _inproj_bwd_dx(dparts, w_all, x2d, dx2, g_in, first, count, prev, name, deps=()):
    T = x2d.shape[0]
    tm = min(DX_TILE, T)
    n_d = len(dparts)
    groups = [(a, k) for a, d in enumerate(dparts) for k in range(d.shape[1] // D_MODEL)]
    dg_start = jnp.zeros((1, D_MODEL), F32) if prev is None else prev[1]
    carried = () if prev is None else (prev[0],)

    def body(*refs):
        d_refs = refs[:n_d]
        x_ref, dx2_ref, g_ref, dg0_ref, w_hbm = refs[n_d:n_d + 5]
        dx_ref, dg_ref, w_ref, sem = refs[-4:]

        @pl.when(pl.program_id(0) == 0)
        def _():
            cp = pltpu.make_async_copy(w_hbm, w_ref, sem)
            cp.start()
            cp.wait()
            dg_ref[...] = dg0_ref[...]

        dh = jnp.zeros((tm, D_MODEL), F32)
        for j, (a, k) in enumerate(groups):
            dh = dh + _dot_nt(d_refs[a][:, k * D_MODEL:(k + 1) * D_MODEL],
                              w_ref[j // 2, :, (j % 2) * D_MODEL:(j % 2 + 1) * D_MODEL])
        x = x_ref[...]
        r = lax.rsqrt(jnp.mean(x * x, axis=-1, keepdims=True) + EPS)
        nx = x * r
        dg_ref[...] += jnp.sum(dh * nx, axis=0, keepdims=True)
        dhg = dh * g_ref[...]
        dx_ref[...] = dx2_ref[...] + r * (dhg - nx * jnp.mean(dhg * nx, axis=-1, keepdims=True))

    tile = pl.BlockSpec((tm, D_MODEL), lambda i: (first + i, 0))
    one = pl.BlockSpec((1, D_MODEL), lambda i: (0, 0))
    return pl.pallas_call(
        body,
        name=name,
        grid=(count,),
        in_specs=[pl.BlockSpec((tm, d.shape[1]), lambda i: (first + i, 0)) for d in dparts]
        + [tile, tile, one, one, ANY_SPEC] + [ANY_SPEC] * (len(carried) + len(deps)),
        out_specs=[tile, one],
        out_shape=[jax.ShapeDtypeStruct((T, D_MODEL), F32), jax.ShapeDtypeStruct((1, D_MODEL), F32)],
        input_output_aliases={n_d + 5: 0} if carried else {},
        scratch_shapes=[pltpu.VMEM(w_all.shape, w_all.dtype), pltpu.SemaphoreType.DMA],
        compiler_params=_params(("arbitrary",)),
    )(*dparts, x2d, dx2, g_in, dg_start, w_all, *carried, *deps)


def _inproj_bwd_dw(ht, dparts, name, deps=()):
    T = ht.shape[1]
    tn = 512
    half = D_MODEL // 2
    per_chip = 2 * D_MODEL // tn
    n_d = len(dparts)
    tiles = [(a, t) for a, d in enumerate(dparts) for t in range(d.shape[1] // tn)]
    offs = [sum(d.shape[1] // tn for d in dparts[:a]) for a in range(n_d)]

    def body(*refs):
        ht_ref = refs[0]
        d_refs = refs[1:1 + n_d]
        out_ref = refs[-1]
        t = pl.program_id(0)

        for a in range(n_d):
            lo, hi = offs[a], offs[a] + dparts[a].shape[1] // tn

            @pl.when((t >= lo) & (t < hi))
            def _(a=a):
                g = _dot(ht_ref[...], d_refs[a][...])
                out_ref[0, 0] = g[:half]
                out_ref[0, 1] = g[half:]

    def dspec(a):
        n_a = dparts[a].shape[1] // tn
        return pl.BlockSpec((T, tn), lambda t: (0, jnp.clip(t - offs[a], 0, n_a - 1)))

    return pl.pallas_call(
        body,
        name=name,
        grid=(len(tiles),),
        in_specs=[pl.BlockSpec((D_MODEL, T), lambda t: (0, 0))] + [dspec(a) for a in range(n_d)]
        + [ANY_SPEC] * len(deps),
        out_specs=pl.BlockSpec((1, 2, half, tn), lambda t: (t // per_chip, 0, 0, t % per_chip)),
        out_shape=jax.ShapeDtypeStruct((len(tiles) // per_chip, 2, half, 2 * D_MODEL), F32),
        compiler_params=_params(("parallel",)),
    )(ht, *dparts, *deps)


def _coords():
    return lax.axis_index("x"), lax.axis_index("y"), lax.axis_index("c")


def _other_chips(x, y):
    return [(1 - x, y), (x, 1 - y), (1 - x, 1 - y)]


def _chunks(rows, n):
    size = rows // n
    return [pl.ds(q * size, size) for q in range(n)]


HBM_SPEC = pl.BlockSpec(memory_space=pltpu.HBM)
SEM_SPEC = pl.BlockSpec(memory_space=pltpu.SEMAPHORE)
DATAFLOW = pltpu.SideEffectType.DATAFLOW_SIDE_EFFECTING


def _copies_start(bufs, plan, n_copies, name, deps=()):
    n = len(bufs)
    n_deps = len(deps)

    def body(*refs):
        ins = refs[:n]
        send_sems, recv_sems = refs[n + n_deps], refs[n + n_deps + 1]
        token = refs[-1]
        for k, send, _ in plan(ins):
            if send is not None:
                src, dst, dev, pred = send
                cp = pltpu.make_async_remote_copy(src_ref=src, dst_ref=dst, send_sem=send_sems.at[k],
                                                  recv_sem=recv_sems.at[k], device_id=dev, device_id_type=MESH)
                if pred is None:
                    cp.start()
                else:
                    pl.when(pred)(cp.start)
        token[...] = jnp.zeros_like(token)

    hbm = [pltpu.with_memory_space_constraint(b, pltpu.HBM) for b in bufs]
    outs = pl.pallas_call(
        body,
        name=name,
        in_specs=[HBM_SPEC] * n + [ANY_SPEC] * n_deps,
        out_specs=(SEM_SPEC, SEM_SPEC, *([HBM_SPEC] * n), pl.BlockSpec(memory_space=pltpu.VMEM)),
        out_shape=(pltpu.SemaphoreType.DMA((n_copies,)), pltpu.SemaphoreType.DMA((n_copies,)),
                   *[pltpu.HBM(b.shape, b.dtype) for b in bufs], jax.ShapeDtypeStruct((8, 128), F32)),
        input_output_aliases={a: 2 + a for a in range(n)},
        compiler_params=pltpu.CompilerParams(has_side_effects=DATAFLOW),
    )(*hbm, *deps)
    return outs[0], outs[1], list(outs[2:2 + n]), outs[-1]


def _copies_wait(send_sems, recv_sems, bufs, after, plan, name, only=None):
    n = len(bufs)

    def body(*refs):
        ins = refs[:n]
        s_sems, r_sems = refs[n], refs[n + 1]
        for k, send, recv in plan(ins):
            if only is not None and k not in only:
                continue
            if send is not None:
                src, dst, dev, pred = send
                cp = pltpu.make_async_remote_copy(src_ref=src, dst_ref=dst, send_sem=s_sems.at[k],
                                                  recv_sem=r_sems.at[k], device_id=dev, device_id_type=MESH)
                if pred is None:
                    cp.wait_send()
                else:
                    pl.when(pred)(cp.wait_send)
            if recv is not None:
                dst, pred = recv
                cp = pltpu.make_async_remote_copy(src_ref=dst, dst_ref=dst, send_sem=s_sems.at[k],
                                                  recv_sem=r_sems.at[k], device_id=_coords(), device_id_type=MESH)
                if pred is None:
                    cp.wait_recv()
                else:
                    pl.when(pred)(cp.wait_recv)

    outs = pl.pallas_call(
        body,
        name=name,
        in_specs=[HBM_SPEC] * n + [SEM_SPEC, SEM_SPEC, pl.BlockSpec(memory_space=pl.ANY)],
        out_specs=[HBM_SPEC] * n,
        out_shape=[pltpu.HBM(b.shape, b.dtype) for b in bufs],
        input_output_aliases={a: a for a in range(n)},
        compiler_params=pltpu.CompilerParams(has_side_effects=DATAFLOW),
    )(*bufs, send_sems, recv_sems, after)
    return list(outs)


def _gather_plan(n_bufs):
    def plan(refs):
        x, y, c = _coords()
        me = 2 * x + y
        out = []
        for k, (px, py) in enumerate(_other_chips(x, y)):
            for a in range(n_bufs):
                out.append((k * n_bufs + a, (refs[a].at[me], refs[a].at[me], (px, py, c), None),
                            (refs[a].at[2 * px + py], None)))
        return out
    return plan


def _cast_into_slot(ws, name):
    n = len(ws)
    nt = 2

    def body(s_ref, *refs):
        for a in range(n):
            refs[n + a][0] = refs[a][...].astype(refs[n + a].dtype)

    xi, yi, _ = _coords()
    return pl.pallas_call(
        body,
        name=name,
        grid_spec=pltpu.PrefetchScalarGridSpec(
            num_scalar_prefetch=1,
            grid=(2, nt),
            in_specs=[pl.BlockSpec((1, w.shape[1] // nt, w.shape[2]), lambda hf, i, s: (hf, i, 0)) for w in ws],
            out_specs=[pl.BlockSpec((1, 1, w.shape[1] // nt, w.shape[2]), lambda hf, i, s: (s[0], hf, i, 0)) for w in ws],
        ),
        out_shape=[jax.ShapeDtypeStruct((N_CHIPS,) + w.shape, _MXU) for w in ws],
        compiler_params=_params(("parallel", "parallel")),
    )((2 * xi + yi).reshape(1).astype(jnp.int32), *ws)


def _chip_gather_plan(stage, n_bufs):
    def plan(refs):
        x, y, c = _coords()
        me = 2 * x + y
        near = [(1 - x, y), (x, 1 - y)]
        slots = [2 * (1 - x) + y, 2 * x + (1 - y), 2 * (1 - x) + (1 - y)]
        sibling = (x, y, 1 - c)
        pass_to = (jnp.where(c == 0, x, 1 - x), jnp.where(c == 0, 1 - y, y), c)
        pass_slot = jnp.where(c == 0, slots[0], slots[1])
        out = []

        def move(src_slot, to, land_slot, land_core, pieces):
            for a, buf in enumerate(refs):
                for rows in _chunks(buf.shape[2], pieces[a]):
                    out.append((len(out), (buf.at[src_slot, c, rows], buf.at[src_slot, c, rows], to, None),
                                (buf.at[land_slot, land_core, rows], None)))

        if stage == "near":
            for k, chip in enumerate(near):
                move(me, (*chip, c), slots[k], c, NEAR_PIECES[:n_bufs])
        elif stage == "pass":
            move(pass_slot, pass_to, slots[2], c, PASS_PIECES[:n_bufs])
            for k in range(2):
                move(slots[k], sibling, slots[k], 1 - c, [1] * n_bufs)
        else:
            move(slots[2], sibling, slots[2], 1 - c, [1] * n_bufs)
        return out
    return plan


NEAR_PIECES = (2, 1)
PASS_PIECES = (2, 1)


def _chip_gather_copies(stage, n_bufs):
    if stage == "near":
        return 2 * sum(NEAR_PIECES[:n_bufs]), None
    if stage == "pass":
        n_pass = sum(PASS_PIECES[:n_bufs])
        return n_pass + 2 * n_bufs, set(range(n_pass))
    return n_bufs, None


def _swap_plan(n_slabs):
    def plan(refs):
        x, y, c = _coords()
        out, k = [], 0
        for i, n in enumerate(n_slabs):
            g, land = refs[2 * i], refs[2 * i + 1]
            for p in range(n):
                out.append((k, (g.at[p, 1 - c], land.at[p], (x, y, 1 - c), None), (land.at[p], None)))
                k += 1
        return out
    return plan


def _is_one_of(chip, dests):
    hit = chip == dests[0]
    for d in dests[1:]:
        hit = hit | (chip == d)
    return hit


def _slab_of(chip, dests):
    return sum(j * (chip == d).astype(jnp.int32) for j, d in enumerate(dests))


def _scatter_plan(dest_sets):
    def plan(refs):
        x, y, c = _coords()
        me = 2 * x + y
        out = []
        for k, (px, py) in enumerate(_other_chips(x, y)):
            peer = 2 * px + py
            for i, dests in enumerate(dest_sets):
                cs, land = refs[2 * i], refs[2 * i + 1]
                everyone = len(dests) == N_CHIPS
                send = (cs.at[_slab_of(peer, dests)], land.at[k], (px, py, c),
                        None if everyone else _is_one_of(peer, dests))
                recv = (land.at[k], None if everyone else _is_one_of(me, dests))
                out.append((k * len(dest_sets) + i, send, recv))
        return out
    return plan


def _join_plan(rows, n_pieces):
    def plan(refs):
        x, y, c = _coords()
        (buf,) = refs
        return [(i, (buf.at[c, piece], buf.at[c, piece], (x, y, 1 - c), None), (buf.at[1 - c, piece], None))
                for i, piece in enumerate(_chunks(rows, n_pieces))]
    return plan


def _join_plans(parts):
    def plan(refs):
        out, b0, k0 = [], 0, 0
        for part_plan, n_bufs, n_copies in parts:
            out += [(k0 + k, send, recv) for k, send, recv in part_plan(refs[b0:b0 + n_bufs])]
            b0 += n_bufs
            k0 += n_copies
        return out
    return plan


def _allgather_plan():
    def plan(refs):
        x, y, c = _coords()
        (land,) = refs
        me = 4 * x + 2 * y + c
        out = []
        for r in range(1, 8):
            px = 1 - x if r & 4 else x
            py = 1 - y if r & 2 else y
            pc = 1 - c if r & 1 else c
            out.append((r - 1, (land.at[me], land.at[me], (px, py, pc), None), (land.at[4 * px + 2 * py + pc], None)))
        return out
    return plan


def _sum_gathered(land, name):
    def body(land_ref, o_ref):
        acc = land_ref[0]
        for d in range(1, 8):
            acc = acc + land_ref[d]
        o_ref[...] = acc

    return pl.pallas_call(
        body,
        name=name,
        out_shape=jax.ShapeDtypeStruct(land.shape[1:], F32),
        compiler_params=_params(),
    )(land)


def _join_halves(bufs, n_chunks, name, deps=()):
    n = len(bufs)
    pieces = [(a, rows) for a in range(n) for rows in _chunks(bufs[a].shape[1], n_chunks[a])]
    n_p = len(pieces)

    def body(*refs):
        outs = refs[-n - 2:-2]
        send_sems, recv_sems = refs[-2:]
        x, y, c = _coords()

        def copy(i, half):
            a, rows = pieces[i]
            return pltpu.make_async_remote_copy(
                src_ref=outs[a].at[half, rows], dst_ref=outs[a].at[half, rows], send_sem=send_sems.at[i],
                recv_sem=recv_sems.at[i], device_id=(x, y, 1 - c), device_id_type=MESH)

        sends = [copy(i, c) for i in range(n_p)]
        for cp in sends:
            cp.start()
        for i in range(n_p):
            copy(i, 1 - c).wait_recv()
        for cp in sends:
            cp.wait_send()

    anyspec = pl.BlockSpec(memory_space=pl.ANY)
    sems = pltpu.SemaphoreType.DMA((n_p,))
    return pl.pallas_call(
        body,
        name=name,
        in_specs=[anyspec] * (n + len(deps)),
        out_specs=[anyspec] * n,
        out_shape=[jax.ShapeDtypeStruct(b.shape, b.dtype) for b in bufs],
        input_output_aliases={a: a for a in range(n)},
        scratch_shapes=[sems, sems],
    )(*bufs, *deps)


def _row_tile(rows, cap):
    t = cap
    while rows % t:
        t //= 2
    return t


def _add_my_half(g, r, name):
    n_slabs, _, R, C = g.shape
    tr = R if n_slabs > 1 else _row_tile(R, 256)

    def body(c_ref, g_ref, r_ref, o_ref):
        o_ref[...] = (g_ref[0] + r_ref[...]).astype(o_ref.dtype)

    return pl.pallas_call(
        body,
        name=name,
        grid_spec=pltpu.PrefetchScalarGridSpec(
            num_scalar_prefetch=1,
            grid=(n_slabs, R // tr),
            in_specs=[pl.BlockSpec((1, 1, tr, C), lambda p, i, c_ref: (p, c_ref[0], i, 0)),
                      pl.BlockSpec((1, tr, C), lambda p, i, c_ref: (p, i, 0))],
            out_specs=pl.BlockSpec((1, tr, C), lambda p, i, c_ref: (p, i, 0)),
        ),
        out_shape=jax.ShapeDtypeStruct(r.shape, jnp.bfloat16),
        compiler_params=_params(("parallel", "parallel")),
    )(lax.axis_index("c").reshape(1).astype(jnp.int32), g, r)


def _sum_slabs(own, got, name, deps=()):
    _, R, C = own.shape
    tr = _row_tile(R, 256)

    def body(s_ref, own_ref, got_ref, *rest):
        rest[-1][0] = ((own_ref[0].astype(F32) + got_ref[0].astype(F32)) + got_ref[1].astype(F32)) + got_ref[2].astype(F32)

    xi, yi, ci = _coords()
    return pl.pallas_call(
        body,
        name=name,
        grid_spec=pltpu.PrefetchScalarGridSpec(
            num_scalar_prefetch=1,
            grid=(R // tr,),
            in_specs=[pl.BlockSpec((1, tr, C), lambda i, s: (s[0], i, 0)),
                      pl.BlockSpec((3, tr, C), lambda i, s: (0, i, 0))] + [ANY_SPEC] * len(deps),
            out_specs=pl.BlockSpec((1, tr, C), lambda i, s: (s[1], i, 0)),
        ),
        out_shape=jax.ShapeDtypeStruct((2, R, C), F32),
        compiler_params=_params(("parallel",)),
    )(jnp.stack([2 * xi + yi, ci]).astype(jnp.int32), own, got, *deps)


def _sum_parts(owns, got, dest_sets, name):
    n = len(owns)
    _, R, C = owns[0].shape
    tr = _row_tile(R, 256)

    def body(s_ref, *refs):
        got_ref, o_ref = refs[n], refs[-1]
        total = jnp.zeros((tr, C), F32)
        for i in range(n):
            total = total + jnp.where(s_ref[2 + 2 * i] == 1, refs[i][0].astype(F32), 0.0)
        o_ref[0] = ((total + got_ref[0].astype(F32)) + got_ref[1].astype(F32)) + got_ref[2].astype(F32)

    xi, yi, ci = _coords()
    me = 2 * xi + yi
    scalars = [ci, ci]
    for dests in dest_sets:
        scalars += [_is_one_of(me, dests).astype(jnp.int32), _slab_of(me, dests)]
    own_spec = lambda i: pl.BlockSpec((1, tr, C), lambda r, s: (s[3 + 2 * i], r, 0))
    return pl.pallas_call(
        body,
        name=name,
        grid_spec=pltpu.PrefetchScalarGridSpec(
            num_scalar_prefetch=1,
            grid=(R // tr,),
            in_specs=[own_spec(i) for i in range(n)] + [pl.BlockSpec((3, tr, C), lambda r, s: (0, r, 0))],
            out_specs=pl.BlockSpec((1, tr, C), lambda r, s: (s[0], r, 0)),
        ),
        out_shape=jax.ShapeDtypeStruct((2, R, C), F32),
        compiler_params=_params(("parallel",)),
    )(jnp.stack(scalars).astype(jnp.int32), *owns, got)


def _adamw_math(w, g, m, v):
    m = ADAM_B1 * m + (1.0 - ADAM_B1) * g
    v = ADAM_B2 * v + (1.0 - ADAM_B2) * (g * g)
    m_hat = m / (1.0 - ADAM_B1 ** ADAM_STEP)
    v_hat = v / (1.0 - ADAM_B2 ** ADAM_STEP)
    delta = -ADAM_LR * (m_hat / (jnp.sqrt(v_hat) + ADAM_EPS) + ADAM_WD * w)
    return delta, m, v


def _adamw_halves(ws, g, ms, vs, half, prev, name, deps=()):
    n = len(ws)
    _, _, R, C = g.shape
    tr = _row_tile(R, 128)
    steps = R // tr
    carried = [] if prev is None else [a for four in prev for a in four]
    both = half is None
    which = (lambda i, s: i // steps) if both else (lambda i, s: s[0])
    half = 0 if both else half

    def body(s_ref, *refs):
        w_refs, g_refs, m_refs, v_refs = (refs[k * n:(k + 1) * n] for k in range(4))
        outs = refs[len(refs) - 4 * n:]
        for a in range(n):
            grad = g_refs[a][0, 0]
            d, mn, vn = _adamw_math(w_refs[a][...], grad, m_refs[a][...], v_refs[a][...])
            for o, val in zip(outs[4 * a:4 * a + 4], (grad, d, mn, vn)):
                o[...] = val

    rows = pl.BlockSpec((tr, C), lambda i, s: (which(i, s) * steps + i % steps, 0))
    grad_spec = lambda a: pl.BlockSpec((1, 1, tr, C), lambda i, s: (which(i, s), a, i % steps, 0))
    n_in = 4 * n
    outs = pl.pallas_call(
        body,
        name=name,
        grid_spec=pltpu.PrefetchScalarGridSpec(
            num_scalar_prefetch=1,
            grid=(2 * steps if both else steps,),
            in_specs=[rows] * n + [grad_spec(a) for a in range(n)] + [rows] * (2 * n)
            + [ANY_SPEC] * (len(carried) + len(deps)),
            out_specs=[rows] * (4 * n),
        ),
        out_shape=[jax.ShapeDtypeStruct((2 * R, C), F32)] * (4 * n),
        input_output_aliases={1 + n_in + k: k for k in range(len(carried))},
        compiler_params=_params(("parallel",)),
    )(jnp.reshape(half, (1,)).astype(jnp.int32), *ws, *([g] * n), *ms, *vs, *carried, *deps)
    return [outs[4 * a:4 * a + 4] for a in range(n)]


def _adamw_small(ws, gs, ms, vs, name):
    n = len(ws)

    def body(*refs):
        for a in range(n):
            d, mn, vn = _adamw_math(refs[a][...], refs[n + a][...], refs[2 * n + a][...], refs[3 * n + a][...])
            refs[4 * n + a][...] = d
            refs[5 * n + a][...] = mn
            refs[6 * n + a][...] = vn

    shapes = [jax.ShapeDtypeStruct(w.shape, F32) for w in ws]
    outs = pl.pallas_call(
        body,
        name=name,
        out_shape=shapes * 3,
        compiler_params=_params(),
    )(*ws, *gs, *ms, *vs)
    return outs[:n], outs[n:2 * n], outs[2 * n:]


def _to_blockdiag(w):
    per = CW // LRU_BW
    w4 = w.reshape(N_CT, per, LRU_BW, LRU_BW)
    eye = jnp.eye(per, dtype=w.dtype)
    return (w4[:, :, :, None, :] * eye[None, :, None, :, None]).reshape(N_CT, CW, CW)


def _from_blockdiag(g):
    per = CW // LRU_BW
    g5 = g.reshape(N_CT, per, LRU_BW, per, LRU_BW)
    return jnp.stack([g5[:, b, :, b, :] for b in range(per)], axis=1).reshape(LRU_BLOCKS, LRU_BW, LRU_BW)


def _local_grads(x2d, tgt2d, B, S, g_in, in_proj, conv_b, gate_x_w, gate_x_b, gate_a_w, gate_a_b, lam,
                 proj_weights, g_fin, reduce):
    wx_bd = _c(_to_blockdiag(gate_x_w))
    wa_bd = _c(_to_blockdiag(gate_a_w))
    tables = _retention_tables(S)

    proj, ht, w_all, conv_w, gain = in_proj(x2d, g_in)
    gain3 = gain.reshape(HEADS, 1, DK)
    hlru, ya = _lru_fwd(proj, conv_w, conv_b, wx_bd, wa_bd, gate_x_b, gate_a_b, lam, B, S)
    o_pre, yb, states = _ret_fwd(proj, tables, gain3, B, S)
    wpa, wpb, wout = proj_weights(yb)
    loss, dx2, dya, dyb, dm, dgf, gw_proj = _mid(ya, yb, proj, x2d, tgt2d, wpa, wpb, wout, g_fin)
    g3 = _inproj_bwd_dw(ht, [dm], "inproj_bwd_dw_m")
    deps = reduce.m_ready(gw_proj, g3)
    dr, dgain = _ret_bwd(dyb, o_pre, proj, states, tables, gain3, B, S, deps)
    deps = reduce.ret_done(dr)
    g12 = _inproj_bwd_dw(ht, [dr], "inproj_bwd_dw_r", deps)
    deps = reduce.r_ready(g12)
    dxa, dga, dcw, dcb, dwx_bd, dwa_bd, dbx, dba, dlam = _lru_bwd(
        dya, proj, hlru, conv_w, conv_b, wx_bd, wa_bd, gate_x_b, gate_a_b, lam, B, S, deps)
    small = dict(conv_w=dcw, conv_b=dcb, gate_x_w=_from_blockdiag(dwx_bd), gate_x_b=dbx,
                 gate_a_w=_from_blockdiag(dwa_bd), gate_a_b=dba, lru_lambda=dlam, gn_gain=dgain.reshape(HEADS, DK),
                 norm_final=dgf)
    loss_rows = jnp.broadcast_to(loss, (SUBLANES, LANES))
    deps = reduce.lru_done(dxa, jnp.concatenate([_pack_small(small), loss_rows], axis=0))
    g0 = _inproj_bwd_dw(ht, [dxa, dga], "inproj_bwd_dw_a", deps)
    deps = reduce.a_ready(g0)
    n_tiles = x2d.shape[0] // min(DX_TILE, x2d.shape[0])
    grad_x, dgin = _inproj_bwd_dx([dxa, dga, dr, dm], w_all, x2d, dx2, g_in, 0, n_tiles, None, "inproj_bwd_dx", deps)
    return grad_x, dgin


ALL_CHIPS = (0, 1, 2, 3)


class _GradReduce:
    def __init__(self, proj_done):
        self.pending = {}
        self.proj_done = proj_done
        self.land_in = None

    def _start(self, key, parts, name):
        bufs, plans, shared = [], [], None
        for part_bufs, plan, n_copies, part_shared in parts:
            if part_shared is not None:
                shared = len(bufs) + part_shared
            plans.append((plan, len(part_bufs), n_copies))
            bufs += part_bufs
        plan = _join_plans(plans)
        send_sems, recv_sems, bufs, token = _copies_start(bufs, plan, sum(p[2] for p in plans), name + "_start")
        if shared is not None:
            self.land_in = bufs[shared]
        self.pending[key] = (send_sems, recv_sems, bufs, plan, name + "_wait", shared)
        return (token,)

    def _finish(self, key, after):
        send_sems, recv_sems, bufs, plan, name, shared = self.pending.pop(key)
        if shared is not None:
            bufs[shared] = self.land_in
        bufs = _copies_wait(send_sems, recv_sems, bufs, after, plan, name)
        if shared is not None:
            self.land_in = bufs[shared]
        return bufs

    @staticmethod
    def _swap(pieces):
        bufs = []
        for g in pieces:
            bufs += [g, lax.empty((g.shape[0],) + g.shape[2:], F32)]
        n_slabs = [g.shape[0] for g in pieces]
        return bufs, _swap_plan(n_slabs), sum(n_slabs), None

    def _scatter(self, sums, dest_sets):
        bufs = []
        for cs in sums:
            bufs += [cs, lax.empty((3,) + cs.shape[1:], cs.dtype)]
        if self.land_in is not None:
            bufs[-1] = self.land_in
        return bufs, _scatter_plan(dest_sets), 3 * len(sums), len(bufs) - 1

    @staticmethod
    def _gather8(block):
        x, y, c = _coords()
        land = lax.dynamic_update_slice(lax.empty((8,) + block.shape, F32), block[None], (4 * x + 2 * y + c, 0, 0))
        return [land], _allgather_plan(), 7, None

    def m_ready(self, gw_proj, g3):
        rows = gw_proj.shape[2] * gw_proj.shape[3]
        return self._start("m", [self._swap([gw_proj.reshape(N_CHIPS, 2, rows, D_MODEL), g3])], "swap_m")

    def ret_done(self, after):
        proj, land_p, g3, land_3 = self._finish("m", after)
        sums_m = [_add_my_half(proj, land_p, "chip_sum_proj"), _add_my_half(g3, land_3, "chip_sum_m")]
        return self._start("sm", [self._scatter(sums_m, [ALL_CHIPS, (3,)])], "scatter_m")

    def r_ready(self, g12):
        return self._start("r", [self._swap([g12])], "swap_r")

    def lru_done(self, after, packed):
        g12, land_12 = self._finish("r", after)
        sums_r = [_add_my_half(g12, land_12, "chip_sum_r")]
        return (self._start("sr", [self._scatter(sums_r, [(1, 2)])], "scatter_r")
                + self._start("small", [self._gather8(packed)], "gather_small"))

    def a_ready(self, g0):
        (token,) = self._start("a", [self._swap([g0])], "swap_a")
        csp, gotp, self.cs3, _ = self._finish("sm", token)
        half_proj = _sum_slabs(csp, gotp, "sum_w_proj")
        g0, land_0 = self._finish("a", half_proj)
        deps = self._start("sa", [self._scatter([_add_my_half(g0, land_0, "chip_sum_a")], [(0,)])], "scatter_a")
        self.proj_done(_join_halves([half_proj], [4], "join_halves_proj", deps)[0])
        return deps

    def finish(self, dgin, w_in_done):
        (token,) = self._start("n", [self._gather8(dgin)], "gather_norm_in")
        (small,) = self._finish("small", token)
        cs12, _ = self._finish("sr", token)
        cs0, _ = self._finish("sa", token)
        half_in = _sum_parts([self.cs3, cs12, cs0], self.land_in, [(3,), (1, 2), (0,)], "sum_w_in")
        deps = self._start("j", [([half_in], _join_plan(half_in.shape[1], 8), 8, None)], "join_w_in")
        first = w_in_done(self.pending["j"][2][0], True, None, deps)
        (g_in,) = self._finish("j", first[1])
        done = w_in_done(g_in, False, first, ())
        (norm_in,) = self._finish("n", done[1])
        return _sum_gathered(small, "sum_small_grads"), _sum_gathered(norm_in, "sum_norm_in_grad")


_SMALL = ("gate_x_w", "gate_a_w", "conv_w", "conv_b", "gate_x_b", "gate_a_b", "lru_lambda", "gn_gain", "norm_final")
_SMALL_SHAPES = dict(gate_x_w=(LRU_BLOCKS, LRU_BW, LRU_BW), gate_a_w=(LRU_BLOCKS, LRU_BW, LRU_BW),
                     norm_in=(1, D_MODEL), conv_w=(CONV, D_MODEL), conv_b=(1, D_MODEL), gate_x_b=(1, D_MODEL),
                     gate_a_b=(1, D_MODEL), lru_lambda=(1, D_MODEL), gn_gain=(HEADS, DK), norm_final=(1, D_MODEL))


def _pack_small(small):
    return jnp.concatenate([small[k].reshape(-1, 128) for k in _SMALL], axis=0)


def _unpack_small(packed):
    out, r = {}, 0
    for k in _SMALL:
        shape = _SMALL_SHAPES[k]
        rows = 1
        for s in shape:
            rows *= s
        rows //= 128
        out[k] = packed[r:r + rows].reshape(shape)
        r += rows
    return out


def kernel(x, norm_in, w_in, conv_w, conv_b, gate_x_w, gate_x_b, gate_a_w, gate_a_b, lru_lambda, gn_gain, w_proj_a, w_proj_b, w_out, norm_final, loss_target, m_norm_in, m_w_in, m_conv_w, m_conv_b, m_gate_x_w, m_gate_x_b, m_gate_a_w, m_gate_a_b, m_lru_lambda, m_gn_gain, m_w_proj_a, m_w_proj_b, m_w_out, m_norm_final, v_norm_in, v_w_in, v_conv_w, v_conv_b, v_gate_x_w, v_gate_x_b, v_gate_a_w, v_gate_a_b, v_lru_lambda, v_gn_gain, v_w_proj_a, v_w_proj_b, v_w_out, v_norm_final):
    B, S, _ = x.shape
    T = B * S
    xi, yi, ci = _coords()
    chip = 2 * xi + yi

    cshard = D_MODEL // N_CHIPS
    mine = _cast_into_slot([w_in[0].reshape(2, D_MODEL // 2, 2 * D_MODEL)]
                           + [w[0].reshape(2, cshard // 2, D_MODEL) for w in (w_proj_a, w_proj_b, w_out)],
                           "cast_weights")
    plan = _gather_plan(3)
    pending_proj = []
    gshard = DK // N_CHIPS
    tiny = jnp.concatenate([conv_w[0], jnp.zeros((4, cshard), F32), jnp.pad(gn_gain[0], ((0, 4), (0, cshard - gshard)))],
                           axis=0).reshape(1, 2, SUBLANES, cshard)
    tiny_buf = lax.dynamic_update_slice(lax.empty((N_CHIPS, 2, SUBLANES, cshard), F32), tiny, (chip, 0, 0, 0))
    near_plan, pass_plan, far_plan = (_chip_gather_plan(stage, 2) for stage in ("near", "pass", "far"))
    (n_near, _), (n_pass, passed_on), (n_far, _) = (_chip_gather_copies(stage, 2) for stage in ("near", "pass", "far"))
    halves = set(range(n_pass)) - passed_on
    near_s, near_r, bufs, near_token = _copies_start([mine[0], tiny_buf], near_plan, n_near, "gather_near_start")

    def in_proj(x2d, g_in):
        as_w = lambda b: b[0].reshape(N_CHIPS, D_MODEL, 2 * D_MODEL)
        slot_x, slot_y, slot_d = 2 * (1 - xi) + yi, 2 * xi + (1 - yi), 2 * (1 - xi) + (1 - yi)
        ids = lambda *chips: jnp.stack(chips).astype(jnp.int32)
        proj, hb, ht = _inproj_first(x2d, g_in, as_w(bufs), ids(chip), "inproj_own", (near_token,))
        got = _copies_wait(near_s, near_r, bufs, proj, near_plan, "gather_near_wait")
        pass_s, pass_r, got, pass_token = _copies_start(got, pass_plan, n_pass, "gather_pass_start")
        got = _copies_wait(pass_s, pass_r, got, pass_token, pass_plan, "gather_pass_wait_halves", only=halves)
        proj = _inproj_more(hb, as_w(got), ids(slot_x, slot_y), proj, "inproj_near")
        got = _copies_wait(pass_s, pass_r, got, proj, pass_plan, "gather_pass_wait_far", only=passed_on)
        pending_proj.append(_copies_start(mine[1:], plan, 9, "gather_proj_start", (got[0],)))
        far_s, far_r, got, far_token = _copies_start(got, far_plan, n_far, "gather_far_start")
        got = _copies_wait(far_s, far_r, got, far_token, far_plan, "gather_far_wait")
        proj = _inproj_more(hb, as_w(got), ids(slot_d), proj, "inproj_far")
        tiny_all = got[1].reshape(N_CHIPS, 2 * SUBLANES, cshard)
        conv_w_full = jnp.transpose(tiny_all[:, 0:CONV, :], (1, 0, 2)).reshape(CONV, D_MODEL)
        gain_full = jnp.transpose(tiny_all[:, 8:8 + HEADS, :gshard], (1, 0, 2)).reshape(HEADS, DK)
        return proj, ht, as_w(got), conv_w_full, gain_full

    def proj_weights(after):
        s_sems, r_sems, pbufs, _ = pending_proj[0]
        got = _copies_wait(s_sems, r_sems, pbufs, after, plan, "gather_proj_wait")
        return [b.reshape(D_MODEL, D_MODEL) for b in got]

    weights = dict(norm_in=norm_in, w_in=w_in, conv_w=conv_w, conv_b=conv_b, gate_x_w=gate_x_w, gate_x_b=gate_x_b,
                   gate_a_w=gate_a_w, gate_a_b=gate_a_b, lru_lambda=lru_lambda, gn_gain=gn_gain, w_proj_a=w_proj_a,
                   w_proj_b=w_proj_b, w_out=w_out, norm_final=norm_final)
    ms = dict(norm_in=m_norm_in, w_in=m_w_in, conv_w=m_conv_w, conv_b=m_conv_b, gate_x_w=m_gate_x_w,
              gate_x_b=m_gate_x_b, gate_a_w=m_gate_a_w, gate_a_b=m_gate_a_b, lru_lambda=m_lru_lambda, gn_gain=m_gn_gain,
              w_proj_a=m_w_proj_a, w_proj_b=m_w_proj_b, w_out=m_w_out, norm_final=m_norm_final)
    vs = dict(norm_in=v_norm_in, w_in=v_w_in, conv_w=v_conv_w, conv_b=v_conv_b, gate_x_w=v_gate_x_w,
              gate_x_b=v_gate_x_b, gate_a_w=v_gate_a_w, gate_a_b=v_gate_a_b, lru_lambda=v_lru_lambda, gn_gain=v_gn_gain,
              w_proj_a=v_w_proj_a, w_proj_b=v_w_proj_b, w_out=v_w_out, norm_final=v_norm_final)
    names = list(weights)
    grads, delta, new_m, new_v = {}, {}, {}, {}

    def update_big(keys, g, half, prev, name, deps=()):
        two = lambda a: a.reshape(a.shape[1], a.shape[2])
        res = _adamw_halves([two(weights[k]) for k in keys], g, [two(ms[k]) for k in keys], [two(vs[k]) for k in keys],
                            half, prev, name, deps)
        for k, (gk, d, mn, vn) in zip(keys, res):
            shp = weights[k].shape
            grads[k], delta[k], new_m[k], new_v[k] = gk.reshape(shp), d.reshape(shp), mn.reshape(shp), vn.reshape(shp)
        return res

    def proj_done(g_proj):
        g4 = g_proj.reshape(2, 3, D_MODEL // (2 * N_CHIPS), D_MODEL)
        return update_big(("w_proj_a", "w_proj_b", "w_out"), g4, None, None, "adamw_proj")[-1][1]

    def w_in_done(g_in, own, prev, deps):
        g4 = g_in.reshape(2, 1, D_MODEL // 2, 2 * D_MODEL)
        return update_big(("w_in",), g4, ci if own else 1 - ci, None if prev is None else [prev],
                          "adamw_w_in_own" if own else "adamw_w_in_other", deps)[0]

    reduce = _GradReduce(proj_done)
    grad_x, dgin = _local_grads(
        x.reshape(T, D_MODEL), loss_target.reshape(T, D_MODEL), B, S, norm_in, in_proj, conv_b,
        gate_x_w[0], gate_x_b, gate_a_w[0], gate_a_b, lru_lambda, proj_weights,
        norm_final.reshape(1, D_MODEL), reduce)

    small_sum, g_norm_in = reduce.finish(dgin.reshape(SUBLANES, LANES), w_in_done)
    loss = small_sum[small_sum.shape[0] - SUBLANES, 0]

    gsm = _unpack_small(small_sum)
    gsm["norm_in"] = g_norm_in
    gsm["conv_w"] = lax.dynamic_slice_in_dim(gsm["conv_w"], chip * cshard, cshard, axis=1)
    gsm["gn_gain"] = lax.dynamic_slice_in_dim(gsm["gn_gain"], chip * gshard, gshard, axis=1)
    smalls = [k for k in names if k not in delta]

    def view(a):
        return a.reshape(1, -1) if a.ndim == 1 else (a.reshape(a.shape[1:]) if a.ndim > 2 else a)

    ds, mns, vns = _adamw_small([view(weights[k]) for k in smalls], [gsm[k].reshape(view(weights[k]).shape) for k in smalls],
                                [view(ms[k]) for k in smalls], [view(vs[k]) for k in smalls], "adamw_small")
    for k, d, mn, vn in zip(smalls, ds, mns, vns):
        shp = weights[k].shape
        grads[k], delta[k], new_m[k], new_v[k] = gsm[k].reshape(shp), d.reshape(shp), mn.reshape(shp), vn.reshape(shp)

    return (loss, grad_x.reshape(B, S, D_MODEL), *[grads[k] for k in names], *[delta[k] for k in names],
            *[new_m[k] for k in names], *[new_v[k] for k in names])
```

```python
import jax
import jax.numpy as jnp
from jax import lax
from jax.experimental import pallas as pl
from jax.experimental.pallas import tpu as pltpu

F32 = jnp.float32
_MXU = jnp.bfloat16

D_MODEL = 1024
N_GROUPS = 8
HEADS = 4
DK = 256
CHUNK = 128
CONV = 4
LRU_BLOCKS = 16
LRU_BW = 64
LRU_C = 8.0
ROPE_THETA = 10000.0
EPS = 1e-6
CW = 256
N_CT = D_MODEL // CW
N_CHIPS = 4
MESH = pl.DeviceIdType.MESH

ADAM_LR = 0.001
ADAM_B1 = 0.9
ADAM_B2 = 0.999
ADAM_EPS = 1e-08
ADAM_WD = 0.01
ADAM_STEP = 10

VMEM_LIMIT = 56 * 1024 * 1024


def _c(v):
    return v.astype(_MXU)


def _dot(a, b):
    return lax.dot_general(a, b, (((1,), (0,)), ((), ())), preferred_element_type=F32)


def _dot_nt(a, b):
    return lax.dot_general(a, b, (((1,), (1,)), ((), ())), preferred_element_type=F32)


def _dot_tn(a, b):
    return lax.dot_general(a, b, (((0,), (0,)), ((), ())), preferred_element_type=F32)


def _sigmoid(z):
    return 0.5 * jnp.tanh(0.5 * z) + 0.5


ANY_SPEC = pl.BlockSpec(memory_space=pl.ANY)


def _after(body, n_in, deps):
    n_deps = len(deps)

    def wrapped(*refs):
        return body(*refs[:n_in], *refs[n_in + n_deps:])

    return wrapped


def _params(sem=None):
    if sem is None:
        return pltpu.CompilerParams(vmem_limit_bytes=VMEM_LIMIT)
    return pltpu.CompilerParams(vmem_limit_bytes=VMEM_LIMIT, dimension_semantics=sem)


def _inproj_first(x2d, g_in, w_all, chips, name, deps=()):
    T = x2d.shape[0]
    tm = min(1024, T)
    n_i = T // tm

    def body(s_ref, *refs):
        x_ref, g_ref, w_ref = refs[:3]
        proj_ref, hb_ref, ht_ref, h_all = refs[-4:]
        i = pl.program_id(1)
        rows = pl.ds(pl.multiple_of(i * tm, tm), tm)

        @pl.when(pl.program_id(0) == 0)
        def _():
            x = x_ref[...]
            r = lax.rsqrt(jnp.mean(x * x, axis=-1, keepdims=True) + EPS)
            h = x * r * g_ref[...]
            hb = h.astype(h_all.dtype)
            h_all[rows, :] = hb
            hb_ref[...] = hb
            ht_ref[...] = h.T.astype(ht_ref.dtype)

        proj_ref[...] = _dot(h_all[rows, :], w_ref[0])

    first = lambda j, i: jnp.where(j == 0, i, n_i - 1)
    return pl.pallas_call(
        body,
        name=name,
        grid_spec=pltpu.PrefetchScalarGridSpec(
            num_scalar_prefetch=1,
            grid=(2 * chips.shape[0], n_i),
            in_specs=[
                pl.BlockSpec((tm, D_MODEL), lambda j, i, s: (first(j, i), 0)),
                pl.BlockSpec((1, D_MODEL), lambda j, i, s: (0, 0)),
                pl.BlockSpec((1, D_MODEL, D_MODEL), lambda j, i, s: (s[j // 2], 0, j % 2)),
            ] + [ANY_SPEC] * len(deps),
            out_specs=[
                pl.BlockSpec((tm, D_MODEL), lambda j, i, s: (i, 2 * s[j // 2] + j % 2)),
                pl.BlockSpec((tm, D_MODEL), lambda j, i, s: (first(j, i), 0)),
                pl.BlockSpec((D_MODEL, tm), lambda j, i, s: (0, first(j, i))),
            ],
            scratch_shapes=[pltpu.VMEM((T, D_MODEL), _MXU)],
        ),
        out_shape=[
            jax.ShapeDtypeStruct((T, N_GROUPS * D_MODEL), F32),
            jax.ShapeDtypeStruct((T, D_MODEL), _MXU),
            jax.ShapeDtypeStruct((D_MODEL, T), _MXU),
        ],
        compiler_params=_params(("arbitrary", "arbitrary")),
    )(chips, x2d, g_in, w_all, *deps)


def _inproj_more(hb, w_all, chips, proj, name):
    T = hb.shape[0]
    tm = min(2048, T)

    def body(s_ref, hb_hbm, w_ref, prev_ref, proj_ref, h_all, sem):
        @pl.when((pl.program_id(0) == 0) & (pl.program_id(1) == 0))
        def _():
            cp = pltpu.make_async_copy(hb_hbm, h_all, sem)
            cp.start()
            cp.wait()

        rows = pl.ds(pl.multiple_of(pl.program_id(1) * tm, tm), tm)
        proj_ref[...] = _dot(h_all[rows, :], w_ref[0])

    return pl.pallas_call(
        body,
        name=name,
        grid_spec=pltpu.PrefetchScalarGridSpec(
            num_scalar_prefetch=1,
            grid=(2 * chips.shape[0], T // tm),
            in_specs=[
                ANY_SPEC,
                pl.BlockSpec((1, D_MODEL, D_MODEL), lambda j, i, s: (s[j // 2], 0, j % 2)),
                ANY_SPEC,
            ],
            out_specs=pl.BlockSpec((tm, D_MODEL), lambda j, i, s: (i, 2 * s[j // 2] + j % 2)),
            scratch_shapes=[pltpu.VMEM((T, D_MODEL), hb.dtype), pltpu.SemaphoreType.DMA],
        ),
        out_shape=jax.ShapeDtypeStruct(proj.shape, F32),
        input_output_aliases={3: 0},
        compiler_params=_params(("arbitrary", "arbitrary")),
    )(chips, hb, w_all, proj)


def _scan_fwd(a, u):
    n = a.shape[0]
    row = lax.broadcasted_iota(jnp.int32, a.shape, 0)
    s = 1
    while s < n:
        m = row >= s
        u = u + a * jnp.where(m, pltpu.roll(u, s, 0), 0.0)
        a = a * jnp.where(m, pltpu.roll(a, s, 0), 1.0)
        s *= 2
    return a, u


def _scan_bwd(b, g):
    n = b.shape[0]
    row = lax.broadcasted_iota(jnp.int32, b.shape, 0)
    s = 1
    while s < n:
        m = row < n - s
        g = g + b * jnp.where(m, pltpu.roll(g, n - s, 0), 0.0)
        b = b * jnp.where(m, pltpu.roll(b, n - s, 0), 1.0)
        s *= 2
    return b, g


LANES = 128
SUBLANES = 8


def _scan_scratch(tc):
    by_lanes = pltpu.VMEM((CW // LANES, tc, LANES), F32)
    return [by_lanes, by_lanes, pltpu.VMEM((tc // SUBLANES, CW), F32), pltpu.VMEM((tc, CW), F32)]


def _scan_tile(a, u, edge, la_ref, lh_ref, c_ref, dst_ref, reverse):
    n, w = a.shape
    groups = n // SUBLANES
    a3 = a.reshape(groups, SUBLANES, w)
    u3 = u.reshape(groups, SUBLANES, w)
    row = lax.broadcasted_iota(jnp.int32, a3.shape, 1)
    for s in (1, 2, 4):
        m = (row < SUBLANES - s) if reverse else (row >= s)
        shift = SUBLANES - s if reverse else s
        u3 = u3 + a3 * jnp.where(m, pltpu.roll(u3, shift, 1), 0.0)
        a3 = a3 * jnp.where(m, pltpu.roll(a3, shift, 1), 1.0)
    al = a3.reshape(n, w)
    hl = u3.reshape(n, w)
    blocks = w // LANES
    for q in range(blocks):
        la_ref[q] = al[:, q * LANES:(q + 1) * LANES]
        lh_ref[q] = hl[:, q * LANES:(q + 1) * LANES]
    ends = pl.ds(0 if reverse else SUBLANES - 1, groups, stride=SUBLANES)
    end_a = jnp.concatenate([la_ref.at[q][ends, :] for q in range(blocks)], axis=-1)
    end_h = jnp.concatenate([lh_ref.at[q][ends, :] for q in range(blocks)], axis=-1)
    prod, part = (_scan_bwd if reverse else _scan_fwd)(end_a, end_h)
    total = part + prod * edge
    g_row = lax.broadcasted_iota(jnp.int32, total.shape, 0)
    if reverse:
        c_ref[...] = jnp.where(g_row == groups - 1, edge, pltpu.roll(total, groups - 1, 0))
    else:
        c_ref[...] = jnp.where(g_row == 0, edge, pltpu.roll(total, 1, 0))
    for g in range(groups):
        rows = slice(g * SUBLANES, (g + 1) * SUBLANES)
        for q in range(blocks):
            cols = slice(q * LANES, (q + 1) * LANES)
            dst_ref[rows, cols] = lh_ref[q, rows, :] + la_ref[q, rows, :] * c_ref[g:g + 1, cols]


def _softplus_neg(lam):
    z = -lam
    return jnp.maximum(z, 0.0) + jnp.log1p(jnp.exp(-jnp.abs(z)))


def _lru_gates(xc, wx_ref, wa_ref, bx_ref, ba_ref, lam_ref):
    xcb = _c(xc)
    i_t = _sigmoid(_dot(xcb, wx_ref[0]) + bx_ref[...])
    r_t = _sigmoid(_dot(xcb, wa_ref[0]) + ba_ref[...])
    sp = _softplus_neg(lam_ref[...])
    log_a = (-LRU_C) * r_t * sp
    a = jnp.exp(log_a)
    mult = jnp.sqrt(1.0 - a * a)
    return xcb, i_t, r_t, sp, a, mult


def _conv_from_ext(ext_ref, xa, cw_ref, cb_ref, tc):
    return (cb_ref[...] + cw_ref[3:4, :] * xa + cw_ref[2:3, :] * ext_ref[7:7 + tc, :]
            + cw_ref[1:2, :] * ext_ref[6:6 + tc, :] + cw_ref[0:1, :] * ext_ref[5:5 + tc, :])


def _lru_fwd(proj, conv_w, conv_b, wx_bd, wa_bd, bx, ba, lam, B, S):
    T = B * S
    tc = min(256, S)
    nt = S // tc
    h8 = tc // 8

    def body(xa_ref, halo_ref, ga_ref, cw_ref, cb_ref, wx_ref, wa_ref, bx_ref, ba_ref, lam_ref,
             h_ref, ya_ref, ext_ref, carry_ref, la_ref, lh_ref, c_ref):
        t = pl.program_id(2)

        @pl.when(t == 0)
        def _():
            carry_ref[...] = jnp.zeros_like(carry_ref)

        xa = xa_ref[...]
        ext_ref[0:8, :] = jnp.where(t == 0, 0.0, halo_ref[...])
        ext_ref[8:8 + tc, :] = xa
        xc = _conv_from_ext(ext_ref, xa, cw_ref, cb_ref, tc)
        _, i_t, _, _, a, mult = _lru_gates(xc, wx_ref, wa_ref, bx_ref, ba_ref, lam_ref)
        u = mult * (i_t * xc)
        _scan_tile(a, u, carry_ref[7:8, :], la_ref, lh_ref, c_ref, h_ref, False)
        h = h_ref[...]
        carry_ref[...] = h[tc - 8:tc, :]
        ga = ga_ref[...]
        ya_ref[...] = (ga * _sigmoid(ga) * h).astype(ya_ref.dtype)

    row = lambda b, t: b * nt + t
    vec = pl.BlockSpec((1, CW), lambda b, c, t: (0, c))
    mat = pl.BlockSpec((1, CW, CW), lambda b, c, t: (c, 0, 0))
    return pl.pallas_call(
        body,
        name="lru_fwd",
        grid=(B, N_CT, nt),
        in_specs=[
            pl.BlockSpec((tc, CW), lambda b, c, t: (row(b, t), c)),
            pl.BlockSpec((8, CW), lambda b, c, t: (jnp.maximum(row(b, t) * h8 - 1, 0), c)),
            pl.BlockSpec((tc, CW), lambda b, c, t: (row(b, t), N_CT + c)),
            pl.BlockSpec((CONV, CW), lambda b, c, t: (0, c)),
            vec, mat, mat, vec, vec, vec,
        ],
        out_specs=[
            pl.BlockSpec((tc, CW), lambda b, c, t: (row(b, t), c)),
            pl.BlockSpec((tc, CW), lambda b, c, t: (row(b, t), c)),
        ],
        out_shape=[
            jax.ShapeDtypeStruct((T, D_MODEL), F32),
            jax.ShapeDtypeStruct((T, D_MODEL), _MXU),
        ],
        scratch_shapes=[pltpu.VMEM((tc + 8, CW), F32), pltpu.VMEM((8, CW), F32)] + _scan_scratch(tc)[:3],
        compiler_params=_params(("parallel", "parallel", "arbitrary")),
    )(proj, proj, proj, conv_w, conv_b, wx_bd, wa_bd, bx, ba, lam)


def _lru_bwd(dya, proj, hlru, conv_w, conv_b, wx_bd, wa_bd, bx, ba, lam, B, S, deps=()):
    T = B * S
    tc = min(256, S)
    nt = S // tc
    h8 = tc // 8

    def body(dya_ref, xa_ref, xhalo_ref, ga_ref, h_ref, hhalo_ref, cw_ref, cb_ref, wx_ref, wa_ref, bx_ref, ba_ref,
             lam_ref, dxa_ref, dga_ref, dcw_ref, dcb_ref, dwx_ref, dwa_ref, dbx_ref, dba_ref, dlam_ref,
             ext_ref, ext2_ref, carry_ref, dhalo_ref, la_ref, lh_ref, c_ref, dh_ref):
        b = pl.program_id(1)
        t = pl.program_id(2)
        tt = nt - 1 - t

        @pl.when(t == 0)
        def _():
            carry_ref[...] = jnp.zeros_like(carry_ref)
            dhalo_ref[...] = jnp.zeros_like(dhalo_ref)

        @pl.when((t == 0) & (b == 0))
        def _():
            for r in (dcw_ref, dcb_ref, dwx_ref, dwa_ref, dbx_ref, dba_ref, dlam_ref):
                r[...] = jnp.zeros_like(r)

        xa = xa_ref[...]
        ext_ref[0:8, :] = jnp.where(tt == 0, 0.0, xhalo_ref[...])
        ext_ref[8:8 + tc, :] = xa
        xc = _conv_from_ext(ext_ref, xa, cw_ref, cb_ref, tc)
        xcb, i_t, r_t, sp, a, mult = _lru_gates(xc, wx_ref, wa_ref, bx_ref, ba_ref, lam_ref)

        h = h_ref[...]
        ga = ga_ref[...]
        dya_t = dya_ref[...]
        sg = _sigmoid(ga)
        dga_ref[...] = (dya_t * h * (sg * (1.0 + ga * (1.0 - sg)))).astype(dga_ref.dtype)
        dlru = dya_t * (ga * sg)

        row = lax.broadcasted_iota(jnp.int32, a.shape, 0)
        coef = jnp.where(row == tc - 1, 1.0, pltpu.roll(a, tc - 1, 0))
        _scan_tile(coef, dlru, carry_ref[0:1, :], la_ref, lh_ref, c_ref, dh_ref, True)
        dh = dh_ref[...]
        ext2_ref[0:tc, :] = a * dh
        carry_ref[...] = ext2_ref[0:8, :]

        ext2_ref[0:8, :] = jnp.where(tt == 0, 0.0, hhalo_ref[...])
        ext2_ref[8:8 + tc, :] = h
        hprev = ext2_ref[7:7 + tc, :]

        da = dh * hprev
        ix = i_t * xc
        dmult = dh * ix
        di = dh * mult * xc
        dxc = dh * mult * i_t
        dlog_a = da * a - dmult * (a * a) / mult
        dr = dlog_a * ((-LRU_C) * sp)
        dlam_ref[...] += jnp.sum(dlog_a * r_t, axis=0, keepdims=True) * (LRU_C * _sigmoid(-lam_ref[...]))
        dza = dr * r_t * (1.0 - r_t)
        dzx = di * i_t * (1.0 - i_t)
        dzab = _c(dza)
        dzxb = _c(dzx)
        dxc = dxc + _dot_nt(dzxb, wx_ref[0]) + _dot_nt(dzab, wa_ref[0])
        dwx_ref[0] += _dot_tn(xcb, dzxb)
        dwa_ref[0] += _dot_tn(xcb, dzab)
        dbx_ref[...] += jnp.sum(dzx, axis=0, keepdims=True)
        dba_ref[...] += jnp.sum(dza, axis=0, keepdims=True)

        dcb_ref[...] += jnp.sum(dxc, axis=0, keepdims=True)
        dcw_ref[3:4, :] += jnp.sum(dxc * xa, axis=0, keepdims=True)
        dcw_ref[2:3, :] += jnp.sum(dxc * ext_ref[7:7 + tc, :], axis=0, keepdims=True)
        dcw_ref[1:2, :] += jnp.sum(dxc * ext_ref[6:6 + tc, :], axis=0, keepdims=True)
        dcw_ref[0:1, :] += jnp.sum(dxc * ext_ref[5:5 + tc, :], axis=0, keepdims=True)
        ext2_ref[0:tc, :] = dxc
        ext2_ref[tc:tc + 8, :] = dhalo_ref[...]
        dxa = (cw_ref[3:4, :] * dxc + cw_ref[2:3, :] * ext2_ref[1:1 + tc, :]
               + cw_ref[1:2, :] * ext2_ref[2:2 + tc, :] + cw_ref[0:1, :] * ext2_ref[3:3 + tc, :])
        dxa_ref[...] = dxa.astype(dxa_ref.dtype)
        dhalo_ref[...] = ext2_ref[0:8, :]

    row_of = lambda b, t: b * nt + (nt - 1 - t)
    tile = lambda off: pl.BlockSpec((tc, CW), lambda c, b, t: (row_of(b, t), off + c))
    halo = pl.BlockSpec((8, CW), lambda c, b, t: (jnp.maximum(row_of(b, t) * h8 - 1, 0), c))
    vec = pl.BlockSpec((1, CW), lambda c, b, t: (0, c))
    mat = pl.BlockSpec((1, CW, CW), lambda c, b, t: (c, 0, 0))
    cwspec = pl.BlockSpec((CONV, CW), lambda c, b, t: (0, c))
    return pl.pallas_call(
        _after(body, 13, deps),
        name="lru_bwd",
        grid=(N_CT, B, nt),
        in_specs=[tile(0), tile(0), halo, tile(N_CT), tile(0), halo, cwspec, vec, mat, mat, vec, vec, vec]
        + [ANY_SPEC] * len(deps),
        out_specs=[tile(0), tile(0), cwspec, vec, mat, mat, vec, vec, vec],
        out_shape=[
            jax.ShapeDtypeStruct((T, D_MODEL), _MXU),
            jax.ShapeDtypeStruct((T, D_MODEL), _MXU),
            jax.ShapeDtypeStruct((CONV, D_MODEL), F32),
            jax.ShapeDtypeStruct((1, D_MODEL), F32),
            jax.ShapeDtypeStruct((N_CT, CW, CW), F32),
            jax.ShapeDtypeStruct((N_CT, CW, CW), F32),
            jax.ShapeDtypeStruct((1, D_MODEL), F32),
            jax.ShapeDtypeStruct((1, D_MODEL), F32),
            jax.ShapeDtypeStruct((1, D_MODEL), F32),
        ],
        scratch_shapes=[pltpu.VMEM((tc + 8, CW), F32), pltpu.VMEM((tc + 8, CW), F32),
                        pltpu.VMEM((8, CW), F32), pltpu.VMEM((8, CW), F32)] + _scan_scratch(tc),
        compiler_params=_params(("parallel", "arbitrary", "arbitrary")),
    )(dya, proj, proj, proj, hlru, hlru, conv_w, conv_b, wx_bd, wa_bd, bx, ba, lam, *deps)


def _retention_tables(S):
    half = DK // 2
    freqs = ROPE_THETA ** (-jnp.arange(half, dtype=F32) / half)
    ang = jnp.arange(S, dtype=F32)[:, None] * freqs[None, :]
    log_g = jnp.log1p(-(2.0 ** (-5.0 - jnp.arange(HEADS, dtype=F32))))
    idx = jnp.arange(CHUNK, dtype=F32)
    diff = idx[:, None] - idx[None, :]
    inner = jnp.where(diff >= 0, jnp.exp(jnp.maximum(diff, 0.0)[None] * log_g[:, None, None]), 0.0)
    cross = jnp.exp((idx[None, :] + 1.0) * log_g[:, None])[:, :, None]
    state = jnp.exp((CHUNK - 1.0 - idx[None, :]) * log_g[:, None])[:, :, None]
    gam = jnp.broadcast_to(jnp.exp(CHUNK * log_g)[:, None, None], (HEADS, 1, DK))
    return jnp.cos(ang), jnp.sin(ang), inner, cross, state, gam


def _rot(x, cos, sin):
    half = DK // 2
    x1, x2 = x[:, :half], x[:, half:]
    return jnp.concatenate([x1 * cos - x2 * sin, x1 * sin + x2 * cos], axis=-1)


def _rot_t(y, cos, sin):
    half = DK // 2
    y1, y2 = y[:, :half], y[:, half:]
    return jnp.concatenate([y1 * cos + y2 * sin, y2 * cos - y1 * sin], axis=-1)


def _groupnorm(o):
    mu = jnp.mean(o, axis=-1, keepdims=True)
    oc = o - mu
    rs = lax.rsqrt(jnp.mean(oc * oc, axis=-1, keepdims=True) + EPS)
    return oc * rs, rs


def _ret_specs(B, chunk_of):
    qkv = lambda g: pl.BlockSpec((B, CHUNK, D_MODEL), lambda c: (0, chunk_of(c), g))
    act = pl.BlockSpec((B, CHUNK, D_MODEL), lambda c: (0, chunk_of(c), 0))
    rope = pl.BlockSpec((CHUNK, DK // 2), lambda c: (chunk_of(c), 0))
    dmat = pl.BlockSpec((HEADS, CHUNK, CHUNK), lambda c: (0, 0, 0))
    dvec = pl.BlockSpec((HEADS, CHUNK, 1), lambda c: (0, 0, 0))
    hrow = pl.BlockSpec((HEADS, 1, DK), lambda c: (0, 0, 0))
    rst = pl.BlockSpec((1, B, HEADS, DK, DK), lambda c: (chunk_of(c), 0, 0, 0, 0))
    return qkv, act, rope, dmat, dvec, hrow, rst


def _ret_fwd(proj, tables, gain3, B, S):
    T = B * S
    nc = S // CHUNK
    cos, sin, dmat_t, cd_t, sd_t, gam_t = tables

    def body(q_ref, k_ref, v_ref, gb_ref, cos_ref, sin_ref, dm_ref, cd_ref, sd_ref, gam_ref, gain_ref,
             o_ref, yb_ref, rs_ref, state_ref):
        @pl.when(pl.program_id(0) == 0)
        def _():
            state_ref[...] = jnp.zeros_like(state_ref)

        cos_t, sin_t = cos_ref[...], sin_ref[...]
        for b, h in [(b, h) for b in range(B) for h in range(HEADS)]:
            cols = slice(h * DK, (h + 1) * DK)
            qb = _c(_rot(q_ref[b, :, cols], cos_t, sin_t))
            kb = _c(_rot(k_ref[b, :, cols], cos_t, sin_t) * (DK ** -0.5))
            v = v_ref[b, :, cols]
            state = state_ref[b, h]
            sb = _c(state)
            rs_ref[0, b, h] = sb
            scores = _dot_nt(qb, kb) * dm_ref[h]
            o = _dot(_c(scores), _c(v)) + _dot(qb, sb) * cd_ref[h]
            state_ref[b, h] = gam_ref[h] * state + _dot_tn(kb, _c(v * sd_ref[h]))
            o_ref[b, :, cols] = o
            n, _ = _groupnorm(o)
            gb = gb_ref[b, :, cols]
            yb_ref[b, :, cols] = (gb * _sigmoid(gb) * (n * gain_ref[h])).astype(yb_ref.dtype)

    qkv, act, rope, dmat, dvec, hrow, rst = _ret_specs(B, lambda c: c)
    proj3 = proj.reshape(B, S, proj.shape[1])
    o_pre, yb, states = pl.pallas_call(
        body,
        name="ret_fwd",
        grid=(nc,),
        in_specs=[qkv(2), qkv(3), qkv(4), qkv(5), rope, rope, dmat, dvec, dvec, hrow, hrow],
        out_specs=[act, act, rst],
        out_shape=[
            jax.ShapeDtypeStruct((B, S, D_MODEL), F32),
            jax.ShapeDtypeStruct((B, S, D_MODEL), _MXU),
            jax.ShapeDtypeStruct((nc, B, HEADS, DK, DK), _MXU),
        ],
        scratch_shapes=[pltpu.VMEM((B, HEADS, DK, DK), F32)],
        compiler_params=_params(("arbitrary",)),
    )(proj3, proj3, proj3, proj3, cos, sin, dmat_t, cd_t, sd_t, gam_t, gain3)
    return o_pre.reshape(T, D_MODEL), yb.reshape(T, D_MODEL), states


def _ret_bwd(dyb, o_pre, proj, states, tables, gain3, B, S, deps=()):
    T = B * S
    nc = S // CHUNK
    cos, sin, dmat_t, cd_t, sd_t, gam_t = tables

    def body(dyb_ref, o_ref, q_ref, k_ref, v_ref, gb_ref, rs_ref, cos_ref, sin_ref, dm_ref, cd_ref, sd_ref, gam_ref,
             gain_ref, dr_ref, dgain_ref, dstate_ref):
        @pl.when(pl.program_id(0) == 0)
        def _():
            dstate_ref[...] = jnp.zeros_like(dstate_ref)
            dgain_ref[...] = jnp.zeros_like(dgain_ref)

        cos_t, sin_t = cos_ref[...], sin_ref[...]
        for b, h in [(b, h) for b in range(B) for h in range(HEADS)]:
            cols = slice(h * DK, (h + 1) * DK)
            gain = gain_ref[h]
            n, rs = _groupnorm(o_ref[b, :, cols])
            gb = gb_ref[b, :, cols]
            sg = _sigmoid(gb)
            dy = dyb_ref[b, :, cols]
            part = lambda g: slice(g * D_MODEL + h * DK, g * D_MODEL + (h + 1) * DK)
            dr_ref[b, :, part(3)] = (dy * (n * gain) * (sg * (1.0 + gb * (1.0 - sg)))).astype(dr_ref.dtype)
            dgn = dy * (gb * sg)
            dgain_ref[h] += jnp.sum(dgn * n, axis=0, keepdims=True)
            dn = dgn * gain
            do = rs * (dn - jnp.mean(dn, axis=-1, keepdims=True) - n * jnp.mean(dn * n, axis=-1, keepdims=True))

            qb = _c(_rot(q_ref[b, :, cols], cos_t, sin_t))
            kb = _c(_rot(k_ref[b, :, cols], cos_t, sin_t) * (DK ** -0.5))
            v = v_ref[b, :, cols]
            vb = _c(v)
            vsb = _c(v * sd_ref[h])
            dob = _c(do)
            docb = _c(do * cd_ref[h])
            dmat = dm_ref[h]
            dstate = dstate_ref[b, h]
            dsb = _c(dstate)
            pb = _c(_dot_nt(qb, kb) * dmat)
            dsc = _c(_dot_nt(dob, vb) * dmat)
            dq = _dot(dsc, kb) + _dot_nt(docb, rs_ref[0, b, h])
            dk = _dot_tn(dsc, qb) + _dot_nt(vsb, dsb)
            dv = _dot_tn(pb, dob) + _dot(kb, dsb) * sd_ref[h]
            dstate_ref[b, h] = gam_ref[h] * dstate + _dot_tn(qb, docb)
            dr_ref[b, :, part(0)] = _rot_t(dq, cos_t, sin_t).astype(dr_ref.dtype)
            dr_ref[b, :, part(1)] = (_rot_t(dk, cos_t, sin_t) * (DK ** -0.5)).astype(dr_ref.dtype)
            dr_ref[b, :, part(2)] = dv.astype(dr_ref.dtype)

    qkv, act, rope, dmat, dvec, hrow, rst = _ret_specs(B, lambda c: nc - 1 - c)
    wide = pl.BlockSpec((B, CHUNK, 4 * D_MODEL), lambda c: (0, nc - 1 - c, 0))
    proj3 = proj.reshape(B, S, proj.shape[1])
    dr, dgain = pl.pallas_call(
        _after(body, 14, deps),
        name="ret_bwd",
        grid=(nc,),
        in_specs=[act, act, qkv(2), qkv(3), qkv(4), qkv(5), rst, rope, rope, dmat, dvec, dvec, hrow, hrow]
        + [ANY_SPEC] * len(deps),
        out_specs=[wide, hrow],
        out_shape=[jax.ShapeDtypeStruct((B, S, 4 * D_MODEL), _MXU), jax.ShapeDtypeStruct((HEADS, 1, DK), F32)],
        scratch_shapes=[pltpu.VMEM((B, HEADS, DK, DK), F32)],
        compiler_params=_params(("arbitrary",)),
    )(dyb.reshape(B, S, D_MODEL), o_pre.reshape(B, S, D_MODEL), proj3, proj3, proj3, proj3, states, cos, sin, dmat_t,
      cd_t, sd_t, gam_t, gain3, *deps)
    return dr.reshape(T, 4 * D_MODEL), dgain


def _mid(ya, yb, proj, x2d, tgt2d, wpa, wpb, wout, g_fin):
    T = x2d.shape[0]
    tm = min(256, T)
    n_steps = T // tm
    rows = D_MODEL // (2 * N_CHIPS)

    def body(ya_ref, yb_ref, ma_ref, mb_ref, x_ref, t_ref, gf_ref, wpa_hbm, wpb_hbm, wout_hbm,
             loss_ref, dx2_ref, dya_ref, dyb_ref, dm_ref, dgf_ref, gw_hbm, w_ref, acc_ref, sem):
        i = pl.program_id(0)

        @pl.when(i == 0)
        def _():
            loads = [pltpu.make_async_copy(src, w_ref.at[k], sem.at[k]) for k, src in enumerate((wpa_hbm, wpb_hbm, wout_hbm))]
            for cp in loads:
                cp.start()
            for cp in loads:
                cp.wait()
            acc_ref[...] = jnp.zeros_like(acc_ref)
            loss_ref[...] = jnp.zeros_like(loss_ref)
            dgf_ref[...] = jnp.zeros_like(dgf_ref)

        ya_t, yb_t = ya_ref[...], yb_ref[...]
        out_a = _dot(ya_t, w_ref[0])
        out_b = _dot(yb_t, w_ref[1])
        sa = _sigmoid(ma_ref[...])
        sb = _sigmoid(mb_ref[...])
        mgb = _c(sa * out_a + sb * out_b)
        x2 = x_ref[...] + _dot(mgb, w_ref[2])
        r2 = lax.rsqrt(jnp.mean(x2 * x2, axis=-1, keepdims=True) + EPS)
        nx = x2 * r2
        gf = gf_ref[...]
        err = nx * gf - t_ref[...]
        loss_ref[...] += 0.5 * jnp.sum(jnp.mean(err * err, axis=-1, keepdims=True), axis=0, keepdims=True)
        dy = err * (1.0 / D_MODEL)
        dgf_ref[...] += jnp.sum(dy * nx, axis=0, keepdims=True)
        dyg = dy * gf
        dx2 = r2 * (dyg - nx * jnp.mean(dyg * nx, axis=-1, keepdims=True))
        dx2_ref[...] = dx2
        dx2b = _c(dx2)
        dmg = _dot_nt(dx2b, w_ref[2])
        acc_ref[2] += _dot_tn(mgb, dx2b)
        dm_ref[:, :D_MODEL] = (dmg * out_a * sa * (1.0 - sa)).astype(dm_ref.dtype)
        dm_ref[:, D_MODEL:] = (dmg * out_b * sb * (1.0 - sb)).astype(dm_ref.dtype)
        dab = _c(dmg * sa)
        dbb = _c(dmg * sb)
        dya_ref[...] = _dot_nt(dab, w_ref[0])
        dyb_ref[...] = _dot_nt(dbb, w_ref[1])
        acc_ref[0] += _dot_tn(ya_t, dab)
        acc_ref[1] += _dot_tn(yb_t, dbb)

        @pl.when(i == n_steps - 1)
        def _():
            copies = [pltpu.make_async_copy(acc_ref.at[k, pl.ds((2 * p + hf) * rows, rows), :], gw_hbm.at[p, hf, k],
                                            sem.at[(k * N_CHIPS + p) * 2 + hf])
                      for k in range(3) for p in range(N_CHIPS) for hf in range(2)]
            for cp in copies:
                cp.start()
            for cp in copies:
                cp.wait()

    tile = lambda j: pl.BlockSpec((tm, D_MODEL), lambda i: (i, j))
    one = pl.BlockSpec((1, D_MODEL), lambda i: (0, 0))
    anyspec = pl.BlockSpec(memory_space=pl.ANY)
    return pl.pallas_call(
        body,
        name="mid",
        grid=(n_steps,),
        in_specs=[tile(0), tile(0), tile(6), tile(7), tile(0), tile(0), one, anyspec, anyspec, anyspec],
        out_specs=[pl.BlockSpec((1, 1), lambda i: (0, 0)), tile(0), tile(0), tile(0),
                   pl.BlockSpec((tm, 2 * D_MODEL), lambda i: (i, 0)), one, anyspec],
        out_shape=[
            jax.ShapeDtypeStruct((1, 1), F32),
            jax.ShapeDtypeStruct((T, D_MODEL), F32),
            jax.ShapeDtypeStruct((T, D_MODEL), F32),
            jax.ShapeDtypeStruct((T, D_MODEL), F32),
            jax.ShapeDtypeStruct((T, 2 * D_MODEL), _MXU),
            jax.ShapeDtypeStruct((1, D_MODEL), F32),
            jax.ShapeDtypeStruct((N_CHIPS, 2, 3, rows, D_MODEL), F32),
        ],
        scratch_shapes=[pltpu.VMEM((3, D_MODEL, D_MODEL), _MXU), pltpu.VMEM((3, D_MODEL, D_MODEL), F32),
                        pltpu.SemaphoreType.DMA((3 * N_CHIPS * 2,))],
        compiler_params=_params(("arbitrary",)),
    )(ya, yb, proj, proj, x2d, tgt2d, g_fin, wpa, wpb, wout)


DX_TILE = 512


def _inproj_bwd_dx(dparts, w_all, x2d, dx2, g_in, first, count, prev, name, deps=()):
    T = x2d.shape[0]
    tm = min(DX_TILE, T)
    n_d = len(dparts)
    groups = [(a, k) for a, d in enumerate(dparts) for k in range(d.shape[1] // D_MODEL)]
    dg_start = jnp.zeros((1, D_MODEL), F32) if prev is None else prev[1]
    carried = () if prev is None else (prev[0],)

    def body(*refs):
        d_refs = refs[:n_d]
        x_ref, dx2_ref, g_ref, dg0_ref, w_hbm = refs[n_d:n_d + 5]
        dx_ref, dg_ref, w_ref, sem = refs[-4:]

        @pl.when(pl.program_id(0) == 0)
        def _():
            cp = pltpu.make_async_copy(w_hbm, w_ref, sem)
            cp.start()
            cp.wait()
            dg_ref[...] = dg0_ref[...]

        dh = jnp.zeros((tm, D_MODEL), F32)
        for j, (a, k) in enumerate(groups):
            dh = dh + _dot_nt(d_refs[a][:, k * D_MODEL:(k + 1) * D_MODEL],
                              w_ref[j // 2, :, (j % 2) * D_MODEL:(j % 2 + 1) * D_MODEL])
        x = x_ref[...]
        r = lax.rsqrt(jnp.mean(x * x, axis=-1, keepdims=True) + EPS)
        nx = x * r
        dg_ref[...] += jnp.sum(dh * nx, axis=0, keepdims=True)
        dhg = dh * g_ref[...]
        dx_ref[...] = dx2_ref[...] + r * (dhg - nx * jnp.mean(dhg * nx, axis=-1, keepdims=True))

    tile = pl.BlockSpec((tm, D_MODEL), lambda i: (first + i, 0))
    one = pl.BlockSpec((1, D_MODEL), lambda i: (0, 0))
    return pl.pallas_call(
        body,
        name=name,
        grid=(count,),
        in_specs=[pl.BlockSpec((tm, d.shape[1]), lambda i: (first + i, 0)) for d in dparts]
        + [tile, tile, one, one, ANY_SPEC] + [ANY_SPEC] * (len(carried) + len(deps)),
        out_specs=[tile, one],
        out_shape=[jax.ShapeDtypeStruct((T, D_MODEL), F32), jax.ShapeDtypeStruct((1, D_MODEL), F32)],
        input_output_aliases={n_d + 5: 0} if carried else {},
        scratch_shapes=[pltpu.VMEM(w_all.shape, w_all.dtype), pltpu.SemaphoreType.DMA],
        compiler_params=_params(("arbitrary",)),
    )(*dparts, x2d, dx2, g_in, dg_start, w_all, *carried, *deps)


def _inproj_bwd_dw(ht, dparts, name, deps=()):
    T = ht.shape[1]
    tn = 512
    half = D_MODEL // 2
    per_chip = 2 * D_MODEL // tn
    n_d = len(dparts)
    tiles = [(a, t) for a, d in enumerate(dparts) for t in range(d.shape[1] // tn)]
    offs = [sum(d.shape[1] // tn for d in dparts[:a]) for a in range(n_d)]

    def body(*refs):
        ht_ref = refs[0]
        d_refs = refs[1:1 + n_d]
        out_ref = refs[-1]
        t = pl.program_id(0)

        for a in range(n_d):
            lo, hi = offs[a], offs[a] + dparts[a].shape[1] // tn

            @pl.when((t >= lo) & (t < hi))
            def _(a=a):
                g = _dot(ht_ref[...], d_refs[a][...])
                out_ref[0, 0] = g[:half]
                out_ref[0, 1] = g[half:]

    def dspec(a):
        n_a = dparts[a].shape[1] // tn
        return pl.BlockSpec((T, tn), lambda t: (0, jnp.clip(t - offs[a], 0, n_a - 1)))

    return pl.pallas_call(
        body,
        name=name,
        grid=(len(tiles),),
        in_specs=[pl.BlockSpec((D_MODEL, T), lambda t: (0, 0))] + [dspec(a) for a in range(n_d)]
        + [ANY_SPEC] * len(deps),
        out_specs=pl.BlockSpec((1, 2, half, tn), lambda t: (t // per_chip, 0, 0, t % per_chip)),
        out_shape=jax.ShapeDtypeStruct((len(tiles) // per_chip, 2, half, 2 * D_MODEL), F32),
        compiler_params=_params(("parallel",)),
    )(ht, *dparts, *deps)


def _coords():
    return lax.axis_index("x"), lax.axis_index("y"), lax.axis_index("c")


def _other_chips(x, y):
    return [(1 - x, y), (x, 1 - y), (1 - x, 1 - y)]


def _chunks(rows, n):
    size = rows // n
    return [pl.ds(q * size, size) for q in range(n)]


HBM_SPEC = pl.BlockSpec(memory_space=pltpu.HBM)
SEM_SPEC = pl.BlockSpec(memory_space=pltpu.SEMAPHORE)
DATAFLOW = pltpu.SideEffectType.DATAFLOW_SIDE_EFFECTING


def _copies_start(bufs, plan, n_copies, name, deps=()):
    n = len(bufs)
    n_deps = len(deps)

    def body(*refs):
        ins = refs[:n]
        send_sems, recv_sems = refs[n + n_deps], refs[n + n_deps + 1]
        token = refs[-1]
        for k, send, _ in plan(ins):
            if send is not None:
                src, dst, dev, pred = send
                cp = pltpu.make_async_remote_copy(src_ref=src, dst_ref=dst, send_sem=send_sems.at[k],
                                                  recv_sem=recv_sems.at[k], device_id=dev, device_id_type=MESH)
                if pred is None:
                    cp.start()
                else:
                    pl.when(pred)(cp.start)
        token[...] = jnp.zeros_like(token)

    hbm = [pltpu.with_memory_space_constraint(b, pltpu.HBM) for b in bufs]
    outs = pl.pallas_call(
        body,
        name=name,
        in_specs=[HBM_SPEC] * n + [ANY_SPEC] * n_deps,
        out_specs=(SEM_SPEC, SEM_SPEC, *([HBM_SPEC] * n), pl.BlockSpec(memory_space=pltpu.VMEM)),
        out_shape=(pltpu.SemaphoreType.DMA((n_copies,)), pltpu.SemaphoreType.DMA((n_copies,)),
                   *[pltpu.HBM(b.shape, b.dtype) for b in bufs], jax.ShapeDtypeStruct((8, 128), F32)),
        input_output_aliases={a: 2 + a for a in range(n)},
        compiler_params=pltpu.CompilerParams(has_side_effects=DATAFLOW),
    )(*hbm, *deps)
    return outs[0], outs[1], list(outs[2:2 + n]), outs[-1]


def _copies_wait(send_sems, recv_sems, bufs, after, plan, name, only=None):
    n = len(bufs)

    def body(*refs):
        ins = refs[:n]
        s_sems, r_sems = refs[n], refs[n + 1]
        for k, send, recv in plan(ins):
            if only is not None and k not in only:
                continue
            if send is not None:
                src, dst, dev, pred = send
                cp = pltpu.make_async_remote_copy(src_ref=src, dst_ref=dst, send_sem=s_sems.at[k],
                                                  recv_sem=r_sems.at[k], device_id=dev, device_id_type=MESH)
                if pred is None:
                    cp.wait_send()
                else:
                    pl.when(pred)(cp.wait_send)
            if recv is not None:
                dst, pred = recv
                cp = pltpu.make_async_remote_copy(src_ref=dst, dst_ref=dst, send_sem=s_sems.at[k],
                                                  recv_sem=r_sems.at[k], device_id=_coords(), device_id_type=MESH)
                if pred is None:
                    cp.wait_recv()
                else:
                    pl.when(pred)(cp.wait_recv)

    outs = pl.pallas_call(
        body,
        name=name,
        in_specs=[HBM_SPEC] * n + [SEM_SPEC, SEM_SPEC, pl.BlockSpec(memory_space=pl.ANY)],
        out_specs=[HBM_SPEC] * n,
        out_shape=[pltpu.HBM(b.shape, b.dtype) for b in bufs],
        input_output_aliases={a: a for a in range(n)},
        compiler_params=pltpu.CompilerParams(has_side_effects=DATAFLOW),
    )(*bufs, send_sems, recv_sems, after)
    return list(outs)


def _gather_plan(n_bufs):
    def plan(refs):
        x, y, c = _coords()
        me = 2 * x + y
        out = []
        for k, (px, py) in enumerate(_other_chips(x, y)):
            for a in range(n_bufs):
                out.append((k * n_bufs + a, (refs[a].at[me], refs[a].at[me], (px, py, c), None),
                            (refs[a].at[2 * px + py], None)))
        return out
    return plan


def _cast_into_slot(ws, name):
    n = len(ws)
    nt = 2

    def body(s_ref, *refs):
        for a in range(n):
            refs[n + a][0] = refs[a][...].astype(refs[n + a].dtype)

    xi, yi, _ = _coords()
    return pl.pallas_call(
        body,
        name=name,
        grid_spec=pltpu.PrefetchScalarGridSpec(
            num_scalar_prefetch=1,
            grid=(2, nt),
            in_specs=[pl.BlockSpec((1, w.shape[1] // nt, w.shape[2]), lambda hf, i, s: (hf, i, 0)) for w in ws],
            out_specs=[pl.BlockSpec((1, 1, w.shape[1] // nt, w.shape[2]), lambda hf, i, s: (s[0], hf, i, 0)) for w in ws],
        ),
        out_shape=[jax.ShapeDtypeStruct((N_CHIPS,) + w.shape, _MXU) for w in ws],
        compiler_params=_params(("parallel", "parallel")),
    )((2 * xi + yi).reshape(1).astype(jnp.int32), *ws)


def _chip_gather_plan(stage, n_bufs):
    def plan(refs):
        x, y, c = _coords()
        me = 2 * x + y
        near = [(1 - x, y), (x, 1 - y)]
        slots = [2 * (1 - x) + y, 2 * x + (1 - y), 2 * (1 - x) + (1 - y)]
        sibling = (x, y, 1 - c)
        pass_to = (jnp.where(c == 0, x, 1 - x), jnp.where(c == 0, 1 - y, y), c)
        pass_slot = jnp.where(c == 0, slots[0], slots[1])
        out = []

        def move(src_slot, to, land_slot, land_core, pieces):
            for a, buf in enumerate(refs):
                for rows in _chunks(buf.shape[2], pieces[a]):
                    out.append((len(out), (buf.at[src_slot, c, rows], buf.at[src_slot, c, rows], to, None),
                                (buf.at[land_slot, land_core, rows], None)))

        if stage == "near":
            for k, chip in enumerate(near):
                move(me, (*chip, c), slots[k], c, NEAR_PIECES[:n_bufs])
        elif stage == "pass":
            move(pass_slot, pass_to, slots[2], c, PASS_PIECES[:n_bufs])
            for k in range(2):
                move(slots[k], sibling, slots[k], 1 - c, [1] * n_bufs)
        else:
            move(slots[2], sibling, slots[2], 1 - c, [1] * n_bufs)
        return out
    return plan


NEAR_PIECES = (2, 1)
PASS_PIECES = (2, 1)


def _chip_gather_copies(stage, n_bufs):
    if stage == "near":
        return 2 * sum(NEAR_PIECES[:n_bufs]), None
    if stage == "pass":
        n_pass = sum(PASS_PIECES[:n_bufs])
        return n_pass + 2 * n_bufs, set(range(n_pass))
    return n_bufs, None


def _swap_plan(n_slabs):
    def plan(refs):
        x, y, c = _coords()
        out, k = [], 0
        for i, n in enumerate(n_slabs):
            g, land = refs[2 * i], refs[2 * i + 1]
            for p in range(n):
                out.append((k, (g.at[p, 1 - c], land.at[p], (x, y, 1 - c), None), (land.at[p], None)))
                k += 1
        return out
    return plan


def _is_one_of(chip, dests):
    hit = chip == dests[0]
    for d in dests[1:]:
        hit = hit | (chip == d)
    return hit


def _slab_of(chip, dests):
    return sum(j * (chip == d).astype(jnp.int32) for j, d in enumerate(dests))


def _scatter_plan(dest_sets):
    def plan(refs):
        x, y, c = _coords()
        me = 2 * x + y
        out = []
        for k, (px, py) in enumerate(_other_chips(x, y)):
            peer = 2 * px + py
            for i, dests in enumerate(dest_sets):
                cs, land = refs[2 * i], refs[2 * i + 1]
                everyone = len(dests) == N_CHIPS
                send = (cs.at[_slab_of(peer, dests)], land.at[k], (px, py, c),
                        None if everyone else _is_one_of(peer, dests))
                recv = (land.at[k], None if everyone else _is_one_of(me, dests))
                out.append((k * len(dest_sets) + i, send, recv))
        return out
    return plan


def _join_plan(rows, n_pieces):
    def plan(refs):
        x, y, c = _coords()
        (buf,) = refs
        return [(i, (buf.at[c, piece], buf.at[c, piece], (x, y, 1 - c), None), (buf.at[1 - c, piece], None))
                for i, piece in enumerate(_chunks(rows, n_pieces))]
    return plan


def _join_plans(parts):
    def plan(refs):
        out, b0, k0 = [], 0, 0
        for part_plan, n_bufs, n_copies in parts:
            out += [(k0 + k, send, recv) for k, send, recv in part_plan(refs[b0:b0 + n_bufs])]
            b0 += n_bufs
            k0 += n_copies
        return out
    return plan


def _allgather_plan():
    def plan(refs):
        x, y, c = _coords()
        (land,) = refs
        me = 4 * x + 2 * y + c
        out = []
        for r in range(1, 8):
            px = 1 - x if r & 4 else x
            py = 1 - y if r & 2 else y
            pc = 1 - c if r & 1 else c
            out.append((r - 1, (land.at[me], land.at[me], (px, py, pc), None), (land.at[4 * px + 2 * py + pc], None)))
        return out
    return plan


def _sum_gathered(land, name):
    def body(land_ref, o_ref):
        acc = land_ref[0]
        for d in range(1, 8):
            acc = acc + land_ref[d]
        o_ref[...] = acc

    return pl.pallas_call(
        body,
        name=name,
        out_shape=jax.ShapeDtypeStruct(land.shape[1:], F32),
        compiler_params=_params(),
    )(land)


def _join_halves(bufs, n_chunks, name, deps=()):
    n = len(bufs)
    pieces = [(a, rows) for a in range(n) for rows in _chunks(bufs[a].shape[1], n_chunks[a])]
    n_p = len(pieces)

    def body(*refs):
        outs = refs[-n - 2:-2]
        send_sems, recv_sems = refs[-2:]
        x, y, c = _coords()

        def copy(i, half):
            a, rows = pieces[i]
            return pltpu.make_async_remote_copy(
                src_ref=outs[a].at[half, rows], dst_ref=outs[a].at[half, rows], send_sem=send_sems.at[i],
                recv_sem=recv_sems.at[i], device_id=(x, y, 1 - c), device_id_type=MESH)

        sends = [copy(i, c) for i in range(n_p)]
        for cp in sends:
            cp.start()
        for i in range(n_p):
            copy(i, 1 - c).wait_recv()
        for cp in sends:
            cp.wait_send()

    anyspec = pl.BlockSpec(memory_space=pl.ANY)
    sems = pltpu.SemaphoreType.DMA((n_p,))
    return pl.pallas_call(
        body,
        name=name,
        in_specs=[anyspec] * (n + len(deps)),
        out_specs=[anyspec] * n,
        out_shape=[jax.ShapeDtypeStruct(b.shape, b.dtype) for b in bufs],
        input_output_aliases={a: a for a in range(n)},
        scratch_shapes=[sems, sems],
    )(*bufs, *deps)


def _row_tile(rows, cap):
    t = cap
    while rows % t:
        t //= 2
    return t


def _add_my_half(g, r, name):
    n_slabs, _, R, C = g.shape
    tr = R if n_slabs > 1 else _row_tile(R, 256)

    def body(c_ref, g_ref, r_ref, o_ref):
        o_ref[...] = (g_ref[0] + r_ref[...]).astype(o_ref.dtype)

    return pl.pallas_call(
        body,
        name=name,
        grid_spec=pltpu.PrefetchScalarGridSpec(
            num_scalar_prefetch=1,
            grid=(n_slabs, R // tr),
            in_specs=[pl.BlockSpec((1, 1, tr, C), lambda p, i, c_ref: (p, c_ref[0], i, 0)),
                      pl.BlockSpec((1, tr, C), lambda p, i, c_ref: (p, i, 0))],
            out_specs=pl.BlockSpec((1, tr, C), lambda p, i, c_ref: (p, i, 0)),
        ),
        out_shape=jax.ShapeDtypeStruct(r.shape, jnp.bfloat16),
        compiler_params=_params(("parallel", "parallel")),
    )(lax.axis_index("c").reshape(1).astype(jnp.int32), g, r)


def _sum_slabs(own, got, name, deps=()):
    _, R, C = own.shape
    tr = _row_tile(R, 256)

    def body(s_ref, own_ref, got_ref, *rest):
        rest[-1][0] = ((own_ref[0].astype(F32) + got_ref[0].astype(F32)) + got_ref[1].astype(F32)) + got_ref[2].astype(F32)

    xi, yi, ci = _coords()
    return pl.pallas_call(
        body,
        name=name,
        grid_spec=pltpu.PrefetchScalarGridSpec(
            num_scalar_prefetch=1,
            grid=(R // tr,),
            in_specs=[pl.BlockSpec((1, tr, C), lambda i, s: (s[0], i, 0)),
                      pl.BlockSpec((3, tr, C), lambda i, s: (0, i, 0))] + [ANY_SPEC] * len(deps),
            out_specs=pl.BlockSpec((1, tr, C), lambda i, s: (s[1], i, 0)),
        ),
        out_shape=jax.ShapeDtypeStruct((2, R, C), F32),
        compiler_params=_params(("parallel",)),
    )(jnp.stack([2 * xi + yi, ci]).astype(jnp.int32), own, got, *deps)


def _sum_parts(owns, got, dest_sets, name):
    n = len(owns)
    _, R, C = owns[0].shape
    tr = _row_tile(R, 256)

    def body(s_ref, *refs):
        got_ref, o_ref = refs[n], refs[-1]
        total = jnp.zeros((tr, C), F32)
        for i in range(n):
            total = total + jnp.where(s_ref[2 + 2 * i] == 1, refs[i][0].astype(F32), 0.0)
        o_ref[0] = ((total + got_ref[0].astype(F32)) + got_ref[1].astype(F32)) + got_ref[2].astype(F32)

    xi, yi, ci = _coords()
    me = 2 * xi + yi
    scalars = [ci, ci]
    for dests in dest_sets:
        scalars += [_is_one_of(me, dests).astype(jnp.int32), _slab_of(me, dests)]
    own_spec = lambda i: pl.BlockSpec((1, tr, C), lambda r, s: (s[3 + 2 * i], r, 0))
    return pl.pallas_call(
        body,
        name=name,
        grid_spec=pltpu.PrefetchScalarGridSpec(
            num_scalar_prefetch=1,
            grid=(R // tr,),
            in_specs=[own_spec(i) for i in range(n)] + [pl.BlockSpec((3, tr, C), lambda r, s: (0, r, 0))],
            out_specs=pl.BlockSpec((1, tr, C), lambda r, s: (s[0], r, 0)),
        ),
        out_shape=jax.ShapeDtypeStruct((2, R, C), F32),
        compiler_params=_params(("parallel",)),
    )(jnp.stack(scalars).astype(jnp.int32), *owns, got)


def _adamw_math(w, g, m, v):
    m = ADAM_B1 * m + (1.0 - ADAM_B1) * g
    v = ADAM_B2 * v + (1.0 - ADAM_B2) * (g * g)
    m_hat = m / (1.0 - ADAM_B1 ** ADAM_STEP)
    v_hat = v / (1.0 - ADAM_B2 ** ADAM_STEP)
    delta = -ADAM_LR * (m_hat / (jnp.sqrt(v_hat) + ADAM_EPS) + ADAM_WD * w)
    return delta, m, v


def _adamw_halves(ws, g, ms, vs, half, prev, name, deps=()):
    n = len(ws)
    _, _, R, C = g.shape
    tr = _row_tile(R, 128)
    steps = R // tr
    carried = [] if prev is None else [a for four in prev for a in four]
    both = half is None
    which = (lambda i, s: i // steps) if both else (lambda i, s: s[0])
    half = 0 if both else half

    def body(s_ref, *refs):
        w_refs, g_refs, m_refs, v_refs = (refs[k * n:(k + 1) * n] for k in range(4))
        outs = refs[len(refs) - 4 * n:]
        for a in range(n):
            grad = g_refs[a][0, 0]
            d, mn, vn = _adamw_math(w_refs[a][...], grad, m_refs[a][...], v_refs[a][...])
            for o, val in zip(outs[4 * a:4 * a + 4], (grad, d, mn, vn)):
                o[...] = val

    rows = pl.BlockSpec((tr, C), lambda i, s: (which(i, s) * steps + i % steps, 0))
    grad_spec = lambda a: pl.BlockSpec((1, 1, tr, C), lambda i, s: (which(i, s), a, i % steps, 0))
    n_in = 4 * n
    outs = pl.pallas_call(
        body,
        name=name,
        grid_spec=pltpu.PrefetchScalarGridSpec(
            num_scalar_prefetch=1,
            grid=(2 * steps if both else steps,),
            in_specs=[rows] * n + [grad_spec(a) for a in range(n)] + [rows] * (2 * n)
            + [ANY_SPEC] * (len(carried) + len(deps)),
            out_specs=[rows] * (4 * n),
        ),
        out_shape=[jax.ShapeDtypeStruct((2 * R, C), F32)] * (4 * n),
        input_output_aliases={1 + n_in + k: k for k in range(len(carried))},
        compiler_params=_params(("parallel",)),
    )(jnp.reshape(half, (1,)).astype(jnp.int32), *ws, *([g] * n), *ms, *vs, *carried, *deps)
    return [outs[4 * a:4 * a + 4] for a in range(n)]


def _adamw_small(ws, gs, ms, vs, name):
    n = len(ws)

    def body(*refs):
        for a in range(n):
            d, mn, vn = _adamw_math(refs[a][...], refs[n + a][...], refs[2 * n + a][...], refs[3 * n + a][...])
            refs[4 * n + a][...] = d
            refs[5 * n + a][...] = mn
            refs[6 * n + a][...] = vn

    shapes = [jax.ShapeDtypeStruct(w.shape, F32) for w in ws]
    outs = pl.pallas_call(
        body,
        name=name,
        out_shape=shapes * 3,
        compiler_params=_params(),
    )(*ws, *gs, *ms, *vs)
    return outs[:n], outs[n:2 * n], outs[2 * n:]


def _to_blockdiag(w):
    per = CW // LRU_BW
    w4 = w.reshape(N_CT, per, LRU_BW, LRU_BW)
    eye = jnp.eye(per, dtype=w.dtype)
    return (w4[:, :, :, None, :] * eye[None, :, None, :, None]).reshape(N_CT, CW, CW)


def _from_blockdiag(g):
    per = CW // LRU_BW
    g5 = g.reshape(N_CT, per, LRU_BW, per, LRU_BW)
    return jnp.stack([g5[:, b, :, b, :] for b in range(per)], axis=1).reshape(LRU_BLOCKS, LRU_BW, LRU_BW)


def _local_grads(x2d, tgt2d, B, S, g_in, in_proj, conv_b, gate_x_w, gate_x_b, gate_a_w, gate_a_b, lam,
                 proj_weights, g_fin, reduce):
    wx_bd = _c(_to_blockdiag(gate_x_w))
    wa_bd = _c(_to_blockdiag(gate_a_w))
    tables = _retention_tables(S)

    proj, ht, w_all, conv_w, gain = in_proj(x2d, g_in)
    gain3 = gain.reshape(HEADS, 1, DK)
    hlru, ya = _lru_fwd(proj, conv_w, conv_b, wx_bd, wa_bd, gate_x_b, gate_a_b, lam, B, S)
    o_pre, yb, states = _ret_fwd(proj, tables, gain3, B, S)
    wpa, wpb, wout = proj_weights(yb)
    loss, dx2, dya, dyb, dm, dgf, gw_proj = _mid(ya, yb, proj, x2d, tgt2d, wpa, wpb, wout, g_fin)
    g3 = _inproj_bwd_dw(ht, [dm], "inproj_bwd_dw_m")
    deps = reduce.m_ready(gw_proj, g3)
    dr, dgain = _ret_bwd(dyb, o_pre, proj, states, tables, gain3, B, S, deps)
    deps = reduce.ret_done(dr)
    g12 = _inproj_bwd_dw(ht, [dr], "inproj_bwd_dw_r", deps)
    deps = reduce.r_ready(g12)
    dxa, dga, dcw, dcb, dwx_bd, dwa_bd, dbx, dba, dlam = _lru_bwd(
        dya, proj, hlru, conv_w, conv_b, wx_bd, wa_bd, gate_x_b, gate_a_b, lam, B, S, deps)
    small = dict(conv_w=dcw, conv_b=dcb, gate_x_w=_from_blockdiag(dwx_bd), gate_x_b=dbx,
                 gate_a_w=_from_blockdiag(dwa_bd), gate_a_b=dba, lru_lambda=dlam, gn_gain=dgain.reshape(HEADS, DK),
                 norm_final=dgf)
    loss_rows = jnp.broadcast_to(loss, (SUBLANES, LANES))
    deps = reduce.lru_done(dxa, jnp.concatenate([_pack_small(small), loss_rows], axis=0))
    g0 = _inproj_bwd_dw(ht, [dxa, dga], "inproj_bwd_dw_a", deps)
    deps = reduce.a_ready(g0)
    n_tiles = x2d.shape[0] // min(DX_TILE, x2d.shape[0])
    grad_x, dgin = _inproj_bwd_dx([dxa, dga, dr, dm], w_all, x2d, dx2, g_in, 0, n_tiles, None, "inproj_bwd_dx", deps)
    return grad_x, dgin


ALL_CHIPS = (0, 1, 2, 3)


class _GradReduce:
    def __init__(self, proj_done):
        self.pending = {}
        self.proj_done = proj_done
        self.land_in = None

    def _start(self, key, parts, name):
        bufs, plans, shared = [], [], None
        for part_bufs, plan, n_copies, part_shared in parts:
            if part_shared is not None:
                shared = len(bufs) + part_shared
            plans.append((plan, len(part_bufs), n_copies))
            bufs += part_bufs
        plan = _join_plans(plans)
        send_sems, recv_sems, bufs, token = _copies_start(bufs, plan, sum(p[2] for p in plans), name + "_start")
        if shared is not None:
            self.land_in = bufs[shared]
        self.pending[key] = (send_sems, recv_sems, bufs, plan, name + "_wait", shared)
        return (token,)

    def _finish(self, key, after):
        send_sems, recv_sems, bufs, plan, name, shared = self.pending.pop(key)
        if shared is not None:
            bufs[shared] = self.land_in
        bufs = _copies_wait(send_sems, recv_sems, bufs, after, plan, name)
        if shared is not None:
            self.land_in = bufs[shared]
        return bufs

    @staticmethod
    def _swap(pieces):
        bufs = []
        for g in pieces:
            bufs += [g, lax.empty((g.shape[0],) + g.shape[2:], F32)]
        n_slabs = [g.shape[0] for g in pieces]
        return bufs, _swap_plan(n_slabs), sum(n_slabs), None

    def _scatter(self, sums, dest_sets):
        bufs = []
        for cs in sums:
            bufs += [cs, lax.empty((3,) + cs.shape[1:], cs.dtype)]
        if self.land_in is not None:
            bufs[-1] = self.land_in
        return bufs, _scatter_plan(dest_sets), 3 * len(sums), len(bufs) - 1

    @staticmethod
    def _gather8(block):
        x, y, c = _coords()
        land = lax.dynamic_update_slice(lax.empty((8,) + block.shape, F32), block[None], (4 * x + 2 * y + c, 0, 0))
        return [land], _allgather_plan(), 7, None

    def m_ready(self, gw_proj, g3):
        rows = gw_proj.shape[2] * gw_proj.shape[3]
        return self._start("m", [self._swap([gw_proj.reshape(N_CHIPS, 2, rows, D_MODEL), g3])], "swap_m")

    def ret_done(self, after):
        proj, land_p, g3, land_3 = self._finish("m", after)
        sums_m = [_add_my_half(proj, land_p, "chip_sum_proj"), _add_my_half(g3, land_3, "chip_sum_m")]
        return self._start("sm", [self._scatter(sums_m, [ALL_CHIPS, (3,)])], "scatter_m")

    def r_ready(self, g12):
        return self._start("r", [self._swap([g12])], "swap_r")

    def lru_done(self, after, packed):
        g12, land_12 = self._finish("r", after)
        sums_r = [_add_my_half(g12, land_12, "chip_sum_r")]
        return (self._start("sr", [self._scatter(sums_r, [(1, 2)])], "scatter_r")
                + self._start("small", [self._gather8(packed)], "gather_small"))

    def a_ready(self, g0):
        (token,) = self._start("a", [self._swap([g0])], "swap_a")
        csp, gotp, self.cs3, _ = self._finish("sm", token)
        half_proj = _sum_slabs(csp, gotp, "sum_w_proj")
        g0, land_0 = self._finish("a", half_proj)
        deps = self._start("sa", [self._scatter([_add_my_half(g0, land_0, "chip_sum_a")], [(0,)])], "scatter_a")
        self.proj_done(_join_halves([half_proj], [4], "join_halves_proj", deps)[0])
        return deps

    def finish(self, dgin, w_in_done):
        (token,) = self._start("n", [self._gather8(dgin)], "gather_norm_in")
        (small,) = self._finish("small", token)
        cs12, _ = self._finish("sr", token)
        cs0, _ = self._finish("sa", token)
        half_in = _sum_parts([self.cs3, cs12, cs0], self.land_in, [(3,), (1, 2), (0,)], "sum_w_in")
        deps = self._start("j", [([half_in], _join_plan(half_in.shape[1], 8), 8, None)], "join_w_in")
        first = w_in_done(self.pending["j"][2][0], True, None, deps)
        (g_in,) = self._finish("j", first[1])
        done = w_in_done(g_in, False, first, ())
        (norm_in,) = self._finish("n", done[1])
        return _sum_gathered(small, "sum_small_grads"), _sum_gathered(norm_in, "sum_norm_in_grad")


_SMALL = ("gate_x_w", "gate_a_w", "conv_w", "conv_b", "gate_x_b", "gate_a_b", "lru_lambda", "gn_gain", "norm_final")
_SMALL_SHAPES = dict(gate_x_w=(LRU_BLOCKS, LRU_BW, LRU_BW), gate_a_w=(LRU_BLOCKS, LRU_BW, LRU_BW),
                     norm_in=(1, D_MODEL), conv_w=(CONV, D_MODEL), conv_b=(1, D_MODEL), gate_x_b=(1, D_MODEL),
                     gate_a_b=(1, D_MODEL), lru_lambda=(1, D_MODEL), gn_gain=(HEADS, DK), norm_final=(1, D_MODEL))


def _pack_small(small):
    return jnp.concatenate([small[k].reshape(-1, 128) for k in _SMALL], axis=0)


def _unpack_small(packed):
    out, r = {}, 0
    for k in _SMALL:
        shape = _SMALL_SHAPES[k]
        rows = 1
        for s in shape:
            rows *= s
        rows //= 128
        out[k] = packed[r:r + rows].reshape(shape)
        r += rows
    return out


def kernel(x, norm_in, w_in, conv_w, conv_b, gate_x_w, gate_x_b, gate_a_w, gate_a_b, lru_lambda, gn_gain, w_proj_a, w_proj_b, w_out, norm_final, loss_target, m_norm_in, m_w_in, m_conv_w, m_conv_b, m_gate_x_w, m_gate_x_b, m_gate_a_w, m_gate_a_b, m_lru_lambda, m_gn_gain, m_w_proj_a, m_w_proj_b, m_w_out, m_norm_final, v_norm_in, v_w_in, v_conv_w, v_conv_b, v_gate_x_w, v_gate_x_b, v_gate_a_w, v_gate_a_b, v_lru_lambda, v_gn_gain, v_w_proj_a, v_w_proj_b, v_w_out, v_norm_final):
    B, S, _ = x.shape
    T = B * S
    xi, yi, ci = _coords()
    chip = 2 * xi + yi

    cshard = D_MODEL // N_CHIPS
    mine = _cast_into_slot([w_in[0].reshape(2, D_MODEL // 2, 2 * D_MODEL)]
                           + [w[0].reshape(2, cshard // 2, D_MODEL) for w in (w_proj_a, w_proj_b, w_out)],
                           "cast_weights")
    plan = _gather_plan(3)
    pending_proj = []
    gshard = DK // N_CHIPS
    tiny = jnp.concatenate([conv_w[0], jnp.zeros((4, cshard), F32), jnp.pad(gn_gain[0], ((0, 4), (0, cshard - gshard)))],
                           axis=0).reshape(1, 2, SUBLANES, cshard)
    tiny_buf = lax.dynamic_update_slice(lax.empty((N_CHIPS, 2, SUBLANES, cshard), F32), tiny, (chip, 0, 0, 0))
    near_plan, pass_plan, far_plan = (_chip_gather_plan(stage, 2) for stage in ("near", "pass", "far"))
    (n_near, _), (n_pass, passed_on), (n_far, _) = (_chip_gather_copies(stage, 2) for stage in ("near", "pass", "far"))
    halves = set(range(n_pass)) - passed_on
    near_s, near_r, bufs, near_token = _copies_start([mine[0], tiny_buf], near_plan, n_near, "gather_near_start")

    def in_proj(x2d, g_in):
        as_w = lambda b: b[0].reshape(N_CHIPS, D_MODEL, 2 * D_MODEL)
        slot_x, slot_y, slot_d = 2 * (1 - xi) + yi, 2 * xi + (1 - yi), 2 * (1 - xi) + (1 - yi)
        ids = lambda *chips: jnp.stack(chips).astype(jnp.int32)
        proj, hb, ht = _inproj_first(x2d, g_in, as_w(bufs), ids(chip), "inproj_own", (near_token,))
        got = _copies_wait(near_s, near_r, bufs, proj, near_plan, "gather_near_wait")
        pass_s, pass_r, got, pass_token = _copies_start(got, pass_plan, n_pass, "gather_pass_start")
        got = _copies_wait(pass_s, pass_r, got, pass_token, pass_plan, "gather_pass_wait_halves", only=halves)
        proj = _inproj_more(hb, as_w(got), ids(slot_x, slot_y), proj, "inproj_near")
        got = _copies_wait(pass_s, pass_r, got, proj, pass_plan, "gather_pass_wait_far", only=passed_on)
        pending_proj.append(_copies_start(mine[1:], plan, 9, "gather_proj_start", (got[0],)))
        far_s, far_r, got, far_token = _copies_start(got, far_plan, n_far, "gather_far_start")
        got = _copies_wait(far_s, far_r, got, far_token, far_plan, "gather_far_wait")
        proj = _inproj_more(hb, as_w(got), ids(slot_d), proj, "inproj_far")
        tiny_all = got[1].reshape(N_CHIPS, 2 * SUBLANES, cshard)
        conv_w_full = jnp.transpose(tiny_all[:, 0:CONV, :], (1, 0, 2)).reshape(CONV, D_MODEL)
        gain_full = jnp.transpose(tiny_all[:, 8:8 + HEADS, :gshard], (1, 0, 2)).reshape(HEADS, DK)
        return proj, ht, as_w(got), conv_w_full, gain_full

    def proj_weights(after):
        s_sems, r_sems, pbufs, _ = pending_proj[0]
        got = _copies_wait(s_sems, r_sems, pbufs, after, plan, "gather_proj_wait")
        return [b.reshape(D_MODEL, D_MODEL) for b in got]

    weights = dict(norm_in=norm_in, w_in=w_in, conv_w=conv_w, conv_b=conv_b, gate_x_w=gate_x_w, gate_x_b=gate_x_b,
                   gate_a_w=gate_a_w, gate_a_b=gate_a_b, lru_lambda=lru_lambda, gn_gain=gn_gain, w_proj_a=w_proj_a,
                   w_proj_b=w_proj_b, w_out=w_out, norm_final=norm_final)
    ms = dict(norm_in=m_norm_in, w_in=m_w_in, conv_w=m_conv_w, conv_b=m_conv_b, gate_x_w=m_gate_x_w,
              gate_x_b=m_gate_x_b, gate_a_w=m_gate_a_w, gate_a_b=m_gate_a_b, lru_lambda=m_lru_lambda, gn_gain=m_gn_gain,
              w_proj_a=m_w_proj_a, w_proj_b=m_w_proj_b, w_out=m_w_out, norm_final=m_norm_final)
    vs = dict(norm_in=v_norm_in, w_in=v_w_in, conv_w=v_conv_w, conv_b=v_conv_b, gate_x_w=v_gate_x_w,
              gate_x_b=v_gate_x_b, gate_a_w=v_gate_a_w, gate_a_b=v_gate_a_b, lru_lambda=v_lru_lambda, gn_gain=v_gn_gain,
              w_proj_a=v_w_proj_a, w_proj_b=v_w_proj_b, w_out=v_w_out, norm_final=v_norm_final)
    names = list(weights)
    grads, delta, new_m, new_v = {}, {}, {}, {}

    def update_big(keys, g, half, prev, name, deps=()):
        two = lambda a: a.reshape(a.shape[1], a.shape[2])
        res = _adamw_halves([two(weights[k]) for k in keys], g, [two(ms[k]) for k in keys], [two(vs[k]) for k in keys],
                            half, prev, name, deps)
        for k, (gk, d, mn, vn) in zip(keys, res):
            shp = weights[k].shape
            grads[k], delta[k], new_m[k], new_v[k] = gk.reshape(shp), d.reshape(shp), mn.reshape(shp), vn.reshape(shp)
        return res

    def proj_done(g_proj):
        g4 = g_proj.reshape(2, 3, D_MODEL // (2 * N_CHIPS), D_MODEL)
        return update_big(("w_proj_a", "w_proj_b", "w_out"), g4, None, None, "adamw_proj")[-1][1]

    def w_in_done(g_in, own, prev, deps):
        g4 = g_in.reshape(2, 1, D_MODEL // 2, 2 * D_MODEL)
        return update_big(("w_in",), g4, ci if own else 1 - ci, None if prev is None else [prev],
                          "adamw_w_in_own" if own else "adamw_w_in_other", deps)[0]

    reduce = _GradReduce(proj_done)
    grad_x, dgin = _local_grads(
        x.reshape(T, D_MODEL), loss_target.reshape(T, D_MODEL), B, S, norm_in, in_proj, conv_b,
        gate_x_w[0], gate_x_b, gate_a_w[0], gate_a_b, lru_lambda, proj_weights,
        norm_final.reshape(1, D_MODEL), reduce)

    small_sum, g_norm_in = reduce.finish(dgin.reshape(SUBLANES, LANES), w_in_done)
    loss = small_sum[small_sum.shape[0] - SUBLANES, 0]

    gsm = _unpack_small(small_sum)
    gsm["norm_in"] = g_norm_in
    gsm["conv_w"] = lax.dynamic_slice_in_dim(gsm["conv_w"], chip * cshard, cshard, axis=1)
    gsm["gn_gain"] = lax.dynamic_slice_in_dim(gsm["gn_gain"], chip * gshard, gshard, axis=1)
    smalls = [k for k in names if k not in delta]

    def view(a):
        return a.reshape(1, -1) if a.ndim == 1 else (a.reshape(a.shape[1:]) if a.ndim > 2 else a)

    ds, mns, vns = _adamw_small([view(weights[k]) for k in smalls], [gsm[k].reshape(view(weights[k]).shape) for k in smalls],
                                [view(ms[k]) for k in smalls], [view(vs[k]) for k in smalls], "adamw_small")
    for k, d, mn, vn in zip(smalls, ds, mns, vns):
        shp = weights[k].shape
        grads[k], delta[k], new_m[k], new_v[k] = gsm[k].reshape(shp), d.reshape(shp), mn.reshape(shp), vn.reshape(shp)

    return (loss, grad_x.reshape(B, S, D_MODEL), *[grads[k] for k in names], *[delta[k] for k in names],
            *[new_m[k] for k in names], *[new_v[k] for k in names])
```

```python
import jax
import jax.numpy as jnp
from jax import lax
from jax.experimental import pallas as pl
from jax.experimental.pallas import tpu as pltpu

F32 = jnp.float32
_MXU = jnp.bfloat16

D_MODEL = 1024
N_GROUPS = 8
HEADS = 4
DK = 256
CHUNK = 128
CONV = 4
LRU_BLOCKS = 16
LRU_BW = 64
LRU_C = 8.0
ROPE_THETA = 10000.0
EPS = 1e-6
CW = 256
N_CT = D_MODEL // CW
N_CHIPS = 4
MESH = pl.DeviceIdType.MESH

ADAM_LR = 0.001
ADAM_B1 = 0.9
ADAM_B2 = 0.999
ADAM_EPS = 1e-08
ADAM_WD = 0.01
ADAM_STEP = 10

VMEM_LIMIT = 56 * 1024 * 1024

FIRST_PROJ_TILE = 1024
MORE_PROJ_TILE = 2048
SCAN_TILE = 256
MID_TILE = 256
DX_TILE = 512
DW_COLS = 512
SUM_ROWS = 256
ADAMW_ROWS = 256
JOIN_PIECES = 8


def _c(v):
    return v.astype(_MXU)


def _dot(a, b):
    return lax.dot_general(a, b, (((1,), (0,)), ((), ())), preferred_element_type=F32)


def _dot_nt(a, b):
    return lax.dot_general(a, b, (((1,), (1,)), ((), ())), preferred_element_type=F32)


def _dot_tn(a, b):
    return lax.dot_general(a, b, (((0,), (0,)), ((), ())), preferred_element_type=F32)


def _sigmoid(z):
    return 0.5 * jnp.tanh(0.5 * z) + 0.5


ANY_SPEC = pl.BlockSpec(memory_space=pl.ANY)


def _after(body, n_in, deps):
    n_deps = len(deps)

    def wrapped(*refs):
        return body(*refs[:n_in], *refs[n_in + n_deps:])

    return wrapped


def _params(sem=None):
    if sem is None:
        return pltpu.CompilerParams(vmem_limit_bytes=VMEM_LIMIT)
    return pltpu.CompilerParams(vmem_limit_bytes=VMEM_LIMIT, dimension_semantics=sem)


def _inproj_first(x2d, g_in, w_all, chips, name, deps=()):
    T = x2d.shape[0]
    tm = min(FIRST_PROJ_TILE, T)
    n_i = T // tm

    def body(s_ref, *refs):
        x_ref, g_ref, w_ref = refs[:3]
        proj_ref, hb_ref, ht_ref, h_all = refs[-4:]
        i = pl.program_id(1)
        rows = pl.ds(pl.multiple_of(i * tm, tm), tm)

        @pl.when(pl.program_id(0) == 0)
        def _():
            x = x_ref[...]
            r = lax.rsqrt(jnp.mean(x * x, axis=-1, keepdims=True) + EPS)
            h = x * r * g_ref[...]
            hb = h.astype(h_all.dtype)
            h_all[rows, :] = hb
            hb_ref[...] = hb
            ht_ref[...] = h.T.astype(ht_ref.dtype)

        proj_ref[...] = _dot(h_all[rows, :], w_ref[0])

    first = lambda j, i: jnp.where(j == 0, i, n_i - 1)
    return pl.pallas_call(
        body,
        name=name,
        grid_spec=pltpu.PrefetchScalarGridSpec(
            num_scalar_prefetch=1,
            grid=(2 * chips.shape[0], n_i),
            in_specs=[
                pl.BlockSpec((tm, D_MODEL), lambda j, i, s: (first(j, i), 0)),
                pl.BlockSpec((1, D_MODEL), lambda j, i, s: (0, 0)),
                pl.BlockSpec((1, D_MODEL, D_MODEL), lambda j, i, s: (s[j // 2], 0, j % 2)),
            ] + [ANY_SPEC] * len(deps),
            out_specs=[
                pl.BlockSpec((tm, D_MODEL), lambda j, i, s: (i, 2 * s[j // 2] + j % 2)),
                pl.BlockSpec((tm, D_MODEL), lambda j, i, s: (first(j, i), 0)),
                pl.BlockSpec((D_MODEL, tm), lambda j, i, s: (0, first(j, i))),
            ],
            scratch_shapes=[pltpu.VMEM((T, D_MODEL), _MXU)],
        ),
        out_shape=[
            jax.ShapeDtypeStruct((T, N_GROUPS * D_MODEL), F32),
            jax.ShapeDtypeStruct((T, D_MODEL), _MXU),
            jax.ShapeDtypeStruct((D_MODEL, T), _MXU),
        ],
        compiler_params=_params(("arbitrary", "arbitrary")),
    )(chips, x2d, g_in, w_all, *deps)


def _inproj_more(hb, w_all, chips, proj, name):
    T = hb.shape[0]
    tm = min(MORE_PROJ_TILE, T)

    def body(s_ref, hb_hbm, w_ref, prev_ref, proj_ref, h_all, sem):
        @pl.when((pl.program_id(0) == 0) & (pl.program_id(1) == 0))
        def _():
            cp = pltpu.make_async_copy(hb_hbm, h_all, sem)
            cp.start()
            cp.wait()

        rows = pl.ds(pl.multiple_of(pl.program_id(1) * tm, tm), tm)
        proj_ref[...] = _dot(h_all[rows, :], w_ref[0])

    return pl.pallas_call(
        body,
        name=name,
        grid_spec=pltpu.PrefetchScalarGridSpec(
            num_scalar_prefetch=1,
            grid=(2 * chips.shape[0], T // tm),
            in_specs=[
                ANY_SPEC,
                pl.BlockSpec((1, D_MODEL, D_MODEL), lambda j, i, s: (s[j // 2], 0, j % 2)),
                ANY_SPEC,
            ],
            out_specs=pl.BlockSpec((tm, D_MODEL), lambda j, i, s: (i, 2 * s[j // 2] + j % 2)),
            scratch_shapes=[pltpu.VMEM((T, D_MODEL), hb.dtype), pltpu.SemaphoreType.DMA],
        ),
        out_shape=jax.ShapeDtypeStruct(proj.shape, F32),
        input_output_aliases={3: 0},
        compiler_params=_params(("arbitrary", "arbitrary")),
    )(chips, hb, w_all, proj)


def _scan_fwd(a, u):
    n = a.shape[0]
    row = lax.broadcasted_iota(jnp.int32, a.shape, 0)
    s = 1
    while s < n:
        m = row >= s
        u = u + a * jnp.where(m, pltpu.roll(u, s, 0), 0.0)
        a = a * jnp.where(m, pltpu.roll(a, s, 0), 1.0)
        s *= 2
    return a, u


def _scan_bwd(b, g):
    n = b.shape[0]
    row = lax.broadcasted_iota(jnp.int32, b.shape, 0)
    s = 1
    while s < n:
        m = row < n - s
        g = g + b * jnp.where(m, pltpu.roll(g, n - s, 0), 0.0)
        b = b * jnp.where(m, pltpu.roll(b, n - s, 0), 1.0)
        s *= 2
    return b, g


LANES = 128
SUBLANES = 8


def _scan_scratch(tc):
    by_lanes = pltpu.VMEM((CW // LANES, tc, LANES), F32)
    return [by_lanes, by_lanes, pltpu.VMEM((tc // SUBLANES, CW), F32), pltpu.VMEM((tc, CW), F32)]


def _scan_tile(a, u, edge, la_ref, lh_ref, c_ref, dst_ref, reverse):
    n, w = a.shape
    groups = n // SUBLANES
    a3 = a.reshape(groups, SUBLANES, w)
    u3 = u.reshape(groups, SUBLANES, w)
    row = lax.broadcasted_iota(jnp.int32, a3.shape, 1)
    for s in (1, 2, 4):
        m = (row < SUBLANES - s) if reverse else (row >= s)
        shift = SUBLANES - s if reverse else s
        u3 = u3 + a3 * jnp.where(m, pltpu.roll(u3, shift, 1), 0.0)
        a3 = a3 * jnp.where(m, pltpu.roll(a3, shift, 1), 1.0)
    al = a3.reshape(n, w)
    hl = u3.reshape(n, w)
    blocks = w // LANES
    for q in range(blocks):
        la_ref[q] = al[:, q * LANES:(q + 1) * LANES]
        lh_ref[q] = hl[:, q * LANES:(q + 1) * LANES]
    ends = pl.ds(0 if reverse else SUBLANES - 1, groups, stride=SUBLANES)
    end_a = jnp.concatenate([la_ref.at[q][ends, :] for q in range(blocks)], axis=-1)
    end_h = jnp.concatenate([lh_ref.at[q][ends, :] for q in range(blocks)], axis=-1)
    prod, part = (_scan_bwd if reverse else _scan_fwd)(end_a, end_h)
    total = part + prod * edge
    g_row = lax.broadcasted_iota(jnp.int32, total.shape, 0)
    if reverse:
        c_ref[...] = jnp.where(g_row == groups - 1, edge, pltpu.roll(total, groups - 1, 0))
    else:
        c_ref[...] = jnp.where(g_row == 0, edge, pltpu.roll(total, 1, 0))
    for g in range(groups):
        rows = slice(g * SUBLANES, (g + 1) * SUBLANES)
        for q in range(blocks):
            cols = slice(q * LANES, (q + 1) * LANES)
            dst_ref[rows, cols] = lh_ref[q, rows, :] + la_ref[q, rows, :] * c_ref[g:g + 1, cols]


def _softplus_neg(lam):
    z = -lam
    return jnp.maximum(z, 0.0) + jnp.log1p(jnp.exp(-jnp.abs(z)))


def _lru_gates(xc, wx_ref, wa_ref, bx_ref, ba_ref, lam_ref):
    xcb = _c(xc)
    i_t = _sigmoid(_dot(xcb, wx_ref[0]) + bx_ref[...])
    r_t = _sigmoid(_dot(xcb, wa_ref[0]) + ba_ref[...])
    sp = _softplus_neg(lam_ref[...])
    log_a = (-LRU_C) * r_t * sp
    a = jnp.exp(log_a)
    mult = jnp.sqrt(1.0 - a * a)
    return xcb, i_t, r_t, sp, a, mult


def _conv_from_ext(ext_ref, xa, cw_ref, cb_ref, tc):
    return (cb_ref[...] + cw_ref[3:4, :] * xa + cw_ref[2:3, :] * ext_ref[7:7 + tc, :]
            + cw_ref[1:2, :] * ext_ref[6:6 + tc, :] + cw_ref[0:1, :] * ext_ref[5:5 + tc, :])


def _lru_fwd(proj, conv_w, conv_b, wx_bd, wa_bd, bx, ba, lam, B, S):
    T = B * S
    tc = min(SCAN_TILE, S)
    nt = S // tc
    h8 = tc // 8

    def body(xa_ref, halo_ref, ga_ref, cw_ref, cb_ref, wx_ref, wa_ref, bx_ref, ba_ref, lam_ref,
             h_ref, ya_ref, ext_ref, carry_ref, la_ref, lh_ref, c_ref):
        t = pl.program_id(2)

        @pl.when(t == 0)
        def _():
            carry_ref[...] = jnp.zeros_like(carry_ref)

        xa = xa_ref[...]
        ext_ref[0:8, :] = jnp.where(t == 0, 0.0, halo_ref[...])
        ext_ref[8:8 + tc, :] = xa
        xc = _conv_from_ext(ext_ref, xa, cw_ref, cb_ref, tc)
        _, i_t, _, _, a, mult = _lru_gates(xc, wx_ref, wa_ref, bx_ref, ba_ref, lam_ref)
        u = mult * (i_t * xc)
        _scan_tile(a, u, carry_ref[7:8, :], la_ref, lh_ref, c_ref, h_ref, False)
        h = h_ref[...]
        carry_ref[...] = h[tc - 8:tc, :]
        ga = ga_ref[...]
        ya_ref[...] = (ga * _sigmoid(ga) * h).astype(ya_ref.dtype)

    row = lambda b, t: b * nt + t
    vec = pl.BlockSpec((1, CW), lambda b, c, t: (0, c))
    mat = pl.BlockSpec((1, CW, CW), lambda b, c, t: (c, 0, 0))
    return pl.pallas_call(
        body,
        name="lru_fwd",
        grid=(B, N_CT, nt),
        in_specs=[
            pl.BlockSpec((tc, CW), lambda b, c, t: (row(b, t), c)),
            pl.BlockSpec((8, CW), lambda b, c, t: (jnp.maximum(row(b, t) * h8 - 1, 0), c)),
            pl.BlockSpec((tc, CW), lambda b, c, t: (row(b, t), N_CT + c)),
            pl.BlockSpec((CONV, CW), lambda b, c, t: (0, c)),
            vec, mat, mat, vec, vec, vec,
        ],
        out_specs=[
            pl.BlockSpec((tc, CW), lambda b, c, t: (row(b, t), c)),
            pl.BlockSpec((tc, CW), lambda b, c, t: (row(b, t), c)),
        ],
        out_shape=[
            jax.ShapeDtypeStruct((T, D_MODEL), F32),
            jax.ShapeDtypeStruct((T, D_MODEL), _MXU),
        ],
        scratch_shapes=[pltpu.VMEM((tc + 8, CW), F32), pltpu.VMEM((8, CW), F32)] + _scan_scratch(tc)[:3],
        compiler_params=_params(("parallel", "parallel", "arbitrary")),
    )(proj, proj, proj, conv_w, conv_b, wx_bd, wa_bd, bx, ba, lam)


def _lru_bwd(dya, proj, hlru, conv_w, conv_b, wx_bd, wa_bd, bx, ba, lam, B, S, deps=()):
    T = B * S
    tc = min(SCAN_TILE, S)
    nt = S // tc
    h8 = tc // 8

    def body(dya_ref, xa_ref, xhalo_ref, ga_ref, h_ref, hhalo_ref, cw_ref, cb_ref, wx_ref, wa_ref, bx_ref, ba_ref,
             lam_ref, dxa_ref, dga_ref, dcw_ref, dcb_ref, dwx_ref, dwa_ref, dbx_ref, dba_ref, dlam_ref,
             ext_ref, ext2_ref, carry_ref, dhalo_ref, la_ref, lh_ref, c_ref, dh_ref):
        b = pl.program_id(1)
        t = pl.program_id(2)
        tt = nt - 1 - t

        @pl.when(t == 0)
        def _():
            carry_ref[...] = jnp.zeros_like(carry_ref)
            dhalo_ref[...] = jnp.zeros_like(dhalo_ref)

        @pl.when((t == 0) & (b == 0))
        def _():
            for r in (dcw_ref, dcb_ref, dwx_ref, dwa_ref, dbx_ref, dba_ref, dlam_ref):
                r[...] = jnp.zeros_like(r)

        xa = xa_ref[...]
        ext_ref[0:8, :] = jnp.where(tt == 0, 0.0, xhalo_ref[...])
        ext_ref[8:8 + tc, :] = xa
        xc = _conv_from_ext(ext_ref, xa, cw_ref, cb_ref, tc)
        xcb, i_t, r_t, sp, a, mult = _lru_gates(xc, wx_ref, wa_ref, bx_ref, ba_ref, lam_ref)

        h = h_ref[...]
        ga = ga_ref[...]
        dya_t = dya_ref[...]
        sg = _sigmoid(ga)
        dga_ref[...] = (dya_t * h * (sg * (1.0 + ga * (1.0 - sg)))).astype(dga_ref.dtype)
        dlru = dya_t * (ga * sg)

        row = lax.broadcasted_iota(jnp.int32, a.shape, 0)
        coef = jnp.where(row == tc - 1, 1.0, pltpu.roll(a, tc - 1, 0))
        _scan_tile(coef, dlru, carry_ref[0:1, :], la_ref, lh_ref, c_ref, dh_ref, True)
        dh = dh_ref[...]
        ext2_ref[0:tc, :] = a * dh
        carry_ref[...] = ext2_ref[0:8, :]

        ext2_ref[0:8, :] = jnp.where(tt == 0, 0.0, hhalo_ref[...])
        ext2_ref[8:8 + tc, :] = h
        hprev = ext2_ref[7:7 + tc, :]

        da = dh * hprev
        ix = i_t * xc
        dmult = dh * ix
        di = dh * mult * xc
        dxc = dh * mult * i_t
        dlog_a = da * a - dmult * (a * a) / mult
        dr = dlog_a * ((-LRU_C) * sp)
        dlam_ref[...] += jnp.sum(dlog_a * r_t, axis=0, keepdims=True) * (LRU_C * _sigmoid(-lam_ref[...]))
        dza = dr * r_t * (1.0 - r_t)
        dzx = di * i_t * (1.0 - i_t)
        dzab = _c(dza)
        dzxb = _c(dzx)
        dxc = dxc + _dot_nt(dzxb, wx_ref[0]) + _dot_nt(dzab, wa_ref[0])
        dwx_ref[0] += _dot_tn(xcb, dzxb)
        dwa_ref[0] += _dot_tn(xcb, dzab)
        dbx_ref[...] += jnp.sum(dzx, axis=0, keepdims=True)
        dba_ref[...] += jnp.sum(dza, axis=0, keepdims=True)

        dcb_ref[...] += jnp.sum(dxc, axis=0, keepdims=True)
        dcw_ref[3:4, :] += jnp.sum(dxc * xa, axis=0, keepdims=True)
        dcw_ref[2:3, :] += jnp.sum(dxc * ext_ref[7:7 + tc, :], axis=0, keepdims=True)
        dcw_ref[1:2, :] += jnp.sum(dxc * ext_ref[6:6 + tc, :], axis=0, keepdims=True)
        dcw_ref[0:1, :] += jnp.sum(dxc * ext_ref[5:5 + tc, :], axis=0, keepdims=True)
        ext2_ref[0:tc, :] = dxc
        ext2_ref[tc:tc + 8, :] = dhalo_ref[...]
        dxa = (cw_ref[3:4, :] * dxc + cw_ref[2:3, :] * ext2_ref[1:1 + tc, :]
               + cw_ref[1:2, :] * ext2_ref[2:2 + tc, :] + cw_ref[0:1, :] * ext2_ref[3:3 + tc, :])
        dxa_ref[...] = dxa.astype(dxa_ref.dtype)
        dhalo_ref[...] = ext2_ref[0:8, :]

    row_of = lambda b, t: b * nt + (nt - 1 - t)
    tile = lambda off: pl.BlockSpec((tc, CW), lambda c, b, t: (row_of(b, t), off + c))
    halo = pl.BlockSpec((8, CW), lambda c, b, t: (jnp.maximum(row_of(b, t) * h8 - 1, 0), c))
    vec = pl.BlockSpec((1, CW), lambda c, b, t: (0, c))
    mat = pl.BlockSpec((1, CW, CW), lambda c, b, t: (c, 0, 0))
    cwspec = pl.BlockSpec((CONV, CW), lambda c, b, t: (0, c))
    return pl.pallas_call(
        _after(body, 13, deps),
        name="lru_bwd",
        grid=(N_CT, B, nt),
        in_specs=[tile(0), tile(0), halo, tile(N_CT), tile(0), halo, cwspec, vec, mat, mat, vec, vec, vec]
        + [ANY_SPEC] * len(deps),
        out_specs=[tile(0), tile(0), cwspec, vec, mat, mat, vec, vec, vec],
        out_shape=[
            jax.ShapeDtypeStruct((T, D_MODEL), _MXU),
            jax.ShapeDtypeStruct((T, D_MODEL), _MXU),
            jax.ShapeDtypeStruct((CONV, D_MODEL), F32),
            jax.ShapeDtypeStruct((1, D_MODEL), F32),
            jax.ShapeDtypeStruct((N_CT, CW, CW), F32),
            jax.ShapeDtypeStruct((N_CT, CW, CW), F32),
            jax.ShapeDtypeStruct((1, D_MODEL), F32),
            jax.ShapeDtypeStruct((1, D_MODEL), F32),
            jax.ShapeDtypeStruct((1, D_MODEL), F32),
        ],
        scratch_shapes=[pltpu.VMEM((tc + 8, CW), F32), pltpu.VMEM((tc + 8, CW), F32),
                        pltpu.VMEM((8, CW), F32), pltpu.VMEM((8, CW), F32)] + _scan_scratch(tc),
        compiler_params=_params(("parallel", "arbitrary", "arbitrary")),
    )(dya, proj, proj, proj, hlru, hlru, conv_w, conv_b, wx_bd, wa_bd, bx, ba, lam, *deps)


def _retention_tables(S):
    half = DK // 2
    freqs = ROPE_THETA ** (-jnp.arange(half, dtype=F32) / half)
    ang = jnp.arange(S, dtype=F32)[:, None] * freqs[None, :]
    log_g = jnp.log1p(-(2.0 ** (-5.0 - jnp.arange(HEADS, dtype=F32))))
    idx = jnp.arange(CHUNK, dtype=F32)
    diff = idx[:, None] - idx[None, :]
    inner = jnp.where(diff >= 0, jnp.exp(jnp.maximum(diff, 0.0)[None] * log_g[:, None, None]), 0.0)
    cross = jnp.exp((idx[None, :] + 1.0) * log_g[:, None])[:, :, None]
    state = jnp.exp((CHUNK - 1.0 - idx[None, :]) * log_g[:, None])[:, :, None]
    gam = jnp.broadcast_to(jnp.exp(CHUNK * log_g)[:, None, None], (HEADS, 1, DK))
    return jnp.cos(ang), jnp.sin(ang), inner, cross, state, gam


def _rot(x, cos, sin):
    half = DK // 2
    x1, x2 = x[:, :half], x[:, half:]
    return jnp.concatenate([x1 * cos - x2 * sin, x1 * sin + x2 * cos], axis=-1)


def _rot_t(y, cos, sin):
    half = DK // 2
    y1, y2 = y[:, :half], y[:, half:]
    return jnp.concatenate([y1 * cos + y2 * sin, y2 * cos - y1 * sin], axis=-1)


def _groupnorm(o):
    mu = jnp.mean(o, axis=-1, keepdims=True)
    oc = o - mu
    rs = lax.rsqrt(jnp.mean(oc * oc, axis=-1, keepdims=True) + EPS)
    return oc * rs, rs


def _ret_specs(B, chunk_of):
    qkv = lambda g: pl.BlockSpec((B, CHUNK, D_MODEL), lambda c: (0, chunk_of(c), g))
    act = pl.BlockSpec((B, CHUNK, D_MODEL), lambda c: (0, chunk_of(c), 0))
    rope = pl.BlockSpec((CHUNK, DK // 2), lambda c: (chunk_of(c), 0))
    dmat = pl.BlockSpec((HEADS, CHUNK, CHUNK), lambda c: (0, 0, 0))
    dvec = pl.BlockSpec((HEADS, CHUNK, 1), lambda c: (0, 0, 0))
    hrow = pl.BlockSpec((HEADS, 1, DK), lambda c: (0, 0, 0))
    rst = pl.BlockSpec((1, B, HEADS, DK, DK), lambda c: (chunk_of(c), 0, 0, 0, 0))
    return qkv, act, rope, dmat, dvec, hrow, rst


def _ret_fwd(proj, tables, gain3, B, S):
    T = B * S
    nc = S // CHUNK
    cos, sin, dmat_t, cd_t, sd_t, gam_t = tables

    def body(q_ref, k_ref, v_ref, gb_ref, cos_ref, sin_ref, dm_ref, cd_ref, sd_ref, gam_ref, gain_ref,
             o_ref, yb_ref, rs_ref, state_ref):
        @pl.when(pl.program_id(0) == 0)
        def _():
            state_ref[...] = jnp.zeros_like(state_ref)

        cos_t, sin_t = cos_ref[...], sin_ref[...]
        for b, h in [(b, h) for b in range(B) for h in range(HEADS)]:
            cols = slice(h * DK, (h + 1) * DK)
            qb = _c(_rot(q_ref[b, :, cols], cos_t, sin_t))
            kb = _c(_rot(k_ref[b, :, cols], cos_t, sin_t) * (DK ** -0.5))
            v = v_ref[b, :, cols]
            state = state_ref[b, h]
            sb = _c(state)
            rs_ref[0, b, h] = sb
            scores = _dot_nt(qb, kb) * dm_ref[h]
            o = _dot(_c(scores), _c(v)) + _dot(qb, sb) * cd_ref[h]
            state_ref[b, h] = gam_ref[h] * state + _dot_tn(kb, _c(v * sd_ref[h]))
            o_ref[b, :, cols] = o
            n, _ = _groupnorm(o)
            gb = gb_ref[b, :, cols]
            yb_ref[b, :, cols] = (gb * _sigmoid(gb) * (n * gain_ref[h])).astype(yb_ref.dtype)

    qkv, act, rope, dmat, dvec, hrow, rst = _ret_specs(B, lambda c: c)
    proj3 = proj.reshape(B, S, proj.shape[1])
    o_pre, yb, states = pl.pallas_call(
        body,
        name="ret_fwd",
        grid=(nc,),
        in_specs=[qkv(2), qkv(3), qkv(4), qkv(5), rope, rope, dmat, dvec, dvec, hrow, hrow],
        out_specs=[act, act, rst],
        out_shape=[
            jax.ShapeDtypeStruct((B, S, D_MODEL), F32),
            jax.ShapeDtypeStruct((B, S, D_MODEL), _MXU),
            jax.ShapeDtypeStruct((nc, B, HEADS, DK, DK), _MXU),
        ],
        scratch_shapes=[pltpu.VMEM((B, HEADS, DK, DK), F32)],
        compiler_params=_params(("arbitrary",)),
    )(proj3, proj3, proj3, proj3, cos, sin, dmat_t, cd_t, sd_t, gam_t, gain3)
    return o_pre.reshape(T, D_MODEL), yb.reshape(T, D_MODEL), states


def _ret_bwd(dyb, o_pre, proj, states, tables, gain3, B, S, deps=()):
    T = B * S
    nc = S // CHUNK
    cos, sin, dmat_t, cd_t, sd_t, gam_t = tables

    def body(dyb_ref, o_ref, q_ref, k_ref, v_ref, gb_ref, rs_ref, cos_ref, sin_ref, dm_ref, cd_ref, sd_ref, gam_ref,
             gain_ref, dr_ref, dgain_ref, dstate_ref):
        @pl.when(pl.program_id(0) == 0)
        def _():
            dstate_ref[...] = jnp.zeros_like(dstate_ref)
            dgain_ref[...] = jnp.zeros_like(dgain_ref)

        cos_t, sin_t = cos_ref[...], sin_ref[...]
        for b, h in [(b, h) for b in range(B) for h in range(HEADS)]:
            cols = slice(h * DK, (h + 1) * DK)
            gain = gain_ref[h]
            n, rs = _groupnorm(o_ref[b, :, cols])
            gb = gb_ref[b, :, cols]
            sg = _sigmoid(gb)
            dy = dyb_ref[b, :, cols]
            part = lambda g: slice(g * D_MODEL + h * DK, g * D_MODEL + (h + 1) * DK)
            dr_ref[b, :, part(3)] = (dy * (n * gain) * (sg * (1.0 + gb * (1.0 - sg)))).astype(dr_ref.dtype)
            dgn = dy * (gb * sg)
            dgain_ref[h] += jnp.sum(dgn * n, axis=0, keepdims=True)
            dn = dgn * gain
            do = rs * (dn - jnp.mean(dn, axis=-1, keepdims=True) - n * jnp.mean(dn * n, axis=-1, keepdims=True))

            qb = _c(_rot(q_ref[b, :, cols], cos_t, sin_t))
            kb = _c(_rot(k_ref[b, :, cols], cos_t, sin_t) * (DK ** -0.5))
            v = v_ref[b, :, cols]
            vb = _c(v)
            vsb = _c(v * sd_ref[h])
            dob = _c(do)
            docb = _c(do * cd_ref[h])
            dmat = dm_ref[h]
            dstate = dstate_ref[b, h]
            dsb = _c(dstate)
            pb = _c(_dot_nt(qb, kb) * dmat)
            dsc = _c(_dot_nt(dob, vb) * dmat)
            dq = _dot(dsc, kb) + _dot_nt(docb, rs_ref[0, b, h])
            dk = _dot_tn(dsc, qb) + _dot_nt(vsb, dsb)
            dv = _dot_tn(pb, dob) + _dot(kb, dsb) * sd_ref[h]
            dstate_ref[b, h] = gam_ref[h] * dstate + _dot_tn(qb, docb)
            dr_ref[b, :, part(0)] = _rot_t(dq, cos_t, sin_t).astype(dr_ref.dtype)
            dr_ref[b, :, part(1)] = (_rot_t(dk, cos_t, sin_t) * (DK ** -0.5)).astype(dr_ref.dtype)
            dr_ref[b, :, part(2)] = dv.astype(dr_ref.dtype)

    qkv, act, rope, dmat, dvec, hrow, rst = _ret_specs(B, lambda c: nc - 1 - c)
    wide = pl.BlockSpec((B, CHUNK, 4 * D_MODEL), lambda c: (0, nc - 1 - c, 0))
    proj3 = proj.reshape(B, S, proj.shape[1])
    dr, dgain = pl.pallas_call(
        _after(body, 14, deps),
        name="ret_bwd",
        grid=(nc,),
        in_specs=[act, act, qkv(2), qkv(3), qkv(4), qkv(5), rst, rope, rope, dmat, dvec, dvec, hrow, hrow]
        + [ANY_SPEC] * len(deps),
        out_specs=[wide, hrow],
        out_shape=[jax.ShapeDtypeStruct((B, S, 4 * D_MODEL), _MXU), jax.ShapeDtypeStruct((HEADS, 1, DK), F32)],
        scratch_shapes=[pltpu.VMEM((B, HEADS, DK, DK), F32)],
        compiler_params=_params(("arbitrary",)),
    )(dyb.reshape(B, S, D_MODEL), o_pre.reshape(B, S, D_MODEL), proj3, proj3, proj3, proj3, states, cos, sin, dmat_t,
      cd_t, sd_t, gam_t, gain3, *deps)
    return dr.reshape(T, 4 * D_MODEL), dgain


def _mid(ya, yb, proj, x2d, tgt2d, wpa, wpb, wout, g_fin):
    T = x2d.shape[0]
    tm = min(MID_TILE, T)
    n_steps = T // tm
    rows = D_MODEL // (2 * N_CHIPS)

    def body(ya_ref, yb_ref, ma_ref, mb_ref, x_ref, t_ref, gf_ref, wpa_hbm, wpb_hbm, wout_hbm,
             loss_ref, dx2_ref, dya_ref, dyb_ref, dm_ref, dgf_ref, gw_hbm, w_ref, acc_ref, sem):
        i = pl.program_id(0)

        @pl.when(i == 0)
        def _():
            loads = [pltpu.make_async_copy(src, w_ref.at[k], sem.at[k]) for k, src in enumerate((wpa_hbm, wpb_hbm, wout_hbm))]
            for cp in loads:
                cp.start()
            for cp in loads:
                cp.wait()
            acc_ref[...] = jnp.zeros_like(acc_ref)
            loss_ref[...] = jnp.zeros_like(loss_ref)
            dgf_ref[...] = jnp.zeros_like(dgf_ref)

        ya_t, yb_t = ya_ref[...], yb_ref[...]
        out_a = _dot(ya_t, w_ref[0])
        out_b = _dot(yb_t, w_ref[1])
        sa = _sigmoid(ma_ref[...])
        sb = _sigmoid(mb_ref[...])
        mgb = _c(sa * out_a + sb * out_b)
        x2 = x_ref[...] + _dot(mgb, w_ref[2])
        r2 = lax.rsqrt(jnp.mean(x2 * x2, axis=-1, keepdims=True) + EPS)
        nx = x2 * r2
        gf = gf_ref[...]
        err = nx * gf - t_ref[...]
        loss_ref[...] += 0.5 * jnp.sum(jnp.mean(err * err, axis=-1, keepdims=True), axis=0, keepdims=True)
        dy = err * (1.0 / D_MODEL)
        dgf_ref[...] += jnp.sum(dy * nx, axis=0, keepdims=True)
        dyg = dy * gf
        dx2 = r2 * (dyg - nx * jnp.mean(dyg * nx, axis=-1, keepdims=True))
        dx2_ref[...] = dx2
        dx2b = _c(dx2)
        dmg = _dot_nt(dx2b, w_ref[2])
        acc_ref[2] += _dot_tn(mgb, dx2b)
        dm_ref[:, :D_MODEL] = (dmg * out_a * sa * (1.0 - sa)).astype(dm_ref.dtype)
        dm_ref[:, D_MODEL:] = (dmg * out_b * sb * (1.0 - sb)).astype(dm_ref.dtype)
        dab = _c(dmg * sa)
        dbb = _c(dmg * sb)
        dya_ref[...] = _dot_nt(dab, w_ref[0])
        dyb_ref[...] = _dot_nt(dbb, w_ref[1])
        acc_ref[0] += _dot_tn(ya_t, dab)
        acc_ref[1] += _dot_tn(yb_t, dbb)

        @pl.when(i == n_steps - 1)
        def _():
            copies = [pltpu.make_async_copy(acc_ref.at[k, pl.ds((2 * p + hf) * rows, rows), :], gw_hbm.at[p, hf, k],
                                            sem.at[(k * N_CHIPS + p) * 2 + hf])
                      for k in range(3) for p in range(N_CHIPS) for hf in range(2)]
            for cp in copies:
                cp.start()
            for cp in copies:
                cp.wait()

    tile = lambda j: pl.BlockSpec((tm, D_MODEL), lambda i: (i, j))
    one = pl.BlockSpec((1, D_MODEL), lambda i: (0, 0))
    anyspec = pl.BlockSpec(memory_space=pl.ANY)
    return pl.pallas_call(
        body,
        name="mid",
        grid=(n_steps,),
        in_specs=[tile(0), tile(0), tile(6), tile(7), tile(0), tile(0), one, anyspec, anyspec, anyspec],
        out_specs=[pl.BlockSpec((1, 1), lambda i: (0, 0)), tile(0), tile(0), tile(0),
                   pl.BlockSpec((tm, 2 * D_MODEL), lambda i: (i, 0)), one, anyspec],
        out_shape=[
            jax.ShapeDtypeStruct((1, 1), F32),
            jax.ShapeDtypeStruct((T, D_MODEL), F32),
            jax.ShapeDtypeStruct((T, D_MODEL), F32),
            jax.ShapeDtypeStruct((T, D_MODEL), F32),
            jax.ShapeDtypeStruct((T, 2 * D_MODEL), _MXU),
            jax.ShapeDtypeStruct((1, D_MODEL), F32),
            jax.ShapeDtypeStruct((N_CHIPS, 2, 3, rows, D_MODEL), F32),
        ],
        scratch_shapes=[pltpu.VMEM((3, D_MODEL, D_MODEL), _MXU), pltpu.VMEM((3, D_MODEL, D_MODEL), F32),
                        pltpu.SemaphoreType.DMA((3 * N_CHIPS * 2,))],
        compiler_params=_params(("arbitrary",)),
    )(ya, yb, proj, proj, x2d, tgt2d, g_fin, wpa, wpb, wout)


def _inproj_bwd_dx(dparts, w_all, x2d, dx2, g_in, first, count, prev, name, deps=()):
    T = x2d.shape[0]
    tm = min(DX_TILE, T)
    n_d = len(dparts)
    groups = [(a, k) for a, d in enumerate(dparts) for k in range(d.shape[1] // D_MODEL)]
    dg_start = jnp.zeros((1, D_MODEL), F32) if prev is None else prev[1]
    carried = () if prev is None else (prev[0],)

    def body(*refs):
        d_refs = refs[:n_d]
        x_ref, dx2_ref, g_ref, dg0_ref, w_hbm = refs[n_d:n_d + 5]
        dx_ref, dg_ref, w_ref, sem = refs[-4:]

        @pl.when(pl.program_id(0) == 0)
        def _():
            cp = pltpu.make_async_copy(w_hbm, w_ref, sem)
            cp.start()
            cp.wait()
            dg_ref[...] = dg0_ref[...]

        dh = jnp.zeros((tm, D_MODEL), F32)
        for j, (a, k) in enumerate(groups):
            dh = dh + _dot_nt(d_refs[a][:, k * D_MODEL:(k + 1) * D_MODEL],
                              w_ref[j // 2, :, (j % 2) * D_MODEL:(j % 2 + 1) * D_MODEL])
        x = x_ref[...]
        r = lax.rsqrt(jnp.mean(x * x, axis=-1, keepdims=True) + EPS)
        nx = x * r
        dg_ref[...] += jnp.sum(dh * nx, axis=0, keepdims=True)
        dhg = dh * g_ref[...]
        dx_ref[...] = dx2_ref[...] + r * (dhg - nx * jnp.mean(dhg * nx, axis=-1, keepdims=True))

    tile = pl.BlockSpec((tm, D_MODEL), lambda i: (first + i, 0))
    one = pl.BlockSpec((1, D_MODEL), lambda i: (0, 0))
    return pl.pallas_call(
        body,
        name=name,
        grid=(count,),
        in_specs=[pl.BlockSpec((tm, d.shape[1]), lambda i: (first + i, 0)) for d in dparts]
        + [tile, tile, one, one, ANY_SPEC] + [ANY_SPEC] * (len(carried) + len(deps)),
        out_specs=[tile, one],
        out_shape=[jax.ShapeDtypeStruct((T, D_MODEL), F32), jax.ShapeDtypeStruct((1, D_MODEL), F32)],
        input_output_aliases={n_d + 5: 0} if carried else {},
        scratch_shapes=[pltpu.VMEM(w_all.shape, w_all.dtype), pltpu.SemaphoreType.DMA],
        compiler_params=_params(("arbitrary",)),
    )(*dparts, x2d, dx2, g_in, dg_start, w_all, *carried, *deps)


def _inproj_bwd_dw(ht, dparts, name, deps=()):
    T = ht.shape[1]
    tn = DW_COLS
    half = D_MODEL // 2
    per_chip = 2 * D_MODEL // tn
    n_d = len(dparts)
    tiles = [(a, t) for a, d in enumerate(dparts) for t in range(d.shape[1] // tn)]
    offs = [sum(d.shape[1] // tn for d in dparts[:a]) for a in range(n_d)]

    def body(*refs):
        ht_ref = refs[0]
        d_refs = refs[1:1 + n_d]
        out_ref = refs[-1]
        t = pl.program_id(0)

        for a in range(n_d):
            lo, hi = offs[a], offs[a] + dparts[a].shape[1] // tn

            @pl.when((t >= lo) & (t < hi))
            def _(a=a):
                g = _dot(ht_ref[...], d_refs[a][...])
                out_ref[0, 0] = g[:half]
                out_ref[0, 1] = g[half:]

    def dspec(a):
        n_a = dparts[a].shape[1] // tn
        return pl.BlockSpec((T, tn), lambda t: (0, jnp.clip(t - offs[a], 0, n_a - 1)))

    return pl.pallas_call(
        body,
        name=name,
        grid=(len(tiles),),
        in_specs=[pl.BlockSpec((D_MODEL, T), lambda t: (0, 0))] + [dspec(a) for a in range(n_d)]
        + [ANY_SPEC] * len(deps),
        out_specs=pl.BlockSpec((1, 2, half, tn), lambda t: (t // per_chip, 0, 0, t % per_chip)),
        out_shape=jax.ShapeDtypeStruct((len(tiles) // per_chip, 2, half, 2 * D_MODEL), F32),
        compiler_params=_params(("parallel",)),
    )(ht, *dparts, *deps)


def _coords():
    return lax.axis_index("x"), lax.axis_index("y"), lax.axis_index("c")


def _other_chips(x, y):
    return [(1 - x, y), (x, 1 - y), (1 - x, 1 - y)]


def _chunks(rows, n):
    size = rows // n
    return [pl.ds(q * size, size) for q in range(n)]


HBM_SPEC = pl.BlockSpec(memory_space=pltpu.HBM)
SEM_SPEC = pl.BlockSpec(memory_space=pltpu.SEMAPHORE)
DATAFLOW = pltpu.SideEffectType.DATAFLOW_SIDE_EFFECTING


def _copies_start(bufs, plan, n_copies, name, deps=()):
    n = len(bufs)
    n_deps = len(deps)

    def body(*refs):
        ins = refs[:n]
        send_sems, recv_sems = refs[n + n_deps], refs[n + n_deps + 1]
        token = refs[-1]
        for k, send, _ in plan(ins):
            if send is not None:
                src, dst, dev, pred = send
                cp = pltpu.make_async_remote_copy(src_ref=src, dst_ref=dst, send_sem=send_sems.at[k],
                                                  recv_sem=recv_sems.at[k], device_id=dev, device_id_type=MESH)
                if pred is None:
                    cp.start()
                else:
                    pl.when(pred)(cp.start)
        token[...] = jnp.zeros_like(token)

    hbm = [pltpu.with_memory_space_constraint(b, pltpu.HBM) for b in bufs]
    outs = pl.pallas_call(
        body,
        name=name,
        in_specs=[HBM_SPEC] * n + [ANY_SPEC] * n_deps,
        out_specs=(SEM_SPEC, SEM_SPEC, *([HBM_SPEC] * n), pl.BlockSpec(memory_space=pltpu.VMEM)),
        out_shape=(pltpu.SemaphoreType.DMA((n_copies,)), pltpu.SemaphoreType.DMA((n_copies,)),
                   *[pltpu.HBM(b.shape, b.dtype) for b in bufs], jax.ShapeDtypeStruct((8, 128), F32)),
        input_output_aliases={a: 2 + a for a in range(n)},
        compiler_params=pltpu.CompilerParams(has_side_effects=DATAFLOW),
    )(*hbm, *deps)
    return outs[0], outs[1], list(outs[2:2 + n]), outs[-1]


def _copies_wait(send_sems, recv_sems, bufs, after, plan, name, only=None):
    n = len(bufs)

    def body(*refs):
        ins = refs[:n]
        s_sems, r_sems = refs[n], refs[n + 1]
        for k, send, recv in plan(ins):
            if only is not None and k not in only:
                continue
            if send is not None:
                src, dst, dev, pred = send
                cp = pltpu.make_async_remote_copy(src_ref=src, dst_ref=dst, send_sem=s_sems.at[k],
                                                  recv_sem=r_sems.at[k], device_id=dev, device_id_type=MESH)
                if pred is None:
                    cp.wait_send()
                else:
                    pl.when(pred)(cp.wait_send)
            if recv is not None:
                dst, pred = recv
                cp = pltpu.make_async_remote_copy(src_ref=dst, dst_ref=dst, send_sem=s_sems.at[k],
                                                  recv_sem=r_sems.at[k], device_id=_coords(), device_id_type=MESH)
                if pred is None:
                    cp.wait_recv()
                else:
                    pl.when(pred)(cp.wait_recv)

    outs = pl.pallas_call(
        body,
        name=name,
        in_specs=[HBM_SPEC] * n + [SEM_SPEC, SEM_SPEC, pl.BlockSpec(memory_space=pl.ANY)],
        out_specs=[HBM_SPEC] * n,
        out_shape=[pltpu.HBM(b.shape, b.dtype) for b in bufs],
        input_output_aliases={a: a for a in range(n)},
        compiler_params=pltpu.CompilerParams(has_side_effects=DATAFLOW),
    )(*bufs, send_sems, recv_sems, after)
    return list(outs)


def _gather_plan(n_bufs):
    def plan(refs):
        x, y, c = _coords()
        me = 2 * x + y
        out = []
        for k, (px, py) in enumerate(_other_chips(x, y)):
            for a in range(n_bufs):
                out.append((k * n_bufs + a, (refs[a].at[me], refs[a].at[me], (px, py, c), None),
                            (refs[a].at[2 * px + py], None)))
        return out
    return plan


def _cast_into_slot(ws, name):
    n = len(ws)
    nt = 2

    def body(s_ref, *refs):
        for a in range(n):
            refs[n + a][0] = refs[a][...].astype(refs[n + a].dtype)

    xi, yi, _ = _coords()
    return pl.pallas_call(
        body,
        name=name,
        grid_spec=pltpu.PrefetchScalarGridSpec(
            num_scalar_prefetch=1,
            grid=(2, nt),
            in_specs=[pl.BlockSpec((1, w.shape[1] // nt, w.shape[2]), lambda hf, i, s: (hf, i, 0)) for w in ws],
            out_specs=[pl.BlockSpec((1, 1, w.shape[1] // nt, w.shape[2]), lambda hf, i, s: (s[0], hf, i, 0)) for w in ws],
        ),
        out_shape=[jax.ShapeDtypeStruct((N_CHIPS,) + w.shape, _MXU) for w in ws],
        compiler_params=_params(("parallel", "parallel")),
    )((2 * xi + yi).reshape(1).astype(jnp.int32), *ws)


def _chip_gather_plan(stage, n_bufs):
    def plan(refs):
        x, y, c = _coords()
        me = 2 * x + y
        near = [(1 - x, y), (x, 1 - y)]
        slots = [2 * (1 - x) + y, 2 * x + (1 - y), 2 * (1 - x) + (1 - y)]
        sibling = (x, y, 1 - c)
        pass_to = (jnp.where(c == 0, x, 1 - x), jnp.where(c == 0, 1 - y, y), c)
        pass_slot = jnp.where(c == 0, slots[0], slots[1])
        out = []

        def move(src_slot, to, land_slot, land_core, pieces):
            for a, buf in enumerate(refs):
                for rows in _chunks(buf.shape[2], pieces[a]):
                    out.append((len(out), (buf.at[src_slot, c, rows], buf.at[src_slot, c, rows], to, None),
                                (buf.at[land_slot, land_core, rows], None)))

        if stage == "near":
            for k, chip in enumerate(near):
                move(me, (*chip, c), slots[k], c, NEAR_PIECES[:n_bufs])
        elif stage == "pass":
            move(pass_slot, pass_to, slots[2], c, PASS_PIECES[:n_bufs])
            for k in range(2):
                move(slots[k], sibling, slots[k], 1 - c, [1] * n_bufs)
        else:
            move(slots[2], sibling, slots[2], 1 - c, [1] * n_bufs)
        return out
    return plan


NEAR_PIECES = (2, 1)
PASS_PIECES = (2, 1)


def _chip_gather_copies(stage, n_bufs):
    if stage == "near":
        return 2 * sum(NEAR_PIECES[:n_bufs]), None
    if stage == "pass":
        n_pass = sum(PASS_PIECES[:n_bufs])
        return n_pass + 2 * n_bufs, set(range(n_pass))
    return n_bufs, None


def _swap_plan(n_slabs):
    def plan(refs):
        x, y, c = _coords()
        out, k = [], 0
        for i, n in enumerate(n_slabs):
            g, land = refs[2 * i], refs[2 * i + 1]
            for p in range(n):
                out.append((k, (g.at[p, 1 - c], land.at[p], (x, y, 1 - c), None), (land.at[p], None)))
                k += 1
        return out
    return plan


def _is_one_of(chip, dests):
    hit = chip == dests[0]
    for d in dests[1:]:
        hit = hit | (chip == d)
    return hit


def _slab_of(chip, dests):
    return sum(j * (chip == d).astype(jnp.int32) for j, d in enumerate(dests))


def _scatter_plan(dest_sets):
    def plan(refs):
        x, y, c = _coords()
        me = 2 * x + y
        out = []
        for k, (px, py) in enumerate(_other_chips(x, y)):
            peer = 2 * px + py
            for i, dests in enumerate(dest_sets):
                cs, land = refs[2 * i], refs[2 * i + 1]
                everyone = len(dests) == N_CHIPS
                send = (cs.at[_slab_of(peer, dests)], land.at[k], (px, py, c),
                        None if everyone else _is_one_of(peer, dests))
                recv = (land.at[k], None if everyone else _is_one_of(me, dests))
                out.append((k * len(dest_sets) + i, send, recv))
        return out
    return plan


def _join_plan(rows, n_pieces):
    def plan(refs):
        x, y, c = _coords()
        (buf,) = refs
        return [(i, (buf.at[c, piece], buf.at[c, piece], (x, y, 1 - c), None), (buf.at[1 - c, piece], None))
                for i, piece in enumerate(_chunks(rows, n_pieces))]
    return plan


def _join_plans(parts):
    def plan(refs):
        out, b0, k0 = [], 0, 0
        for part_plan, n_bufs, n_copies in parts:
            out += [(k0 + k, send, recv) for k, send, recv in part_plan(refs[b0:b0 + n_bufs])]
            b0 += n_bufs
            k0 += n_copies
        return out
    return plan


def _allgather_plan():
    def plan(refs):
        x, y, c = _coords()
        (land,) = refs
        me = 4 * x + 2 * y + c
        out = []
        for r in range(1, 8):
            px = 1 - x if r & 4 else x
            py = 1 - y if r & 2 else y
            pc = 1 - c if r & 1 else c
            out.append((r - 1, (land.at[me], land.at[me], (px, py, pc), None), (land.at[4 * px + 2 * py + pc], None)))
        return out
    return plan


def _sum_gathered(land, name):
    def body(land_ref, o_ref):
        acc = land_ref[0]
        for d in range(1, 8):
            acc = acc + land_ref[d]
        o_ref[...] = acc

    return pl.pallas_call(
        body,
        name=name,
        out_shape=jax.ShapeDtypeStruct(land.shape[1:], F32),
        compiler_params=_params(),
    )(land)


def _join_halves(bufs, n_chunks, name, deps=()):
    n = len(bufs)
    pieces = [(a, rows) for a in range(n) for rows in _chunks(bufs[a].shape[1], n_chunks[a])]
    n_p = len(pieces)

    def body(*refs):
        outs = refs[-n - 2:-2]
        send_sems, recv_sems = refs[-2:]
        x, y, c = _coords()

        def copy(i, half):
            a, rows = pieces[i]
            return pltpu.make_async_remote_copy(
                src_ref=outs[a].at[half, rows], dst_ref=outs[a].at[half, rows], send_sem=send_sems.at[i],
                recv_sem=recv_sems.at[i], device_id=(x, y, 1 - c), device_id_type=MESH)

        sends = [copy(i, c) for i in range(n_p)]
        for cp in sends:
            cp.start()
        for i in range(n_p):
            copy(i, 1 - c).wait_recv()
        for cp in sends:
            cp.wait_send()

    anyspec = pl.BlockSpec(memory_space=pl.ANY)
    sems = pltpu.SemaphoreType.DMA((n_p,))
    return pl.pallas_call(
        body,
        name=name,
        in_specs=[anyspec] * (n + len(deps)),
        out_specs=[anyspec] * n,
        out_shape=[jax.ShapeDtypeStruct(b.shape, b.dtype) for b in bufs],
        input_output_aliases={a: a for a in range(n)},
        scratch_shapes=[sems, sems],
    )(*bufs, *deps)


def _row_tile(rows, cap):
    t = cap
    while rows % t:
        t //= 2
    return t


def _add_my_half(g, r, name):
    n_slabs, _, R, C = g.shape
    tr = R if n_slabs > 1 else _row_tile(R, SUM_ROWS)

    def body(c_ref, g_ref, r_ref, o_ref):
        o_ref[...] = (g_ref[0] + r_ref[...]).astype(o_ref.dtype)

    return pl.pallas_call(
        body,
        name=name,
        grid_spec=pltpu.PrefetchScalarGridSpec(
            num_scalar_prefetch=1,
            grid=(n_slabs, R // tr),
            in_specs=[pl.BlockSpec((1, 1, tr, C), lambda p, i, c_ref: (p, c_ref[0], i, 0)),
                      pl.BlockSpec((1, tr, C), lambda p, i, c_ref: (p, i, 0))],
            out_specs=pl.BlockSpec((1, tr, C), lambda p, i, c_ref: (p, i, 0)),
        ),
        out_shape=jax.ShapeDtypeStruct(r.shape, jnp.bfloat16),
        compiler_params=_params(("parallel", "parallel")),
    )(lax.axis_index("c").reshape(1).astype(jnp.int32), g, r)


def _sum_slabs(own, got, name, deps=()):
    _, R, C = own.shape
    tr = _row_tile(R, SUM_ROWS)

    def body(s_ref, own_ref, got_ref, *rest):
        rest[-1][0] = ((own_ref[0].astype(F32) + got_ref[0].astype(F32)) + got_ref[1].astype(F32)) + got_ref[2].astype(F32)

    xi, yi, ci = _coords()
    return pl.pallas_call(
        body,
        name=name,
        grid_spec=pltpu.PrefetchScalarGridSpec(
            num_scalar_prefetch=1,
            grid=(R // tr,),
            in_specs=[pl.BlockSpec((1, tr, C), lambda i, s: (s[0], i, 0)),
                      pl.BlockSpec((3, tr, C), lambda i, s: (0, i, 0))] + [ANY_SPEC] * len(deps),
            out_specs=pl.BlockSpec((1, tr, C), lambda i, s: (s[1], i, 0)),
        ),
        out_shape=jax.ShapeDtypeStruct((2, R, C), F32),
        compiler_params=_params(("parallel",)),
    )(jnp.stack([2 * xi + yi, ci]).astype(jnp.int32), own, got, *deps)


def _sum_parts(owns, got, dest_sets, name):
    n = len(owns)
    _, R, C = owns[0].shape
    tr = _row_tile(R, SUM_ROWS)

    def body(s_ref, *refs):
        got_ref, o_ref = refs[n], refs[-1]
        total = jnp.zeros((tr, C), F32)
        for i in range(n):
            total = total + jnp.where(s_ref[2 + 2 * i] == 1, refs[i][0].astype(F32), 0.0)
        o_ref[0] = ((total + got_ref[0].astype(F32)) + got_ref[1].astype(F32)) + got_ref[2].astype(F32)

    xi, yi, ci = _coords()
    me = 2 * xi + yi
    scalars = [ci, ci]
    for dests in dest_sets:
        scalars += [_is_one_of(me, dests).astype(jnp.int32), _slab_of(me, dests)]
    own_spec = lambda i: pl.BlockSpec((1, tr, C), lambda r, s: (s[3 + 2 * i], r, 0))
    return pl.pallas_call(
        body,
        name=name,
        grid_spec=pltpu.PrefetchScalarGridSpec(
            num_scalar_prefetch=1,
            grid=(R // tr,),
            in_specs=[own_spec(i) for i in range(n)] + [pl.BlockSpec((3, tr, C), lambda r, s: (0, r, 0))],
            out_specs=pl.BlockSpec((1, tr, C), lambda r, s: (s[0], r, 0)),
        ),
        out_shape=jax.ShapeDtypeStruct((2, R, C), F32),
        compiler_params=_params(("parallel",)),
    )(jnp.stack(scalars).astype(jnp.int32), *owns, got)


def _adamw_math(w, g, m, v):
    m = ADAM_B1 * m + (1.0 - ADAM_B1) * g
    v = ADAM_B2 * v + (1.0 - ADAM_B2) * (g * g)
    m_hat = m / (1.0 - ADAM_B1 ** ADAM_STEP)
    v_hat = v / (1.0 - ADAM_B2 ** ADAM_STEP)
    delta = -ADAM_LR * (m_hat / (jnp.sqrt(v_hat) + ADAM_EPS) + ADAM_WD * w)
    return delta, m, v


def _adamw_halves(ws, g, ms, vs, half, prev, name, deps=()):
    n = len(ws)
    _, _, R, C = g.shape
    tr = _row_tile(R, ADAMW_ROWS)
    steps = R // tr
    carried = [] if prev is None else [a for four in prev for a in four]
    both = half is None
    which = (lambda i, s: i // steps) if both else (lambda i, s: s[0])
    half = 0 if both else half

    def body(s_ref, *refs):
        w_refs, g_refs, m_refs, v_refs = (refs[k * n:(k + 1) * n] for k in range(4))
        outs = refs[len(refs) - 4 * n:]
        for a in range(n):
            grad = g_refs[a][0, 0]
            d, mn, vn = _adamw_math(w_refs[a][...], grad, m_refs[a][...], v_refs[a][...])
            for o, val in zip(outs[4 * a:4 * a + 4], (grad, d, mn, vn)):
                o[...] = val

    rows = pl.BlockSpec((tr, C), lambda i, s: (which(i, s) * steps + i % steps, 0))
    grad_spec = lambda a: pl.BlockSpec((1, 1, tr, C), lambda i, s: (which(i, s), a, i % steps, 0))
    n_in = 4 * n
    outs = pl.pallas_call(
        body,
        name=name,
        grid_spec=pltpu.PrefetchScalarGridSpec(
            num_scalar_prefetch=1,
            grid=(2 * steps if both else steps,),
            in_specs=[rows] * n + [grad_spec(a) for a in range(n)] + [rows] * (2 * n)
            + [ANY_SPEC] * (len(carried) + len(deps)),
            out_specs=[rows] * (4 * n),
        ),
        out_shape=[jax.ShapeDtypeStruct((2 * R, C), F32)] * (4 * n),
        input_output_aliases={1 + n_in + k: k for k in range(len(carried))},
        compiler_params=_params(("parallel",)),
    )(jnp.reshape(half, (1,)).astype(jnp.int32), *ws, *([g] * n), *ms, *vs, *carried, *deps)
    return [outs[4 * a:4 * a + 4] for a in range(n)]


def _adamw_small(ws, gs, ms, vs, name):
    n = len(ws)

    def body(*refs):
        for a in range(n):
            d, mn, vn = _adamw_math(refs[a][...], refs[n + a][...], refs[2 * n + a][...], refs[3 * n + a][...])
            refs[4 * n + a][...] = d
            refs[5 * n + a][...] = mn
            refs[6 * n + a][...] = vn

    shapes = [jax.ShapeDtypeStruct(w.shape, F32) for w in ws]
    outs = pl.pallas_call(
        body,
        name=name,
        out_shape=shapes * 3,
        compiler_params=_params(),
    )(*ws, *gs, *ms, *vs)
    return outs[:n], outs[n:2 * n], outs[2 * n:]


def _to_blockdiag(w):
    per = CW // LRU_BW
    w4 = w.reshape(N_CT, per, LRU_BW, LRU_BW)
    eye = jnp.eye(per, dtype=w.dtype)
    return (w4[:, :, :, None, :] * eye[None, :, None, :, None]).reshape(N_CT, CW, CW)


def _from_blockdiag(g):
    per = CW // LRU_BW
    g5 = g.reshape(N_CT, per, LRU_BW, per, LRU_BW)
    return jnp.stack([g5[:, b, :, b, :] for b in range(per)], axis=1).reshape(LRU_BLOCKS, LRU_BW, LRU_BW)


def _local_grads(x2d, tgt2d, B, S, g_in, in_proj, conv_b, gate_x_w, gate_x_b, gate_a_w, gate_a_b, lam,
                 proj_weights, g_fin, reduce):
    wx_bd = _c(_to_blockdiag(gate_x_w))
    wa_bd = _c(_to_blockdiag(gate_a_w))
    tables = _retention_tables(S)

    proj, ht, w_all, conv_w, gain = in_proj(x2d, g_in)
    gain3 = gain.reshape(HEADS, 1, DK)
    hlru, ya = _lru_fwd(proj, conv_w, conv_b, wx_bd, wa_bd, gate_x_b, gate_a_b, lam, B, S)
    o_pre, yb, states = _ret_fwd(proj, tables, gain3, B, S)
    wpa, wpb, wout = proj_weights(yb)
    loss, dx2, dya, dyb, dm, dgf, gw_proj = _mid(ya, yb, proj, x2d, tgt2d, wpa, wpb, wout, g_fin)
    g3 = _inproj_bwd_dw(ht, [dm], "inproj_bwd_dw_m")
    deps = reduce.m_ready(gw_proj, g3)
    dr, dgain = _ret_bwd(dyb, o_pre, proj, states, tables, gain3, B, S, deps)
    deps = reduce.ret_done(dr)
    g12 = _inproj_bwd_dw(ht, [dr], "inproj_bwd_dw_r", deps)
    deps = reduce.r_ready(g12)
    dxa, dga, dcw, dcb, dwx_bd, dwa_bd, dbx, dba, dlam = _lru_bwd(
        dya, proj, hlru, conv_w, conv_b, wx_bd, wa_bd, gate_x_b, gate_a_b, lam, B, S, deps)
    small = dict(conv_w=dcw, conv_b=dcb, gate_x_w=_from_blockdiag(dwx_bd), gate_x_b=dbx,
                 gate_a_w=_from_blockdiag(dwa_bd), gate_a_b=dba, lru_lambda=dlam, gn_gain=dgain.reshape(HEADS, DK),
                 norm_final=dgf)
    loss_rows = jnp.broadcast_to(loss, (SUBLANES, LANES))
    deps = reduce.lru_done(dxa, jnp.concatenate([_pack_small(small), loss_rows], axis=0))
    g0 = _inproj_bwd_dw(ht, [dxa, dga], "inproj_bwd_dw_a", deps)
    deps = reduce.a_ready(g0)
    n_tiles = x2d.shape[0] // min(DX_TILE, x2d.shape[0])
    grad_x, dgin = _inproj_bwd_dx([dxa, dga, dr, dm], w_all, x2d, dx2, g_in, 0, n_tiles, None, "inproj_bwd_dx", deps)
    return grad_x, dgin


ALL_CHIPS = (0, 1, 2, 3)


class _GradReduce:
    def __init__(self, proj_done):
        self.pending = {}
        self.proj_done = proj_done
        self.land_in = None

    def _start(self, key, parts, name):
        bufs, plans, shared = [], [], None
        for part_bufs, plan, n_copies, part_shared in parts:
            if part_shared is not None:
                shared = len(bufs) + part_shared
            plans.append((plan, len(part_bufs), n_copies))
            bufs += part_bufs
        plan = _join_plans(plans)
        send_sems, recv_sems, bufs, token = _copies_start(bufs, plan, sum(p[2] for p in plans), name + "_start")
        if shared is not None:
            self.land_in = bufs[shared]
        self.pending[key] = (send_sems, recv_sems, bufs, plan, name + "_wait", shared)
        return (token,)

    def _finish(self, key, after):
        send_sems, recv_sems, bufs, plan, name, shared = self.pending.pop(key)
        if shared is not None:
            bufs[shared] = self.land_in
        bufs = _copies_wait(send_sems, recv_sems, bufs, after, plan, name)
        if shared is not None:
            self.land_in = bufs[shared]
        return bufs

    @staticmethod
    def _swap(pieces):
        bufs = []
        for g in pieces:
            bufs += [g, lax.empty((g.shape[0],) + g.shape[2:], F32)]
        n_slabs = [g.shape[0] for g in pieces]
        return bufs, _swap_plan(n_slabs), sum(n_slabs), None

    def _scatter(self, sums, dest_sets):
        bufs = []
        for cs in sums:
            bufs += [cs, lax.empty((3,) + cs.shape[1:], cs.dtype)]
        if self.land_in is not None:
            bufs[-1] = self.land_in
        return bufs, _scatter_plan(dest_sets), 3 * len(sums), len(bufs) - 1

    @staticmethod
    def _gather8(block):
        x, y, c = _coords()
        land = lax.dynamic_update_slice(lax.empty((8,) + block.shape, F32), block[None], (4 * x + 2 * y + c, 0, 0))
        return [land], _allgather_plan(), 7, None

    def m_ready(self, gw_proj, g3):
        rows = gw_proj.shape[2] * gw_proj.shape[3]
        return self._start("m", [self._swap([gw_proj.reshape(N_CHIPS, 2, rows, D_MODEL), g3])], "swap_m")

    def ret_done(self, after):
        proj, land_p, g3, land_3 = self._finish("m", after)
        sums_m = [_add_my_half(proj, land_p, "chip_sum_proj"), _add_my_half(g3, land_3, "chip_sum_m")]
        return self._start("sm", [self._scatter(sums_m, [ALL_CHIPS, (3,)])], "scatter_m")

    def r_ready(self, g12):
        return self._start("r", [self._swap([g12])], "swap_r")

    def lru_done(self, after, packed):
        g12, land_12 = self._finish("r", after)
        sums_r = [_add_my_half(g12, land_12, "chip_sum_r")]
        return (self._start("sr", [self._scatter(sums_r, [(1, 2)])], "scatter_r")
                + self._start("small", [self._gather8(packed)], "gather_small"))

    def a_ready(self, g0):
        (token,) = self._start("a", [self._swap([g0])], "swap_a")
        csp, gotp, self.cs3, _ = self._finish("sm", token)
        half_proj = _sum_slabs(csp, gotp, "sum_w_proj")
        g0, land_0 = self._finish("a", half_proj)
        deps = self._start("sa", [self._scatter([_add_my_half(g0, land_0, "chip_sum_a")], [(0,)])], "scatter_a")
        self.proj_done(_join_halves([half_proj], [4], "join_halves_proj", deps)[0])
        return deps

    def finish(self, dgin, w_in_done):
        (token,) = self._start("n", [self._gather8(dgin)], "gather_norm_in")
        (small,) = self._finish("small", token)
        cs12, _ = self._finish("sr", token)
        cs0, _ = self._finish("sa", token)
        half_in = _sum_parts([self.cs3, cs12, cs0], self.land_in, [(3,), (1, 2), (0,)], "sum_w_in")
        deps = self._start("j", [([half_in], _join_plan(half_in.shape[1], JOIN_PIECES), JOIN_PIECES, None)], "join_w_in")
        first = w_in_done(self.pending["j"][2][0], True, None, deps)
        (g_in,) = self._finish("j", first[1])
        done = w_in_done(g_in, False, first, ())
        (norm_in,) = self._finish("n", done[1])
        return _sum_gathered(small, "sum_small_grads"), _sum_gathered(norm_in, "sum_norm_in_grad")


_SMALL = ("gate_x_w", "gate_a_w", "conv_w", "conv_b", "gate_x_b", "gate_a_b", "lru_lambda", "gn_gain", "norm_final")
_SMALL_SHAPES = dict(gate_x_w=(LRU_BLOCKS, LRU_BW, LRU_BW), gate_a_w=(LRU_BLOCKS, LRU_BW, LRU_BW),
                     norm_in=(1, D_MODEL), conv_w=(CONV, D_MODEL), conv_b=(1, D_MODEL), gate_x_b=(1, D_MODEL),
                     gate_a_b=(1, D_MODEL), lru_lambda=(1, D_MODEL), gn_gain=(HEADS, DK), norm_final=(1, D_MODEL))


def _pack_small(small):
    return jnp.concatenate([small[k].reshape(-1, 128) for k in _SMALL], axis=0)


def _unpack_small(packed):
    out, r = {}, 0
    for k in _SMALL:
        shape = _SMALL_SHAPES[k]
        rows = 1
        for s in shape:
            rows *= s
        rows //= 128
        out[k] = packed[r:r + rows].reshape(shape)
        r += rows
    return out


def kernel(x, norm_in, w_in, conv_w, conv_b, gate_x_w, gate_x_b, gate_a_w, gate_a_b, lru_lambda, gn_gain, w_proj_a, w_proj_b, w_out, norm_final, loss_target, m_norm_in, m_w_in, m_conv_w, m_conv_b, m_gate_x_w, m_gate_x_b, m_gate_a_w, m_gate_a_b, m_lru_lambda, m_gn_gain, m_w_proj_a, m_w_proj_b, m_w_out, m_norm_final, v_norm_in, v_w_in, v_conv_w, v_conv_b, v_gate_x_w, v_gate_x_b, v_gate_a_w, v_gate_a_b, v_lru_lambda, v_gn_gain, v_w_proj_a, v_w_proj_b, v_w_out, v_norm_final):
    B, S, _ = x.shape
    T = B * S
    xi, yi, ci = _coords()
    chip = 2 * xi + yi

    cshard = D_MODEL // N_CHIPS
    mine = _cast_into_slot([w_in[0].reshape(2, D_MODEL // 2, 2 * D_MODEL)]
                           + [w[0].reshape(2, cshard // 2, D_MODEL) for w in (w_proj_a, w_proj_b, w_out)],
                           "cast_weights")
    plan = _gather_plan(3)
    pending_proj = []
    gshard = DK // N_CHIPS
    tiny = jnp.concatenate([conv_w[0], jnp.zeros((4, cshard), F32), jnp.pad(gn_gain[0], ((0, 4), (0, cshard - gshard)))],
                           axis=0).reshape(1, 2, SUBLANES, cshard)
    tiny_buf = lax.dynamic_update_slice(lax.empty((N_CHIPS, 2, SUBLANES, cshard), F32), tiny, (chip, 0, 0, 0))
    near_plan, pass_plan, far_plan = (_chip_gather_plan(stage, 2) for stage in ("near", "pass", "far"))
    (n_near, _), (n_pass, passed_on), (n_far, _) = (_chip_gather_copies(stage, 2) for stage in ("near", "pass", "far"))
    halves = set(range(n_pass)) - passed_on
    near_s, near_r, bufs, near_token = _copies_start([mine[0], tiny_buf], near_plan, n_near, "gather_near_start")

    def in_proj(x2d, g_in):
        as_w = lambda b: b[0].reshape(N_CHIPS, D_MODEL, 2 * D_MODEL)
        slot_x, slot_y, slot_d = 2 * (1 - xi) + yi, 2 * xi + (1 - yi), 2 * (1 - xi) + (1 - yi)
        ids = lambda *chips: jnp.stack(chips).astype(jnp.int32)
        proj, hb, ht = _inproj_first(x2d, g_in, as_w(bufs), ids(chip), "inproj_own", (near_token,))
        got = _copies_wait(near_s, near_r, bufs, proj, near_plan, "gather_near_wait")
        pass_s, pass_r, got, pass_token = _copies_start(got, pass_plan, n_pass, "gather_pass_start")
        got = _copies_wait(pass_s, pass_r, got, pass_token, pass_plan, "gather_pass_wait_halves", only=halves)
        proj = _inproj_more(hb, as_w(got), ids(slot_x, slot_y), proj, "inproj_near")
        got = _copies_wait(pass_s, pass_r, got, proj, pass_plan, "gather_pass_wait_far", only=passed_on)
        pending_proj.append(_copies_start(mine[1:], plan, 9, "gather_proj_start", (got[0],)))
        far_s, far_r, got, far_token = _copies_start(got, far_plan, n_far, "gather_far_start")
        got = _copies_wait(far_s, far_r, got, far_token, far_plan, "gather_far_wait")
        proj = _inproj_more(hb, as_w(got), ids(slot_d), proj, "inproj_far")
        tiny_all = got[1].reshape(N_CHIPS, 2 * SUBLANES, cshard)
        conv_w_full = jnp.transpose(tiny_all[:, 0:CONV, :], (1, 0, 2)).reshape(CONV, D_MODEL)
        gain_full = jnp.transpose(tiny_all[:, 8:8 + HEADS, :gshard], (1, 0, 2)).reshape(HEADS, DK)
        return proj, ht, as_w(got), conv_w_full, gain_full

    def proj_weights(after):
        s_sems, r_sems, pbufs, _ = pending_proj[0]
        got = _copies_wait(s_sems, r_sems, pbufs, after, plan, "gather_proj_wait")
        return [b.reshape(D_MODEL, D_MODEL) for b in got]

    weights = dict(norm_in=norm_in, w_in=w_in, conv_w=conv_w, conv_b=conv_b, gate_x_w=gate_x_w, gate_x_b=gate_x_b,
                   gate_a_w=gate_a_w, gate_a_b=gate_a_b, lru_lambda=lru_lambda, gn_gain=gn_gain, w_proj_a=w_proj_a,
                   w_proj_b=w_proj_b, w_out=w_out, norm_final=norm_final)
    ms = dict(norm_in=m_norm_in, w_in=m_w_in, conv_w=m_conv_w, conv_b=m_conv_b, gate_x_w=m_gate_x_w,
              gate_x_b=m_gate_x_b, gate_a_w=m_gate_a_w, gate_a_b=m_gate_a_b, lru_lambda=m_lru_lambda, gn_gain=m_gn_gain,
              w_proj_a=m_w_proj_a, w_proj_b=m_w_proj_b, w_out=m_w_out, norm_final=m_norm_final)
    vs = dict(norm_in=v_norm_in, w_in=v_w_in, conv_w=v_conv_w, conv_b=v_conv_b, gate_x_w=v_gate_x_w,
              gate_x_b=v_gate_x_b, gate_a_w=v_gate_a_w, gate_a_b=v_gate_a_b, lru_lambda=v_lru_lambda, gn_gain=v_gn_gain,
              w_proj_a=v_w_proj_a, w_proj_b=v_w_proj_b, w_out=v_w_out, norm_final=v_norm_final)
    names = list(weights)
    grads, delta, new_m, new_v = {}, {}, {}, {}

    def update_big(keys, g, half, prev, name, deps=()):
        two = lambda a: a.reshape(a.shape[1], a.shape[2])
        res = _adamw_halves([two(weights[k]) for k in keys], g, [two(ms[k]) for k in keys], [two(vs[k]) for k in keys],
                            half, prev, name, deps)
        for k, (gk, d, mn, vn) in zip(keys, res):
            shp = weights[k].shape
            grads[k], delta[k], new_m[k], new_v[k] = gk.reshape(shp), d.reshape(shp), mn.reshape(shp), vn.reshape(shp)
        return res

    def proj_done(g_proj):
        g4 = g_proj.reshape(2, 3, D_MODEL // (2 * N_CHIPS), D_MODEL)
        return update_big(("w_proj_a", "w_proj_b", "w_out"), g4, None, None, "adamw_proj")[-1][1]

    def w_in_done(g_in, own, prev, deps):
        g4 = g_in.reshape(2, 1, D_MODEL // 2, 2 * D_MODEL)
        return update_big(("w_in",), g4, ci if own else 1 - ci, None if prev is None else [prev],
                          "adamw_w_in_own" if own else "adamw_w_in_other", deps)[0]

    reduce = _GradReduce(proj_done)
    grad_x, dgin = _local_grads(
        x.reshape(T, D_MODEL), loss_target.reshape(T, D_MODEL), B, S, norm_in, in_proj, conv_b,
        gate_x_w[0], gate_x_b, gate_a_w[0], gate_a_b, lru_lambda, proj_weights,
        norm_final.reshape(1, D_MODEL), reduce)

    small_sum, g_norm_in = reduce.finish(dgin.reshape(SUBLANES, LANES), w_in_done)
    loss = small_sum[small_sum.shape[0] - SUBLANES, 0]

    gsm = _unpack_small(small_sum)
    gsm["norm_in"] = g_norm_in
    gsm["conv_w"] = lax.dynamic_slice_in_dim(gsm["conv_w"], chip * cshard, cshard, axis=1)
    gsm["gn_gain"] = lax.dynamic_slice_in_dim(gsm["gn_gain"], chip * gshard, gshard, axis=1)
    smalls = [k for k in names if k not in delta]

    def view(a):
        return a.reshape(1, -1) if a.ndim == 1 else (a.reshape(a.shape[1:]) if a.ndim > 2 else a)

    ds, mns, vns = _adamw_small([view(weights[k]) for k in smalls], [gsm[k].reshape(view(weights[k]).shape) for k in smalls],
                                [view(ms[k]) for k in smalls], [view(vs[k]) for k in smalls], "adamw_small")
    for k, d, mn, vn in zip(smalls, ds, mns, vns):
        shp = weights[k].shape
        grads[k], delta[k], new_m[k], new_v[k] = gsm[k].reshape(shp), d.reshape(shp), mn.reshape(shp), vn.reshape(shp)

    return (loss, grad_x.reshape(B, S, D_MODEL), *[grads[k] for k in names], *[delta[k] for k in names],
            *[new_m[k] for k in names], *[new_v[k] for k in names])
```

```python
import jax
import jax.numpy as jnp
from jax import lax
from jax.experimental import pallas as pl
from jax.experimental.pallas import tpu as pltpu

F32 = jnp.float32
_MXU = jnp.bfloat16

D_MODEL = 1024
N_GROUPS = 8
HEADS = 4
DK = 256
CHUNK = 128
CONV = 4
LRU_BLOCKS = 16
LRU_BW = 64
LRU_C = 8.0
ROPE_THETA = 10000.0
EPS = 1e-6
CW = 256
N_CT = D_MODEL // CW
N_CHIPS = 4
MESH = pl.DeviceIdType.MESH

ADAM_LR = 0.001
ADAM_B1 = 0.9
ADAM_B2 = 0.999
ADAM_EPS = 1e-08
ADAM_WD = 0.01
ADAM_STEP = 10

VMEM_LIMIT = 56 * 1024 * 1024

FIRST_PROJ_TILE = 1024
MORE_PROJ_TILE = 2048
SCAN_TILE = 256
MID_TILE = 256
DX_TILE = 512
DW_COLS = 512
SUM_ROWS = 256
ADAMW_ROWS = 256
JOIN_PIECES = 8


def _c(v):
    return v.astype(_MXU)


def _dot(a, b):
    return lax.dot_general(a, b, (((1,), (0,)), ((), ())), preferred_element_type=F32)


def _dot_nt(a, b):
    return lax.dot_general(a, b, (((1,), (1,)), ((), ())), preferred_element_type=F32)


def _dot_tn(a, b):
    return lax.dot_general(a, b, (((0,), (0,)), ((), ())), preferred_element_type=F32)


def _sigmoid(z):
    return 0.5 * jnp.tanh(0.5 * z) + 0.5


ANY_SPEC = pl.BlockSpec(memory_space=pl.ANY)


def _after(body, n_in, deps):
    n_deps = len(deps)

    def wrapped(*refs):
        return body(*refs[:n_in], *refs[n_in + n_deps:])

    return wrapped


def _params(sem=None):
    if sem is None:
        return pltpu.CompilerParams(vmem_limit_bytes=VMEM_LIMIT)
    return pltpu.CompilerParams(vmem_limit_bytes=VMEM_LIMIT, dimension_semantics=sem)


def _inproj_first(x2d, g_in, w_all, chips, name, deps=()):
    T = x2d.shape[0]
    tm = min(FIRST_PROJ_TILE, T)
    n_i = T // tm

    def body(s_ref, *refs):
        x_ref, g_ref, w_ref = refs[:3]
        proj_ref, hb_ref, ht_ref, h_all = refs[-4:]
        i = pl.program_id(1)
        rows = pl.ds(pl.multiple_of(i * tm, tm), tm)

        @pl.when(pl.program_id(0) == 0)
        def _():
            x = x_ref[...]
            r = lax.rsqrt(jnp.mean(x * x, axis=-1, keepdims=True) + EPS)
            h = x * r * g_ref[...]
            hb = h.astype(h_all.dtype)
            h_all[rows, :] = hb
            hb_ref[...] = hb
            ht_ref[...] = h.T.astype(ht_ref.dtype)

        proj_ref[...] = _dot(h_all[rows, :], w_ref[0])

    first = lambda j, i: jnp.where(j == 0, i, n_i - 1)
    return pl.pallas_call(
        body,
        name=name,
        grid_spec=pltpu.PrefetchScalarGridSpec(
            num_scalar_prefetch=1,
            grid=(2 * chips.shape[0], n_i),
            in_specs=[
                pl.BlockSpec((tm, D_MODEL), lambda j, i, s: (first(j, i), 0)),
                pl.BlockSpec((1, D_MODEL), lambda j, i, s: (0, 0)),
                pl.BlockSpec((1, D_MODEL, D_MODEL), lambda j, i, s: (s[j // 2], 0, j % 2)),
            ] + [ANY_SPEC] * len(deps),
            out_specs=[
                pl.BlockSpec((tm, D_MODEL), lambda j, i, s: (i, 2 * s[j // 2] + j % 2)),
                pl.BlockSpec((tm, D_MODEL), lambda j, i, s: (first(j, i), 0)),
                pl.BlockSpec((D_MODEL, tm), lambda j, i, s: (0, first(j, i))),
            ],
            scratch_shapes=[pltpu.VMEM((T, D_MODEL), _MXU)],
        ),
        out_shape=[
            jax.ShapeDtypeStruct((T, N_GROUPS * D_MODEL), F32),
            jax.ShapeDtypeStruct((T, D_MODEL), _MXU),
            jax.ShapeDtypeStruct((D_MODEL, T), _MXU),
        ],
        compiler_params=_params(("arbitrary", "arbitrary")),
    )(chips, x2d, g_in, w_all, *deps)


def _inproj_more(hb, w_all, chips, proj, name):
    T = hb.shape[0]
    tm = min(MORE_PROJ_TILE, T)

    def body(s_ref, hb_hbm, w_ref, prev_ref, proj_ref, h_all, sem):
        @pl.when((pl.program_id(0) == 0) & (pl.program_id(1) == 0))
        def _():
            cp = pltpu.make_async_copy(hb_hbm, h_all, sem)
            cp.start()
            cp.wait()

        rows = pl.ds(pl.multiple_of(pl.program_id(1) * tm, tm), tm)
        proj_ref[...] = _dot(h_all[rows, :], w_ref[0])

    return pl.pallas_call(
        body,
        name=name,
        grid_spec=pltpu.PrefetchScalarGridSpec(
            num_scalar_prefetch=1,
            grid=(2 * chips.shape[0], T // tm),
            in_specs=[
                ANY_SPEC,
                pl.BlockSpec((1, D_MODEL, D_MODEL), lambda j, i, s: (s[j // 2], 0, j % 2)),
                ANY_SPEC,
            ],
            out_specs=pl.BlockSpec((tm, D_MODEL), lambda j, i, s: (i, 2 * s[j // 2] + j % 2)),
            scratch_shapes=[pltpu.VMEM((T, D_MODEL), hb.dtype), pltpu.SemaphoreType.DMA],
        ),
        out_shape=jax.ShapeDtypeStruct(proj.shape, F32),
        input_output_aliases={3: 0},
        compiler_params=_params(("arbitrary", "arbitrary")),
    )(chips, hb, w_all, proj)


def _scan_fwd(a, u):
    n = a.shape[0]
    row = lax.broadcasted_iota(jnp.int32, a.shape, 0)
    s = 1
    while s < n:
        m = row >= s
        u = u + a * jnp.where(m, pltpu.roll(u, s, 0), 0.0)
        a = a * jnp.where(m, pltpu.roll(a, s, 0), 1.0)
        s *= 2
    return a, u


def _scan_bwd(b, g):
    n = b.shape[0]
    row = lax.broadcasted_iota(jnp.int32, b.shape, 0)
    s = 1
    while s < n:
        m = row < n - s
        g = g + b * jnp.where(m, pltpu.roll(g, n - s, 0), 0.0)
        b = b * jnp.where(m, pltpu.roll(b, n - s, 0), 1.0)
        s *= 2
    return b, g


LANES = 128
SUBLANES = 8


def _scan_scratch(tc):
    by_lanes = pltpu.VMEM((CW // LANES, tc, LANES), F32)
    return [by_lanes, by_lanes, pltpu.VMEM((tc // SUBLANES, CW), F32), pltpu.VMEM((tc, CW), F32)]


def _scan_tile(a, u, edge, la_ref, lh_ref, c_ref, dst_ref, reverse):
    n, w = a.shape
    groups = n // SUBLANES
    a3 = a.reshape(groups, SUBLANES, w)
    u3 = u.reshape(groups, SUBLANES, w)
    row = lax.broadcasted_iota(jnp.int32, a3.shape, 1)
    for s in (1, 2, 4):
        m = (row < SUBLANES - s) if reverse else (row >= s)
        shift = SUBLANES - s if reverse else s
        u3 = u3 + a3 * jnp.where(m, pltpu.roll(u3, shift, 1), 0.0)
        a3 = a3 * jnp.where(m, pltpu.roll(a3, shift, 1), 1.0)
    al = a3.reshape(n, w)
    hl = u3.reshape(n, w)
    blocks = w // LANES
    for q in range(blocks):
        la_ref[q] = al[:, q * LANES:(q + 1) * LANES]
        lh_ref[q] = hl[:, q * LANES:(q + 1) * LANES]
    ends = pl.ds(0 if reverse else SUBLANES - 1, groups, stride=SUBLANES)
    end_a = jnp.concatenate([la_ref.at[q][ends, :] for q in range(blocks)], axis=-1)
    end_h = jnp.concatenate([lh_ref.at[q][ends, :] for q in range(blocks)], axis=-1)
    prod, part = (_scan_bwd if reverse else _scan_fwd)(end_a, end_h)
    total = part + prod * edge
    g_row = lax.broadcasted_iota(jnp.int32, total.shape, 0)
    if reverse:
        c_ref[...] = jnp.where(g_row == groups - 1, edge, pltpu.roll(total, groups - 1, 0))
    else:
        c_ref[...] = jnp.where(g_row == 0, edge, pltpu.roll(total, 1, 0))
    for g in range(groups):
        rows = slice(g * SUBLANES, (g + 1) * SUBLANES)
        for q in range(blocks):
            cols = slice(q * LANES, (q + 1) * LANES)
            dst_ref[rows, cols] = lh_ref[q, rows, :] + la_ref[q, rows, :] * c_ref[g:g + 1, cols]


def _softplus_neg(lam):
    z = -lam
    return jnp.maximum(z, 0.0) + jnp.log1p(jnp.exp(-jnp.abs(z)))


def _lru_gates(xc, wx_ref, wa_ref, bx_ref, ba_ref, lam_ref):
    xcb = _c(xc)
    i_t = _sigmoid(_dot(xcb, wx_ref[0]) + bx_ref[...])
    r_t = _sigmoid(_dot(xcb, wa_ref[0]) + ba_ref[...])
    sp = _softplus_neg(lam_ref[...])
    log_a = (-LRU_C) * r_t * sp
    a = jnp.exp(log_a)
    mult = jnp.sqrt(1.0 - a * a)
    return xcb, i_t, r_t, sp, a, mult


def _conv_from_ext(ext_ref, xa, cw_ref, cb_ref, tc):
    return (cb_ref[...] + cw_ref[3:4, :] * xa + cw_ref[2:3, :] * ext_ref[7:7 + tc, :]
            + cw_ref[1:2, :] * ext_ref[6:6 + tc, :] + cw_ref[0:1, :] * ext_ref[5:5 + tc, :])


def _lru_fwd(proj, conv_w, conv_b, wx_bd, wa_bd, bx, ba, lam, B, S):
    T = B * S
    tc = min(SCAN_TILE, S)
    nt = S // tc
    h8 = tc // 8

    def body(xa_ref, halo_ref, ga_ref, cw_ref, cb_ref, wx_ref, wa_ref, bx_ref, ba_ref, lam_ref,
             h_ref, ya_ref, ext_ref, carry_ref, la_ref, lh_ref, c_ref):
        t = pl.program_id(2)

        @pl.when(t == 0)
        def _():
            carry_ref[...] = jnp.zeros_like(carry_ref)

        xa = xa_ref[...]
        ext_ref[0:8, :] = jnp.where(t == 0, 0.0, halo_ref[...])
        ext_ref[8:8 + tc, :] = xa
        xc = _conv_from_ext(ext_ref, xa, cw_ref, cb_ref, tc)
        _, i_t, _, _, a, mult = _lru_gates(xc, wx_ref, wa_ref, bx_ref, ba_ref, lam_ref)
        u = mult * (i_t * xc)
        _scan_tile(a, u, carry_ref[7:8, :], la_ref, lh_ref, c_ref, h_ref, False)
        h = h_ref[...]
        carry_ref[...] = h[tc - 8:tc, :]
        ga = ga_ref[...]
        ya_ref[...] = (ga * _sigmoid(ga) * h).astype(ya_ref.dtype)

    row = lambda b, t: b * nt + t
    vec = pl.BlockSpec((1, CW), lambda b, c, t: (0, c))
    mat = pl.BlockSpec((1, CW, CW), lambda b, c, t: (c, 0, 0))
    return pl.pallas_call(
        body,
        name="lru_fwd",
        grid=(B, N_CT, nt),
        in_specs=[
            pl.BlockSpec((tc, CW), lambda b, c, t: (row(b, t), c)),
            pl.BlockSpec((8, CW), lambda b, c, t: (jnp.maximum(row(b, t) * h8 - 1, 0), c)),
            pl.BlockSpec((tc, CW), lambda b, c, t: (row(b, t), N_CT + c)),
            pl.BlockSpec((CONV, CW), lambda b, c, t: (0, c)),
            vec, mat, mat, vec, vec, vec,
        ],
        out_specs=[
            pl.BlockSpec((tc, CW), lambda b, c, t: (row(b, t), c)),
            pl.BlockSpec((tc, CW), lambda b, c, t: (row(b, t), c)),
        ],
        out_shape=[
            jax.ShapeDtypeStruct((T, D_MODEL), F32),
            jax.ShapeDtypeStruct((T, D_MODEL), _MXU),
        ],
        scratch_shapes=[pltpu.VMEM((tc + 8, CW), F32), pltpu.VMEM((8, CW), F32)] + _scan_scratch(tc)[:3],
        compiler_params=_params(("parallel", "parallel", "arbitrary")),
    )(proj, proj, proj, conv_w, conv_b, wx_bd, wa_bd, bx, ba, lam)


def _lru_bwd(dya, proj, hlru, conv_w, conv_b, wx_bd, wa_bd, bx, ba, lam, B, S, deps=()):
    T = B * S
    tc = min(SCAN_TILE, S)
    nt = S // tc
    h8 = tc // 8

    def body(dya_ref, xa_ref, xhalo_ref, ga_ref, h_ref, hhalo_ref, cw_ref, cb_ref, wx_ref, wa_ref, bx_ref, ba_ref,
             lam_ref, dxa_ref, dga_ref, dcw_ref, dcb_ref, dwx_ref, dwa_ref, dbx_ref, dba_ref, dlam_ref,
             ext_ref, ext2_ref, carry_ref, dhalo_ref, la_ref, lh_ref, c_ref, dh_ref):
        b = pl.program_id(1)
        t = pl.program_id(2)
        tt = nt - 1 - t

        @pl.when(t == 0)
        def _():
            carry_ref[...] = jnp.zeros_like(carry_ref)
            dhalo_ref[...] = jnp.zeros_like(dhalo_ref)

        @pl.when((t == 0) & (b == 0))
        def _():
            for r in (dcw_ref, dcb_ref, dwx_ref, dwa_ref, dbx_ref, dba_ref, dlam_ref):
                r[...] = jnp.zeros_like(r)

        xa = xa_ref[...]
        ext_ref[0:8, :] = jnp.where(tt == 0, 0.0, xhalo_ref[...])
        ext_ref[8:8 + tc, :] = xa
        xc = _conv_from_ext(ext_ref, xa, cw_ref, cb_ref, tc)
        xcb, i_t, r_t, sp, a, mult = _lru_gates(xc, wx_ref, wa_ref, bx_ref, ba_ref, lam_ref)

        h = h_ref[...]
        ga = ga_ref[...]
        dya_t = dya_ref[...]
        sg = _sigmoid(ga)
        dga_ref[...] = (dya_t * h * (sg * (1.0 + ga * (1.0 - sg)))).astype(dga_ref.dtype)
        dlru = dya_t * (ga * sg)

        row = lax.broadcasted_iota(jnp.int32, a.shape, 0)
        coef = jnp.where(row == tc - 1, 1.0, pltpu.roll(a, tc - 1, 0))
        _scan_tile(coef, dlru, carry_ref[0:1, :], la_ref, lh_ref, c_ref, dh_ref, True)
        dh = dh_ref[...]
        ext2_ref[0:tc, :] = a * dh
        carry_ref[...] = ext2_ref[0:8, :]

        ext2_ref[0:8, :] = jnp.where(tt == 0, 0.0, hhalo_ref[...])
        ext2_ref[8:8 + tc, :] = h
        hprev = ext2_ref[7:7 + tc, :]

        da = dh * hprev
        ix = i_t * xc
        dmult = dh * ix
        di = dh * mult * xc
        dxc = dh * mult * i_t
        dlog_a = da * a - dmult * (a * a) / mult
        dr = dlog_a * ((-LRU_C) * sp)
        dlam_ref[...] += jnp.sum(dlog_a * r_t, axis=0, keepdims=True) * (LRU_C * _sigmoid(-lam_ref[...]))
        dza = dr * r_t * (1.0 - r_t)
        dzx = di * i_t * (1.0 - i_t)
        dzab = _c(dza)
        dzxb = _c(dzx)
        dxc = dxc + _dot_nt(dzxb, wx_ref[0]) + _dot_nt(dzab, wa_ref[0])
        dwx_ref[0] += _dot_tn(xcb, dzxb)
        dwa_ref[0] += _dot_tn(xcb, dzab)
        dbx_ref[...] += jnp.sum(dzx, axis=0, keepdims=True)
        dba_ref[...] += jnp.sum(dza, axis=0, keepdims=True)

        dcb_ref[...] += jnp.sum(dxc, axis=0, keepdims=True)
        dcw_ref[3:4, :] += jnp.sum(dxc * xa, axis=0, keepdims=True)
        dcw_ref[2:3, :] += jnp.sum(dxc * ext_ref[7:7 + tc, :], axis=0, keepdims=True)
        dcw_ref[1:2, :] += jnp.sum(dxc * ext_ref[6:6 + tc, :], axis=0, keepdims=True)
        dcw_ref[0:1, :] += jnp.sum(dxc * ext_ref[5:5 + tc, :], axis=0, keepdims=True)
        ext2_ref[0:tc, :] = dxc
        ext2_ref[tc:tc + 8, :] = dhalo_ref[...]
        dxa = (cw_ref[3:4, :] * dxc + cw_ref[2:3, :] * ext2_ref[1:1 + tc, :]
               + cw_ref[1:2, :] * ext2_ref[2:2 + tc, :] + cw_ref[0:1, :] * ext2_ref[3:3 + tc, :])
        dxa_ref[...] = dxa.astype(dxa_ref.dtype)
        dhalo_ref[...] = ext2_ref[0:8, :]

    row_of = lambda b, t: b * nt + (nt - 1 - t)
    tile = lambda off: pl.BlockSpec((tc, CW), lambda c, b, t: (row_of(b, t), off + c))
    halo = pl.BlockSpec((8, CW), lambda c, b, t: (jnp.maximum(row_of(b, t) * h8 - 1, 0), c))
    vec = pl.BlockSpec((1, CW), lambda c, b, t: (0, c))
    mat = pl.BlockSpec((1, CW, CW), lambda c, b, t: (c, 0, 0))
    cwspec = pl.BlockSpec((CONV, CW), lambda c, b, t: (0, c))
    return pl.pallas_call(
        _after(body, 13, deps),
        name="lru_bwd",
        grid=(N_CT, B, nt),
        in_specs=[tile(0), tile(0), halo, tile(N_CT), tile(0), halo, cwspec, vec, mat, mat, vec, vec, vec]
        + [ANY_SPEC] * len(deps),
        out_specs=[tile(0), tile(0), cwspec, vec, mat, mat, vec, vec, vec],
        out_shape=[
            jax.ShapeDtypeStruct((T, D_MODEL), _MXU),
            jax.ShapeDtypeStruct((T, D_MODEL), _MXU),
            jax.ShapeDtypeStruct((CONV, D_MODEL), F32),
            jax.ShapeDtypeStruct((1, D_MODEL), F32),
            jax.ShapeDtypeStruct((N_CT, CW, CW), F32),
            jax.ShapeDtypeStruct((N_CT, CW, CW), F32),
            jax.ShapeDtypeStruct((1, D_MODEL), F32),
            jax.ShapeDtypeStruct((1, D_MODEL), F32),
            jax.ShapeDtypeStruct((1, D_MODEL), F32),
        ],
        scratch_shapes=[pltpu.VMEM((tc + 8, CW), F32), pltpu.VMEM((tc + 8, CW), F32),
                        pltpu.VMEM((8, CW), F32), pltpu.VMEM((8, CW), F32)] + _scan_scratch(tc),
        compiler_params=_params(("parallel", "arbitrary", "arbitrary")),
    )(dya, proj, proj, proj, hlru, hlru, conv_w, conv_b, wx_bd, wa_bd, bx, ba, lam, *deps)


def _retention_tables(S):
    half = DK // 2
    freqs = ROPE_THETA ** (-jnp.arange(half, dtype=F32) / half)
    ang = jnp.arange(S, dtype=F32)[:, None] * freqs[None, :]
    log_g = jnp.log1p(-(2.0 ** (-5.0 - jnp.arange(HEADS, dtype=F32))))
    idx = jnp.arange(CHUNK, dtype=F32)
    diff = idx[:, None] - idx[None, :]
    inner = jnp.where(diff >= 0, jnp.exp(jnp.maximum(diff, 0.0)[None] * log_g[:, None, None]), 0.0)
    cross = jnp.exp((idx[None, :] + 1.0) * log_g[:, None])[:, :, None]
    state = jnp.exp((CHUNK - 1.0 - idx[None, :]) * log_g[:, None])[:, :, None]
    gam = jnp.broadcast_to(jnp.exp(CHUNK * log_g)[:, None, None], (HEADS, 1, DK))
    return jnp.cos(ang), jnp.sin(ang), inner, cross, state, gam


def _rot(x, cos, sin):
    half = DK // 2
    x1, x2 = x[:, :half], x[:, half:]
    return jnp.concatenate([x1 * cos - x2 * sin, x1 * sin + x2 * cos], axis=-1)


def _rot_t(y, cos, sin):
    half = DK // 2
    y1, y2 = y[:, :half], y[:, half:]
    return jnp.concatenate([y1 * cos + y2 * sin, y2 * cos - y1 * sin], axis=-1)


def _groupnorm(o):
    mu = jnp.mean(o, axis=-1, keepdims=True)
    oc = o - mu
    rs = lax.rsqrt(jnp.mean(oc * oc, axis=-1, keepdims=True) + EPS)
    return oc * rs, rs


def _ret_specs(B, chunk_of):
    qkv = lambda g: pl.BlockSpec((B, CHUNK, D_MODEL), lambda c: (0, chunk_of(c), g))
    act = pl.BlockSpec((B, CHUNK, D_MODEL), lambda c: (0, chunk_of(c), 0))
    rope = pl.BlockSpec((CHUNK, DK // 2), lambda c: (chunk_of(c), 0))
    dmat = pl.BlockSpec((HEADS, CHUNK, CHUNK), lambda c: (0, 0, 0))
    dvec = pl.BlockSpec((HEADS, CHUNK, 1), lambda c: (0, 0, 0))
    hrow = pl.BlockSpec((HEADS, 1, DK), lambda c: (0, 0, 0))
    rst = pl.BlockSpec((1, B, HEADS, DK, DK), lambda c: (chunk_of(c), 0, 0, 0, 0))
    return qkv, act, rope, dmat, dvec, hrow, rst


def _ret_fwd(proj, tables, gain3, B, S):
    T = B * S
    nc = S // CHUNK
    cos, sin, dmat_t, cd_t, sd_t, gam_t = tables

    def body(q_ref, k_ref, v_ref, gb_ref, cos_ref, sin_ref, dm_ref, cd_ref, sd_ref, gam_ref, gain_ref,
             o_ref, yb_ref, rs_ref, state_ref):
        @pl.when(pl.program_id(0) == 0)
        def _():
            state_ref[...] = jnp.zeros_like(state_ref)

        cos_t, sin_t = cos_ref[...], sin_ref[...]
        for b, h in [(b, h) for b in range(B) for h in range(HEADS)]:
            cols = slice(h * DK, (h + 1) * DK)
            qb = _c(_rot(q_ref[b, :, cols], cos_t, sin_t))
            kb = _c(_rot(k_ref[b, :, cols], cos_t, sin_t) * (DK ** -0.5))
            v = v_ref[b, :, cols]
            state = state_ref[b, h]
            sb = _c(state)
            rs_ref[0, b, h] = sb
            scores = _dot_nt(qb, kb) * dm_ref[h]
            o = _dot(_c(scores), _c(v)) + _dot(qb, sb) * cd_ref[h]
            state_ref[b, h] = gam_ref[h] * state + _dot_tn(kb, _c(v * sd_ref[h]))
            o_ref[b, :, cols] = o
            n, _ = _groupnorm(o)
            gb = gb_ref[b, :, cols]
            yb_ref[b, :, cols] = (gb * _sigmoid(gb) * (n * gain_ref[h])).astype(yb_ref.dtype)

    qkv, act, rope, dmat, dvec, hrow, rst = _ret_specs(B, lambda c: c)
    proj3 = proj.reshape(B, S, proj.shape[1])
    o_pre, yb, states = pl.pallas_call(
        body,
        name="ret_fwd",
        grid=(nc,),
        in_specs=[qkv(2), qkv(3), qkv(4), qkv(5), rope, rope, dmat, dvec, dvec, hrow, hrow],
        out_specs=[act, act, rst],
        out_shape=[
            jax.ShapeDtypeStruct((B, S, D_MODEL), F32),
            jax.ShapeDtypeStruct((B, S, D_MODEL), _MXU),
            jax.ShapeDtypeStruct((nc, B, HEADS, DK, DK), _MXU),
        ],
        scratch_shapes=[pltpu.VMEM((B, HEADS, DK, DK), F32)],
        compiler_params=_params(("arbitrary",)),
    )(proj3, proj3, proj3, proj3, cos, sin, dmat_t, cd_t, sd_t, gam_t, gain3)
    return o_pre.reshape(T, D_MODEL), yb.reshape(T, D_MODEL), states


def _ret_bwd(dyb, o_pre, proj, states, tables, gain3, B, S, deps=()):
    T = B * S
    nc = S // CHUNK
    cos, sin, dmat_t, cd_t, sd_t, gam_t = tables

    def body(dyb_ref, o_ref, q_ref, k_ref, v_ref, gb_ref, rs_ref, cos_ref, sin_ref, dm_ref, cd_ref, sd_ref, gam_ref,
             gain_ref, dr_ref, dgain_ref, dstate_ref):
        @pl.when(pl.program_id(0) == 0)
        def _():
            dstate_ref[...] = jnp.zeros_like(dstate_ref)
            dgain_ref[...] = jnp.zeros_like(dgain_ref)

        cos_t, sin_t = cos_ref[...], sin_ref[...]
        for b, h in [(b, h) for b in range(B) for h in range(HEADS)]:
            cols = slice(h * DK, (h + 1) * DK)
            gain = gain_ref[h]
            n, rs = _groupnorm(o_ref[b, :, cols])
            gb = gb_ref[b, :, cols]
            sg = _sigmoid(gb)
            dy = dyb_ref[b, :, cols]
            part = lambda g: slice(g * D_MODEL + h * DK, g * D_MODEL + (h + 1) * DK)
            dr_ref[b, :, part(3)] = (dy * (n * gain) * (sg * (1.0 + gb * (1.0 - sg)))).astype(dr_ref.dtype)
            dgn = dy * (gb * sg)
            dgain_ref[h] += jnp.sum(dgn * n, axis=0, keepdims=True)
            dn = dgn * gain
            do = rs * (dn - jnp.mean(dn, axis=-1, keepdims=True) - n * jnp.mean(dn * n, axis=-1, keepdims=True))

            qb = _c(_rot(q_ref[b, :, cols], cos_t, sin_t))
            kb = _c(_rot(k_ref[b, :, cols], cos_t, sin_t) * (DK ** -0.5))
            v = v_ref[b, :, cols]
            vb = _c(v)
            vsb = _c(v * sd_ref[h])
            dob = _c(do)
            docb = _c(do * cd_ref[h])
            dmat = dm_ref[h]
            dstate = dstate_ref[b, h]
            dsb = _c(dstate)
            pb = _c(_dot_nt(qb, kb) * dmat)
            dsc = _c(_dot_nt(dob, vb) * dmat)
            dq = _dot(dsc, kb) + _dot_nt(docb, rs_ref[0, b, h])
            dk = _dot_tn(dsc, qb) + _dot_nt(vsb, dsb)
            dv = _dot_tn(pb, dob) + _dot(kb, dsb) * sd_ref[h]
            dstate_ref[b, h] = gam_ref[h] * dstate + _dot_tn(qb, docb)
            dr_ref[b, :, part(0)] = _rot_t(dq, cos_t, sin_t).astype(dr_ref.dtype)
            dr_ref[b, :, part(1)] = (_rot_t(dk, cos_t, sin_t) * (DK ** -0.5)).astype(dr_ref.dtype)
            dr_ref[b, :, part(2)] = dv.astype(dr_ref.dtype)

    qkv, act, rope, dmat, dvec, hrow, rst = _ret_specs(B, lambda c: nc - 1 - c)
    wide = pl.BlockSpec((B, CHUNK, 4 * D_MODEL), lambda c: (0, nc - 1 - c, 0))
    proj3 = proj.reshape(B, S, proj.shape[1])
    dr, dgain = pl.pallas_call(
        _after(body, 14, deps),
        name="ret_bwd",
        grid=(nc,),
        in_specs=[act, act, qkv(2), qkv(3), qkv(4), qkv(5), rst, rope, rope, dmat, dvec, dvec, hrow, hrow]
        + [ANY_SPEC] * len(deps),
        out_specs=[wide, hrow],
        out_shape=[jax.ShapeDtypeStruct((B, S, 4 * D_MODEL), _MXU), jax.ShapeDtypeStruct((HEADS, 1, DK), F32)],
        scratch_shapes=[pltpu.VMEM((B, HEADS, DK, DK), F32)],
        compiler_params=_params(("arbitrary",)),
    )(dyb.reshape(B, S, D_MODEL), o_pre.reshape(B, S, D_MODEL), proj3, proj3, proj3, proj3, states, cos, sin, dmat_t,
      cd_t, sd_t, gam_t, gain3, *deps)
    return dr.reshape(T, 4 * D_MODEL), dgain


def _mid(ya, yb, proj, x2d, tgt2d, wpa, wpb, wout, g_fin):
    T = x2d.shape[0]
    tm = min(MID_TILE, T)
    n_steps = T // tm
    rows = D_MODEL // (2 * N_CHIPS)

    def body(ya_ref, yb_ref, ma_ref, mb_ref, x_ref, t_ref, gf_ref, wpa_hbm, wpb_hbm, wout_hbm,
             loss_ref, dx2_ref, dya_ref, dyb_ref, dm_ref, dgf_ref, gw_hbm, w_ref, acc_ref, sem):
        i = pl.program_id(0)

        @pl.when(i == 0)
        def _():
            loads = [pltpu.make_async_copy(src, w_ref.at[k], sem.at[k]) for k, src in enumerate((wpa_hbm, wpb_hbm, wout_hbm))]
            for cp in loads:
                cp.start()
            for cp in loads:
                cp.wait()
            acc_ref[...] = jnp.zeros_like(acc_ref)
            loss_ref[...] = jnp.zeros_like(loss_ref)
            dgf_ref[...] = jnp.zeros_like(dgf_ref)

        ya_t, yb_t = ya_ref[...], yb_ref[...]
        out_a = _dot(ya_t, w_ref[0])
        out_b = _dot(yb_t, w_ref[1])
        sa = _sigmoid(ma_ref[...])
        sb = _sigmoid(mb_ref[...])
        mgb = _c(sa * out_a + sb * out_b)
        x2 = x_ref[...] + _dot(mgb, w_ref[2])
        r2 = lax.rsqrt(jnp.mean(x2 * x2, axis=-1, keepdims=True) + EPS)
        nx = x2 * r2
        gf = gf_ref[...]
        err = nx * gf - t_ref[...]
        loss_ref[...] += 0.5 * jnp.sum(jnp.mean(err * err, axis=-1, keepdims=True), axis=0, keepdims=True)
        dy = err * (1.0 / D_MODEL)
        dgf_ref[...] += jnp.sum(dy * nx, axis=0, keepdims=True)
        dyg = dy * gf
        dx2 = r2 * (dyg - nx * jnp.mean(dyg * nx, axis=-1, keepdims=True))
        dx2_ref[...] = dx2
        dx2b = _c(dx2)
        dmg = _dot_nt(dx2b, w_ref[2])
        acc_ref[2] += _dot_tn(mgb, dx2b)
        dm_ref[:, :D_MODEL] = (dmg * out_a * sa * (1.0 - sa)).astype(dm_ref.dtype)
        dm_ref[:, D_MODEL:] = (dmg * out_b * sb * (1.0 - sb)).astype(dm_ref.dtype)
        dab = _c(dmg * sa)
        dbb = _c(dmg * sb)
        dya_ref[...] = _dot_nt(dab, w_ref[0])
        dyb_ref[...] = _dot_nt(dbb, w_ref[1])
        acc_ref[0] += _dot_tn(ya_t, dab)
        acc_ref[1] += _dot_tn(yb_t, dbb)

        @pl.when(i == n_steps - 1)
        def _():
            copies = [pltpu.make_async_copy(acc_ref.at[k, pl.ds((2 * p + hf) * rows, rows), :], gw_hbm.at[p, hf, k],
                                            sem.at[(k * N_CHIPS + p) * 2 + hf])
                      for k in range(3) for p in range(N_CHIPS) for hf in range(2)]
            for cp in copies:
                cp.start()
            for cp in copies:
                cp.wait()

    tile = lambda j: pl.BlockSpec((tm, D_MODEL), lambda i: (i, j))
    one = pl.BlockSpec((1, D_MODEL), lambda i: (0, 0))
    anyspec = pl.BlockSpec(memory_space=pl.ANY)
    return pl.pallas_call(
        body,
        name="mid",
        grid=(n_steps,),
        in_specs=[tile(0), tile(0), tile(6), tile(7), tile(0), tile(0), one, anyspec, anyspec, anyspec],
        out_specs=[pl.BlockSpec((1, 1), lambda i: (0, 0)), tile(0), tile(0), tile(0),
                   pl.BlockSpec((tm, 2 * D_MODEL), lambda i: (i, 0)), one, anyspec],
        out_shape=[
            jax.ShapeDtypeStruct((1, 1), F32),
            jax.ShapeDtypeStruct((T, D_MODEL), F32),
            jax.ShapeDtypeStruct((T, D_MODEL), F32),
            jax.ShapeDtypeStruct((T, D_MODEL), F32),
            jax.ShapeDtypeStruct((T, 2 * D_MODEL), _MXU),
            jax.ShapeDtypeStruct((1, D_MODEL), F32),
            jax.ShapeDtypeStruct((N_CHIPS, 2, 3, rows, D_MODEL), F32),
        ],
        scratch_shapes=[pltpu.VMEM((3, D_MODEL, D_MODEL), _MXU), pltpu.VMEM((3, D_MODEL, D_MODEL), F32),
                        pltpu.SemaphoreType.DMA((3 * N_CHIPS * 2,))],
        compiler_params=_params(("arbitrary",)),
    )(ya, yb, proj, proj, x2d, tgt2d, g_fin, wpa, wpb, wout)


def _inproj_bwd_dx(dparts, w_all, x2d, dx2, g_in, first, count, prev, name, deps=()):
    T = x2d.shape[0]
    tm = min(DX_TILE, T)
    n_d = len(dparts)
    groups = [(a, k) for a, d in enumerate(dparts) for k in range(d.shape[1] // D_MODEL)]
    dg_start = jnp.zeros((1, D_MODEL), F32) if prev is None else prev[1]
    carried = () if prev is None else (prev[0],)

    def body(*refs):
        d_refs = refs[:n_d]
        x_ref, dx2_ref, g_ref, dg0_ref, w_hbm = refs[n_d:n_d + 5]
        dx_ref, dg_ref, w_ref, sem = refs[-4:]

        @pl.when(pl.program_id(0) == 0)
        def _():
            cp = pltpu.make_async_copy(w_hbm, w_ref, sem)
            cp.start()
            cp.wait()
            dg_ref[...] = dg0_ref[...]

        dh = jnp.zeros((tm, D_MODEL), F32)
        for j, (a, k) in enumerate(groups):
            dh = dh + _dot_nt(d_refs[a][:, k * D_MODEL:(k + 1) * D_MODEL],
                              w_ref[j // 2, :, (j % 2) * D_MODEL:(j % 2 + 1) * D_MODEL])
        x = x_ref[...]
        r = lax.rsqrt(jnp.mean(x * x, axis=-1, keepdims=True) + EPS)
        nx = x * r
        dg_ref[...] += jnp.sum(dh * nx, axis=0, keepdims=True)
        dhg = dh * g_ref[...]
        dx_ref[...] = dx2_ref[...] + r * (dhg - nx * jnp.mean(dhg * nx, axis=-1, keepdims=True))

    tile = pl.BlockSpec((tm, D_MODEL), lambda i: (first + i, 0))
    one = pl.BlockSpec((1, D_MODEL), lambda i: (0, 0))
    return pl.pallas_call(
        body,
        name=name,
        grid=(count,),
        in_specs=[pl.BlockSpec((tm, d.shape[1]), lambda i: (first + i, 0)) for d in dparts]
        + [tile, tile, one, one, ANY_SPEC] + [ANY_SPEC] * (len(carried) + len(deps)),
        out_specs=[tile, one],
        out_shape=[jax.ShapeDtypeStruct((T, D_MODEL), F32), jax.ShapeDtypeStruct((1, D_MODEL), F32)],
        input_output_aliases={n_d + 5: 0} if carried else {},
        scratch_shapes=[pltpu.VMEM(w_all.shape, w_all.dtype), pltpu.SemaphoreType.DMA],
        compiler_params=_params(("arbitrary",)),
    )(*dparts, x2d, dx2, g_in, dg_start, w_all, *carried, *deps)


def _inproj_bwd_dw(ht, dparts, name, deps=()):
    T = ht.shape[1]
    tn = DW_COLS
    half = D_MODEL // 2
    per_chip = 2 * D_MODEL // tn
    n_d = len(dparts)
    tiles = [(a, t) for a, d in enumerate(dparts) for t in range(d.shape[1] // tn)]
    offs = [sum(d.shape[1] // tn for d in dparts[:a]) for a in range(n_d)]

    def body(*refs):
        ht_ref = refs[0]
        d_refs = refs[1:1 + n_d]
        out_ref = refs[-1]
        t = pl.program_id(0)

        for a in range(n_d):
            lo, hi = offs[a], offs[a] + dparts[a].shape[1] // tn

            @pl.when((t >= lo) & (t < hi))
            def _(a=a):
                g = _dot(ht_ref[...], d_refs[a][...])
                out_ref[0, 0] = g[:half]
                out_ref[0, 1] = g[half:]

    def dspec(a):
        n_a = dparts[a].shape[1] // tn
        return pl.BlockSpec((T, tn), lambda t: (0, jnp.clip(t - offs[a], 0, n_a - 1)))

    return pl.pallas_call(
        body,
        name=name,
        grid=(len(tiles),),
        in_specs=[pl.BlockSpec((D_MODEL, T), lambda t: (0, 0))] + [dspec(a) for a in range(n_d)]
        + [ANY_SPEC] * len(deps),
        out_specs=pl.BlockSpec((1, 2, half, tn), lambda t: (t // per_chip, 0, 0, t % per_chip)),
        out_shape=jax.ShapeDtypeStruct((len(tiles) // per_chip, 2, half, 2 * D_MODEL), F32),
        compiler_params=_params(("parallel",)),
    )(ht, *dparts, *deps)


def _coords():
    return lax.axis_index("x"), lax.axis_index("y"), lax.axis_index("c")


def _other_chips(x, y):
    return [(1 - x, y), (x, 1 - y), (1 - x, 1 - y)]


def _chunks(rows, n):
    size = rows // n
    return [pl.ds(q * size, size) for q in range(n)]


HBM_SPEC = pl.BlockSpec(memory_space=pltpu.HBM)
SEM_SPEC = pl.BlockSpec(memory_space=pltpu.SEMAPHORE)
DATAFLOW = pltpu.SideEffectType.DATAFLOW_SIDE_EFFECTING


def _copies_start(bufs, plan, n_copies, name, deps=()):
    n = len(bufs)
    n_deps = len(deps)

    def body(*refs):
        ins = refs[:n]
        send_sems, recv_sems = refs[n + n_deps], refs[n + n_deps + 1]
        token = refs[-1]
        for k, send, _ in plan(ins):
            if send is not None:
                src, dst, dev, pred = send
                cp = pltpu.make_async_remote_copy(src_ref=src, dst_ref=dst, send_sem=send_sems.at[k],
                                                  recv_sem=recv_sems.at[k], device_id=dev, device_id_type=MESH)
                if pred is None:
                    cp.start()
                else:
                    pl.when(pred)(cp.start)
        token[...] = jnp.zeros_like(token)

    hbm = [pltpu.with_memory_space_constraint(b, pltpu.HBM) for b in bufs]
    outs = pl.pallas_call(
        body,
        name=name,
        in_specs=[HBM_SPEC] * n + [ANY_SPEC] * n_deps,
        out_specs=(SEM_SPEC, SEM_SPEC, *([HBM_SPEC] * n), pl.BlockSpec(memory_space=pltpu.VMEM)),
        out_shape=(pltpu.SemaphoreType.DMA((n_copies,)), pltpu.SemaphoreType.DMA((n_copies,)),
                   *[pltpu.HBM(b.shape, b.dtype) for b in bufs], jax.ShapeDtypeStruct((8, 128), F32)),
        input_output_aliases={a: 2 + a for a in range(n)},
        compiler_params=pltpu.CompilerParams(has_side_effects=DATAFLOW),
    )(*hbm, *deps)
    return outs[0], outs[1], list(outs[2:2 + n]), outs[-1]


def _copies_wait(send_sems, recv_sems, bufs, after, plan, name, only=None):
    n = len(bufs)

    def body(*refs):
        ins = refs[:n]
        s_sems, r_sems = refs[n], refs[n + 1]
        for k, send, recv in plan(ins):
            if only is not None and k not in only:
                continue
            if send is not None:
                src, dst, dev, pred = send
                cp = pltpu.make_async_remote_copy(src_ref=src, dst_ref=dst, send_sem=s_sems.at[k],
                                                  recv_sem=r_sems.at[k], device_id=dev, device_id_type=MESH)
                if pred is None:
                    cp.wait_send()
                else:
                    pl.when(pred)(cp.wait_send)
            if recv is not None:
                dst, pred = recv
                cp = pltpu.make_async_remote_copy(src_ref=dst, dst_ref=dst, send_sem=s_sems.at[k],
                                                  recv_sem=r_sems.at[k], device_id=_coords(), device_id_type=MESH)
                if pred is None:
                    cp.wait_recv()
                else:
                    pl.when(pred)(cp.wait_recv)

    outs = pl.pallas_call(
        body,
        name=name,
        in_specs=[HBM_SPEC] * n + [SEM_SPEC, SEM_SPEC, pl.BlockSpec(memory_space=pl.ANY)],
        out_specs=[HBM_SPEC] * n,
        out_shape=[pltpu.HBM(b.shape, b.dtype) for b in bufs],
        input_output_aliases={a: a for a in range(n)},
        compiler_params=pltpu.CompilerParams(has_side_effects=DATAFLOW),
    )(*bufs, send_sems, recv_sems, after)
    return list(outs)


def _gather_plan(n_bufs):
    def plan(refs):
        x, y, c = _coords()
        me = 2 * x + y
        out = []
        for k, (px, py) in enumerate(_other_chips(x, y)):
            for a in range(n_bufs):
                out.append((k * n_bufs + a, (refs[a].at[me], refs[a].at[me], (px, py, c), None),
                            (refs[a].at[2 * px + py], None)))
        return out
    return plan


def _cast_into_slot(ws, name, deps=()):
    n = len(ws)
    nt = 2

    def body(s_ref, *refs):
        outs = refs[len(refs) - n:]
        for a in range(n):
            outs[a][0] = refs[a][...].astype(outs[a].dtype)

    xi, yi, _ = _coords()
    return pl.pallas_call(
        body,
        name=name,
        grid_spec=pltpu.PrefetchScalarGridSpec(
            num_scalar_prefetch=1,
            grid=(2, nt),
            in_specs=[pl.BlockSpec((1, w.shape[1] // nt, w.shape[2]), lambda hf, i, s: (hf, i, 0)) for w in ws]
            + [ANY_SPEC] * len(deps),
            out_specs=[pl.BlockSpec((1, 1, w.shape[1] // nt, w.shape[2]), lambda hf, i, s: (s[0], hf, i, 0)) for w in ws],
        ),
        out_shape=[jax.ShapeDtypeStruct((N_CHIPS,) + w.shape, _MXU) for w in ws],
        compiler_params=_params(("parallel", "parallel")),
    )((2 * xi + yi).reshape(1).astype(jnp.int32), *ws, *deps)


def _chip_gather_plan(stage, n_bufs):
    def plan(refs):
        x, y, c = _coords()
        me = 2 * x + y
        near = [(1 - x, y), (x, 1 - y)]
        slots = [2 * (1 - x) + y, 2 * x + (1 - y), 2 * (1 - x) + (1 - y)]
        sibling = (x, y, 1 - c)
        pass_to = (jnp.where(c == 0, x, 1 - x), jnp.where(c == 0, 1 - y, y), c)
        pass_slot = jnp.where(c == 0, slots[0], slots[1])
        out = []

        def move(src_slot, to, land_slot, land_core, pieces):
            for a, buf in enumerate(refs):
                for rows in _chunks(buf.shape[2], pieces[a]):
                    out.append((len(out), (buf.at[src_slot, c, rows], buf.at[src_slot, c, rows], to, None),
                                (buf.at[land_slot, land_core, rows], None)))

        if stage == "near":
            for k, chip in enumerate(near):
                move(me, (*chip, c), slots[k], c, NEAR_PIECES[:n_bufs])
        elif stage == "pass":
            move(pass_slot, pass_to, slots[2], c, PASS_PIECES[:n_bufs])
            for k in range(2):
                move(slots[k], sibling, slots[k], 1 - c, [1] * n_bufs)
        else:
            move(slots[2], sibling, slots[2], 1 - c, [1] * n_bufs)
        return out
    return plan


NEAR_PIECES = (2, 1)
PASS_PIECES = (2, 1)


def _chip_gather_copies(stage, n_bufs):
    if stage == "near":
        return 2 * sum(NEAR_PIECES[:n_bufs]), None
    if stage == "pass":
        n_pass = sum(PASS_PIECES[:n_bufs])
        return n_pass + 2 * n_bufs, set(range(n_pass))
    return n_bufs, None


def _swap_plan(n_slabs):
    def plan(refs):
        x, y, c = _coords()
        out, k = [], 0
        for i, n in enumerate(n_slabs):
            g, land = refs[2 * i], refs[2 * i + 1]
            for p in range(n):
                out.append((k, (g.at[p, 1 - c], land.at[p], (x, y, 1 - c), None), (land.at[p], None)))
                k += 1
        return out
    return plan


def _is_one_of(chip, dests):
    hit = chip == dests[0]
    for d in dests[1:]:
        hit = hit | (chip == d)
    return hit


def _slab_of(chip, dests):
    return sum(j * (chip == d).astype(jnp.int32) for j, d in enumerate(dests))


def _scatter_plan(dest_sets):
    def plan(refs):
        x, y, c = _coords()
        me = 2 * x + y
        out = []
        for k, (px, py) in enumerate(_other_chips(x, y)):
            peer = 2 * px + py
            for i, dests in enumerate(dest_sets):
                cs, land = refs[2 * i], refs[2 * i + 1]
                everyone = len(dests) == N_CHIPS
                send = (cs.at[_slab_of(peer, dests)], land.at[k], (px, py, c),
                        None if everyone else _is_one_of(peer, dests))
                recv = (land.at[k], None if everyone else _is_one_of(me, dests))
                out.append((k * len(dest_sets) + i, send, recv))
        return out
    return plan


def _join_plan(rows, n_pieces):
    def plan(refs):
        x, y, c = _coords()
        (buf,) = refs
        return [(i, (buf.at[c, piece], buf.at[c, piece], (x, y, 1 - c), None), (buf.at[1 - c, piece], None))
                for i, piece in enumerate(_chunks(rows, n_pieces))]
    return plan


def _join_plans(parts):
    def plan(refs):
        out, b0, k0 = [], 0, 0
        for part_plan, n_bufs, n_copies in parts:
            out += [(k0 + k, send, recv) for k, send, recv in part_plan(refs[b0:b0 + n_bufs])]
            b0 += n_bufs
            k0 += n_copies
        return out
    return plan


def _allgather_plan():
    def plan(refs):
        x, y, c = _coords()
        (land,) = refs
        me = 4 * x + 2 * y + c
        out = []
        for r in range(1, 8):
            px = 1 - x if r & 4 else x
            py = 1 - y if r & 2 else y
            pc = 1 - c if r & 1 else c
            out.append((r - 1, (land.at[me], land.at[me], (px, py, pc), None), (land.at[4 * px + 2 * py + pc], None)))
        return out
    return plan


def _sum_gathered(land, name):
    def body(land_ref, o_ref):
        acc = land_ref[0]
        for d in range(1, 8):
            acc = acc + land_ref[d]
        o_ref[...] = acc

    return pl.pallas_call(
        body,
        name=name,
        out_shape=jax.ShapeDtypeStruct(land.shape[1:], F32),
        compiler_params=_params(),
    )(land)


def _join_halves(bufs, n_chunks, name, deps=()):
    n = len(bufs)
    pieces = [(a, rows) for a in range(n) for rows in _chunks(bufs[a].shape[1], n_chunks[a])]
    n_p = len(pieces)

    def body(*refs):
        outs = refs[-n - 2:-2]
        send_sems, recv_sems = refs[-2:]
        x, y, c = _coords()

        def copy(i, half):
            a, rows = pieces[i]
            return pltpu.make_async_remote_copy(
                src_ref=outs[a].at[half, rows], dst_ref=outs[a].at[half, rows], send_sem=send_sems.at[i],
                recv_sem=recv_sems.at[i], device_id=(x, y, 1 - c), device_id_type=MESH)

        sends = [copy(i, c) for i in range(n_p)]
        for cp in sends:
            cp.start()
        for i in range(n_p):
            copy(i, 1 - c).wait_recv()
        for cp in sends:
            cp.wait_send()

    anyspec = pl.BlockSpec(memory_space=pl.ANY)
    sems = pltpu.SemaphoreType.DMA((n_p,))
    return pl.pallas_call(
        body,
        name=name,
        in_specs=[anyspec] * (n + len(deps)),
        out_specs=[anyspec] * n,
        out_shape=[jax.ShapeDtypeStruct(b.shape, b.dtype) for b in bufs],
        input_output_aliases={a: a for a in range(n)},
        scratch_shapes=[sems, sems],
    )(*bufs, *deps)


def _row_tile(rows, cap):
    t = cap
    while rows % t:
        t //= 2
    return t


def _add_my_half(g, r, name):
    n_slabs, _, R, C = g.shape
    tr = R if n_slabs > 1 else _row_tile(R, SUM_ROWS)

    def body(c_ref, g_ref, r_ref, o_ref):
        o_ref[...] = (g_ref[0] + r_ref[...]).astype(o_ref.dtype)

    return pl.pallas_call(
        body,
        name=name,
        grid_spec=pltpu.PrefetchScalarGridSpec(
            num_scalar_prefetch=1,
            grid=(n_slabs, R // tr),
            in_specs=[pl.BlockSpec((1, 1, tr, C), lambda p, i, c_ref: (p, c_ref[0], i, 0)),
                      pl.BlockSpec((1, tr, C), lambda p, i, c_ref: (p, i, 0))],
            out_specs=pl.BlockSpec((1, tr, C), lambda p, i, c_ref: (p, i, 0)),
        ),
        out_shape=jax.ShapeDtypeStruct(r.shape, jnp.bfloat16),
        compiler_params=_params(("parallel", "parallel")),
    )(lax.axis_index("c").reshape(1).astype(jnp.int32), g, r)


def _sum_slabs(own, got, name, deps=()):
    _, R, C = own.shape
    tr = _row_tile(R, SUM_ROWS)

    def body(s_ref, own_ref, got_ref, *rest):
        rest[-1][0] = ((own_ref[0].astype(F32) + got_ref[0].astype(F32)) + got_ref[1].astype(F32)) + got_ref[2].astype(F32)

    xi, yi, ci = _coords()
    return pl.pallas_call(
        body,
        name=name,
        grid_spec=pltpu.PrefetchScalarGridSpec(
            num_scalar_prefetch=1,
            grid=(R // tr,),
            in_specs=[pl.BlockSpec((1, tr, C), lambda i, s: (s[0], i, 0)),
                      pl.BlockSpec((3, tr, C), lambda i, s: (0, i, 0))] + [ANY_SPEC] * len(deps),
            out_specs=pl.BlockSpec((1, tr, C), lambda i, s: (s[1], i, 0)),
        ),
        out_shape=jax.ShapeDtypeStruct((2, R, C), F32),
        compiler_params=_params(("parallel",)),
    )(jnp.stack([2 * xi + yi, ci]).astype(jnp.int32), own, got, *deps)


def _sum_parts(owns, got, dest_sets, name):
    n = len(owns)
    _, R, C = owns[0].shape
    tr = _row_tile(R, SUM_ROWS)

    def body(s_ref, *refs):
        got_ref, o_ref = refs[n], refs[-1]
        total = jnp.zeros((tr, C), F32)
        for i in range(n):
            total = total + jnp.where(s_ref[2 + 2 * i] == 1, refs[i][0].astype(F32), 0.0)
        o_ref[0] = ((total + got_ref[0].astype(F32)) + got_ref[1].astype(F32)) + got_ref[2].astype(F32)

    xi, yi, ci = _coords()
    me = 2 * xi + yi
    scalars = [ci, ci]
    for dests in dest_sets:
        scalars += [_is_one_of(me, dests).astype(jnp.int32), _slab_of(me, dests)]
    own_spec = lambda i: pl.BlockSpec((1, tr, C), lambda r, s: (s[3 + 2 * i], r, 0))
    return pl.pallas_call(
        body,
        name=name,
        grid_spec=pltpu.PrefetchScalarGridSpec(
            num_scalar_prefetch=1,
            grid=(R // tr,),
            in_specs=[own_spec(i) for i in range(n)] + [pl.BlockSpec((3, tr, C), lambda r, s: (0, r, 0))],
            out_specs=pl.BlockSpec((1, tr, C), lambda r, s: (s[0], r, 0)),
        ),
        out_shape=jax.ShapeDtypeStruct((2, R, C), F32),
        compiler_params=_params(("parallel",)),
    )(jnp.stack(scalars).astype(jnp.int32), *owns, got)


def _adamw_math(w, g, m, v):
    m = ADAM_B1 * m + (1.0 - ADAM_B1) * g
    v = ADAM_B2 * v + (1.0 - ADAM_B2) * (g * g)
    m_hat = m / (1.0 - ADAM_B1 ** ADAM_STEP)
    v_hat = v / (1.0 - ADAM_B2 ** ADAM_STEP)
    delta = -ADAM_LR * (m_hat / (jnp.sqrt(v_hat) + ADAM_EPS) + ADAM_WD * w)
    return delta, m, v


def _adamw_halves(ws, g, ms, vs, half, prev, name, deps=()):
    n = len(ws)
    _, _, R, C = g.shape
    tr = _row_tile(R, ADAMW_ROWS)
    steps = R // tr
    carried = [] if prev is None else [a for four in prev for a in four]
    both = half is None
    which = (lambda i, s: i // steps) if both else (lambda i, s: s[0])
    half = 0 if both else half

    def body(s_ref, *refs):
        w_refs, g_refs, m_refs, v_refs = (refs[k * n:(k + 1) * n] for k in range(4))
        outs = refs[len(refs) - 4 * n:]
        for a in range(n):
            grad = g_refs[a][0, 0]
            d, mn, vn = _adamw_math(w_refs[a][...], grad, m_refs[a][...], v_refs[a][...])
            for o, val in zip(outs[4 * a:4 * a + 4], (grad, d, mn, vn)):
                o[...] = val

    rows = pl.BlockSpec((tr, C), lambda i, s: (which(i, s) * steps + i % steps, 0))
    grad_spec = lambda a: pl.BlockSpec((1, 1, tr, C), lambda i, s: (which(i, s), a, i % steps, 0))
    n_in = 4 * n
    outs = pl.pallas_call(
        body,
        name=name,
        grid_spec=pltpu.PrefetchScalarGridSpec(
            num_scalar_prefetch=1,
            grid=(2 * steps if both else steps,),
            in_specs=[rows] * n + [grad_spec(a) for a in range(n)] + [rows] * (2 * n)
            + [ANY_SPEC] * (len(carried) + len(deps)),
            out_specs=[rows] * (4 * n),
        ),
        out_shape=[jax.ShapeDtypeStruct((2 * R, C), F32)] * (4 * n),
        input_output_aliases={1 + n_in + k: k for k in range(len(carried))},
        compiler_params=_params(("parallel",)),
    )(jnp.reshape(half, (1,)).astype(jnp.int32), *ws, *([g] * n), *ms, *vs, *carried, *deps)
    return [outs[4 * a:4 * a + 4] for a in range(n)]


def _adamw_small(ws, gs, ms, vs, name):
    n = len(ws)

    def body(*refs):
        for a in range(n):
            d, mn, vn = _adamw_math(refs[a][...], refs[n + a][...], refs[2 * n + a][...], refs[3 * n + a][...])
            refs[4 * n + a][...] = d
            refs[5 * n + a][...] = mn
            refs[6 * n + a][...] = vn

    shapes = [jax.ShapeDtypeStruct(w.shape, F32) for w in ws]
    outs = pl.pallas_call(
        body,
        name=name,
        out_shape=shapes * 3,
        compiler_params=_params(),
    )(*ws, *gs, *ms, *vs)
    return outs[:n], outs[n:2 * n], outs[2 * n:]


def _to_blockdiag(w):
    per = CW // LRU_BW
    w4 = w.reshape(N_CT, per, LRU_BW, LRU_BW)
    eye = jnp.eye(per, dtype=w.dtype)
    return (w4[:, :, :, None, :] * eye[None, :, None, :, None]).reshape(N_CT, CW, CW)


def _from_blockdiag(g):
    per = CW // LRU_BW
    g5 = g.reshape(N_CT, per, LRU_BW, per, LRU_BW)
    return jnp.stack([g5[:, b, :, b, :] for b in range(per)], axis=1).reshape(LRU_BLOCKS, LRU_BW, LRU_BW)


def _local_grads(x2d, tgt2d, B, S, g_in, in_proj, conv_b, gate_x_w, gate_x_b, gate_a_w, gate_a_b, lam,
                 proj_weights, g_fin, reduce):
    wx_bd = _c(_to_blockdiag(gate_x_w))
    wa_bd = _c(_to_blockdiag(gate_a_w))
    tables = _retention_tables(S)

    proj, ht, w_all, conv_w, gain = in_proj(x2d, g_in, (tables[0], tables[1], wx_bd, wa_bd))
    gain3 = gain.reshape(HEADS, 1, DK)
    hlru, ya = _lru_fwd(proj, conv_w, conv_b, wx_bd, wa_bd, gate_x_b, gate_a_b, lam, B, S)
    o_pre, yb, states = _ret_fwd(proj, tables, gain3, B, S)
    wpa, wpb, wout = proj_weights(yb)
    loss, dx2, dya, dyb, dm, dgf, gw_proj = _mid(ya, yb, proj, x2d, tgt2d, wpa, wpb, wout, g_fin)
    g3 = _inproj_bwd_dw(ht, [dm], "inproj_bwd_dw_m")
    deps = reduce.m_ready(gw_proj, g3)
    dr, dgain = _ret_bwd(dyb, o_pre, proj, states, tables, gain3, B, S, deps)
    deps = reduce.ret_done(dr)
    g12 = _inproj_bwd_dw(ht, [dr], "inproj_bwd_dw_r", deps)
    deps = reduce.r_ready(g12)
    dxa, dga, dcw, dcb, dwx_bd, dwa_bd, dbx, dba, dlam = _lru_bwd(
        dya, proj, hlru, conv_w, conv_b, wx_bd, wa_bd, gate_x_b, gate_a_b, lam, B, S, deps)
    small = dict(conv_w=dcw, conv_b=dcb, gate_x_w=_from_blockdiag(dwx_bd), gate_x_b=dbx,
                 gate_a_w=_from_blockdiag(dwa_bd), gate_a_b=dba, lru_lambda=dlam, gn_gain=dgain.reshape(HEADS, DK),
                 norm_final=dgf)
    loss_rows = jnp.broadcast_to(loss, (SUBLANES, LANES))
    deps = reduce.lru_done(dxa, jnp.concatenate([_pack_small(small), loss_rows], axis=0))
    g0 = _inproj_bwd_dw(ht, [dxa, dga], "inproj_bwd_dw_a", deps)
    deps = reduce.a_ready(g0)
    n_tiles = x2d.shape[0] // min(DX_TILE, x2d.shape[0])
    grad_x, dgin = _inproj_bwd_dx([dxa, dga, dr, dm], w_all, x2d, dx2, g_in, 0, n_tiles, None, "inproj_bwd_dx", deps)
    return grad_x, dgin


ALL_CHIPS = (0, 1, 2, 3)


class _GradReduce:
    def __init__(self, proj_done):
        self.pending = {}
        self.proj_done = proj_done
        self.land_in = None

    def _start(self, key, parts, name):
        bufs, plans, shared = [], [], None
        for part_bufs, plan, n_copies, part_shared in parts:
            if part_shared is not None:
                shared = len(bufs) + part_shared
            plans.append((plan, len(part_bufs), n_copies))
            bufs += part_bufs
        plan = _join_plans(plans)
        send_sems, recv_sems, bufs, token = _copies_start(bufs, plan, sum(p[2] for p in plans), name + "_start")
        if shared is not None:
            self.land_in = bufs[shared]
        self.pending[key] = (send_sems, recv_sems, bufs, plan, name + "_wait", shared)
        return (token,)

    def _finish(self, key, after):
        send_sems, recv_sems, bufs, plan, name, shared = self.pending.pop(key)
        if shared is not None:
            bufs[shared] = self.land_in
        bufs = _copies_wait(send_sems, recv_sems, bufs, after, plan, name)
        if shared is not None:
            self.land_in = bufs[shared]
        return bufs

    @staticmethod
    def _swap(pieces):
        bufs = []
        for g in pieces:
            bufs += [g, lax.empty((g.shape[0],) + g.shape[2:], F32)]
        n_slabs = [g.shape[0] for g in pieces]
        return bufs, _swap_plan(n_slabs), sum(n_slabs), None

    def _scatter(self, sums, dest_sets):
        bufs = []
        for cs in sums:
            bufs += [cs, lax.empty((3,) + cs.shape[1:], cs.dtype)]
        if self.land_in is not None:
            bufs[-1] = self.land_in
        return bufs, _scatter_plan(dest_sets), 3 * len(sums), len(bufs) - 1

    @staticmethod
    def _gather8(block):
        x, y, c = _coords()
        land = lax.dynamic_update_slice(lax.empty((8,) + block.shape, F32), block[None], (4 * x + 2 * y + c, 0, 0))
        return [land], _allgather_plan(), 7, None

    def m_ready(self, gw_proj, g3):
        rows = gw_proj.shape[2] * gw_proj.shape[3]
        return self._start("m", [self._swap([gw_proj.reshape(N_CHIPS, 2, rows, D_MODEL), g3])], "swap_m")

    def ret_done(self, after):
        proj, land_p, g3, land_3 = self._finish("m", after)
        sums_m = [_add_my_half(proj, land_p, "chip_sum_proj"), _add_my_half(g3, land_3, "chip_sum_m")]
        return self._start("sm", [self._scatter(sums_m, [ALL_CHIPS, (3,)])], "scatter_m")

    def r_ready(self, g12):
        return self._start("r", [self._swap([g12])], "swap_r")

    def lru_done(self, after, packed):
        g12, land_12 = self._finish("r", after)
        sums_r = [_add_my_half(g12, land_12, "chip_sum_r")]
        return (self._start("sr", [self._scatter(sums_r, [(1, 2)])], "scatter_r")
                + self._start("small", [self._gather8(packed)], "gather_small"))

    def a_ready(self, g0):
        (token,) = self._start("a", [self._swap([g0])], "swap_a")
        csp, gotp, self.cs3, _ = self._finish("sm", token)
        half_proj = _sum_slabs(csp, gotp, "sum_w_proj")
        g0, land_0 = self._finish("a", half_proj)
        deps = self._start("sa", [self._scatter([_add_my_half(g0, land_0, "chip_sum_a")], [(0,)])], "scatter_a")
        self.proj_done(_join_halves([half_proj], [4], "join_halves_proj", deps)[0])
        return deps

    def finish(self, dgin, w_in_done):
        (token,) = self._start("n", [self._gather8(dgin)], "gather_norm_in")
        (small,) = self._finish("small", token)
        cs12, _ = self._finish("sr", token)
        cs0, _ = self._finish("sa", token)
        half_in = _sum_parts([self.cs3, cs12, cs0], self.land_in, [(3,), (1, 2), (0,)], "sum_w_in")
        deps = self._start("j", [([half_in], _join_plan(half_in.shape[1], JOIN_PIECES), JOIN_PIECES, None)], "join_w_in")
        first = w_in_done(self.pending["j"][2][0], True, None, deps)
        (g_in,) = self._finish("j", first[1])
        done = w_in_done(g_in, False, first, ())
        (norm_in,) = self._finish("n", done[1])
        return _sum_gathered(small, "sum_small_grads"), _sum_gathered(norm_in, "sum_norm_in_grad")


_SMALL = ("gate_x_w", "gate_a_w", "conv_w", "conv_b", "gate_x_b", "gate_a_b", "lru_lambda", "gn_gain", "norm_final")
_SMALL_SHAPES = dict(gate_x_w=(LRU_BLOCKS, LRU_BW, LRU_BW), gate_a_w=(LRU_BLOCKS, LRU_BW, LRU_BW),
                     norm_in=(1, D_MODEL), conv_w=(CONV, D_MODEL), conv_b=(1, D_MODEL), gate_x_b=(1, D_MODEL),
                     gate_a_b=(1, D_MODEL), lru_lambda=(1, D_MODEL), gn_gain=(HEADS, DK), norm_final=(1, D_MODEL))


def _pack_small(small):
    return jnp.concatenate([small[k].reshape(-1, 128) for k in _SMALL], axis=0)


def _unpack_small(packed):
    out, r = {}, 0
    for k in _SMALL:
        shape = _SMALL_SHAPES[k]
        rows = 1
        for s in shape:
            rows *= s
        rows //= 128
        out[k] = packed[r:r + rows].reshape(shape)
        r += rows
    return out


def kernel(x, norm_in, w_in, conv_w, conv_b, gate_x_w, gate_x_b, gate_a_w, gate_a_b, lru_lambda, gn_gain, w_proj_a, w_proj_b, w_out, norm_final, loss_target, m_norm_in, m_w_in, m_conv_w, m_conv_b, m_gate_x_w, m_gate_x_b, m_gate_a_w, m_gate_a_b, m_lru_lambda, m_gn_gain, m_w_proj_a, m_w_proj_b, m_w_out, m_norm_final, v_norm_in, v_w_in, v_conv_w, v_conv_b, v_gate_x_w, v_gate_x_b, v_gate_a_w, v_gate_a_b, v_lru_lambda, v_gn_gain, v_w_proj_a, v_w_proj_b, v_w_out, v_norm_final):
    B, S, _ = x.shape
    T = B * S
    xi, yi, ci = _coords()
    chip = 2 * xi + yi

    cshard = D_MODEL // N_CHIPS
    mine = _cast_into_slot([w_in[0].reshape(2, D_MODEL // 2, 2 * D_MODEL)], "cast_w_in")
    plan = _gather_plan(3)
    pending_proj = []
    gshard = DK // N_CHIPS
    tiny = jnp.concatenate([conv_w[0], jnp.zeros((4, cshard), F32), jnp.pad(gn_gain[0], ((0, 4), (0, cshard - gshard)))],
                           axis=0).reshape(1, 2, SUBLANES, cshard)
    tiny_buf = lax.dynamic_update_slice(lax.empty((N_CHIPS, 2, SUBLANES, cshard), F32), tiny, (chip, 0, 0, 0))
    near_plan, pass_plan, far_plan = (_chip_gather_plan(stage, 2) for stage in ("near", "pass", "far"))
    (n_near, _), (n_pass, passed_on), (n_far, _) = (_chip_gather_copies(stage, 2) for stage in ("near", "pass", "far"))
    halves = set(range(n_pass)) - passed_on
    near_s, near_r, bufs, near_token = _copies_start([mine[0], tiny_buf], near_plan, n_near, "gather_near_start")

    def in_proj(x2d, g_in, meanwhile):
        mine_proj = _cast_into_slot([w[0].reshape(2, cshard // 2, D_MODEL) for w in (w_proj_a, w_proj_b, w_out)],
                                    "cast_w_proj", (near_token,))
        as_w = lambda b: b[0].reshape(N_CHIPS, D_MODEL, 2 * D_MODEL)
        slot_x, slot_y, slot_d = 2 * (1 - xi) + yi, 2 * xi + (1 - yi), 2 * (1 - xi) + (1 - yi)
        ids = lambda *chips: jnp.stack(chips).astype(jnp.int32)
        proj, hb, ht = _inproj_first(x2d, g_in, as_w(bufs), ids(chip), "inproj_own", (near_token, *meanwhile))
        got = _copies_wait(near_s, near_r, bufs, proj, near_plan, "gather_near_wait")
        pass_s, pass_r, got, pass_token = _copies_start(got, pass_plan, n_pass, "gather_pass_start")
        got = _copies_wait(pass_s, pass_r, got, pass_token, pass_plan, "gather_pass_wait_halves", only=halves)
        proj = _inproj_more(hb, as_w(got), ids(slot_x, slot_y), proj, "inproj_near")
        got = _copies_wait(pass_s, pass_r, got, proj, pass_plan, "gather_pass_wait_far", only=passed_on)
        pending_proj.append(_copies_start(mine_proj, plan, 9, "gather_proj_start", (got[0],)))
        far_s, far_r, got, far_token = _copies_start(got, far_plan, n_far, "gather_far_start")
        got = _copies_wait(far_s, far_r, got, far_token, far_plan, "gather_far_wait")
        proj = _inproj_more(hb, as_w(got), ids(slot_d), proj, "inproj_far")
        tiny_all = got[1].reshape(N_CHIPS, 2 * SUBLANES, cshard)
        conv_w_full = jnp.transpose(tiny_all[:, 0:CONV, :], (1, 0, 2)).reshape(CONV, D_MODEL)
        gain_full = jnp.transpose(tiny_all[:, 8:8 + HEADS, :gshard], (1, 0, 2)).reshape(HEADS, DK)
        return proj, ht, as_w(got), conv_w_full, gain_full

    def proj_weights(after):
        s_sems, r_sems, pbufs, _ = pending_proj[0]
        got = _copies_wait(s_sems, r_sems, pbufs, after, plan, "gather_proj_wait")
        return [b.reshape(D_MODEL, D_MODEL) for b in got]

    weights = dict(norm_in=norm_in, w_in=w_in, conv_w=conv_w, conv_b=conv_b, gate_x_w=gate_x_w, gate_x_b=gate_x_b,
                   gate_a_w=gate_a_w, gate_a_b=gate_a_b, lru_lambda=lru_lambda, gn_gain=gn_gain, w_proj_a=w_proj_a,
                   w_proj_b=w_proj_b, w_out=w_out, norm_final=norm_final)
    ms = dict(norm_in=m_norm_in, w_in=m_w_in, conv_w=m_conv_w, conv_b=m_conv_b, gate_x_w=m_gate_x_w,
              gate_x_b=m_gate_x_b, gate_a_w=m_gate_a_w, gate_a_b=m_gate_a_b, lru_lambda=m_lru_lambda, gn_gain=m_gn_gain,
              w_proj_a=m_w_proj_a, w_proj_b=m_w_proj_b, w_out=m_w_out, norm_final=m_norm_final)
    vs = dict(norm_in=v_norm_in, w_in=v_w_in, conv_w=v_conv_w, conv_b=v_conv_b, gate_x_w=v_gate_x_w,
              gate_x_b=v_gate_x_b, gate_a_w=v_gate_a_w, gate_a_b=v_gate_a_b, lru_lambda=v_lru_lambda, gn_gain=v_gn_gain,
              w_proj_a=v_w_proj_a, w_proj_b=v_w_proj_b, w_out=v_w_out, norm_final=v_norm_final)
    names = list(weights)
    grads, delta, new_m, new_v = {}, {}, {}, {}

    def update_big(keys, g, half, prev, name, deps=()):
        two = lambda a: a.reshape(a.shape[1], a.shape[2])
        res = _adamw_halves([two(weights[k]) for k in keys], g, [two(ms[k]) for k in keys], [two(vs[k]) for k in keys],
                            half, prev, name, deps)
        for k, (gk, d, mn, vn) in zip(keys, res):
            shp = weights[k].shape
            grads[k], delta[k], new_m[k], new_v[k] = gk.reshape(shp), d.reshape(shp), mn.reshape(shp), vn.reshape(shp)
        return res

    def proj_done(g_proj):
        g4 = g_proj.reshape(2, 3, D_MODEL // (2 * N_CHIPS), D_MODEL)
        return update_big(("w_proj_a", "w_proj_b", "w_out"), g4, None, None, "adamw_proj")[-1][1]

    def w_in_done(g_in, own, prev, deps):
        g4 = g_in.reshape(2, 1, D_MODEL // 2, 2 * D_MODEL)
        return update_big(("w_in",), g4, ci if own else 1 - ci, None if prev is None else [prev],
                          "adamw_w_in_own" if own else "adamw_w_in_other", deps)[0]

    reduce = _GradReduce(proj_done)
    grad_x, dgin = _local_grads(
        x.reshape(T, D_MODEL), loss_target.reshape(T, D_MODEL), B, S, norm_in, in_proj, conv_b,
        gate_x_w[0], gate_x_b, gate_a_w[0], gate_a_b, lru_lambda, proj_weights,
        norm_final.reshape(1, D_MODEL), reduce)

    small_sum, g_norm_in = reduce.finish(dgin.reshape(SUBLANES, LANES), w_in_done)
    loss = small_sum[small_sum.shape[0] - SUBLANES, 0]

    gsm = _unpack_small(small_sum)
    gsm["norm_in"] = g_norm_in
    gsm["conv_w"] = lax.dynamic_slice_in_dim(gsm["conv_w"], chip * cshard, cshard, axis=1)
    gsm["gn_gain"] = lax.dynamic_slice_in_dim(gsm["gn_gain"], chip * gshard, gshard, axis=1)
    smalls = [k for k in names if k not in delta]

    def view(a):
        return a.reshape(1, -1) if a.ndim == 1 else (a.reshape(a.shape[1:]) if a.ndim > 2 else a)

    ds, mns, vns = _adamw_small([view(weights[k]) for k in smalls], [gsm[k].reshape(view(weights[k]).shape) for k in smalls],
                                [view(ms[k]) for k in smalls], [view(vs[k]) for k in smalls], "adamw_small")
    for k, d, mn, vn in zip(smalls, ds, mns, vns):
        shp = weights[k].shape
        grads[k], delta[k], new_m[k], new_v[k] = gsm[k].reshape(shp), d.reshape(shp), mn.reshape(shp), vn.reshape(shp)

    return (loss, grad_x.reshape(B, S, D_MODEL), *[grads[k] for k in names], *[delta[k] for k in names],
            *[new_m[k] for k in names], *[new_v[k] for k in names])
```

```python
import jax
import jax.numpy as jnp
from jax import lax
from jax.experimental import pallas as pl
from jax.experimental.pallas import tpu as pltpu

F32 = jnp.float32
_MXU = jnp.bfloat16

D_MODEL = 1024
N_GROUPS = 8
HEADS = 4
DK = 256
CHUNK = 128
CONV = 4
LRU_BLOCKS = 16
LRU_BW = 64
LRU_C = 8.0
ROPE_THETA = 10000.0
EPS = 1e-6
CW = 256
N_CT = D_MODEL // CW
N_CHIPS = 4
MESH = pl.DeviceIdType.MESH

ADAM_LR = 0.001
ADAM_B1 = 0.9
ADAM_B2 = 0.999
ADAM_EPS = 1e-08
ADAM_WD = 0.01
ADAM_STEP = 10

VMEM_LIMIT = 56 * 1024 * 1024

FIRST_PROJ_TILE = 1024
MORE_PROJ_TILE = 2048
SCAN_TILE = 512
MID_TILE = 256
DX_TILE = 512
DW_COLS = 512
SUM_ROWS = 256
ADAMW_ROWS = 256
JOIN_PIECES = 8


def _c(v):
    return v.astype(_MXU)


def _dot(a, b):
    return lax.dot_general(a, b, (((1,), (0,)), ((), ())), preferred_element_type=F32)


def _dot_nt(a, b):
    return lax.dot_general(a, b, (((1,), (1,)), ((), ())), preferred_element_type=F32)


def _dot_tn(a, b):
    return lax.dot_general(a, b, (((0,), (0,)), ((), ())), preferred_element_type=F32)


def _sigmoid(z):
    return 0.5 * jnp.tanh(0.5 * z) + 0.5


ANY_SPEC = pl.BlockSpec(memory_space=pl.ANY)


def _after(body, n_in, deps):
    n_deps = len(deps)

    def wrapped(*refs):
        return body(*refs[:n_in], *refs[n_in + n_deps:])

    return wrapped


def _params(sem=None):
    if sem is None:
        return pltpu.CompilerParams(vmem_limit_bytes=VMEM_LIMIT)
    return pltpu.CompilerParams(vmem_limit_bytes=VMEM_LIMIT, dimension_semantics=sem)


def _inproj_first(x2d, g_in, w_all, chips, name, deps=()):
    T = x2d.shape[0]
    tm = min(FIRST_PROJ_TILE, T)
    n_i = T // tm

    def body(s_ref, *refs):
        x_ref, g_ref, w_ref = refs[:3]
        proj_ref, hb_ref, ht_ref, h_all = refs[-4:]
        i = pl.program_id(1)
        rows = pl.ds(pl.multiple_of(i * tm, tm), tm)

        @pl.when(pl.program_id(0) == 0)
        def _():
            x = x_ref[...]
            r = lax.rsqrt(jnp.mean(x * x, axis=-1, keepdims=True) + EPS)
            h = x * r * g_ref[...]
            hb = h.astype(h_all.dtype)
            h_all[rows, :] = hb
            hb_ref[...] = hb
            ht_ref[...] = h.T.astype(ht_ref.dtype)

        proj_ref[...] = _dot(h_all[rows, :], w_ref[0])

    first = lambda j, i: jnp.where(j == 0, i, n_i - 1)
    return pl.pallas_call(
        body,
        name=name,
        grid_spec=pltpu.PrefetchScalarGridSpec(
            num_scalar_prefetch=1,
            grid=(2 * chips.shape[0], n_i),
            in_specs=[
                pl.BlockSpec((tm, D_MODEL), lambda j, i, s: (first(j, i), 0)),
                pl.BlockSpec((1, D_MODEL), lambda j, i, s: (0, 0)),
                pl.BlockSpec((1, D_MODEL, D_MODEL), lambda j, i, s: (s[j // 2], 0, j % 2)),
            ] + [ANY_SPEC] * len(deps),
            out_specs=[
                pl.BlockSpec((tm, D_MODEL), lambda j, i, s: (i, 2 * s[j // 2] + j % 2)),
                pl.BlockSpec((tm, D_MODEL), lambda j, i, s: (first(j, i), 0)),
                pl.BlockSpec((D_MODEL, tm), lambda j, i, s: (0, first(j, i))),
            ],
            scratch_shapes=[pltpu.VMEM((T, D_MODEL), _MXU)],
        ),
        out_shape=[
            jax.ShapeDtypeStruct((T, N_GROUPS * D_MODEL), F32),
            jax.ShapeDtypeStruct((T, D_MODEL), _MXU),
            jax.ShapeDtypeStruct((D_MODEL, T), _MXU),
        ],
        compiler_params=_params(("arbitrary", "arbitrary")),
    )(chips, x2d, g_in, w_all, *deps)


def _inproj_more(hb, w_all, chips, proj, name):
    T = hb.shape[0]
    tm = min(MORE_PROJ_TILE, T)

    def body(s_ref, hb_hbm, w_ref, prev_ref, proj_ref, h_all, sem):
        @pl.when((pl.program_id(0) == 0) & (pl.program_id(1) == 0))
        def _():
            cp = pltpu.make_async_copy(hb_hbm, h_all, sem)
            cp.start()
            cp.wait()

        rows = pl.ds(pl.multiple_of(pl.program_id(1) * tm, tm), tm)
        proj_ref[...] = _dot(h_all[rows, :], w_ref[0])

    return pl.pallas_call(
        body,
        name=name,
        grid_spec=pltpu.PrefetchScalarGridSpec(
            num_scalar_prefetch=1,
            grid=(2 * chips.shape[0], T // tm),
            in_specs=[
                ANY_SPEC,
                pl.BlockSpec((1, D_MODEL, D_MODEL), lambda j, i, s: (s[j // 2], 0, j % 2)),
                ANY_SPEC,
            ],
            out_specs=pl.BlockSpec((tm, D_MODEL), lambda j, i, s: (i, 2 * s[j // 2] + j % 2)),
            scratch_shapes=[pltpu.VMEM((T, D_MODEL), hb.dtype), pltpu.SemaphoreType.DMA],
        ),
        out_shape=jax.ShapeDtypeStruct(proj.shape, F32),
        input_output_aliases={3: 0},
        compiler_params=_params(("arbitrary", "arbitrary")),
    )(chips, hb, w_all, proj)


def _scan_fwd(a, u):
    n = a.shape[0]
    row = lax.broadcasted_iota(jnp.int32, a.shape, 0)
    s = 1
    while s < n:
        m = row >= s
        u = u + a * jnp.where(m, pltpu.roll(u, s, 0), 0.0)
        a = a * jnp.where(m, pltpu.roll(a, s, 0), 1.0)
        s *= 2
    return a, u


def _scan_bwd(b, g):
    n = b.shape[0]
    row = lax.broadcasted_iota(jnp.int32, b.shape, 0)
    s = 1
    while s < n:
        m = row < n - s
        g = g + b * jnp.where(m, pltpu.roll(g, n - s, 0), 0.0)
        b = b * jnp.where(m, pltpu.roll(b, n - s, 0), 1.0)
        s *= 2
    return b, g


LANES = 128
SUBLANES = 8


def _scan_scratch(tc):
    by_lanes = pltpu.VMEM((CW // LANES, tc, LANES), F32)
    return [by_lanes, by_lanes, pltpu.VMEM((tc // SUBLANES, CW), F32), pltpu.VMEM((tc, CW), F32)]


def _scan_tile(a, u, edge, la_ref, lh_ref, c_ref, dst_ref, reverse):
    n, w = a.shape
    groups = n // SUBLANES
    a3 = a.reshape(groups, SUBLANES, w)
    u3 = u.reshape(groups, SUBLANES, w)
    row = lax.broadcasted_iota(jnp.int32, a3.shape, 1)
    for s in (1, 2, 4):
        m = (row < SUBLANES - s) if reverse else (row >= s)
        shift = SUBLANES - s if reverse else s
        u3 = u3 + a3 * jnp.where(m, pltpu.roll(u3, shift, 1), 0.0)
        a3 = a3 * jnp.where(m, pltpu.roll(a3, shift, 1), 1.0)
    al = a3.reshape(n, w)
    hl = u3.reshape(n, w)
    blocks = w // LANES
    for q in range(blocks):
        la_ref[q] = al[:, q * LANES:(q + 1) * LANES]
        lh_ref[q] = hl[:, q * LANES:(q + 1) * LANES]
    ends = pl.ds(0 if reverse else SUBLANES - 1, groups, stride=SUBLANES)
    end_a = jnp.concatenate([la_ref.at[q][ends, :] for q in range(blocks)], axis=-1)
    end_h = jnp.concatenate([lh_ref.at[q][ends, :] for q in range(blocks)], axis=-1)
    prod, part = (_scan_bwd if reverse else _scan_fwd)(end_a, end_h)
    total = part + prod * edge
    g_row = lax.broadcasted_iota(jnp.int32, total.shape, 0)
    if reverse:
        c_ref[...] = jnp.where(g_row == groups - 1, edge, pltpu.roll(total, groups - 1, 0))
    else:
        c_ref[...] = jnp.where(g_row == 0, edge, pltpu.roll(total, 1, 0))
    for g in range(groups):
        rows = slice(g * SUBLANES, (g + 1) * SUBLANES)
        for q in range(blocks):
            cols = slice(q * LANES, (q + 1) * LANES)
            dst_ref[rows, cols] = lh_ref[q, rows, :] + la_ref[q, rows, :] * c_ref[g:g + 1, cols]


def _softplus_neg(lam):
    z = -lam
    return jnp.maximum(z, 0.0) + jnp.log1p(jnp.exp(-jnp.abs(z)))


def _lru_gates(xc, wx_ref, wa_ref, bx_ref, ba_ref, lam_ref):
    xcb = _c(xc)
    i_t = _sigmoid(_dot(xcb, wx_ref[0]) + bx_ref[...])
    r_t = _sigmoid(_dot(xcb, wa_ref[0]) + ba_ref[...])
    sp = _softplus_neg(lam_ref[...])
    log_a = (-LRU_C) * r_t * sp
    a = jnp.exp(log_a)
    mult = jnp.sqrt(1.0 - a * a)
    return xcb, i_t, r_t, sp, a, mult


def _conv_from_ext(ext_ref, xa, cw_ref, cb_ref, tc):
    return (cb_ref[...] + cw_ref[3:4, :] * xa + cw_ref[2:3, :] * ext_ref[7:7 + tc, :]
            + cw_ref[1:2, :] * ext_ref[6:6 + tc, :] + cw_ref[0:1, :] * ext_ref[5:5 + tc, :])


def _lru_fwd(proj, conv_w, conv_b, wx_bd, wa_bd, bx, ba, lam, B, S):
    T = B * S
    tc = min(SCAN_TILE, S)
    nt = S // tc
    h8 = tc // 8

    def body(xa_ref, halo_ref, ga_ref, cw_ref, cb_ref, wx_ref, wa_ref, bx_ref, ba_ref, lam_ref,
             h_ref, ya_ref, ext_ref, carry_ref, la_ref, lh_ref, c_ref):
        t = pl.program_id(2)

        @pl.when(t == 0)
        def _():
            carry_ref[...] = jnp.zeros_like(carry_ref)

        xa = xa_ref[...]
        ext_ref[0:8, :] = jnp.where(t == 0, 0.0, halo_ref[...])
        ext_ref[8:8 + tc, :] = xa
        xc = _conv_from_ext(ext_ref, xa, cw_ref, cb_ref, tc)
        _, i_t, _, _, a, mult = _lru_gates(xc, wx_ref, wa_ref, bx_ref, ba_ref, lam_ref)
        u = mult * (i_t * xc)
        _scan_tile(a, u, carry_ref[7:8, :], la_ref, lh_ref, c_ref, h_ref, False)
        h = h_ref[...]
        carry_ref[...] = h[tc - 8:tc, :]
        ga = ga_ref[...]
        ya_ref[...] = (ga * _sigmoid(ga) * h).astype(ya_ref.dtype)

    row = lambda b, t: b * nt + t
    vec = pl.BlockSpec((1, CW), lambda b, c, t: (0, c))
    mat = pl.BlockSpec((1, CW, CW), lambda b, c, t: (c, 0, 0))
    return pl.pallas_call(
        body,
        name="lru_fwd",
        grid=(B, N_CT, nt),
        in_specs=[
            pl.BlockSpec((tc, CW), lambda b, c, t: (row(b, t), c)),
            pl.BlockSpec((8, CW), lambda b, c, t: (jnp.maximum(row(b, t) * h8 - 1, 0), c)),
            pl.BlockSpec((tc, CW), lambda b, c, t: (row(b, t), N_CT + c)),
            pl.BlockSpec((CONV, CW), lambda b, c, t: (0, c)),
            vec, mat, mat, vec, vec, vec,
        ],
        out_specs=[
            pl.BlockSpec((tc, CW), lambda b, c, t: (row(b, t), c)),
            pl.BlockSpec((tc, CW), lambda b, c, t: (row(b, t), c)),
        ],
        out_shape=[
            jax.ShapeDtypeStruct((T, D_MODEL), F32),
            jax.ShapeDtypeStruct((T, D_MODEL), _MXU),
        ],
        scratch_shapes=[pltpu.VMEM((tc + 8, CW), F32), pltpu.VMEM((8, CW), F32)] + _scan_scratch(tc)[:3],
        compiler_params=_params(("parallel", "parallel", "arbitrary")),
    )(proj, proj, proj, conv_w, conv_b, wx_bd, wa_bd, bx, ba, lam)


def _lru_bwd(dya, proj, hlru, conv_w, conv_b, wx_bd, wa_bd, bx, ba, lam, B, S, deps=()):
    T = B * S
    tc = min(SCAN_TILE, S)
    nt = S // tc
    h8 = tc // 8

    def body(dya_ref, xa_ref, xhalo_ref, ga_ref, h_ref, hhalo_ref, cw_ref, cb_ref, wx_ref, wa_ref, bx_ref, ba_ref,
             lam_ref, dxa_ref, dga_ref, dcw_ref, dcb_ref, dwx_ref, dwa_ref, dbx_ref, dba_ref, dlam_ref,
             ext_ref, ext2_ref, carry_ref, dhalo_ref, la_ref, lh_ref, c_ref, dh_ref):
        b = pl.program_id(1)
        t = pl.program_id(2)
        tt = nt - 1 - t

        @pl.when(t == 0)
        def _():
            carry_ref[...] = jnp.zeros_like(carry_ref)
            dhalo_ref[...] = jnp.zeros_like(dhalo_ref)

        @pl.when((t == 0) & (b == 0))
        def _():
            for r in (dcw_ref, dcb_ref, dwx_ref, dwa_ref, dbx_ref, dba_ref, dlam_ref):
                r[...] = jnp.zeros_like(r)

        xa = xa_ref[...]
        ext_ref[0:8, :] = jnp.where(tt == 0, 0.0, xhalo_ref[...])
        ext_ref[8:8 + tc, :] = xa
        xc = _conv_from_ext(ext_ref, xa, cw_ref, cb_ref, tc)
        xcb, i_t, r_t, sp, a, mult = _lru_gates(xc, wx_ref, wa_ref, bx_ref, ba_ref, lam_ref)

        h = h_ref[...]
        ga = ga_ref[...]
        dya_t = dya_ref[...]
        sg = _sigmoid(ga)
        dga_ref[...] = (dya_t * h * (sg * (1.0 + ga * (1.0 - sg)))).astype(dga_ref.dtype)
        dlru = dya_t * (ga * sg)

        row = lax.broadcasted_iota(jnp.int32, a.shape, 0)
        coef = jnp.where(row == tc - 1, 1.0, pltpu.roll(a, tc - 1, 0))
        _scan_tile(coef, dlru, carry_ref[0:1, :], la_ref, lh_ref, c_ref, dh_ref, True)
        dh = dh_ref[...]
        ext2_ref[0:tc, :] = a * dh
        carry_ref[...] = ext2_ref[0:8, :]

        ext2_ref[0:8, :] = jnp.where(tt == 0, 0.0, hhalo_ref[...])
        ext2_ref[8:8 + tc, :] = h
        hprev = ext2_ref[7:7 + tc, :]

        da = dh * hprev
        ix = i_t * xc
        dmult = dh * ix
        di = dh * mult * xc
        dxc = dh * mult * i_t
        dlog_a = da * a - dmult * (a * a) / mult
        dr = dlog_a * ((-LRU_C) * sp)
        dlam_ref[...] += jnp.sum(dlog_a * r_t, axis=0, keepdims=True) * (LRU_C * _sigmoid(-lam_ref[...]))
        dza = dr * r_t * (1.0 - r_t)
        dzx = di * i_t * (1.0 - i_t)
        dzab = _c(dza)
        dzxb = _c(dzx)
        dxc = dxc + _dot_nt(dzxb, wx_ref[0]) + _dot_nt(dzab, wa_ref[0])
        dwx_ref[0] += _dot_tn(xcb, dzxb)
        dwa_ref[0] += _dot_tn(xcb, dzab)
        dbx_ref[...] += jnp.sum(dzx, axis=0, keepdims=True)
        dba_ref[...] += jnp.sum(dza, axis=0, keepdims=True)

        dcb_ref[...] += jnp.sum(dxc, axis=0, keepdims=True)
        dcw_ref[3:4, :] += jnp.sum(dxc * xa, axis=0, keepdims=True)
        dcw_ref[2:3, :] += jnp.sum(dxc * ext_ref[7:7 + tc, :], axis=0, keepdims=True)
        dcw_ref[1:2, :] += jnp.sum(dxc * ext_ref[6:6 + tc, :], axis=0, keepdims=True)
        dcw_ref[0:1, :] += jnp.sum(dxc * ext_ref[5:5 + tc, :], axis=0, keepdims=True)
        ext2_ref[0:tc, :] = dxc
        ext2_ref[tc:tc + 8, :] = dhalo_ref[...]
        dxa = (cw_ref[3:4, :] * dxc + cw_ref[2:3, :] * ext2_ref[1:1 + tc, :]
               + cw_ref[1:2, :] * ext2_ref[2:2 + tc, :] + cw_ref[0:1, :] * ext2_ref[3:3 + tc, :])
        dxa_ref[...] = dxa.astype(dxa_ref.dtype)
        dhalo_ref[...] = ext2_ref[0:8, :]

    row_of = lambda b, t: b * nt + (nt - 1 - t)
    tile = lambda off: pl.BlockSpec((tc, CW), lambda c, b, t: (row_of(b, t), off + c))
    halo = pl.BlockSpec((8, CW), lambda c, b, t: (jnp.maximum(row_of(b, t) * h8 - 1, 0), c))
    vec = pl.BlockSpec((1, CW), lambda c, b, t: (0, c))
    mat = pl.BlockSpec((1, CW, CW), lambda c, b, t: (c, 0, 0))
    cwspec = pl.BlockSpec((CONV, CW), lambda c, b, t: (0, c))
    return pl.pallas_call(
        _after(body, 13, deps),
        name="lru_bwd",
        grid=(N_CT, B, nt),
        in_specs=[tile(0), tile(0), halo, tile(N_CT), tile(0), halo, cwspec, vec, mat, mat, vec, vec, vec]
        + [ANY_SPEC] * len(deps),
        out_specs=[tile(0), tile(0), cwspec, vec, mat, mat, vec, vec, vec],
        out_shape=[
            jax.ShapeDtypeStruct((T, D_MODEL), _MXU),
            jax.ShapeDtypeStruct((T, D_MODEL), _MXU),
            jax.ShapeDtypeStruct((CONV, D_MODEL), F32),
            jax.ShapeDtypeStruct((1, D_MODEL), F32),
            jax.ShapeDtypeStruct((N_CT, CW, CW), F32),
            jax.ShapeDtypeStruct((N_CT, CW, CW), F32),
            jax.ShapeDtypeStruct((1, D_MODEL), F32),
            jax.ShapeDtypeStruct((1, D_MODEL), F32),
            jax.ShapeDtypeStruct((1, D_MODEL), F32),
        ],
        scratch_shapes=[pltpu.VMEM((tc + 8, CW), F32), pltpu.VMEM((tc + 8, CW), F32),
                        pltpu.VMEM((8, CW), F32), pltpu.VMEM((8, CW), F32)] + _scan_scratch(tc),
        compiler_params=_params(("parallel", "arbitrary", "arbitrary")),
    )(dya, proj, proj, proj, hlru, hlru, conv_w, conv_b, wx_bd, wa_bd, bx, ba, lam, *deps)


def _retention_tables(S):
    half = DK // 2
    freqs = ROPE_THETA ** (-jnp.arange(half, dtype=F32) / half)
    ang = jnp.arange(S, dtype=F32)[:, None] * freqs[None, :]
    log_g = jnp.log1p(-(2.0 ** (-5.0 - jnp.arange(HEADS, dtype=F32))))
    idx = jnp.arange(CHUNK, dtype=F32)
    diff = idx[:, None] - idx[None, :]
    inner = jnp.where(diff >= 0, jnp.exp(jnp.maximum(diff, 0.0)[None] * log_g[:, None, None]), 0.0)
    cross = jnp.exp((idx[None, :] + 1.0) * log_g[:, None])[:, :, None]
    state = jnp.exp((CHUNK - 1.0 - idx[None, :]) * log_g[:, None])[:, :, None]
    gam = jnp.broadcast_to(jnp.exp(CHUNK * log_g)[:, None, None], (HEADS, 1, DK))
    return jnp.cos(ang), jnp.sin(ang), inner, cross, state, gam


def _rot(x, cos, sin):
    half = DK // 2
    x1, x2 = x[:, :half], x[:, half:]
    return jnp.concatenate([x1 * cos - x2 * sin, x1 * sin + x2 * cos], axis=-1)


def _rot_t(y, cos, sin):
    half = DK // 2
    y1, y2 = y[:, :half], y[:, half:]
    return jnp.concatenate([y1 * cos + y2 * sin, y2 * cos - y1 * sin], axis=-1)


def _groupnorm(o):
    mu = jnp.mean(o, axis=-1, keepdims=True)
    oc = o - mu
    rs = lax.rsqrt(jnp.mean(oc * oc, axis=-1, keepdims=True) + EPS)
    return oc * rs, rs


def _ret_specs(B, chunk_of):
    qkv = lambda g: pl.BlockSpec((B, CHUNK, D_MODEL), lambda c: (0, chunk_of(c), g))
    act = pl.BlockSpec((B, CHUNK, D_MODEL), lambda c: (0, chunk_of(c), 0))
    rope = pl.BlockSpec((CHUNK, DK // 2), lambda c: (chunk_of(c), 0))
    dmat = pl.BlockSpec((HEADS, CHUNK, CHUNK), lambda c: (0, 0, 0))
    dvec = pl.BlockSpec((HEADS, CHUNK, 1), lambda c: (0, 0, 0))
    hrow = pl.BlockSpec((HEADS, 1, DK), lambda c: (0, 0, 0))
    rst = pl.BlockSpec((1, B, HEADS, DK, DK), lambda c: (chunk_of(c), 0, 0, 0, 0))
    return qkv, act, rope, dmat, dvec, hrow, rst


def _ret_fwd(proj, tables, gain3, B, S):
    T = B * S
    nc = S // CHUNK
    cos, sin, dmat_t, cd_t, sd_t, gam_t = tables

    def body(q_ref, k_ref, v_ref, gb_ref, cos_ref, sin_ref, dm_ref, cd_ref, sd_ref, gam_ref, gain_ref,
             o_ref, yb_ref, rs_ref, state_ref):
        @pl.when(pl.program_id(0) == 0)
        def _():
            state_ref[...] = jnp.zeros_like(state_ref)

        cos_t, sin_t = cos_ref[...], sin_ref[...]
        for b, h in [(b, h) for b in range(B) for h in range(HEADS)]:
            cols = slice(h * DK, (h + 1) * DK)
            qb = _c(_rot(q_ref[b, :, cols], cos_t, sin_t))
            kb = _c(_rot(k_ref[b, :, cols], cos_t, sin_t) * (DK ** -0.5))
            v = v_ref[b, :, cols]
            state = state_ref[b, h]
            sb = _c(state)
            rs_ref[0, b, h] = sb
            scores = _dot_nt(qb, kb) * dm_ref[h]
            o = _dot(_c(scores), _c(v)) + _dot(qb, sb) * cd_ref[h]
            state_ref[b, h] = gam_ref[h] * state + _dot_tn(kb, _c(v * sd_ref[h]))
            o_ref[b, :, cols] = o
            n, _ = _groupnorm(o)
            gb = gb_ref[b, :, cols]
            yb_ref[b, :, cols] = (gb * _sigmoid(gb) * (n * gain_ref[h])).astype(yb_ref.dtype)

    qkv, act, rope, dmat, dvec, hrow, rst = _ret_specs(B, lambda c: c)
    proj3 = proj.reshape(B, S, proj.shape[1])
    o_pre, yb, states = pl.pallas_call(
        body,
        name="ret_fwd",
        grid=(nc,),
        in_specs=[qkv(2), qkv(3), qkv(4), qkv(5), rope, rope, dmat, dvec, dvec, hrow, hrow],
        out_specs=[act, act, rst],
        out_shape=[
            jax.ShapeDtypeStruct((B, S, D_MODEL), F32),
            jax.ShapeDtypeStruct((B, S, D_MODEL), _MXU),
            jax.ShapeDtypeStruct((nc, B, HEADS, DK, DK), _MXU),
        ],
        scratch_shapes=[pltpu.VMEM((B, HEADS, DK, DK), F32)],
        compiler_params=_params(("arbitrary",)),
    )(proj3, proj3, proj3, proj3, cos, sin, dmat_t, cd_t, sd_t, gam_t, gain3)
    return o_pre.reshape(T, D_MODEL), yb.reshape(T, D_MODEL), states


def _ret_bwd(dyb, o_pre, proj, states, tables, gain3, B, S, deps=()):
    T = B * S
    nc = S // CHUNK
    cos, sin, dmat_t, cd_t, sd_t, gam_t = tables

    def body(dyb_ref, o_ref, q_ref, k_ref, v_ref, gb_ref, rs_ref, cos_ref, sin_ref, dm_ref, cd_ref, sd_ref, gam_ref,
             gain_ref, dr_ref, dgain_ref, dstate_ref):
        @pl.when(pl.program_id(0) == 0)
        def _():
            dstate_ref[...] = jnp.zeros_like(dstate_ref)
            dgain_ref[...] = jnp.zeros_like(dgain_ref)

        cos_t, sin_t = cos_ref[...], sin_ref[...]
        for b, h in [(b, h) for b in range(B) for h in range(HEADS)]:
            cols = slice(h * DK, (h + 1) * DK)
            gain = gain_ref[h]
            n, rs = _groupnorm(o_ref[b, :, cols])
            gb = gb_ref[b, :, cols]
            sg = _sigmoid(gb)
            dy = dyb_ref[b, :, cols]
            part = lambda g: slice(g * D_MODEL + h * DK, g * D_MODEL + (h + 1) * DK)
            dr_ref[b, :, part(3)] = (dy * (n * gain) * (sg * (1.0 + gb * (1.0 - sg)))).astype(dr_ref.dtype)
            dgn = dy * (gb * sg)
            dgain_ref[h] += jnp.sum(dgn * n, axis=0, keepdims=True)
            dn = dgn * gain
            do = rs * (dn - jnp.mean(dn, axis=-1, keepdims=True) - n * jnp.mean(dn * n, axis=-1, keepdims=True))

            qb = _c(_rot(q_ref[b, :, cols], cos_t, sin_t))
            kb = _c(_rot(k_ref[b, :, cols], cos_t, sin_t) * (DK ** -0.5))
            v = v_ref[b, :, cols]
            vb = _c(v)
            vsb = _c(v * sd_ref[h])
            dob = _c(do)
            docb = _c(do * cd_ref[h])
            dmat = dm_ref[h]
            dstate = dstate_ref[b, h]
            dsb = _c(dstate)
            pb = _c(_dot_nt(qb, kb) * dmat)
            dsc = _c(_dot_nt(dob, vb) * dmat)
            dq = _dot(dsc, kb) + _dot_nt(docb, rs_ref[0, b, h])
            dk = _dot_tn(dsc, qb) + _dot_nt(vsb, dsb)
            dv = _dot_tn(pb, dob) + _dot(kb, dsb) * sd_ref[h]
            dstate_ref[b, h] = gam_ref[h] * dstate + _dot_tn(qb, docb)
            dr_ref[b, :, part(0)] = _rot_t(dq, cos_t, sin_t).astype(dr_ref.dtype)
            dr_ref[b, :, part(1)] = (_rot_t(dk, cos_t, sin_t) * (DK ** -0.5)).astype(dr_ref.dtype)
            dr_ref[b, :, part(2)] = dv.astype(dr_ref.dtype)

    qkv, act, rope, dmat, dvec, hrow, rst = _ret_specs(B, lambda c: nc - 1 - c)
    wide = pl.BlockSpec((B, CHUNK, 4 * D_MODEL), lambda c: (0, nc - 1 - c, 0))
    proj3 = proj.reshape(B, S, proj.shape[1])
    dr, dgain = pl.pallas_call(
        _after(body, 14, deps),
        name="ret_bwd",
        grid=(nc,),
        in_specs=[act, act, qkv(2), qkv(3), qkv(4), qkv(5), rst, rope, rope, dmat, dvec, dvec, hrow, hrow]
        + [ANY_SPEC] * len(deps),
        out_specs=[wide, hrow],
        out_shape=[jax.ShapeDtypeStruct((B, S, 4 * D_MODEL), _MXU), jax.ShapeDtypeStruct((HEADS, 1, DK), F32)],
        scratch_shapes=[pltpu.VMEM((B, HEADS, DK, DK), F32)],
        compiler_params=_params(("arbitrary",)),
    )(dyb.reshape(B, S, D_MODEL), o_pre.reshape(B, S, D_MODEL), proj3, proj3, proj3, proj3, states, cos, sin, dmat_t,
      cd_t, sd_t, gam_t, gain3, *deps)
    return dr.reshape(T, 4 * D_MODEL), dgain


def _mid(ya, yb, proj, x2d, tgt2d, wpa, wpb, wout, g_fin):
    T = x2d.shape[0]
    tm = min(MID_TILE, T)
    n_steps = T // tm
    rows = D_MODEL // (2 * N_CHIPS)

    def body(ya_ref, yb_ref, ma_ref, mb_ref, x_ref, t_ref, gf_ref, wpa_hbm, wpb_hbm, wout_hbm,
             loss_ref, dx2_ref, dya_ref, dyb_ref, dm_ref, dgf_ref, gw_hbm, w_ref, acc_ref, sem):
        i = pl.program_id(0)

        @pl.when(i == 0)
        def _():
            loads = [pltpu.make_async_copy(src, w_ref.at[k], sem.at[k]) for k, src in enumerate((wpa_hbm, wpb_hbm, wout_hbm))]
            for cp in loads:
                cp.start()
            for cp in loads:
                cp.wait()
            acc_ref[...] = jnp.zeros_like(acc_ref)
            loss_ref[...] = jnp.zeros_like(loss_ref)
            dgf_ref[...] = jnp.zeros_like(dgf_ref)

        ya_t, yb_t = ya_ref[...], yb_ref[...]
        out_a = _dot(ya_t, w_ref[0])
        out_b = _dot(yb_t, w_ref[1])
        sa = _sigmoid(ma_ref[...])
        sb = _sigmoid(mb_ref[...])
        mgb = _c(sa * out_a + sb * out_b)
        x2 = x_ref[...] + _dot(mgb, w_ref[2])
        r2 = lax.rsqrt(jnp.mean(x2 * x2, axis=-1, keepdims=True) + EPS)
        nx = x2 * r2
        gf = gf_ref[...]
        err = nx * gf - t_ref[...]
        loss_ref[...] += 0.5 * jnp.sum(jnp.mean(err * err, axis=-1, keepdims=True), axis=0, keepdims=True)
        dy = err * (1.0 / D_MODEL)
        dgf_ref[...] += jnp.sum(dy * nx, axis=0, keepdims=True)
        dyg = dy * gf
        dx2 = r2 * (dyg - nx * jnp.mean(dyg * nx, axis=-1, keepdims=True))
        dx2_ref[...] = dx2
        dx2b = _c(dx2)
        dmg = _dot_nt(dx2b, w_ref[2])
        acc_ref[2] += _dot_tn(mgb, dx2b)
        dm_ref[:, :D_MODEL] = (dmg * out_a * sa * (1.0 - sa)).astype(dm_ref.dtype)
        dm_ref[:, D_MODEL:] = (dmg * out_b * sb * (1.0 - sb)).astype(dm_ref.dtype)
        dab = _c(dmg * sa)
        dbb = _c(dmg * sb)
        dya_ref[...] = _dot_nt(dab, w_ref[0])
        dyb_ref[...] = _dot_nt(dbb, w_ref[1])
        acc_ref[0] += _dot_tn(ya_t, dab)
        acc_ref[1] += _dot_tn(yb_t, dbb)

        @pl.when(i == n_steps - 1)
        def _():
            copies = [pltpu.make_async_copy(acc_ref.at[k, pl.ds((2 * p + hf) * rows, rows), :], gw_hbm.at[p, hf, k],
                                            sem.at[(k * N_CHIPS + p) * 2 + hf])
                      for k in range(3) for p in range(N_CHIPS) for hf in range(2)]
            for cp in copies:
                cp.start()
            for cp in copies:
                cp.wait()

    tile = lambda j: pl.BlockSpec((tm, D_MODEL), lambda i: (i, j))
    one = pl.BlockSpec((1, D_MODEL), lambda i: (0, 0))
    anyspec = pl.BlockSpec(memory_space=pl.ANY)
    return pl.pallas_call(
        body,
        name="mid",
        grid=(n_steps,),
        in_specs=[tile(0), tile(0), tile(6), tile(7), tile(0), tile(0), one, anyspec, anyspec, anyspec],
        out_specs=[pl.BlockSpec((1, 1), lambda i: (0, 0)), tile(0), tile(0), tile(0),
                   pl.BlockSpec((tm, 2 * D_MODEL), lambda i: (i, 0)), one, anyspec],
        out_shape=[
            jax.ShapeDtypeStruct((1, 1), F32),
            jax.ShapeDtypeStruct((T, D_MODEL), F32),
            jax.ShapeDtypeStruct((T, D_MODEL), F32),
            jax.ShapeDtypeStruct((T, D_MODEL), F32),
            jax.ShapeDtypeStruct((T, 2 * D_MODEL), _MXU),
            jax.ShapeDtypeStruct((1, D_MODEL), F32),
            jax.ShapeDtypeStruct((N_CHIPS, 2, 3, rows, D_MODEL), F32),
        ],
        scratch_shapes=[pltpu.VMEM((3, D_MODEL, D_MODEL), _MXU), pltpu.VMEM((3, D_MODEL, D_MODEL), F32),
                        pltpu.SemaphoreType.DMA((3 * N_CHIPS * 2,))],
        compiler_params=_params(("arbitrary",)),
    )(ya, yb, proj, proj, x2d, tgt2d, g_fin, wpa, wpb, wout)


def _inproj_bwd_dx(dparts, w_all, x2d, dx2, g_in, first, count, prev, name, deps=()):
    T = x2d.shape[0]
    tm = min(DX_TILE, T)
    n_d = len(dparts)
    groups = [(a, k) for a, d in enumerate(dparts) for k in range(d.shape[1] // D_MODEL)]
    dg_start = jnp.zeros((1, D_MODEL), F32) if prev is None else prev[1]
    carried = () if prev is None else (prev[0],)

    def body(*refs):
        d_refs = refs[:n_d]
        x_ref, dx2_ref, g_ref, dg0_ref, w_hbm = refs[n_d:n_d + 5]
        dx_ref, dg_ref, w_ref, sem = refs[-4:]

        @pl.when(pl.program_id(0) == 0)
        def _():
            cp = pltpu.make_async_copy(w_hbm, w_ref, sem)
            cp.start()
            cp.wait()
            dg_ref[...] = dg0_ref[...]

        dh = jnp.zeros((tm, D_MODEL), F32)
        for j, (a, k) in enumerate(groups):
            dh = dh + _dot_nt(d_refs[a][:, k * D_MODEL:(k + 1) * D_MODEL],
                              w_ref[j // 2, :, (j % 2) * D_MODEL:(j % 2 + 1) * D_MODEL])
        x = x_ref[...]
        r = lax.rsqrt(jnp.mean(x * x, axis=-1, keepdims=True) + EPS)
        nx = x * r
        dg_ref[...] += jnp.sum(dh * nx, axis=0, keepdims=True)
        dhg = dh * g_ref[...]
        dx_ref[...] = dx2_ref[...] + r * (dhg - nx * jnp.mean(dhg * nx, axis=-1, keepdims=True))

    tile = pl.BlockSpec((tm, D_MODEL), lambda i: (first + i, 0))
    one = pl.BlockSpec((1, D_MODEL), lambda i: (0, 0))
    return pl.pallas_call(
        body,
        name=name,
        grid=(count,),
        in_specs=[pl.BlockSpec((tm, d.shape[1]), lambda i: (first + i, 0)) for d in dparts]
        + [tile, tile, one, one, ANY_SPEC] + [ANY_SPEC] * (len(carried) + len(deps)),
        out_specs=[tile, one],
        out_shape=[jax.ShapeDtypeStruct((T, D_MODEL), F32), jax.ShapeDtypeStruct((1, D_MODEL), F32)],
        input_output_aliases={n_d + 5: 0} if carried else {},
        scratch_shapes=[pltpu.VMEM(w_all.shape, w_all.dtype), pltpu.SemaphoreType.DMA],
        compiler_params=_params(("arbitrary",)),
    )(*dparts, x2d, dx2, g_in, dg_start, w_all, *carried, *deps)


def _inproj_bwd_dw(ht, dparts, name, deps=()):
    T = ht.shape[1]
    tn = DW_COLS
    half = D_MODEL // 2
    per_chip = 2 * D_MODEL // tn
    n_d = len(dparts)
    tiles = [(a, t) for a, d in enumerate(dparts) for t in range(d.shape[1] // tn)]
    offs = [sum(d.shape[1] // tn for d in dparts[:a]) for a in range(n_d)]

    def body(*refs):
        ht_ref = refs[0]
        d_refs = refs[1:1 + n_d]
        out_ref = refs[-1]
        t = pl.program_id(0)

        for a in range(n_d):
            lo, hi = offs[a], offs[a] + dparts[a].shape[1] // tn

            @pl.when((t >= lo) & (t < hi))
            def _(a=a):
                g = _dot(ht_ref[...], d_refs[a][...])
                out_ref[0, 0] = g[:half]
                out_ref[0, 1] = g[half:]

    def dspec(a):
        n_a = dparts[a].shape[1] // tn
        return pl.BlockSpec((T, tn), lambda t: (0, jnp.clip(t - offs[a], 0, n_a - 1)))

    return pl.pallas_call(
        body,
        name=name,
        grid=(len(tiles),),
        in_specs=[pl.BlockSpec((D_MODEL, T), lambda t: (0, 0))] + [dspec(a) for a in range(n_d)]
        + [ANY_SPEC] * len(deps),
        out_specs=pl.BlockSpec((1, 2, half, tn), lambda t: (t // per_chip, 0, 0, t % per_chip)),
        out_shape=jax.ShapeDtypeStruct((len(tiles) // per_chip, 2, half, 2 * D_MODEL), F32),
        compiler_params=_params(("parallel",)),
    )(ht, *dparts, *deps)


def _coords():
    return lax.axis_index("x"), lax.axis_index("y"), lax.axis_index("c")


def _other_chips(x, y):
    return [(1 - x, y), (x, 1 - y), (1 - x, 1 - y)]


def _chunks(rows, n):
    size = rows // n
    return [pl.ds(q * size, size) for q in range(n)]


HBM_SPEC = pl.BlockSpec(memory_space=pltpu.HBM)
SEM_SPEC = pl.BlockSpec(memory_space=pltpu.SEMAPHORE)
DATAFLOW = pltpu.SideEffectType.DATAFLOW_SIDE_EFFECTING


def _copies_start(bufs, plan, n_copies, name, deps=()):
    n = len(bufs)
    n_deps = len(deps)

    def body(*refs):
        ins = refs[:n]
        send_sems, recv_sems = refs[n + n_deps], refs[n + n_deps + 1]
        token = refs[-1]
        for k, send, _ in plan(ins):
            if send is not None:
                src, dst, dev, pred = send
                cp = pltpu.make_async_remote_copy(src_ref=src, dst_ref=dst, send_sem=send_sems.at[k],
                                                  recv_sem=recv_sems.at[k], device_id=dev, device_id_type=MESH)
                if pred is None:
                    cp.start()
                else:
                    pl.when(pred)(cp.start)
        token[...] = jnp.zeros_like(token)

    hbm = [pltpu.with_memory_space_constraint(b, pltpu.HBM) for b in bufs]
    outs = pl.pallas_call(
        body,
        name=name,
        in_specs=[HBM_SPEC] * n + [ANY_SPEC] * n_deps,
        out_specs=(SEM_SPEC, SEM_SPEC, *([HBM_SPEC] * n), pl.BlockSpec(memory_space=pltpu.VMEM)),
        out_shape=(pltpu.SemaphoreType.DMA((n_copies,)), pltpu.SemaphoreType.DMA((n_copies,)),
                   *[pltpu.HBM(b.shape, b.dtype) for b in bufs], jax.ShapeDtypeStruct((8, 128), F32)),
        input_output_aliases={a: 2 + a for a in range(n)},
        compiler_params=pltpu.CompilerParams(has_side_effects=DATAFLOW),
    )(*hbm, *deps)
    return outs[0], outs[1], list(outs[2:2 + n]), outs[-1]


def _copies_wait(send_sems, recv_sems, bufs, after, plan, name, only=None):
    n = len(bufs)

    def body(*refs):
        ins = refs[:n]
        s_sems, r_sems = refs[n], refs[n + 1]
        for k, send, recv in plan(ins):
            if only is not None and k not in only:
                continue
            if send is not None:
                src, dst, dev, pred = send
                cp = pltpu.make_async_remote_copy(src_ref=src, dst_ref=dst, send_sem=s_sems.at[k],
                                                  recv_sem=r_sems.at[k], device_id=dev, device_id_type=MESH)
                if pred is None:
                    cp.wait_send()
                else:
                    pl.when(pred)(cp.wait_send)
            if recv is not None:
                dst, pred = recv
                cp = pltpu.make_async_remote_copy(src_ref=dst, dst_ref=dst, send_sem=s_sems.at[k],
                                                  recv_sem=r_sems.at[k], device_id=_coords(), device_id_type=MESH)
                if pred is None:
                    cp.wait_recv()
                else:
                    pl.when(pred)(cp.wait_recv)

    outs = pl.pallas_call(
        body,
        name=name,
        in_specs=[HBM_SPEC] * n + [SEM_SPEC, SEM_SPEC, pl.BlockSpec(memory_space=pl.ANY)],
        out_specs=[HBM_SPEC] * n,
        out_shape=[pltpu.HBM(b.shape, b.dtype) for b in bufs],
        input_output_aliases={a: a for a in range(n)},
        compiler_params=pltpu.CompilerParams(has_side_effects=DATAFLOW),
    )(*bufs, send_sems, recv_sems, after)
    return list(outs)


def _gather_plan(n_bufs):
    def plan(refs):
        x, y, c = _coords()
        me = 2 * x + y
        out = []
        for k, (px, py) in enumerate(_other_chips(x, y)):
            for a in range(n_bufs):
                out.append((k * n_bufs + a, (refs[a].at[me], refs[a].at[me], (px, py, c), None),
                            (refs[a].at[2 * px + py], None)))
        return out
    return plan


def _cast_into_slot(ws, name, deps=()):
    n = len(ws)
    nt = 2

    def body(s_ref, *refs):
        outs = refs[len(refs) - n:]
        for a in range(n):
            outs[a][0] = refs[a][...].astype(outs[a].dtype)

    xi, yi, _ = _coords()
    return pl.pallas_call(
        body,
        name=name,
        grid_spec=pltpu.PrefetchScalarGridSpec(
            num_scalar_prefetch=1,
            grid=(2, nt),
            in_specs=[pl.BlockSpec((1, w.shape[1] // nt, w.shape[2]), lambda hf, i, s: (hf, i, 0)) for w in ws]
            + [ANY_SPEC] * len(deps),
            out_specs=[pl.BlockSpec((1, 1, w.shape[1] // nt, w.shape[2]), lambda hf, i, s: (s[0], hf, i, 0)) for w in ws],
        ),
        out_shape=[jax.ShapeDtypeStruct((N_CHIPS,) + w.shape, _MXU) for w in ws],
        compiler_params=_params(("parallel", "parallel")),
    )((2 * xi + yi).reshape(1).astype(jnp.int32), *ws, *deps)


def _chip_gather_plan(stage, n_bufs):
    def plan(refs):
        x, y, c = _coords()
        me = 2 * x + y
        near = [(1 - x, y), (x, 1 - y)]
        slots = [2 * (1 - x) + y, 2 * x + (1 - y), 2 * (1 - x) + (1 - y)]
        sibling = (x, y, 1 - c)
        pass_to = (jnp.where(c == 0, x, 1 - x), jnp.where(c == 0, 1 - y, y), c)
        pass_slot = jnp.where(c == 0, slots[0], slots[1])
        out = []

        def move(src_slot, to, land_slot, land_core, pieces):
            for a, buf in enumerate(refs):
                for rows in _chunks(buf.shape[2], pieces[a]):
                    out.append((len(out), (buf.at[src_slot, c, rows], buf.at[src_slot, c, rows], to, None),
                                (buf.at[land_slot, land_core, rows], None)))

        if stage == "near":
            for k, chip in enumerate(near):
                move(me, (*chip, c), slots[k], c, NEAR_PIECES[:n_bufs])
        elif stage == "pass":
            move(pass_slot, pass_to, slots[2], c, PASS_PIECES[:n_bufs])
            for k in range(2):
                move(slots[k], sibling, slots[k], 1 - c, [1] * n_bufs)
        else:
            move(slots[2], sibling, slots[2], 1 - c, [1] * n_bufs)
        return out
    return plan


NEAR_PIECES = (2, 1)
PASS_PIECES = (2, 1)


def _chip_gather_copies(stage, n_bufs):
    if stage == "near":
        return 2 * sum(NEAR_PIECES[:n_bufs]), None
    if stage == "pass":
        n_pass = sum(PASS_PIECES[:n_bufs])
        return n_pass + 2 * n_bufs, set(range(n_pass))
    return n_bufs, None


def _swap_plan(n_slabs):
    def plan(refs):
        x, y, c = _coords()
        out, k = [], 0
        for i, n in enumerate(n_slabs):
            g, land = refs[2 * i], refs[2 * i + 1]
            for p in range(n):
                out.append((k, (g.at[p, 1 - c], land.at[p], (x, y, 1 - c), None), (land.at[p], None)))
                k += 1
        return out
    return plan


def _is_one_of(chip, dests):
    hit = chip == dests[0]
    for d in dests[1:]:
        hit = hit | (chip == d)
    return hit


def _slab_of(chip, dests):
    return sum(j * (chip == d).astype(jnp.int32) for j, d in enumerate(dests))


def _scatter_plan(dest_sets):
    def plan(refs):
        x, y, c = _coords()
        me = 2 * x + y
        out = []
        for k, (px, py) in enumerate(_other_chips(x, y)):
            peer = 2 * px + py
            for i, dests in enumerate(dest_sets):
                cs, land = refs[2 * i], refs[2 * i + 1]
                everyone = len(dests) == N_CHIPS
                send = (cs.at[_slab_of(peer, dests)], land.at[k], (px, py, c),
                        None if everyone else _is_one_of(peer, dests))
                recv = (land.at[k], None if everyone else _is_one_of(me, dests))
                out.append((k * len(dest_sets) + i, send, recv))
        return out
    return plan


def _join_plan(rows, n_pieces):
    def plan(refs):
        x, y, c = _coords()
        (buf,) = refs
        return [(i, (buf.at[c, piece], buf.at[c, piece], (x, y, 1 - c), None), (buf.at[1 - c, piece], None))
                for i, piece in enumerate(_chunks(rows, n_pieces))]
    return plan


def _join_plans(parts):
    def plan(refs):
        out, b0, k0 = [], 0, 0
        for part_plan, n_bufs, n_copies in parts:
            out += [(k0 + k, send, recv) for k, send, recv in part_plan(refs[b0:b0 + n_bufs])]
            b0 += n_bufs
            k0 += n_copies
        return out
    return plan


def _allgather_plan():
    def plan(refs):
        x, y, c = _coords()
        (land,) = refs
        me = 4 * x + 2 * y + c
        out = []
        for r in range(1, 8):
            px = 1 - x if r & 4 else x
            py = 1 - y if r & 2 else y
            pc = 1 - c if r & 1 else c
            out.append((r - 1, (land.at[me], land.at[me], (px, py, pc), None), (land.at[4 * px + 2 * py + pc], None)))
        return out
    return plan


def _sum_gathered(land, name):
    def body(land_ref, o_ref):
        acc = land_ref[0]
        for d in range(1, 8):
            acc = acc + land_ref[d]
        o_ref[...] = acc

    return pl.pallas_call(
        body,
        name=name,
        out_shape=jax.ShapeDtypeStruct(land.shape[1:], F32),
        compiler_params=_params(),
    )(land)


def _join_halves(bufs, n_chunks, name, deps=()):
    n = len(bufs)
    pieces = [(a, rows) for a in range(n) for rows in _chunks(bufs[a].shape[1], n_chunks[a])]
    n_p = len(pieces)

    def body(*refs):
        outs = refs[-n - 2:-2]
        send_sems, recv_sems = refs[-2:]
        x, y, c = _coords()

        def copy(i, half):
            a, rows = pieces[i]
            return pltpu.make_async_remote_copy(
                src_ref=outs[a].at[half, rows], dst_ref=outs[a].at[half, rows], send_sem=send_sems.at[i],
                recv_sem=recv_sems.at[i], device_id=(x, y, 1 - c), device_id_type=MESH)

        sends = [copy(i, c) for i in range(n_p)]
        for cp in sends:
            cp.start()
        for i in range(n_p):
            copy(i, 1 - c).wait_recv()
        for cp in sends:
            cp.wait_send()

    anyspec = pl.BlockSpec(memory_space=pl.ANY)
    sems = pltpu.SemaphoreType.DMA((n_p,))
    return pl.pallas_call(
        body,
        name=name,
        in_specs=[anyspec] * (n + len(deps)),
        out_specs=[anyspec] * n,
        out_shape=[jax.ShapeDtypeStruct(b.shape, b.dtype) for b in bufs],
        input_output_aliases={a: a for a in range(n)},
        scratch_shapes=[sems, sems],
    )(*bufs, *deps)


def _row_tile(rows, cap):
    t = cap
    while rows % t:
        t //= 2
    return t


def _add_my_half(g, r, name):
    n_slabs, _, R, C = g.shape
    tr = R if n_slabs > 1 else _row_tile(R, SUM_ROWS)

    def body(c_ref, g_ref, r_ref, o_ref):
        o_ref[...] = (g_ref[0] + r_ref[...]).astype(o_ref.dtype)

    return pl.pallas_call(
        body,
        name=name,
        grid_spec=pltpu.PrefetchScalarGridSpec(
            num_scalar_prefetch=1,
            grid=(n_slabs, R // tr),
            in_specs=[pl.BlockSpec((1, 1, tr, C), lambda p, i, c_ref: (p, c_ref[0], i, 0)),
                      pl.BlockSpec((1, tr, C), lambda p, i, c_ref: (p, i, 0))],
            out_specs=pl.BlockSpec((1, tr, C), lambda p, i, c_ref: (p, i, 0)),
        ),
        out_shape=jax.ShapeDtypeStruct(r.shape, jnp.bfloat16),
        compiler_params=_params(("parallel", "parallel")),
    )(lax.axis_index("c").reshape(1).astype(jnp.int32), g, r)


def _sum_slabs(own, got, name, deps=()):
    _, R, C = own.shape
    tr = _row_tile(R, SUM_ROWS)

    def body(s_ref, own_ref, got_ref, *rest):
        rest[-1][0] = ((own_ref[0].astype(F32) + got_ref[0].astype(F32)) + got_ref[1].astype(F32)) + got_ref[2].astype(F32)

    xi, yi, ci = _coords()
    return pl.pallas_call(
        body,
        name=name,
        grid_spec=pltpu.PrefetchScalarGridSpec(
            num_scalar_prefetch=1,
            grid=(R // tr,),
            in_specs=[pl.BlockSpec((1, tr, C), lambda i, s: (s[0], i, 0)),
                      pl.BlockSpec((3, tr, C), lambda i, s: (0, i, 0))] + [ANY_SPEC] * len(deps),
            out_specs=pl.BlockSpec((1, tr, C), lambda i, s: (s[1], i, 0)),
        ),
        out_shape=jax.ShapeDtypeStruct((2, R, C), F32),
        compiler_params=_params(("parallel",)),
    )(jnp.stack([2 * xi + yi, ci]).astype(jnp.int32), own, got, *deps)


def _sum_parts(owns, got, dest_sets, name):
    n = len(owns)
    _, R, C = owns[0].shape
    tr = _row_tile(R, SUM_ROWS)

    def body(s_ref, *refs):
        got_ref, o_ref = refs[n], refs[-1]
        total = jnp.zeros((tr, C), F32)
        for i in range(n):
            total = total + jnp.where(s_ref[2 + 2 * i] == 1, refs[i][0].astype(F32), 0.0)
        o_ref[0] = ((total + got_ref[0].astype(F32)) + got_ref[1].astype(F32)) + got_ref[2].astype(F32)

    xi, yi, ci = _coords()
    me = 2 * xi + yi
    scalars = [ci, ci]
    for dests in dest_sets:
        scalars += [_is_one_of(me, dests).astype(jnp.int32), _slab_of(me, dests)]
    own_spec = lambda i: pl.BlockSpec((1, tr, C), lambda r, s: (s[3 + 2 * i], r, 0))
    return pl.pallas_call(
        body,
        name=name,
        grid_spec=pltpu.PrefetchScalarGridSpec(
            num_scalar_prefetch=1,
            grid=(R // tr,),
            in_specs=[own_spec(i) for i in range(n)] + [pl.BlockSpec((3, tr, C), lambda r, s: (0, r, 0))],
            out_specs=pl.BlockSpec((1, tr, C), lambda r, s: (s[0], r, 0)),
        ),
        out_shape=jax.ShapeDtypeStruct((2, R, C), F32),
        compiler_params=_params(("parallel",)),
    )(jnp.stack(scalars).astype(jnp.int32), *owns, got)


def _adamw_math(w, g, m, v):
    m = ADAM_B1 * m + (1.0 - ADAM_B1) * g
    v = ADAM_B2 * v + (1.0 - ADAM_B2) * (g * g)
    m_hat = m / (1.0 - ADAM_B1 ** ADAM_STEP)
    v_hat = v / (1.0 - ADAM_B2 ** ADAM_STEP)
    delta = -ADAM_LR * (m_hat / (jnp.sqrt(v_hat) + ADAM_EPS) + ADAM_WD * w)
    return delta, m, v


def _adamw_halves(ws, g, ms, vs, half, prev, name, deps=()):
    n = len(ws)
    _, _, R, C = g.shape
    tr = _row_tile(R, ADAMW_ROWS)
    steps = R // tr
    carried = [] if prev is None else [a for four in prev for a in four]
    both = half is None
    which = (lambda i, s: i // steps) if both else (lambda i, s: s[0])
    half = 0 if both else half

    def body(s_ref, *refs):
        w_refs, g_refs, m_refs, v_refs = (refs[k * n:(k + 1) * n] for k in range(4))
        outs = refs[len(refs) - 4 * n:]
        for a in range(n):
            grad = g_refs[a][0, 0]
            d, mn, vn = _adamw_math(w_refs[a][...], grad, m_refs[a][...], v_refs[a][...])
            for o, val in zip(outs[4 * a:4 * a + 4], (grad, d, mn, vn)):
                o[...] = val

    rows = pl.BlockSpec((tr, C), lambda i, s: (which(i, s) * steps + i % steps, 0))
    grad_spec = lambda a: pl.BlockSpec((1, 1, tr, C), lambda i, s: (which(i, s), a, i % steps, 0))
    n_in = 4 * n
    outs = pl.pallas_call(
        body,
        name=name,
        grid_spec=pltpu.PrefetchScalarGridSpec(
            num_scalar_prefetch=1,
            grid=(2 * steps if both else steps,),
            in_specs=[rows] * n + [grad_spec(a) for a in range(n)] + [rows] * (2 * n)
            + [ANY_SPEC] * (len(carried) + len(deps)),
            out_specs=[rows] * (4 * n),
        ),
        out_shape=[jax.ShapeDtypeStruct((2 * R, C), F32)] * (4 * n),
        input_output_aliases={1 + n_in + k: k for k in range(len(carried))},
        compiler_params=_params(("parallel",)),
    )(jnp.reshape(half, (1,)).astype(jnp.int32), *ws, *([g] * n), *ms, *vs, *carried, *deps)
    return [outs[4 * a:4 * a + 4] for a in range(n)]


def _adamw_small(ws, gs, ms, vs, name):
    n = len(ws)

    def body(*refs):
        for a in range(n):
            d, mn, vn = _adamw_math(refs[a][...], refs[n + a][...], refs[2 * n + a][...], refs[3 * n + a][...])
            refs[4 * n + a][...] = d
            refs[5 * n + a][...] = mn
            refs[6 * n + a][...] = vn

    shapes = [jax.ShapeDtypeStruct(w.shape, F32) for w in ws]
    outs = pl.pallas_call(
        body,
        name=name,
        out_shape=shapes * 3,
        compiler_params=_params(),
    )(*ws, *gs, *ms, *vs)
    return outs[:n], outs[n:2 * n], outs[2 * n:]


def _to_blockdiag(w):
    per = CW // LRU_BW
    w4 = w.reshape(N_CT, per, LRU_BW, LRU_BW)
    eye = jnp.eye(per, dtype=w.dtype)
    return (w4[:, :, :, None, :] * eye[None, :, None, :, None]).reshape(N_CT, CW, CW)


def _from_blockdiag(g):
    per = CW // LRU_BW
    g5 = g.reshape(N_CT, per, LRU_BW, per, LRU_BW)
    return jnp.stack([g5[:, b, :, b, :] for b in range(per)], axis=1).reshape(LRU_BLOCKS, LRU_BW, LRU_BW)


def _local_grads(x2d, tgt2d, B, S, g_in, in_proj, conv_b, gate_x_w, gate_x_b, gate_a_w, gate_a_b, lam,
                 proj_weights, g_fin, reduce):
    wx_bd = _c(_to_blockdiag(gate_x_w))
    wa_bd = _c(_to_blockdiag(gate_a_w))
    tables = _retention_tables(S)

    proj, ht, w_all, conv_w, gain = in_proj(x2d, g_in, (tables[0], tables[1], wx_bd, wa_bd))
    gain3 = gain.reshape(HEADS, 1, DK)
    hlru, ya = _lru_fwd(proj, conv_w, conv_b, wx_bd, wa_bd, gate_x_b, gate_a_b, lam, B, S)
    o_pre, yb, states = _ret_fwd(proj, tables, gain3, B, S)
    wpa, wpb, wout = proj_weights(yb)
    loss, dx2, dya, dyb, dm, dgf, gw_proj = _mid(ya, yb, proj, x2d, tgt2d, wpa, wpb, wout, g_fin)
    g3 = _inproj_bwd_dw(ht, [dm], "inproj_bwd_dw_m")
    deps = reduce.m_ready(gw_proj, g3)
    dr, dgain = _ret_bwd(dyb, o_pre, proj, states, tables, gain3, B, S, deps)
    deps = reduce.ret_done(dr)
    g12 = _inproj_bwd_dw(ht, [dr], "inproj_bwd_dw_r", deps)
    deps = reduce.r_ready(g12)
    dxa, dga, dcw, dcb, dwx_bd, dwa_bd, dbx, dba, dlam = _lru_bwd(
        dya, proj, hlru, conv_w, conv_b, wx_bd, wa_bd, gate_x_b, gate_a_b, lam, B, S, deps)
    small = dict(conv_w=dcw, conv_b=dcb, gate_x_w=_from_blockdiag(dwx_bd), gate_x_b=dbx,
                 gate_a_w=_from_blockdiag(dwa_bd), gate_a_b=dba, lru_lambda=dlam, gn_gain=dgain.reshape(HEADS, DK),
                 norm_final=dgf)
    loss_rows = jnp.broadcast_to(loss, (SUBLANES, LANES))
    deps = reduce.lru_done(dxa, jnp.concatenate([_pack_small(small), loss_rows], axis=0))
    g0 = _inproj_bwd_dw(ht, [dxa, dga], "inproj_bwd_dw_a", deps)
    deps = reduce.a_ready(g0)
    n_tiles = x2d.shape[0] // min(DX_TILE, x2d.shape[0])
    grad_x, dgin = _inproj_bwd_dx([dxa, dga, dr, dm], w_all, x2d, dx2, g_in, 0, n_tiles, None, "inproj_bwd_dx", deps)
    return grad_x, dgin


ALL_CHIPS = (0, 1, 2, 3)


class _GradReduce:
    def __init__(self, proj_done):
        self.pending = {}
        self.proj_done = proj_done
        self.land_in = None

    def _start(self, key, parts, name):
        bufs, plans, shared = [], [], None
        for part_bufs, plan, n_copies, part_shared in parts:
            if part_shared is not None:
                shared = len(bufs) + part_shared
            plans.append((plan, len(part_bufs), n_copies))
            bufs += part_bufs
        plan = _join_plans(plans)
        send_sems, recv_sems, bufs, token = _copies_start(bufs, plan, sum(p[2] for p in plans), name + "_start")
        if shared is not None:
            self.land_in = bufs[shared]
        self.pending[key] = (send_sems, recv_sems, bufs, plan, name + "_wait", shared)
        return (token,)

    def _finish(self, key, after):
        send_sems, recv_sems, bufs, plan, name, shared = self.pending.pop(key)
        if shared is not None:
            bufs[shared] = self.land_in
        bufs = _copies_wait(send_sems, recv_sems, bufs, after, plan, name)
        if shared is not None:
            self.land_in = bufs[shared]
        return bufs

    @staticmethod
    def _swap(pieces):
        bufs = []
        for g in pieces:
            bufs += [g, lax.empty((g.shape[0],) + g.shape[2:], F32)]
        n_slabs = [g.shape[0] for g in pieces]
        return bufs, _swap_plan(n_slabs), sum(n_slabs), None

    def _scatter(self, sums, dest_sets):
        bufs = []
        for cs in sums:
            bufs += [cs, lax.empty((3,) + cs.shape[1:], cs.dtype)]
        if self.land_in is not None:
            bufs[-1] = self.land_in
        return bufs, _scatter_plan(dest_sets), 3 * len(sums), len(bufs) - 1

    @staticmethod
    def _gather8(block):
        x, y, c = _coords()
        land = lax.dynamic_update_slice(lax.empty((8,) + block.shape, F32), block[None], (4 * x + 2 * y + c, 0, 0))
        return [land], _allgather_plan(), 7, None

    def m_ready(self, gw_proj, g3):
        rows = gw_proj.shape[2] * gw_proj.shape[3]
        return self._start("m", [self._swap([gw_proj.reshape(N_CHIPS, 2, rows, D_MODEL), g3])], "swap_m")

    def ret_done(self, after):
        proj, land_p, g3, land_3 = self._finish("m", after)
        sums_m = [_add_my_half(proj, land_p, "chip_sum_proj"), _add_my_half(g3, land_3, "chip_sum_m")]
        return self._start("sm", [self._scatter(sums_m, [ALL_CHIPS, (3,)])], "scatter_m")

    def r_ready(self, g12):
        return self._start("r", [self._swap([g12])], "swap_r")

    def lru_done(self, after, packed):
        g12, land_12 = self._finish("r", after)
        sums_r = [_add_my_half(g12, land_12, "chip_sum_r")]
        return (self._start("sr", [self._scatter(sums_r, [(1, 2)])], "scatter_r")
                + self._start("small", [self._gather8(packed)], "gather_small"))

    def a_ready(self, g0):
        (token,) = self._start("a", [self._swap([g0])], "swap_a")
        csp, gotp, self.cs3, _ = self._finish("sm", token)
        half_proj = _sum_slabs(csp, gotp, "sum_w_proj")
        g0, land_0 = self._finish("a", half_proj)
        deps = self._start("sa", [self._scatter([_add_my_half(g0, land_0, "chip_sum_a")], [(0,)])], "scatter_a")
        self.proj_done(_join_halves([half_proj], [4], "join_halves_proj", deps)[0])
        return deps

    def finish(self, dgin, w_in_done):
        (token,) = self._start("n", [self._gather8(dgin)], "gather_norm_in")
        (small,) = self._finish("small", token)
        cs12, _ = self._finish("sr", token)
        cs0, _ = self._finish("sa", token)
        half_in = _sum_parts([self.cs3, cs12, cs0], self.land_in, [(3,), (1, 2), (0,)], "sum_w_in")
        deps = self._start("j", [([half_in], _join_plan(half_in.shape[1], JOIN_PIECES), JOIN_PIECES, None)], "join_w_in")
        first = w_in_done(self.pending["j"][2][0], True, None, deps)
        (g_in,) = self._finish("j", first[1])
        done = w_in_done(g_in, False, first, ())
        (norm_in,) = self._finish("n", done[1])
        return _sum_gathered(small, "sum_small_grads"), _sum_gathered(norm_in, "sum_norm_in_grad")


_SMALL = ("gate_x_w", "gate_a_w", "conv_w", "conv_b", "gate_x_b", "gate_a_b", "lru_lambda", "gn_gain", "norm_final")
_SMALL_SHAPES = dict(gate_x_w=(LRU_BLOCKS, LRU_BW, LRU_BW), gate_a_w=(LRU_BLOCKS, LRU_BW, LRU_BW),
                     norm_in=(1, D_MODEL), conv_w=(CONV, D_MODEL), conv_b=(1, D_MODEL), gate_x_b=(1, D_MODEL),
                     gate_a_b=(1, D_MODEL), lru_lambda=(1, D_MODEL), gn_gain=(HEADS, DK), norm_final=(1, D_MODEL))


def _pack_small(small):
    return jnp.concatenate([small[k].reshape(-1, 128) for k in _SMALL], axis=0)


def _unpack_small(packed):
    out, r = {}, 0
    for k in _SMALL:
        shape = _SMALL_SHAPES[k]
        rows = 1
        for s in shape:
            rows *= s
        rows //= 128
        out[k] = packed[r:r + rows].reshape(shape)
        r += rows
    return out


def kernel(x, norm_in, w_in, conv_w, conv_b, gate_x_w, gate_x_b, gate_a_w, gate_a_b, lru_lambda, gn_gain, w_proj_a, w_proj_b, w_out, norm_final, loss_target, m_norm_in, m_w_in, m_conv_w, m_conv_b, m_gate_x_w, m_gate_x_b, m_gate_a_w, m_gate_a_b, m_lru_lambda, m_gn_gain, m_w_proj_a, m_w_proj_b, m_w_out, m_norm_final, v_norm_in, v_w_in, v_conv_w, v_conv_b, v_gate_x_w, v_gate_x_b, v_gate_a_w, v_gate_a_b, v_lru_lambda, v_gn_gain, v_w_proj_a, v_w_proj_b, v_w_out, v_norm_final):
    B, S, _ = x.shape
    T = B * S
    xi, yi, ci = _coords()
    chip = 2 * xi + yi

    cshard = D_MODEL // N_CHIPS
    mine = _cast_into_slot([w_in[0].reshape(2, D_MODEL // 2, 2 * D_MODEL)], "cast_w_in")
    plan = _gather_plan(3)
    pending_proj = []
    gshard = DK // N_CHIPS
    tiny = jnp.concatenate([conv_w[0], jnp.zeros((4, cshard), F32), jnp.pad(gn_gain[0], ((0, 4), (0, cshard - gshard)))],
                           axis=0).reshape(1, 2, SUBLANES, cshard)
    tiny_buf = lax.dynamic_update_slice(lax.empty((N_CHIPS, 2, SUBLANES, cshard), F32), tiny, (chip, 0, 0, 0))
    near_plan, pass_plan, far_plan = (_chip_gather_plan(stage, 2) for stage in ("near", "pass", "far"))
    (n_near, _), (n_pass, passed_on), (n_far, _) = (_chip_gather_copies(stage, 2) for stage in ("near", "pass", "far"))
    halves = set(range(n_pass)) - passed_on
    near_s, near_r, bufs, near_token = _copies_start([mine[0], tiny_buf], near_plan, n_near, "gather_near_start")

    def in_proj(x2d, g_in, meanwhile):
        mine_proj = _cast_into_slot([w[0].reshape(2, cshard // 2, D_MODEL) for w in (w_proj_a, w_proj_b, w_out)],
                                    "cast_w_proj", (near_token,))
        as_w = lambda b: b[0].reshape(N_CHIPS, D_MODEL, 2 * D_MODEL)
        slot_x, slot_y, slot_d = 2 * (1 - xi) + yi, 2 * xi + (1 - yi), 2 * (1 - xi) + (1 - yi)
        ids = lambda *chips: jnp.stack(chips).astype(jnp.int32)
        proj, hb, ht = _inproj_first(x2d, g_in, as_w(bufs), ids(chip), "inproj_own", (near_token, *meanwhile))
        got = _copies_wait(near_s, near_r, bufs, proj, near_plan, "gather_near_wait")
        pass_s, pass_r, got, pass_token = _copies_start(got, pass_plan, n_pass, "gather_pass_start")
        got = _copies_wait(pass_s, pass_r, got, pass_token, pass_plan, "gather_pass_wait_halves", only=halves)
        proj = _inproj_more(hb, as_w(got), ids(slot_x, slot_y), proj, "inproj_near")
        got = _copies_wait(pass_s, pass_r, got, proj, pass_plan, "gather_pass_wait_far", only=passed_on)
        pending_proj.append(_copies_start(mine_proj, plan, 9, "gather_proj_start", (got[0],)))
        far_s, far_r, got, far_token = _copies_start(got, far_plan, n_far, "gather_far_start")
        got = _copies_wait(far_s, far_r, got, far_token, far_plan, "gather_far_wait")
        proj = _inproj_more(hb, as_w(got), ids(slot_d), proj, "inproj_far")
        tiny_all = got[1].reshape(N_CHIPS, 2 * SUBLANES, cshard)
        conv_w_full = jnp.transpose(tiny_all[:, 0:CONV, :], (1, 0, 2)).reshape(CONV, D_MODEL)
        gain_full = jnp.transpose(tiny_all[:, 8:8 + HEADS, :gshard], (1, 0, 2)).reshape(HEADS, DK)
        return proj, ht, as_w(got), conv_w_full, gain_full

    def proj_weights(after):
        s_sems, r_sems, pbufs, _ = pending_proj[0]
        got = _copies_wait(s_sems, r_sems, pbufs, after, plan, "gather_proj_wait")
        return [b.reshape(D_MODEL, D_MODEL) for b in got]

    weights = dict(norm_in=norm_in, w_in=w_in, conv_w=conv_w, conv_b=conv_b, gate_x_w=gate_x_w, gate_x_b=gate_x_b,
                   gate_a_w=gate_a_w, gate_a_b=gate_a_b, lru_lambda=lru_lambda, gn_gain=gn_gain, w_proj_a=w_proj_a,
                   w_proj_b=w_proj_b, w_out=w_out, norm_final=norm_final)
    ms = dict(norm_in=m_norm_in, w_in=m_w_in, conv_w=m_conv_w, conv_b=m_conv_b, gate_x_w=m_gate_x_w,
              gate_x_b=m_gate_x_b, gate_a_w=m_gate_a_w, gate_a_b=m_gate_a_b, lru_lambda=m_lru_lambda, gn_gain=m_gn_gain,
              w_proj_a=m_w_proj_a, w_proj_b=m_w_proj_b, w_out=m_w_out, norm_final=m_norm_final)
    vs = dict(norm_in=v_norm_in, w_in=v_w_in, conv_w=v_conv_w, conv_b=v_conv_b, gate_x_w=v_gate_x_w,
              gate_x_b=v_gate_x_b, gate_a_w=v_gate_a_w, gate_a_b=v_gate_a_b, lru_lambda=v_lru_lambda, gn_gain=v_gn_gain,
              w_proj_a=v_w_proj_a, w_proj_b=v_w_proj_b, w_out=v_w_out, norm_final=v_norm_final)
    names = list(weights)
    grads, delta, new_m, new_v = {}, {}, {}, {}

    def update_big(keys, g, half, prev, name, deps=()):
        two = lambda a: a.reshape(a.shape[1], a.shape[2])
        res = _adamw_halves([two(weights[k]) for k in keys], g, [two(ms[k]) for k in keys], [two(vs[k]) for k in keys],
                            half, prev, name, deps)
        for k, (gk, d, mn, vn) in zip(keys, res):
            shp = weights[k].shape
            grads[k], delta[k], new_m[k], new_v[k] = gk.reshape(shp), d.reshape(shp), mn.reshape(shp), vn.reshape(shp)
        return res

    def proj_done(g_proj):
        g4 = g_proj.reshape(2, 3, D_MODEL // (2 * N_CHIPS), D_MODEL)
        return update_big(("w_proj_a", "w_proj_b", "w_out"), g4, None, None, "adamw_proj")[-1][1]

    def w_in_done(g_in, own, prev, deps):
        g4 = g_in.reshape(2, 1, D_MODEL // 2, 2 * D_MODEL)
        return update_big(("w_in",), g4, ci if own else 1 - ci, None if prev is None else [prev],
                          "adamw_w_in_own" if own else "adamw_w_in_other", deps)[0]

    reduce = _GradReduce(proj_done)
    grad_x, dgin = _local_grads(
        x.reshape(T, D_MODEL), loss_target.reshape(T, D_MODEL), B, S, norm_in, in_proj, conv_b,
        gate_x_w[0], gate_x_b, gate_a_w[0], gate_a_b, lru_lambda, proj_weights,
        norm_final.reshape(1, D_MODEL), reduce)

    small_sum, g_norm_in = reduce.finish(dgin.reshape(SUBLANES, LANES), w_in_done)
    loss = small_sum[small_sum.shape[0] - SUBLANES, 0]

    gsm = _unpack_small(small_sum)
    gsm["norm_in"] = g_norm_in
    gsm["conv_w"] = lax.dynamic_slice_in_dim(gsm["conv_w"], chip * cshard, cshard, axis=1)
    gsm["gn_gain"] = lax.dynamic_slice_in_dim(gsm["gn_gain"], chip * gshard, gshard, axis=1)
    smalls = [k for k in names if k not in delta]

    def view(a):
        return a.reshape(1, -1) if a.ndim == 1 else (a.reshape(a.shape[1:]) if a.ndim > 2 else a)

    ds, mns, vns = _adamw_small([view(weights[k]) for k in smalls], [gsm[k].reshape(view(weights[k]).shape) for k in smalls],
                                [view(ms[k]) for k in smalls], [view(vs[k]) for k in smalls], "adamw_small")
    for k, d, mn, vn in zip(smalls, ds, mns, vns):
        shp = weights[k].shape
        grads[k], delta[k], new_m[k], new_v[k] = gsm[k].reshape(shp), d.reshape(shp), mn.reshape(shp), vn.reshape(shp)

    return (loss, grad_x.reshape(B, S, D_MODEL), *[grads[k] for k in names], *[delta[k] for k in names],
            *[new_m[k] for k in names], *[new_v[k] for k in names])
```

```python
import jax
import jax.numpy as jnp
from jax import lax
from jax.experimental import pallas as pl
from jax.experimental.pallas import tpu as pltpu

F32 = jnp.float32
_MXU = jnp.bfloat16

D_MODEL = 1024
N_GROUPS = 8
HEADS = 4
DK = 256
CHUNK = 128
CONV = 4
LRU_BLOCKS = 16
LRU_BW = 64
LRU_C = 8.0
ROPE_THETA = 10000.0
EPS = 1e-6
CW = 256
N_CT = D_MODEL // CW
N_CHIPS = 4
MESH = pl.DeviceIdType.MESH

ADAM_LR = 0.001
ADAM_B1 = 0.9
ADAM_B2 = 0.999
ADAM_EPS = 1e-08
ADAM_WD = 0.01
ADAM_STEP = 10

VMEM_LIMIT = 56 * 1024 * 1024

FIRST_PROJ_TILE = 1024
MORE_PROJ_TILE = 2048
SCAN_TILE = 1024
MID_TILE = 256
DX_TILE = 512
DW_COLS = 512
SUM_ROWS = 256
ADAMW_ROWS = 256
JOIN_PIECES = 8


def _c(v):
    return v.astype(_MXU)


def _dot(a, b):
    return lax.dot_general(a, b, (((1,), (0,)), ((), ())), preferred_element_type=F32)


def _dot_nt(a, b):
    return lax.dot_general(a, b, (((1,), (1,)), ((), ())), preferred_element_type=F32)


def _dot_tn(a, b):
    return lax.dot_general(a, b, (((0,), (0,)), ((), ())), preferred_element_type=F32)


def _sigmoid(z):
    return 0.5 * jnp.tanh(0.5 * z) + 0.5


ANY_SPEC = pl.BlockSpec(memory_space=pl.ANY)


def _after(body, n_in, deps):
    n_deps = len(deps)

    def wrapped(*refs):
        return body(*refs[:n_in], *refs[n_in + n_deps:])

    return wrapped


def _params(sem=None):
    if sem is None:
        return pltpu.CompilerParams(vmem_limit_bytes=VMEM_LIMIT)
    return pltpu.CompilerParams(vmem_limit_bytes=VMEM_LIMIT, dimension_semantics=sem)


def _inproj_first(x2d, g_in, w_all, chips, name, deps=()):
    T = x2d.shape[0]
    tm = min(FIRST_PROJ_TILE, T)
    n_i = T // tm

    def body(s_ref, *refs):
        x_ref, g_ref, w_ref = refs[:3]
        proj_ref, hb_ref, ht_ref, h_all = refs[-4:]
        i = pl.program_id(1)
        rows = pl.ds(pl.multiple_of(i * tm, tm), tm)

        @pl.when(pl.program_id(0) == 0)
        def _():
            x = x_ref[...]
            r = lax.rsqrt(jnp.mean(x * x, axis=-1, keepdims=True) + EPS)
            h = x * r * g_ref[...]
            hb = h.astype(h_all.dtype)
            h_all[rows, :] = hb
            hb_ref[...] = hb
            ht_ref[...] = h.T.astype(ht_ref.dtype)

        proj_ref[...] = _dot(h_all[rows, :], w_ref[0])

    first = lambda j, i: jnp.where(j == 0, i, n_i - 1)
    return pl.pallas_call(
        body,
        name=name,
        grid_spec=pltpu.PrefetchScalarGridSpec(
            num_scalar_prefetch=1,
            grid=(2 * chips.shape[0], n_i),
            in_specs=[
                pl.BlockSpec((tm, D_MODEL), lambda j, i, s: (first(j, i), 0)),
                pl.BlockSpec((1, D_MODEL), lambda j, i, s: (0, 0)),
                pl.BlockSpec((1, D_MODEL, D_MODEL), lambda j, i, s: (s[j // 2], 0, j % 2)),
            ] + [ANY_SPEC] * len(deps),
            out_specs=[
                pl.BlockSpec((tm, D_MODEL), lambda j, i, s: (i, 2 * s[j // 2] + j % 2)),
                pl.BlockSpec((tm, D_MODEL), lambda j, i, s: (first(j, i), 0)),
                pl.BlockSpec((D_MODEL, tm), lambda j, i, s: (0, first(j, i))),
            ],
            scratch_shapes=[pltpu.VMEM((T, D_MODEL), _MXU)],
        ),
        out_shape=[
            jax.ShapeDtypeStruct((T, N_GROUPS * D_MODEL), F32),
            jax.ShapeDtypeStruct((T, D_MODEL), _MXU),
            jax.ShapeDtypeStruct((D_MODEL, T), _MXU),
        ],
        compiler_params=_params(("arbitrary", "arbitrary")),
    )(chips, x2d, g_in, w_all, *deps)


def _inproj_more(hb, w_all, chips, proj, name):
    T = hb.shape[0]
    tm = min(MORE_PROJ_TILE, T)

    def body(s_ref, hb_hbm, w_ref, prev_ref, proj_ref, h_all, sem):
        @pl.when((pl.program_id(0) == 0) & (pl.program_id(1) == 0))
        def _():
            cp = pltpu.make_async_copy(hb_hbm, h_all, sem)
            cp.start()
            cp.wait()

        rows = pl.ds(pl.multiple_of(pl.program_id(1) * tm, tm), tm)
        proj_ref[...] = _dot(h_all[rows, :], w_ref[0])

    return pl.pallas_call(
        body,
        name=name,
        grid_spec=pltpu.PrefetchScalarGridSpec(
            num_scalar_prefetch=1,
            grid=(2 * chips.shape[0], T // tm),
            in_specs=[
                ANY_SPEC,
                pl.BlockSpec((1, D_MODEL, D_MODEL), lambda j, i, s: (s[j // 2], 0, j % 2)),
                ANY_SPEC,
            ],
            out_specs=pl.BlockSpec((tm, D_MODEL), lambda j, i, s: (i, 2 * s[j // 2] + j % 2)),
            scratch_shapes=[pltpu.VMEM((T, D_MODEL), hb.dtype), pltpu.SemaphoreType.DMA],
        ),
        out_shape=jax.ShapeDtypeStruct(proj.shape, F32),
        input_output_aliases={3: 0},
        compiler_params=_params(("arbitrary", "arbitrary")),
    )(chips, hb, w_all, proj)


def _scan_fwd(a, u):
    n = a.shape[0]
    row = lax.broadcasted_iota(jnp.int32, a.shape, 0)
    s = 1
    while s < n:
        m = row >= s
        u = u + a * jnp.where(m, pltpu.roll(u, s, 0), 0.0)
        a = a * jnp.where(m, pltpu.roll(a, s, 0), 1.0)
        s *= 2
    return a, u


def _scan_bwd(b, g):
    n = b.shape[0]
    row = lax.broadcasted_iota(jnp.int32, b.shape, 0)
    s = 1
    while s < n:
        m = row < n - s
        g = g + b * jnp.where(m, pltpu.roll(g, n - s, 0), 0.0)
        b = b * jnp.where(m, pltpu.roll(b, n - s, 0), 1.0)
        s *= 2
    return b, g


LANES = 128
SUBLANES = 8


def _scan_scratch(tc):
    by_lanes = pltpu.VMEM((CW // LANES, tc, LANES), F32)
    return [by_lanes, by_lanes, pltpu.VMEM((tc // SUBLANES, CW), F32), pltpu.VMEM((tc, CW), F32)]


def _scan_tile(a, u, edge, la_ref, lh_ref, c_ref, dst_ref, reverse):
    n, w = a.shape
    groups = n // SUBLANES
    a3 = a.reshape(groups, SUBLANES, w)
    u3 = u.reshape(groups, SUBLANES, w)
    row = lax.broadcasted_iota(jnp.int32, a3.shape, 1)
    for s in (1, 2, 4):
        m = (row < SUBLANES - s) if reverse else (row >= s)
        shift = SUBLANES - s if reverse else s
        u3 = u3 + a3 * jnp.where(m, pltpu.roll(u3, shift, 1), 0.0)
        a3 = a3 * jnp.where(m, pltpu.roll(a3, shift, 1), 1.0)
    al = a3.reshape(n, w)
    hl = u3.reshape(n, w)
    blocks = w // LANES
    for q in range(blocks):
        la_ref[q] = al[:, q * LANES:(q + 1) * LANES]
        lh_ref[q] = hl[:, q * LANES:(q + 1) * LANES]
    ends = pl.ds(0 if reverse else SUBLANES - 1, groups, stride=SUBLANES)
    end_a = jnp.concatenate([la_ref.at[q][ends, :] for q in range(blocks)], axis=-1)
    end_h = jnp.concatenate([lh_ref.at[q][ends, :] for q in range(blocks)], axis=-1)
    prod, part = (_scan_bwd if reverse else _scan_fwd)(end_a, end_h)
    total = part + prod * edge
    g_row = lax.broadcasted_iota(jnp.int32, total.shape, 0)
    if reverse:
        c_ref[...] = jnp.where(g_row == groups - 1, edge, pltpu.roll(total, groups - 1, 0))
    else:
        c_ref[...] = jnp.where(g_row == 0, edge, pltpu.roll(total, 1, 0))
    for g in range(groups):
        rows = slice(g * SUBLANES, (g + 1) * SUBLANES)
        for q in range(blocks):
            cols = slice(q * LANES, (q + 1) * LANES)
            dst_ref[rows, cols] = lh_ref[q, rows, :] + la_ref[q, rows, :] * c_ref[g:g + 1, cols]


def _softplus_neg(lam):
    z = -lam
    return jnp.maximum(z, 0.0) + jnp.log1p(jnp.exp(-jnp.abs(z)))


def _lru_gates(xc, wx_ref, wa_ref, bx_ref, ba_ref, lam_ref):
    xcb = _c(xc)
    i_t = _sigmoid(_dot(xcb, wx_ref[0]) + bx_ref[...])
    r_t = _sigmoid(_dot(xcb, wa_ref[0]) + ba_ref[...])
    sp = _softplus_neg(lam_ref[...])
    log_a = (-LRU_C) * r_t * sp
    a = jnp.exp(log_a)
    mult = jnp.sqrt(1.0 - a * a)
    return xcb, i_t, r_t, sp, a, mult


def _conv_from_ext(ext_ref, xa, cw_ref, cb_ref, tc):
    return (cb_ref[...] + cw_ref[3:4, :] * xa + cw_ref[2:3, :] * ext_ref[7:7 + tc, :]
            + cw_ref[1:2, :] * ext_ref[6:6 + tc, :] + cw_ref[0:1, :] * ext_ref[5:5 + tc, :])


def _lru_fwd(proj, conv_w, conv_b, wx_bd, wa_bd, bx, ba, lam, B, S):
    T = B * S
    tc = min(SCAN_TILE, S)
    nt = S // tc
    h8 = tc // 8

    def body(xa_ref, halo_ref, ga_ref, cw_ref, cb_ref, wx_ref, wa_ref, bx_ref, ba_ref, lam_ref,
             h_ref, ya_ref, ext_ref, carry_ref, la_ref, lh_ref, c_ref):
        t = pl.program_id(2)

        @pl.when(t == 0)
        def _():
            carry_ref[...] = jnp.zeros_like(carry_ref)

        xa = xa_ref[...]
        ext_ref[0:8, :] = jnp.where(t == 0, 0.0, halo_ref[...])
        ext_ref[8:8 + tc, :] = xa
        xc = _conv_from_ext(ext_ref, xa, cw_ref, cb_ref, tc)
        _, i_t, _, _, a, mult = _lru_gates(xc, wx_ref, wa_ref, bx_ref, ba_ref, lam_ref)
        u = mult * (i_t * xc)
        _scan_tile(a, u, carry_ref[7:8, :], la_ref, lh_ref, c_ref, h_ref, False)
        h = h_ref[...]
        carry_ref[...] = h[tc - 8:tc, :]
        ga = ga_ref[...]
        ya_ref[...] = (ga * _sigmoid(ga) * h).astype(ya_ref.dtype)

    row = lambda b, t: b * nt + t
    vec = pl.BlockSpec((1, CW), lambda b, c, t: (0, c))
    mat = pl.BlockSpec((1, CW, CW), lambda b, c, t: (c, 0, 0))
    return pl.pallas_call(
        body,
        name="lru_fwd",
        grid=(B, N_CT, nt),
        in_specs=[
            pl.BlockSpec((tc, CW), lambda b, c, t: (row(b, t), c)),
            pl.BlockSpec((8, CW), lambda b, c, t: (jnp.maximum(row(b, t) * h8 - 1, 0), c)),
            pl.BlockSpec((tc, CW), lambda b, c, t: (row(b, t), N_CT + c)),
            pl.BlockSpec((CONV, CW), lambda b, c, t: (0, c)),
            vec, mat, mat, vec, vec, vec,
        ],
        out_specs=[
            pl.BlockSpec((tc, CW), lambda b, c, t: (row(b, t), c)),
            pl.BlockSpec((tc, CW), lambda b, c, t: (row(b, t), c)),
        ],
        out_shape=[
            jax.ShapeDtypeStruct((T, D_MODEL), F32),
            jax.ShapeDtypeStruct((T, D_MODEL), _MXU),
        ],
        scratch_shapes=[pltpu.VMEM((tc + 8, CW), F32), pltpu.VMEM((8, CW), F32)] + _scan_scratch(tc)[:3],
        compiler_params=_params(("parallel", "parallel", "arbitrary")),
    )(proj, proj, proj, conv_w, conv_b, wx_bd, wa_bd, bx, ba, lam)


def _lru_bwd(dya, proj, hlru, conv_w, conv_b, wx_bd, wa_bd, bx, ba, lam, B, S, deps=()):
    T = B * S
    tc = min(SCAN_TILE, S)
    nt = S // tc
    h8 = tc // 8

    def body(dya_ref, xa_ref, xhalo_ref, ga_ref, h_ref, hhalo_ref, cw_ref, cb_ref, wx_ref, wa_ref, bx_ref, ba_ref,
             lam_ref, dxa_ref, dga_ref, dcw_ref, dcb_ref, dwx_ref, dwa_ref, dbx_ref, dba_ref, dlam_ref,
             ext_ref, ext2_ref, carry_ref, dhalo_ref, la_ref, lh_ref, c_ref, dh_ref):
        b = pl.program_id(1)
        t = pl.program_id(2)
        tt = nt - 1 - t

        @pl.when(t == 0)
        def _():
            carry_ref[...] = jnp.zeros_like(carry_ref)
            dhalo_ref[...] = jnp.zeros_like(dhalo_ref)

        @pl.when((t == 0) & (b == 0))
        def _():
            for r in (dcw_ref, dcb_ref, dwx_ref, dwa_ref, dbx_ref, dba_ref, dlam_ref):
                r[...] = jnp.zeros_like(r)

        xa = xa_ref[...]
        ext_ref[0:8, :] = jnp.where(tt == 0, 0.0, xhalo_ref[...])
        ext_ref[8:8 + tc, :] = xa
        xc = _conv_from_ext(ext_ref, xa, cw_ref, cb_ref, tc)
        xcb, i_t, r_t, sp, a, mult = _lru_gates(xc, wx_ref, wa_ref, bx_ref, ba_ref, lam_ref)

        h = h_ref[...]
        ga = ga_ref[...]
        dya_t = dya_ref[...]
        sg = _sigmoid(ga)
        dga_ref[...] = (dya_t * h * (sg * (1.0 + ga * (1.0 - sg)))).astype(dga_ref.dtype)
        dlru = dya_t * (ga * sg)

        row = lax.broadcasted_iota(jnp.int32, a.shape, 0)
        coef = jnp.where(row == tc - 1, 1.0, pltpu.roll(a, tc - 1, 0))
        _scan_tile(coef, dlru, carry_ref[0:1, :], la_ref, lh_ref, c_ref, dh_ref, True)
        dh = dh_ref[...]
        ext2_ref[0:tc, :] = a * dh
        carry_ref[...] = ext2_ref[0:8, :]

        ext2_ref[0:8, :] = jnp.where(tt == 0, 0.0, hhalo_ref[...])
        ext2_ref[8:8 + tc, :] = h
        hprev = ext2_ref[7:7 + tc, :]

        da = dh * hprev
        ix = i_t * xc
        dmult = dh * ix
        di = dh * mult * xc
        dxc = dh * mult * i_t
        dlog_a = da * a - dmult * (a * a) / mult
        dr = dlog_a * ((-LRU_C) * sp)
        dlam_ref[...] += jnp.sum(dlog_a * r_t, axis=0, keepdims=True) * (LRU_C * _sigmoid(-lam_ref[...]))
        dza = dr * r_t * (1.0 - r_t)
        dzx = di * i_t * (1.0 - i_t)
        dzab = _c(dza)
        dzxb = _c(dzx)
        dxc = dxc + _dot_nt(dzxb, wx_ref[0]) + _dot_nt(dzab, wa_ref[0])
        dwx_ref[0] += _dot_tn(xcb, dzxb)
        dwa_ref[0] += _dot_tn(xcb, dzab)
        dbx_ref[...] += jnp.sum(dzx, axis=0, keepdims=True)
        dba_ref[...] += jnp.sum(dza, axis=0, keepdims=True)

        dcb_ref[...] += jnp.sum(dxc, axis=0, keepdims=True)
        dcw_ref[3:4, :] += jnp.sum(dxc * xa, axis=0, keepdims=True)
        dcw_ref[2:3, :] += jnp.sum(dxc * ext_ref[7:7 + tc, :], axis=0, keepdims=True)
        dcw_ref[1:2, :] += jnp.sum(dxc * ext_ref[6:6 + tc, :], axis=0, keepdims=True)
        dcw_ref[0:1, :] += jnp.sum(dxc * ext_ref[5:5 + tc, :], axis=0, keepdims=True)
        ext2_ref[0:tc, :] = dxc
        ext2_ref[tc:tc + 8, :] = dhalo_ref[...]
        dxa = (cw_ref[3:4, :] * dxc + cw_ref[2:3, :] * ext2_ref[1:1 + tc, :]
               + cw_ref[1:2, :] * ext2_ref[2:2 + tc, :] + cw_ref[0:1, :] * ext2_ref[3:3 + tc, :])
        dxa_ref[...] = dxa.astype(dxa_ref.dtype)
        dhalo_ref[...] = ext2_ref[0:8, :]

    row_of = lambda b, t: b * nt + (nt - 1 - t)
    tile = lambda off: pl.BlockSpec((tc, CW), lambda c, b, t: (row_of(b, t), off + c))
    halo = pl.BlockSpec((8, CW), lambda c, b, t: (jnp.maximum(row_of(b, t) * h8 - 1, 0), c))
    vec = pl.BlockSpec((1, CW), lambda c, b, t: (0, c))
    mat = pl.BlockSpec((1, CW, CW), lambda c, b, t: (c, 0, 0))
    cwspec = pl.BlockSpec((CONV, CW), lambda c, b, t: (0, c))
    return pl.pallas_call(
        _after(body, 13, deps),
        name="lru_bwd",
        grid=(N_CT, B, nt),
        in_specs=[tile(0), tile(0), halo, tile(N_CT), tile(0), halo, cwspec, vec, mat, mat, vec, vec, vec]
        + [ANY_SPEC] * len(deps),
        out_specs=[tile(0), tile(0), cwspec, vec, mat, mat, vec, vec, vec],
        out_shape=[
            jax.ShapeDtypeStruct((T, D_MODEL), _MXU),
            jax.ShapeDtypeStruct((T, D_MODEL), _MXU),
            jax.ShapeDtypeStruct((CONV, D_MODEL), F32),
            jax.ShapeDtypeStruct((1, D_MODEL), F32),
            jax.ShapeDtypeStruct((N_CT, CW, CW), F32),
            jax.ShapeDtypeStruct((N_CT, CW, CW), F32),
            jax.ShapeDtypeStruct((1, D_MODEL), F32),
            jax.ShapeDtypeStruct((1, D_MODEL), F32),
            jax.ShapeDtypeStruct((1, D_MODEL), F32),
        ],
        scratch_shapes=[pltpu.VMEM((tc + 8, CW), F32), pltpu.VMEM((tc + 8, CW), F32),
                        pltpu.VMEM((8, CW), F32), pltpu.VMEM((8, CW), F32)] + _scan_scratch(tc),
        compiler_params=_params(("parallel", "arbitrary", "arbitrary")),
    )(dya, proj, proj, proj, hlru, hlru, conv_w, conv_b, wx_bd, wa_bd, bx, ba, lam, *deps)


def _retention_tables(S):
    half = DK // 2
    freqs = ROPE_THETA ** (-jnp.arange(half, dtype=F32) / half)
    ang = jnp.arange(S, dtype=F32)[:, None] * freqs[None, :]
    log_g = jnp.log1p(-(2.0 ** (-5.0 - jnp.arange(HEADS, dtype=F32))))
    idx = jnp.arange(CHUNK, dtype=F32)
    diff = idx[:, None] - idx[None, :]
    inner = jnp.where(diff >= 0, jnp.exp(jnp.maximum(diff, 0.0)[None] * log_g[:, None, None]), 0.0)
    cross = jnp.exp((idx[None, :] + 1.0) * log_g[:, None])[:, :, None]
    state = jnp.exp((CHUNK - 1.0 - idx[None, :]) * log_g[:, None])[:, :, None]
    gam = jnp.broadcast_to(jnp.exp(CHUNK * log_g)[:, None, None], (HEADS, 1, DK))
    return jnp.cos(ang), jnp.sin(ang), inner, cross, state, gam


def _rot(x, cos, sin):
    half = DK // 2
    x1, x2 = x[:, :half], x[:, half:]
    return jnp.concatenate([x1 * cos - x2 * sin, x1 * sin + x2 * cos], axis=-1)


def _rot_t(y, cos, sin):
    half = DK // 2
    y1, y2 = y[:, :half], y[:, half:]
    return jnp.concatenate([y1 * cos + y2 * sin, y2 * cos - y1 * sin], axis=-1)


def _groupnorm(o):
    mu = jnp.mean(o, axis=-1, keepdims=True)
    oc = o - mu
    rs = lax.rsqrt(jnp.mean(oc * oc, axis=-1, keepdims=True) + EPS)
    return oc * rs, rs


def _ret_specs(B, chunk_of):
    qkv = lambda g: pl.BlockSpec((B, CHUNK, D_MODEL), lambda c: (0, chunk_of(c), g))
    act = pl.BlockSpec((B, CHUNK, D_MODEL), lambda c: (0, chunk_of(c), 0))
    rope = pl.BlockSpec((CHUNK, DK // 2), lambda c: (chunk_of(c), 0))
    dmat = pl.BlockSpec((HEADS, CHUNK, CHUNK), lambda c: (0, 0, 0))
    dvec = pl.BlockSpec((HEADS, CHUNK, 1), lambda c: (0, 0, 0))
    hrow = pl.BlockSpec((HEADS, 1, DK), lambda c: (0, 0, 0))
    rst = pl.BlockSpec((1, B, HEADS, DK, DK), lambda c: (chunk_of(c), 0, 0, 0, 0))
    return qkv, act, rope, dmat, dvec, hrow, rst


def _ret_fwd(proj, tables, gain3, B, S):
    T = B * S
    nc = S // CHUNK
    cos, sin, dmat_t, cd_t, sd_t, gam_t = tables

    def body(q_ref, k_ref, v_ref, gb_ref, cos_ref, sin_ref, dm_ref, cd_ref, sd_ref, gam_ref, gain_ref,
             o_ref, yb_ref, rs_ref, state_ref):
        @pl.when(pl.program_id(0) == 0)
        def _():
            state_ref[...] = jnp.zeros_like(state_ref)

        cos_t, sin_t = cos_ref[...], sin_ref[...]
        for b, h in [(b, h) for b in range(B) for h in range(HEADS)]:
            cols = slice(h * DK, (h + 1) * DK)
            qb = _c(_rot(q_ref[b, :, cols], cos_t, sin_t))
            kb = _c(_rot(k_ref[b, :, cols], cos_t, sin_t) * (DK ** -0.5))
            v = v_ref[b, :, cols]
            state = state_ref[b, h]
            sb = _c(state)
            rs_ref[0, b, h] = sb
            scores = _dot_nt(qb, kb) * dm_ref[h]
            o = _dot(_c(scores), _c(v)) + _dot(qb, sb) * cd_ref[h]
            state_ref[b, h] = gam_ref[h] * state + _dot_tn(kb, _c(v * sd_ref[h]))
            o_ref[b, :, cols] = o
            n, _ = _groupnorm(o)
            gb = gb_ref[b, :, cols]
            yb_ref[b, :, cols] = (gb * _sigmoid(gb) * (n * gain_ref[h])).astype(yb_ref.dtype)

    qkv, act, rope, dmat, dvec, hrow, rst = _ret_specs(B, lambda c: c)
    proj3 = proj.reshape(B, S, proj.shape[1])
    o_pre, yb, states = pl.pallas_call(
        body,
        name="ret_fwd",
        grid=(nc,),
        in_specs=[qkv(2), qkv(3), qkv(4), qkv(5), rope, rope, dmat, dvec, dvec, hrow, hrow],
        out_specs=[act, act, rst],
        out_shape=[
            jax.ShapeDtypeStruct((B, S, D_MODEL), F32),
            jax.ShapeDtypeStruct((B, S, D_MODEL), _MXU),
            jax.ShapeDtypeStruct((nc, B, HEADS, DK, DK), _MXU),
        ],
        scratch_shapes=[pltpu.VMEM((B, HEADS, DK, DK), F32)],
        compiler_params=_params(("arbitrary",)),
    )(proj3, proj3, proj3, proj3, cos, sin, dmat_t, cd_t, sd_t, gam_t, gain3)
    return o_pre.reshape(T, D_MODEL), yb.reshape(T, D_MODEL), states


def _ret_bwd(dyb, o_pre, proj, states, tables, gain3, B, S, deps=()):
    T = B * S
    nc = S // CHUNK
    cos, sin, dmat_t, cd_t, sd_t, gam_t = tables

    def body(dyb_ref, o_ref, q_ref, k_ref, v_ref, gb_ref, rs_ref, cos_ref, sin_ref, dm_ref, cd_ref, sd_ref, gam_ref,
             gain_ref, dr_ref, dgain_ref, dstate_ref):
        @pl.when(pl.program_id(0) == 0)
        def _():
            dstate_ref[...] = jnp.zeros_like(dstate_ref)
            dgain_ref[...] = jnp.zeros_like(dgain_ref)

        cos_t, sin_t = cos_ref[...], sin_ref[...]
        for b, h in [(b, h) for b in range(B) for h in range(HEADS)]:
            cols = slice(h * DK, (h + 1) * DK)
            gain = gain_ref[h]
            n, rs = _groupnorm(o_ref[b, :, cols])
            gb = gb_ref[b, :, cols]
            sg = _sigmoid(gb)
            dy = dyb_ref[b, :, cols]
            part = lambda g: slice(g * D_MODEL + h * DK, g * D_MODEL + (h + 1) * DK)
            dr_ref[b, :, part(3)] = (dy * (n * gain) * (sg * (1.0 + gb * (1.0 - sg)))).astype(dr_ref.dtype)
            dgn = dy * (gb * sg)
            dgain_ref[h] += jnp.sum(dgn * n, axis=0, keepdims=True)
            dn = dgn * gain
            do = rs * (dn - jnp.mean(dn, axis=-1, keepdims=True) - n * jnp.mean(dn * n, axis=-1, keepdims=True))

            qb = _c(_rot(q_ref[b, :, cols], cos_t, sin_t))
            kb = _c(_rot(k_ref[b, :, cols], cos_t, sin_t) * (DK ** -0.5))
            v = v_ref[b, :, cols]
            vb = _c(v)
            vsb = _c(v * sd_ref[h])
            dob = _c(do)
            docb = _c(do * cd_ref[h])
            dmat = dm_ref[h]
            dstate = dstate_ref[b, h]
            dsb = _c(dstate)
            pb = _c(_dot_nt(qb, kb) * dmat)
            dsc = _c(_dot_nt(dob, vb) * dmat)
            dq = _dot(dsc, kb) + _dot_nt(docb, rs_ref[0, b, h])
            dk = _dot_tn(dsc, qb) + _dot_nt(vsb, dsb)
            dv = _dot_tn(pb, dob) + _dot(kb, dsb) * sd_ref[h]
            dstate_ref[b, h] = gam_ref[h] * dstate + _dot_tn(qb, docb)
            dr_ref[b, :, part(0)] = _rot_t(dq, cos_t, sin_t).astype(dr_ref.dtype)
            dr_ref[b, :, part(1)] = (_rot_t(dk, cos_t, sin_t) * (DK ** -0.5)).astype(dr_ref.dtype)
            dr_ref[b, :, part(2)] = dv.astype(dr_ref.dtype)

    qkv, act, rope, dmat, dvec, hrow, rst = _ret_specs(B, lambda c: nc - 1 - c)
    wide = pl.BlockSpec((B, CHUNK, 4 * D_MODEL), lambda c: (0, nc - 1 - c, 0))
    proj3 = proj.reshape(B, S, proj.shape[1])
    dr, dgain = pl.pallas_call(
        _after(body, 14, deps),
        name="ret_bwd",
        grid=(nc,),
        in_specs=[act, act, qkv(2), qkv(3), qkv(4), qkv(5), rst, rope, rope, dmat, dvec, dvec, hrow, hrow]
        + [ANY_SPEC] * len(deps),
        out_specs=[wide, hrow],
        out_shape=[jax.ShapeDtypeStruct((B, S, 4 * D_MODEL), _MXU), jax.ShapeDtypeStruct((HEADS, 1, DK), F32)],
        scratch_shapes=[pltpu.VMEM((B, HEADS, DK, DK), F32)],
        compiler_params=_params(("arbitrary",)),
    )(dyb.reshape(B, S, D_MODEL), o_pre.reshape(B, S, D_MODEL), proj3, proj3, proj3, proj3, states, cos, sin, dmat_t,
      cd_t, sd_t, gam_t, gain3, *deps)
    return dr.reshape(T, 4 * D_MODEL), dgain


def _mid(ya, yb, proj, x2d, tgt2d, wpa, wpb, wout, g_fin):
    T = x2d.shape[0]
    tm = min(MID_TILE, T)
    n_steps = T // tm
    rows = D_MODEL // (2 * N_CHIPS)

    def body(ya_ref, yb_ref, ma_ref, mb_ref, x_ref, t_ref, gf_ref, wpa_hbm, wpb_hbm, wout_hbm,
             loss_ref, dx2_ref, dya_ref, dyb_ref, dm_ref, dgf_ref, gw_hbm, w_ref, acc_ref, sem):
        i = pl.program_id(0)

        @pl.when(i == 0)
        def _():
            loads = [pltpu.make_async_copy(src, w_ref.at[k], sem.at[k]) for k, src in enumerate((wpa_hbm, wpb_hbm, wout_hbm))]
            for cp in loads:
                cp.start()
            for cp in loads:
                cp.wait()
            acc_ref[...] = jnp.zeros_like(acc_ref)
            loss_ref[...] = jnp.zeros_like(loss_ref)
            dgf_ref[...] = jnp.zeros_like(dgf_ref)

        ya_t, yb_t = ya_ref[...], yb_ref[...]
        out_a = _dot(ya_t, w_ref[0])
        out_b = _dot(yb_t, w_ref[1])
        sa = _sigmoid(ma_ref[...])
        sb = _sigmoid(mb_ref[...])
        mgb = _c(sa * out_a + sb * out_b)
        x2 = x_ref[...] + _dot(mgb, w_ref[2])
        r2 = lax.rsqrt(jnp.mean(x2 * x2, axis=-1, keepdims=True) + EPS)
        nx = x2 * r2
        gf = gf_ref[...]
        err = nx * gf - t_ref[...]
        loss_ref[...] += 0.5 * jnp.sum(jnp.mean(err * err, axis=-1, keepdims=True), axis=0, keepdims=True)
        dy = err * (1.0 / D_MODEL)
        dgf_ref[...] += jnp.sum(dy * nx, axis=0, keepdims=True)
        dyg = dy * gf
        dx2 = r2 * (dyg - nx * jnp.mean(dyg * nx, axis=-1, keepdims=True))
        dx2_ref[...] = dx2
        dx2b = _c(dx2)
        dmg = _dot_nt(dx2b, w_ref[2])
        acc_ref[2] += _dot_tn(mgb, dx2b)
        dm_ref[:, :D_MODEL] = (dmg * out_a * sa * (1.0 - sa)).astype(dm_ref.dtype)
        dm_ref[:, D_MODEL:] = (dmg * out_b * sb * (1.0 - sb)).astype(dm_ref.dtype)
        dab = _c(dmg * sa)
        dbb = _c(dmg * sb)
        dya_ref[...] = _dot_nt(dab, w_ref[0])
        dyb_ref[...] = _dot_nt(dbb, w_ref[1])
        acc_ref[0] += _dot_tn(ya_t, dab)
        acc_ref[1] += _dot_tn(yb_t, dbb)

        @pl.when(i == n_steps - 1)
        def _():
            copies = [pltpu.make_async_copy(acc_ref.at[k, pl.ds((2 * p + hf) * rows, rows), :], gw_hbm.at[p, hf, k],
                                            sem.at[(k * N_CHIPS + p) * 2 + hf])
                      for k in range(3) for p in range(N_CHIPS) for hf in range(2)]
            for cp in copies:
                cp.start()
            for cp in copies:
                cp.wait()

    tile = lambda j: pl.BlockSpec((tm, D_MODEL), lambda i: (i, j))
    one = pl.BlockSpec((1, D_MODEL), lambda i: (0, 0))
    anyspec = pl.BlockSpec(memory_space=pl.ANY)
    return pl.pallas_call(
        body,
        name="mid",
        grid=(n_steps,),
        in_specs=[tile(0), tile(0), tile(6), tile(7), tile(0), tile(0), one, anyspec, anyspec, anyspec],
        out_specs=[pl.BlockSpec((1, 1), lambda i: (0, 0)), tile(0), tile(0), tile(0),
                   pl.BlockSpec((tm, 2 * D_MODEL), lambda i: (i, 0)), one, anyspec],
        out_shape=[
            jax.ShapeDtypeStruct((1, 1), F32),
            jax.ShapeDtypeStruct((T, D_MODEL), F32),
            jax.ShapeDtypeStruct((T, D_MODEL), F32),
            jax.ShapeDtypeStruct((T, D_MODEL), F32),
            jax.ShapeDtypeStruct((T, 2 * D_MODEL), _MXU),
            jax.ShapeDtypeStruct((1, D_MODEL), F32),
            jax.ShapeDtypeStruct((N_CHIPS, 2, 3, rows, D_MODEL), F32),
        ],
        scratch_shapes=[pltpu.VMEM((3, D_MODEL, D_MODEL), _MXU), pltpu.VMEM((3, D_MODEL, D_MODEL), F32),
                        pltpu.SemaphoreType.DMA((3 * N_CHIPS * 2,))],
        compiler_params=_params(("arbitrary",)),
    )(ya, yb, proj, proj, x2d, tgt2d, g_fin, wpa, wpb, wout)


def _inproj_bwd_dx(dparts, w_all, x2d, dx2, g_in, first, count, prev, name, deps=()):
    T = x2d.shape[0]
    tm = min(DX_TILE, T)
    n_d = len(dparts)
    groups = [(a, k) for a, d in enumerate(dparts) for k in range(d.shape[1] // D_MODEL)]
    dg_start = jnp.zeros((1, D_MODEL), F32) if prev is None else prev[1]
    carried = () if prev is None else (prev[0],)

    def body(*refs):
        d_refs = refs[:n_d]
        x_ref, dx2_ref, g_ref, dg0_ref, w_hbm = refs[n_d:n_d + 5]
        dx_ref, dg_ref, w_ref, sem = refs[-4:]

        @pl.when(pl.program_id(0) == 0)
        def _():
            cp = pltpu.make_async_copy(w_hbm, w_ref, sem)
            cp.start()
            cp.wait()
            dg_ref[...] = dg0_ref[...]

        dh = jnp.zeros((tm, D_MODEL), F32)
        for j, (a, k) in enumerate(groups):
            dh = dh + _dot_nt(d_refs[a][:, k * D_MODEL:(k + 1) * D_MODEL],
                              w_ref[j // 2, :, (j % 2) * D_MODEL:(j % 2 + 1) * D_MODEL])
        x = x_ref[...]
        r = lax.rsqrt(jnp.mean(x * x, axis=-1, keepdims=True) + EPS)
        nx = x * r
        dg_ref[...] += jnp.sum(dh * nx, axis=0, keepdims=True)
        dhg = dh * g_ref[...]
        dx_ref[...] = dx2_ref[...] + r * (dhg - nx * jnp.mean(dhg * nx, axis=-1, keepdims=True))

    tile = pl.BlockSpec((tm, D_MODEL), lambda i: (first + i, 0))
    one = pl.BlockSpec((1, D_MODEL), lambda i: (0, 0))
    return pl.pallas_call(
        body,
        name=name,
        grid=(count,),
        in_specs=[pl.BlockSpec((tm, d.shape[1]), lambda i: (first + i, 0)) for d in dparts]
        + [tile, tile, one, one, ANY_SPEC] + [ANY_SPEC] * (len(carried) + len(deps)),
        out_specs=[tile, one],
        out_shape=[jax.ShapeDtypeStruct((T, D_MODEL), F32), jax.ShapeDtypeStruct((1, D_MODEL), F32)],
        input_output_aliases={n_d + 5: 0} if carried else {},
        scratch_shapes=[pltpu.VMEM(w_all.shape, w_all.dtype), pltpu.SemaphoreType.DMA],
        compiler_params=_params(("arbitrary",)),
    )(*dparts, x2d, dx2, g_in, dg_start, w_all, *carried, *deps)


def _inproj_bwd_dw(ht, dparts, name, deps=()):
    T = ht.shape[1]
    tn = DW_COLS
    half = D_MODEL // 2
    per_chip = 2 * D_MODEL // tn
    n_d = len(dparts)
    tiles = [(a, t) for a, d in enumerate(dparts) for t in range(d.shape[1] // tn)]
    offs = [sum(d.shape[1] // tn for d in dparts[:a]) for a in range(n_d)]

    def body(*refs):
        ht_ref = refs[0]
        d_refs = refs[1:1 + n_d]
        out_ref = refs[-1]
        t = pl.program_id(0)

        for a in range(n_d):
            lo, hi = offs[a], offs[a] + dparts[a].shape[1] // tn

            @pl.when((t >= lo) & (t < hi))
            def _(a=a):
                g = _dot(ht_ref[...], d_refs[a][...])
                out_ref[0, 0] = g[:half]
                out_ref[0, 1] = g[half:]

    def dspec(a):
        n_a = dparts[a].shape[1] // tn
        return pl.BlockSpec((T, tn), lambda t: (0, jnp.clip(t - offs[a], 0, n_a - 1)))

    return pl.pallas_call(
        body,
        name=name,
        grid=(len(tiles),),
        in_specs=[pl.BlockSpec((D_MODEL, T), lambda t: (0, 0))] + [dspec(a) for a in range(n_d)]
        + [ANY_SPEC] * len(deps),
        out_specs=pl.BlockSpec((1, 2, half, tn), lambda t: (t // per_chip, 0, 0, t % per_chip)),
        out_shape=jax.ShapeDtypeStruct((len(tiles) // per_chip, 2, half, 2 * D_MODEL), F32),
        compiler_params=_params(("parallel",)),
    )(ht, *dparts, *deps)


def _coords():
    return lax.axis_index("x"), lax.axis_index("y"), lax.axis_index("c")


def _other_chips(x, y):
    return [(1 - x, y), (x, 1 - y), (1 - x, 1 - y)]


def _chunks(rows, n):
    size = rows // n
    return [pl.ds(q * size, size) for q in range(n)]


HBM_SPEC = pl.BlockSpec(memory_space=pltpu.HBM)
SEM_SPEC = pl.BlockSpec(memory_space=pltpu.SEMAPHORE)
DATAFLOW = pltpu.SideEffectType.DATAFLOW_SIDE_EFFECTING


def _copies_start(bufs, plan, n_copies, name, deps=()):
    n = len(bufs)
    n_deps = len(deps)

    def body(*refs):
        ins = refs[:n]
        send_sems, recv_sems = refs[n + n_deps], refs[n + n_deps + 1]
        token = refs[-1]
        for k, send, _ in plan(ins):
            if send is not None:
                src, dst, dev, pred = send
                cp = pltpu.make_async_remote_copy(src_ref=src, dst_ref=dst, send_sem=send_sems.at[k],
                                                  recv_sem=recv_sems.at[k], device_id=dev, device_id_type=MESH)
                if pred is None:
                    cp.start()
                else:
                    pl.when(pred)(cp.start)
        token[...] = jnp.zeros_like(token)

    hbm = [pltpu.with_memory_space_constraint(b, pltpu.HBM) for b in bufs]
    outs = pl.pallas_call(
        body,
        name=name,
        in_specs=[HBM_SPEC] * n + [ANY_SPEC] * n_deps,
        out_specs=(SEM_SPEC, SEM_SPEC, *([HBM_SPEC] * n), pl.BlockSpec(memory_space=pltpu.VMEM)),
        out_shape=(pltpu.SemaphoreType.DMA((n_copies,)), pltpu.SemaphoreType.DMA((n_copies,)),
                   *[pltpu.HBM(b.shape, b.dtype) for b in bufs], jax.ShapeDtypeStruct((8, 128), F32)),
        input_output_aliases={a: 2 + a for a in range(n)},
        compiler_params=pltpu.CompilerParams(has_side_effects=DATAFLOW),
    )(*hbm, *deps)
    return outs[0], outs[1], list(outs[2:2 + n]), outs[-1]


def _copies_wait(send_sems, recv_sems, bufs, after, plan, name, only=None):
    n = len(bufs)

    def body(*refs):
        ins = refs[:n]
        s_sems, r_sems = refs[n], refs[n + 1]
        for k, send, recv in plan(ins):
            if only is not None and k not in only:
                continue
            if send is not None:
                src, dst, dev, pred = send
                cp = pltpu.make_async_remote_copy(src_ref=src, dst_ref=dst, send_sem=s_sems.at[k],
                                                  recv_sem=r_sems.at[k], device_id=dev, device_id_type=MESH)
                if pred is None:
                    cp.wait_send()
                else:
                    pl.when(pred)(cp.wait_send)
            if recv is not None:
                dst, pred = recv
                cp = pltpu.make_async_remote_copy(src_ref=dst, dst_ref=dst, send_sem=s_sems.at[k],
                                                  recv_sem=r_sems.at[k], device_id=_coords(), device_id_type=MESH)
                if pred is None:
                    cp.wait_recv()
                else:
                    pl.when(pred)(cp.wait_recv)

    outs = pl.pallas_call(
        body,
        name=name,
        in_specs=[HBM_SPEC] * n + [SEM_SPEC, SEM_SPEC, pl.BlockSpec(memory_space=pl.ANY)],
        out_specs=[HBM_SPEC] * n,
        out_shape=[pltpu.HBM(b.shape, b.dtype) for b in bufs],
        input_output_aliases={a: a for a in range(n)},
        compiler_params=pltpu.CompilerParams(has_side_effects=DATAFLOW),
    )(*bufs, send_sems, recv_sems, after)
    return list(outs)


def _gather_plan(n_bufs):
    def plan(refs):
        x, y, c = _coords()
        me = 2 * x + y
        out = []
        for k, (px, py) in enumerate(_other_chips(x, y)):
            for a in range(n_bufs):
                out.append((k * n_bufs + a, (refs[a].at[me], refs[a].at[me], (px, py, c), None),
                            (refs[a].at[2 * px + py], None)))
        return out
    return plan


def _cast_into_slot(ws, name, deps=()):
    n = len(ws)
    nt = 2

    def body(s_ref, *refs):
        outs = refs[len(refs) - n:]
        for a in range(n):
            outs[a][0] = refs[a][...].astype(outs[a].dtype)

    xi, yi, _ = _coords()
    return pl.pallas_call(
        body,
        name=name,
        grid_spec=pltpu.PrefetchScalarGridSpec(
            num_scalar_prefetch=1,
            grid=(2, nt),
            in_specs=[pl.BlockSpec((1, w.shape[1] // nt, w.shape[2]), lambda hf, i, s: (hf, i, 0)) for w in ws]
            + [ANY_SPEC] * len(deps),
            out_specs=[pl.BlockSpec((1, 1, w.shape[1] // nt, w.shape[2]), lambda hf, i, s: (s[0], hf, i, 0)) for w in ws],
        ),
        out_shape=[jax.ShapeDtypeStruct((N_CHIPS,) + w.shape, _MXU) for w in ws],
        compiler_params=_params(("parallel", "parallel")),
    )((2 * xi + yi).reshape(1).astype(jnp.int32), *ws, *deps)


def _chip_gather_plan(stage, n_bufs):
    def plan(refs):
        x, y, c = _coords()
        me = 2 * x + y
        near = [(1 - x, y), (x, 1 - y)]
        slots = [2 * (1 - x) + y, 2 * x + (1 - y), 2 * (1 - x) + (1 - y)]
        sibling = (x, y, 1 - c)
        pass_to = (jnp.where(c == 0, x, 1 - x), jnp.where(c == 0, 1 - y, y), c)
        pass_slot = jnp.where(c == 0, slots[0], slots[1])
        out = []

        def move(src_slot, to, land_slot, land_core, pieces):
            for a, buf in enumerate(refs):
                for rows in _chunks(buf.shape[2], pieces[a]):
                    out.append((len(out), (buf.at[src_slot, c, rows], buf.at[src_slot, c, rows], to, None),
                                (buf.at[land_slot, land_core, rows], None)))

        if stage == "near":
            for k, chip in enumerate(near):
                move(me, (*chip, c), slots[k], c, NEAR_PIECES[:n_bufs])
        elif stage == "pass":
            move(pass_slot, pass_to, slots[2], c, PASS_PIECES[:n_bufs])
            for k in range(2):
                move(slots[k], sibling, slots[k], 1 - c, [1] * n_bufs)
        else:
            move(slots[2], sibling, slots[2], 1 - c, [1] * n_bufs)
        return out
    return plan


NEAR_PIECES = (2, 1)
PASS_PIECES = (2, 1)


def _chip_gather_copies(stage, n_bufs):
    if stage == "near":
        return 2 * sum(NEAR_PIECES[:n_bufs]), None
    if stage == "pass":
        n_pass = sum(PASS_PIECES[:n_bufs])
        return n_pass + 2 * n_bufs, set(range(n_pass))
    return n_bufs, None


def _swap_plan(n_slabs):
    def plan(refs):
        x, y, c = _coords()
        out, k = [], 0
        for i, n in enumerate(n_slabs):
            g, land = refs[2 * i], refs[2 * i + 1]
            for p in range(n):
                out.append((k, (g.at[p, 1 - c], land.at[p], (x, y, 1 - c), None), (land.at[p], None)))
                k += 1
        return out
    return plan


def _is_one_of(chip, dests):
    hit = chip == dests[0]
    for d in dests[1:]:
        hit = hit | (chip == d)
    return hit


def _slab_of(chip, dests):
    return sum(j * (chip == d).astype(jnp.int32) for j, d in enumerate(dests))


def _scatter_plan(dest_sets):
    def plan(refs):
        x, y, c = _coords()
        me = 2 * x + y
        out = []
        for k, (px, py) in enumerate(_other_chips(x, y)):
            peer = 2 * px + py
            for i, dests in enumerate(dest_sets):
                cs, land = refs[2 * i], refs[2 * i + 1]
                everyone = len(dests) == N_CHIPS
                send = (cs.at[_slab_of(peer, dests)], land.at[k], (px, py, c),
                        None if everyone else _is_one_of(peer, dests))
                recv = (land.at[k], None if everyone else _is_one_of(me, dests))
                out.append((k * len(dest_sets) + i, send, recv))
        return out
    return plan


def _join_plan(rows, n_pieces):
    def plan(refs):
        x, y, c = _coords()
        (buf,) = refs
        return [(i, (buf.at[c, piece], buf.at[c, piece], (x, y, 1 - c), None), (buf.at[1 - c, piece], None))
                for i, piece in enumerate(_chunks(rows, n_pieces))]
    return plan


def _join_plans(parts):
    def plan(refs):
        out, b0, k0 = [], 0, 0
        for part_plan, n_bufs, n_copies in parts:
            out += [(k0 + k, send, recv) for k, send, recv in part_plan(refs[b0:b0 + n_bufs])]
            b0 += n_bufs
            k0 += n_copies
        return out
    return plan


def _allgather_plan():
    def plan(refs):
        x, y, c = _coords()
        (land,) = refs
        me = 4 * x + 2 * y + c
        out = []
        for r in range(1, 8):
            px = 1 - x if r & 4 else x
            py = 1 - y if r & 2 else y
            pc = 1 - c if r & 1 else c
            out.append((r - 1, (land.at[me], land.at[me], (px, py, pc), None), (land.at[4 * px + 2 * py + pc], None)))
        return out
    return plan


def _sum_gathered(land, name):
    def body(land_ref, o_ref):
        acc = land_ref[0]
        for d in range(1, 8):
            acc = acc + land_ref[d]
        o_ref[...] = acc

    return pl.pallas_call(
        body,
        name=name,
        out_shape=jax.ShapeDtypeStruct(land.shape[1:], F32),
        compiler_params=_params(),
    )(land)


def _join_halves(bufs, n_chunks, name, deps=()):
    n = len(bufs)
    pieces = [(a, rows) for a in range(n) for rows in _chunks(bufs[a].shape[1], n_chunks[a])]
    n_p = len(pieces)

    def body(*refs):
        outs = refs[-n - 2:-2]
        send_sems, recv_sems = refs[-2:]
        x, y, c = _coords()

        def copy(i, half):
            a, rows = pieces[i]
            return pltpu.make_async_remote_copy(
                src_ref=outs[a].at[half, rows], dst_ref=outs[a].at[half, rows], send_sem=send_sems.at[i],
                recv_sem=recv_sems.at[i], device_id=(x, y, 1 - c), device_id_type=MESH)

        sends = [copy(i, c) for i in range(n_p)]
        for cp in sends:
            cp.start()
        for i in range(n_p):
            copy(i, 1 - c).wait_recv()
        for cp in sends:
            cp.wait_send()

    anyspec = pl.BlockSpec(memory_space=pl.ANY)
    sems = pltpu.SemaphoreType.DMA((n_p,))
    return pl.pallas_call(
        body,
        name=name,
        in_specs=[anyspec] * (n + len(deps)),
        out_specs=[anyspec] * n,
        out_shape=[jax.ShapeDtypeStruct(b.shape, b.dtype) for b in bufs],
        input_output_aliases={a: a for a in range(n)},
        scratch_shapes=[sems, sems],
    )(*bufs, *deps)


def _row_tile(rows, cap):
    t = cap
    while rows % t:
        t //= 2
    return t


def _add_my_half(g, r, name):
    n_slabs, _, R, C = g.shape
    tr = R if n_slabs > 1 else _row_tile(R, SUM_ROWS)

    def body(c_ref, g_ref, r_ref, o_ref):
        o_ref[...] = (g_ref[0] + r_ref[...]).astype(o_ref.dtype)

    return pl.pallas_call(
        body,
        name=name,
        grid_spec=pltpu.PrefetchScalarGridSpec(
            num_scalar_prefetch=1,
            grid=(n_slabs, R // tr),
            in_specs=[pl.BlockSpec((1, 1, tr, C), lambda p, i, c_ref: (p, c_ref[0], i, 0)),
                      pl.BlockSpec((1, tr, C), lambda p, i, c_ref: (p, i, 0))],
            out_specs=pl.BlockSpec((1, tr, C), lambda p, i, c_ref: (p, i, 0)),
        ),
        out_shape=jax.ShapeDtypeStruct(r.shape, jnp.bfloat16),
        compiler_params=_params(("parallel", "parallel")),
    )(lax.axis_index("c").reshape(1).astype(jnp.int32), g, r)


def _sum_slabs(own, got, name, deps=()):
    _, R, C = own.shape
    tr = _row_tile(R, SUM_ROWS)

    def body(s_ref, own_ref, got_ref, *rest):
        rest[-1][0] = ((own_ref[0].astype(F32) + got_ref[0].astype(F32)) + got_ref[1].astype(F32)) + got_ref[2].astype(F32)

    xi, yi, ci = _coords()
    return pl.pallas_call(
        body,
        name=name,
        grid_spec=pltpu.PrefetchScalarGridSpec(
            num_scalar_prefetch=1,
            grid=(R // tr,),
            in_specs=[pl.BlockSpec((1, tr, C), lambda i, s: (s[0], i, 0)),
                      pl.BlockSpec((3, tr, C), lambda i, s: (0, i, 0))] + [ANY_SPEC] * len(deps),
            out_specs=pl.BlockSpec((1, tr, C), lambda i, s: (s[1], i, 0)),
        ),
        out_shape=jax.ShapeDtypeStruct((2, R, C), F32),
        compiler_params=_params(("parallel",)),
    )(jnp.stack([2 * xi + yi, ci]).astype(jnp.int32), own, got, *deps)


def _sum_parts(owns, got, dest_sets, name):
    n = len(owns)
    _, R, C = owns[0].shape
    tr = _row_tile(R, SUM_ROWS)

    def body(s_ref, *refs):
        got_ref, o_ref = refs[n], refs[-1]
        total = jnp.zeros((tr, C), F32)
        for i in range(n):
            total = total + jnp.where(s_ref[2 + 2 * i] == 1, refs[i][0].astype(F32), 0.0)
        o_ref[0] = ((total + got_ref[0].astype(F32)) + got_ref[1].astype(F32)) + got_ref[2].astype(F32)

    xi, yi, ci = _coords()
    me = 2 * xi + yi
    scalars = [ci, ci]
    for dests in dest_sets:
        scalars += [_is_one_of(me, dests).astype(jnp.int32), _slab_of(me, dests)]
    own_spec = lambda i: pl.BlockSpec((1, tr, C), lambda r, s: (s[3 + 2 * i], r, 0))
    return pl.pallas_call(
        body,
        name=name,
        grid_spec=pltpu.PrefetchScalarGridSpec(
            num_scalar_prefetch=1,
            grid=(R // tr,),
            in_specs=[own_spec(i) for i in range(n)] + [pl.BlockSpec((3, tr, C), lambda r, s: (0, r, 0))],
            out_specs=pl.BlockSpec((1, tr, C), lambda r, s: (s[0], r, 0)),
        ),
        out_shape=jax.ShapeDtypeStruct((2, R, C), F32),
        compiler_params=_params(("parallel",)),
    )(jnp.stack(scalars).astype(jnp.int32), *owns, got)


def _adamw_math(w, g, m, v):
    m = ADAM_B1 * m + (1.0 - ADAM_B1) * g
    v = ADAM_B2 * v + (1.0 - ADAM_B2) * (g * g)
    m_hat = m / (1.0 - ADAM_B1 ** ADAM_STEP)
    v_hat = v / (1.0 - ADAM_B2 ** ADAM_STEP)
    delta = -ADAM_LR * (m_hat / (jnp.sqrt(v_hat) + ADAM_EPS) + ADAM_WD * w)
    return delta, m, v


def _adamw_halves(ws, g, ms, vs, half, prev, name, deps=()):
    n = len(ws)
    _, _, R, C = g.shape
    tr = _row_tile(R, ADAMW_ROWS)
    steps = R // tr
    carried = [] if prev is None else [a for four in prev for a in four]
    both = half is None
    which = (lambda i, s: i // steps) if both else (lambda i, s: s[0])
    half = 0 if both else half

    def body(s_ref, *refs):
        w_refs, g_refs, m_refs, v_refs = (refs[k * n:(k + 1) * n] for k in range(4))
        outs = refs[len(refs) - 4 * n:]
        for a in range(n):
            grad = g_refs[a][0, 0]
            d, mn, vn = _adamw_math(w_refs[a][...], grad, m_refs[a][...], v_refs[a][...])
            for o, val in zip(outs[4 * a:4 * a + 4], (grad, d, mn, vn)):
                o[...] = val

    rows = pl.BlockSpec((tr, C), lambda i, s: (which(i, s) * steps + i % steps, 0))
    grad_spec = lambda a: pl.BlockSpec((1, 1, tr, C), lambda i, s: (which(i, s), a, i % steps, 0))
    n_in = 4 * n
    outs = pl.pallas_call(
        body,
        name=name,
        grid_spec=pltpu.PrefetchScalarGridSpec(
            num_scalar_prefetch=1,
            grid=(2 * steps if both else steps,),
            in_specs=[rows] * n + [grad_spec(a) for a in range(n)] + [rows] * (2 * n)
            + [ANY_SPEC] * (len(carried) + len(deps)),
            out_specs=[rows] * (4 * n),
        ),
        out_shape=[jax.ShapeDtypeStruct((2 * R, C), F32)] * (4 * n),
        input_output_aliases={1 + n_in + k: k for k in range(len(carried))},
        compiler_params=_params(("parallel",)),
    )(jnp.reshape(half, (1,)).astype(jnp.int32), *ws, *([g] * n), *ms, *vs, *carried, *deps)
    return [outs[4 * a:4 * a + 4] for a in range(n)]


def _adamw_small(ws, gs, ms, vs, name):
    n = len(ws)

    def body(*refs):
        for a in range(n):
            d, mn, vn = _adamw_math(refs[a][...], refs[n + a][...], refs[2 * n + a][...], refs[3 * n + a][...])
            refs[4 * n + a][...] = d
            refs[5 * n + a][...] = mn
            refs[6 * n + a][...] = vn

    shapes = [jax.ShapeDtypeStruct(w.shape, F32) for w in ws]
    outs = pl.pallas_call(
        body,
        name=name,
        out_shape=shapes * 3,
        compiler_params=_params(),
    )(*ws, *gs, *ms, *vs)
    return outs[:n], outs[n:2 * n], outs[2 * n:]


def _to_blockdiag(w):
    per = CW // LRU_BW
    w4 = w.reshape(N_CT, per, LRU_BW, LRU_BW)
    eye = jnp.eye(per, dtype=w.dtype)
    return (w4[:, :, :, None, :] * eye[None, :, None, :, None]).reshape(N_CT, CW, CW)


def _from_blockdiag(g):
    per = CW // LRU_BW
    g5 = g.reshape(N_CT, per, LRU_BW, per, LRU_BW)
    return jnp.stack([g5[:, b, :, b, :] for b in range(per)], axis=1).reshape(LRU_BLOCKS, LRU_BW, LRU_BW)


def _local_grads(x2d, tgt2d, B, S, g_in, in_proj, conv_b, gate_x_w, gate_x_b, gate_a_w, gate_a_b, lam,
                 proj_weights, g_fin, reduce):
    wx_bd = _c(_to_blockdiag(gate_x_w))
    wa_bd = _c(_to_blockdiag(gate_a_w))
    tables = _retention_tables(S)

    proj, ht, w_all, conv_w, gain = in_proj(x2d, g_in, (tables[0], tables[1], wx_bd, wa_bd))
    gain3 = gain.reshape(HEADS, 1, DK)
    hlru, ya = _lru_fwd(proj, conv_w, conv_b, wx_bd, wa_bd, gate_x_b, gate_a_b, lam, B, S)
    o_pre, yb, states = _ret_fwd(proj, tables, gain3, B, S)
    wpa, wpb, wout = proj_weights(yb)
    loss, dx2, dya, dyb, dm, dgf, gw_proj = _mid(ya, yb, proj, x2d, tgt2d, wpa, wpb, wout, g_fin)
    g3 = _inproj_bwd_dw(ht, [dm], "inproj_bwd_dw_m")
    deps = reduce.m_ready(gw_proj, g3)
    dr, dgain = _ret_bwd(dyb, o_pre, proj, states, tables, gain3, B, S, deps)
    deps = reduce.ret_done(dr)
    g12 = _inproj_bwd_dw(ht, [dr], "inproj_bwd_dw_r", deps)
    deps = reduce.r_ready(g12)
    dxa, dga, dcw, dcb, dwx_bd, dwa_bd, dbx, dba, dlam = _lru_bwd(
        dya, proj, hlru, conv_w, conv_b, wx_bd, wa_bd, gate_x_b, gate_a_b, lam, B, S, deps)
    small = dict(conv_w=dcw, conv_b=dcb, gate_x_w=_from_blockdiag(dwx_bd), gate_x_b=dbx,
                 gate_a_w=_from_blockdiag(dwa_bd), gate_a_b=dba, lru_lambda=dlam, gn_gain=dgain.reshape(HEADS, DK),
                 norm_final=dgf)
    loss_rows = jnp.broadcast_to(loss, (SUBLANES, LANES))
    deps = reduce.lru_done(dxa, jnp.concatenate([_pack_small(small), loss_rows], axis=0))
    g0 = _inproj_bwd_dw(ht, [dxa, dga], "inproj_bwd_dw_a", deps)
    deps = reduce.a_ready(g0)
    n_tiles = x2d.shape[0] // min(DX_TILE, x2d.shape[0])
    grad_x, dgin = _inproj_bwd_dx([dxa, dga, dr, dm], w_all, x2d, dx2, g_in, 0, n_tiles, None, "inproj_bwd_dx", deps)
    return grad_x, dgin


ALL_CHIPS = (0, 1, 2, 3)


class _GradReduce:
    def __init__(self, proj_done):
        self.pending = {}
        self.proj_done = proj_done
        self.land_in = None

    def _start(self, key, parts, name):
        bufs, plans, shared = [], [], None
        for part_bufs, plan, n_copies, part_shared in parts:
            if part_shared is not None:
                shared = len(bufs) + part_shared
            plans.append((plan, len(part_bufs), n_copies))
            bufs += part_bufs
        plan = _join_plans(plans)
        send_sems, recv_sems, bufs, token = _copies_start(bufs, plan, sum(p[2] for p in plans), name + "_start")
        if shared is not None:
            self.land_in = bufs[shared]
        self.pending[key] = (send_sems, recv_sems, bufs, plan, name + "_wait", shared)
        return (token,)

    def _finish(self, key, after):
        send_sems, recv_sems, bufs, plan, name, shared = self.pending.pop(key)
        if shared is not None:
            bufs[shared] = self.land_in
        bufs = _copies_wait(send_sems, recv_sems, bufs, after, plan, name)
        if shared is not None:
            self.land_in = bufs[shared]
        return bufs

    @staticmethod
    def _swap(pieces):
        bufs = []
        for g in pieces:
            bufs += [g, lax.empty((g.shape[0],) + g.shape[2:], F32)]
        n_slabs = [g.shape[0] for g in pieces]
        return bufs, _swap_plan(n_slabs), sum(n_slabs), None

    def _scatter(self, sums, dest_sets):
        bufs = []
        for cs in sums:
            bufs += [cs, lax.empty((3,) + cs.shape[1:], cs.dtype)]
        if self.land_in is not None:
            bufs[-1] = self.land_in
        return bufs, _scatter_plan(dest_sets), 3 * len(sums), len(bufs) - 1

    @staticmethod
    def _gather8(block):
        x, y, c = _coords()
        land = lax.dynamic_update_slice(lax.empty((8,) + block.shape, F32), block[None], (4 * x + 2 * y + c, 0, 0))
        return [land], _allgather_plan(), 7, None

    def m_ready(self, gw_proj, g3):
        rows = gw_proj.shape[2] * gw_proj.shape[3]
        return self._start("m", [self._swap([gw_proj.reshape(N_CHIPS, 2, rows, D_MODEL), g3])], "swap_m")

    def ret_done(self, after):
        proj, land_p, g3, land_3 = self._finish("m", after)
        sums_m = [_add_my_half(proj, land_p, "chip_sum_proj"), _add_my_half(g3, land_3, "chip_sum_m")]
        return self._start("sm", [self._scatter(sums_m, [ALL_CHIPS, (3,)])], "scatter_m")

    def r_ready(self, g12):
        return self._start("r", [self._swap([g12])], "swap_r")

    def lru_done(self, after, packed):
        g12, land_12 = self._finish("r", after)
        sums_r = [_add_my_half(g12, land_12, "chip_sum_r")]
        return (self._start("sr", [self._scatter(sums_r, [(1, 2)])], "scatter_r")
                + self._start("small", [self._gather8(packed)], "gather_small"))

    def a_ready(self, g0):
        (token,) = self._start("a", [self._swap([g0])], "swap_a")
        csp, gotp, self.cs3, _ = self._finish("sm", token)
        half_proj = _sum_slabs(csp, gotp, "sum_w_proj")
        g0, land_0 = self._finish("a", half_proj)
        deps = self._start("sa", [self._scatter([_add_my_half(g0, land_0, "chip_sum_a")], [(0,)])], "scatter_a")
        self.proj_done(_join_halves([half_proj], [4], "join_halves_proj", deps)[0])
        return deps

    def finish(self, dgin, w_in_done):
        (token,) = self._start("n", [self._gather8(dgin)], "gather_norm_in")
        (small,) = self._finish("small", token)
        cs12, _ = self._finish("sr", token)
        cs0, _ = self._finish("sa", token)
        half_in = _sum_parts([self.cs3, cs12, cs0], self.land_in, [(3,), (1, 2), (0,)], "sum_w_in")
        deps = self._start("j", [([half_in], _join_plan(half_in.shape[1], JOIN_PIECES), JOIN_PIECES, None)], "join_w_in")
        first = w_in_done(self.pending["j"][2][0], True, None, deps)
        (g_in,) = self._finish("j", first[1])
        done = w_in_done(g_in, False, first, ())
        (norm_in,) = self._finish("n", done[1])
        return _sum_gathered(small, "sum_small_grads"), _sum_gathered(norm_in, "sum_norm_in_grad")


_SMALL = ("gate_x_w", "gate_a_w", "conv_w", "conv_b", "gate_x_b", "gate_a_b", "lru_lambda", "gn_gain", "norm_final")
_SMALL_SHAPES = dict(gate_x_w=(LRU_BLOCKS, LRU_BW, LRU_BW), gate_a_w=(LRU_BLOCKS, LRU_BW, LRU_BW),
                     norm_in=(1, D_MODEL), conv_w=(CONV, D_MODEL), conv_b=(1, D_MODEL), gate_x_b=(1, D_MODEL),
                     gate_a_b=(1, D_MODEL), lru_lambda=(1, D_MODEL), gn_gain=(HEADS, DK), norm_final=(1, D_MODEL))


def _pack_small(small):
    return jnp.concatenate([small[k].reshape(-1, 128) for k in _SMALL], axis=0)


def _unpack_small(packed):
    out, r = {}, 0
    for k in _SMALL:
        shape = _SMALL_SHAPES[k]
        rows = 1
        for s in shape:
            rows *= s
        rows //= 128
        out[k] = packed[r:r + rows].reshape(shape)
        r += rows
    return out


def kernel(x, norm_in, w_in, conv_w, conv_b, gate_x_w, gate_x_b, gate_a_w, gate_a_b, lru_lambda, gn_gain, w_proj_a, w_proj_b, w_out, norm_final, loss_target, m_norm_in, m_w_in, m_conv_w, m_conv_b, m_gate_x_w, m_gate_x_b, m_gate_a_w, m_gate_a_b, m_lru_lambda, m_gn_gain, m_w_proj_a, m_w_proj_b, m_w_out, m_norm_final, v_norm_in, v_w_in, v_conv_w, v_conv_b, v_gate_x_w, v_gate_x_b, v_gate_a_w, v_gate_a_b, v_lru_lambda, v_gn_gain, v_w_proj_a, v_w_proj_b, v_w_out, v_norm_final):
    B, S, _ = x.shape
    T = B * S
    xi, yi, ci = _coords()
    chip = 2 * xi + yi

    cshard = D_MODEL // N_CHIPS
    mine = _cast_into_slot([w_in[0].reshape(2, D_MODEL // 2, 2 * D_MODEL)], "cast_w_in")
    plan = _gather_plan(3)
    pending_proj = []
    gshard = DK // N_CHIPS
    tiny = jnp.concatenate([conv_w[0], jnp.zeros((4, cshard), F32), jnp.pad(gn_gain[0], ((0, 4), (0, cshard - gshard)))],
                           axis=0).reshape(1, 2, SUBLANES, cshard)
    tiny_buf = lax.dynamic_update_slice(lax.empty((N_CHIPS, 2, SUBLANES, cshard), F32), tiny, (chip, 0, 0, 0))
    near_plan, pass_plan, far_plan = (_chip_gather_plan(stage, 2) for stage in ("near", "pass", "far"))
    (n_near, _), (n_pass, passed_on), (n_far, _) = (_chip_gather_copies(stage, 2) for stage in ("near", "pass", "far"))
    halves = set(range(n_pass)) - passed_on
    near_s, near_r, bufs, near_token = _copies_start([mine[0], tiny_buf], near_plan, n_near, "gather_near_start")

    def in_proj(x2d, g_in, meanwhile):
        mine_proj = _cast_into_slot([w[0].reshape(2, cshard // 2, D_MODEL) for w in (w_proj_a, w_proj_b, w_out)],
                                    "cast_w_proj", (near_token,))
        as_w = lambda b: b[0].reshape(N_CHIPS, D_MODEL, 2 * D_MODEL)
        slot_x, slot_y, slot_d = 2 * (1 - xi) + yi, 2 * xi + (1 - yi), 2 * (1 - xi) + (1 - yi)
        ids = lambda *chips: jnp.stack(chips).astype(jnp.int32)
        proj, hb, ht = _inproj_first(x2d, g_in, as_w(bufs), ids(chip), "inproj_own", (near_token, *meanwhile))
        got = _copies_wait(near_s, near_r, bufs, proj, near_plan, "gather_near_wait")
        pass_s, pass_r, got, pass_token = _copies_start(got, pass_plan, n_pass, "gather_pass_start")
        got = _copies_wait(pass_s, pass_r, got, pass_token, pass_plan, "gather_pass_wait_halves", only=halves)
        proj = _inproj_more(hb, as_w(got), ids(slot_x, slot_y), proj, "inproj_near")
        got = _copies_wait(pass_s, pass_r, got, proj, pass_plan, "gather_pass_wait_far", only=passed_on)
        pending_proj.append(_copies_start(mine_proj, plan, 9, "gather_proj_start", (got[0],)))
        far_s, far_r, got, far_token = _copies_start(got, far_plan, n_far, "gather_far_start")
        got = _copies_wait(far_s, far_r, got, far_token, far_plan, "gather_far_wait")
        proj = _inproj_more(hb, as_w(got), ids(slot_d), proj, "inproj_far")
        tiny_all = got[1].reshape(N_CHIPS, 2 * SUBLANES, cshard)
        conv_w_full = jnp.transpose(tiny_all[:, 0:CONV, :], (1, 0, 2)).reshape(CONV, D_MODEL)
        gain_full = jnp.transpose(tiny_all[:, 8:8 + HEADS, :gshard], (1, 0, 2)).reshape(HEADS, DK)
        return proj, ht, as_w(got), conv_w_full, gain_full

    def proj_weights(after):
        s_sems, r_sems, pbufs, _ = pending_proj[0]
        got = _copies_wait(s_sems, r_sems, pbufs, after, plan, "gather_proj_wait")
        return [b.reshape(D_MODEL, D_MODEL) for b in got]

    weights = dict(norm_in=norm_in, w_in=w_in, conv_w=conv_w, conv_b=conv_b, gate_x_w=gate_x_w, gate_x_b=gate_x_b,
                   gate_a_w=gate_a_w, gate_a_b=gate_a_b, lru_lambda=lru_lambda, gn_gain=gn_gain, w_proj_a=w_proj_a,
                   w_proj_b=w_proj_b, w_out=w_out, norm_final=norm_final)
    ms = dict(norm_in=m_norm_in, w_in=m_w_in, conv_w=m_conv_w, conv_b=m_conv_b, gate_x_w=m_gate_x_w,
              gate_x_b=m_gate_x_b, gate_a_w=m_gate_a_w, gate_a_b=m_gate_a_b, lru_lambda=m_lru_lambda, gn_gain=m_gn_gain,
              w_proj_a=m_w_proj_a, w_proj_b=m_w_proj_b, w_out=m_w_out, norm_final=m_norm_final)
    vs = dict(norm_in=v_norm_in, w_in=v_w_in, conv_w=v_conv_w, conv_b=v_conv_b, gate_x_w=v_gate_x_w,
              gate_x_b=v_gate_x_b, gate_a_w=v_gate_a_w, gate_a_b=v_gate_a_b, lru_lambda=v_lru_lambda, gn_gain=v_gn_gain,
              w_proj_a=v_w_proj_a, w_proj_b=v_w_proj_b, w_out=v_w_out, norm_final=v_norm_final)
    names = list(weights)
    grads, delta, new_m, new_v = {}, {}, {}, {}

    def update_big(keys, g, half, prev, name, deps=()):
        two = lambda a: a.reshape(a.shape[1], a.shape[2])
        res = _adamw_halves([two(weights[k]) for k in keys], g, [two(ms[k]) for k in keys], [two(vs[k]) for k in keys],
                            half, prev, name, deps)
        for k, (gk, d, mn, vn) in zip(keys, res):
            shp = weights[k].shape
            grads[k], delta[k], new_m[k], new_v[k] = gk.reshape(shp), d.reshape(shp), mn.reshape(shp), vn.reshape(shp)
        return res

    def proj_done(g_proj):
        g4 = g_proj.reshape(2, 3, D_MODEL // (2 * N_CHIPS), D_MODEL)
        return update_big(("w_proj_a", "w_proj_b", "w_out"), g4, None, None, "adamw_proj")[-1][1]

    def w_in_done(g_in, own, prev, deps):
        g4 = g_in.reshape(2, 1, D_MODEL // 2, 2 * D_MODEL)
        return update_big(("w_in",), g4, ci if own else 1 - ci, None if prev is None else [prev],
                          "adamw_w_in_own" if own else "adamw_w_in_other", deps)[0]

    reduce = _GradReduce(proj_done)
    grad_x, dgin = _local_grads(
        x.reshape(T, D_MODEL), loss_target.reshape(T, D_MODEL), B, S, norm_in, in_proj, conv_b,
        gate_x_w[0], gate_x_b, gate_a_w[0], gate_a_b, lru_lambda, proj_weights,
        norm_final.reshape(1, D_MODEL), reduce)

    small_sum, g_norm_in = reduce.finish(dgin.reshape(SUBLANES, LANES), w_in_done)
    loss = small_sum[small_sum.shape[0] - SUBLANES, 0]

    gsm = _unpack_small(small_sum)
    gsm["norm_in"] = g_norm_in
    gsm["conv_w"] = lax.dynamic_slice_in_dim(gsm["conv_w"], chip * cshard, cshard, axis=1)
    gsm["gn_gain"] = lax.dynamic_slice_in_dim(gsm["gn_gain"], chip * gshard, gshard, axis=1)
    smalls = [k for k in names if k not in delta]

    def view(a):
        return a.reshape(1, -1) if a.ndim == 1 else (a.reshape(a.shape[1:]) if a.ndim > 2 else a)

    ds, mns, vns = _adamw_small([view(weights[k]) for k in smalls], [gsm[k].reshape(view(weights[k]).shape) for k in smalls],
                                [view(ms[k]) for k in smalls], [view(vs[k]) for k in smalls], "adamw_small")
    for k, d, mn, vn in zip(smalls, ds, mns, vns):
        shp = weights[k].shape
        grads[k], delta[k], new_m[k], new_v[k] = gsm[k].reshape(shp), d.reshape(shp), mn.reshape(shp), vn.reshape(shp)

    return (loss, grad_x.reshape(B, S, D_MODEL), *[grads[k] for k in names], *[delta[k] for k in names],
            *[new_m[k] for k in names], *[new_v[k] for k in names])
```

```python
import jax
import jax.numpy as jnp
from jax import lax
from jax.experimental import pallas as pl
from jax.experimental.pallas import tpu as pltpu

F32 = jnp.float32
_MXU = jnp.bfloat16

D_MODEL = 1024
N_GROUPS = 8
HEADS = 4
DK = 256
CHUNK = 128
CONV = 4
LRU_BLOCKS = 16
LRU_BW = 64
LRU_C = 8.0
ROPE_THETA = 10000.0
EPS = 1e-6
CW = 256
N_CT = D_MODEL // CW
N_CHIPS = 4
MESH = pl.DeviceIdType.MESH

ADAM_LR = 0.001
ADAM_B1 = 0.9
ADAM_B2 = 0.999
ADAM_EPS = 1e-08
ADAM_WD = 0.01
ADAM_STEP = 10

VMEM_LIMIT = 56 * 1024 * 1024

FIRST_PROJ_TILE = 1024
MORE_PROJ_TILE = 2048
SCAN_TILE = 1024
MID_TILE = 256
DX_TILE = 512
DW_COLS = 512
RET_CHUNKS = 2
SUM_ROWS = 256
ADAMW_ROWS = 256
JOIN_PIECES = 8


def _c(v):
    return v.astype(_MXU)


def _dot(a, b):
    return lax.dot_general(a, b, (((1,), (0,)), ((), ())), preferred_element_type=F32)


def _dot_nt(a, b):
    return lax.dot_general(a, b, (((1,), (1,)), ((), ())), preferred_element_type=F32)


def _dot_tn(a, b):
    return lax.dot_general(a, b, (((0,), (0,)), ((), ())), preferred_element_type=F32)


def _sigmoid(z):
    return 0.5 * jnp.tanh(0.5 * z) + 0.5


ANY_SPEC = pl.BlockSpec(memory_space=pl.ANY)


def _after(body, n_in, deps):
    n_deps = len(deps)

    def wrapped(*refs):
        return body(*refs[:n_in], *refs[n_in + n_deps:])

    return wrapped


def _params(sem=None):
    if sem is None:
        return pltpu.CompilerParams(vmem_limit_bytes=VMEM_LIMIT)
    return pltpu.CompilerParams(vmem_limit_bytes=VMEM_LIMIT, dimension_semantics=sem)


def _inproj_first(x2d, g_in, w_all, chips, name, deps=()):
    T = x2d.shape[0]
    tm = min(FIRST_PROJ_TILE, T)
    n_i = T // tm

    def body(s_ref, *refs):
        x_ref, g_ref, w_ref = refs[:3]
        proj_ref, hb_ref, ht_ref, h_all = refs[-4:]
        i = pl.program_id(1)
        rows = pl.ds(pl.multiple_of(i * tm, tm), tm)

        @pl.when(pl.program_id(0) == 0)
        def _():
            x = x_ref[...]
            r = lax.rsqrt(jnp.mean(x * x, axis=-1, keepdims=True) + EPS)
            h = x * r * g_ref[...]
            hb = h.astype(h_all.dtype)
            h_all[rows, :] = hb
            hb_ref[...] = hb
            ht_ref[...] = h.T.astype(ht_ref.dtype)

        proj_ref[...] = _dot(h_all[rows, :], w_ref[0])

    first = lambda j, i: jnp.where(j == 0, i, n_i - 1)
    return pl.pallas_call(
        body,
        name=name,
        grid_spec=pltpu.PrefetchScalarGridSpec(
            num_scalar_prefetch=1,
            grid=(2 * chips.shape[0], n_i),
            in_specs=[
                pl.BlockSpec((tm, D_MODEL), lambda j, i, s: (first(j, i), 0)),
                pl.BlockSpec((1, D_MODEL), lambda j, i, s: (0, 0)),
                pl.BlockSpec((1, D_MODEL, D_MODEL), lambda j, i, s: (s[j // 2], 0, j % 2)),
            ] + [ANY_SPEC] * len(deps),
            out_specs=[
                pl.BlockSpec((tm, D_MODEL), lambda j, i, s: (i, 2 * s[j // 2] + j % 2)),
                pl.BlockSpec((tm, D_MODEL), lambda j, i, s: (first(j, i), 0)),
                pl.BlockSpec((D_MODEL, tm), lambda j, i, s: (0, first(j, i))),
            ],
            scratch_shapes=[pltpu.VMEM((T, D_MODEL), _MXU)],
        ),
        out_shape=[
            jax.ShapeDtypeStruct((T, N_GROUPS * D_MODEL), F32),
            jax.ShapeDtypeStruct((T, D_MODEL), _MXU),
            jax.ShapeDtypeStruct((D_MODEL, T), _MXU),
        ],
        compiler_params=_params(("arbitrary", "arbitrary")),
    )(chips, x2d, g_in, w_all, *deps)


def _inproj_more(hb, w_all, chips, proj, name):
    T = hb.shape[0]
    tm = min(MORE_PROJ_TILE, T)

    def body(s_ref, hb_hbm, w_ref, prev_ref, proj_ref, h_all, sem):
        @pl.when((pl.program_id(0) == 0) & (pl.program_id(1) == 0))
        def _():
            cp = pltpu.make_async_copy(hb_hbm, h_all, sem)
            cp.start()
            cp.wait()

        rows = pl.ds(pl.multiple_of(pl.program_id(1) * tm, tm), tm)
        proj_ref[...] = _dot(h_all[rows, :], w_ref[0])

    return pl.pallas_call(
        body,
        name=name,
        grid_spec=pltpu.PrefetchScalarGridSpec(
            num_scalar_prefetch=1,
            grid=(2 * chips.shape[0], T // tm),
            in_specs=[
                ANY_SPEC,
                pl.BlockSpec((1, D_MODEL, D_MODEL), lambda j, i, s: (s[j // 2], 0, j % 2)),
                ANY_SPEC,
            ],
            out_specs=pl.BlockSpec((tm, D_MODEL), lambda j, i, s: (i, 2 * s[j // 2] + j % 2)),
            scratch_shapes=[pltpu.VMEM((T, D_MODEL), hb.dtype), pltpu.SemaphoreType.DMA],
        ),
        out_shape=jax.ShapeDtypeStruct(proj.shape, F32),
        input_output_aliases={3: 0},
        compiler_params=_params(("arbitrary", "arbitrary")),
    )(chips, hb, w_all, proj)


def _scan_fwd(a, u):
    n = a.shape[0]
    row = lax.broadcasted_iota(jnp.int32, a.shape, 0)
    s = 1
    while s < n:
        m = row >= s
        u = u + a * jnp.where(m, pltpu.roll(u, s, 0), 0.0)
        a = a * jnp.where(m, pltpu.roll(a, s, 0), 1.0)
        s *= 2
    return a, u


def _scan_bwd(b, g):
    n = b.shape[0]
    row = lax.broadcasted_iota(jnp.int32, b.shape, 0)
    s = 1
    while s < n:
        m = row < n - s
        g = g + b * jnp.where(m, pltpu.roll(g, n - s, 0), 0.0)
        b = b * jnp.where(m, pltpu.roll(b, n - s, 0), 1.0)
        s *= 2
    return b, g


LANES = 128
SUBLANES = 8


def _scan_scratch(tc):
    by_lanes = pltpu.VMEM((CW // LANES, tc, LANES), F32)
    return [by_lanes, by_lanes, pltpu.VMEM((tc // SUBLANES, CW), F32), pltpu.VMEM((tc, CW), F32)]


def _scan_tile(a, u, edge, la_ref, lh_ref, c_ref, dst_ref, reverse):
    n, w = a.shape
    groups = n // SUBLANES
    a3 = a.reshape(groups, SUBLANES, w)
    u3 = u.reshape(groups, SUBLANES, w)
    row = lax.broadcasted_iota(jnp.int32, a3.shape, 1)
    for s in (1, 2, 4):
        m = (row < SUBLANES - s) if reverse else (row >= s)
        shift = SUBLANES - s if reverse else s
        u3 = u3 + a3 * jnp.where(m, pltpu.roll(u3, shift, 1), 0.0)
        a3 = a3 * jnp.where(m, pltpu.roll(a3, shift, 1), 1.0)
    al = a3.reshape(n, w)
    hl = u3.reshape(n, w)
    blocks = w // LANES
    for q in range(blocks):
        la_ref[q] = al[:, q * LANES:(q + 1) * LANES]
        lh_ref[q] = hl[:, q * LANES:(q + 1) * LANES]
    ends = pl.ds(0 if reverse else SUBLANES - 1, groups, stride=SUBLANES)
    end_a = jnp.concatenate([la_ref.at[q][ends, :] for q in range(blocks)], axis=-1)
    end_h = jnp.concatenate([lh_ref.at[q][ends, :] for q in range(blocks)], axis=-1)
    prod, part = (_scan_bwd if reverse else _scan_fwd)(end_a, end_h)
    total = part + prod * edge
    g_row = lax.broadcasted_iota(jnp.int32, total.shape, 0)
    if reverse:
        c_ref[...] = jnp.where(g_row == groups - 1, edge, pltpu.roll(total, groups - 1, 0))
    else:
        c_ref[...] = jnp.where(g_row == 0, edge, pltpu.roll(total, 1, 0))
    for g in range(groups):
        rows = slice(g * SUBLANES, (g + 1) * SUBLANES)
        for q in range(blocks):
            cols = slice(q * LANES, (q + 1) * LANES)
            dst_ref[rows, cols] = lh_ref[q, rows, :] + la_ref[q, rows, :] * c_ref[g:g + 1, cols]


def _softplus_neg(lam):
    z = -lam
    return jnp.maximum(z, 0.0) + jnp.log1p(jnp.exp(-jnp.abs(z)))


def _lru_gates(xc, wx_ref, wa_ref, bx_ref, ba_ref, lam_ref):
    xcb = _c(xc)
    i_t = _sigmoid(_dot(xcb, wx_ref[0]) + bx_ref[...])
    r_t = _sigmoid(_dot(xcb, wa_ref[0]) + ba_ref[...])
    sp = _softplus_neg(lam_ref[...])
    log_a = (-LRU_C) * r_t * sp
    a = jnp.exp(log_a)
    mult = jnp.sqrt(1.0 - a * a)
    return xcb, i_t, r_t, sp, a, mult


def _conv_from_ext(ext_ref, xa, cw_ref, cb_ref, tc):
    return (cb_ref[...] + cw_ref[3:4, :] * xa + cw_ref[2:3, :] * ext_ref[7:7 + tc, :]
            + cw_ref[1:2, :] * ext_ref[6:6 + tc, :] + cw_ref[0:1, :] * ext_ref[5:5 + tc, :])


def _lru_fwd(proj, conv_w, conv_b, wx_bd, wa_bd, bx, ba, lam, B, S):
    T = B * S
    tc = min(SCAN_TILE, S)
    nt = S // tc
    h8 = tc // 8

    def body(xa_ref, halo_ref, ga_ref, cw_ref, cb_ref, wx_ref, wa_ref, bx_ref, ba_ref, lam_ref,
             h_ref, ya_ref, ext_ref, carry_ref, la_ref, lh_ref, c_ref):
        t = pl.program_id(2)

        @pl.when(t == 0)
        def _():
            carry_ref[...] = jnp.zeros_like(carry_ref)

        xa = xa_ref[...]
        ext_ref[0:8, :] = jnp.where(t == 0, 0.0, halo_ref[...])
        ext_ref[8:8 + tc, :] = xa
        xc = _conv_from_ext(ext_ref, xa, cw_ref, cb_ref, tc)
        _, i_t, _, _, a, mult = _lru_gates(xc, wx_ref, wa_ref, bx_ref, ba_ref, lam_ref)
        u = mult * (i_t * xc)
        _scan_tile(a, u, carry_ref[7:8, :], la_ref, lh_ref, c_ref, h_ref, False)
        h = h_ref[...]
        carry_ref[...] = h[tc - 8:tc, :]
        ga = ga_ref[...]
        ya_ref[...] = (ga * _sigmoid(ga) * h).astype(ya_ref.dtype)

    row = lambda b, t: b * nt + t
    vec = pl.BlockSpec((1, CW), lambda b, c, t: (0, c))
    mat = pl.BlockSpec((1, CW, CW), lambda b, c, t: (c, 0, 0))
    return pl.pallas_call(
        body,
        name="lru_fwd",
        grid=(B, N_CT, nt),
        in_specs=[
            pl.BlockSpec((tc, CW), lambda b, c, t: (row(b, t), c)),
            pl.BlockSpec((8, CW), lambda b, c, t: (jnp.maximum(row(b, t) * h8 - 1, 0), c)),
            pl.BlockSpec((tc, CW), lambda b, c, t: (row(b, t), N_CT + c)),
            pl.BlockSpec((CONV, CW), lambda b, c, t: (0, c)),
            vec, mat, mat, vec, vec, vec,
        ],
        out_specs=[
            pl.BlockSpec((tc, CW), lambda b, c, t: (row(b, t), c)),
            pl.BlockSpec((tc, CW), lambda b, c, t: (row(b, t), c)),
        ],
        out_shape=[
            jax.ShapeDtypeStruct((T, D_MODEL), F32),
            jax.ShapeDtypeStruct((T, D_MODEL), _MXU),
        ],
        scratch_shapes=[pltpu.VMEM((tc + 8, CW), F32), pltpu.VMEM((8, CW), F32)] + _scan_scratch(tc)[:3],
        compiler_params=_params(("parallel", "parallel", "arbitrary")),
    )(proj, proj, proj, conv_w, conv_b, wx_bd, wa_bd, bx, ba, lam)


def _lru_bwd(dya, proj, hlru, conv_w, conv_b, wx_bd, wa_bd, bx, ba, lam, B, S, deps=()):
    T = B * S
    tc = min(SCAN_TILE, S)
    nt = S // tc
    h8 = tc // 8

    def body(dya_ref, xa_ref, xhalo_ref, ga_ref, h_ref, hhalo_ref, cw_ref, cb_ref, wx_ref, wa_ref, bx_ref, ba_ref,
             lam_ref, dxa_ref, dga_ref, dcw_ref, dcb_ref, dwx_ref, dwa_ref, dbx_ref, dba_ref, dlam_ref,
             ext_ref, ext2_ref, carry_ref, dhalo_ref, la_ref, lh_ref, c_ref, dh_ref):
        b = pl.program_id(1)
        t = pl.program_id(2)
        tt = nt - 1 - t

        @pl.when(t == 0)
        def _():
            carry_ref[...] = jnp.zeros_like(carry_ref)
            dhalo_ref[...] = jnp.zeros_like(dhalo_ref)

        @pl.when((t == 0) & (b == 0))
        def _():
            for r in (dcw_ref, dcb_ref, dwx_ref, dwa_ref, dbx_ref, dba_ref, dlam_ref):
                r[...] = jnp.zeros_like(r)

        xa = xa_ref[...]
        ext_ref[0:8, :] = jnp.where(tt == 0, 0.0, xhalo_ref[...])
        ext_ref[8:8 + tc, :] = xa
        xc = _conv_from_ext(ext_ref, xa, cw_ref, cb_ref, tc)
        xcb, i_t, r_t, sp, a, mult = _lru_gates(xc, wx_ref, wa_ref, bx_ref, ba_ref, lam_ref)

        h = h_ref[...]
        ga = ga_ref[...]
        dya_t = dya_ref[...]
        sg = _sigmoid(ga)
        dga_ref[...] = (dya_t * h * (sg * (1.0 + ga * (1.0 - sg)))).astype(dga_ref.dtype)
        dlru = dya_t * (ga * sg)

        row = lax.broadcasted_iota(jnp.int32, a.shape, 0)
        coef = jnp.where(row == tc - 1, 1.0, pltpu.roll(a, tc - 1, 0))
        _scan_tile(coef, dlru, carry_ref[0:1, :], la_ref, lh_ref, c_ref, dh_ref, True)
        dh = dh_ref[...]
        ext2_ref[0:tc, :] = a * dh
        carry_ref[...] = ext2_ref[0:8, :]

        ext2_ref[0:8, :] = jnp.where(tt == 0, 0.0, hhalo_ref[...])
        ext2_ref[8:8 + tc, :] = h
        hprev = ext2_ref[7:7 + tc, :]

        da = dh * hprev
        ix = i_t * xc
        dmult = dh * ix
        di = dh * mult * xc
        dxc = dh * mult * i_t
        dlog_a = da * a - dmult * (a * a) / mult
        dr = dlog_a * ((-LRU_C) * sp)
        dlam_ref[...] += jnp.sum(dlog_a * r_t, axis=0, keepdims=True) * (LRU_C * _sigmoid(-lam_ref[...]))
        dza = dr * r_t * (1.0 - r_t)
        dzx = di * i_t * (1.0 - i_t)
        dzab = _c(dza)
        dzxb = _c(dzx)
        dxc = dxc + _dot_nt(dzxb, wx_ref[0]) + _dot_nt(dzab, wa_ref[0])
        dwx_ref[0] += _dot_tn(xcb, dzxb)
        dwa_ref[0] += _dot_tn(xcb, dzab)
        dbx_ref[...] += jnp.sum(dzx, axis=0, keepdims=True)
        dba_ref[...] += jnp.sum(dza, axis=0, keepdims=True)

        dcb_ref[...] += jnp.sum(dxc, axis=0, keepdims=True)
        dcw_ref[3:4, :] += jnp.sum(dxc * xa, axis=0, keepdims=True)
        dcw_ref[2:3, :] += jnp.sum(dxc * ext_ref[7:7 + tc, :], axis=0, keepdims=True)
        dcw_ref[1:2, :] += jnp.sum(dxc * ext_ref[6:6 + tc, :], axis=0, keepdims=True)
        dcw_ref[0:1, :] += jnp.sum(dxc * ext_ref[5:5 + tc, :], axis=0, keepdims=True)
        ext2_ref[0:tc, :] = dxc
        ext2_ref[tc:tc + 8, :] = dhalo_ref[...]
        dxa = (cw_ref[3:4, :] * dxc + cw_ref[2:3, :] * ext2_ref[1:1 + tc, :]
               + cw_ref[1:2, :] * ext2_ref[2:2 + tc, :] + cw_ref[0:1, :] * ext2_ref[3:3 + tc, :])
        dxa_ref[...] = dxa.astype(dxa_ref.dtype)
        dhalo_ref[...] = ext2_ref[0:8, :]

    row_of = lambda b, t: b * nt + (nt - 1 - t)
    tile = lambda off: pl.BlockSpec((tc, CW), lambda c, b, t: (row_of(b, t), off + c))
    halo = pl.BlockSpec((8, CW), lambda c, b, t: (jnp.maximum(row_of(b, t) * h8 - 1, 0), c))
    vec = pl.BlockSpec((1, CW), lambda c, b, t: (0, c))
    mat = pl.BlockSpec((1, CW, CW), lambda c, b, t: (c, 0, 0))
    cwspec = pl.BlockSpec((CONV, CW), lambda c, b, t: (0, c))
    return pl.pallas_call(
        _after(body, 13, deps),
        name="lru_bwd",
        grid=(N_CT, B, nt),
        in_specs=[tile(0), tile(0), halo, tile(N_CT), tile(0), halo, cwspec, vec, mat, mat, vec, vec, vec]
        + [ANY_SPEC] * len(deps),
        out_specs=[tile(0), tile(0), cwspec, vec, mat, mat, vec, vec, vec],
        out_shape=[
            jax.ShapeDtypeStruct((T, D_MODEL), _MXU),
            jax.ShapeDtypeStruct((T, D_MODEL), _MXU),
            jax.ShapeDtypeStruct((CONV, D_MODEL), F32),
            jax.ShapeDtypeStruct((1, D_MODEL), F32),
            jax.ShapeDtypeStruct((N_CT, CW, CW), F32),
            jax.ShapeDtypeStruct((N_CT, CW, CW), F32),
            jax.ShapeDtypeStruct((1, D_MODEL), F32),
            jax.ShapeDtypeStruct((1, D_MODEL), F32),
            jax.ShapeDtypeStruct((1, D_MODEL), F32),
        ],
        scratch_shapes=[pltpu.VMEM((tc + 8, CW), F32), pltpu.VMEM((tc + 8, CW), F32),
                        pltpu.VMEM((8, CW), F32), pltpu.VMEM((8, CW), F32)] + _scan_scratch(tc),
        compiler_params=_params(("parallel", "arbitrary", "arbitrary")),
    )(dya, proj, proj, proj, hlru, hlru, conv_w, conv_b, wx_bd, wa_bd, bx, ba, lam, *deps)


def _retention_tables(S):
    half = DK // 2
    freqs = ROPE_THETA ** (-jnp.arange(half, dtype=F32) / half)
    ang = jnp.arange(S, dtype=F32)[:, None] * freqs[None, :]
    log_g = jnp.log1p(-(2.0 ** (-5.0 - jnp.arange(HEADS, dtype=F32))))
    idx = jnp.arange(CHUNK, dtype=F32)
    diff = idx[:, None] - idx[None, :]
    inner = jnp.where(diff >= 0, jnp.exp(jnp.maximum(diff, 0.0)[None] * log_g[:, None, None]), 0.0)
    cross = jnp.exp((idx[None, :] + 1.0) * log_g[:, None])[:, :, None]
    state = jnp.exp((CHUNK - 1.0 - idx[None, :]) * log_g[:, None])[:, :, None]
    gam = jnp.broadcast_to(jnp.exp(CHUNK * log_g)[:, None, None], (HEADS, 1, DK))
    return jnp.cos(ang), jnp.sin(ang), inner, cross, state, gam


def _rot(x, cos, sin):
    half = DK // 2
    x1, x2 = x[:, :half], x[:, half:]
    return jnp.concatenate([x1 * cos - x2 * sin, x1 * sin + x2 * cos], axis=-1)


def _rot_t(y, cos, sin):
    half = DK // 2
    y1, y2 = y[:, :half], y[:, half:]
    return jnp.concatenate([y1 * cos + y2 * sin, y2 * cos - y1 * sin], axis=-1)


def _groupnorm(o):
    mu = jnp.mean(o, axis=-1, keepdims=True)
    oc = o - mu
    rs = lax.rsqrt(jnp.mean(oc * oc, axis=-1, keepdims=True) + EPS)
    return oc * rs, rs


def _ret_specs(B, chunk_of):
    rows = RET_CHUNKS * CHUNK
    qkv = lambda g: pl.BlockSpec((B, rows, D_MODEL), lambda c: (0, chunk_of(c), g))
    act = pl.BlockSpec((B, rows, D_MODEL), lambda c: (0, chunk_of(c), 0))
    rope = pl.BlockSpec((rows, DK // 2), lambda c: (chunk_of(c), 0))
    dmat = pl.BlockSpec((HEADS, CHUNK, CHUNK), lambda c: (0, 0, 0))
    dvec = pl.BlockSpec((HEADS, CHUNK, 1), lambda c: (0, 0, 0))
    hrow = pl.BlockSpec((HEADS, 1, DK), lambda c: (0, 0, 0))
    rst = pl.BlockSpec((RET_CHUNKS, B, HEADS, DK, DK), lambda c: (chunk_of(c), 0, 0, 0, 0))
    return qkv, act, rope, dmat, dvec, hrow, rst


def _ret_fwd(proj, tables, gain3, B, S):
    T = B * S
    nc = S // CHUNK
    cos, sin, dmat_t, cd_t, sd_t, gam_t = tables

    def body(q_ref, k_ref, v_ref, gb_ref, cos_ref, sin_ref, dm_ref, cd_ref, sd_ref, gam_ref, gain_ref,
             o_ref, yb_ref, rs_ref, state_ref):
        @pl.when(pl.program_id(0) == 0)
        def _():
            state_ref[...] = jnp.zeros_like(state_ref)

        for cc, b, h in [(cc, b, h) for cc in range(RET_CHUNKS) for b in range(B) for h in range(HEADS)]:
            rows = slice(cc * CHUNK, (cc + 1) * CHUNK)
            cos_t, sin_t = cos_ref[rows, :], sin_ref[rows, :]
            cols = slice(h * DK, (h + 1) * DK)
            qb = _c(_rot(q_ref[b, rows, cols], cos_t, sin_t))
            kb = _c(_rot(k_ref[b, rows, cols], cos_t, sin_t) * (DK ** -0.5))
            v = v_ref[b, rows, cols]
            state = state_ref[b, h]
            sb = _c(state)
            rs_ref[cc, b, h] = sb
            scores = _dot_nt(qb, kb) * dm_ref[h]
            o = _dot(_c(scores), _c(v)) + _dot(qb, sb) * cd_ref[h]
            state_ref[b, h] = gam_ref[h] * state + _dot_tn(kb, _c(v * sd_ref[h]))
            o_ref[b, rows, cols] = o
            n, _ = _groupnorm(o)
            gb = gb_ref[b, rows, cols]
            yb_ref[b, rows, cols] = (gb * _sigmoid(gb) * (n * gain_ref[h])).astype(yb_ref.dtype)

    qkv, act, rope, dmat, dvec, hrow, rst = _ret_specs(B, lambda c: c)
    proj3 = proj.reshape(B, S, proj.shape[1])
    o_pre, yb, states = pl.pallas_call(
        body,
        name="ret_fwd",
        grid=(nc // RET_CHUNKS,),
        in_specs=[qkv(2), qkv(3), qkv(4), qkv(5), rope, rope, dmat, dvec, dvec, hrow, hrow],
        out_specs=[act, act, rst],
        out_shape=[
            jax.ShapeDtypeStruct((B, S, D_MODEL), F32),
            jax.ShapeDtypeStruct((B, S, D_MODEL), _MXU),
            jax.ShapeDtypeStruct((nc, B, HEADS, DK, DK), _MXU),
        ],
        scratch_shapes=[pltpu.VMEM((B, HEADS, DK, DK), F32)],
        compiler_params=_params(("arbitrary",)),
    )(proj3, proj3, proj3, proj3, cos, sin, dmat_t, cd_t, sd_t, gam_t, gain3)
    return o_pre.reshape(T, D_MODEL), yb.reshape(T, D_MODEL), states


def _ret_bwd(dyb, o_pre, proj, states, tables, gain3, B, S, deps=()):
    T = B * S
    nc = S // CHUNK
    cos, sin, dmat_t, cd_t, sd_t, gam_t = tables

    def body(dyb_ref, o_ref, q_ref, k_ref, v_ref, gb_ref, rs_ref, cos_ref, sin_ref, dm_ref, cd_ref, sd_ref, gam_ref,
             gain_ref, dr_ref, dgain_ref, dstate_ref):
        @pl.when(pl.program_id(0) == 0)
        def _():
            dstate_ref[...] = jnp.zeros_like(dstate_ref)
            dgain_ref[...] = jnp.zeros_like(dgain_ref)

        for cc, b, h in [(cc, b, h) for cc in reversed(range(RET_CHUNKS)) for b in range(B) for h in range(HEADS)]:
            rows = slice(cc * CHUNK, (cc + 1) * CHUNK)
            cos_t, sin_t = cos_ref[rows, :], sin_ref[rows, :]
            cols = slice(h * DK, (h + 1) * DK)
            gain = gain_ref[h]
            n, rs = _groupnorm(o_ref[b, rows, cols])
            gb = gb_ref[b, rows, cols]
            sg = _sigmoid(gb)
            dy = dyb_ref[b, rows, cols]
            part = lambda g: slice(g * D_MODEL + h * DK, g * D_MODEL + (h + 1) * DK)
            dr_ref[b, rows, part(3)] = (dy * (n * gain) * (sg * (1.0 + gb * (1.0 - sg)))).astype(dr_ref.dtype)
            dgn = dy * (gb * sg)
            dgain_ref[h] += jnp.sum(dgn * n, axis=0, keepdims=True)
            dn = dgn * gain
            do = rs * (dn - jnp.mean(dn, axis=-1, keepdims=True) - n * jnp.mean(dn * n, axis=-1, keepdims=True))

            qb = _c(_rot(q_ref[b, rows, cols], cos_t, sin_t))
            kb = _c(_rot(k_ref[b, rows, cols], cos_t, sin_t) * (DK ** -0.5))
            v = v_ref[b, rows, cols]
            vb = _c(v)
            vsb = _c(v * sd_ref[h])
            dob = _c(do)
            docb = _c(do * cd_ref[h])
            dmat = dm_ref[h]
            dstate = dstate_ref[b, h]
            dsb = _c(dstate)
            pb = _c(_dot_nt(qb, kb) * dmat)
            dsc = _c(_dot_nt(dob, vb) * dmat)
            dq = _dot(dsc, kb) + _dot_nt(docb, rs_ref[cc, b, h])
            dk = _dot_tn(dsc, qb) + _dot_nt(vsb, dsb)
            dv = _dot_tn(pb, dob) + _dot(kb, dsb) * sd_ref[h]
            dstate_ref[b, h] = gam_ref[h] * dstate + _dot_tn(qb, docb)
            dr_ref[b, rows, part(0)] = _rot_t(dq, cos_t, sin_t).astype(dr_ref.dtype)
            dr_ref[b, rows, part(1)] = (_rot_t(dk, cos_t, sin_t) * (DK ** -0.5)).astype(dr_ref.dtype)
            dr_ref[b, rows, part(2)] = dv.astype(dr_ref.dtype)

    n_steps = nc // RET_CHUNKS
    qkv, act, rope, dmat, dvec, hrow, rst = _ret_specs(B, lambda c: n_steps - 1 - c)
    wide = pl.BlockSpec((B, RET_CHUNKS * CHUNK, 4 * D_MODEL), lambda c: (0, n_steps - 1 - c, 0))
    proj3 = proj.reshape(B, S, proj.shape[1])
    dr, dgain = pl.pallas_call(
        _after(body, 14, deps),
        name="ret_bwd",
        grid=(n_steps,),
        in_specs=[act, act, qkv(2), qkv(3), qkv(4), qkv(5), rst, rope, rope, dmat, dvec, dvec, hrow, hrow]
        + [ANY_SPEC] * len(deps),
        out_specs=[wide, hrow],
        out_shape=[jax.ShapeDtypeStruct((B, S, 4 * D_MODEL), _MXU), jax.ShapeDtypeStruct((HEADS, 1, DK), F32)],
        scratch_shapes=[pltpu.VMEM((B, HEADS, DK, DK), F32)],
        compiler_params=_params(("arbitrary",)),
    )(dyb.reshape(B, S, D_MODEL), o_pre.reshape(B, S, D_MODEL), proj3, proj3, proj3, proj3, states, cos, sin, dmat_t,
      cd_t, sd_t, gam_t, gain3, *deps)
    return dr.reshape(T, 4 * D_MODEL), dgain


def _mid(ya, yb, proj, x2d, tgt2d, wpa, wpb, wout, g_fin):
    T = x2d.shape[0]
    tm = min(MID_TILE, T)
    n_steps = T // tm
    rows = D_MODEL // (2 * N_CHIPS)

    def body(ya_ref, yb_ref, ma_ref, mb_ref, x_ref, t_ref, gf_ref, wpa_hbm, wpb_hbm, wout_hbm,
             loss_ref, dx2_ref, dya_ref, dyb_ref, dm_ref, dgf_ref, gw_hbm, w_ref, acc_ref, sem):
        i = pl.program_id(0)

        @pl.when(i == 0)
        def _():
            loads = [pltpu.make_async_copy(src, w_ref.at[k], sem.at[k]) for k, src in enumerate((wpa_hbm, wpb_hbm, wout_hbm))]
            for cp in loads:
                cp.start()
            for cp in loads:
                cp.wait()
            acc_ref[...] = jnp.zeros_like(acc_ref)
            loss_ref[...] = jnp.zeros_like(loss_ref)
            dgf_ref[...] = jnp.zeros_like(dgf_ref)

        ya_t, yb_t = ya_ref[...], yb_ref[...]
        out_a = _dot(ya_t, w_ref[0])
        out_b = _dot(yb_t, w_ref[1])
        sa = _sigmoid(ma_ref[...])
        sb = _sigmoid(mb_ref[...])
        mgb = _c(sa * out_a + sb * out_b)
        x2 = x_ref[...] + _dot(mgb, w_ref[2])
        r2 = lax.rsqrt(jnp.mean(x2 * x2, axis=-1, keepdims=True) + EPS)
        nx = x2 * r2
        gf = gf_ref[...]
        err = nx * gf - t_ref[...]
        loss_ref[...] += 0.5 * jnp.sum(jnp.mean(err * err, axis=-1, keepdims=True), axis=0, keepdims=True)
        dy = err * (1.0 / D_MODEL)
        dgf_ref[...] += jnp.sum(dy * nx, axis=0, keepdims=True)
        dyg = dy * gf
        dx2 = r2 * (dyg - nx * jnp.mean(dyg * nx, axis=-1, keepdims=True))
        dx2_ref[...] = dx2
        dx2b = _c(dx2)
        dmg = _dot_nt(dx2b, w_ref[2])
        acc_ref[2] += _dot_tn(mgb, dx2b)
        dm_ref[:, :D_MODEL] = (dmg * out_a * sa * (1.0 - sa)).astype(dm_ref.dtype)
        dm_ref[:, D_MODEL:] = (dmg * out_b * sb * (1.0 - sb)).astype(dm_ref.dtype)
        dab = _c(dmg * sa)
        dbb = _c(dmg * sb)
        dya_ref[...] = _dot_nt(dab, w_ref[0])
        dyb_ref[...] = _dot_nt(dbb, w_ref[1])
        acc_ref[0] += _dot_tn(ya_t, dab)
        acc_ref[1] += _dot_tn(yb_t, dbb)

        @pl.when(i == n_steps - 1)
        def _():
            copies = [pltpu.make_async_copy(acc_ref.at[k, pl.ds((2 * p + hf) * rows, rows), :], gw_hbm.at[p, hf, k],
                                            sem.at[(k * N_CHIPS + p) * 2 + hf])
                      for k in range(3) for p in range(N_CHIPS) for hf in range(2)]
            for cp in copies:
                cp.start()
            for cp in copies:
                cp.wait()

    tile = lambda j: pl.BlockSpec((tm, D_MODEL), lambda i: (i, j))
    one = pl.BlockSpec((1, D_MODEL), lambda i: (0, 0))
    anyspec = pl.BlockSpec(memory_space=pl.ANY)
    return pl.pallas_call(
        body,
        name="mid",
        grid=(n_steps,),
        in_specs=[tile(0), tile(0), tile(6), tile(7), tile(0), tile(0), one, anyspec, anyspec, anyspec],
        out_specs=[pl.BlockSpec((1, 1), lambda i: (0, 0)), tile(0), tile(0), tile(0),
                   pl.BlockSpec((tm, 2 * D_MODEL), lambda i: (i, 0)), one, anyspec],
        out_shape=[
            jax.ShapeDtypeStruct((1, 1), F32),
            jax.ShapeDtypeStruct((T, D_MODEL), F32),
            jax.ShapeDtypeStruct((T, D_MODEL), F32),
            jax.ShapeDtypeStruct((T, D_MODEL), F32),
            jax.ShapeDtypeStruct((T, 2 * D_MODEL), _MXU),
            jax.ShapeDtypeStruct((1, D_MODEL), F32),
            jax.ShapeDtypeStruct((N_CHIPS, 2, 3, rows, D_MODEL), F32),
        ],
        scratch_shapes=[pltpu.VMEM((3, D_MODEL, D_MODEL), _MXU), pltpu.VMEM((3, D_MODEL, D_MODEL), F32),
                        pltpu.SemaphoreType.DMA((3 * N_CHIPS * 2,))],
        compiler_params=_params(("arbitrary",)),
    )(ya, yb, proj, proj, x2d, tgt2d, g_fin, wpa, wpb, wout)


def _inproj_bwd_dx(dparts, w_all, x2d, dx2, g_in, first, count, prev, name, deps=()):
    T = x2d.shape[0]
    tm = min(DX_TILE, T)
    n_d = len(dparts)
    groups = [(a, k) for a, d in enumerate(dparts) for k in range(d.shape[1] // D_MODEL)]
    dg_start = jnp.zeros((1, D_MODEL), F32) if prev is None else prev[1]
    carried = () if prev is None else (prev[0],)

    def body(*refs):
        d_refs = refs[:n_d]
        x_ref, dx2_ref, g_ref, dg0_ref, w_hbm = refs[n_d:n_d + 5]
        dx_ref, dg_ref, w_ref, sem = refs[-4:]

        @pl.when(pl.program_id(0) == 0)
        def _():
            cp = pltpu.make_async_copy(w_hbm, w_ref, sem)
            cp.start()
            cp.wait()
            dg_ref[...] = dg0_ref[...]

        dh = jnp.zeros((tm, D_MODEL), F32)
        for j, (a, k) in enumerate(groups):
            dh = dh + _dot_nt(d_refs[a][:, k * D_MODEL:(k + 1) * D_MODEL],
                              w_ref[j // 2, :, (j % 2) * D_MODEL:(j % 2 + 1) * D_MODEL])
        x = x_ref[...]
        r = lax.rsqrt(jnp.mean(x * x, axis=-1, keepdims=True) + EPS)
        nx = x * r
        dg_ref[...] += jnp.sum(dh * nx, axis=0, keepdims=True)
        dhg = dh * g_ref[...]
        dx_ref[...] = dx2_ref[...] + r * (dhg - nx * jnp.mean(dhg * nx, axis=-1, keepdims=True))

    tile = pl.BlockSpec((tm, D_MODEL), lambda i: (first + i, 0))
    one = pl.BlockSpec((1, D_MODEL), lambda i: (0, 0))
    return pl.pallas_call(
        body,
        name=name,
        grid=(count,),
        in_specs=[pl.BlockSpec((tm, d.shape[1]), lambda i: (first + i, 0)) for d in dparts]
        + [tile, tile, one, one, ANY_SPEC] + [ANY_SPEC] * (len(carried) + len(deps)),
        out_specs=[tile, one],
        out_shape=[jax.ShapeDtypeStruct((T, D_MODEL), F32), jax.ShapeDtypeStruct((1, D_MODEL), F32)],
        input_output_aliases={n_d + 5: 0} if carried else {},
        scratch_shapes=[pltpu.VMEM(w_all.shape, w_all.dtype), pltpu.SemaphoreType.DMA],
        compiler_params=_params(("arbitrary",)),
    )(*dparts, x2d, dx2, g_in, dg_start, w_all, *carried, *deps)


def _inproj_bwd_dw(ht, dparts, name, deps=()):
    T = ht.shape[1]
    tn = DW_COLS
    half = D_MODEL // 2
    per_chip = 2 * D_MODEL // tn
    n_d = len(dparts)
    tiles = [(a, t) for a, d in enumerate(dparts) for t in range(d.shape[1] // tn)]
    offs = [sum(d.shape[1] // tn for d in dparts[:a]) for a in range(n_d)]

    def body(*refs):
        ht_ref = refs[0]
        d_refs = refs[1:1 + n_d]
        out_ref = refs[-1]
        t = pl.program_id(0)

        for a in range(n_d):
            lo, hi = offs[a], offs[a] + dparts[a].shape[1] // tn

            @pl.when((t >= lo) & (t < hi))
            def _(a=a):
                g = _dot(ht_ref[...], d_refs[a][...])
                out_ref[0, 0] = g[:half]
                out_ref[0, 1] = g[half:]

    def dspec(a):
        n_a = dparts[a].shape[1] // tn
        return pl.BlockSpec((T, tn), lambda t: (0, jnp.clip(t - offs[a], 0, n_a - 1)))

    return pl.pallas_call(
        body,
        name=name,
        grid=(len(tiles),),
        in_specs=[pl.BlockSpec((D_MODEL, T), lambda t: (0, 0))] + [dspec(a) for a in range(n_d)]
        + [ANY_SPEC] * len(deps),
        out_specs=pl.BlockSpec((1, 2, half, tn), lambda t: (t // per_chip, 0, 0, t % per_chip)),
        out_shape=jax.ShapeDtypeStruct((len(tiles) // per_chip, 2, half, 2 * D_MODEL), F32),
        compiler_params=_params(("parallel",)),
    )(ht, *dparts, *deps)


def _coords():
    return lax.axis_index("x"), lax.axis_index("y"), lax.axis_index("c")


def _other_chips(x, y):
    return [(1 - x, y), (x, 1 - y), (1 - x, 1 - y)]


def _chunks(rows, n):
    size = rows // n
    return [pl.ds(q * size, size) for q in range(n)]


HBM_SPEC = pl.BlockSpec(memory_space=pltpu.HBM)
SEM_SPEC = pl.BlockSpec(memory_space=pltpu.SEMAPHORE)
DATAFLOW = pltpu.SideEffectType.DATAFLOW_SIDE_EFFECTING


def _copies_start(bufs, plan, n_copies, name, deps=()):
    n = len(bufs)
    n_deps = len(deps)

    def body(*refs):
        ins = refs[:n]
        send_sems, recv_sems = refs[n + n_deps], refs[n + n_deps + 1]
        token = refs[-1]
        for k, send, _ in plan(ins):
            if send is not None:
                src, dst, dev, pred = send
                cp = pltpu.make_async_remote_copy(src_ref=src, dst_ref=dst, send_sem=send_sems.at[k],
                                                  recv_sem=recv_sems.at[k], device_id=dev, device_id_type=MESH)
                if pred is None:
                    cp.start()
                else:
                    pl.when(pred)(cp.start)
        token[...] = jnp.zeros_like(token)

    hbm = [pltpu.with_memory_space_constraint(b, pltpu.HBM) for b in bufs]
    outs = pl.pallas_call(
        body,
        name=name,
        in_specs=[HBM_SPEC] * n + [ANY_SPEC] * n_deps,
        out_specs=(SEM_SPEC, SEM_SPEC, *([HBM_SPEC] * n), pl.BlockSpec(memory_space=pltpu.VMEM)),
        out_shape=(pltpu.SemaphoreType.DMA((n_copies,)), pltpu.SemaphoreType.DMA((n_copies,)),
                   *[pltpu.HBM(b.shape, b.dtype) for b in bufs], jax.ShapeDtypeStruct((8, 128), F32)),
        input_output_aliases={a: 2 + a for a in range(n)},
        compiler_params=pltpu.CompilerParams(has_side_effects=DATAFLOW),
    )(*hbm, *deps)
    return outs[0], outs[1], list(outs[2:2 + n]), outs[-1]


def _copies_wait(send_sems, recv_sems, bufs, after, plan, name, only=None):
    n = len(bufs)

    def body(*refs):
        ins = refs[:n]
        s_sems, r_sems = refs[n], refs[n + 1]
        for k, send, recv in plan(ins):
            if only is not None and k not in only:
                continue
            if send is not None:
                src, dst, dev, pred = send
                cp = pltpu.make_async_remote_copy(src_ref=src, dst_ref=dst, send_sem=s_sems.at[k],
                                                  recv_sem=r_sems.at[k], device_id=dev, device_id_type=MESH)
                if pred is None:
                    cp.wait_send()
                else:
                    pl.when(pred)(cp.wait_send)
            if recv is not None:
                dst, pred = recv
                cp = pltpu.make_async_remote_copy(src_ref=dst, dst_ref=dst, send_sem=s_sems.at[k],
                                                  recv_sem=r_sems.at[k], device_id=_coords(), device_id_type=MESH)
                if pred is None:
                    cp.wait_recv()
                else:
                    pl.when(pred)(cp.wait_recv)

    outs = pl.pallas_call(
        body,
        name=name,
        in_specs=[HBM_SPEC] * n + [SEM_SPEC, SEM_SPEC, pl.BlockSpec(memory_space=pl.ANY)],
        out_specs=[HBM_SPEC] * n,
        out_shape=[pltpu.HBM(b.shape, b.dtype) for b in bufs],
        input_output_aliases={a: a for a in range(n)},
        compiler_params=pltpu.CompilerParams(has_side_effects=DATAFLOW),
    )(*bufs, send_sems, recv_sems, after)
    return list(outs)


def _gather_plan(n_bufs):
    def plan(refs):
        x, y, c = _coords()
        me = 2 * x + y
        out = []
        for k, (px, py) in enumerate(_other_chips(x, y)):
            for a in range(n_bufs):
                out.append((k * n_bufs + a, (refs[a].at[me], refs[a].at[me], (px, py, c), None),
                            (refs[a].at[2 * px + py], None)))
        return out
    return plan


def _cast_into_slot(ws, name, deps=()):
    n = len(ws)
    nt = 2

    def body(s_ref, *refs):
        outs = refs[len(refs) - n:]
        for a in range(n):
            outs[a][0] = refs[a][...].astype(outs[a].dtype)

    xi, yi, _ = _coords()
    return pl.pallas_call(
        body,
        name=name,
        grid_spec=pltpu.PrefetchScalarGridSpec(
            num_scalar_prefetch=1,
            grid=(2, nt),
            in_specs=[pl.BlockSpec((1, w.shape[1] // nt, w.shape[2]), lambda hf, i, s: (hf, i, 0)) for w in ws]
            + [ANY_SPEC] * len(deps),
            out_specs=[pl.BlockSpec((1, 1, w.shape[1] // nt, w.shape[2]), lambda hf, i, s: (s[0], hf, i, 0)) for w in ws],
        ),
        out_shape=[jax.ShapeDtypeStruct((N_CHIPS,) + w.shape, _MXU) for w in ws],
        compiler_params=_params(("parallel", "parallel")),
    )((2 * xi + yi).reshape(1).astype(jnp.int32), *ws, *deps)


def _chip_gather_plan(stage, n_bufs):
    def plan(refs):
        x, y, c = _coords()
        me = 2 * x + y
        near = [(1 - x, y), (x, 1 - y)]
        slots = [2 * (1 - x) + y, 2 * x + (1 - y), 2 * (1 - x) + (1 - y)]
        sibling = (x, y, 1 - c)
        pass_to = (jnp.where(c == 0, x, 1 - x), jnp.where(c == 0, 1 - y, y), c)
        pass_slot = jnp.where(c == 0, slots[0], slots[1])
        out = []

        def move(src_slot, to, land_slot, land_core, pieces):
            for a, buf in enumerate(refs):
                for rows in _chunks(buf.shape[2], pieces[a]):
                    out.append((len(out), (buf.at[src_slot, c, rows], buf.at[src_slot, c, rows], to, None),
                                (buf.at[land_slot, land_core, rows], None)))

        if stage == "near":
            for k, chip in enumerate(near):
                move(me, (*chip, c), slots[k], c, NEAR_PIECES[:n_bufs])
        elif stage == "pass":
            move(pass_slot, pass_to, slots[2], c, PASS_PIECES[:n_bufs])
            for k in range(2):
                move(slots[k], sibling, slots[k], 1 - c, [1] * n_bufs)
        else:
            move(slots[2], sibling, slots[2], 1 - c, [1] * n_bufs)
        return out
    return plan


NEAR_PIECES = (2, 1)
PASS_PIECES = (2, 1)


def _chip_gather_copies(stage, n_bufs):
    if stage == "near":
        return 2 * sum(NEAR_PIECES[:n_bufs]), None
    if stage == "pass":
        n_pass = sum(PASS_PIECES[:n_bufs])
        return n_pass + 2 * n_bufs, set(range(n_pass))
    return n_bufs, None


def _swap_plan(n_slabs):
    def plan(refs):
        x, y, c = _coords()
        out, k = [], 0
        for i, n in enumerate(n_slabs):
            g, land = refs[2 * i], refs[2 * i + 1]
            for p in range(n):
                out.append((k, (g.at[p, 1 - c], land.at[p], (x, y, 1 - c), None), (land.at[p], None)))
                k += 1
        return out
    return plan


def _is_one_of(chip, dests):
    hit = chip == dests[0]
    for d in dests[1:]:
        hit = hit | (chip == d)
    return hit


def _slab_of(chip, dests):
    return sum(j * (chip == d).astype(jnp.int32) for j, d in enumerate(dests))


def _scatter_plan(dest_sets):
    def plan(refs):
        x, y, c = _coords()
        me = 2 * x + y
        out = []
        for k, (px, py) in enumerate(_other_chips(x, y)):
            peer = 2 * px + py
            for i, dests in enumerate(dest_sets):
                cs, land = refs[2 * i], refs[2 * i + 1]
                everyone = len(dests) == N_CHIPS
                send = (cs.at[_slab_of(peer, dests)], land.at[k], (px, py, c),
                        None if everyone else _is_one_of(peer, dests))
                recv = (land.at[k], None if everyone else _is_one_of(me, dests))
                out.append((k * len(dest_sets) + i, send, recv))
        return out
    return plan


def _join_plan(rows, n_pieces):
    def plan(refs):
        x, y, c = _coords()
        (buf,) = refs
        return [(i, (buf.at[c, piece], buf.at[c, piece], (x, y, 1 - c), None), (buf.at[1 - c, piece], None))
                for i, piece in enumerate(_chunks(rows, n_pieces))]
    return plan


def _join_plans(parts):
    def plan(refs):
        out, b0, k0 = [], 0, 0
        for part_plan, n_bufs, n_copies in parts:
            out += [(k0 + k, send, recv) for k, send, recv in part_plan(refs[b0:b0 + n_bufs])]
            b0 += n_bufs
            k0 += n_copies
        return out
    return plan


def _allgather_plan():
    def plan(refs):
        x, y, c = _coords()
        (land,) = refs
        me = 4 * x + 2 * y + c
        out = []
        for r in range(1, 8):
            px = 1 - x if r & 4 else x
            py = 1 - y if r & 2 else y
            pc = 1 - c if r & 1 else c
            out.append((r - 1, (land.at[me], land.at[me], (px, py, pc), None), (land.at[4 * px + 2 * py + pc], None)))
        return out
    return plan


def _sum_gathered(land, name):
    def body(land_ref, o_ref):
        acc = land_ref[0]
        for d in range(1, 8):
            acc = acc + land_ref[d]
        o_ref[...] = acc

    return pl.pallas_call(
        body,
        name=name,
        out_shape=jax.ShapeDtypeStruct(land.shape[1:], F32),
        compiler_params=_params(),
    )(land)


def _join_halves(bufs, n_chunks, name, deps=()):
    n = len(bufs)
    pieces = [(a, rows) for a in range(n) for rows in _chunks(bufs[a].shape[1], n_chunks[a])]
    n_p = len(pieces)

    def body(*refs):
        outs = refs[-n - 2:-2]
        send_sems, recv_sems = refs[-2:]
        x, y, c = _coords()

        def copy(i, half):
            a, rows = pieces[i]
            return pltpu.make_async_remote_copy(
                src_ref=outs[a].at[half, rows], dst_ref=outs[a].at[half, rows], send_sem=send_sems.at[i],
                recv_sem=recv_sems.at[i], device_id=(x, y, 1 - c), device_id_type=MESH)

        sends = [copy(i, c) for i in range(n_p)]
        for cp in sends:
            cp.start()
        for i in range(n_p):
            copy(i, 1 - c).wait_recv()
        for cp in sends:
            cp.wait_send()

    anyspec = pl.BlockSpec(memory_space=pl.ANY)
    sems = pltpu.SemaphoreType.DMA((n_p,))
    return pl.pallas_call(
        body,
        name=name,
        in_specs=[anyspec] * (n + len(deps)),
        out_specs=[anyspec] * n,
        out_shape=[jax.ShapeDtypeStruct(b.shape, b.dtype) for b in bufs],
        input_output_aliases={a: a for a in range(n)},
        scratch_shapes=[sems, sems],
    )(*bufs, *deps)


def _row_tile(rows, cap):
    t = cap
    while rows % t:
        t //= 2
    return t


def _add_my_half(g, r, name):
    n_slabs, _, R, C = g.shape
    tr = R if n_slabs > 1 else _row_tile(R, SUM_ROWS)

    def body(c_ref, g_ref, r_ref, o_ref):
        o_ref[...] = (g_ref[0] + r_ref[...]).astype(o_ref.dtype)

    return pl.pallas_call(
        body,
        name=name,
        grid_spec=pltpu.PrefetchScalarGridSpec(
            num_scalar_prefetch=1,
            grid=(n_slabs, R // tr),
            in_specs=[pl.BlockSpec((1, 1, tr, C), lambda p, i, c_ref: (p, c_ref[0], i, 0)),
                      pl.BlockSpec((1, tr, C), lambda p, i, c_ref: (p, i, 0))],
            out_specs=pl.BlockSpec((1, tr, C), lambda p, i, c_ref: (p, i, 0)),
        ),
        out_shape=jax.ShapeDtypeStruct(r.shape, jnp.bfloat16),
        compiler_params=_params(("parallel", "parallel")),
    )(lax.axis_index("c").reshape(1).astype(jnp.int32), g, r)


def _sum_slabs(own, got, name, deps=()):
    _, R, C = own.shape
    tr = _row_tile(R, SUM_ROWS)

    def body(s_ref, own_ref, got_ref, *rest):
        rest[-1][0] = ((own_ref[0].astype(F32) + got_ref[0].astype(F32)) + got_ref[1].astype(F32)) + got_ref[2].astype(F32)

    xi, yi, ci = _coords()
    return pl.pallas_call(
        body,
        name=name,
        grid_spec=pltpu.PrefetchScalarGridSpec(
            num_scalar_prefetch=1,
            grid=(R // tr,),
            in_specs=[pl.BlockSpec((1, tr, C), lambda i, s: (s[0], i, 0)),
                      pl.BlockSpec((3, tr, C), lambda i, s: (0, i, 0))] + [ANY_SPEC] * len(deps),
            out_specs=pl.BlockSpec((1, tr, C), lambda i, s: (s[1], i, 0)),
        ),
        out_shape=jax.ShapeDtypeStruct((2, R, C), F32),
        compiler_params=_params(("parallel",)),
    )(jnp.stack([2 * xi + yi, ci]).astype(jnp.int32), own, got, *deps)


def _sum_parts(owns, got, dest_sets, name):
    n = len(owns)
    _, R, C = owns[0].shape
    tr = _row_tile(R, SUM_ROWS)

    def body(s_ref, *refs):
        got_ref, o_ref = refs[n], refs[-1]
        total = jnp.zeros((tr, C), F32)
        for i in range(n):
            total = total + jnp.where(s_ref[2 + 2 * i] == 1, refs[i][0].astype(F32), 0.0)
        o_ref[0] = ((total + got_ref[0].astype(F32)) + got_ref[1].astype(F32)) + got_ref[2].astype(F32)

    xi, yi, ci = _coords()
    me = 2 * xi + yi
    scalars = [ci, ci]
    for dests in dest_sets:
        scalars += [_is_one_of(me, dests).astype(jnp.int32), _slab_of(me, dests)]
    own_spec = lambda i: pl.BlockSpec((1, tr, C), lambda r, s: (s[3 + 2 * i], r, 0))
    return pl.pallas_call(
        body,
        name=name,
        grid_spec=pltpu.PrefetchScalarGridSpec(
            num_scalar_prefetch=1,
            grid=(R // tr,),
            in_specs=[own_spec(i) for i in range(n)] + [pl.BlockSpec((3, tr, C), lambda r, s: (0, r, 0))],
            out_specs=pl.BlockSpec((1, tr, C), lambda r, s: (s[0], r, 0)),
        ),
        out_shape=jax.ShapeDtypeStruct((2, R, C), F32),
        compiler_params=_params(("parallel",)),
    )(jnp.stack(scalars).astype(jnp.int32), *owns, got)


def _adamw_math(w, g, m, v):
    m = ADAM_B1 * m + (1.0 - ADAM_B1) * g
    v = ADAM_B2 * v + (1.0 - ADAM_B2) * (g * g)
    m_hat = m / (1.0 - ADAM_B1 ** ADAM_STEP)
    v_hat = v / (1.0 - ADAM_B2 ** ADAM_STEP)
    delta = -ADAM_LR * (m_hat / (jnp.sqrt(v_hat) + ADAM_EPS) + ADAM_WD * w)
    return delta, m, v


def _adamw_halves(ws, g, ms, vs, half, prev, name, deps=()):
    n = len(ws)
    _, _, R, C = g.shape
    tr = _row_tile(R, ADAMW_ROWS)
    steps = R // tr
    carried = [] if prev is None else [a for four in prev for a in four]
    both = half is None
    which = (lambda i, s: i // steps) if both else (lambda i, s: s[0])
    half = 0 if both else half

    def body(s_ref, *refs):
        w_refs, g_refs, m_refs, v_refs = (refs[k * n:(k + 1) * n] for k in range(4))
        outs = refs[len(refs) - 4 * n:]
        for a in range(n):
            grad = g_refs[a][0, 0]
            d, mn, vn = _adamw_math(w_refs[a][...], grad, m_refs[a][...], v_refs[a][...])
            for o, val in zip(outs[4 * a:4 * a + 4], (grad, d, mn, vn)):
                o[...] = val

    rows = pl.BlockSpec((tr, C), lambda i, s: (which(i, s) * steps + i % steps, 0))
    grad_spec = lambda a: pl.BlockSpec((1, 1, tr, C), lambda i, s: (which(i, s), a, i % steps, 0))
    n_in = 4 * n
    outs = pl.pallas_call(
        body,
        name=name,
        grid_spec=pltpu.PrefetchScalarGridSpec(
            num_scalar_prefetch=1,
            grid=(2 * steps if both else steps,),
            in_specs=[rows] * n + [grad_spec(a) for a in range(n)] + [rows] * (2 * n)
            + [ANY_SPEC] * (len(carried) + len(deps)),
            out_specs=[rows] * (4 * n),
        ),
        out_shape=[jax.ShapeDtypeStruct((2 * R, C), F32)] * (4 * n),
        input_output_aliases={1 + n_in + k: k for k in range(len(carried))},
        compiler_params=_params(("parallel",)),
    )(jnp.reshape(half, (1,)).astype(jnp.int32), *ws, *([g] * n), *ms, *vs, *carried, *deps)
    return [outs[4 * a:4 * a + 4] for a in range(n)]


def _adamw_small(ws, gs, ms, vs, name):
    n = len(ws)

    def body(*refs):
        for a in range(n):
            d, mn, vn = _adamw_math(refs[a][...], refs[n + a][...], refs[2 * n + a][...], refs[3 * n + a][...])
            refs[4 * n + a][...] = d
            refs[5 * n + a][...] = mn
            refs[6 * n + a][...] = vn

    shapes = [jax.ShapeDtypeStruct(w.shape, F32) for w in ws]
    outs = pl.pallas_call(
        body,
        name=name,
        out_shape=shapes * 3,
        compiler_params=_params(),
    )(*ws, *gs, *ms, *vs)
    return outs[:n], outs[n:2 * n], outs[2 * n:]


def _to_blockdiag(w):
    per = CW // LRU_BW
    w4 = w.reshape(N_CT, per, LRU_BW, LRU_BW)
    eye = jnp.eye(per, dtype=w.dtype)
    return (w4[:, :, :, None, :] * eye[None, :, None, :, None]).reshape(N_CT, CW, CW)


def _from_blockdiag(g):
    per = CW // LRU_BW
    g5 = g.reshape(N_CT, per, LRU_BW, per, LRU_BW)
    return jnp.stack([g5[:, b, :, b, :] for b in range(per)], axis=1).reshape(LRU_BLOCKS, LRU_BW, LRU_BW)


def _local_grads(x2d, tgt2d, B, S, g_in, in_proj, conv_b, gate_x_w, gate_x_b, gate_a_w, gate_a_b, lam,
                 proj_weights, g_fin, reduce):
    wx_bd = _c(_to_blockdiag(gate_x_w))
    wa_bd = _c(_to_blockdiag(gate_a_w))
    tables = _retention_tables(S)

    proj, ht, w_all, conv_w, gain = in_proj(x2d, g_in, (tables[0], tables[1], wx_bd, wa_bd))
    gain3 = gain.reshape(HEADS, 1, DK)
    hlru, ya = _lru_fwd(proj, conv_w, conv_b, wx_bd, wa_bd, gate_x_b, gate_a_b, lam, B, S)
    o_pre, yb, states = _ret_fwd(proj, tables, gain3, B, S)
    wpa, wpb, wout = proj_weights(yb)
    loss, dx2, dya, dyb, dm, dgf, gw_proj = _mid(ya, yb, proj, x2d, tgt2d, wpa, wpb, wout, g_fin)
    g3 = _inproj_bwd_dw(ht, [dm], "inproj_bwd_dw_m")
    deps = reduce.m_ready(gw_proj, g3)
    dr, dgain = _ret_bwd(dyb, o_pre, proj, states, tables, gain3, B, S, deps)
    deps = reduce.ret_done(dr)
    g12 = _inproj_bwd_dw(ht, [dr], "inproj_bwd_dw_r", deps)
    deps = reduce.r_ready(g12)
    dxa, dga, dcw, dcb, dwx_bd, dwa_bd, dbx, dba, dlam = _lru_bwd(
        dya, proj, hlru, conv_w, conv_b, wx_bd, wa_bd, gate_x_b, gate_a_b, lam, B, S, deps)
    small = dict(conv_w=dcw, conv_b=dcb, gate_x_w=_from_blockdiag(dwx_bd), gate_x_b=dbx,
                 gate_a_w=_from_blockdiag(dwa_bd), gate_a_b=dba, lru_lambda=dlam, gn_gain=dgain.reshape(HEADS, DK),
                 norm_final=dgf)
    loss_rows = jnp.broadcast_to(loss, (SUBLANES, LANES))
    deps = reduce.lru_done(dxa, jnp.concatenate([_pack_small(small), loss_rows], axis=0))
    g0 = _inproj_bwd_dw(ht, [dxa, dga], "inproj_bwd_dw_a", deps)
    deps = reduce.a_ready(g0)
    n_tiles = x2d.shape[0] // min(DX_TILE, x2d.shape[0])
    grad_x, dgin = _inproj_bwd_dx([dxa, dga, dr, dm], w_all, x2d, dx2, g_in, 0, n_tiles, None, "inproj_bwd_dx", deps)
    return grad_x, dgin


ALL_CHIPS = (0, 1, 2, 3)


class _GradReduce:
    def __init__(self, proj_done):
        self.pending = {}
        self.proj_done = proj_done
        self.land_in = None

    def _start(self, key, parts, name):
        bufs, plans, shared = [], [], None
        for part_bufs, plan, n_copies, part_shared in parts:
            if part_shared is not None:
                shared = len(bufs) + part_shared
            plans.append((plan, len(part_bufs), n_copies))
            bufs += part_bufs
        plan = _join_plans(plans)
        send_sems, recv_sems, bufs, token = _copies_start(bufs, plan, sum(p[2] for p in plans), name + "_start")
        if shared is not None:
            self.land_in = bufs[shared]
        self.pending[key] = (send_sems, recv_sems, bufs, plan, name + "_wait", shared)
        return (token,)

    def _finish(self, key, after):
        send_sems, recv_sems, bufs, plan, name, shared = self.pending.pop(key)
        if shared is not None:
            bufs[shared] = self.land_in
        bufs = _copies_wait(send_sems, recv_sems, bufs, after, plan, name)
        if shared is not None:
            self.land_in = bufs[shared]
        return bufs

    @staticmethod
    def _swap(pieces):
        bufs = []
        for g in pieces:
            bufs += [g, lax.empty((g.shape[0],) + g.shape[2:], F32)]
        n_slabs = [g.shape[0] for g in pieces]
        return bufs, _swap_plan(n_slabs), sum(n_slabs), None

    def _scatter(self, sums, dest_sets):
        bufs = []
        for cs in sums:
            bufs += [cs, lax.empty((3,) + cs.shape[1:], cs.dtype)]
        if self.land_in is not None:
            bufs[-1] = self.land_in
        return bufs, _scatter_plan(dest_sets), 3 * len(sums), len(bufs) - 1

    @staticmethod
    def _gather8(block):
        x, y, c = _coords()
        land = lax.dynamic_update_slice(lax.empty((8,) + block.shape, F32), block[None], (4 * x + 2 * y + c, 0, 0))
        return [land], _allgather_plan(), 7, None

    def m_ready(self, gw_proj, g3):
        rows = gw_proj.shape[2] * gw_proj.shape[3]
        return self._start("m", [self._swap([gw_proj.reshape(N_CHIPS, 2, rows, D_MODEL), g3])], "swap_m")

    def ret_done(self, after):
        proj, land_p, g3, land_3 = self._finish("m", after)
        sums_m = [_add_my_half(proj, land_p, "chip_sum_proj"), _add_my_half(g3, land_3, "chip_sum_m")]
        return self._start("sm", [self._scatter(sums_m, [ALL_CHIPS, (3,)])], "scatter_m")

    def r_ready(self, g12):
        return self._start("r", [self._swap([g12])], "swap_r")

    def lru_done(self, after, packed):
        g12, land_12 = self._finish("r", after)
        sums_r = [_add_my_half(g12, land_12, "chip_sum_r")]
        return (self._start("sr", [self._scatter(sums_r, [(1, 2)])], "scatter_r")
                + self._start("small", [self._gather8(packed)], "gather_small"))

    def a_ready(self, g0):
        (token,) = self._start("a", [self._swap([g0])], "swap_a")
        csp, gotp, self.cs3, _ = self._finish("sm", token)
        half_proj = _sum_slabs(csp, gotp, "sum_w_proj")
        g0, land_0 = self._finish("a", half_proj)
        deps = self._start("sa", [self._scatter([_add_my_half(g0, land_0, "chip_sum_a")], [(0,)])], "scatter_a")
        self.proj_done(_join_halves([half_proj], [4], "join_halves_proj", deps)[0])
        return deps

    def finish(self, dgin, w_in_done):
        (token,) = self._start("n", [self._gather8(dgin)], "gather_norm_in")
        (small,) = self._finish("small", token)
        cs12, _ = self._finish("sr", token)
        cs0, _ = self._finish("sa", token)
        half_in = _sum_parts([self.cs3, cs12, cs0], self.land_in, [(3,), (1, 2), (0,)], "sum_w_in")
        deps = self._start("j", [([half_in], _join_plan(half_in.shape[1], JOIN_PIECES), JOIN_PIECES, None)], "join_w_in")
        first = w_in_done(self.pending["j"][2][0], True, None, deps)
        (g_in,) = self._finish("j", first[1])
        done = w_in_done(g_in, False, first, ())
        (norm_in,) = self._finish("n", done[1])
        return _sum_gathered(small, "sum_small_grads"), _sum_gathered(norm_in, "sum_norm_in_grad")


_SMALL = ("gate_x_w", "gate_a_w", "conv_w", "conv_b", "gate_x_b", "gate_a_b", "lru_lambda", "gn_gain", "norm_final")
_SMALL_SHAPES = dict(gate_x_w=(LRU_BLOCKS, LRU_BW, LRU_BW), gate_a_w=(LRU_BLOCKS, LRU_BW, LRU_BW),
                     norm_in=(1, D_MODEL), conv_w=(CONV, D_MODEL), conv_b=(1, D_MODEL), gate_x_b=(1, D_MODEL),
                     gate_a_b=(1, D_MODEL), lru_lambda=(1, D_MODEL), gn_gain=(HEADS, DK), norm_final=(1, D_MODEL))


def _pack_small(small):
    return jnp.concatenate([small[k].reshape(-1, 128) for k in _SMALL], axis=0)


def _unpack_small(packed):
    out, r = {}, 0
    for k in _SMALL:
        shape = _SMALL_SHAPES[k]
        rows = 1
        for s in shape:
            rows *= s
        rows //= 128
        out[k] = packed[r:r + rows].reshape(shape)
        r += rows
    return out


def kernel(x, norm_in, w_in, conv_w, conv_b, gate_x_w, gate_x_b, gate_a_w, gate_a_b, lru_lambda, gn_gain, w_proj_a, w_proj_b, w_out, norm_final, loss_target, m_norm_in, m_w_in, m_conv_w, m_conv_b, m_gate_x_w, m_gate_x_b, m_gate_a_w, m_gate_a_b, m_lru_lambda, m_gn_gain, m_w_proj_a, m_w_proj_b, m_w_out, m_norm_final, v_norm_in, v_w_in, v_conv_w, v_conv_b, v_gate_x_w, v_gate_x_b, v_gate_a_w, v_gate_a_b, v_lru_lambda, v_gn_gain, v_w_proj_a, v_w_proj_b, v_w_out, v_norm_final):
    B, S, _ = x.shape
    T = B * S
    xi, yi, ci = _coords()
    chip = 2 * xi + yi

    cshard = D_MODEL // N_CHIPS
    mine = _cast_into_slot([w_in[0].reshape(2, D_MODEL // 2, 2 * D_MODEL)], "cast_w_in")
    plan = _gather_plan(3)
    pending_proj = []
    gshard = DK // N_CHIPS
    tiny = jnp.concatenate([conv_w[0], jnp.zeros((4, cshard), F32), jnp.pad(gn_gain[0], ((0, 4), (0, cshard - gshard)))],
                           axis=0).reshape(1, 2, SUBLANES, cshard)
    tiny_buf = lax.dynamic_update_slice(lax.empty((N_CHIPS, 2, SUBLANES, cshard), F32), tiny, (chip, 0, 0, 0))
    near_plan, pass_plan, far_plan = (_chip_gather_plan(stage, 2) for stage in ("near", "pass", "far"))
    (n_near, _), (n_pass, passed_on), (n_far, _) = (_chip_gather_copies(stage, 2) for stage in ("near", "pass", "far"))
    halves = set(range(n_pass)) - passed_on
    near_s, near_r, bufs, near_token = _copies_start([mine[0], tiny_buf], near_plan, n_near, "gather_near_start")

    def in_proj(x2d, g_in, meanwhile):
        mine_proj = _cast_into_slot([w[0].reshape(2, cshard // 2, D_MODEL) for w in (w_proj_a, w_proj_b, w_out)],
                                    "cast_w_proj", (near_token,))
        as_w = lambda b: b[0].reshape(N_CHIPS, D_MODEL, 2 * D_MODEL)
        slot_x, slot_y, slot_d = 2 * (1 - xi) + yi, 2 * xi + (1 - yi), 2 * (1 - xi) + (1 - yi)
        ids = lambda *chips: jnp.stack(chips).astype(jnp.int32)
        proj, hb, ht = _inproj_first(x2d, g_in, as_w(bufs), ids(chip), "inproj_own", (near_token, *meanwhile))
        got = _copies_wait(near_s, near_r, bufs, proj, near_plan, "gather_near_wait")
        pass_s, pass_r, got, pass_token = _copies_start(got, pass_plan, n_pass, "gather_pass_start")
        got = _copies_wait(pass_s, pass_r, got, pass_token, pass_plan, "gather_pass_wait_halves", only=halves)
        proj = _inproj_more(hb, as_w(got), ids(slot_x, slot_y), proj, "inproj_near")
        got = _copies_wait(pass_s, pass_r, got, proj, pass_plan, "gather_pass_wait_far", only=passed_on)
        pending_proj.append(_copies_start(mine_proj, plan, 9, "gather_proj_start", (got[0],)))
        far_s, far_r, got, far_token = _copies_start(got, far_plan, n_far, "gather_far_start")
        got = _copies_wait(far_s, far_r, got, far_token, far_plan, "gather_far_wait")
        proj = _inproj_more(hb, as_w(got), ids(slot_d), proj, "inproj_far")
        tiny_all = got[1].reshape(N_CHIPS, 2 * SUBLANES, cshard)
        conv_w_full = jnp.transpose(tiny_all[:, 0:CONV, :], (1, 0, 2)).reshape(CONV, D_MODEL)
        gain_full = jnp.transpose(tiny_all[:, 8:8 + HEADS, :gshard], (1, 0, 2)).reshape(HEADS, DK)
        return proj, ht, as_w(got), conv_w_full, gain_full

    def proj_weights(after):
        s_sems, r_sems, pbufs, _ = pending_proj[0]
        got = _copies_wait(s_sems, r_sems, pbufs, after, plan, "gather_proj_wait")
        return [b.reshape(D_MODEL, D_MODEL) for b in got]

    weights = dict(norm_in=norm_in, w_in=w_in, conv_w=conv_w, conv_b=conv_b, gate_x_w=gate_x_w, gate_x_b=gate_x_b,
                   gate_a_w=gate_a_w, gate_a_b=gate_a_b, lru_lambda=lru_lambda, gn_gain=gn_gain, w_proj_a=w_proj_a,
                   w_proj_b=w_proj_b, w_out=w_out, norm_final=norm_final)
    ms = dict(norm_in=m_norm_in, w_in=m_w_in, conv_w=m_conv_w, conv_b=m_conv_b, gate_x_w=m_gate_x_w,
              gate_x_b=m_gate_x_b, gate_a_w=m_gate_a_w, gate_a_b=m_gate_a_b, lru_lambda=m_lru_lambda, gn_gain=m_gn_gain,
              w_proj_a=m_w_proj_a, w_proj_b=m_w_proj_b, w_out=m_w_out, norm_final=m_norm_final)
    vs = dict(norm_in=v_norm_in, w_in=v_w_in, conv_w=v_conv_w, conv_b=v_conv_b, gate_x_w=v_gate_x_w,
              gate_x_b=v_gate_x_b, gate_a_w=v_gate_a_w, gate_a_b=v_gate_a_b, lru_lambda=v_lru_lambda, gn_gain=v_gn_gain,
              w_proj_a=v_w_proj_a, w_proj_b=v_w_proj_b, w_out=v_w_out, norm_final=v_norm_final)
    names = list(weights)
    grads, delta, new_m, new_v = {}, {}, {}, {}

    def update_big(keys, g, half, prev, name, deps=()):
        two = lambda a: a.reshape(a.shape[1], a.shape[2])
        res = _adamw_halves([two(weights[k]) for k in keys], g, [two(ms[k]) for k in keys], [two(vs[k]) for k in keys],
                            half, prev, name, deps)
        for k, (gk, d, mn, vn) in zip(keys, res):
            shp = weights[k].shape
            grads[k], delta[k], new_m[k], new_v[k] = gk.reshape(shp), d.reshape(shp), mn.reshape(shp), vn.reshape(shp)
        return res

    def proj_done(g_proj):
        g4 = g_proj.reshape(2, 3, D_MODEL // (2 * N_CHIPS), D_MODEL)
        return update_big(("w_proj_a", "w_proj_b", "w_out"), g4, None, None, "adamw_proj")[-1][1]

    def w_in_done(g_in, own, prev, deps):
        g4 = g_in.reshape(2, 1, D_MODEL // 2, 2 * D_MODEL)
        return update_big(("w_in",), g4, ci if own else 1 - ci, None if prev is None else [prev],
                          "adamw_w_in_own" if own else "adamw_w_in_other", deps)[0]

    reduce = _GradReduce(proj_done)
    grad_x, dgin = _local_grads(
        x.reshape(T, D_MODEL), loss_target.reshape(T, D_MODEL), B, S, norm_in, in_proj, conv_b,
        gate_x_w[0], gate_x_b, gate_a_w[0], gate_a_b, lru_lambda, proj_weights,
        norm_final.reshape(1, D_MODEL), reduce)

    small_sum, g_norm_in = reduce.finish(dgin.reshape(SUBLANES, LANES), w_in_done)
    loss = small_sum[small_sum.shape[0] - SUBLANES, 0]

    gsm = _unpack_small(small_sum)
    gsm["norm_in"] = g_norm_in
    gsm["conv_w"] = lax.dynamic_slice_in_dim(gsm["conv_w"], chip * cshard, cshard, axis=1)
    gsm["gn_gain"] = lax.dynamic_slice_in_dim(gsm["gn_gain"], chip * gshard, gshard, axis=1)
    smalls = [k for k in names if k not in delta]

    def view(a):
        return a.reshape(1, -1) if a.ndim == 1 else (a.reshape(a.shape[1:]) if a.ndim > 2 else a)

    ds, mns, vns = _adamw_small([view(weights[k]) for k in smalls], [gsm[k].reshape(view(weights[k]).shape) for k in smalls],
                                [view(ms[k]) for k in smalls], [view(vs[k]) for k in smalls], "adamw_small")
    for k, d, mn, vn in zip(smalls, ds, mns, vns):
        shp = weights[k].shape
        grads[k], delta[k], new_m[k], new_v[k] = gsm[k].reshape(shp), d.reshape(shp), mn.reshape(shp), vn.reshape(shp)

    return (loss, grad_x.reshape(B, S, D_MODEL), *[grads[k] for k in names], *[delta[k] for k in names],
            *[new_m[k] for k in names], *[new_v[k] for k in names])
```

```python
import jax
import jax.numpy as jnp
from jax import lax
from jax.experimental import pallas as pl
from jax.experimental.pallas import tpu as pltpu

F32 = jnp.float32
_MXU = jnp.bfloat16

D_MODEL = 1024
N_GROUPS = 8
HEADS = 4
DK = 256
CHUNK = 128
CONV = 4
LRU_BLOCKS = 16
LRU_BW = 64
LRU_C = 8.0
ROPE_THETA = 10000.0
EPS = 1e-6
CW = 256
N_CT = D_MODEL // CW
N_CHIPS = 4
MESH = pl.DeviceIdType.MESH

ADAM_LR = 0.001
ADAM_B1 = 0.9
ADAM_B2 = 0.999
ADAM_EPS = 1e-08
ADAM_WD = 0.01
ADAM_STEP = 10

VMEM_LIMIT = 56 * 1024 * 1024

FIRST_PROJ_TILE = 1024
MORE_PROJ_TILE = 2048
SCAN_TILE = 1024
MID_TILE = 256
DX_TILE = 512
DW_COLS = 1024
RET_CHUNKS = 2
SUM_ROWS = 256
ADAMW_ROWS = 256
JOIN_PIECES = 8


def _c(v):
    return v.astype(_MXU)


def _dot(a, b):
    return lax.dot_general(a, b, (((1,), (0,)), ((), ())), preferred_element_type=F32)


def _dot_nt(a, b):
    return lax.dot_general(a, b, (((1,), (1,)), ((), ())), preferred_element_type=F32)


def _dot_tn(a, b):
    return lax.dot_general(a, b, (((0,), (0,)), ((), ())), preferred_element_type=F32)


def _sigmoid(z):
    return 0.5 * jnp.tanh(0.5 * z) + 0.5


ANY_SPEC = pl.BlockSpec(memory_space=pl.ANY)


def _after(body, n_in, deps):
    n_deps = len(deps)

    def wrapped(*refs):
        return body(*refs[:n_in], *refs[n_in + n_deps:])

    return wrapped


def _params(sem=None):
    if sem is None:
        return pltpu.CompilerParams(vmem_limit_bytes=VMEM_LIMIT)
    return pltpu.CompilerParams(vmem_limit_bytes=VMEM_LIMIT, dimension_semantics=sem)


def _inproj_first(x2d, g_in, w_all, chips, name, deps=()):
    T = x2d.shape[0]
    tm = min(FIRST_PROJ_TILE, T)
    n_i = T // tm

    def body(s_ref, *refs):
        x_ref, g_ref, w_ref = refs[:3]
        proj_ref, hb_ref, ht_ref, h_all = refs[-4:]
        i = pl.program_id(1)
        rows = pl.ds(pl.multiple_of(i * tm, tm), tm)

        @pl.when(pl.program_id(0) == 0)
        def _():
            x = x_ref[...]
            r = lax.rsqrt(jnp.mean(x * x, axis=-1, keepdims=True) + EPS)
            h = x * r * g_ref[...]
            hb = h.astype(h_all.dtype)
            h_all[rows, :] = hb
            hb_ref[...] = hb
            ht_ref[...] = h.T.astype(ht_ref.dtype)

        proj_ref[...] = _dot(h_all[rows, :], w_ref[0])

    first = lambda j, i: jnp.where(j == 0, i, n_i - 1)
    return pl.pallas_call(
        body,
        name=name,
        grid_spec=pltpu.PrefetchScalarGridSpec(
            num_scalar_prefetch=1,
            grid=(2 * chips.shape[0], n_i),
            in_specs=[
                pl.BlockSpec((tm, D_MODEL), lambda j, i, s: (first(j, i), 0)),
                pl.BlockSpec((1, D_MODEL), lambda j, i, s: (0, 0)),
                pl.BlockSpec((1, D_MODEL, D_MODEL), lambda j, i, s: (s[j // 2], 0, j % 2)),
            ] + [ANY_SPEC] * len(deps),
            out_specs=[
                pl.BlockSpec((tm, D_MODEL), lambda j, i, s: (i, 2 * s[j // 2] + j % 2)),
                pl.BlockSpec((tm, D_MODEL), lambda j, i, s: (first(j, i), 0)),
                pl.BlockSpec((D_MODEL, tm), lambda j, i, s: (0, first(j, i))),
            ],
            scratch_shapes=[pltpu.VMEM((T, D_MODEL), _MXU)],
        ),
        out_shape=[
            jax.ShapeDtypeStruct((T, N_GROUPS * D_MODEL), F32),
            jax.ShapeDtypeStruct((T, D_MODEL), _MXU),
            jax.ShapeDtypeStruct((D_MODEL, T), _MXU),
        ],
        compiler_params=_params(("arbitrary", "arbitrary")),
    )(chips, x2d, g_in, w_all, *deps)


def _inproj_more(hb, w_all, chips, proj, name):
    T = hb.shape[0]
    tm = min(MORE_PROJ_TILE, T)

    def body(s_ref, hb_hbm, w_ref, prev_ref, proj_ref, h_all, sem):
        @pl.when((pl.program_id(0) == 0) & (pl.program_id(1) == 0))
        def _():
            cp = pltpu.make_async_copy(hb_hbm, h_all, sem)
            cp.start()
            cp.wait()

        rows = pl.ds(pl.multiple_of(pl.program_id(1) * tm, tm), tm)
        proj_ref[...] = _dot(h_all[rows, :], w_ref[0])

    return pl.pallas_call(
        body,
        name=name,
        grid_spec=pltpu.PrefetchScalarGridSpec(
            num_scalar_prefetch=1,
            grid=(2 * chips.shape[0], T // tm),
            in_specs=[
                ANY_SPEC,
                pl.BlockSpec((1, D_MODEL, D_MODEL), lambda j, i, s: (s[j // 2], 0, j % 2)),
                ANY_SPEC,
            ],
            out_specs=pl.BlockSpec((tm, D_MODEL), lambda j, i, s: (i, 2 * s[j // 2] + j % 2)),
            scratch_shapes=[pltpu.VMEM((T, D_MODEL), hb.dtype), pltpu.SemaphoreType.DMA],
        ),
        out_shape=jax.ShapeDtypeStruct(proj.shape, F32),
        input_output_aliases={3: 0},
        compiler_params=_params(("arbitrary", "arbitrary")),
    )(chips, hb, w_all, proj)


def _scan_fwd(a, u):
    n = a.shape[0]
    row = lax.broadcasted_iota(jnp.int32, a.shape, 0)
    s = 1
    while s < n:
        m = row >= s
        u = u + a * jnp.where(m, pltpu.roll(u, s, 0), 0.0)
        a = a * jnp.where(m, pltpu.roll(a, s, 0), 1.0)
        s *= 2
    return a, u


def _scan_bwd(b, g):
    n = b.shape[0]
    row = lax.broadcasted_iota(jnp.int32, b.shape, 0)
    s = 1
    while s < n:
        m = row < n - s
        g = g + b * jnp.where(m, pltpu.roll(g, n - s, 0), 0.0)
        b = b * jnp.where(m, pltpu.roll(b, n - s, 0), 1.0)
        s *= 2
    return b, g


LANES = 128
SUBLANES = 8


def _scan_scratch(tc):
    by_lanes = pltpu.VMEM((CW // LANES, tc, LANES), F32)
    return [by_lanes, by_lanes, pltpu.VMEM((tc // SUBLANES, CW), F32), pltpu.VMEM((tc, CW), F32)]


def _scan_tile(a, u, edge, la_ref, lh_ref, c_ref, dst_ref, reverse):
    n, w = a.shape
    groups = n // SUBLANES
    a3 = a.reshape(groups, SUBLANES, w)
    u3 = u.reshape(groups, SUBLANES, w)
    row = lax.broadcasted_iota(jnp.int32, a3.shape, 1)
    for s in (1, 2, 4):
        m = (row < SUBLANES - s) if reverse else (row >= s)
        shift = SUBLANES - s if reverse else s
        u3 = u3 + a3 * jnp.where(m, pltpu.roll(u3, shift, 1), 0.0)
        a3 = a3 * jnp.where(m, pltpu.roll(a3, shift, 1), 1.0)
    al = a3.reshape(n, w)
    hl = u3.reshape(n, w)
    blocks = w // LANES
    for q in range(blocks):
        la_ref[q] = al[:, q * LANES:(q + 1) * LANES]
        lh_ref[q] = hl[:, q * LANES:(q + 1) * LANES]
    ends = pl.ds(0 if reverse else SUBLANES - 1, groups, stride=SUBLANES)
    end_a = jnp.concatenate([la_ref.at[q][ends, :] for q in range(blocks)], axis=-1)
    end_h = jnp.concatenate([lh_ref.at[q][ends, :] for q in range(blocks)], axis=-1)
    prod, part = (_scan_bwd if reverse else _scan_fwd)(end_a, end_h)
    total = part + prod * edge
    g_row = lax.broadcasted_iota(jnp.int32, total.shape, 0)
    if reverse:
        c_ref[...] = jnp.where(g_row == groups - 1, edge, pltpu.roll(total, groups - 1, 0))
    else:
        c_ref[...] = jnp.where(g_row == 0, edge, pltpu.roll(total, 1, 0))
    for g in range(groups):
        rows = slice(g * SUBLANES, (g + 1) * SUBLANES)
        for q in range(blocks):
            cols = slice(q * LANES, (q + 1) * LANES)
            dst_ref[rows, cols] = lh_ref[q, rows, :] + la_ref[q, rows, :] * c_ref[g:g + 1, cols]


def _softplus_neg(lam):
    z = -lam
    return jnp.maximum(z, 0.0) + jnp.log1p(jnp.exp(-jnp.abs(z)))


def _lru_gates(xc, wx_ref, wa_ref, bx_ref, ba_ref, lam_ref):
    xcb = _c(xc)
    i_t = _sigmoid(_dot(xcb, wx_ref[0]) + bx_ref[...])
    r_t = _sigmoid(_dot(xcb, wa_ref[0]) + ba_ref[...])
    sp = _softplus_neg(lam_ref[...])
    log_a = (-LRU_C) * r_t * sp
    a = jnp.exp(log_a)
    mult = jnp.sqrt(1.0 - a * a)
    return xcb, i_t, r_t, sp, a, mult


def _conv_from_ext(ext_ref, xa, cw_ref, cb_ref, tc):
    return (cb_ref[...] + cw_ref[3:4, :] * xa + cw_ref[2:3, :] * ext_ref[7:7 + tc, :]
            + cw_ref[1:2, :] * ext_ref[6:6 + tc, :] + cw_ref[0:1, :] * ext_ref[5:5 + tc, :])


def _lru_fwd(proj, conv_w, conv_b, wx_bd, wa_bd, bx, ba, lam, B, S):
    T = B * S
    tc = min(SCAN_TILE, S)
    nt = S // tc
    h8 = tc // 8

    def body(xa_ref, halo_ref, ga_ref, cw_ref, cb_ref, wx_ref, wa_ref, bx_ref, ba_ref, lam_ref,
             h_ref, ya_ref, ext_ref, carry_ref, la_ref, lh_ref, c_ref):
        t = pl.program_id(2)

        @pl.when(t == 0)
        def _():
            carry_ref[...] = jnp.zeros_like(carry_ref)

        xa = xa_ref[...]
        ext_ref[0:8, :] = jnp.where(t == 0, 0.0, halo_ref[...])
        ext_ref[8:8 + tc, :] = xa
        xc = _conv_from_ext(ext_ref, xa, cw_ref, cb_ref, tc)
        _, i_t, _, _, a, mult = _lru_gates(xc, wx_ref, wa_ref, bx_ref, ba_ref, lam_ref)
        u = mult * (i_t * xc)
        _scan_tile(a, u, carry_ref[7:8, :], la_ref, lh_ref, c_ref, h_ref, False)
        h = h_ref[...]
        carry_ref[...] = h[tc - 8:tc, :]
        ga = ga_ref[...]
        ya_ref[...] = (ga * _sigmoid(ga) * h).astype(ya_ref.dtype)

    row = lambda b, t: b * nt + t
    vec = pl.BlockSpec((1, CW), lambda b, c, t: (0, c))
    mat = pl.BlockSpec((1, CW, CW), lambda b, c, t: (c, 0, 0))
    return pl.pallas_call(
        body,
        name="lru_fwd",
        grid=(B, N_CT, nt),
        in_specs=[
            pl.BlockSpec((tc, CW), lambda b, c, t: (row(b, t), c)),
            pl.BlockSpec((8, CW), lambda b, c, t: (jnp.maximum(row(b, t) * h8 - 1, 0), c)),
            pl.BlockSpec((tc, CW), lambda b, c, t: (row(b, t), N_CT + c)),
            pl.BlockSpec((CONV, CW), lambda b, c, t: (0, c)),
            vec, mat, mat, vec, vec, vec,
        ],
        out_specs=[
            pl.BlockSpec((tc, CW), lambda b, c, t: (row(b, t), c)),
            pl.BlockSpec((tc, CW), lambda b, c, t: (row(b, t), c)),
        ],
        out_shape=[
            jax.ShapeDtypeStruct((T, D_MODEL), F32),
            jax.ShapeDtypeStruct((T, D_MODEL), _MXU),
        ],
        scratch_shapes=[pltpu.VMEM((tc + 8, CW), F32), pltpu.VMEM((8, CW), F32)] + _scan_scratch(tc)[:3],
        compiler_params=_params(("parallel", "parallel", "arbitrary")),
    )(proj, proj, proj, conv_w, conv_b, wx_bd, wa_bd, bx, ba, lam)


def _lru_bwd(dya, proj, hlru, conv_w, conv_b, wx_bd, wa_bd, bx, ba, lam, B, S, deps=()):
    T = B * S
    tc = min(SCAN_TILE, S)
    nt = S // tc
    h8 = tc // 8

    def body(dya_ref, xa_ref, xhalo_ref, ga_ref, h_ref, hhalo_ref, cw_ref, cb_ref, wx_ref, wa_ref, bx_ref, ba_ref,
             lam_ref, dxa_ref, dga_ref, dcw_ref, dcb_ref, dwx_ref, dwa_ref, dbx_ref, dba_ref, dlam_ref,
             ext_ref, ext2_ref, carry_ref, dhalo_ref, la_ref, lh_ref, c_ref, dh_ref):
        b = pl.program_id(1)
        t = pl.program_id(2)
        tt = nt - 1 - t

        @pl.when(t == 0)
        def _():
            carry_ref[...] = jnp.zeros_like(carry_ref)
            dhalo_ref[...] = jnp.zeros_like(dhalo_ref)

        @pl.when((t == 0) & (b == 0))
        def _():
            for r in (dcw_ref, dcb_ref, dwx_ref, dwa_ref, dbx_ref, dba_ref, dlam_ref):
                r[...] = jnp.zeros_like(r)

        xa = xa_ref[...]
        ext_ref[0:8, :] = jnp.where(tt == 0, 0.0, xhalo_ref[...])
        ext_ref[8:8 + tc, :] = xa
        xc = _conv_from_ext(ext_ref, xa, cw_ref, cb_ref, tc)
        xcb, i_t, r_t, sp, a, mult = _lru_gates(xc, wx_ref, wa_ref, bx_ref, ba_ref, lam_ref)

        h = h_ref[...]
        ga = ga_ref[...]
        dya_t = dya_ref[...]
        sg = _sigmoid(ga)
        dga_ref[...] = (dya_t * h * (sg * (1.0 + ga * (1.0 - sg)))).astype(dga_ref.dtype)
        dlru = dya_t * (ga * sg)

        row = lax.broadcasted_iota(jnp.int32, a.shape, 0)
        coef = jnp.where(row == tc - 1, 1.0, pltpu.roll(a, tc - 1, 0))
        _scan_tile(coef, dlru, carry_ref[0:1, :], la_ref, lh_ref, c_ref, dh_ref, True)
        dh = dh_ref[...]
        ext2_ref[0:tc, :] = a * dh
        carry_ref[...] = ext2_ref[0:8, :]

        ext2_ref[0:8, :] = jnp.where(tt == 0, 0.0, hhalo_ref[...])
        ext2_ref[8:8 + tc, :] = h
        hprev = ext2_ref[7:7 + tc, :]

        da = dh * hprev
        ix = i_t * xc
        dmult = dh * ix
        di = dh * mult * xc
        dxc = dh * mult * i_t
        dlog_a = da * a - dmult * (a * a) / mult
        dr = dlog_a * ((-LRU_C) * sp)
        dlam_ref[...] += jnp.sum(dlog_a * r_t, axis=0, keepdims=True) * (LRU_C * _sigmoid(-lam_ref[...]))
        dza = dr * r_t * (1.0 - r_t)
        dzx = di * i_t * (1.0 - i_t)
        dzab = _c(dza)
        dzxb = _c(dzx)
        dxc = dxc + _dot_nt(dzxb, wx_ref[0]) + _dot_nt(dzab, wa_ref[0])
        dwx_ref[0] += _dot_tn(xcb, dzxb)
        dwa_ref[0] += _dot_tn(xcb, dzab)
        dbx_ref[...] += jnp.sum(dzx, axis=0, keepdims=True)
        dba_ref[...] += jnp.sum(dza, axis=0, keepdims=True)

        dcb_ref[...] += jnp.sum(dxc, axis=0, keepdims=True)
        dcw_ref[3:4, :] += jnp.sum(dxc * xa, axis=0, keepdims=True)
        dcw_ref[2:3, :] += jnp.sum(dxc * ext_ref[7:7 + tc, :], axis=0, keepdims=True)
        dcw_ref[1:2, :] += jnp.sum(dxc * ext_ref[6:6 + tc, :], axis=0, keepdims=True)
        dcw_ref[0:1, :] += jnp.sum(dxc * ext_ref[5:5 + tc, :], axis=0, keepdims=True)
        ext2_ref[0:tc, :] = dxc
        ext2_ref[tc:tc + 8, :] = dhalo_ref[...]
        dxa = (cw_ref[3:4, :] * dxc + cw_ref[2:3, :] * ext2_ref[1:1 + tc, :]
               + cw_ref[1:2, :] * ext2_ref[2:2 + tc, :] + cw_ref[0:1, :] * ext2_ref[3:3 + tc, :])
        dxa_ref[...] = dxa.astype(dxa_ref.dtype)
        dhalo_ref[...] = ext2_ref[0:8, :]

    row_of = lambda b, t: b * nt + (nt - 1 - t)
    tile = lambda off: pl.BlockSpec((tc, CW), lambda c, b, t: (row_of(b, t), off + c))
    halo = pl.BlockSpec((8, CW), lambda c, b, t: (jnp.maximum(row_of(b, t) * h8 - 1, 0), c))
    vec = pl.BlockSpec((1, CW), lambda c, b, t: (0, c))
    mat = pl.BlockSpec((1, CW, CW), lambda c, b, t: (c, 0, 0))
    cwspec = pl.BlockSpec((CONV, CW), lambda c, b, t: (0, c))
    return pl.pallas_call(
        _after(body, 13, deps),
        name="lru_bwd",
        grid=(N_CT, B, nt),
        in_specs=[tile(0), tile(0), halo, tile(N_CT), tile(0), halo, cwspec, vec, mat, mat, vec, vec, vec]
        + [ANY_SPEC] * len(deps),
        out_specs=[tile(0), tile(0), cwspec, vec, mat, mat, vec, vec, vec],
        out_shape=[
            jax.ShapeDtypeStruct((T, D_MODEL), _MXU),
            jax.ShapeDtypeStruct((T, D_MODEL), _MXU),
            jax.ShapeDtypeStruct((CONV, D_MODEL), F32),
            jax.ShapeDtypeStruct((1, D_MODEL), F32),
            jax.ShapeDtypeStruct((N_CT, CW, CW), F32),
            jax.ShapeDtypeStruct((N_CT, CW, CW), F32),
            jax.ShapeDtypeStruct((1, D_MODEL), F32),
            jax.ShapeDtypeStruct((1, D_MODEL), F32),
            jax.ShapeDtypeStruct((1, D_MODEL), F32),
        ],
        scratch_shapes=[pltpu.VMEM((tc + 8, CW), F32), pltpu.VMEM((tc + 8, CW), F32),
                        pltpu.VMEM((8, CW), F32), pltpu.VMEM((8, CW), F32)] + _scan_scratch(tc),
        compiler_params=_params(("parallel", "arbitrary", "arbitrary")),
    )(dya, proj, proj, proj, hlru, hlru, conv_w, conv_b, wx_bd, wa_bd, bx, ba, lam, *deps)


def _retention_tables(S):
    half = DK // 2
    freqs = ROPE_THETA ** (-jnp.arange(half, dtype=F32) / half)
    ang = jnp.arange(S, dtype=F32)[:, None] * freqs[None, :]
    log_g = jnp.log1p(-(2.0 ** (-5.0 - jnp.arange(HEADS, dtype=F32))))
    idx = jnp.arange(CHUNK, dtype=F32)
    diff = idx[:, None] - idx[None, :]
    inner = jnp.where(diff >= 0, jnp.exp(jnp.maximum(diff, 0.0)[None] * log_g[:, None, None]), 0.0)
    cross = jnp.exp((idx[None, :] + 1.0) * log_g[:, None])[:, :, None]
    state = jnp.exp((CHUNK - 1.0 - idx[None, :]) * log_g[:, None])[:, :, None]
    gam = jnp.broadcast_to(jnp.exp(CHUNK * log_g)[:, None, None], (HEADS, 1, DK))
    return jnp.cos(ang), jnp.sin(ang), inner, cross, state, gam


def _rot(x, cos, sin):
    half = DK // 2
    x1, x2 = x[:, :half], x[:, half:]
    return jnp.concatenate([x1 * cos - x2 * sin, x1 * sin + x2 * cos], axis=-1)


def _rot_t(y, cos, sin):
    half = DK // 2
    y1, y2 = y[:, :half], y[:, half:]
    return jnp.concatenate([y1 * cos + y2 * sin, y2 * cos - y1 * sin], axis=-1)


def _groupnorm(o):
    mu = jnp.mean(o, axis=-1, keepdims=True)
    oc = o - mu
    rs = lax.rsqrt(jnp.mean(oc * oc, axis=-1, keepdims=True) + EPS)
    return oc * rs, rs


def _ret_specs(B, chunk_of):
    rows = RET_CHUNKS * CHUNK
    qkv = lambda g: pl.BlockSpec((B, rows, D_MODEL), lambda c: (0, chunk_of(c), g))
    act = pl.BlockSpec((B, rows, D_MODEL), lambda c: (0, chunk_of(c), 0))
    rope = pl.BlockSpec((rows, DK // 2), lambda c: (chunk_of(c), 0))
    dmat = pl.BlockSpec((HEADS, CHUNK, CHUNK), lambda c: (0, 0, 0))
    dvec = pl.BlockSpec((HEADS, CHUNK, 1), lambda c: (0, 0, 0))
    hrow = pl.BlockSpec((HEADS, 1, DK), lambda c: (0, 0, 0))
    rst = pl.BlockSpec((RET_CHUNKS, B, HEADS, DK, DK), lambda c: (chunk_of(c), 0, 0, 0, 0))
    return qkv, act, rope, dmat, dvec, hrow, rst


def _ret_fwd(proj, tables, gain3, B, S):
    T = B * S
    nc = S // CHUNK
    cos, sin, dmat_t, cd_t, sd_t, gam_t = tables

    def body(q_ref, k_ref, v_ref, gb_ref, cos_ref, sin_ref, dm_ref, cd_ref, sd_ref, gam_ref, gain_ref,
             o_ref, yb_ref, rs_ref, state_ref):
        @pl.when(pl.program_id(0) == 0)
        def _():
            state_ref[...] = jnp.zeros_like(state_ref)

        for cc, b, h in [(cc, b, h) for cc in range(RET_CHUNKS) for b in range(B) for h in range(HEADS)]:
            rows = slice(cc * CHUNK, (cc + 1) * CHUNK)
            cos_t, sin_t = cos_ref[rows, :], sin_ref[rows, :]
            cols = slice(h * DK, (h + 1) * DK)
            qb = _c(_rot(q_ref[b, rows, cols], cos_t, sin_t))
            kb = _c(_rot(k_ref[b, rows, cols], cos_t, sin_t) * (DK ** -0.5))
            v = v_ref[b, rows, cols]
            state = state_ref[b, h]
            sb = _c(state)
            rs_ref[cc, b, h] = sb
            scores = _dot_nt(qb, kb) * dm_ref[h]
            o = _dot(_c(scores), _c(v)) + _dot(qb, sb) * cd_ref[h]
            state_ref[b, h] = gam_ref[h] * state + _dot_tn(kb, _c(v * sd_ref[h]))
            o_ref[b, rows, cols] = o
            n, _ = _groupnorm(o)
            gb = gb_ref[b, rows, cols]
            yb_ref[b, rows, cols] = (gb * _sigmoid(gb) * (n * gain_ref[h])).astype(yb_ref.dtype)

    qkv, act, rope, dmat, dvec, hrow, rst = _ret_specs(B, lambda c: c)
    proj3 = proj.reshape(B, S, proj.shape[1])
    o_pre, yb, states = pl.pallas_call(
        body,
        name="ret_fwd",
        grid=(nc // RET_CHUNKS,),
        in_specs=[qkv(2), qkv(3), qkv(4), qkv(5), rope, rope, dmat, dvec, dvec, hrow, hrow],
        out_specs=[act, act, rst],
        out_shape=[
            jax.ShapeDtypeStruct((B, S, D_MODEL), F32),
            jax.ShapeDtypeStruct((B, S, D_MODEL), _MXU),
            jax.ShapeDtypeStruct((nc, B, HEADS, DK, DK), _MXU),
        ],
        scratch_shapes=[pltpu.VMEM((B, HEADS, DK, DK), F32)],
        compiler_params=_params(("arbitrary",)),
    )(proj3, proj3, proj3, proj3, cos, sin, dmat_t, cd_t, sd_t, gam_t, gain3)
    return o_pre.reshape(T, D_MODEL), yb.reshape(T, D_MODEL), states


def _ret_bwd(dyb, o_pre, proj, states, tables, gain3, B, S, deps=()):
    T = B * S
    nc = S // CHUNK
    cos, sin, dmat_t, cd_t, sd_t, gam_t = tables

    def body(dyb_ref, o_ref, q_ref, k_ref, v_ref, gb_ref, rs_ref, cos_ref, sin_ref, dm_ref, cd_ref, sd_ref, gam_ref,
             gain_ref, dr_ref, dgain_ref, dstate_ref):
        @pl.when(pl.program_id(0) == 0)
        def _():
            dstate_ref[...] = jnp.zeros_like(dstate_ref)
            dgain_ref[...] = jnp.zeros_like(dgain_ref)

        for cc, b, h in [(cc, b, h) for cc in reversed(range(RET_CHUNKS)) for b in range(B) for h in range(HEADS)]:
            rows = slice(cc * CHUNK, (cc + 1) * CHUNK)
            cos_t, sin_t = cos_ref[rows, :], sin_ref[rows, :]
            cols = slice(h * DK, (h + 1) * DK)
            gain = gain_ref[h]
            n, rs = _groupnorm(o_ref[b, rows, cols])
            gb = gb_ref[b, rows, cols]
            sg = _sigmoid(gb)
            dy = dyb_ref[b, rows, cols]
            part = lambda g: slice(g * D_MODEL + h * DK, g * D_MODEL + (h + 1) * DK)
            dr_ref[b, rows, part(3)] = (dy * (n * gain) * (sg * (1.0 + gb * (1.0 - sg)))).astype(dr_ref.dtype)
            dgn = dy * (gb * sg)
            dgain_ref[h] += jnp.sum(dgn * n, axis=0, keepdims=True)
            dn = dgn * gain
            do = rs * (dn - jnp.mean(dn, axis=-1, keepdims=True) - n * jnp.mean(dn * n, axis=-1, keepdims=True))

            qb = _c(_rot(q_ref[b, rows, cols], cos_t, sin_t))
            kb = _c(_rot(k_ref[b, rows, cols], cos_t, sin_t) * (DK ** -0.5))
            v = v_ref[b, rows, cols]
            vb = _c(v)
            vsb = _c(v * sd_ref[h])
            dob = _c(do)
            docb = _c(do * cd_ref[h])
            dmat = dm_ref[h]
            dstate = dstate_ref[b, h]
            dsb = _c(dstate)
            pb = _c(_dot_nt(qb, kb) * dmat)
            dsc = _c(_dot_nt(dob, vb) * dmat)
            dq = _dot(dsc, kb) + _dot_nt(docb, rs_ref[cc, b, h])
            dk = _dot_tn(dsc, qb) + _dot_nt(vsb, dsb)
            dv = _dot_tn(pb, dob) + _dot(kb, dsb) * sd_ref[h]
            dstate_ref[b, h] = gam_ref[h] * dstate + _dot_tn(qb, docb)
            dr_ref[b, rows, part(0)] = _rot_t(dq, cos_t, sin_t).astype(dr_ref.dtype)
            dr_ref[b, rows, part(1)] = (_rot_t(dk, cos_t, sin_t) * (DK ** -0.5)).astype(dr_ref.dtype)
            dr_ref[b, rows, part(2)] = dv.astype(dr_ref.dtype)

    n_steps = nc // RET_CHUNKS
    qkv, act, rope, dmat, dvec, hrow, rst = _ret_specs(B, lambda c: n_steps - 1 - c)
    wide = pl.BlockSpec((B, RET_CHUNKS * CHUNK, 4 * D_MODEL), lambda c: (0, n_steps - 1 - c, 0))
    proj3 = proj.reshape(B, S, proj.shape[1])
    dr, dgain = pl.pallas_call(
        _after(body, 14, deps),
        name="ret_bwd",
        grid=(n_steps,),
        in_specs=[act, act, qkv(2), qkv(3), qkv(4), qkv(5), rst, rope, rope, dmat, dvec, dvec, hrow, hrow]
        + [ANY_SPEC] * len(deps),
        out_specs=[wide, hrow],
        out_shape=[jax.ShapeDtypeStruct((B, S, 4 * D_MODEL), _MXU), jax.ShapeDtypeStruct((HEADS, 1, DK), F32)],
        scratch_shapes=[pltpu.VMEM((B, HEADS, DK, DK), F32)],
        compiler_params=_params(("arbitrary",)),
    )(dyb.reshape(B, S, D_MODEL), o_pre.reshape(B, S, D_MODEL), proj3, proj3, proj3, proj3, states, cos, sin, dmat_t,
      cd_t, sd_t, gam_t, gain3, *deps)
    return dr.reshape(T, 4 * D_MODEL), dgain


def _mid(ya, yb, proj, x2d, tgt2d, wpa, wpb, wout, g_fin):
    T = x2d.shape[0]
    tm = min(MID_TILE, T)
    n_steps = T // tm
    rows = D_MODEL // (2 * N_CHIPS)

    def body(ya_ref, yb_ref, ma_ref, mb_ref, x_ref, t_ref, gf_ref, wpa_hbm, wpb_hbm, wout_hbm,
             loss_ref, dx2_ref, dya_ref, dyb_ref, dm_ref, dgf_ref, gw_hbm, w_ref, acc_ref, sem):
        i = pl.program_id(0)

        @pl.when(i == 0)
        def _():
            loads = [pltpu.make_async_copy(src, w_ref.at[k], sem.at[k]) for k, src in enumerate((wpa_hbm, wpb_hbm, wout_hbm))]
            for cp in loads:
                cp.start()
            for cp in loads:
                cp.wait()
            acc_ref[...] = jnp.zeros_like(acc_ref)
            loss_ref[...] = jnp.zeros_like(loss_ref)
            dgf_ref[...] = jnp.zeros_like(dgf_ref)

        ya_t, yb_t = ya_ref[...], yb_ref[...]
        out_a = _dot(ya_t, w_ref[0])
        out_b = _dot(yb_t, w_ref[1])
        sa = _sigmoid(ma_ref[...])
        sb = _sigmoid(mb_ref[...])
        mgb = _c(sa * out_a + sb * out_b)
        x2 = x_ref[...] + _dot(mgb, w_ref[2])
        r2 = lax.rsqrt(jnp.mean(x2 * x2, axis=-1, keepdims=True) + EPS)
        nx = x2 * r2
        gf = gf_ref[...]
        err = nx * gf - t_ref[...]
        loss_ref[...] += 0.5 * jnp.sum(jnp.mean(err * err, axis=-1, keepdims=True), axis=0, keepdims=True)
        dy = err * (1.0 / D_MODEL)
        dgf_ref[...] += jnp.sum(dy * nx, axis=0, keepdims=True)
        dyg = dy * gf
        dx2 = r2 * (dyg - nx * jnp.mean(dyg * nx, axis=-1, keepdims=True))
        dx2_ref[...] = dx2
        dx2b = _c(dx2)
        dmg = _dot_nt(dx2b, w_ref[2])
        acc_ref[2] += _dot_tn(mgb, dx2b)
        dm_ref[:, :D_MODEL] = (dmg * out_a * sa * (1.0 - sa)).astype(dm_ref.dtype)
        dm_ref[:, D_MODEL:] = (dmg * out_b * sb * (1.0 - sb)).astype(dm_ref.dtype)
        dab = _c(dmg * sa)
        dbb = _c(dmg * sb)
        dya_ref[...] = _dot_nt(dab, w_ref[0])
        dyb_ref[...] = _dot_nt(dbb, w_ref[1])
        acc_ref[0] += _dot_tn(ya_t, dab)
        acc_ref[1] += _dot_tn(yb_t, dbb)

        @pl.when(i == n_steps - 1)
        def _():
            copies = [pltpu.make_async_copy(acc_ref.at[k, pl.ds((2 * p + hf) * rows, rows), :], gw_hbm.at[p, hf, k],
                                            sem.at[(k * N_CHIPS + p) * 2 + hf])
                      for k in range(3) for p in range(N_CHIPS) for hf in range(2)]
            for cp in copies:
                cp.start()
            for cp in copies:
                cp.wait()

    tile = lambda j: pl.BlockSpec((tm, D_MODEL), lambda i: (i, j))
    one = pl.BlockSpec((1, D_MODEL), lambda i: (0, 0))
    anyspec = pl.BlockSpec(memory_space=pl.ANY)
    return pl.pallas_call(
        body,
        name="mid",
        grid=(n_steps,),
        in_specs=[tile(0), tile(0), tile(6), tile(7), tile(0), tile(0), one, anyspec, anyspec, anyspec],
        out_specs=[pl.BlockSpec((1, 1), lambda i: (0, 0)), tile(0), tile(0), tile(0),
                   pl.BlockSpec((tm, 2 * D_MODEL), lambda i: (i, 0)), one, anyspec],
        out_shape=[
            jax.ShapeDtypeStruct((1, 1), F32),
            jax.ShapeDtypeStruct((T, D_MODEL), F32),
            jax.ShapeDtypeStruct((T, D_MODEL), F32),
            jax.ShapeDtypeStruct((T, D_MODEL), F32),
            jax.ShapeDtypeStruct((T, 2 * D_MODEL), _MXU),
            jax.ShapeDtypeStruct((1, D_MODEL), F32),
            jax.ShapeDtypeStruct((N_CHIPS, 2, 3, rows, D_MODEL), F32),
        ],
        scratch_shapes=[pltpu.VMEM((3, D_MODEL, D_MODEL), _MXU), pltpu.VMEM((3, D_MODEL, D_MODEL), F32),
                        pltpu.SemaphoreType.DMA((3 * N_CHIPS * 2,))],
        compiler_params=_params(("arbitrary",)),
    )(ya, yb, proj, proj, x2d, tgt2d, g_fin, wpa, wpb, wout)


def _inproj_bwd_dx(dparts, w_all, x2d, dx2, g_in, first, count, prev, name, deps=()):
    T = x2d.shape[0]
    tm = min(DX_TILE, T)
    n_d = len(dparts)
    groups = [(a, k) for a, d in enumerate(dparts) for k in range(d.shape[1] // D_MODEL)]
    dg_start = jnp.zeros((1, D_MODEL), F32) if prev is None else prev[1]
    carried = () if prev is None else (prev[0],)

    def body(*refs):
        d_refs = refs[:n_d]
        x_ref, dx2_ref, g_ref, dg0_ref, w_hbm = refs[n_d:n_d + 5]
        dx_ref, dg_ref, w_ref, sem = refs[-4:]

        @pl.when(pl.program_id(0) == 0)
        def _():
            cp = pltpu.make_async_copy(w_hbm, w_ref, sem)
            cp.start()
            cp.wait()
            dg_ref[...] = dg0_ref[...]

        dh = jnp.zeros((tm, D_MODEL), F32)
        for j, (a, k) in enumerate(groups):
            dh = dh + _dot_nt(d_refs[a][:, k * D_MODEL:(k + 1) * D_MODEL],
                              w_ref[j // 2, :, (j % 2) * D_MODEL:(j % 2 + 1) * D_MODEL])
        x = x_ref[...]
        r = lax.rsqrt(jnp.mean(x * x, axis=-1, keepdims=True) + EPS)
        nx = x * r
        dg_ref[...] += jnp.sum(dh * nx, axis=0, keepdims=True)
        dhg = dh * g_ref[...]
        dx_ref[...] = dx2_ref[...] + r * (dhg - nx * jnp.mean(dhg * nx, axis=-1, keepdims=True))

    tile = pl.BlockSpec((tm, D_MODEL), lambda i: (first + i, 0))
    one = pl.BlockSpec((1, D_MODEL), lambda i: (0, 0))
    return pl.pallas_call(
        body,
        name=name,
        grid=(count,),
        in_specs=[pl.BlockSpec((tm, d.shape[1]), lambda i: (first + i, 0)) for d in dparts]
        + [tile, tile, one, one, ANY_SPEC] + [ANY_SPEC] * (len(carried) + len(deps)),
        out_specs=[tile, one],
        out_shape=[jax.ShapeDtypeStruct((T, D_MODEL), F32), jax.ShapeDtypeStruct((1, D_MODEL), F32)],
        input_output_aliases={n_d + 5: 0} if carried else {},
        scratch_shapes=[pltpu.VMEM(w_all.shape, w_all.dtype), pltpu.SemaphoreType.DMA],
        compiler_params=_params(("arbitrary",)),
    )(*dparts, x2d, dx2, g_in, dg_start, w_all, *carried, *deps)


def _inproj_bwd_dw(ht, dparts, name, deps=()):
    T = ht.shape[1]
    tn = DW_COLS // len(dparts)
    half = D_MODEL // 2
    per_chip = 2 * D_MODEL // tn
    n_d = len(dparts)
    tiles = [(a, t) for a, d in enumerate(dparts) for t in range(d.shape[1] // tn)]
    offs = [sum(d.shape[1] // tn for d in dparts[:a]) for a in range(n_d)]

    def body(*refs):
        ht_ref = refs[0]
        d_refs = refs[1:1 + n_d]
        out_ref = refs[-1]
        t = pl.program_id(0)

        for a in range(n_d):
            lo, hi = offs[a], offs[a] + dparts[a].shape[1] // tn

            @pl.when((t >= lo) & (t < hi))
            def _(a=a):
                g = _dot(ht_ref[...], d_refs[a][...])
                out_ref[0, 0] = g[:half]
                out_ref[0, 1] = g[half:]

    def dspec(a):
        n_a = dparts[a].shape[1] // tn
        return pl.BlockSpec((T, tn), lambda t: (0, jnp.clip(t - offs[a], 0, n_a - 1)))

    return pl.pallas_call(
        body,
        name=name,
        grid=(len(tiles),),
        in_specs=[pl.BlockSpec((D_MODEL, T), lambda t: (0, 0))] + [dspec(a) for a in range(n_d)]
        + [ANY_SPEC] * len(deps),
        out_specs=pl.BlockSpec((1, 2, half, tn), lambda t: (t // per_chip, 0, 0, t % per_chip)),
        out_shape=jax.ShapeDtypeStruct((len(tiles) // per_chip, 2, half, 2 * D_MODEL), F32),
        compiler_params=_params(("parallel",)),
    )(ht, *dparts, *deps)


def _coords():
    return lax.axis_index("x"), lax.axis_index("y"), lax.axis_index("c")


def _other_chips(x, y):
    return [(1 - x, y), (x, 1 - y), (1 - x, 1 - y)]


def _chunks(rows, n):
    size = rows // n
    return [pl.ds(q * size, size) for q in range(n)]


HBM_SPEC = pl.BlockSpec(memory_space=pltpu.HBM)
SEM_SPEC = pl.BlockSpec(memory_space=pltpu.SEMAPHORE)
DATAFLOW = pltpu.SideEffectType.DATAFLOW_SIDE_EFFECTING


def _copies_start(bufs, plan, n_copies, name, deps=()):
    n = len(bufs)
    n_deps = len(deps)

    def body(*refs):
        ins = refs[:n]
        send_sems, recv_sems = refs[n + n_deps], refs[n + n_deps + 1]
        token = refs[-1]
        for k, send, _ in plan(ins):
            if send is not None:
                src, dst, dev, pred = send
                cp = pltpu.make_async_remote_copy(src_ref=src, dst_ref=dst, send_sem=send_sems.at[k],
                                                  recv_sem=recv_sems.at[k], device_id=dev, device_id_type=MESH)
                if pred is None:
                    cp.start()
                else:
                    pl.when(pred)(cp.start)
        token[...] = jnp.zeros_like(token)

    hbm = [pltpu.with_memory_space_constraint(b, pltpu.HBM) for b in bufs]
    outs = pl.pallas_call(
        body,
        name=name,
        in_specs=[HBM_SPEC] * n + [ANY_SPEC] * n_deps,
        out_specs=(SEM_SPEC, SEM_SPEC, *([HBM_SPEC] * n), pl.BlockSpec(memory_space=pltpu.VMEM)),
        out_shape=(pltpu.SemaphoreType.DMA((n_copies,)), pltpu.SemaphoreType.DMA((n_copies,)),
                   *[pltpu.HBM(b.shape, b.dtype) for b in bufs], jax.ShapeDtypeStruct((8, 128), F32)),
        input_output_aliases={a: 2 + a for a in range(n)},
        compiler_params=pltpu.CompilerParams(has_side_effects=DATAFLOW),
    )(*hbm, *deps)
    return outs[0], outs[1], list(outs[2:2 + n]), outs[-1]


def _copies_wait(send_sems, recv_sems, bufs, after, plan, name, only=None):
    n = len(bufs)

    def body(*refs):
        ins = refs[:n]
        s_sems, r_sems = refs[n], refs[n + 1]
        for k, send, recv in plan(ins):
            if only is not None and k not in only:
                continue
            if send is not None:
                src, dst, dev, pred = send
                cp = pltpu.make_async_remote_copy(src_ref=src, dst_ref=dst, send_sem=s_sems.at[k],
                                                  recv_sem=r_sems.at[k], device_id=dev, device_id_type=MESH)
                if pred is None:
                    cp.wait_send()
                else:
                    pl.when(pred)(cp.wait_send)
            if recv is not None:
                dst, pred = recv
                cp = pltpu.make_async_remote_copy(src_ref=dst, dst_ref=dst, send_sem=s_sems.at[k],
                                                  recv_sem=r_sems.at[k], device_id=_coords(), device_id_type=MESH)
                if pred is None:
                    cp.wait_recv()
                else:
                    pl.when(pred)(cp.wait_recv)

    outs = pl.pallas_call(
        body,
        name=name,
        in_specs=[HBM_SPEC] * n + [SEM_SPEC, SEM_SPEC, pl.BlockSpec(memory_space=pl.ANY)],
        out_specs=[HBM_SPEC] * n,
        out_shape=[pltpu.HBM(b.shape, b.dtype) for b in bufs],
        input_output_aliases={a: a for a in range(n)},
        compiler_params=pltpu.CompilerParams(has_side_effects=DATAFLOW),
    )(*bufs, send_sems, recv_sems, after)
    return list(outs)


def _gather_plan(n_bufs):
    def plan(refs):
        x, y, c = _coords()
        me = 2 * x + y
        out = []
        for k, (px, py) in enumerate(_other_chips(x, y)):
            for a in range(n_bufs):
                out.append((k * n_bufs + a, (refs[a].at[me], refs[a].at[me], (px, py, c), None),
                            (refs[a].at[2 * px + py], None)))
        return out
    return plan


def _cast_into_slot(ws, name, deps=()):
    n = len(ws)
    nt = 2

    def body(s_ref, *refs):
        outs = refs[len(refs) - n:]
        for a in range(n):
            outs[a][0] = refs[a][...].astype(outs[a].dtype)

    xi, yi, _ = _coords()
    return pl.pallas_call(
        body,
        name=name,
        grid_spec=pltpu.PrefetchScalarGridSpec(
            num_scalar_prefetch=1,
            grid=(2, nt),
            in_specs=[pl.BlockSpec((1, w.shape[1] // nt, w.shape[2]), lambda hf, i, s: (hf, i, 0)) for w in ws]
            + [ANY_SPEC] * len(deps),
            out_specs=[pl.BlockSpec((1, 1, w.shape[1] // nt, w.shape[2]), lambda hf, i, s: (s[0], hf, i, 0)) for w in ws],
        ),
        out_shape=[jax.ShapeDtypeStruct((N_CHIPS,) + w.shape, _MXU) for w in ws],
        compiler_params=_params(("parallel", "parallel")),
    )((2 * xi + yi).reshape(1).astype(jnp.int32), *ws, *deps)


def _chip_gather_plan(stage, n_bufs):
    def plan(refs):
        x, y, c = _coords()
        me = 2 * x + y
        near = [(1 - x, y), (x, 1 - y)]
        slots = [2 * (1 - x) + y, 2 * x + (1 - y), 2 * (1 - x) + (1 - y)]
        sibling = (x, y, 1 - c)
        pass_to = (jnp.where(c == 0, x, 1 - x), jnp.where(c == 0, 1 - y, y), c)
        pass_slot = jnp.where(c == 0, slots[0], slots[1])
        out = []

        def move(src_slot, to, land_slot, land_core, pieces):
            for a, buf in enumerate(refs):
                for rows in _chunks(buf.shape[2], pieces[a]):
                    out.append((len(out), (buf.at[src_slot, c, rows], buf.at[src_slot, c, rows], to, None),
                                (buf.at[land_slot, land_core, rows], None)))

        if stage == "near":
            for k, chip in enumerate(near):
                move(me, (*chip, c), slots[k], c, NEAR_PIECES[:n_bufs])
        elif stage == "pass":
            move(pass_slot, pass_to, slots[2], c, PASS_PIECES[:n_bufs])
            for k in range(2):
                move(slots[k], sibling, slots[k], 1 - c, [1] * n_bufs)
        else:
            move(slots[2], sibling, slots[2], 1 - c, [1] * n_bufs)
        return out
    return plan


NEAR_PIECES = (2, 1)
PASS_PIECES = (2, 1)


def _chip_gather_copies(stage, n_bufs):
    if stage == "near":
        return 2 * sum(NEAR_PIECES[:n_bufs]), None
    if stage == "pass":
        n_pass = sum(PASS_PIECES[:n_bufs])
        return n_pass + 2 * n_bufs, set(range(n_pass))
    return n_bufs, None


def _swap_plan(n_slabs):
    def plan(refs):
        x, y, c = _coords()
        out, k = [], 0
        for i, n in enumerate(n_slabs):
            g, land = refs[2 * i], refs[2 * i + 1]
            for p in range(n):
                out.append((k, (g.at[p, 1 - c], land.at[p], (x, y, 1 - c), None), (land.at[p], None)))
                k += 1
        return out
    return plan


def _is_one_of(chip, dests):
    hit = chip == dests[0]
    for d in dests[1:]:
        hit = hit | (chip == d)
    return hit


def _slab_of(chip, dests):
    return sum(j * (chip == d).astype(jnp.int32) for j, d in enumerate(dests))


def _scatter_plan(dest_sets):
    def plan(refs):
        x, y, c = _coords()
        me = 2 * x + y
        out = []
        for k, (px, py) in enumerate(_other_chips(x, y)):
            peer = 2 * px + py
            for i, dests in enumerate(dest_sets):
                cs, land = refs[2 * i], refs[2 * i + 1]
                everyone = len(dests) == N_CHIPS
                send = (cs.at[_slab_of(peer, dests)], land.at[k], (px, py, c),
                        None if everyone else _is_one_of(peer, dests))
                recv = (land.at[k], None if everyone else _is_one_of(me, dests))
                out.append((k * len(dest_sets) + i, send, recv))
        return out
    return plan


def _join_plan(rows, n_pieces):
    def plan(refs):
        x, y, c = _coords()
        (buf,) = refs
        return [(i, (buf.at[c, piece], buf.at[c, piece], (x, y, 1 - c), None), (buf.at[1 - c, piece], None))
                for i, piece in enumerate(_chunks(rows, n_pieces))]
    return plan


def _join_plans(parts):
    def plan(refs):
        out, b0, k0 = [], 0, 0
        for part_plan, n_bufs, n_copies in parts:
            out += [(k0 + k, send, recv) for k, send, recv in part_plan(refs[b0:b0 + n_bufs])]
            b0 += n_bufs
            k0 += n_copies
        return out
    return plan


def _allgather_plan():
    def plan(refs):
        x, y, c = _coords()
        (land,) = refs
        me = 4 * x + 2 * y + c
        out = []
        for r in range(1, 8):
            px = 1 - x if r & 4 else x
            py = 1 - y if r & 2 else y
            pc = 1 - c if r & 1 else c
            out.append((r - 1, (land.at[me], land.at[me], (px, py, pc), None), (land.at[4 * px + 2 * py + pc], None)))
        return out
    return plan


def _sum_gathered(land, name):
    def body(land_ref, o_ref):
        acc = land_ref[0]
        for d in range(1, 8):
            acc = acc + land_ref[d]
        o_ref[...] = acc

    return pl.pallas_call(
        body,
        name=name,
        out_shape=jax.ShapeDtypeStruct(land.shape[1:], F32),
        compiler_params=_params(),
    )(land)


def _join_halves(bufs, n_chunks, name, deps=()):
    n = len(bufs)
    pieces = [(a, rows) for a in range(n) for rows in _chunks(bufs[a].shape[1], n_chunks[a])]
    n_p = len(pieces)

    def body(*refs):
        outs = refs[-n - 2:-2]
        send_sems, recv_sems = refs[-2:]
        x, y, c = _coords()

        def copy(i, half):
            a, rows = pieces[i]
            return pltpu.make_async_remote_copy(
                src_ref=outs[a].at[half, rows], dst_ref=outs[a].at[half, rows], send_sem=send_sems.at[i],
                recv_sem=recv_sems.at[i], device_id=(x, y, 1 - c), device_id_type=MESH)

        sends = [copy(i, c) for i in range(n_p)]
        for cp in sends:
            cp.start()
        for i in range(n_p):
            copy(i, 1 - c).wait_recv()
        for cp in sends:
            cp.wait_send()

    anyspec = pl.BlockSpec(memory_space=pl.ANY)
    sems = pltpu.SemaphoreType.DMA((n_p,))
    return pl.pallas_call(
        body,
        name=name,
        in_specs=[anyspec] * (n + len(deps)),
        out_specs=[anyspec] * n,
        out_shape=[jax.ShapeDtypeStruct(b.shape, b.dtype) for b in bufs],
        input_output_aliases={a: a for a in range(n)},
        scratch_shapes=[sems, sems],
    )(*bufs, *deps)


def _row_tile(rows, cap):
    t = cap
    while rows % t:
        t //= 2
    return t


def _add_my_half(g, r, name):
    n_slabs, _, R, C = g.shape
    tr = R if n_slabs > 1 else _row_tile(R, SUM_ROWS)

    def body(c_ref, g_ref, r_ref, o_ref):
        o_ref[...] = (g_ref[0] + r_ref[...]).astype(o_ref.dtype)

    return pl.pallas_call(
        body,
        name=name,
        grid_spec=pltpu.PrefetchScalarGridSpec(
            num_scalar_prefetch=1,
            grid=(n_slabs, R // tr),
            in_specs=[pl.BlockSpec((1, 1, tr, C), lambda p, i, c_ref: (p, c_ref[0], i, 0)),
                      pl.BlockSpec((1, tr, C), lambda p, i, c_ref: (p, i, 0))],
            out_specs=pl.BlockSpec((1, tr, C), lambda p, i, c_ref: (p, i, 0)),
        ),
        out_shape=jax.ShapeDtypeStruct(r.shape, jnp.bfloat16),
        compiler_params=_params(("parallel", "parallel")),
    )(lax.axis_index("c").reshape(1).astype(jnp.int32), g, r)


def _sum_slabs(own, got, name, deps=()):
    _, R, C = own.shape
    tr = _row_tile(R, SUM_ROWS)

    def body(s_ref, own_ref, got_ref, *rest):
        rest[-1][0] = ((own_ref[0].astype(F32) + got_ref[0].astype(F32)) + got_ref[1].astype(F32)) + got_ref[2].astype(F32)

    xi, yi, ci = _coords()
    return pl.pallas_call(
        body,
        name=name,
        grid_spec=pltpu.PrefetchScalarGridSpec(
            num_scalar_prefetch=1,
            grid=(R // tr,),
            in_specs=[pl.BlockSpec((1, tr, C), lambda i, s: (s[0], i, 0)),
                      pl.BlockSpec((3, tr, C), lambda i, s: (0, i, 0))] + [ANY_SPEC] * len(deps),
            out_specs=pl.BlockSpec((1, tr, C), lambda i, s: (s[1], i, 0)),
        ),
        out_shape=jax.ShapeDtypeStruct((2, R, C), F32),
        compiler_params=_params(("parallel",)),
    )(jnp.stack([2 * xi + yi, ci]).astype(jnp.int32), own, got, *deps)


def _sum_parts(owns, got, dest_sets, name):
    n = len(owns)
    _, R, C = owns[0].shape
    tr = _row_tile(R, SUM_ROWS)

    def body(s_ref, *refs):
        got_ref, o_ref = refs[n], refs[-1]
        total = jnp.zeros((tr, C), F32)
        for i in range(n):
            total = total + jnp.where(s_ref[2 + 2 * i] == 1, refs[i][0].astype(F32), 0.0)
        o_ref[0] = ((total + got_ref[0].astype(F32)) + got_ref[1].astype(F32)) + got_ref[2].astype(F32)

    xi, yi, ci = _coords()
    me = 2 * xi + yi
    scalars = [ci, ci]
    for dests in dest_sets:
        scalars += [_is_one_of(me, dests).astype(jnp.int32), _slab_of(me, dests)]
    own_spec = lambda i: pl.BlockSpec((1, tr, C), lambda r, s: (s[3 + 2 * i], r, 0))
    return pl.pallas_call(
        body,
        name=name,
        grid_spec=pltpu.PrefetchScalarGridSpec(
            num_scalar_prefetch=1,
            grid=(R // tr,),
            in_specs=[own_spec(i) for i in range(n)] + [pl.BlockSpec((3, tr, C), lambda r, s: (0, r, 0))],
            out_specs=pl.BlockSpec((1, tr, C), lambda r, s: (s[0], r, 0)),
        ),
        out_shape=jax.ShapeDtypeStruct((2, R, C), F32),
        compiler_params=_params(("parallel",)),
    )(jnp.stack(scalars).astype(jnp.int32), *owns, got)


def _adamw_math(w, g, m, v):
    m = ADAM_B1 * m + (1.0 - ADAM_B1) * g
    v = ADAM_B2 * v + (1.0 - ADAM_B2) * (g * g)
    m_hat = m / (1.0 - ADAM_B1 ** ADAM_STEP)
    v_hat = v / (1.0 - ADAM_B2 ** ADAM_STEP)
    delta = -ADAM_LR * (m_hat / (jnp.sqrt(v_hat) + ADAM_EPS) + ADAM_WD * w)
    return delta, m, v


def _adamw_halves(ws, g, ms, vs, half, prev, name, deps=()):
    n = len(ws)
    _, _, R, C = g.shape
    tr = _row_tile(R, ADAMW_ROWS)
    steps = R // tr
    carried = [] if prev is None else [a for four in prev for a in four]
    both = half is None
    which = (lambda i, s: i // steps) if both else (lambda i, s: s[0])
    half = 0 if both else half

    def body(s_ref, *refs):
        w_refs, g_refs, m_refs, v_refs = (refs[k * n:(k + 1) * n] for k in range(4))
        outs = refs[len(refs) - 4 * n:]
        for a in range(n):
            grad = g_refs[a][0, 0]
            d, mn, vn = _adamw_math(w_refs[a][...], grad, m_refs[a][...], v_refs[a][...])
            for o, val in zip(outs[4 * a:4 * a + 4], (grad, d, mn, vn)):
                o[...] = val

    rows = pl.BlockSpec((tr, C), lambda i, s: (which(i, s) * steps + i % steps, 0))
    grad_spec = lambda a: pl.BlockSpec((1, 1, tr, C), lambda i, s: (which(i, s), a, i % steps, 0))
    n_in = 4 * n
    outs = pl.pallas_call(
        body,
        name=name,
        grid_spec=pltpu.PrefetchScalarGridSpec(
            num_scalar_prefetch=1,
            grid=(2 * steps if both else steps,),
            in_specs=[rows] * n + [grad_spec(a) for a in range(n)] + [rows] * (2 * n)
            + [ANY_SPEC] * (len(carried) + len(deps)),
            out_specs=[rows] * (4 * n),
        ),
        out_shape=[jax.ShapeDtypeStruct((2 * R, C), F32)] * (4 * n),
        input_output_aliases={1 + n_in + k: k for k in range(len(carried))},
        compiler_params=_params(("parallel",)),
    )(jnp.reshape(half, (1,)).astype(jnp.int32), *ws, *([g] * n), *ms, *vs, *carried, *deps)
    return [outs[4 * a:4 * a + 4] for a in range(n)]


def _adamw_small(ws, gs, ms, vs, name):
    n = len(ws)

    def body(*refs):
        for a in range(n):
            d, mn, vn = _adamw_math(refs[a][...], refs[n + a][...], refs[2 * n + a][...], refs[3 * n + a][...])
            refs[4 * n + a][...] = d
            refs[5 * n + a][...] = mn
            refs[6 * n + a][...] = vn

    shapes = [jax.ShapeDtypeStruct(w.shape, F32) for w in ws]
    outs = pl.pallas_call(
        body,
        name=name,
        out_shape=shapes * 3,
        compiler_params=_params(),
    )(*ws, *gs, *ms, *vs)
    return outs[:n], outs[n:2 * n], outs[2 * n:]


def _to_blockdiag(w):
    per = CW // LRU_BW
    w4 = w.reshape(N_CT, per, LRU_BW, LRU_BW)
    eye = jnp.eye(per, dtype=w.dtype)
    return (w4[:, :, :, None, :] * eye[None, :, None, :, None]).reshape(N_CT, CW, CW)


def _from_blockdiag(g):
    per = CW // LRU_BW
    g5 = g.reshape(N_CT, per, LRU_BW, per, LRU_BW)
    return jnp.stack([g5[:, b, :, b, :] for b in range(per)], axis=1).reshape(LRU_BLOCKS, LRU_BW, LRU_BW)


def _local_grads(x2d, tgt2d, B, S, g_in, in_proj, conv_b, gate_x_w, gate_x_b, gate_a_w, gate_a_b, lam,
                 proj_weights, g_fin, reduce):
    wx_bd = _c(_to_blockdiag(gate_x_w))
    wa_bd = _c(_to_blockdiag(gate_a_w))
    tables = _retention_tables(S)

    proj, ht, w_all, conv_w, gain = in_proj(x2d, g_in, (tables[0], tables[1], wx_bd, wa_bd))
    gain3 = gain.reshape(HEADS, 1, DK)
    hlru, ya = _lru_fwd(proj, conv_w, conv_b, wx_bd, wa_bd, gate_x_b, gate_a_b, lam, B, S)
    o_pre, yb, states = _ret_fwd(proj, tables, gain3, B, S)
    wpa, wpb, wout = proj_weights(yb)
    loss, dx2, dya, dyb, dm, dgf, gw_proj = _mid(ya, yb, proj, x2d, tgt2d, wpa, wpb, wout, g_fin)
    g3 = _inproj_bwd_dw(ht, [dm], "inproj_bwd_dw_m")
    deps = reduce.m_ready(gw_proj, g3)
    dr, dgain = _ret_bwd(dyb, o_pre, proj, states, tables, gain3, B, S, deps)
    deps = reduce.ret_done(dr)
    g12 = _inproj_bwd_dw(ht, [dr], "inproj_bwd_dw_r", deps)
    deps = reduce.r_ready(g12)
    dxa, dga, dcw, dcb, dwx_bd, dwa_bd, dbx, dba, dlam = _lru_bwd(
        dya, proj, hlru, conv_w, conv_b, wx_bd, wa_bd, gate_x_b, gate_a_b, lam, B, S, deps)
    small = dict(conv_w=dcw, conv_b=dcb, gate_x_w=_from_blockdiag(dwx_bd), gate_x_b=dbx,
                 gate_a_w=_from_blockdiag(dwa_bd), gate_a_b=dba, lru_lambda=dlam, gn_gain=dgain.reshape(HEADS, DK),
                 norm_final=dgf)
    loss_rows = jnp.broadcast_to(loss, (SUBLANES, LANES))
    deps = reduce.lru_done(dxa, jnp.concatenate([_pack_small(small), loss_rows], axis=0))
    g0 = _inproj_bwd_dw(ht, [dxa, dga], "inproj_bwd_dw_a", deps)
    deps = reduce.a_ready(g0)
    n_tiles = x2d.shape[0] // min(DX_TILE, x2d.shape[0])
    grad_x, dgin = _inproj_bwd_dx([dxa, dga, dr, dm], w_all, x2d, dx2, g_in, 0, n_tiles, None, "inproj_bwd_dx", deps)
    return grad_x, dgin


ALL_CHIPS = (0, 1, 2, 3)


class _GradReduce:
    def __init__(self, proj_done):
        self.pending = {}
        self.proj_done = proj_done
        self.land_in = None

    def _start(self, key, parts, name):
        bufs, plans, shared = [], [], None
        for part_bufs, plan, n_copies, part_shared in parts:
            if part_shared is not None:
                shared = len(bufs) + part_shared
            plans.append((plan, len(part_bufs), n_copies))
            bufs += part_bufs
        plan = _join_plans(plans)
        send_sems, recv_sems, bufs, token = _copies_start(bufs, plan, sum(p[2] for p in plans), name + "_start")
        if shared is not None:
            self.land_in = bufs[shared]
        self.pending[key] = (send_sems, recv_sems, bufs, plan, name + "_wait", shared)
        return (token,)

    def _finish(self, key, after):
        send_sems, recv_sems, bufs, plan, name, shared = self.pending.pop(key)
        if shared is not None:
            bufs[shared] = self.land_in
        bufs = _copies_wait(send_sems, recv_sems, bufs, after, plan, name)
        if shared is not None:
            self.land_in = bufs[shared]
        return bufs

    @staticmethod
    def _swap(pieces):
        bufs = []
        for g in pieces:
            bufs += [g, lax.empty((g.shape[0],) + g.shape[2:], F32)]
        n_slabs = [g.shape[0] for g in pieces]
        return bufs, _swap_plan(n_slabs), sum(n_slabs), None

    def _scatter(self, sums, dest_sets):
        bufs = []
        for cs in sums:
            bufs += [cs, lax.empty((3,) + cs.shape[1:], cs.dtype)]
        if self.land_in is not None:
            bufs[-1] = self.land_in
        return bufs, _scatter_plan(dest_sets), 3 * len(sums), len(bufs) - 1

    @staticmethod
    def _gather8(block):
        x, y, c = _coords()
        land = lax.dynamic_update_slice(lax.empty((8,) + block.shape, F32), block[None], (4 * x + 2 * y + c, 0, 0))
        return [land], _allgather_plan(), 7, None

    def m_ready(self, gw_proj, g3):
        rows = gw_proj.shape[2] * gw_proj.shape[3]
        return self._start("m", [self._swap([gw_proj.reshape(N_CHIPS, 2, rows, D_MODEL), g3])], "swap_m")

    def ret_done(self, after):
        proj, land_p, g3, land_3 = self._finish("m", after)
        sums_m = [_add_my_half(proj, land_p, "chip_sum_proj"), _add_my_half(g3, land_3, "chip_sum_m")]
        return self._start("sm", [self._scatter(sums_m, [ALL_CHIPS, (3,)])], "scatter_m")

    def r_ready(self, g12):
        return self._start("r", [self._swap([g12])], "swap_r")

    def lru_done(self, after, packed):
        g12, land_12 = self._finish("r", after)
        sums_r = [_add_my_half(g12, land_12, "chip_sum_r")]
        return (self._start("sr", [self._scatter(sums_r, [(1, 2)])], "scatter_r")
                + self._start("small", [self._gather8(packed)], "gather_small"))

    def a_ready(self, g0):
        (token,) = self._start("a", [self._swap([g0])], "swap_a")
        csp, gotp, self.cs3, _ = self._finish("sm", token)
        half_proj = _sum_slabs(csp, gotp, "sum_w_proj")
        g0, land_0 = self._finish("a", half_proj)
        deps = self._start("sa", [self._scatter([_add_my_half(g0, land_0, "chip_sum_a")], [(0,)])], "scatter_a")
        self.proj_done(_join_halves([half_proj], [4], "join_halves_proj", deps)[0])
        return deps

    def finish(self, dgin, w_in_done):
        (token,) = self._start("n", [self._gather8(dgin)], "gather_norm_in")
        (small,) = self._finish("small", token)
        cs12, _ = self._finish("sr", token)
        cs0, _ = self._finish("sa", token)
        half_in = _sum_parts([self.cs3, cs12, cs0], self.land_in, [(3,), (1, 2), (0,)], "sum_w_in")
        deps = self._start("j", [([half_in], _join_plan(half_in.shape[1], JOIN_PIECES), JOIN_PIECES, None)], "join_w_in")
        first = w_in_done(self.pending["j"][2][0], True, None, deps)
        (g_in,) = self._finish("j", first[1])
        done = w_in_done(g_in, False, first, ())
        (norm_in,) = self._finish("n", done[1])
        return _sum_gathered(small, "sum_small_grads"), _sum_gathered(norm_in, "sum_norm_in_grad")


_SMALL = ("gate_x_w", "gate_a_w", "conv_w", "conv_b", "gate_x_b", "gate_a_b", "lru_lambda", "gn_gain", "norm_final")
_SMALL_SHAPES = dict(gate_x_w=(LRU_BLOCKS, LRU_BW, LRU_BW), gate_a_w=(LRU_BLOCKS, LRU_BW, LRU_BW),
                     norm_in=(1, D_MODEL), conv_w=(CONV, D_MODEL), conv_b=(1, D_MODEL), gate_x_b=(1, D_MODEL),
                     gate_a_b=(1, D_MODEL), lru_lambda=(1, D_MODEL), gn_gain=(HEADS, DK), norm_final=(1, D_MODEL))


def _pack_small(small):
    return jnp.concatenate([small[k].reshape(-1, 128) for k in _SMALL], axis=0)


def _unpack_small(packed):
    out, r = {}, 0
    for k in _SMALL:
        shape = _SMALL_SHAPES[k]
        rows = 1
        for s in shape:
            rows *= s
        rows //= 128
        out[k] = packed[r:r + rows].reshape(shape)
        r += rows
    return out


def kernel(x, norm_in, w_in, conv_w, conv_b, gate_x_w, gate_x_b, gate_a_w, gate_a_b, lru_lambda, gn_gain, w_proj_a, w_proj_b, w_out, norm_final, loss_target, m_norm_in, m_w_in, m_conv_w, m_conv_b, m_gate_x_w, m_gate_x_b, m_gate_a_w, m_gate_a_b, m_lru_lambda, m_gn_gain, m_w_proj_a, m_w_proj_b, m_w_out, m_norm_final, v_norm_in, v_w_in, v_conv_w, v_conv_b, v_gate_x_w, v_gate_x_b, v_gate_a_w, v_gate_a_b, v_lru_lambda, v_gn_gain, v_w_proj_a, v_w_proj_b, v_w_out, v_norm_final):
    B, S, _ = x.shape
    T = B * S
    xi, yi, ci = _coords()
    chip = 2 * xi + yi

    cshard = D_MODEL // N_CHIPS
    mine = _cast_into_slot([w_in[0].reshape(2, D_MODEL // 2, 2 * D_MODEL)], "cast_w_in")
    plan = _gather_plan(3)
    pending_proj = []
    gshard = DK // N_CHIPS
    tiny = jnp.concatenate([conv_w[0], jnp.zeros((4, cshard), F32), jnp.pad(gn_gain[0], ((0, 4), (0, cshard - gshard)))],
                           axis=0).reshape(1, 2, SUBLANES, cshard)
    tiny_buf = lax.dynamic_update_slice(lax.empty((N_CHIPS, 2, SUBLANES, cshard), F32), tiny, (chip, 0, 0, 0))
    near_plan, pass_plan, far_plan = (_chip_gather_plan(stage, 2) for stage in ("near", "pass", "far"))
    (n_near, _), (n_pass, passed_on), (n_far, _) = (_chip_gather_copies(stage, 2) for stage in ("near", "pass", "far"))
    halves = set(range(n_pass)) - passed_on
    near_s, near_r, bufs, near_token = _copies_start([mine[0], tiny_buf], near_plan, n_near, "gather_near_start")

    def in_proj(x2d, g_in, meanwhile):
        mine_proj = _cast_into_slot([w[0].reshape(2, cshard // 2, D_MODEL) for w in (w_proj_a, w_proj_b, w_out)],
                                    "cast_w_proj", (near_token,))
        as_w = lambda b: b[0].reshape(N_CHIPS, D_MODEL, 2 * D_MODEL)
        slot_x, slot_y, slot_d = 2 * (1 - xi) + yi, 2 * xi + (1 - yi), 2 * (1 - xi) + (1 - yi)
        ids = lambda *chips: jnp.stack(chips).astype(jnp.int32)
        proj, hb, ht = _inproj_first(x2d, g_in, as_w(bufs), ids(chip), "inproj_own", (near_token, *meanwhile))
        got = _copies_wait(near_s, near_r, bufs, proj, near_plan, "gather_near_wait")
        pass_s, pass_r, got, pass_token = _copies_start(got, pass_plan, n_pass, "gather_pass_start")
        got = _copies_wait(pass_s, pass_r, got, pass_token, pass_plan, "gather_pass_wait_halves", only=halves)
        proj = _inproj_more(hb, as_w(got), ids(slot_x, slot_y), proj, "inproj_near")
        got = _copies_wait(pass_s, pass_r, got, proj, pass_plan, "gather_pass_wait_far", only=passed_on)
        pending_proj.append(_copies_start(mine_proj, plan, 9, "gather_proj_start", (got[0],)))
        far_s, far_r, got, far_token = _copies_start(got, far_plan, n_far, "gather_far_start")
        got = _copies_wait(far_s, far_r, got, far_token, far_plan, "gather_far_wait")
        proj = _inproj_more(hb, as_w(got), ids(slot_d), proj, "inproj_far")
        tiny_all = got[1].reshape(N_CHIPS, 2 * SUBLANES, cshard)
        conv_w_full = jnp.transpose(tiny_all[:, 0:CONV, :], (1, 0, 2)).reshape(CONV, D_MODEL)
        gain_full = jnp.transpose(tiny_all[:, 8:8 + HEADS, :gshard], (1, 0, 2)).reshape(HEADS, DK)
        return proj, ht, as_w(got), conv_w_full, gain_full

    def proj_weights(after):
        s_sems, r_sems, pbufs, _ = pending_proj[0]
        got = _copies_wait(s_sems, r_sems, pbufs, after, plan, "gather_proj_wait")
        return [b.reshape(D_MODEL, D_MODEL) for b in got]

    weights = dict(norm_in=norm_in, w_in=w_in, conv_w=conv_w, conv_b=conv_b, gate_x_w=gate_x_w, gate_x_b=gate_x_b,
                   gate_a_w=gate_a_w, gate_a_b=gate_a_b, lru_lambda=lru_lambda, gn_gain=gn_gain, w_proj_a=w_proj_a,
                   w_proj_b=w_proj_b, w_out=w_out, norm_final=norm_final)
    ms = dict(norm_in=m_norm_in, w_in=m_w_in, conv_w=m_conv_w, conv_b=m_conv_b, gate_x_w=m_gate_x_w,
              gate_x_b=m_gate_x_b, gate_a_w=m_gate_a_w, gate_a_b=m_gate_a_b, lru_lambda=m_lru_lambda, gn_gain=m_gn_gain,
              w_proj_a=m_w_proj_a, w_proj_b=m_w_proj_b, w_out=m_w_out, norm_final=m_norm_final)
    vs = dict(norm_in=v_norm_in, w_in=v_w_in, conv_w=v_conv_w, conv_b=v_conv_b, gate_x_w=v_gate_x_w,
              gate_x_b=v_gate_x_b, gate_a_w=v_gate_a_w, gate_a_b=v_gate_a_b, lru_lambda=v_lru_lambda, gn_gain=v_gn_gain,
              w_proj_a=v_w_proj_a, w_proj_b=v_w_proj_b, w_out=v_w_out, norm_final=v_norm_final)
    names = list(weights)
    grads, delta, new_m, new_v = {}, {}, {}, {}

    def update_big(keys, g, half, prev, name, deps=()):
        two = lambda a: a.reshape(a.shape[1], a.shape[2])
        res = _adamw_halves([two(weights[k]) for k in keys], g, [two(ms[k]) for k in keys], [two(vs[k]) for k in keys],
                            half, prev, name, deps)
        for k, (gk, d, mn, vn) in zip(keys, res):
            shp = weights[k].shape
            grads[k], delta[k], new_m[k], new_v[k] = gk.reshape(shp), d.reshape(shp), mn.reshape(shp), vn.reshape(shp)
        return res

    def proj_done(g_proj):
        g4 = g_proj.reshape(2, 3, D_MODEL // (2 * N_CHIPS), D_MODEL)
        return update_big(("w_proj_a", "w_proj_b", "w_out"), g4, None, None, "adamw_proj")[-1][1]

    def w_in_done(g_in, own, prev, deps):
        g4 = g_in.reshape(2, 1, D_MODEL // 2, 2 * D_MODEL)
        return update_big(("w_in",), g4, ci if own else 1 - ci, None if prev is None else [prev],
                          "adamw_w_in_own" if own else "adamw_w_in_other", deps)[0]

    reduce = _GradReduce(proj_done)
    grad_x, dgin = _local_grads(
        x.reshape(T, D_MODEL), loss_target.reshape(T, D_MODEL), B, S, norm_in, in_proj, conv_b,
        gate_x_w[0], gate_x_b, gate_a_w[0], gate_a_b, lru_lambda, proj_weights,
        norm_final.reshape(1, D_MODEL), reduce)

    small_sum, g_norm_in = reduce.finish(dgin.reshape(SUBLANES, LANES), w_in_done)
    loss = small_sum[small_sum.shape[0] - SUBLANES, 0]

    gsm = _unpack_small(small_sum)
    gsm["norm_in"] = g_norm_in
    gsm["conv_w"] = lax.dynamic_slice_in_dim(gsm["conv_w"], chip * cshard, cshard, axis=1)
    gsm["gn_gain"] = lax.dynamic_slice_in_dim(gsm["gn_gain"], chip * gshard, gshard, axis=1)
    smalls = [k for k in names if k not in delta]

    def view(a):
        return a.reshape(1, -1) if a.ndim == 1 else (a.reshape(a.shape[1:]) if a.ndim > 2 else a)

    ds, mns, vns = _adamw_small([view(weights[k]) for k in smalls], [gsm[k].reshape(view(weights[k]).shape) for k in smalls],
                                [view(ms[k]) for k in smalls], [view(vs[k]) for k in smalls], "adamw_small")
    for k, d, mn, vn in zip(smalls, ds, mns, vns):
        shp = weights[k].shape
        grads[k], delta[k], new_m[k], new_v[k] = gsm[k].reshape(shp), d.reshape(shp), mn.reshape(shp), vn.reshape(shp)

    return (loss, grad_x.reshape(B, S, D_MODEL), *[grads[k] for k in names], *[delta[k] for k in names],
            *[new_m[k] for k in names], *[new_v[k] for k in names])
```

```python
import jax
import jax.numpy as jnp
from jax import lax
from jax.experimental import pallas as pl
from jax.experimental.pallas import tpu as pltpu

F32 = jnp.float32
_MXU = jnp.bfloat16

D_MODEL = 1024
N_GROUPS = 8
HEADS = 4
DK = 256
CHUNK = 128
CONV = 4
LRU_BLOCKS = 16
LRU_BW = 64
LRU_C = 8.0
ROPE_THETA = 10000.0
EPS = 1e-6
CW = 256
N_CT = D_MODEL // CW
N_CHIPS = 4
MESH = pl.DeviceIdType.MESH

ADAM_LR = 0.001
ADAM_B1 = 0.9
ADAM_B2 = 0.999
ADAM_EPS = 1e-08
ADAM_WD = 0.01
ADAM_STEP = 10

VMEM_LIMIT = 56 * 1024 * 1024

FIRST_PROJ_TILE = 1024
MORE_PROJ_TILE = 2048
SCAN_TILE = 1024
MID_TILE = 256
DX_TILE = 512
DW_COLS = 512
RET_CHUNKS = 2
SUM_ROWS = 256
ADAMW_ROWS = 256
JOIN_PIECES = 8
PROJ_JOIN_PIECES = 4


def _c(v):
    return v.astype(_MXU)


def _dot(a, b):
    return lax.dot_general(a, b, (((1,), (0,)), ((), ())), preferred_element_type=F32)


def _dot_nt(a, b):
    return lax.dot_general(a, b, (((1,), (1,)), ((), ())), preferred_element_type=F32)


def _dot_tn(a, b):
    return lax.dot_general(a, b, (((0,), (0,)), ((), ())), preferred_element_type=F32)


def _sigmoid(z):
    return 0.5 * jnp.tanh(0.5 * z) + 0.5


ANY_SPEC = pl.BlockSpec(memory_space=pl.ANY)


def _after(body, n_in, deps):
    n_deps = len(deps)

    def wrapped(*refs):
        return body(*refs[:n_in], *refs[n_in + n_deps:])

    return wrapped


def _params(sem=None):
    if sem is None:
        return pltpu.CompilerParams(vmem_limit_bytes=VMEM_LIMIT)
    return pltpu.CompilerParams(vmem_limit_bytes=VMEM_LIMIT, dimension_semantics=sem)


def _inproj_first(x2d, g_in, w_all, chips, name, deps=()):
    T = x2d.shape[0]
    tm = min(FIRST_PROJ_TILE, T)
    n_i = T // tm

    def body(s_ref, *refs):
        x_ref, g_ref, w_ref = refs[:3]
        proj_ref, hb_ref, ht_ref, h_all = refs[-4:]
        i = pl.program_id(1)
        rows = pl.ds(pl.multiple_of(i * tm, tm), tm)

        @pl.when(pl.program_id(0) == 0)
        def _():
            x = x_ref[...]
            r = lax.rsqrt(jnp.mean(x * x, axis=-1, keepdims=True) + EPS)
            h = x * r * g_ref[...]
            hb = h.astype(h_all.dtype)
            h_all[rows, :] = hb
            hb_ref[...] = hb
            ht_ref[...] = h.T.astype(ht_ref.dtype)

        proj_ref[...] = _dot(h_all[rows, :], w_ref[0])

    first = lambda j, i: jnp.where(j == 0, i, n_i - 1)
    return pl.pallas_call(
        body,
        name=name,
        grid_spec=pltpu.PrefetchScalarGridSpec(
            num_scalar_prefetch=1,
            grid=(2 * chips.shape[0], n_i),
            in_specs=[
                pl.BlockSpec((tm, D_MODEL), lambda j, i, s: (first(j, i), 0)),
                pl.BlockSpec((1, D_MODEL), lambda j, i, s: (0, 0)),
                pl.BlockSpec((1, D_MODEL, D_MODEL), lambda j, i, s: (s[j // 2], 0, j % 2)),
            ] + [ANY_SPEC] * len(deps),
            out_specs=[
                pl.BlockSpec((tm, D_MODEL), lambda j, i, s: (i, 2 * s[j // 2] + j % 2)),
                pl.BlockSpec((tm, D_MODEL), lambda j, i, s: (first(j, i), 0)),
                pl.BlockSpec((D_MODEL, tm), lambda j, i, s: (0, first(j, i))),
            ],
            scratch_shapes=[pltpu.VMEM((T, D_MODEL), _MXU)],
        ),
        out_shape=[
            jax.ShapeDtypeStruct((T, N_GROUPS * D_MODEL), F32),
            jax.ShapeDtypeStruct((T, D_MODEL), _MXU),
            jax.ShapeDtypeStruct((D_MODEL, T), _MXU),
        ],
        compiler_params=_params(("arbitrary", "arbitrary")),
    )(chips, x2d, g_in, w_all, *deps)


def _inproj_more(hb, w_all, chips, proj, name):
    T = hb.shape[0]
    tm = min(MORE_PROJ_TILE, T)

    def body(s_ref, hb_hbm, w_ref, prev_ref, proj_ref, h_all, sem):
        @pl.when((pl.program_id(0) == 0) & (pl.program_id(1) == 0))
        def _():
            cp = pltpu.make_async_copy(hb_hbm, h_all, sem)
            cp.start()
            cp.wait()

        rows = pl.ds(pl.multiple_of(pl.program_id(1) * tm, tm), tm)
        proj_ref[...] = _dot(h_all[rows, :], w_ref[0])

    return pl.pallas_call(
        body,
        name=name,
        grid_spec=pltpu.PrefetchScalarGridSpec(
            num_scalar_prefetch=1,
            grid=(2 * chips.shape[0], T // tm),
            in_specs=[
                ANY_SPEC,
                pl.BlockSpec((1, D_MODEL, D_MODEL), lambda j, i, s: (s[j // 2], 0, j % 2)),
                ANY_SPEC,
            ],
            out_specs=pl.BlockSpec((tm, D_MODEL), lambda j, i, s: (i, 2 * s[j // 2] + j % 2)),
            scratch_shapes=[pltpu.VMEM((T, D_MODEL), hb.dtype), pltpu.SemaphoreType.DMA],
        ),
        out_shape=jax.ShapeDtypeStruct(proj.shape, F32),
        input_output_aliases={3: 0},
        compiler_params=_params(("arbitrary", "arbitrary")),
    )(chips, hb, w_all, proj)


def _scan_fwd(a, u):
    n = a.shape[0]
    row = lax.broadcasted_iota(jnp.int32, a.shape, 0)
    s = 1
    while s < n:
        m = row >= s
        u = u + a * jnp.where(m, pltpu.roll(u, s, 0), 0.0)
        a = a * jnp.where(m, pltpu.roll(a, s, 0), 1.0)
        s *= 2
    return a, u


def _scan_bwd(b, g):
    n = b.shape[0]
    row = lax.broadcasted_iota(jnp.int32, b.shape, 0)
    s = 1
    while s < n:
        m = row < n - s
        g = g + b * jnp.where(m, pltpu.roll(g, n - s, 0), 0.0)
        b = b * jnp.where(m, pltpu.roll(b, n - s, 0), 1.0)
        s *= 2
    return b, g


LANES = 128
SUBLANES = 8


def _scan_scratch(tc):
    by_lanes = pltpu.VMEM((CW // LANES, tc, LANES), F32)
    return [by_lanes, by_lanes, pltpu.VMEM((tc // SUBLANES, CW), F32), pltpu.VMEM((tc, CW), F32)]


def _scan_tile(a, u, edge, la_ref, lh_ref, c_ref, dst_ref, reverse):
    n, w = a.shape
    groups = n // SUBLANES
    a3 = a.reshape(groups, SUBLANES, w)
    u3 = u.reshape(groups, SUBLANES, w)
    row = lax.broadcasted_iota(jnp.int32, a3.shape, 1)
    for s in (1, 2, 4):
        m = (row < SUBLANES - s) if reverse else (row >= s)
        shift = SUBLANES - s if reverse else s
        u3 = u3 + a3 * jnp.where(m, pltpu.roll(u3, shift, 1), 0.0)
        a3 = a3 * jnp.where(m, pltpu.roll(a3, shift, 1), 1.0)
    al = a3.reshape(n, w)
    hl = u3.reshape(n, w)
    blocks = w // LANES
    for q in range(blocks):
        la_ref[q] = al[:, q * LANES:(q + 1) * LANES]
        lh_ref[q] = hl[:, q * LANES:(q + 1) * LANES]
    ends = pl.ds(0 if reverse else SUBLANES - 1, groups, stride=SUBLANES)
    end_a = jnp.concatenate([la_ref.at[q][ends, :] for q in range(blocks)], axis=-1)
    end_h = jnp.concatenate([lh_ref.at[q][ends, :] for q in range(blocks)], axis=-1)
    prod, part = (_scan_bwd if reverse else _scan_fwd)(end_a, end_h)
    total = part + prod * edge
    g_row = lax.broadcasted_iota(jnp.int32, total.shape, 0)
    if reverse:
        c_ref[...] = jnp.where(g_row == groups - 1, edge, pltpu.roll(total, groups - 1, 0))
    else:
        c_ref[...] = jnp.where(g_row == 0, edge, pltpu.roll(total, 1, 0))
    for g in range(groups):
        rows = slice(g * SUBLANES, (g + 1) * SUBLANES)
        for q in range(blocks):
            cols = slice(q * LANES, (q + 1) * LANES)
            dst_ref[rows, cols] = lh_ref[q, rows, :] + la_ref[q, rows, :] * c_ref[g:g + 1, cols]


def _softplus_neg(lam):
    z = -lam
    return jnp.maximum(z, 0.0) + jnp.log1p(jnp.exp(-jnp.abs(z)))


def _lru_gates(xc, wx_ref, wa_ref, bx_ref, ba_ref, lam_ref):
    xcb = _c(xc)
    i_t = _sigmoid(_dot(xcb, wx_ref[0]) + bx_ref[...])
    r_t = _sigmoid(_dot(xcb, wa_ref[0]) + ba_ref[...])
    sp = _softplus_neg(lam_ref[...])
    log_a = (-LRU_C) * r_t * sp
    a = jnp.exp(log_a)
    mult = jnp.sqrt(1.0 - a * a)
    return xcb, i_t, r_t, sp, a, mult


def _conv_from_ext(ext_ref, xa, cw_ref, cb_ref, tc):
    return (cb_ref[...] + cw_ref[3:4, :] * xa + cw_ref[2:3, :] * ext_ref[7:7 + tc, :]
            + cw_ref[1:2, :] * ext_ref[6:6 + tc, :] + cw_ref[0:1, :] * ext_ref[5:5 + tc, :])


def _lru_fwd(proj, conv_w, conv_b, wx_bd, wa_bd, bx, ba, lam, B, S):
    T = B * S
    tc = min(SCAN_TILE, S)
    nt = S // tc
    h8 = tc // 8

    def body(xa_ref, halo_ref, ga_ref, cw_ref, cb_ref, wx_ref, wa_ref, bx_ref, ba_ref, lam_ref,
             h_ref, ya_ref, ext_ref, carry_ref, la_ref, lh_ref, c_ref):
        t = pl.program_id(2)

        @pl.when(t == 0)
        def _():
            carry_ref[...] = jnp.zeros_like(carry_ref)

        xa = xa_ref[...]
        ext_ref[0:8, :] = jnp.where(t == 0, 0.0, halo_ref[...])
        ext_ref[8:8 + tc, :] = xa
        xc = _conv_from_ext(ext_ref, xa, cw_ref, cb_ref, tc)
        _, i_t, _, _, a, mult = _lru_gates(xc, wx_ref, wa_ref, bx_ref, ba_ref, lam_ref)
        u = mult * (i_t * xc)
        _scan_tile(a, u, carry_ref[7:8, :], la_ref, lh_ref, c_ref, h_ref, False)
        h = h_ref[...]
        carry_ref[...] = h[tc - 8:tc, :]
        ga = ga_ref[...]
        ya_ref[...] = (ga * _sigmoid(ga) * h).astype(ya_ref.dtype)

    row = lambda b, t: b * nt + t
    vec = pl.BlockSpec((1, CW), lambda b, c, t: (0, c))
    mat = pl.BlockSpec((1, CW, CW), lambda b, c, t: (c, 0, 0))
    return pl.pallas_call(
        body,
        name="lru_fwd",
        grid=(B, N_CT, nt),
        in_specs=[
            pl.BlockSpec((tc, CW), lambda b, c, t: (row(b, t), c)),
            pl.BlockSpec((8, CW), lambda b, c, t: (jnp.maximum(row(b, t) * h8 - 1, 0), c)),
            pl.BlockSpec((tc, CW), lambda b, c, t: (row(b, t), N_CT + c)),
            pl.BlockSpec((CONV, CW), lambda b, c, t: (0, c)),
            vec, mat, mat, vec, vec, vec,
        ],
        out_specs=[
            pl.BlockSpec((tc, CW), lambda b, c, t: (row(b, t), c)),
            pl.BlockSpec((tc, CW), lambda b, c, t: (row(b, t), c)),
        ],
        out_shape=[
            jax.ShapeDtypeStruct((T, D_MODEL), F32),
            jax.ShapeDtypeStruct((T, D_MODEL), _MXU),
        ],
        scratch_shapes=[pltpu.VMEM((tc + 8, CW), F32), pltpu.VMEM((8, CW), F32)] + _scan_scratch(tc)[:3],
        compiler_params=_params(("parallel", "parallel", "arbitrary")),
    )(proj, proj, proj, conv_w, conv_b, wx_bd, wa_bd, bx, ba, lam)


def _lru_bwd(dya, proj, hlru, conv_w, conv_b, wx_bd, wa_bd, bx, ba, lam, B, S, deps=()):
    T = B * S
    tc = min(SCAN_TILE, S)
    nt = S // tc
    h8 = tc // 8

    def body(dya_ref, xa_ref, xhalo_ref, ga_ref, h_ref, hhalo_ref, cw_ref, cb_ref, wx_ref, wa_ref, bx_ref, ba_ref,
             lam_ref, dxa_ref, dga_ref, dcw_ref, dcb_ref, dwx_ref, dwa_ref, dbx_ref, dba_ref, dlam_ref,
             ext_ref, ext2_ref, carry_ref, dhalo_ref, la_ref, lh_ref, c_ref, dh_ref):
        b = pl.program_id(1)
        t = pl.program_id(2)
        tt = nt - 1 - t

        @pl.when(t == 0)
        def _():
            carry_ref[...] = jnp.zeros_like(carry_ref)
            dhalo_ref[...] = jnp.zeros_like(dhalo_ref)

        @pl.when((t == 0) & (b == 0))
        def _():
            for r in (dcw_ref, dcb_ref, dwx_ref, dwa_ref, dbx_ref, dba_ref, dlam_ref):
                r[...] = jnp.zeros_like(r)

        xa = xa_ref[...]
        ext_ref[0:8, :] = jnp.where(tt == 0, 0.0, xhalo_ref[...])
        ext_ref[8:8 + tc, :] = xa
        xc = _conv_from_ext(ext_ref, xa, cw_ref, cb_ref, tc)
        xcb, i_t, r_t, sp, a, mult = _lru_gates(xc, wx_ref, wa_ref, bx_ref, ba_ref, lam_ref)

        h = h_ref[...]
        ga = ga_ref[...]
        dya_t = dya_ref[...]
        sg = _sigmoid(ga)
        dga_ref[...] = (dya_t * h * (sg * (1.0 + ga * (1.0 - sg)))).astype(dga_ref.dtype)
        dlru = dya_t * (ga * sg)

        row = lax.broadcasted_iota(jnp.int32, a.shape, 0)
        coef = jnp.where(row == tc - 1, 1.0, pltpu.roll(a, tc - 1, 0))
        _scan_tile(coef, dlru, carry_ref[0:1, :], la_ref, lh_ref, c_ref, dh_ref, True)
        dh = dh_ref[...]
        ext2_ref[0:tc, :] = a * dh
        carry_ref[...] = ext2_ref[0:8, :]

        ext2_ref[0:8, :] = jnp.where(tt == 0, 0.0, hhalo_ref[...])
        ext2_ref[8:8 + tc, :] = h
        hprev = ext2_ref[7:7 + tc, :]

        da = dh * hprev
        ix = i_t * xc
        dmult = dh * ix
        di = dh * mult * xc
        dxc = dh * mult * i_t
        dlog_a = da * a - dmult * (a * a) / mult
        dr = dlog_a * ((-LRU_C) * sp)
        dlam_ref[...] += jnp.sum(dlog_a * r_t, axis=0, keepdims=True) * (LRU_C * _sigmoid(-lam_ref[...]))
        dza = dr * r_t * (1.0 - r_t)
        dzx = di * i_t * (1.0 - i_t)
        dzab = _c(dza)
        dzxb = _c(dzx)
        dxc = dxc + _dot_nt(dzxb, wx_ref[0]) + _dot_nt(dzab, wa_ref[0])
        dwx_ref[0] += _dot_tn(xcb, dzxb)
        dwa_ref[0] += _dot_tn(xcb, dzab)
        dbx_ref[...] += jnp.sum(dzx, axis=0, keepdims=True)
        dba_ref[...] += jnp.sum(dza, axis=0, keepdims=True)

        dcb_ref[...] += jnp.sum(dxc, axis=0, keepdims=True)
        dcw_ref[3:4, :] += jnp.sum(dxc * xa, axis=0, keepdims=True)
        dcw_ref[2:3, :] += jnp.sum(dxc * ext_ref[7:7 + tc, :], axis=0, keepdims=True)
        dcw_ref[1:2, :] += jnp.sum(dxc * ext_ref[6:6 + tc, :], axis=0, keepdims=True)
        dcw_ref[0:1, :] += jnp.sum(dxc * ext_ref[5:5 + tc, :], axis=0, keepdims=True)
        ext2_ref[0:tc, :] = dxc
        ext2_ref[tc:tc + 8, :] = dhalo_ref[...]
        dxa = (cw_ref[3:4, :] * dxc + cw_ref[2:3, :] * ext2_ref[1:1 + tc, :]
               + cw_ref[1:2, :] * ext2_ref[2:2 + tc, :] + cw_ref[0:1, :] * ext2_ref[3:3 + tc, :])
        dxa_ref[...] = dxa.astype(dxa_ref.dtype)
        dhalo_ref[...] = ext2_ref[0:8, :]

    row_of = lambda b, t: b * nt + (nt - 1 - t)
    tile = lambda off: pl.BlockSpec((tc, CW), lambda c, b, t: (row_of(b, t), off + c))
    halo = pl.BlockSpec((8, CW), lambda c, b, t: (jnp.maximum(row_of(b, t) * h8 - 1, 0), c))
    vec = pl.BlockSpec((1, CW), lambda c, b, t: (0, c))
    mat = pl.BlockSpec((1, CW, CW), lambda c, b, t: (c, 0, 0))
    cwspec = pl.BlockSpec((CONV, CW), lambda c, b, t: (0, c))
    return pl.pallas_call(
        _after(body, 13, deps),
        name="lru_bwd",
        grid=(N_CT, B, nt),
        in_specs=[tile(0), tile(0), halo, tile(N_CT), tile(0), halo, cwspec, vec, mat, mat, vec, vec, vec]
        + [ANY_SPEC] * len(deps),
        out_specs=[tile(0), tile(0), cwspec, vec, mat, mat, vec, vec, vec],
        out_shape=[
            jax.ShapeDtypeStruct((T, D_MODEL), _MXU),
            jax.ShapeDtypeStruct((T, D_MODEL), _MXU),
            jax.ShapeDtypeStruct((CONV, D_MODEL), F32),
            jax.ShapeDtypeStruct((1, D_MODEL), F32),
            jax.ShapeDtypeStruct((N_CT, CW, CW), F32),
            jax.ShapeDtypeStruct((N_CT, CW, CW), F32),
            jax.ShapeDtypeStruct((1, D_MODEL), F32),
            jax.ShapeDtypeStruct((1, D_MODEL), F32),
            jax.ShapeDtypeStruct((1, D_MODEL), F32),
        ],
        scratch_shapes=[pltpu.VMEM((tc + 8, CW), F32), pltpu.VMEM((tc + 8, CW), F32),
                        pltpu.VMEM((8, CW), F32), pltpu.VMEM((8, CW), F32)] + _scan_scratch(tc),
        compiler_params=_params(("parallel", "arbitrary", "arbitrary")),
    )(dya, proj, proj, proj, hlru, hlru, conv_w, conv_b, wx_bd, wa_bd, bx, ba, lam, *deps)


def _retention_tables(S):
    half = DK // 2
    freqs = ROPE_THETA ** (-jnp.arange(half, dtype=F32) / half)
    ang = jnp.arange(S, dtype=F32)[:, None] * freqs[None, :]
    log_g = jnp.log1p(-(2.0 ** (-5.0 - jnp.arange(HEADS, dtype=F32))))
    idx = jnp.arange(CHUNK, dtype=F32)
    diff = idx[:, None] - idx[None, :]
    inner = jnp.where(diff >= 0, jnp.exp(jnp.maximum(diff, 0.0)[None] * log_g[:, None, None]), 0.0)
    cross = jnp.exp((idx[None, :] + 1.0) * log_g[:, None])[:, :, None]
    state = jnp.exp((CHUNK - 1.0 - idx[None, :]) * log_g[:, None])[:, :, None]
    gam = jnp.broadcast_to(jnp.exp(CHUNK * log_g)[:, None, None], (HEADS, 1, DK))
    return jnp.cos(ang), jnp.sin(ang), inner, cross, state, gam


def _rot(x, cos, sin):
    half = DK // 2
    x1, x2 = x[:, :half], x[:, half:]
    return jnp.concatenate([x1 * cos - x2 * sin, x1 * sin + x2 * cos], axis=-1)


def _rot_t(y, cos, sin):
    half = DK // 2
    y1, y2 = y[:, :half], y[:, half:]
    return jnp.concatenate([y1 * cos + y2 * sin, y2 * cos - y1 * sin], axis=-1)


def _groupnorm(o):
    mu = jnp.mean(o, axis=-1, keepdims=True)
    oc = o - mu
    rs = lax.rsqrt(jnp.mean(oc * oc, axis=-1, keepdims=True) + EPS)
    return oc * rs, rs


def _ret_specs(B, chunk_of):
    rows = RET_CHUNKS * CHUNK
    qkv = lambda g: pl.BlockSpec((B, rows, D_MODEL), lambda c: (0, chunk_of(c), g))
    act = pl.BlockSpec((B, rows, D_MODEL), lambda c: (0, chunk_of(c), 0))
    rope = pl.BlockSpec((rows, DK // 2), lambda c: (chunk_of(c), 0))
    dmat = pl.BlockSpec((HEADS, CHUNK, CHUNK), lambda c: (0, 0, 0))
    dvec = pl.BlockSpec((HEADS, CHUNK, 1), lambda c: (0, 0, 0))
    hrow = pl.BlockSpec((HEADS, 1, DK), lambda c: (0, 0, 0))
    rst = pl.BlockSpec((RET_CHUNKS, B, HEADS, DK, DK), lambda c: (chunk_of(c), 0, 0, 0, 0))
    return qkv, act, rope, dmat, dvec, hrow, rst


def _ret_fwd(proj, tables, gain3, B, S):
    T = B * S
    nc = S // CHUNK
    cos, sin, dmat_t, cd_t, sd_t, gam_t = tables

    def body(q_ref, k_ref, v_ref, gb_ref, cos_ref, sin_ref, dm_ref, cd_ref, sd_ref, gam_ref, gain_ref,
             o_ref, yb_ref, rs_ref, state_ref):
        @pl.when(pl.program_id(0) == 0)
        def _():
            state_ref[...] = jnp.zeros_like(state_ref)

        for cc, b, h in [(cc, b, h) for cc in range(RET_CHUNKS) for b in range(B) for h in range(HEADS)]:
            rows = slice(cc * CHUNK, (cc + 1) * CHUNK)
            cos_t, sin_t = cos_ref[rows, :], sin_ref[rows, :]
            cols = slice(h * DK, (h + 1) * DK)
            qb = _c(_rot(q_ref[b, rows, cols], cos_t, sin_t))
            kb = _c(_rot(k_ref[b, rows, cols], cos_t, sin_t) * (DK ** -0.5))
            v = v_ref[b, rows, cols]
            state = state_ref[b, h]
            sb = _c(state)
            rs_ref[cc, b, h] = sb
            scores = _dot_nt(qb, kb) * dm_ref[h]
            o = _dot(_c(scores), _c(v)) + _dot(qb, sb) * cd_ref[h]
            state_ref[b, h] = gam_ref[h] * state + _dot_tn(kb, _c(v * sd_ref[h]))
            o_ref[b, rows, cols] = o
            n, _ = _groupnorm(o)
            gb = gb_ref[b, rows, cols]
            yb_ref[b, rows, cols] = (gb * _sigmoid(gb) * (n * gain_ref[h])).astype(yb_ref.dtype)

    qkv, act, rope, dmat, dvec, hrow, rst = _ret_specs(B, lambda c: c)
    proj3 = proj.reshape(B, S, proj.shape[1])
    o_pre, yb, states = pl.pallas_call(
        body,
        name="ret_fwd",
        grid=(nc // RET_CHUNKS,),
        in_specs=[qkv(2), qkv(3), qkv(4), qkv(5), rope, rope, dmat, dvec, dvec, hrow, hrow],
        out_specs=[act, act, rst],
        out_shape=[
            jax.ShapeDtypeStruct((B, S, D_MODEL), F32),
            jax.ShapeDtypeStruct((B, S, D_MODEL), _MXU),
            jax.ShapeDtypeStruct((nc, B, HEADS, DK, DK), _MXU),
        ],
        scratch_shapes=[pltpu.VMEM((B, HEADS, DK, DK), F32)],
        compiler_params=_params(("arbitrary",)),
    )(proj3, proj3, proj3, proj3, cos, sin, dmat_t, cd_t, sd_t, gam_t, gain3)
    return o_pre.reshape(T, D_MODEL), yb.reshape(T, D_MODEL), states


def _ret_bwd(dyb, o_pre, proj, states, tables, gain3, B, S, deps=()):
    T = B * S
    nc = S // CHUNK
    cos, sin, dmat_t, cd_t, sd_t, gam_t = tables

    def body(dyb_ref, o_ref, q_ref, k_ref, v_ref, gb_ref, rs_ref, cos_ref, sin_ref, dm_ref, cd_ref, sd_ref, gam_ref,
             gain_ref, dr_ref, dgain_ref, dstate_ref):
        @pl.when(pl.program_id(0) == 0)
        def _():
            dstate_ref[...] = jnp.zeros_like(dstate_ref)
            dgain_ref[...] = jnp.zeros_like(dgain_ref)

        for cc, b, h in [(cc, b, h) for cc in reversed(range(RET_CHUNKS)) for b in range(B) for h in range(HEADS)]:
            rows = slice(cc * CHUNK, (cc + 1) * CHUNK)
            cos_t, sin_t = cos_ref[rows, :], sin_ref[rows, :]
            cols = slice(h * DK, (h + 1) * DK)
            gain = gain_ref[h]
            n, rs = _groupnorm(o_ref[b, rows, cols])
            gb = gb_ref[b, rows, cols]
            sg = _sigmoid(gb)
            dy = dyb_ref[b, rows, cols]
            part = lambda g: slice(g * D_MODEL + h * DK, g * D_MODEL + (h + 1) * DK)
            dr_ref[b, rows, part(3)] = (dy * (n * gain) * (sg * (1.0 + gb * (1.0 - sg)))).astype(dr_ref.dtype)
            dgn = dy * (gb * sg)
            dgain_ref[h] += jnp.sum(dgn * n, axis=0, keepdims=True)
            dn = dgn * gain
            do = rs * (dn - jnp.mean(dn, axis=-1, keepdims=True) - n * jnp.mean(dn * n, axis=-1, keepdims=True))

            qb = _c(_rot(q_ref[b, rows, cols], cos_t, sin_t))
            kb = _c(_rot(k_ref[b, rows, cols], cos_t, sin_t) * (DK ** -0.5))
            v = v_ref[b, rows, cols]
            vb = _c(v)
            vsb = _c(v * sd_ref[h])
            dob = _c(do)
            docb = _c(do * cd_ref[h])
            dmat = dm_ref[h]
            dstate = dstate_ref[b, h]
            dsb = _c(dstate)
            pb = _c(_dot_nt(qb, kb) * dmat)
            dsc = _c(_dot_nt(dob, vb) * dmat)
            dq = _dot(dsc, kb) + _dot_nt(docb, rs_ref[cc, b, h])
            dk = _dot_tn(dsc, qb) + _dot_nt(vsb, dsb)
            dv = _dot_tn(pb, dob) + _dot(kb, dsb) * sd_ref[h]
            dstate_ref[b, h] = gam_ref[h] * dstate + _dot_tn(qb, docb)
            dr_ref[b, rows, part(0)] = _rot_t(dq, cos_t, sin_t).astype(dr_ref.dtype)
            dr_ref[b, rows, part(1)] = (_rot_t(dk, cos_t, sin_t) * (DK ** -0.5)).astype(dr_ref.dtype)
            dr_ref[b, rows, part(2)] = dv.astype(dr_ref.dtype)

    n_steps = nc // RET_CHUNKS
    qkv, act, rope, dmat, dvec, hrow, rst = _ret_specs(B, lambda c: n_steps - 1 - c)
    wide = pl.BlockSpec((B, RET_CHUNKS * CHUNK, 4 * D_MODEL), lambda c: (0, n_steps - 1 - c, 0))
    proj3 = proj.reshape(B, S, proj.shape[1])
    dr, dgain = pl.pallas_call(
        _after(body, 14, deps),
        name="ret_bwd",
        grid=(n_steps,),
        in_specs=[act, act, qkv(2), qkv(3), qkv(4), qkv(5), rst, rope, rope, dmat, dvec, dvec, hrow, hrow]
        + [ANY_SPEC] * len(deps),
        out_specs=[wide, hrow],
        out_shape=[jax.ShapeDtypeStruct((B, S, 4 * D_MODEL), _MXU), jax.ShapeDtypeStruct((HEADS, 1, DK), F32)],
        scratch_shapes=[pltpu.VMEM((B, HEADS, DK, DK), F32)],
        compiler_params=_params(("arbitrary",)),
    )(dyb.reshape(B, S, D_MODEL), o_pre.reshape(B, S, D_MODEL), proj3, proj3, proj3, proj3, states, cos, sin, dmat_t,
      cd_t, sd_t, gam_t, gain3, *deps)
    return dr.reshape(T, 4 * D_MODEL), dgain


def _mid(ya, yb, proj, x2d, tgt2d, wpa, wpb, wout, g_fin):
    T = x2d.shape[0]
    tm = min(MID_TILE, T)
    n_steps = T // tm
    rows = D_MODEL // (2 * N_CHIPS)

    def body(ya_ref, yb_ref, ma_ref, mb_ref, x_ref, t_ref, gf_ref, wpa_hbm, wpb_hbm, wout_hbm,
             loss_ref, dx2_ref, dya_ref, dyb_ref, dm_ref, dgf_ref, gw_hbm, w_ref, acc_ref, sem):
        i = pl.program_id(0)

        @pl.when(i == 0)
        def _():
            loads = [pltpu.make_async_copy(src, w_ref.at[k], sem.at[k]) for k, src in enumerate((wpa_hbm, wpb_hbm, wout_hbm))]
            for cp in loads:
                cp.start()
            for cp in loads:
                cp.wait()
            acc_ref[...] = jnp.zeros_like(acc_ref)
            loss_ref[...] = jnp.zeros_like(loss_ref)
            dgf_ref[...] = jnp.zeros_like(dgf_ref)

        ya_t, yb_t = ya_ref[...], yb_ref[...]
        out_a = _dot(ya_t, w_ref[0])
        out_b = _dot(yb_t, w_ref[1])
        sa = _sigmoid(ma_ref[...])
        sb = _sigmoid(mb_ref[...])
        mgb = _c(sa * out_a + sb * out_b)
        x2 = x_ref[...] + _dot(mgb, w_ref[2])
        r2 = lax.rsqrt(jnp.mean(x2 * x2, axis=-1, keepdims=True) + EPS)
        nx = x2 * r2
        gf = gf_ref[...]
        err = nx * gf - t_ref[...]
        loss_ref[...] += 0.5 * jnp.sum(jnp.mean(err * err, axis=-1, keepdims=True), axis=0, keepdims=True)
        dy = err * (1.0 / D_MODEL)
        dgf_ref[...] += jnp.sum(dy * nx, axis=0, keepdims=True)
        dyg = dy * gf
        dx2 = r2 * (dyg - nx * jnp.mean(dyg * nx, axis=-1, keepdims=True))
        dx2_ref[...] = dx2
        dx2b = _c(dx2)
        dmg = _dot_nt(dx2b, w_ref[2])
        acc_ref[2] += _dot_tn(mgb, dx2b)
        dm_ref[:, :D_MODEL] = (dmg * out_a * sa * (1.0 - sa)).astype(dm_ref.dtype)
        dm_ref[:, D_MODEL:] = (dmg * out_b * sb * (1.0 - sb)).astype(dm_ref.dtype)
        dab = _c(dmg * sa)
        dbb = _c(dmg * sb)
        dya_ref[...] = _dot_nt(dab, w_ref[0])
        dyb_ref[...] = _dot_nt(dbb, w_ref[1])
        acc_ref[0] += _dot_tn(ya_t, dab)
        acc_ref[1] += _dot_tn(yb_t, dbb)

        @pl.when(i == n_steps - 1)
        def _():
            copies = [pltpu.make_async_copy(acc_ref.at[k, pl.ds((2 * p + hf) * rows, rows), :], gw_hbm.at[p, hf, k],
                                            sem.at[(k * N_CHIPS + p) * 2 + hf])
                      for k in range(3) for p in range(N_CHIPS) for hf in range(2)]
            for cp in copies:
                cp.start()
            for cp in copies:
                cp.wait()

    tile = lambda j: pl.BlockSpec((tm, D_MODEL), lambda i: (i, j))
    one = pl.BlockSpec((1, D_MODEL), lambda i: (0, 0))
    anyspec = pl.BlockSpec(memory_space=pl.ANY)
    return pl.pallas_call(
        body,
        name="mid",
        grid=(n_steps,),
        in_specs=[tile(0), tile(0), tile(6), tile(7), tile(0), tile(0), one, anyspec, anyspec, anyspec],
        out_specs=[pl.BlockSpec((1, 1), lambda i: (0, 0)), tile(0), tile(0), tile(0),
                   pl.BlockSpec((tm, 2 * D_MODEL), lambda i: (i, 0)), one, anyspec],
        out_shape=[
            jax.ShapeDtypeStruct((1, 1), F32),
            jax.ShapeDtypeStruct((T, D_MODEL), F32),
            jax.ShapeDtypeStruct((T, D_MODEL), F32),
            jax.ShapeDtypeStruct((T, D_MODEL), F32),
            jax.ShapeDtypeStruct((T, 2 * D_MODEL), _MXU),
            jax.ShapeDtypeStruct((1, D_MODEL), F32),
            jax.ShapeDtypeStruct((N_CHIPS, 2, 3, rows, D_MODEL), F32),
        ],
        scratch_shapes=[pltpu.VMEM((3, D_MODEL, D_MODEL), _MXU), pltpu.VMEM((3, D_MODEL, D_MODEL), F32),
                        pltpu.SemaphoreType.DMA((3 * N_CHIPS * 2,))],
        compiler_params=_params(("arbitrary",)),
    )(ya, yb, proj, proj, x2d, tgt2d, g_fin, wpa, wpb, wout)


def _inproj_bwd_dx(dparts, w_all, x2d, dx2, g_in, first, count, prev, name, deps=()):
    T = x2d.shape[0]
    tm = min(DX_TILE, T)
    n_d = len(dparts)
    groups = [(a, k) for a, d in enumerate(dparts) for k in range(d.shape[1] // D_MODEL)]
    dg_start = jnp.zeros((1, D_MODEL), F32) if prev is None else prev[1]
    carried = () if prev is None else (prev[0],)

    def body(*refs):
        d_refs = refs[:n_d]
        x_ref, dx2_ref, g_ref, dg0_ref, w_hbm = refs[n_d:n_d + 5]
        dx_ref, dg_ref, w_ref, sem = refs[-4:]

        @pl.when(pl.program_id(0) == 0)
        def _():
            cp = pltpu.make_async_copy(w_hbm, w_ref, sem)
            cp.start()
            cp.wait()
            dg_ref[...] = dg0_ref[...]

        dh = jnp.zeros((tm, D_MODEL), F32)
        for j, (a, k) in enumerate(groups):
            dh = dh + _dot_nt(d_refs[a][:, k * D_MODEL:(k + 1) * D_MODEL],
                              w_ref[j // 2, :, (j % 2) * D_MODEL:(j % 2 + 1) * D_MODEL])
        x = x_ref[...]
        r = lax.rsqrt(jnp.mean(x * x, axis=-1, keepdims=True) + EPS)
        nx = x * r
        dg_ref[...] += jnp.sum(dh * nx, axis=0, keepdims=True)
        dhg = dh * g_ref[...]
        dx_ref[...] = dx2_ref[...] + r * (dhg - nx * jnp.mean(dhg * nx, axis=-1, keepdims=True))

    tile = pl.BlockSpec((tm, D_MODEL), lambda i: (first + i, 0))
    one = pl.BlockSpec((1, D_MODEL), lambda i: (0, 0))
    return pl.pallas_call(
        body,
        name=name,
        grid=(count,),
        in_specs=[pl.BlockSpec((tm, d.shape[1]), lambda i: (first + i, 0)) for d in dparts]
        + [tile, tile, one, one, ANY_SPEC] + [ANY_SPEC] * (len(carried) + len(deps)),
        out_specs=[tile, one],
        out_shape=[jax.ShapeDtypeStruct((T, D_MODEL), F32), jax.ShapeDtypeStruct((1, D_MODEL), F32)],
        input_output_aliases={n_d + 5: 0} if carried else {},
        scratch_shapes=[pltpu.VMEM(w_all.shape, w_all.dtype), pltpu.SemaphoreType.DMA],
        compiler_params=_params(("arbitrary",)),
    )(*dparts, x2d, dx2, g_in, dg_start, w_all, *carried, *deps)


def _inproj_bwd_dw(ht, dparts, name, deps=()):
    T = ht.shape[1]
    tn = DW_COLS
    half = D_MODEL // 2
    per_chip = 2 * D_MODEL // tn
    n_d = len(dparts)
    tiles = [(a, t) for a, d in enumerate(dparts) for t in range(d.shape[1] // tn)]
    offs = [sum(d.shape[1] // tn for d in dparts[:a]) for a in range(n_d)]

    def body(*refs):
        ht_ref = refs[0]
        d_refs = refs[1:1 + n_d]
        out_ref = refs[-1]
        t = pl.program_id(0)

        for a in range(n_d):
            lo, hi = offs[a], offs[a] + dparts[a].shape[1] // tn

            @pl.when((t >= lo) & (t < hi))
            def _(a=a):
                g = _dot(ht_ref[...], d_refs[a][...])
                out_ref[0, 0] = g[:half]
                out_ref[0, 1] = g[half:]

    def dspec(a):
        n_a = dparts[a].shape[1] // tn
        return pl.BlockSpec((T, tn), lambda t: (0, jnp.clip(t - offs[a], 0, n_a - 1)))

    return pl.pallas_call(
        body,
        name=name,
        grid=(len(tiles),),
        in_specs=[pl.BlockSpec((D_MODEL, T), lambda t: (0, 0))] + [dspec(a) for a in range(n_d)]
        + [ANY_SPEC] * len(deps),
        out_specs=pl.BlockSpec((1, 2, half, tn), lambda t: (t // per_chip, 0, 0, t % per_chip)),
        out_shape=jax.ShapeDtypeStruct((len(tiles) // per_chip, 2, half, 2 * D_MODEL), F32),
        compiler_params=_params(("parallel",)),
    )(ht, *dparts, *deps)


def _coords():
    return lax.axis_index("x"), lax.axis_index("y"), lax.axis_index("c")


def _other_chips(x, y):
    return [(1 - x, y), (x, 1 - y), (1 - x, 1 - y)]


def _chunks(rows, n):
    size = rows // n
    return [pl.ds(q * size, size) for q in range(n)]


HBM_SPEC = pl.BlockSpec(memory_space=pltpu.HBM)
SEM_SPEC = pl.BlockSpec(memory_space=pltpu.SEMAPHORE)
DATAFLOW = pltpu.SideEffectType.DATAFLOW_SIDE_EFFECTING


def _copies_start(bufs, plan, n_copies, name, deps=()):
    n = len(bufs)
    n_deps = len(deps)

    def body(*refs):
        ins = refs[:n]
        send_sems, recv_sems = refs[n + n_deps], refs[n + n_deps + 1]
        token = refs[-1]
        for k, send, _ in plan(ins):
            if send is not None:
                src, dst, dev, pred = send
                cp = pltpu.make_async_remote_copy(src_ref=src, dst_ref=dst, send_sem=send_sems.at[k],
                                                  recv_sem=recv_sems.at[k], device_id=dev, device_id_type=MESH)
                if pred is None:
                    cp.start()
                else:
                    pl.when(pred)(cp.start)
        token[...] = jnp.zeros_like(token)

    hbm = [pltpu.with_memory_space_constraint(b, pltpu.HBM) for b in bufs]
    outs = pl.pallas_call(
        body,
        name=name,
        in_specs=[HBM_SPEC] * n + [ANY_SPEC] * n_deps,
        out_specs=(SEM_SPEC, SEM_SPEC, *([HBM_SPEC] * n), pl.BlockSpec(memory_space=pltpu.VMEM)),
        out_shape=(pltpu.SemaphoreType.DMA((n_copies,)), pltpu.SemaphoreType.DMA((n_copies,)),
                   *[pltpu.HBM(b.shape, b.dtype) for b in bufs], jax.ShapeDtypeStruct((8, 128), F32)),
        input_output_aliases={a: 2 + a for a in range(n)},
        compiler_params=pltpu.CompilerParams(has_side_effects=DATAFLOW),
    )(*hbm, *deps)
    return outs[0], outs[1], list(outs[2:2 + n]), outs[-1]


def _copies_wait(send_sems, recv_sems, bufs, after, plan, name, only=None):
    n = len(bufs)

    def body(*refs):
        ins = refs[:n]
        s_sems, r_sems = refs[n], refs[n + 1]
        for k, send, recv in plan(ins):
            if only is not None and k not in only:
                continue
            if send is not None:
                src, dst, dev, pred = send
                cp = pltpu.make_async_remote_copy(src_ref=src, dst_ref=dst, send_sem=s_sems.at[k],
                                                  recv_sem=r_sems.at[k], device_id=dev, device_id_type=MESH)
                if pred is None:
                    cp.wait_send()
                else:
                    pl.when(pred)(cp.wait_send)
            if recv is not None:
                dst, pred = recv
                cp = pltpu.make_async_remote_copy(src_ref=dst, dst_ref=dst, send_sem=s_sems.at[k],
                                                  recv_sem=r_sems.at[k], device_id=_coords(), device_id_type=MESH)
                if pred is None:
                    cp.wait_recv()
                else:
                    pl.when(pred)(cp.wait_recv)

    outs = pl.pallas_call(
        body,
        name=name,
        in_specs=[HBM_SPEC] * n + [SEM_SPEC, SEM_SPEC, pl.BlockSpec(memory_space=pl.ANY)],
        out_specs=[HBM_SPEC] * n,
        out_shape=[pltpu.HBM(b.shape, b.dtype) for b in bufs],
        input_output_aliases={a: a for a in range(n)},
        compiler_params=pltpu.CompilerParams(has_side_effects=DATAFLOW),
    )(*bufs, send_sems, recv_sems, after)
    return list(outs)


def _gather_plan(n_bufs):
    def plan(refs):
        x, y, c = _coords()
        me = 2 * x + y
        out = []
        for k, (px, py) in enumerate(_other_chips(x, y)):
            for a in range(n_bufs):
                out.append((k * n_bufs + a, (refs[a].at[me], refs[a].at[me], (px, py, c), None),
                            (refs[a].at[2 * px + py], None)))
        return out
    return plan


def _cast_into_slot(ws, name, deps=()):
    n = len(ws)
    nt = 2

    def body(s_ref, *refs):
        outs = refs[len(refs) - n:]
        for a in range(n):
            outs[a][0] = refs[a][...].astype(outs[a].dtype)

    xi, yi, _ = _coords()
    return pl.pallas_call(
        body,
        name=name,
        grid_spec=pltpu.PrefetchScalarGridSpec(
            num_scalar_prefetch=1,
            grid=(2, nt),
            in_specs=[pl.BlockSpec((1, w.shape[1] // nt, w.shape[2]), lambda hf, i, s: (hf, i, 0)) for w in ws]
            + [ANY_SPEC] * len(deps),
            out_specs=[pl.BlockSpec((1, 1, w.shape[1] // nt, w.shape[2]), lambda hf, i, s: (s[0], hf, i, 0)) for w in ws],
        ),
        out_shape=[jax.ShapeDtypeStruct((N_CHIPS,) + w.shape, _MXU) for w in ws],
        compiler_params=_params(("parallel", "parallel")),
    )((2 * xi + yi).reshape(1).astype(jnp.int32), *ws, *deps)


def _chip_gather_plan(stage, n_bufs):
    def plan(refs):
        x, y, c = _coords()
        me = 2 * x + y
        near = [(1 - x, y), (x, 1 - y)]
        slots = [2 * (1 - x) + y, 2 * x + (1 - y), 2 * (1 - x) + (1 - y)]
        sibling = (x, y, 1 - c)
        pass_to = (jnp.where(c == 0, x, 1 - x), jnp.where(c == 0, 1 - y, y), c)
        pass_slot = jnp.where(c == 0, slots[0], slots[1])
        out = []

        def move(src_slot, to, land_slot, land_core, pieces):
            for a, buf in enumerate(refs):
                for rows in _chunks(buf.shape[2], pieces[a]):
                    out.append((len(out), (buf.at[src_slot, c, rows], buf.at[src_slot, c, rows], to, None),
                                (buf.at[land_slot, land_core, rows], None)))

        if stage == "near":
            for k, chip in enumerate(near):
                move(me, (*chip, c), slots[k], c, NEAR_PIECES[:n_bufs])
        elif stage == "pass":
            move(pass_slot, pass_to, slots[2], c, PASS_PIECES[:n_bufs])
            for k in range(2):
                move(slots[k], sibling, slots[k], 1 - c, [1] * n_bufs)
        else:
            move(slots[2], sibling, slots[2], 1 - c, [1] * n_bufs)
        return out
    return plan


NEAR_PIECES = (2, 1)
PASS_PIECES = (2, 1)


def _chip_gather_copies(stage, n_bufs):
    if stage == "near":
        return 2 * sum(NEAR_PIECES[:n_bufs]), None
    if stage == "pass":
        n_pass = sum(PASS_PIECES[:n_bufs])
        return n_pass + 2 * n_bufs, set(range(n_pass))
    return n_bufs, None


def _swap_plan(n_slabs):
    def plan(refs):
        x, y, c = _coords()
        out, k = [], 0
        for i, n in enumerate(n_slabs):
            g, land = refs[2 * i], refs[2 * i + 1]
            for p in range(n):
                out.append((k, (g.at[p, 1 - c], land.at[p], (x, y, 1 - c), None), (land.at[p], None)))
                k += 1
        return out
    return plan


def _is_one_of(chip, dests):
    hit = chip == dests[0]
    for d in dests[1:]:
        hit = hit | (chip == d)
    return hit


def _slab_of(chip, dests):
    return sum(j * (chip == d).astype(jnp.int32) for j, d in enumerate(dests))


def _scatter_plan(dest_sets):
    def plan(refs):
        x, y, c = _coords()
        me = 2 * x + y
        out = []
        for k, (px, py) in enumerate(_other_chips(x, y)):
            peer = 2 * px + py
            for i, dests in enumerate(dest_sets):
                cs, land = refs[2 * i], refs[2 * i + 1]
                everyone = len(dests) == N_CHIPS
                send = (cs.at[_slab_of(peer, dests)], land.at[k], (px, py, c),
                        None if everyone else _is_one_of(peer, dests))
                recv = (land.at[k], None if everyone else _is_one_of(me, dests))
                out.append((k * len(dest_sets) + i, send, recv))
        return out
    return plan


def _join_plan(rows, n_pieces):
    def plan(refs):
        x, y, c = _coords()
        (buf,) = refs
        return [(i, (buf.at[c, piece], buf.at[c, piece], (x, y, 1 - c), None), (buf.at[1 - c, piece], None))
                for i, piece in enumerate(_chunks(rows, n_pieces))]
    return plan


def _join_plans(parts):
    def plan(refs):
        out, b0, k0 = [], 0, 0
        for part_plan, n_bufs, n_copies in parts:
            out += [(k0 + k, send, recv) for k, send, recv in part_plan(refs[b0:b0 + n_bufs])]
            b0 += n_bufs
            k0 += n_copies
        return out
    return plan


def _allgather_plan():
    def plan(refs):
        x, y, c = _coords()
        (land,) = refs
        me = 4 * x + 2 * y + c
        out = []
        for r in range(1, 8):
            px = 1 - x if r & 4 else x
            py = 1 - y if r & 2 else y
            pc = 1 - c if r & 1 else c
            out.append((r - 1, (land.at[me], land.at[me], (px, py, pc), None), (land.at[4 * px + 2 * py + pc], None)))
        return out
    return plan


def _sum_gathered(land, name):
    def body(land_ref, o_ref):
        acc = land_ref[0]
        for d in range(1, 8):
            acc = acc + land_ref[d]
        o_ref[...] = acc

    return pl.pallas_call(
        body,
        name=name,
        out_shape=jax.ShapeDtypeStruct(land.shape[1:], F32),
        compiler_params=_params(),
    )(land)


def _row_tile(rows, cap):
    t = cap
    while rows % t:
        t //= 2
    return t


def _add_my_half(g, r, name):
    n_slabs, _, R, C = g.shape
    tr = R if n_slabs > 1 else _row_tile(R, SUM_ROWS)

    def body(c_ref, g_ref, r_ref, o_ref):
        o_ref[...] = (g_ref[0] + r_ref[...]).astype(o_ref.dtype)

    return pl.pallas_call(
        body,
        name=name,
        grid_spec=pltpu.PrefetchScalarGridSpec(
            num_scalar_prefetch=1,
            grid=(n_slabs, R // tr),
            in_specs=[pl.BlockSpec((1, 1, tr, C), lambda p, i, c_ref: (p, c_ref[0], i, 0)),
                      pl.BlockSpec((1, tr, C), lambda p, i, c_ref: (p, i, 0))],
            out_specs=pl.BlockSpec((1, tr, C), lambda p, i, c_ref: (p, i, 0)),
        ),
        out_shape=jax.ShapeDtypeStruct(r.shape, jnp.bfloat16),
        compiler_params=_params(("parallel", "parallel")),
    )(lax.axis_index("c").reshape(1).astype(jnp.int32), g, r)


def _sum_slabs(own, got, name, deps=()):
    _, R, C = own.shape
    tr = _row_tile(R, SUM_ROWS)

    def body(s_ref, own_ref, got_ref, *rest):
        rest[-1][0] = ((own_ref[0].astype(F32) + got_ref[0].astype(F32)) + got_ref[1].astype(F32)) + got_ref[2].astype(F32)

    xi, yi, ci = _coords()
    return pl.pallas_call(
        body,
        name=name,
        grid_spec=pltpu.PrefetchScalarGridSpec(
            num_scalar_prefetch=1,
            grid=(R // tr,),
            in_specs=[pl.BlockSpec((1, tr, C), lambda i, s: (s[0], i, 0)),
                      pl.BlockSpec((3, tr, C), lambda i, s: (0, i, 0))] + [ANY_SPEC] * len(deps),
            out_specs=pl.BlockSpec((1, tr, C), lambda i, s: (s[1], i, 0)),
        ),
        out_shape=jax.ShapeDtypeStruct((2, R, C), F32),
        compiler_params=_params(("parallel",)),
    )(jnp.stack([2 * xi + yi, ci]).astype(jnp.int32), own, got, *deps)


def _sum_parts(owns, got, dest_sets, name):
    n = len(owns)
    _, R, C = owns[0].shape
    tr = _row_tile(R, SUM_ROWS)

    def body(s_ref, *refs):
        got_ref, o_ref = refs[n], refs[-1]
        total = jnp.zeros((tr, C), F32)
        for i in range(n):
            total = total + jnp.where(s_ref[2 + 2 * i] == 1, refs[i][0].astype(F32), 0.0)
        o_ref[0] = ((total + got_ref[0].astype(F32)) + got_ref[1].astype(F32)) + got_ref[2].astype(F32)

    xi, yi, ci = _coords()
    me = 2 * xi + yi
    scalars = [ci, ci]
    for dests in dest_sets:
        scalars += [_is_one_of(me, dests).astype(jnp.int32), _slab_of(me, dests)]
    own_spec = lambda i: pl.BlockSpec((1, tr, C), lambda r, s: (s[3 + 2 * i], r, 0))
    return pl.pallas_call(
        body,
        name=name,
        grid_spec=pltpu.PrefetchScalarGridSpec(
            num_scalar_prefetch=1,
            grid=(R // tr,),
            in_specs=[own_spec(i) for i in range(n)] + [pl.BlockSpec((3, tr, C), lambda r, s: (0, r, 0))],
            out_specs=pl.BlockSpec((1, tr, C), lambda r, s: (s[0], r, 0)),
        ),
        out_shape=jax.ShapeDtypeStruct((2, R, C), F32),
        compiler_params=_params(("parallel",)),
    )(jnp.stack(scalars).astype(jnp.int32), *owns, got)


def _adamw_math(w, g, m, v):
    m = ADAM_B1 * m + (1.0 - ADAM_B1) * g
    v = ADAM_B2 * v + (1.0 - ADAM_B2) * (g * g)
    m_hat = m / (1.0 - ADAM_B1 ** ADAM_STEP)
    v_hat = v / (1.0 - ADAM_B2 ** ADAM_STEP)
    delta = -ADAM_LR * (m_hat / (jnp.sqrt(v_hat) + ADAM_EPS) + ADAM_WD * w)
    return delta, m, v


def _adamw_halves(ws, g, ms, vs, half, prev, name, deps=()):
    n = len(ws)
    _, _, R, C = g.shape
    tr = _row_tile(R, ADAMW_ROWS)
    steps = R // tr
    carried = [] if prev is None else [a for four in prev for a in four]
    both = half is None
    which = (lambda i, s: i // steps) if both else (lambda i, s: s[0])
    half = 0 if both else half

    def body(s_ref, *refs):
        w_refs, g_refs, m_refs, v_refs = (refs[k * n:(k + 1) * n] for k in range(4))
        outs = refs[len(refs) - 4 * n:]
        for a in range(n):
            grad = g_refs[a][0, 0]
            d, mn, vn = _adamw_math(w_refs[a][...], grad, m_refs[a][...], v_refs[a][...])
            for o, val in zip(outs[4 * a:4 * a + 4], (grad, d, mn, vn)):
                o[...] = val

    rows = pl.BlockSpec((tr, C), lambda i, s: (which(i, s) * steps + i % steps, 0))
    grad_spec = lambda a: pl.BlockSpec((1, 1, tr, C), lambda i, s: (which(i, s), a, i % steps, 0))
    n_in = 4 * n
    outs = pl.pallas_call(
        body,
        name=name,
        grid_spec=pltpu.PrefetchScalarGridSpec(
            num_scalar_prefetch=1,
            grid=(2 * steps if both else steps,),
            in_specs=[rows] * n + [grad_spec(a) for a in range(n)] + [rows] * (2 * n)
            + [ANY_SPEC] * (len(carried) + len(deps)),
            out_specs=[rows] * (4 * n),
        ),
        out_shape=[jax.ShapeDtypeStruct((2 * R, C), F32)] * (4 * n),
        input_output_aliases={1 + n_in + k: k for k in range(len(carried))},
        compiler_params=_params(("parallel",)),
    )(jnp.reshape(half, (1,)).astype(jnp.int32), *ws, *([g] * n), *ms, *vs, *carried, *deps)
    return [outs[4 * a:4 * a + 4] for a in range(n)]


def _adamw_small(ws, gs, ms, vs, name):
    n = len(ws)

    def body(*refs):
        for a in range(n):
            d, mn, vn = _adamw_math(refs[a][...], refs[n + a][...], refs[2 * n + a][...], refs[3 * n + a][...])
            refs[4 * n + a][...] = d
            refs[5 * n + a][...] = mn
            refs[6 * n + a][...] = vn

    shapes = [jax.ShapeDtypeStruct(w.shape, F32) for w in ws]
    outs = pl.pallas_call(
        body,
        name=name,
        out_shape=shapes * 3,
        compiler_params=_params(),
    )(*ws, *gs, *ms, *vs)
    return outs[:n], outs[n:2 * n], outs[2 * n:]


def _to_blockdiag(w):
    per = CW // LRU_BW
    w4 = w.reshape(N_CT, per, LRU_BW, LRU_BW)
    eye = jnp.eye(per, dtype=w.dtype)
    return (w4[:, :, :, None, :] * eye[None, :, None, :, None]).reshape(N_CT, CW, CW)


def _from_blockdiag(g):
    per = CW // LRU_BW
    g5 = g.reshape(N_CT, per, LRU_BW, per, LRU_BW)
    return jnp.stack([g5[:, b, :, b, :] for b in range(per)], axis=1).reshape(LRU_BLOCKS, LRU_BW, LRU_BW)


def _local_grads(x2d, tgt2d, B, S, g_in, in_proj, conv_b, gate_x_w, gate_x_b, gate_a_w, gate_a_b, lam,
                 proj_weights, g_fin, reduce):
    wx_bd = _c(_to_blockdiag(gate_x_w))
    wa_bd = _c(_to_blockdiag(gate_a_w))
    tables = _retention_tables(S)

    proj, ht, w_all, conv_w, gain = in_proj(x2d, g_in, (*tables, wx_bd, wa_bd))
    gain3 = gain.reshape(HEADS, 1, DK)
    hlru, ya = _lru_fwd(proj, conv_w, conv_b, wx_bd, wa_bd, gate_x_b, gate_a_b, lam, B, S)
    o_pre, yb, states = _ret_fwd(proj, tables, gain3, B, S)
    wpa, wpb, wout = proj_weights(yb)
    loss, dx2, dya, dyb, dm, dgf, gw_proj = _mid(ya, yb, proj, x2d, tgt2d, wpa, wpb, wout, g_fin)
    g3 = _inproj_bwd_dw(ht, [dm], "inproj_bwd_dw_m")
    deps = reduce.m_ready(gw_proj, g3)
    dr, dgain = _ret_bwd(dyb, o_pre, proj, states, tables, gain3, B, S, deps)
    deps = reduce.ret_done(dr)
    g12 = _inproj_bwd_dw(ht, [dr], "inproj_bwd_dw_r", deps)
    deps = reduce.r_ready(g12)
    dxa, dga, dcw, dcb, dwx_bd, dwa_bd, dbx, dba, dlam = _lru_bwd(
        dya, proj, hlru, conv_w, conv_b, wx_bd, wa_bd, gate_x_b, gate_a_b, lam, B, S, deps)
    small = dict(conv_w=dcw, conv_b=dcb, gate_x_w=_from_blockdiag(dwx_bd), gate_x_b=dbx,
                 gate_a_w=_from_blockdiag(dwa_bd), gate_a_b=dba, lru_lambda=dlam, gn_gain=dgain.reshape(HEADS, DK),
                 norm_final=dgf)
    loss_rows = jnp.broadcast_to(loss, (SUBLANES, LANES))
    deps = reduce.lru_done(dxa, jnp.concatenate([_pack_small(small), loss_rows], axis=0))
    g0 = _inproj_bwd_dw(ht, [dxa, dga], "inproj_bwd_dw_a", deps)
    deps = reduce.a_ready(g0)
    n_tiles = x2d.shape[0] // min(DX_TILE, x2d.shape[0])
    grad_x, dgin = _inproj_bwd_dx([dxa, dga, dr, dm], w_all, x2d, dx2, g_in, 0, n_tiles, None, "inproj_bwd_dx", deps)
    return grad_x, dgin


ALL_CHIPS = (0, 1, 2, 3)


class _GradReduce:
    def __init__(self, proj_done):
        self.pending = {}
        self.proj_done = proj_done
        self.land_in = None

    def _start(self, key, parts, name):
        bufs, plans, shared = [], [], None
        for part_bufs, plan, n_copies, part_shared in parts:
            if part_shared is not None:
                shared = len(bufs) + part_shared
            plans.append((plan, len(part_bufs), n_copies))
            bufs += part_bufs
        plan = _join_plans(plans)
        send_sems, recv_sems, bufs, token = _copies_start(bufs, plan, sum(p[2] for p in plans), name + "_start")
        if shared is not None:
            self.land_in = bufs[shared]
        self.pending[key] = (send_sems, recv_sems, bufs, plan, name + "_wait", shared)
        return (token,)

    def _finish(self, key, after):
        send_sems, recv_sems, bufs, plan, name, shared = self.pending.pop(key)
        if shared is not None:
            bufs[shared] = self.land_in
        bufs = _copies_wait(send_sems, recv_sems, bufs, after, plan, name)
        if shared is not None:
            self.land_in = bufs[shared]
        return bufs

    @staticmethod
    def _swap(pieces):
        bufs = []
        for g in pieces:
            bufs += [g, lax.empty((g.shape[0],) + g.shape[2:], F32)]
        n_slabs = [g.shape[0] for g in pieces]
        return bufs, _swap_plan(n_slabs), sum(n_slabs), None

    def _scatter(self, sums, dest_sets):
        bufs = []
        for cs in sums:
            bufs += [cs, lax.empty((3,) + cs.shape[1:], cs.dtype)]
        if self.land_in is not None:
            bufs[-1] = self.land_in
        return bufs, _scatter_plan(dest_sets), 3 * len(sums), len(bufs) - 1

    @staticmethod
    def _gather8(block):
        x, y, c = _coords()
        land = lax.dynamic_update_slice(lax.empty((8,) + block.shape, F32), block[None], (4 * x + 2 * y + c, 0, 0))
        return [land], _allgather_plan(), 7, None

    def m_ready(self, gw_proj, g3):
        rows = gw_proj.shape[2] * gw_proj.shape[3]
        return self._start("m", [self._swap([gw_proj.reshape(N_CHIPS, 2, rows, D_MODEL), g3])], "swap_m")

    def ret_done(self, after):
        proj, land_p, g3, land_3 = self._finish("m", after)
        sums_m = [_add_my_half(proj, land_p, "chip_sum_proj"), _add_my_half(g3, land_3, "chip_sum_m")]
        return self._start("sm", [self._scatter(sums_m, [ALL_CHIPS, (3,)])], "scatter_m")

    def r_ready(self, g12):
        return self._start("r", [self._swap([g12])], "swap_r")

    def lru_done(self, after, packed):
        g12, land_12 = self._finish("r", after)
        sums_r = [_add_my_half(g12, land_12, "chip_sum_r")]
        return (self._start("sr", [self._scatter(sums_r, [(1, 2)])], "scatter_r")
                + self._start("small", [self._gather8(packed)], "gather_small"))

    def a_ready(self, g0):
        (token,) = self._start("a", [self._swap([g0])], "swap_a")
        csp, gotp, self.cs3, _ = self._finish("sm", token)
        half_proj = _sum_slabs(csp, gotp, "sum_w_proj")
        g0, land_0 = self._finish("a", half_proj)
        join = ([half_proj], _join_plan(half_proj.shape[1], PROJ_JOIN_PIECES), PROJ_JOIN_PIECES, None)
        return self._start("sa", [self._scatter([_add_my_half(g0, land_0, "chip_sum_a")], [(0,)]), join], "scatter_a")

    def finish(self, dgin, w_in_done):
        (token,) = self._start("n", [self._gather8(dgin)], "gather_norm_in")
        (small,) = self._finish("small", token)
        cs12, _ = self._finish("sr", token)
        cs0, _, g_proj = self._finish("sa", token)
        self.proj_done(g_proj)
        half_in =_sum_parts([self.cs3, cs12, cs0], self.land_in, [(3,), (1, 2), (0,)], "sum_w_in")
        deps = self._start("j", [([half_in], _join_plan(half_in.shape[1], JOIN_PIECES), JOIN_PIECES, None)], "join_w_in")
        first = w_in_done(self.pending["j"][2][0], True, None, deps)
        (g_in,) = self._finish("j", first[1])
        done = w_in_done(g_in, False, first, ())
        (norm_in,) = self._finish("n", done[1])
        return _sum_gathered(small, "sum_small_grads"), _sum_gathered(norm_in, "sum_norm_in_grad")


_SMALL = ("gate_x_w", "gate_a_w", "conv_w", "conv_b", "gate_x_b", "gate_a_b", "lru_lambda", "gn_gain", "norm_final")
_SMALL_SHAPES = dict(gate_x_w=(LRU_BLOCKS, LRU_BW, LRU_BW), gate_a_w=(LRU_BLOCKS, LRU_BW, LRU_BW),
                     norm_in=(1, D_MODEL), conv_w=(CONV, D_MODEL), conv_b=(1, D_MODEL), gate_x_b=(1, D_MODEL),
                     gate_a_b=(1, D_MODEL), lru_lambda=(1, D_MODEL), gn_gain=(HEADS, DK), norm_final=(1, D_MODEL))


def _pack_small(small):
    return jnp.concatenate([small[k].reshape(-1, 128) for k in _SMALL], axis=0)


def _unpack_small(packed):
    out, r = {}, 0
    for k in _SMALL:
        shape = _SMALL_SHAPES[k]
        rows = 1
        for s in shape:
            rows *= s
        rows //= 128
        out[k] = packed[r:r + rows].reshape(shape)
        r += rows
    return out


def kernel(x, norm_in, w_in, conv_w, conv_b, gate_x_w, gate_x_b, gate_a_w, gate_a_b, lru_lambda, gn_gain, w_proj_a, w_proj_b, w_out, norm_final, loss_target, m_norm_in, m_w_in, m_conv_w, m_conv_b, m_gate_x_w, m_gate_x_b, m_gate_a_w, m_gate_a_b, m_lru_lambda, m_gn_gain, m_w_proj_a, m_w_proj_b, m_w_out, m_norm_final, v_norm_in, v_w_in, v_conv_w, v_conv_b, v_gate_x_w, v_gate_x_b, v_gate_a_w, v_gate_a_b, v_lru_lambda, v_gn_gain, v_w_proj_a, v_w_proj_b, v_w_out, v_norm_final):
    B, S, _ = x.shape
    T = B * S
    xi, yi, ci = _coords()
    chip = 2 * xi + yi

    cshard = D_MODEL // N_CHIPS
    mine = _cast_into_slot([w_in[0].reshape(2, D_MODEL // 2, 2 * D_MODEL)], "cast_w_in")
    plan = _gather_plan(3)
    pending_proj = []
    gshard = DK // N_CHIPS
    tiny = jnp.concatenate([conv_w[0], jnp.zeros((4, cshard), F32), jnp.pad(gn_gain[0], ((0, 4), (0, cshard - gshard)))],
                           axis=0).reshape(1, 2, SUBLANES, cshard)
    tiny_buf = lax.dynamic_update_slice(lax.empty((N_CHIPS, 2, SUBLANES, cshard), F32), tiny, (chip, 0, 0, 0))
    near_plan, pass_plan, far_plan = (_chip_gather_plan(stage, 2) for stage in ("near", "pass", "far"))
    (n_near, _), (n_pass, passed_on), (n_far, _) = (_chip_gather_copies(stage, 2) for stage in ("near", "pass", "far"))
    halves = set(range(n_pass)) - passed_on
    near_s, near_r, bufs, near_token = _copies_start([mine[0], tiny_buf], near_plan, n_near, "gather_near_start")

    def in_proj(x2d, g_in, meanwhile):
        mine_proj = _cast_into_slot([w[0].reshape(2, cshard // 2, D_MODEL) for w in (w_proj_a, w_proj_b, w_out)],
                                    "cast_w_proj", (near_token,))
        as_w = lambda b: b[0].reshape(N_CHIPS, D_MODEL, 2 * D_MODEL)
        slot_x, slot_y, slot_d = 2 * (1 - xi) + yi, 2 * xi + (1 - yi), 2 * (1 - xi) + (1 - yi)
        ids = lambda *chips: jnp.stack(chips).astype(jnp.int32)
        proj, hb, ht = _inproj_first(x2d, g_in, as_w(bufs), ids(chip), "inproj_own", (near_token, *meanwhile))
        got = _copies_wait(near_s, near_r, bufs, proj, near_plan, "gather_near_wait")
        pass_s, pass_r, got, pass_token = _copies_start(got, pass_plan, n_pass, "gather_pass_start")
        got = _copies_wait(pass_s, pass_r, got, pass_token, pass_plan, "gather_pass_wait_halves", only=halves)
        proj = _inproj_more(hb, as_w(got), ids(slot_x, slot_y), proj, "inproj_near")
        got = _copies_wait(pass_s, pass_r, got, proj, pass_plan, "gather_pass_wait_far", only=passed_on)
        pending_proj.append(_copies_start(mine_proj, plan, 9, "gather_proj_start", (got[0],)))
        far_s, far_r, got, far_token = _copies_start(got, far_plan, n_far, "gather_far_start")
        got = _copies_wait(far_s, far_r, got, far_token, far_plan, "gather_far_wait")
        proj = _inproj_more(hb, as_w(got), ids(slot_d), proj, "inproj_far")
        tiny_all = got[1].reshape(N_CHIPS, 2 * SUBLANES, cshard)
        conv_w_full = jnp.transpose(tiny_all[:, 0:CONV, :], (1, 0, 2)).reshape(CONV, D_MODEL)
        gain_full = jnp.transpose(tiny_all[:, 8:8 + HEADS, :gshard], (1, 0, 2)).reshape(HEADS, DK)
        return proj, ht, as_w(got), conv_w_full, gain_full

    def proj_weights(after):
        s_sems, r_sems, pbufs, _ = pending_proj[0]
        got = _copies_wait(s_sems, r_sems, pbufs, after, plan, "gather_proj_wait")
        return [b.reshape(D_MODEL, D_MODEL) for b in got]

    weights = dict(norm_in=norm_in, w_in=w_in, conv_w=conv_w, conv_b=conv_b, gate_x_w=gate_x_w, gate_x_b=gate_x_b,
                   gate_a_w=gate_a_w, gate_a_b=gate_a_b, lru_lambda=lru_lambda, gn_gain=gn_gain, w_proj_a=w_proj_a,
                   w_proj_b=w_proj_b, w_out=w_out, norm_final=norm_final)
    ms = dict(norm_in=m_norm_in, w_in=m_w_in, conv_w=m_conv_w, conv_b=m_conv_b, gate_x_w=m_gate_x_w,
              gate_x_b=m_gate_x_b, gate_a_w=m_gate_a_w, gate_a_b=m_gate_a_b, lru_lambda=m_lru_lambda, gn_gain=m_gn_gain,
              w_proj_a=m_w_proj_a, w_proj_b=m_w_proj_b, w_out=m_w_out, norm_final=m_norm_final)
    vs = dict(norm_in=v_norm_in, w_in=v_w_in, conv_w=v_conv_w, conv_b=v_conv_b, gate_x_w=v_gate_x_w,
              gate_x_b=v_gate_x_b, gate_a_w=v_gate_a_w, gate_a_b=v_gate_a_b, lru_lambda=v_lru_lambda, gn_gain=v_gn_gain,
              w_proj_a=v_w_proj_a, w_proj_b=v_w_proj_b, w_out=v_w_out, norm_final=v_norm_final)
    names = list(weights)
    grads, delta, new_m, new_v = {}, {}, {}, {}

    def update_big(keys, g, half, prev, name, deps=()):
        two = lambda a: a.reshape(a.shape[1], a.shape[2])
        res = _adamw_halves([two(weights[k]) for k in keys], g, [two(ms[k]) for k in keys], [two(vs[k]) for k in keys],
                            half, prev, name, deps)
        for k, (gk, d, mn, vn) in zip(keys, res):
            shp = weights[k].shape
            grads[k], delta[k], new_m[k], new_v[k] = gk.reshape(shp), d.reshape(shp), mn.reshape(shp), vn.reshape(shp)
        return res

    def proj_done(g_proj):
        g4 = g_proj.reshape(2, 3, D_MODEL // (2 * N_CHIPS), D_MODEL)
        return update_big(("w_proj_a", "w_proj_b", "w_out"), g4, None, None, "adamw_proj")[-1][1]

    def w_in_done(g_in, own, prev, deps):
        g4 = g_in.reshape(2, 1, D_MODEL // 2, 2 * D_MODEL)
        return update_big(("w_in",), g4, ci if own else 1 - ci, None if prev is None else [prev],
                          "adamw_w_in_own" if own else "adamw_w_in_other", deps)[0]

    reduce = _GradReduce(proj_done)
    grad_x, dgin = _local_grads(
        x.reshape(T, D_MODEL), loss_target.reshape(T, D_MODEL), B, S, norm_in, in_proj, conv_b,
        gate_x_w[0], gate_x_b, gate_a_w[0], gate_a_b, lru_lambda, proj_weights,
        norm_final.reshape(1, D_MODEL), reduce)

    small_sum, g_norm_in = reduce.finish(dgin.reshape(SUBLANES, LANES), w_in_done)
    loss = small_sum[small_sum.shape[0] - SUBLANES, 0]

    gsm = _unpack_small(small_sum)
    gsm["norm_in"] = g_norm_in
    gsm["conv_w"] = lax.dynamic_slice_in_dim(gsm["conv_w"], chip * cshard, cshard, axis=1)
    gsm["gn_gain"] = lax.dynamic_slice_in_dim(gsm["gn_gain"], chip * gshard, gshard, axis=1)
    smalls = [k for k in names if k not in delta]

    def view(a):
        return a.reshape(1, -1) if a.ndim == 1 else (a.reshape(a.shape[1:]) if a.ndim > 2 else a)

    ds, mns, vns = _adamw_small([view(weights[k]) for k in smalls], [gsm[k].reshape(view(weights[k]).shape) for k in smalls],
                                [view(ms[k]) for k in smalls], [view(vs[k]) for k in smalls], "adamw_small")
    for k, d, mn, vn in zip(smalls, ds, mns, vns):
        shp = weights[k].shape
        grads[k], delta[k], new_m[k], new_v[k] = gsm[k].reshape(shp), d.reshape(shp), mn.reshape(shp), vn.reshape(shp)

    return (loss, grad_x.reshape(B, S, D_MODEL), *[grads[k] for k in names], *[delta[k] for k in names],
            *[new_m[k] for k in names], *[new_v[k] for k in names])
```

```python
import jax
import jax.numpy as jnp
from jax import lax
from jax.experimental import pallas as pl
from jax.experimental.pallas import tpu as pltpu

F32 = jnp.float32
_MXU = jnp.bfloat16

D_MODEL = 1024
N_GROUPS = 8
HEADS = 4
DK = 256
CHUNK = 128
CONV = 4
LRU_BLOCKS = 16
LRU_BW = 64
LRU_C = 8.0
ROPE_THETA = 10000.0
EPS = 1e-6
CW = 256
N_CT = D_MODEL // CW
N_CHIPS = 4
MESH = pl.DeviceIdType.MESH

ADAM_LR = 0.001
ADAM_B1 = 0.9
ADAM_B2 = 0.999
ADAM_EPS = 1e-08
ADAM_WD = 0.01
ADAM_STEP = 10

VMEM_LIMIT = 56 * 1024 * 1024

FIRST_PROJ_TILE = 1024
MORE_PROJ_TILE = 2048
SCAN_TILE = 1024
MID_TILE = 256
DX_TILE = 512
DW_COLS = 512
DW_LOADS = 4
RET_CHUNKS = 2
SUM_ROWS = 256
ADAMW_ROWS = 256
JOIN_PIECES = 8
PROJ_JOIN_PIECES = 4


def _c(v):
    return v.astype(_MXU)


def _dot(a, b):
    return lax.dot_general(a, b, (((1,), (0,)), ((), ())), preferred_element_type=F32)


def _dot_nt(a, b):
    return lax.dot_general(a, b, (((1,), (1,)), ((), ())), preferred_element_type=F32)


def _dot_tn(a, b):
    return lax.dot_general(a, b, (((0,), (0,)), ((), ())), preferred_element_type=F32)


def _sigmoid(z):
    return 0.5 * jnp.tanh(0.5 * z) + 0.5


ANY_SPEC = pl.BlockSpec(memory_space=pl.ANY)


def _after(body, n_in, deps):
    n_deps = len(deps)

    def wrapped(*refs):
        return body(*refs[:n_in], *refs[n_in + n_deps:])

    return wrapped


def _params(sem=None):
    if sem is None:
        return pltpu.CompilerParams(vmem_limit_bytes=VMEM_LIMIT)
    return pltpu.CompilerParams(vmem_limit_bytes=VMEM_LIMIT, dimension_semantics=sem)


def _inproj_first(x2d, g_in, w_all, chips, name, deps=()):
    T = x2d.shape[0]
    tm = min(FIRST_PROJ_TILE, T)
    n_i = T // tm

    def body(s_ref, *refs):
        x_ref, g_ref, w_ref = refs[:3]
        proj_ref, hb_ref, ht_ref, h_all = refs[-4:]
        i = pl.program_id(1)
        rows = pl.ds(pl.multiple_of(i * tm, tm), tm)

        @pl.when(pl.program_id(0) == 0)
        def _():
            x = x_ref[...]
            r = lax.rsqrt(jnp.mean(x * x, axis=-1, keepdims=True) + EPS)
            h = x * r * g_ref[...]
            hb = h.astype(h_all.dtype)
            h_all[rows, :] = hb
            hb_ref[...] = hb
            ht_ref[...] = h.T.astype(ht_ref.dtype)

        proj_ref[...] = _dot(h_all[rows, :], w_ref[0])

    first = lambda j, i: jnp.where(j == 0, i, n_i - 1)
    return pl.pallas_call(
        body,
        name=name,
        grid_spec=pltpu.PrefetchScalarGridSpec(
            num_scalar_prefetch=1,
            grid=(2 * chips.shape[0], n_i),
            in_specs=[
                pl.BlockSpec((tm, D_MODEL), lambda j, i, s: (first(j, i), 0)),
                pl.BlockSpec((1, D_MODEL), lambda j, i, s: (0, 0)),
                pl.BlockSpec((1, D_MODEL, D_MODEL), lambda j, i, s: (s[j // 2], 0, j % 2)),
            ] + [ANY_SPEC] * len(deps),
            out_specs=[
                pl.BlockSpec((tm, D_MODEL), lambda j, i, s: (i, 2 * s[j // 2] + j % 2)),
                pl.BlockSpec((tm, D_MODEL), lambda j, i, s: (first(j, i), 0)),
                pl.BlockSpec((D_MODEL, tm), lambda j, i, s: (0, first(j, i))),
            ],
            scratch_shapes=[pltpu.VMEM((T, D_MODEL), _MXU)],
        ),
        out_shape=[
            jax.ShapeDtypeStruct((T, N_GROUPS * D_MODEL), F32),
            jax.ShapeDtypeStruct((T, D_MODEL), _MXU),
            jax.ShapeDtypeStruct((D_MODEL, T), _MXU),
        ],
        compiler_params=_params(("arbitrary", "arbitrary")),
    )(chips, x2d, g_in, w_all, *deps)


def _inproj_more(hb, w_all, chips, proj, name):
    T = hb.shape[0]
    tm = min(MORE_PROJ_TILE, T)

    def body(s_ref, hb_hbm, w_ref, prev_ref, proj_ref, h_all, sem):
        @pl.when((pl.program_id(0) == 0) & (pl.program_id(1) == 0))
        def _():
            cp = pltpu.make_async_copy(hb_hbm, h_all, sem)
            cp.start()
            cp.wait()

        rows = pl.ds(pl.multiple_of(pl.program_id(1) * tm, tm), tm)
        proj_ref[...] = _dot(h_all[rows, :], w_ref[0])

    return pl.pallas_call(
        body,
        name=name,
        grid_spec=pltpu.PrefetchScalarGridSpec(
            num_scalar_prefetch=1,
            grid=(2 * chips.shape[0], T // tm),
            in_specs=[
                ANY_SPEC,
                pl.BlockSpec((1, D_MODEL, D_MODEL), lambda j, i, s: (s[j // 2], 0, j % 2)),
                ANY_SPEC,
            ],
            out_specs=pl.BlockSpec((tm, D_MODEL), lambda j, i, s: (i, 2 * s[j // 2] + j % 2)),
            scratch_shapes=[pltpu.VMEM((T, D_MODEL), hb.dtype), pltpu.SemaphoreType.DMA],
        ),
        out_shape=jax.ShapeDtypeStruct(proj.shape, F32),
        input_output_aliases={3: 0},
        compiler_params=_params(("arbitrary", "arbitrary")),
    )(chips, hb, w_all, proj)


def _scan_fwd(a, u):
    n = a.shape[0]
    row = lax.broadcasted_iota(jnp.int32, a.shape, 0)
    s = 1
    while s < n:
        m = row >= s
        u = u + a * jnp.where(m, pltpu.roll(u, s, 0), 0.0)
        a = a * jnp.where(m, pltpu.roll(a, s, 0), 1.0)
        s *= 2
    return a, u


def _scan_bwd(b, g):
    n = b.shape[0]
    row = lax.broadcasted_iota(jnp.int32, b.shape, 0)
    s = 1
    while s < n:
        m = row < n - s
        g = g + b * jnp.where(m, pltpu.roll(g, n - s, 0), 0.0)
        b = b * jnp.where(m, pltpu.roll(b, n - s, 0), 1.0)
        s *= 2
    return b, g


LANES = 128
SUBLANES = 8


def _scan_scratch(tc):
    by_lanes = pltpu.VMEM((CW // LANES, tc, LANES), F32)
    return [by_lanes, by_lanes, pltpu.VMEM((tc // SUBLANES, CW), F32), pltpu.VMEM((tc, CW), F32)]


def _scan_tile(a, u, edge, la_ref, lh_ref, c_ref, dst_ref, reverse):
    n, w = a.shape
    groups = n // SUBLANES
    a3 = a.reshape(groups, SUBLANES, w)
    u3 = u.reshape(groups, SUBLANES, w)
    row = lax.broadcasted_iota(jnp.int32, a3.shape, 1)
    for s in (1, 2, 4):
        m = (row < SUBLANES - s) if reverse else (row >= s)
        shift = SUBLANES - s if reverse else s
        u3 = u3 + a3 * jnp.where(m, pltpu.roll(u3, shift, 1), 0.0)
        a3 = a3 * jnp.where(m, pltpu.roll(a3, shift, 1), 1.0)
    al = a3.reshape(n, w)
    hl = u3.reshape(n, w)
    blocks = w // LANES
    for q in range(blocks):
        la_ref[q] = al[:, q * LANES:(q + 1) * LANES]
        lh_ref[q] = hl[:, q * LANES:(q + 1) * LANES]
    ends = pl.ds(0 if reverse else SUBLANES - 1, groups, stride=SUBLANES)
    end_a = jnp.concatenate([la_ref.at[q][ends, :] for q in range(blocks)], axis=-1)
    end_h = jnp.concatenate([lh_ref.at[q][ends, :] for q in range(blocks)], axis=-1)
    prod, part = (_scan_bwd if reverse else _scan_fwd)(end_a, end_h)
    total = part + prod * edge
    g_row = lax.broadcasted_iota(jnp.int32, total.shape, 0)
    if reverse:
        c_ref[...] = jnp.where(g_row == groups - 1, edge, pltpu.roll(total, groups - 1, 0))
    else:
        c_ref[...] = jnp.where(g_row == 0, edge, pltpu.roll(total, 1, 0))
    for g in range(groups):
        rows = slice(g * SUBLANES, (g + 1) * SUBLANES)
        for q in range(blocks):
            cols = slice(q * LANES, (q + 1) * LANES)
            dst_ref[rows, cols] = lh_ref[q, rows, :] + la_ref[q, rows, :] * c_ref[g:g + 1, cols]


def _softplus_neg(lam):
    z = -lam
    return jnp.maximum(z, 0.0) + jnp.log1p(jnp.exp(-jnp.abs(z)))


def _lru_gates(xc, wx_ref, wa_ref, bx_ref, ba_ref, lam_ref):
    xcb = _c(xc)
    i_t = _sigmoid(_dot(xcb, wx_ref[0]) + bx_ref[...])
    r_t = _sigmoid(_dot(xcb, wa_ref[0]) + ba_ref[...])
    sp = _softplus_neg(lam_ref[...])
    log_a = (-LRU_C) * r_t * sp
    a = jnp.exp(log_a)
    mult = jnp.sqrt(1.0 - a * a)
    return xcb, i_t, r_t, sp, a, mult


def _conv_from_ext(ext_ref, xa, cw_ref, cb_ref, tc):
    return (cb_ref[...] + cw_ref[3:4, :] * xa + cw_ref[2:3, :] * ext_ref[7:7 + tc, :]
            + cw_ref[1:2, :] * ext_ref[6:6 + tc, :] + cw_ref[0:1, :] * ext_ref[5:5 + tc, :])


def _lru_fwd(proj, conv_w, conv_b, wx_bd, wa_bd, bx, ba, lam, B, S):
    T = B * S
    tc = min(SCAN_TILE, S)
    nt = S // tc
    h8 = tc // 8

    def body(xa_ref, halo_ref, ga_ref, cw_ref, cb_ref, wx_ref, wa_ref, bx_ref, ba_ref, lam_ref,
             h_ref, ya_ref, ext_ref, carry_ref, la_ref, lh_ref, c_ref):
        t = pl.program_id(2)

        @pl.when(t == 0)
        def _():
            carry_ref[...] = jnp.zeros_like(carry_ref)

        xa = xa_ref[...]
        ext_ref[0:8, :] = jnp.where(t == 0, 0.0, halo_ref[...])
        ext_ref[8:8 + tc, :] = xa
        xc = _conv_from_ext(ext_ref, xa, cw_ref, cb_ref, tc)
        _, i_t, _, _, a, mult = _lru_gates(xc, wx_ref, wa_ref, bx_ref, ba_ref, lam_ref)
        u = mult * (i_t * xc)
        _scan_tile(a, u, carry_ref[7:8, :], la_ref, lh_ref, c_ref, h_ref, False)
        h = h_ref[...]
        carry_ref[...] = h[tc - 8:tc, :]
        ga = ga_ref[...]
        ya_ref[...] = (ga * _sigmoid(ga) * h).astype(ya_ref.dtype)

    row = lambda b, t: b * nt + t
    vec = pl.BlockSpec((1, CW), lambda b, c, t: (0, c))
    mat = pl.BlockSpec((1, CW, CW), lambda b, c, t: (c, 0, 0))
    return pl.pallas_call(
        body,
        name="lru_fwd",
        grid=(B, N_CT, nt),
        in_specs=[
            pl.BlockSpec((tc, CW), lambda b, c, t: (row(b, t), c)),
            pl.BlockSpec((8, CW), lambda b, c, t: (jnp.maximum(row(b, t) * h8 - 1, 0), c)),
            pl.BlockSpec((tc, CW), lambda b, c, t: (row(b, t), N_CT + c)),
            pl.BlockSpec((CONV, CW), lambda b, c, t: (0, c)),
            vec, mat, mat, vec, vec, vec,
        ],
        out_specs=[
            pl.BlockSpec((tc, CW), lambda b, c, t: (row(b, t), c)),
            pl.BlockSpec((tc, CW), lambda b, c, t: (row(b, t), c)),
        ],
        out_shape=[
            jax.ShapeDtypeStruct((T, D_MODEL), F32),
            jax.ShapeDtypeStruct((T, D_MODEL), _MXU),
        ],
        scratch_shapes=[pltpu.VMEM((tc + 8, CW), F32), pltpu.VMEM((8, CW), F32)] + _scan_scratch(tc)[:3],
        compiler_params=_params(("parallel", "parallel", "arbitrary")),
    )(proj, proj, proj, conv_w, conv_b, wx_bd, wa_bd, bx, ba, lam)


def _lru_bwd(dya, proj, hlru, conv_w, conv_b, wx_bd, wa_bd, bx, ba, lam, B, S, deps=()):
    T = B * S
    tc = min(SCAN_TILE, S)
    nt = S // tc
    h8 = tc // 8

    def body(dya_ref, xa_ref, xhalo_ref, ga_ref, h_ref, hhalo_ref, cw_ref, cb_ref, wx_ref, wa_ref, bx_ref, ba_ref,
             lam_ref, dxa_ref, dga_ref, dcw_ref, dcb_ref, dwx_ref, dwa_ref, dbx_ref, dba_ref, dlam_ref,
             ext_ref, ext2_ref, carry_ref, dhalo_ref, la_ref, lh_ref, c_ref, dh_ref):
        b = pl.program_id(1)
        t = pl.program_id(2)
        tt = nt - 1 - t

        @pl.when(t == 0)
        def _():
            carry_ref[...] = jnp.zeros_like(carry_ref)
            dhalo_ref[...] = jnp.zeros_like(dhalo_ref)

        @pl.when((t == 0) & (b == 0))
        def _():
            for r in (dcw_ref, dcb_ref, dwx_ref, dwa_ref, dbx_ref, dba_ref, dlam_ref):
                r[...] = jnp.zeros_like(r)

        xa = xa_ref[...]
        ext_ref[0:8, :] = jnp.where(tt == 0, 0.0, xhalo_ref[...])
        ext_ref[8:8 + tc, :] = xa
        xc = _conv_from_ext(ext_ref, xa, cw_ref, cb_ref, tc)
        xcb, i_t, r_t, sp, a, mult = _lru_gates(xc, wx_ref, wa_ref, bx_ref, ba_ref, lam_ref)

        h = h_ref[...]
        ga = ga_ref[...]
        dya_t = dya_ref[...]
        sg = _sigmoid(ga)
        dga_ref[...] = (dya_t * h * (sg * (1.0 + ga * (1.0 - sg)))).astype(dga_ref.dtype)
        dlru = dya_t * (ga * sg)

        row = lax.broadcasted_iota(jnp.int32, a.shape, 0)
        coef = jnp.where(row == tc - 1, 1.0, pltpu.roll(a, tc - 1, 0))
        _scan_tile(coef, dlru, carry_ref[0:1, :], la_ref, lh_ref, c_ref, dh_ref, True)
        dh = dh_ref[...]
        ext2_ref[0:tc, :] = a * dh
        carry_ref[...] = ext2_ref[0:8, :]

        ext2_ref[0:8, :] = jnp.where(tt == 0, 0.0, hhalo_ref[...])
        ext2_ref[8:8 + tc, :] = h
        hprev = ext2_ref[7:7 + tc, :]

        da = dh * hprev
        ix = i_t * xc
        dmult = dh * ix
        di = dh * mult * xc
        dxc = dh * mult * i_t
        dlog_a = da * a - dmult * (a * a) / mult
        dr = dlog_a * ((-LRU_C) * sp)
        dlam_ref[...] += jnp.sum(dlog_a * r_t, axis=0, keepdims=True) * (LRU_C * _sigmoid(-lam_ref[...]))
        dza = dr * r_t * (1.0 - r_t)
        dzx = di * i_t * (1.0 - i_t)
        dzab = _c(dza)
        dzxb = _c(dzx)
        dxc = dxc + _dot_nt(dzxb, wx_ref[0]) + _dot_nt(dzab, wa_ref[0])
        dwx_ref[0] += _dot_tn(xcb, dzxb)
        dwa_ref[0] += _dot_tn(xcb, dzab)
        dbx_ref[...] += jnp.sum(dzx, axis=0, keepdims=True)
        dba_ref[...] += jnp.sum(dza, axis=0, keepdims=True)

        dcb_ref[...] += jnp.sum(dxc, axis=0, keepdims=True)
        dcw_ref[3:4, :] += jnp.sum(dxc * xa, axis=0, keepdims=True)
        dcw_ref[2:3, :] += jnp.sum(dxc * ext_ref[7:7 + tc, :], axis=0, keepdims=True)
        dcw_ref[1:2, :] += jnp.sum(dxc * ext_ref[6:6 + tc, :], axis=0, keepdims=True)
        dcw_ref[0:1, :] += jnp.sum(dxc * ext_ref[5:5 + tc, :], axis=0, keepdims=True)
        ext2_ref[0:tc, :] = dxc
        ext2_ref[tc:tc + 8, :] = dhalo_ref[...]
        dxa = (cw_ref[3:4, :] * dxc + cw_ref[2:3, :] * ext2_ref[1:1 + tc, :]
               + cw_ref[1:2, :] * ext2_ref[2:2 + tc, :] + cw_ref[0:1, :] * ext2_ref[3:3 + tc, :])
        dxa_ref[...] = dxa.astype(dxa_ref.dtype)
        dhalo_ref[...] = ext2_ref[0:8, :]

    row_of = lambda b, t: b * nt + (nt - 1 - t)
    tile = lambda off: pl.BlockSpec((tc, CW), lambda c, b, t: (row_of(b, t), off + c))
    halo = pl.BlockSpec((8, CW), lambda c, b, t: (jnp.maximum(row_of(b, t) * h8 - 1, 0), c))
    vec = pl.BlockSpec((1, CW), lambda c, b, t: (0, c))
    mat = pl.BlockSpec((1, CW, CW), lambda c, b, t: (c, 0, 0))
    cwspec = pl.BlockSpec((CONV, CW), lambda c, b, t: (0, c))
    return pl.pallas_call(
        _after(body, 13, deps),
        name="lru_bwd",
        grid=(N_CT, B, nt),
        in_specs=[tile(0), tile(0), halo, tile(N_CT), tile(0), halo, cwspec, vec, mat, mat, vec, vec, vec]
        + [ANY_SPEC] * len(deps),
        out_specs=[tile(0), tile(0), cwspec, vec, mat, mat, vec, vec, vec],
        out_shape=[
            jax.ShapeDtypeStruct((T, D_MODEL), _MXU),
            jax.ShapeDtypeStruct((T, D_MODEL), _MXU),
            jax.ShapeDtypeStruct((CONV, D_MODEL), F32),
            jax.ShapeDtypeStruct((1, D_MODEL), F32),
            jax.ShapeDtypeStruct((N_CT, CW, CW), F32),
            jax.ShapeDtypeStruct((N_CT, CW, CW), F32),
            jax.ShapeDtypeStruct((1, D_MODEL), F32),
            jax.ShapeDtypeStruct((1, D_MODEL), F32),
            jax.ShapeDtypeStruct((1, D_MODEL), F32),
        ],
        scratch_shapes=[pltpu.VMEM((tc + 8, CW), F32), pltpu.VMEM((tc + 8, CW), F32),
                        pltpu.VMEM((8, CW), F32), pltpu.VMEM((8, CW), F32)] + _scan_scratch(tc),
        compiler_params=_params(("parallel", "arbitrary", "arbitrary")),
    )(dya, proj, proj, proj, hlru, hlru, conv_w, conv_b, wx_bd, wa_bd, bx, ba, lam, *deps)


def _retention_tables(S):
    half = DK // 2
    freqs = ROPE_THETA ** (-jnp.arange(half, dtype=F32) / half)
    ang = jnp.arange(S, dtype=F32)[:, None] * freqs[None, :]
    log_g = jnp.log1p(-(2.0 ** (-5.0 - jnp.arange(HEADS, dtype=F32))))
    idx = jnp.arange(CHUNK, dtype=F32)
    diff = idx[:, None] - idx[None, :]
    inner = jnp.where(diff >= 0, jnp.exp(jnp.maximum(diff, 0.0)[None] * log_g[:, None, None]), 0.0)
    cross = jnp.exp((idx[None, :] + 1.0) * log_g[:, None])[:, :, None]
    state = jnp.exp((CHUNK - 1.0 - idx[None, :]) * log_g[:, None])[:, :, None]
    gam = jnp.broadcast_to(jnp.exp(CHUNK * log_g)[:, None, None], (HEADS, 1, DK))
    return jnp.cos(ang), jnp.sin(ang), inner, cross, state, gam


def _rot(x, cos, sin):
    half = DK // 2
    x1, x2 = x[:, :half], x[:, half:]
    return jnp.concatenate([x1 * cos - x2 * sin, x1 * sin + x2 * cos], axis=-1)


def _rot_t(y, cos, sin):
    half = DK // 2
    y1, y2 = y[:, :half], y[:, half:]
    return jnp.concatenate([y1 * cos + y2 * sin, y2 * cos - y1 * sin], axis=-1)


def _groupnorm(o):
    mu = jnp.mean(o, axis=-1, keepdims=True)
    oc = o - mu
    rs = lax.rsqrt(jnp.mean(oc * oc, axis=-1, keepdims=True) + EPS)
    return oc * rs, rs


def _ret_specs(B, chunk_of):
    rows = RET_CHUNKS * CHUNK
    qkv = lambda g: pl.BlockSpec((B, rows, D_MODEL), lambda c: (0, chunk_of(c), g))
    act = pl.BlockSpec((B, rows, D_MODEL), lambda c: (0, chunk_of(c), 0))
    rope = pl.BlockSpec((rows, DK // 2), lambda c: (chunk_of(c), 0))
    dmat = pl.BlockSpec((HEADS, CHUNK, CHUNK), lambda c: (0, 0, 0))
    dvec = pl.BlockSpec((HEADS, CHUNK, 1), lambda c: (0, 0, 0))
    hrow = pl.BlockSpec((HEADS, 1, DK), lambda c: (0, 0, 0))
    rst = pl.BlockSpec((RET_CHUNKS, B, HEADS, DK, DK), lambda c: (chunk_of(c), 0, 0, 0, 0))
    return qkv, act, rope, dmat, dvec, hrow, rst


def _ret_fwd(proj, tables, gain3, B, S):
    T = B * S
    nc = S // CHUNK
    cos, sin, dmat_t, cd_t, sd_t, gam_t = tables

    def body(q_ref, k_ref, v_ref, gb_ref, cos_ref, sin_ref, dm_ref, cd_ref, sd_ref, gam_ref, gain_ref,
             o_ref, yb_ref, rs_ref, state_ref):
        @pl.when(pl.program_id(0) == 0)
        def _():
            state_ref[...] = jnp.zeros_like(state_ref)

        for cc, b, h in [(cc, b, h) for cc in range(RET_CHUNKS) for b in range(B) for h in range(HEADS)]:
            rows = slice(cc * CHUNK, (cc + 1) * CHUNK)
            cos_t, sin_t = cos_ref[rows, :], sin_ref[rows, :]
            cols = slice(h * DK, (h + 1) * DK)
            qb = _c(_rot(q_ref[b, rows, cols], cos_t, sin_t))
            kb = _c(_rot(k_ref[b, rows, cols], cos_t, sin_t) * (DK ** -0.5))
            v = v_ref[b, rows, cols]
            state = state_ref[b, h]
            sb = _c(state)
            rs_ref[cc, b, h] = sb
            scores = _dot_nt(qb, kb) * dm_ref[h]
            o = _dot(_c(scores), _c(v)) + _dot(qb, sb) * cd_ref[h]
            state_ref[b, h] = gam_ref[h] * state + _dot_tn(kb, _c(v * sd_ref[h]))
            o_ref[b, rows, cols] = o
            n, _ = _groupnorm(o)
            gb = gb_ref[b, rows, cols]
            yb_ref[b, rows, cols] = (gb * _sigmoid(gb) * (n * gain_ref[h])).astype(yb_ref.dtype)

    qkv, act, rope, dmat, dvec, hrow, rst = _ret_specs(B, lambda c: c)
    proj3 = proj.reshape(B, S, proj.shape[1])
    o_pre, yb, states = pl.pallas_call(
        body,
        name="ret_fwd",
        grid=(nc // RET_CHUNKS,),
        in_specs=[qkv(2), qkv(3), qkv(4), qkv(5), rope, rope, dmat, dvec, dvec, hrow, hrow],
        out_specs=[act, act, rst],
        out_shape=[
            jax.ShapeDtypeStruct((B, S, D_MODEL), F32),
            jax.ShapeDtypeStruct((B, S, D_MODEL), _MXU),
            jax.ShapeDtypeStruct((nc, B, HEADS, DK, DK), _MXU),
        ],
        scratch_shapes=[pltpu.VMEM((B, HEADS, DK, DK), F32)],
        compiler_params=_params(("arbitrary",)),
    )(proj3, proj3, proj3, proj3, cos, sin, dmat_t, cd_t, sd_t, gam_t, gain3)
    return o_pre.reshape(T, D_MODEL), yb.reshape(T, D_MODEL), states


def _ret_bwd(dyb, o_pre, proj, states, tables, gain3, B, S, deps=()):
    T = B * S
    nc = S // CHUNK
    cos, sin, dmat_t, cd_t, sd_t, gam_t = tables

    def body(dyb_ref, o_ref, q_ref, k_ref, v_ref, gb_ref, rs_ref, cos_ref, sin_ref, dm_ref, cd_ref, sd_ref, gam_ref,
             gain_ref, dr_ref, dgain_ref, dstate_ref):
        @pl.when(pl.program_id(0) == 0)
        def _():
            dstate_ref[...] = jnp.zeros_like(dstate_ref)
            dgain_ref[...] = jnp.zeros_like(dgain_ref)

        for cc, b, h in [(cc, b, h) for cc in reversed(range(RET_CHUNKS)) for b in range(B) for h in range(HEADS)]:
            rows = slice(cc * CHUNK, (cc + 1) * CHUNK)
            cos_t, sin_t = cos_ref[rows, :], sin_ref[rows, :]
            cols = slice(h * DK, (h + 1) * DK)
            gain = gain_ref[h]
            n, rs = _groupnorm(o_ref[b, rows, cols])
            gb = gb_ref[b, rows, cols]
            sg = _sigmoid(gb)
            dy = dyb_ref[b, rows, cols]
            part = lambda g: slice(g * D_MODEL + h * DK, g * D_MODEL + (h + 1) * DK)
            dr_ref[b, rows, part(3)] = (dy * (n * gain) * (sg * (1.0 + gb * (1.0 - sg)))).astype(dr_ref.dtype)
            dgn = dy * (gb * sg)
            dgain_ref[h] += jnp.sum(dgn * n, axis=0, keepdims=True)
            dn = dgn * gain
            do = rs * (dn - jnp.mean(dn, axis=-1, keepdims=True) - n * jnp.mean(dn * n, axis=-1, keepdims=True))

            qb = _c(_rot(q_ref[b, rows, cols], cos_t, sin_t))
            kb = _c(_rot(k_ref[b, rows, cols], cos_t, sin_t) * (DK ** -0.5))
            v = v_ref[b, rows, cols]
            vb = _c(v)
            vsb = _c(v * sd_ref[h])
            dob = _c(do)
            docb = _c(do * cd_ref[h])
            dmat = dm_ref[h]
            dstate = dstate_ref[b, h]
            dsb = _c(dstate)
            pb = _c(_dot_nt(qb, kb) * dmat)
            dsc = _c(_dot_nt(dob, vb) * dmat)
            dq = _dot(dsc, kb) + _dot_nt(docb, rs_ref[cc, b, h])
            dk = _dot_tn(dsc, qb) + _dot_nt(vsb, dsb)
            dv = _dot_tn(pb, dob) + _dot(kb, dsb) * sd_ref[h]
            dstate_ref[b, h] = gam_ref[h] * dstate + _dot_tn(qb, docb)
            dr_ref[b, rows, part(0)] = _rot_t(dq, cos_t, sin_t).astype(dr_ref.dtype)
            dr_ref[b, rows, part(1)] = (_rot_t(dk, cos_t, sin_t) * (DK ** -0.5)).astype(dr_ref.dtype)
            dr_ref[b, rows, part(2)] = dv.astype(dr_ref.dtype)

    n_steps = nc // RET_CHUNKS
    qkv, act, rope, dmat, dvec, hrow, rst = _ret_specs(B, lambda c: n_steps - 1 - c)
    wide = pl.BlockSpec((B, RET_CHUNKS * CHUNK, 4 * D_MODEL), lambda c: (0, n_steps - 1 - c, 0))
    proj3 = proj.reshape(B, S, proj.shape[1])
    dr, dgain = pl.pallas_call(
        _after(body, 14, deps),
        name="ret_bwd",
        grid=(n_steps,),
        in_specs=[act, act, qkv(2), qkv(3), qkv(4), qkv(5), rst, rope, rope, dmat, dvec, dvec, hrow, hrow]
        + [ANY_SPEC] * len(deps),
        out_specs=[wide, hrow],
        out_shape=[jax.ShapeDtypeStruct((B, S, 4 * D_MODEL), _MXU), jax.ShapeDtypeStruct((HEADS, 1, DK), F32)],
        scratch_shapes=[pltpu.VMEM((B, HEADS, DK, DK), F32)],
        compiler_params=_params(("arbitrary",)),
    )(dyb.reshape(B, S, D_MODEL), o_pre.reshape(B, S, D_MODEL), proj3, proj3, proj3, proj3, states, cos, sin, dmat_t,
      cd_t, sd_t, gam_t, gain3, *deps)
    return dr.reshape(T, 4 * D_MODEL), dgain


def _mid(ya, yb, proj, x2d, tgt2d, wpa, wpb, wout, g_fin):
    T = x2d.shape[0]
    tm = min(MID_TILE, T)
    n_steps = T // tm
    rows = D_MODEL // (2 * N_CHIPS)

    def body(ya_ref, yb_ref, ma_ref, mb_ref, x_ref, t_ref, gf_ref, wpa_hbm, wpb_hbm, wout_hbm,
             loss_ref, dx2_ref, dya_ref, dyb_ref, dm_ref, dgf_ref, gw_hbm, w_ref, acc_ref, sem):
        i = pl.program_id(0)

        @pl.when(i == 0)
        def _():
            loads = [pltpu.make_async_copy(src, w_ref.at[k], sem.at[k]) for k, src in enumerate((wpa_hbm, wpb_hbm, wout_hbm))]
            for cp in loads:
                cp.start()
            for cp in loads:
                cp.wait()
            acc_ref[...] = jnp.zeros_like(acc_ref)
            loss_ref[...] = jnp.zeros_like(loss_ref)
            dgf_ref[...] = jnp.zeros_like(dgf_ref)

        ya_t, yb_t = ya_ref[...], yb_ref[...]
        out_a = _dot(ya_t, w_ref[0])
        out_b = _dot(yb_t, w_ref[1])
        sa = _sigmoid(ma_ref[...])
        sb = _sigmoid(mb_ref[...])
        mgb = _c(sa * out_a + sb * out_b)
        x2 = x_ref[...] + _dot(mgb, w_ref[2])
        r2 = lax.rsqrt(jnp.mean(x2 * x2, axis=-1, keepdims=True) + EPS)
        nx = x2 * r2
        gf = gf_ref[...]
        err = nx * gf - t_ref[...]
        loss_ref[...] += 0.5 * jnp.sum(jnp.mean(err * err, axis=-1, keepdims=True), axis=0, keepdims=True)
        dy = err * (1.0 / D_MODEL)
        dgf_ref[...] += jnp.sum(dy * nx, axis=0, keepdims=True)
        dyg = dy * gf
        dx2 = r2 * (dyg - nx * jnp.mean(dyg * nx, axis=-1, keepdims=True))
        dx2_ref[...] = dx2
        dx2b = _c(dx2)
        dmg = _dot_nt(dx2b, w_ref[2])
        acc_ref[2] += _dot_tn(mgb, dx2b)
        dm_ref[:, :D_MODEL] = (dmg * out_a * sa * (1.0 - sa)).astype(dm_ref.dtype)
        dm_ref[:, D_MODEL:] = (dmg * out_b * sb * (1.0 - sb)).astype(dm_ref.dtype)
        dab = _c(dmg * sa)
        dbb = _c(dmg * sb)
        dya_ref[...] = _dot_nt(dab, w_ref[0])
        dyb_ref[...] = _dot_nt(dbb, w_ref[1])
        acc_ref[0] += _dot_tn(ya_t, dab)
        acc_ref[1] += _dot_tn(yb_t, dbb)

        @pl.when(i == n_steps - 1)
        def _():
            copies = [pltpu.make_async_copy(acc_ref.at[k, pl.ds((2 * p + hf) * rows, rows), :], gw_hbm.at[p, hf, k],
                                            sem.at[(k * N_CHIPS + p) * 2 + hf])
                      for k in range(3) for p in range(N_CHIPS) for hf in range(2)]
            for cp in copies:
                cp.start()
            for cp in copies:
                cp.wait()

    tile = lambda j: pl.BlockSpec((tm, D_MODEL), lambda i: (i, j))
    one = pl.BlockSpec((1, D_MODEL), lambda i: (0, 0))
    anyspec = pl.BlockSpec(memory_space=pl.ANY)
    return pl.pallas_call(
        body,
        name="mid",
        grid=(n_steps,),
        in_specs=[tile(0), tile(0), tile(6), tile(7), tile(0), tile(0), one, anyspec, anyspec, anyspec],
        out_specs=[pl.BlockSpec((1, 1), lambda i: (0, 0)), tile(0), tile(0), tile(0),
                   pl.BlockSpec((tm, 2 * D_MODEL), lambda i: (i, 0)), one, anyspec],
        out_shape=[
            jax.ShapeDtypeStruct((1, 1), F32),
            jax.ShapeDtypeStruct((T, D_MODEL), F32),
            jax.ShapeDtypeStruct((T, D_MODEL), F32),
            jax.ShapeDtypeStruct((T, D_MODEL), F32),
            jax.ShapeDtypeStruct((T, 2 * D_MODEL), _MXU),
            jax.ShapeDtypeStruct((1, D_MODEL), F32),
            jax.ShapeDtypeStruct((N_CHIPS, 2, 3, rows, D_MODEL), F32),
        ],
        scratch_shapes=[pltpu.VMEM((3, D_MODEL, D_MODEL), _MXU), pltpu.VMEM((3, D_MODEL, D_MODEL), F32),
                        pltpu.SemaphoreType.DMA((3 * N_CHIPS * 2,))],
        compiler_params=_params(("arbitrary",)),
    )(ya, yb, proj, proj, x2d, tgt2d, g_fin, wpa, wpb, wout)


def _inproj_bwd_dx(dparts, w_all, x2d, dx2, g_in, first, count, prev, name, deps=()):
    T = x2d.shape[0]
    tm = min(DX_TILE, T)
    n_d = len(dparts)
    groups = [(a, k) for a, d in enumerate(dparts) for k in range(d.shape[1] // D_MODEL)]
    dg_start = jnp.zeros((1, D_MODEL), F32) if prev is None else prev[1]
    carried = () if prev is None else (prev[0],)

    def body(*refs):
        d_refs = refs[:n_d]
        x_ref, dx2_ref, g_ref, dg0_ref, w_hbm = refs[n_d:n_d + 5]
        dx_ref, dg_ref, w_ref, sem = refs[-4:]

        def load(j):
            part = (j // 2, slice(None), pl.ds((j % 2) * D_MODEL, D_MODEL))
            return pltpu.make_async_copy(w_hbm.at[part], w_ref.at[part], sem.at[j])

        def tile(before_group):
            dh = jnp.zeros((tm, D_MODEL), F32)
            for j, (a, k) in enumerate(groups):
                before_group(j)
                dh = dh + _dot_nt(d_refs[a][:, k * D_MODEL:(k + 1) * D_MODEL],
                                  w_ref[j // 2, :, (j % 2) * D_MODEL:(j % 2 + 1) * D_MODEL])
            x = x_ref[...]
            r = lax.rsqrt(jnp.mean(x * x, axis=-1, keepdims=True) + EPS)
            nx = x * r
            dg_ref[...] += jnp.sum(dh * nx, axis=0, keepdims=True)
            dhg = dh * g_ref[...]
            dx_ref[...] = dx2_ref[...] + r * (dhg - nx * jnp.mean(dhg * nx, axis=-1, keepdims=True))

        first = pl.program_id(0) == 0

        @pl.when(first)
        def _():
            for j in range(len(groups)):
                load(j).start()
            dg_ref[...] = dg0_ref[...]
            tile(lambda j: load(j).wait())

        @pl.when(jnp.logical_not(first))
        def _():
            tile(lambda j: None)

    tile = pl.BlockSpec((tm, D_MODEL), lambda i: (first + i, 0))
    one = pl.BlockSpec((1, D_MODEL), lambda i: (0, 0))
    return pl.pallas_call(
        body,
        name=name,
        grid=(count,),
        in_specs=[pl.BlockSpec((tm, d.shape[1]), lambda i: (first + i, 0)) for d in dparts]
        + [tile, tile, one, one, ANY_SPEC] + [ANY_SPEC] * (len(carried) + len(deps)),
        out_specs=[tile, one],
        out_shape=[jax.ShapeDtypeStruct((T, D_MODEL), F32), jax.ShapeDtypeStruct((1, D_MODEL), F32)],
        input_output_aliases={n_d + 5: 0} if carried else {},
        scratch_shapes=[pltpu.VMEM(w_all.shape, w_all.dtype), pltpu.SemaphoreType.DMA((len(groups),))],
        compiler_params=_params(("arbitrary",)),
    )(*dparts, x2d, dx2, g_in, dg_start, w_all, *carried, *deps)


def _inproj_bwd_dw(ht, dparts, name, deps=()):
    T = ht.shape[1]
    tn = DW_COLS
    half = D_MODEL // 2
    per_chip = 2 * D_MODEL // tn
    n_d = len(dparts)
    tiles = [(a, t) for a, d in enumerate(dparts) for t in range(d.shape[1] // tn)]
    offs = [sum(d.shape[1] // tn for d in dparts[:a]) for a in range(n_d)]

    def body(*refs):
        ht_hbm = refs[0]
        d_refs = refs[1:1 + n_d]
        out_ref, ht_ref, sem = refs[-3:]
        t = pl.program_id(0)

        def load(k):
            cols = pl.ds(k * (T // DW_LOADS), T // DW_LOADS)
            return pltpu.make_async_copy(ht_hbm.at[:, cols], ht_ref.at[:, cols], sem.at[k])

        def store(g):
            out_ref[0, 0] = g[:half]
            out_ref[0, 1] = g[half:]

        @pl.when(t == 0)
        def _():
            for k in range(DW_LOADS):
                load(k).start()
            g = jnp.zeros((D_MODEL, tn), F32)
            for k in range(DW_LOADS):
                load(k).wait()
                tokens = slice(k * (T // DW_LOADS), (k + 1) * (T // DW_LOADS))
                g = g + _dot(ht_ref[:, tokens], d_refs[0][tokens, :])
            store(g)

        for a in range(n_d):
            lo, hi = max(offs[a], 1), offs[a] + dparts[a].shape[1] // tn

            @pl.when((t >= lo) & (t < hi))
            def _(a=a):
                store(_dot(ht_ref[...], d_refs[a][...]))

    def dspec(a):
        n_a = dparts[a].shape[1] // tn
        return pl.BlockSpec((T, tn), lambda t: (0, jnp.clip(t - offs[a], 0, n_a - 1)))

    return pl.pallas_call(
        body,
        name=name,
        grid=(len(tiles),),
        in_specs=[ANY_SPEC] + [dspec(a) for a in range(n_d)] + [ANY_SPEC] * len(deps),
        out_specs=pl.BlockSpec((1, 2, half, tn), lambda t: (t // per_chip, 0, 0, t % per_chip)),
        out_shape=jax.ShapeDtypeStruct((len(tiles) // per_chip, 2, half, 2 * D_MODEL), F32),
        scratch_shapes=[pltpu.VMEM(ht.shape, ht.dtype), pltpu.SemaphoreType.DMA((DW_LOADS,))],
        compiler_params=_params(("arbitrary",)),
    )(ht, *dparts, *deps)


def _coords():
    return lax.axis_index("x"), lax.axis_index("y"), lax.axis_index("c")


def _other_chips(x, y):
    return [(1 - x, y), (x, 1 - y), (1 - x, 1 - y)]


def _chunks(rows, n):
    size = rows // n
    return [pl.ds(q * size, size) for q in range(n)]


HBM_SPEC = pl.BlockSpec(memory_space=pltpu.HBM)
SEM_SPEC = pl.BlockSpec(memory_space=pltpu.SEMAPHORE)
DATAFLOW = pltpu.SideEffectType.DATAFLOW_SIDE_EFFECTING


def _copies_start(bufs, plan, n_copies, name, deps=()):
    n = len(bufs)
    n_deps = len(deps)

    def body(*refs):
        ins = refs[:n]
        send_sems, recv_sems = refs[n + n_deps], refs[n + n_deps + 1]
        token = refs[-1]
        for k, send, _ in plan(ins):
            if send is not None:
                src, dst, dev, pred = send
                cp = pltpu.make_async_remote_copy(src_ref=src, dst_ref=dst, send_sem=send_sems.at[k],
                                                  recv_sem=recv_sems.at[k], device_id=dev, device_id_type=MESH)
                if pred is None:
                    cp.start()
                else:
                    pl.when(pred)(cp.start)
        token[...] = jnp.zeros_like(token)

    hbm = [pltpu.with_memory_space_constraint(b, pltpu.HBM) for b in bufs]
    outs = pl.pallas_call(
        body,
        name=name,
        in_specs=[HBM_SPEC] * n + [ANY_SPEC] * n_deps,
        out_specs=(SEM_SPEC, SEM_SPEC, *([HBM_SPEC] * n), pl.BlockSpec(memory_space=pltpu.VMEM)),
        out_shape=(pltpu.SemaphoreType.DMA((n_copies,)), pltpu.SemaphoreType.DMA((n_copies,)),
                   *[pltpu.HBM(b.shape, b.dtype) for b in bufs], jax.ShapeDtypeStruct((8, 128), F32)),
        input_output_aliases={a: 2 + a for a in range(n)},
        compiler_params=pltpu.CompilerParams(has_side_effects=DATAFLOW),
    )(*hbm, *deps)
    return outs[0], outs[1], list(outs[2:2 + n]), outs[-1]


def _copies_wait(send_sems, recv_sems, bufs, after, plan, name, only=None):
    n = len(bufs)

    def body(*refs):
        ins = refs[:n]
        s_sems, r_sems = refs[n], refs[n + 1]
        for k, send, recv in plan(ins):
            if only is not None and k not in only:
                continue
            if send is not None:
                src, dst, dev, pred = send
                cp = pltpu.make_async_remote_copy(src_ref=src, dst_ref=dst, send_sem=s_sems.at[k],
                                                  recv_sem=r_sems.at[k], device_id=dev, device_id_type=MESH)
                if pred is None:
                    cp.wait_send()
                else:
                    pl.when(pred)(cp.wait_send)
            if recv is not None:
                dst, pred = recv
                cp = pltpu.make_async_remote_copy(src_ref=dst, dst_ref=dst, send_sem=s_sems.at[k],
                                                  recv_sem=r_sems.at[k], device_id=_coords(), device_id_type=MESH)
                if pred is None:
                    cp.wait_recv()
                else:
                    pl.when(pred)(cp.wait_recv)

    outs = pl.pallas_call(
        body,
        name=name,
        in_specs=[HBM_SPEC] * n + [SEM_SPEC, SEM_SPEC, pl.BlockSpec(memory_space=pl.ANY)],
        out_specs=[HBM_SPEC] * n,
        out_shape=[pltpu.HBM(b.shape, b.dtype) for b in bufs],
        input_output_aliases={a: a for a in range(n)},
        compiler_params=pltpu.CompilerParams(has_side_effects=DATAFLOW),
    )(*bufs, send_sems, recv_sems, after)
    return list(outs)


def _gather_plan(n_bufs):
    def plan(refs):
        x, y, c = _coords()
        me = 2 * x + y
        out = []
        for k, (px, py) in enumerate(_other_chips(x, y)):
            for a in range(n_bufs):
                out.append((k * n_bufs + a, (refs[a].at[me], refs[a].at[me], (px, py, c), None),
                            (refs[a].at[2 * px + py], None)))
        return out
    return plan


def _cast_into_slot(ws, name, deps=()):
    n = len(ws)
    nt = 2

    def body(s_ref, *refs):
        outs = refs[len(refs) - n:]
        for a in range(n):
            outs[a][0] = refs[a][...].astype(outs[a].dtype)

    xi, yi, _ = _coords()
    return pl.pallas_call(
        body,
        name=name,
        grid_spec=pltpu.PrefetchScalarGridSpec(
            num_scalar_prefetch=1,
            grid=(2, nt),
            in_specs=[pl.BlockSpec((1, w.shape[1] // nt, w.shape[2]), lambda hf, i, s: (hf, i, 0)) for w in ws]
            + [ANY_SPEC] * len(deps),
            out_specs=[pl.BlockSpec((1, 1, w.shape[1] // nt, w.shape[2]), lambda hf, i, s: (s[0], hf, i, 0)) for w in ws],
        ),
        out_shape=[jax.ShapeDtypeStruct((N_CHIPS,) + w.shape, _MXU) for w in ws],
        compiler_params=_params(("parallel", "parallel")),
    )((2 * xi + yi).reshape(1).astype(jnp.int32), *ws, *deps)


def _chip_gather_plan(stage, n_bufs):
    def plan(refs):
        x, y, c = _coords()
        me = 2 * x + y
        near = [(1 - x, y), (x, 1 - y)]
        slots = [2 * (1 - x) + y, 2 * x + (1 - y), 2 * (1 - x) + (1 - y)]
        sibling = (x, y, 1 - c)
        pass_to = (jnp.where(c == 0, x, 1 - x), jnp.where(c == 0, 1 - y, y), c)
        pass_slot = jnp.where(c == 0, slots[0], slots[1])
        out = []

        def move(src_slot, to, land_slot, land_core, pieces):
            for a, buf in enumerate(refs):
                for rows in _chunks(buf.shape[2], pieces[a]):
                    out.append((len(out), (buf.at[src_slot, c, rows], buf.at[src_slot, c, rows], to, None),
                                (buf.at[land_slot, land_core, rows], None)))

        if stage == "near":
            for k, chip in enumerate(near):
                move(me, (*chip, c), slots[k], c, NEAR_PIECES[:n_bufs])
        elif stage == "pass":
            move(pass_slot, pass_to, slots[2], c, PASS_PIECES[:n_bufs])
            for k in range(2):
                move(slots[k], sibling, slots[k], 1 - c, [1] * n_bufs)
        else:
            move(slots[2], sibling, slots[2], 1 - c, [1] * n_bufs)
        return out
    return plan


NEAR_PIECES = (2, 1)
PASS_PIECES = (2, 1)


def _chip_gather_copies(stage, n_bufs):
    if stage == "near":
        return 2 * sum(NEAR_PIECES[:n_bufs]), None
    if stage == "pass":
        n_pass = sum(PASS_PIECES[:n_bufs])
        return n_pass + 2 * n_bufs, set(range(n_pass))
    return n_bufs, None


def _swap_plan(n_slabs):
    def plan(refs):
        x, y, c = _coords()
        out, k = [], 0
        for i, n in enumerate(n_slabs):
            g, land = refs[2 * i], refs[2 * i + 1]
            for p in range(n):
                out.append((k, (g.at[p, 1 - c], land.at[p], (x, y, 1 - c), None), (land.at[p], None)))
                k += 1
        return out
    return plan


def _is_one_of(chip, dests):
    hit = chip == dests[0]
    for d in dests[1:]:
        hit = hit | (chip == d)
    return hit


def _slab_of(chip, dests):
    return sum(j * (chip == d).astype(jnp.int32) for j, d in enumerate(dests))


def _scatter_plan(dest_sets):
    def plan(refs):
        x, y, c = _coords()
        me = 2 * x + y
        out = []
        for k, (px, py) in enumerate(_other_chips(x, y)):
            peer = 2 * px + py
            for i, dests in enumerate(dest_sets):
                cs, land = refs[2 * i], refs[2 * i + 1]
                everyone = len(dests) == N_CHIPS
                send = (cs.at[_slab_of(peer, dests)], land.at[k], (px, py, c),
                        None if everyone else _is_one_of(peer, dests))
                recv = (land.at[k], None if everyone else _is_one_of(me, dests))
                out.append((k * len(dest_sets) + i, send, recv))
        return out
    return plan


def _join_plan(rows, n_pieces):
    def plan(refs):
        x, y, c = _coords()
        (buf,) = refs
        return [(i, (buf.at[c, piece], buf.at[c, piece], (x, y, 1 - c), None), (buf.at[1 - c, piece], None))
                for i, piece in enumerate(_chunks(rows, n_pieces))]
    return plan


def _join_plans(parts):
    def plan(refs):
        out, b0, k0 = [], 0, 0
        for part_plan, n_bufs, n_copies in parts:
            out += [(k0 + k, send, recv) for k, send, recv in part_plan(refs[b0:b0 + n_bufs])]
            b0 += n_bufs
            k0 += n_copies
        return out
    return plan


def _allgather_plan():
    def plan(refs):
        x, y, c = _coords()
        (land,) = refs
        me = 4 * x + 2 * y + c
        out = []
        for r in range(1, 8):
            px = 1 - x if r & 4 else x
            py = 1 - y if r & 2 else y
            pc = 1 - c if r & 1 else c
            out.append((r - 1, (land.at[me], land.at[me], (px, py, pc), None), (land.at[4 * px + 2 * py + pc], None)))
        return out
    return plan


def _sum_gathered(land, name):
    def body(land_ref, o_ref):
        acc = land_ref[0]
        for d in range(1, 8):
            acc = acc + land_ref[d]
        o_ref[...] = acc

    return pl.pallas_call(
        body,
        name=name,
        out_shape=jax.ShapeDtypeStruct(land.shape[1:], F32),
        compiler_params=_params(),
    )(land)


def _row_tile(rows, cap):
    t = cap
    while rows % t:
        t //= 2
    return t


def _add_my_half(g, r, name):
    n_slabs, _, R, C = g.shape
    tr = R if n_slabs > 1 else _row_tile(R, SUM_ROWS)

    def body(c_ref, g_ref, r_ref, o_ref):
        o_ref[...] = (g_ref[0] + r_ref[...]).astype(o_ref.dtype)

    return pl.pallas_call(
        body,
        name=name,
        grid_spec=pltpu.PrefetchScalarGridSpec(
            num_scalar_prefetch=1,
            grid=(n_slabs, R // tr),
            in_specs=[pl.BlockSpec((1, 1, tr, C), lambda p, i, c_ref: (p, c_ref[0], i, 0)),
                      pl.BlockSpec((1, tr, C), lambda p, i, c_ref: (p, i, 0))],
            out_specs=pl.BlockSpec((1, tr, C), lambda p, i, c_ref: (p, i, 0)),
        ),
        out_shape=jax.ShapeDtypeStruct(r.shape, jnp.bfloat16),
        compiler_params=_params(("parallel", "parallel")),
    )(lax.axis_index("c").reshape(1).astype(jnp.int32), g, r)


def _sum_slabs(own, got, name, deps=()):
    _, R, C = own.shape
    tr = _row_tile(R, SUM_ROWS)

    def body(s_ref, own_ref, got_ref, *rest):
        rest[-1][0] = ((own_ref[0].astype(F32) + got_ref[0].astype(F32)) + got_ref[1].astype(F32)) + got_ref[2].astype(F32)

    xi, yi, ci = _coords()
    return pl.pallas_call(
        body,
        name=name,
        grid_spec=pltpu.PrefetchScalarGridSpec(
            num_scalar_prefetch=1,
            grid=(R // tr,),
            in_specs=[pl.BlockSpec((1, tr, C), lambda i, s: (s[0], i, 0)),
                      pl.BlockSpec((3, tr, C), lambda i, s: (0, i, 0))] + [ANY_SPEC] * len(deps),
            out_specs=pl.BlockSpec((1, tr, C), lambda i, s: (s[1], i, 0)),
        ),
        out_shape=jax.ShapeDtypeStruct((2, R, C), F32),
        compiler_params=_params(("parallel",)),
    )(jnp.stack([2 * xi + yi, ci]).astype(jnp.int32), own, got, *deps)


def _sum_parts(owns, got, dest_sets, name):
    n = len(owns)
    _, R, C = owns[0].shape
    tr = _row_tile(R, SUM_ROWS)

    def body(s_ref, *refs):
        got_ref, o_ref = refs[n], refs[-1]
        total = jnp.zeros((tr, C), F32)
        for i in range(n):
            total = total + jnp.where(s_ref[2 + 2 * i] == 1, refs[i][0].astype(F32), 0.0)
        o_ref[0] = ((total + got_ref[0].astype(F32)) + got_ref[1].astype(F32)) + got_ref[2].astype(F32)

    xi, yi, ci = _coords()
    me = 2 * xi + yi
    scalars = [ci, ci]
    for dests in dest_sets:
        scalars += [_is_one_of(me, dests).astype(jnp.int32), _slab_of(me, dests)]
    own_spec = lambda i: pl.BlockSpec((1, tr, C), lambda r, s: (s[3 + 2 * i], r, 0))
    return pl.pallas_call(
        body,
        name=name,
        grid_spec=pltpu.PrefetchScalarGridSpec(
            num_scalar_prefetch=1,
            grid=(R // tr,),
            in_specs=[own_spec(i) for i in range(n)] + [pl.BlockSpec((3, tr, C), lambda r, s: (0, r, 0))],
            out_specs=pl.BlockSpec((1, tr, C), lambda r, s: (s[0], r, 0)),
        ),
        out_shape=jax.ShapeDtypeStruct((2, R, C), F32),
        compiler_params=_params(("parallel",)),
    )(jnp.stack(scalars).astype(jnp.int32), *owns, got)


def _adamw_math(w, g, m, v):
    m = ADAM_B1 * m + (1.0 - ADAM_B1) * g
    v = ADAM_B2 * v + (1.0 - ADAM_B2) * (g * g)
    m_hat = m / (1.0 - ADAM_B1 ** ADAM_STEP)
    v_hat = v / (1.0 - ADAM_B2 ** ADAM_STEP)
    delta = -ADAM_LR * (m_hat / (jnp.sqrt(v_hat) + ADAM_EPS) + ADAM_WD * w)
    return delta, m, v


def _adamw_halves(ws, g, ms, vs, half, prev, name, deps=()):
    n = len(ws)
    _, _, R, C = g.shape
    tr = _row_tile(R, ADAMW_ROWS)
    steps = R // tr
    carried = [] if prev is None else [a for four in prev for a in four]
    both = half is None
    which = (lambda i, s: i // steps) if both else (lambda i, s: s[0])
    half = 0 if both else half

    def body(s_ref, *refs):
        w_refs, g_refs, m_refs, v_refs = (refs[k * n:(k + 1) * n] for k in range(4))
        outs = refs[len(refs) - 4 * n:]
        for a in range(n):
            grad = g_refs[a][0, 0]
            d, mn, vn = _adamw_math(w_refs[a][...], grad, m_refs[a][...], v_refs[a][...])
            for o, val in zip(outs[4 * a:4 * a + 4], (grad, d, mn, vn)):
                o[...] = val

    rows = pl.BlockSpec((tr, C), lambda i, s: (which(i, s) * steps + i % steps, 0))
    grad_spec = lambda a: pl.BlockSpec((1, 1, tr, C), lambda i, s: (which(i, s), a, i % steps, 0))
    n_in = 4 * n
    outs = pl.pallas_call(
        body,
        name=name,
        grid_spec=pltpu.PrefetchScalarGridSpec(
            num_scalar_prefetch=1,
            grid=(2 * steps if both else steps,),
            in_specs=[rows] * n + [grad_spec(a) for a in range(n)] + [rows] * (2 * n)
            + [ANY_SPEC] * (len(carried) + len(deps)),
            out_specs=[rows] * (4 * n),
        ),
        out_shape=[jax.ShapeDtypeStruct((2 * R, C), F32)] * (4 * n),
        input_output_aliases={1 + n_in + k: k for k in range(len(carried))},
        compiler_params=_params(("parallel",)),
    )(jnp.reshape(half, (1,)).astype(jnp.int32), *ws, *([g] * n), *ms, *vs, *carried, *deps)
    return [outs[4 * a:4 * a + 4] for a in range(n)]


def _adamw_small(ws, gs, ms, vs, name):
    n = len(ws)

    def body(*refs):
        for a in range(n):
            d, mn, vn = _adamw_math(refs[a][...], refs[n + a][...], refs[2 * n + a][...], refs[3 * n + a][...])
            refs[4 * n + a][...] = d
            refs[5 * n + a][...] = mn
            refs[6 * n + a][...] = vn

    shapes = [jax.ShapeDtypeStruct(w.shape, F32) for w in ws]
    outs = pl.pallas_call(
        body,
        name=name,
        out_shape=shapes * 3,
        compiler_params=_params(),
    )(*ws, *gs, *ms, *vs)
    return outs[:n], outs[n:2 * n], outs[2 * n:]


def _to_blockdiag(w):
    per = CW // LRU_BW
    w4 = w.reshape(N_CT, per, LRU_BW, LRU_BW)
    eye = jnp.eye(per, dtype=w.dtype)
    return (w4[:, :, :, None, :] * eye[None, :, None, :, None]).reshape(N_CT, CW, CW)


def _from_blockdiag(g):
    per = CW // LRU_BW
    g5 = g.reshape(N_CT, per, LRU_BW, per, LRU_BW)
    return jnp.stack([g5[:, b, :, b, :] for b in range(per)], axis=1).reshape(LRU_BLOCKS, LRU_BW, LRU_BW)


def _local_grads(x2d, tgt2d, B, S, g_in, in_proj, conv_b, gate_x_w, gate_x_b, gate_a_w, gate_a_b, lam,
                 proj_weights, g_fin, reduce):
    wx_bd = _c(_to_blockdiag(gate_x_w))
    wa_bd = _c(_to_blockdiag(gate_a_w))
    tables = _retention_tables(S)

    proj, ht, w_all, conv_w, gain = in_proj(x2d, g_in, (*tables, wx_bd, wa_bd))
    gain3 = gain.reshape(HEADS, 1, DK)
    hlru, ya = _lru_fwd(proj, conv_w, conv_b, wx_bd, wa_bd, gate_x_b, gate_a_b, lam, B, S)
    o_pre, yb, states = _ret_fwd(proj, tables, gain3, B, S)
    wpa, wpb, wout = proj_weights(yb)
    loss, dx2, dya, dyb, dm, dgf, gw_proj = _mid(ya, yb, proj, x2d, tgt2d, wpa, wpb, wout, g_fin)
    g3 = _inproj_bwd_dw(ht, [dm], "inproj_bwd_dw_m")
    deps = reduce.m_ready(gw_proj, g3)
    dr, dgain = _ret_bwd(dyb, o_pre, proj, states, tables, gain3, B, S, deps)
    deps = reduce.ret_done(dr)
    g12 = _inproj_bwd_dw(ht, [dr], "inproj_bwd_dw_r", deps)
    deps = reduce.r_ready(g12)
    dxa, dga, dcw, dcb, dwx_bd, dwa_bd, dbx, dba, dlam = _lru_bwd(
        dya, proj, hlru, conv_w, conv_b, wx_bd, wa_bd, gate_x_b, gate_a_b, lam, B, S, deps)
    small = dict(conv_w=dcw, conv_b=dcb, gate_x_w=_from_blockdiag(dwx_bd), gate_x_b=dbx,
                 gate_a_w=_from_blockdiag(dwa_bd), gate_a_b=dba, lru_lambda=dlam, gn_gain=dgain.reshape(HEADS, DK),
                 norm_final=dgf)
    loss_rows = jnp.broadcast_to(loss, (SUBLANES, LANES))
    deps = reduce.lru_done(dxa, jnp.concatenate([_pack_small(small), loss_rows], axis=0))
    g0 = _inproj_bwd_dw(ht, [dxa, dga], "inproj_bwd_dw_a", deps)
    deps = reduce.a_ready(g0)
    n_tiles = x2d.shape[0] // min(DX_TILE, x2d.shape[0])
    grad_x, dgin = _inproj_bwd_dx([dxa, dga, dr, dm], w_all, x2d, dx2, g_in, 0, n_tiles, None, "inproj_bwd_dx", deps)
    return grad_x, dgin


ALL_CHIPS = (0, 1, 2, 3)


class _GradReduce:
    def __init__(self, proj_done):
        self.pending = {}
        self.proj_done = proj_done
        self.land_in = None

    def _start(self, key, parts, name):
        bufs, plans, shared = [], [], None
        for part_bufs, plan, n_copies, part_shared in parts:
            if part_shared is not None:
                shared = len(bufs) + part_shared
            plans.append((plan, len(part_bufs), n_copies))
            bufs += part_bufs
        plan = _join_plans(plans)
        send_sems, recv_sems, bufs, token = _copies_start(bufs, plan, sum(p[2] for p in plans), name + "_start")
        if shared is not None:
            self.land_in = bufs[shared]
        self.pending[key] = (send_sems, recv_sems, bufs, plan, name + "_wait", shared)
        return (token,)

    def _finish(self, key, after):
        send_sems, recv_sems, bufs, plan, name, shared = self.pending.pop(key)
        if shared is not None:
            bufs[shared] = self.land_in
        bufs = _copies_wait(send_sems, recv_sems, bufs, after, plan, name)
        if shared is not None:
            self.land_in = bufs[shared]
        return bufs

    @staticmethod
    def _swap(pieces):
        bufs = []
        for g in pieces:
            bufs += [g, lax.empty((g.shape[0],) + g.shape[2:], F32)]
        n_slabs = [g.shape[0] for g in pieces]
        return bufs, _swap_plan(n_slabs), sum(n_slabs), None

    def _scatter(self, sums, dest_sets):
        bufs = []
        for cs in sums:
            bufs += [cs, lax.empty((3,) + cs.shape[1:], cs.dtype)]
        if self.land_in is not None:
            bufs[-1] = self.land_in
        return bufs, _scatter_plan(dest_sets), 3 * len(sums), len(bufs) - 1

    @staticmethod
    def _gather8(block):
        x, y, c = _coords()
        land = lax.dynamic_update_slice(lax.empty((8,) + block.shape, F32), block[None], (4 * x + 2 * y + c, 0, 0))
        return [land], _allgather_plan(), 7, None

    def m_ready(self, gw_proj, g3):
        rows = gw_proj.shape[2] * gw_proj.shape[3]
        return self._start("m", [self._swap([gw_proj.reshape(N_CHIPS, 2, rows, D_MODEL), g3])], "swap_m")

    def ret_done(self, after):
        proj, land_p, g3, land_3 = self._finish("m", after)
        sums_m = [_add_my_half(proj, land_p, "chip_sum_proj"), _add_my_half(g3, land_3, "chip_sum_m")]
        return self._start("sm", [self._scatter(sums_m, [ALL_CHIPS, (3,)])], "scatter_m")

    def r_ready(self, g12):
        return self._start("r", [self._swap([g12])], "swap_r")

    def lru_done(self, after, packed):
        g12, land_12 = self._finish("r", after)
        sums_r = [_add_my_half(g12, land_12, "chip_sum_r")]
        return (self._start("sr", [self._scatter(sums_r, [(1, 2)])], "scatter_r")
                + self._start("small", [self._gather8(packed)], "gather_small"))

    def a_ready(self, g0):
        (token,) = self._start("a", [self._swap([g0])], "swap_a")
        csp, gotp, self.cs3, _ = self._finish("sm", token)
        half_proj = _sum_slabs(csp, gotp, "sum_w_proj")
        g0, land_0 = self._finish("a", half_proj)
        join = ([half_proj], _join_plan(half_proj.shape[1], PROJ_JOIN_PIECES), PROJ_JOIN_PIECES, None)
        return self._start("sa", [self._scatter([_add_my_half(g0, land_0, "chip_sum_a")], [(0,)]), join], "scatter_a")

    def finish(self, dgin, w_in_done):
        (token,) = self._start("n", [self._gather8(dgin)], "gather_norm_in")
        (small,) = self._finish("small", token)
        cs12, _ = self._finish("sr", token)
        cs0, _, g_proj = self._finish("sa", token)
        self.proj_done(g_proj)
        half_in =_sum_parts([self.cs3, cs12, cs0], self.land_in, [(3,), (1, 2), (0,)], "sum_w_in")
        deps = self._start("j", [([half_in], _join_plan(half_in.shape[1], JOIN_PIECES), JOIN_PIECES, None)], "join_w_in")
        first = w_in_done(self.pending["j"][2][0], True, None, deps)
        (g_in,) = self._finish("j", first[1])
        done = w_in_done(g_in, False, first, ())
        (norm_in,) = self._finish("n", done[1])
        return _sum_gathered(small, "sum_small_grads"), _sum_gathered(norm_in, "sum_norm_in_grad")


_SMALL = ("gate_x_w", "gate_a_w", "conv_w", "conv_b", "gate_x_b", "gate_a_b", "lru_lambda", "gn_gain", "norm_final")
_SMALL_SHAPES = dict(gate_x_w=(LRU_BLOCKS, LRU_BW, LRU_BW), gate_a_w=(LRU_BLOCKS, LRU_BW, LRU_BW),
                     norm_in=(1, D_MODEL), conv_w=(CONV, D_MODEL), conv_b=(1, D_MODEL), gate_x_b=(1, D_MODEL),
                     gate_a_b=(1, D_MODEL), lru_lambda=(1, D_MODEL), gn_gain=(HEADS, DK), norm_final=(1, D_MODEL))


def _pack_small(small):
    return jnp.concatenate([small[k].reshape(-1, 128) for k in _SMALL], axis=0)


def _unpack_small(packed):
    out, r = {}, 0
    for k in _SMALL:
        shape = _SMALL_SHAPES[k]
        rows = 1
        for s in shape:
            rows *= s
        rows //= 128
        out[k] = packed[r:r + rows].reshape(shape)
        r += rows
    return out


def kernel(x, norm_in, w_in, conv_w, conv_b, gate_x_w, gate_x_b, gate_a_w, gate_a_b, lru_lambda, gn_gain, w_proj_a, w_proj_b, w_out, norm_final, loss_target, m_norm_in, m_w_in, m_conv_w, m_conv_b, m_gate_x_w, m_gate_x_b, m_gate_a_w, m_gate_a_b, m_lru_lambda, m_gn_gain, m_w_proj_a, m_w_proj_b, m_w_out, m_norm_final, v_norm_in, v_w_in, v_conv_w, v_conv_b, v_gate_x_w, v_gate_x_b, v_gate_a_w, v_gate_a_b, v_lru_lambda, v_gn_gain, v_w_proj_a, v_w_proj_b, v_w_out, v_norm_final):
    B, S, _ = x.shape
    T = B * S
    xi, yi, ci = _coords()
    chip = 2 * xi + yi

    cshard = D_MODEL // N_CHIPS
    mine = _cast_into_slot([w_in[0].reshape(2, D_MODEL // 2, 2 * D_MODEL)], "cast_w_in")
    plan = _gather_plan(3)
    pending_proj = []
    gshard = DK // N_CHIPS
    tiny = jnp.concatenate([conv_w[0], jnp.zeros((4, cshard), F32), jnp.pad(gn_gain[0], ((0, 4), (0, cshard - gshard)))],
                           axis=0).reshape(1, 2, SUBLANES, cshard)
    tiny_buf = lax.dynamic_update_slice(lax.empty((N_CHIPS, 2, SUBLANES, cshard), F32), tiny, (chip, 0, 0, 0))
    near_plan, pass_plan, far_plan = (_chip_gather_plan(stage, 2) for stage in ("near", "pass", "far"))
    (n_near, _), (n_pass, passed_on), (n_far, _) = (_chip_gather_copies(stage, 2) for stage in ("near", "pass", "far"))
    halves = set(range(n_pass)) - passed_on
    near_s, near_r, bufs, near_token = _copies_start([mine[0], tiny_buf], near_plan, n_near, "gather_near_start")

    def in_proj(x2d, g_in, meanwhile):
        mine_proj = _cast_into_slot([w[0].reshape(2, cshard // 2, D_MODEL) for w in (w_proj_a, w_proj_b, w_out)],
                                    "cast_w_proj", (near_token,))
        as_w = lambda b: b[0].reshape(N_CHIPS, D_MODEL, 2 * D_MODEL)
        slot_x, slot_y, slot_d = 2 * (1 - xi) + yi, 2 * xi + (1 - yi), 2 * (1 - xi) + (1 - yi)
        ids = lambda *chips: jnp.stack(chips).astype(jnp.int32)
        proj, hb, ht = _inproj_first(x2d, g_in, as_w(bufs), ids(chip), "inproj_own", (near_token, *meanwhile))
        got = _copies_wait(near_s, near_r, bufs, proj, near_plan, "gather_near_wait")
        pass_s, pass_r, got, pass_token = _copies_start(got, pass_plan, n_pass, "gather_pass_start")
        got = _copies_wait(pass_s, pass_r, got, pass_token, pass_plan, "gather_pass_wait_halves", only=halves)
        proj = _inproj_more(hb, as_w(got), ids(slot_x, slot_y), proj, "inproj_near")
        got = _copies_wait(pass_s, pass_r, got, proj, pass_plan, "gather_pass_wait_far", only=passed_on)
        pending_proj.append(_copies_start(mine_proj, plan, 9, "gather_proj_start", (got[0],)))
        far_s, far_r, got, far_token = _copies_start(got, far_plan, n_far, "gather_far_start")
        got = _copies_wait(far_s, far_r, got, far_token, far_plan, "gather_far_wait")
        proj = _inproj_more(hb, as_w(got), ids(slot_d), proj, "inproj_far")
        tiny_all = got[1].reshape(N_CHIPS, 2 * SUBLANES, cshard)
        conv_w_full = jnp.transpose(tiny_all[:, 0:CONV, :], (1, 0, 2)).reshape(CONV, D_MODEL)
        gain_full = jnp.transpose(tiny_all[:, 8:8 + HEADS, :gshard], (1, 0, 2)).reshape(HEADS, DK)
        return proj, ht, as_w(got), conv_w_full, gain_full

    def proj_weights(after):
        s_sems, r_sems, pbufs, _ = pending_proj[0]
        got = _copies_wait(s_sems, r_sems, pbufs, after, plan, "gather_proj_wait")
        return [b.reshape(D_MODEL, D_MODEL) for b in got]

    weights = dict(norm_in=norm_in, w_in=w_in, conv_w=conv_w, conv_b=conv_b, gate_x_w=gate_x_w, gate_x_b=gate_x_b,
                   gate_a_w=gate_a_w, gate_a_b=gate_a_b, lru_lambda=lru_lambda, gn_gain=gn_gain, w_proj_a=w_proj_a,
                   w_proj_b=w_proj_b, w_out=w_out, norm_final=norm_final)
    ms = dict(norm_in=m_norm_in, w_in=m_w_in, conv_w=m_conv_w, conv_b=m_conv_b, gate_x_w=m_gate_x_w,
              gate_x_b=m_gate_x_b, gate_a_w=m_gate_a_w, gate_a_b=m_gate_a_b, lru_lambda=m_lru_lambda, gn_gain=m_gn_gain,
              w_proj_a=m_w_proj_a, w_proj_b=m_w_proj_b, w_out=m_w_out, norm_final=m_norm_final)
    vs = dict(norm_in=v_norm_in, w_in=v_w_in, conv_w=v_conv_w, conv_b=v_conv_b, gate_x_w=v_gate_x_w,
              gate_x_b=v_gate_x_b, gate_a_w=v_gate_a_w, gate_a_b=v_gate_a_b, lru_lambda=v_lru_lambda, gn_gain=v_gn_gain,
              w_proj_a=v_w_proj_a, w_proj_b=v_w_proj_b, w_out=v_w_out, norm_final=v_norm_final)
    names = list(weights)
    grads, delta, new_m, new_v = {}, {}, {}, {}

    def update_big(keys, g, half, prev, name, deps=()):
        two = lambda a: a.reshape(a.shape[1], a.shape[2])
        res = _adamw_halves([two(weights[k]) for k in keys], g, [two(ms[k]) for k in keys], [two(vs[k]) for k in keys],
                            half, prev, name, deps)
        for k, (gk, d, mn, vn) in zip(keys, res):
            shp = weights[k].shape
            grads[k], delta[k], new_m[k], new_v[k] = gk.reshape(shp), d.reshape(shp), mn.reshape(shp), vn.reshape(shp)
        return res

    def proj_done(g_proj):
        g4 = g_proj.reshape(2, 3, D_MODEL // (2 * N_CHIPS), D_MODEL)
        return update_big(("w_proj_a", "w_proj_b", "w_out"), g4, None, None, "adamw_proj")[-1][1]

    def w_in_done(g_in, own, prev, deps):
        g4 = g_in.reshape(2, 1, D_MODEL // 2, 2 * D_MODEL)
        return update_big(("w_in",), g4, ci if own else 1 - ci, None if prev is None else [prev],
                          "adamw_w_in_own" if own else "adamw_w_in_other", deps)[0]

    reduce = _GradReduce(proj_done)
    grad_x, dgin = _local_grads(
        x.reshape(T, D_MODEL), loss_target.reshape(T, D_MODEL), B, S, norm_in, in_proj, conv_b,
        gate_x_w[0], gate_x_b, gate_a_w[0], gate_a_b, lru_lambda, proj_weights,
        norm_final.reshape(1, D_MODEL), reduce)

    small_sum, g_norm_in = reduce.finish(dgin.reshape(SUBLANES, LANES), w_in_done)
    loss = small_sum[small_sum.shape[0] - SUBLANES, 0]

    gsm = _unpack_small(small_sum)
    gsm["norm_in"] = g_norm_in
    gsm["conv_w"] = lax.dynamic_slice_in_dim(gsm["conv_w"], chip * cshard, cshard, axis=1)
    gsm["gn_gain"] = lax.dynamic_slice_in_dim(gsm["gn_gain"], chip * gshard, gshard, axis=1)
    smalls = [k for k in names if k not in delta]

    def view(a):
        return a.reshape(1, -1) if a.ndim == 1 else (a.reshape(a.shape[1:]) if a.ndim > 2 else a)

    ds, mns, vns = _adamw_small([view(weights[k]) for k in smalls], [gsm[k].reshape(view(weights[k]).shape) for k in smalls],
                                [view(ms[k]) for k in smalls], [view(vs[k]) for k in smalls], "adamw_small")
    for k, d, mn, vn in zip(smalls, ds, mns, vns):
        shp = weights[k].shape
        grads[k], delta[k], new_m[k], new_v[k] = gsm[k].reshape(shp), d.reshape(shp), mn.reshape(shp), vn.reshape(shp)

    return (loss, grad_x.reshape(B, S, D_MODEL), *[grads[k] for k in names], *[delta[k] for k in names],
            *[new_m[k] for k in names], *[new_v[k] for k in names])
```

```python
import jax
import jax.numpy as jnp
from jax import lax
from jax.experimental import pallas as pl
from jax.experimental.pallas import tpu as pltpu

F32 = jnp.float32
_MXU = jnp.bfloat16

D_MODEL = 1024
N_GROUPS = 8
HEADS = 4
DK = 256
CHUNK = 128
CONV = 4
LRU_BLOCKS = 16
LRU_BW = 64
LRU_C = 8.0
ROPE_THETA = 10000.0
EPS = 1e-6
CW = 256
N_CT = D_MODEL // CW
N_CHIPS = 4
MESH = pl.DeviceIdType.MESH

ADAM_LR = 0.001
ADAM_B1 = 0.9
ADAM_B2 = 0.999
ADAM_EPS = 1e-08
ADAM_WD = 0.01
ADAM_STEP = 10

VMEM_LIMIT = 56 * 1024 * 1024

FIRST_PROJ_TILE = 1024
MORE_PROJ_TILE = 2048
SCAN_TILE = 1024
MID_TILE = 256
DX_TILE = 512
DW_COLS = 512
DW_LOADS = 4
RET_CHUNKS = 2
SUM_ROWS = 256
ADAMW_ROWS = 256
JOIN_PIECES = 8
PROJ_JOIN_PIECES = 4


def _c(v):
    return v.astype(_MXU)


def _dot(a, b):
    return lax.dot_general(a, b, (((1,), (0,)), ((), ())), preferred_element_type=F32)


def _dot_nt(a, b):
    return lax.dot_general(a, b, (((1,), (1,)), ((), ())), preferred_element_type=F32)


def _dot_tn(a, b):
    return lax.dot_general(a, b, (((0,), (0,)), ((), ())), preferred_element_type=F32)


def _sigmoid(z):
    return 0.5 * jnp.tanh(0.5 * z) + 0.5


ANY_SPEC = pl.BlockSpec(memory_space=pl.ANY)


def _after(body, n_in, deps):
    n_deps = len(deps)

    def wrapped(*refs):
        return body(*refs[:n_in], *refs[n_in + n_deps:])

    return wrapped


def _params(sem=None):
    if sem is None:
        return pltpu.CompilerParams(vmem_limit_bytes=VMEM_LIMIT)
    return pltpu.CompilerParams(vmem_limit_bytes=VMEM_LIMIT, dimension_semantics=sem)


def _inproj_first(x2d, g_in, w_all, chips, name, deps=()):
    T = x2d.shape[0]
    tm = min(FIRST_PROJ_TILE, T)
    n_i = T // tm

    def body(s_ref, *refs):
        x_ref, g_ref, w_ref = refs[:3]
        proj_ref, hb_ref, ht_ref, h_all = refs[-4:]
        i = pl.program_id(1)
        rows = pl.ds(pl.multiple_of(i * tm, tm), tm)

        @pl.when(pl.program_id(0) == 0)
        def _():
            x = x_ref[...]
            r = lax.rsqrt(jnp.mean(x * x, axis=-1, keepdims=True) + EPS)
            h = x * r * g_ref[...]
            hb = h.astype(h_all.dtype)
            h_all[rows, :] = hb
            hb_ref[...] = hb
            ht_ref[...] = h.T.astype(ht_ref.dtype)

        proj_ref[...] = _dot(h_all[rows, :], w_ref[0])

    first = lambda j, i: jnp.where(j == 0, i, n_i - 1)
    return pl.pallas_call(
        body,
        name=name,
        grid_spec=pltpu.PrefetchScalarGridSpec(
            num_scalar_prefetch=1,
            grid=(2 * chips.shape[0], n_i),
            in_specs=[
                pl.BlockSpec((tm, D_MODEL), lambda j, i, s: (first(j, i), 0)),
                pl.BlockSpec((1, D_MODEL), lambda j, i, s: (0, 0)),
                pl.BlockSpec((1, D_MODEL, D_MODEL), lambda j, i, s: (s[j // 2], 0, j % 2)),
            ] + [ANY_SPEC] * len(deps),
            out_specs=[
                pl.BlockSpec((tm, D_MODEL), lambda j, i, s: (i, 2 * s[j // 2] + j % 2)),
                pl.BlockSpec((tm, D_MODEL), lambda j, i, s: (first(j, i), 0)),
                pl.BlockSpec((D_MODEL, tm), lambda j, i, s: (0, first(j, i))),
            ],
            scratch_shapes=[pltpu.VMEM((T, D_MODEL), _MXU)],
        ),
        out_shape=[
            jax.ShapeDtypeStruct((T, N_GROUPS * D_MODEL), F32),
            jax.ShapeDtypeStruct((T, D_MODEL), _MXU),
            jax.ShapeDtypeStruct((D_MODEL, T), _MXU),
        ],
        compiler_params=_params(("arbitrary", "arbitrary")),
    )(chips, x2d, g_in, w_all, *deps)


def _inproj_more(hb, w_all, chips, proj, name):
    T = hb.shape[0]
    tm = min(MORE_PROJ_TILE, T)

    def body(s_ref, hb_hbm, w_ref, prev_ref, proj_ref, h_all, sem):
        @pl.when((pl.program_id(0) == 0) & (pl.program_id(1) == 0))
        def _():
            cp = pltpu.make_async_copy(hb_hbm, h_all, sem)
            cp.start()
            cp.wait()

        rows = pl.ds(pl.multiple_of(pl.program_id(1) * tm, tm), tm)
        proj_ref[...] = _dot(h_all[rows, :], w_ref[0])

    return pl.pallas_call(
        body,
        name=name,
        grid_spec=pltpu.PrefetchScalarGridSpec(
            num_scalar_prefetch=1,
            grid=(2 * chips.shape[0], T // tm),
            in_specs=[
                ANY_SPEC,
                pl.BlockSpec((1, D_MODEL, D_MODEL), lambda j, i, s: (s[j // 2], 0, j % 2)),
                ANY_SPEC,
            ],
            out_specs=pl.BlockSpec((tm, D_MODEL), lambda j, i, s: (i, 2 * s[j // 2] + j % 2)),
            scratch_shapes=[pltpu.VMEM((T, D_MODEL), hb.dtype), pltpu.SemaphoreType.DMA],
        ),
        out_shape=jax.ShapeDtypeStruct(proj.shape, F32),
        input_output_aliases={3: 0},
        compiler_params=_params(("arbitrary", "arbitrary")),
    )(chips, hb, w_all, proj)


def _scan_fwd(a, u):
    n = a.shape[0]
    row = lax.broadcasted_iota(jnp.int32, a.shape, 0)
    s = 1
    while s < n:
        m = row >= s
        u = u + a * jnp.where(m, pltpu.roll(u, s, 0), 0.0)
        a = a * jnp.where(m, pltpu.roll(a, s, 0), 1.0)
        s *= 2
    return a, u


def _scan_bwd(b, g):
    n = b.shape[0]
    row = lax.broadcasted_iota(jnp.int32, b.shape, 0)
    s = 1
    while s < n:
        m = row < n - s
        g = g + b * jnp.where(m, pltpu.roll(g, n - s, 0), 0.0)
        b = b * jnp.where(m, pltpu.roll(b, n - s, 0), 1.0)
        s *= 2
    return b, g


LANES = 128
SUBLANES = 8


def _scan_scratch(tc):
    by_lanes = pltpu.VMEM((CW // LANES, tc, LANES), F32)
    return [by_lanes, by_lanes, pltpu.VMEM((tc // SUBLANES, CW), F32), pltpu.VMEM((tc, CW), F32)]


def _scan_tile(a, u, edge, la_ref, lh_ref, c_ref, dst_ref, reverse):
    n, w = a.shape
    groups = n // SUBLANES
    a3 = a.reshape(groups, SUBLANES, w)
    u3 = u.reshape(groups, SUBLANES, w)
    row = lax.broadcasted_iota(jnp.int32, a3.shape, 1)
    for s in (1, 2, 4):
        m = (row < SUBLANES - s) if reverse else (row >= s)
        shift = SUBLANES - s if reverse else s
        u3 = u3 + a3 * jnp.where(m, pltpu.roll(u3, shift, 1), 0.0)
        a3 = a3 * jnp.where(m, pltpu.roll(a3, shift, 1), 1.0)
    al = a3.reshape(n, w)
    hl = u3.reshape(n, w)
    blocks = w // LANES
    for q in range(blocks):
        la_ref[q] = al[:, q * LANES:(q + 1) * LANES]
        lh_ref[q] = hl[:, q * LANES:(q + 1) * LANES]
    ends = pl.ds(0 if reverse else SUBLANES - 1, groups, stride=SUBLANES)
    end_a = jnp.concatenate([la_ref.at[q][ends, :] for q in range(blocks)], axis=-1)
    end_h = jnp.concatenate([lh_ref.at[q][ends, :] for q in range(blocks)], axis=-1)
    prod, part = (_scan_bwd if reverse else _scan_fwd)(end_a, end_h)
    total = part + prod * edge
    g_row = lax.broadcasted_iota(jnp.int32, total.shape, 0)
    if reverse:
        c_ref[...] = jnp.where(g_row == groups - 1, edge, pltpu.roll(total, groups - 1, 0))
    else:
        c_ref[...] = jnp.where(g_row == 0, edge, pltpu.roll(total, 1, 0))
    for g in range(groups):
        rows = slice(g * SUBLANES, (g + 1) * SUBLANES)
        for q in range(blocks):
            cols = slice(q * LANES, (q + 1) * LANES)
            dst_ref[rows, cols] = lh_ref[q, rows, :] + la_ref[q, rows, :] * c_ref[g:g + 1, cols]


def _softplus_neg(lam):
    z = -lam
    return jnp.maximum(z, 0.0) + jnp.log1p(jnp.exp(-jnp.abs(z)))


def _lru_gates(xc, wx_ref, wa_ref, bx_ref, ba_ref, lam_ref):
    xcb = _c(xc)
    i_t = _sigmoid(_dot(xcb, wx_ref[0]) + bx_ref[...])
    r_t = _sigmoid(_dot(xcb, wa_ref[0]) + ba_ref[...])
    sp = _softplus_neg(lam_ref[...])
    log_a = (-LRU_C) * r_t * sp
    a = jnp.exp(log_a)
    mult = jnp.sqrt(1.0 - a * a)
    return xcb, i_t, r_t, sp, a, mult


def _conv_from_ext(ext_ref, xa, cw_ref, cb_ref, tc):
    return (cb_ref[...] + cw_ref[3:4, :] * xa + cw_ref[2:3, :] * ext_ref[7:7 + tc, :]
            + cw_ref[1:2, :] * ext_ref[6:6 + tc, :] + cw_ref[0:1, :] * ext_ref[5:5 + tc, :])


def _lru_fwd(proj, conv_w, conv_b, wx_bd, wa_bd, bx, ba, lam, B, S):
    T = B * S
    tc = min(SCAN_TILE, S)
    nt = S // tc
    h8 = tc // 8

    def body(xa_ref, halo_ref, ga_ref, cw_ref, cb_ref, wx_ref, wa_ref, bx_ref, ba_ref, lam_ref,
             h_ref, ya_ref, ext_ref, carry_ref, la_ref, lh_ref, c_ref):
        t = pl.program_id(2)

        @pl.when(t == 0)
        def _():
            carry_ref[...] = jnp.zeros_like(carry_ref)

        xa = xa_ref[...]
        ext_ref[0:8, :] = jnp.where(t == 0, 0.0, halo_ref[...])
        ext_ref[8:8 + tc, :] = xa
        xc = _conv_from_ext(ext_ref, xa, cw_ref, cb_ref, tc)
        _, i_t, _, _, a, mult = _lru_gates(xc, wx_ref, wa_ref, bx_ref, ba_ref, lam_ref)
        u = mult * (i_t * xc)
        _scan_tile(a, u, carry_ref[7:8, :], la_ref, lh_ref, c_ref, h_ref, False)
        h = h_ref[...]
        carry_ref[...] = h[tc - 8:tc, :]
        ga = ga_ref[...]
        ya_ref[...] = (ga * _sigmoid(ga) * h).astype(ya_ref.dtype)

    row = lambda b, t: b * nt + t
    vec = pl.BlockSpec((1, CW), lambda b, c, t: (0, c))
    mat = pl.BlockSpec((1, CW, CW), lambda b, c, t: (c, 0, 0))
    return pl.pallas_call(
        body,
        name="lru_fwd",
        grid=(B, N_CT, nt),
        in_specs=[
            pl.BlockSpec((tc, CW), lambda b, c, t: (row(b, t), c)),
            pl.BlockSpec((8, CW), lambda b, c, t: (jnp.maximum(row(b, t) * h8 - 1, 0), c)),
            pl.BlockSpec((tc, CW), lambda b, c, t: (row(b, t), N_CT + c)),
            pl.BlockSpec((CONV, CW), lambda b, c, t: (0, c)),
            vec, mat, mat, vec, vec, vec,
        ],
        out_specs=[
            pl.BlockSpec((tc, CW), lambda b, c, t: (row(b, t), c)),
            pl.BlockSpec((tc, CW), lambda b, c, t: (row(b, t), c)),
        ],
        out_shape=[
            jax.ShapeDtypeStruct((T, D_MODEL), F32),
            jax.ShapeDtypeStruct((T, D_MODEL), _MXU),
        ],
        scratch_shapes=[pltpu.VMEM((tc + 8, CW), F32), pltpu.VMEM((8, CW), F32)] + _scan_scratch(tc)[:3],
        compiler_params=_params(("parallel", "parallel", "arbitrary")),
    )(proj, proj, proj, conv_w, conv_b, wx_bd, wa_bd, bx, ba, lam)


def _lru_bwd(dya, proj, hlru, conv_w, conv_b, wx_bd, wa_bd, bx, ba, lam, B, S, deps=()):
    T = B * S
    tc = min(SCAN_TILE, S)
    nt = S // tc
    h8 = tc // 8

    def body(dya_ref, xa_ref, xhalo_ref, ga_ref, h_ref, hhalo_ref, cw_ref, cb_ref, wx_ref, wa_ref, bx_ref, ba_ref,
             lam_ref, dxa_ref, dga_ref, dcw_ref, dcb_ref, dwx_ref, dwa_ref, dbx_ref, dba_ref, dlam_ref,
             ext_ref, ext2_ref, carry_ref, dhalo_ref, la_ref, lh_ref, c_ref, dh_ref):
        b = pl.program_id(1)
        t = pl.program_id(2)
        tt = nt - 1 - t

        @pl.when(t == 0)
        def _():
            carry_ref[...] = jnp.zeros_like(carry_ref)
            dhalo_ref[...] = jnp.zeros_like(dhalo_ref)

        @pl.when((t == 0) & (b == 0))
        def _():
            for r in (dcw_ref, dcb_ref, dwx_ref, dwa_ref, dbx_ref, dba_ref, dlam_ref):
                r[...] = jnp.zeros_like(r)

        xa = xa_ref[...]
        ext_ref[0:8, :] = jnp.where(tt == 0, 0.0, xhalo_ref[...])
        ext_ref[8:8 + tc, :] = xa
        xc = _conv_from_ext(ext_ref, xa, cw_ref, cb_ref, tc)
        xcb, i_t, r_t, sp, a, mult = _lru_gates(xc, wx_ref, wa_ref, bx_ref, ba_ref, lam_ref)

        h = h_ref[...]
        ga = ga_ref[...]
        dya_t = dya_ref[...]
        sg = _sigmoid(ga)
        dga_ref[...] = (dya_t * h * (sg * (1.0 + ga * (1.0 - sg)))).astype(dga_ref.dtype)
        dlru = dya_t * (ga * sg)

        row = lax.broadcasted_iota(jnp.int32, a.shape, 0)
        coef = jnp.where(row == tc - 1, 1.0, pltpu.roll(a, tc - 1, 0))
        _scan_tile(coef, dlru, carry_ref[0:1, :], la_ref, lh_ref, c_ref, dh_ref, True)
        dh = dh_ref[...]
        ext2_ref[0:tc, :] = a * dh
        carry_ref[...] = ext2_ref[0:8, :]

        ext2_ref[0:8, :] = jnp.where(tt == 0, 0.0, hhalo_ref[...])
        ext2_ref[8:8 + tc, :] = h
        hprev = ext2_ref[7:7 + tc, :]

        da = dh * hprev
        ix = i_t * xc
        dmult = dh * ix
        di = dh * mult * xc
        dxc = dh * mult * i_t
        dlog_a = da * a - dmult * (a * a) / mult
        dr = dlog_a * ((-LRU_C) * sp)
        dlam_ref[...] += jnp.sum(dlog_a * r_t, axis=0, keepdims=True) * (LRU_C * _sigmoid(-lam_ref[...]))
        dza = dr * r_t * (1.0 - r_t)
        dzx = di * i_t * (1.0 - i_t)
        dzab = _c(dza)
        dzxb = _c(dzx)
        dxc = dxc + _dot_nt(dzxb, wx_ref[0]) + _dot_nt(dzab, wa_ref[0])
        dwx_ref[0] += _dot_tn(xcb, dzxb)
        dwa_ref[0] += _dot_tn(xcb, dzab)
        dbx_ref[...] += jnp.sum(dzx, axis=0, keepdims=True)
        dba_ref[...] += jnp.sum(dza, axis=0, keepdims=True)

        dcb_ref[...] += jnp.sum(dxc, axis=0, keepdims=True)
        dcw_ref[3:4, :] += jnp.sum(dxc * xa, axis=0, keepdims=True)
        dcw_ref[2:3, :] += jnp.sum(dxc * ext_ref[7:7 + tc, :], axis=0, keepdims=True)
        dcw_ref[1:2, :] += jnp.sum(dxc * ext_ref[6:6 + tc, :], axis=0, keepdims=True)
        dcw_ref[0:1, :] += jnp.sum(dxc * ext_ref[5:5 + tc, :], axis=0, keepdims=True)
        ext2_ref[0:tc, :] = dxc
        ext2_ref[tc:tc + 8, :] = dhalo_ref[...]
        dxa = (cw_ref[3:4, :] * dxc + cw_ref[2:3, :] * ext2_ref[1:1 + tc, :]
               + cw_ref[1:2, :] * ext2_ref[2:2 + tc, :] + cw_ref[0:1, :] * ext2_ref[3:3 + tc, :])
        dxa_ref[...] = dxa.astype(dxa_ref.dtype)
        dhalo_ref[...] = ext2_ref[0:8, :]

    row_of = lambda b, t: b * nt + (nt - 1 - t)
    tile = lambda off: pl.BlockSpec((tc, CW), lambda c, b, t: (row_of(b, t), off + c))
    halo = pl.BlockSpec((8, CW), lambda c, b, t: (jnp.maximum(row_of(b, t) * h8 - 1, 0), c))
    vec = pl.BlockSpec((1, CW), lambda c, b, t: (0, c))
    mat = pl.BlockSpec((1, CW, CW), lambda c, b, t: (c, 0, 0))
    cwspec = pl.BlockSpec((CONV, CW), lambda c, b, t: (0, c))
    return pl.pallas_call(
        _after(body, 13, deps),
        name="lru_bwd",
        grid=(N_CT, B, nt),
        in_specs=[tile(0), tile(0), halo, tile(N_CT), tile(0), halo, cwspec, vec, mat, mat, vec, vec, vec]
        + [ANY_SPEC] * len(deps),
        out_specs=[tile(0), tile(0), cwspec, vec, mat, mat, vec, vec, vec],
        out_shape=[
            jax.ShapeDtypeStruct((T, D_MODEL), _MXU),
            jax.ShapeDtypeStruct((T, D_MODEL), _MXU),
            jax.ShapeDtypeStruct((CONV, D_MODEL), F32),
            jax.ShapeDtypeStruct((1, D_MODEL), F32),
            jax.ShapeDtypeStruct((N_CT, CW, CW), F32),
            jax.ShapeDtypeStruct((N_CT, CW, CW), F32),
            jax.ShapeDtypeStruct((1, D_MODEL), F32),
            jax.ShapeDtypeStruct((1, D_MODEL), F32),
            jax.ShapeDtypeStruct((1, D_MODEL), F32),
        ],
        scratch_shapes=[pltpu.VMEM((tc + 8, CW), F32), pltpu.VMEM((tc + 8, CW), F32),
                        pltpu.VMEM((8, CW), F32), pltpu.VMEM((8, CW), F32)] + _scan_scratch(tc),
        compiler_params=_params(("parallel", "arbitrary", "arbitrary")),
    )(dya, proj, proj, proj, hlru, hlru, conv_w, conv_b, wx_bd, wa_bd, bx, ba, lam, *deps)


def _retention_tables(S):
    half = DK // 2
    freqs = ROPE_THETA ** (-jnp.arange(half, dtype=F32) / half)
    ang = jnp.arange(S, dtype=F32)[:, None] * freqs[None, :]
    log_g = jnp.log1p(-(2.0 ** (-5.0 - jnp.arange(HEADS, dtype=F32))))
    idx = jnp.arange(CHUNK, dtype=F32)
    diff = idx[:, None] - idx[None, :]
    inner = jnp.where(diff >= 0, jnp.exp(jnp.maximum(diff, 0.0)[None] * log_g[:, None, None]), 0.0)
    cross = jnp.exp((idx[None, :] + 1.0) * log_g[:, None])[:, :, None]
    state = jnp.exp((CHUNK - 1.0 - idx[None, :]) * log_g[:, None])[:, :, None]
    gam = jnp.broadcast_to(jnp.exp(CHUNK * log_g)[:, None, None], (HEADS, 1, DK))
    return jnp.cos(ang), jnp.sin(ang), inner, cross, state, gam


def _rot(x, cos, sin):
    half = DK // 2
    x1, x2 = x[:, :half], x[:, half:]
    return jnp.concatenate([x1 * cos - x2 * sin, x1 * sin + x2 * cos], axis=-1)


def _rot_t(y, cos, sin):
    half = DK // 2
    y1, y2 = y[:, :half], y[:, half:]
    return jnp.concatenate([y1 * cos + y2 * sin, y2 * cos - y1 * sin], axis=-1)


def _groupnorm(o):
    mu = jnp.mean(o, axis=-1, keepdims=True)
    oc = o - mu
    rs = lax.rsqrt(jnp.mean(oc * oc, axis=-1, keepdims=True) + EPS)
    return oc * rs, rs


def _ret_specs(B, chunk_of):
    rows = RET_CHUNKS * CHUNK
    qkv = lambda g: pl.BlockSpec((B, rows, D_MODEL), lambda c: (0, chunk_of(c), g))
    act = pl.BlockSpec((B, rows, D_MODEL), lambda c: (0, chunk_of(c), 0))
    rope = pl.BlockSpec((rows, DK // 2), lambda c: (chunk_of(c), 0))
    dmat = pl.BlockSpec((HEADS, CHUNK, CHUNK), lambda c: (0, 0, 0))
    dvec = pl.BlockSpec((HEADS, CHUNK, 1), lambda c: (0, 0, 0))
    hrow = pl.BlockSpec((HEADS, 1, DK), lambda c: (0, 0, 0))
    rst = pl.BlockSpec((RET_CHUNKS, B, HEADS, DK, DK), lambda c: (chunk_of(c), 0, 0, 0, 0))
    return qkv, act, rope, dmat, dvec, hrow, rst


def _ret_fwd(proj, tables, gain3, B, S):
    T = B * S
    nc = S // CHUNK
    cos, sin, dmat_t, cd_t, sd_t, gam_t = tables

    def body(q_ref, k_ref, v_ref, gb_ref, cos_ref, sin_ref, dm_ref, cd_ref, sd_ref, gam_ref, gain_ref,
             o_ref, yb_ref, rs_ref, state_ref):
        @pl.when(pl.program_id(0) == 0)
        def _():
            state_ref[...] = jnp.zeros_like(state_ref)

        for cc, b, h in [(cc, b, h) for cc in range(RET_CHUNKS) for b in range(B) for h in range(HEADS)]:
            rows = slice(cc * CHUNK, (cc + 1) * CHUNK)
            cos_t, sin_t = cos_ref[rows, :], sin_ref[rows, :]
            cols = slice(h * DK, (h + 1) * DK)
            qb = _c(_rot(q_ref[b, rows, cols], cos_t, sin_t))
            kb = _c(_rot(k_ref[b, rows, cols], cos_t, sin_t) * (DK ** -0.5))
            v = v_ref[b, rows, cols]
            state = state_ref[b, h]
            sb = _c(state)
            rs_ref[cc, b, h] = sb
            scores = _dot_nt(qb, kb) * dm_ref[h]
            o = _dot(_c(scores), _c(v)) + _dot(qb, sb) * cd_ref[h]
            state_ref[b, h] = gam_ref[h] * state + _dot_tn(kb, _c(v * sd_ref[h]))
            o_ref[b, rows, cols] = o
            n, _ = _groupnorm(o)
            gb = gb_ref[b, rows, cols]
            yb_ref[b, rows, cols] = (gb * _sigmoid(gb) * (n * gain_ref[h])).astype(yb_ref.dtype)

    qkv, act, rope, dmat, dvec, hrow, rst = _ret_specs(B, lambda c: c)
    proj3 = proj.reshape(B, S, proj.shape[1])
    o_pre, yb, states = pl.pallas_call(
        body,
        name="ret_fwd",
        grid=(nc // RET_CHUNKS,),
        in_specs=[qkv(2), qkv(3), qkv(4), qkv(5), rope, rope, dmat, dvec, dvec, hrow, hrow],
        out_specs=[act, act, rst],
        out_shape=[
            jax.ShapeDtypeStruct((B, S, D_MODEL), F32),
            jax.ShapeDtypeStruct((B, S, D_MODEL), _MXU),
            jax.ShapeDtypeStruct((nc, B, HEADS, DK, DK), _MXU),
        ],
        scratch_shapes=[pltpu.VMEM((B, HEADS, DK, DK), F32)],
        compiler_params=_params(("arbitrary",)),
    )(proj3, proj3, proj3, proj3, cos, sin, dmat_t, cd_t, sd_t, gam_t, gain3)
    return o_pre.reshape(T, D_MODEL), yb.reshape(T, D_MODEL), states


def _ret_bwd(dyb, o_pre, proj, states, tables, gain3, B, S, deps=()):
    T = B * S
    nc = S // CHUNK
    cos, sin, dmat_t, cd_t, sd_t, gam_t = tables

    def body(dyb_ref, o_ref, q_ref, k_ref, v_ref, gb_ref, rs_ref, cos_ref, sin_ref, dm_ref, cd_ref, sd_ref, gam_ref,
             gain_ref, dr_ref, dgain_ref, dstate_ref):
        @pl.when(pl.program_id(0) == 0)
        def _():
            dstate_ref[...] = jnp.zeros_like(dstate_ref)
            dgain_ref[...] = jnp.zeros_like(dgain_ref)

        for cc, b, h in [(cc, b, h) for cc in reversed(range(RET_CHUNKS)) for b in range(B) for h in range(HEADS)]:
            rows = slice(cc * CHUNK, (cc + 1) * CHUNK)
            cos_t, sin_t = cos_ref[rows, :], sin_ref[rows, :]
            cols = slice(h * DK, (h + 1) * DK)
            gain = gain_ref[h]
            n, rs = _groupnorm(o_ref[b, rows, cols])
            gb = gb_ref[b, rows, cols]
            sg = _sigmoid(gb)
            dy = dyb_ref[b, rows, cols]
            part = lambda g: slice(g * D_MODEL + h * DK, g * D_MODEL + (h + 1) * DK)
            dr_ref[b, rows, part(3)] = (dy * (n * gain) * (sg * (1.0 + gb * (1.0 - sg)))).astype(dr_ref.dtype)
            dgn = dy * (gb * sg)
            dgain_ref[h] += jnp.sum(dgn * n, axis=0, keepdims=True)
            dn = dgn * gain
            do = rs * (dn - jnp.mean(dn, axis=-1, keepdims=True) - n * jnp.mean(dn * n, axis=-1, keepdims=True))

            qb = _c(_rot(q_ref[b, rows, cols], cos_t, sin_t))
            kb = _c(_rot(k_ref[b, rows, cols], cos_t, sin_t) * (DK ** -0.5))
            v = v_ref[b, rows, cols]
            vb = _c(v)
            vsb = _c(v * sd_ref[h])
            dob = _c(do)
            docb = _c(do * cd_ref[h])
            dmat = dm_ref[h]
            dstate = dstate_ref[b, h]
            dsb = _c(dstate)
            pb = _c(_dot_nt(qb, kb) * dmat)
            dsc = _c(_dot_nt(dob, vb) * dmat)
            dq = _dot(dsc, kb) + _dot_nt(docb, rs_ref[cc, b, h])
            dk = _dot_tn(dsc, qb) + _dot_nt(vsb, dsb)
            dv = _dot_tn(pb, dob) + _dot(kb, dsb) * sd_ref[h]
            dstate_ref[b, h] = gam_ref[h] * dstate + _dot_tn(qb, docb)
            dr_ref[b, rows, part(0)] = _rot_t(dq, cos_t, sin_t).astype(dr_ref.dtype)
            dr_ref[b, rows, part(1)] = (_rot_t(dk, cos_t, sin_t) * (DK ** -0.5)).astype(dr_ref.dtype)
            dr_ref[b, rows, part(2)] = dv.astype(dr_ref.dtype)

    n_steps = nc // RET_CHUNKS
    qkv, act, rope, dmat, dvec, hrow, rst = _ret_specs(B, lambda c: n_steps - 1 - c)
    wide = pl.BlockSpec((B, RET_CHUNKS * CHUNK, 4 * D_MODEL), lambda c: (0, n_steps - 1 - c, 0))
    proj3 = proj.reshape(B, S, proj.shape[1])
    dr, dgain = pl.pallas_call(
        _after(body, 14, deps),
        name="ret_bwd",
        grid=(n_steps,),
        in_specs=[act, act, qkv(2), qkv(3), qkv(4), qkv(5), rst, rope, rope, dmat, dvec, dvec, hrow, hrow]
        + [ANY_SPEC] * len(deps),
        out_specs=[wide, hrow],
        out_shape=[jax.ShapeDtypeStruct((B, S, 4 * D_MODEL), _MXU), jax.ShapeDtypeStruct((HEADS, 1, DK), F32)],
        scratch_shapes=[pltpu.VMEM((B, HEADS, DK, DK), F32)],
        compiler_params=_params(("arbitrary",)),
    )(dyb.reshape(B, S, D_MODEL), o_pre.reshape(B, S, D_MODEL), proj3, proj3, proj3, proj3, states, cos, sin, dmat_t,
      cd_t, sd_t, gam_t, gain3, *deps)
    return dr.reshape(T, 4 * D_MODEL), dgain


def _mid(ya, yb, proj, x2d, tgt2d, wpa, wpb, wout, g_fin):
    T = x2d.shape[0]
    tm = min(MID_TILE, T)
    n_steps = T // tm
    rows = D_MODEL // (2 * N_CHIPS)

    def body(ya_ref, yb_ref, ma_ref, mb_ref, x_ref, t_ref, gf_ref, wpa_hbm, wpb_hbm, wout_hbm,
             loss_ref, dx2_ref, dya_ref, dyb_ref, dm_ref, dgf_ref, gw_hbm, w_ref, acc_ref, sem):
        i = pl.program_id(0)

        @pl.when(i == 0)
        def _():
            loads = [pltpu.make_async_copy(src, w_ref.at[k], sem.at[k]) for k, src in enumerate((wpa_hbm, wpb_hbm, wout_hbm))]
            for cp in loads:
                cp.start()
            for cp in loads:
                cp.wait()
            acc_ref[...] = jnp.zeros_like(acc_ref)
            loss_ref[...] = jnp.zeros_like(loss_ref)
            dgf_ref[...] = jnp.zeros_like(dgf_ref)

        ya_t, yb_t = ya_ref[...], yb_ref[...]
        out_a = _dot(ya_t, w_ref[0])
        out_b = _dot(yb_t, w_ref[1])
        sa = _sigmoid(ma_ref[...])
        sb = _sigmoid(mb_ref[...])
        mgb = _c(sa * out_a + sb * out_b)
        x2 = x_ref[...] + _dot(mgb, w_ref[2])
        r2 = lax.rsqrt(jnp.mean(x2 * x2, axis=-1, keepdims=True) + EPS)
        nx = x2 * r2
        gf = gf_ref[...]
        err = nx * gf - t_ref[...]
        loss_ref[...] += 0.5 * jnp.sum(jnp.mean(err * err, axis=-1, keepdims=True), axis=0, keepdims=True)
        dy = err * (1.0 / D_MODEL)
        dgf_ref[...] += jnp.sum(dy * nx, axis=0, keepdims=True)
        dyg = dy * gf
        dx2 = r2 * (dyg - nx * jnp.mean(dyg * nx, axis=-1, keepdims=True))
        dx2_ref[...] = dx2
        dx2b = _c(dx2)
        dmg = _dot_nt(dx2b, w_ref[2])
        acc_ref[2] += _dot_tn(mgb, dx2b)
        dm_ref[:, :D_MODEL] = (dmg * out_a * sa * (1.0 - sa)).astype(dm_ref.dtype)
        dm_ref[:, D_MODEL:] = (dmg * out_b * sb * (1.0 - sb)).astype(dm_ref.dtype)
        dab = _c(dmg * sa)
        dbb = _c(dmg * sb)
        dya_ref[...] = _dot_nt(dab, w_ref[0])
        dyb_ref[...] = _dot_nt(dbb, w_ref[1])
        acc_ref[0] += _dot_tn(ya_t, dab)
        acc_ref[1] += _dot_tn(yb_t, dbb)

        @pl.when(i == n_steps - 1)
        def _():
            copies = [pltpu.make_async_copy(acc_ref.at[k, pl.ds((2 * p + hf) * rows, rows), :], gw_hbm.at[p, hf, k],
                                            sem.at[(k * N_CHIPS + p) * 2 + hf])
                      for k in range(3) for p in range(N_CHIPS) for hf in range(2)]
            for cp in copies:
                cp.start()
            for cp in copies:
                cp.wait()

    tile = lambda j: pl.BlockSpec((tm, D_MODEL), lambda i: (i, j))
    one = pl.BlockSpec((1, D_MODEL), lambda i: (0, 0))
    anyspec = pl.BlockSpec(memory_space=pl.ANY)
    return pl.pallas_call(
        body,
        name="mid",
        grid=(n_steps,),
        in_specs=[tile(0), tile(0), tile(6), tile(7), tile(0), tile(0), one, anyspec, anyspec, anyspec],
        out_specs=[pl.BlockSpec((1, 1), lambda i: (0, 0)), tile(0), tile(0), tile(0),
                   pl.BlockSpec((tm, 2 * D_MODEL), lambda i: (i, 0)), one, anyspec],
        out_shape=[
            jax.ShapeDtypeStruct((1, 1), F32),
            jax.ShapeDtypeStruct((T, D_MODEL), F32),
            jax.ShapeDtypeStruct((T, D_MODEL), F32),
            jax.ShapeDtypeStruct((T, D_MODEL), F32),
            jax.ShapeDtypeStruct((T, 2 * D_MODEL), _MXU),
            jax.ShapeDtypeStruct((1, D_MODEL), F32),
            jax.ShapeDtypeStruct((N_CHIPS, 2, 3, rows, D_MODEL), F32),
        ],
        scratch_shapes=[pltpu.VMEM((3, D_MODEL, D_MODEL), _MXU), pltpu.VMEM((3, D_MODEL, D_MODEL), F32),
                        pltpu.SemaphoreType.DMA((3 * N_CHIPS * 2,))],
        compiler_params=_params(("arbitrary",)),
    )(ya, yb, proj, proj, x2d, tgt2d, g_fin, wpa, wpb, wout)


def _inproj_bwd_dx(dparts, w_all, x2d, dx2, g_in, first, count, prev, name, deps=()):
    T = x2d.shape[0]
    tm = min(DX_TILE, T)
    n_d = len(dparts)
    groups = [(a, k) for a, d in enumerate(dparts) for k in range(d.shape[1] // D_MODEL)]
    dg_start = jnp.zeros((1, D_MODEL), F32) if prev is None else prev[1]
    carried = () if prev is None else (prev[0],)

    def body(*refs):
        d_refs = refs[:n_d]
        x_ref, dx2_ref, g_ref, dg0_ref, w_hbm = refs[n_d:n_d + 5]
        dx_ref, dg_ref, w_ref, sem = refs[-4:]

        def load(j):
            part = (j // 2, slice(None), pl.ds((j % 2) * D_MODEL, D_MODEL))
            return pltpu.make_async_copy(w_hbm.at[part], w_ref.at[part], sem.at[j])

        def tile(before_group):
            dh = jnp.zeros((tm, D_MODEL), F32)
            for j, (a, k) in enumerate(groups):
                before_group(j)
                dh = dh + _dot_nt(d_refs[a][:, k * D_MODEL:(k + 1) * D_MODEL],
                                  w_ref[j // 2, :, (j % 2) * D_MODEL:(j % 2 + 1) * D_MODEL])
            x = x_ref[...]
            r = lax.rsqrt(jnp.mean(x * x, axis=-1, keepdims=True) + EPS)
            nx = x * r
            dg_ref[...] += jnp.sum(dh * nx, axis=0, keepdims=True)
            dhg = dh * g_ref[...]
            dx_ref[...] = dx2_ref[...] + r * (dhg - nx * jnp.mean(dhg * nx, axis=-1, keepdims=True))

        first = pl.program_id(0) == 0

        @pl.when(first)
        def _():
            for j in range(len(groups)):
                load(j).start()
            dg_ref[...] = dg0_ref[...]
            tile(lambda j: load(j).wait())

        @pl.when(jnp.logical_not(first))
        def _():
            tile(lambda j: None)

    tile = pl.BlockSpec((tm, D_MODEL), lambda i: (first + i, 0))
    one = pl.BlockSpec((1, D_MODEL), lambda i: (0, 0))
    return pl.pallas_call(
        body,
        name=name,
        grid=(count,),
        in_specs=[pl.BlockSpec((tm, d.shape[1]), lambda i: (first + i, 0)) for d in dparts]
        + [tile, tile, one, one, ANY_SPEC] + [ANY_SPEC] * (len(carried) + len(deps)),
        out_specs=[tile, one],
        out_shape=[jax.ShapeDtypeStruct((T, D_MODEL), F32), jax.ShapeDtypeStruct((1, D_MODEL), F32)],
        input_output_aliases={n_d + 5: 0} if carried else {},
        scratch_shapes=[pltpu.VMEM(w_all.shape, w_all.dtype), pltpu.SemaphoreType.DMA((len(groups),))],
        compiler_params=_params(("arbitrary",)),
    )(*dparts, x2d, dx2, g_in, dg_start, w_all, *carried, *deps)


def _inproj_bwd_dw(ht, dparts, name, deps=()):
    T = ht.shape[1]
    tn = DW_COLS
    half = D_MODEL // 2
    per_chip = 2 * D_MODEL // tn
    n_d = len(dparts)
    tiles = [(a, t) for a, d in enumerate(dparts) for t in range(d.shape[1] // tn)]
    offs = [sum(d.shape[1] // tn for d in dparts[:a]) for a in range(n_d)]

    def body(*refs):
        ht_hbm = refs[0]
        d_refs = refs[1:1 + n_d]
        out_ref, ht_ref, sem = refs[-3:]
        t = pl.program_id(0)

        def load(k):
            cols = pl.ds(k * (T // DW_LOADS), T // DW_LOADS)
            return pltpu.make_async_copy(ht_hbm.at[:, cols], ht_ref.at[:, cols], sem.at[k])

        def store(g):
            out_ref[0, 0] = g[:half]
            out_ref[0, 1] = g[half:]

        @pl.when(t == 0)
        def _():
            for k in range(DW_LOADS):
                load(k).start()
            g = jnp.zeros((D_MODEL, tn), F32)
            for k in range(DW_LOADS):
                load(k).wait()
                tokens = slice(k * (T // DW_LOADS), (k + 1) * (T // DW_LOADS))
                g = g + _dot(ht_ref[:, tokens], d_refs[0][tokens, :])
            store(g)

        for a in range(n_d):
            lo, hi = max(offs[a], 1), offs[a] + dparts[a].shape[1] // tn

            @pl.when((t >= lo) & (t < hi))
            def _(a=a):
                store(_dot(ht_ref[...], d_refs[a][...]))

    def dspec(a):
        n_a = dparts[a].shape[1] // tn
        return pl.BlockSpec((T, tn), lambda t: (0, jnp.clip(t - offs[a], 0, n_a - 1)))

    return pl.pallas_call(
        body,
        name=name,
        grid=(len(tiles),),
        in_specs=[ANY_SPEC] + [dspec(a) for a in range(n_d)] + [ANY_SPEC] * len(deps),
        out_specs=pl.BlockSpec((1, 2, half, tn), lambda t: (t // per_chip, 0, 0, t % per_chip)),
        out_shape=jax.ShapeDtypeStruct((len(tiles) // per_chip, 2, half, 2 * D_MODEL), F32),
        scratch_shapes=[pltpu.VMEM(ht.shape, ht.dtype), pltpu.SemaphoreType.DMA((DW_LOADS,))],
        compiler_params=_params(("arbitrary",)),
    )(ht, *dparts, *deps)


def _coords():
    return lax.axis_index("x"), lax.axis_index("y"), lax.axis_index("c")


def _other_chips(x, y):
    return [(1 - x, y), (x, 1 - y), (1 - x, 1 - y)]


def _chunks(rows, n):
    size = rows // n
    return [pl.ds(q * size, size) for q in range(n)]


HBM_SPEC = pl.BlockSpec(memory_space=pltpu.HBM)
SEM_SPEC = pl.BlockSpec(memory_space=pltpu.SEMAPHORE)
DATAFLOW = pltpu.SideEffectType.DATAFLOW_SIDE_EFFECTING


def _copies_start(bufs, plan, n_copies, name, deps=()):
    n = len(bufs)
    n_deps = len(deps)

    def body(*refs):
        ins = refs[:n]
        send_sems, recv_sems = refs[n + n_deps], refs[n + n_deps + 1]
        token = refs[-1]
        for k, send, _ in plan(ins):
            if send is not None:
                src, dst, dev, pred = send
                cp = pltpu.make_async_remote_copy(src_ref=src, dst_ref=dst, send_sem=send_sems.at[k],
                                                  recv_sem=recv_sems.at[k], device_id=dev, device_id_type=MESH)
                if pred is None:
                    cp.start()
                else:
                    pl.when(pred)(cp.start)
        token[...] = jnp.zeros_like(token)

    hbm = [pltpu.with_memory_space_constraint(b, pltpu.HBM) for b in bufs]
    outs = pl.pallas_call(
        body,
        name=name,
        in_specs=[HBM_SPEC] * n + [ANY_SPEC] * n_deps,
        out_specs=(SEM_SPEC, SEM_SPEC, *([HBM_SPEC] * n), pl.BlockSpec(memory_space=pltpu.VMEM)),
        out_shape=(pltpu.SemaphoreType.DMA((n_copies,)), pltpu.SemaphoreType.DMA((n_copies,)),
                   *[pltpu.HBM(b.shape, b.dtype) for b in bufs], jax.ShapeDtypeStruct((8, 128), F32)),
        input_output_aliases={a: 2 + a for a in range(n)},
        compiler_params=pltpu.CompilerParams(has_side_effects=DATAFLOW),
    )(*hbm, *deps)
    return outs[0], outs[1], list(outs[2:2 + n]), outs[-1]


def _copies_wait(send_sems, recv_sems, bufs, after, plan, name, only=None):
    n = len(bufs)

    def body(*refs):
        ins = refs[:n]
        s_sems, r_sems = refs[n], refs[n + 1]
        for k, send, recv in plan(ins):
            if only is not None and k not in only:
                continue
            if send is not None:
                src, dst, dev, pred = send
                cp = pltpu.make_async_remote_copy(src_ref=src, dst_ref=dst, send_sem=s_sems.at[k],
                                                  recv_sem=r_sems.at[k], device_id=dev, device_id_type=MESH)
                if pred is None:
                    cp.wait_send()
                else:
                    pl.when(pred)(cp.wait_send)
            if recv is not None:
                dst, pred = recv
                cp = pltpu.make_async_remote_copy(src_ref=dst, dst_ref=dst, send_sem=s_sems.at[k],
                                                  recv_sem=r_sems.at[k], device_id=_coords(), device_id_type=MESH)
                if pred is None:
                    cp.wait_recv()
                else:
                    pl.when(pred)(cp.wait_recv)

    outs = pl.pallas_call(
        body,
        name=name,
        in_specs=[HBM_SPEC] * n + [SEM_SPEC, SEM_SPEC, pl.BlockSpec(memory_space=pl.ANY)],
        out_specs=[HBM_SPEC] * n,
        out_shape=[pltpu.HBM(b.shape, b.dtype) for b in bufs],
        input_output_aliases={a: a for a in range(n)},
        compiler_params=pltpu.CompilerParams(has_side_effects=DATAFLOW),
    )(*bufs, send_sems, recv_sems, after)
    return list(outs)


def _gather_plan(n_bufs):
    def plan(refs):
        x, y, c = _coords()
        me = 2 * x + y
        out = []
        for k, (px, py) in enumerate(_other_chips(x, y)):
            for a in range(n_bufs):
                out.append((k * n_bufs + a, (refs[a].at[me], refs[a].at[me], (px, py, c), None),
                            (refs[a].at[2 * px + py], None)))
        return out
    return plan


def _cast_into_slot(ws, name, deps=()):
    n = len(ws)
    nt = 2

    def body(s_ref, *refs):
        outs = refs[len(refs) - n:]
        for a in range(n):
            outs[a][0] = refs[a][...].astype(outs[a].dtype)

    xi, yi, _ = _coords()
    return pl.pallas_call(
        body,
        name=name,
        grid_spec=pltpu.PrefetchScalarGridSpec(
            num_scalar_prefetch=1,
            grid=(2, nt),
            in_specs=[pl.BlockSpec((1, w.shape[1] // nt, w.shape[2]), lambda hf, i, s: (hf, i, 0)) for w in ws]
            + [ANY_SPEC] * len(deps),
            out_specs=[pl.BlockSpec((1, 1, w.shape[1] // nt, w.shape[2]), lambda hf, i, s: (s[0], hf, i, 0)) for w in ws],
        ),
        out_shape=[jax.ShapeDtypeStruct((N_CHIPS,) + w.shape, _MXU) for w in ws],
        compiler_params=_params(("parallel", "parallel")),
    )((2 * xi + yi).reshape(1).astype(jnp.int32), *ws, *deps)


def _chip_gather_plan(stage, n_bufs):
    def plan(refs):
        x, y, c = _coords()
        me = 2 * x + y
        near = [(1 - x, y), (x, 1 - y)]
        slots = [2 * (1 - x) + y, 2 * x + (1 - y), 2 * (1 - x) + (1 - y)]
        sibling = (x, y, 1 - c)
        pass_to = (jnp.where(c == 0, x, 1 - x), jnp.where(c == 0, 1 - y, y), c)
        pass_slot = jnp.where(c == 0, slots[0], slots[1])
        out = []

        def move(src_slot, to, land_slot, land_core, pieces):
            for a, buf in enumerate(refs):
                for rows in _chunks(buf.shape[2], pieces[a]):
                    out.append((len(out), (buf.at[src_slot, c, rows], buf.at[src_slot, c, rows], to, None),
                                (buf.at[land_slot, land_core, rows], None)))

        if stage == "near":
            for k, chip in enumerate(near):
                move(me, (*chip, c), slots[k], c, NEAR_PIECES[:n_bufs])
        elif stage == "pass":
            move(pass_slot, pass_to, slots[2], c, PASS_PIECES[:n_bufs])
            for k in range(2):
                move(slots[k], sibling, slots[k], 1 - c, [1] * n_bufs)
        else:
            move(slots[2], sibling, slots[2], 1 - c, [1] * n_bufs)
        return out
    return plan


NEAR_PIECES = (4,)
PASS_PIECES = (2,)


def _chip_gather_copies(stage, n_bufs):
    if stage == "near":
        return 2 * sum(NEAR_PIECES[:n_bufs]), None
    if stage == "pass":
        n_pass = sum(PASS_PIECES[:n_bufs])
        return n_pass + 2 * n_bufs, set(range(n_pass))
    return n_bufs, None


def _swap_plan(n_slabs):
    def plan(refs):
        x, y, c = _coords()
        out, k = [], 0
        for i, n in enumerate(n_slabs):
            g, land = refs[2 * i], refs[2 * i + 1]
            for p in range(n):
                out.append((k, (g.at[p, 1 - c], land.at[p], (x, y, 1 - c), None), (land.at[p], None)))
                k += 1
        return out
    return plan


def _is_one_of(chip, dests):
    hit = chip == dests[0]
    for d in dests[1:]:
        hit = hit | (chip == d)
    return hit


def _slab_of(chip, dests):
    return sum(j * (chip == d).astype(jnp.int32) for j, d in enumerate(dests))


def _scatter_plan(dest_sets):
    def plan(refs):
        x, y, c = _coords()
        me = 2 * x + y
        out = []
        for k, (px, py) in enumerate(_other_chips(x, y)):
            peer = 2 * px + py
            for i, dests in enumerate(dest_sets):
                cs, land = refs[2 * i], refs[2 * i + 1]
                everyone = len(dests) == N_CHIPS
                send = (cs.at[_slab_of(peer, dests)], land.at[k], (px, py, c),
                        None if everyone else _is_one_of(peer, dests))
                recv = (land.at[k], None if everyone else _is_one_of(me, dests))
                out.append((k * len(dest_sets) + i, send, recv))
        return out
    return plan


def _join_plan(rows, n_pieces):
    def plan(refs):
        x, y, c = _coords()
        (buf,) = refs
        return [(i, (buf.at[c, piece], buf.at[c, piece], (x, y, 1 - c), None), (buf.at[1 - c, piece], None))
                for i, piece in enumerate(_chunks(rows, n_pieces))]
    return plan


def _join_plans(parts):
    def plan(refs):
        out, b0, k0 = [], 0, 0
        for part_plan, n_bufs, n_copies in parts:
            out += [(k0 + k, send, recv) for k, send, recv in part_plan(refs[b0:b0 + n_bufs])]
            b0 += n_bufs
            k0 += n_copies
        return out
    return plan


def _allgather_plan():
    def plan(refs):
        x, y, c = _coords()
        (land,) = refs
        me = 4 * x + 2 * y + c
        out = []
        for r in range(1, 8):
            px = 1 - x if r & 4 else x
            py = 1 - y if r & 2 else y
            pc = 1 - c if r & 1 else c
            out.append((r - 1, (land.at[me], land.at[me], (px, py, pc), None), (land.at[4 * px + 2 * py + pc], None)))
        return out
    return plan


def _sum_gathered(land, name):
    def body(land_ref, o_ref):
        acc = land_ref[0]
        for d in range(1, 8):
            acc = acc + land_ref[d]
        o_ref[...] = acc

    return pl.pallas_call(
        body,
        name=name,
        out_shape=jax.ShapeDtypeStruct(land.shape[1:], F32),
        compiler_params=_params(),
    )(land)


def _row_tile(rows, cap):
    t = cap
    while rows % t:
        t //= 2
    return t


def _add_my_half(g, r, name):
    n_slabs, _, R, C = g.shape
    tr = R if n_slabs > 1 else _row_tile(R, SUM_ROWS)

    def body(c_ref, g_ref, r_ref, o_ref):
        o_ref[...] = (g_ref[0] + r_ref[...]).astype(o_ref.dtype)

    return pl.pallas_call(
        body,
        name=name,
        grid_spec=pltpu.PrefetchScalarGridSpec(
            num_scalar_prefetch=1,
            grid=(n_slabs, R // tr),
            in_specs=[pl.BlockSpec((1, 1, tr, C), lambda p, i, c_ref: (p, c_ref[0], i, 0)),
                      pl.BlockSpec((1, tr, C), lambda p, i, c_ref: (p, i, 0))],
            out_specs=pl.BlockSpec((1, tr, C), lambda p, i, c_ref: (p, i, 0)),
        ),
        out_shape=jax.ShapeDtypeStruct(r.shape, jnp.bfloat16),
        compiler_params=_params(("parallel", "parallel")),
    )(lax.axis_index("c").reshape(1).astype(jnp.int32), g, r)


def _sum_slabs(own, got, name, deps=()):
    _, R, C = own.shape
    tr = _row_tile(R, SUM_ROWS)

    def body(s_ref, own_ref, got_ref, *rest):
        rest[-1][0] = ((own_ref[0].astype(F32) + got_ref[0].astype(F32)) + got_ref[1].astype(F32)) + got_ref[2].astype(F32)

    xi, yi, ci = _coords()
    return pl.pallas_call(
        body,
        name=name,
        grid_spec=pltpu.PrefetchScalarGridSpec(
            num_scalar_prefetch=1,
            grid=(R // tr,),
            in_specs=[pl.BlockSpec((1, tr, C), lambda i, s: (s[0], i, 0)),
                      pl.BlockSpec((3, tr, C), lambda i, s: (0, i, 0))] + [ANY_SPEC] * len(deps),
            out_specs=pl.BlockSpec((1, tr, C), lambda i, s: (s[1], i, 0)),
        ),
        out_shape=jax.ShapeDtypeStruct((2, R, C), F32),
        compiler_params=_params(("parallel",)),
    )(jnp.stack([2 * xi + yi, ci]).astype(jnp.int32), own, got, *deps)


def _sum_parts(owns, got, dest_sets, name):
    n = len(owns)
    _, R, C = owns[0].shape
    tr = _row_tile(R, SUM_ROWS)

    def body(s_ref, *refs):
        got_ref, o_ref = refs[n], refs[-1]
        total = jnp.zeros((tr, C), F32)
        for i in range(n):
            total = total + jnp.where(s_ref[2 + 2 * i] == 1, refs[i][0].astype(F32), 0.0)
        o_ref[0] = ((total + got_ref[0].astype(F32)) + got_ref[1].astype(F32)) + got_ref[2].astype(F32)

    xi, yi, ci = _coords()
    me = 2 * xi + yi
    scalars = [ci, ci]
    for dests in dest_sets:
        scalars += [_is_one_of(me, dests).astype(jnp.int32), _slab_of(me, dests)]
    own_spec = lambda i: pl.BlockSpec((1, tr, C), lambda r, s: (s[3 + 2 * i], r, 0))
    return pl.pallas_call(
        body,
        name=name,
        grid_spec=pltpu.PrefetchScalarGridSpec(
            num_scalar_prefetch=1,
            grid=(R // tr,),
            in_specs=[own_spec(i) for i in range(n)] + [pl.BlockSpec((3, tr, C), lambda r, s: (0, r, 0))],
            out_specs=pl.BlockSpec((1, tr, C), lambda r, s: (s[0], r, 0)),
        ),
        out_shape=jax.ShapeDtypeStruct((2, R, C), F32),
        compiler_params=_params(("parallel",)),
    )(jnp.stack(scalars).astype(jnp.int32), *owns, got)


def _adamw_math(w, g, m, v):
    m = ADAM_B1 * m + (1.0 - ADAM_B1) * g
    v = ADAM_B2 * v + (1.0 - ADAM_B2) * (g * g)
    m_hat = m / (1.0 - ADAM_B1 ** ADAM_STEP)
    v_hat = v / (1.0 - ADAM_B2 ** ADAM_STEP)
    delta = -ADAM_LR * (m_hat / (jnp.sqrt(v_hat) + ADAM_EPS) + ADAM_WD * w)
    return delta, m, v


def _adamw_halves(ws, g, ms, vs, half, prev, name, deps=()):
    n = len(ws)
    _, _, R, C = g.shape
    tr = _row_tile(R, ADAMW_ROWS)
    steps = R // tr
    carried = [] if prev is None else [a for four in prev for a in four]
    both = half is None
    which = (lambda i, s: i // steps) if both else (lambda i, s: s[0])
    half = 0 if both else half

    def body(s_ref, *refs):
        w_refs, g_refs, m_refs, v_refs = (refs[k * n:(k + 1) * n] for k in range(4))
        outs = refs[len(refs) - 4 * n:]
        for a in range(n):
            grad = g_refs[a][0, 0]
            d, mn, vn = _adamw_math(w_refs[a][...], grad, m_refs[a][...], v_refs[a][...])
            for o, val in zip(outs[4 * a:4 * a + 4], (grad, d, mn, vn)):
                o[...] = val

    rows = pl.BlockSpec((tr, C), lambda i, s: (which(i, s) * steps + i % steps, 0))
    grad_spec = lambda a: pl.BlockSpec((1, 1, tr, C), lambda i, s: (which(i, s), a, i % steps, 0))
    n_in = 4 * n
    outs = pl.pallas_call(
        body,
        name=name,
        grid_spec=pltpu.PrefetchScalarGridSpec(
            num_scalar_prefetch=1,
            grid=(2 * steps if both else steps,),
            in_specs=[rows] * n + [grad_spec(a) for a in range(n)] + [rows] * (2 * n)
            + [ANY_SPEC] * (len(carried) + len(deps)),
            out_specs=[rows] * (4 * n),
        ),
        out_shape=[jax.ShapeDtypeStruct((2 * R, C), F32)] * (4 * n),
        input_output_aliases={1 + n_in + k: k for k in range(len(carried))},
        compiler_params=_params(("parallel",)),
    )(jnp.reshape(half, (1,)).astype(jnp.int32), *ws, *([g] * n), *ms, *vs, *carried, *deps)
    return [outs[4 * a:4 * a + 4] for a in range(n)]


def _adamw_small(ws, gs, ms, vs, name):
    n = len(ws)

    def body(*refs):
        for a in range(n):
            d, mn, vn = _adamw_math(refs[a][...], refs[n + a][...], refs[2 * n + a][...], refs[3 * n + a][...])
            refs[4 * n + a][...] = d
            refs[5 * n + a][...] = mn
            refs[6 * n + a][...] = vn

    shapes = [jax.ShapeDtypeStruct(w.shape, F32) for w in ws]
    outs = pl.pallas_call(
        body,
        name=name,
        out_shape=shapes * 3,
        compiler_params=_params(),
    )(*ws, *gs, *ms, *vs)
    return outs[:n], outs[n:2 * n], outs[2 * n:]


def _to_blockdiag(w):
    per = CW // LRU_BW
    w4 = w.reshape(N_CT, per, LRU_BW, LRU_BW)
    eye = jnp.eye(per, dtype=w.dtype)
    return (w4[:, :, :, None, :] * eye[None, :, None, :, None]).reshape(N_CT, CW, CW)


def _from_blockdiag(g):
    per = CW // LRU_BW
    g5 = g.reshape(N_CT, per, LRU_BW, per, LRU_BW)
    return jnp.stack([g5[:, b, :, b, :] for b in range(per)], axis=1).reshape(LRU_BLOCKS, LRU_BW, LRU_BW)


def _local_grads(x2d, tgt2d, B, S, g_in, in_proj, conv_b, gate_x_w, gate_x_b, gate_a_w, gate_a_b, lam,
                 proj_weights, g_fin, reduce):
    wx_bd = _c(_to_blockdiag(gate_x_w))
    wa_bd = _c(_to_blockdiag(gate_a_w))
    tables = _retention_tables(S)

    proj, ht, w_all, conv_w, gain = in_proj(x2d, g_in, (*tables, wx_bd, wa_bd))
    gain3 = gain.reshape(HEADS, 1, DK)
    hlru, ya = _lru_fwd(proj, conv_w, conv_b, wx_bd, wa_bd, gate_x_b, gate_a_b, lam, B, S)
    o_pre, yb, states = _ret_fwd(proj, tables, gain3, B, S)
    wpa, wpb, wout = proj_weights(yb)
    loss, dx2, dya, dyb, dm, dgf, gw_proj = _mid(ya, yb, proj, x2d, tgt2d, wpa, wpb, wout, g_fin)
    g3 = _inproj_bwd_dw(ht, [dm], "inproj_bwd_dw_m")
    deps = reduce.m_ready(gw_proj, g3)
    dr, dgain = _ret_bwd(dyb, o_pre, proj, states, tables, gain3, B, S, deps)
    deps = reduce.ret_done(dr)
    g12 = _inproj_bwd_dw(ht, [dr], "inproj_bwd_dw_r", deps)
    deps = reduce.r_ready(g12)
    dxa, dga, dcw, dcb, dwx_bd, dwa_bd, dbx, dba, dlam = _lru_bwd(
        dya, proj, hlru, conv_w, conv_b, wx_bd, wa_bd, gate_x_b, gate_a_b, lam, B, S, deps)
    small = dict(conv_w=dcw, conv_b=dcb, gate_x_w=_from_blockdiag(dwx_bd), gate_x_b=dbx,
                 gate_a_w=_from_blockdiag(dwa_bd), gate_a_b=dba, lru_lambda=dlam, gn_gain=dgain.reshape(HEADS, DK),
                 norm_final=dgf)
    loss_rows = jnp.broadcast_to(loss, (SUBLANES, LANES))
    deps = reduce.lru_done(dxa, jnp.concatenate([_pack_small(small), loss_rows], axis=0))
    g0 = _inproj_bwd_dw(ht, [dxa, dga], "inproj_bwd_dw_a", deps)
    deps = reduce.a_ready(g0)
    n_tiles = x2d.shape[0] // min(DX_TILE, x2d.shape[0])
    grad_x, dgin = _inproj_bwd_dx([dxa, dga, dr, dm], w_all, x2d, dx2, g_in, 0, n_tiles, None, "inproj_bwd_dx", deps)
    return grad_x, dgin


ALL_CHIPS = (0, 1, 2, 3)


class _GradReduce:
    def __init__(self, proj_done):
        self.pending = {}
        self.proj_done = proj_done
        self.land_in = None

    def _start(self, key, parts, name):
        bufs, plans, shared = [], [], None
        for part_bufs, plan, n_copies, part_shared in parts:
            if part_shared is not None:
                shared = len(bufs) + part_shared
            plans.append((plan, len(part_bufs), n_copies))
            bufs += part_bufs
        plan = _join_plans(plans)
        send_sems, recv_sems, bufs, token = _copies_start(bufs, plan, sum(p[2] for p in plans), name + "_start")
        if shared is not None:
            self.land_in = bufs[shared]
        self.pending[key] = (send_sems, recv_sems, bufs, plan, name + "_wait", shared)
        return (token,)

    def _finish(self, key, after):
        send_sems, recv_sems, bufs, plan, name, shared = self.pending.pop(key)
        if shared is not None:
            bufs[shared] = self.land_in
        bufs = _copies_wait(send_sems, recv_sems, bufs, after, plan, name)
        if shared is not None:
            self.land_in = bufs[shared]
        return bufs

    @staticmethod
    def _swap(pieces):
        bufs = []
        for g in pieces:
            bufs += [g, lax.empty((g.shape[0],) + g.shape[2:], F32)]
        n_slabs = [g.shape[0] for g in pieces]
        return bufs, _swap_plan(n_slabs), sum(n_slabs), None

    def _scatter(self, sums, dest_sets):
        bufs = []
        for cs in sums:
            bufs += [cs, lax.empty((3,) + cs.shape[1:], cs.dtype)]
        if self.land_in is not None:
            bufs[-1] = self.land_in
        return bufs, _scatter_plan(dest_sets), 3 * len(sums), len(bufs) - 1

    @staticmethod
    def _gather8(block):
        x, y, c = _coords()
        land = lax.dynamic_update_slice(lax.empty((8,) + block.shape, F32), block[None], (4 * x + 2 * y + c, 0, 0))
        return [land], _allgather_plan(), 7, None

    def m_ready(self, gw_proj, g3):
        rows = gw_proj.shape[2] * gw_proj.shape[3]
        return self._start("m", [self._swap([gw_proj.reshape(N_CHIPS, 2, rows, D_MODEL), g3])], "swap_m")

    def ret_done(self, after):
        proj, land_p, g3, land_3 = self._finish("m", after)
        sums_m = [_add_my_half(proj, land_p, "chip_sum_proj"), _add_my_half(g3, land_3, "chip_sum_m")]
        return self._start("sm", [self._scatter(sums_m, [ALL_CHIPS, (3,)])], "scatter_m")

    def r_ready(self, g12):
        return self._start("r", [self._swap([g12])], "swap_r")

    def lru_done(self, after, packed):
        g12, land_12 = self._finish("r", after)
        sums_r = [_add_my_half(g12, land_12, "chip_sum_r")]
        return (self._start("sr", [self._scatter(sums_r, [(1, 2)])], "scatter_r")
                + self._start("small", [self._gather8(packed)], "gather_small"))

    def a_ready(self, g0):
        (token,) = self._start("a", [self._swap([g0])], "swap_a")
        csp, gotp, self.cs3, _ = self._finish("sm", token)
        half_proj = _sum_slabs(csp, gotp, "sum_w_proj")
        g0, land_0 = self._finish("a", half_proj)
        join = ([half_proj], _join_plan(half_proj.shape[1], PROJ_JOIN_PIECES), PROJ_JOIN_PIECES, None)
        return self._start("sa", [self._scatter([_add_my_half(g0, land_0, "chip_sum_a")], [(0,)]), join], "scatter_a")

    def finish(self, dgin, w_in_done):
        (token,) = self._start("n", [self._gather8(dgin)], "gather_norm_in")
        (small,) = self._finish("small", token)
        cs12, _ = self._finish("sr", token)
        cs0, _, g_proj = self._finish("sa", token)
        self.proj_done(g_proj)
        half_in =_sum_parts([self.cs3, cs12, cs0], self.land_in, [(3,), (1, 2), (0,)], "sum_w_in")
        deps = self._start("j", [([half_in], _join_plan(half_in.shape[1], JOIN_PIECES), JOIN_PIECES, None)], "join_w_in")
        first = w_in_done(self.pending["j"][2][0], True, None, deps)
        (g_in,) = self._finish("j", first[1])
        done = w_in_done(g_in, False, first, ())
        (norm_in,) = self._finish("n", done[1])
        return _sum_gathered(small, "sum_small_grads"), _sum_gathered(norm_in, "sum_norm_in_grad")


_SMALL = ("gate_x_w", "gate_a_w", "conv_w", "conv_b", "gate_x_b", "gate_a_b", "lru_lambda", "gn_gain", "norm_final")
_SMALL_SHAPES = dict(gate_x_w=(LRU_BLOCKS, LRU_BW, LRU_BW), gate_a_w=(LRU_BLOCKS, LRU_BW, LRU_BW),
                     norm_in=(1, D_MODEL), conv_w=(CONV, D_MODEL), conv_b=(1, D_MODEL), gate_x_b=(1, D_MODEL),
                     gate_a_b=(1, D_MODEL), lru_lambda=(1, D_MODEL), gn_gain=(HEADS, DK), norm_final=(1, D_MODEL))


def _pack_small(small):
    return jnp.concatenate([small[k].reshape(-1, 128) for k in _SMALL], axis=0)


def _unpack_small(packed):
    out, r = {}, 0
    for k in _SMALL:
        shape = _SMALL_SHAPES[k]
        rows = 1
        for s in shape:
            rows *= s
        rows //= 128
        out[k] = packed[r:r + rows].reshape(shape)
        r += rows
    return out


def kernel(x, norm_in, w_in, conv_w, conv_b, gate_x_w, gate_x_b, gate_a_w, gate_a_b, lru_lambda, gn_gain, w_proj_a, w_proj_b, w_out, norm_final, loss_target, m_norm_in, m_w_in, m_conv_w, m_conv_b, m_gate_x_w, m_gate_x_b, m_gate_a_w, m_gate_a_b, m_lru_lambda, m_gn_gain, m_w_proj_a, m_w_proj_b, m_w_out, m_norm_final, v_norm_in, v_w_in, v_conv_w, v_conv_b, v_gate_x_w, v_gate_x_b, v_gate_a_w, v_gate_a_b, v_lru_lambda, v_gn_gain, v_w_proj_a, v_w_proj_b, v_w_out, v_norm_final):
    B, S, _ = x.shape
    T = B * S
    xi, yi, ci = _coords()
    chip = 2 * xi + yi

    cshard = D_MODEL // N_CHIPS
    mine = _cast_into_slot([w_in[0].reshape(2, D_MODEL // 2, 2 * D_MODEL)], "cast_w_in")
    plan = _gather_plan(3)
    pending_proj = []
    gshard = DK // N_CHIPS
    tiny = jnp.concatenate([conv_w[0], jnp.zeros((4, cshard), F32), jnp.pad(gn_gain[0], ((0, 4), (0, cshard - gshard)))],
                           axis=0).reshape(1, 2, SUBLANES, cshard)
    tiny_buf = lax.dynamic_update_slice(lax.empty((N_CHIPS, 2, SUBLANES, cshard), F32), tiny, (chip, 0, 0, 0))
    near_plan, chip_pass_plan, far_plan = (_chip_gather_plan(stage, 1) for stage in ("near", "pass", "far"))
    (n_near, _), (n_chip_pass, passed_on), (n_far, _) = (_chip_gather_copies(stage, 1) for stage in ("near", "pass", "far"))
    halves = set(range(n_chip_pass)) - passed_on
    n_pass = n_chip_pass + len(_other_chips(0, 0))
    tiny_copies = set(range(n_chip_pass, n_pass))
    pass_plan = _join_plans([(chip_pass_plan, 1, n_chip_pass), (_gather_plan(1), 1, n_pass - n_chip_pass)])
    near_s, near_r, bufs, near_token = _copies_start([mine[0]], near_plan, n_near, "gather_near_start")

    def in_proj(x2d, g_in, meanwhile):
        mine_proj = _cast_into_slot([w[0].reshape(2, cshard // 2, D_MODEL) for w in (w_proj_a, w_proj_b, w_out)],
                                    "cast_w_proj", (near_token,))
        as_w = lambda b: b[0].reshape(N_CHIPS, D_MODEL, 2 * D_MODEL)
        slot_x, slot_y, slot_d = 2 * (1 - xi) + yi, 2 * xi + (1 - yi), 2 * (1 - xi) + (1 - yi)
        ids = lambda *chips: jnp.stack(chips).astype(jnp.int32)
        proj, hb, ht = _inproj_first(x2d, g_in, as_w(bufs), ids(chip), "inproj_own", (near_token, *meanwhile))
        got = _copies_wait(near_s, near_r, bufs, proj, near_plan, "gather_near_wait")
        pass_s, pass_r, got, pass_token = _copies_start(got + [tiny_buf], pass_plan, n_pass, "gather_pass_start")
        got = _copies_wait(pass_s, pass_r, got, pass_token, pass_plan, "gather_pass_wait_halves", only=halves)
        proj = _inproj_more(hb, as_w(got), ids(slot_x, slot_y), proj, "inproj_near")
        got, tiny_all = _copies_wait(pass_s, pass_r, got, proj, pass_plan, "gather_pass_wait_far",
                                     only=passed_on | tiny_copies)
        pending_proj.append(_copies_start(mine_proj, plan, 9, "gather_proj_start", (got,)))
        far_s, far_r, got, far_token = _copies_start([got], far_plan, n_far, "gather_far_start")
        got = _copies_wait(far_s, far_r, got, far_token, far_plan, "gather_far_wait")
        proj = _inproj_more(hb, as_w(got), ids(slot_d), proj, "inproj_far")
        tiny_all = tiny_all.reshape(N_CHIPS, 2 * SUBLANES, cshard)
        conv_w_full = jnp.transpose(tiny_all[:, 0:CONV, :], (1, 0, 2)).reshape(CONV, D_MODEL)
        gain_full = jnp.transpose(tiny_all[:, 8:8 + HEADS, :gshard], (1, 0, 2)).reshape(HEADS, DK)
        return proj, ht, as_w(got), conv_w_full, gain_full

    def proj_weights(after):
        s_sems, r_sems, pbufs, _ = pending_proj[0]
        got = _copies_wait(s_sems, r_sems, pbufs, after, plan, "gather_proj_wait")
        return [b.reshape(D_MODEL, D_MODEL) for b in got]

    weights = dict(norm_in=norm_in, w_in=w_in, conv_w=conv_w, conv_b=conv_b, gate_x_w=gate_x_w, gate_x_b=gate_x_b,
                   gate_a_w=gate_a_w, gate_a_b=gate_a_b, lru_lambda=lru_lambda, gn_gain=gn_gain, w_proj_a=w_proj_a,
                   w_proj_b=w_proj_b, w_out=w_out, norm_final=norm_final)
    ms = dict(norm_in=m_norm_in, w_in=m_w_in, conv_w=m_conv_w, conv_b=m_conv_b, gate_x_w=m_gate_x_w,
              gate_x_b=m_gate_x_b, gate_a_w=m_gate_a_w, gate_a_b=m_gate_a_b, lru_lambda=m_lru_lambda, gn_gain=m_gn_gain,
              w_proj_a=m_w_proj_a, w_proj_b=m_w_proj_b, w_out=m_w_out, norm_final=m_norm_final)
    vs = dict(norm_in=v_norm_in, w_in=v_w_in, conv_w=v_conv_w, conv_b=v_conv_b, gate_x_w=v_gate_x_w,
              gate_x_b=v_gate_x_b, gate_a_w=v_gate_a_w, gate_a_b=v_gate_a_b, lru_lambda=v_lru_lambda, gn_gain=v_gn_gain,
              w_proj_a=v_w_proj_a, w_proj_b=v_w_proj_b, w_out=v_w_out, norm_final=v_norm_final)
    names = list(weights)
    grads, delta, new_m, new_v = {}, {}, {}, {}

    def update_big(keys, g, half, prev, name, deps=()):
        two = lambda a: a.reshape(a.shape[1], a.shape[2])
        res = _adamw_halves([two(weights[k]) for k in keys], g, [two(ms[k]) for k in keys], [two(vs[k]) for k in keys],
                            half, prev, name, deps)
        for k, (gk, d, mn, vn) in zip(keys, res):
            shp = weights[k].shape
            grads[k], delta[k], new_m[k], new_v[k] = gk.reshape(shp), d.reshape(shp), mn.reshape(shp), vn.reshape(shp)
        return res

    def proj_done(g_proj):
        g4 = g_proj.reshape(2, 3, D_MODEL // (2 * N_CHIPS), D_MODEL)
        return update_big(("w_proj_a", "w_proj_b", "w_out"), g4, None, None, "adamw_proj")[-1][1]

    def w_in_done(g_in, own, prev, deps):
        g4 = g_in.reshape(2, 1, D_MODEL // 2, 2 * D_MODEL)
        return update_big(("w_in",), g4, ci if own else 1 - ci, None if prev is None else [prev],
                          "adamw_w_in_own" if own else "adamw_w_in_other", deps)[0]

    reduce = _GradReduce(proj_done)
    grad_x, dgin = _local_grads(
        x.reshape(T, D_MODEL), loss_target.reshape(T, D_MODEL), B, S, norm_in, in_proj, conv_b,
        gate_x_w[0], gate_x_b, gate_a_w[0], gate_a_b, lru_lambda, proj_weights,
        norm_final.reshape(1, D_MODEL), reduce)

    small_sum, g_norm_in = reduce.finish(dgin.reshape(SUBLANES, LANES), w_in_done)
    loss = small_sum[small_sum.shape[0] - SUBLANES, 0]

    gsm = _unpack_small(small_sum)
    gsm["norm_in"] = g_norm_in
    gsm["conv_w"] = lax.dynamic_slice_in_dim(gsm["conv_w"], chip * cshard, cshard, axis=1)
    gsm["gn_gain"] = lax.dynamic_slice_in_dim(gsm["gn_gain"], chip * gshard, gshard, axis=1)
    smalls = [k for k in names if k not in delta]

    def view(a):
        return a.reshape(1, -1) if a.ndim == 1 else (a.reshape(a.shape[1:]) if a.ndim > 2 else a)

    ds, mns, vns = _adamw_small([view(weights[k]) for k in smalls], [gsm[k].reshape(view(weights[k]).shape) for k in smalls],
                                [view(ms[k]) for k in smalls], [view(vs[k]) for k in smalls], "adamw_small")
    for k, d, mn, vn in zip(smalls, ds, mns, vns):
        shp = weights[k].shape
        grads[k], delta[k], new_m[k], new_v[k] = gsm[k].reshape(shp), d.reshape(shp), mn.reshape(shp), vn.reshape(shp)

    return (loss, grad_x.reshape(B, S, D_MODEL), *[grads[k] for k in names], *[delta[k] for k in names],
            *[new_m[k] for k in names], *[new_v[k] for k in names])
```

```python
import jax
import jax.numpy as jnp
from jax import lax
from jax.experimental import pallas as pl
from jax.experimental.pallas import tpu as pltpu

F32 = jnp.float32
_MXU = jnp.bfloat16

D_MODEL = 1024
N_GROUPS = 8
HEADS = 4
DK = 256
CHUNK = 128
CONV = 4
LRU_BLOCKS = 16
LRU_BW = 64
LRU_C = 8.0
ROPE_THETA = 10000.0
EPS = 1e-6
CW = 256
N_CT = D_MODEL // CW
N_CHIPS = 4
MESH = pl.DeviceIdType.MESH

ADAM_LR = 0.001
ADAM_B1 = 0.9
ADAM_B2 = 0.999
ADAM_EPS = 1e-08
ADAM_WD = 0.01
ADAM_STEP = 10

VMEM_LIMIT = 56 * 1024 * 1024

FIRST_PROJ_TILE = 1024
MORE_PROJ_TILE = 2048
SCAN_TILE = 1024
MID_TILE = 256
DX_TILE = 512
DW_COLS = 512
DW_LOADS = 4
RET_CHUNKS = 2
SUM_ROWS = 256
ADAMW_ROWS = 256
JOIN_PIECES = 8
PROJ_JOIN_PIECES = 4


def _c(v):
    return v.astype(_MXU)


def _dot(a, b):
    return lax.dot_general(a, b, (((1,), (0,)), ((), ())), preferred_element_type=F32)


def _dot_nt(a, b):
    return lax.dot_general(a, b, (((1,), (1,)), ((), ())), preferred_element_type=F32)


def _dot_tn(a, b):
    return lax.dot_general(a, b, (((0,), (0,)), ((), ())), preferred_element_type=F32)


def _sigmoid(z):
    return 0.5 * jnp.tanh(0.5 * z) + 0.5


ANY_SPEC = pl.BlockSpec(memory_space=pl.ANY)


def _after(body, n_in, deps):
    n_deps = len(deps)

    def wrapped(*refs):
        return body(*refs[:n_in], *refs[n_in + n_deps:])

    return wrapped


def _params(sem=None):
    if sem is None:
        return pltpu.CompilerParams(vmem_limit_bytes=VMEM_LIMIT)
    return pltpu.CompilerParams(vmem_limit_bytes=VMEM_LIMIT, dimension_semantics=sem)


def _inproj_first(x2d, g_in, w_all, chips, name, deps=()):
    T = x2d.shape[0]
    tm = min(FIRST_PROJ_TILE, T)
    n_i = T // tm

    def body(s_ref, *refs):
        x_ref, g_ref, w_ref = refs[:3]
        proj_ref, hb_ref, ht_ref, h_all = refs[-4:]
        i = pl.program_id(1)
        rows = pl.ds(pl.multiple_of(i * tm, tm), tm)

        @pl.when(pl.program_id(0) == 0)
        def _():
            x = x_ref[...]
            r = lax.rsqrt(jnp.mean(x * x, axis=-1, keepdims=True) + EPS)
            h = x * r * g_ref[...]
            hb = h.astype(h_all.dtype)
            h_all[rows, :] = hb
            hb_ref[...] = hb
            ht_ref[...] = h.T.astype(ht_ref.dtype)

        proj_ref[...] = _dot(h_all[rows, :], w_ref[0])

    first = lambda j, i: jnp.where(j == 0, i, n_i - 1)
    return pl.pallas_call(
        body,
        name=name,
        grid_spec=pltpu.PrefetchScalarGridSpec(
            num_scalar_prefetch=1,
            grid=(2 * chips.shape[0], n_i),
            in_specs=[
                pl.BlockSpec((tm, D_MODEL), lambda j, i, s: (first(j, i), 0)),
                pl.BlockSpec((1, D_MODEL), lambda j, i, s: (0, 0)),
                pl.BlockSpec((1, D_MODEL, D_MODEL), lambda j, i, s: (s[j // 2], 0, j % 2)),
            ] + [ANY_SPEC] * len(deps),
            out_specs=[
                pl.BlockSpec((tm, D_MODEL), lambda j, i, s: (i, 2 * s[j // 2] + j % 2)),
                pl.BlockSpec((tm, D_MODEL), lambda j, i, s: (first(j, i), 0)),
                pl.BlockSpec((D_MODEL, tm), lambda j, i, s: (0, first(j, i))),
            ],
            scratch_shapes=[pltpu.VMEM((T, D_MODEL), _MXU)],
        ),
        out_shape=[
            jax.ShapeDtypeStruct((T, N_GROUPS * D_MODEL), F32),
            jax.ShapeDtypeStruct((T, D_MODEL), _MXU),
            jax.ShapeDtypeStruct((D_MODEL, T), _MXU),
        ],
        compiler_params=_params(("arbitrary", "arbitrary")),
    )(chips, x2d, g_in, w_all, *deps)


def _inproj_more(hb, w_all, chips, proj, name):
    T = hb.shape[0]
    tm = min(MORE_PROJ_TILE, T)

    def body(s_ref, hb_hbm, w_ref, prev_ref, proj_ref, h_all, sem):
        @pl.when((pl.program_id(0) == 0) & (pl.program_id(1) == 0))
        def _():
            cp = pltpu.make_async_copy(hb_hbm, h_all, sem)
            cp.start()
            cp.wait()

        rows = pl.ds(pl.multiple_of(pl.program_id(1) * tm, tm), tm)
        proj_ref[...] = _dot(h_all[rows, :], w_ref[0])

    return pl.pallas_call(
        body,
        name=name,
        grid_spec=pltpu.PrefetchScalarGridSpec(
            num_scalar_prefetch=1,
            grid=(2 * chips.shape[0], T // tm),
            in_specs=[
                ANY_SPEC,
                pl.BlockSpec((1, D_MODEL, D_MODEL), lambda j, i, s: (s[j // 2], 0, j % 2)),
                ANY_SPEC,
            ],
            out_specs=pl.BlockSpec((tm, D_MODEL), lambda j, i, s: (i, 2 * s[j // 2] + j % 2)),
            scratch_shapes=[pltpu.VMEM((T, D_MODEL), hb.dtype), pltpu.SemaphoreType.DMA],
        ),
        out_shape=jax.ShapeDtypeStruct(proj.shape, F32),
        input_output_aliases={3: 0},
        compiler_params=_params(("arbitrary", "arbitrary")),
    )(chips, hb, w_all, proj)


def _scan_fwd(a, u):
    n = a.shape[0]
    row = lax.broadcasted_iota(jnp.int32, a.shape, 0)
    s = 1
    while s < n:
        m = row >= s
        u = u + a * jnp.where(m, pltpu.roll(u, s, 0), 0.0)
        a = a * jnp.where(m, pltpu.roll(a, s, 0), 1.0)
        s *= 2
    return a, u


def _scan_bwd(b, g):
    n = b.shape[0]
    row = lax.broadcasted_iota(jnp.int32, b.shape, 0)
    s = 1
    while s < n:
        m = row < n - s
        g = g + b * jnp.where(m, pltpu.roll(g, n - s, 0), 0.0)
        b = b * jnp.where(m, pltpu.roll(b, n - s, 0), 1.0)
        s *= 2
    return b, g


LANES = 128
SUBLANES = 8


def _scan_scratch(tc):
    by_lanes = pltpu.VMEM((CW // LANES, tc, LANES), F32)
    return [by_lanes, by_lanes, pltpu.VMEM((tc // SUBLANES, CW), F32), pltpu.VMEM((tc, CW), F32)]


def _scan_tile(a, u, edge, la_ref, lh_ref, c_ref, dst_ref, reverse):
    n, w = a.shape
    groups = n // SUBLANES
    a3 = a.reshape(groups, SUBLANES, w)
    u3 = u.reshape(groups, SUBLANES, w)
    row = lax.broadcasted_iota(jnp.int32, a3.shape, 1)
    for s in (1, 2, 4):
        m = (row < SUBLANES - s) if reverse else (row >= s)
        shift = SUBLANES - s if reverse else s
        u3 = u3 + a3 * jnp.where(m, pltpu.roll(u3, shift, 1), 0.0)
        a3 = a3 * jnp.where(m, pltpu.roll(a3, shift, 1), 1.0)
    al = a3.reshape(n, w)
    hl = u3.reshape(n, w)
    blocks = w // LANES
    for q in range(blocks):
        la_ref[q] = al[:, q * LANES:(q + 1) * LANES]
        lh_ref[q] = hl[:, q * LANES:(q + 1) * LANES]
    ends = pl.ds(0 if reverse else SUBLANES - 1, groups, stride=SUBLANES)
    end_a = jnp.concatenate([la_ref.at[q][ends, :] for q in range(blocks)], axis=-1)
    end_h = jnp.concatenate([lh_ref.at[q][ends, :] for q in range(blocks)], axis=-1)
    prod, part = (_scan_bwd if reverse else _scan_fwd)(end_a, end_h)
    total = part + prod * edge
    g_row = lax.broadcasted_iota(jnp.int32, total.shape, 0)
    if reverse:
        c_ref[...] = jnp.where(g_row == groups - 1, edge, pltpu.roll(total, groups - 1, 0))
    else:
        c_ref[...] = jnp.where(g_row == 0, edge, pltpu.roll(total, 1, 0))
    for g in range(groups):
        rows = slice(g * SUBLANES, (g + 1) * SUBLANES)
        for q in range(blocks):
            cols = slice(q * LANES, (q + 1) * LANES)
            dst_ref[rows, cols] = lh_ref[q, rows, :] + la_ref[q, rows, :] * c_ref[g:g + 1, cols]


def _softplus_neg(lam):
    z = -lam
    return jnp.maximum(z, 0.0) + jnp.log1p(jnp.exp(-jnp.abs(z)))


def _lru_gates(xc, wx_ref, wa_ref, bx_ref, ba_ref, lam_ref):
    xcb = _c(xc)
    i_t = _sigmoid(_dot(xcb, wx_ref[0]) + bx_ref[...])
    r_t = _sigmoid(_dot(xcb, wa_ref[0]) + ba_ref[...])
    sp = _softplus_neg(lam_ref[...])
    log_a = (-LRU_C) * r_t * sp
    a = jnp.exp(log_a)
    mult = jnp.sqrt(1.0 - a * a)
    return xcb, i_t, r_t, sp, a, mult


def _conv_from_ext(ext_ref, xa, cw_ref, cb_ref, tc):
    return (cb_ref[...] + cw_ref[3:4, :] * xa + cw_ref[2:3, :] * ext_ref[7:7 + tc, :]
            + cw_ref[1:2, :] * ext_ref[6:6 + tc, :] + cw_ref[0:1, :] * ext_ref[5:5 + tc, :])


def _lru_fwd(proj, conv_w, conv_b, wx_bd, wa_bd, bx, ba, lam, B, S):
    T = B * S
    tc = min(SCAN_TILE, S)
    nt = S // tc
    h8 = tc // 8

    def body(xa_ref, halo_ref, ga_ref, cw_ref, cb_ref, wx_ref, wa_ref, bx_ref, ba_ref, lam_ref,
             h_ref, ya_ref, ext_ref, carry_ref, la_ref, lh_ref, c_ref):
        t = pl.program_id(2)

        @pl.when(t == 0)
        def _():
            carry_ref[...] = jnp.zeros_like(carry_ref)

        xa = xa_ref[...]
        ext_ref[0:8, :] = jnp.where(t == 0, 0.0, halo_ref[...])
        ext_ref[8:8 + tc, :] = xa
        xc = _conv_from_ext(ext_ref, xa, cw_ref, cb_ref, tc)
        _, i_t, _, _, a, mult = _lru_gates(xc, wx_ref, wa_ref, bx_ref, ba_ref, lam_ref)
        u = mult * (i_t * xc)
        _scan_tile(a, u, carry_ref[7:8, :], la_ref, lh_ref, c_ref, h_ref, False)
        h = h_ref[...]
        carry_ref[...] = h[tc - 8:tc, :]
        ga = ga_ref[...]
        ya_ref[...] = (ga * _sigmoid(ga) * h).astype(ya_ref.dtype)

    row = lambda b, t: b * nt + t
    vec = pl.BlockSpec((1, CW), lambda b, c, t: (0, c))
    mat = pl.BlockSpec((1, CW, CW), lambda b, c, t: (c, 0, 0))
    return pl.pallas_call(
        body,
        name="lru_fwd",
        grid=(B, N_CT, nt),
        in_specs=[
            pl.BlockSpec((tc, CW), lambda b, c, t: (row(b, t), c)),
            pl.BlockSpec((8, CW), lambda b, c, t: (jnp.maximum(row(b, t) * h8 - 1, 0), c)),
            pl.BlockSpec((tc, CW), lambda b, c, t: (row(b, t), N_CT + c)),
            pl.BlockSpec((CONV, CW), lambda b, c, t: (0, c)),
            vec, mat, mat, vec, vec, vec,
        ],
        out_specs=[
            pl.BlockSpec((tc, CW), lambda b, c, t: (row(b, t), c)),
            pl.BlockSpec((tc, CW), lambda b, c, t: (row(b, t), c)),
        ],
        out_shape=[
            jax.ShapeDtypeStruct((T, D_MODEL), F32),
            jax.ShapeDtypeStruct((T, D_MODEL), _MXU),
        ],
        scratch_shapes=[pltpu.VMEM((tc + 8, CW), F32), pltpu.VMEM((8, CW), F32)] + _scan_scratch(tc)[:3],
        compiler_params=_params(("parallel", "parallel", "arbitrary")),
    )(proj, proj, proj, conv_w, conv_b, wx_bd, wa_bd, bx, ba, lam)


def _lru_bwd(dya, proj, hlru, conv_w, conv_b, wx_bd, wa_bd, bx, ba, lam, B, S, deps=()):
    T = B * S
    tc = min(SCAN_TILE, S)
    nt = S // tc
    h8 = tc // 8

    def body(dya_ref, xa_ref, xhalo_ref, ga_ref, h_ref, hhalo_ref, cw_ref, cb_ref, wx_ref, wa_ref, bx_ref, ba_ref,
             lam_ref, dxa_ref, dga_ref, dcw_ref, dcb_ref, dwx_ref, dwa_ref, dbx_ref, dba_ref, dlam_ref,
             ext_ref, ext2_ref, carry_ref, dhalo_ref, la_ref, lh_ref, c_ref, dh_ref):
        b = pl.program_id(1)
        t = pl.program_id(2)
        tt = nt - 1 - t

        @pl.when(t == 0)
        def _():
            carry_ref[...] = jnp.zeros_like(carry_ref)
            dhalo_ref[...] = jnp.zeros_like(dhalo_ref)

        @pl.when((t == 0) & (b == 0))
        def _():
            for r in (dcw_ref, dcb_ref, dwx_ref, dwa_ref, dbx_ref, dba_ref, dlam_ref):
                r[...] = jnp.zeros_like(r)

        xa = xa_ref[...]
        ext_ref[0:8, :] = jnp.where(tt == 0, 0.0, xhalo_ref[...])
        ext_ref[8:8 + tc, :] = xa
        xc = _conv_from_ext(ext_ref, xa, cw_ref, cb_ref, tc)
        xcb, i_t, r_t, sp, a, mult = _lru_gates(xc, wx_ref, wa_ref, bx_ref, ba_ref, lam_ref)

        h = h_ref[...]
        ga = ga_ref[...]
        dya_t = dya_ref[...]
        sg = _sigmoid(ga)
        dga_ref[...] = (dya_t * h * (sg * (1.0 + ga * (1.0 - sg)))).astype(dga_ref.dtype)
        dlru = dya_t * (ga * sg)

        row = lax.broadcasted_iota(jnp.int32, a.shape, 0)
        coef = jnp.where(row == tc - 1, 1.0, pltpu.roll(a, tc - 1, 0))
        _scan_tile(coef, dlru, carry_ref[0:1, :], la_ref, lh_ref, c_ref, dh_ref, True)
        dh = dh_ref[...]
        ext2_ref[0:tc, :] = a * dh
        carry_ref[...] = ext2_ref[0:8, :]

        ext2_ref[0:8, :] = jnp.where(tt == 0, 0.0, hhalo_ref[...])
        ext2_ref[8:8 + tc, :] = h
        hprev = ext2_ref[7:7 + tc, :]

        da = dh * hprev
        ix = i_t * xc
        dmult = dh * ix
        di = dh * mult * xc
        dxc = dh * mult * i_t
        dlog_a = da * a - dmult * (a * a) / mult
        dr = dlog_a * ((-LRU_C) * sp)
        dlam_ref[...] += jnp.sum(dlog_a * r_t, axis=0, keepdims=True) * (LRU_C * _sigmoid(-lam_ref[...]))
        dza = dr * r_t * (1.0 - r_t)
        dzx = di * i_t * (1.0 - i_t)
        dzab = _c(dza)
        dzxb = _c(dzx)
        dxc = dxc + _dot_nt(dzxb, wx_ref[0]) + _dot_nt(dzab, wa_ref[0])
        dwx_ref[0] += _dot_tn(xcb, dzxb)
        dwa_ref[0] += _dot_tn(xcb, dzab)
        dbx_ref[...] += jnp.sum(dzx, axis=0, keepdims=True)
        dba_ref[...] += jnp.sum(dza, axis=0, keepdims=True)

        dcb_ref[...] += jnp.sum(dxc, axis=0, keepdims=True)
        dcw_ref[3:4, :] += jnp.sum(dxc * xa, axis=0, keepdims=True)
        dcw_ref[2:3, :] += jnp.sum(dxc * ext_ref[7:7 + tc, :], axis=0, keepdims=True)
        dcw_ref[1:2, :] += jnp.sum(dxc * ext_ref[6:6 + tc, :], axis=0, keepdims=True)
        dcw_ref[0:1, :] += jnp.sum(dxc * ext_ref[5:5 + tc, :], axis=0, keepdims=True)
        ext2_ref[0:tc, :] = dxc
        ext2_ref[tc:tc + 8, :] = dhalo_ref[...]
        dxa = (cw_ref[3:4, :] * dxc + cw_ref[2:3, :] * ext2_ref[1:1 + tc, :]
               + cw_ref[1:2, :] * ext2_ref[2:2 + tc, :] + cw_ref[0:1, :] * ext2_ref[3:3 + tc, :])
        dxa_ref[...] = dxa.astype(dxa_ref.dtype)
        dhalo_ref[...] = ext2_ref[0:8, :]

    row_of = lambda b, t: b * nt + (nt - 1 - t)
    tile = lambda off: pl.BlockSpec((tc, CW), lambda c, b, t: (row_of(b, t), off + c))
    halo = pl.BlockSpec((8, CW), lambda c, b, t: (jnp.maximum(row_of(b, t) * h8 - 1, 0), c))
    vec = pl.BlockSpec((1, CW), lambda c, b, t: (0, c))
    mat = pl.BlockSpec((1, CW, CW), lambda c, b, t: (c, 0, 0))
    cwspec = pl.BlockSpec((CONV, CW), lambda c, b, t: (0, c))
    return pl.pallas_call(
        _after(body, 13, deps),
        name="lru_bwd",
        grid=(N_CT, B, nt),
        in_specs=[tile(0), tile(0), halo, tile(N_CT), tile(0), halo, cwspec, vec, mat, mat, vec, vec, vec]
        + [ANY_SPEC] * len(deps),
        out_specs=[tile(0), tile(0), cwspec, vec, mat, mat, vec, vec, vec],
        out_shape=[
            jax.ShapeDtypeStruct((T, D_MODEL), _MXU),
            jax.ShapeDtypeStruct((T, D_MODEL), _MXU),
            jax.ShapeDtypeStruct((CONV, D_MODEL), F32),
            jax.ShapeDtypeStruct((1, D_MODEL), F32),
            jax.ShapeDtypeStruct((N_CT, CW, CW), F32),
            jax.ShapeDtypeStruct((N_CT, CW, CW), F32),
            jax.ShapeDtypeStruct((1, D_MODEL), F32),
            jax.ShapeDtypeStruct((1, D_MODEL), F32),
            jax.ShapeDtypeStruct((1, D_MODEL), F32),
        ],
        scratch_shapes=[pltpu.VMEM((tc + 8, CW), F32), pltpu.VMEM((tc + 8, CW), F32),
                        pltpu.VMEM((8, CW), F32), pltpu.VMEM((8, CW), F32)] + _scan_scratch(tc),
        compiler_params=_params(("parallel", "arbitrary", "arbitrary")),
    )(dya, proj, proj, proj, hlru, hlru, conv_w, conv_b, wx_bd, wa_bd, bx, ba, lam, *deps)


def _retention_tables(S):
    half = DK // 2
    freqs = ROPE_THETA ** (-jnp.arange(half, dtype=F32) / half)
    ang = jnp.arange(S, dtype=F32)[:, None] * freqs[None, :]
    log_g = jnp.log1p(-(2.0 ** (-5.0 - jnp.arange(HEADS, dtype=F32))))
    idx = jnp.arange(CHUNK, dtype=F32)
    diff = idx[:, None] - idx[None, :]
    inner = jnp.where(diff >= 0, jnp.exp(jnp.maximum(diff, 0.0)[None] * log_g[:, None, None]), 0.0)
    cross = jnp.exp((idx[None, :] + 1.0) * log_g[:, None])[:, :, None]
    state = jnp.exp((CHUNK - 1.0 - idx[None, :]) * log_g[:, None])[:, :, None]
    gam = jnp.broadcast_to(jnp.exp(CHUNK * log_g)[:, None, None], (HEADS, 1, DK))
    return jnp.cos(ang), jnp.sin(ang), inner, cross, state, gam


def _rot(x, cos, sin):
    half = DK // 2
    x1, x2 = x[:, :half], x[:, half:]
    return jnp.concatenate([x1 * cos - x2 * sin, x1 * sin + x2 * cos], axis=-1)


def _rot_t(y, cos, sin):
    half = DK // 2
    y1, y2 = y[:, :half], y[:, half:]
    return jnp.concatenate([y1 * cos + y2 * sin, y2 * cos - y1 * sin], axis=-1)


def _groupnorm(o):
    mu = jnp.mean(o, axis=-1, keepdims=True)
    oc = o - mu
    rs = lax.rsqrt(jnp.mean(oc * oc, axis=-1, keepdims=True) + EPS)
    return oc * rs, rs


def _ret_specs(B, chunk_of):
    rows = RET_CHUNKS * CHUNK
    qkv = lambda g: pl.BlockSpec((B, rows, D_MODEL), lambda c: (0, chunk_of(c), g))
    act = pl.BlockSpec((B, rows, D_MODEL), lambda c: (0, chunk_of(c), 0))
    rope = pl.BlockSpec((rows, DK // 2), lambda c: (chunk_of(c), 0))
    dmat = pl.BlockSpec((HEADS, CHUNK, CHUNK), lambda c: (0, 0, 0))
    dvec = pl.BlockSpec((HEADS, CHUNK, 1), lambda c: (0, 0, 0))
    hrow = pl.BlockSpec((HEADS, 1, DK), lambda c: (0, 0, 0))
    rst = pl.BlockSpec((RET_CHUNKS, B, HEADS, DK, DK), lambda c: (chunk_of(c), 0, 0, 0, 0))
    return qkv, act, rope, dmat, dvec, hrow, rst


def _ret_fwd(proj, tables, gain3, B, S):
    T = B * S
    nc = S // CHUNK
    cos, sin, dmat_t, cd_t, sd_t, gam_t = tables

    def body(q_ref, k_ref, v_ref, gb_ref, cos_ref, sin_ref, dm_ref, cd_ref, sd_ref, gam_ref, gain_ref,
             o_ref, yb_ref, rs_ref, state_ref):
        @pl.when(pl.program_id(0) == 0)
        def _():
            state_ref[...] = jnp.zeros_like(state_ref)

        for cc, b, h in [(cc, b, h) for cc in range(RET_CHUNKS) for b in range(B) for h in range(HEADS)]:
            rows = slice(cc * CHUNK, (cc + 1) * CHUNK)
            cos_t, sin_t = cos_ref[rows, :], sin_ref[rows, :]
            cols = slice(h * DK, (h + 1) * DK)
            qb = _c(_rot(q_ref[b, rows, cols], cos_t, sin_t))
            kb = _c(_rot(k_ref[b, rows, cols], cos_t, sin_t) * (DK ** -0.5))
            v = v_ref[b, rows, cols]
            state = state_ref[b, h]
            sb = _c(state)
            rs_ref[cc, b, h] = sb
            scores = _dot_nt(qb, kb) * dm_ref[h]
            o = _dot(_c(scores), _c(v)) + _dot(qb, sb) * cd_ref[h]
            state_ref[b, h] = gam_ref[h] * state + _dot_tn(kb, _c(v * sd_ref[h]))
            o_ref[b, rows, cols] = o
            n, _ = _groupnorm(o)
            gb = gb_ref[b, rows, cols]
            yb_ref[b, rows, cols] = (gb * _sigmoid(gb) * (n * gain_ref[h])).astype(yb_ref.dtype)

    qkv, act, rope, dmat, dvec, hrow, rst = _ret_specs(B, lambda c: c)
    proj3 = proj.reshape(B, S, proj.shape[1])
    o_pre, yb, states = pl.pallas_call(
        body,
        name="ret_fwd",
        grid=(nc // RET_CHUNKS,),
        in_specs=[qkv(2), qkv(3), qkv(4), qkv(5), rope, rope, dmat, dvec, dvec, hrow, hrow],
        out_specs=[act, act, rst],
        out_shape=[
            jax.ShapeDtypeStruct((B, S, D_MODEL), F32),
            jax.ShapeDtypeStruct((B, S, D_MODEL), _MXU),
            jax.ShapeDtypeStruct((nc, B, HEADS, DK, DK), _MXU),
        ],
        scratch_shapes=[pltpu.VMEM((B, HEADS, DK, DK), F32)],
        compiler_params=_params(("arbitrary",)),
    )(proj3, proj3, proj3, proj3, cos, sin, dmat_t, cd_t, sd_t, gam_t, gain3)
    return o_pre.reshape(T, D_MODEL), yb.reshape(T, D_MODEL), states


def _ret_bwd(dyb, o_pre, proj, states, tables, gain3, B, S, deps=()):
    T = B * S
    nc = S // CHUNK
    cos, sin, dmat_t, cd_t, sd_t, gam_t = tables

    def body(dyb_ref, o_ref, q_ref, k_ref, v_ref, gb_ref, rs_ref, cos_ref, sin_ref, dm_ref, cd_ref, sd_ref, gam_ref,
             gain_ref, dr_ref, dgain_ref, dstate_ref):
        @pl.when(pl.program_id(0) == 0)
        def _():
            dstate_ref[...] = jnp.zeros_like(dstate_ref)
            dgain_ref[...] = jnp.zeros_like(dgain_ref)

        for cc, b, h in [(cc, b, h) for cc in reversed(range(RET_CHUNKS)) for b in range(B) for h in range(HEADS)]:
            rows = slice(cc * CHUNK, (cc + 1) * CHUNK)
            cos_t, sin_t = cos_ref[rows, :], sin_ref[rows, :]
            cols = slice(h * DK, (h + 1) * DK)
            gain = gain_ref[h]
            n, rs = _groupnorm(o_ref[b, rows, cols])
            gb = gb_ref[b, rows, cols]
            sg = _sigmoid(gb)
            dy = dyb_ref[b, rows, cols]
            part = lambda g: slice(g * D_MODEL + h * DK, g * D_MODEL + (h + 1) * DK)
            dr_ref[b, rows, part(3)] = (dy * (n * gain) * (sg * (1.0 + gb * (1.0 - sg)))).astype(dr_ref.dtype)
            dgn = dy * (gb * sg)
            dgain_ref[h] += jnp.sum(dgn * n, axis=0, keepdims=True)
            dn = dgn * gain
            do = rs * (dn - jnp.mean(dn, axis=-1, keepdims=True) - n * jnp.mean(dn * n, axis=-1, keepdims=True))

            qb = _c(_rot(q_ref[b, rows, cols], cos_t, sin_t))
            kb = _c(_rot(k_ref[b, rows, cols], cos_t, sin_t) * (DK ** -0.5))
            v = v_ref[b, rows, cols]
            vb = _c(v)
            vsb = _c(v * sd_ref[h])
            dob = _c(do)
            docb = _c(do * cd_ref[h])
            dmat = dm_ref[h]
            dstate = dstate_ref[b, h]
            dsb = _c(dstate)
            pb = _c(_dot_nt(qb, kb) * dmat)
            dsc = _c(_dot_nt(dob, vb) * dmat)
            dq = _dot(dsc, kb) + _dot_nt(docb, rs_ref[cc, b, h])
            dk = _dot_tn(dsc, qb) + _dot_nt(vsb, dsb)
            dv = _dot_tn(pb, dob) + _dot(kb, dsb) * sd_ref[h]
            dstate_ref[b, h] = gam_ref[h] * dstate + _dot_tn(qb, docb)
            dr_ref[b, rows, part(0)] = _rot_t(dq, cos_t, sin_t).astype(dr_ref.dtype)
            dr_ref[b, rows, part(1)] = (_rot_t(dk, cos_t, sin_t) * (DK ** -0.5)).astype(dr_ref.dtype)
            dr_ref[b, rows, part(2)] = dv.astype(dr_ref.dtype)

    n_steps = nc // RET_CHUNKS
    qkv, act, rope, dmat, dvec, hrow, rst = _ret_specs(B, lambda c: n_steps - 1 - c)
    wide = pl.BlockSpec((B, RET_CHUNKS * CHUNK, 4 * D_MODEL), lambda c: (0, n_steps - 1 - c, 0))
    proj3 = proj.reshape(B, S, proj.shape[1])
    dr, dgain = pl.pallas_call(
        _after(body, 14, deps),
        name="ret_bwd",
        grid=(n_steps,),
        in_specs=[act, act, qkv(2), qkv(3), qkv(4), qkv(5), rst, rope, rope, dmat, dvec, dvec, hrow, hrow]
        + [ANY_SPEC] * len(deps),
        out_specs=[wide, hrow],
        out_shape=[jax.ShapeDtypeStruct((B, S, 4 * D_MODEL), _MXU), jax.ShapeDtypeStruct((HEADS, 1, DK), F32)],
        scratch_shapes=[pltpu.VMEM((B, HEADS, DK, DK), F32)],
        compiler_params=_params(("arbitrary",)),
    )(dyb.reshape(B, S, D_MODEL), o_pre.reshape(B, S, D_MODEL), proj3, proj3, proj3, proj3, states, cos, sin, dmat_t,
      cd_t, sd_t, gam_t, gain3, *deps)
    return dr.reshape(T, 4 * D_MODEL), dgain


def _mid(ya, yb, proj, x2d, tgt2d, wpa, wpb, wout, g_fin):
    T = x2d.shape[0]
    tm = min(MID_TILE, T)
    n_steps = T // tm
    rows = D_MODEL // (2 * N_CHIPS)

    def body(ya_ref, yb_ref, ma_ref, mb_ref, x_ref, t_ref, gf_ref, wpa_hbm, wpb_hbm, wout_hbm,
             loss_ref, dx2_ref, dya_ref, dyb_ref, dm_ref, dgf_ref, gw_hbm, w_ref, acc_ref, sem):
        i = pl.program_id(0)

        @pl.when(i == 0)
        def _():
            loads = [pltpu.make_async_copy(src, w_ref.at[k], sem.at[k]) for k, src in enumerate((wpa_hbm, wpb_hbm, wout_hbm))]
            for cp in loads:
                cp.start()
            for cp in loads:
                cp.wait()
            acc_ref[...] = jnp.zeros_like(acc_ref)
            loss_ref[...] = jnp.zeros_like(loss_ref)
            dgf_ref[...] = jnp.zeros_like(dgf_ref)

        ya_t, yb_t = ya_ref[...], yb_ref[...]
        out_a = _dot(ya_t, w_ref[0])
        out_b = _dot(yb_t, w_ref[1])
        sa = _sigmoid(ma_ref[...])
        sb = _sigmoid(mb_ref[...])
        mgb = _c(sa * out_a + sb * out_b)
        x2 = x_ref[...] + _dot(mgb, w_ref[2])
        r2 = lax.rsqrt(jnp.mean(x2 * x2, axis=-1, keepdims=True) + EPS)
        nx = x2 * r2
        gf = gf_ref[...]
        err = nx * gf - t_ref[...]
        loss_ref[...] += 0.5 * jnp.sum(jnp.mean(err * err, axis=-1, keepdims=True), axis=0, keepdims=True)
        dy = err * (1.0 / D_MODEL)
        dgf_ref[...] += jnp.sum(dy * nx, axis=0, keepdims=True)
        dyg = dy * gf
        dx2 = r2 * (dyg - nx * jnp.mean(dyg * nx, axis=-1, keepdims=True))
        dx2_ref[...] = dx2
        dx2b = _c(dx2)
        dmg = _dot_nt(dx2b, w_ref[2])
        acc_ref[2] += _dot_tn(mgb, dx2b)
        dm_ref[:, :D_MODEL] = (dmg * out_a * sa * (1.0 - sa)).astype(dm_ref.dtype)
        dm_ref[:, D_MODEL:] = (dmg * out_b * sb * (1.0 - sb)).astype(dm_ref.dtype)
        dab = _c(dmg * sa)
        dbb = _c(dmg * sb)
        dya_ref[...] = _dot_nt(dab, w_ref[0])
        dyb_ref[...] = _dot_nt(dbb, w_ref[1])
        acc_ref[0] += _dot_tn(ya_t, dab)
        acc_ref[1] += _dot_tn(yb_t, dbb)

        @pl.when(i == n_steps - 1)
        def _():
            copies = [pltpu.make_async_copy(acc_ref.at[k, pl.ds((2 * p + hf) * rows, rows), :], gw_hbm.at[p, hf, k],
                                            sem.at[(k * N_CHIPS + p) * 2 + hf])
                      for k in range(3) for p in range(N_CHIPS) for hf in range(2)]
            for cp in copies:
                cp.start()
            for cp in copies:
                cp.wait()

    tile = lambda j: pl.BlockSpec((tm, D_MODEL), lambda i: (i, j))
    one = pl.BlockSpec((1, D_MODEL), lambda i: (0, 0))
    anyspec = pl.BlockSpec(memory_space=pl.ANY)
    return pl.pallas_call(
        body,
        name="mid",
        grid=(n_steps,),
        in_specs=[tile(0), tile(0), tile(6), tile(7), tile(0), tile(0), one, anyspec, anyspec, anyspec],
        out_specs=[pl.BlockSpec((1, 1), lambda i: (0, 0)), tile(0), tile(0), tile(0),
                   pl.BlockSpec((tm, 2 * D_MODEL), lambda i: (i, 0)), one, anyspec],
        out_shape=[
            jax.ShapeDtypeStruct((1, 1), F32),
            jax.ShapeDtypeStruct((T, D_MODEL), F32),
            jax.ShapeDtypeStruct((T, D_MODEL), F32),
            jax.ShapeDtypeStruct((T, D_MODEL), F32),
            jax.ShapeDtypeStruct((T, 2 * D_MODEL), _MXU),
            jax.ShapeDtypeStruct((1, D_MODEL), F32),
            jax.ShapeDtypeStruct((N_CHIPS, 2, 3, rows, D_MODEL), F32),
        ],
        scratch_shapes=[pltpu.VMEM((3, D_MODEL, D_MODEL), _MXU), pltpu.VMEM((3, D_MODEL, D_MODEL), F32),
                        pltpu.SemaphoreType.DMA((3 * N_CHIPS * 2,))],
        compiler_params=_params(("arbitrary",)),
    )(ya, yb, proj, proj, x2d, tgt2d, g_fin, wpa, wpb, wout)


def _inproj_bwd_dx(dparts, w_all, x2d, dx2, g_in, first, count, prev, name, deps=()):
    T = x2d.shape[0]
    tm = min(DX_TILE, T)
    n_d = len(dparts)
    groups = [(a, k) for a, d in enumerate(dparts) for k in range(d.shape[1] // D_MODEL)]
    dg_start = jnp.zeros((1, D_MODEL), F32) if prev is None else prev[1]
    carried = () if prev is None else (prev[0],)

    def body(*refs):
        d_refs = refs[:n_d]
        x_ref, dx2_ref, g_ref, dg0_ref, w_hbm = refs[n_d:n_d + 5]
        dx_ref, dg_ref, w_ref, sem = refs[-4:]

        def load(j):
            part = (j // 2, slice(None), pl.ds((j % 2) * D_MODEL, D_MODEL))
            return pltpu.make_async_copy(w_hbm.at[part], w_ref.at[part], sem.at[j])

        def tile(before_group):
            dh = jnp.zeros((tm, D_MODEL), F32)
            for j, (a, k) in enumerate(groups):
                before_group(j)
                dh = dh + _dot_nt(d_refs[a][:, k * D_MODEL:(k + 1) * D_MODEL],
                                  w_ref[j // 2, :, (j % 2) * D_MODEL:(j % 2 + 1) * D_MODEL])
            x = x_ref[...]
            r = lax.rsqrt(jnp.mean(x * x, axis=-1, keepdims=True) + EPS)
            nx = x * r
            dg_ref[...] += jnp.sum(dh * nx, axis=0, keepdims=True)
            dhg = dh * g_ref[...]
            dx_ref[...] = dx2_ref[...] + r * (dhg - nx * jnp.mean(dhg * nx, axis=-1, keepdims=True))

        first = pl.program_id(0) == 0

        @pl.when(first)
        def _():
            for j in range(len(groups)):
                load(j).start()
            dg_ref[...] = dg0_ref[...]
            tile(lambda j: load(j).wait())

        @pl.when(jnp.logical_not(first))
        def _():
            tile(lambda j: None)

    tile = pl.BlockSpec((tm, D_MODEL), lambda i: (first + i, 0))
    one = pl.BlockSpec((1, D_MODEL), lambda i: (0, 0))
    return pl.pallas_call(
        body,
        name=name,
        grid=(count,),
        in_specs=[pl.BlockSpec((tm, d.shape[1]), lambda i: (first + i, 0)) for d in dparts]
        + [tile, tile, one, one, ANY_SPEC] + [ANY_SPEC] * (len(carried) + len(deps)),
        out_specs=[tile, one],
        out_shape=[jax.ShapeDtypeStruct((T, D_MODEL), F32), jax.ShapeDtypeStruct((1, D_MODEL), F32)],
        input_output_aliases={n_d + 5: 0} if carried else {},
        scratch_shapes=[pltpu.VMEM(w_all.shape, w_all.dtype), pltpu.SemaphoreType.DMA((len(groups),))],
        compiler_params=_params(("arbitrary",)),
    )(*dparts, x2d, dx2, g_in, dg_start, w_all, *carried, *deps)


def _inproj_bwd_dw(ht, dparts, name, deps=()):
    T = ht.shape[1]
    tn = DW_COLS
    half = D_MODEL // 2
    per_chip = 2 * D_MODEL // tn
    n_d = len(dparts)
    tiles = [(a, t) for a, d in enumerate(dparts) for t in range(d.shape[1] // tn)]
    offs = [sum(d.shape[1] // tn for d in dparts[:a]) for a in range(n_d)]

    def body(*refs):
        ht_hbm = refs[0]
        d_refs = refs[1:1 + n_d]
        out_ref, ht_ref, sem = refs[-3:]
        t = pl.program_id(0)

        def load(k):
            cols = pl.ds(k * (T // DW_LOADS), T // DW_LOADS)
            return pltpu.make_async_copy(ht_hbm.at[:, cols], ht_ref.at[:, cols], sem.at[k])

        def store(g):
            out_ref[0, 0] = g[:half]
            out_ref[0, 1] = g[half:]

        @pl.when(t == 0)
        def _():
            for k in range(DW_LOADS):
                load(k).start()
            g = jnp.zeros((D_MODEL, tn), F32)
            for k in range(DW_LOADS):
                load(k).wait()
                tokens = slice(k * (T // DW_LOADS), (k + 1) * (T // DW_LOADS))
                g = g + _dot(ht_ref[:, tokens], d_refs[0][tokens, :])
            store(g)

        for a in range(n_d):
            lo, hi = max(offs[a], 1), offs[a] + dparts[a].shape[1] // tn

            @pl.when((t >= lo) & (t < hi))
            def _(a=a):
                store(_dot(ht_ref[...], d_refs[a][...]))

    def dspec(a):
        n_a = dparts[a].shape[1] // tn
        return pl.BlockSpec((T, tn), lambda t: (0, jnp.clip(t - offs[a], 0, n_a - 1)))

    return pl.pallas_call(
        body,
        name=name,
        grid=(len(tiles),),
        in_specs=[ANY_SPEC] + [dspec(a) for a in range(n_d)] + [ANY_SPEC] * len(deps),
        out_specs=pl.BlockSpec((1, 2, half, tn), lambda t: (t // per_chip, 0, 0, t % per_chip)),
        out_shape=jax.ShapeDtypeStruct((len(tiles) // per_chip, 2, half, 2 * D_MODEL), F32),
        scratch_shapes=[pltpu.VMEM(ht.shape, ht.dtype), pltpu.SemaphoreType.DMA((DW_LOADS,))],
        compiler_params=_params(("arbitrary",)),
    )(ht, *dparts, *deps)


def _coords():
    return lax.axis_index("x"), lax.axis_index("y"), lax.axis_index("c")


def _other_chips(x, y):
    return [(1 - x, y), (x, 1 - y), (1 - x, 1 - y)]


def _chunks(rows, n):
    size = rows // n
    return [pl.ds(q * size, size) for q in range(n)]


HBM_SPEC = pl.BlockSpec(memory_space=pltpu.HBM)
SEM_SPEC = pl.BlockSpec(memory_space=pltpu.SEMAPHORE)
DATAFLOW = pltpu.SideEffectType.DATAFLOW_SIDE_EFFECTING


def _copies_start(bufs, plan, n_copies, name, deps=()):
    n = len(bufs)
    n_deps = len(deps)

    def body(*refs):
        ins = refs[:n]
        send_sems, recv_sems = refs[n + n_deps], refs[n + n_deps + 1]
        token = refs[-1]
        for k, send, _ in plan(ins):
            if send is not None:
                src, dst, dev, pred = send
                cp = pltpu.make_async_remote_copy(src_ref=src, dst_ref=dst, send_sem=send_sems.at[k],
                                                  recv_sem=recv_sems.at[k], device_id=dev, device_id_type=MESH)
                if pred is None:
                    cp.start()
                else:
                    pl.when(pred)(cp.start)
        token[...] = jnp.zeros_like(token)

    hbm = [pltpu.with_memory_space_constraint(b, pltpu.HBM) for b in bufs]
    outs = pl.pallas_call(
        body,
        name=name,
        in_specs=[HBM_SPEC] * n + [ANY_SPEC] * n_deps,
        out_specs=(SEM_SPEC, SEM_SPEC, *([HBM_SPEC] * n), pl.BlockSpec(memory_space=pltpu.VMEM)),
        out_shape=(pltpu.SemaphoreType.DMA((n_copies,)), pltpu.SemaphoreType.DMA((n_copies,)),
                   *[pltpu.HBM(b.shape, b.dtype) for b in bufs], jax.ShapeDtypeStruct((8, 128), F32)),
        input_output_aliases={a: 2 + a for a in range(n)},
        compiler_params=pltpu.CompilerParams(has_side_effects=DATAFLOW),
    )(*hbm, *deps)
    return outs[0], outs[1], list(outs[2:2 + n]), outs[-1]


def _copies_wait(send_sems, recv_sems, bufs, after, plan, name, only=None):
    n = len(bufs)

    def body(*refs):
        ins = refs[:n]
        s_sems, r_sems = refs[n], refs[n + 1]
        for k, send, recv in plan(ins):
            if only is not None and k not in only:
                continue
            if send is not None:
                src, dst, dev, pred = send
                cp = pltpu.make_async_remote_copy(src_ref=src, dst_ref=dst, send_sem=s_sems.at[k],
                                                  recv_sem=r_sems.at[k], device_id=dev, device_id_type=MESH)
                if pred is None:
                    cp.wait_send()
                else:
                    pl.when(pred)(cp.wait_send)
            if recv is not None:
                dst, pred = recv
                cp = pltpu.make_async_remote_copy(src_ref=dst, dst_ref=dst, send_sem=s_sems.at[k],
                                                  recv_sem=r_sems.at[k], device_id=_coords(), device_id_type=MESH)
                if pred is None:
                    cp.wait_recv()
                else:
                    pl.when(pred)(cp.wait_recv)

    outs = pl.pallas_call(
        body,
        name=name,
        in_specs=[HBM_SPEC] * n + [SEM_SPEC, SEM_SPEC, pl.BlockSpec(memory_space=pl.ANY)],
        out_specs=[HBM_SPEC] * n,
        out_shape=[pltpu.HBM(b.shape, b.dtype) for b in bufs],
        input_output_aliases={a: a for a in range(n)},
        compiler_params=pltpu.CompilerParams(has_side_effects=DATAFLOW),
    )(*bufs, send_sems, recv_sems, after)
    return list(outs)


def _gather_plan(n_bufs):
    def plan(refs):
        x, y, c = _coords()
        me = 2 * x + y
        out = []
        for k, (px, py) in enumerate(_other_chips(x, y)):
            for a in range(n_bufs):
                out.append((k * n_bufs + a, (refs[a].at[me], refs[a].at[me], (px, py, c), None),
                            (refs[a].at[2 * px + py], None)))
        return out
    return plan


def _cast_into_slot(ws, name, deps=()):
    n = len(ws)
    nt = 2

    def body(s_ref, *refs):
        outs = refs[len(refs) - n:]
        for a in range(n):
            outs[a][0] = refs[a][...].astype(outs[a].dtype)

    xi, yi, _ = _coords()
    return pl.pallas_call(
        body,
        name=name,
        grid_spec=pltpu.PrefetchScalarGridSpec(
            num_scalar_prefetch=1,
            grid=(2, nt),
            in_specs=[pl.BlockSpec((1, w.shape[1] // nt, w.shape[2]), lambda hf, i, s: (hf, i, 0)) for w in ws]
            + [ANY_SPEC] * len(deps),
            out_specs=[pl.BlockSpec((1, 1, w.shape[1] // nt, w.shape[2]), lambda hf, i, s: (s[0], hf, i, 0)) for w in ws],
        ),
        out_shape=[jax.ShapeDtypeStruct((N_CHIPS,) + w.shape, _MXU) for w in ws],
        compiler_params=_params(("parallel", "parallel")),
    )((2 * xi + yi).reshape(1).astype(jnp.int32), *ws, *deps)


def _chip_gather_plan(stage, n_bufs):
    def plan(refs):
        x, y, c = _coords()
        me = 2 * x + y
        near = [(1 - x, y), (x, 1 - y)]
        slots = [2 * (1 - x) + y, 2 * x + (1 - y), 2 * (1 - x) + (1 - y)]
        sibling = (x, y, 1 - c)
        pass_to = (jnp.where(c == 0, x, 1 - x), jnp.where(c == 0, 1 - y, y), c)
        pass_slot = jnp.where(c == 0, slots[0], slots[1])
        out = []

        def move(src_slot, to, land_slot, land_core, pieces):
            for a, buf in enumerate(refs):
                for rows in _chunks(buf.shape[2], pieces[a]):
                    out.append((len(out), (buf.at[src_slot, c, rows], buf.at[src_slot, c, rows], to, None),
                                (buf.at[land_slot, land_core, rows], None)))

        if stage == "near":
            for k, chip in enumerate(near):
                move(me, (*chip, c), slots[k], c, NEAR_PIECES[:n_bufs])
        elif stage == "pass":
            move(pass_slot, pass_to, slots[2], c, PASS_PIECES[:n_bufs])
            for k in range(2):
                move(slots[k], sibling, slots[k], 1 - c, [1] * n_bufs)
        else:
            move(slots[2], sibling, slots[2], 1 - c, [1] * n_bufs)
        return out
    return plan


NEAR_PIECES = (2, 1)
PASS_PIECES = (2, 1)


def _chip_gather_copies(stage, n_bufs):
    if stage == "near":
        return 2 * sum(NEAR_PIECES[:n_bufs]), None
    if stage == "pass":
        n_pass = sum(PASS_PIECES[:n_bufs])
        return n_pass + 2 * n_bufs, set(range(n_pass))
    return n_bufs, None


def _swap_plan(n_slabs):
    def plan(refs):
        x, y, c = _coords()
        out, k = [], 0
        for i, n in enumerate(n_slabs):
            g, land = refs[2 * i], refs[2 * i + 1]
            for p in range(n):
                out.append((k, (g.at[p, 1 - c], land.at[p], (x, y, 1 - c), None), (land.at[p], None)))
                k += 1
        return out
    return plan


def _is_one_of(chip, dests):
    hit = chip == dests[0]
    for d in dests[1:]:
        hit = hit | (chip == d)
    return hit


def _slab_of(chip, dests):
    return sum(j * (chip == d).astype(jnp.int32) for j, d in enumerate(dests))


def _scatter_plan(dest_sets):
    def plan(refs):
        x, y, c = _coords()
        me = 2 * x + y
        out = []
        for k, (px, py) in enumerate(_other_chips(x, y)):
            peer = 2 * px + py
            for i, dests in enumerate(dest_sets):
                cs, land = refs[2 * i], refs[2 * i + 1]
                everyone = len(dests) == N_CHIPS
                send = (cs.at[_slab_of(peer, dests)], land.at[k], (px, py, c),
                        None if everyone else _is_one_of(peer, dests))
                recv = (land.at[k], None if everyone else _is_one_of(me, dests))
                out.append((k * len(dest_sets) + i, send, recv))
        return out
    return plan


def _join_plan(rows, n_pieces):
    def plan(refs):
        x, y, c = _coords()
        (buf,) = refs
        return [(i, (buf.at[c, piece], buf.at[c, piece], (x, y, 1 - c), None), (buf.at[1 - c, piece], None))
                for i, piece in enumerate(_chunks(rows, n_pieces))]
    return plan


def _join_plans(parts):
    def plan(refs):
        out, b0, k0 = [], 0, 0
        for part_plan, n_bufs, n_copies in parts:
            out += [(k0 + k, send, recv) for k, send, recv in part_plan(refs[b0:b0 + n_bufs])]
            b0 += n_bufs
            k0 += n_copies
        return out
    return plan


def _allgather_plan():
    def plan(refs):
        x, y, c = _coords()
        (land,) = refs
        me = 4 * x + 2 * y + c
        out = []
        for r in range(1, 8):
            px = 1 - x if r & 4 else x
            py = 1 - y if r & 2 else y
            pc = 1 - c if r & 1 else c
            out.append((r - 1, (land.at[me], land.at[me], (px, py, pc), None), (land.at[4 * px + 2 * py + pc], None)))
        return out
    return plan


def _sum_gathered(land, name):
    def body(land_ref, o_ref):
        acc = land_ref[0]
        for d in range(1, 8):
            acc = acc + land_ref[d]
        o_ref[...] = acc

    return pl.pallas_call(
        body,
        name=name,
        out_shape=jax.ShapeDtypeStruct(land.shape[1:], F32),
        compiler_params=_params(),
    )(land)


def _row_tile(rows, cap):
    t = cap
    while rows % t:
        t //= 2
    return t


def _add_my_half(g, r, name):
    n_slabs, _, R, C = g.shape
    tr = R if n_slabs > 1 else _row_tile(R, SUM_ROWS)

    def body(c_ref, g_ref, r_ref, o_ref):
        o_ref[...] = (g_ref[0] + r_ref[...]).astype(o_ref.dtype)

    return pl.pallas_call(
        body,
        name=name,
        grid_spec=pltpu.PrefetchScalarGridSpec(
            num_scalar_prefetch=1,
            grid=(n_slabs, R // tr),
            in_specs=[pl.BlockSpec((1, 1, tr, C), lambda p, i, c_ref: (p, c_ref[0], i, 0)),
                      pl.BlockSpec((1, tr, C), lambda p, i, c_ref: (p, i, 0))],
            out_specs=pl.BlockSpec((1, tr, C), lambda p, i, c_ref: (p, i, 0)),
        ),
        out_shape=jax.ShapeDtypeStruct(r.shape, jnp.bfloat16),
        compiler_params=_params(("parallel", "parallel")),
    )(lax.axis_index("c").reshape(1).astype(jnp.int32), g, r)


def _sum_slabs(own, got, name, deps=()):
    _, R, C = own.shape
    tr = _row_tile(R, SUM_ROWS)

    def body(s_ref, own_ref, got_ref, *rest):
        rest[-1][0] = ((own_ref[0].astype(F32) + got_ref[0].astype(F32)) + got_ref[1].astype(F32)) + got_ref[2].astype(F32)

    xi, yi, ci = _coords()
    return pl.pallas_call(
        body,
        name=name,
        grid_spec=pltpu.PrefetchScalarGridSpec(
            num_scalar_prefetch=1,
            grid=(R // tr,),
            in_specs=[pl.BlockSpec((1, tr, C), lambda i, s: (s[0], i, 0)),
                      pl.BlockSpec((3, tr, C), lambda i, s: (0, i, 0))] + [ANY_SPEC] * len(deps),
            out_specs=pl.BlockSpec((1, tr, C), lambda i, s: (s[1], i, 0)),
        ),
        out_shape=jax.ShapeDtypeStruct((2, R, C), F32),
        compiler_params=_params(("parallel",)),
    )(jnp.stack([2 * xi + yi, ci]).astype(jnp.int32), own, got, *deps)


def _sum_parts(owns, got, dest_sets, name):
    n = len(owns)
    _, R, C = owns[0].shape
    tr = _row_tile(R, SUM_ROWS)

    def body(s_ref, *refs):
        got_ref, o_ref = refs[n], refs[-1]
        total = jnp.zeros((tr, C), F32)
        for i in range(n):
            total = total + jnp.where(s_ref[2 + 2 * i] == 1, refs[i][0].astype(F32), 0.0)
        o_ref[0] = ((total + got_ref[0].astype(F32)) + got_ref[1].astype(F32)) + got_ref[2].astype(F32)

    xi, yi, ci = _coords()
    me = 2 * xi + yi
    scalars = [ci, ci]
    for dests in dest_sets:
        scalars += [_is_one_of(me, dests).astype(jnp.int32), _slab_of(me, dests)]
    own_spec = lambda i: pl.BlockSpec((1, tr, C), lambda r, s: (s[3 + 2 * i], r, 0))
    return pl.pallas_call(
        body,
        name=name,
        grid_spec=pltpu.PrefetchScalarGridSpec(
            num_scalar_prefetch=1,
            grid=(R // tr,),
            in_specs=[own_spec(i) for i in range(n)] + [pl.BlockSpec((3, tr, C), lambda r, s: (0, r, 0))],
            out_specs=pl.BlockSpec((1, tr, C), lambda r, s: (s[0], r, 0)),
        ),
        out_shape=jax.ShapeDtypeStruct((2, R, C), F32),
        compiler_params=_params(("parallel",)),
    )(jnp.stack(scalars).astype(jnp.int32), *owns, got)


def _adamw_math(w, g, m, v):
    m = ADAM_B1 * m + (1.0 - ADAM_B1) * g
    v = ADAM_B2 * v + (1.0 - ADAM_B2) * (g * g)
    m_hat = m / (1.0 - ADAM_B1 ** ADAM_STEP)
    v_hat = v / (1.0 - ADAM_B2 ** ADAM_STEP)
    delta = -ADAM_LR * (m_hat / (jnp.sqrt(v_hat) + ADAM_EPS) + ADAM_WD * w)
    return delta, m, v


def _adamw_halves(ws, g, ms, vs, half, prev, name, deps=()):
    n = len(ws)
    _, _, R, C = g.shape
    tr = _row_tile(R, ADAMW_ROWS)
    steps = R // tr
    carried = [] if prev is None else [a for four in prev for a in four]
    both = half is None
    which = (lambda i, s: i // steps) if both else (lambda i, s: s[0])
    half = 0 if both else half

    def body(s_ref, *refs):
        w_refs, g_refs, m_refs, v_refs = (refs[k * n:(k + 1) * n] for k in range(4))
        outs = refs[len(refs) - 4 * n:]
        for a in range(n):
            grad = g_refs[a][0, 0]
            d, mn, vn = _adamw_math(w_refs[a][...], grad, m_refs[a][...], v_refs[a][...])
            for o, val in zip(outs[4 * a:4 * a + 4], (grad, d, mn, vn)):
                o[...] = val

    rows = pl.BlockSpec((tr, C), lambda i, s: (which(i, s) * steps + i % steps, 0))
    grad_spec = lambda a: pl.BlockSpec((1, 1, tr, C), lambda i, s: (which(i, s), a, i % steps, 0))
    n_in = 4 * n
    outs = pl.pallas_call(
        body,
        name=name,
        grid_spec=pltpu.PrefetchScalarGridSpec(
            num_scalar_prefetch=1,
            grid=(2 * steps if both else steps,),
            in_specs=[rows] * n + [grad_spec(a) for a in range(n)] + [rows] * (2 * n)
            + [ANY_SPEC] * (len(carried) + len(deps)),
            out_specs=[rows] * (4 * n),
        ),
        out_shape=[jax.ShapeDtypeStruct((2 * R, C), F32)] * (4 * n),
        input_output_aliases={1 + n_in + k: k for k in range(len(carried))},
        compiler_params=_params(("parallel",)),
    )(jnp.reshape(half, (1,)).astype(jnp.int32), *ws, *([g] * n), *ms, *vs, *carried, *deps)
    return [outs[4 * a:4 * a + 4] for a in range(n)]


def _adamw_small(ws, gs, ms, vs, name):
    n = len(ws)

    def body(*refs):
        for a in range(n):
            d, mn, vn = _adamw_math(refs[a][...], refs[n + a][...], refs[2 * n + a][...], refs[3 * n + a][...])
            refs[4 * n + a][...] = d
            refs[5 * n + a][...] = mn
            refs[6 * n + a][...] = vn

    shapes = [jax.ShapeDtypeStruct(w.shape, F32) for w in ws]
    outs = pl.pallas_call(
        body,
        name=name,
        out_shape=shapes * 3,
        compiler_params=_params(),
    )(*ws, *gs, *ms, *vs)
    return outs[:n], outs[n:2 * n], outs[2 * n:]


def _to_blockdiag(w):
    per = CW // LRU_BW
    w4 = w.reshape(N_CT, per, LRU_BW, LRU_BW)
    eye = jnp.eye(per, dtype=w.dtype)
    return (w4[:, :, :, None, :] * eye[None, :, None, :, None]).reshape(N_CT, CW, CW)


def _from_blockdiag(g):
    per = CW // LRU_BW
    g5 = g.reshape(N_CT, per, LRU_BW, per, LRU_BW)
    return jnp.stack([g5[:, b, :, b, :] for b in range(per)], axis=1).reshape(LRU_BLOCKS, LRU_BW, LRU_BW)


def _local_grads(x2d, tgt2d, B, S, g_in, in_proj, conv_b, gate_x_w, gate_x_b, gate_a_w, gate_a_b, lam,
                 proj_weights, g_fin, reduce):
    wx_bd = _c(_to_blockdiag(gate_x_w))
    wa_bd = _c(_to_blockdiag(gate_a_w))
    tables = _retention_tables(S)

    proj, ht, w_all, conv_w, gain = in_proj(x2d, g_in, (*tables, wx_bd, wa_bd))
    gain3 = gain.reshape(HEADS, 1, DK)
    hlru, ya = _lru_fwd(proj, conv_w, conv_b, wx_bd, wa_bd, gate_x_b, gate_a_b, lam, B, S)
    o_pre, yb, states = _ret_fwd(proj, tables, gain3, B, S)
    wpa, wpb, wout = proj_weights(yb)
    loss, dx2, dya, dyb, dm, dgf, gw_proj = _mid(ya, yb, proj, x2d, tgt2d, wpa, wpb, wout, g_fin)
    g3 = _inproj_bwd_dw(ht, [dm], "inproj_bwd_dw_m")
    deps = reduce.m_ready(gw_proj, g3)
    dr, dgain = _ret_bwd(dyb, o_pre, proj, states, tables, gain3, B, S, deps)
    deps = reduce.ret_done(dr)
    g12 = _inproj_bwd_dw(ht, [dr], "inproj_bwd_dw_r", deps)
    deps = reduce.r_ready(g12)
    dxa, dga, dcw, dcb, dwx_bd, dwa_bd, dbx, dba, dlam = _lru_bwd(
        dya, proj, hlru, conv_w, conv_b, wx_bd, wa_bd, gate_x_b, gate_a_b, lam, B, S, deps)
    small = dict(conv_w=dcw, conv_b=dcb, gate_x_w=_from_blockdiag(dwx_bd), gate_x_b=dbx,
                 gate_a_w=_from_blockdiag(dwa_bd), gate_a_b=dba, lru_lambda=dlam, gn_gain=dgain.reshape(HEADS, DK),
                 norm_final=dgf)
    loss_rows = jnp.broadcast_to(loss, (SUBLANES, LANES))
    deps = reduce.lru_done(dxa, jnp.concatenate([_pack_small(small), loss_rows], axis=0))
    g0 = _inproj_bwd_dw(ht, [dxa, dga], "inproj_bwd_dw_a", deps)
    deps = reduce.a_ready(g0)
    n_tiles = x2d.shape[0] // min(DX_TILE, x2d.shape[0])
    grad_x, dgin = _inproj_bwd_dx([dxa, dga, dr, dm], w_all, x2d, dx2, g_in, 0, n_tiles, None, "inproj_bwd_dx", deps)
    return grad_x, dgin


ALL_CHIPS = (0, 1, 2, 3)


class _GradReduce:
    def __init__(self, proj_done):
        self.pending = {}
        self.proj_done = proj_done
        self.land_in = None

    def _start(self, key, parts, name):
        bufs, plans, shared = [], [], None
        for part_bufs, plan, n_copies, part_shared in parts:
            if part_shared is not None:
                shared = len(bufs) + part_shared
            plans.append((plan, len(part_bufs), n_copies))
            bufs += part_bufs
        plan = _join_plans(plans)
        send_sems, recv_sems, bufs, token = _copies_start(bufs, plan, sum(p[2] for p in plans), name + "_start")
        if shared is not None:
            self.land_in = bufs[shared]
        self.pending[key] = (send_sems, recv_sems, bufs, plan, name + "_wait", shared)
        return (token,)

    def _finish(self, key, after):
        send_sems, recv_sems, bufs, plan, name, shared = self.pending.pop(key)
        if shared is not None:
            bufs[shared] = self.land_in
        bufs = _copies_wait(send_sems, recv_sems, bufs, after, plan, name)
        if shared is not None:
            self.land_in = bufs[shared]
        return bufs

    @staticmethod
    def _swap(pieces):
        bufs = []
        for g in pieces:
            bufs += [g, lax.empty((g.shape[0],) + g.shape[2:], F32)]
        n_slabs = [g.shape[0] for g in pieces]
        return bufs, _swap_plan(n_slabs), sum(n_slabs), None

    def _scatter(self, sums, dest_sets):
        bufs = []
        for cs in sums:
            bufs += [cs, lax.empty((3,) + cs.shape[1:], cs.dtype)]
        if self.land_in is not None:
            bufs[-1] = self.land_in
        return bufs, _scatter_plan(dest_sets), 3 * len(sums), len(bufs) - 1

    @staticmethod
    def _gather8(block):
        x, y, c = _coords()
        land = lax.dynamic_update_slice(lax.empty((8,) + block.shape, F32), block[None], (4 * x + 2 * y + c, 0, 0))
        return [land], _allgather_plan(), 7, None

    def m_ready(self, gw_proj, g3):
        rows = gw_proj.shape[2] * gw_proj.shape[3]
        return self._start("m", [self._swap([gw_proj.reshape(N_CHIPS, 2, rows, D_MODEL), g3])], "swap_m")

    def ret_done(self, after):
        proj, land_p, g3, land_3 = self._finish("m", after)
        sums_m = [_add_my_half(proj, land_p, "chip_sum_proj"), _add_my_half(g3, land_3, "chip_sum_m")]
        return self._start("sm", [self._scatter(sums_m, [ALL_CHIPS, (3,)])], "scatter_m")

    def r_ready(self, g12):
        return self._start("r", [self._swap([g12])], "swap_r")

    def lru_done(self, after, packed):
        g12, land_12 = self._finish("r", after)
        sums_r = [_add_my_half(g12, land_12, "chip_sum_r")]
        return (self._start("sr", [self._scatter(sums_r, [(1, 2)])], "scatter_r")
                + self._start("small", [self._gather8(packed)], "gather_small"))

    def a_ready(self, g0):
        (token,) = self._start("a", [self._swap([g0])], "swap_a")
        csp, gotp, self.cs3, _ = self._finish("sm", token)
        half_proj = _sum_slabs(csp, gotp, "sum_w_proj")
        g0, land_0 = self._finish("a", half_proj)
        join = ([half_proj], _join_plan(half_proj.shape[1], PROJ_JOIN_PIECES), PROJ_JOIN_PIECES, None)
        return self._start("sa", [self._scatter([_add_my_half(g0, land_0, "chip_sum_a")], [(0,)]), join], "scatter_a")

    def finish(self, dgin, w_in_done):
        (token,) = self._start("n", [self._gather8(dgin)], "gather_norm_in")
        (small,) = self._finish("small", token)
        cs12, _ = self._finish("sr", token)
        cs0, _, g_proj = self._finish("sa", token)
        self.proj_done(g_proj)
        half_in =_sum_parts([self.cs3, cs12, cs0], self.land_in, [(3,), (1, 2), (0,)], "sum_w_in")
        deps = self._start("j", [([half_in], _join_plan(half_in.shape[1], JOIN_PIECES), JOIN_PIECES, None)], "join_w_in")
        first = w_in_done(self.pending["j"][2][0], True, None, deps)
        (g_in,) = self._finish("j", first[1])
        done = w_in_done(g_in, False, first, ())
        (norm_in,) = self._finish("n", done[1])
        return _sum_gathered(small, "sum_small_grads"), _sum_gathered(norm_in, "sum_norm_in_grad")


_SMALL = ("gate_x_w", "gate_a_w", "conv_w", "conv_b", "gate_x_b", "gate_a_b", "lru_lambda", "gn_gain", "norm_final")
_SMALL_SHAPES = dict(gate_x_w=(LRU_BLOCKS, LRU_BW, LRU_BW), gate_a_w=(LRU_BLOCKS, LRU_BW, LRU_BW),
                     norm_in=(1, D_MODEL), conv_w=(CONV, D_MODEL), conv_b=(1, D_MODEL), gate_x_b=(1, D_MODEL),
                     gate_a_b=(1, D_MODEL), lru_lambda=(1, D_MODEL), gn_gain=(HEADS, DK), norm_final=(1, D_MODEL))


def _pack_small(small):
    return jnp.concatenate([small[k].reshape(-1, 128) for k in _SMALL], axis=0)


def _unpack_small(packed):
    out, r = {}, 0
    for k in _SMALL:
        shape = _SMALL_SHAPES[k]
        rows = 1
        for s in shape:
            rows *= s
        rows //= 128
        out[k] = packed[r:r + rows].reshape(shape)
        r += rows
    return out


def kernel(x, norm_in, w_in, conv_w, conv_b, gate_x_w, gate_x_b, gate_a_w, gate_a_b, lru_lambda, gn_gain, w_proj_a, w_proj_b, w_out, norm_final, loss_target, m_norm_in, m_w_in, m_conv_w, m_conv_b, m_gate_x_w, m_gate_x_b, m_gate_a_w, m_gate_a_b, m_lru_lambda, m_gn_gain, m_w_proj_a, m_w_proj_b, m_w_out, m_norm_final, v_norm_in, v_w_in, v_conv_w, v_conv_b, v_gate_x_w, v_gate_x_b, v_gate_a_w, v_gate_a_b, v_lru_lambda, v_gn_gain, v_w_proj_a, v_w_proj_b, v_w_out, v_norm_final):
    B, S, _ = x.shape
    T = B * S
    xi, yi, ci = _coords()
    chip = 2 * xi + yi

    cshard = D_MODEL // N_CHIPS
    mine = _cast_into_slot([w_in[0].reshape(2, D_MODEL // 2, 2 * D_MODEL)], "cast_w_in")
    plan = _gather_plan(3)
    pending_proj = []
    gshard = DK // N_CHIPS
    tiny = jnp.concatenate([conv_w[0], jnp.zeros((4, cshard), F32), jnp.pad(gn_gain[0], ((0, 4), (0, cshard - gshard)))],
                           axis=0).reshape(1, 2, SUBLANES, cshard)
    tiny_buf = lax.dynamic_update_slice(lax.empty((N_CHIPS, 2, SUBLANES, cshard), F32), tiny, (chip, 0, 0, 0))
    near_plan, pass_plan, far_plan = (_chip_gather_plan(stage, 2) for stage in ("near", "pass", "far"))
    (n_near, _), (n_pass, passed_on), (n_far, _) = (_chip_gather_copies(stage, 2) for stage in ("near", "pass", "far"))
    halves = set(range(n_pass)) - passed_on
    near_s, near_r, bufs, near_token = _copies_start([mine[0], tiny_buf], near_plan, n_near, "gather_near_start")

    def in_proj(x2d, g_in, meanwhile):
        as_w = lambda b: b[0].reshape(N_CHIPS, D_MODEL, 2 * D_MODEL)
        slot_x, slot_y, slot_d = 2 * (1 - xi) + yi, 2 * xi + (1 - yi), 2 * (1 - xi) + (1 - yi)
        ids = lambda *chips: jnp.stack(chips).astype(jnp.int32)
        proj, hb, ht = _inproj_first(x2d, g_in, as_w(bufs), ids(chip), "inproj_own", (near_token, *meanwhile))
        got = _copies_wait(near_s, near_r, bufs, proj, near_plan, "gather_near_wait")
        pass_s, pass_r, got, pass_token = _copies_start(got, pass_plan, n_pass, "gather_pass_start")
        mine_proj = _cast_into_slot([w[0].reshape(2, cshard // 2, D_MODEL) for w in (w_proj_a, w_proj_b, w_out)],
                                    "cast_w_proj", (pass_token,))
        got = _copies_wait(pass_s, pass_r, got, mine_proj[0], pass_plan, "gather_pass_wait_halves", only=halves)
        proj = _inproj_more(hb, as_w(got), ids(slot_x, slot_y), proj, "inproj_near")
        got = _copies_wait(pass_s, pass_r, got, proj, pass_plan, "gather_pass_wait_far", only=passed_on)
        far_s, far_r, got, far_token = _copies_start(got, far_plan, n_far, "gather_far_start")
        pending_proj.append(_copies_start(mine_proj, plan, 9, "gather_proj_start", (far_token,)))
        got = _copies_wait(far_s, far_r, got, pending_proj[0][3], far_plan, "gather_far_wait")
        proj = _inproj_more(hb, as_w(got), ids(slot_d), proj, "inproj_far")
        tiny_all = got[1].reshape(N_CHIPS, 2 * SUBLANES, cshard)
        conv_w_full = jnp.transpose(tiny_all[:, 0:CONV, :], (1, 0, 2)).reshape(CONV, D_MODEL)
        gain_full = jnp.transpose(tiny_all[:, 8:8 + HEADS, :gshard], (1, 0, 2)).reshape(HEADS, DK)
        return proj, ht, as_w(got), conv_w_full, gain_full

    def proj_weights(after):
        s_sems, r_sems, pbufs, _ = pending_proj[0]
        got = _copies_wait(s_sems, r_sems, pbufs, after, plan, "gather_proj_wait")
        return [b.reshape(D_MODEL, D_MODEL) for b in got]

    weights = dict(norm_in=norm_in, w_in=w_in, conv_w=conv_w, conv_b=conv_b, gate_x_w=gate_x_w, gate_x_b=gate_x_b,
                   gate_a_w=gate_a_w, gate_a_b=gate_a_b, lru_lambda=lru_lambda, gn_gain=gn_gain, w_proj_a=w_proj_a,
                   w_proj_b=w_proj_b, w_out=w_out, norm_final=norm_final)
    ms = dict(norm_in=m_norm_in, w_in=m_w_in, conv_w=m_conv_w, conv_b=m_conv_b, gate_x_w=m_gate_x_w,
              gate_x_b=m_gate_x_b, gate_a_w=m_gate_a_w, gate_a_b=m_gate_a_b, lru_lambda=m_lru_lambda, gn_gain=m_gn_gain,
              w_proj_a=m_w_proj_a, w_proj_b=m_w_proj_b, w_out=m_w_out, norm_final=m_norm_final)
    vs = dict(norm_in=v_norm_in, w_in=v_w_in, conv_w=v_conv_w, conv_b=v_conv_b, gate_x_w=v_gate_x_w,
              gate_x_b=v_gate_x_b, gate_a_w=v_gate_a_w, gate_a_b=v_gate_a_b, lru_lambda=v_lru_lambda, gn_gain=v_gn_gain,
              w_proj_a=v_w_proj_a, w_proj_b=v_w_proj_b, w_out=v_w_out, norm_final=v_norm_final)
    names = list(weights)
    grads, delta, new_m, new_v = {}, {}, {}, {}

    def update_big(keys, g, half, prev, name, deps=()):
        two = lambda a: a.reshape(a.shape[1], a.shape[2])
        res = _adamw_halves([two(weights[k]) for k in keys], g, [two(ms[k]) for k in keys], [two(vs[k]) for k in keys],
                            half, prev, name, deps)
        for k, (gk, d, mn, vn) in zip(keys, res):
            shp = weights[k].shape
            grads[k], delta[k], new_m[k], new_v[k] = gk.reshape(shp), d.reshape(shp), mn.reshape(shp), vn.reshape(shp)
        return res

    def proj_done(g_proj):
        g4 = g_proj.reshape(2, 3, D_MODEL // (2 * N_CHIPS), D_MODEL)
        return update_big(("w_proj_a", "w_proj_b", "w_out"), g4, None, None, "adamw_proj")[-1][1]

    def w_in_done(g_in, own, prev, deps):
        g4 = g_in.reshape(2, 1, D_MODEL // 2, 2 * D_MODEL)
        return update_big(("w_in",), g4, ci if own else 1 - ci, None if prev is None else [prev],
                          "adamw_w_in_own" if own else "adamw_w_in_other", deps)[0]

    reduce = _GradReduce(proj_done)
    grad_x, dgin = _local_grads(
        x.reshape(T, D_MODEL), loss_target.reshape(T, D_MODEL), B, S, norm_in, in_proj, conv_b,
        gate_x_w[0], gate_x_b, gate_a_w[0], gate_a_b, lru_lambda, proj_weights,
        norm_final.reshape(1, D_MODEL), reduce)

    small_sum, g_norm_in = reduce.finish(dgin.reshape(SUBLANES, LANES), w_in_done)
    loss = small_sum[small_sum.shape[0] - SUBLANES, 0]

    gsm = _unpack_small(small_sum)
    gsm["norm_in"] = g_norm_in
    gsm["conv_w"] = lax.dynamic_slice_in_dim(gsm["conv_w"], chip * cshard, cshard, axis=1)
    gsm["gn_gain"] = lax.dynamic_slice_in_dim(gsm["gn_gain"], chip * gshard, gshard, axis=1)
    smalls = [k for k in names if k not in delta]

    def view(a):
        return a.reshape(1, -1) if a.ndim == 1 else (a.reshape(a.shape[1:]) if a.ndim > 2 else a)

    ds, mns, vns = _adamw_small([view(weights[k]) for k in smalls], [gsm[k].reshape(view(weights[k]).shape) for k in smalls],
                                [view(ms[k]) for k in smalls], [view(vs[k]) for k in smalls], "adamw_small")
    for k, d, mn, vn in zip(smalls, ds, mns, vns):
        shp = weights[k].shape
        grads[k], delta[k], new_m[k], new_v[k] = gsm[k].reshape(shp), d.reshape(shp), mn.reshape(shp), vn.reshape(shp)

    return (loss, grad_x.reshape(B, S, D_MODEL), *[grads[k] for k in names], *[delta[k] for k in names],
            *[new_m[k] for k in names], *[new_v[k] for k in names])
```

```python
import jax
import jax.numpy as jnp
from jax import lax
from jax.experimental import pallas as pl
from jax.experimental.pallas import tpu as pltpu

F32 = jnp.float32
_MXU = jnp.bfloat16

D_MODEL = 1024
N_GROUPS = 8
HEADS = 4
DK = 256
CHUNK = 128
CONV = 4
LRU_BLOCKS = 16
LRU_BW = 64
LRU_C = 8.0
ROPE_THETA = 10000.0
EPS = 1e-6
CW = 256
N_CT = D_MODEL // CW
N_CHIPS = 4
MESH = pl.DeviceIdType.MESH

ADAM_LR = 0.001
ADAM_B1 = 0.9
ADAM_B2 = 0.999
ADAM_EPS = 1e-08
ADAM_WD = 0.01
ADAM_STEP = 10

VMEM_LIMIT = 56 * 1024 * 1024

FIRST_PROJ_TILE = 1024
MORE_PROJ_TILE = 2048
SCAN_TILE = 1024
MID_TILE = 256
DX_TILE = 512
DW_COLS = 512
DW_LOADS = 4
RET_CHUNKS = 2
SUM_ROWS = 256
ADAMW_ROWS = 256
JOIN_PIECES = 8
PROJ_JOIN_PIECES = 4


def _c(v):
    return v.astype(_MXU)


def _dot(a, b):
    return lax.dot_general(a, b, (((1,), (0,)), ((), ())), preferred_element_type=F32)


def _dot_nt(a, b):
    return lax.dot_general(a, b, (((1,), (1,)), ((), ())), preferred_element_type=F32)


def _dot_tn(a, b):
    return lax.dot_general(a, b, (((0,), (0,)), ((), ())), preferred_element_type=F32)


def _sigmoid(z):
    return 0.5 * jnp.tanh(0.5 * z) + 0.5


ANY_SPEC = pl.BlockSpec(memory_space=pl.ANY)


def _after(body, n_in, deps):
    n_deps = len(deps)

    def wrapped(*refs):
        return body(*refs[:n_in], *refs[n_in + n_deps:])

    return wrapped


def _params(sem=None):
    if sem is None:
        return pltpu.CompilerParams(vmem_limit_bytes=VMEM_LIMIT)
    return pltpu.CompilerParams(vmem_limit_bytes=VMEM_LIMIT, dimension_semantics=sem)


def _inproj_first(x2d, g_in, w_all, chips, name, deps=()):
    T = x2d.shape[0]
    tm = min(FIRST_PROJ_TILE, T)
    n_i = T // tm

    def body(s_ref, *refs):
        x_ref, g_ref, w_ref = refs[:3]
        proj_ref, hb_ref, ht_ref, h_all = refs[-4:]
        i = pl.program_id(1)
        rows = pl.ds(pl.multiple_of(i * tm, tm), tm)

        @pl.when(pl.program_id(0) == 0)
        def _():
            x = x_ref[...]
            r = lax.rsqrt(jnp.mean(x * x, axis=-1, keepdims=True) + EPS)
            h = x * r * g_ref[...]
            hb = h.astype(h_all.dtype)
            h_all[rows, :] = hb
            hb_ref[...] = hb
            ht_ref[...] = h.T.astype(ht_ref.dtype)

        proj_ref[...] = _dot(h_all[rows, :], w_ref[0])

    first = lambda j, i: jnp.where(j == 0, i, n_i - 1)
    return pl.pallas_call(
        body,
        name=name,
        grid_spec=pltpu.PrefetchScalarGridSpec(
            num_scalar_prefetch=1,
            grid=(2 * chips.shape[0], n_i),
            in_specs=[
                pl.BlockSpec((tm, D_MODEL), lambda j, i, s: (first(j, i), 0)),
                pl.BlockSpec((1, D_MODEL), lambda j, i, s: (0, 0)),
                pl.BlockSpec((1, D_MODEL, D_MODEL), lambda j, i, s: (s[j // 2], 0, j % 2)),
            ] + [ANY_SPEC] * len(deps),
            out_specs=[
                pl.BlockSpec((tm, D_MODEL), lambda j, i, s: (i, 2 * s[j // 2] + j % 2)),
                pl.BlockSpec((tm, D_MODEL), lambda j, i, s: (first(j, i), 0)),
                pl.BlockSpec((D_MODEL, tm), lambda j, i, s: (0, first(j, i))),
            ],
            scratch_shapes=[pltpu.VMEM((T, D_MODEL), _MXU)],
        ),
        out_shape=[
            jax.ShapeDtypeStruct((T, N_GROUPS * D_MODEL), F32),
            jax.ShapeDtypeStruct((T, D_MODEL), _MXU),
            jax.ShapeDtypeStruct((D_MODEL, T), _MXU),
        ],
        compiler_params=_params(("arbitrary", "arbitrary")),
    )(chips, x2d, g_in, w_all, *deps)


def _inproj_more(hb, w_all, chips, proj, name):
    T = hb.shape[0]
    tm = min(MORE_PROJ_TILE, T)

    def body(s_ref, hb_hbm, w_ref, prev_ref, proj_ref, h_all, sem):
        @pl.when((pl.program_id(0) == 0) & (pl.program_id(1) == 0))
        def _():
            cp = pltpu.make_async_copy(hb_hbm, h_all, sem)
            cp.start()
            cp.wait()

        rows = pl.ds(pl.multiple_of(pl.program_id(1) * tm, tm), tm)
        proj_ref[...] = _dot(h_all[rows, :], w_ref[0])

    return pl.pallas_call(
        body,
        name=name,
        grid_spec=pltpu.PrefetchScalarGridSpec(
            num_scalar_prefetch=1,
            grid=(2 * chips.shape[0], T // tm),
            in_specs=[
                ANY_SPEC,
                pl.BlockSpec((1, D_MODEL, D_MODEL), lambda j, i, s: (s[j // 2], 0, j % 2)),
                ANY_SPEC,
            ],
            out_specs=pl.BlockSpec((tm, D_MODEL), lambda j, i, s: (i, 2 * s[j // 2] + j % 2)),
            scratch_shapes=[pltpu.VMEM((T, D_MODEL), hb.dtype), pltpu.SemaphoreType.DMA],
        ),
        out_shape=jax.ShapeDtypeStruct(proj.shape, F32),
        input_output_aliases={3: 0},
        compiler_params=_params(("arbitrary", "arbitrary")),
    )(chips, hb, w_all, proj)


def _scan_fwd(a, u):
    n = a.shape[0]
    row = lax.broadcasted_iota(jnp.int32, a.shape, 0)
    s = 1
    while s < n:
        m = row >= s
        u = u + a * jnp.where(m, pltpu.roll(u, s, 0), 0.0)
        a = a * jnp.where(m, pltpu.roll(a, s, 0), 1.0)
        s *= 2
    return a, u


def _scan_bwd(b, g):
    n = b.shape[0]
    row = lax.broadcasted_iota(jnp.int32, b.shape, 0)
    s = 1
    while s < n:
        m = row < n - s
        g = g + b * jnp.where(m, pltpu.roll(g, n - s, 0), 0.0)
        b = b * jnp.where(m, pltpu.roll(b, n - s, 0), 1.0)
        s *= 2
    return b, g


LANES = 128
SUBLANES = 8


def _scan_scratch(tc):
    by_lanes = pltpu.VMEM((CW // LANES, tc, LANES), F32)
    return [by_lanes, by_lanes, pltpu.VMEM((tc // SUBLANES, CW), F32), pltpu.VMEM((tc, CW), F32)]


def _scan_tile(a, u, edge, la_ref, lh_ref, c_ref, dst_ref, reverse):
    n, w = a.shape
    groups = n // SUBLANES
    a3 = a.reshape(groups, SUBLANES, w)
    u3 = u.reshape(groups, SUBLANES, w)
    row = lax.broadcasted_iota(jnp.int32, a3.shape, 1)
    for s in (1, 2, 4):
        m = (row < SUBLANES - s) if reverse else (row >= s)
        shift = SUBLANES - s if reverse else s
        u3 = u3 + a3 * jnp.where(m, pltpu.roll(u3, shift, 1), 0.0)
        a3 = a3 * jnp.where(m, pltpu.roll(a3, shift, 1), 1.0)
    al = a3.reshape(n, w)
    hl = u3.reshape(n, w)
    blocks = w // LANES
    for q in range(blocks):
        la_ref[q] = al[:, q * LANES:(q + 1) * LANES]
        lh_ref[q] = hl[:, q * LANES:(q + 1) * LANES]
    ends = pl.ds(0 if reverse else SUBLANES - 1, groups, stride=SUBLANES)
    end_a = jnp.concatenate([la_ref.at[q][ends, :] for q in range(blocks)], axis=-1)
    end_h = jnp.concatenate([lh_ref.at[q][ends, :] for q in range(blocks)], axis=-1)
    prod, part = (_scan_bwd if reverse else _scan_fwd)(end_a, end_h)
    total = part + prod * edge
    g_row = lax.broadcasted_iota(jnp.int32, total.shape, 0)
    if reverse:
        c_ref[...] = jnp.where(g_row == groups - 1, edge, pltpu.roll(total, groups - 1, 0))
    else:
        c_ref[...] = jnp.where(g_row == 0, edge, pltpu.roll(total, 1, 0))
    for g in range(groups):
        rows = slice(g * SUBLANES, (g + 1) * SUBLANES)
        for q in range(blocks):
            cols = slice(q * LANES, (q + 1) * LANES)
            dst_ref[rows, cols] = lh_ref[q, rows, :] + la_ref[q, rows, :] * c_ref[g:g + 1, cols]


def _softplus_neg(lam):
    z = -lam
    return jnp.maximum(z, 0.0) + jnp.log1p(jnp.exp(-jnp.abs(z)))


def _lru_gates(xc, wx_ref, wa_ref, bx_ref, ba_ref, lam_ref):
    xcb = _c(xc)
    i_t = _sigmoid(_dot(xcb, wx_ref[0]) + bx_ref[...])
    r_t = _sigmoid(_dot(xcb, wa_ref[0]) + ba_ref[...])
    sp = _softplus_neg(lam_ref[...])
    log_a = (-LRU_C) * r_t * sp
    a = jnp.exp(log_a)
    mult = jnp.sqrt(1.0 - a * a)
    return xcb, i_t, r_t, sp, a, mult


def _conv_from_ext(ext_ref, xa, cw_ref, cb_ref, tc):
    return (cb_ref[...] + cw_ref[3:4, :] * xa + cw_ref[2:3, :] * ext_ref[7:7 + tc, :]
            + cw_ref[1:2, :] * ext_ref[6:6 + tc, :] + cw_ref[0:1, :] * ext_ref[5:5 + tc, :])


def _lru_fwd(proj, conv_w, conv_b, wx_bd, wa_bd, bx, ba, lam, B, S):
    T = B * S
    tc = min(SCAN_TILE, S)
    nt = S // tc
    h8 = tc // 8

    def body(xa_ref, halo_ref, ga_ref, cw_ref, cb_ref, wx_ref, wa_ref, bx_ref, ba_ref, lam_ref,
             h_ref, ya_ref, ext_ref, carry_ref, la_ref, lh_ref, c_ref):
        t = pl.program_id(2)

        @pl.when(t == 0)
        def _():
            carry_ref[...] = jnp.zeros_like(carry_ref)

        xa = xa_ref[...]
        ext_ref[0:8, :] = jnp.where(t == 0, 0.0, halo_ref[...])
        ext_ref[8:8 + tc, :] = xa
        xc = _conv_from_ext(ext_ref, xa, cw_ref, cb_ref, tc)
        _, i_t, _, _, a, mult = _lru_gates(xc, wx_ref, wa_ref, bx_ref, ba_ref, lam_ref)
        u = mult * (i_t * xc)
        _scan_tile(a, u, carry_ref[7:8, :], la_ref, lh_ref, c_ref, h_ref, False)
        h = h_ref[...]
        carry_ref[...] = h[tc - 8:tc, :]
        ga = ga_ref[...]
        ya_ref[...] = (ga * _sigmoid(ga) * h).astype(ya_ref.dtype)

    row = lambda b, t: b * nt + t
    vec = pl.BlockSpec((1, CW), lambda b, c, t: (0, c))
    mat = pl.BlockSpec((1, CW, CW), lambda b, c, t: (c, 0, 0))
    return pl.pallas_call(
        body,
        name="lru_fwd",
        grid=(B, N_CT, nt),
        in_specs=[
            pl.BlockSpec((tc, CW), lambda b, c, t: (row(b, t), c)),
            pl.BlockSpec((8, CW), lambda b, c, t: (jnp.maximum(row(b, t) * h8 - 1, 0), c)),
            pl.BlockSpec((tc, CW), lambda b, c, t: (row(b, t), N_CT + c)),
            pl.BlockSpec((CONV, CW), lambda b, c, t: (0, c)),
            vec, mat, mat, vec, vec, vec,
        ],
        out_specs=[
            pl.BlockSpec((tc, CW), lambda b, c, t: (row(b, t), c)),
            pl.BlockSpec((tc, CW), lambda b, c, t: (row(b, t), c)),
        ],
        out_shape=[
            jax.ShapeDtypeStruct((T, D_MODEL), F32),
            jax.ShapeDtypeStruct((T, D_MODEL), _MXU),
        ],
        scratch_shapes=[pltpu.VMEM((tc + 8, CW), F32), pltpu.VMEM((8, CW), F32)] + _scan_scratch(tc)[:3],
        compiler_params=_params(("parallel", "parallel", "arbitrary")),
    )(proj, proj, proj, conv_w, conv_b, wx_bd, wa_bd, bx, ba, lam)


def _lru_bwd(dya, proj, hlru, conv_w, conv_b, wx_bd, wa_bd, bx, ba, lam, B, S, deps=()):
    T = B * S
    tc = min(SCAN_TILE, S)
    nt = S // tc
    h8 = tc // 8

    def body(dya_ref, xa_ref, xhalo_ref, ga_ref, h_ref, hhalo_ref, cw_ref, cb_ref, wx_ref, wa_ref, bx_ref, ba_ref,
             lam_ref, dxa_ref, dga_ref, dcw_ref, dcb_ref, dwx_ref, dwa_ref, dbx_ref, dba_ref, dlam_ref,
             ext_ref, ext2_ref, carry_ref, dhalo_ref, la_ref, lh_ref, c_ref, dh_ref, accx_ref, acca_ref):
        b = pl.program_id(1)
        t = pl.program_id(2)
        tt = nt - 1 - t

        @pl.when(t == 0)
        def _():
            carry_ref[...] = jnp.zeros_like(carry_ref)
            dhalo_ref[...] = jnp.zeros_like(dhalo_ref)

        @pl.when((t == 0) & (b == 0))
        def _():
            for r in (dcw_ref, dcb_ref, accx_ref, acca_ref, dbx_ref, dba_ref, dlam_ref):
                r[...] = jnp.zeros_like(r)

        xa = xa_ref[...]
        ext_ref[0:8, :] = jnp.where(tt == 0, 0.0, xhalo_ref[...])
        ext_ref[8:8 + tc, :] = xa
        xc = _conv_from_ext(ext_ref, xa, cw_ref, cb_ref, tc)
        xcb, i_t, r_t, sp, a, mult = _lru_gates(xc, wx_ref, wa_ref, bx_ref, ba_ref, lam_ref)

        h = h_ref[...]
        ga = ga_ref[...]
        dya_t = dya_ref[...]
        sg = _sigmoid(ga)
        dga_ref[...] = (dya_t * h * (sg * (1.0 + ga * (1.0 - sg)))).astype(dga_ref.dtype)
        dlru = dya_t * (ga * sg)

        row = lax.broadcasted_iota(jnp.int32, a.shape, 0)
        coef = jnp.where(row == tc - 1, 1.0, pltpu.roll(a, tc - 1, 0))
        _scan_tile(coef, dlru, carry_ref[0:1, :], la_ref, lh_ref, c_ref, dh_ref, True)
        dh = dh_ref[...]
        ext2_ref[0:tc, :] = a * dh
        carry_ref[...] = ext2_ref[0:8, :]

        ext2_ref[0:8, :] = jnp.where(tt == 0, 0.0, hhalo_ref[...])
        ext2_ref[8:8 + tc, :] = h
        hprev = ext2_ref[7:7 + tc, :]

        da = dh * hprev
        ix = i_t * xc
        dmult = dh * ix
        di = dh * mult * xc
        dxc = dh * mult * i_t
        dlog_a = da * a - dmult * (a * a) / mult
        dr = dlog_a * ((-LRU_C) * sp)
        dlam_ref[...] += jnp.sum(dlog_a * r_t, axis=0, keepdims=True) * (LRU_C * _sigmoid(-lam_ref[...]))
        dza = dr * r_t * (1.0 - r_t)
        dzx = di * i_t * (1.0 - i_t)
        dzab = _c(dza)
        dzxb = _c(dzx)
        dxc = dxc + _dot_nt(dzxb, wx_ref[0]) + _dot_nt(dzab, wa_ref[0])
        accx_ref[...] += _dot_tn(xcb, dzxb)
        acca_ref[...] += _dot_tn(xcb, dzab)
        dbx_ref[...] += jnp.sum(dzx, axis=0, keepdims=True)
        dba_ref[...] += jnp.sum(dza, axis=0, keepdims=True)

        dcb_ref[...] += jnp.sum(dxc, axis=0, keepdims=True)
        dcw_ref[3:4, :] += jnp.sum(dxc * xa, axis=0, keepdims=True)
        dcw_ref[2:3, :] += jnp.sum(dxc * ext_ref[7:7 + tc, :], axis=0, keepdims=True)
        dcw_ref[1:2, :] += jnp.sum(dxc * ext_ref[6:6 + tc, :], axis=0, keepdims=True)
        dcw_ref[0:1, :] += jnp.sum(dxc * ext_ref[5:5 + tc, :], axis=0, keepdims=True)
        ext2_ref[0:tc, :] = dxc
        ext2_ref[tc:tc + 8, :] = dhalo_ref[...]
        dxa = (cw_ref[3:4, :] * dxc + cw_ref[2:3, :] * ext2_ref[1:1 + tc, :]
               + cw_ref[1:2, :] * ext2_ref[2:2 + tc, :] + cw_ref[0:1, :] * ext2_ref[3:3 + tc, :])
        dxa_ref[...] = dxa.astype(dxa_ref.dtype)
        dhalo_ref[...] = ext2_ref[0:8, :]

        @pl.when((b == B - 1) & (t == nt - 1))
        def _():
            lane_block = lax.broadcasted_iota(jnp.int32, (LRU_BW, CW), 1) // LRU_BW
            for acc_ref, out_ref in ((accx_ref, dwx_ref), (acca_ref, dwa_ref)):
                diag = jnp.zeros((LRU_BW, CW), F32)
                for j in range(CW // LRU_BW):
                    diag = jnp.where(lane_block == j, acc_ref[j * LRU_BW:(j + 1) * LRU_BW, :], diag)
                for q in range(CW // LANES):
                    out_ref[0, q] = diag[:, q * LANES:(q + 1) * LANES]

    row_of = lambda b, t: b * nt + (nt - 1 - t)
    tile = lambda off: pl.BlockSpec((tc, CW), lambda c, b, t: (row_of(b, t), off + c))
    halo = pl.BlockSpec((8, CW), lambda c, b, t: (jnp.maximum(row_of(b, t) * h8 - 1, 0), c))
    vec = pl.BlockSpec((1, CW), lambda c, b, t: (0, c))
    mat = pl.BlockSpec((1, CW, CW), lambda c, b, t: (c, 0, 0))
    cwspec = pl.BlockSpec((CONV, CW), lambda c, b, t: (0, c))
    diag = pl.BlockSpec((1, CW // LANES, LRU_BW, LANES), lambda c, b, t: (c, 0, 0, 0))
    return pl.pallas_call(
        _after(body, 13, deps),
        name="lru_bwd",
        grid=(N_CT, B, nt),
        in_specs=[tile(0), tile(0), halo, tile(N_CT), tile(0), halo, cwspec, vec, mat, mat, vec, vec, vec]
        + [ANY_SPEC] * len(deps),
        out_specs=[tile(0), tile(0), cwspec, vec, diag, diag, vec, vec, vec],
        out_shape=[
            jax.ShapeDtypeStruct((T, D_MODEL), _MXU),
            jax.ShapeDtypeStruct((T, D_MODEL), _MXU),
            jax.ShapeDtypeStruct((CONV, D_MODEL), F32),
            jax.ShapeDtypeStruct((1, D_MODEL), F32),
            jax.ShapeDtypeStruct((N_CT, CW // LANES, LRU_BW, LANES), F32),
            jax.ShapeDtypeStruct((N_CT, CW // LANES, LRU_BW, LANES), F32),
            jax.ShapeDtypeStruct((1, D_MODEL), F32),
            jax.ShapeDtypeStruct((1, D_MODEL), F32),
            jax.ShapeDtypeStruct((1, D_MODEL), F32),
        ],
        scratch_shapes=[pltpu.VMEM((tc + 8, CW), F32), pltpu.VMEM((tc + 8, CW), F32),
                        pltpu.VMEM((8, CW), F32), pltpu.VMEM((8, CW), F32)] + _scan_scratch(tc)
        + [pltpu.VMEM((CW, CW), F32), pltpu.VMEM((CW, CW), F32)],
        compiler_params=_params(("parallel", "arbitrary", "arbitrary")),
    )(dya, proj, proj, proj, hlru, hlru, conv_w, conv_b, wx_bd, wa_bd, bx, ba, lam, *deps)


def _retention_tables(S):
    half = DK // 2
    freqs = ROPE_THETA ** (-jnp.arange(half, dtype=F32) / half)
    ang = jnp.arange(S, dtype=F32)[:, None] * freqs[None, :]
    log_g = jnp.log1p(-(2.0 ** (-5.0 - jnp.arange(HEADS, dtype=F32))))
    idx = jnp.arange(CHUNK, dtype=F32)
    diff = idx[:, None] - idx[None, :]
    inner = jnp.where(diff >= 0, jnp.exp(jnp.maximum(diff, 0.0)[None] * log_g[:, None, None]), 0.0)
    cross = jnp.exp((idx[None, :] + 1.0) * log_g[:, None])[:, :, None]
    state = jnp.exp((CHUNK - 1.0 - idx[None, :]) * log_g[:, None])[:, :, None]
    gam = jnp.broadcast_to(jnp.exp(CHUNK * log_g)[:, None, None], (HEADS, 1, DK))
    return jnp.cos(ang), jnp.sin(ang), inner, cross, state, gam


def _rot(x, cos, sin):
    half = DK // 2
    x1, x2 = x[:, :half], x[:, half:]
    return jnp.concatenate([x1 * cos - x2 * sin, x1 * sin + x2 * cos], axis=-1)


def _rot_t(y, cos, sin):
    half = DK // 2
    y1, y2 = y[:, :half], y[:, half:]
    return jnp.concatenate([y1 * cos + y2 * sin, y2 * cos - y1 * sin], axis=-1)


def _groupnorm(o):
    mu = jnp.mean(o, axis=-1, keepdims=True)
    oc = o - mu
    rs = lax.rsqrt(jnp.mean(oc * oc, axis=-1, keepdims=True) + EPS)
    return oc * rs, rs


def _ret_specs(B, chunk_of):
    rows = RET_CHUNKS * CHUNK
    qkv = lambda g: pl.BlockSpec((B, rows, D_MODEL), lambda c: (0, chunk_of(c), g))
    act = pl.BlockSpec((B, rows, D_MODEL), lambda c: (0, chunk_of(c), 0))
    rope = pl.BlockSpec((rows, DK // 2), lambda c: (chunk_of(c), 0))
    dmat = pl.BlockSpec((HEADS, CHUNK, CHUNK), lambda c: (0, 0, 0))
    dvec = pl.BlockSpec((HEADS, CHUNK, 1), lambda c: (0, 0, 0))
    hrow = pl.BlockSpec((HEADS, 1, DK), lambda c: (0, 0, 0))
    rst = pl.BlockSpec((RET_CHUNKS, B, HEADS, DK, DK), lambda c: (chunk_of(c), 0, 0, 0, 0))
    return qkv, act, rope, dmat, dvec, hrow, rst


def _ret_fwd(proj, tables, gain3, B, S):
    T = B * S
    nc = S // CHUNK
    cos, sin, dmat_t, cd_t, sd_t, gam_t = tables

    def body(q_ref, k_ref, v_ref, gb_ref, cos_ref, sin_ref, dm_ref, cd_ref, sd_ref, gam_ref, gain_ref,
             o_ref, yb_ref, rs_ref, state_ref):
        @pl.when(pl.program_id(0) == 0)
        def _():
            state_ref[...] = jnp.zeros_like(state_ref)

        for cc, b, h in [(cc, b, h) for cc in range(RET_CHUNKS) for b in range(B) for h in range(HEADS)]:
            rows = slice(cc * CHUNK, (cc + 1) * CHUNK)
            cos_t, sin_t = cos_ref[rows, :], sin_ref[rows, :]
            cols = slice(h * DK, (h + 1) * DK)
            qb = _c(_rot(q_ref[b, rows, cols], cos_t, sin_t))
            kb = _c(_rot(k_ref[b, rows, cols], cos_t, sin_t) * (DK ** -0.5))
            v = v_ref[b, rows, cols]
            state = state_ref[b, h]
            sb = _c(state)
            rs_ref[cc, b, h] = sb
            scores = _dot_nt(qb, kb) * dm_ref[h]
            o = _dot(_c(scores), _c(v)) + _dot(qb, sb) * cd_ref[h]
            state_ref[b, h] = gam_ref[h] * state + _dot_tn(kb, _c(v * sd_ref[h]))
            o_ref[b, rows, cols] = o
            n, _ = _groupnorm(o)
            gb = gb_ref[b, rows, cols]
            yb_ref[b, rows, cols] = (gb * _sigmoid(gb) * (n * gain_ref[h])).astype(yb_ref.dtype)

    qkv, act, rope, dmat, dvec, hrow, rst = _ret_specs(B, lambda c: c)
    proj3 = proj.reshape(B, S, proj.shape[1])
    o_pre, yb, states = pl.pallas_call(
        body,
        name="ret_fwd",
        grid=(nc // RET_CHUNKS,),
        in_specs=[qkv(2), qkv(3), qkv(4), qkv(5), rope, rope, dmat, dvec, dvec, hrow, hrow],
        out_specs=[act, act, rst],
        out_shape=[
            jax.ShapeDtypeStruct((B, S, D_MODEL), F32),
            jax.ShapeDtypeStruct((B, S, D_MODEL), _MXU),
            jax.ShapeDtypeStruct((nc, B, HEADS, DK, DK), _MXU),
        ],
        scratch_shapes=[pltpu.VMEM((B, HEADS, DK, DK), F32)],
        compiler_params=_params(("arbitrary",)),
    )(proj3, proj3, proj3, proj3, cos, sin, dmat_t, cd_t, sd_t, gam_t, gain3)
    return o_pre.reshape(T, D_MODEL), yb.reshape(T, D_MODEL), states


def _ret_bwd(dyb, o_pre, proj, states, tables, gain3, B, S, deps=()):
    T = B * S
    nc = S // CHUNK
    cos, sin, dmat_t, cd_t, sd_t, gam_t = tables

    def body(dyb_ref, o_ref, q_ref, k_ref, v_ref, gb_ref, rs_ref, cos_ref, sin_ref, dm_ref, cd_ref, sd_ref, gam_ref,
             gain_ref, dr_ref, dgain_ref, dstate_ref):
        @pl.when(pl.program_id(0) == 0)
        def _():
            dstate_ref[...] = jnp.zeros_like(dstate_ref)
            dgain_ref[...] = jnp.zeros_like(dgain_ref)

        for cc, b, h in [(cc, b, h) for cc in reversed(range(RET_CHUNKS)) for b in range(B) for h in range(HEADS)]:
            rows = slice(cc * CHUNK, (cc + 1) * CHUNK)
            cos_t, sin_t = cos_ref[rows, :], sin_ref[rows, :]
            cols = slice(h * DK, (h + 1) * DK)
            gain = gain_ref[h]
            n, rs = _groupnorm(o_ref[b, rows, cols])
            gb = gb_ref[b, rows, cols]
            sg = _sigmoid(gb)
            dy = dyb_ref[b, rows, cols]
            part = lambda g: slice(g * D_MODEL + h * DK, g * D_MODEL + (h + 1) * DK)
            dr_ref[b, rows, part(3)] = (dy * (n * gain) * (sg * (1.0 + gb * (1.0 - sg)))).astype(dr_ref.dtype)
            dgn = dy * (gb * sg)
            dgain_ref[h] += jnp.sum(dgn * n, axis=0, keepdims=True)
            dn = dgn * gain
            do = rs * (dn - jnp.mean(dn, axis=-1, keepdims=True) - n * jnp.mean(dn * n, axis=-1, keepdims=True))

            qb = _c(_rot(q_ref[b, rows, cols], cos_t, sin_t))
            kb = _c(_rot(k_ref[b, rows, cols], cos_t, sin_t) * (DK ** -0.5))
            v = v_ref[b, rows, cols]
            vb = _c(v)
            vsb = _c(v * sd_ref[h])
            dob = _c(do)
            docb = _c(do * cd_ref[h])
            dmat = dm_ref[h]
            dstate = dstate_ref[b, h]
            dsb = _c(dstate)
            pb = _c(_dot_nt(qb, kb) * dmat)
            dsc = _c(_dot_nt(dob, vb) * dmat)
            dq = _dot(dsc, kb) + _dot_nt(docb, rs_ref[cc, b, h])
            dk = _dot_tn(dsc, qb) + _dot_nt(vsb, dsb)
            dv = _dot_tn(pb, dob) + _dot(kb, dsb) * sd_ref[h]
            dstate_ref[b, h] = gam_ref[h] * dstate + _dot_tn(qb, docb)
            dr_ref[b, rows, part(0)] = _rot_t(dq, cos_t, sin_t).astype(dr_ref.dtype)
            dr_ref[b, rows, part(1)] = (_rot_t(dk, cos_t, sin_t) * (DK ** -0.5)).astype(dr_ref.dtype)
            dr_ref[b, rows, part(2)] = dv.astype(dr_ref.dtype)

    n_steps = nc // RET_CHUNKS
    qkv, act, rope, dmat, dvec, hrow, rst = _ret_specs(B, lambda c: n_steps - 1 - c)
    wide = pl.BlockSpec((B, RET_CHUNKS * CHUNK, 4 * D_MODEL), lambda c: (0, n_steps - 1 - c, 0))
    proj3 = proj.reshape(B, S, proj.shape[1])
    dr, dgain = pl.pallas_call(
        _after(body, 14, deps),
        name="ret_bwd",
        grid=(n_steps,),
        in_specs=[act, act, qkv(2), qkv(3), qkv(4), qkv(5), rst, rope, rope, dmat, dvec, dvec, hrow, hrow]
        + [ANY_SPEC] * len(deps),
        out_specs=[wide, hrow],
        out_shape=[jax.ShapeDtypeStruct((B, S, 4 * D_MODEL), _MXU), jax.ShapeDtypeStruct((HEADS, 1, DK), F32)],
        scratch_shapes=[pltpu.VMEM((B, HEADS, DK, DK), F32)],
        compiler_params=_params(("arbitrary",)),
    )(dyb.reshape(B, S, D_MODEL), o_pre.reshape(B, S, D_MODEL), proj3, proj3, proj3, proj3, states, cos, sin, dmat_t,
      cd_t, sd_t, gam_t, gain3, *deps)
    return dr.reshape(T, 4 * D_MODEL), dgain


def _mid(ya, yb, proj, x2d, tgt2d, wpa, wpb, wout, g_fin):
    T = x2d.shape[0]
    tm = min(MID_TILE, T)
    n_steps = T // tm
    rows = D_MODEL // (2 * N_CHIPS)

    def body(ya_ref, yb_ref, ma_ref, mb_ref, x_ref, t_ref, gf_ref, wpa_hbm, wpb_hbm, wout_hbm,
             loss_ref, dx2_ref, dya_ref, dyb_ref, dm_ref, dgf_ref, gw_hbm, w_ref, acc_ref, sem):
        i = pl.program_id(0)

        @pl.when(i == 0)
        def _():
            loads = [pltpu.make_async_copy(src, w_ref.at[k], sem.at[k]) for k, src in enumerate((wpa_hbm, wpb_hbm, wout_hbm))]
            for cp in loads:
                cp.start()
            for cp in loads:
                cp.wait()
            acc_ref[...] = jnp.zeros_like(acc_ref)
            loss_ref[...] = jnp.zeros_like(loss_ref)
            dgf_ref[...] = jnp.zeros_like(dgf_ref)

        ya_t, yb_t = ya_ref[...], yb_ref[...]
        out_a = _dot(ya_t, w_ref[0])
        out_b = _dot(yb_t, w_ref[1])
        sa = _sigmoid(ma_ref[...])
        sb = _sigmoid(mb_ref[...])
        mgb = _c(sa * out_a + sb * out_b)
        x2 = x_ref[...] + _dot(mgb, w_ref[2])
        r2 = lax.rsqrt(jnp.mean(x2 * x2, axis=-1, keepdims=True) + EPS)
        nx = x2 * r2
        gf = gf_ref[...]
        err = nx * gf - t_ref[...]
        loss_ref[...] += 0.5 * jnp.sum(jnp.mean(err * err, axis=-1, keepdims=True), axis=0, keepdims=True)
        dy = err * (1.0 / D_MODEL)
        dgf_ref[...] += jnp.sum(dy * nx, axis=0, keepdims=True)
        dyg = dy * gf
        dx2 = r2 * (dyg - nx * jnp.mean(dyg * nx, axis=-1, keepdims=True))
        dx2_ref[...] = dx2
        dx2b = _c(dx2)
        dmg = _dot_nt(dx2b, w_ref[2])
        acc_ref[2] += _dot_tn(mgb, dx2b)
        dm_ref[:, :D_MODEL] = (dmg * out_a * sa * (1.0 - sa)).astype(dm_ref.dtype)
        dm_ref[:, D_MODEL:] = (dmg * out_b * sb * (1.0 - sb)).astype(dm_ref.dtype)
        dab = _c(dmg * sa)
        dbb = _c(dmg * sb)
        dya_ref[...] = _dot_nt(dab, w_ref[0])
        dyb_ref[...] = _dot_nt(dbb, w_ref[1])
        acc_ref[0] += _dot_tn(ya_t, dab)
        acc_ref[1] += _dot_tn(yb_t, dbb)

        @pl.when(i == n_steps - 1)
        def _():
            copies = [pltpu.make_async_copy(acc_ref.at[k, pl.ds((2 * p + hf) * rows, rows), :], gw_hbm.at[p, hf, k],
                                            sem.at[(k * N_CHIPS + p) * 2 + hf])
                      for k in range(3) for p in range(N_CHIPS) for hf in range(2)]
            for cp in copies:
                cp.start()
            for cp in copies:
                cp.wait()

    tile = lambda j: pl.BlockSpec((tm, D_MODEL), lambda i: (i, j))
    one = pl.BlockSpec((1, D_MODEL), lambda i: (0, 0))
    anyspec = pl.BlockSpec(memory_space=pl.ANY)
    return pl.pallas_call(
        body,
        name="mid",
        grid=(n_steps,),
        in_specs=[tile(0), tile(0), tile(6), tile(7), tile(0), tile(0), one, anyspec, anyspec, anyspec],
        out_specs=[pl.BlockSpec((1, 1), lambda i: (0, 0)), tile(0), tile(0), tile(0),
                   pl.BlockSpec((tm, 2 * D_MODEL), lambda i: (i, 0)), one, anyspec],
        out_shape=[
            jax.ShapeDtypeStruct((1, 1), F32),
            jax.ShapeDtypeStruct((T, D_MODEL), F32),
            jax.ShapeDtypeStruct((T, D_MODEL), F32),
            jax.ShapeDtypeStruct((T, D_MODEL), F32),
            jax.ShapeDtypeStruct((T, 2 * D_MODEL), _MXU),
            jax.ShapeDtypeStruct((1, D_MODEL), F32),
            jax.ShapeDtypeStruct((N_CHIPS, 2, 3, rows, D_MODEL), F32),
        ],
        scratch_shapes=[pltpu.VMEM((3, D_MODEL, D_MODEL), _MXU), pltpu.VMEM((3, D_MODEL, D_MODEL), F32),
                        pltpu.SemaphoreType.DMA((3 * N_CHIPS * 2,))],
        compiler_params=_params(("arbitrary",)),
    )(ya, yb, proj, proj, x2d, tgt2d, g_fin, wpa, wpb, wout)


def _inproj_bwd_dx(dparts, w_all, x2d, dx2, g_in, first, count, prev, name, deps=()):
    T = x2d.shape[0]
    tm = min(DX_TILE, T)
    n_d = len(dparts)
    groups = [(a, k) for a, d in enumerate(dparts) for k in range(d.shape[1] // D_MODEL)]
    dg_start = jnp.zeros((1, D_MODEL), F32) if prev is None else prev[1]
    carried = () if prev is None else (prev[0],)

    def body(*refs):
        d_refs = refs[:n_d]
        x_ref, dx2_ref, g_ref, dg0_ref, w_hbm = refs[n_d:n_d + 5]
        dx_ref, dg_ref, w_ref, sem = refs[-4:]

        def load(j):
            part = (j // 2, slice(None), pl.ds((j % 2) * D_MODEL, D_MODEL))
            return pltpu.make_async_copy(w_hbm.at[part], w_ref.at[part], sem.at[j])

        def tile(before_group):
            dh = jnp.zeros((tm, D_MODEL), F32)
            for j, (a, k) in enumerate(groups):
                before_group(j)
                dh = dh + _dot_nt(d_refs[a][:, k * D_MODEL:(k + 1) * D_MODEL],
                                  w_ref[j // 2, :, (j % 2) * D_MODEL:(j % 2 + 1) * D_MODEL])
            x = x_ref[...]
            r = lax.rsqrt(jnp.mean(x * x, axis=-1, keepdims=True) + EPS)
            nx = x * r
            dg_ref[...] += jnp.sum(dh * nx, axis=0, keepdims=True)
            dhg = dh * g_ref[...]
            dx_ref[...] = dx2_ref[...] + r * (dhg - nx * jnp.mean(dhg * nx, axis=-1, keepdims=True))

        first = pl.program_id(0) == 0

        @pl.when(first)
        def _():
            for j in range(len(groups)):
                load(j).start()
            dg_ref[...] = dg0_ref[...]
            tile(lambda j: load(j).wait())

        @pl.when(jnp.logical_not(first))
        def _():
            tile(lambda j: None)

    tile = pl.BlockSpec((tm, D_MODEL), lambda i: (first + i, 0))
    one = pl.BlockSpec((1, D_MODEL), lambda i: (0, 0))
    return pl.pallas_call(
        body,
        name=name,
        grid=(count,),
        in_specs=[pl.BlockSpec((tm, d.shape[1]), lambda i: (first + i, 0)) for d in dparts]
        + [tile, tile, one, one, ANY_SPEC] + [ANY_SPEC] * (len(carried) + len(deps)),
        out_specs=[tile, one],
        out_shape=[jax.ShapeDtypeStruct((T, D_MODEL), F32), jax.ShapeDtypeStruct((1, D_MODEL), F32)],
        input_output_aliases={n_d + 5: 0} if carried else {},
        scratch_shapes=[pltpu.VMEM(w_all.shape, w_all.dtype), pltpu.SemaphoreType.DMA((len(groups),))],
        compiler_params=_params(("arbitrary",)),
    )(*dparts, x2d, dx2, g_in, dg_start, w_all, *carried, *deps)


def _inproj_bwd_dw(ht, dparts, name, deps=()):
    T = ht.shape[1]
    tn = DW_COLS
    half = D_MODEL // 2
    per_chip = 2 * D_MODEL // tn
    n_d = len(dparts)
    tiles = [(a, t) for a, d in enumerate(dparts) for t in range(d.shape[1] // tn)]
    offs = [sum(d.shape[1] // tn for d in dparts[:a]) for a in range(n_d)]

    def body(*refs):
        ht_hbm = refs[0]
        d_refs = refs[1:1 + n_d]
        out_ref, ht_ref, sem = refs[-3:]
        t = pl.program_id(0)

        def load(k):
            cols = pl.ds(k * (T // DW_LOADS), T // DW_LOADS)
            return pltpu.make_async_copy(ht_hbm.at[:, cols], ht_ref.at[:, cols], sem.at[k])

        def store(g):
            out_ref[0, 0] = g[:half]
            out_ref[0, 1] = g[half:]

        @pl.when(t == 0)
        def _():
            for k in range(DW_LOADS):
                load(k).start()
            g = jnp.zeros((D_MODEL, tn), F32)
            for k in range(DW_LOADS):
                load(k).wait()
                tokens = slice(k * (T // DW_LOADS), (k + 1) * (T // DW_LOADS))
                g = g + _dot(ht_ref[:, tokens], d_refs[0][tokens, :])
            store(g)

        for a in range(n_d):
            lo, hi = max(offs[a], 1), offs[a] + dparts[a].shape[1] // tn

            @pl.when((t >= lo) & (t < hi))
            def _(a=a):
                store(_dot(ht_ref[...], d_refs[a][...]))

    def dspec(a):
        n_a = dparts[a].shape[1] // tn
        return pl.BlockSpec((T, tn), lambda t: (0, jnp.clip(t - offs[a], 0, n_a - 1)))

    return pl.pallas_call(
        body,
        name=name,
        grid=(len(tiles),),
        in_specs=[ANY_SPEC] + [dspec(a) for a in range(n_d)] + [ANY_SPEC] * len(deps),
        out_specs=pl.BlockSpec((1, 2, half, tn), lambda t: (t // per_chip, 0, 0, t % per_chip)),
        out_shape=jax.ShapeDtypeStruct((len(tiles) // per_chip, 2, half, 2 * D_MODEL), F32),
        scratch_shapes=[pltpu.VMEM(ht.shape, ht.dtype), pltpu.SemaphoreType.DMA((DW_LOADS,))],
        compiler_params=_params(("arbitrary",)),
    )(ht, *dparts, *deps)


def _coords():
    return lax.axis_index("x"), lax.axis_index("y"), lax.axis_index("c")


def _other_chips(x, y):
    return [(1 - x, y), (x, 1 - y), (1 - x, 1 - y)]


def _chunks(rows, n):
    size = rows // n
    return [pl.ds(q * size, size) for q in range(n)]


HBM_SPEC = pl.BlockSpec(memory_space=pltpu.HBM)
SEM_SPEC = pl.BlockSpec(memory_space=pltpu.SEMAPHORE)
DATAFLOW = pltpu.SideEffectType.DATAFLOW_SIDE_EFFECTING


def _copies_start(bufs, plan, n_copies, name, deps=()):
    n = len(bufs)
    n_deps = len(deps)

    def body(*refs):
        ins = refs[:n]
        send_sems, recv_sems = refs[n + n_deps], refs[n + n_deps + 1]
        token = refs[-1]
        for k, send, _ in plan(ins):
            if send is not None:
                src, dst, dev, pred = send
                cp = pltpu.make_async_remote_copy(src_ref=src, dst_ref=dst, send_sem=send_sems.at[k],
                                                  recv_sem=recv_sems.at[k], device_id=dev, device_id_type=MESH)
                if pred is None:
                    cp.start()
                else:
                    pl.when(pred)(cp.start)
        token[...] = jnp.zeros_like(token)

    hbm = [pltpu.with_memory_space_constraint(b, pltpu.HBM) for b in bufs]
    outs = pl.pallas_call(
        body,
        name=name,
        in_specs=[HBM_SPEC] * n + [ANY_SPEC] * n_deps,
        out_specs=(SEM_SPEC, SEM_SPEC, *([HBM_SPEC] * n), pl.BlockSpec(memory_space=pltpu.VMEM)),
        out_shape=(pltpu.SemaphoreType.DMA((n_copies,)), pltpu.SemaphoreType.DMA((n_copies,)),
                   *[pltpu.HBM(b.shape, b.dtype) for b in bufs], jax.ShapeDtypeStruct((8, 128), F32)),
        input_output_aliases={a: 2 + a for a in range(n)},
        compiler_params=pltpu.CompilerParams(has_side_effects=DATAFLOW),
    )(*hbm, *deps)
    return outs[0], outs[1], list(outs[2:2 + n]), outs[-1]


def _copies_wait(send_sems, recv_sems, bufs, after, plan, name, only=None):
    n = len(bufs)

    def body(*refs):
        ins = refs[:n]
        s_sems, r_sems = refs[n], refs[n + 1]
        for k, send, recv in plan(ins):
            if only is not None and k not in only:
                continue
            if send is not None:
                src, dst, dev, pred = send
                cp = pltpu.make_async_remote_copy(src_ref=src, dst_ref=dst, send_sem=s_sems.at[k],
                                                  recv_sem=r_sems.at[k], device_id=dev, device_id_type=MESH)
                if pred is None:
                    cp.wait_send()
                else:
                    pl.when(pred)(cp.wait_send)
            if recv is not None:
                dst, pred = recv
                cp = pltpu.make_async_remote_copy(src_ref=dst, dst_ref=dst, send_sem=s_sems.at[k],
                                                  recv_sem=r_sems.at[k], device_id=_coords(), device_id_type=MESH)
                if pred is None:
                    cp.wait_recv()
                else:
                    pl.when(pred)(cp.wait_recv)

    outs = pl.pallas_call(
        body,
        name=name,
        in_specs=[HBM_SPEC] * n + [SEM_SPEC, SEM_SPEC, pl.BlockSpec(memory_space=pl.ANY)],
        out_specs=[HBM_SPEC] * n,
        out_shape=[pltpu.HBM(b.shape, b.dtype) for b in bufs],
        input_output_aliases={a: a for a in range(n)},
        compiler_params=pltpu.CompilerParams(has_side_effects=DATAFLOW),
    )(*bufs, send_sems, recv_sems, after)
    return list(outs)


def _gather_plan(n_bufs):
    def plan(refs):
        x, y, c = _coords()
        me = 2 * x + y
        out = []
        for k, (px, py) in enumerate(_other_chips(x, y)):
            for a in range(n_bufs):
                out.append((k * n_bufs + a, (refs[a].at[me], refs[a].at[me], (px, py, c), None),
                            (refs[a].at[2 * px + py], None)))
        return out
    return plan


def _cast_into_slot(ws, name, deps=()):
    n = len(ws)
    nt = 2

    def body(s_ref, *refs):
        outs = refs[len(refs) - n:]
        for a in range(n):
            outs[a][0] = refs[a][...].astype(outs[a].dtype)

    xi, yi, _ = _coords()
    return pl.pallas_call(
        body,
        name=name,
        grid_spec=pltpu.PrefetchScalarGridSpec(
            num_scalar_prefetch=1,
            grid=(2, nt),
            in_specs=[pl.BlockSpec((1, w.shape[1] // nt, w.shape[2]), lambda hf, i, s: (hf, i, 0)) for w in ws]
            + [ANY_SPEC] * len(deps),
            out_specs=[pl.BlockSpec((1, 1, w.shape[1] // nt, w.shape[2]), lambda hf, i, s: (s[0], hf, i, 0)) for w in ws],
        ),
        out_shape=[jax.ShapeDtypeStruct((N_CHIPS,) + w.shape, _MXU) for w in ws],
        compiler_params=_params(("parallel", "parallel")),
    )((2 * xi + yi).reshape(1).astype(jnp.int32), *ws, *deps)


def _chip_gather_plan(stage, n_bufs):
    def plan(refs):
        x, y, c = _coords()
        me = 2 * x + y
        near = [(1 - x, y), (x, 1 - y)]
        slots = [2 * (1 - x) + y, 2 * x + (1 - y), 2 * (1 - x) + (1 - y)]
        sibling = (x, y, 1 - c)
        pass_to = (jnp.where(c == 0, x, 1 - x), jnp.where(c == 0, 1 - y, y), c)
        pass_slot = jnp.where(c == 0, slots[0], slots[1])
        out = []

        def move(src_slot, to, land_slot, land_core, pieces):
            for a, buf in enumerate(refs):
                for rows in _chunks(buf.shape[2], pieces[a]):
                    out.append((len(out), (buf.at[src_slot, c, rows], buf.at[src_slot, c, rows], to, None),
                                (buf.at[land_slot, land_core, rows], None)))

        if stage == "near":
            for k, chip in enumerate(near):
                move(me, (*chip, c), slots[k], c, NEAR_PIECES[:n_bufs])
        elif stage == "pass":
            move(pass_slot, pass_to, slots[2], c, PASS_PIECES[:n_bufs])
            for k in range(2):
                move(slots[k], sibling, slots[k], 1 - c, [1] * n_bufs)
        else:
            move(slots[2], sibling, slots[2], 1 - c, [1] * n_bufs)
        return out
    return plan


NEAR_PIECES = (2, 1)
PASS_PIECES = (2, 1)


def _chip_gather_copies(stage, n_bufs):
    if stage == "near":
        return 2 * sum(NEAR_PIECES[:n_bufs]), None
    if stage == "pass":
        n_pass = sum(PASS_PIECES[:n_bufs])
        return n_pass + 2 * n_bufs, set(range(n_pass))
    return n_bufs, None


def _swap_plan(n_slabs):
    def plan(refs):
        x, y, c = _coords()
        out, k = [], 0
        for i, n in enumerate(n_slabs):
            g, land = refs[2 * i], refs[2 * i + 1]
            for p in range(n):
                out.append((k, (g.at[p, 1 - c], land.at[p], (x, y, 1 - c), None), (land.at[p], None)))
                k += 1
        return out
    return plan


def _is_one_of(chip, dests):
    hit = chip == dests[0]
    for d in dests[1:]:
        hit = hit | (chip == d)
    return hit


def _slab_of(chip, dests):
    return sum(j * (chip == d).astype(jnp.int32) for j, d in enumerate(dests))


def _scatter_plan(dest_sets):
    def plan(refs):
        x, y, c = _coords()
        me = 2 * x + y
        out = []
        for k, (px, py) in enumerate(_other_chips(x, y)):
            peer = 2 * px + py
            for i, dests in enumerate(dest_sets):
                cs, land = refs[2 * i], refs[2 * i + 1]
                everyone = len(dests) == N_CHIPS
                send = (cs.at[_slab_of(peer, dests)], land.at[k], (px, py, c),
                        None if everyone else _is_one_of(peer, dests))
                recv = (land.at[k], None if everyone else _is_one_of(me, dests))
                out.append((k * len(dest_sets) + i, send, recv))
        return out
    return plan


def _join_plan(rows, n_pieces):
    def plan(refs):
        x, y, c = _coords()
        (buf,) = refs
        return [(i, (buf.at[c, piece], buf.at[c, piece], (x, y, 1 - c), None), (buf.at[1 - c, piece], None))
                for i, piece in enumerate(_chunks(rows, n_pieces))]
    return plan


def _join_plans(parts):
    def plan(refs):
        out, b0, k0 = [], 0, 0
        for part_plan, n_bufs, n_copies in parts:
            out += [(k0 + k, send, recv) for k, send, recv in part_plan(refs[b0:b0 + n_bufs])]
            b0 += n_bufs
            k0 += n_copies
        return out
    return plan


def _allgather_plan():
    def plan(refs):
        x, y, c = _coords()
        (land,) = refs
        me = 4 * x + 2 * y + c
        out = []
        for r in range(1, 8):
            px = 1 - x if r & 4 else x
            py = 1 - y if r & 2 else y
            pc = 1 - c if r & 1 else c
            out.append((r - 1, (land.at[me], land.at[me], (px, py, pc), None), (land.at[4 * px + 2 * py + pc], None)))
        return out
    return plan


def _sum_gathered(land, name):
    def body(land_ref, o_ref):
        acc = land_ref[0]
        for d in range(1, 8):
            acc = acc + land_ref[d]
        o_ref[...] = acc

    return pl.pallas_call(
        body,
        name=name,
        out_shape=jax.ShapeDtypeStruct(land.shape[1:], F32),
        compiler_params=_params(),
    )(land)


def _row_tile(rows, cap):
    t = cap
    while rows % t:
        t //= 2
    return t


def _add_my_half(g, r, name):
    n_slabs, _, R, C = g.shape
    tr = R if n_slabs > 1 else _row_tile(R, SUM_ROWS)

    def body(c_ref, g_ref, r_ref, o_ref):
        o_ref[...] = (g_ref[0] + r_ref[...]).astype(o_ref.dtype)

    return pl.pallas_call(
        body,
        name=name,
        grid_spec=pltpu.PrefetchScalarGridSpec(
            num_scalar_prefetch=1,
            grid=(n_slabs, R // tr),
            in_specs=[pl.BlockSpec((1, 1, tr, C), lambda p, i, c_ref: (p, c_ref[0], i, 0)),
                      pl.BlockSpec((1, tr, C), lambda p, i, c_ref: (p, i, 0))],
            out_specs=pl.BlockSpec((1, tr, C), lambda p, i, c_ref: (p, i, 0)),
        ),
        out_shape=jax.ShapeDtypeStruct(r.shape, jnp.bfloat16),
        compiler_params=_params(("parallel", "parallel")),
    )(lax.axis_index("c").reshape(1).astype(jnp.int32), g, r)


def _sum_slabs(own, got, name, deps=()):
    _, R, C = own.shape
    tr = _row_tile(R, SUM_ROWS)

    def body(s_ref, own_ref, got_ref, *rest):
        rest[-1][0] = ((own_ref[0].astype(F32) + got_ref[0].astype(F32)) + got_ref[1].astype(F32)) + got_ref[2].astype(F32)

    xi, yi, ci = _coords()
    return pl.pallas_call(
        body,
        name=name,
        grid_spec=pltpu.PrefetchScalarGridSpec(
            num_scalar_prefetch=1,
            grid=(R // tr,),
            in_specs=[pl.BlockSpec((1, tr, C), lambda i, s: (s[0], i, 0)),
                      pl.BlockSpec((3, tr, C), lambda i, s: (0, i, 0))] + [ANY_SPEC] * len(deps),
            out_specs=pl.BlockSpec((1, tr, C), lambda i, s: (s[1], i, 0)),
        ),
        out_shape=jax.ShapeDtypeStruct((2, R, C), F32),
        compiler_params=_params(("parallel",)),
    )(jnp.stack([2 * xi + yi, ci]).astype(jnp.int32), own, got, *deps)


def _sum_parts(owns, got, dest_sets, name):
    n = len(owns)
    _, R, C = owns[0].shape
    tr = _row_tile(R, SUM_ROWS)

    def body(s_ref, *refs):
        got_ref, o_ref = refs[n], refs[-1]
        total = jnp.zeros((tr, C), F32)
        for i in range(n):
            total = total + jnp.where(s_ref[2 + 2 * i] == 1, refs[i][0].astype(F32), 0.0)
        o_ref[0] = ((total + got_ref[0].astype(F32)) + got_ref[1].astype(F32)) + got_ref[2].astype(F32)

    xi, yi, ci = _coords()
    me = 2 * xi + yi
    scalars = [ci, ci]
    for dests in dest_sets:
        scalars += [_is_one_of(me, dests).astype(jnp.int32), _slab_of(me, dests)]
    own_spec = lambda i: pl.BlockSpec((1, tr, C), lambda r, s: (s[3 + 2 * i], r, 0))
    return pl.pallas_call(
        body,
        name=name,
        grid_spec=pltpu.PrefetchScalarGridSpec(
            num_scalar_prefetch=1,
            grid=(R // tr,),
            in_specs=[own_spec(i) for i in range(n)] + [pl.BlockSpec((3, tr, C), lambda r, s: (0, r, 0))],
            out_specs=pl.BlockSpec((1, tr, C), lambda r, s: (s[0], r, 0)),
        ),
        out_shape=jax.ShapeDtypeStruct((2, R, C), F32),
        compiler_params=_params(("parallel",)),
    )(jnp.stack(scalars).astype(jnp.int32), *owns, got)


def _adamw_math(w, g, m, v):
    m = ADAM_B1 * m + (1.0 - ADAM_B1) * g
    v = ADAM_B2 * v + (1.0 - ADAM_B2) * (g * g)
    m_hat = m / (1.0 - ADAM_B1 ** ADAM_STEP)
    v_hat = v / (1.0 - ADAM_B2 ** ADAM_STEP)
    delta = -ADAM_LR * (m_hat / (jnp.sqrt(v_hat) + ADAM_EPS) + ADAM_WD * w)
    return delta, m, v


def _adamw_halves(ws, g, ms, vs, half, prev, name, deps=()):
    n = len(ws)
    _, _, R, C = g.shape
    tr = _row_tile(R, ADAMW_ROWS)
    steps = R // tr
    carried = [] if prev is None else [a for four in prev for a in four]
    both = half is None
    which = (lambda i, s: i // steps) if both else (lambda i, s: s[0])
    half = 0 if both else half

    def body(s_ref, *refs):
        w_refs, g_refs, m_refs, v_refs = (refs[k * n:(k + 1) * n] for k in range(4))
        outs = refs[len(refs) - 4 * n:]
        for a in range(n):
            grad = g_refs[a][0, 0]
            d, mn, vn = _adamw_math(w_refs[a][...], grad, m_refs[a][...], v_refs[a][...])
            for o, val in zip(outs[4 * a:4 * a + 4], (grad, d, mn, vn)):
                o[...] = val

    rows = pl.BlockSpec((tr, C), lambda i, s: (which(i, s) * steps + i % steps, 0))
    grad_spec = lambda a: pl.BlockSpec((1, 1, tr, C), lambda i, s: (which(i, s), a, i % steps, 0))
    n_in = 4 * n
    outs = pl.pallas_call(
        body,
        name=name,
        grid_spec=pltpu.PrefetchScalarGridSpec(
            num_scalar_prefetch=1,
            grid=(2 * steps if both else steps,),
            in_specs=[rows] * n + [grad_spec(a) for a in range(n)] + [rows] * (2 * n)
            + [ANY_SPEC] * (len(carried) + len(deps)),
            out_specs=[rows] * (4 * n),
        ),
        out_shape=[jax.ShapeDtypeStruct((2 * R, C), F32)] * (4 * n),
        input_output_aliases={1 + n_in + k: k for k in range(len(carried))},
        compiler_params=_params(("parallel",)),
    )(jnp.reshape(half, (1,)).astype(jnp.int32), *ws, *([g] * n), *ms, *vs, *carried, *deps)
    return [outs[4 * a:4 * a + 4] for a in range(n)]


def _adamw_small(ws, gs, ms, vs, name):
    n = len(ws)

    def body(*refs):
        for a in range(n):
            d, mn, vn = _adamw_math(refs[a][...], refs[n + a][...], refs[2 * n + a][...], refs[3 * n + a][...])
            refs[4 * n + a][...] = d
            refs[5 * n + a][...] = mn
            refs[6 * n + a][...] = vn

    shapes = [jax.ShapeDtypeStruct(w.shape, F32) for w in ws]
    outs = pl.pallas_call(
        body,
        name=name,
        out_shape=shapes * 3,
        compiler_params=_params(),
    )(*ws, *gs, *ms, *vs)
    return outs[:n], outs[n:2 * n], outs[2 * n:]


def _to_blockdiag(w):
    per = CW // LRU_BW
    w4 = w.reshape(N_CT, per, LRU_BW, LRU_BW)
    eye = jnp.eye(per, dtype=w.dtype)
    return (w4[:, :, :, None, :] * eye[None, :, None, :, None]).reshape(N_CT, CW, CW)


def _blocks_from_lanes(g):
    side = LANES // LRU_BW
    g5 = g.reshape(N_CT, CW // LANES, LRU_BW, side, LRU_BW)
    return jnp.transpose(g5, (0, 1, 3, 2, 4)).reshape(LRU_BLOCKS, LRU_BW, LRU_BW)


def _local_grads(x2d, tgt2d, B, S, g_in, in_proj, conv_b, gate_x_w, gate_x_b, gate_a_w, gate_a_b, lam,
                 proj_weights, g_fin, reduce):
    wx_bd = _c(_to_blockdiag(gate_x_w))
    wa_bd = _c(_to_blockdiag(gate_a_w))
    tables = _retention_tables(S)

    proj, ht, w_all, conv_w, gain = in_proj(x2d, g_in, (*tables, wx_bd, wa_bd))
    gain3 = gain.reshape(HEADS, 1, DK)
    hlru, ya = _lru_fwd(proj, conv_w, conv_b, wx_bd, wa_bd, gate_x_b, gate_a_b, lam, B, S)
    o_pre, yb, states = _ret_fwd(proj, tables, gain3, B, S)
    wpa, wpb, wout = proj_weights(yb)
    loss, dx2, dya, dyb, dm, dgf, gw_proj = _mid(ya, yb, proj, x2d, tgt2d, wpa, wpb, wout, g_fin)
    g3 = _inproj_bwd_dw(ht, [dm], "inproj_bwd_dw_m")
    deps = reduce.m_ready(gw_proj, g3)
    dr, dgain = _ret_bwd(dyb, o_pre, proj, states, tables, gain3, B, S, deps)
    deps = reduce.ret_done(dr)
    g12 = _inproj_bwd_dw(ht, [dr], "inproj_bwd_dw_r", deps)
    deps = reduce.r_ready(g12)
    dxa, dga, dcw, dcb, dwx, dwa, dbx, dba, dlam = _lru_bwd(
        dya, proj, hlru, conv_w, conv_b, wx_bd, wa_bd, gate_x_b, gate_a_b, lam, B, S, deps)
    small = dict(conv_w=dcw, conv_b=dcb, gate_x_w=dwx, gate_x_b=dbx, gate_a_w=dwa, gate_a_b=dba, lru_lambda=dlam,
                 gn_gain=dgain.reshape(HEADS, DK), norm_final=dgf)
    loss_rows = jnp.broadcast_to(loss, (SUBLANES, LANES))
    deps = reduce.lru_done(dxa, jnp.concatenate([_pack_small(small), loss_rows], axis=0))
    g0 = _inproj_bwd_dw(ht, [dxa, dga], "inproj_bwd_dw_a", deps)
    deps = reduce.a_ready(g0)
    n_tiles = x2d.shape[0] // min(DX_TILE, x2d.shape[0])
    grad_x, dgin = _inproj_bwd_dx([dxa, dga, dr, dm], w_all, x2d, dx2, g_in, 0, n_tiles, None, "inproj_bwd_dx", deps)
    return grad_x, dgin


ALL_CHIPS = (0, 1, 2, 3)


class _GradReduce:
    def __init__(self, proj_done):
        self.pending = {}
        self.proj_done = proj_done
        self.land_in = None

    def _start(self, key, parts, name):
        bufs, plans, shared = [], [], None
        for part_bufs, plan, n_copies, part_shared in parts:
            if part_shared is not None:
                shared = len(bufs) + part_shared
            plans.append((plan, len(part_bufs), n_copies))
            bufs += part_bufs
        plan = _join_plans(plans)
        send_sems, recv_sems, bufs, token = _copies_start(bufs, plan, sum(p[2] for p in plans), name + "_start")
        if shared is not None:
            self.land_in = bufs[shared]
        self.pending[key] = (send_sems, recv_sems, bufs, plan, name + "_wait", shared)
        return (token,)

    def _finish(self, key, after):
        send_sems, recv_sems, bufs, plan, name, shared = self.pending.pop(key)
        if shared is not None:
            bufs[shared] = self.land_in
        bufs = _copies_wait(send_sems, recv_sems, bufs, after, plan, name)
        if shared is not None:
            self.land_in = bufs[shared]
        return bufs

    @staticmethod
    def _swap(pieces):
        bufs = []
        for g in pieces:
            bufs += [g, lax.empty((g.shape[0],) + g.shape[2:], F32)]
        n_slabs = [g.shape[0] for g in pieces]
        return bufs, _swap_plan(n_slabs), sum(n_slabs), None

    def _scatter(self, sums, dest_sets):
        bufs = []
        for cs in sums:
            bufs += [cs, lax.empty((3,) + cs.shape[1:], cs.dtype)]
        if self.land_in is not None:
            bufs[-1] = self.land_in
        return bufs, _scatter_plan(dest_sets), 3 * len(sums), len(bufs) - 1

    @staticmethod
    def _gather8(block):
        x, y, c = _coords()
        land = lax.dynamic_update_slice(lax.empty((8,) + block.shape, F32), block[None], (4 * x + 2 * y + c, 0, 0))
        return [land], _allgather_plan(), 7, None

    def m_ready(self, gw_proj, g3):
        rows = gw_proj.shape[2] * gw_proj.shape[3]
        return self._start("m", [self._swap([gw_proj.reshape(N_CHIPS, 2, rows, D_MODEL), g3])], "swap_m")

    def ret_done(self, after):
        proj, land_p, g3, land_3 = self._finish("m", after)
        sums_m = [_add_my_half(proj, land_p, "chip_sum_proj"), _add_my_half(g3, land_3, "chip_sum_m")]
        return self._start("sm", [self._scatter(sums_m, [ALL_CHIPS, (3,)])], "scatter_m")

    def r_ready(self, g12):
        return self._start("r", [self._swap([g12])], "swap_r")

    def lru_done(self, after, packed):
        g12, land_12 = self._finish("r", after)
        sums_r = [_add_my_half(g12, land_12, "chip_sum_r")]
        return (self._start("sr", [self._scatter(sums_r, [(1, 2)])], "scatter_r")
                + self._start("small", [self._gather8(packed)], "gather_small"))

    def a_ready(self, g0):
        (token,) = self._start("a", [self._swap([g0])], "swap_a")
        csp, gotp, self.cs3, _ = self._finish("sm", token)
        half_proj = _sum_slabs(csp, gotp, "sum_w_proj")
        g0, land_0 = self._finish("a", half_proj)
        join = ([half_proj], _join_plan(half_proj.shape[1], PROJ_JOIN_PIECES), PROJ_JOIN_PIECES, None)
        return self._start("sa", [self._scatter([_add_my_half(g0, land_0, "chip_sum_a")], [(0,)]), join], "scatter_a")

    def finish(self, dgin, w_in_done):
        (token,) = self._start("n", [self._gather8(dgin)], "gather_norm_in")
        (small,) = self._finish("small", token)
        cs12, _ = self._finish("sr", token)
        cs0, _, g_proj = self._finish("sa", token)
        self.proj_done(g_proj)
        half_in =_sum_parts([self.cs3, cs12, cs0], self.land_in, [(3,), (1, 2), (0,)], "sum_w_in")
        deps = self._start("j", [([half_in], _join_plan(half_in.shape[1], JOIN_PIECES), JOIN_PIECES, None)], "join_w_in")
        first = w_in_done(self.pending["j"][2][0], True, None, deps)
        (g_in,) = self._finish("j", first[1])
        done = w_in_done(g_in, False, first, ())
        (norm_in,) = self._finish("n", done[1])
        return _sum_gathered(small, "sum_small_grads"), _sum_gathered(norm_in, "sum_norm_in_grad")


_SMALL = ("gate_x_w", "gate_a_w", "conv_w", "conv_b", "gate_x_b", "gate_a_b", "lru_lambda", "gn_gain", "norm_final")
_SMALL_SHAPES = dict(gate_x_w=(LRU_BLOCKS, LRU_BW, LRU_BW), gate_a_w=(LRU_BLOCKS, LRU_BW, LRU_BW),
                     norm_in=(1, D_MODEL), conv_w=(CONV, D_MODEL), conv_b=(1, D_MODEL), gate_x_b=(1, D_MODEL),
                     gate_a_b=(1, D_MODEL), lru_lambda=(1, D_MODEL), gn_gain=(HEADS, DK), norm_final=(1, D_MODEL))


def _pack_small(small):
    return jnp.concatenate([small[k].reshape(-1, 128) for k in _SMALL], axis=0)


def _unpack_small(packed):
    out, r = {}, 0
    for k in _SMALL:
        shape = _SMALL_SHAPES[k]
        rows = 1
        for s in shape:
            rows *= s
        rows //= 128
        part = packed[r:r + rows]
        out[k] = _blocks_from_lanes(part) if k in ("gate_x_w", "gate_a_w") else part.reshape(shape)
        r += rows
    return out


def kernel(x, norm_in, w_in, conv_w, conv_b, gate_x_w, gate_x_b, gate_a_w, gate_a_b, lru_lambda, gn_gain, w_proj_a, w_proj_b, w_out, norm_final, loss_target, m_norm_in, m_w_in, m_conv_w, m_conv_b, m_gate_x_w, m_gate_x_b, m_gate_a_w, m_gate_a_b, m_lru_lambda, m_gn_gain, m_w_proj_a, m_w_proj_b, m_w_out, m_norm_final, v_norm_in, v_w_in, v_conv_w, v_conv_b, v_gate_x_w, v_gate_x_b, v_gate_a_w, v_gate_a_b, v_lru_lambda, v_gn_gain, v_w_proj_a, v_w_proj_b, v_w_out, v_norm_final):
    B, S, _ = x.shape
    T = B * S
    xi, yi, ci = _coords()
    chip = 2 * xi + yi

    cshard = D_MODEL // N_CHIPS
    mine = _cast_into_slot([w_in[0].reshape(2, D_MODEL // 2, 2 * D_MODEL)], "cast_w_in")
    plan = _gather_plan(3)
    pending_proj = []
    gshard = DK // N_CHIPS
    tiny = jnp.concatenate([conv_w[0], jnp.zeros((4, cshard), F32), jnp.pad(gn_gain[0], ((0, 4), (0, cshard - gshard)))],
                           axis=0).reshape(1, 2, SUBLANES, cshard)
    tiny_buf = lax.dynamic_update_slice(lax.empty((N_CHIPS, 2, SUBLANES, cshard), F32), tiny, (chip, 0, 0, 0))
    near_plan, pass_plan, far_plan = (_chip_gather_plan(stage, 2) for stage in ("near", "pass", "far"))
    (n_near, _), (n_pass, passed_on), (n_far, _) = (_chip_gather_copies(stage, 2) for stage in ("near", "pass", "far"))
    halves = set(range(n_pass)) - passed_on
    near_s, near_r, bufs, near_token = _copies_start([mine[0], tiny_buf], near_plan, n_near, "gather_near_start")

    def in_proj(x2d, g_in, meanwhile):
        as_w = lambda b: b[0].reshape(N_CHIPS, D_MODEL, 2 * D_MODEL)
        slot_x, slot_y, slot_d = 2 * (1 - xi) + yi, 2 * xi + (1 - yi), 2 * (1 - xi) + (1 - yi)
        ids = lambda *chips: jnp.stack(chips).astype(jnp.int32)
        proj, hb, ht = _inproj_first(x2d, g_in, as_w(bufs), ids(chip), "inproj_own", (near_token, *meanwhile))
        got = _copies_wait(near_s, near_r, bufs, proj, near_plan, "gather_near_wait")
        pass_s, pass_r, got, pass_token = _copies_start(got, pass_plan, n_pass, "gather_pass_start")
        mine_proj = _cast_into_slot([w[0].reshape(2, cshard // 2, D_MODEL) for w in (w_proj_a, w_proj_b, w_out)],
                                    "cast_w_proj", (pass_token,))
        got = _copies_wait(pass_s, pass_r, got, mine_proj[0], pass_plan, "gather_pass_wait_halves", only=halves)
        proj = _inproj_more(hb, as_w(got), ids(slot_x, slot_y), proj, "inproj_near")
        got = _copies_wait(pass_s, pass_r, got, proj, pass_plan, "gather_pass_wait_far", only=passed_on)
        far_s, far_r, got, far_token = _copies_start(got, far_plan, n_far, "gather_far_start")
        pending_proj.append(_copies_start(mine_proj, plan, 9, "gather_proj_start", (far_token,)))
        got = _copies_wait(far_s, far_r, got, pending_proj[0][3], far_plan, "gather_far_wait")
        proj = _inproj_more(hb, as_w(got), ids(slot_d), proj, "inproj_far")
        tiny_all = got[1].reshape(N_CHIPS, 2 * SUBLANES, cshard)
        conv_w_full = jnp.transpose(tiny_all[:, 0:CONV, :], (1, 0, 2)).reshape(CONV, D_MODEL)
        gain_full = jnp.transpose(tiny_all[:, 8:8 + HEADS, :gshard], (1, 0, 2)).reshape(HEADS, DK)
        return proj, ht, as_w(got), conv_w_full, gain_full

    def proj_weights(after):
        s_sems, r_sems, pbufs, _ = pending_proj[0]
        got = _copies_wait(s_sems, r_sems, pbufs, after, plan, "gather_proj_wait")
        return [b.reshape(D_MODEL, D_MODEL) for b in got]

    weights = dict(norm_in=norm_in, w_in=w_in, conv_w=conv_w, conv_b=conv_b, gate_x_w=gate_x_w, gate_x_b=gate_x_b,
                   gate_a_w=gate_a_w, gate_a_b=gate_a_b, lru_lambda=lru_lambda, gn_gain=gn_gain, w_proj_a=w_proj_a,
                   w_proj_b=w_proj_b, w_out=w_out, norm_final=norm_final)
    ms = dict(norm_in=m_norm_in, w_in=m_w_in, conv_w=m_conv_w, conv_b=m_conv_b, gate_x_w=m_gate_x_w,
              gate_x_b=m_gate_x_b, gate_a_w=m_gate_a_w, gate_a_b=m_gate_a_b, lru_lambda=m_lru_lambda, gn_gain=m_gn_gain,
              w_proj_a=m_w_proj_a, w_proj_b=m_w_proj_b, w_out=m_w_out, norm_final=m_norm_final)
    vs = dict(norm_in=v_norm_in, w_in=v_w_in, conv_w=v_conv_w, conv_b=v_conv_b, gate_x_w=v_gate_x_w,
              gate_x_b=v_gate_x_b, gate_a_w=v_gate_a_w, gate_a_b=v_gate_a_b, lru_lambda=v_lru_lambda, gn_gain=v_gn_gain,
              w_proj_a=v_w_proj_a, w_proj_b=v_w_proj_b, w_out=v_w_out, norm_final=v_norm_final)
    names = list(weights)
    grads, delta, new_m, new_v = {}, {}, {}, {}

    def update_big(keys, g, half, prev, name, deps=()):
        two = lambda a: a.reshape(a.shape[1], a.shape[2])
        res = _adamw_halves([two(weights[k]) for k in keys], g, [two(ms[k]) for k in keys], [two(vs[k]) for k in keys],
                            half, prev, name, deps)
        for k, (gk, d, mn, vn) in zip(keys, res):
            shp = weights[k].shape
            grads[k], delta[k], new_m[k], new_v[k] = gk.reshape(shp), d.reshape(shp), mn.reshape(shp), vn.reshape(shp)
        return res

    def proj_done(g_proj):
        g4 = g_proj.reshape(2, 3, D_MODEL // (2 * N_CHIPS), D_MODEL)
        return update_big(("w_proj_a", "w_proj_b", "w_out"), g4, None, None, "adamw_proj")[-1][1]

    def w_in_done(g_in, own, prev, deps):
        g4 = g_in.reshape(2, 1, D_MODEL // 2, 2 * D_MODEL)
        return update_big(("w_in",), g4, ci if own else 1 - ci, None if prev is None else [prev],
                          "adamw_w_in_own" if own else "adamw_w_in_other", deps)[0]

    reduce = _GradReduce(proj_done)
    grad_x, dgin = _local_grads(
        x.reshape(T, D_MODEL), loss_target.reshape(T, D_MODEL), B, S, norm_in, in_proj, conv_b,
        gate_x_w[0], gate_x_b, gate_a_w[0], gate_a_b, lru_lambda, proj_weights,
        norm_final.reshape(1, D_MODEL), reduce)

    small_sum, g_norm_in = reduce.finish(dgin.reshape(SUBLANES, LANES), w_in_done)
    loss = small_sum[small_sum.shape[0] - SUBLANES, 0]

    gsm = _unpack_small(small_sum)
    gsm["norm_in"] = g_norm_in
    gsm["conv_w"] = lax.dynamic_slice_in_dim(gsm["conv_w"], chip * cshard, cshard, axis=1)
    gsm["gn_gain"] = lax.dynamic_slice_in_dim(gsm["gn_gain"], chip * gshard, gshard, axis=1)
    smalls = [k for k in names if k not in delta]

    def view(a):
        return a.reshape(1, -1) if a.ndim == 1 else (a.reshape(a.shape[1:]) if a.ndim > 2 else a)

    ds, mns, vns = _adamw_small([view(weights[k]) for k in smalls], [gsm[k].reshape(view(weights[k]).shape) for k in smalls],
                                [view(ms[k]) for k in smalls], [view(vs[k]) for k in smalls], "adamw_small")
    for k, d, mn, vn in zip(smalls, ds, mns, vns):
        shp = weights[k].shape
        grads[k], delta[k], new_m[k], new_v[k] = gsm[k].reshape(shp), d.reshape(shp), mn.reshape(shp), vn.reshape(shp)

    return (loss, grad_x.reshape(B, S, D_MODEL), *[grads[k] for k in names], *[delta[k] for k in names],
            *[new_m[k] for k in names], *[new_v[k] for k in names])
```

```python
import jax
import jax.numpy as jnp
from jax import lax
from jax.experimental import pallas as pl
from jax.experimental.pallas import tpu as pltpu

F32 = jnp.float32
_MXU = jnp.bfloat16

D_MODEL = 1024
N_GROUPS = 8
HEADS = 4
DK = 256
CHUNK = 128
CONV = 4
LRU_BLOCKS = 16
LRU_BW = 64
LRU_C = 8.0
ROPE_THETA = 10000.0
EPS = 1e-6
CW = 256
N_CT = D_MODEL // CW
N_CHIPS = 4
MESH = pl.DeviceIdType.MESH

ADAM_LR = 0.001
ADAM_B1 = 0.9
ADAM_B2 = 0.999
ADAM_EPS = 1e-08
ADAM_WD = 0.01
ADAM_STEP = 10

VMEM_LIMIT = 56 * 1024 * 1024

FIRST_PROJ_TILE = 1024
MORE_PROJ_TILE = 2048
SCAN_TILE = 1024
MID_TILE = 256
DX_TILE = 512
DW_COLS = 512
DW_LOADS = 4
RET_CHUNKS = 2
SUM_ROWS = 256
ADAMW_ROWS = 256
JOIN_PIECES = 8
PROJ_JOIN_PIECES = 4


def _c(v):
    return v.astype(_MXU)


def _dot(a, b):
    return lax.dot_general(a, b, (((1,), (0,)), ((), ())), preferred_element_type=F32)


def _dot_nt(a, b):
    return lax.dot_general(a, b, (((1,), (1,)), ((), ())), preferred_element_type=F32)


def _dot_tn(a, b):
    return lax.dot_general(a, b, (((0,), (0,)), ((), ())), preferred_element_type=F32)


def _sigmoid(z):
    return 0.5 * jnp.tanh(0.5 * z) + 0.5


ANY_SPEC = pl.BlockSpec(memory_space=pl.ANY)


def _after(body, n_in, deps):
    n_deps = len(deps)

    def wrapped(*refs):
        return body(*refs[:n_in], *refs[n_in + n_deps:])

    return wrapped


def _params(sem=None):
    if sem is None:
        return pltpu.CompilerParams(vmem_limit_bytes=VMEM_LIMIT)
    return pltpu.CompilerParams(vmem_limit_bytes=VMEM_LIMIT, dimension_semantics=sem)


def _inproj_first(x2d, g_in, w_all, chips, name, deps=()):
    T = x2d.shape[0]
    tm = min(FIRST_PROJ_TILE, T)
    n_i = T // tm

    def body(s_ref, *refs):
        x_ref, g_ref, w_ref = refs[:3]
        proj_ref, hb_ref, ht_ref, h_all = refs[-4:]
        i = pl.program_id(1)
        rows = pl.ds(pl.multiple_of(i * tm, tm), tm)

        @pl.when(pl.program_id(0) == 0)
        def _():
            x = x_ref[...]
            r = lax.rsqrt(jnp.mean(x * x, axis=-1, keepdims=True) + EPS)
            h = x * r * g_ref[...]
            hb = h.astype(h_all.dtype)
            h_all[rows, :] = hb
            hb_ref[...] = hb
            ht_ref[...] = h.T.astype(ht_ref.dtype)

        proj_ref[...] = _dot(h_all[rows, :], w_ref[0])

    first = lambda j, i: jnp.where(j == 0, i, n_i - 1)
    return pl.pallas_call(
        body,
        name=name,
        grid_spec=pltpu.PrefetchScalarGridSpec(
            num_scalar_prefetch=1,
            grid=(2 * chips.shape[0], n_i),
            in_specs=[
                pl.BlockSpec((tm, D_MODEL), lambda j, i, s: (first(j, i), 0)),
                pl.BlockSpec((1, D_MODEL), lambda j, i, s: (0, 0)),
                pl.BlockSpec((1, D_MODEL, D_MODEL), lambda j, i, s: (s[j // 2], 0, j % 2)),
            ] + [ANY_SPEC] * len(deps),
            out_specs=[
                pl.BlockSpec((tm, D_MODEL), lambda j, i, s: (i, 2 * s[j // 2] + j % 2)),
                pl.BlockSpec((tm, D_MODEL), lambda j, i, s: (first(j, i), 0)),
                pl.BlockSpec((D_MODEL, tm), lambda j, i, s: (0, first(j, i))),
            ],
            scratch_shapes=[pltpu.VMEM((T, D_MODEL), _MXU)],
        ),
        out_shape=[
            jax.ShapeDtypeStruct((T, N_GROUPS * D_MODEL), F32),
            jax.ShapeDtypeStruct((T, D_MODEL), _MXU),
            jax.ShapeDtypeStruct((D_MODEL, T), _MXU),
        ],
        compiler_params=_params(("arbitrary", "arbitrary")),
    )(chips, x2d, g_in, w_all, *deps)


def _inproj_more(hb, w_all, chips, proj, name):
    T = hb.shape[0]
    tm = min(MORE_PROJ_TILE, T)

    def body(s_ref, hb_hbm, w_ref, prev_ref, proj_ref, h_all, sem):
        @pl.when((pl.program_id(0) == 0) & (pl.program_id(1) == 0))
        def _():
            cp = pltpu.make_async_copy(hb_hbm, h_all, sem)
            cp.start()
            cp.wait()

        rows = pl.ds(pl.multiple_of(pl.program_id(1) * tm, tm), tm)
        proj_ref[...] = _dot(h_all[rows, :], w_ref[0])

    return pl.pallas_call(
        body,
        name=name,
        grid_spec=pltpu.PrefetchScalarGridSpec(
            num_scalar_prefetch=1,
            grid=(2 * chips.shape[0], T // tm),
            in_specs=[
                ANY_SPEC,
                pl.BlockSpec((1, D_MODEL, D_MODEL), lambda j, i, s: (s[j // 2], 0, j % 2)),
                ANY_SPEC,
            ],
            out_specs=pl.BlockSpec((tm, D_MODEL), lambda j, i, s: (i, 2 * s[j // 2] + j % 2)),
            scratch_shapes=[pltpu.VMEM((T, D_MODEL), hb.dtype), pltpu.SemaphoreType.DMA],
        ),
        out_shape=jax.ShapeDtypeStruct(proj.shape, F32),
        input_output_aliases={3: 0},
        compiler_params=_params(("arbitrary", "arbitrary")),
    )(chips, hb, w_all, proj)


def _scan_fwd(a, u):
    n = a.shape[0]
    row = lax.broadcasted_iota(jnp.int32, a.shape, 0)
    s = 1
    while s < n:
        m = row >= s
        u = u + a * jnp.where(m, pltpu.roll(u, s, 0), 0.0)
        a = a * jnp.where(m, pltpu.roll(a, s, 0), 1.0)
        s *= 2
    return a, u


def _scan_bwd(b, g):
    n = b.shape[0]
    row = lax.broadcasted_iota(jnp.int32, b.shape, 0)
    s = 1
    while s < n:
        m = row < n - s
        g = g + b * jnp.where(m, pltpu.roll(g, n - s, 0), 0.0)
        b = b * jnp.where(m, pltpu.roll(b, n - s, 0), 1.0)
        s *= 2
    return b, g


LANES = 128
SUBLANES = 8


def _scan_scratch(tc):
    by_lanes = pltpu.VMEM((CW // LANES, tc, LANES), F32)
    return [by_lanes, by_lanes, pltpu.VMEM((tc // SUBLANES, CW), F32), pltpu.VMEM((tc, CW), F32)]


def _scan_tile(a, u, edge, la_ref, lh_ref, c_ref, dst_ref, reverse):
    n, w = a.shape
    groups = n // SUBLANES
    a3 = a.reshape(groups, SUBLANES, w)
    u3 = u.reshape(groups, SUBLANES, w)
    row = lax.broadcasted_iota(jnp.int32, a3.shape, 1)
    for s in (1, 2, 4):
        m = (row < SUBLANES - s) if reverse else (row >= s)
        shift = SUBLANES - s if reverse else s
        u3 = u3 + a3 * jnp.where(m, pltpu.roll(u3, shift, 1), 0.0)
        a3 = a3 * jnp.where(m, pltpu.roll(a3, shift, 1), 1.0)
    al = a3.reshape(n, w)
    hl = u3.reshape(n, w)
    blocks = w // LANES
    for q in range(blocks):
        la_ref[q] = al[:, q * LANES:(q + 1) * LANES]
        lh_ref[q] = hl[:, q * LANES:(q + 1) * LANES]
    ends = pl.ds(0 if reverse else SUBLANES - 1, groups, stride=SUBLANES)
    end_a = jnp.concatenate([la_ref.at[q][ends, :] for q in range(blocks)], axis=-1)
    end_h = jnp.concatenate([lh_ref.at[q][ends, :] for q in range(blocks)], axis=-1)
    prod, part = (_scan_bwd if reverse else _scan_fwd)(end_a, end_h)
    total = part + prod * edge
    g_row = lax.broadcasted_iota(jnp.int32, total.shape, 0)
    if reverse:
        c_ref[...] = jnp.where(g_row == groups - 1, edge, pltpu.roll(total, groups - 1, 0))
    else:
        c_ref[...] = jnp.where(g_row == 0, edge, pltpu.roll(total, 1, 0))
    for g in range(groups):
        rows = slice(g * SUBLANES, (g + 1) * SUBLANES)
        for q in range(blocks):
            cols = slice(q * LANES, (q + 1) * LANES)
            dst_ref[rows, cols] = lh_ref[q, rows, :] + la_ref[q, rows, :] * c_ref[g:g + 1, cols]


def _softplus_neg(lam):
    z = -lam
    return jnp.maximum(z, 0.0) + jnp.log1p(jnp.exp(-jnp.abs(z)))


def _lru_gates(xc, wx_ref, wa_ref, bx_ref, ba_ref, lam_ref):
    xcb = _c(xc)
    i_t = _sigmoid(_dot(xcb, wx_ref[0]) + bx_ref[...])
    r_t = _sigmoid(_dot(xcb, wa_ref[0]) + ba_ref[...])
    sp = _softplus_neg(lam_ref[...])
    log_a = (-LRU_C) * r_t * sp
    a = jnp.exp(log_a)
    mult = jnp.sqrt(1.0 - a * a)
    return xcb, i_t, r_t, sp, a, mult


def _conv_from_ext(ext_ref, xa, cw_ref, cb_ref, tc):
    return (cb_ref[...] + cw_ref[3:4, :] * xa + cw_ref[2:3, :] * ext_ref[7:7 + tc, :]
            + cw_ref[1:2, :] * ext_ref[6:6 + tc, :] + cw_ref[0:1, :] * ext_ref[5:5 + tc, :])


def _lru_fwd(proj, conv_w, conv_b, wx_bd, wa_bd, bx, ba, lam, B, S):
    T = B * S
    tc = min(SCAN_TILE, S)
    nt = S // tc
    h8 = tc // 8

    def body(xa_ref, halo_ref, ga_ref, cw_ref, cb_ref, wx_ref, wa_ref, bx_ref, ba_ref, lam_ref,
             h_ref, ya_ref, ext_ref, carry_ref, la_ref, lh_ref, c_ref):
        t = pl.program_id(2)

        @pl.when(t == 0)
        def _():
            carry_ref[...] = jnp.zeros_like(carry_ref)

        xa = xa_ref[...]
        ext_ref[0:8, :] = jnp.where(t == 0, 0.0, halo_ref[...])
        ext_ref[8:8 + tc, :] = xa
        xc = _conv_from_ext(ext_ref, xa, cw_ref, cb_ref, tc)
        _, i_t, _, _, a, mult = _lru_gates(xc, wx_ref, wa_ref, bx_ref, ba_ref, lam_ref)
        u = mult * (i_t * xc)
        _scan_tile(a, u, carry_ref[7:8, :], la_ref, lh_ref, c_ref, h_ref, False)
        h = h_ref[...]
        carry_ref[...] = h[tc - 8:tc, :]
        ga = ga_ref[...]
        ya_ref[...] = (ga * _sigmoid(ga) * h).astype(ya_ref.dtype)

    row = lambda b, t: b * nt + t
    vec = pl.BlockSpec((1, CW), lambda b, c, t: (0, c))
    mat = pl.BlockSpec((1, CW, CW), lambda b, c, t: (c, 0, 0))
    return pl.pallas_call(
        body,
        name="lru_fwd",
        grid=(B, N_CT, nt),
        in_specs=[
            pl.BlockSpec((tc, CW), lambda b, c, t: (row(b, t), c)),
            pl.BlockSpec((8, CW), lambda b, c, t: (jnp.maximum(row(b, t) * h8 - 1, 0), c)),
            pl.BlockSpec((tc, CW), lambda b, c, t: (row(b, t), N_CT + c)),
            pl.BlockSpec((CONV, CW), lambda b, c, t: (0, c)),
            vec, mat, mat, vec, vec, vec,
        ],
        out_specs=[
            pl.BlockSpec((tc, CW), lambda b, c, t: (row(b, t), c)),
            pl.BlockSpec((tc, CW), lambda b, c, t: (row(b, t), c)),
        ],
        out_shape=[
            jax.ShapeDtypeStruct((T, D_MODEL), F32),
            jax.ShapeDtypeStruct((T, D_MODEL), _MXU),
        ],
        scratch_shapes=[pltpu.VMEM((tc + 8, CW), F32), pltpu.VMEM((8, CW), F32)] + _scan_scratch(tc)[:3],
        compiler_params=_params(("parallel", "parallel", "arbitrary")),
    )(proj, proj, proj, conv_w, conv_b, wx_bd, wa_bd, bx, ba, lam)


def _lru_bwd(dya, proj, hlru, conv_w, conv_b, wx_bd, wa_bd, bx, ba, lam, B, S, deps=()):
    T = B * S
    tc = min(SCAN_TILE, S)
    nt = S // tc
    h8 = tc // 8

    def body(dya_ref, xa_ref, xhalo_ref, ga_ref, h_ref, hhalo_ref, cw_ref, cb_ref, wx_ref, wa_ref, bx_ref, ba_ref,
             lam_ref, dxa_ref, dga_ref, dcw_ref, dcb_ref, dwx_ref, dwa_ref, dbx_ref, dba_ref, dlam_ref,
             ext_ref, ext2_ref, carry_ref, dhalo_ref, la_ref, lh_ref, c_ref, dh_ref, accx_ref, acca_ref):
        b = pl.program_id(1)
        t = pl.program_id(2)
        tt = nt - 1 - t

        @pl.when(t == 0)
        def _():
            carry_ref[...] = jnp.zeros_like(carry_ref)
            dhalo_ref[...] = jnp.zeros_like(dhalo_ref)

        @pl.when((t == 0) & (b == 0))
        def _():
            for r in (dcw_ref, dcb_ref, accx_ref, acca_ref, dbx_ref, dba_ref, dlam_ref):
                r[...] = jnp.zeros_like(r)

        xa = xa_ref[...]
        ext_ref[0:8, :] = jnp.where(tt == 0, 0.0, xhalo_ref[...])
        ext_ref[8:8 + tc, :] = xa
        xc = _conv_from_ext(ext_ref, xa, cw_ref, cb_ref, tc)
        xcb, i_t, r_t, sp, a, mult = _lru_gates(xc, wx_ref, wa_ref, bx_ref, ba_ref, lam_ref)

        h = h_ref[...]
        ga = ga_ref[...]
        dya_t = dya_ref[...]
        sg = _sigmoid(ga)
        dga_ref[...] = (dya_t * h * (sg * (1.0 + ga * (1.0 - sg)))).astype(dga_ref.dtype)
        dlru = dya_t * (ga * sg)

        row = lax.broadcasted_iota(jnp.int32, a.shape, 0)
        coef = jnp.where(row == tc - 1, 1.0, pltpu.roll(a, tc - 1, 0))
        _scan_tile(coef, dlru, carry_ref[0:1, :], la_ref, lh_ref, c_ref, dh_ref, True)
        dh = dh_ref[...]
        ext2_ref[0:tc, :] = a * dh
        carry_ref[...] = ext2_ref[0:8, :]

        ext2_ref[0:8, :] = jnp.where(tt == 0, 0.0, hhalo_ref[...])
        ext2_ref[8:8 + tc, :] = h
        hprev = ext2_ref[7:7 + tc, :]

        da = dh * hprev
        ix = i_t * xc
        dmult = dh * ix
        di = dh * mult * xc
        dxc = dh * mult * i_t
        dlog_a = da * a - dmult * (a * a) / mult
        dr = dlog_a * ((-LRU_C) * sp)
        dlam_ref[...] += jnp.sum(dlog_a * r_t, axis=0, keepdims=True) * (LRU_C * _sigmoid(-lam_ref[...]))
        dza = dr * r_t * (1.0 - r_t)
        dzx = di * i_t * (1.0 - i_t)
        dzab = _c(dza)
        dzxb = _c(dzx)
        dxc = dxc + _dot_nt(dzxb, wx_ref[0]) + _dot_nt(dzab, wa_ref[0])
        accx_ref[...] += _dot_tn(xcb, dzxb)
        acca_ref[...] += _dot_tn(xcb, dzab)
        dbx_ref[...] += jnp.sum(dzx, axis=0, keepdims=True)
        dba_ref[...] += jnp.sum(dza, axis=0, keepdims=True)

        dcb_ref[...] += jnp.sum(dxc, axis=0, keepdims=True)
        dcw_ref[3:4, :] += jnp.sum(dxc * xa, axis=0, keepdims=True)
        dcw_ref[2:3, :] += jnp.sum(dxc * ext_ref[7:7 + tc, :], axis=0, keepdims=True)
        dcw_ref[1:2, :] += jnp.sum(dxc * ext_ref[6:6 + tc, :], axis=0, keepdims=True)
        dcw_ref[0:1, :] += jnp.sum(dxc * ext_ref[5:5 + tc, :], axis=0, keepdims=True)
        ext2_ref[0:tc, :] = dxc
        ext2_ref[tc:tc + 8, :] = dhalo_ref[...]
        dxa = (cw_ref[3:4, :] * dxc + cw_ref[2:3, :] * ext2_ref[1:1 + tc, :]
               + cw_ref[1:2, :] * ext2_ref[2:2 + tc, :] + cw_ref[0:1, :] * ext2_ref[3:3 + tc, :])
        dxa_ref[...] = dxa.astype(dxa_ref.dtype)
        dhalo_ref[...] = ext2_ref[0:8, :]

        @pl.when((b == B - 1) & (t == nt - 1))
        def _():
            lane_block = lax.broadcasted_iota(jnp.int32, (LRU_BW, CW), 1) // LRU_BW
            for acc_ref, out_ref in ((accx_ref, dwx_ref), (acca_ref, dwa_ref)):
                diag = jnp.zeros((LRU_BW, CW), F32)
                for j in range(CW // LRU_BW):
                    diag = jnp.where(lane_block == j, acc_ref[j * LRU_BW:(j + 1) * LRU_BW, :], diag)
                for q in range(CW // LANES):
                    out_ref[0, q] = diag[:, q * LANES:(q + 1) * LANES]

    row_of = lambda b, t: b * nt + (nt - 1 - t)
    tile = lambda off: pl.BlockSpec((tc, CW), lambda c, b, t: (row_of(b, t), off + c))
    halo = pl.BlockSpec((8, CW), lambda c, b, t: (jnp.maximum(row_of(b, t) * h8 - 1, 0), c))
    vec = pl.BlockSpec((1, CW), lambda c, b, t: (0, c))
    mat = pl.BlockSpec((1, CW, CW), lambda c, b, t: (c, 0, 0))
    cwspec = pl.BlockSpec((CONV, CW), lambda c, b, t: (0, c))
    diag = pl.BlockSpec((1, CW // LANES, LRU_BW, LANES), lambda c, b, t: (c, 0, 0, 0))
    return pl.pallas_call(
        _after(body, 13, deps),
        name="lru_bwd",
        grid=(N_CT, B, nt),
        in_specs=[tile(0), tile(0), halo, tile(N_CT), tile(0), halo, cwspec, vec, mat, mat, vec, vec, vec]
        + [ANY_SPEC] * len(deps),
        out_specs=[tile(0), tile(0), cwspec, vec, diag, diag, vec, vec, vec],
        out_shape=[
            jax.ShapeDtypeStruct((T, D_MODEL), _MXU),
            jax.ShapeDtypeStruct((T, D_MODEL), _MXU),
            jax.ShapeDtypeStruct((CONV, D_MODEL), F32),
            jax.ShapeDtypeStruct((1, D_MODEL), F32),
            jax.ShapeDtypeStruct((N_CT, CW // LANES, LRU_BW, LANES), F32),
            jax.ShapeDtypeStruct((N_CT, CW // LANES, LRU_BW, LANES), F32),
            jax.ShapeDtypeStruct((1, D_MODEL), F32),
            jax.ShapeDtypeStruct((1, D_MODEL), F32),
            jax.ShapeDtypeStruct((1, D_MODEL), F32),
        ],
        scratch_shapes=[pltpu.VMEM((tc + 8, CW), F32), pltpu.VMEM((tc + 8, CW), F32),
                        pltpu.VMEM((8, CW), F32), pltpu.VMEM((8, CW), F32)] + _scan_scratch(tc)
        + [pltpu.VMEM((CW, CW), F32), pltpu.VMEM((CW, CW), F32)],
        compiler_params=_params(("parallel", "arbitrary", "arbitrary")),
    )(dya, proj, proj, proj, hlru, hlru, conv_w, conv_b, wx_bd, wa_bd, bx, ba, lam, *deps)


def _retention_tables(S):
    half = DK // 2
    freqs = ROPE_THETA ** (-jnp.arange(half, dtype=F32) / half)
    ang = jnp.arange(S, dtype=F32)[:, None] * freqs[None, :]
    log_g = jnp.log1p(-(2.0 ** (-5.0 - jnp.arange(HEADS, dtype=F32))))
    idx = jnp.arange(CHUNK, dtype=F32)
    diff = idx[:, None] - idx[None, :]
    inner = jnp.where(diff >= 0, jnp.exp(jnp.maximum(diff, 0.0)[None] * log_g[:, None, None]), 0.0)
    cross = jnp.exp((idx[None, :] + 1.0) * log_g[:, None])[:, :, None]
    state = jnp.exp((CHUNK - 1.0 - idx[None, :]) * log_g[:, None])[:, :, None]
    gam = jnp.broadcast_to(jnp.exp(CHUNK * log_g)[:, None, None], (HEADS, 1, DK))
    return jnp.cos(ang), jnp.sin(ang), inner, cross, state, gam


def _rot(x, cos, sin):
    half = DK // 2
    x1, x2 = x[:, :half], x[:, half:]
    return jnp.concatenate([x1 * cos - x2 * sin, x1 * sin + x2 * cos], axis=-1)


def _rot_t(y, cos, sin):
    half = DK // 2
    y1, y2 = y[:, :half], y[:, half:]
    return jnp.concatenate([y1 * cos + y2 * sin, y2 * cos - y1 * sin], axis=-1)


def _groupnorm(o):
    mu = jnp.mean(o, axis=-1, keepdims=True)
    oc = o - mu
    rs = lax.rsqrt(jnp.mean(oc * oc, axis=-1, keepdims=True) + EPS)
    return oc * rs, rs


def _ret_specs(B, chunk_of):
    rows = RET_CHUNKS * CHUNK
    qkv = lambda g: pl.BlockSpec((B, rows, D_MODEL), lambda c: (0, chunk_of(c), g))
    act = pl.BlockSpec((B, rows, D_MODEL), lambda c: (0, chunk_of(c), 0))
    rope = pl.BlockSpec((rows, DK // 2), lambda c: (chunk_of(c), 0))
    dmat = pl.BlockSpec((HEADS, CHUNK, CHUNK), lambda c: (0, 0, 0))
    dvec = pl.BlockSpec((HEADS, CHUNK, 1), lambda c: (0, 0, 0))
    hrow = pl.BlockSpec((HEADS, 1, DK), lambda c: (0, 0, 0))
    rst = pl.BlockSpec((RET_CHUNKS, B, HEADS, DK, DK), lambda c: (chunk_of(c), 0, 0, 0, 0))
    return qkv, act, rope, dmat, dvec, hrow, rst


def _ret_fwd(proj, tables, gain3, B, S):
    T = B * S
    nc = S // CHUNK
    cos, sin, dmat_t, cd_t, sd_t, gam_t = tables

    def body(q_ref, k_ref, v_ref, gb_ref, cos_ref, sin_ref, dm_ref, cd_ref, sd_ref, gam_ref, gain_ref,
             o_ref, yb_ref, rs_ref, state_ref):
        @pl.when(pl.program_id(0) == 0)
        def _():
            state_ref[...] = jnp.zeros_like(state_ref)

        for cc, b, h in [(cc, b, h) for cc in range(RET_CHUNKS) for b in range(B) for h in range(HEADS)]:
            rows = slice(cc * CHUNK, (cc + 1) * CHUNK)
            cos_t, sin_t = cos_ref[rows, :], sin_ref[rows, :]
            cols = slice(h * DK, (h + 1) * DK)
            qb = _c(_rot(q_ref[b, rows, cols], cos_t, sin_t))
            kb = _c(_rot(k_ref[b, rows, cols], cos_t, sin_t) * (DK ** -0.5))
            v = v_ref[b, rows, cols]
            state = state_ref[b, h]
            sb = _c(state)
            rs_ref[cc, b, h] = sb
            scores = _dot_nt(qb, kb) * dm_ref[h]
            o = _dot(_c(scores), _c(v)) + _dot(qb, sb) * cd_ref[h]
            state_ref[b, h] = gam_ref[h] * state + _dot_tn(kb, _c(v * sd_ref[h]))
            o_ref[b, rows, cols] = o
            n, _ = _groupnorm(o)
            gb = gb_ref[b, rows, cols]
            yb_ref[b, rows, cols] = (gb * _sigmoid(gb) * (n * gain_ref[h])).astype(yb_ref.dtype)

    qkv, act, rope, dmat, dvec, hrow, rst = _ret_specs(B, lambda c: c)
    proj3 = proj.reshape(B, S, proj.shape[1])
    o_pre, yb, states = pl.pallas_call(
        body,
        name="ret_fwd",
        grid=(nc // RET_CHUNKS,),
        in_specs=[qkv(2), qkv(3), qkv(4), qkv(5), rope, rope, dmat, dvec, dvec, hrow, hrow],
        out_specs=[act, act, rst],
        out_shape=[
            jax.ShapeDtypeStruct((B, S, D_MODEL), F32),
            jax.ShapeDtypeStruct((B, S, D_MODEL), _MXU),
            jax.ShapeDtypeStruct((nc, B, HEADS, DK, DK), _MXU),
        ],
        scratch_shapes=[pltpu.VMEM((B, HEADS, DK, DK), F32)],
        compiler_params=_params(("arbitrary",)),
    )(proj3, proj3, proj3, proj3, cos, sin, dmat_t, cd_t, sd_t, gam_t, gain3)
    return o_pre.reshape(T, D_MODEL), yb.reshape(T, D_MODEL), states


def _ret_bwd(dyb, o_pre, proj, states, tables, gain3, B, S, deps=()):
    T = B * S
    nc = S // CHUNK
    cos, sin, dmat_t, cd_t, sd_t, gam_t = tables

    def body(dyb_ref, o_ref, q_ref, k_ref, v_ref, gb_ref, rs_ref, cos_ref, sin_ref, dm_ref, cd_ref, sd_ref, gam_ref,
             gain_ref, dr_ref, dgain_ref, dstate_ref):
        @pl.when(pl.program_id(0) == 0)
        def _():
            dstate_ref[...] = jnp.zeros_like(dstate_ref)
            dgain_ref[...] = jnp.zeros_like(dgain_ref)

        for cc, b, h in [(cc, b, h) for cc in reversed(range(RET_CHUNKS)) for b in range(B) for h in range(HEADS)]:
            rows = slice(cc * CHUNK, (cc + 1) * CHUNK)
            cos_t, sin_t = cos_ref[rows, :], sin_ref[rows, :]
            cols = slice(h * DK, (h + 1) * DK)
            gain = gain_ref[h]
            n, rs = _groupnorm(o_ref[b, rows, cols])
            gb = gb_ref[b, rows, cols]
            sg = _sigmoid(gb)
            dy = dyb_ref[b, rows, cols]
            part = lambda g: slice(g * D_MODEL + h * DK, g * D_MODEL + (h + 1) * DK)
            dr_ref[b, rows, part(3)] = (dy * (n * gain) * (sg * (1.0 + gb * (1.0 - sg)))).astype(dr_ref.dtype)
            dgn = dy * (gb * sg)
            dgain_ref[h] += jnp.sum(dgn * n, axis=0, keepdims=True)
            dn = dgn * gain
            do = rs * (dn - jnp.mean(dn, axis=-1, keepdims=True) - n * jnp.mean(dn * n, axis=-1, keepdims=True))

            qb = _c(_rot(q_ref[b, rows, cols], cos_t, sin_t))
            kb = _c(_rot(k_ref[b, rows, cols], cos_t, sin_t) * (DK ** -0.5))
            v = v_ref[b, rows, cols]
            vb = _c(v)
            vsb = _c(v * sd_ref[h])
            dob = _c(do)
            docb = _c(do * cd_ref[h])
            dmat = dm_ref[h]
            dstate = dstate_ref[b, h]
            dsb = _c(dstate)
            pb = _c(_dot_nt(qb, kb) * dmat)
            dsc = _c(_dot_nt(dob, vb) * dmat)
            dq = _dot(dsc, kb) + _dot_nt(docb, rs_ref[cc, b, h])
            dk = _dot_tn(dsc, qb) + _dot_nt(vsb, dsb)
            dv = _dot_tn(pb, dob) + _dot(kb, dsb) * sd_ref[h]
            dstate_ref[b, h] = gam_ref[h] * dstate + _dot_tn(qb, docb)
            dr_ref[b, rows, part(0)] = _rot_t(dq, cos_t, sin_t).astype(dr_ref.dtype)
            dr_ref[b, rows, part(1)] = (_rot_t(dk, cos_t, sin_t) * (DK ** -0.5)).astype(dr_ref.dtype)
            dr_ref[b, rows, part(2)] = dv.astype(dr_ref.dtype)

    n_steps = nc // RET_CHUNKS
    qkv, act, rope, dmat, dvec, hrow, rst = _ret_specs(B, lambda c: n_steps - 1 - c)
    wide = pl.BlockSpec((B, RET_CHUNKS * CHUNK, 4 * D_MODEL), lambda c: (0, n_steps - 1 - c, 0))
    proj3 = proj.reshape(B, S, proj.shape[1])
    dr, dgain = pl.pallas_call(
        _after(body, 14, deps),
        name="ret_bwd",
        grid=(n_steps,),
        in_specs=[act, act, qkv(2), qkv(3), qkv(4), qkv(5), rst, rope, rope, dmat, dvec, dvec, hrow, hrow]
        + [ANY_SPEC] * len(deps),
        out_specs=[wide, hrow],
        out_shape=[jax.ShapeDtypeStruct((B, S, 4 * D_MODEL), _MXU), jax.ShapeDtypeStruct((HEADS, 1, DK), F32)],
        scratch_shapes=[pltpu.VMEM((B, HEADS, DK, DK), F32)],
        compiler_params=_params(("arbitrary",)),
    )(dyb.reshape(B, S, D_MODEL), o_pre.reshape(B, S, D_MODEL), proj3, proj3, proj3, proj3, states, cos, sin, dmat_t,
      cd_t, sd_t, gam_t, gain3, *deps)
    return dr.reshape(T, 4 * D_MODEL), dgain


def _mid(ya, yb, proj, x2d, tgt2d, wpa, wpb, wout, g_fin):
    T = x2d.shape[0]
    tm = min(MID_TILE, T)
    n_steps = T // tm
    rows = D_MODEL // (2 * N_CHIPS)

    def body(ya_ref, yb_ref, ma_ref, mb_ref, x_ref, t_ref, gf_ref, wpa_hbm, wpb_hbm, wout_hbm,
             loss_ref, dx2_ref, dya_ref, dyb_ref, dm_ref, dgf_ref, gw_hbm, w_ref, acc_ref, sem):
        i = pl.program_id(0)

        @pl.when(i == 0)
        def _():
            loads = [pltpu.make_async_copy(src, w_ref.at[k], sem.at[k]) for k, src in enumerate((wpa_hbm, wpb_hbm, wout_hbm))]
            for cp in loads:
                cp.start()
            for cp in loads:
                cp.wait()
            acc_ref[...] = jnp.zeros_like(acc_ref)
            loss_ref[...] = jnp.zeros_like(loss_ref)
            dgf_ref[...] = jnp.zeros_like(dgf_ref)

        ya_t, yb_t = ya_ref[...], yb_ref[...]
        out_a = _dot(ya_t, w_ref[0])
        out_b = _dot(yb_t, w_ref[1])
        sa = _sigmoid(ma_ref[...])
        sb = _sigmoid(mb_ref[...])
        mgb = _c(sa * out_a + sb * out_b)
        x2 = x_ref[...] + _dot(mgb, w_ref[2])
        r2 = lax.rsqrt(jnp.mean(x2 * x2, axis=-1, keepdims=True) + EPS)
        nx = x2 * r2
        gf = gf_ref[...]
        err = nx * gf - t_ref[...]
        loss_ref[...] += 0.5 * jnp.sum(jnp.mean(err * err, axis=-1, keepdims=True), axis=0, keepdims=True)
        dy = err * (1.0 / D_MODEL)
        dgf_ref[...] += jnp.sum(dy * nx, axis=0, keepdims=True)
        dyg = dy * gf
        dx2 = r2 * (dyg - nx * jnp.mean(dyg * nx, axis=-1, keepdims=True))
        dx2_ref[...] = dx2
        dx2b = _c(dx2)
        dmg = _dot_nt(dx2b, w_ref[2])
        acc_ref[2] += _dot_tn(mgb, dx2b)
        dm_ref[:, :D_MODEL] = (dmg * out_a * sa * (1.0 - sa)).astype(dm_ref.dtype)
        dm_ref[:, D_MODEL:] = (dmg * out_b * sb * (1.0 - sb)).astype(dm_ref.dtype)
        dab = _c(dmg * sa)
        dbb = _c(dmg * sb)
        dya_ref[...] = _dot_nt(dab, w_ref[0])
        dyb_ref[...] = _dot_nt(dbb, w_ref[1])
        acc_ref[0] += _dot_tn(ya_t, dab)
        acc_ref[1] += _dot_tn(yb_t, dbb)

        @pl.when(i == n_steps - 1)
        def _():
            copies = [pltpu.make_async_copy(acc_ref.at[k, pl.ds((2 * p + hf) * rows, rows), :], gw_hbm.at[p, hf, k],
                                            sem.at[(k * N_CHIPS + p) * 2 + hf])
                      for k in range(3) for p in range(N_CHIPS) for hf in range(2)]
            for cp in copies:
                cp.start()
            for cp in copies:
                cp.wait()

    tile = lambda j: pl.BlockSpec((tm, D_MODEL), lambda i: (i, j))
    one = pl.BlockSpec((1, D_MODEL), lambda i: (0, 0))
    anyspec = pl.BlockSpec(memory_space=pl.ANY)
    return pl.pallas_call(
        body,
        name="mid",
        grid=(n_steps,),
        in_specs=[tile(0), tile(0), tile(6), tile(7), tile(0), tile(0), one, anyspec, anyspec, anyspec],
        out_specs=[pl.BlockSpec((1, 1), lambda i: (0, 0)), tile(0), tile(0), tile(0),
                   pl.BlockSpec((tm, 2 * D_MODEL), lambda i: (i, 0)), one, anyspec],
        out_shape=[
            jax.ShapeDtypeStruct((1, 1), F32),
            jax.ShapeDtypeStruct((T, D_MODEL), F32),
            jax.ShapeDtypeStruct((T, D_MODEL), F32),
            jax.ShapeDtypeStruct((T, D_MODEL), F32),
            jax.ShapeDtypeStruct((T, 2 * D_MODEL), _MXU),
            jax.ShapeDtypeStruct((1, D_MODEL), F32),
            jax.ShapeDtypeStruct((N_CHIPS, 2, 3, rows, D_MODEL), F32),
        ],
        scratch_shapes=[pltpu.VMEM((3, D_MODEL, D_MODEL), _MXU), pltpu.VMEM((3, D_MODEL, D_MODEL), F32),
                        pltpu.SemaphoreType.DMA((3 * N_CHIPS * 2,))],
        compiler_params=_params(("arbitrary",)),
    )(ya, yb, proj, proj, x2d, tgt2d, g_fin, wpa, wpb, wout)


def _inproj_bwd_dx(dparts, w_all, x2d, dx2, g_in, first, count, prev, name, deps=()):
    T = x2d.shape[0]
    tm = min(DX_TILE, T)
    n_d = len(dparts)
    groups = [(a, k) for a, d in enumerate(dparts) for k in range(d.shape[1] // D_MODEL)]
    dg_start = jnp.zeros((1, D_MODEL), F32) if prev is None else prev[1]
    carried = () if prev is None else (prev[0],)

    def body(*refs):
        d_refs = refs[:n_d]
        x_ref, dx2_ref, g_ref, dg0_ref, w_hbm = refs[n_d:n_d + 5]
        dx_ref, dg_ref, w_ref, sem = refs[-4:]

        def load(j):
            part = (j // 2, slice(None), pl.ds((j % 2) * D_MODEL, D_MODEL))
            return pltpu.make_async_copy(w_hbm.at[part], w_ref.at[part], sem.at[j])

        def tile(before_group):
            dh = jnp.zeros((tm, D_MODEL), F32)
            for j, (a, k) in enumerate(groups):
                before_group(j)
                dh = dh + _dot_nt(d_refs[a][:, k * D_MODEL:(k + 1) * D_MODEL],
                                  w_ref[j // 2, :, (j % 2) * D_MODEL:(j % 2 + 1) * D_MODEL])
            x = x_ref[...]
            r = lax.rsqrt(jnp.mean(x * x, axis=-1, keepdims=True) + EPS)
            nx = x * r
            dg_ref[...] += jnp.sum(dh * nx, axis=0, keepdims=True)
            dhg = dh * g_ref[...]
            dx_ref[...] = dx2_ref[...] + r * (dhg - nx * jnp.mean(dhg * nx, axis=-1, keepdims=True))

        first = pl.program_id(0) == 0

        @pl.when(first)
        def _():
            for j in range(len(groups)):
                load(j).start()
            dg_ref[...] = dg0_ref[...]
            tile(lambda j: load(j).wait())

        @pl.when(jnp.logical_not(first))
        def _():
            tile(lambda j: None)

    tile = pl.BlockSpec((tm, D_MODEL), lambda i: (first + i, 0))
    one = pl.BlockSpec((1, D_MODEL), lambda i: (0, 0))
    return pl.pallas_call(
        body,
        name=name,
        grid=(count,),
        in_specs=[pl.BlockSpec((tm, d.shape[1]), lambda i: (first + i, 0)) for d in dparts]
        + [tile, tile, one, one, ANY_SPEC] + [ANY_SPEC] * (len(carried) + len(deps)),
        out_specs=[tile, one],
        out_shape=[jax.ShapeDtypeStruct((T, D_MODEL), F32), jax.ShapeDtypeStruct((1, D_MODEL), F32)],
        input_output_aliases={n_d + 5: 0} if carried else {},
        scratch_shapes=[pltpu.VMEM(w_all.shape, w_all.dtype), pltpu.SemaphoreType.DMA((len(groups),))],
        compiler_params=_params(("arbitrary",)),
    )(*dparts, x2d, dx2, g_in, dg_start, w_all, *carried, *deps)


def _inproj_bwd_dw(ht, dparts, name, deps=()):
    T = ht.shape[1]
    tn = DW_COLS
    half = D_MODEL // 2
    per_chip = 2 * D_MODEL // tn
    n_d = len(dparts)
    tiles = [(a, t) for a, d in enumerate(dparts) for t in range(d.shape[1] // tn)]
    offs = [sum(d.shape[1] // tn for d in dparts[:a]) for a in range(n_d)]

    def body(*refs):
        ht_hbm = refs[0]
        d_refs = refs[1:1 + n_d]
        out_ref, ht_ref, sem = refs[-3:]
        t = pl.program_id(0)

        def load(k):
            cols = pl.ds(k * (T // DW_LOADS), T // DW_LOADS)
            return pltpu.make_async_copy(ht_hbm.at[:, cols], ht_ref.at[:, cols], sem.at[k])

        def store(g):
            out_ref[0, 0] = g[:half]
            out_ref[0, 1] = g[half:]

        @pl.when(t == 0)
        def _():
            for k in range(DW_LOADS):
                load(k).start()
            g = jnp.zeros((D_MODEL, tn), F32)
            for k in range(DW_LOADS):
                load(k).wait()
                tokens = slice(k * (T // DW_LOADS), (k + 1) * (T // DW_LOADS))
                g = g + _dot(ht_ref[:, tokens], d_refs[0][tokens, :])
            store(g)

        for a in range(n_d):
            lo, hi = max(offs[a], 1), offs[a] + dparts[a].shape[1] // tn

            @pl.when((t >= lo) & (t < hi))
            def _(a=a):
                store(_dot(ht_ref[...], d_refs[a][...]))

    def dspec(a):
        n_a = dparts[a].shape[1] // tn
        return pl.BlockSpec((T, tn), lambda t: (0, jnp.clip(t - offs[a], 0, n_a - 1)))

    return pl.pallas_call(
        body,
        name=name,
        grid=(len(tiles),),
        in_specs=[ANY_SPEC] + [dspec(a) for a in range(n_d)] + [ANY_SPEC] * len(deps),
        out_specs=pl.BlockSpec((1, 2, half, tn), lambda t: (t // per_chip, 0, 0, t % per_chip)),
        out_shape=jax.ShapeDtypeStruct((len(tiles) // per_chip, 2, half, 2 * D_MODEL), F32),
        scratch_shapes=[pltpu.VMEM(ht.shape, ht.dtype), pltpu.SemaphoreType.DMA((DW_LOADS,))],
        compiler_params=_params(("arbitrary",)),
    )(ht, *dparts, *deps)


def _coords():
    return lax.axis_index("x"), lax.axis_index("y"), lax.axis_index("c")


def _other_chips(x, y):
    return [(1 - x, y), (x, 1 - y), (1 - x, 1 - y)]


def _chunks(rows, n):
    size = rows // n
    return [pl.ds(q * size, size) for q in range(n)]


HBM_SPEC = pl.BlockSpec(memory_space=pltpu.HBM)
SEM_SPEC = pl.BlockSpec(memory_space=pltpu.SEMAPHORE)
DATAFLOW = pltpu.SideEffectType.DATAFLOW_SIDE_EFFECTING


def _copies_start(bufs, plan, n_copies, name, deps=()):
    n = len(bufs)
    n_deps = len(deps)

    def body(*refs):
        ins = refs[:n]
        send_sems, recv_sems = refs[n + n_deps], refs[n + n_deps + 1]
        token = refs[-1]
        for k, send, _ in plan(ins):
            if send is not None:
                src, dst, dev, pred = send
                cp = pltpu.make_async_remote_copy(src_ref=src, dst_ref=dst, send_sem=send_sems.at[k],
                                                  recv_sem=recv_sems.at[k], device_id=dev, device_id_type=MESH)
                if pred is None:
                    cp.start()
                else:
                    pl.when(pred)(cp.start)
        token[...] = jnp.zeros_like(token)

    hbm = [pltpu.with_memory_space_constraint(b, pltpu.HBM) for b in bufs]
    outs = pl.pallas_call(
        body,
        name=name,
        in_specs=[HBM_SPEC] * n + [ANY_SPEC] * n_deps,
        out_specs=(SEM_SPEC, SEM_SPEC, *([HBM_SPEC] * n), pl.BlockSpec(memory_space=pltpu.VMEM)),
        out_shape=(pltpu.SemaphoreType.DMA((n_copies,)), pltpu.SemaphoreType.DMA((n_copies,)),
                   *[pltpu.HBM(b.shape, b.dtype) for b in bufs], jax.ShapeDtypeStruct((8, 128), F32)),
        input_output_aliases={a: 2 + a for a in range(n)},
        compiler_params=pltpu.CompilerParams(has_side_effects=DATAFLOW),
    )(*hbm, *deps)
    return outs[0], outs[1], list(outs[2:2 + n]), outs[-1]


def _copies_wait(send_sems, recv_sems, bufs, after, plan, name, only=None):
    n = len(bufs)

    def body(*refs):
        ins = refs[:n]
        s_sems, r_sems = refs[n], refs[n + 1]
        for k, send, recv in plan(ins):
            if only is not None and k not in only:
                continue
            if send is not None:
                src, dst, dev, pred = send
                cp = pltpu.make_async_remote_copy(src_ref=src, dst_ref=dst, send_sem=s_sems.at[k],
                                                  recv_sem=r_sems.at[k], device_id=dev, device_id_type=MESH)
                if pred is None:
                    cp.wait_send()
                else:
                    pl.when(pred)(cp.wait_send)
            if recv is not None:
                dst, pred = recv
                cp = pltpu.make_async_remote_copy(src_ref=dst, dst_ref=dst, send_sem=s_sems.at[k],
                                                  recv_sem=r_sems.at[k], device_id=_coords(), device_id_type=MESH)
                if pred is None:
                    cp.wait_recv()
                else:
                    pl.when(pred)(cp.wait_recv)

    outs = pl.pallas_call(
        body,
        name=name,
        in_specs=[HBM_SPEC] * n + [SEM_SPEC, SEM_SPEC, pl.BlockSpec(memory_space=pl.ANY)],
        out_specs=[HBM_SPEC] * n,
        out_shape=[pltpu.HBM(b.shape, b.dtype) for b in bufs],
        input_output_aliases={a: a for a in range(n)},
        compiler_params=pltpu.CompilerParams(has_side_effects=DATAFLOW),
    )(*bufs, send_sems, recv_sems, after)
    return list(outs)


def _gather_plan(n_bufs):
    def plan(refs):
        x, y, c = _coords()
        me = 2 * x + y
        out = []
        for k, (px, py) in enumerate(_other_chips(x, y)):
            for a in range(n_bufs):
                out.append((k * n_bufs + a, (refs[a].at[me], refs[a].at[me], (px, py, c), None),
                            (refs[a].at[2 * px + py], None)))
        return out
    return plan


def _cast_into_slot(ws, name, deps=()):
    n = len(ws)
    nt = 2

    def body(s_ref, *refs):
        outs = refs[len(refs) - n:]
        for a in range(n):
            outs[a][0] = refs[a][...].astype(outs[a].dtype)

    xi, yi, _ = _coords()
    return pl.pallas_call(
        body,
        name=name,
        grid_spec=pltpu.PrefetchScalarGridSpec(
            num_scalar_prefetch=1,
            grid=(2, nt),
            in_specs=[pl.BlockSpec((1, w.shape[1] // nt, w.shape[2]), lambda hf, i, s: (hf, i, 0)) for w in ws]
            + [ANY_SPEC] * len(deps),
            out_specs=[pl.BlockSpec((1, 1, w.shape[1] // nt, w.shape[2]), lambda hf, i, s: (s[0], hf, i, 0)) for w in ws],
        ),
        out_shape=[jax.ShapeDtypeStruct((N_CHIPS,) + w.shape, _MXU) for w in ws],
        compiler_params=_params(("parallel", "parallel")),
    )((2 * xi + yi).reshape(1).astype(jnp.int32), *ws, *deps)


def _chip_gather_plan(stage, n_bufs):
    def plan(refs):
        x, y, c = _coords()
        me = 2 * x + y
        near = [(1 - x, y), (x, 1 - y)]
        slots = [2 * (1 - x) + y, 2 * x + (1 - y), 2 * (1 - x) + (1 - y)]
        sibling = (x, y, 1 - c)
        pass_to = (jnp.where(c == 0, x, 1 - x), jnp.where(c == 0, 1 - y, y), c)
        pass_slot = jnp.where(c == 0, slots[0], slots[1])
        out = []

        def move(src_slot, to, land_slot, land_core, pieces):
            for a, buf in enumerate(refs):
                for rows in _chunks(buf.shape[2], pieces[a]):
                    out.append((len(out), (buf.at[src_slot, c, rows], buf.at[src_slot, c, rows], to, None),
                                (buf.at[land_slot, land_core, rows], None)))

        if stage == "near":
            for k, chip in enumerate(near):
                move(me, (*chip, c), slots[k], c, NEAR_PIECES[:n_bufs])
        elif stage == "pass":
            move(pass_slot, pass_to, slots[2], c, PASS_PIECES[:n_bufs])
            for k in range(2):
                move(slots[k], sibling, slots[k], 1 - c, [1] * n_bufs)
        else:
            move(slots[2], sibling, slots[2], 1 - c, [1] * n_bufs)
        return out
    return plan


NEAR_PIECES = (2, 1)
PASS_PIECES = (2, 1)


def _chip_gather_copies(stage, n_bufs):
    if stage == "near":
        return 2 * sum(NEAR_PIECES[:n_bufs]), None
    if stage == "pass":
        n_pass = sum(PASS_PIECES[:n_bufs])
        return n_pass + 2 * n_bufs, set(range(n_pass))
    return n_bufs, None


def _swap_plan(n_slabs):
    def plan(refs):
        x, y, c = _coords()
        out, k = [], 0
        for i, n in enumerate(n_slabs):
            g, land = refs[2 * i], refs[2 * i + 1]
            for p in range(n):
                out.append((k, (g.at[p, 1 - c], land.at[p], (x, y, 1 - c), None), (land.at[p], None)))
                k += 1
        return out
    return plan


def _is_one_of(chip, dests):
    hit = chip == dests[0]
    for d in dests[1:]:
        hit = hit | (chip == d)
    return hit


def _slab_of(chip, dests):
    return sum(j * (chip == d).astype(jnp.int32) for j, d in enumerate(dests))


def _scatter_plan(dest_sets):
    def plan(refs):
        x, y, c = _coords()
        me = 2 * x + y
        out = []
        for k, (px, py) in enumerate(_other_chips(x, y)):
            peer = 2 * px + py
            for i, dests in enumerate(dest_sets):
                cs, land = refs[2 * i], refs[2 * i + 1]
                everyone = len(dests) == N_CHIPS
                send = (cs.at[_slab_of(peer, dests)], land.at[k], (px, py, c),
                        None if everyone else _is_one_of(peer, dests))
                recv = (land.at[k], None if everyone else _is_one_of(me, dests))
                out.append((k * len(dest_sets) + i, send, recv))
        return out
    return plan


def _join_plan(rows, n_pieces):
    def plan(refs):
        x, y, c = _coords()
        (buf,) = refs
        return [(i, (buf.at[c, piece], buf.at[c, piece], (x, y, 1 - c), None), (buf.at[1 - c, piece], None))
                for i, piece in enumerate(_chunks(rows, n_pieces))]
    return plan


def _join_plans(parts):
    def plan(refs):
        out, b0, k0 = [], 0, 0
        for part_plan, n_bufs, n_copies in parts:
            out += [(k0 + k, send, recv) for k, send, recv in part_plan(refs[b0:b0 + n_bufs])]
            b0 += n_bufs
            k0 += n_copies
        return out
    return plan


def _allgather_plan():
    def plan(refs):
        x, y, c = _coords()
        (land,) = refs
        me = 4 * x + 2 * y + c
        out = []
        for r in range(1, 8):
            px = 1 - x if r & 4 else x
            py = 1 - y if r & 2 else y
            pc = 1 - c if r & 1 else c
            out.append((r - 1, (land.at[me], land.at[me], (px, py, pc), None), (land.at[4 * px + 2 * py + pc], None)))
        return out
    return plan


def _sum_gathered(land, name):
    def body(land_ref, o_ref):
        acc = land_ref[0]
        for d in range(1, 8):
            acc = acc + land_ref[d]
        o_ref[...] = acc

    return pl.pallas_call(
        body,
        name=name,
        out_shape=jax.ShapeDtypeStruct(land.shape[1:], F32),
        compiler_params=_params(),
    )(land)


def _row_tile(rows, cap):
    t = cap
    while rows % t:
        t //= 2
    return t


def _add_my_half(g, r, name):
    n_slabs, _, R, C = g.shape
    tr = R if n_slabs > 1 else _row_tile(R, SUM_ROWS)

    def body(c_ref, g_ref, r_ref, o_ref):
        o_ref[...] = (g_ref[0] + r_ref[...]).astype(o_ref.dtype)

    return pl.pallas_call(
        body,
        name=name,
        grid_spec=pltpu.PrefetchScalarGridSpec(
            num_scalar_prefetch=1,
            grid=(n_slabs, R // tr),
            in_specs=[pl.BlockSpec((1, 1, tr, C), lambda p, i, c_ref: (p, c_ref[0], i, 0)),
                      pl.BlockSpec((1, tr, C), lambda p, i, c_ref: (p, i, 0))],
            out_specs=pl.BlockSpec((1, tr, C), lambda p, i, c_ref: (p, i, 0)),
        ),
        out_shape=jax.ShapeDtypeStruct(r.shape, jnp.bfloat16),
        compiler_params=_params(("parallel", "parallel")),
    )(lax.axis_index("c").reshape(1).astype(jnp.int32), g, r)


def _sum_slabs(own, got, name, deps=()):
    _, R, C = own.shape
    tr = _row_tile(R, SUM_ROWS)

    def body(s_ref, own_ref, got_ref, *rest):
        rest[-1][0] = ((own_ref[0].astype(F32) + got_ref[0].astype(F32)) + got_ref[1].astype(F32)) + got_ref[2].astype(F32)

    xi, yi, ci = _coords()
    return pl.pallas_call(
        body,
        name=name,
        grid_spec=pltpu.PrefetchScalarGridSpec(
            num_scalar_prefetch=1,
            grid=(R // tr,),
            in_specs=[pl.BlockSpec((1, tr, C), lambda i, s: (s[0], i, 0)),
                      pl.BlockSpec((3, tr, C), lambda i, s: (0, i, 0))] + [ANY_SPEC] * len(deps),
            out_specs=pl.BlockSpec((1, tr, C), lambda i, s: (s[1], i, 0)),
        ),
        out_shape=jax.ShapeDtypeStruct((2, R, C), F32),
        compiler_params=_params(("parallel",)),
    )(jnp.stack([2 * xi + yi, ci]).astype(jnp.int32), own, got, *deps)


def _sum_parts(owns, got, dest_sets, name):
    n = len(owns)
    _, R, C = owns[0].shape
    tr = _row_tile(R, SUM_ROWS)

    def body(s_ref, *refs):
        got_ref, o_ref = refs[n], refs[-1]
        total = jnp.zeros((tr, C), F32)
        for i in range(n):
            total = total + jnp.where(s_ref[2 + 2 * i] == 1, refs[i][0].astype(F32), 0.0)
        o_ref[0] = ((total + got_ref[0].astype(F32)) + got_ref[1].astype(F32)) + got_ref[2].astype(F32)

    xi, yi, ci = _coords()
    me = 2 * xi + yi
    scalars = [ci, ci]
    for dests in dest_sets:
        scalars += [_is_one_of(me, dests).astype(jnp.int32), _slab_of(me, dests)]
    own_spec = lambda i: pl.BlockSpec((1, tr, C), lambda r, s: (s[3 + 2 * i], r, 0))
    return pl.pallas_call(
        body,
        name=name,
        grid_spec=pltpu.PrefetchScalarGridSpec(
            num_scalar_prefetch=1,
            grid=(R // tr,),
            in_specs=[own_spec(i) for i in range(n)] + [pl.BlockSpec((3, tr, C), lambda r, s: (0, r, 0))],
            out_specs=pl.BlockSpec((1, tr, C), lambda r, s: (s[0], r, 0)),
        ),
        out_shape=jax.ShapeDtypeStruct((2, R, C), F32),
        compiler_params=_params(("parallel",)),
    )(jnp.stack(scalars).astype(jnp.int32), *owns, got)


def _adamw_math(w, g, m, v):
    m = ADAM_B1 * m + (1.0 - ADAM_B1) * g
    v = ADAM_B2 * v + (1.0 - ADAM_B2) * (g * g)
    m_hat = m / (1.0 - ADAM_B1 ** ADAM_STEP)
    v_hat = v / (1.0 - ADAM_B2 ** ADAM_STEP)
    delta = -ADAM_LR * (m_hat / (jnp.sqrt(v_hat) + ADAM_EPS) + ADAM_WD * w)
    return delta, m, v


def _adamw_halves(ws, g, ms, vs, half, prev, name, deps=()):
    n = len(ws)
    _, _, R, C = g.shape
    tr = _row_tile(R, ADAMW_ROWS)
    steps = R // tr
    carried = [] if prev is None else [a for four in prev for a in four]
    both = half is None
    which = (lambda i, s: i // steps) if both else (lambda i, s: s[0])
    half = 0 if both else half

    def body(s_ref, *refs):
        w_refs, g_refs, m_refs, v_refs = (refs[k * n:(k + 1) * n] for k in range(4))
        outs = refs[len(refs) - 4 * n:]
        for a in range(n):
            grad = g_refs[a][0, 0]
            d, mn, vn = _adamw_math(w_refs[a][...], grad, m_refs[a][...], v_refs[a][...])
            for o, val in zip(outs[4 * a:4 * a + 4], (grad, d, mn, vn)):
                o[...] = val

    rows = pl.BlockSpec((tr, C), lambda i, s: (which(i, s) * steps + i % steps, 0))
    grad_spec = lambda a: pl.BlockSpec((1, 1, tr, C), lambda i, s: (which(i, s), a, i % steps, 0))
    n_in = 4 * n
    outs = pl.pallas_call(
        body,
        name=name,
        grid_spec=pltpu.PrefetchScalarGridSpec(
            num_scalar_prefetch=1,
            grid=(2 * steps if both else steps,),
            in_specs=[rows] * n + [grad_spec(a) for a in range(n)] + [rows] * (2 * n)
            + [ANY_SPEC] * (len(carried) + len(deps)),
            out_specs=[rows] * (4 * n),
        ),
        out_shape=[jax.ShapeDtypeStruct((2 * R, C), F32)] * (4 * n),
        input_output_aliases={1 + n_in + k: k for k in range(len(carried))},
        compiler_params=_params(("parallel",)),
    )(jnp.reshape(half, (1,)).astype(jnp.int32), *ws, *([g] * n), *ms, *vs, *carried, *deps)
    return [outs[4 * a:4 * a + 4] for a in range(n)]


def _adamw_small(ws, gs, ms, vs, name):
    n = len(ws)

    def body(*refs):
        for a in range(n):
            d, mn, vn = _adamw_math(refs[a][...], refs[n + a][...], refs[2 * n + a][...], refs[3 * n + a][...])
            refs[4 * n + a][...] = d
            refs[5 * n + a][...] = mn
            refs[6 * n + a][...] = vn

    shapes = [jax.ShapeDtypeStruct(w.shape, F32) for w in ws]
    outs = pl.pallas_call(
        body,
        name=name,
        out_shape=shapes * 3,
        compiler_params=_params(),
    )(*ws, *gs, *ms, *vs)
    return outs[:n], outs[n:2 * n], outs[2 * n:]


def _to_blockdiag(w):
    per = CW // LRU_BW
    w4 = w.reshape(N_CT, per, LRU_BW, LRU_BW)
    eye = jnp.eye(per, dtype=w.dtype)
    return (w4[:, :, :, None, :] * eye[None, :, None, :, None]).reshape(N_CT, CW, CW)


def _blocks_from_lanes(g):
    side = LANES // LRU_BW
    g5 = g.reshape(N_CT, CW // LANES, LRU_BW, side, LRU_BW)
    return jnp.transpose(g5, (0, 1, 3, 2, 4)).reshape(LRU_BLOCKS, LRU_BW, LRU_BW)


def _local_grads(x2d, tgt2d, B, S, g_in, in_proj, conv_b, gate_x_w, gate_x_b, gate_a_w, gate_a_b, lam,
                 proj_weights, g_fin, reduce):
    wx_bd = _c(_to_blockdiag(gate_x_w))
    wa_bd = _c(_to_blockdiag(gate_a_w))
    tables = _retention_tables(S)

    proj, ht, w_all, conv_w, gain = in_proj(x2d, g_in, (*tables, wx_bd, wa_bd))
    gain3 = gain.reshape(HEADS, 1, DK)
    hlru, ya = _lru_fwd(proj, conv_w, conv_b, wx_bd, wa_bd, gate_x_b, gate_a_b, lam, B, S)
    o_pre, yb, states = _ret_fwd(proj, tables, gain3, B, S)
    wpa, wpb, wout = proj_weights(yb)
    loss, dx2, dya, dyb, dm, dgf, gw_proj = _mid(ya, yb, proj, x2d, tgt2d, wpa, wpb, wout, g_fin)
    g3 = _inproj_bwd_dw(ht, [dm], "inproj_bwd_dw_m")
    deps = reduce.m_ready(gw_proj, g3)
    dr, dgain = _ret_bwd(dyb, o_pre, proj, states, tables, gain3, B, S, deps)
    deps = reduce.ret_done(dr)
    g12 = _inproj_bwd_dw(ht, [dr], "inproj_bwd_dw_r", deps)
    deps = reduce.r_ready(g12)
    dxa, dga, dcw, dcb, dwx, dwa, dbx, dba, dlam = _lru_bwd(
        dya, proj, hlru, conv_w, conv_b, wx_bd, wa_bd, gate_x_b, gate_a_b, lam, B, S, deps)
    small = dict(conv_w=dcw, conv_b=dcb, gate_x_w=dwx, gate_x_b=dbx, gate_a_w=dwa, gate_a_b=dba, lru_lambda=dlam,
                 gn_gain=dgain.reshape(HEADS, DK), norm_final=dgf)
    deps = reduce.lru_done(dxa, _pack_small(small, loss, reduce.slot()))
    g0 = _inproj_bwd_dw(ht, [dxa, dga], "inproj_bwd_dw_a", deps)
    deps = reduce.a_ready(g0)
    n_tiles = x2d.shape[0] // min(DX_TILE, x2d.shape[0])
    grad_x, dgin = _inproj_bwd_dx([dxa, dga, dr, dm], w_all, x2d, dx2, g_in, 0, n_tiles, None, "inproj_bwd_dx", deps)
    return grad_x, dgin


ALL_CHIPS = (0, 1, 2, 3)


class _GradReduce:
    def __init__(self, proj_done):
        self.pending = {}
        self.proj_done = proj_done
        self.land_in = None

    def _start(self, key, parts, name):
        bufs, plans, shared = [], [], None
        for part_bufs, plan, n_copies, part_shared in parts:
            if part_shared is not None:
                shared = len(bufs) + part_shared
            plans.append((plan, len(part_bufs), n_copies))
            bufs += part_bufs
        plan = _join_plans(plans)
        send_sems, recv_sems, bufs, token = _copies_start(bufs, plan, sum(p[2] for p in plans), name + "_start")
        if shared is not None:
            self.land_in = bufs[shared]
        self.pending[key] = (send_sems, recv_sems, bufs, plan, name + "_wait", shared)
        return (token,)

    def _finish(self, key, after):
        send_sems, recv_sems, bufs, plan, name, shared = self.pending.pop(key)
        if shared is not None:
            bufs[shared] = self.land_in
        bufs = _copies_wait(send_sems, recv_sems, bufs, after, plan, name)
        if shared is not None:
            self.land_in = bufs[shared]
        return bufs

    @staticmethod
    def _swap(pieces):
        bufs = []
        for g in pieces:
            bufs += [g, lax.empty((g.shape[0],) + g.shape[2:], F32)]
        n_slabs = [g.shape[0] for g in pieces]
        return bufs, _swap_plan(n_slabs), sum(n_slabs), None

    def _scatter(self, sums, dest_sets):
        bufs = []
        for cs in sums:
            bufs += [cs, lax.empty((3,) + cs.shape[1:], cs.dtype)]
        if self.land_in is not None:
            bufs[-1] = self.land_in
        return bufs, _scatter_plan(dest_sets), 3 * len(sums), len(bufs) - 1

    @staticmethod
    def slot():
        x, y, c = _coords()
        return 4 * x + 2 * y + c

    def _gather8(self, block):
        land = lax.dynamic_update_slice(lax.empty((8,) + block.shape, F32), block[None], (self.slot(), 0, 0))
        return [land], _allgather_plan(), 7, None

    def m_ready(self, gw_proj, g3):
        rows = gw_proj.shape[2] * gw_proj.shape[3]
        return self._start("m", [self._swap([gw_proj.reshape(N_CHIPS, 2, rows, D_MODEL), g3])], "swap_m")

    def ret_done(self, after):
        proj, land_p, g3, land_3 = self._finish("m", after)
        sums_m = [_add_my_half(proj, land_p, "chip_sum_proj"), _add_my_half(g3, land_3, "chip_sum_m")]
        return self._start("sm", [self._scatter(sums_m, [ALL_CHIPS, (3,)])], "scatter_m")

    def r_ready(self, g12):
        return self._start("r", [self._swap([g12])], "swap_r")

    def lru_done(self, after, packed):
        g12, land_12 = self._finish("r", after)
        sums_r = [_add_my_half(g12, land_12, "chip_sum_r")]
        return (self._start("sr", [self._scatter(sums_r, [(1, 2)])], "scatter_r")
                + self._start("small", [([packed], _allgather_plan(), 7, None)], "gather_small"))

    def a_ready(self, g0):
        (token,) = self._start("a", [self._swap([g0])], "swap_a")
        csp, gotp, self.cs3, _ = self._finish("sm", token)
        half_proj = _sum_slabs(csp, gotp, "sum_w_proj")
        g0, land_0 = self._finish("a", half_proj)
        join = ([half_proj], _join_plan(half_proj.shape[1], PROJ_JOIN_PIECES), PROJ_JOIN_PIECES, None)
        return self._start("sa", [self._scatter([_add_my_half(g0, land_0, "chip_sum_a")], [(0,)]), join], "scatter_a")

    def finish(self, dgin, w_in_done):
        (token,) = self._start("n", [self._gather8(dgin)], "gather_norm_in")
        (small,) = self._finish("small", token)
        cs12, _ = self._finish("sr", token)
        cs0, _, g_proj = self._finish("sa", token)
        self.proj_done(g_proj)
        half_in =_sum_parts([self.cs3, cs12, cs0], self.land_in, [(3,), (1, 2), (0,)], "sum_w_in")
        deps = self._start("j", [([half_in], _join_plan(half_in.shape[1], JOIN_PIECES), JOIN_PIECES, None)], "join_w_in")
        first = w_in_done(self.pending["j"][2][0], True, None, deps)
        (g_in,) = self._finish("j", first[1])
        done = w_in_done(g_in, False, first, ())
        (norm_in,) = self._finish("n", done[1])
        return _sum_gathered(small, "sum_small_grads"), _sum_gathered(norm_in, "sum_norm_in_grad")


_SMALL = ("gate_x_w", "gate_a_w", "conv_w", "conv_b", "gate_x_b", "gate_a_b", "lru_lambda", "gn_gain", "norm_final")
_SMALL_SHAPES = dict(gate_x_w=(LRU_BLOCKS, LRU_BW, LRU_BW), gate_a_w=(LRU_BLOCKS, LRU_BW, LRU_BW),
                     norm_in=(1, D_MODEL), conv_w=(CONV, D_MODEL), conv_b=(1, D_MODEL), gate_x_b=(1, D_MODEL),
                     gate_a_b=(1, D_MODEL), lru_lambda=(1, D_MODEL), gn_gain=(HEADS, DK), norm_final=(1, D_MODEL))


def _pack_small(small, loss, slot):
    parts = [small[k] if small[k].ndim == 2 else small[k].reshape(-1, LANES) for k in _SMALL]
    m = sum(p.size for p in parts) // LANES + SUBLANES

    def body(s_ref, *refs):
        o_ref = refs[-1]
        r = 0
        for ref, part in zip(refs, parts):
            if part.shape[1] == LANES:
                o_ref[0, r:r + part.shape[0], :] = ref[...]
                r += part.shape[0]
                continue
            for k in range(part.shape[0]):
                for q in range(part.shape[1] // LANES):
                    o_ref[0, r:r + 1, :] = ref[k:k + 1, q * LANES:(q + 1) * LANES]
                    r += 1
        o_ref[0, r:r + SUBLANES, :] = jnp.broadcast_to(refs[len(parts)][...], (SUBLANES, LANES))

    return pl.pallas_call(
        body,
        name="pack_small_grads",
        grid_spec=pltpu.PrefetchScalarGridSpec(
            num_scalar_prefetch=1,
            grid=(1,),
            in_specs=[pl.BlockSpec(p.shape, lambda i, s: (0, 0)) for p in parts] + [pl.BlockSpec((1, 1), lambda i, s: (0, 0))],
            out_specs=pl.BlockSpec((1, m, LANES), lambda i, s: (s[0], 0, 0)),
        ),
        out_shape=jax.ShapeDtypeStruct((8, m, LANES), F32),
        compiler_params=_params(("arbitrary",)),
    )(jnp.reshape(slot, (1,)).astype(jnp.int32), *parts, loss)


def _unpack_small(packed):
    out, r = {}, 0
    for k in _SMALL:
        shape = _SMALL_SHAPES[k]
        rows = 1
        for s in shape:
            rows *= s
        rows //= 128
        part = packed[r:r + rows]
        out[k] = _blocks_from_lanes(part) if k in ("gate_x_w", "gate_a_w") else part.reshape(shape)
        r += rows
    return out


def kernel(x, norm_in, w_in, conv_w, conv_b, gate_x_w, gate_x_b, gate_a_w, gate_a_b, lru_lambda, gn_gain, w_proj_a, w_proj_b, w_out, norm_final, loss_target, m_norm_in, m_w_in, m_conv_w, m_conv_b, m_gate_x_w, m_gate_x_b, m_gate_a_w, m_gate_a_b, m_lru_lambda, m_gn_gain, m_w_proj_a, m_w_proj_b, m_w_out, m_norm_final, v_norm_in, v_w_in, v_conv_w, v_conv_b, v_gate_x_w, v_gate_x_b, v_gate_a_w, v_gate_a_b, v_lru_lambda, v_gn_gain, v_w_proj_a, v_w_proj_b, v_w_out, v_norm_final):
    B, S, _ = x.shape
    T = B * S
    xi, yi, ci = _coords()
    chip = 2 * xi + yi

    cshard = D_MODEL // N_CHIPS
    mine = _cast_into_slot([w_in[0].reshape(2, D_MODEL // 2, 2 * D_MODEL)], "cast_w_in")
    plan = _gather_plan(3)
    pending_proj = []
    gshard = DK // N_CHIPS
    tiny = jnp.concatenate([conv_w[0], jnp.zeros((4, cshard), F32), jnp.pad(gn_gain[0], ((0, 4), (0, cshard - gshard)))],
                           axis=0).reshape(1, 2, SUBLANES, cshard)
    tiny_buf = lax.dynamic_update_slice(lax.empty((N_CHIPS, 2, SUBLANES, cshard), F32), tiny, (chip, 0, 0, 0))
    near_plan, pass_plan, far_plan = (_chip_gather_plan(stage, 2) for stage in ("near", "pass", "far"))
    (n_near, _), (n_pass, passed_on), (n_far, _) = (_chip_gather_copies(stage, 2) for stage in ("near", "pass", "far"))
    halves = set(range(n_pass)) - passed_on
    near_s, near_r, bufs, near_token = _copies_start([mine[0], tiny_buf], near_plan, n_near, "gather_near_start")

    def in_proj(x2d, g_in, meanwhile):
        as_w = lambda b: b[0].reshape(N_CHIPS, D_MODEL, 2 * D_MODEL)
        slot_x, slot_y, slot_d = 2 * (1 - xi) + yi, 2 * xi + (1 - yi), 2 * (1 - xi) + (1 - yi)
        ids = lambda *chips: jnp.stack(chips).astype(jnp.int32)
        proj, hb, ht = _inproj_first(x2d, g_in, as_w(bufs), ids(chip), "inproj_own", (near_token, *meanwhile))
        got = _copies_wait(near_s, near_r, bufs, proj, near_plan, "gather_near_wait")
        pass_s, pass_r, got, pass_token = _copies_start(got, pass_plan, n_pass, "gather_pass_start")
        mine_proj = _cast_into_slot([w[0].reshape(2, cshard // 2, D_MODEL) for w in (w_proj_a, w_proj_b, w_out)],
                                    "cast_w_proj", (pass_token,))
        got = _copies_wait(pass_s, pass_r, got, mine_proj[0], pass_plan, "gather_pass_wait_halves", only=halves)
        proj = _inproj_more(hb, as_w(got), ids(slot_x, slot_y), proj, "inproj_near")
        got = _copies_wait(pass_s, pass_r, got, proj, pass_plan, "gather_pass_wait_far", only=passed_on)
        far_s, far_r, got, far_token = _copies_start(got, far_plan, n_far, "gather_far_start")
        pending_proj.append(_copies_start(mine_proj, plan, 9, "gather_proj_start", (far_token,)))
        got = _copies_wait(far_s, far_r, got, pending_proj[0][3], far_plan, "gather_far_wait")
        proj = _inproj_more(hb, as_w(got), ids(slot_d), proj, "inproj_far")
        tiny_all = got[1].reshape(N_CHIPS, 2 * SUBLANES, cshard)
        conv_w_full = jnp.transpose(tiny_all[:, 0:CONV, :], (1, 0, 2)).reshape(CONV, D_MODEL)
        gain_full = jnp.transpose(tiny_all[:, 8:8 + HEADS, :gshard], (1, 0, 2)).reshape(HEADS, DK)
        return proj, ht, as_w(got), conv_w_full, gain_full

    def proj_weights(after):
        s_sems, r_sems, pbufs, _ = pending_proj[0]
        got = _copies_wait(s_sems, r_sems, pbufs, after, plan, "gather_proj_wait")
        return [b.reshape(D_MODEL, D_MODEL) for b in got]

    weights = dict(norm_in=norm_in, w_in=w_in, conv_w=conv_w, conv_b=conv_b, gate_x_w=gate_x_w, gate_x_b=gate_x_b,
                   gate_a_w=gate_a_w, gate_a_b=gate_a_b, lru_lambda=lru_lambda, gn_gain=gn_gain, w_proj_a=w_proj_a,
                   w_proj_b=w_proj_b, w_out=w_out, norm_final=norm_final)
    ms = dict(norm_in=m_norm_in, w_in=m_w_in, conv_w=m_conv_w, conv_b=m_conv_b, gate_x_w=m_gate_x_w,
              gate_x_b=m_gate_x_b, gate_a_w=m_gate_a_w, gate_a_b=m_gate_a_b, lru_lambda=m_lru_lambda, gn_gain=m_gn_gain,
              w_proj_a=m_w_proj_a, w_proj_b=m_w_proj_b, w_out=m_w_out, norm_final=m_norm_final)
    vs = dict(norm_in=v_norm_in, w_in=v_w_in, conv_w=v_conv_w, conv_b=v_conv_b, gate_x_w=v_gate_x_w,
              gate_x_b=v_gate_x_b, gate_a_w=v_gate_a_w, gate_a_b=v_gate_a_b, lru_lambda=v_lru_lambda, gn_gain=v_gn_gain,
              w_proj_a=v_w_proj_a, w_proj_b=v_w_proj_b, w_out=v_w_out, norm_final=v_norm_final)
    names = list(weights)
    grads, delta, new_m, new_v = {}, {}, {}, {}

    def update_big(keys, g, half, prev, name, deps=()):
        two = lambda a: a.reshape(a.shape[1], a.shape[2])
        res = _adamw_halves([two(weights[k]) for k in keys], g, [two(ms[k]) for k in keys], [two(vs[k]) for k in keys],
                            half, prev, name, deps)
        for k, (gk, d, mn, vn) in zip(keys, res):
            shp = weights[k].shape
            grads[k], delta[k], new_m[k], new_v[k] = gk.reshape(shp), d.reshape(shp), mn.reshape(shp), vn.reshape(shp)
        return res

    def proj_done(g_proj):
        g4 = g_proj.reshape(2, 3, D_MODEL // (2 * N_CHIPS), D_MODEL)
        return update_big(("w_proj_a", "w_proj_b", "w_out"), g4, None, None, "adamw_proj")[-1][1]

    def w_in_done(g_in, own, prev, deps):
        g4 = g_in.reshape(2, 1, D_MODEL // 2, 2 * D_MODEL)
        return update_big(("w_in",), g4, ci if own else 1 - ci, None if prev is None else [prev],
                          "adamw_w_in_own" if own else "adamw_w_in_other", deps)[0]

    reduce = _GradReduce(proj_done)
    grad_x, dgin = _local_grads(
        x.reshape(T, D_MODEL), loss_target.reshape(T, D_MODEL), B, S, norm_in, in_proj, conv_b,
        gate_x_w[0], gate_x_b, gate_a_w[0], gate_a_b, lru_lambda, proj_weights,
        norm_final.reshape(1, D_MODEL), reduce)

    small_sum, g_norm_in = reduce.finish(dgin.reshape(SUBLANES, LANES), w_in_done)
    loss = small_sum[small_sum.shape[0] - SUBLANES, 0]

    gsm = _unpack_small(small_sum)
    gsm["norm_in"] = g_norm_in
    gsm["conv_w"] = lax.dynamic_slice_in_dim(gsm["conv_w"], chip * cshard, cshard, axis=1)
    gsm["gn_gain"] = lax.dynamic_slice_in_dim(gsm["gn_gain"], chip * gshard, gshard, axis=1)
    smalls = [k for k in names if k not in delta]

    def view(a):
        return a.reshape(1, -1) if a.ndim == 1 else (a.reshape(a.shape[1:]) if a.ndim > 2 else a)

    ds, mns, vns = _adamw_small([view(weights[k]) for k in smalls], [gsm[k].reshape(view(weights[k]).shape) for k in smalls],
                                [view(ms[k]) for k in smalls], [view(vs[k]) for k in smalls], "adamw_small")
    for k, d, mn, vn in zip(smalls, ds, mns, vns):
        shp = weights[k].shape
        grads[k], delta[k], new_m[k], new_v[k] = gsm[k].reshape(shp), d.reshape(shp), mn.reshape(shp), vn.reshape(shp)

    return (loss, grad_x.reshape(B, S, D_MODEL), *[grads[k] for k in names], *[delta[k] for k in names],
            *[new_m[k] for k in names], *[new_v[k] for k in names])
```

```python
import jax
import jax.numpy as jnp
from jax import lax
from jax.experimental import pallas as pl
from jax.experimental.pallas import tpu as pltpu

F32 = jnp.float32
_MXU = jnp.bfloat16

D_MODEL = 1024
N_GROUPS = 8
HEADS = 4
DK = 256
CHUNK = 128
CONV = 4
LRU_BLOCKS = 16
LRU_BW = 64
LRU_C = 8.0
ROPE_THETA = 10000.0
EPS = 1e-6
CW = 256
N_CT = D_MODEL // CW
N_CHIPS = 4
MESH = pl.DeviceIdType.MESH

ADAM_LR = 0.001
ADAM_B1 = 0.9
ADAM_B2 = 0.999
ADAM_EPS = 1e-08
ADAM_WD = 0.01
ADAM_STEP = 10

VMEM_LIMIT = 56 * 1024 * 1024

FIRST_PROJ_TILE = 1024
MORE_PROJ_TILE = 2048
SCAN_TILE = 1024
MID_TILE = 256
DX_TILE = 512
DW_COLS = 512
DW_LOADS = 4
RET_CHUNKS = 2
SUM_ROWS = 256
ADAMW_ROWS = 256
JOIN_PIECES = 8
PROJ_JOIN_PIECES = 4


def _c(v):
    return v.astype(_MXU)


def _dot(a, b):
    return lax.dot_general(a, b, (((1,), (0,)), ((), ())), preferred_element_type=F32)


def _dot_nt(a, b):
    return lax.dot_general(a, b, (((1,), (1,)), ((), ())), preferred_element_type=F32)


def _dot_tn(a, b):
    return lax.dot_general(a, b, (((0,), (0,)), ((), ())), preferred_element_type=F32)


def _sigmoid(z):
    return 0.5 * jnp.tanh(0.5 * z) + 0.5


ANY_SPEC = pl.BlockSpec(memory_space=pl.ANY)


def _after(body, n_in, deps):
    n_deps = len(deps)

    def wrapped(*refs):
        return body(*refs[:n_in], *refs[n_in + n_deps:])

    return wrapped


def _params(sem=None):
    if sem is None:
        return pltpu.CompilerParams(vmem_limit_bytes=VMEM_LIMIT)
    return pltpu.CompilerParams(vmem_limit_bytes=VMEM_LIMIT, dimension_semantics=sem)


def _inproj_first(x2d, g_in, w_all, chips, name, deps=()):
    T = x2d.shape[0]
    tm = min(FIRST_PROJ_TILE, T)
    n_i = T // tm

    def body(s_ref, *refs):
        x_ref, g_ref, w_ref = refs[:3]
        proj_ref, hb_ref, ht_ref, h_all = refs[-4:]
        i = pl.program_id(1)
        rows = pl.ds(pl.multiple_of(i * tm, tm), tm)

        @pl.when(pl.program_id(0) == 0)
        def _():
            x = x_ref[...]
            r = lax.rsqrt(jnp.mean(x * x, axis=-1, keepdims=True) + EPS)
            h = x * r * g_ref[...]
            hb = h.astype(h_all.dtype)
            h_all[rows, :] = hb
            hb_ref[...] = hb
            ht_ref[...] = h.T.astype(ht_ref.dtype)

        proj_ref[...] = _dot(h_all[rows, :], w_ref[0])

    first = lambda j, i: jnp.where(j == 0, i, n_i - 1)
    return pl.pallas_call(
        body,
        name=name,
        grid_spec=pltpu.PrefetchScalarGridSpec(
            num_scalar_prefetch=1,
            grid=(2 * chips.shape[0], n_i),
            in_specs=[
                pl.BlockSpec((tm, D_MODEL), lambda j, i, s: (first(j, i), 0)),
                pl.BlockSpec((1, D_MODEL), lambda j, i, s: (0, 0)),
                pl.BlockSpec((1, D_MODEL, D_MODEL), lambda j, i, s: (s[j // 2], 0, j % 2)),
            ] + [ANY_SPEC] * len(deps),
            out_specs=[
                pl.BlockSpec((tm, D_MODEL), lambda j, i, s: (i, 2 * s[j // 2] + j % 2)),
                pl.BlockSpec((tm, D_MODEL), lambda j, i, s: (first(j, i), 0)),
                pl.BlockSpec((D_MODEL, tm), lambda j, i, s: (0, first(j, i))),
            ],
            scratch_shapes=[pltpu.VMEM((T, D_MODEL), _MXU)],
        ),
        out_shape=[
            jax.ShapeDtypeStruct((T, N_GROUPS * D_MODEL), F32),
            jax.ShapeDtypeStruct((T, D_MODEL), _MXU),
            jax.ShapeDtypeStruct((D_MODEL, T), _MXU),
        ],
        compiler_params=_params(("arbitrary", "arbitrary")),
    )(chips, x2d, g_in, w_all, *deps)


def _inproj_more(hb, w_all, chips, proj, name):
    T = hb.shape[0]
    tm = min(MORE_PROJ_TILE, T)

    def body(s_ref, hb_hbm, w_ref, prev_ref, proj_ref, h_all, sem):
        @pl.when((pl.program_id(0) == 0) & (pl.program_id(1) == 0))
        def _():
            cp = pltpu.make_async_copy(hb_hbm, h_all, sem)
            cp.start()
            cp.wait()

        rows = pl.ds(pl.multiple_of(pl.program_id(1) * tm, tm), tm)
        proj_ref[...] = _dot(h_all[rows, :], w_ref[0])

    return pl.pallas_call(
        body,
        name=name,
        grid_spec=pltpu.PrefetchScalarGridSpec(
            num_scalar_prefetch=1,
            grid=(2 * chips.shape[0], T // tm),
            in_specs=[
                ANY_SPEC,
                pl.BlockSpec((1, D_MODEL, D_MODEL), lambda j, i, s: (s[j // 2], 0, j % 2)),
                ANY_SPEC,
            ],
            out_specs=pl.BlockSpec((tm, D_MODEL), lambda j, i, s: (i, 2 * s[j // 2] + j % 2)),
            scratch_shapes=[pltpu.VMEM((T, D_MODEL), hb.dtype), pltpu.SemaphoreType.DMA],
        ),
        out_shape=jax.ShapeDtypeStruct(proj.shape, F32),
        input_output_aliases={3: 0},
        compiler_params=_params(("arbitrary", "arbitrary")),
    )(chips, hb, w_all, proj)


def _scan_fwd(a, u):
    n = a.shape[0]
    row = lax.broadcasted_iota(jnp.int32, a.shape, 0)
    s = 1
    while s < n:
        m = row >= s
        u = u + a * jnp.where(m, pltpu.roll(u, s, 0), 0.0)
        a = a * jnp.where(m, pltpu.roll(a, s, 0), 1.0)
        s *= 2
    return a, u


def _scan_bwd(b, g):
    n = b.shape[0]
    row = lax.broadcasted_iota(jnp.int32, b.shape, 0)
    s = 1
    while s < n:
        m = row < n - s
        g = g + b * jnp.where(m, pltpu.roll(g, n - s, 0), 0.0)
        b = b * jnp.where(m, pltpu.roll(b, n - s, 0), 1.0)
        s *= 2
    return b, g


LANES = 128
SUBLANES = 8


def _scan_scratch(tc):
    by_lanes = pltpu.VMEM((CW // LANES, tc, LANES), F32)
    return [by_lanes, by_lanes, pltpu.VMEM((tc // SUBLANES, CW), F32), pltpu.VMEM((tc, CW), F32)]


def _scan_tile(a, u, edge, la_ref, lh_ref, c_ref, dst_ref, reverse):
    n, w = a.shape
    groups = n // SUBLANES
    a3 = a.reshape(groups, SUBLANES, w)
    u3 = u.reshape(groups, SUBLANES, w)
    row = lax.broadcasted_iota(jnp.int32, a3.shape, 1)
    for s in (1, 2, 4):
        m = (row < SUBLANES - s) if reverse else (row >= s)
        shift = SUBLANES - s if reverse else s
        u3 = u3 + a3 * jnp.where(m, pltpu.roll(u3, shift, 1), 0.0)
        a3 = a3 * jnp.where(m, pltpu.roll(a3, shift, 1), 1.0)
    al = a3.reshape(n, w)
    hl = u3.reshape(n, w)
    blocks = w // LANES
    for q in range(blocks):
        la_ref[q] = al[:, q * LANES:(q + 1) * LANES]
        lh_ref[q] = hl[:, q * LANES:(q + 1) * LANES]
    ends = pl.ds(0 if reverse else SUBLANES - 1, groups, stride=SUBLANES)
    end_a = jnp.concatenate([la_ref.at[q][ends, :] for q in range(blocks)], axis=-1)
    end_h = jnp.concatenate([lh_ref.at[q][ends, :] for q in range(blocks)], axis=-1)
    prod, part = (_scan_bwd if reverse else _scan_fwd)(end_a, end_h)
    total = part + prod * edge
    g_row = lax.broadcasted_iota(jnp.int32, total.shape, 0)
    if reverse:
        c_ref[...] = jnp.where(g_row == groups - 1, edge, pltpu.roll(total, groups - 1, 0))
    else:
        c_ref[...] = jnp.where(g_row == 0, edge, pltpu.roll(total, 1, 0))
    for g in range(groups):
        rows = slice(g * SUBLANES, (g + 1) * SUBLANES)
        for q in range(blocks):
            cols = slice(q * LANES, (q + 1) * LANES)
            dst_ref[rows, cols] = lh_ref[q, rows, :] + la_ref[q, rows, :] * c_ref[g:g + 1, cols]


def _softplus_neg(lam):
    z = -lam
    return jnp.maximum(z, 0.0) + jnp.log1p(jnp.exp(-jnp.abs(z)))


def _lru_gates(xc, wx_ref, wa_ref, bx_ref, ba_ref, lam_ref):
    xcb = _c(xc)
    i_t = _sigmoid(_dot(xcb, wx_ref[0]) + bx_ref[...])
    r_t = _sigmoid(_dot(xcb, wa_ref[0]) + ba_ref[...])
    sp = _softplus_neg(lam_ref[...])
    log_a = (-LRU_C) * r_t * sp
    a = jnp.exp(log_a)
    mult = jnp.sqrt(1.0 - a * a)
    return xcb, i_t, r_t, sp, a, mult


def _conv_from_ext(ext_ref, xa, cw_ref, cb_ref, tc):
    return (cb_ref[...] + cw_ref[3:4, :] * xa + cw_ref[2:3, :] * ext_ref[7:7 + tc, :]
            + cw_ref[1:2, :] * ext_ref[6:6 + tc, :] + cw_ref[0:1, :] * ext_ref[5:5 + tc, :])


def _lru_fwd(proj, conv_w, conv_b, wx_bd, wa_bd, bx, ba, lam, B, S):
    T = B * S
    tc = min(SCAN_TILE, S)
    nt = S // tc
    h8 = tc // 8

    def body(xa_ref, halo_ref, ga_ref, cw_ref, cb_ref, wx_ref, wa_ref, bx_ref, ba_ref, lam_ref,
             h_ref, ya_ref, ext_ref, carry_ref, la_ref, lh_ref, c_ref):
        t = pl.program_id(2)

        @pl.when(t == 0)
        def _():
            carry_ref[...] = jnp.zeros_like(carry_ref)

        xa = xa_ref[...]
        ext_ref[0:8, :] = jnp.where(t == 0, 0.0, halo_ref[...])
        ext_ref[8:8 + tc, :] = xa
        xc = _conv_from_ext(ext_ref, xa, cw_ref, cb_ref, tc)
        _, i_t, _, _, a, mult = _lru_gates(xc, wx_ref, wa_ref, bx_ref, ba_ref, lam_ref)
        u = mult * (i_t * xc)
        _scan_tile(a, u, carry_ref[7:8, :], la_ref, lh_ref, c_ref, h_ref, False)
        h = h_ref[...]
        carry_ref[...] = h[tc - 8:tc, :]
        ga = ga_ref[...]
        ya_ref[...] = (ga * _sigmoid(ga) * h).astype(ya_ref.dtype)

    row = lambda b, t: b * nt + t
    vec = pl.BlockSpec((1, CW), lambda b, c, t: (0, c))
    mat = pl.BlockSpec((1, CW, CW), lambda b, c, t: (c, 0, 0))
    return pl.pallas_call(
        body,
        name="lru_fwd",
        grid=(B, N_CT, nt),
        in_specs=[
            pl.BlockSpec((tc, CW), lambda b, c, t: (row(b, t), c)),
            pl.BlockSpec((8, CW), lambda b, c, t: (jnp.maximum(row(b, t) * h8 - 1, 0), c)),
            pl.BlockSpec((tc, CW), lambda b, c, t: (row(b, t), N_CT + c)),
            pl.BlockSpec((CONV, CW), lambda b, c, t: (0, c)),
            vec, mat, mat, vec, vec, vec,
        ],
        out_specs=[
            pl.BlockSpec((tc, CW), lambda b, c, t: (row(b, t), c)),
            pl.BlockSpec((tc, CW), lambda b, c, t: (row(b, t), c)),
        ],
        out_shape=[
            jax.ShapeDtypeStruct((T, D_MODEL), F32),
            jax.ShapeDtypeStruct((T, D_MODEL), _MXU),
        ],
        scratch_shapes=[pltpu.VMEM((tc + 8, CW), F32), pltpu.VMEM((8, CW), F32)] + _scan_scratch(tc)[:3],
        compiler_params=_params(("parallel", "parallel", "arbitrary")),
    )(proj, proj, proj, conv_w, conv_b, wx_bd, wa_bd, bx, ba, lam)


def _lru_bwd(dya, proj, hlru, conv_w, conv_b, wx_bd, wa_bd, bx, ba, lam, B, S, deps=()):
    T = B * S
    tc = min(SCAN_TILE, S)
    nt = S // tc
    h8 = tc // 8

    def body(dya_ref, xa_ref, xhalo_ref, ga_ref, h_ref, hhalo_ref, cw_ref, cb_ref, wx_ref, wa_ref, bx_ref, ba_ref,
             lam_ref, dxa_ref, dga_ref, dcw_ref, dcb_ref, dwx_ref, dwa_ref, dbx_ref, dba_ref, dlam_ref,
             ext_ref, ext2_ref, carry_ref, dhalo_ref, la_ref, lh_ref, c_ref, dh_ref, accx_ref, acca_ref):
        b = pl.program_id(1)
        t = pl.program_id(2)
        tt = nt - 1 - t

        @pl.when(t == 0)
        def _():
            carry_ref[...] = jnp.zeros_like(carry_ref)
            dhalo_ref[...] = jnp.zeros_like(dhalo_ref)

        @pl.when((t == 0) & (b == 0))
        def _():
            for r in (dcw_ref, dcb_ref, accx_ref, acca_ref, dbx_ref, dba_ref, dlam_ref):
                r[...] = jnp.zeros_like(r)

        xa = xa_ref[...]
        ext_ref[0:8, :] = jnp.where(tt == 0, 0.0, xhalo_ref[...])
        ext_ref[8:8 + tc, :] = xa
        xc = _conv_from_ext(ext_ref, xa, cw_ref, cb_ref, tc)
        xcb, i_t, r_t, sp, a, mult = _lru_gates(xc, wx_ref, wa_ref, bx_ref, ba_ref, lam_ref)

        h = h_ref[...]
        ga = ga_ref[...]
        dya_t = dya_ref[...]
        sg = _sigmoid(ga)
        dga_ref[...] = (dya_t * h * (sg * (1.0 + ga * (1.0 - sg)))).astype(dga_ref.dtype)
        dlru = dya_t * (ga * sg)

        row = lax.broadcasted_iota(jnp.int32, a.shape, 0)
        coef = jnp.where(row == tc - 1, 1.0, pltpu.roll(a, tc - 1, 0))
        _scan_tile(coef, dlru, carry_ref[0:1, :], la_ref, lh_ref, c_ref, dh_ref, True)
        dh = dh_ref[...]
        ext2_ref[0:tc, :] = a * dh
        carry_ref[...] = ext2_ref[0:8, :]

        ext2_ref[0:8, :] = jnp.where(tt == 0, 0.0, hhalo_ref[...])
        ext2_ref[8:8 + tc, :] = h
        hprev = ext2_ref[7:7 + tc, :]

        da = dh * hprev
        ix = i_t * xc
        dmult = dh * ix
        di = dh * mult * xc
        dxc = dh * mult * i_t
        dlog_a = da * a - dmult * (a * a) / mult
        dr = dlog_a * ((-LRU_C) * sp)
        dlam_ref[...] += jnp.sum(dlog_a * r_t, axis=0, keepdims=True) * (LRU_C * _sigmoid(-lam_ref[...]))
        dza = dr * r_t * (1.0 - r_t)
        dzx = di * i_t * (1.0 - i_t)
        dzab = _c(dza)
        dzxb = _c(dzx)
        dxc = dxc + _dot_nt(dzxb, wx_ref[0]) + _dot_nt(dzab, wa_ref[0])
        accx_ref[...] += _dot_tn(xcb, dzxb)
        acca_ref[...] += _dot_tn(xcb, dzab)
        dbx_ref[...] += jnp.sum(dzx, axis=0, keepdims=True)
        dba_ref[...] += jnp.sum(dza, axis=0, keepdims=True)

        dcb_ref[...] += jnp.sum(dxc, axis=0, keepdims=True)
        dcw_ref[3:4, :] += jnp.sum(dxc * xa, axis=0, keepdims=True)
        dcw_ref[2:3, :] += jnp.sum(dxc * ext_ref[7:7 + tc, :], axis=0, keepdims=True)
        dcw_ref[1:2, :] += jnp.sum(dxc * ext_ref[6:6 + tc, :], axis=0, keepdims=True)
        dcw_ref[0:1, :] += jnp.sum(dxc * ext_ref[5:5 + tc, :], axis=0, keepdims=True)
        ext2_ref[0:tc, :] = dxc
        ext2_ref[tc:tc + 8, :] = dhalo_ref[...]
        dxa = (cw_ref[3:4, :] * dxc + cw_ref[2:3, :] * ext2_ref[1:1 + tc, :]
               + cw_ref[1:2, :] * ext2_ref[2:2 + tc, :] + cw_ref[0:1, :] * ext2_ref[3:3 + tc, :])
        dxa_ref[...] = dxa.astype(dxa_ref.dtype)
        dhalo_ref[...] = ext2_ref[0:8, :]

        @pl.when((b == B - 1) & (t == nt - 1))
        def _():
            lane_block = lax.broadcasted_iota(jnp.int32, (LRU_BW, CW), 1) // LRU_BW
            for acc_ref, out_ref in ((accx_ref, dwx_ref), (acca_ref, dwa_ref)):
                diag = jnp.zeros((LRU_BW, CW), F32)
                for j in range(CW // LRU_BW):
                    diag = jnp.where(lane_block == j, acc_ref[j * LRU_BW:(j + 1) * LRU_BW, :], diag)
                for q in range(CW // LANES):
                    out_ref[0, q] = diag[:, q * LANES:(q + 1) * LANES]

    row_of = lambda b, t: b * nt + (nt - 1 - t)
    tile = lambda off: pl.BlockSpec((tc, CW), lambda c, b, t: (row_of(b, t), off + c))
    halo = pl.BlockSpec((8, CW), lambda c, b, t: (jnp.maximum(row_of(b, t) * h8 - 1, 0), c))
    vec = pl.BlockSpec((1, CW), lambda c, b, t: (0, c))
    mat = pl.BlockSpec((1, CW, CW), lambda c, b, t: (c, 0, 0))
    cwspec = pl.BlockSpec((CONV, CW), lambda c, b, t: (0, c))
    diag = pl.BlockSpec((1, CW // LANES, LRU_BW, LANES), lambda c, b, t: (c, 0, 0, 0))
    return pl.pallas_call(
        _after(body, 13, deps),
        name="lru_bwd",
        grid=(N_CT, B, nt),
        in_specs=[tile(0), tile(0), halo, tile(N_CT), tile(0), halo, cwspec, vec, mat, mat, vec, vec, vec]
        + [ANY_SPEC] * len(deps),
        out_specs=[tile(0), tile(0), cwspec, vec, diag, diag, vec, vec, vec],
        out_shape=[
            jax.ShapeDtypeStruct((T, D_MODEL), _MXU),
            jax.ShapeDtypeStruct((T, D_MODEL), _MXU),
            jax.ShapeDtypeStruct((CONV, D_MODEL), F32),
            jax.ShapeDtypeStruct((1, D_MODEL), F32),
            jax.ShapeDtypeStruct((N_CT, CW // LANES, LRU_BW, LANES), F32),
            jax.ShapeDtypeStruct((N_CT, CW // LANES, LRU_BW, LANES), F32),
            jax.ShapeDtypeStruct((1, D_MODEL), F32),
            jax.ShapeDtypeStruct((1, D_MODEL), F32),
            jax.ShapeDtypeStruct((1, D_MODEL), F32),
        ],
        scratch_shapes=[pltpu.VMEM((tc + 8, CW), F32), pltpu.VMEM((tc + 8, CW), F32),
                        pltpu.VMEM((8, CW), F32), pltpu.VMEM((8, CW), F32)] + _scan_scratch(tc)
        + [pltpu.VMEM((CW, CW), F32), pltpu.VMEM((CW, CW), F32)],
        compiler_params=_params(("parallel", "arbitrary", "arbitrary")),
    )(dya, proj, proj, proj, hlru, hlru, conv_w, conv_b, wx_bd, wa_bd, bx, ba, lam, *deps)


def _retention_tables(S):
    half = DK // 2
    freqs = ROPE_THETA ** (-jnp.arange(half, dtype=F32) / half)
    ang = jnp.arange(S, dtype=F32)[:, None] * freqs[None, :]
    log_g = jnp.log1p(-(2.0 ** (-5.0 - jnp.arange(HEADS, dtype=F32))))
    idx = jnp.arange(CHUNK, dtype=F32)
    diff = idx[:, None] - idx[None, :]
    inner = jnp.where(diff >= 0, jnp.exp(jnp.maximum(diff, 0.0)[None] * log_g[:, None, None]), 0.0)
    cross = jnp.exp((idx[None, :] + 1.0) * log_g[:, None])[:, :, None]
    state = jnp.exp((CHUNK - 1.0 - idx[None, :]) * log_g[:, None])[:, :, None]
    gam = jnp.broadcast_to(jnp.exp(CHUNK * log_g)[:, None, None], (HEADS, 1, DK))
    return jnp.cos(ang), jnp.sin(ang), inner, cross, state, gam


def _rot(x, cos, sin):
    half = DK // 2
    x1, x2 = x[:, :half], x[:, half:]
    return jnp.concatenate([x1 * cos - x2 * sin, x1 * sin + x2 * cos], axis=-1)


def _rot_t(y, cos, sin):
    half = DK // 2
    y1, y2 = y[:, :half], y[:, half:]
    return jnp.concatenate([y1 * cos + y2 * sin, y2 * cos - y1 * sin], axis=-1)


def _groupnorm(o):
    mu = jnp.mean(o, axis=-1, keepdims=True)
    oc = o - mu
    rs = lax.rsqrt(jnp.mean(oc * oc, axis=-1, keepdims=True) + EPS)
    return oc * rs, rs


def _ret_specs(B, chunk_of):
    rows = RET_CHUNKS * CHUNK
    qkv = lambda g: pl.BlockSpec((B, rows, D_MODEL), lambda c: (0, chunk_of(c), g))
    act = pl.BlockSpec((B, rows, D_MODEL), lambda c: (0, chunk_of(c), 0))
    rope = pl.BlockSpec((rows, DK // 2), lambda c: (chunk_of(c), 0))
    dmat = pl.BlockSpec((HEADS, CHUNK, CHUNK), lambda c: (0, 0, 0))
    dvec = pl.BlockSpec((HEADS, CHUNK, 1), lambda c: (0, 0, 0))
    hrow = pl.BlockSpec((HEADS, 1, DK), lambda c: (0, 0, 0))
    rst = pl.BlockSpec((RET_CHUNKS, B, HEADS, DK, DK), lambda c: (chunk_of(c), 0, 0, 0, 0))
    return qkv, act, rope, dmat, dvec, hrow, rst


def _ret_fwd(proj, tables, gain3, B, S):
    T = B * S
    nc = S // CHUNK
    cos, sin, dmat_t, cd_t, sd_t, gam_t = tables

    def body(q_ref, k_ref, v_ref, gb_ref, cos_ref, sin_ref, dm_ref, cd_ref, sd_ref, gam_ref, gain_ref,
             o_ref, yb_ref, rs_ref, state_ref):
        @pl.when(pl.program_id(0) == 0)
        def _():
            state_ref[...] = jnp.zeros_like(state_ref)

        for cc, b, h in [(cc, b, h) for cc in range(RET_CHUNKS) for b in range(B) for h in range(HEADS)]:
            rows = slice(cc * CHUNK, (cc + 1) * CHUNK)
            cos_t, sin_t = cos_ref[rows, :], sin_ref[rows, :]
            cols = slice(h * DK, (h + 1) * DK)
            qb = _c(_rot(q_ref[b, rows, cols], cos_t, sin_t))
            kb = _c(_rot(k_ref[b, rows, cols], cos_t, sin_t) * (DK ** -0.5))
            v = v_ref[b, rows, cols]
            state = state_ref[b, h]
            sb = _c(state)
            rs_ref[cc, b, h] = sb
            scores = _dot_nt(qb, kb) * dm_ref[h]
            o = _dot(_c(scores), _c(v)) + _dot(qb, sb) * cd_ref[h]
            state_ref[b, h] = gam_ref[h] * state + _dot_tn(kb, _c(v * sd_ref[h]))
            o_ref[b, rows, cols] = o
            n, _ = _groupnorm(o)
            gb = gb_ref[b, rows, cols]
            yb_ref[b, rows, cols] = (gb * _sigmoid(gb) * (n * gain_ref[h])).astype(yb_ref.dtype)

    qkv, act, rope, dmat, dvec, hrow, rst = _ret_specs(B, lambda c: c)
    proj3 = proj.reshape(B, S, proj.shape[1])
    o_pre, yb, states = pl.pallas_call(
        body,
        name="ret_fwd",
        grid=(nc // RET_CHUNKS,),
        in_specs=[qkv(2), qkv(3), qkv(4), qkv(5), rope, rope, dmat, dvec, dvec, hrow, hrow],
        out_specs=[act, act, rst],
        out_shape=[
            jax.ShapeDtypeStruct((B, S, D_MODEL), F32),
            jax.ShapeDtypeStruct((B, S, D_MODEL), _MXU),
            jax.ShapeDtypeStruct((nc, B, HEADS, DK, DK), _MXU),
        ],
        scratch_shapes=[pltpu.VMEM((B, HEADS, DK, DK), F32)],
        compiler_params=_params(("arbitrary",)),
    )(proj3, proj3, proj3, proj3, cos, sin, dmat_t, cd_t, sd_t, gam_t, gain3)
    return o_pre.reshape(T, D_MODEL), yb.reshape(T, D_MODEL), states


def _ret_bwd(dyb, o_pre, proj, states, tables, gain3, B, S, deps=()):
    T = B * S
    nc = S // CHUNK
    cos, sin, dmat_t, cd_t, sd_t, gam_t = tables

    def body(dyb_ref, o_ref, q_ref, k_ref, v_ref, gb_ref, rs_ref, cos_ref, sin_ref, dm_ref, cd_ref, sd_ref, gam_ref,
             gain_ref, dr_ref, dgain_ref, dstate_ref):
        @pl.when(pl.program_id(0) == 0)
        def _():
            dstate_ref[...] = jnp.zeros_like(dstate_ref)
            dgain_ref[...] = jnp.zeros_like(dgain_ref)

        for cc, b, h in [(cc, b, h) for cc in reversed(range(RET_CHUNKS)) for b in range(B) for h in range(HEADS)]:
            rows = slice(cc * CHUNK, (cc + 1) * CHUNK)
            cos_t, sin_t = cos_ref[rows, :], sin_ref[rows, :]
            cols = slice(h * DK, (h + 1) * DK)
            gain = gain_ref[h]
            n, rs = _groupnorm(o_ref[b, rows, cols])
            gb = gb_ref[b, rows, cols]
            sg = _sigmoid(gb)
            dy = dyb_ref[b, rows, cols]
            part = lambda g: slice(g * D_MODEL + h * DK, g * D_MODEL + (h + 1) * DK)
            dr_ref[b, rows, part(3)] = (dy * (n * gain) * (sg * (1.0 + gb * (1.0 - sg)))).astype(dr_ref.dtype)
            dgn = dy * (gb * sg)
            dgain_ref[h] += jnp.sum(dgn * n, axis=0, keepdims=True)
            dn = dgn * gain
            do = rs * (dn - jnp.mean(dn, axis=-1, keepdims=True) - n * jnp.mean(dn * n, axis=-1, keepdims=True))

            qb = _c(_rot(q_ref[b, rows, cols], cos_t, sin_t))
            kb = _c(_rot(k_ref[b, rows, cols], cos_t, sin_t) * (DK ** -0.5))
            v = v_ref[b, rows, cols]
            vb = _c(v)
            vsb = _c(v * sd_ref[h])
            dob = _c(do)
            docb = _c(do * cd_ref[h])
            dmat = dm_ref[h]
            dstate = dstate_ref[b, h]
            dsb = _c(dstate)
            pb = _c(_dot_nt(qb, kb) * dmat)
            dsc = _c(_dot_nt(dob, vb) * dmat)
            dq = _dot(dsc, kb) + _dot_nt(docb, rs_ref[cc, b, h])
            dk = _dot_tn(dsc, qb) + _dot_nt(vsb, dsb)
            dv = _dot_tn(pb, dob) + _dot(kb, dsb) * sd_ref[h]
            dstate_ref[b, h] = gam_ref[h] * dstate + _dot_tn(qb, docb)
            dr_ref[b, rows, part(0)] = _rot_t(dq, cos_t, sin_t).astype(dr_ref.dtype)
            dr_ref[b, rows, part(1)] = (_rot_t(dk, cos_t, sin_t) * (DK ** -0.5)).astype(dr_ref.dtype)
            dr_ref[b, rows, part(2)] = dv.astype(dr_ref.dtype)

    n_steps = nc // RET_CHUNKS
    qkv, act, rope, dmat, dvec, hrow, rst = _ret_specs(B, lambda c: n_steps - 1 - c)
    wide = pl.BlockSpec((B, RET_CHUNKS * CHUNK, 4 * D_MODEL), lambda c: (0, n_steps - 1 - c, 0))
    proj3 = proj.reshape(B, S, proj.shape[1])
    dr, dgain = pl.pallas_call(
        _after(body, 14, deps),
        name="ret_bwd",
        grid=(n_steps,),
        in_specs=[act, act, qkv(2), qkv(3), qkv(4), qkv(5), rst, rope, rope, dmat, dvec, dvec, hrow, hrow]
        + [ANY_SPEC] * len(deps),
        out_specs=[wide, hrow],
        out_shape=[jax.ShapeDtypeStruct((B, S, 4 * D_MODEL), _MXU), jax.ShapeDtypeStruct((HEADS, 1, DK), F32)],
        scratch_shapes=[pltpu.VMEM((B, HEADS, DK, DK), F32)],
        compiler_params=_params(("arbitrary",)),
    )(dyb.reshape(B, S, D_MODEL), o_pre.reshape(B, S, D_MODEL), proj3, proj3, proj3, proj3, states, cos, sin, dmat_t,
      cd_t, sd_t, gam_t, gain3, *deps)
    return dr.reshape(T, 4 * D_MODEL), dgain


def _mid(ya, yb, proj, x2d, tgt2d, wpa, wpb, wout, g_fin):
    T = x2d.shape[0]
    tm = min(MID_TILE, T)
    n_steps = T // tm
    rows = D_MODEL // (2 * N_CHIPS)

    def body(ya_ref, yb_ref, ma_ref, mb_ref, x_ref, t_ref, gf_ref, wpa_hbm, wpb_hbm, wout_hbm,
             loss_ref, dx2_ref, dya_ref, dyb_ref, dm_ref, dgf_ref, gw_hbm, w_ref, acc_ref, sem):
        i = pl.program_id(0)

        @pl.when(i == 0)
        def _():
            loads = [pltpu.make_async_copy(src, w_ref.at[k], sem.at[k]) for k, src in enumerate((wpa_hbm, wpb_hbm, wout_hbm))]
            for cp in loads:
                cp.start()
            for cp in loads:
                cp.wait()
            acc_ref[...] = jnp.zeros_like(acc_ref)
            loss_ref[...] = jnp.zeros_like(loss_ref)
            dgf_ref[...] = jnp.zeros_like(dgf_ref)

        ya_t, yb_t = ya_ref[...], yb_ref[...]
        out_a = _dot(ya_t, w_ref[0])
        out_b = _dot(yb_t, w_ref[1])
        sa = _sigmoid(ma_ref[...])
        sb = _sigmoid(mb_ref[...])
        mgb = _c(sa * out_a + sb * out_b)
        x2 = x_ref[...] + _dot(mgb, w_ref[2])
        r2 = lax.rsqrt(jnp.mean(x2 * x2, axis=-1, keepdims=True) + EPS)
        nx = x2 * r2
        gf = gf_ref[...]
        err = nx * gf - t_ref[...]
        loss_ref[...] += 0.5 * jnp.sum(jnp.mean(err * err, axis=-1, keepdims=True), axis=0, keepdims=True)
        dy = err * (1.0 / D_MODEL)
        dgf_ref[...] += jnp.sum(dy * nx, axis=0, keepdims=True)
        dyg = dy * gf
        dx2 = r2 * (dyg - nx * jnp.mean(dyg * nx, axis=-1, keepdims=True))
        dx2_ref[...] = dx2
        dx2b = _c(dx2)
        dmg = _dot_nt(dx2b, w_ref[2])
        acc_ref[2] += _dot_tn(mgb, dx2b)
        dm_ref[:, :D_MODEL] = (dmg * out_a * sa * (1.0 - sa)).astype(dm_ref.dtype)
        dm_ref[:, D_MODEL:] = (dmg * out_b * sb * (1.0 - sb)).astype(dm_ref.dtype)
        dab = _c(dmg * sa)
        dbb = _c(dmg * sb)
        dya_ref[...] = _dot_nt(dab, w_ref[0])
        dyb_ref[...] = _dot_nt(dbb, w_ref[1])
        acc_ref[0] += _dot_tn(ya_t, dab)
        acc_ref[1] += _dot_tn(yb_t, dbb)

        @pl.when(i == n_steps - 1)
        def _():
            copies = [pltpu.make_async_copy(acc_ref.at[k, pl.ds((2 * p + hf) * rows, rows), :], gw_hbm.at[p, hf, k],
                                            sem.at[(k * N_CHIPS + p) * 2 + hf])
                      for k in range(3) for p in range(N_CHIPS) for hf in range(2)]
            for cp in copies:
                cp.start()
            for cp in copies:
                cp.wait()

    tile = lambda j: pl.BlockSpec((tm, D_MODEL), lambda i: (i, j))
    one = pl.BlockSpec((1, D_MODEL), lambda i: (0, 0))
    anyspec = pl.BlockSpec(memory_space=pl.ANY)
    return pl.pallas_call(
        body,
        name="mid",
        grid=(n_steps,),
        in_specs=[tile(0), tile(0), tile(6), tile(7), tile(0), tile(0), one, anyspec, anyspec, anyspec],
        out_specs=[pl.BlockSpec((1, 1), lambda i: (0, 0)), tile(0), tile(0), tile(0),
                   pl.BlockSpec((tm, 2 * D_MODEL), lambda i: (i, 0)), one, anyspec],
        out_shape=[
            jax.ShapeDtypeStruct((1, 1), F32),
            jax.ShapeDtypeStruct((T, D_MODEL), F32),
            jax.ShapeDtypeStruct((T, D_MODEL), F32),
            jax.ShapeDtypeStruct((T, D_MODEL), F32),
            jax.ShapeDtypeStruct((T, 2 * D_MODEL), _MXU),
            jax.ShapeDtypeStruct((1, D_MODEL), F32),
            jax.ShapeDtypeStruct((N_CHIPS, 2, 3, rows, D_MODEL), F32),
        ],
        scratch_shapes=[pltpu.VMEM((3, D_MODEL, D_MODEL), _MXU), pltpu.VMEM((3, D_MODEL, D_MODEL), F32),
                        pltpu.SemaphoreType.DMA((3 * N_CHIPS * 2,))],
        compiler_params=_params(("arbitrary",)),
    )(ya, yb, proj, proj, x2d, tgt2d, g_fin, wpa, wpb, wout)


def _inproj_bwd_dx(dparts, w_all, x2d, dx2, g_in, first, count, prev, name, deps=()):
    T = x2d.shape[0]
    tm = min(DX_TILE, T)
    n_d = len(dparts)
    groups = [(a, k) for a, d in enumerate(dparts) for k in range(d.shape[1] // D_MODEL)]
    dg_start = jnp.zeros((1, D_MODEL), F32) if prev is None else prev[1]
    carried = () if prev is None else (prev[0],)

    def body(*refs):
        d_refs = refs[:n_d]
        x_ref, dx2_ref, g_ref, dg0_ref, w_hbm = refs[n_d:n_d + 5]
        dx_ref, dg_ref, w_ref, sem = refs[-4:]

        def load(j):
            part = (j // 2, slice(None), pl.ds((j % 2) * D_MODEL, D_MODEL))
            return pltpu.make_async_copy(w_hbm.at[part], w_ref.at[part], sem.at[j])

        def tile(before_group):
            dh = jnp.zeros((tm, D_MODEL), F32)
            for j, (a, k) in enumerate(groups):
                before_group(j)
                dh = dh + _dot_nt(d_refs[a][:, k * D_MODEL:(k + 1) * D_MODEL],
                                  w_ref[j // 2, :, (j % 2) * D_MODEL:(j % 2 + 1) * D_MODEL])
            x = x_ref[...]
            r = lax.rsqrt(jnp.mean(x * x, axis=-1, keepdims=True) + EPS)
            nx = x * r
            dg_ref[...] += jnp.sum(dh * nx, axis=0, keepdims=True)
            dhg = dh * g_ref[...]
            dx_ref[...] = dx2_ref[...] + r * (dhg - nx * jnp.mean(dhg * nx, axis=-1, keepdims=True))

        first = pl.program_id(0) == 0

        @pl.when(first)
        def _():
            for j in range(len(groups)):
                load(j).start()
            dg_ref[...] = dg0_ref[...]
            tile(lambda j: load(j).wait())

        @pl.when(jnp.logical_not(first))
        def _():
            tile(lambda j: None)

    tile = pl.BlockSpec((tm, D_MODEL), lambda i: (first + i, 0))
    one = pl.BlockSpec((1, D_MODEL), lambda i: (0, 0))
    return pl.pallas_call(
        body,
        name=name,
        grid=(count,),
        in_specs=[pl.BlockSpec((tm, d.shape[1]), lambda i: (first + i, 0)) for d in dparts]
        + [tile, tile, one, one, ANY_SPEC] + [ANY_SPEC] * (len(carried) + len(deps)),
        out_specs=[tile, one],
        out_shape=[jax.ShapeDtypeStruct((T, D_MODEL), F32), jax.ShapeDtypeStruct((1, D_MODEL), F32)],
        input_output_aliases={n_d + 5: 0} if carried else {},
        scratch_shapes=[pltpu.VMEM(w_all.shape, w_all.dtype), pltpu.SemaphoreType.DMA((len(groups),))],
        compiler_params=_params(("arbitrary",)),
    )(*dparts, x2d, dx2, g_in, dg_start, w_all, *carried, *deps)


def _inproj_bwd_dw(ht, dparts, name, deps=()):
    T = ht.shape[1]
    tn = DW_COLS
    half = D_MODEL // 2
    per_chip = 2 * D_MODEL // tn
    n_d = len(dparts)
    tiles = [(a, t) for a, d in enumerate(dparts) for t in range(d.shape[1] // tn)]
    offs = [sum(d.shape[1] // tn for d in dparts[:a]) for a in range(n_d)]

    def body(*refs):
        ht_hbm = refs[0]
        d_refs = refs[1:1 + n_d]
        out_ref, ht_ref, sem = refs[-3:]
        t = pl.program_id(0)

        def load(k):
            cols = pl.ds(k * (T // DW_LOADS), T // DW_LOADS)
            return pltpu.make_async_copy(ht_hbm.at[:, cols], ht_ref.at[:, cols], sem.at[k])

        def store(g):
            out_ref[0, 0] = g[:half]
            out_ref[0, 1] = g[half:]

        @pl.when(t == 0)
        def _():
            for k in range(DW_LOADS):
                load(k).start()
            g = jnp.zeros((D_MODEL, tn), F32)
            for k in range(DW_LOADS):
                load(k).wait()
                tokens = slice(k * (T // DW_LOADS), (k + 1) * (T // DW_LOADS))
                g = g + _dot(ht_ref[:, tokens], d_refs[0][tokens, :])
            store(g)

        for a in range(n_d):
            lo, hi = max(offs[a], 1), offs[a] + dparts[a].shape[1] // tn

            @pl.when((t >= lo) & (t < hi))
            def _(a=a):
                store(_dot(ht_ref[...], d_refs[a][...]))

    def dspec(a):
        n_a = dparts[a].shape[1] // tn
        return pl.BlockSpec((T, tn), lambda t: (0, jnp.clip(t - offs[a], 0, n_a - 1)))

    return pl.pallas_call(
        body,
        name=name,
        grid=(len(tiles),),
        in_specs=[ANY_SPEC] + [dspec(a) for a in range(n_d)] + [ANY_SPEC] * len(deps),
        out_specs=pl.BlockSpec((1, 2, half, tn), lambda t: (t // per_chip, 0, 0, t % per_chip)),
        out_shape=jax.ShapeDtypeStruct((len(tiles) // per_chip, 2, half, 2 * D_MODEL), F32),
        scratch_shapes=[pltpu.VMEM(ht.shape, ht.dtype), pltpu.SemaphoreType.DMA((DW_LOADS,))],
        compiler_params=_params(("arbitrary",)),
    )(ht, *dparts, *deps)


def _coords():
    return lax.axis_index("x"), lax.axis_index("y"), lax.axis_index("c")


def _other_chips(x, y):
    return [(1 - x, y), (x, 1 - y), (1 - x, 1 - y)]


def _chunks(rows, n):
    size = rows // n
    return [pl.ds(q * size, size) for q in range(n)]


HBM_SPEC = pl.BlockSpec(memory_space=pltpu.HBM)
SEM_SPEC = pl.BlockSpec(memory_space=pltpu.SEMAPHORE)
DATAFLOW = pltpu.SideEffectType.DATAFLOW_SIDE_EFFECTING


def _copies_start(bufs, plan, n_copies, name, deps=()):
    n = len(bufs)
    n_deps = len(deps)

    def body(*refs):
        ins = refs[:n]
        send_sems, recv_sems = refs[n + n_deps], refs[n + n_deps + 1]
        token = refs[-1]
        for k, send, _ in plan(ins):
            if send is not None:
                src, dst, dev, pred = send
                cp = pltpu.make_async_remote_copy(src_ref=src, dst_ref=dst, send_sem=send_sems.at[k],
                                                  recv_sem=recv_sems.at[k], device_id=dev, device_id_type=MESH)
                if pred is None:
                    cp.start()
                else:
                    pl.when(pred)(cp.start)
        token[...] = jnp.zeros_like(token)

    hbm = [pltpu.with_memory_space_constraint(b, pltpu.HBM) for b in bufs]
    outs = pl.pallas_call(
        body,
        name=name,
        in_specs=[HBM_SPEC] * n + [ANY_SPEC] * n_deps,
        out_specs=(SEM_SPEC, SEM_SPEC, *([HBM_SPEC] * n), pl.BlockSpec(memory_space=pltpu.VMEM)),
        out_shape=(pltpu.SemaphoreType.DMA((n_copies,)), pltpu.SemaphoreType.DMA((n_copies,)),
                   *[pltpu.HBM(b.shape, b.dtype) for b in bufs], jax.ShapeDtypeStruct((8, 128), F32)),
        input_output_aliases={a: 2 + a for a in range(n)},
        compiler_params=pltpu.CompilerParams(has_side_effects=DATAFLOW),
    )(*hbm, *deps)
    return outs[0], outs[1], list(outs[2:2 + n]), outs[-1]


def _copies_wait(send_sems, recv_sems, bufs, after, plan, name, only=None):
    n = len(bufs)

    def body(*refs):
        ins = refs[:n]
        s_sems, r_sems = refs[n], refs[n + 1]
        for k, send, recv in plan(ins):
            if only is not None and k not in only:
                continue
            if send is not None:
                src, dst, dev, pred = send
                cp = pltpu.make_async_remote_copy(src_ref=src, dst_ref=dst, send_sem=s_sems.at[k],
                                                  recv_sem=r_sems.at[k], device_id=dev, device_id_type=MESH)
                if pred is None:
                    cp.wait_send()
                else:
                    pl.when(pred)(cp.wait_send)
            if recv is not None:
                dst, pred = recv
                cp = pltpu.make_async_remote_copy(src_ref=dst, dst_ref=dst, send_sem=s_sems.at[k],
                                                  recv_sem=r_sems.at[k], device_id=_coords(), device_id_type=MESH)
                if pred is None:
                    cp.wait_recv()
                else:
                    pl.when(pred)(cp.wait_recv)

    outs = pl.pallas_call(
        body,
        name=name,
        in_specs=[HBM_SPEC] * n + [SEM_SPEC, SEM_SPEC, pl.BlockSpec(memory_space=pl.ANY)],
        out_specs=[HBM_SPEC] * n,
        out_shape=[pltpu.HBM(b.shape, b.dtype) for b in bufs],
        input_output_aliases={a: a for a in range(n)},
        compiler_params=pltpu.CompilerParams(has_side_effects=DATAFLOW),
    )(*bufs, send_sems, recv_sems, after)
    return list(outs)


def _gather_plan(n_bufs):
    def plan(refs):
        x, y, c = _coords()
        me = 2 * x + y
        out = []
        for k, (px, py) in enumerate(_other_chips(x, y)):
            for a in range(n_bufs):
                out.append((k * n_bufs + a, (refs[a].at[me], refs[a].at[me], (px, py, c), None),
                            (refs[a].at[2 * px + py], None)))
        return out
    return plan


def _cast_into_slot(ws, name, deps=()):
    n = len(ws)
    nt = 2

    def body(s_ref, *refs):
        outs = refs[len(refs) - n:]
        for a in range(n):
            outs[a][0] = refs[a][...].astype(outs[a].dtype)

    xi, yi, _ = _coords()
    return pl.pallas_call(
        body,
        name=name,
        grid_spec=pltpu.PrefetchScalarGridSpec(
            num_scalar_prefetch=1,
            grid=(2, nt),
            in_specs=[pl.BlockSpec((1, w.shape[1] // nt, w.shape[2]), lambda hf, i, s: (hf, i, 0)) for w in ws]
            + [ANY_SPEC] * len(deps),
            out_specs=[pl.BlockSpec((1, 1, w.shape[1] // nt, w.shape[2]), lambda hf, i, s: (s[0], hf, i, 0)) for w in ws],
        ),
        out_shape=[jax.ShapeDtypeStruct((N_CHIPS,) + w.shape, _MXU) for w in ws],
        compiler_params=_params(("parallel", "parallel")),
    )((2 * xi + yi).reshape(1).astype(jnp.int32), *ws, *deps)


def _chip_gather_plan(stage, n_bufs):
    def plan(refs):
        x, y, c = _coords()
        me = 2 * x + y
        near = [(1 - x, y), (x, 1 - y)]
        slots = [2 * (1 - x) + y, 2 * x + (1 - y), 2 * (1 - x) + (1 - y)]
        sibling = (x, y, 1 - c)
        pass_to = (jnp.where(c == 0, x, 1 - x), jnp.where(c == 0, 1 - y, y), c)
        pass_slot = jnp.where(c == 0, slots[0], slots[1])
        out = []

        def move(src_slot, to, land_slot, land_core, pieces):
            for a, buf in enumerate(refs):
                for rows in _chunks(buf.shape[2], pieces[a]):
                    out.append((len(out), (buf.at[src_slot, c, rows], buf.at[src_slot, c, rows], to, None),
                                (buf.at[land_slot, land_core, rows], None)))

        if stage == "near":
            for k, chip in enumerate(near):
                move(me, (*chip, c), slots[k], c, NEAR_PIECES[:n_bufs])
        elif stage == "pass":
            move(pass_slot, pass_to, slots[2], c, PASS_PIECES[:n_bufs])
            for k in range(2):
                move(slots[k], sibling, slots[k], 1 - c, [1] * n_bufs)
        else:
            move(slots[2], sibling, slots[2], 1 - c, [1] * n_bufs)
        return out
    return plan


NEAR_PIECES = (2, 1)
PASS_PIECES = (2, 1)


def _chip_gather_copies(stage, n_bufs):
    if stage == "near":
        return 2 * sum(NEAR_PIECES[:n_bufs]), None
    if stage == "pass":
        n_pass = sum(PASS_PIECES[:n_bufs])
        return n_pass + 2 * n_bufs, set(range(n_pass))
    return n_bufs, None


def _swap_plan(n_slabs):
    def plan(refs):
        x, y, c = _coords()
        out, k = [], 0
        for i, n in enumerate(n_slabs):
            g, land = refs[2 * i], refs[2 * i + 1]
            for p in range(n):
                out.append((k, (g.at[p, 1 - c], land.at[p], (x, y, 1 - c), None), (land.at[p], None)))
                k += 1
        return out
    return plan


def _is_one_of(chip, dests):
    hit = chip == dests[0]
    for d in dests[1:]:
        hit = hit | (chip == d)
    return hit


def _slab_of(chip, dests):
    return sum(j * (chip == d).astype(jnp.int32) for j, d in enumerate(dests))


def _scatter_plan(dest_sets):
    def plan(refs):
        x, y, c = _coords()
        me = 2 * x + y
        out = []
        for k, (px, py) in enumerate(_other_chips(x, y)):
            peer = 2 * px + py
            for i, dests in enumerate(dest_sets):
                cs, land = refs[2 * i], refs[2 * i + 1]
                everyone = len(dests) == N_CHIPS
                send = (cs.at[_slab_of(peer, dests)], land.at[k], (px, py, c),
                        None if everyone else _is_one_of(peer, dests))
                recv = (land.at[k], None if everyone else _is_one_of(me, dests))
                out.append((k * len(dest_sets) + i, send, recv))
        return out
    return plan


def _join_plan(rows, n_pieces):
    def plan(refs):
        x, y, c = _coords()
        (buf,) = refs
        return [(i, (buf.at[c, piece], buf.at[c, piece], (x, y, 1 - c), None), (buf.at[1 - c, piece], None))
                for i, piece in enumerate(_chunks(rows, n_pieces))]
    return plan


def _join_plans(parts):
    def plan(refs):
        out, b0, k0 = [], 0, 0
        for part_plan, n_bufs, n_copies in parts:
            out += [(k0 + k, send, recv) for k, send, recv in part_plan(refs[b0:b0 + n_bufs])]
            b0 += n_bufs
            k0 += n_copies
        return out
    return plan


def _allgather_plan():
    def plan(refs):
        x, y, c = _coords()
        (land,) = refs
        me = 4 * x + 2 * y + c
        out = []
        for r in range(1, 8):
            px = 1 - x if r & 4 else x
            py = 1 - y if r & 2 else y
            pc = 1 - c if r & 1 else c
            out.append((r - 1, (land.at[me], land.at[me], (px, py, pc), None), (land.at[4 * px + 2 * py + pc], None)))
        return out
    return plan


def _sum_gathered(land, shapes, name):
    m = land.shape[1]

    def body(land_ref, *refs):
        outs, acc_ref = refs[:-1], refs[-1]
        acc = land_ref[0]
        for d in range(1, 8):
            acc = acc + land_ref[d]
        acc_ref[...] = acc
        r = 0
        for o_ref, (n, w) in zip(outs, shapes):
            if w == LANES:
                o_ref[...] = acc_ref[r:r + n, :]
                r += n
            elif w < LANES:
                o_ref[...] = acc_ref[r:r + n, 0:w]
                r += SUBLANES
            else:
                for k in range(n):
                    for q in range(w // LANES):
                        o_ref[k:k + 1, q * LANES:(q + 1) * LANES] = acc_ref[r:r + 1, :]
                        r += 1
        assert r == m, (r, m)

    return pl.pallas_call(
        body,
        name=name,
        out_shape=[jax.ShapeDtypeStruct(s, F32) for s in shapes],
        scratch_shapes=[pltpu.VMEM((m, LANES), F32)],
        compiler_params=_params(),
    )(land)


def _row_tile(rows, cap):
    t = cap
    while rows % t:
        t //= 2
    return t


def _add_my_half(g, r, name):
    n_slabs, _, R, C = g.shape
    tr = R if n_slabs > 1 else _row_tile(R, SUM_ROWS)

    def body(c_ref, g_ref, r_ref, o_ref):
        o_ref[...] = (g_ref[0] + r_ref[...]).astype(o_ref.dtype)

    return pl.pallas_call(
        body,
        name=name,
        grid_spec=pltpu.PrefetchScalarGridSpec(
            num_scalar_prefetch=1,
            grid=(n_slabs, R // tr),
            in_specs=[pl.BlockSpec((1, 1, tr, C), lambda p, i, c_ref: (p, c_ref[0], i, 0)),
                      pl.BlockSpec((1, tr, C), lambda p, i, c_ref: (p, i, 0))],
            out_specs=pl.BlockSpec((1, tr, C), lambda p, i, c_ref: (p, i, 0)),
        ),
        out_shape=jax.ShapeDtypeStruct(r.shape, jnp.bfloat16),
        compiler_params=_params(("parallel", "parallel")),
    )(lax.axis_index("c").reshape(1).astype(jnp.int32), g, r)


def _sum_slabs(own, got, name, deps=()):
    _, R, C = own.shape
    tr = _row_tile(R, SUM_ROWS)

    def body(s_ref, own_ref, got_ref, *rest):
        rest[-1][0] = ((own_ref[0].astype(F32) + got_ref[0].astype(F32)) + got_ref[1].astype(F32)) + got_ref[2].astype(F32)

    xi, yi, ci = _coords()
    return pl.pallas_call(
        body,
        name=name,
        grid_spec=pltpu.PrefetchScalarGridSpec(
            num_scalar_prefetch=1,
            grid=(R // tr,),
            in_specs=[pl.BlockSpec((1, tr, C), lambda i, s: (s[0], i, 0)),
                      pl.BlockSpec((3, tr, C), lambda i, s: (0, i, 0))] + [ANY_SPEC] * len(deps),
            out_specs=pl.BlockSpec((1, tr, C), lambda i, s: (s[1], i, 0)),
        ),
        out_shape=jax.ShapeDtypeStruct((2, R, C), F32),
        compiler_params=_params(("parallel",)),
    )(jnp.stack([2 * xi + yi, ci]).astype(jnp.int32), own, got, *deps)


def _sum_parts(owns, got, dest_sets, name):
    n = len(owns)
    _, R, C = owns[0].shape
    tr = _row_tile(R, SUM_ROWS)

    def body(s_ref, *refs):
        got_ref, o_ref = refs[n], refs[-1]
        total = jnp.zeros((tr, C), F32)
        for i in range(n):
            total = total + jnp.where(s_ref[2 + 2 * i] == 1, refs[i][0].astype(F32), 0.0)
        o_ref[0] = ((total + got_ref[0].astype(F32)) + got_ref[1].astype(F32)) + got_ref[2].astype(F32)

    xi, yi, ci = _coords()
    me = 2 * xi + yi
    scalars = [ci, ci]
    for dests in dest_sets:
        scalars += [_is_one_of(me, dests).astype(jnp.int32), _slab_of(me, dests)]
    own_spec = lambda i: pl.BlockSpec((1, tr, C), lambda r, s: (s[3 + 2 * i], r, 0))
    return pl.pallas_call(
        body,
        name=name,
        grid_spec=pltpu.PrefetchScalarGridSpec(
            num_scalar_prefetch=1,
            grid=(R // tr,),
            in_specs=[own_spec(i) for i in range(n)] + [pl.BlockSpec((3, tr, C), lambda r, s: (0, r, 0))],
            out_specs=pl.BlockSpec((1, tr, C), lambda r, s: (s[0], r, 0)),
        ),
        out_shape=jax.ShapeDtypeStruct((2, R, C), F32),
        compiler_params=_params(("parallel",)),
    )(jnp.stack(scalars).astype(jnp.int32), *owns, got)


def _adamw_math(w, g, m, v):
    m = ADAM_B1 * m + (1.0 - ADAM_B1) * g
    v = ADAM_B2 * v + (1.0 - ADAM_B2) * (g * g)
    m_hat = m / (1.0 - ADAM_B1 ** ADAM_STEP)
    v_hat = v / (1.0 - ADAM_B2 ** ADAM_STEP)
    delta = -ADAM_LR * (m_hat / (jnp.sqrt(v_hat) + ADAM_EPS) + ADAM_WD * w)
    return delta, m, v


def _adamw_halves(ws, g, ms, vs, half, prev, name, deps=()):
    n = len(ws)
    _, _, R, C = g.shape
    tr = _row_tile(R, ADAMW_ROWS)
    steps = R // tr
    carried = [] if prev is None else [a for four in prev for a in four]
    both = half is None
    which = (lambda i, s: i // steps) if both else (lambda i, s: s[0])
    half = 0 if both else half

    def body(s_ref, *refs):
        w_refs, g_refs, m_refs, v_refs = (refs[k * n:(k + 1) * n] for k in range(4))
        outs = refs[len(refs) - 4 * n:]
        for a in range(n):
            grad = g_refs[a][0, 0]
            d, mn, vn = _adamw_math(w_refs[a][...], grad, m_refs[a][...], v_refs[a][...])
            for o, val in zip(outs[4 * a:4 * a + 4], (grad, d, mn, vn)):
                o[...] = val

    rows = pl.BlockSpec((tr, C), lambda i, s: (which(i, s) * steps + i % steps, 0))
    grad_spec = lambda a: pl.BlockSpec((1, 1, tr, C), lambda i, s: (which(i, s), a, i % steps, 0))
    n_in = 4 * n
    outs = pl.pallas_call(
        body,
        name=name,
        grid_spec=pltpu.PrefetchScalarGridSpec(
            num_scalar_prefetch=1,
            grid=(2 * steps if both else steps,),
            in_specs=[rows] * n + [grad_spec(a) for a in range(n)] + [rows] * (2 * n)
            + [ANY_SPEC] * (len(carried) + len(deps)),
            out_specs=[rows] * (4 * n),
        ),
        out_shape=[jax.ShapeDtypeStruct((2 * R, C), F32)] * (4 * n),
        input_output_aliases={1 + n_in + k: k for k in range(len(carried))},
        compiler_params=_params(("parallel",)),
    )(jnp.reshape(half, (1,)).astype(jnp.int32), *ws, *([g] * n), *ms, *vs, *carried, *deps)
    return [outs[4 * a:4 * a + 4] for a in range(n)]


def _adamw_small(ws, gs, ms, vs, name):
    n = len(ws)

    def body(*refs):
        for a in range(n):
            d, mn, vn = _adamw_math(refs[a][...], refs[n + a][...], refs[2 * n + a][...], refs[3 * n + a][...])
            refs[4 * n + a][...] = d
            refs[5 * n + a][...] = mn
            refs[6 * n + a][...] = vn

    shapes = [jax.ShapeDtypeStruct(w.shape, F32) for w in ws]
    outs = pl.pallas_call(
        body,
        name=name,
        out_shape=shapes * 3,
        compiler_params=_params(),
    )(*ws, *gs, *ms, *vs)
    return outs[:n], outs[n:2 * n], outs[2 * n:]


def _to_blockdiag(w):
    per = CW // LRU_BW
    w4 = w.reshape(N_CT, per, LRU_BW, LRU_BW)
    eye = jnp.eye(per, dtype=w.dtype)
    return (w4[:, :, :, None, :] * eye[None, :, None, :, None]).reshape(N_CT, CW, CW)


def _blocks_from_lanes(g):
    side = LANES // LRU_BW
    g5 = g.reshape(N_CT, CW // LANES, LRU_BW, side, LRU_BW)
    return jnp.transpose(g5, (0, 1, 3, 2, 4)).reshape(LRU_BLOCKS, LRU_BW, LRU_BW)


def _local_grads(x2d, tgt2d, B, S, g_in, in_proj, conv_b, gate_x_w, gate_x_b, gate_a_w, gate_a_b, lam,
                 proj_weights, g_fin, reduce):
    wx_bd = _c(_to_blockdiag(gate_x_w))
    wa_bd = _c(_to_blockdiag(gate_a_w))
    tables = _retention_tables(S)

    proj, ht, w_all, conv_w, gain = in_proj(x2d, g_in, (*tables, wx_bd, wa_bd))
    gain3 = gain.reshape(HEADS, 1, DK)
    hlru, ya = _lru_fwd(proj, conv_w, conv_b, wx_bd, wa_bd, gate_x_b, gate_a_b, lam, B, S)
    o_pre, yb, states = _ret_fwd(proj, tables, gain3, B, S)
    wpa, wpb, wout = proj_weights(yb)
    loss, dx2, dya, dyb, dm, dgf, gw_proj = _mid(ya, yb, proj, x2d, tgt2d, wpa, wpb, wout, g_fin)
    g3 = _inproj_bwd_dw(ht, [dm], "inproj_bwd_dw_m")
    deps = reduce.m_ready(gw_proj, g3)
    dr, dgain = _ret_bwd(dyb, o_pre, proj, states, tables, gain3, B, S, deps)
    deps = reduce.ret_done(dr)
    g12 = _inproj_bwd_dw(ht, [dr], "inproj_bwd_dw_r", deps)
    deps = reduce.r_ready(g12)
    dxa, dga, dcw, dcb, dwx, dwa, dbx, dba, dlam = _lru_bwd(
        dya, proj, hlru, conv_w, conv_b, wx_bd, wa_bd, gate_x_b, gate_a_b, lam, B, S, deps)
    small = dict(conv_w=dcw, conv_b=dcb, gate_x_w=dwx, gate_x_b=dbx, gate_a_w=dwa, gate_a_b=dba, lru_lambda=dlam,
                 gn_gain=dgain.reshape(HEADS, DK), norm_final=dgf)
    deps = reduce.lru_done(dxa, _pack_small(small, loss, reduce.slot()))
    g0 = _inproj_bwd_dw(ht, [dxa, dga], "inproj_bwd_dw_a", deps)
    deps = reduce.a_ready(g0)
    n_tiles = x2d.shape[0] // min(DX_TILE, x2d.shape[0])
    grad_x, dgin = _inproj_bwd_dx([dxa, dga, dr, dm], w_all, x2d, dx2, g_in, 0, n_tiles, None, "inproj_bwd_dx", deps)
    return grad_x, dgin


ALL_CHIPS = (0, 1, 2, 3)


class _GradReduce:
    def __init__(self, proj_done):
        self.pending = {}
        self.proj_done = proj_done
        self.land_in = None

    def _start(self, key, parts, name):
        bufs, plans, shared = [], [], None
        for part_bufs, plan, n_copies, part_shared in parts:
            if part_shared is not None:
                shared = len(bufs) + part_shared
            plans.append((plan, len(part_bufs), n_copies))
            bufs += part_bufs
        plan = _join_plans(plans)
        send_sems, recv_sems, bufs, token = _copies_start(bufs, plan, sum(p[2] for p in plans), name + "_start")
        if shared is not None:
            self.land_in = bufs[shared]
        self.pending[key] = (send_sems, recv_sems, bufs, plan, name + "_wait", shared)
        return (token,)

    def _finish(self, key, after):
        send_sems, recv_sems, bufs, plan, name, shared = self.pending.pop(key)
        if shared is not None:
            bufs[shared] = self.land_in
        bufs = _copies_wait(send_sems, recv_sems, bufs, after, plan, name)
        if shared is not None:
            self.land_in = bufs[shared]
        return bufs

    @staticmethod
    def _swap(pieces):
        bufs = []
        for g in pieces:
            bufs += [g, lax.empty((g.shape[0],) + g.shape[2:], F32)]
        n_slabs = [g.shape[0] for g in pieces]
        return bufs, _swap_plan(n_slabs), sum(n_slabs), None

    def _scatter(self, sums, dest_sets):
        bufs = []
        for cs in sums:
            bufs += [cs, lax.empty((3,) + cs.shape[1:], cs.dtype)]
        if self.land_in is not None:
            bufs[-1] = self.land_in
        return bufs, _scatter_plan(dest_sets), 3 * len(sums), len(bufs) - 1

    @staticmethod
    def slot():
        x, y, c = _coords()
        return 4 * x + 2 * y + c

    def _gather8(self, block):
        land = lax.dynamic_update_slice(lax.empty((8,) + block.shape, F32), block[None], (self.slot(), 0, 0))
        return [land], _allgather_plan(), 7, None

    def m_ready(self, gw_proj, g3):
        rows = gw_proj.shape[2] * gw_proj.shape[3]
        return self._start("m", [self._swap([gw_proj.reshape(N_CHIPS, 2, rows, D_MODEL), g3])], "swap_m")

    def ret_done(self, after):
        proj, land_p, g3, land_3 = self._finish("m", after)
        sums_m = [_add_my_half(proj, land_p, "chip_sum_proj"), _add_my_half(g3, land_3, "chip_sum_m")]
        return self._start("sm", [self._scatter(sums_m, [ALL_CHIPS, (3,)])], "scatter_m")

    def r_ready(self, g12):
        return self._start("r", [self._swap([g12])], "swap_r")

    def lru_done(self, after, packed):
        g12, land_12 = self._finish("r", after)
        sums_r = [_add_my_half(g12, land_12, "chip_sum_r")]
        return (self._start("sr", [self._scatter(sums_r, [(1, 2)])], "scatter_r")
                + self._start("small", [([packed], _allgather_plan(), 7, None)], "gather_small"))

    def a_ready(self, g0):
        (token,) = self._start("a", [self._swap([g0])], "swap_a")
        csp, gotp, self.cs3, _ = self._finish("sm", token)
        half_proj = _sum_slabs(csp, gotp, "sum_w_proj")
        g0, land_0 = self._finish("a", half_proj)
        join = ([half_proj], _join_plan(half_proj.shape[1], PROJ_JOIN_PIECES), PROJ_JOIN_PIECES, None)
        return self._start("sa", [self._scatter([_add_my_half(g0, land_0, "chip_sum_a")], [(0,)]), join], "scatter_a")

    def finish(self, dgin, w_in_done):
        (token,) = self._start("n", [self._gather8(dgin)], "gather_norm_in")
        (small,) = self._finish("small", token)
        cs12, _ = self._finish("sr", token)
        cs0, _, g_proj = self._finish("sa", token)
        self.proj_done(g_proj)
        half_in =_sum_parts([self.cs3, cs12, cs0], self.land_in, [(3,), (1, 2), (0,)], "sum_w_in")
        deps = self._start("j", [([half_in], _join_plan(half_in.shape[1], JOIN_PIECES), JOIN_PIECES, None)], "join_w_in")
        first = w_in_done(self.pending["j"][2][0], True, None, deps)
        (g_in,) = self._finish("j", first[1])
        done = w_in_done(g_in, False, first, ())
        (norm_in,) = self._finish("n", done[1])
        return _unpack_small(small), _sum_gathered(norm_in, [(1, D_MODEL)], "sum_norm_in_grad")[0]


_SMALL = ("gate_x_w", "gate_a_w", "conv_w", "conv_b", "gate_x_b", "gate_a_b", "lru_lambda", "gn_gain", "norm_final")
_SMALL_SHAPES = dict(gate_x_w=(LRU_BLOCKS, LRU_BW, LRU_BW), gate_a_w=(LRU_BLOCKS, LRU_BW, LRU_BW),
                     norm_in=(1, D_MODEL), conv_w=(CONV, D_MODEL), conv_b=(1, D_MODEL), gate_x_b=(1, D_MODEL),
                     gate_a_b=(1, D_MODEL), lru_lambda=(1, D_MODEL), gn_gain=(HEADS, DK), norm_final=(1, D_MODEL))


def _pack_small(small, loss, slot):
    parts = [small[k] if small[k].ndim == 2 else small[k].reshape(-1, LANES) for k in _SMALL]
    m = sum(p.size for p in parts) // LANES + SUBLANES

    def body(s_ref, *refs):
        o_ref = refs[-1]
        r = 0
        for ref, part in zip(refs, parts):
            if part.shape[1] == LANES:
                o_ref[0, r:r + part.shape[0], :] = ref[...]
                r += part.shape[0]
                continue
            for k in range(part.shape[0]):
                for q in range(part.shape[1] // LANES):
                    o_ref[0, r:r + 1, :] = ref[k:k + 1, q * LANES:(q + 1) * LANES]
                    r += 1
        o_ref[0, r:r + SUBLANES, :] = jnp.broadcast_to(refs[len(parts)][...], (SUBLANES, LANES))

    return pl.pallas_call(
        body,
        name="pack_small_grads",
        grid_spec=pltpu.PrefetchScalarGridSpec(
            num_scalar_prefetch=1,
            grid=(1,),
            in_specs=[pl.BlockSpec(p.shape, lambda i, s: (0, 0)) for p in parts] + [pl.BlockSpec((1, 1), lambda i, s: (0, 0))],
            out_specs=pl.BlockSpec((1, m, LANES), lambda i, s: (s[0], 0, 0)),
        ),
        out_shape=jax.ShapeDtypeStruct((8, m, LANES), F32),
        compiler_params=_params(("arbitrary",)),
    )(jnp.reshape(slot, (1,)).astype(jnp.int32), *parts, loss)


def _unpack_small(land):
    gates = ("gate_x_w", "gate_a_w")
    packed_shape = lambda k: (LRU_BLOCKS * LRU_BW * LRU_BW // LANES, LANES) if k in gates else _SMALL_SHAPES[k]
    *sums, loss = _sum_gathered(land, [packed_shape(k) for k in _SMALL] + [(1, 1)], "sum_small_grads")
    return {k: _blocks_from_lanes(g) if k in gates else g for k, g in zip(_SMALL, sums)}, loss


def kernel(x, norm_in, w_in, conv_w, conv_b, gate_x_w, gate_x_b, gate_a_w, gate_a_b, lru_lambda, gn_gain, w_proj_a, w_proj_b, w_out, norm_final, loss_target, m_norm_in, m_w_in, m_conv_w, m_conv_b, m_gate_x_w, m_gate_x_b, m_gate_a_w, m_gate_a_b, m_lru_lambda, m_gn_gain, m_w_proj_a, m_w_proj_b, m_w_out, m_norm_final, v_norm_in, v_w_in, v_conv_w, v_conv_b, v_gate_x_w, v_gate_x_b, v_gate_a_w, v_gate_a_b, v_lru_lambda, v_gn_gain, v_w_proj_a, v_w_proj_b, v_w_out, v_norm_final):
    B, S, _ = x.shape
    T = B * S
    xi, yi, ci = _coords()
    chip = 2 * xi + yi

    cshard = D_MODEL // N_CHIPS
    mine = _cast_into_slot([w_in[0].reshape(2, D_MODEL // 2, 2 * D_MODEL)], "cast_w_in")
    plan = _gather_plan(3)
    pending_proj = []
    gshard = DK // N_CHIPS
    tiny = jnp.concatenate([conv_w[0], jnp.zeros((4, cshard), F32), jnp.pad(gn_gain[0], ((0, 4), (0, cshard - gshard)))],
                           axis=0).reshape(1, 2, SUBLANES, cshard)
    tiny_buf = lax.dynamic_update_slice(lax.empty((N_CHIPS, 2, SUBLANES, cshard), F32), tiny, (chip, 0, 0, 0))
    near_plan, pass_plan, far_plan = (_chip_gather_plan(stage, 2) for stage in ("near", "pass", "far"))
    (n_near, _), (n_pass, passed_on), (n_far, _) = (_chip_gather_copies(stage, 2) for stage in ("near", "pass", "far"))
    halves = set(range(n_pass)) - passed_on
    near_s, near_r, bufs, near_token = _copies_start([mine[0], tiny_buf], near_plan, n_near, "gather_near_start")

    def in_proj(x2d, g_in, meanwhile):
        as_w = lambda b: b[0].reshape(N_CHIPS, D_MODEL, 2 * D_MODEL)
        slot_x, slot_y, slot_d = 2 * (1 - xi) + yi, 2 * xi + (1 - yi), 2 * (1 - xi) + (1 - yi)
        ids = lambda *chips: jnp.stack(chips).astype(jnp.int32)
        proj, hb, ht = _inproj_first(x2d, g_in, as_w(bufs), ids(chip), "inproj_own", (near_token, *meanwhile))
        got = _copies_wait(near_s, near_r, bufs, proj, near_plan, "gather_near_wait")
        pass_s, pass_r, got, pass_token = _copies_start(got, pass_plan, n_pass, "gather_pass_start")
        mine_proj = _cast_into_slot([w[0].reshape(2, cshard // 2, D_MODEL) for w in (w_proj_a, w_proj_b, w_out)],
                                    "cast_w_proj", (pass_token,))
        got = _copies_wait(pass_s, pass_r, got, mine_proj[0], pass_plan, "gather_pass_wait_halves", only=halves)
        proj = _inproj_more(hb, as_w(got), ids(slot_x, slot_y), proj, "inproj_near")
        got = _copies_wait(pass_s, pass_r, got, proj, pass_plan, "gather_pass_wait_far", only=passed_on)
        far_s, far_r, got, far_token = _copies_start(got, far_plan, n_far, "gather_far_start")
        pending_proj.append(_copies_start(mine_proj, plan, 9, "gather_proj_start", (far_token,)))
        got = _copies_wait(far_s, far_r, got, pending_proj[0][3], far_plan, "gather_far_wait")
        proj = _inproj_more(hb, as_w(got), ids(slot_d), proj, "inproj_far")
        tiny_all = got[1].reshape(N_CHIPS, 2 * SUBLANES, cshard)
        conv_w_full = jnp.transpose(tiny_all[:, 0:CONV, :], (1, 0, 2)).reshape(CONV, D_MODEL)
        gain_full = jnp.transpose(tiny_all[:, 8:8 + HEADS, :gshard], (1, 0, 2)).reshape(HEADS, DK)
        return proj, ht, as_w(got), conv_w_full, gain_full

    def proj_weights(after):
        s_sems, r_sems, pbufs, _ = pending_proj[0]
        got = _copies_wait(s_sems, r_sems, pbufs, after, plan, "gather_proj_wait")
        return [b.reshape(D_MODEL, D_MODEL) for b in got]

    weights = dict(norm_in=norm_in, w_in=w_in, conv_w=conv_w, conv_b=conv_b, gate_x_w=gate_x_w, gate_x_b=gate_x_b,
                   gate_a_w=gate_a_w, gate_a_b=gate_a_b, lru_lambda=lru_lambda, gn_gain=gn_gain, w_proj_a=w_proj_a,
                   w_proj_b=w_proj_b, w_out=w_out, norm_final=norm_final)
    ms = dict(norm_in=m_norm_in, w_in=m_w_in, conv_w=m_conv_w, conv_b=m_conv_b, gate_x_w=m_gate_x_w,
              gate_x_b=m_gate_x_b, gate_a_w=m_gate_a_w, gate_a_b=m_gate_a_b, lru_lambda=m_lru_lambda, gn_gain=m_gn_gain,
              w_proj_a=m_w_proj_a, w_proj_b=m_w_proj_b, w_out=m_w_out, norm_final=m_norm_final)
    vs = dict(norm_in=v_norm_in, w_in=v_w_in, conv_w=v_conv_w, conv_b=v_conv_b, gate_x_w=v_gate_x_w,
              gate_x_b=v_gate_x_b, gate_a_w=v_gate_a_w, gate_a_b=v_gate_a_b, lru_lambda=v_lru_lambda, gn_gain=v_gn_gain,
              w_proj_a=v_w_proj_a, w_proj_b=v_w_proj_b, w_out=v_w_out, norm_final=v_norm_final)
    names = list(weights)
    grads, delta, new_m, new_v = {}, {}, {}, {}

    def update_big(keys, g, half, prev, name, deps=()):
        two = lambda a: a.reshape(a.shape[1], a.shape[2])
        res = _adamw_halves([two(weights[k]) for k in keys], g, [two(ms[k]) for k in keys], [two(vs[k]) for k in keys],
                            half, prev, name, deps)
        for k, (gk, d, mn, vn) in zip(keys, res):
            shp = weights[k].shape
            grads[k], delta[k], new_m[k], new_v[k] = gk.reshape(shp), d.reshape(shp), mn.reshape(shp), vn.reshape(shp)
        return res

    def proj_done(g_proj):
        g4 = g_proj.reshape(2, 3, D_MODEL // (2 * N_CHIPS), D_MODEL)
        return update_big(("w_proj_a", "w_proj_b", "w_out"), g4, None, None, "adamw_proj")[-1][1]

    def w_in_done(g_in, own, prev, deps):
        g4 = g_in.reshape(2, 1, D_MODEL // 2, 2 * D_MODEL)
        return update_big(("w_in",), g4, ci if own else 1 - ci, None if prev is None else [prev],
                          "adamw_w_in_own" if own else "adamw_w_in_other", deps)[0]

    reduce = _GradReduce(proj_done)
    grad_x, dgin = _local_grads(
        x.reshape(T, D_MODEL), loss_target.reshape(T, D_MODEL), B, S, norm_in, in_proj, conv_b,
        gate_x_w[0], gate_x_b, gate_a_w[0], gate_a_b, lru_lambda, proj_weights,
        norm_final.reshape(1, D_MODEL), reduce)

    (gsm, loss), g_norm_in = reduce.finish(dgin.reshape(SUBLANES, LANES), w_in_done)
    loss = loss[0, 0]
    gsm["norm_in"] = g_norm_in
    gsm["conv_w"] = lax.dynamic_slice_in_dim(gsm["conv_w"], chip * cshard, cshard, axis=1)
    gsm["gn_gain"] = lax.dynamic_slice_in_dim(gsm["gn_gain"], chip * gshard, gshard, axis=1)
    smalls = [k for k in names if k not in delta]

    def view(a):
        return a.reshape(1, -1) if a.ndim == 1 else (a.reshape(a.shape[1:]) if a.ndim > 2 else a)

    ds, mns, vns = _adamw_small([view(weights[k]) for k in smalls], [gsm[k].reshape(view(weights[k]).shape) for k in smalls],
                                [view(ms[k]) for k in smalls], [view(vs[k]) for k in smalls], "adamw_small")
    for k, d, mn, vn in zip(smalls, ds, mns, vns):
        shp = weights[k].shape
        grads[k], delta[k], new_m[k], new_v[k] = gsm[k].reshape(shp), d.reshape(shp), mn.reshape(shp), vn.reshape(shp)

    return (loss, grad_x.reshape(B, S, D_MODEL), *[grads[k] for k in names], *[delta[k] for k in names],
            *[new_m[k] for k in names], *[new_v[k] for k in names])
```

```python
import jax
import jax.numpy as jnp
from jax import lax
from jax.experimental import pallas as pl
from jax.experimental.pallas import tpu as pltpu

F32 = jnp.float32
_MXU = jnp.bfloat16

D_MODEL = 1024
N_GROUPS = 8
HEADS = 4
DK = 256
CHUNK = 128
CONV = 4
LRU_BLOCKS = 16
LRU_BW = 64
LRU_C = 8.0
ROPE_THETA = 10000.0
EPS = 1e-6
CW = 256
N_CT = D_MODEL // CW
N_CHIPS = 4
MESH = pl.DeviceIdType.MESH

ADAM_LR = 0.001
ADAM_B1 = 0.9
ADAM_B2 = 0.999
ADAM_EPS = 1e-08
ADAM_WD = 0.01
ADAM_STEP = 10

VMEM_LIMIT = 56 * 1024 * 1024

FIRST_PROJ_TILE = 1024
MORE_PROJ_TILE = 2048
SCAN_TILE = 1024
MID_TILE = 256
DX_TILE = 512
DW_COLS = 512
DW_LOADS = 4
RET_CHUNKS = 2
SUM_ROWS = 256
ADAMW_ROWS = 256
JOIN_PIECES = 8
PROJ_JOIN_PIECES = 4


def _c(v):
    return v.astype(_MXU)


def _dot(a, b):
    return lax.dot_general(a, b, (((1,), (0,)), ((), ())), preferred_element_type=F32)


def _dot_nt(a, b):
    return lax.dot_general(a, b, (((1,), (1,)), ((), ())), preferred_element_type=F32)


def _dot_tn(a, b):
    return lax.dot_general(a, b, (((0,), (0,)), ((), ())), preferred_element_type=F32)


def _sigmoid(z):
    return 0.5 * jnp.tanh(0.5 * z) + 0.5


ANY_SPEC = pl.BlockSpec(memory_space=pl.ANY)


def _after(body, n_in, deps):
    n_deps = len(deps)

    def wrapped(*refs):
        return body(*refs[:n_in], *refs[n_in + n_deps:])

    return wrapped


def _params(sem=None):
    if sem is None:
        return pltpu.CompilerParams(vmem_limit_bytes=VMEM_LIMIT)
    return pltpu.CompilerParams(vmem_limit_bytes=VMEM_LIMIT, dimension_semantics=sem)


def _inproj_first(x2d, g_in, w_all, chips, name, deps=()):
    T = x2d.shape[0]
    tm = min(FIRST_PROJ_TILE, T)
    n_i = T // tm

    def body(s_ref, *refs):
        x_ref, g_ref, w_ref = refs[:3]
        proj_ref, hb_ref, ht_ref, h_all = refs[-4:]
        i = pl.program_id(1)
        rows = pl.ds(pl.multiple_of(i * tm, tm), tm)

        @pl.when(pl.program_id(0) == 0)
        def _():
            x = x_ref[...]
            r = lax.rsqrt(jnp.mean(x * x, axis=-1, keepdims=True) + EPS)
            h = x * r * g_ref[...]
            hb = h.astype(h_all.dtype)
            h_all[rows, :] = hb
            hb_ref[...] = hb
            ht_ref[...] = h.T.astype(ht_ref.dtype)

        proj_ref[...] = _dot(h_all[rows, :], w_ref[0])

    first = lambda j, i: jnp.where(j == 0, i, n_i - 1)
    return pl.pallas_call(
        body,
        name=name,
        grid_spec=pltpu.PrefetchScalarGridSpec(
            num_scalar_prefetch=1,
            grid=(2 * chips.shape[0], n_i),
            in_specs=[
                pl.BlockSpec((tm, D_MODEL), lambda j, i, s: (first(j, i), 0)),
                pl.BlockSpec((1, D_MODEL), lambda j, i, s: (0, 0)),
                pl.BlockSpec((1, D_MODEL, D_MODEL), lambda j, i, s: (s[j // 2], 0, j % 2)),
            ] + [ANY_SPEC] * len(deps),
            out_specs=[
                pl.BlockSpec((tm, D_MODEL), lambda j, i, s: (i, 2 * s[j // 2] + j % 2)),
                pl.BlockSpec((tm, D_MODEL), lambda j, i, s: (first(j, i), 0)),
                pl.BlockSpec((D_MODEL, tm), lambda j, i, s: (0, first(j, i))),
            ],
            scratch_shapes=[pltpu.VMEM((T, D_MODEL), _MXU)],
        ),
        out_shape=[
            jax.ShapeDtypeStruct((T, N_GROUPS * D_MODEL), F32),
            jax.ShapeDtypeStruct((T, D_MODEL), _MXU),
            jax.ShapeDtypeStruct((D_MODEL, T), _MXU),
        ],
        compiler_params=_params(("arbitrary", "arbitrary")),
    )(chips, x2d, g_in, w_all, *deps)


def _inproj_more(hb, w_all, chips, proj, name):
    T = hb.shape[0]
    tm = min(MORE_PROJ_TILE, T)

    def body(s_ref, hb_hbm, w_ref, prev_ref, proj_ref, h_all, sem):
        @pl.when((pl.program_id(0) == 0) & (pl.program_id(1) == 0))
        def _():
            cp = pltpu.make_async_copy(hb_hbm, h_all, sem)
            cp.start()
            cp.wait()

        rows = pl.ds(pl.multiple_of(pl.program_id(1) * tm, tm), tm)
        proj_ref[...] = _dot(h_all[rows, :], w_ref[0])

    return pl.pallas_call(
        body,
        name=name,
        grid_spec=pltpu.PrefetchScalarGridSpec(
            num_scalar_prefetch=1,
            grid=(2 * chips.shape[0], T // tm),
            in_specs=[
                ANY_SPEC,
                pl.BlockSpec((1, D_MODEL, D_MODEL), lambda j, i, s: (s[j // 2], 0, j % 2)),
                ANY_SPEC,
            ],
            out_specs=pl.BlockSpec((tm, D_MODEL), lambda j, i, s: (i, 2 * s[j // 2] + j % 2)),
            scratch_shapes=[pltpu.VMEM((T, D_MODEL), hb.dtype), pltpu.SemaphoreType.DMA],
        ),
        out_shape=jax.ShapeDtypeStruct(proj.shape, F32),
        input_output_aliases={3: 0},
        compiler_params=_params(("arbitrary", "arbitrary")),
    )(chips, hb, w_all, proj)


def _scan_fwd(a, u):
    n = a.shape[0]
    row = lax.broadcasted_iota(jnp.int32, a.shape, 0)
    s = 1
    while s < n:
        m = row >= s
        u = u + a * jnp.where(m, pltpu.roll(u, s, 0), 0.0)
        a = a * jnp.where(m, pltpu.roll(a, s, 0), 1.0)
        s *= 2
    return a, u


def _scan_bwd(b, g):
    n = b.shape[0]
    row = lax.broadcasted_iota(jnp.int32, b.shape, 0)
    s = 1
    while s < n:
        m = row < n - s
        g = g + b * jnp.where(m, pltpu.roll(g, n - s, 0), 0.0)
        b = b * jnp.where(m, pltpu.roll(b, n - s, 0), 1.0)
        s *= 2
    return b, g


LANES = 128
SUBLANES = 8


def _scan_scratch(tc):
    by_lanes = pltpu.VMEM((CW // LANES, tc, LANES), F32)
    return [by_lanes, by_lanes, pltpu.VMEM((tc // SUBLANES, CW), F32), pltpu.VMEM((tc, CW), F32)]


def _scan_tile(a, u, edge, la_ref, lh_ref, c_ref, dst_ref, reverse):
    n, w = a.shape
    groups = n // SUBLANES
    a3 = a.reshape(groups, SUBLANES, w)
    u3 = u.reshape(groups, SUBLANES, w)
    row = lax.broadcasted_iota(jnp.int32, a3.shape, 1)
    for s in (1, 2, 4):
        m = (row < SUBLANES - s) if reverse else (row >= s)
        shift = SUBLANES - s if reverse else s
        u3 = u3 + a3 * jnp.where(m, pltpu.roll(u3, shift, 1), 0.0)
        a3 = a3 * jnp.where(m, pltpu.roll(a3, shift, 1), 1.0)
    al = a3.reshape(n, w)
    hl = u3.reshape(n, w)
    blocks = w // LANES
    for q in range(blocks):
        la_ref[q] = al[:, q * LANES:(q + 1) * LANES]
        lh_ref[q] = hl[:, q * LANES:(q + 1) * LANES]
    ends = pl.ds(0 if reverse else SUBLANES - 1, groups, stride=SUBLANES)
    end_a = jnp.concatenate([la_ref.at[q][ends, :] for q in range(blocks)], axis=-1)
    end_h = jnp.concatenate([lh_ref.at[q][ends, :] for q in range(blocks)], axis=-1)
    prod, part = (_scan_bwd if reverse else _scan_fwd)(end_a, end_h)
    total = part + prod * edge
    g_row = lax.broadcasted_iota(jnp.int32, total.shape, 0)
    if reverse:
        c_ref[...] = jnp.where(g_row == groups - 1, edge, pltpu.roll(total, groups - 1, 0))
    else:
        c_ref[...] = jnp.where(g_row == 0, edge, pltpu.roll(total, 1, 0))
    for g in range(groups):
        rows = slice(g * SUBLANES, (g + 1) * SUBLANES)
        for q in range(blocks):
            cols = slice(q * LANES, (q + 1) * LANES)
            dst_ref[rows, cols] = lh_ref[q, rows, :] + la_ref[q, rows, :] * c_ref[g:g + 1, cols]


def _softplus_neg(lam):
    z = -lam
    return jnp.maximum(z, 0.0) + jnp.log1p(jnp.exp(-jnp.abs(z)))


def _lru_gates(xc, wx_ref, wa_ref, bx_ref, ba_ref, lam_ref):
    xcb = _c(xc)
    i_t = _sigmoid(_dot(xcb, wx_ref[0]) + bx_ref[...])
    r_t = _sigmoid(_dot(xcb, wa_ref[0]) + ba_ref[...])
    sp = _softplus_neg(lam_ref[...])
    log_a = (-LRU_C) * r_t * sp
    a = jnp.exp(log_a)
    mult = jnp.sqrt(1.0 - a * a)
    return xcb, i_t, r_t, sp, a, mult


def _conv_from_ext(ext_ref, xa, cw_ref, cb_ref, tc):
    return (cb_ref[...] + cw_ref[3:4, :] * xa + cw_ref[2:3, :] * ext_ref[7:7 + tc, :]
            + cw_ref[1:2, :] * ext_ref[6:6 + tc, :] + cw_ref[0:1, :] * ext_ref[5:5 + tc, :])


def _lru_fwd(proj, conv_w, conv_b, wx_bd, wa_bd, bx, ba, lam, B, S):
    T = B * S
    tc = min(SCAN_TILE, S)
    nt = S // tc
    h8 = tc // 8

    def body(xa_ref, halo_ref, ga_ref, cw_ref, cb_ref, wx_ref, wa_ref, bx_ref, ba_ref, lam_ref,
             h_ref, ya_ref, ext_ref, carry_ref, la_ref, lh_ref, c_ref):
        t = pl.program_id(2)

        @pl.when(t == 0)
        def _():
            carry_ref[...] = jnp.zeros_like(carry_ref)

        xa = xa_ref[...]
        ext_ref[0:8, :] = jnp.where(t == 0, 0.0, halo_ref[...])
        ext_ref[8:8 + tc, :] = xa
        xc = _conv_from_ext(ext_ref, xa, cw_ref, cb_ref, tc)
        _, i_t, _, _, a, mult = _lru_gates(xc, wx_ref, wa_ref, bx_ref, ba_ref, lam_ref)
        u = mult * (i_t * xc)
        _scan_tile(a, u, carry_ref[7:8, :], la_ref, lh_ref, c_ref, h_ref, False)
        h = h_ref[...]
        carry_ref[...] = h[tc - 8:tc, :]
        ga = ga_ref[...]
        ya_ref[...] = (ga * _sigmoid(ga) * h).astype(ya_ref.dtype)

    row = lambda b, t: b * nt + t
    vec = pl.BlockSpec((1, CW), lambda b, c, t: (0, c))
    mat = pl.BlockSpec((1, CW, CW), lambda b, c, t: (c, 0, 0))
    return pl.pallas_call(
        body,
        name="lru_fwd",
        grid=(B, N_CT, nt),
        in_specs=[
            pl.BlockSpec((tc, CW), lambda b, c, t: (row(b, t), c)),
            pl.BlockSpec((8, CW), lambda b, c, t: (jnp.maximum(row(b, t) * h8 - 1, 0), c)),
            pl.BlockSpec((tc, CW), lambda b, c, t: (row(b, t), N_CT + c)),
            pl.BlockSpec((CONV, CW), lambda b, c, t: (0, c)),
            vec, mat, mat, vec, vec, vec,
        ],
        out_specs=[
            pl.BlockSpec((tc, CW), lambda b, c, t: (row(b, t), c)),
            pl.BlockSpec((tc, CW), lambda b, c, t: (row(b, t), c)),
        ],
        out_shape=[
            jax.ShapeDtypeStruct((T, D_MODEL), F32),
            jax.ShapeDtypeStruct((T, D_MODEL), _MXU),
        ],
        scratch_shapes=[pltpu.VMEM((tc + 8, CW), F32), pltpu.VMEM((8, CW), F32)] + _scan_scratch(tc)[:3],
        compiler_params=_params(("parallel", "parallel", "arbitrary")),
    )(proj, proj, proj, conv_w, conv_b, wx_bd, wa_bd, bx, ba, lam)


def _lru_bwd(dya, proj, hlru, conv_w, conv_b, wx_bd, wa_bd, bx, ba, lam, B, S, deps=()):
    T = B * S
    tc = min(SCAN_TILE, S)
    nt = S // tc
    h8 = tc // 8

    def body(dya_ref, xa_ref, xhalo_ref, ga_ref, h_ref, hhalo_ref, cw_ref, cb_ref, wx_ref, wa_ref, bx_ref, ba_ref,
             lam_ref, dxa_ref, dga_ref, dcw_ref, dcb_ref, dwx_ref, dwa_ref, dbx_ref, dba_ref, dlam_ref,
             ext_ref, ext2_ref, carry_ref, dhalo_ref, la_ref, lh_ref, c_ref, dh_ref, accx_ref, acca_ref):
        b = pl.program_id(1)
        t = pl.program_id(2)
        tt = nt - 1 - t

        @pl.when(t == 0)
        def _():
            carry_ref[...] = jnp.zeros_like(carry_ref)
            dhalo_ref[...] = jnp.zeros_like(dhalo_ref)

        @pl.when((t == 0) & (b == 0))
        def _():
            for r in (dcw_ref, dcb_ref, accx_ref, acca_ref, dbx_ref, dba_ref, dlam_ref):
                r[...] = jnp.zeros_like(r)

        xa = xa_ref[...]
        ext_ref[0:8, :] = jnp.where(tt == 0, 0.0, xhalo_ref[...])
        ext_ref[8:8 + tc, :] = xa
        xc = _conv_from_ext(ext_ref, xa, cw_ref, cb_ref, tc)
        xcb, i_t, r_t, sp, a, mult = _lru_gates(xc, wx_ref, wa_ref, bx_ref, ba_ref, lam_ref)

        h = h_ref[...]
        ga = ga_ref[...]
        dya_t = dya_ref[...]
        sg = _sigmoid(ga)
        dga_ref[...] = (dya_t * h * (sg * (1.0 + ga * (1.0 - sg)))).astype(dga_ref.dtype)
        dlru = dya_t * (ga * sg)

        row = lax.broadcasted_iota(jnp.int32, a.shape, 0)
        coef = jnp.where(row == tc - 1, 1.0, pltpu.roll(a, tc - 1, 0))
        _scan_tile(coef, dlru, carry_ref[0:1, :], la_ref, lh_ref, c_ref, dh_ref, True)
        dh = dh_ref[...]
        ext2_ref[0:tc, :] = a * dh
        carry_ref[...] = ext2_ref[0:8, :]

        ext2_ref[0:8, :] = jnp.where(tt == 0, 0.0, hhalo_ref[...])
        ext2_ref[8:8 + tc, :] = h
        hprev = ext2_ref[7:7 + tc, :]

        da = dh * hprev
        ix = i_t * xc
        dmult = dh * ix
        di = dh * mult * xc
        dxc = dh * mult * i_t
        dlog_a = da * a - dmult * (a * a) / mult
        dr = dlog_a * ((-LRU_C) * sp)
        dlam_ref[...] += jnp.sum(dlog_a * r_t, axis=0, keepdims=True) * (LRU_C * _sigmoid(-lam_ref[...]))
        dza = dr * r_t * (1.0 - r_t)
        dzx = di * i_t * (1.0 - i_t)
        dzab = _c(dza)
        dzxb = _c(dzx)
        dxc = dxc + _dot_nt(dzxb, wx_ref[0]) + _dot_nt(dzab, wa_ref[0])
        accx_ref[...] += _dot_tn(xcb, dzxb)
        acca_ref[...] += _dot_tn(xcb, dzab)
        dbx_ref[...] += jnp.sum(dzx, axis=0, keepdims=True)
        dba_ref[...] += jnp.sum(dza, axis=0, keepdims=True)

        dcb_ref[...] += jnp.sum(dxc, axis=0, keepdims=True)
        dcw_ref[3:4, :] += jnp.sum(dxc * xa, axis=0, keepdims=True)
        dcw_ref[2:3, :] += jnp.sum(dxc * ext_ref[7:7 + tc, :], axis=0, keepdims=True)
        dcw_ref[1:2, :] += jnp.sum(dxc * ext_ref[6:6 + tc, :], axis=0, keepdims=True)
        dcw_ref[0:1, :] += jnp.sum(dxc * ext_ref[5:5 + tc, :], axis=0, keepdims=True)
        ext2_ref[0:tc, :] = dxc
        ext2_ref[tc:tc + 8, :] = dhalo_ref[...]
        dxa = (cw_ref[3:4, :] * dxc + cw_ref[2:3, :] * ext2_ref[1:1 + tc, :]
               + cw_ref[1:2, :] * ext2_ref[2:2 + tc, :] + cw_ref[0:1, :] * ext2_ref[3:3 + tc, :])
        dxa_ref[...] = dxa.astype(dxa_ref.dtype)
        dhalo_ref[...] = ext2_ref[0:8, :]

        @pl.when((b == B - 1) & (t == nt - 1))
        def _():
            lane_block = lax.broadcasted_iota(jnp.int32, (LRU_BW, CW), 1) // LRU_BW
            for acc_ref, out_ref in ((accx_ref, dwx_ref), (acca_ref, dwa_ref)):
                diag = jnp.zeros((LRU_BW, CW), F32)
                for j in range(CW // LRU_BW):
                    diag = jnp.where(lane_block == j, acc_ref[j * LRU_BW:(j + 1) * LRU_BW, :], diag)
                for q in range(CW // LANES):
                    out_ref[0, q] = diag[:, q * LANES:(q + 1) * LANES]

    row_of = lambda b, t: b * nt + (nt - 1 - t)
    tile = lambda off: pl.BlockSpec((tc, CW), lambda c, b, t: (row_of(b, t), off + c))
    halo = pl.BlockSpec((8, CW), lambda c, b, t: (jnp.maximum(row_of(b, t) * h8 - 1, 0), c))
    vec = pl.BlockSpec((1, CW), lambda c, b, t: (0, c))
    mat = pl.BlockSpec((1, CW, CW), lambda c, b, t: (c, 0, 0))
    cwspec = pl.BlockSpec((CONV, CW), lambda c, b, t: (0, c))
    diag = pl.BlockSpec((1, CW // LANES, LRU_BW, LANES), lambda c, b, t: (c, 0, 0, 0))
    return pl.pallas_call(
        _after(body, 13, deps),
        name="lru_bwd",
        grid=(N_CT, B, nt),
        in_specs=[tile(0), tile(0), halo, tile(N_CT), tile(0), halo, cwspec, vec, mat, mat, vec, vec, vec]
        + [ANY_SPEC] * len(deps),
        out_specs=[tile(0), tile(0), cwspec, vec, diag, diag, vec, vec, vec],
        out_shape=[
            jax.ShapeDtypeStruct((T, D_MODEL), _MXU),
            jax.ShapeDtypeStruct((T, D_MODEL), _MXU),
            jax.ShapeDtypeStruct((CONV, D_MODEL), F32),
            jax.ShapeDtypeStruct((1, D_MODEL), F32),
            jax.ShapeDtypeStruct((N_CT, CW // LANES, LRU_BW, LANES), F32),
            jax.ShapeDtypeStruct((N_CT, CW // LANES, LRU_BW, LANES), F32),
            jax.ShapeDtypeStruct((1, D_MODEL), F32),
            jax.ShapeDtypeStruct((1, D_MODEL), F32),
            jax.ShapeDtypeStruct((1, D_MODEL), F32),
        ],
        scratch_shapes=[pltpu.VMEM((tc + 8, CW), F32), pltpu.VMEM((tc + 8, CW), F32),
                        pltpu.VMEM((8, CW), F32), pltpu.VMEM((8, CW), F32)] + _scan_scratch(tc)
        + [pltpu.VMEM((CW, CW), F32), pltpu.VMEM((CW, CW), F32)],
        compiler_params=_params(("parallel", "arbitrary", "arbitrary")),
    )(dya, proj, proj, proj, hlru, hlru, conv_w, conv_b, wx_bd, wa_bd, bx, ba, lam, *deps)


def _retention_tables(S):
    half = DK // 2
    freqs = ROPE_THETA ** (-jnp.arange(half, dtype=F32) / half)
    ang = jnp.arange(S, dtype=F32)[:, None] * freqs[None, :]
    log_g = jnp.log1p(-(2.0 ** (-5.0 - jnp.arange(HEADS, dtype=F32))))
    idx = jnp.arange(CHUNK, dtype=F32)
    diff = idx[:, None] - idx[None, :]
    inner = jnp.where(diff >= 0, jnp.exp(jnp.maximum(diff, 0.0)[None] * log_g[:, None, None]), 0.0)
    cross = jnp.exp((idx[None, :] + 1.0) * log_g[:, None])[:, :, None]
    state = jnp.exp((CHUNK - 1.0 - idx[None, :]) * log_g[:, None])[:, :, None]
    gam = jnp.broadcast_to(jnp.exp(CHUNK * log_g)[:, None, None], (HEADS, 1, DK))
    return jnp.cos(ang), jnp.sin(ang), inner, cross, state, gam


def _rot(x, cos, sin):
    half = DK // 2
    x1, x2 = x[:, :half], x[:, half:]
    return jnp.concatenate([x1 * cos - x2 * sin, x1 * sin + x2 * cos], axis=-1)


def _rot_t(y, cos, sin):
    half = DK // 2
    y1, y2 = y[:, :half], y[:, half:]
    return jnp.concatenate([y1 * cos + y2 * sin, y2 * cos - y1 * sin], axis=-1)


def _groupnorm(o):
    mu = jnp.mean(o, axis=-1, keepdims=True)
    oc = o - mu
    rs = lax.rsqrt(jnp.mean(oc * oc, axis=-1, keepdims=True) + EPS)
    return oc * rs, rs


def _ret_specs(B, chunk_of):
    rows = RET_CHUNKS * CHUNK
    qkv = lambda g: pl.BlockSpec((B, rows, D_MODEL), lambda c: (0, chunk_of(c), g))
    act = pl.BlockSpec((B, rows, D_MODEL), lambda c: (0, chunk_of(c), 0))
    rope = pl.BlockSpec((rows, DK // 2), lambda c: (chunk_of(c), 0))
    dmat = pl.BlockSpec((HEADS, CHUNK, CHUNK), lambda c: (0, 0, 0))
    dvec = pl.BlockSpec((HEADS, CHUNK, 1), lambda c: (0, 0, 0))
    hrow = pl.BlockSpec((HEADS, 1, DK), lambda c: (0, 0, 0))
    rst = pl.BlockSpec((RET_CHUNKS, B, HEADS, DK, DK), lambda c: (chunk_of(c), 0, 0, 0, 0))
    return qkv, act, rope, dmat, dvec, hrow, rst


def _ret_fwd(proj, tables, gain3, B, S):
    T = B * S
    nc = S // CHUNK
    cos, sin, dmat_t, cd_t, sd_t, gam_t = tables

    def body(q_ref, k_ref, v_ref, gb_ref, cos_ref, sin_ref, dm_ref, cd_ref, sd_ref, gam_ref, gain_ref,
             o_ref, yb_ref, rs_ref, state_ref):
        @pl.when(pl.program_id(0) == 0)
        def _():
            state_ref[...] = jnp.zeros_like(state_ref)

        for cc, b, h in [(cc, b, h) for cc in range(RET_CHUNKS) for b in range(B) for h in range(HEADS)]:
            rows = slice(cc * CHUNK, (cc + 1) * CHUNK)
            cos_t, sin_t = cos_ref[rows, :], sin_ref[rows, :]
            cols = slice(h * DK, (h + 1) * DK)
            qb = _c(_rot(q_ref[b, rows, cols], cos_t, sin_t))
            kb = _c(_rot(k_ref[b, rows, cols], cos_t, sin_t) * (DK ** -0.5))
            v = v_ref[b, rows, cols]
            state = state_ref[b, h]
            sb = _c(state)
            rs_ref[cc, b, h] = sb
            scores = _dot_nt(qb, kb) * dm_ref[h]
            o = _dot(_c(scores), _c(v)) + _dot(qb, sb) * cd_ref[h]
            state_ref[b, h] = gam_ref[h] * state + _dot_tn(kb, _c(v * sd_ref[h]))
            o_ref[b, rows, cols] = o
            n, _ = _groupnorm(o)
            gb = gb_ref[b, rows, cols]
            yb_ref[b, rows, cols] = (gb * _sigmoid(gb) * (n * gain_ref[h])).astype(yb_ref.dtype)

    qkv, act, rope, dmat, dvec, hrow, rst = _ret_specs(B, lambda c: c)
    proj3 = proj.reshape(B, S, proj.shape[1])
    o_pre, yb, states = pl.pallas_call(
        body,
        name="ret_fwd",
        grid=(nc // RET_CHUNKS,),
        in_specs=[qkv(2), qkv(3), qkv(4), qkv(5), rope, rope, dmat, dvec, dvec, hrow, hrow],
        out_specs=[act, act, rst],
        out_shape=[
            jax.ShapeDtypeStruct((B, S, D_MODEL), F32),
            jax.ShapeDtypeStruct((B, S, D_MODEL), _MXU),
            jax.ShapeDtypeStruct((nc, B, HEADS, DK, DK), _MXU),
        ],
        scratch_shapes=[pltpu.VMEM((B, HEADS, DK, DK), F32)],
        compiler_params=_params(("arbitrary",)),
    )(proj3, proj3, proj3, proj3, cos, sin, dmat_t, cd_t, sd_t, gam_t, gain3)
    return o_pre.reshape(T, D_MODEL), yb.reshape(T, D_MODEL), states


def _ret_bwd(dyb, o_pre, proj, states, tables, gain3, B, S, deps=()):
    T = B * S
    nc = S // CHUNK
    cos, sin, dmat_t, cd_t, sd_t, gam_t = tables

    def body(dyb_ref, o_ref, q_ref, k_ref, v_ref, gb_ref, rs_ref, cos_ref, sin_ref, dm_ref, cd_ref, sd_ref, gam_ref,
             gain_ref, dr_ref, dgain_ref, dstate_ref):
        @pl.when(pl.program_id(0) == 0)
        def _():
            dstate_ref[...] = jnp.zeros_like(dstate_ref)
            dgain_ref[...] = jnp.zeros_like(dgain_ref)

        for cc, b, h in [(cc, b, h) for cc in reversed(range(RET_CHUNKS)) for b in range(B) for h in range(HEADS)]:
            rows = slice(cc * CHUNK, (cc + 1) * CHUNK)
            cos_t, sin_t = cos_ref[rows, :], sin_ref[rows, :]
            cols = slice(h * DK, (h + 1) * DK)
            gain = gain_ref[h]
            n, rs = _groupnorm(o_ref[b, rows, cols])
            gb = gb_ref[b, rows, cols]
            sg = _sigmoid(gb)
            dy = dyb_ref[b, rows, cols]
            part = lambda g: slice(g * D_MODEL + h * DK, g * D_MODEL + (h + 1) * DK)
            dr_ref[b, rows, part(3)] = (dy * (n * gain) * (sg * (1.0 + gb * (1.0 - sg)))).astype(dr_ref.dtype)
            dgn = dy * (gb * sg)
            dgain_ref[h] += jnp.sum(dgn * n, axis=0, keepdims=True)
            dn = dgn * gain
            do = rs * (dn - jnp.mean(dn, axis=-1, keepdims=True) - n * jnp.mean(dn * n, axis=-1, keepdims=True))

            qb = _c(_rot(q_ref[b, rows, cols], cos_t, sin_t))
            kb = _c(_rot(k_ref[b, rows, cols], cos_t, sin_t) * (DK ** -0.5))
            v = v_ref[b, rows, cols]
            vb = _c(v)
            vsb = _c(v * sd_ref[h])
            dob = _c(do)
            docb = _c(do * cd_ref[h])
            dmat = dm_ref[h]
            dstate = dstate_ref[b, h]
            dsb = _c(dstate)
            pb = _c(_dot_nt(qb, kb) * dmat)
            dsc = _c(_dot_nt(dob, vb) * dmat)
            dq = _dot(dsc, kb) + _dot_nt(docb, rs_ref[cc, b, h])
            dk = _dot_tn(dsc, qb) + _dot_nt(vsb, dsb)
            dv = _dot_tn(pb, dob) + _dot(kb, dsb) * sd_ref[h]
            dstate_ref[b, h] = gam_ref[h] * dstate + _dot_tn(qb, docb)
            dr_ref[b, rows, part(0)] = _rot_t(dq, cos_t, sin_t).astype(dr_ref.dtype)
            dr_ref[b, rows, part(1)] = (_rot_t(dk, cos_t, sin_t) * (DK ** -0.5)).astype(dr_ref.dtype)
            dr_ref[b, rows, part(2)] = dv.astype(dr_ref.dtype)

    n_steps = nc // RET_CHUNKS
    qkv, act, rope, dmat, dvec, hrow, rst = _ret_specs(B, lambda c: n_steps - 1 - c)
    wide = pl.BlockSpec((B, RET_CHUNKS * CHUNK, 4 * D_MODEL), lambda c: (0, n_steps - 1 - c, 0))
    proj3 = proj.reshape(B, S, proj.shape[1])
    dr, dgain = pl.pallas_call(
        _after(body, 14, deps),
        name="ret_bwd",
        grid=(n_steps,),
        in_specs=[act, act, qkv(2), qkv(3), qkv(4), qkv(5), rst, rope, rope, dmat, dvec, dvec, hrow, hrow]
        + [ANY_SPEC] * len(deps),
        out_specs=[wide, hrow],
        out_shape=[jax.ShapeDtypeStruct((B, S, 4 * D_MODEL), _MXU), jax.ShapeDtypeStruct((HEADS, 1, DK), F32)],
        scratch_shapes=[pltpu.VMEM((B, HEADS, DK, DK), F32)],
        compiler_params=_params(("arbitrary",)),
    )(dyb.reshape(B, S, D_MODEL), o_pre.reshape(B, S, D_MODEL), proj3, proj3, proj3, proj3, states, cos, sin, dmat_t,
      cd_t, sd_t, gam_t, gain3, *deps)
    return dr.reshape(T, 4 * D_MODEL), dgain


def _mid(ya, yb, proj, x2d, tgt2d, wpa, wpb, wout, g_fin):
    T = x2d.shape[0]
    tm = min(MID_TILE, T)
    n_steps = T // tm
    rows = D_MODEL // (2 * N_CHIPS)

    def body(ya_ref, yb_ref, ma_ref, mb_ref, x_ref, t_ref, gf_ref, wpa_hbm, wpb_hbm, wout_hbm,
             loss_ref, dx2_ref, dya_ref, dyb_ref, dm_ref, dgf_ref, gw_hbm, w_ref, acc_ref, sem):
        i = pl.program_id(0)

        @pl.when(i == 0)
        def _():
            loads = [pltpu.make_async_copy(src, w_ref.at[k], sem.at[k]) for k, src in enumerate((wpa_hbm, wpb_hbm, wout_hbm))]
            for cp in loads:
                cp.start()
            for cp in loads:
                cp.wait()
            acc_ref[...] = jnp.zeros_like(acc_ref)
            loss_ref[...] = jnp.zeros_like(loss_ref)
            dgf_ref[...] = jnp.zeros_like(dgf_ref)

        ya_t, yb_t = ya_ref[...], yb_ref[...]
        out_a = _dot(ya_t, w_ref[0])
        out_b = _dot(yb_t, w_ref[1])
        sa = _sigmoid(ma_ref[...])
        sb = _sigmoid(mb_ref[...])
        mgb = _c(sa * out_a + sb * out_b)
        x2 = x_ref[...] + _dot(mgb, w_ref[2])
        r2 = lax.rsqrt(jnp.mean(x2 * x2, axis=-1, keepdims=True) + EPS)
        nx = x2 * r2
        gf = gf_ref[...]
        err = nx * gf - t_ref[...]
        loss_ref[...] += 0.5 * jnp.sum(jnp.mean(err * err, axis=-1, keepdims=True), axis=0, keepdims=True)
        dy = err * (1.0 / D_MODEL)
        dgf_ref[...] += jnp.sum(dy * nx, axis=0, keepdims=True)
        dyg = dy * gf
        dx2 = r2 * (dyg - nx * jnp.mean(dyg * nx, axis=-1, keepdims=True))
        dx2_ref[...] = dx2
        dx2b = _c(dx2)
        dmg = _dot_nt(dx2b, w_ref[2])
        acc_ref[2] += _dot_tn(mgb, dx2b)
        dm_ref[:, :D_MODEL] = (dmg * out_a * sa * (1.0 - sa)).astype(dm_ref.dtype)
        dm_ref[:, D_MODEL:] = (dmg * out_b * sb * (1.0 - sb)).astype(dm_ref.dtype)
        dab = _c(dmg * sa)
        dbb = _c(dmg * sb)
        dya_ref[...] = _dot_nt(dab, w_ref[0])
        dyb_ref[...] = _dot_nt(dbb, w_ref[1])
        acc_ref[0] += _dot_tn(ya_t, dab)
        acc_ref[1] += _dot_tn(yb_t, dbb)

        @pl.when(i == n_steps - 1)
        def _():
            copies = [pltpu.make_async_copy(acc_ref.at[k, pl.ds((2 * p + hf) * rows, rows), :], gw_hbm.at[p, hf, k],
                                            sem.at[(k * N_CHIPS + p) * 2 + hf])
                      for k in range(3) for p in range(N_CHIPS) for hf in range(2)]
            for cp in copies:
                cp.start()
            for cp in copies:
                cp.wait()

    tile = lambda j: pl.BlockSpec((tm, D_MODEL), lambda i: (i, j))
    one = pl.BlockSpec((1, D_MODEL), lambda i: (0, 0))
    anyspec = pl.BlockSpec(memory_space=pl.ANY)
    return pl.pallas_call(
        body,
        name="mid",
        grid=(n_steps,),
        in_specs=[tile(0), tile(0), tile(6), tile(7), tile(0), tile(0), one, anyspec, anyspec, anyspec],
        out_specs=[pl.BlockSpec((1, 1), lambda i: (0, 0)), tile(0), tile(0), tile(0),
                   pl.BlockSpec((tm, 2 * D_MODEL), lambda i: (i, 0)), one, anyspec],
        out_shape=[
            jax.ShapeDtypeStruct((1, 1), F32),
            jax.ShapeDtypeStruct((T, D_MODEL), F32),
            jax.ShapeDtypeStruct((T, D_MODEL), F32),
            jax.ShapeDtypeStruct((T, D_MODEL), F32),
            jax.ShapeDtypeStruct((T, 2 * D_MODEL), _MXU),
            jax.ShapeDtypeStruct((1, D_MODEL), F32),
            jax.ShapeDtypeStruct((N_CHIPS, 2, 3, rows, D_MODEL), F32),
        ],
        scratch_shapes=[pltpu.VMEM((3, D_MODEL, D_MODEL), _MXU), pltpu.VMEM((3, D_MODEL, D_MODEL), F32),
                        pltpu.SemaphoreType.DMA((3 * N_CHIPS * 2,))],
        compiler_params=_params(("arbitrary",)),
    )(ya, yb, proj, proj, x2d, tgt2d, g_fin, wpa, wpb, wout)


def _inproj_bwd_dx(dparts, w_all, x2d, dx2, g_in, first, count, prev, name, deps=()):
    T = x2d.shape[0]
    tm = min(DX_TILE, T)
    n_d = len(dparts)
    groups = [(a, k) for a, d in enumerate(dparts) for k in range(d.shape[1] // D_MODEL)]
    dg_start = jnp.zeros((1, D_MODEL), F32) if prev is None else prev[1]
    carried = () if prev is None else (prev[0],)

    def body(*refs):
        d_refs = refs[:n_d]
        x_ref, dx2_ref, g_ref, dg0_ref, w_hbm = refs[n_d:n_d + 5]
        dx_ref, dg_ref, w_ref, sem = refs[-4:]

        def load(j):
            part = (j // 2, slice(None), pl.ds((j % 2) * D_MODEL, D_MODEL))
            return pltpu.make_async_copy(w_hbm.at[part], w_ref.at[part], sem.at[j])

        def tile(before_group):
            dh = jnp.zeros((tm, D_MODEL), F32)
            for j, (a, k) in enumerate(groups):
                before_group(j)
                dh = dh + _dot_nt(d_refs[a][:, k * D_MODEL:(k + 1) * D_MODEL],
                                  w_ref[j // 2, :, (j % 2) * D_MODEL:(j % 2 + 1) * D_MODEL])
            x = x_ref[...]
            r = lax.rsqrt(jnp.mean(x * x, axis=-1, keepdims=True) + EPS)
            nx = x * r
            dg_ref[...] += jnp.sum(dh * nx, axis=0, keepdims=True)
            dhg = dh * g_ref[...]
            dx_ref[...] = dx2_ref[...] + r * (dhg - nx * jnp.mean(dhg * nx, axis=-1, keepdims=True))

        first = pl.program_id(0) == 0

        @pl.when(first)
        def _():
            for j in range(len(groups)):
                load(j).start()
            dg_ref[...] = dg0_ref[...]
            tile(lambda j: load(j).wait())

        @pl.when(jnp.logical_not(first))
        def _():
            tile(lambda j: None)

    tile = pl.BlockSpec((tm, D_MODEL), lambda i: (first + i, 0))
    one = pl.BlockSpec((1, D_MODEL), lambda i: (0, 0))
    return pl.pallas_call(
        body,
        name=name,
        grid=(count,),
        in_specs=[pl.BlockSpec((tm, d.shape[1]), lambda i: (first + i, 0)) for d in dparts]
        + [tile, tile, one, one, ANY_SPEC] + [ANY_SPEC] * (len(carried) + len(deps)),
        out_specs=[tile, one],
        out_shape=[jax.ShapeDtypeStruct((T, D_MODEL), F32), jax.ShapeDtypeStruct((1, D_MODEL), F32)],
        input_output_aliases={n_d + 5: 0} if carried else {},
        scratch_shapes=[pltpu.VMEM(w_all.shape, w_all.dtype), pltpu.SemaphoreType.DMA((len(groups),))],
        compiler_params=_params(("arbitrary",)),
    )(*dparts, x2d, dx2, g_in, dg_start, w_all, *carried, *deps)


def _inproj_bwd_dw(ht, dparts, name, deps=()):
    T = ht.shape[1]
    tn = DW_COLS
    half = D_MODEL // 2
    per_chip = 2 * D_MODEL // tn
    n_d = len(dparts)
    tiles = [(a, t) for a, d in enumerate(dparts) for t in range(d.shape[1] // tn)]
    offs = [sum(d.shape[1] // tn for d in dparts[:a]) for a in range(n_d)]

    def body(*refs):
        ht_hbm = refs[0]
        d_refs = refs[1:1 + n_d]
        out_ref, ht_ref, sem = refs[-3:]
        t = pl.program_id(0)

        def load(k):
            cols = pl.ds(k * (T // DW_LOADS), T // DW_LOADS)
            return pltpu.make_async_copy(ht_hbm.at[:, cols], ht_ref.at[:, cols], sem.at[k])

        def store(g):
            out_ref[0, 0] = g[:half]
            out_ref[0, 1] = g[half:]

        @pl.when(t == 0)
        def _():
            for k in range(DW_LOADS):
                load(k).start()
            g = jnp.zeros((D_MODEL, tn), F32)
            for k in range(DW_LOADS):
                load(k).wait()
                tokens = slice(k * (T // DW_LOADS), (k + 1) * (T // DW_LOADS))
                g = g + _dot(ht_ref[:, tokens], d_refs[0][tokens, :])
            store(g)

        for a in range(n_d):
            lo, hi = max(offs[a], 1), offs[a] + dparts[a].shape[1] // tn

            @pl.when((t >= lo) & (t < hi))
            def _(a=a):
                store(_dot(ht_ref[...], d_refs[a][...]))

    def dspec(a):
        n_a = dparts[a].shape[1] // tn
        return pl.BlockSpec((T, tn), lambda t: (0, jnp.clip(t - offs[a], 0, n_a - 1)))

    return pl.pallas_call(
        body,
        name=name,
        grid=(len(tiles),),
        in_specs=[ANY_SPEC] + [dspec(a) for a in range(n_d)] + [ANY_SPEC] * len(deps),
        out_specs=pl.BlockSpec((1, 2, half, tn), lambda t: (t // per_chip, 0, 0, t % per_chip)),
        out_shape=jax.ShapeDtypeStruct((len(tiles) // per_chip, 2, half, 2 * D_MODEL), F32),
        scratch_shapes=[pltpu.VMEM(ht.shape, ht.dtype), pltpu.SemaphoreType.DMA((DW_LOADS,))],
        compiler_params=_params(("arbitrary",)),
    )(ht, *dparts, *deps)


def _coords():
    return lax.axis_index("x"), lax.axis_index("y"), lax.axis_index("c")


def _other_chips(x, y):
    return [(1 - x, y), (x, 1 - y), (1 - x, 1 - y)]


def _chunks(rows, n):
    size = rows // n
    return [pl.ds(q * size, size) for q in range(n)]


HBM_SPEC = pl.BlockSpec(memory_space=pltpu.HBM)
SEM_SPEC = pl.BlockSpec(memory_space=pltpu.SEMAPHORE)
DATAFLOW = pltpu.SideEffectType.DATAFLOW_SIDE_EFFECTING


def _copies_start(bufs, plan, n_copies, name, deps=()):
    n = len(bufs)
    n_deps = len(deps)

    def body(*refs):
        ins = refs[:n]
        send_sems, recv_sems = refs[n + n_deps], refs[n + n_deps + 1]
        token = refs[-1]
        for k, send, _ in plan(ins):
            if send is not None:
                src, dst, dev, pred = send
                cp = pltpu.make_async_remote_copy(src_ref=src, dst_ref=dst, send_sem=send_sems.at[k],
                                                  recv_sem=recv_sems.at[k], device_id=dev, device_id_type=MESH)
                if pred is None:
                    cp.start()
                else:
                    pl.when(pred)(cp.start)
        token[...] = jnp.zeros_like(token)

    hbm = [pltpu.with_memory_space_constraint(b, pltpu.HBM) for b in bufs]
    outs = pl.pallas_call(
        body,
        name=name,
        in_specs=[HBM_SPEC] * n + [ANY_SPEC] * n_deps,
        out_specs=(SEM_SPEC, SEM_SPEC, *([HBM_SPEC] * n), pl.BlockSpec(memory_space=pltpu.VMEM)),
        out_shape=(pltpu.SemaphoreType.DMA((n_copies,)), pltpu.SemaphoreType.DMA((n_copies,)),
                   *[pltpu.HBM(b.shape, b.dtype) for b in bufs], jax.ShapeDtypeStruct((8, 128), F32)),
        input_output_aliases={a: 2 + a for a in range(n)},
        compiler_params=pltpu.CompilerParams(has_side_effects=DATAFLOW),
    )(*hbm, *deps)
    return outs[0], outs[1], list(outs[2:2 + n]), outs[-1]


def _copies_wait(send_sems, recv_sems, bufs, after, plan, name, only=None):
    n = len(bufs)

    def body(*refs):
        ins = refs[:n]
        s_sems, r_sems = refs[n], refs[n + 1]
        for k, send, recv in plan(ins):
            if only is not None and k not in only:
                continue
            if send is not None:
                src, dst, dev, pred = send
                cp = pltpu.make_async_remote_copy(src_ref=src, dst_ref=dst, send_sem=s_sems.at[k],
                                                  recv_sem=r_sems.at[k], device_id=dev, device_id_type=MESH)
                if pred is None:
                    cp.wait_send()
                else:
                    pl.when(pred)(cp.wait_send)
            if recv is not None:
                dst, pred = recv
                cp = pltpu.make_async_remote_copy(src_ref=dst, dst_ref=dst, send_sem=s_sems.at[k],
                                                  recv_sem=r_sems.at[k], device_id=_coords(), device_id_type=MESH)
                if pred is None:
                    cp.wait_recv()
                else:
                    pl.when(pred)(cp.wait_recv)

    outs = pl.pallas_call(
        body,
        name=name,
        in_specs=[HBM_SPEC] * n + [SEM_SPEC, SEM_SPEC, pl.BlockSpec(memory_space=pl.ANY)],
        out_specs=[HBM_SPEC] * n,
        out_shape=[pltpu.HBM(b.shape, b.dtype) for b in bufs],
        input_output_aliases={a: a for a in range(n)},
        compiler_params=pltpu.CompilerParams(has_side_effects=DATAFLOW),
    )(*bufs, send_sems, recv_sems, after)
    return list(outs)


def _gather_plan(n_bufs):
    def plan(refs):
        x, y, c = _coords()
        me = 2 * x + y
        out = []
        for k, (px, py) in enumerate(_other_chips(x, y)):
            for a in range(n_bufs):
                out.append((k * n_bufs + a, (refs[a].at[me], refs[a].at[me], (px, py, c), None),
                            (refs[a].at[2 * px + py], None)))
        return out
    return plan


def _cast_into_slot(ws, name, deps=()):
    n = len(ws)
    nt = 2

    def body(s_ref, *refs):
        outs = refs[len(refs) - n:]
        for a in range(n):
            outs[a][0] = refs[a][...].astype(outs[a].dtype)

    xi, yi, _ = _coords()
    return pl.pallas_call(
        body,
        name=name,
        grid_spec=pltpu.PrefetchScalarGridSpec(
            num_scalar_prefetch=1,
            grid=(2, nt),
            in_specs=[pl.BlockSpec((1, w.shape[1] // nt, w.shape[2]), lambda hf, i, s: (hf, i, 0)) for w in ws]
            + [ANY_SPEC] * len(deps),
            out_specs=[pl.BlockSpec((1, 1, w.shape[1] // nt, w.shape[2]), lambda hf, i, s: (s[0], hf, i, 0)) for w in ws],
        ),
        out_shape=[jax.ShapeDtypeStruct((N_CHIPS,) + w.shape, _MXU) for w in ws],
        compiler_params=_params(("parallel", "parallel")),
    )((2 * xi + yi).reshape(1).astype(jnp.int32), *ws, *deps)


def _chip_gather_plan(stage, n_bufs):
    def plan(refs):
        x, y, c = _coords()
        me = 2 * x + y
        near = [(1 - x, y), (x, 1 - y)]
        slots = [2 * (1 - x) + y, 2 * x + (1 - y), 2 * (1 - x) + (1 - y)]
        sibling = (x, y, 1 - c)
        pass_to = (jnp.where(c == 0, x, 1 - x), jnp.where(c == 0, 1 - y, y), c)
        pass_slot = jnp.where(c == 0, slots[0], slots[1])
        out = []

        def move(src_slot, to, land_slot, land_core, pieces):
            for a, buf in enumerate(refs):
                for rows in _chunks(buf.shape[2], pieces[a]):
                    out.append((len(out), (buf.at[src_slot, c, rows], buf.at[src_slot, c, rows], to, None),
                                (buf.at[land_slot, land_core, rows], None)))

        if stage == "near":
            for k, chip in enumerate(near):
                move(me, (*chip, c), slots[k], c, NEAR_PIECES[:n_bufs])
        elif stage == "pass":
            move(pass_slot, pass_to, slots[2], c, PASS_PIECES[:n_bufs])
            for k in range(2):
                move(slots[k], sibling, slots[k], 1 - c, [1] * n_bufs)
        else:
            move(slots[2], sibling, slots[2], 1 - c, [1] * n_bufs)
        return out
    return plan


NEAR_PIECES = (2, 1)
PASS_PIECES = (2, 1)


def _chip_gather_copies(stage, n_bufs):
    if stage == "near":
        return 2 * sum(NEAR_PIECES[:n_bufs]), None
    if stage == "pass":
        n_pass = sum(PASS_PIECES[:n_bufs])
        return n_pass + 2 * n_bufs, set(range(n_pass))
    return n_bufs, None


def _swap_plan(n_slabs):
    def plan(refs):
        x, y, c = _coords()
        out, k = [], 0
        for i, n in enumerate(n_slabs):
            g, land = refs[2 * i], refs[2 * i + 1]
            for p in range(n):
                out.append((k, (g.at[p, 1 - c], land.at[p], (x, y, 1 - c), None), (land.at[p], None)))
                k += 1
        return out
    return plan


def _is_one_of(chip, dests):
    hit = chip == dests[0]
    for d in dests[1:]:
        hit = hit | (chip == d)
    return hit


def _slab_of(chip, dests):
    return sum(j * (chip == d).astype(jnp.int32) for j, d in enumerate(dests))


def _scatter_plan(dest_sets):
    def plan(refs):
        x, y, c = _coords()
        me = 2 * x + y
        out = []
        for k, (px, py) in enumerate(_other_chips(x, y)):
            peer = 2 * px + py
            for i, dests in enumerate(dest_sets):
                cs, land = refs[2 * i], refs[2 * i + 1]
                everyone = len(dests) == N_CHIPS
                send = (cs.at[_slab_of(peer, dests)], land.at[k], (px, py, c),
                        None if everyone else _is_one_of(peer, dests))
                recv = (land.at[k], None if everyone else _is_one_of(me, dests))
                out.append((k * len(dest_sets) + i, send, recv))
        return out
    return plan


def _join_plan(rows, n_pieces):
    def plan(refs):
        x, y, c = _coords()
        (buf,) = refs
        return [(i, (buf.at[c, piece], buf.at[c, piece], (x, y, 1 - c), None), (buf.at[1 - c, piece], None))
                for i, piece in enumerate(_chunks(rows, n_pieces))]
    return plan


def _join_plans(parts):
    def plan(refs):
        out, b0, k0 = [], 0, 0
        for part_plan, n_bufs, n_copies in parts:
            out += [(k0 + k, send, recv) for k, send, recv in part_plan(refs[b0:b0 + n_bufs])]
            b0 += n_bufs
            k0 += n_copies
        return out
    return plan


def _allgather_plan():
    def plan(refs):
        x, y, c = _coords()
        (land,) = refs
        me = 4 * x + 2 * y + c
        out = []
        for r in range(1, 8):
            px = 1 - x if r & 4 else x
            py = 1 - y if r & 2 else y
            pc = 1 - c if r & 1 else c
            out.append((r - 1, (land.at[me], land.at[me], (px, py, pc), None), (land.at[4 * px + 2 * py + pc], None)))
        return out
    return plan


def _sum_gathered(land, shapes, name):
    m = land.shape[1]

    def body(land_ref, *refs):
        outs, acc_ref = refs[:-1], refs[-1]
        acc = land_ref[0]
        for d in range(1, 8):
            acc = acc + land_ref[d]
        acc_ref[...] = acc
        r = 0
        for o_ref, shape in zip(outs, shapes):
            if len(shape) == 3:
                n_blocks, bw, _ = shape
                side = LANES // bw
                for blk in range(n_blocks):
                    first = r + (blk // side) * bw
                    o_ref[blk] = acc_ref[first:first + bw, (blk % side) * bw:(blk % side + 1) * bw]
                r += n_blocks * bw // side
                continue
            n, w = shape
            if w == LANES:
                o_ref[...] = acc_ref[r:r + n, :]
                r += n
            elif w < LANES:
                o_ref[...] = acc_ref[r:r + n, 0:w]
                r += SUBLANES
            else:
                for k in range(n):
                    for q in range(w // LANES):
                        o_ref[k:k + 1, q * LANES:(q + 1) * LANES] = acc_ref[r:r + 1, :]
                        r += 1
        assert r == m, (r, m)

    return pl.pallas_call(
        body,
        name=name,
        out_shape=[jax.ShapeDtypeStruct(s, F32) for s in shapes],
        scratch_shapes=[pltpu.VMEM((m, LANES), F32)],
        compiler_params=_params(),
    )(land)


def _row_tile(rows, cap):
    t = cap
    while rows % t:
        t //= 2
    return t


def _add_my_half(g, r, name):
    n_slabs, _, R, C = g.shape
    tr = R if n_slabs > 1 else _row_tile(R, SUM_ROWS)

    def body(c_ref, g_ref, r_ref, o_ref):
        o_ref[...] = (g_ref[0] + r_ref[...]).astype(o_ref.dtype)

    return pl.pallas_call(
        body,
        name=name,
        grid_spec=pltpu.PrefetchScalarGridSpec(
            num_scalar_prefetch=1,
            grid=(n_slabs, R // tr),
            in_specs=[pl.BlockSpec((1, 1, tr, C), lambda p, i, c_ref: (p, c_ref[0], i, 0)),
                      pl.BlockSpec((1, tr, C), lambda p, i, c_ref: (p, i, 0))],
            out_specs=pl.BlockSpec((1, tr, C), lambda p, i, c_ref: (p, i, 0)),
        ),
        out_shape=jax.ShapeDtypeStruct(r.shape, jnp.bfloat16),
        compiler_params=_params(("parallel", "parallel")),
    )(lax.axis_index("c").reshape(1).astype(jnp.int32), g, r)


def _sum_slabs(own, got, name, deps=()):
    _, R, C = own.shape
    tr = _row_tile(R, SUM_ROWS)

    def body(s_ref, own_ref, got_ref, *rest):
        rest[-1][0] = ((own_ref[0].astype(F32) + got_ref[0].astype(F32)) + got_ref[1].astype(F32)) + got_ref[2].astype(F32)

    xi, yi, ci = _coords()
    return pl.pallas_call(
        body,
        name=name,
        grid_spec=pltpu.PrefetchScalarGridSpec(
            num_scalar_prefetch=1,
            grid=(R // tr,),
            in_specs=[pl.BlockSpec((1, tr, C), lambda i, s: (s[0], i, 0)),
                      pl.BlockSpec((3, tr, C), lambda i, s: (0, i, 0))] + [ANY_SPEC] * len(deps),
            out_specs=pl.BlockSpec((1, tr, C), lambda i, s: (s[1], i, 0)),
        ),
        out_shape=jax.ShapeDtypeStruct((2, R, C), F32),
        compiler_params=_params(("parallel",)),
    )(jnp.stack([2 * xi + yi, ci]).astype(jnp.int32), own, got, *deps)


def _sum_parts(owns, got, dest_sets, name):
    n = len(owns)
    _, R, C = owns[0].shape
    tr = _row_tile(R, SUM_ROWS)

    def body(s_ref, *refs):
        got_ref, o_ref = refs[n], refs[-1]
        total = jnp.zeros((tr, C), F32)
        for i in range(n):
            total = total + jnp.where(s_ref[2 + 2 * i] == 1, refs[i][0].astype(F32), 0.0)
        o_ref[0] = ((total + got_ref[0].astype(F32)) + got_ref[1].astype(F32)) + got_ref[2].astype(F32)

    xi, yi, ci = _coords()
    me = 2 * xi + yi
    scalars = [ci, ci]
    for dests in dest_sets:
        scalars += [_is_one_of(me, dests).astype(jnp.int32), _slab_of(me, dests)]
    own_spec = lambda i: pl.BlockSpec((1, tr, C), lambda r, s: (s[3 + 2 * i], r, 0))
    return pl.pallas_call(
        body,
        name=name,
        grid_spec=pltpu.PrefetchScalarGridSpec(
            num_scalar_prefetch=1,
            grid=(R // tr,),
            in_specs=[own_spec(i) for i in range(n)] + [pl.BlockSpec((3, tr, C), lambda r, s: (0, r, 0))],
            out_specs=pl.BlockSpec((1, tr, C), lambda r, s: (s[0], r, 0)),
        ),
        out_shape=jax.ShapeDtypeStruct((2, R, C), F32),
        compiler_params=_params(("parallel",)),
    )(jnp.stack(scalars).astype(jnp.int32), *owns, got)


def _adamw_math(w, g, m, v):
    m = ADAM_B1 * m + (1.0 - ADAM_B1) * g
    v = ADAM_B2 * v + (1.0 - ADAM_B2) * (g * g)
    m_hat = m / (1.0 - ADAM_B1 ** ADAM_STEP)
    v_hat = v / (1.0 - ADAM_B2 ** ADAM_STEP)
    delta = -ADAM_LR * (m_hat / (jnp.sqrt(v_hat) + ADAM_EPS) + ADAM_WD * w)
    return delta, m, v


def _adamw_halves(ws, g, ms, vs, half, prev, name, deps=()):
    n = len(ws)
    _, _, R, C = g.shape
    tr = _row_tile(R, ADAMW_ROWS)
    steps = R // tr
    carried = [] if prev is None else [a for four in prev for a in four]
    both = half is None
    which = (lambda i, s: i // steps) if both else (lambda i, s: s[0])
    half = 0 if both else half

    def body(s_ref, *refs):
        w_refs, g_refs, m_refs, v_refs = (refs[k * n:(k + 1) * n] for k in range(4))
        outs = refs[len(refs) - 4 * n:]
        for a in range(n):
            grad = g_refs[a][0, 0]
            d, mn, vn = _adamw_math(w_refs[a][...], grad, m_refs[a][...], v_refs[a][...])
            for o, val in zip(outs[4 * a:4 * a + 4], (grad, d, mn, vn)):
                o[...] = val

    rows = pl.BlockSpec((tr, C), lambda i, s: (which(i, s) * steps + i % steps, 0))
    grad_spec = lambda a: pl.BlockSpec((1, 1, tr, C), lambda i, s: (which(i, s), a, i % steps, 0))
    n_in = 4 * n
    outs = pl.pallas_call(
        body,
        name=name,
        grid_spec=pltpu.PrefetchScalarGridSpec(
            num_scalar_prefetch=1,
            grid=(2 * steps if both else steps,),
            in_specs=[rows] * n + [grad_spec(a) for a in range(n)] + [rows] * (2 * n)
            + [ANY_SPEC] * (len(carried) + len(deps)),
            out_specs=[rows] * (4 * n),
        ),
        out_shape=[jax.ShapeDtypeStruct((2 * R, C), F32)] * (4 * n),
        input_output_aliases={1 + n_in + k: k for k in range(len(carried))},
        compiler_params=_params(("parallel",)),
    )(jnp.reshape(half, (1,)).astype(jnp.int32), *ws, *([g] * n), *ms, *vs, *carried, *deps)
    return [outs[4 * a:4 * a + 4] for a in range(n)]


def _adamw_small(ws, gs, ms, vs, name):
    n = len(ws)

    def body(*refs):
        for a in range(n):
            d, mn, vn = _adamw_math(refs[a][...], refs[n + a][...], refs[2 * n + a][...], refs[3 * n + a][...])
            refs[4 * n + a][...] = d
            refs[5 * n + a][...] = mn
            refs[6 * n + a][...] = vn

    shapes = [jax.ShapeDtypeStruct(w.shape, F32) for w in ws]
    outs = pl.pallas_call(
        body,
        name=name,
        out_shape=shapes * 3,
        compiler_params=_params(),
    )(*ws, *gs, *ms, *vs)
    return outs[:n], outs[n:2 * n], outs[2 * n:]


def _to_blockdiag(w):
    per = CW // LRU_BW
    w4 = w.reshape(N_CT, per, LRU_BW, LRU_BW)
    eye = jnp.eye(per, dtype=w.dtype)
    return (w4[:, :, :, None, :] * eye[None, :, None, :, None]).reshape(N_CT, CW, CW)


def _local_grads(x2d, tgt2d, B, S, g_in, in_proj, conv_b, gate_x_w, gate_x_b, gate_a_w, gate_a_b, lam,
                 proj_weights, g_fin, reduce):
    wx_bd = _c(_to_blockdiag(gate_x_w))
    wa_bd = _c(_to_blockdiag(gate_a_w))
    tables = _retention_tables(S)

    proj, ht, w_all, conv_w, gain = in_proj(x2d, g_in, (*tables, wx_bd, wa_bd))
    gain3 = gain.reshape(HEADS, 1, DK)
    hlru, ya = _lru_fwd(proj, conv_w, conv_b, wx_bd, wa_bd, gate_x_b, gate_a_b, lam, B, S)
    o_pre, yb, states = _ret_fwd(proj, tables, gain3, B, S)
    wpa, wpb, wout = proj_weights(yb)
    loss, dx2, dya, dyb, dm, dgf, gw_proj = _mid(ya, yb, proj, x2d, tgt2d, wpa, wpb, wout, g_fin)
    g3 = _inproj_bwd_dw(ht, [dm], "inproj_bwd_dw_m")
    deps = reduce.m_ready(gw_proj, g3)
    dr, dgain = _ret_bwd(dyb, o_pre, proj, states, tables, gain3, B, S, deps)
    deps = reduce.ret_done(dr)
    g12 = _inproj_bwd_dw(ht, [dr], "inproj_bwd_dw_r", deps)
    deps = reduce.r_ready(g12)
    dxa, dga, dcw, dcb, dwx, dwa, dbx, dba, dlam = _lru_bwd(
        dya, proj, hlru, conv_w, conv_b, wx_bd, wa_bd, gate_x_b, gate_a_b, lam, B, S, deps)
    small = dict(conv_w=dcw, conv_b=dcb, gate_x_w=dwx, gate_x_b=dbx, gate_a_w=dwa, gate_a_b=dba, lru_lambda=dlam,
                 gn_gain=dgain.reshape(HEADS, DK), norm_final=dgf)
    deps = reduce.lru_done(dxa, _pack_small(small, loss, reduce.slot()))
    g0 = _inproj_bwd_dw(ht, [dxa, dga], "inproj_bwd_dw_a", deps)
    deps = reduce.a_ready(g0)
    n_tiles = x2d.shape[0] // min(DX_TILE, x2d.shape[0])
    grad_x, dgin = _inproj_bwd_dx([dxa, dga, dr, dm], w_all, x2d, dx2, g_in, 0, n_tiles, None, "inproj_bwd_dx", deps)
    return grad_x, dgin


ALL_CHIPS = (0, 1, 2, 3)


class _GradReduce:
    def __init__(self, proj_done):
        self.pending = {}
        self.proj_done = proj_done
        self.land_in = None

    def _start(self, key, parts, name):
        bufs, plans, shared = [], [], None
        for part_bufs, plan, n_copies, part_shared in parts:
            if part_shared is not None:
                shared = len(bufs) + part_shared
            plans.append((plan, len(part_bufs), n_copies))
            bufs += part_bufs
        plan = _join_plans(plans)
        send_sems, recv_sems, bufs, token = _copies_start(bufs, plan, sum(p[2] for p in plans), name + "_start")
        if shared is not None:
            self.land_in = bufs[shared]
        self.pending[key] = (send_sems, recv_sems, bufs, plan, name + "_wait", shared)
        return (token,)

    def _finish(self, key, after):
        send_sems, recv_sems, bufs, plan, name, shared = self.pending.pop(key)
        if shared is not None:
            bufs[shared] = self.land_in
        bufs = _copies_wait(send_sems, recv_sems, bufs, after, plan, name)
        if shared is not None:
            self.land_in = bufs[shared]
        return bufs

    @staticmethod
    def _swap(pieces):
        bufs = []
        for g in pieces:
            bufs += [g, lax.empty((g.shape[0],) + g.shape[2:], F32)]
        n_slabs = [g.shape[0] for g in pieces]
        return bufs, _swap_plan(n_slabs), sum(n_slabs), None

    def _scatter(self, sums, dest_sets):
        bufs = []
        for cs in sums:
            bufs += [cs, lax.empty((3,) + cs.shape[1:], cs.dtype)]
        if self.land_in is not None:
            bufs[-1] = self.land_in
        return bufs, _scatter_plan(dest_sets), 3 * len(sums), len(bufs) - 1

    @staticmethod
    def slot():
        x, y, c = _coords()
        return 4 * x + 2 * y + c

    def _gather8(self, block):
        land = lax.dynamic_update_slice(lax.empty((8,) + block.shape, F32), block[None], (self.slot(), 0, 0))
        return [land], _allgather_plan(), 7, None

    def m_ready(self, gw_proj, g3):
        rows = gw_proj.shape[2] * gw_proj.shape[3]
        return self._start("m", [self._swap([gw_proj.reshape(N_CHIPS, 2, rows, D_MODEL), g3])], "swap_m")

    def ret_done(self, after):
        proj, land_p, g3, land_3 = self._finish("m", after)
        sums_m = [_add_my_half(proj, land_p, "chip_sum_proj"), _add_my_half(g3, land_3, "chip_sum_m")]
        return self._start("sm", [self._scatter(sums_m, [ALL_CHIPS, (3,)])], "scatter_m")

    def r_ready(self, g12):
        return self._start("r", [self._swap([g12])], "swap_r")

    def lru_done(self, after, packed):
        g12, land_12 = self._finish("r", after)
        sums_r = [_add_my_half(g12, land_12, "chip_sum_r")]
        return (self._start("sr", [self._scatter(sums_r, [(1, 2)])], "scatter_r")
                + self._start("small", [([packed], _allgather_plan(), 7, None)], "gather_small"))

    def a_ready(self, g0):
        (token,) = self._start("a", [self._swap([g0])], "swap_a")
        csp, gotp, self.cs3, _ = self._finish("sm", token)
        half_proj = _sum_slabs(csp, gotp, "sum_w_proj")
        g0, land_0 = self._finish("a", half_proj)
        join = ([half_proj], _join_plan(half_proj.shape[1], PROJ_JOIN_PIECES), PROJ_JOIN_PIECES, None)
        return self._start("sa", [self._scatter([_add_my_half(g0, land_0, "chip_sum_a")], [(0,)]), join], "scatter_a")

    def finish(self, dgin, w_in_done):
        (token,) = self._start("n", [self._gather8(dgin)], "gather_norm_in")
        (small,) = self._finish("small", token)
        cs12, _ = self._finish("sr", token)
        cs0, _, g_proj = self._finish("sa", token)
        self.proj_done(g_proj)
        half_in =_sum_parts([self.cs3, cs12, cs0], self.land_in, [(3,), (1, 2), (0,)], "sum_w_in")
        deps = self._start("j", [([half_in], _join_plan(half_in.shape[1], JOIN_PIECES), JOIN_PIECES, None)], "join_w_in")
        first = w_in_done(self.pending["j"][2][0], True, None, deps)
        (g_in,) = self._finish("j", first[1])
        done = w_in_done(g_in, False, first, ())
        (norm_in,) = self._finish("n", done[1])
        return _unpack_small(small), _sum_gathered(norm_in, [(1, D_MODEL)], "sum_norm_in_grad")[0]


_SMALL = ("gate_x_w", "gate_a_w", "conv_w", "conv_b", "gate_x_b", "gate_a_b", "lru_lambda", "gn_gain", "norm_final")
_SMALL_SHAPES = dict(gate_x_w=(LRU_BLOCKS, LRU_BW, LRU_BW), gate_a_w=(LRU_BLOCKS, LRU_BW, LRU_BW),
                     conv_w=(CONV, D_MODEL), conv_b=(1, D_MODEL), gate_x_b=(1, D_MODEL),
                     gate_a_b=(1, D_MODEL), lru_lambda=(1, D_MODEL), gn_gain=(HEADS, DK), norm_final=(1, D_MODEL))


def _pack_small(small, loss, slot):
    parts = [small[k] if small[k].ndim == 2 else small[k].reshape(-1, LANES) for k in _SMALL]
    m = sum(p.size for p in parts) // LANES + SUBLANES

    def body(s_ref, *refs):
        o_ref = refs[-1]
        r = 0
        for ref, part in zip(refs, parts):
            if part.shape[1] == LANES:
                o_ref[0, r:r + part.shape[0], :] = ref[...]
                r += part.shape[0]
                continue
            for k in range(part.shape[0]):
                for q in range(part.shape[1] // LANES):
                    o_ref[0, r:r + 1, :] = ref[k:k + 1, q * LANES:(q + 1) * LANES]
                    r += 1
        o_ref[0, r:r + SUBLANES, :] = jnp.broadcast_to(refs[len(parts)][...], (SUBLANES, LANES))

    return pl.pallas_call(
        body,
        name="pack_small_grads",
        grid_spec=pltpu.PrefetchScalarGridSpec(
            num_scalar_prefetch=1,
            grid=(1,),
            in_specs=[pl.BlockSpec(p.shape, lambda i, s: (0, 0)) for p in parts] + [pl.BlockSpec((1, 1), lambda i, s: (0, 0))],
            out_specs=pl.BlockSpec((1, m, LANES), lambda i, s: (s[0], 0, 0)),
        ),
        out_shape=jax.ShapeDtypeStruct((8, m, LANES), F32),
        compiler_params=_params(("arbitrary",)),
    )(jnp.reshape(slot, (1,)).astype(jnp.int32), *parts, loss)


def _unpack_small(land):
    *sums, loss = _sum_gathered(land, [_SMALL_SHAPES[k] for k in _SMALL] + [(1, 1)], "sum_small_grads")
    return dict(zip(_SMALL, sums)), loss


def kernel(x, norm_in, w_in, conv_w, conv_b, gate_x_w, gate_x_b, gate_a_w, gate_a_b, lru_lambda, gn_gain, w_proj_a, w_proj_b, w_out, norm_final, loss_target, m_norm_in, m_w_in, m_conv_w, m_conv_b, m_gate_x_w, m_gate_x_b, m_gate_a_w, m_gate_a_b, m_lru_lambda, m_gn_gain, m_w_proj_a, m_w_proj_b, m_w_out, m_norm_final, v_norm_in, v_w_in, v_conv_w, v_conv_b, v_gate_x_w, v_gate_x_b, v_gate_a_w, v_gate_a_b, v_lru_lambda, v_gn_gain, v_w_proj_a, v_w_proj_b, v_w_out, v_norm_final):
    B, S, _ = x.shape
    T = B * S
    xi, yi, ci = _coords()
    chip = 2 * xi + yi

    cshard = D_MODEL // N_CHIPS
    mine = _cast_into_slot([w_in[0].reshape(2, D_MODEL // 2, 2 * D_MODEL)], "cast_w_in")
    plan = _gather_plan(3)
    pending_proj = []
    gshard = DK // N_CHIPS
    tiny = jnp.concatenate([conv_w[0], jnp.zeros((4, cshard), F32), jnp.pad(gn_gain[0], ((0, 4), (0, cshard - gshard)))],
                           axis=0).reshape(1, 2, SUBLANES, cshard)
    tiny_buf = lax.dynamic_update_slice(lax.empty((N_CHIPS, 2, SUBLANES, cshard), F32), tiny, (chip, 0, 0, 0))
    near_plan, pass_plan, far_plan = (_chip_gather_plan(stage, 2) for stage in ("near", "pass", "far"))
    (n_near, _), (n_pass, passed_on), (n_far, _) = (_chip_gather_copies(stage, 2) for stage in ("near", "pass", "far"))
    halves = set(range(n_pass)) - passed_on
    near_s, near_r, bufs, near_token = _copies_start([mine[0], tiny_buf], near_plan, n_near, "gather_near_start")

    def in_proj(x2d, g_in, meanwhile):
        as_w = lambda b: b[0].reshape(N_CHIPS, D_MODEL, 2 * D_MODEL)
        slot_x, slot_y, slot_d = 2 * (1 - xi) + yi, 2 * xi + (1 - yi), 2 * (1 - xi) + (1 - yi)
        ids = lambda *chips: jnp.stack(chips).astype(jnp.int32)
        proj, hb, ht = _inproj_first(x2d, g_in, as_w(bufs), ids(chip), "inproj_own", (near_token, *meanwhile))
        got = _copies_wait(near_s, near_r, bufs, proj, near_plan, "gather_near_wait")
        pass_s, pass_r, got, pass_token = _copies_start(got, pass_plan, n_pass, "gather_pass_start")
        mine_proj = _cast_into_slot([w[0].reshape(2, cshard // 2, D_MODEL) for w in (w_proj_a, w_proj_b, w_out)],
                                    "cast_w_proj", (pass_token,))
        got = _copies_wait(pass_s, pass_r, got, mine_proj[0], pass_plan, "gather_pass_wait_halves", only=halves)
        proj = _inproj_more(hb, as_w(got), ids(slot_x, slot_y), proj, "inproj_near")
        got = _copies_wait(pass_s, pass_r, got, proj, pass_plan, "gather_pass_wait_far", only=passed_on)
        far_s, far_r, got, far_token = _copies_start(got, far_plan, n_far, "gather_far_start")
        pending_proj.append(_copies_start(mine_proj, plan, 9, "gather_proj_start", (far_token,)))
        got = _copies_wait(far_s, far_r, got, pending_proj[0][3], far_plan, "gather_far_wait")
        proj = _inproj_more(hb, as_w(got), ids(slot_d), proj, "inproj_far")
        tiny_all = got[1].reshape(N_CHIPS, 2 * SUBLANES, cshard)
        conv_w_full = jnp.transpose(tiny_all[:, 0:CONV, :], (1, 0, 2)).reshape(CONV, D_MODEL)
        gain_full = jnp.transpose(tiny_all[:, 8:8 + HEADS, :gshard], (1, 0, 2)).reshape(HEADS, DK)
        return proj, ht, as_w(got), conv_w_full, gain_full

    def proj_weights(after):
        s_sems, r_sems, pbufs, _ = pending_proj[0]
        got = _copies_wait(s_sems, r_sems, pbufs, after, plan, "gather_proj_wait")
        return [b.reshape(D_MODEL, D_MODEL) for b in got]

    weights = dict(norm_in=norm_in, w_in=w_in, conv_w=conv_w, conv_b=conv_b, gate_x_w=gate_x_w, gate_x_b=gate_x_b,
                   gate_a_w=gate_a_w, gate_a_b=gate_a_b, lru_lambda=lru_lambda, gn_gain=gn_gain, w_proj_a=w_proj_a,
                   w_proj_b=w_proj_b, w_out=w_out, norm_final=norm_final)
    ms = dict(norm_in=m_norm_in, w_in=m_w_in, conv_w=m_conv_w, conv_b=m_conv_b, gate_x_w=m_gate_x_w,
              gate_x_b=m_gate_x_b, gate_a_w=m_gate_a_w, gate_a_b=m_gate_a_b, lru_lambda=m_lru_lambda, gn_gain=m_gn_gain,
              w_proj_a=m_w_proj_a, w_proj_b=m_w_proj_b, w_out=m_w_out, norm_final=m_norm_final)
    vs = dict(norm_in=v_norm_in, w_in=v_w_in, conv_w=v_conv_w, conv_b=v_conv_b, gate_x_w=v_gate_x_w,
              gate_x_b=v_gate_x_b, gate_a_w=v_gate_a_w, gate_a_b=v_gate_a_b, lru_lambda=v_lru_lambda, gn_gain=v_gn_gain,
              w_proj_a=v_w_proj_a, w_proj_b=v_w_proj_b, w_out=v_w_out, norm_final=v_norm_final)
    names = list(weights)
    grads, delta, new_m, new_v = {}, {}, {}, {}

    def update_big(keys, g, half, prev, name, deps=()):
        two = lambda a: a.reshape(a.shape[1], a.shape[2])
        res = _adamw_halves([two(weights[k]) for k in keys], g, [two(ms[k]) for k in keys], [two(vs[k]) for k in keys],
                            half, prev, name, deps)
        for k, (gk, d, mn, vn) in zip(keys, res):
            shp = weights[k].shape
            grads[k], delta[k], new_m[k], new_v[k] = gk.reshape(shp), d.reshape(shp), mn.reshape(shp), vn.reshape(shp)
        return res

    def proj_done(g_proj):
        g4 = g_proj.reshape(2, 3, D_MODEL // (2 * N_CHIPS), D_MODEL)
        return update_big(("w_proj_a", "w_proj_b", "w_out"), g4, None, None, "adamw_proj")[-1][1]

    def w_in_done(g_in, own, prev, deps):
        g4 = g_in.reshape(2, 1, D_MODEL // 2, 2 * D_MODEL)
        return update_big(("w_in",), g4, ci if own else 1 - ci, None if prev is None else [prev],
                          "adamw_w_in_own" if own else "adamw_w_in_other", deps)[0]

    reduce = _GradReduce(proj_done)
    grad_x, dgin = _local_grads(
        x.reshape(T, D_MODEL), loss_target.reshape(T, D_MODEL), B, S, norm_in, in_proj, conv_b,
        gate_x_w[0], gate_x_b, gate_a_w[0], gate_a_b, lru_lambda, proj_weights,
        norm_final.reshape(1, D_MODEL), reduce)

    (gsm, loss), g_norm_in = reduce.finish(dgin.reshape(SUBLANES, LANES), w_in_done)
    loss = loss[0, 0]
    gsm["norm_in"] = g_norm_in
    gsm["conv_w"] = lax.dynamic_slice_in_dim(gsm["conv_w"], chip * cshard, cshard, axis=1)
    gsm["gn_gain"] = lax.dynamic_slice_in_dim(gsm["gn_gain"], chip * gshard, gshard, axis=1)
    smalls = [k for k in names if k not in delta]

    def view(a):
        return a.reshape(1, -1) if a.ndim == 1 else (a.reshape(a.shape[1:]) if a.ndim > 2 else a)

    ds, mns, vns = _adamw_small([view(weights[k]) for k in smalls], [gsm[k].reshape(view(weights[k]).shape) for k in smalls],
                                [view(ms[k]) for k in smalls], [view(vs[k]) for k in smalls], "adamw_small")
    for k, d, mn, vn in zip(smalls, ds, mns, vns):
        shp = weights[k].shape
        grads[k], delta[k], new_m[k], new_v[k] = gsm[k].reshape(shp), d.reshape(shp), mn.reshape(shp), vn.reshape(shp)

    return (loss, grad_x.reshape(B, S, D_MODEL), *[grads[k] for k in names], *[delta[k] for k in names],
            *[new_m[k] for k in names], *[new_v[k] for k in names])
```

```python
import jax
import jax.numpy as jnp
from jax import lax
from jax.experimental import pallas as pl
from jax.experimental.pallas import tpu as pltpu

F32 = jnp.float32
_MXU = jnp.bfloat16

D_MODEL = 1024
N_GROUPS = 8
HEADS = 4
DK = 256
CHUNK = 128
CONV = 4
LRU_BLOCKS = 16
LRU_BW = 64
LRU_C = 8.0
ROPE_THETA = 10000.0
EPS = 1e-6
CW = 256
N_CT = D_MODEL // CW
N_CHIPS = 4
MESH = pl.DeviceIdType.MESH

ADAM_LR = 0.001
ADAM_B1 = 0.9
ADAM_B2 = 0.999
ADAM_EPS = 1e-08
ADAM_WD = 0.01
ADAM_STEP = 10

VMEM_LIMIT = 56 * 1024 * 1024

FIRST_PROJ_TILE = 1024
MORE_PROJ_TILE = 2048
SCAN_TILE = 1024
MID_TILE = 256
DX_TILE = 512
DW_COLS = 512
DW_LOADS = 4
RET_CHUNKS = 2
SUM_ROWS = 256
ADAMW_ROWS = 256
JOIN_PIECES = 8
PROJ_JOIN_PIECES = 4


def _c(v):
    return v.astype(_MXU)


def _dot(a, b):
    return lax.dot_general(a, b, (((1,), (0,)), ((), ())), preferred_element_type=F32)


def _dot_nt(a, b):
    return lax.dot_general(a, b, (((1,), (1,)), ((), ())), preferred_element_type=F32)


def _dot_tn(a, b):
    return lax.dot_general(a, b, (((0,), (0,)), ((), ())), preferred_element_type=F32)


def _sigmoid(z):
    return 0.5 * jnp.tanh(0.5 * z) + 0.5


ANY_SPEC = pl.BlockSpec(memory_space=pl.ANY)


def _after(body, n_in, deps):
    n_deps = len(deps)

    def wrapped(*refs):
        return body(*refs[:n_in], *refs[n_in + n_deps:])

    return wrapped


def _params(sem=None):
    if sem is None:
        return pltpu.CompilerParams(vmem_limit_bytes=VMEM_LIMIT)
    return pltpu.CompilerParams(vmem_limit_bytes=VMEM_LIMIT, dimension_semantics=sem)


def _inproj_first(x2d, g_in, w_all, chips, name, deps=()):
    T = x2d.shape[0]
    tm = min(FIRST_PROJ_TILE, T)
    n_i = T // tm

    def body(s_ref, *refs):
        x_ref, g_ref, w_ref = refs[:3]
        proj_ref, hb_ref, ht_ref, h_all = refs[-4:]
        i = pl.program_id(1)
        rows = pl.ds(pl.multiple_of(i * tm, tm), tm)

        @pl.when(pl.program_id(0) == 0)
        def _():
            x = x_ref[...]
            r = lax.rsqrt(jnp.mean(x * x, axis=-1, keepdims=True) + EPS)
            h = x * r * g_ref[...]
            hb = h.astype(h_all.dtype)
            h_all[rows, :] = hb
            hb_ref[...] = hb
            ht_ref[...] = h.T.astype(ht_ref.dtype)

        proj_ref[...] = _dot(h_all[rows, :], w_ref[0])

    first = lambda j, i: jnp.where(j == 0, i, n_i - 1)
    return pl.pallas_call(
        body,
        name=name,
        grid_spec=pltpu.PrefetchScalarGridSpec(
            num_scalar_prefetch=1,
            grid=(2 * chips.shape[0], n_i),
            in_specs=[
                pl.BlockSpec((tm, D_MODEL), lambda j, i, s: (first(j, i), 0)),
                pl.BlockSpec((1, D_MODEL), lambda j, i, s: (0, 0)),
                pl.BlockSpec((1, D_MODEL, D_MODEL), lambda j, i, s: (s[j // 2], 0, j % 2)),
            ] + [ANY_SPEC] * len(deps),
            out_specs=[
                pl.BlockSpec((tm, D_MODEL), lambda j, i, s: (i, 2 * s[j // 2] + j % 2)),
                pl.BlockSpec((tm, D_MODEL), lambda j, i, s: (first(j, i), 0)),
                pl.BlockSpec((D_MODEL, tm), lambda j, i, s: (0, first(j, i))),
            ],
            scratch_shapes=[pltpu.VMEM((T, D_MODEL), _MXU)],
        ),
        out_shape=[
            jax.ShapeDtypeStruct((T, N_GROUPS * D_MODEL), F32),
            jax.ShapeDtypeStruct((T, D_MODEL), _MXU),
            jax.ShapeDtypeStruct((D_MODEL, T), _MXU),
        ],
        compiler_params=_params(("arbitrary", "arbitrary")),
    )(chips, x2d, g_in, w_all, *deps)


def _inproj_more(hb, w_all, chips, proj, name):
    T = hb.shape[0]
    tm = min(MORE_PROJ_TILE, T)

    def body(s_ref, hb_hbm, w_ref, prev_ref, proj_ref, h_all, sem):
        @pl.when((pl.program_id(0) == 0) & (pl.program_id(1) == 0))
        def _():
            cp = pltpu.make_async_copy(hb_hbm, h_all, sem)
            cp.start()
            cp.wait()

        rows = pl.ds(pl.multiple_of(pl.program_id(1) * tm, tm), tm)
        proj_ref[...] = _dot(h_all[rows, :], w_ref[0])

    return pl.pallas_call(
        body,
        name=name,
        grid_spec=pltpu.PrefetchScalarGridSpec(
            num_scalar_prefetch=1,
            grid=(2 * chips.shape[0], T // tm),
            in_specs=[
                ANY_SPEC,
                pl.BlockSpec((1, D_MODEL, D_MODEL), lambda j, i, s: (s[j // 2], 0, j % 2)),
                ANY_SPEC,
            ],
            out_specs=pl.BlockSpec((tm, D_MODEL), lambda j, i, s: (i, 2 * s[j // 2] + j % 2)),
            scratch_shapes=[pltpu.VMEM((T, D_MODEL), hb.dtype), pltpu.SemaphoreType.DMA],
        ),
        out_shape=jax.ShapeDtypeStruct(proj.shape, F32),
        input_output_aliases={3: 0},
        compiler_params=_params(("arbitrary", "arbitrary")),
    )(chips, hb, w_all, proj)


def _scan_fwd(a, u):
    n = a.shape[0]
    row = lax.broadcasted_iota(jnp.int32, a.shape, 0)
    s = 1
    while s < n:
        m = row >= s
        u = u + a * jnp.where(m, pltpu.roll(u, s, 0), 0.0)
        a = a * jnp.where(m, pltpu.roll(a, s, 0), 1.0)
        s *= 2
    return a, u


def _scan_bwd(b, g):
    n = b.shape[0]
    row = lax.broadcasted_iota(jnp.int32, b.shape, 0)
    s = 1
    while s < n:
        m = row < n - s
        g = g + b * jnp.where(m, pltpu.roll(g, n - s, 0), 0.0)
        b = b * jnp.where(m, pltpu.roll(b, n - s, 0), 1.0)
        s *= 2
    return b, g


LANES = 128
SUBLANES = 8


def _scan_scratch(tc):
    by_lanes = pltpu.VMEM((CW // LANES, tc, LANES), F32)
    return [by_lanes, by_lanes, pltpu.VMEM((tc // SUBLANES, CW), F32), pltpu.VMEM((tc, CW), F32)]


def _scan_tile(a, u, edge, la_ref, lh_ref, c_ref, dst_ref, reverse):
    n, w = a.shape
    groups = n // SUBLANES
    a3 = a.reshape(groups, SUBLANES, w)
    u3 = u.reshape(groups, SUBLANES, w)
    row = lax.broadcasted_iota(jnp.int32, a3.shape, 1)
    for s in (1, 2, 4):
        m = (row < SUBLANES - s) if reverse else (row >= s)
        shift = SUBLANES - s if reverse else s
        u3 = u3 + a3 * jnp.where(m, pltpu.roll(u3, shift, 1), 0.0)
        a3 = a3 * jnp.where(m, pltpu.roll(a3, shift, 1), 1.0)
    al = a3.reshape(n, w)
    hl = u3.reshape(n, w)
    blocks = w // LANES
    for q in range(blocks):
        la_ref[q] = al[:, q * LANES:(q + 1) * LANES]
        lh_ref[q] = hl[:, q * LANES:(q + 1) * LANES]
    ends = pl.ds(0 if reverse else SUBLANES - 1, groups, stride=SUBLANES)
    end_a = jnp.concatenate([la_ref.at[q][ends, :] for q in range(blocks)], axis=-1)
    end_h = jnp.concatenate([lh_ref.at[q][ends, :] for q in range(blocks)], axis=-1)
    prod, part = (_scan_bwd if reverse else _scan_fwd)(end_a, end_h)
    total = part + prod * edge
    g_row = lax.broadcasted_iota(jnp.int32, total.shape, 0)
    if reverse:
        c_ref[...] = jnp.where(g_row == groups - 1, edge, pltpu.roll(total, groups - 1, 0))
    else:
        c_ref[...] = jnp.where(g_row == 0, edge, pltpu.roll(total, 1, 0))
    for g in range(groups):
        rows = slice(g * SUBLANES, (g + 1) * SUBLANES)
        for q in range(blocks):
            cols = slice(q * LANES, (q + 1) * LANES)
            dst_ref[rows, cols] = lh_ref[q, rows, :] + la_ref[q, rows, :] * c_ref[g:g + 1, cols]


def _softplus_neg(lam):
    z = -lam
    return jnp.maximum(z, 0.0) + jnp.log1p(jnp.exp(-jnp.abs(z)))


def _lru_gates(xc, wx_ref, wa_ref, bx_ref, ba_ref, lam_ref):
    xcb = _c(xc)
    i_t = _sigmoid(_dot(xcb, wx_ref[0]) + bx_ref[...])
    r_t = _sigmoid(_dot(xcb, wa_ref[0]) + ba_ref[...])
    sp = _softplus_neg(lam_ref[...])
    log_a = (-LRU_C) * r_t * sp
    a = jnp.exp(log_a)
    mult = jnp.sqrt(1.0 - a * a)
    return xcb, i_t, r_t, sp, a, mult


def _conv_from_ext(ext_ref, xa, cw_ref, cb_ref, tc):
    return (cb_ref[...] + cw_ref[3:4, :] * xa + cw_ref[2:3, :] * ext_ref[7:7 + tc, :]
            + cw_ref[1:2, :] * ext_ref[6:6 + tc, :] + cw_ref[0:1, :] * ext_ref[5:5 + tc, :])


def _lru_fwd(proj, conv_w, conv_b, wx_bd, wa_bd, bx, ba, lam, B, S):
    T = B * S
    tc = min(SCAN_TILE, S)
    nt = S // tc
    h8 = tc // 8

    def body(xa_ref, halo_ref, ga_ref, cw_ref, cb_ref, wx_ref, wa_ref, bx_ref, ba_ref, lam_ref,
             h_ref, ya_ref, ext_ref, carry_ref, la_ref, lh_ref, c_ref):
        t = pl.program_id(2)

        @pl.when(t == 0)
        def _():
            carry_ref[...] = jnp.zeros_like(carry_ref)

        xa = xa_ref[...]
        ext_ref[0:8, :] = jnp.where(t == 0, 0.0, halo_ref[...])
        ext_ref[8:8 + tc, :] = xa
        xc = _conv_from_ext(ext_ref, xa, cw_ref, cb_ref, tc)
        _, i_t, _, _, a, mult = _lru_gates(xc, wx_ref, wa_ref, bx_ref, ba_ref, lam_ref)
        u = mult * (i_t * xc)
        _scan_tile(a, u, carry_ref[7:8, :], la_ref, lh_ref, c_ref, h_ref, False)
        h = h_ref[...]
        carry_ref[...] = h[tc - 8:tc, :]
        ga = ga_ref[...]
        ya_ref[...] = (ga * _sigmoid(ga) * h).astype(ya_ref.dtype)

    row = lambda b, t: b * nt + t
    vec = pl.BlockSpec((1, CW), lambda b, c, t: (0, c))
    mat = pl.BlockSpec((1, CW, CW), lambda b, c, t: (c, 0, 0))
    return pl.pallas_call(
        body,
        name="lru_fwd",
        grid=(B, N_CT, nt),
        in_specs=[
            pl.BlockSpec((tc, CW), lambda b, c, t: (row(b, t), c)),
            pl.BlockSpec((8, CW), lambda b, c, t: (jnp.maximum(row(b, t) * h8 - 1, 0), c)),
            pl.BlockSpec((tc, CW), lambda b, c, t: (row(b, t), N_CT + c)),
            pl.BlockSpec((CONV, CW), lambda b, c, t: (0, c)),
            vec, mat, mat, vec, vec, vec,
        ],
        out_specs=[
            pl.BlockSpec((tc, CW), lambda b, c, t: (row(b, t), c)),
            pl.BlockSpec((tc, CW), lambda b, c, t: (row(b, t), c)),
        ],
        out_shape=[
            jax.ShapeDtypeStruct((T, D_MODEL), F32),
            jax.ShapeDtypeStruct((T, D_MODEL), _MXU),
        ],
        scratch_shapes=[pltpu.VMEM((tc + 8, CW), F32), pltpu.VMEM((8, CW), F32)] + _scan_scratch(tc)[:3],
        compiler_params=_params(("parallel", "parallel", "arbitrary")),
    )(proj, proj, proj, conv_w, conv_b, wx_bd, wa_bd, bx, ba, lam)


def _lru_bwd(dya, proj, hlru, conv_w, conv_b, wx_bd, wa_bd, bx, ba, lam, B, S, deps=()):
    T = B * S
    tc = min(SCAN_TILE, S)
    nt = S // tc
    h8 = tc // 8

    def body(dya_ref, xa_ref, xhalo_ref, ga_ref, h_ref, hhalo_ref, cw_ref, cb_ref, wx_ref, wa_ref, bx_ref, ba_ref,
             lam_ref, dxa_ref, dga_ref, dcw_ref, dcb_ref, dwx_ref, dwa_ref, dbx_ref, dba_ref, dlam_ref,
             ext_ref, ext2_ref, carry_ref, dhalo_ref, la_ref, lh_ref, c_ref, dh_ref, accx_ref, acca_ref):
        b = pl.program_id(1)
        t = pl.program_id(2)
        tt = nt - 1 - t

        @pl.when(t == 0)
        def _():
            carry_ref[...] = jnp.zeros_like(carry_ref)
            dhalo_ref[...] = jnp.zeros_like(dhalo_ref)

        @pl.when((t == 0) & (b == 0))
        def _():
            for r in (dcw_ref, dcb_ref, accx_ref, acca_ref, dbx_ref, dba_ref, dlam_ref):
                r[...] = jnp.zeros_like(r)

        xa = xa_ref[...]
        ext_ref[0:8, :] = jnp.where(tt == 0, 0.0, xhalo_ref[...])
        ext_ref[8:8 + tc, :] = xa
        xc = _conv_from_ext(ext_ref, xa, cw_ref, cb_ref, tc)
        xcb, i_t, r_t, sp, a, mult = _lru_gates(xc, wx_ref, wa_ref, bx_ref, ba_ref, lam_ref)

        h = h_ref[...]
        ga = ga_ref[...]
        dya_t = dya_ref[...]
        sg = _sigmoid(ga)
        dga_ref[...] = (dya_t * h * (sg * (1.0 + ga * (1.0 - sg)))).astype(dga_ref.dtype)
        dlru = dya_t * (ga * sg)

        row = lax.broadcasted_iota(jnp.int32, a.shape, 0)
        coef = jnp.where(row == tc - 1, 1.0, pltpu.roll(a, tc - 1, 0))
        _scan_tile(coef, dlru, carry_ref[0:1, :], la_ref, lh_ref, c_ref, dh_ref, True)
        dh = dh_ref[...]
        ext2_ref[0:tc, :] = a * dh
        carry_ref[...] = ext2_ref[0:8, :]

        ext2_ref[0:8, :] = jnp.where(tt == 0, 0.0, hhalo_ref[...])
        ext2_ref[8:8 + tc, :] = h
        hprev = ext2_ref[7:7 + tc, :]

        da = dh * hprev
        ix = i_t * xc
        dmult = dh * ix
        di = dh * mult * xc
        dxc = dh * mult * i_t
        dlog_a = da * a - dmult * (a * a) / mult
        dr = dlog_a * ((-LRU_C) * sp)
        dlam_ref[...] += jnp.sum(dlog_a * r_t, axis=0, keepdims=True) * (LRU_C * _sigmoid(-lam_ref[...]))
        dza = dr * r_t * (1.0 - r_t)
        dzx = di * i_t * (1.0 - i_t)
        dzab = _c(dza)
        dzxb = _c(dzx)
        dxc = dxc + _dot_nt(dzxb, wx_ref[0]) + _dot_nt(dzab, wa_ref[0])
        accx_ref[...] += _dot_tn(xcb, dzxb)
        acca_ref[...] += _dot_tn(xcb, dzab)
        dbx_ref[...] += jnp.sum(dzx, axis=0, keepdims=True)
        dba_ref[...] += jnp.sum(dza, axis=0, keepdims=True)

        dcb_ref[...] += jnp.sum(dxc, axis=0, keepdims=True)
        dcw_ref[3:4, :] += jnp.sum(dxc * xa, axis=0, keepdims=True)
        dcw_ref[2:3, :] += jnp.sum(dxc * ext_ref[7:7 + tc, :], axis=0, keepdims=True)
        dcw_ref[1:2, :] += jnp.sum(dxc * ext_ref[6:6 + tc, :], axis=0, keepdims=True)
        dcw_ref[0:1, :] += jnp.sum(dxc * ext_ref[5:5 + tc, :], axis=0, keepdims=True)
        ext2_ref[0:tc, :] = dxc
        ext2_ref[tc:tc + 8, :] = dhalo_ref[...]
        dxa = (cw_ref[3:4, :] * dxc + cw_ref[2:3, :] * ext2_ref[1:1 + tc, :]
               + cw_ref[1:2, :] * ext2_ref[2:2 + tc, :] + cw_ref[0:1, :] * ext2_ref[3:3 + tc, :])
        dxa_ref[...] = dxa.astype(dxa_ref.dtype)
        dhalo_ref[...] = ext2_ref[0:8, :]

        @pl.when((b == B - 1) & (t == nt - 1))
        def _():
            lane_block = lax.broadcasted_iota(jnp.int32, (LRU_BW, CW), 1) // LRU_BW
            for acc_ref, out_ref in ((accx_ref, dwx_ref), (acca_ref, dwa_ref)):
                diag = jnp.zeros((LRU_BW, CW), F32)
                for j in range(CW // LRU_BW):
                    diag = jnp.where(lane_block == j, acc_ref[j * LRU_BW:(j + 1) * LRU_BW, :], diag)
                for q in range(CW // LANES):
                    out_ref[0, q] = diag[:, q * LANES:(q + 1) * LANES]

    row_of = lambda b, t: b * nt + (nt - 1 - t)
    tile = lambda off: pl.BlockSpec((tc, CW), lambda c, b, t: (row_of(b, t), off + c))
    halo = pl.BlockSpec((8, CW), lambda c, b, t: (jnp.maximum(row_of(b, t) * h8 - 1, 0), c))
    vec = pl.BlockSpec((1, CW), lambda c, b, t: (0, c))
    mat = pl.BlockSpec((1, CW, CW), lambda c, b, t: (c, 0, 0))
    cwspec = pl.BlockSpec((CONV, CW), lambda c, b, t: (0, c))
    diag = pl.BlockSpec((1, CW // LANES, LRU_BW, LANES), lambda c, b, t: (c, 0, 0, 0))
    return pl.pallas_call(
        _after(body, 13, deps),
        name="lru_bwd",
        grid=(N_CT, B, nt),
        in_specs=[tile(0), tile(0), halo, tile(N_CT), tile(0), halo, cwspec, vec, mat, mat, vec, vec, vec]
        + [ANY_SPEC] * len(deps),
        out_specs=[tile(0), tile(0), cwspec, vec, diag, diag, vec, vec, vec],
        out_shape=[
            jax.ShapeDtypeStruct((T, D_MODEL), _MXU),
            jax.ShapeDtypeStruct((T, D_MODEL), _MXU),
            jax.ShapeDtypeStruct((CONV, D_MODEL), F32),
            jax.ShapeDtypeStruct((1, D_MODEL), F32),
            jax.ShapeDtypeStruct((N_CT, CW // LANES, LRU_BW, LANES), F32),
            jax.ShapeDtypeStruct((N_CT, CW // LANES, LRU_BW, LANES), F32),
            jax.ShapeDtypeStruct((1, D_MODEL), F32),
            jax.ShapeDtypeStruct((1, D_MODEL), F32),
            jax.ShapeDtypeStruct((1, D_MODEL), F32),
        ],
        scratch_shapes=[pltpu.VMEM((tc + 8, CW), F32), pltpu.VMEM((tc + 8, CW), F32),
                        pltpu.VMEM((8, CW), F32), pltpu.VMEM((8, CW), F32)] + _scan_scratch(tc)
        + [pltpu.VMEM((CW, CW), F32), pltpu.VMEM((CW, CW), F32)],
        compiler_params=_params(("parallel", "arbitrary", "arbitrary")),
    )(dya, proj, proj, proj, hlru, hlru, conv_w, conv_b, wx_bd, wa_bd, bx, ba, lam, *deps)


def _retention_tables(S):
    half = DK // 2
    freqs = ROPE_THETA ** (-jnp.arange(half, dtype=F32) / half)
    ang = jnp.arange(S, dtype=F32)[:, None] * freqs[None, :]
    log_g = jnp.log1p(-(2.0 ** (-5.0 - jnp.arange(HEADS, dtype=F32))))
    idx = jnp.arange(CHUNK, dtype=F32)
    diff = idx[:, None] - idx[None, :]
    inner = jnp.where(diff >= 0, jnp.exp(jnp.maximum(diff, 0.0)[None] * log_g[:, None, None]), 0.0)
    cross = jnp.exp((idx[None, :] + 1.0) * log_g[:, None])[:, :, None]
    state = jnp.exp((CHUNK - 1.0 - idx[None, :]) * log_g[:, None])[:, :, None]
    gam = jnp.broadcast_to(jnp.exp(CHUNK * log_g)[:, None, None], (HEADS, 1, DK))
    return jnp.cos(ang), jnp.sin(ang), inner, cross, state, gam


def _rot(x, cos, sin):
    half = DK // 2
    x1, x2 = x[:, :half], x[:, half:]
    return jnp.concatenate([x1 * cos - x2 * sin, x1 * sin + x2 * cos], axis=-1)


def _rot_t(y, cos, sin):
    half = DK // 2
    y1, y2 = y[:, :half], y[:, half:]
    return jnp.concatenate([y1 * cos + y2 * sin, y2 * cos - y1 * sin], axis=-1)


def _groupnorm(o):
    mu = jnp.mean(o, axis=-1, keepdims=True)
    oc = o - mu
    rs = lax.rsqrt(jnp.mean(oc * oc, axis=-1, keepdims=True) + EPS)
    return oc * rs, rs


def _ret_specs(B, chunk_of):
    rows = RET_CHUNKS * CHUNK
    qkv = lambda g: pl.BlockSpec((B, rows, D_MODEL), lambda c: (0, chunk_of(c), g))
    act = pl.BlockSpec((B, rows, D_MODEL), lambda c: (0, chunk_of(c), 0))
    rope = pl.BlockSpec((rows, DK // 2), lambda c: (chunk_of(c), 0))
    dmat = pl.BlockSpec((HEADS, CHUNK, CHUNK), lambda c: (0, 0, 0))
    dvec = pl.BlockSpec((HEADS, CHUNK, 1), lambda c: (0, 0, 0))
    hrow = pl.BlockSpec((HEADS, 1, DK), lambda c: (0, 0, 0))
    rst = pl.BlockSpec((RET_CHUNKS, B, HEADS, DK, DK), lambda c: (chunk_of(c), 0, 0, 0, 0))
    return qkv, act, rope, dmat, dvec, hrow, rst


def _ret_fwd(proj, tables, gain3, B, S):
    T = B * S
    nc = S // CHUNK
    cos, sin, dmat_t, cd_t, sd_t, gam_t = tables

    def body(q_ref, k_ref, v_ref, gb_ref, cos_ref, sin_ref, dm_ref, cd_ref, sd_ref, gam_ref, gain_ref,
             o_ref, yb_ref, rs_ref, state_ref):
        @pl.when(pl.program_id(0) == 0)
        def _():
            state_ref[...] = jnp.zeros_like(state_ref)

        for cc, b, h in [(cc, b, h) for cc in range(RET_CHUNKS) for b in range(B) for h in range(HEADS)]:
            rows = slice(cc * CHUNK, (cc + 1) * CHUNK)
            cos_t, sin_t = cos_ref[rows, :], sin_ref[rows, :]
            cols = slice(h * DK, (h + 1) * DK)
            qb = _c(_rot(q_ref[b, rows, cols], cos_t, sin_t))
            kb = _c(_rot(k_ref[b, rows, cols], cos_t, sin_t) * (DK ** -0.5))
            v = v_ref[b, rows, cols]
            state = state_ref[b, h]
            sb = _c(state)
            rs_ref[cc, b, h] = sb
            scores = _dot_nt(qb, kb) * dm_ref[h]
            o = _dot(_c(scores), _c(v)) + _dot(qb, sb) * cd_ref[h]
            state_ref[b, h] = gam_ref[h] * state + _dot_tn(kb, _c(v * sd_ref[h]))
            o_ref[b, rows, cols] = o
            n, _ = _groupnorm(o)
            gb = gb_ref[b, rows, cols]
            yb_ref[b, rows, cols] = (gb * _sigmoid(gb) * (n * gain_ref[h])).astype(yb_ref.dtype)

    qkv, act, rope, dmat, dvec, hrow, rst = _ret_specs(B, lambda c: c)
    proj3 = proj.reshape(B, S, proj.shape[1])
    o_pre, yb, states = pl.pallas_call(
        body,
        name="ret_fwd",
        grid=(nc // RET_CHUNKS,),
        in_specs=[qkv(2), qkv(3), qkv(4), qkv(5), rope, rope, dmat, dvec, dvec, hrow, hrow],
        out_specs=[act, act, rst],
        out_shape=[
            jax.ShapeDtypeStruct((B, S, D_MODEL), F32),
            jax.ShapeDtypeStruct((B, S, D_MODEL), _MXU),
            jax.ShapeDtypeStruct((nc, B, HEADS, DK, DK), _MXU),
        ],
        scratch_shapes=[pltpu.VMEM((B, HEADS, DK, DK), F32)],
        compiler_params=_params(("arbitrary",)),
    )(proj3, proj3, proj3, proj3, cos, sin, dmat_t, cd_t, sd_t, gam_t, gain3)
    return o_pre.reshape(T, D_MODEL), yb.reshape(T, D_MODEL), states


def _ret_bwd(dyb, o_pre, proj, states, tables, gain3, B, S, deps=()):
    T = B * S
    nc = S // CHUNK
    cos, sin, dmat_t, cd_t, sd_t, gam_t = tables

    def body(dyb_ref, o_ref, q_ref, k_ref, v_ref, gb_ref, rs_ref, cos_ref, sin_ref, dm_ref, cd_ref, sd_ref, gam_ref,
             gain_ref, dr_ref, dgain_ref, dstate_ref):
        @pl.when(pl.program_id(0) == 0)
        def _():
            dstate_ref[...] = jnp.zeros_like(dstate_ref)
            dgain_ref[...] = jnp.zeros_like(dgain_ref)

        for cc, b, h in [(cc, b, h) for cc in reversed(range(RET_CHUNKS)) for b in range(B) for h in range(HEADS)]:
            rows = slice(cc * CHUNK, (cc + 1) * CHUNK)
            cos_t, sin_t = cos_ref[rows, :], sin_ref[rows, :]
            cols = slice(h * DK, (h + 1) * DK)
            gain = gain_ref[h]
            n, rs = _groupnorm(o_ref[b, rows, cols])
            gb = gb_ref[b, rows, cols]
            sg = _sigmoid(gb)
            dy = dyb_ref[b, rows, cols]
            part = lambda g: slice(g * D_MODEL + h * DK, g * D_MODEL + (h + 1) * DK)
            dr_ref[b, rows, part(3)] = (dy * (n * gain) * (sg * (1.0 + gb * (1.0 - sg)))).astype(dr_ref.dtype)
            dgn = dy * (gb * sg)
            dgain_ref[h] += jnp.sum(dgn * n, axis=0, keepdims=True)
            dn = dgn * gain
            do = rs * (dn - jnp.mean(dn, axis=-1, keepdims=True) - n * jnp.mean(dn * n, axis=-1, keepdims=True))

            qb = _c(_rot(q_ref[b, rows, cols], cos_t, sin_t))
            kb = _c(_rot(k_ref[b, rows, cols], cos_t, sin_t) * (DK ** -0.5))
            v = v_ref[b, rows, cols]
            vb = _c(v)
            vsb = _c(v * sd_ref[h])
            dob = _c(do)
            docb = _c(do * cd_ref[h])
            dmat = dm_ref[h]
            dstate = dstate_ref[b, h]
            dsb = _c(dstate)
            pb = _c(_dot_nt(qb, kb) * dmat)
            dsc = _c(_dot_nt(dob, vb) * dmat)
            dq = _dot(dsc, kb) + _dot_nt(docb, rs_ref[cc, b, h])
            dk = _dot_tn(dsc, qb) + _dot_nt(vsb, dsb)
            dv = _dot_tn(pb, dob) + _dot(kb, dsb) * sd_ref[h]
            dstate_ref[b, h] = gam_ref[h] * dstate + _dot_tn(qb, docb)
            dr_ref[b, rows, part(0)] = _rot_t(dq, cos_t, sin_t).astype(dr_ref.dtype)
            dr_ref[b, rows, part(1)] = (_rot_t(dk, cos_t, sin_t) * (DK ** -0.5)).astype(dr_ref.dtype)
            dr_ref[b, rows, part(2)] = dv.astype(dr_ref.dtype)

    n_steps = nc // RET_CHUNKS
    qkv, act, rope, dmat, dvec, hrow, rst = _ret_specs(B, lambda c: n_steps - 1 - c)
    wide = pl.BlockSpec((B, RET_CHUNKS * CHUNK, 4 * D_MODEL), lambda c: (0, n_steps - 1 - c, 0))
    proj3 = proj.reshape(B, S, proj.shape[1])
    dr, dgain = pl.pallas_call(
        _after(body, 14, deps),
        name="ret_bwd",
        grid=(n_steps,),
        in_specs=[act, act, qkv(2), qkv(3), qkv(4), qkv(5), rst, rope, rope, dmat, dvec, dvec, hrow, hrow]
        + [ANY_SPEC] * len(deps),
        out_specs=[wide, hrow],
        out_shape=[jax.ShapeDtypeStruct((B, S, 4 * D_MODEL), _MXU), jax.ShapeDtypeStruct((HEADS, 1, DK), F32)],
        scratch_shapes=[pltpu.VMEM((B, HEADS, DK, DK), F32)],
        compiler_params=_params(("arbitrary",)),
    )(dyb.reshape(B, S, D_MODEL), o_pre.reshape(B, S, D_MODEL), proj3, proj3, proj3, proj3, states, cos, sin, dmat_t,
      cd_t, sd_t, gam_t, gain3, *deps)
    return dr.reshape(T, 4 * D_MODEL), dgain


def _mid(ya, yb, proj, x2d, tgt2d, wpa, wpb, wout, g_fin):
    T = x2d.shape[0]
    tm = min(MID_TILE, T)
    n_steps = T // tm
    rows = D_MODEL // (2 * N_CHIPS)

    def body(ya_ref, yb_ref, ma_ref, mb_ref, x_ref, t_ref, gf_ref, wpa_hbm, wpb_hbm, wout_hbm,
             loss_ref, dx2_ref, dya_ref, dyb_ref, dm_ref, dgf_ref, gw_hbm, w_ref, acc_ref, sem):
        i = pl.program_id(0)

        @pl.when(i == 0)
        def _():
            loads = [pltpu.make_async_copy(src, w_ref.at[k], sem.at[k]) for k, src in enumerate((wpa_hbm, wpb_hbm, wout_hbm))]
            for cp in loads:
                cp.start()
            for cp in loads:
                cp.wait()
            acc_ref[...] = jnp.zeros_like(acc_ref)
            loss_ref[...] = jnp.zeros_like(loss_ref)
            dgf_ref[...] = jnp.zeros_like(dgf_ref)

        ya_t, yb_t = ya_ref[...], yb_ref[...]
        out_a = _dot(ya_t, w_ref[0])
        out_b = _dot(yb_t, w_ref[1])
        sa = _sigmoid(ma_ref[...])
        sb = _sigmoid(mb_ref[...])
        mgb = _c(sa * out_a + sb * out_b)
        x2 = x_ref[...] + _dot(mgb, w_ref[2])
        r2 = lax.rsqrt(jnp.mean(x2 * x2, axis=-1, keepdims=True) + EPS)
        nx = x2 * r2
        gf = gf_ref[...]
        err = nx * gf - t_ref[...]
        loss_ref[...] += 0.5 * jnp.sum(jnp.mean(err * err, axis=-1, keepdims=True), axis=0, keepdims=True)
        dy = err * (1.0 / D_MODEL)
        dgf_ref[...] += jnp.sum(dy * nx, axis=0, keepdims=True)
        dyg = dy * gf
        dx2 = r2 * (dyg - nx * jnp.mean(dyg * nx, axis=-1, keepdims=True))
        dx2_ref[...] = dx2
        dx2b = _c(dx2)
        dmg = _dot_nt(dx2b, w_ref[2])
        acc_ref[2] += _dot_tn(mgb, dx2b)
        dm_ref[:, :D_MODEL] = (dmg * out_a * sa * (1.0 - sa)).astype(dm_ref.dtype)
        dm_ref[:, D_MODEL:] = (dmg * out_b * sb * (1.0 - sb)).astype(dm_ref.dtype)
        dab = _c(dmg * sa)
        dbb = _c(dmg * sb)
        dya_ref[...] = _dot_nt(dab, w_ref[0])
        dyb_ref[...] = _dot_nt(dbb, w_ref[1])
        acc_ref[0] += _dot_tn(ya_t, dab)
        acc_ref[1] += _dot_tn(yb_t, dbb)

        @pl.when(i == n_steps - 1)
        def _():
            copies = [pltpu.make_async_copy(acc_ref.at[k, pl.ds((2 * p + hf) * rows, rows), :], gw_hbm.at[p, hf, k],
                                            sem.at[(k * N_CHIPS + p) * 2 + hf])
                      for k in range(3) for p in range(N_CHIPS) for hf in range(2)]
            for cp in copies:
                cp.start()
            for cp in copies:
                cp.wait()

    tile = lambda j: pl.BlockSpec((tm, D_MODEL), lambda i: (i, j))
    one = pl.BlockSpec((1, D_MODEL), lambda i: (0, 0))
    anyspec = pl.BlockSpec(memory_space=pl.ANY)
    return pl.pallas_call(
        body,
        name="mid",
        grid=(n_steps,),
        in_specs=[tile(0), tile(0), tile(6), tile(7), tile(0), tile(0), one, anyspec, anyspec, anyspec],
        out_specs=[pl.BlockSpec((1, 1), lambda i: (0, 0)), tile(0), tile(0), tile(0),
                   pl.BlockSpec((tm, 2 * D_MODEL), lambda i: (i, 0)), one, anyspec],
        out_shape=[
            jax.ShapeDtypeStruct((1, 1), F32),
            jax.ShapeDtypeStruct((T, D_MODEL), F32),
            jax.ShapeDtypeStruct((T, D_MODEL), F32),
            jax.ShapeDtypeStruct((T, D_MODEL), F32),
            jax.ShapeDtypeStruct((T, 2 * D_MODEL), _MXU),
            jax.ShapeDtypeStruct((1, D_MODEL), F32),
            jax.ShapeDtypeStruct((N_CHIPS, 2, 3, rows, D_MODEL), F32),
        ],
        scratch_shapes=[pltpu.VMEM((3, D_MODEL, D_MODEL), _MXU), pltpu.VMEM((3, D_MODEL, D_MODEL), F32),
                        pltpu.SemaphoreType.DMA((3 * N_CHIPS * 2,))],
        compiler_params=_params(("arbitrary",)),
    )(ya, yb, proj, proj, x2d, tgt2d, g_fin, wpa, wpb, wout)


def _inproj_bwd_dx(dparts, w_all, x2d, dx2, g_in, first, count, prev, name, deps=()):
    T = x2d.shape[0]
    tm = min(DX_TILE, T)
    n_d = len(dparts)
    groups = [(a, k) for a, d in enumerate(dparts) for k in range(d.shape[1] // D_MODEL)]
    dg_start = jnp.zeros((1, D_MODEL), F32) if prev is None else prev[1]
    carried = () if prev is None else (prev[0],)

    def body(*refs):
        d_refs = refs[:n_d]
        x_ref, dx2_ref, g_ref, dg0_ref, w_hbm = refs[n_d:n_d + 5]
        dx_ref, dg_ref, w_ref, sem = refs[-4:]

        def load(j):
            part = (j // 2, slice(None), pl.ds((j % 2) * D_MODEL, D_MODEL))
            return pltpu.make_async_copy(w_hbm.at[part], w_ref.at[part], sem.at[j])

        def tile(before_group):
            dh = jnp.zeros((tm, D_MODEL), F32)
            for j, (a, k) in enumerate(groups):
                before_group(j)
                dh = dh + _dot_nt(d_refs[a][:, k * D_MODEL:(k + 1) * D_MODEL],
                                  w_ref[j // 2, :, (j % 2) * D_MODEL:(j % 2 + 1) * D_MODEL])
            x = x_ref[...]
            r = lax.rsqrt(jnp.mean(x * x, axis=-1, keepdims=True) + EPS)
            nx = x * r
            dg_ref[...] += jnp.sum(dh * nx, axis=0, keepdims=True)
            dhg = dh * g_ref[...]
            dx_ref[...] = dx2_ref[...] + r * (dhg - nx * jnp.mean(dhg * nx, axis=-1, keepdims=True))

        first = pl.program_id(0) == 0

        @pl.when(first)
        def _():
            for j in range(len(groups)):
                load(j).start()
            dg_ref[...] = dg0_ref[...]
            tile(lambda j: load(j).wait())

        @pl.when(jnp.logical_not(first))
        def _():
            tile(lambda j: None)

    tile = pl.BlockSpec((tm, D_MODEL), lambda i: (first + i, 0))
    one = pl.BlockSpec((1, D_MODEL), lambda i: (0, 0))
    return pl.pallas_call(
        body,
        name=name,
        grid=(count,),
        in_specs=[pl.BlockSpec((tm, d.shape[1]), lambda i: (first + i, 0)) for d in dparts]
        + [tile, tile, one, one, ANY_SPEC] + [ANY_SPEC] * (len(carried) + len(deps)),
        out_specs=[tile, one],
        out_shape=[jax.ShapeDtypeStruct((T, D_MODEL), F32), jax.ShapeDtypeStruct((1, D_MODEL), F32)],
        input_output_aliases={n_d + 5: 0} if carried else {},
        scratch_shapes=[pltpu.VMEM(w_all.shape, w_all.dtype), pltpu.SemaphoreType.DMA((len(groups),))],
        compiler_params=_params(("arbitrary",)),
    )(*dparts, x2d, dx2, g_in, dg_start, w_all, *carried, *deps)


def _inproj_bwd_dw(ht, dparts, name, deps=()):
    T = ht.shape[1]
    tn = DW_COLS
    half = D_MODEL // 2
    per_chip = 2 * D_MODEL // tn
    n_d = len(dparts)
    tiles = [(a, t) for a, d in enumerate(dparts) for t in range(d.shape[1] // tn)]
    offs = [sum(d.shape[1] // tn for d in dparts[:a]) for a in range(n_d)]

    def body(*refs):
        ht_hbm = refs[0]
        d_refs = refs[1:1 + n_d]
        out_ref, ht_ref, sem = refs[-3:]
        t = pl.program_id(0)

        def load(k):
            cols = pl.ds(k * (T // DW_LOADS), T // DW_LOADS)
            return pltpu.make_async_copy(ht_hbm.at[:, cols], ht_ref.at[:, cols], sem.at[k])

        def store(g):
            out_ref[0, 0] = g[:half]
            out_ref[0, 1] = g[half:]

        @pl.when(t == 0)
        def _():
            for k in range(DW_LOADS):
                load(k).start()
            g = jnp.zeros((D_MODEL, tn), F32)
            for k in range(DW_LOADS):
                load(k).wait()
                tokens = slice(k * (T // DW_LOADS), (k + 1) * (T // DW_LOADS))
                g = g + _dot(ht_ref[:, tokens], d_refs[0][tokens, :])
            store(g)

        for a in range(n_d):
            lo, hi = max(offs[a], 1), offs[a] + dparts[a].shape[1] // tn

            @pl.when((t >= lo) & (t < hi))
            def _(a=a):
                store(_dot(ht_ref[...], d_refs[a][...]))

    def dspec(a):
        n_a = dparts[a].shape[1] // tn
        return pl.BlockSpec((T, tn), lambda t: (0, jnp.clip(t - offs[a], 0, n_a - 1)))

    return pl.pallas_call(
        body,
        name=name,
        grid=(len(tiles),),
        in_specs=[ANY_SPEC] + [dspec(a) for a in range(n_d)] + [ANY_SPEC] * len(deps),
        out_specs=pl.BlockSpec((1, 2, half, tn), lambda t: (t // per_chip, 0, 0, t % per_chip)),
        out_shape=jax.ShapeDtypeStruct((len(tiles) // per_chip, 2, half, 2 * D_MODEL), F32),
        scratch_shapes=[pltpu.VMEM(ht.shape, ht.dtype), pltpu.SemaphoreType.DMA((DW_LOADS,))],
        compiler_params=_params(("arbitrary",)),
    )(ht, *dparts, *deps)


def _coords():
    return lax.axis_index("x"), lax.axis_index("y"), lax.axis_index("c")


def _other_chips(x, y):
    return [(1 - x, y), (x, 1 - y), (1 - x, 1 - y)]


def _chunks(rows, n):
    size = rows // n
    return [pl.ds(q * size, size) for q in range(n)]


HBM_SPEC = pl.BlockSpec(memory_space=pltpu.HBM)
SEM_SPEC = pl.BlockSpec(memory_space=pltpu.SEMAPHORE)
DATAFLOW = pltpu.SideEffectType.DATAFLOW_SIDE_EFFECTING


def _copies_start(bufs, plan, n_copies, name, deps=()):
    n = len(bufs)
    n_deps = len(deps)

    def body(*refs):
        ins = refs[:n]
        send_sems, recv_sems = refs[n + n_deps], refs[n + n_deps + 1]
        token = refs[-1]
        for k, send, _ in plan(ins):
            if send is not None:
                src, dst, dev, pred = send
                cp = pltpu.make_async_remote_copy(src_ref=src, dst_ref=dst, send_sem=send_sems.at[k],
                                                  recv_sem=recv_sems.at[k], device_id=dev, device_id_type=MESH)
                if pred is None:
                    cp.start()
                else:
                    pl.when(pred)(cp.start)
        token[...] = jnp.zeros_like(token)

    hbm = [pltpu.with_memory_space_constraint(b, pltpu.HBM) for b in bufs]
    outs = pl.pallas_call(
        body,
        name=name,
        in_specs=[HBM_SPEC] * n + [ANY_SPEC] * n_deps,
        out_specs=(SEM_SPEC, SEM_SPEC, *([HBM_SPEC] * n), pl.BlockSpec(memory_space=pltpu.VMEM)),
        out_shape=(pltpu.SemaphoreType.DMA((n_copies,)), pltpu.SemaphoreType.DMA((n_copies,)),
                   *[pltpu.HBM(b.shape, b.dtype) for b in bufs], jax.ShapeDtypeStruct((8, 128), F32)),
        input_output_aliases={a: 2 + a for a in range(n)},
        compiler_params=pltpu.CompilerParams(has_side_effects=DATAFLOW),
    )(*hbm, *deps)
    return outs[0], outs[1], list(outs[2:2 + n]), outs[-1]


def _copies_wait(send_sems, recv_sems, bufs, after, plan, name, only=None):
    n = len(bufs)

    def body(*refs):
        ins = refs[:n]
        s_sems, r_sems = refs[n], refs[n + 1]
        for k, send, recv in plan(ins):
            if only is not None and k not in only:
                continue
            if send is not None:
                src, dst, dev, pred = send
                cp = pltpu.make_async_remote_copy(src_ref=src, dst_ref=dst, send_sem=s_sems.at[k],
                                                  recv_sem=r_sems.at[k], device_id=dev, device_id_type=MESH)
                if pred is None:
                    cp.wait_send()
                else:
                    pl.when(pred)(cp.wait_send)
            if recv is not None:
                dst, pred = recv
                cp = pltpu.make_async_remote_copy(src_ref=dst, dst_ref=dst, send_sem=s_sems.at[k],
                                                  recv_sem=r_sems.at[k], device_id=_coords(), device_id_type=MESH)
                if pred is None:
                    cp.wait_recv()
                else:
                    pl.when(pred)(cp.wait_recv)

    outs = pl.pallas_call(
        body,
        name=name,
        in_specs=[HBM_SPEC] * n + [SEM_SPEC, SEM_SPEC, pl.BlockSpec(memory_space=pl.ANY)],
        out_specs=[HBM_SPEC] * n,
        out_shape=[pltpu.HBM(b.shape, b.dtype) for b in bufs],
        input_output_aliases={a: a for a in range(n)},
        compiler_params=pltpu.CompilerParams(has_side_effects=DATAFLOW),
    )(*bufs, send_sems, recv_sems, after)
    return list(outs)


def _gather_plan(n_bufs):
    def plan(refs):
        x, y, c = _coords()
        me = 2 * x + y
        out = []
        for k, (px, py) in enumerate(_other_chips(x, y)):
            for a in range(n_bufs):
                out.append((k * n_bufs + a, (refs[a].at[me], refs[a].at[me], (px, py, c), None),
                            (refs[a].at[2 * px + py], None)))
        return out
    return plan


def _cast_into_slot(ws, name, deps=()):
    n = len(ws)
    nt = 2

    def body(s_ref, *refs):
        outs = refs[len(refs) - n:]
        for a in range(n):
            outs[a][0] = refs[a][...].astype(outs[a].dtype)

    xi, yi, _ = _coords()
    return pl.pallas_call(
        body,
        name=name,
        grid_spec=pltpu.PrefetchScalarGridSpec(
            num_scalar_prefetch=1,
            grid=(2, nt),
            in_specs=[pl.BlockSpec((1, w.shape[1] // nt, w.shape[2]), lambda hf, i, s: (hf, i, 0)) for w in ws]
            + [ANY_SPEC] * len(deps),
            out_specs=[pl.BlockSpec((1, 1, w.shape[1] // nt, w.shape[2]), lambda hf, i, s: (s[0], hf, i, 0)) for w in ws],
        ),
        out_shape=[jax.ShapeDtypeStruct((N_CHIPS,) + w.shape, _MXU) for w in ws],
        compiler_params=_params(("parallel", "parallel")),
    )((2 * xi + yi).reshape(1).astype(jnp.int32), *ws, *deps)


def _chip_gather_plan(stage, n_bufs):
    def plan(refs):
        x, y, c = _coords()
        me = 2 * x + y
        near = [(1 - x, y), (x, 1 - y)]
        slots = [2 * (1 - x) + y, 2 * x + (1 - y), 2 * (1 - x) + (1 - y)]
        sibling = (x, y, 1 - c)
        pass_to = (jnp.where(c == 0, x, 1 - x), jnp.where(c == 0, 1 - y, y), c)
        pass_slot = jnp.where(c == 0, slots[0], slots[1])
        out = []

        def move(src_slot, to, land_slot, land_core, pieces):
            for a, buf in enumerate(refs):
                for rows in _chunks(buf.shape[2], pieces[a]):
                    out.append((len(out), (buf.at[src_slot, c, rows], buf.at[src_slot, c, rows], to, None),
                                (buf.at[land_slot, land_core, rows], None)))

        if stage == "near":
            for k, chip in enumerate(near):
                move(me, (*chip, c), slots[k], c, NEAR_PIECES[:n_bufs])
        elif stage == "pass":
            move(pass_slot, pass_to, slots[2], c, PASS_PIECES[:n_bufs])
            for k in range(2):
                move(slots[k], sibling, slots[k], 1 - c, [1] * n_bufs)
        else:
            move(slots[2], sibling, slots[2], 1 - c, [1] * n_bufs)
        return out
    return plan


NEAR_PIECES = (2, 1)
PASS_PIECES = (2, 1)


def _chip_gather_copies(stage, n_bufs):
    if stage == "near":
        return 2 * sum(NEAR_PIECES[:n_bufs]), None
    if stage == "pass":
        n_pass = sum(PASS_PIECES[:n_bufs])
        return n_pass + 2 * n_bufs, set(range(n_pass))
    return n_bufs, None


def _swap_plan(n_slabs):
    def plan(refs):
        x, y, c = _coords()
        out, k = [], 0
        for i, n in enumerate(n_slabs):
            g, land = refs[2 * i], refs[2 * i + 1]
            for p in range(n):
                out.append((k, (g.at[p, 1 - c], land.at[p], (x, y, 1 - c), None), (land.at[p], None)))
                k += 1
        return out
    return plan


def _is_one_of(chip, dests):
    hit = chip == dests[0]
    for d in dests[1:]:
        hit = hit | (chip == d)
    return hit


def _slab_of(chip, dests):
    return sum(j * (chip == d).astype(jnp.int32) for j, d in enumerate(dests))


def _scatter_plan(dest_sets):
    def plan(refs):
        x, y, c = _coords()
        me = 2 * x + y
        out = []
        for k, (px, py) in enumerate(_other_chips(x, y)):
            peer = 2 * px + py
            for i, dests in enumerate(dest_sets):
                cs, land = refs[2 * i], refs[2 * i + 1]
                everyone = len(dests) == N_CHIPS
                send = (cs.at[_slab_of(peer, dests)], land.at[k], (px, py, c),
                        None if everyone else _is_one_of(peer, dests))
                recv = (land.at[k], None if everyone else _is_one_of(me, dests))
                out.append((k * len(dest_sets) + i, send, recv))
        return out
    return plan


def _join_plan(rows, n_pieces):
    def plan(refs):
        x, y, c = _coords()
        (buf,) = refs
        return [(i, (buf.at[c, piece], buf.at[c, piece], (x, y, 1 - c), None), (buf.at[1 - c, piece], None))
                for i, piece in enumerate(_chunks(rows, n_pieces))]
    return plan


def _join_plans(parts):
    def plan(refs):
        out, b0, k0 = [], 0, 0
        for part_plan, n_bufs, n_copies in parts:
            out += [(k0 + k, send, recv) for k, send, recv in part_plan(refs[b0:b0 + n_bufs])]
            b0 += n_bufs
            k0 += n_copies
        return out
    return plan


def _allgather_plan():
    def plan(refs):
        x, y, c = _coords()
        (land,) = refs
        me = 4 * x + 2 * y + c
        out = []
        for r in range(1, 8):
            px = 1 - x if r & 4 else x
            py = 1 - y if r & 2 else y
            pc = 1 - c if r & 1 else c
            out.append((r - 1, (land.at[me], land.at[me], (px, py, pc), None), (land.at[4 * px + 2 * py + pc], None)))
        return out
    return plan


def _sum_gathered(land, shapes, name):
    m = land.shape[1]

    def body(land_ref, *refs):
        outs, acc_ref = refs[:-1], refs[-1]
        acc = land_ref[0]
        for d in range(1, 8):
            acc = acc + land_ref[d]
        acc_ref[...] = acc
        r = 0
        for o_ref, (n, w) in zip(outs, shapes):
            if w == LANES:
                o_ref[...] = acc_ref[r:r + n, :]
                r += n
            elif w < LANES:
                o_ref[...] = acc_ref[r:r + n, 0:w]
                r += SUBLANES
            else:
                for k in range(n):
                    for q in range(w // LANES):
                        o_ref[k:k + 1, q * LANES:(q + 1) * LANES] = acc_ref[r:r + 1, :]
                        r += 1
        assert r == m, (r, m)

    return pl.pallas_call(
        body,
        name=name,
        out_shape=[jax.ShapeDtypeStruct(s, F32) for s in shapes],
        scratch_shapes=[pltpu.VMEM((m, LANES), F32)],
        compiler_params=_params(),
    )(land)


def _row_tile(rows, cap):
    t = cap
    while rows % t:
        t //= 2
    return t


def _add_my_half(g, r, name):
    n_slabs, _, R, C = g.shape
    tr = R if n_slabs > 1 else _row_tile(R, SUM_ROWS)

    def body(c_ref, g_ref, r_ref, o_ref):
        o_ref[...] = (g_ref[0] + r_ref[...]).astype(o_ref.dtype)

    return pl.pallas_call(
        body,
        name=name,
        grid_spec=pltpu.PrefetchScalarGridSpec(
            num_scalar_prefetch=1,
            grid=(n_slabs, R // tr),
            in_specs=[pl.BlockSpec((1, 1, tr, C), lambda p, i, c_ref: (p, c_ref[0], i, 0)),
                      pl.BlockSpec((1, tr, C), lambda p, i, c_ref: (p, i, 0))],
            out_specs=pl.BlockSpec((1, tr, C), lambda p, i, c_ref: (p, i, 0)),
        ),
        out_shape=jax.ShapeDtypeStruct(r.shape, jnp.bfloat16),
        compiler_params=_params(("parallel", "parallel")),
    )(lax.axis_index("c").reshape(1).astype(jnp.int32), g, r)


def _sum_slabs(own, got, name, deps=()):
    _, R, C = own.shape
    tr = _row_tile(R, SUM_ROWS)

    def body(s_ref, own_ref, got_ref, *rest):
        rest[-1][0] = ((own_ref[0].astype(F32) + got_ref[0].astype(F32)) + got_ref[1].astype(F32)) + got_ref[2].astype(F32)

    xi, yi, ci = _coords()
    return pl.pallas_call(
        body,
        name=name,
        grid_spec=pltpu.PrefetchScalarGridSpec(
            num_scalar_prefetch=1,
            grid=(R // tr,),
            in_specs=[pl.BlockSpec((1, tr, C), lambda i, s: (s[0], i, 0)),
                      pl.BlockSpec((3, tr, C), lambda i, s: (0, i, 0))] + [ANY_SPEC] * len(deps),
            out_specs=pl.BlockSpec((1, tr, C), lambda i, s: (s[1], i, 0)),
        ),
        out_shape=jax.ShapeDtypeStruct((2, R, C), F32),
        compiler_params=_params(("parallel",)),
    )(jnp.stack([2 * xi + yi, ci]).astype(jnp.int32), own, got, *deps)


def _adamw_math(w, g, m, v):
    m = ADAM_B1 * m + (1.0 - ADAM_B1) * g
    v = ADAM_B2 * v + (1.0 - ADAM_B2) * (g * g)
    m_hat = m / (1.0 - ADAM_B1 ** ADAM_STEP)
    v_hat = v / (1.0 - ADAM_B2 ** ADAM_STEP)
    delta = -ADAM_LR * (m_hat / (jnp.sqrt(v_hat) + ADAM_EPS) + ADAM_WD * w)
    return delta, m, v


def _sum_adamw_own(owns, got, dest_sets, w, m, v, name):
    n = len(owns)
    _, R, C = owns[0].shape
    tr = _row_tile(R, ADAMW_ROWS)
    steps = R // tr

    def body(s_ref, *refs):
        got_ref, w_ref, m_ref, v_ref = refs[n:n + 4]
        half_ref, g_ref, d_ref, mn_ref, vn_ref = refs[len(refs) - 5:]
        total = jnp.zeros((tr, C), F32)
        for i in range(n):
            total = total + jnp.where(s_ref[2 + 2 * i] == 1, refs[i][0].astype(F32), 0.0)
        grad = ((total + got_ref[0].astype(F32)) + got_ref[1].astype(F32)) + got_ref[2].astype(F32)
        d, mn, vn = _adamw_math(w_ref[...], grad, m_ref[...], v_ref[...])
        half_ref[0] = grad
        for o, val in zip((g_ref, d_ref, mn_ref, vn_ref), (grad, d, mn, vn)):
            o[...] = val

    xi, yi, ci = _coords()
    me = 2 * xi + yi
    scalars = [ci, ci]
    for dests in dest_sets:
        scalars += [_is_one_of(me, dests).astype(jnp.int32), _slab_of(me, dests)]
    own_spec = lambda i: pl.BlockSpec((1, tr, C), lambda r, s: (s[3 + 2 * i], r, 0))
    rows = pl.BlockSpec((tr, C), lambda r, s: (s[0] * steps + r, 0))
    outs = pl.pallas_call(
        body,
        name=name,
        grid_spec=pltpu.PrefetchScalarGridSpec(
            num_scalar_prefetch=1,
            grid=(steps,),
            in_specs=[own_spec(i) for i in range(n)] + [pl.BlockSpec((3, tr, C), lambda r, s: (0, r, 0))] + [rows] * 3,
            out_specs=[pl.BlockSpec((1, tr, C), lambda r, s: (s[0], r, 0))] + [rows] * 4,
        ),
        out_shape=[jax.ShapeDtypeStruct((2, R, C), F32)] + [jax.ShapeDtypeStruct((2 * R, C), F32)] * 4,
        compiler_params=_params(("parallel",)),
    )(jnp.stack(scalars).astype(jnp.int32), *owns, got, w, m, v)
    return outs[0], list(outs[1:])


def _adamw_halves(ws, g, ms, vs, half, prev, name, deps=()):
    n = len(ws)
    _, _, R, C = g.shape
    tr = _row_tile(R, ADAMW_ROWS)
    steps = R // tr
    carried = [] if prev is None else [a for four in prev for a in four]
    both = half is None
    which = (lambda i, s: i // steps) if both else (lambda i, s: s[0])
    half = 0 if both else half

    def body(s_ref, *refs):
        w_refs, g_refs, m_refs, v_refs = (refs[k * n:(k + 1) * n] for k in range(4))
        outs = refs[len(refs) - 4 * n:]
        for a in range(n):
            grad = g_refs[a][0, 0]
            d, mn, vn = _adamw_math(w_refs[a][...], grad, m_refs[a][...], v_refs[a][...])
            for o, val in zip(outs[4 * a:4 * a + 4], (grad, d, mn, vn)):
                o[...] = val

    rows = pl.BlockSpec((tr, C), lambda i, s: (which(i, s) * steps + i % steps, 0))
    grad_spec = lambda a: pl.BlockSpec((1, 1, tr, C), lambda i, s: (which(i, s), a, i % steps, 0))
    n_in = 4 * n
    outs = pl.pallas_call(
        body,
        name=name,
        grid_spec=pltpu.PrefetchScalarGridSpec(
            num_scalar_prefetch=1,
            grid=(2 * steps if both else steps,),
            in_specs=[rows] * n + [grad_spec(a) for a in range(n)] + [rows] * (2 * n)
            + [ANY_SPEC] * (len(carried) + len(deps)),
            out_specs=[rows] * (4 * n),
        ),
        out_shape=[jax.ShapeDtypeStruct((2 * R, C), F32)] * (4 * n),
        input_output_aliases={1 + n_in + k: k for k in range(len(carried))},
        compiler_params=_params(("parallel",)),
    )(jnp.reshape(half, (1,)).astype(jnp.int32), *ws, *([g] * n), *ms, *vs, *carried, *deps)
    return [outs[4 * a:4 * a + 4] for a in range(n)]


def _adamw_small(ws, gs, ms, vs, name):
    n = len(ws)

    def body(*refs):
        for a in range(n):
            d, mn, vn = _adamw_math(refs[a][...], refs[n + a][...], refs[2 * n + a][...], refs[3 * n + a][...])
            refs[4 * n + a][...] = d
            refs[5 * n + a][...] = mn
            refs[6 * n + a][...] = vn

    shapes = [jax.ShapeDtypeStruct(w.shape, F32) for w in ws]
    outs = pl.pallas_call(
        body,
        name=name,
        out_shape=shapes * 3,
        compiler_params=_params(),
    )(*ws, *gs, *ms, *vs)
    return outs[:n], outs[n:2 * n], outs[2 * n:]


def _to_blockdiag(w):
    per = CW // LRU_BW
    w4 = w.reshape(N_CT, per, LRU_BW, LRU_BW)
    eye = jnp.eye(per, dtype=w.dtype)
    return (w4[:, :, :, None, :] * eye[None, :, None, :, None]).reshape(N_CT, CW, CW)


def _blocks_from_lanes(g):
    side = LANES // LRU_BW
    g5 = g.reshape(N_CT, CW // LANES, LRU_BW, side, LRU_BW)
    return jnp.transpose(g5, (0, 1, 3, 2, 4)).reshape(LRU_BLOCKS, LRU_BW, LRU_BW)


def _local_grads(x2d, tgt2d, B, S, g_in, in_proj, conv_b, gate_x_w, gate_x_b, gate_a_w, gate_a_b, lam,
                 proj_weights, g_fin, reduce):
    wx_bd = _c(_to_blockdiag(gate_x_w))
    wa_bd = _c(_to_blockdiag(gate_a_w))
    tables = _retention_tables(S)

    proj, ht, w_all, conv_w, gain = in_proj(x2d, g_in, (*tables, wx_bd, wa_bd))
    gain3 = gain.reshape(HEADS, 1, DK)
    hlru, ya = _lru_fwd(proj, conv_w, conv_b, wx_bd, wa_bd, gate_x_b, gate_a_b, lam, B, S)
    o_pre, yb, states = _ret_fwd(proj, tables, gain3, B, S)
    wpa, wpb, wout = proj_weights(yb)
    loss, dx2, dya, dyb, dm, dgf, gw_proj = _mid(ya, yb, proj, x2d, tgt2d, wpa, wpb, wout, g_fin)
    g3 = _inproj_bwd_dw(ht, [dm], "inproj_bwd_dw_m")
    deps = reduce.m_ready(gw_proj, g3)
    dr, dgain = _ret_bwd(dyb, o_pre, proj, states, tables, gain3, B, S, deps)
    deps = reduce.ret_done(dr)
    g12 = _inproj_bwd_dw(ht, [dr], "inproj_bwd_dw_r", deps)
    deps = reduce.r_ready(g12)
    dxa, dga, dcw, dcb, dwx, dwa, dbx, dba, dlam = _lru_bwd(
        dya, proj, hlru, conv_w, conv_b, wx_bd, wa_bd, gate_x_b, gate_a_b, lam, B, S, deps)
    small = dict(conv_w=dcw, conv_b=dcb, gate_x_w=dwx, gate_x_b=dbx, gate_a_w=dwa, gate_a_b=dba, lru_lambda=dlam,
                 gn_gain=dgain.reshape(HEADS, DK), norm_final=dgf)
    deps = reduce.lru_done(dxa, _pack_small(small, loss, reduce.slot()))
    g0 = _inproj_bwd_dw(ht, [dxa, dga], "inproj_bwd_dw_a", deps)
    deps = reduce.a_ready(g0)
    n_tiles = x2d.shape[0] // min(DX_TILE, x2d.shape[0])
    grad_x, dgin = _inproj_bwd_dx([dxa, dga, dr, dm], w_all, x2d, dx2, g_in, 0, n_tiles, None, "inproj_bwd_dx", deps)
    return grad_x, dgin


ALL_CHIPS = (0, 1, 2, 3)


class _GradReduce:
    def __init__(self, proj_done):
        self.pending = {}
        self.proj_done = proj_done
        self.land_in = None

    def _start(self, key, parts, name):
        bufs, plans, shared = [], [], None
        for part_bufs, plan, n_copies, part_shared in parts:
            if part_shared is not None:
                shared = len(bufs) + part_shared
            plans.append((plan, len(part_bufs), n_copies))
            bufs += part_bufs
        plan = _join_plans(plans)
        send_sems, recv_sems, bufs, token = _copies_start(bufs, plan, sum(p[2] for p in plans), name + "_start")
        if shared is not None:
            self.land_in = bufs[shared]
        self.pending[key] = (send_sems, recv_sems, bufs, plan, name + "_wait", shared)
        return (token,)

    def _finish(self, key, after):
        send_sems, recv_sems, bufs, plan, name, shared = self.pending.pop(key)
        if shared is not None:
            bufs[shared] = self.land_in
        bufs = _copies_wait(send_sems, recv_sems, bufs, after, plan, name)
        if shared is not None:
            self.land_in = bufs[shared]
        return bufs

    @staticmethod
    def _swap(pieces):
        bufs = []
        for g in pieces:
            bufs += [g, lax.empty((g.shape[0],) + g.shape[2:], F32)]
        n_slabs = [g.shape[0] for g in pieces]
        return bufs, _swap_plan(n_slabs), sum(n_slabs), None

    def _scatter(self, sums, dest_sets):
        bufs = []
        for cs in sums:
            bufs += [cs, lax.empty((3,) + cs.shape[1:], cs.dtype)]
        if self.land_in is not None:
            bufs[-1] = self.land_in
        return bufs, _scatter_plan(dest_sets), 3 * len(sums), len(bufs) - 1

    @staticmethod
    def slot():
        x, y, c = _coords()
        return 4 * x + 2 * y + c

    def _gather8(self, block):
        land = lax.dynamic_update_slice(lax.empty((8,) + block.shape, F32), block[None], (self.slot(), 0, 0))
        return [land], _allgather_plan(), 7, None

    def m_ready(self, gw_proj, g3):
        rows = gw_proj.shape[2] * gw_proj.shape[3]
        return self._start("m", [self._swap([gw_proj.reshape(N_CHIPS, 2, rows, D_MODEL), g3])], "swap_m")

    def ret_done(self, after):
        proj, land_p, g3, land_3 = self._finish("m", after)
        sums_m = [_add_my_half(proj, land_p, "chip_sum_proj"), _add_my_half(g3, land_3, "chip_sum_m")]
        return self._start("sm", [self._scatter(sums_m, [ALL_CHIPS, (3,)])], "scatter_m")

    def r_ready(self, g12):
        return self._start("r", [self._swap([g12])], "swap_r")

    def lru_done(self, after, packed):
        g12, land_12 = self._finish("r", after)
        sums_r = [_add_my_half(g12, land_12, "chip_sum_r")]
        return (self._start("sr", [self._scatter(sums_r, [(1, 2)])], "scatter_r")
                + self._start("small", [([packed], _allgather_plan(), 7, None)], "gather_small"))

    def a_ready(self, g0):
        (token,) = self._start("a", [self._swap([g0])], "swap_a")
        csp, gotp, self.cs3, _ = self._finish("sm", token)
        half_proj = _sum_slabs(csp, gotp, "sum_w_proj")
        g0, land_0 = self._finish("a", half_proj)
        join = ([half_proj], _join_plan(half_proj.shape[1], PROJ_JOIN_PIECES), PROJ_JOIN_PIECES, None)
        return self._start("sa", [self._scatter([_add_my_half(g0, land_0, "chip_sum_a")], [(0,)]), join], "scatter_a")

    def finish(self, dgin, w_in_own, w_in_done):
        (token,) = self._start("n", [self._gather8(dgin)], "gather_norm_in")
        (small,) = self._finish("small", token)
        cs12, _ = self._finish("sr", token)
        cs0, _, g_proj = self._finish("sa", token)
        half_in, first = w_in_own([self.cs3, cs12, cs0], self.land_in, [(3,), (1, 2), (0,)])
        deps = self._start("j", [([half_in], _join_plan(half_in.shape[1], JOIN_PIECES), JOIN_PIECES, None)], "join_w_in")
        (g_in,) = self._finish("j", self.proj_done(g_proj, deps))
        done = w_in_done(g_in, first)
        (norm_in,) = self._finish("n", done[1])
        return _unpack_small(small), _sum_gathered(norm_in, [(1, D_MODEL)], "sum_norm_in_grad")[0]


_SMALL = ("gate_x_w", "gate_a_w", "conv_w", "conv_b", "gate_x_b", "gate_a_b", "lru_lambda", "gn_gain", "norm_final")
_SMALL_SHAPES = dict(gate_x_w=(LRU_BLOCKS, LRU_BW, LRU_BW), gate_a_w=(LRU_BLOCKS, LRU_BW, LRU_BW),
                     norm_in=(1, D_MODEL), conv_w=(CONV, D_MODEL), conv_b=(1, D_MODEL), gate_x_b=(1, D_MODEL),
                     gate_a_b=(1, D_MODEL), lru_lambda=(1, D_MODEL), gn_gain=(HEADS, DK), norm_final=(1, D_MODEL))


def _pack_small(small, loss, slot):
    parts = [small[k] if small[k].ndim == 2 else small[k].reshape(-1, LANES) for k in _SMALL]
    m = sum(p.size for p in parts) // LANES + SUBLANES

    def body(s_ref, *refs):
        o_ref = refs[-1]
        r = 0
        for ref, part in zip(refs, parts):
            if part.shape[1] == LANES:
                o_ref[0, r:r + part.shape[0], :] = ref[...]
                r += part.shape[0]
                continue
            for k in range(part.shape[0]):
                for q in range(part.shape[1] // LANES):
                    o_ref[0, r:r + 1, :] = ref[k:k + 1, q * LANES:(q + 1) * LANES]
                    r += 1
        o_ref[0, r:r + SUBLANES, :] = jnp.broadcast_to(refs[len(parts)][...], (SUBLANES, LANES))

    return pl.pallas_call(
        body,
        name="pack_small_grads",
        grid_spec=pltpu.PrefetchScalarGridSpec(
            num_scalar_prefetch=1,
            grid=(1,),
            in_specs=[pl.BlockSpec(p.shape, lambda i, s: (0, 0)) for p in parts] + [pl.BlockSpec((1, 1), lambda i, s: (0, 0))],
            out_specs=pl.BlockSpec((1, m, LANES), lambda i, s: (s[0], 0, 0)),
        ),
        out_shape=jax.ShapeDtypeStruct((8, m, LANES), F32),
        compiler_params=_params(("arbitrary",)),
    )(jnp.reshape(slot, (1,)).astype(jnp.int32), *parts, loss)


def _unpack_small(land):
    gates = ("gate_x_w", "gate_a_w")
    packed_shape = lambda k: (LRU_BLOCKS * LRU_BW * LRU_BW // LANES, LANES) if k in gates else _SMALL_SHAPES[k]
    *sums, loss = _sum_gathered(land, [packed_shape(k) for k in _SMALL] + [(1, 1)], "sum_small_grads")
    return {k: _blocks_from_lanes(g) if k in gates else g for k, g in zip(_SMALL, sums)}, loss


def kernel(x, norm_in, w_in, conv_w, conv_b, gate_x_w, gate_x_b, gate_a_w, gate_a_b, lru_lambda, gn_gain, w_proj_a, w_proj_b, w_out, norm_final, loss_target, m_norm_in, m_w_in, m_conv_w, m_conv_b, m_gate_x_w, m_gate_x_b, m_gate_a_w, m_gate_a_b, m_lru_lambda, m_gn_gain, m_w_proj_a, m_w_proj_b, m_w_out, m_norm_final, v_norm_in, v_w_in, v_conv_w, v_conv_b, v_gate_x_w, v_gate_x_b, v_gate_a_w, v_gate_a_b, v_lru_lambda, v_gn_gain, v_w_proj_a, v_w_proj_b, v_w_out, v_norm_final):
    B, S, _ = x.shape
    T = B * S
    xi, yi, ci = _coords()
    chip = 2 * xi + yi

    cshard = D_MODEL // N_CHIPS
    mine = _cast_into_slot([w_in[0].reshape(2, D_MODEL // 2, 2 * D_MODEL)], "cast_w_in")
    plan = _gather_plan(3)
    pending_proj = []
    gshard = DK // N_CHIPS
    tiny = jnp.concatenate([conv_w[0], jnp.zeros((4, cshard), F32), jnp.pad(gn_gain[0], ((0, 4), (0, cshard - gshard)))],
                           axis=0).reshape(1, 2, SUBLANES, cshard)
    tiny_buf = lax.dynamic_update_slice(lax.empty((N_CHIPS, 2, SUBLANES, cshard), F32), tiny, (chip, 0, 0, 0))
    near_plan, pass_plan, far_plan = (_chip_gather_plan(stage, 2) for stage in ("near", "pass", "far"))
    (n_near, _), (n_pass, passed_on), (n_far, _) = (_chip_gather_copies(stage, 2) for stage in ("near", "pass", "far"))
    halves = set(range(n_pass)) - passed_on
    near_s, near_r, bufs, near_token = _copies_start([mine[0], tiny_buf], near_plan, n_near, "gather_near_start")

    def in_proj(x2d, g_in, meanwhile):
        as_w = lambda b: b[0].reshape(N_CHIPS, D_MODEL, 2 * D_MODEL)
        slot_x, slot_y, slot_d = 2 * (1 - xi) + yi, 2 * xi + (1 - yi), 2 * (1 - xi) + (1 - yi)
        ids = lambda *chips: jnp.stack(chips).astype(jnp.int32)
        proj, hb, ht = _inproj_first(x2d, g_in, as_w(bufs), ids(chip), "inproj_own", (near_token, *meanwhile))
        got = _copies_wait(near_s, near_r, bufs, proj, near_plan, "gather_near_wait")
        pass_s, pass_r, got, pass_token = _copies_start(got, pass_plan, n_pass, "gather_pass_start")
        mine_proj = _cast_into_slot([w[0].reshape(2, cshard // 2, D_MODEL) for w in (w_proj_a, w_proj_b, w_out)],
                                    "cast_w_proj", (pass_token,))
        got = _copies_wait(pass_s, pass_r, got, mine_proj[0], pass_plan, "gather_pass_wait_halves", only=halves)
        proj = _inproj_more(hb, as_w(got), ids(slot_x, slot_y), proj, "inproj_near")
        got = _copies_wait(pass_s, pass_r, got, proj, pass_plan, "gather_pass_wait_far", only=passed_on)
        far_s, far_r, got, far_token = _copies_start(got, far_plan, n_far, "gather_far_start")
        pending_proj.append(_copies_start(mine_proj, plan, 9, "gather_proj_start", (far_token,)))
        got = _copies_wait(far_s, far_r, got, pending_proj[0][3], far_plan, "gather_far_wait")
        proj = _inproj_more(hb, as_w(got), ids(slot_d), proj, "inproj_far")
        tiny_all = got[1].reshape(N_CHIPS, 2 * SUBLANES, cshard)
        conv_w_full = jnp.transpose(tiny_all[:, 0:CONV, :], (1, 0, 2)).reshape(CONV, D_MODEL)
        gain_full = jnp.transpose(tiny_all[:, 8:8 + HEADS, :gshard], (1, 0, 2)).reshape(HEADS, DK)
        return proj, ht, as_w(got), conv_w_full, gain_full

    def proj_weights(after):
        s_sems, r_sems, pbufs, _ = pending_proj[0]
        got = _copies_wait(s_sems, r_sems, pbufs, after, plan, "gather_proj_wait")
        return [b.reshape(D_MODEL, D_MODEL) for b in got]

    weights = dict(norm_in=norm_in, w_in=w_in, conv_w=conv_w, conv_b=conv_b, gate_x_w=gate_x_w, gate_x_b=gate_x_b,
                   gate_a_w=gate_a_w, gate_a_b=gate_a_b, lru_lambda=lru_lambda, gn_gain=gn_gain, w_proj_a=w_proj_a,
                   w_proj_b=w_proj_b, w_out=w_out, norm_final=norm_final)
    ms = dict(norm_in=m_norm_in, w_in=m_w_in, conv_w=m_conv_w, conv_b=m_conv_b, gate_x_w=m_gate_x_w,
              gate_x_b=m_gate_x_b, gate_a_w=m_gate_a_w, gate_a_b=m_gate_a_b, lru_lambda=m_lru_lambda, gn_gain=m_gn_gain,
              w_proj_a=m_w_proj_a, w_proj_b=m_w_proj_b, w_out=m_w_out, norm_final=m_norm_final)
    vs = dict(norm_in=v_norm_in, w_in=v_w_in, conv_w=v_conv_w, conv_b=v_conv_b, gate_x_w=v_gate_x_w,
              gate_x_b=v_gate_x_b, gate_a_w=v_gate_a_w, gate_a_b=v_gate_a_b, lru_lambda=v_lru_lambda, gn_gain=v_gn_gain,
              w_proj_a=v_w_proj_a, w_proj_b=v_w_proj_b, w_out=v_w_out, norm_final=v_norm_final)
    names = list(weights)
    grads, delta, new_m, new_v = {}, {}, {}, {}

    def update_big(keys, g, half, prev, name, deps=()):
        two = lambda a: a.reshape(a.shape[1], a.shape[2])
        res = _adamw_halves([two(weights[k]) for k in keys], g, [two(ms[k]) for k in keys], [two(vs[k]) for k in keys],
                            half, prev, name, deps)
        for k, (gk, d, mn, vn) in zip(keys, res):
            shp = weights[k].shape
            grads[k], delta[k], new_m[k], new_v[k] = gk.reshape(shp), d.reshape(shp), mn.reshape(shp), vn.reshape(shp)
        return res

    def proj_done(g_proj, deps):
        g4 = g_proj.reshape(2, 3, D_MODEL // (2 * N_CHIPS), D_MODEL)
        return update_big(("w_proj_a", "w_proj_b", "w_out"), g4, None, None, "adamw_proj", deps)[-1][1]

    def w_in_own(owns, got, dest_sets):
        two = lambda a: a.reshape(a.shape[1], a.shape[2])
        return _sum_adamw_own(owns, got, dest_sets, two(w_in), two(m_w_in), two(v_w_in), "adamw_w_in_own")

    def w_in_done(g_in, prev):
        g4 = g_in.reshape(2, 1, D_MODEL // 2, 2 * D_MODEL)
        return update_big(("w_in",), g4, 1 - ci, [prev], "adamw_w_in_other")[0]

    reduce = _GradReduce(proj_done)
    grad_x, dgin = _local_grads(
        x.reshape(T, D_MODEL), loss_target.reshape(T, D_MODEL), B, S, norm_in, in_proj, conv_b,
        gate_x_w[0], gate_x_b, gate_a_w[0], gate_a_b, lru_lambda, proj_weights,
        norm_final.reshape(1, D_MODEL), reduce)

    (gsm, loss), g_norm_in = reduce.finish(dgin.reshape(SUBLANES, LANES), w_in_own, w_in_done)
    loss = loss[0, 0]
    gsm["norm_in"] = g_norm_in
    gsm["conv_w"] = lax.dynamic_slice_in_dim(gsm["conv_w"], chip * cshard, cshard, axis=1)
    gsm["gn_gain"] = lax.dynamic_slice_in_dim(gsm["gn_gain"], chip * gshard, gshard, axis=1)
    smalls = [k for k in names if k not in delta]

    def view(a):
        return a.reshape(1, -1) if a.ndim == 1 else (a.reshape(a.shape[1:]) if a.ndim > 2 else a)

    ds, mns, vns = _adamw_small([view(weights[k]) for k in smalls], [gsm[k].reshape(view(weights[k]).shape) for k in smalls],
                                [view(ms[k]) for k in smalls], [view(vs[k]) for k in smalls], "adamw_small")
    for k, d, mn, vn in zip(smalls, ds, mns, vns):
        shp = weights[k].shape
        grads[k], delta[k], new_m[k], new_v[k] = gsm[k].reshape(shp), d.reshape(shp), mn.reshape(shp), vn.reshape(shp)

    return (loss, grad_x.reshape(B, S, D_MODEL), *[grads[k] for k in names], *[delta[k] for k in names],
            *[new_m[k] for k in names], *[new_v[k] for k in names])
```

```python
import jax
import jax.numpy as jnp
from jax import lax
from jax.experimental import pallas as pl
from jax.experimental.pallas import tpu as pltpu

F32 = jnp.float32
_MXU = jnp.bfloat16

D_MODEL = 1024
N_GROUPS = 8
HEADS = 4
DK = 256
CHUNK = 128
CONV = 4
LRU_BLOCKS = 16
LRU_BW = 64
LRU_C = 8.0
ROPE_THETA = 10000.0
EPS = 1e-6
CW = 256
N_CT = D_MODEL // CW
N_CHIPS = 4
MESH = pl.DeviceIdType.MESH

ADAM_LR = 0.001
ADAM_B1 = 0.9
ADAM_B2 = 0.999
ADAM_EPS = 1e-08
ADAM_WD = 0.01
ADAM_STEP = 10

VMEM_LIMIT = 56 * 1024 * 1024

FIRST_PROJ_TILE = 1024
MORE_PROJ_TILE = 2048
SCAN_TILE = 1024
MID_TILE = 256
DX_TILE = 512
DW_COLS = 512
DW_LOADS = 4
RET_CHUNKS = 2
SUM_ROWS = 256
ADAMW_ROWS = 256
JOIN_PIECES = 8
PROJ_JOIN_PIECES = 4


def _c(v):
    return v.astype(_MXU)


def _dot(a, b):
    return lax.dot_general(a, b, (((1,), (0,)), ((), ())), preferred_element_type=F32)


def _dot_nt(a, b):
    return lax.dot_general(a, b, (((1,), (1,)), ((), ())), preferred_element_type=F32)


def _dot_tn(a, b):
    return lax.dot_general(a, b, (((0,), (0,)), ((), ())), preferred_element_type=F32)


def _sigmoid(z):
    return 0.5 * jnp.tanh(0.5 * z) + 0.5


ANY_SPEC = pl.BlockSpec(memory_space=pl.ANY)


def _after(body, n_in, deps):
    n_deps = len(deps)

    def wrapped(*refs):
        return body(*refs[:n_in], *refs[n_in + n_deps:])

    return wrapped


def _params(sem=None):
    if sem is None:
        return pltpu.CompilerParams(vmem_limit_bytes=VMEM_LIMIT)
    return pltpu.CompilerParams(vmem_limit_bytes=VMEM_LIMIT, dimension_semantics=sem)


def _inproj_first(x2d, g_in, w_all, chips, name, deps=()):
    T = x2d.shape[0]
    tm = min(FIRST_PROJ_TILE, T)
    n_i = T // tm

    def body(s_ref, *refs):
        x_ref, g_ref, w_ref = refs[:3]
        proj_ref, hb_ref, ht_ref, h_all = refs[-4:]
        i = pl.program_id(1)
        rows = pl.ds(pl.multiple_of(i * tm, tm), tm)

        @pl.when(pl.program_id(0) == 0)
        def _():
            x = x_ref[...]
            r = lax.rsqrt(jnp.mean(x * x, axis=-1, keepdims=True) + EPS)
            h = x * r * g_ref[...]
            hb = h.astype(h_all.dtype)
            h_all[rows, :] = hb
            hb_ref[...] = hb
            ht_ref[...] = h.T.astype(ht_ref.dtype)

        proj_ref[...] = _dot(h_all[rows, :], w_ref[0])

    first = lambda j, i: jnp.where(j == 0, i, n_i - 1)
    return pl.pallas_call(
        body,
        name=name,
        grid_spec=pltpu.PrefetchScalarGridSpec(
            num_scalar_prefetch=1,
            grid=(2 * chips.shape[0], n_i),
            in_specs=[
                pl.BlockSpec((tm, D_MODEL), lambda j, i, s: (first(j, i), 0)),
                pl.BlockSpec((1, D_MODEL), lambda j, i, s: (0, 0)),
                pl.BlockSpec((1, D_MODEL, D_MODEL), lambda j, i, s: (s[j // 2], 0, j % 2)),
            ] + [ANY_SPEC] * len(deps),
            out_specs=[
                pl.BlockSpec((tm, D_MODEL), lambda j, i, s: (i, 2 * s[j // 2] + j % 2)),
                pl.BlockSpec((tm, D_MODEL), lambda j, i, s: (first(j, i), 0)),
                pl.BlockSpec((D_MODEL, tm), lambda j, i, s: (0, first(j, i))),
            ],
            scratch_shapes=[pltpu.VMEM((T, D_MODEL), _MXU)],
        ),
        out_shape=[
            jax.ShapeDtypeStruct((T, N_GROUPS * D_MODEL), F32),
            jax.ShapeDtypeStruct((T, D_MODEL), _MXU),
            jax.ShapeDtypeStruct((D_MODEL, T), _MXU),
        ],
        compiler_params=_params(("arbitrary", "arbitrary")),
    )(chips, x2d, g_in, w_all, *deps)


def _inproj_more(hb, w_all, chips, proj, name):
    T = hb.shape[0]
    tm = min(MORE_PROJ_TILE, T)

    def body(s_ref, hb_hbm, w_ref, prev_ref, proj_ref, h_all, sem):
        @pl.when((pl.program_id(0) == 0) & (pl.program_id(1) == 0))
        def _():
            cp = pltpu.make_async_copy(hb_hbm, h_all, sem)
            cp.start()
            cp.wait()

        rows = pl.ds(pl.multiple_of(pl.program_id(1) * tm, tm), tm)
        proj_ref[...] = _dot(h_all[rows, :], w_ref[0])

    return pl.pallas_call(
        body,
        name=name,
        grid_spec=pltpu.PrefetchScalarGridSpec(
            num_scalar_prefetch=1,
            grid=(2 * chips.shape[0], T // tm),
            in_specs=[
                ANY_SPEC,
                pl.BlockSpec((1, D_MODEL, D_MODEL), lambda j, i, s: (s[j // 2], 0, j % 2)),
                ANY_SPEC,
            ],
            out_specs=pl.BlockSpec((tm, D_MODEL), lambda j, i, s: (i, 2 * s[j // 2] + j % 2)),
            scratch_shapes=[pltpu.VMEM((T, D_MODEL), hb.dtype), pltpu.SemaphoreType.DMA],
        ),
        out_shape=jax.ShapeDtypeStruct(proj.shape, F32),
        input_output_aliases={3: 0},
        compiler_params=_params(("arbitrary", "arbitrary")),
    )(chips, hb, w_all, proj)


def _scan_fwd(a, u):
    n = a.shape[0]
    row = lax.broadcasted_iota(jnp.int32, a.shape, 0)
    s = 1
    while s < n:
        m = row >= s
        u = u + a * jnp.where(m, pltpu.roll(u, s, 0), 0.0)
        a = a * jnp.where(m, pltpu.roll(a, s, 0), 1.0)
        s *= 2
    return a, u


def _scan_bwd(b, g):
    n = b.shape[0]
    row = lax.broadcasted_iota(jnp.int32, b.shape, 0)
    s = 1
    while s < n:
        m = row < n - s
        g = g + b * jnp.where(m, pltpu.roll(g, n - s, 0), 0.0)
        b = b * jnp.where(m, pltpu.roll(b, n - s, 0), 1.0)
        s *= 2
    return b, g


LANES = 128
SUBLANES = 8


def _scan_scratch(tc):
    by_lanes = pltpu.VMEM((CW // LANES, tc, LANES), F32)
    return [by_lanes, by_lanes, pltpu.VMEM((tc // SUBLANES, CW), F32), pltpu.VMEM((tc, CW), F32)]


def _scan_tile(a, u, edge, la_ref, lh_ref, c_ref, dst_ref, reverse):
    n, w = a.shape
    groups = n // SUBLANES
    a3 = a.reshape(groups, SUBLANES, w)
    u3 = u.reshape(groups, SUBLANES, w)
    row = lax.broadcasted_iota(jnp.int32, a3.shape, 1)
    for s in (1, 2, 4):
        m = (row < SUBLANES - s) if reverse else (row >= s)
        shift = SUBLANES - s if reverse else s
        u3 = u3 + a3 * jnp.where(m, pltpu.roll(u3, shift, 1), 0.0)
        a3 = a3 * jnp.where(m, pltpu.roll(a3, shift, 1), 1.0)
    al = a3.reshape(n, w)
    hl = u3.reshape(n, w)
    blocks = w // LANES
    for q in range(blocks):
        la_ref[q] = al[:, q * LANES:(q + 1) * LANES]
        lh_ref[q] = hl[:, q * LANES:(q + 1) * LANES]
    ends = pl.ds(0 if reverse else SUBLANES - 1, groups, stride=SUBLANES)
    end_a = jnp.concatenate([la_ref.at[q][ends, :] for q in range(blocks)], axis=-1)
    end_h = jnp.concatenate([lh_ref.at[q][ends, :] for q in range(blocks)], axis=-1)
    prod, part = (_scan_bwd if reverse else _scan_fwd)(end_a, end_h)
    total = part + prod * edge
    g_row = lax.broadcasted_iota(jnp.int32, total.shape, 0)
    if reverse:
        c_ref[...] = jnp.where(g_row == groups - 1, edge, pltpu.roll(total, groups - 1, 0))
    else:
        c_ref[...] = jnp.where(g_row == 0, edge, pltpu.roll(total, 1, 0))
    for g in range(groups):
        rows = slice(g * SUBLANES, (g + 1) * SUBLANES)
        for q in range(blocks):
            cols = slice(q * LANES, (q + 1) * LANES)
            dst_ref[rows, cols] = lh_ref[q, rows, :] + la_ref[q, rows, :] * c_ref[g:g + 1, cols]


def _softplus_neg(lam):
    z = -lam
    return jnp.maximum(z, 0.0) + jnp.log1p(jnp.exp(-jnp.abs(z)))


def _lru_gates(xc, wx_ref, wa_ref, bx_ref, ba_ref, lam_ref):
    xcb = _c(xc)
    i_t = _sigmoid(_dot(xcb, wx_ref[0]) + bx_ref[...])
    r_t = _sigmoid(_dot(xcb, wa_ref[0]) + ba_ref[...])
    sp = _softplus_neg(lam_ref[...])
    log_a = (-LRU_C) * r_t * sp
    a = jnp.exp(log_a)
    mult = jnp.sqrt(1.0 - a * a)
    return xcb, i_t, r_t, sp, a, mult


def _conv_from_ext(ext_ref, xa, cw_ref, cb_ref, tc):
    return (cb_ref[...] + cw_ref[3:4, :] * xa + cw_ref[2:3, :] * ext_ref[7:7 + tc, :]
            + cw_ref[1:2, :] * ext_ref[6:6 + tc, :] + cw_ref[0:1, :] * ext_ref[5:5 + tc, :])


def _lru_fwd(proj, conv_w, conv_b, wx_bd, wa_bd, bx, ba, lam, B, S):
    T = B * S
    tc = min(SCAN_TILE, S)
    nt = S // tc
    h8 = tc // 8

    def body(xa_ref, halo_ref, ga_ref, cw_ref, cb_ref, wx_ref, wa_ref, bx_ref, ba_ref, lam_ref,
             h_ref, ya_ref, ext_ref, carry_ref, la_ref, lh_ref, c_ref):
        t = pl.program_id(2)

        @pl.when(t == 0)
        def _():
            carry_ref[...] = jnp.zeros_like(carry_ref)

        xa = xa_ref[...]
        ext_ref[0:8, :] = jnp.where(t == 0, 0.0, halo_ref[...])
        ext_ref[8:8 + tc, :] = xa
        xc = _conv_from_ext(ext_ref, xa, cw_ref, cb_ref, tc)
        _, i_t, _, _, a, mult = _lru_gates(xc, wx_ref, wa_ref, bx_ref, ba_ref, lam_ref)
        u = mult * (i_t * xc)
        _scan_tile(a, u, carry_ref[7:8, :], la_ref, lh_ref, c_ref, h_ref, False)
        h = h_ref[...]
        carry_ref[...] = h[tc - 8:tc, :]
        ga = ga_ref[...]
        ya_ref[...] = (ga * _sigmoid(ga) * h).astype(ya_ref.dtype)

    row = lambda b, t: b * nt + t
    vec = pl.BlockSpec((1, CW), lambda b, c, t: (0, c))
    mat = pl.BlockSpec((1, CW, CW), lambda b, c, t: (c, 0, 0))
    return pl.pallas_call(
        body,
        name="lru_fwd",
        grid=(B, N_CT, nt),
        in_specs=[
            pl.BlockSpec((tc, CW), lambda b, c, t: (row(b, t), c)),
            pl.BlockSpec((8, CW), lambda b, c, t: (jnp.maximum(row(b, t) * h8 - 1, 0), c)),
            pl.BlockSpec((tc, CW), lambda b, c, t: (row(b, t), N_CT + c)),
            pl.BlockSpec((CONV, CW), lambda b, c, t: (0, c)),
            vec, mat, mat, vec, vec, vec,
        ],
        out_specs=[
            pl.BlockSpec((tc, CW), lambda b, c, t: (row(b, t), c)),
            pl.BlockSpec((tc, CW), lambda b, c, t: (row(b, t), c)),
        ],
        out_shape=[
            jax.ShapeDtypeStruct((T, D_MODEL), F32),
            jax.ShapeDtypeStruct((T, D_MODEL), _MXU),
        ],
        scratch_shapes=[pltpu.VMEM((tc + 8, CW), F32), pltpu.VMEM((8, CW), F32)] + _scan_scratch(tc)[:3],
        compiler_params=_params(("parallel", "parallel", "arbitrary")),
    )(proj, proj, proj, conv_w, conv_b, wx_bd, wa_bd, bx, ba, lam)


def _lru_bwd(dya, proj, hlru, conv_w, conv_b, wx_bd, wa_bd, bx, ba, lam, B, S, deps=()):
    T = B * S
    tc = min(SCAN_TILE, S)
    nt = S // tc
    h8 = tc // 8

    def body(dya_ref, xa_ref, xhalo_ref, ga_ref, h_ref, hhalo_ref, cw_ref, cb_ref, wx_ref, wa_ref, bx_ref, ba_ref,
             lam_ref, dxa_ref, dga_ref, dcw_ref, dcb_ref, dwx_ref, dwa_ref, dbx_ref, dba_ref, dlam_ref,
             ext_ref, ext2_ref, carry_ref, dhalo_ref, la_ref, lh_ref, c_ref, dh_ref, accx_ref, acca_ref):
        b = pl.program_id(1)
        t = pl.program_id(2)
        tt = nt - 1 - t

        @pl.when(t == 0)
        def _():
            carry_ref[...] = jnp.zeros_like(carry_ref)
            dhalo_ref[...] = jnp.zeros_like(dhalo_ref)

        @pl.when((t == 0) & (b == 0))
        def _():
            for r in (dcw_ref, dcb_ref, accx_ref, acca_ref, dbx_ref, dba_ref, dlam_ref):
                r[...] = jnp.zeros_like(r)

        xa = xa_ref[...]
        ext_ref[0:8, :] = jnp.where(tt == 0, 0.0, xhalo_ref[...])
        ext_ref[8:8 + tc, :] = xa
        xc = _conv_from_ext(ext_ref, xa, cw_ref, cb_ref, tc)
        xcb, i_t, r_t, sp, a, mult = _lru_gates(xc, wx_ref, wa_ref, bx_ref, ba_ref, lam_ref)

        h = h_ref[...]
        ga = ga_ref[...]
        dya_t = dya_ref[...]
        sg = _sigmoid(ga)
        dga_ref[...] = (dya_t * h * (sg * (1.0 + ga * (1.0 - sg)))).astype(dga_ref.dtype)
        dlru = dya_t * (ga * sg)

        row = lax.broadcasted_iota(jnp.int32, a.shape, 0)
        coef = jnp.where(row == tc - 1, 1.0, pltpu.roll(a, tc - 1, 0))
        _scan_tile(coef, dlru, carry_ref[0:1, :], la_ref, lh_ref, c_ref, dh_ref, True)
        dh = dh_ref[...]
        ext2_ref[0:tc, :] = a * dh
        carry_ref[...] = ext2_ref[0:8, :]

        ext2_ref[0:8, :] = jnp.where(tt == 0, 0.0, hhalo_ref[...])
        ext2_ref[8:8 + tc, :] = h
        hprev = ext2_ref[7:7 + tc, :]

        da = dh * hprev
        ix = i_t * xc
        dmult = dh * ix
        di = dh * mult * xc
        dxc = dh * mult * i_t
        dlog_a = da * a - dmult * (a * a) / mult
        dr = dlog_a * ((-LRU_C) * sp)
        dlam_ref[...] += jnp.sum(dlog_a * r_t, axis=0, keepdims=True) * (LRU_C * _sigmoid(-lam_ref[...]))
        dza = dr * r_t * (1.0 - r_t)
        dzx = di * i_t * (1.0 - i_t)
        dzab = _c(dza)
        dzxb = _c(dzx)
        dxc = dxc + _dot_nt(dzxb, wx_ref[0]) + _dot_nt(dzab, wa_ref[0])
        accx_ref[...] += _dot_tn(xcb, dzxb)
        acca_ref[...] += _dot_tn(xcb, dzab)
        dbx_ref[...] += jnp.sum(dzx, axis=0, keepdims=True)
        dba_ref[...] += jnp.sum(dza, axis=0, keepdims=True)

        dcb_ref[...] += jnp.sum(dxc, axis=0, keepdims=True)
        dcw_ref[3:4, :] += jnp.sum(dxc * xa, axis=0, keepdims=True)
        dcw_ref[2:3, :] += jnp.sum(dxc * ext_ref[7:7 + tc, :], axis=0, keepdims=True)
        dcw_ref[1:2, :] += jnp.sum(dxc * ext_ref[6:6 + tc, :], axis=0, keepdims=True)
        dcw_ref[0:1, :] += jnp.sum(dxc * ext_ref[5:5 + tc, :], axis=0, keepdims=True)
        ext2_ref[0:tc, :] = dxc
        ext2_ref[tc:tc + 8, :] = dhalo_ref[...]
        dxa = (cw_ref[3:4, :] * dxc + cw_ref[2:3, :] * ext2_ref[1:1 + tc, :]
               + cw_ref[1:2, :] * ext2_ref[2:2 + tc, :] + cw_ref[0:1, :] * ext2_ref[3:3 + tc, :])
        dxa_ref[...] = dxa.astype(dxa_ref.dtype)
        dhalo_ref[...] = ext2_ref[0:8, :]

        @pl.when((b == B - 1) & (t == nt - 1))
        def _():
            lane_block = lax.broadcasted_iota(jnp.int32, (LRU_BW, CW), 1) // LRU_BW
            for acc_ref, out_ref in ((accx_ref, dwx_ref), (acca_ref, dwa_ref)):
                diag = jnp.zeros((LRU_BW, CW), F32)
                for j in range(CW // LRU_BW):
                    diag = jnp.where(lane_block == j, acc_ref[j * LRU_BW:(j + 1) * LRU_BW, :], diag)
                for q in range(CW // LANES):
                    out_ref[0, q] = diag[:, q * LANES:(q + 1) * LANES]

    row_of = lambda b, t: b * nt + (nt - 1 - t)
    tile = lambda off: pl.BlockSpec((tc, CW), lambda c, b, t: (row_of(b, t), off + c))
    halo = pl.BlockSpec((8, CW), lambda c, b, t: (jnp.maximum(row_of(b, t) * h8 - 1, 0), c))
    vec = pl.BlockSpec((1, CW), lambda c, b, t: (0, c))
    mat = pl.BlockSpec((1, CW, CW), lambda c, b, t: (c, 0, 0))
    cwspec = pl.BlockSpec((CONV, CW), lambda c, b, t: (0, c))
    diag = pl.BlockSpec((1, CW // LANES, LRU_BW, LANES), lambda c, b, t: (c, 0, 0, 0))
    return pl.pallas_call(
        _after(body, 13, deps),
        name="lru_bwd",
        grid=(N_CT, B, nt),
        in_specs=[tile(0), tile(0), halo, tile(N_CT), tile(0), halo, cwspec, vec, mat, mat, vec, vec, vec]
        + [ANY_SPEC] * len(deps),
        out_specs=[tile(0), tile(0), cwspec, vec, diag, diag, vec, vec, vec],
        out_shape=[
            jax.ShapeDtypeStruct((T, D_MODEL), _MXU),
            jax.ShapeDtypeStruct((T, D_MODEL), _MXU),
            jax.ShapeDtypeStruct((CONV, D_MODEL), F32),
            jax.ShapeDtypeStruct((1, D_MODEL), F32),
            jax.ShapeDtypeStruct((N_CT, CW // LANES, LRU_BW, LANES), F32),
            jax.ShapeDtypeStruct((N_CT, CW // LANES, LRU_BW, LANES), F32),
            jax.ShapeDtypeStruct((1, D_MODEL), F32),
            jax.ShapeDtypeStruct((1, D_MODEL), F32),
            jax.ShapeDtypeStruct((1, D_MODEL), F32),
        ],
        scratch_shapes=[pltpu.VMEM((tc + 8, CW), F32), pltpu.VMEM((tc + 8, CW), F32),
                        pltpu.VMEM((8, CW), F32), pltpu.VMEM((8, CW), F32)] + _scan_scratch(tc)
        + [pltpu.VMEM((CW, CW), F32), pltpu.VMEM((CW, CW), F32)],
        compiler_params=_params(("parallel", "arbitrary", "arbitrary")),
    )(dya, proj, proj, proj, hlru, hlru, conv_w, conv_b, wx_bd, wa_bd, bx, ba, lam, *deps)


def _retention_tables(S):
    half = DK // 2
    freqs = ROPE_THETA ** (-jnp.arange(half, dtype=F32) / half)
    ang = jnp.arange(S, dtype=F32)[:, None] * freqs[None, :]
    log_g = jnp.log1p(-(2.0 ** (-5.0 - jnp.arange(HEADS, dtype=F32))))
    idx = jnp.arange(CHUNK, dtype=F32)
    diff = idx[:, None] - idx[None, :]
    inner = jnp.where(diff >= 0, jnp.exp(jnp.maximum(diff, 0.0)[None] * log_g[:, None, None]), 0.0)
    cross = jnp.exp((idx[None, :] + 1.0) * log_g[:, None])[:, :, None]
    state = jnp.exp((CHUNK - 1.0 - idx[None, :]) * log_g[:, None])[:, :, None]
    gam = jnp.broadcast_to(jnp.exp(CHUNK * log_g)[:, None, None], (HEADS, 1, DK))
    return jnp.cos(ang), jnp.sin(ang), inner, cross, state, gam


def _rot(x, cos, sin):
    half = DK // 2
    x1, x2 = x[:, :half], x[:, half:]
    return jnp.concatenate([x1 * cos - x2 * sin, x1 * sin + x2 * cos], axis=-1)


def _rot_t(y, cos, sin):
    half = DK // 2
    y1, y2 = y[:, :half], y[:, half:]
    return jnp.concatenate([y1 * cos + y2 * sin, y2 * cos - y1 * sin], axis=-1)


def _groupnorm(o):
    mu = jnp.mean(o, axis=-1, keepdims=True)
    oc = o - mu
    rs = lax.rsqrt(jnp.mean(oc * oc, axis=-1, keepdims=True) + EPS)
    return oc * rs, rs


def _ret_specs(B, chunk_of):
    rows = RET_CHUNKS * CHUNK
    qkv = lambda g: pl.BlockSpec((B, rows, D_MODEL), lambda c: (0, chunk_of(c), g))
    act = pl.BlockSpec((B, rows, D_MODEL), lambda c: (0, chunk_of(c), 0))
    rope = pl.BlockSpec((rows, DK // 2), lambda c: (chunk_of(c), 0))
    dmat = pl.BlockSpec((HEADS, CHUNK, CHUNK), lambda c: (0, 0, 0))
    dvec = pl.BlockSpec((HEADS, CHUNK, 1), lambda c: (0, 0, 0))
    hrow = pl.BlockSpec((HEADS, 1, DK), lambda c: (0, 0, 0))
    rst = pl.BlockSpec((RET_CHUNKS, B, HEADS, DK, DK), lambda c: (chunk_of(c), 0, 0, 0, 0))
    return qkv, act, rope, dmat, dvec, hrow, rst


def _ret_fwd(proj, tables, gain3, B, S):
    T = B * S
    nc = S // CHUNK
    cos, sin, dmat_t, cd_t, sd_t, gam_t = tables

    def body(q_ref, k_ref, v_ref, gb_ref, cos_ref, sin_ref, dm_ref, cd_ref, sd_ref, gam_ref, gain_ref,
             o_ref, yb_ref, rs_ref, state_ref):
        @pl.when(pl.program_id(0) == 0)
        def _():
            state_ref[...] = jnp.zeros_like(state_ref)

        for cc, b, h in [(cc, b, h) for cc in range(RET_CHUNKS) for b in range(B) for h in range(HEADS)]:
            rows = slice(cc * CHUNK, (cc + 1) * CHUNK)
            cos_t, sin_t = cos_ref[rows, :], sin_ref[rows, :]
            cols = slice(h * DK, (h + 1) * DK)
            qb = _c(_rot(q_ref[b, rows, cols], cos_t, sin_t))
            kb = _c(_rot(k_ref[b, rows, cols], cos_t, sin_t) * (DK ** -0.5))
            v = v_ref[b, rows, cols]
            state = state_ref[b, h]
            sb = _c(state)
            rs_ref[cc, b, h] = sb
            scores = _dot_nt(qb, kb) * dm_ref[h]
            o = _dot(_c(scores), _c(v)) + _dot(qb, sb) * cd_ref[h]
            state_ref[b, h] = gam_ref[h] * state + _dot_tn(kb, _c(v * sd_ref[h]))
            o_ref[b, rows, cols] = o
            n, _ = _groupnorm(o)
            gb = gb_ref[b, rows, cols]
            yb_ref[b, rows, cols] = (gb * _sigmoid(gb) * (n * gain_ref[h])).astype(yb_ref.dtype)

    qkv, act, rope, dmat, dvec, hrow, rst = _ret_specs(B, lambda c: c)
    proj3 = proj.reshape(B, S, proj.shape[1])
    o_pre, yb, states = pl.pallas_call(
        body,
        name="ret_fwd",
        grid=(nc // RET_CHUNKS,),
        in_specs=[qkv(2), qkv(3), qkv(4), qkv(5), rope, rope, dmat, dvec, dvec, hrow, hrow],
        out_specs=[act, act, rst],
        out_shape=[
            jax.ShapeDtypeStruct((B, S, D_MODEL), F32),
            jax.ShapeDtypeStruct((B, S, D_MODEL), _MXU),
            jax.ShapeDtypeStruct((nc, B, HEADS, DK, DK), _MXU),
        ],
        scratch_shapes=[pltpu.VMEM((B, HEADS, DK, DK), F32)],
        compiler_params=_params(("arbitrary",)),
    )(proj3, proj3, proj3, proj3, cos, sin, dmat_t, cd_t, sd_t, gam_t, gain3)
    return o_pre.reshape(T, D_MODEL), yb.reshape(T, D_MODEL), states


def _ret_bwd(dyb, o_pre, proj, states, tables, gain3, B, S, deps=()):
    T = B * S
    nc = S // CHUNK
    cos, sin, dmat_t, cd_t, sd_t, gam_t = tables

    def body(dyb_ref, o_ref, q_ref, k_ref, v_ref, gb_ref, rs_ref, cos_ref, sin_ref, dm_ref, cd_ref, sd_ref, gam_ref,
             gain_ref, dr_ref, dgain_ref, dstate_ref):
        @pl.when(pl.program_id(0) == 0)
        def _():
            dstate_ref[...] = jnp.zeros_like(dstate_ref)
            dgain_ref[...] = jnp.zeros_like(dgain_ref)

        for cc, b, h in [(cc, b, h) for cc in reversed(range(RET_CHUNKS)) for b in range(B) for h in range(HEADS)]:
            rows = slice(cc * CHUNK, (cc + 1) * CHUNK)
            cos_t, sin_t = cos_ref[rows, :], sin_ref[rows, :]
            cols = slice(h * DK, (h + 1) * DK)
            gain = gain_ref[h]
            n, rs = _groupnorm(o_ref[b, rows, cols])
            gb = gb_ref[b, rows, cols]
            sg = _sigmoid(gb)
            dy = dyb_ref[b, rows, cols]
            part = lambda g: slice(g * D_MODEL + h * DK, g * D_MODEL + (h + 1) * DK)
            dr_ref[b, rows, part(3)] = (dy * (n * gain) * (sg * (1.0 + gb * (1.0 - sg)))).astype(dr_ref.dtype)
            dgn = dy * (gb * sg)
            dgain_ref[h] += jnp.sum(dgn * n, axis=0, keepdims=True)
            dn = dgn * gain
            do = rs * (dn - jnp.mean(dn, axis=-1, keepdims=True) - n * jnp.mean(dn * n, axis=-1, keepdims=True))

            qb = _c(_rot(q_ref[b, rows, cols], cos_t, sin_t))
            kb = _c(_rot(k_ref[b, rows, cols], cos_t, sin_t) * (DK ** -0.5))
            v = v_ref[b, rows, cols]
            vb = _c(v)
            vsb = _c(v * sd_ref[h])
            dob = _c(do)
            docb = _c(do * cd_ref[h])
            dmat = dm_ref[h]
            dstate = dstate_ref[b, h]
            dsb = _c(dstate)
            pb = _c(_dot_nt(qb, kb) * dmat)
            dsc = _c(_dot_nt(dob, vb) * dmat)
            dq = _dot(dsc, kb) + _dot_nt(docb, rs_ref[cc, b, h])
            dk = _dot_tn(dsc, qb) + _dot_nt(vsb, dsb)
            dv = _dot_tn(pb, dob) + _dot(kb, dsb) * sd_ref[h]
            dstate_ref[b, h] = gam_ref[h] * dstate + _dot_tn(qb, docb)
            dr_ref[b, rows, part(0)] = _rot_t(dq, cos_t, sin_t).astype(dr_ref.dtype)
            dr_ref[b, rows, part(1)] = (_rot_t(dk, cos_t, sin_t) * (DK ** -0.5)).astype(dr_ref.dtype)
            dr_ref[b, rows, part(2)] = dv.astype(dr_ref.dtype)

    n_steps = nc // RET_CHUNKS
    qkv, act, rope, dmat, dvec, hrow, rst = _ret_specs(B, lambda c: n_steps - 1 - c)
    wide = pl.BlockSpec((B, RET_CHUNKS * CHUNK, 4 * D_MODEL), lambda c: (0, n_steps - 1 - c, 0))
    proj3 = proj.reshape(B, S, proj.shape[1])
    dr, dgain = pl.pallas_call(
        _after(body, 14, deps),
        name="ret_bwd",
        grid=(n_steps,),
        in_specs=[act, act, qkv(2), qkv(3), qkv(4), qkv(5), rst, rope, rope, dmat, dvec, dvec, hrow, hrow]
        + [ANY_SPEC] * len(deps),
        out_specs=[wide, hrow],
        out_shape=[jax.ShapeDtypeStruct((B, S, 4 * D_MODEL), _MXU), jax.ShapeDtypeStruct((HEADS, 1, DK), F32)],
        scratch_shapes=[pltpu.VMEM((B, HEADS, DK, DK), F32)],
        compiler_params=_params(("arbitrary",)),
    )(dyb.reshape(B, S, D_MODEL), o_pre.reshape(B, S, D_MODEL), proj3, proj3, proj3, proj3, states, cos, sin, dmat_t,
      cd_t, sd_t, gam_t, gain3, *deps)
    return dr.reshape(T, 4 * D_MODEL), dgain


def _mid(ya, yb, proj, x2d, tgt2d, wpa, wpb, wout, g_fin):
    T = x2d.shape[0]
    tm = min(MID_TILE, T)
    n_steps = T // tm
    rows = D_MODEL // (2 * N_CHIPS)

    def body(ya_ref, yb_ref, ma_ref, mb_ref, x_ref, t_ref, gf_ref, wpa_hbm, wpb_hbm, wout_hbm,
             loss_ref, dx2_ref, dya_ref, dyb_ref, dm_ref, dgf_ref, gw_hbm, w_ref, acc_ref, sem):
        i = pl.program_id(0)

        @pl.when(i == 0)
        def _():
            loads = [pltpu.make_async_copy(src, w_ref.at[k], sem.at[k]) for k, src in enumerate((wpa_hbm, wpb_hbm, wout_hbm))]
            for cp in loads:
                cp.start()
            for cp in loads:
                cp.wait()
            acc_ref[...] = jnp.zeros_like(acc_ref)
            loss_ref[...] = jnp.zeros_like(loss_ref)
            dgf_ref[...] = jnp.zeros_like(dgf_ref)

        ya_t, yb_t = ya_ref[...], yb_ref[...]
        out_a = _dot(ya_t, w_ref[0])
        out_b = _dot(yb_t, w_ref[1])
        sa = _sigmoid(ma_ref[...])
        sb = _sigmoid(mb_ref[...])
        mgb = _c(sa * out_a + sb * out_b)
        x2 = x_ref[...] + _dot(mgb, w_ref[2])
        r2 = lax.rsqrt(jnp.mean(x2 * x2, axis=-1, keepdims=True) + EPS)
        nx = x2 * r2
        gf = gf_ref[...]
        err = nx * gf - t_ref[...]
        loss_ref[...] += 0.5 * jnp.sum(jnp.mean(err * err, axis=-1, keepdims=True), axis=0, keepdims=True)
        dy = err * (1.0 / D_MODEL)
        dgf_ref[...] += jnp.sum(dy * nx, axis=0, keepdims=True)
        dyg = dy * gf
        dx2 = r2 * (dyg - nx * jnp.mean(dyg * nx, axis=-1, keepdims=True))
        dx2_ref[...] = dx2
        dx2b = _c(dx2)
        dmg = _dot_nt(dx2b, w_ref[2])
        acc_ref[2] += _dot_tn(mgb, dx2b)
        dm_ref[:, :D_MODEL] = (dmg * out_a * sa * (1.0 - sa)).astype(dm_ref.dtype)
        dm_ref[:, D_MODEL:] = (dmg * out_b * sb * (1.0 - sb)).astype(dm_ref.dtype)
        dab = _c(dmg * sa)
        dbb = _c(dmg * sb)
        dya_ref[...] = _dot_nt(dab, w_ref[0])
        dyb_ref[...] = _dot_nt(dbb, w_ref[1])
        acc_ref[0] += _dot_tn(ya_t, dab)
        acc_ref[1] += _dot_tn(yb_t, dbb)

        @pl.when(i == n_steps - 1)
        def _():
            copies = [pltpu.make_async_copy(acc_ref.at[k, pl.ds((2 * p + hf) * rows, rows), :], gw_hbm.at[p, hf, k],
                                            sem.at[(k * N_CHIPS + p) * 2 + hf])
                      for k in range(3) for p in range(N_CHIPS) for hf in range(2)]
            for cp in copies:
                cp.start()
            for cp in copies:
                cp.wait()

    tile = lambda j: pl.BlockSpec((tm, D_MODEL), lambda i: (i, j))
    one = pl.BlockSpec((1, D_MODEL), lambda i: (0, 0))
    anyspec = pl.BlockSpec(memory_space=pl.ANY)
    return pl.pallas_call(
        body,
        name="mid",
        grid=(n_steps,),
        in_specs=[tile(0), tile(0), tile(6), tile(7), tile(0), tile(0), one, anyspec, anyspec, anyspec],
        out_specs=[pl.BlockSpec((1, 1), lambda i: (0, 0)), tile(0), tile(0), tile(0),
                   pl.BlockSpec((tm, 2 * D_MODEL), lambda i: (i, 0)), one, anyspec],
        out_shape=[
            jax.ShapeDtypeStruct((1, 1), F32),
            jax.ShapeDtypeStruct((T, D_MODEL), F32),
            jax.ShapeDtypeStruct((T, D_MODEL), F32),
            jax.ShapeDtypeStruct((T, D_MODEL), F32),
            jax.ShapeDtypeStruct((T, 2 * D_MODEL), _MXU),
            jax.ShapeDtypeStruct((1, D_MODEL), F32),
            jax.ShapeDtypeStruct((N_CHIPS, 2, 3, rows, D_MODEL), F32),
        ],
        scratch_shapes=[pltpu.VMEM((3, D_MODEL, D_MODEL), _MXU), pltpu.VMEM((3, D_MODEL, D_MODEL), F32),
                        pltpu.SemaphoreType.DMA((3 * N_CHIPS * 2,))],
        compiler_params=_params(("arbitrary",)),
    )(ya, yb, proj, proj, x2d, tgt2d, g_fin, wpa, wpb, wout)


def _inproj_bwd_dx(dparts, w_all, x2d, dx2, g_in, first, count, prev, name, deps=()):
    T = x2d.shape[0]
    tm = min(DX_TILE, T)
    n_d = len(dparts)
    groups = [(a, k) for a, d in enumerate(dparts) for k in range(d.shape[1] // D_MODEL)]
    dg_start = jnp.zeros((1, D_MODEL), F32) if prev is None else prev[1]
    carried = () if prev is None else (prev[0],)

    def body(*refs):
        d_refs = refs[:n_d]
        x_ref, dx2_ref, g_ref, dg0_ref, w_hbm = refs[n_d:n_d + 5]
        dx_ref, dg_ref, w_ref, sem = refs[-4:]

        def load(j):
            part = (j // 2, slice(None), pl.ds((j % 2) * D_MODEL, D_MODEL))
            return pltpu.make_async_copy(w_hbm.at[part], w_ref.at[part], sem.at[j])

        def tile(before_group):
            dh = jnp.zeros((tm, D_MODEL), F32)
            for j, (a, k) in enumerate(groups):
                before_group(j)
                dh = dh + _dot_nt(d_refs[a][:, k * D_MODEL:(k + 1) * D_MODEL],
                                  w_ref[j // 2, :, (j % 2) * D_MODEL:(j % 2 + 1) * D_MODEL])
            x = x_ref[...]
            r = lax.rsqrt(jnp.mean(x * x, axis=-1, keepdims=True) + EPS)
            nx = x * r
            dg_ref[...] += jnp.sum(dh * nx, axis=0, keepdims=True)
            dhg = dh * g_ref[...]
            dx_ref[...] = dx2_ref[...] + r * (dhg - nx * jnp.mean(dhg * nx, axis=-1, keepdims=True))

        first = pl.program_id(0) == 0

        @pl.when(first)
        def _():
            for j in range(len(groups)):
                load(j).start()
            dg_ref[...] = dg0_ref[...]
            tile(lambda j: load(j).wait())

        @pl.when(jnp.logical_not(first))
        def _():
            tile(lambda j: None)

    tile = pl.BlockSpec((tm, D_MODEL), lambda i: (first + i, 0))
    one = pl.BlockSpec((1, D_MODEL), lambda i: (0, 0))
    return pl.pallas_call(
        body,
        name=name,
        grid=(count,),
        in_specs=[pl.BlockSpec((tm, d.shape[1]), lambda i: (first + i, 0)) for d in dparts]
        + [tile, tile, one, one, ANY_SPEC] + [ANY_SPEC] * (len(carried) + len(deps)),
        out_specs=[tile, one],
        out_shape=[jax.ShapeDtypeStruct((T, D_MODEL), F32), jax.ShapeDtypeStruct((1, D_MODEL), F32)],
        input_output_aliases={n_d + 5: 0} if carried else {},
        scratch_shapes=[pltpu.VMEM(w_all.shape, w_all.dtype), pltpu.SemaphoreType.DMA((len(groups),))],
        compiler_params=_params(("arbitrary",)),
    )(*dparts, x2d, dx2, g_in, dg_start, w_all, *carried, *deps)


def _inproj_bwd_dw(ht, dparts, name, deps=()):
    T = ht.shape[1]
    tn = DW_COLS
    half = D_MODEL // 2
    per_chip = 2 * D_MODEL // tn
    n_d = len(dparts)
    tiles = [(a, t) for a, d in enumerate(dparts) for t in range(d.shape[1] // tn)]
    offs = [sum(d.shape[1] // tn for d in dparts[:a]) for a in range(n_d)]

    def body(*refs):
        ht_hbm = refs[0]
        d_refs = refs[1:1 + n_d]
        out_ref, ht_ref, sem = refs[-3:]
        t = pl.program_id(0)

        def load(k):
            cols = pl.ds(k * (T // DW_LOADS), T // DW_LOADS)
            return pltpu.make_async_copy(ht_hbm.at[:, cols], ht_ref.at[:, cols], sem.at[k])

        def store(g):
            out_ref[0, 0] = g[:half]
            out_ref[0, 1] = g[half:]

        @pl.when(t == 0)
        def _():
            for k in range(DW_LOADS):
                load(k).start()
            g = jnp.zeros((D_MODEL, tn), F32)
            for k in range(DW_LOADS):
                load(k).wait()
                tokens = slice(k * (T // DW_LOADS), (k + 1) * (T // DW_LOADS))
                g = g + _dot(ht_ref[:, tokens], d_refs[0][tokens, :])
            store(g)

        for a in range(n_d):
            lo, hi = max(offs[a], 1), offs[a] + dparts[a].shape[1] // tn

            @pl.when((t >= lo) & (t < hi))
            def _(a=a):
                store(_dot(ht_ref[...], d_refs[a][...]))

    def dspec(a):
        n_a = dparts[a].shape[1] // tn
        return pl.BlockSpec((T, tn), lambda t: (0, jnp.clip(t - offs[a], 0, n_a - 1)))

    return pl.pallas_call(
        body,
        name=name,
        grid=(len(tiles),),
        in_specs=[ANY_SPEC] + [dspec(a) for a in range(n_d)] + [ANY_SPEC] * len(deps),
        out_specs=pl.BlockSpec((1, 2, half, tn), lambda t: (t // per_chip, 0, 0, t % per_chip)),
        out_shape=jax.ShapeDtypeStruct((len(tiles) // per_chip, 2, half, 2 * D_MODEL), F32),
        scratch_shapes=[pltpu.VMEM(ht.shape, ht.dtype), pltpu.SemaphoreType.DMA((DW_LOADS,))],
        compiler_params=_params(("arbitrary",)),
    )(ht, *dparts, *deps)


def _coords():
    return lax.axis_index("x"), lax.axis_index("y"), lax.axis_index("c")


def _other_chips(x, y):
    return [(1 - x, y), (x, 1 - y), (1 - x, 1 - y)]


def _chunks(rows, n):
    size = rows // n
    return [pl.ds(q * size, size) for q in range(n)]


HBM_SPEC = pl.BlockSpec(memory_space=pltpu.HBM)
SEM_SPEC = pl.BlockSpec(memory_space=pltpu.SEMAPHORE)
DATAFLOW = pltpu.SideEffectType.DATAFLOW_SIDE_EFFECTING


def _copies_start(bufs, plan, n_copies, name, deps=()):
    n = len(bufs)
    n_deps = len(deps)

    def body(*refs):
        ins = refs[:n]
        send_sems, recv_sems = refs[n + n_deps], refs[n + n_deps + 1]
        token = refs[-1]
        for k, send, _ in plan(ins):
            if send is not None:
                src, dst, dev, pred = send
                cp = pltpu.make_async_remote_copy(src_ref=src, dst_ref=dst, send_sem=send_sems.at[k],
                                                  recv_sem=recv_sems.at[k], device_id=dev, device_id_type=MESH)
                if pred is None:
                    cp.start()
                else:
                    pl.when(pred)(cp.start)
        token[...] = jnp.zeros_like(token)

    hbm = [pltpu.with_memory_space_constraint(b, pltpu.HBM) for b in bufs]
    outs = pl.pallas_call(
        body,
        name=name,
        in_specs=[HBM_SPEC] * n + [ANY_SPEC] * n_deps,
        out_specs=(SEM_SPEC, SEM_SPEC, *([HBM_SPEC] * n), pl.BlockSpec(memory_space=pltpu.VMEM)),
        out_shape=(pltpu.SemaphoreType.DMA((n_copies,)), pltpu.SemaphoreType.DMA((n_copies,)),
                   *[pltpu.HBM(b.shape, b.dtype) for b in bufs], jax.ShapeDtypeStruct((8, 128), F32)),
        input_output_aliases={a: 2 + a for a in range(n)},
        compiler_params=pltpu.CompilerParams(has_side_effects=DATAFLOW),
    )(*hbm, *deps)
    return outs[0], outs[1], list(outs[2:2 + n]), outs[-1]


def _copies_wait(send_sems, recv_sems, bufs, after, plan, name, only=None):
    n = len(bufs)

    def body(*refs):
        ins = refs[:n]
        s_sems, r_sems = refs[n], refs[n + 1]
        for k, send, recv in plan(ins):
            if only is not None and k not in only:
                continue
            if send is not None:
                src, dst, dev, pred = send
                cp = pltpu.make_async_remote_copy(src_ref=src, dst_ref=dst, send_sem=s_sems.at[k],
                                                  recv_sem=r_sems.at[k], device_id=dev, device_id_type=MESH)
                if pred is None:
                    cp.wait_send()
                else:
                    pl.when(pred)(cp.wait_send)
            if recv is not None:
                dst, pred = recv
                cp = pltpu.make_async_remote_copy(src_ref=dst, dst_ref=dst, send_sem=s_sems.at[k],
                                                  recv_sem=r_sems.at[k], device_id=_coords(), device_id_type=MESH)
                if pred is None:
                    cp.wait_recv()
                else:
                    pl.when(pred)(cp.wait_recv)

    outs = pl.pallas_call(
        body,
        name=name,
        in_specs=[HBM_SPEC] * n + [SEM_SPEC, SEM_SPEC, pl.BlockSpec(memory_space=pl.ANY)],
        out_specs=[HBM_SPEC] * n,
        out_shape=[pltpu.HBM(b.shape, b.dtype) for b in bufs],
        input_output_aliases={a: a for a in range(n)},
        compiler_params=pltpu.CompilerParams(has_side_effects=DATAFLOW),
    )(*bufs, send_sems, recv_sems, after)
    return list(outs)


def _gather_plan(n_bufs):
    def plan(refs):
        x, y, c = _coords()
        me = 2 * x + y
        out = []
        for k, (px, py) in enumerate(_other_chips(x, y)):
            for a in range(n_bufs):
                out.append((k * n_bufs + a, (refs[a].at[me], refs[a].at[me], (px, py, c), None),
                            (refs[a].at[2 * px + py], None)))
        return out
    return plan


def _cast_into_slot(ws, name, deps=()):
    n = len(ws)
    nt = 2

    def body(s_ref, *refs):
        outs = refs[len(refs) - n:]
        for a in range(n):
            outs[a][0] = refs[a][...].astype(outs[a].dtype)

    xi, yi, _ = _coords()
    return pl.pallas_call(
        body,
        name=name,
        grid_spec=pltpu.PrefetchScalarGridSpec(
            num_scalar_prefetch=1,
            grid=(2, nt),
            in_specs=[pl.BlockSpec((1, w.shape[1] // nt, w.shape[2]), lambda hf, i, s: (hf, i, 0)) for w in ws]
            + [ANY_SPEC] * len(deps),
            out_specs=[pl.BlockSpec((1, 1, w.shape[1] // nt, w.shape[2]), lambda hf, i, s: (s[0], hf, i, 0)) for w in ws],
        ),
        out_shape=[jax.ShapeDtypeStruct((N_CHIPS,) + w.shape, _MXU) for w in ws],
        compiler_params=_params(("parallel", "parallel")),
    )((2 * xi + yi).reshape(1).astype(jnp.int32), *ws, *deps)


def _chip_gather_plan(stage, n_bufs):
    def plan(refs):
        x, y, c = _coords()
        me = 2 * x + y
        near = [(1 - x, y), (x, 1 - y)]
        slots = [2 * (1 - x) + y, 2 * x + (1 - y), 2 * (1 - x) + (1 - y)]
        sibling = (x, y, 1 - c)
        pass_to = (jnp.where(c == 0, x, 1 - x), jnp.where(c == 0, 1 - y, y), c)
        pass_slot = jnp.where(c == 0, slots[0], slots[1])
        out = []

        def move(src_slot, to, land_slot, land_core, pieces):
            for a, buf in enumerate(refs):
                for rows in _chunks(buf.shape[2], pieces[a]):
                    out.append((len(out), (buf.at[src_slot, c, rows], buf.at[src_slot, c, rows], to, None),
                                (buf.at[land_slot, land_core, rows], None)))

        if stage == "near":
            for k, chip in enumerate(near):
                move(me, (*chip, c), slots[k], c, NEAR_PIECES[:n_bufs])
        elif stage == "pass":
            move(pass_slot, pass_to, slots[2], c, PASS_PIECES[:n_bufs])
            for k in range(2):
                move(slots[k], sibling, slots[k], 1 - c, [1] * n_bufs)
        else:
            move(slots[2], sibling, slots[2], 1 - c, [1] * n_bufs)
        return out
    return plan


NEAR_PIECES = (2, 1)
PASS_PIECES = (2, 1)


def _chip_gather_copies(stage, n_bufs):
    if stage == "near":
        return 2 * sum(NEAR_PIECES[:n_bufs]), None
    if stage == "pass":
        n_pass = sum(PASS_PIECES[:n_bufs])
        return n_pass + 2 * n_bufs, set(range(n_pass))
    return n_bufs, None


def _swap_plan(n_slabs):
    def plan(refs):
        x, y, c = _coords()
        out, k = [], 0
        for i, n in enumerate(n_slabs):
            g, land = refs[2 * i], refs[2 * i + 1]
            for p in range(n):
                out.append((k, (g.at[p, 1 - c], land.at[p], (x, y, 1 - c), None), (land.at[p], None)))
                k += 1
        return out
    return plan


def _is_one_of(chip, dests):
    hit = chip == dests[0]
    for d in dests[1:]:
        hit = hit | (chip == d)
    return hit


def _slab_of(chip, dests):
    return sum(j * (chip == d).astype(jnp.int32) for j, d in enumerate(dests))


def _scatter_plan(dest_sets):
    def plan(refs):
        x, y, c = _coords()
        me = 2 * x + y
        out = []
        for k, (px, py) in enumerate(_other_chips(x, y)):
            peer = 2 * px + py
            for i, dests in enumerate(dest_sets):
                cs, land = refs[2 * i], refs[2 * i + 1]
                everyone = len(dests) == N_CHIPS
                send = (cs.at[_slab_of(peer, dests)], land.at[k], (px, py, c),
                        None if everyone else _is_one_of(peer, dests))
                recv = (land.at[k], None if everyone else _is_one_of(me, dests))
                out.append((k * len(dest_sets) + i, send, recv))
        return out
    return plan


def _join_plan(rows, n_pieces):
    def plan(refs):
        x, y, c = _coords()
        (buf,) = refs
        return [(i, (buf.at[c, piece], buf.at[c, piece], (x, y, 1 - c), None), (buf.at[1 - c, piece], None))
                for i, piece in enumerate(_chunks(rows, n_pieces))]
    return plan


def _join_plans(parts):
    def plan(refs):
        out, b0, k0 = [], 0, 0
        for part_plan, n_bufs, n_copies in parts:
            out += [(k0 + k, send, recv) for k, send, recv in part_plan(refs[b0:b0 + n_bufs])]
            b0 += n_bufs
            k0 += n_copies
        return out
    return plan


def _allgather_plan():
    def plan(refs):
        x, y, c = _coords()
        (land,) = refs
        me = 4 * x + 2 * y + c
        out = []
        for r in range(1, 8):
            px = 1 - x if r & 4 else x
            py = 1 - y if r & 2 else y
            pc = 1 - c if r & 1 else c
            out.append((r - 1, (land.at[me], land.at[me], (px, py, pc), None), (land.at[4 * px + 2 * py + pc], None)))
        return out
    return plan


def _sum_gathered(land, shapes, name):
    m = land.shape[1]

    def body(land_ref, *refs):
        outs, acc_ref = refs[:-1], refs[-1]
        acc = land_ref[0]
        for d in range(1, 8):
            acc = acc + land_ref[d]
        acc_ref[...] = acc
        r = 0
        for o_ref, (n, w) in zip(outs, shapes):
            if w == LANES:
                o_ref[...] = acc_ref[r:r + n, :]
                r += n
            elif w < LANES:
                o_ref[...] = acc_ref[r:r + n, 0:w]
                r += SUBLANES
            else:
                for k in range(n):
                    for q in range(w // LANES):
                        o_ref[k:k + 1, q * LANES:(q + 1) * LANES] = acc_ref[r:r + 1, :]
                        r += 1
        assert r == m, (r, m)

    return pl.pallas_call(
        body,
        name=name,
        out_shape=[jax.ShapeDtypeStruct(s, F32) for s in shapes],
        scratch_shapes=[pltpu.VMEM((m, LANES), F32)],
        compiler_params=_params(),
    )(land)


def _row_tile(rows, cap):
    t = cap
    while rows % t:
        t //= 2
    return t


def _add_my_halves(pairs, steps, name):
    k = len(pairs)

    def body(c_ref, *refs):
        for a in range(k):
            o_ref = refs[2 * k + a]
            o_ref[...] = (refs[2 * a][0] + refs[2 * a + 1][...]).astype(o_ref.dtype)

    in_specs, out_specs = [], []
    for g, _ in pairs:
        n_slabs, _, R, C = g.shape
        per = steps // n_slabs
        tr = R // per
        in_specs += [pl.BlockSpec((1, 1, tr, C), lambda p, c_ref, per=per: (p // per, c_ref[0], p % per, 0)),
                     pl.BlockSpec((1, tr, C), lambda p, c_ref, per=per: (p // per, p % per, 0))]
        out_specs.append(pl.BlockSpec((1, tr, C), lambda p, c_ref, per=per: (p // per, p % per, 0)))
    return pl.pallas_call(
        body,
        name=name,
        grid_spec=pltpu.PrefetchScalarGridSpec(num_scalar_prefetch=1, grid=(steps,), in_specs=in_specs, out_specs=out_specs),
        out_shape=[jax.ShapeDtypeStruct(r.shape, jnp.bfloat16) for _, r in pairs],
        compiler_params=_params(("parallel",)),
    )(lax.axis_index("c").reshape(1).astype(jnp.int32), *[a for pair in pairs for a in pair])


def _add_my_half(g, r, name):
    n_slabs, _, R, _ = g.shape
    steps = n_slabs if n_slabs > 1 else R // _row_tile(R, SUM_ROWS)
    return _add_my_halves([(g, r)], steps, name)[0]


def _sum_slabs(own, got, name, deps=()):
    _, R, C = own.shape
    tr = _row_tile(R, SUM_ROWS)

    def body(s_ref, own_ref, got_ref, *rest):
        rest[-1][0] = ((own_ref[0].astype(F32) + got_ref[0].astype(F32)) + got_ref[1].astype(F32)) + got_ref[2].astype(F32)

    xi, yi, ci = _coords()
    return pl.pallas_call(
        body,
        name=name,
        grid_spec=pltpu.PrefetchScalarGridSpec(
            num_scalar_prefetch=1,
            grid=(R // tr,),
            in_specs=[pl.BlockSpec((1, tr, C), lambda i, s: (s[0], i, 0)),
                      pl.BlockSpec((3, tr, C), lambda i, s: (0, i, 0))] + [ANY_SPEC] * len(deps),
            out_specs=pl.BlockSpec((1, tr, C), lambda i, s: (s[1], i, 0)),
        ),
        out_shape=jax.ShapeDtypeStruct((2, R, C), F32),
        compiler_params=_params(("parallel",)),
    )(jnp.stack([2 * xi + yi, ci]).astype(jnp.int32), own, got, *deps)


def _adamw_math(w, g, m, v):
    m = ADAM_B1 * m + (1.0 - ADAM_B1) * g
    v = ADAM_B2 * v + (1.0 - ADAM_B2) * (g * g)
    m_hat = m / (1.0 - ADAM_B1 ** ADAM_STEP)
    v_hat = v / (1.0 - ADAM_B2 ** ADAM_STEP)
    delta = -ADAM_LR * (m_hat / (jnp.sqrt(v_hat) + ADAM_EPS) + ADAM_WD * w)
    return delta, m, v


def _sum_adamw_own(owns, got, dest_sets, w, m, v, name):
    n = len(owns)
    _, R, C = owns[0].shape
    tr = _row_tile(R, ADAMW_ROWS)
    steps = R // tr

    def body(s_ref, *refs):
        got_ref, w_ref, m_ref, v_ref = refs[n:n + 4]
        half_ref, g_ref, d_ref, mn_ref, vn_ref = refs[len(refs) - 5:]
        total = jnp.zeros((tr, C), F32)
        for i in range(n):
            total = total + jnp.where(s_ref[2 + 2 * i] == 1, refs[i][0].astype(F32), 0.0)
        grad = ((total + got_ref[0].astype(F32)) + got_ref[1].astype(F32)) + got_ref[2].astype(F32)
        d, mn, vn = _adamw_math(w_ref[...], grad, m_ref[...], v_ref[...])
        half_ref[0] = grad
        for o, val in zip((g_ref, d_ref, mn_ref, vn_ref), (grad, d, mn, vn)):
            o[...] = val

    xi, yi, ci = _coords()
    me = 2 * xi + yi
    scalars = [ci, ci]
    for dests in dest_sets:
        scalars += [_is_one_of(me, dests).astype(jnp.int32), _slab_of(me, dests)]
    own_spec = lambda i: pl.BlockSpec((1, tr, C), lambda r, s: (s[3 + 2 * i], r, 0))
    rows = pl.BlockSpec((tr, C), lambda r, s: (s[0] * steps + r, 0))
    outs = pl.pallas_call(
        body,
        name=name,
        grid_spec=pltpu.PrefetchScalarGridSpec(
            num_scalar_prefetch=1,
            grid=(steps,),
            in_specs=[own_spec(i) for i in range(n)] + [pl.BlockSpec((3, tr, C), lambda r, s: (0, r, 0))] + [rows] * 3,
            out_specs=[pl.BlockSpec((1, tr, C), lambda r, s: (s[0], r, 0))] + [rows] * 4,
        ),
        out_shape=[jax.ShapeDtypeStruct((2, R, C), F32)] + [jax.ShapeDtypeStruct((2 * R, C), F32)] * 4,
        compiler_params=_params(("parallel",)),
    )(jnp.stack(scalars).astype(jnp.int32), *owns, got, w, m, v)
    return outs[0], list(outs[1:])


def _adamw_halves(ws, g, ms, vs, half, prev, name, deps=()):
    n = len(ws)
    _, _, R, C = g.shape
    tr = _row_tile(R, ADAMW_ROWS)
    steps = R // tr
    carried = [] if prev is None else [a for four in prev for a in four]
    both = half is None
    which = (lambda i, s: i // steps) if both else (lambda i, s: s[0])
    half = 0 if both else half

    def body(s_ref, *refs):
        w_refs, g_refs, m_refs, v_refs = (refs[k * n:(k + 1) * n] for k in range(4))
        outs = refs[len(refs) - 4 * n:]
        for a in range(n):
            grad = g_refs[a][0, 0]
            d, mn, vn = _adamw_math(w_refs[a][...], grad, m_refs[a][...], v_refs[a][...])
            for o, val in zip(outs[4 * a:4 * a + 4], (grad, d, mn, vn)):
                o[...] = val

    rows = pl.BlockSpec((tr, C), lambda i, s: (which(i, s) * steps + i % steps, 0))
    grad_spec = lambda a: pl.BlockSpec((1, 1, tr, C), lambda i, s: (which(i, s), a, i % steps, 0))
    n_in = 4 * n
    outs = pl.pallas_call(
        body,
        name=name,
        grid_spec=pltpu.PrefetchScalarGridSpec(
            num_scalar_prefetch=1,
            grid=(2 * steps if both else steps,),
            in_specs=[rows] * n + [grad_spec(a) for a in range(n)] + [rows] * (2 * n)
            + [ANY_SPEC] * (len(carried) + len(deps)),
            out_specs=[rows] * (4 * n),
        ),
        out_shape=[jax.ShapeDtypeStruct((2 * R, C), F32)] * (4 * n),
        input_output_aliases={1 + n_in + k: k for k in range(len(carried))},
        compiler_params=_params(("parallel",)),
    )(jnp.reshape(half, (1,)).astype(jnp.int32), *ws, *([g] * n), *ms, *vs, *carried, *deps)
    return [outs[4 * a:4 * a + 4] for a in range(n)]


def _adamw_small(ws, gs, ms, vs, name):
    n = len(ws)

    def body(*refs):
        for a in range(n):
            d, mn, vn = _adamw_math(refs[a][...], refs[n + a][...], refs[2 * n + a][...], refs[3 * n + a][...])
            refs[4 * n + a][...] = d
            refs[5 * n + a][...] = mn
            refs[6 * n + a][...] = vn

    shapes = [jax.ShapeDtypeStruct(w.shape, F32) for w in ws]
    outs = pl.pallas_call(
        body,
        name=name,
        out_shape=shapes * 3,
        compiler_params=_params(),
    )(*ws, *gs, *ms, *vs)
    return outs[:n], outs[n:2 * n], outs[2 * n:]


def _to_blockdiag(w):
    per = CW // LRU_BW
    w4 = w.reshape(N_CT, per, LRU_BW, LRU_BW)
    eye = jnp.eye(per, dtype=w.dtype)
    return (w4[:, :, :, None, :] * eye[None, :, None, :, None]).reshape(N_CT, CW, CW)


def _blocks_from_lanes(g):
    side = LANES // LRU_BW
    g5 = g.reshape(N_CT, CW // LANES, LRU_BW, side, LRU_BW)
    return jnp.transpose(g5, (0, 1, 3, 2, 4)).reshape(LRU_BLOCKS, LRU_BW, LRU_BW)


def _local_grads(x2d, tgt2d, B, S, g_in, in_proj, conv_b, gate_x_w, gate_x_b, gate_a_w, gate_a_b, lam,
                 proj_weights, g_fin, reduce):
    wx_bd = _c(_to_blockdiag(gate_x_w))
    wa_bd = _c(_to_blockdiag(gate_a_w))
    tables = _retention_tables(S)

    proj, ht, w_all, conv_w, gain = in_proj(x2d, g_in, (*tables, wx_bd, wa_bd))
    gain3 = gain.reshape(HEADS, 1, DK)
    hlru, ya = _lru_fwd(proj, conv_w, conv_b, wx_bd, wa_bd, gate_x_b, gate_a_b, lam, B, S)
    o_pre, yb, states = _ret_fwd(proj, tables, gain3, B, S)
    wpa, wpb, wout = proj_weights(yb)
    loss, dx2, dya, dyb, dm, dgf, gw_proj = _mid(ya, yb, proj, x2d, tgt2d, wpa, wpb, wout, g_fin)
    g3 = _inproj_bwd_dw(ht, [dm], "inproj_bwd_dw_m")
    deps = reduce.m_ready(gw_proj, g3)
    dr, dgain = _ret_bwd(dyb, o_pre, proj, states, tables, gain3, B, S, deps)
    deps = reduce.ret_done(dr)
    g12 = _inproj_bwd_dw(ht, [dr], "inproj_bwd_dw_r", deps)
    deps = reduce.r_ready(g12)
    dxa, dga, dcw, dcb, dwx, dwa, dbx, dba, dlam = _lru_bwd(
        dya, proj, hlru, conv_w, conv_b, wx_bd, wa_bd, gate_x_b, gate_a_b, lam, B, S, deps)
    small = dict(conv_w=dcw, conv_b=dcb, gate_x_w=dwx, gate_x_b=dbx, gate_a_w=dwa, gate_a_b=dba, lru_lambda=dlam,
                 gn_gain=dgain.reshape(HEADS, DK), norm_final=dgf)
    deps = reduce.lru_done(dxa, _pack_small(small, loss, reduce.slot()))
    g0 = _inproj_bwd_dw(ht, [dxa, dga], "inproj_bwd_dw_a", deps)
    deps = reduce.a_ready(g0)
    n_tiles = x2d.shape[0] // min(DX_TILE, x2d.shape[0])
    grad_x, dgin = _inproj_bwd_dx([dxa, dga, dr, dm], w_all, x2d, dx2, g_in, 0, n_tiles, None, "inproj_bwd_dx", deps)
    return grad_x, dgin


ALL_CHIPS = (0, 1, 2, 3)


class _GradReduce:
    def __init__(self, proj_done):
        self.pending = {}
        self.proj_done = proj_done
        self.land_in = None

    def _start(self, key, parts, name):
        bufs, plans, shared = [], [], None
        for part_bufs, plan, n_copies, part_shared in parts:
            if part_shared is not None:
                shared = len(bufs) + part_shared
            plans.append((plan, len(part_bufs), n_copies))
            bufs += part_bufs
        plan = _join_plans(plans)
        send_sems, recv_sems, bufs, token = _copies_start(bufs, plan, sum(p[2] for p in plans), name + "_start")
        if shared is not None:
            self.land_in = bufs[shared]
        self.pending[key] = (send_sems, recv_sems, bufs, plan, name + "_wait", shared)
        return (token,)

    def _finish(self, key, after):
        send_sems, recv_sems, bufs, plan, name, shared = self.pending.pop(key)
        if shared is not None:
            bufs[shared] = self.land_in
        bufs = _copies_wait(send_sems, recv_sems, bufs, after, plan, name)
        if shared is not None:
            self.land_in = bufs[shared]
        return bufs

    @staticmethod
    def _swap(pieces):
        bufs = []
        for g in pieces:
            bufs += [g, lax.empty((g.shape[0],) + g.shape[2:], F32)]
        n_slabs = [g.shape[0] for g in pieces]
        return bufs, _swap_plan(n_slabs), sum(n_slabs), None

    def _scatter(self, sums, dest_sets):
        bufs = []
        for cs in sums:
            bufs += [cs, lax.empty((3,) + cs.shape[1:], cs.dtype)]
        if self.land_in is not None:
            bufs[-1] = self.land_in
        return bufs, _scatter_plan(dest_sets), 3 * len(sums), len(bufs) - 1

    @staticmethod
    def slot():
        x, y, c = _coords()
        return 4 * x + 2 * y + c

    def _gather8(self, block):
        land = lax.dynamic_update_slice(lax.empty((8,) + block.shape, F32), block[None], (self.slot(), 0, 0))
        return [land], _allgather_plan(), 7, None

    def m_ready(self, gw_proj, g3):
        rows = gw_proj.shape[2] * gw_proj.shape[3]
        return self._start("m", [self._swap([gw_proj.reshape(N_CHIPS, 2, rows, D_MODEL), g3])], "swap_m")

    def ret_done(self, after):
        proj, land_p, g3, land_3 = self._finish("m", after)
        sums_m = _add_my_halves([(proj, land_p), (g3, land_3)], N_CHIPS, "chip_sum_m")
        return self._start("sm", [self._scatter(sums_m, [ALL_CHIPS, (3,)])], "scatter_m")

    def r_ready(self, g12):
        return self._start("r", [self._swap([g12])], "swap_r")

    def lru_done(self, after, packed):
        g12, land_12 = self._finish("r", after)
        sums_r = [_add_my_half(g12, land_12, "chip_sum_r")]
        return (self._start("sr", [self._scatter(sums_r, [(1, 2)])], "scatter_r")
                + self._start("small", [([packed], _allgather_plan(), 7, None)], "gather_small"))

    def a_ready(self, g0):
        (token,) = self._start("a", [self._swap([g0])], "swap_a")
        csp, gotp, self.cs3, _ = self._finish("sm", token)
        half_proj = _sum_slabs(csp, gotp, "sum_w_proj")
        g0, land_0 = self._finish("a", half_proj)
        join = ([half_proj], _join_plan(half_proj.shape[1], PROJ_JOIN_PIECES), PROJ_JOIN_PIECES, None)
        return self._start("sa", [self._scatter([_add_my_half(g0, land_0, "chip_sum_a")], [(0,)]), join], "scatter_a")

    def finish(self, dgin, w_in_own, w_in_done):
        (token,) = self._start("n", [self._gather8(dgin)], "gather_norm_in")
        (small,) = self._finish("small", token)
        cs12, _ = self._finish("sr", token)
        cs0, _, g_proj = self._finish("sa", token)
        half_in, first = w_in_own([self.cs3, cs12, cs0], self.land_in, [(3,), (1, 2), (0,)])
        deps = self._start("j", [([half_in], _join_plan(half_in.shape[1], JOIN_PIECES), JOIN_PIECES, None)], "join_w_in")
        (g_in,) = self._finish("j", self.proj_done(g_proj, deps))
        done = w_in_done(g_in, first)
        (norm_in,) = self._finish("n", done[1])
        return _unpack_small(small), _sum_gathered(norm_in, [(1, D_MODEL)], "sum_norm_in_grad")[0]


_SMALL = ("gate_x_w", "gate_a_w", "conv_w", "conv_b", "gate_x_b", "gate_a_b", "lru_lambda", "gn_gain", "norm_final")
_SMALL_SHAPES = dict(gate_x_w=(LRU_BLOCKS, LRU_BW, LRU_BW), gate_a_w=(LRU_BLOCKS, LRU_BW, LRU_BW),
                     norm_in=(1, D_MODEL), conv_w=(CONV, D_MODEL), conv_b=(1, D_MODEL), gate_x_b=(1, D_MODEL),
                     gate_a_b=(1, D_MODEL), lru_lambda=(1, D_MODEL), gn_gain=(HEADS, DK), norm_final=(1, D_MODEL))


def _pack_small(small, loss, slot):
    parts = [small[k] if small[k].ndim == 2 else small[k].reshape(-1, LANES) for k in _SMALL]
    m = sum(p.size for p in parts) // LANES + SUBLANES

    def body(s_ref, *refs):
        o_ref = refs[-1]
        r = 0
        for ref, part in zip(refs, parts):
            if part.shape[1] == LANES:
                o_ref[0, r:r + part.shape[0], :] = ref[...]
                r += part.shape[0]
                continue
            for k in range(part.shape[0]):
                for q in range(part.shape[1] // LANES):
                    o_ref[0, r:r + 1, :] = ref[k:k + 1, q * LANES:(q + 1) * LANES]
                    r += 1
        o_ref[0, r:r + SUBLANES, :] = jnp.broadcast_to(refs[len(parts)][...], (SUBLANES, LANES))

    return pl.pallas_call(
        body,
        name="pack_small_grads",
        grid_spec=pltpu.PrefetchScalarGridSpec(
            num_scalar_prefetch=1,
            grid=(1,),
            in_specs=[pl.BlockSpec(p.shape, lambda i, s: (0, 0)) for p in parts] + [pl.BlockSpec((1, 1), lambda i, s: (0, 0))],
            out_specs=pl.BlockSpec((1, m, LANES), lambda i, s: (s[0], 0, 0)),
        ),
        out_shape=jax.ShapeDtypeStruct((8, m, LANES), F32),
        compiler_params=_params(("arbitrary",)),
    )(jnp.reshape(slot, (1,)).astype(jnp.int32), *parts, loss)


def _unpack_small(land):
    gates = ("gate_x_w", "gate_a_w")
    packed_shape = lambda k: (LRU_BLOCKS * LRU_BW * LRU_BW // LANES, LANES) if k in gates else _SMALL_SHAPES[k]
    *sums, loss = _sum_gathered(land, [packed_shape(k) for k in _SMALL] + [(1, 1)], "sum_small_grads")
    return {k: _blocks_from_lanes(g) if k in gates else g for k, g in zip(_SMALL, sums)}, loss


def kernel(x, norm_in, w_in, conv_w, conv_b, gate_x_w, gate_x_b, gate_a_w, gate_a_b, lru_lambda, gn_gain, w_proj_a, w_proj_b, w_out, norm_final, loss_target, m_norm_in, m_w_in, m_conv_w, m_conv_b, m_gate_x_w, m_gate_x_b, m_gate_a_w, m_gate_a_b, m_lru_lambda, m_gn_gain, m_w_proj_a, m_w_proj_b, m_w_out, m_norm_final, v_norm_in, v_w_in, v_conv_w, v_conv_b, v_gate_x_w, v_gate_x_b, v_gate_a_w, v_gate_a_b, v_lru_lambda, v_gn_gain, v_w_proj_a, v_w_proj_b, v_w_out, v_norm_final):
    B, S, _ = x.shape
    T = B * S
    xi, yi, ci = _coords()
    chip = 2 * xi + yi

    cshard = D_MODEL // N_CHIPS
    mine = _cast_into_slot([w_in[0].reshape(2, D_MODEL // 2, 2 * D_MODEL)], "cast_w_in")
    plan = _gather_plan(3)
    pending_proj = []
    gshard = DK // N_CHIPS
    tiny = jnp.concatenate([conv_w[0], jnp.zeros((4, cshard), F32), jnp.pad(gn_gain[0], ((0, 4), (0, cshard - gshard)))],
                           axis=0).reshape(1, 2, SUBLANES, cshard)
    tiny_buf = lax.dynamic_update_slice(lax.empty((N_CHIPS, 2, SUBLANES, cshard), F32), tiny, (chip, 0, 0, 0))
    near_plan, pass_plan, far_plan = (_chip_gather_plan(stage, 2) for stage in ("near", "pass", "far"))
    (n_near, _), (n_pass, passed_on), (n_far, _) = (_chip_gather_copies(stage, 2) for stage in ("near", "pass", "far"))
    halves = set(range(n_pass)) - passed_on
    near_s, near_r, bufs, near_token = _copies_start([mine[0], tiny_buf], near_plan, n_near, "gather_near_start")

    def in_proj(x2d, g_in, meanwhile):
        as_w = lambda b: b[0].reshape(N_CHIPS, D_MODEL, 2 * D_MODEL)
        slot_x, slot_y, slot_d = 2 * (1 - xi) + yi, 2 * xi + (1 - yi), 2 * (1 - xi) + (1 - yi)
        ids = lambda *chips: jnp.stack(chips).astype(jnp.int32)
        proj, hb, ht = _inproj_first(x2d, g_in, as_w(bufs), ids(chip), "inproj_own", (near_token, *meanwhile))
        got = _copies_wait(near_s, near_r, bufs, proj, near_plan, "gather_near_wait")
        pass_s, pass_r, got, pass_token = _copies_start(got, pass_plan, n_pass, "gather_pass_start")
        mine_proj = _cast_into_slot([w[0].reshape(2, cshard // 2, D_MODEL) for w in (w_proj_a, w_proj_b, w_out)],
                                    "cast_w_proj", (pass_token,))
        got = _copies_wait(pass_s, pass_r, got, mine_proj[0], pass_plan, "gather_pass_wait_halves", only=halves)
        proj = _inproj_more(hb, as_w(got), ids(slot_x, slot_y), proj, "inproj_near")
        got = _copies_wait(pass_s, pass_r, got, proj, pass_plan, "gather_pass_wait_far", only=passed_on)
        far_s, far_r, got, far_token = _copies_start(got, far_plan, n_far, "gather_far_start")
        pending_proj.append(_copies_start(mine_proj, plan, 9, "gather_proj_start", (far_token,)))
        got = _copies_wait(far_s, far_r, got, pending_proj[0][3], far_plan, "gather_far_wait")
        proj = _inproj_more(hb, as_w(got), ids(slot_d), proj, "inproj_far")
        tiny_all = got[1].reshape(N_CHIPS, 2 * SUBLANES, cshard)
        conv_w_full = jnp.transpose(tiny_all[:, 0:CONV, :], (1, 0, 2)).reshape(CONV, D_MODEL)
        gain_full = jnp.transpose(tiny_all[:, 8:8 + HEADS, :gshard], (1, 0, 2)).reshape(HEADS, DK)
        return proj, ht, as_w(got), conv_w_full, gain_full

    def proj_weights(after):
        s_sems, r_sems, pbufs, _ = pending_proj[0]
        got = _copies_wait(s_sems, r_sems, pbufs, after, plan, "gather_proj_wait")
        return [b.reshape(D_MODEL, D_MODEL) for b in got]

    weights = dict(norm_in=norm_in, w_in=w_in, conv_w=conv_w, conv_b=conv_b, gate_x_w=gate_x_w, gate_x_b=gate_x_b,
                   gate_a_w=gate_a_w, gate_a_b=gate_a_b, lru_lambda=lru_lambda, gn_gain=gn_gain, w_proj_a=w_proj_a,
                   w_proj_b=w_proj_b, w_out=w_out, norm_final=norm_final)
    ms = dict(norm_in=m_norm_in, w_in=m_w_in, conv_w=m_conv_w, conv_b=m_conv_b, gate_x_w=m_gate_x_w,
              gate_x_b=m_gate_x_b, gate_a_w=m_gate_a_w, gate_a_b=m_gate_a_b, lru_lambda=m_lru_lambda, gn_gain=m_gn_gain,
              w_proj_a=m_w_proj_a, w_proj_b=m_w_proj_b, w_out=m_w_out, norm_final=m_norm_final)
    vs = dict(norm_in=v_norm_in, w_in=v_w_in, conv_w=v_conv_w, conv_b=v_conv_b, gate_x_w=v_gate_x_w,
              gate_x_b=v_gate_x_b, gate_a_w=v_gate_a_w, gate_a_b=v_gate_a_b, lru_lambda=v_lru_lambda, gn_gain=v_gn_gain,
              w_proj_a=v_w_proj_a, w_proj_b=v_w_proj_b, w_out=v_w_out, norm_final=v_norm_final)
    names = list(weights)
    grads, delta, new_m, new_v = {}, {}, {}, {}

    def update_big(keys, g, half, prev, name, deps=()):
        two = lambda a: a.reshape(a.shape[1], a.shape[2])
        res = _adamw_halves([two(weights[k]) for k in keys], g, [two(ms[k]) for k in keys], [two(vs[k]) for k in keys],
                            half, prev, name, deps)
        for k, (gk, d, mn, vn) in zip(keys, res):
            shp = weights[k].shape
            grads[k], delta[k], new_m[k], new_v[k] = gk.reshape(shp), d.reshape(shp), mn.reshape(shp), vn.reshape(shp)
        return res

    def proj_done(g_proj, deps):
        g4 = g_proj.reshape(2, 3, D_MODEL // (2 * N_CHIPS), D_MODEL)
        return update_big(("w_proj_a", "w_proj_b", "w_out"), g4, None, None, "adamw_proj", deps)[-1][1]

    def w_in_own(owns, got, dest_sets):
        two = lambda a: a.reshape(a.shape[1], a.shape[2])
        return _sum_adamw_own(owns, got, dest_sets, two(w_in), two(m_w_in), two(v_w_in), "adamw_w_in_own")

    def w_in_done(g_in, prev):
        g4 = g_in.reshape(2, 1, D_MODEL // 2, 2 * D_MODEL)
        return update_big(("w_in",), g4, 1 - ci, [prev], "adamw_w_in_other")[0]

    reduce = _GradReduce(proj_done)
    grad_x, dgin = _local_grads(
        x.reshape(T, D_MODEL), loss_target.reshape(T, D_MODEL), B, S, norm_in, in_proj, conv_b,
        gate_x_w[0], gate_x_b, gate_a_w[0], gate_a_b, lru_lambda, proj_weights,
        norm_final.reshape(1, D_MODEL), reduce)

    (gsm, loss), g_norm_in = reduce.finish(dgin.reshape(SUBLANES, LANES), w_in_own, w_in_done)
    loss = loss[0, 0]
    gsm["norm_in"] = g_norm_in
    gsm["conv_w"] = lax.dynamic_slice_in_dim(gsm["conv_w"], chip * cshard, cshard, axis=1)
    gsm["gn_gain"] = lax.dynamic_slice_in_dim(gsm["gn_gain"], chip * gshard, gshard, axis=1)
    smalls = [k for k in names if k not in delta]

    def view(a):
        return a.reshape(1, -1) if a.ndim == 1 else (a.reshape(a.shape[1:]) if a.ndim > 2 else a)

    ds, mns, vns = _adamw_small([view(weights[k]) for k in smalls], [gsm[k].reshape(view(weights[k]).shape) for k in smalls],
                                [view(ms[k]) for k in smalls], [view(vs[k]) for k in smalls], "adamw_small")
    for k, d, mn, vn in zip(smalls, ds, mns, vns):
        shp = weights[k].shape
        grads[k], delta[k], new_m[k], new_v[k] = gsm[k].reshape(shp), d.reshape(shp), mn.reshape(shp), vn.reshape(shp)

    return (loss, grad_x.reshape(B, S, D_MODEL), *[grads[k] for k in names], *[delta[k] for k in names],
            *[new_m[k] for k in names], *[new_v[k] for k in names])
```

```python
import jax
import jax.numpy as jnp
from jax import lax
from jax.experimental import pallas as pl
from jax.experimental.pallas import tpu as pltpu

F32 = jnp.float32
_MXU = jnp.bfloat16

D_MODEL = 1024
N_GROUPS = 8
HEADS = 4
DK = 256
CHUNK = 128
CONV = 4
LRU_BLOCKS = 16
LRU_BW = 64
LRU_C = 8.0
ROPE_THETA = 10000.0
EPS = 1e-6
CW = 256
N_CT = D_MODEL // CW
N_CHIPS = 4
MESH = pl.DeviceIdType.MESH

ADAM_LR = 0.001
ADAM_B1 = 0.9
ADAM_B2 = 0.999
ADAM_EPS = 1e-08
ADAM_WD = 0.01
ADAM_STEP = 10

VMEM_LIMIT = 56 * 1024 * 1024

FIRST_PROJ_TILE = 1024
MORE_PROJ_TILE = 2048
SCAN_TILE = 1024
MID_TILE = 256
DX_TILE = 512
DW_COLS = 512
DW_LOADS = 4
RET_CHUNKS = 2
SUM_ROWS = 256
ADAMW_ROWS = 256
JOIN_PIECES = 1
PROJ_JOIN_PIECES = 4


def _c(v):
    return v.astype(_MXU)


def _dot(a, b):
    return lax.dot_general(a, b, (((1,), (0,)), ((), ())), preferred_element_type=F32)


def _dot_nt(a, b):
    return lax.dot_general(a, b, (((1,), (1,)), ((), ())), preferred_element_type=F32)


def _dot_tn(a, b):
    return lax.dot_general(a, b, (((0,), (0,)), ((), ())), preferred_element_type=F32)


def _sigmoid(z):
    return 0.5 * jnp.tanh(0.5 * z) + 0.5


ANY_SPEC = pl.BlockSpec(memory_space=pl.ANY)


def _after(body, n_in, deps):
    n_deps = len(deps)

    def wrapped(*refs):
        return body(*refs[:n_in], *refs[n_in + n_deps:])

    return wrapped


def _params(sem=None):
    if sem is None:
        return pltpu.CompilerParams(vmem_limit_bytes=VMEM_LIMIT)
    return pltpu.CompilerParams(vmem_limit_bytes=VMEM_LIMIT, dimension_semantics=sem)


def _inproj_first(x2d, g_in, w_all, chips, name, deps=()):
    T = x2d.shape[0]
    tm = min(FIRST_PROJ_TILE, T)
    n_i = T // tm

    def body(s_ref, *refs):
        x_ref, g_ref, w_ref = refs[:3]
        proj_ref, hb_ref, ht_ref, h_all = refs[-4:]
        i = pl.program_id(1)
        rows = pl.ds(pl.multiple_of(i * tm, tm), tm)

        @pl.when(pl.program_id(0) == 0)
        def _():
            x = x_ref[...]
            r = lax.rsqrt(jnp.mean(x * x, axis=-1, keepdims=True) + EPS)
            h = x * r * g_ref[...]
            hb = h.astype(h_all.dtype)
            h_all[rows, :] = hb
            hb_ref[...] = hb
            ht_ref[...] = h.T.astype(ht_ref.dtype)

        proj_ref[...] = _dot(h_all[rows, :], w_ref[0])

    first = lambda j, i: jnp.where(j == 0, i, n_i - 1)
    return pl.pallas_call(
        body,
        name=name,
        grid_spec=pltpu.PrefetchScalarGridSpec(
            num_scalar_prefetch=1,
            grid=(2 * chips.shape[0], n_i),
            in_specs=[
                pl.BlockSpec((tm, D_MODEL), lambda j, i, s: (first(j, i), 0)),
                pl.BlockSpec((1, D_MODEL), lambda j, i, s: (0, 0)),
                pl.BlockSpec((1, D_MODEL, D_MODEL), lambda j, i, s: (s[j // 2], 0, j % 2)),
            ] + [ANY_SPEC] * len(deps),
            out_specs=[
                pl.BlockSpec((tm, D_MODEL), lambda j, i, s: (i, 2 * s[j // 2] + j % 2)),
                pl.BlockSpec((tm, D_MODEL), lambda j, i, s: (first(j, i), 0)),
                pl.BlockSpec((D_MODEL, tm), lambda j, i, s: (0, first(j, i))),
            ],
            scratch_shapes=[pltpu.VMEM((T, D_MODEL), _MXU)],
        ),
        out_shape=[
            jax.ShapeDtypeStruct((T, N_GROUPS * D_MODEL), F32),
            jax.ShapeDtypeStruct((T, D_MODEL), _MXU),
            jax.ShapeDtypeStruct((D_MODEL, T), _MXU),
        ],
        compiler_params=_params(("arbitrary", "arbitrary")),
    )(chips, x2d, g_in, w_all, *deps)


def _inproj_more(hb, w_all, chips, proj, name):
    T = hb.shape[0]
    tm = min(MORE_PROJ_TILE, T)

    def body(s_ref, hb_hbm, w_ref, prev_ref, proj_ref, h_all, sem):
        @pl.when((pl.program_id(0) == 0) & (pl.program_id(1) == 0))
        def _():
            cp = pltpu.make_async_copy(hb_hbm, h_all, sem)
            cp.start()
            cp.wait()

        rows = pl.ds(pl.multiple_of(pl.program_id(1) * tm, tm), tm)
        proj_ref[...] = _dot(h_all[rows, :], w_ref[0])

    return pl.pallas_call(
        body,
        name=name,
        grid_spec=pltpu.PrefetchScalarGridSpec(
            num_scalar_prefetch=1,
            grid=(2 * chips.shape[0], T // tm),
            in_specs=[
                ANY_SPEC,
                pl.BlockSpec((1, D_MODEL, D_MODEL), lambda j, i, s: (s[j // 2], 0, j % 2)),
                ANY_SPEC,
            ],
            out_specs=pl.BlockSpec((tm, D_MODEL), lambda j, i, s: (i, 2 * s[j // 2] + j % 2)),
            scratch_shapes=[pltpu.VMEM((T, D_MODEL), hb.dtype), pltpu.SemaphoreType.DMA],
        ),
        out_shape=jax.ShapeDtypeStruct(proj.shape, F32),
        input_output_aliases={3: 0},
        compiler_params=_params(("arbitrary", "arbitrary")),
    )(chips, hb, w_all, proj)


def _scan_fwd(a, u):
    n = a.shape[0]
    row = lax.broadcasted_iota(jnp.int32, a.shape, 0)
    s = 1
    while s < n:
        m = row >= s
        u = u + a * jnp.where(m, pltpu.roll(u, s, 0), 0.0)
        a = a * jnp.where(m, pltpu.roll(a, s, 0), 1.0)
        s *= 2
    return a, u


def _scan_bwd(b, g):
    n = b.shape[0]
    row = lax.broadcasted_iota(jnp.int32, b.shape, 0)
    s = 1
    while s < n:
        m = row < n - s
        g = g + b * jnp.where(m, pltpu.roll(g, n - s, 0), 0.0)
        b = b * jnp.where(m, pltpu.roll(b, n - s, 0), 1.0)
        s *= 2
    return b, g


LANES = 128
SUBLANES = 8


def _scan_scratch(tc):
    by_lanes = pltpu.VMEM((CW // LANES, tc, LANES), F32)
    return [by_lanes, by_lanes, pltpu.VMEM((tc // SUBLANES, CW), F32), pltpu.VMEM((tc, CW), F32)]


def _scan_tile(a, u, edge, la_ref, lh_ref, c_ref, dst_ref, reverse):
    n, w = a.shape
    groups = n // SUBLANES
    a3 = a.reshape(groups, SUBLANES, w)
    u3 = u.reshape(groups, SUBLANES, w)
    row = lax.broadcasted_iota(jnp.int32, a3.shape, 1)
    for s in (1, 2, 4):
        m = (row < SUBLANES - s) if reverse else (row >= s)
        shift = SUBLANES - s if reverse else s
        u3 = u3 + a3 * jnp.where(m, pltpu.roll(u3, shift, 1), 0.0)
        a3 = a3 * jnp.where(m, pltpu.roll(a3, shift, 1), 1.0)
    al = a3.reshape(n, w)
    hl = u3.reshape(n, w)
    blocks = w // LANES
    for q in range(blocks):
        la_ref[q] = al[:, q * LANES:(q + 1) * LANES]
        lh_ref[q] = hl[:, q * LANES:(q + 1) * LANES]
    ends = pl.ds(0 if reverse else SUBLANES - 1, groups, stride=SUBLANES)
    end_a = jnp.concatenate([la_ref.at[q][ends, :] for q in range(blocks)], axis=-1)
    end_h = jnp.concatenate([lh_ref.at[q][ends, :] for q in range(blocks)], axis=-1)
    prod, part = (_scan_bwd if reverse else _scan_fwd)(end_a, end_h)
    total = part + prod * edge
    g_row = lax.broadcasted_iota(jnp.int32, total.shape, 0)
    if reverse:
        c_ref[...] = jnp.where(g_row == groups - 1, edge, pltpu.roll(total, groups - 1, 0))
    else:
        c_ref[...] = jnp.where(g_row == 0, edge, pltpu.roll(total, 1, 0))
    for g in range(groups):
        rows = slice(g * SUBLANES, (g + 1) * SUBLANES)
        for q in range(blocks):
            cols = slice(q * LANES, (q + 1) * LANES)
            dst_ref[rows, cols] = lh_ref[q, rows, :] + la_ref[q, rows, :] * c_ref[g:g + 1, cols]


def _softplus_neg(lam):
    z = -lam
    return jnp.maximum(z, 0.0) + jnp.log1p(jnp.exp(-jnp.abs(z)))


def _lru_gates(xc, wx_ref, wa_ref, bx_ref, ba_ref, lam_ref):
    xcb = _c(xc)
    i_t = _sigmoid(_dot(xcb, wx_ref[0]) + bx_ref[...])
    r_t = _sigmoid(_dot(xcb, wa_ref[0]) + ba_ref[...])
    sp = _softplus_neg(lam_ref[...])
    log_a = (-LRU_C) * r_t * sp
    a = jnp.exp(log_a)
    mult = jnp.sqrt(1.0 - a * a)
    return xcb, i_t, r_t, sp, a, mult


def _conv_from_ext(ext_ref, xa, cw_ref, cb_ref, tc):
    return (cb_ref[...] + cw_ref[3:4, :] * xa + cw_ref[2:3, :] * ext_ref[7:7 + tc, :]
            + cw_ref[1:2, :] * ext_ref[6:6 + tc, :] + cw_ref[0:1, :] * ext_ref[5:5 + tc, :])


def _lru_fwd(proj, conv_w, conv_b, wx_bd, wa_bd, bx, ba, lam, B, S):
    T = B * S
    tc = min(SCAN_TILE, S)
    nt = S // tc
    h8 = tc // 8

    def body(xa_ref, halo_ref, ga_ref, cw_ref, cb_ref, wx_ref, wa_ref, bx_ref, ba_ref, lam_ref,
             h_ref, ya_ref, ext_ref, carry_ref, la_ref, lh_ref, c_ref):
        t = pl.program_id(2)

        @pl.when(t == 0)
        def _():
            carry_ref[...] = jnp.zeros_like(carry_ref)

        xa = xa_ref[...]
        ext_ref[0:8, :] = jnp.where(t == 0, 0.0, halo_ref[...])
        ext_ref[8:8 + tc, :] = xa
        xc = _conv_from_ext(ext_ref, xa, cw_ref, cb_ref, tc)
        _, i_t, _, _, a, mult = _lru_gates(xc, wx_ref, wa_ref, bx_ref, ba_ref, lam_ref)
        u = mult * (i_t * xc)
        _scan_tile(a, u, carry_ref[7:8, :], la_ref, lh_ref, c_ref, h_ref, False)
        h = h_ref[...]
        carry_ref[...] = h[tc - 8:tc, :]
        ga = ga_ref[...]
        ya_ref[...] = (ga * _sigmoid(ga) * h).astype(ya_ref.dtype)

    row = lambda b, t: b * nt + t
    vec = pl.BlockSpec((1, CW), lambda b, c, t: (0, c))
    mat = pl.BlockSpec((1, CW, CW), lambda b, c, t: (c, 0, 0))
    return pl.pallas_call(
        body,
        name="lru_fwd",
        grid=(B, N_CT, nt),
        in_specs=[
            pl.BlockSpec((tc, CW), lambda b, c, t: (row(b, t), c)),
            pl.BlockSpec((8, CW), lambda b, c, t: (jnp.maximum(row(b, t) * h8 - 1, 0), c)),
            pl.BlockSpec((tc, CW), lambda b, c, t: (row(b, t), N_CT + c)),
            pl.BlockSpec((CONV, CW), lambda b, c, t: (0, c)),
            vec, mat, mat, vec, vec, vec,
        ],
        out_specs=[
            pl.BlockSpec((tc, CW), lambda b, c, t: (row(b, t), c)),
            pl.BlockSpec((tc, CW), lambda b, c, t: (row(b, t), c)),
        ],
        out_shape=[
            jax.ShapeDtypeStruct((T, D_MODEL), F32),
            jax.ShapeDtypeStruct((T, D_MODEL), _MXU),
        ],
        scratch_shapes=[pltpu.VMEM((tc + 8, CW), F32), pltpu.VMEM((8, CW), F32)] + _scan_scratch(tc)[:3],
        compiler_params=_params(("parallel", "parallel", "arbitrary")),
    )(proj, proj, proj, conv_w, conv_b, wx_bd, wa_bd, bx, ba, lam)


def _lru_bwd(dya, proj, hlru, conv_w, conv_b, wx_bd, wa_bd, bx, ba, lam, B, S, deps=()):
    T = B * S
    tc = min(SCAN_TILE, S)
    nt = S // tc
    h8 = tc // 8

    def body(dya_ref, xa_ref, xhalo_ref, ga_ref, h_ref, hhalo_ref, cw_ref, cb_ref, wx_ref, wa_ref, bx_ref, ba_ref,
             lam_ref, dxa_ref, dga_ref, dcw_ref, dcb_ref, dwx_ref, dwa_ref, dbx_ref, dba_ref, dlam_ref,
             ext_ref, ext2_ref, carry_ref, dhalo_ref, la_ref, lh_ref, c_ref, dh_ref, accx_ref, acca_ref):
        b = pl.program_id(1)
        t = pl.program_id(2)
        tt = nt - 1 - t

        @pl.when(t == 0)
        def _():
            carry_ref[...] = jnp.zeros_like(carry_ref)
            dhalo_ref[...] = jnp.zeros_like(dhalo_ref)

        @pl.when((t == 0) & (b == 0))
        def _():
            for r in (dcw_ref, dcb_ref, accx_ref, acca_ref, dbx_ref, dba_ref, dlam_ref):
                r[...] = jnp.zeros_like(r)

        xa = xa_ref[...]
        ext_ref[0:8, :] = jnp.where(tt == 0, 0.0, xhalo_ref[...])
        ext_ref[8:8 + tc, :] = xa
        xc = _conv_from_ext(ext_ref, xa, cw_ref, cb_ref, tc)
        xcb, i_t, r_t, sp, a, mult = _lru_gates(xc, wx_ref, wa_ref, bx_ref, ba_ref, lam_ref)

        h = h_ref[...]
        ga = ga_ref[...]
        dya_t = dya_ref[...]
        sg = _sigmoid(ga)
        dga_ref[...] = (dya_t * h * (sg * (1.0 + ga * (1.0 - sg)))).astype(dga_ref.dtype)
        dlru = dya_t * (ga * sg)

        row = lax.broadcasted_iota(jnp.int32, a.shape, 0)
        coef = jnp.where(row == tc - 1, 1.0, pltpu.roll(a, tc - 1, 0))
        _scan_tile(coef, dlru, carry_ref[0:1, :], la_ref, lh_ref, c_ref, dh_ref, True)
        dh = dh_ref[...]
        ext2_ref[0:tc, :] = a * dh
        carry_ref[...] = ext2_ref[0:8, :]

        ext2_ref[0:8, :] = jnp.where(tt == 0, 0.0, hhalo_ref[...])
        ext2_ref[8:8 + tc, :] = h
        hprev = ext2_ref[7:7 + tc, :]

        da = dh * hprev
        ix = i_t * xc
        dmult = dh * ix
        di = dh * mult * xc
        dxc = dh * mult * i_t
        dlog_a = da * a - dmult * (a * a) / mult
        dr = dlog_a * ((-LRU_C) * sp)
        dlam_ref[...] += jnp.sum(dlog_a * r_t, axis=0, keepdims=True) * (LRU_C * _sigmoid(-lam_ref[...]))
        dza = dr * r_t * (1.0 - r_t)
        dzx = di * i_t * (1.0 - i_t)
        dzab = _c(dza)
        dzxb = _c(dzx)
        dxc = dxc + _dot_nt(dzxb, wx_ref[0]) + _dot_nt(dzab, wa_ref[0])
        accx_ref[...] += _dot_tn(xcb, dzxb)
        acca_ref[...] += _dot_tn(xcb, dzab)
        dbx_ref[...] += jnp.sum(dzx, axis=0, keepdims=True)
        dba_ref[...] += jnp.sum(dza, axis=0, keepdims=True)

        dcb_ref[...] += jnp.sum(dxc, axis=0, keepdims=True)
        dcw_ref[3:4, :] += jnp.sum(dxc * xa, axis=0, keepdims=True)
        dcw_ref[2:3, :] += jnp.sum(dxc * ext_ref[7:7 + tc, :], axis=0, keepdims=True)
        dcw_ref[1:2, :] += jnp.sum(dxc * ext_ref[6:6 + tc, :], axis=0, keepdims=True)
        dcw_ref[0:1, :] += jnp.sum(dxc * ext_ref[5:5 + tc, :], axis=0, keepdims=True)
        ext2_ref[0:tc, :] = dxc
        ext2_ref[tc:tc + 8, :] = dhalo_ref[...]
        dxa = (cw_ref[3:4, :] * dxc + cw_ref[2:3, :] * ext2_ref[1:1 + tc, :]
               + cw_ref[1:2, :] * ext2_ref[2:2 + tc, :] + cw_ref[0:1, :] * ext2_ref[3:3 + tc, :])
        dxa_ref[...] = dxa.astype(dxa_ref.dtype)
        dhalo_ref[...] = ext2_ref[0:8, :]

        @pl.when((b == B - 1) & (t == nt - 1))
        def _():
            lane_block = lax.broadcasted_iota(jnp.int32, (LRU_BW, CW), 1) // LRU_BW
            for acc_ref, out_ref in ((accx_ref, dwx_ref), (acca_ref, dwa_ref)):
                diag = jnp.zeros((LRU_BW, CW), F32)
                for j in range(CW // LRU_BW):
                    diag = jnp.where(lane_block == j, acc_ref[j * LRU_BW:(j + 1) * LRU_BW, :], diag)
                for q in range(CW // LANES):
                    out_ref[0, q] = diag[:, q * LANES:(q + 1) * LANES]

    row_of = lambda b, t: b * nt + (nt - 1 - t)
    tile = lambda off: pl.BlockSpec((tc, CW), lambda c, b, t: (row_of(b, t), off + c))
    halo = pl.BlockSpec((8, CW), lambda c, b, t: (jnp.maximum(row_of(b, t) * h8 - 1, 0), c))
    vec = pl.BlockSpec((1, CW), lambda c, b, t: (0, c))
    mat = pl.BlockSpec((1, CW, CW), lambda c, b, t: (c, 0, 0))
    cwspec = pl.BlockSpec((CONV, CW), lambda c, b, t: (0, c))
    diag = pl.BlockSpec((1, CW // LANES, LRU_BW, LANES), lambda c, b, t: (c, 0, 0, 0))
    return pl.pallas_call(
        _after(body, 13, deps),
        name="lru_bwd",
        grid=(N_CT, B, nt),
        in_specs=[tile(0), tile(0), halo, tile(N_CT), tile(0), halo, cwspec, vec, mat, mat, vec, vec, vec]
        + [ANY_SPEC] * len(deps),
        out_specs=[tile(0), tile(0), cwspec, vec, diag, diag, vec, vec, vec],
        out_shape=[
            jax.ShapeDtypeStruct((T, D_MODEL), _MXU),
            jax.ShapeDtypeStruct((T, D_MODEL), _MXU),
            jax.ShapeDtypeStruct((CONV, D_MODEL), F32),
            jax.ShapeDtypeStruct((1, D_MODEL), F32),
            jax.ShapeDtypeStruct((N_CT, CW // LANES, LRU_BW, LANES), F32),
            jax.ShapeDtypeStruct((N_CT, CW // LANES, LRU_BW, LANES), F32),
            jax.ShapeDtypeStruct((1, D_MODEL), F32),
            jax.ShapeDtypeStruct((1, D_MODEL), F32),
            jax.ShapeDtypeStruct((1, D_MODEL), F32),
        ],
        scratch_shapes=[pltpu.VMEM((tc + 8, CW), F32), pltpu.VMEM((tc + 8, CW), F32),
                        pltpu.VMEM((8, CW), F32), pltpu.VMEM((8, CW), F32)] + _scan_scratch(tc)
        + [pltpu.VMEM((CW, CW), F32), pltpu.VMEM((CW, CW), F32)],
        compiler_params=_params(("parallel", "arbitrary", "arbitrary")),
    )(dya, proj, proj, proj, hlru, hlru, conv_w, conv_b, wx_bd, wa_bd, bx, ba, lam, *deps)


def _retention_tables(S):
    half = DK // 2
    freqs = ROPE_THETA ** (-jnp.arange(half, dtype=F32) / half)
    ang = jnp.arange(S, dtype=F32)[:, None] * freqs[None, :]
    log_g = jnp.log1p(-(2.0 ** (-5.0 - jnp.arange(HEADS, dtype=F32))))
    idx = jnp.arange(CHUNK, dtype=F32)
    diff = idx[:, None] - idx[None, :]
    inner = jnp.where(diff >= 0, jnp.exp(jnp.maximum(diff, 0.0)[None] * log_g[:, None, None]), 0.0)
    cross = jnp.exp((idx[None, :] + 1.0) * log_g[:, None])[:, :, None]
    state = jnp.exp((CHUNK - 1.0 - idx[None, :]) * log_g[:, None])[:, :, None]
    gam = jnp.broadcast_to(jnp.exp(CHUNK * log_g)[:, None, None], (HEADS, 1, DK))
    return jnp.cos(ang), jnp.sin(ang), inner, cross, state, gam


def _rot(x, cos, sin):
    half = DK // 2
    x1, x2 = x[:, :half], x[:, half:]
    return jnp.concatenate([x1 * cos - x2 * sin, x1 * sin + x2 * cos], axis=-1)


def _rot_t(y, cos, sin):
    half = DK // 2
    y1, y2 = y[:, :half], y[:, half:]
    return jnp.concatenate([y1 * cos + y2 * sin, y2 * cos - y1 * sin], axis=-1)


def _groupnorm(o):
    mu = jnp.mean(o, axis=-1, keepdims=True)
    oc = o - mu
    rs = lax.rsqrt(jnp.mean(oc * oc, axis=-1, keepdims=True) + EPS)
    return oc * rs, rs


def _ret_specs(B, chunk_of):
    rows = RET_CHUNKS * CHUNK
    qkv = lambda g: pl.BlockSpec((B, rows, D_MODEL), lambda c: (0, chunk_of(c), g))
    act = pl.BlockSpec((B, rows, D_MODEL), lambda c: (0, chunk_of(c), 0))
    rope = pl.BlockSpec((rows, DK // 2), lambda c: (chunk_of(c), 0))
    dmat = pl.BlockSpec((HEADS, CHUNK, CHUNK), lambda c: (0, 0, 0))
    dvec = pl.BlockSpec((HEADS, CHUNK, 1), lambda c: (0, 0, 0))
    hrow = pl.BlockSpec((HEADS, 1, DK), lambda c: (0, 0, 0))
    rst = pl.BlockSpec((RET_CHUNKS, B, HEADS, DK, DK), lambda c: (chunk_of(c), 0, 0, 0, 0))
    return qkv, act, rope, dmat, dvec, hrow, rst


def _ret_fwd(proj, tables, gain3, B, S):
    T = B * S
    nc = S // CHUNK
    cos, sin, dmat_t, cd_t, sd_t, gam_t = tables

    def body(q_ref, k_ref, v_ref, gb_ref, cos_ref, sin_ref, dm_ref, cd_ref, sd_ref, gam_ref, gain_ref,
             o_ref, yb_ref, rs_ref, state_ref):
        @pl.when(pl.program_id(0) == 0)
        def _():
            state_ref[...] = jnp.zeros_like(state_ref)

        for cc, b, h in [(cc, b, h) for cc in range(RET_CHUNKS) for b in range(B) for h in range(HEADS)]:
            rows = slice(cc * CHUNK, (cc + 1) * CHUNK)
            cos_t, sin_t = cos_ref[rows, :], sin_ref[rows, :]
            cols = slice(h * DK, (h + 1) * DK)
            qb = _c(_rot(q_ref[b, rows, cols], cos_t, sin_t))
            kb = _c(_rot(k_ref[b, rows, cols], cos_t, sin_t) * (DK ** -0.5))
            v = v_ref[b, rows, cols]
            state = state_ref[b, h]
            sb = _c(state)
            rs_ref[cc, b, h] = sb
            scores = _dot_nt(qb, kb) * dm_ref[h]
            o = _dot(_c(scores), _c(v)) + _dot(qb, sb) * cd_ref[h]
            state_ref[b, h] = gam_ref[h] * state + _dot_tn(kb, _c(v * sd_ref[h]))
            o_ref[b, rows, cols] = o
            n, _ = _groupnorm(o)
            gb = gb_ref[b, rows, cols]
            yb_ref[b, rows, cols] = (gb * _sigmoid(gb) * (n * gain_ref[h])).astype(yb_ref.dtype)

    qkv, act, rope, dmat, dvec, hrow, rst = _ret_specs(B, lambda c: c)
    proj3 = proj.reshape(B, S, proj.shape[1])
    o_pre, yb, states = pl.pallas_call(
        body,
        name="ret_fwd",
        grid=(nc // RET_CHUNKS,),
        in_specs=[qkv(2), qkv(3), qkv(4), qkv(5), rope, rope, dmat, dvec, dvec, hrow, hrow],
        out_specs=[act, act, rst],
        out_shape=[
            jax.ShapeDtypeStruct((B, S, D_MODEL), F32),
            jax.ShapeDtypeStruct((B, S, D_MODEL), _MXU),
            jax.ShapeDtypeStruct((nc, B, HEADS, DK, DK), _MXU),
        ],
        scratch_shapes=[pltpu.VMEM((B, HEADS, DK, DK), F32)],
        compiler_params=_params(("arbitrary",)),
    )(proj3, proj3, proj3, proj3, cos, sin, dmat_t, cd_t, sd_t, gam_t, gain3)
    return o_pre.reshape(T, D_MODEL), yb.reshape(T, D_MODEL), states


def _ret_bwd(dyb, o_pre, proj, states, tables, gain3, B, S, deps=()):
    T = B * S
    nc = S // CHUNK
    cos, sin, dmat_t, cd_t, sd_t, gam_t = tables

    def body(dyb_ref, o_ref, q_ref, k_ref, v_ref, gb_ref, rs_ref, cos_ref, sin_ref, dm_ref, cd_ref, sd_ref, gam_ref,
             gain_ref, dr_ref, dgain_ref, dstate_ref):
        @pl.when(pl.program_id(0) == 0)
        def _():
            dstate_ref[...] = jnp.zeros_like(dstate_ref)
            dgain_ref[...] = jnp.zeros_like(dgain_ref)

        for cc, b, h in [(cc, b, h) for cc in reversed(range(RET_CHUNKS)) for b in range(B) for h in range(HEADS)]:
            rows = slice(cc * CHUNK, (cc + 1) * CHUNK)
            cos_t, sin_t = cos_ref[rows, :], sin_ref[rows, :]
            cols = slice(h * DK, (h + 1) * DK)
            gain = gain_ref[h]
            n, rs = _groupnorm(o_ref[b, rows, cols])
            gb = gb_ref[b, rows, cols]
            sg = _sigmoid(gb)
            dy = dyb_ref[b, rows, cols]
            part = lambda g: slice(g * D_MODEL + h * DK, g * D_MODEL + (h + 1) * DK)
            dr_ref[b, rows, part(3)] = (dy * (n * gain) * (sg * (1.0 + gb * (1.0 - sg)))).astype(dr_ref.dtype)
            dgn = dy * (gb * sg)
            dgain_ref[h] += jnp.sum(dgn * n, axis=0, keepdims=True)
            dn = dgn * gain
            do = rs * (dn - jnp.mean(dn, axis=-1, keepdims=True) - n * jnp.mean(dn * n, axis=-1, keepdims=True))

            qb = _c(_rot(q_ref[b, rows, cols], cos_t, sin_t))
            kb = _c(_rot(k_ref[b, rows, cols], cos_t, sin_t) * (DK ** -0.5))
            v = v_ref[b, rows, cols]
            vb = _c(v)
            vsb = _c(v * sd_ref[h])
            dob = _c(do)
            docb = _c(do * cd_ref[h])
            dmat = dm_ref[h]
            dstate = dstate_ref[b, h]
            dsb = _c(dstate)
            pb = _c(_dot_nt(qb, kb) * dmat)
            dsc = _c(_dot_nt(dob, vb) * dmat)
            dq = _dot(dsc, kb) + _dot_nt(docb, rs_ref[cc, b, h])
            dk = _dot_tn(dsc, qb) + _dot_nt(vsb, dsb)
            dv = _dot_tn(pb, dob) + _dot(kb, dsb) * sd_ref[h]
            dstate_ref[b, h] = gam_ref[h] * dstate + _dot_tn(qb, docb)
            dr_ref[b, rows, part(0)] = _rot_t(dq, cos_t, sin_t).astype(dr_ref.dtype)
            dr_ref[b, rows, part(1)] = (_rot_t(dk, cos_t, sin_t) * (DK ** -0.5)).astype(dr_ref.dtype)
            dr_ref[b, rows, part(2)] = dv.astype(dr_ref.dtype)

    n_steps = nc // RET_CHUNKS
    qkv, act, rope, dmat, dvec, hrow, rst = _ret_specs(B, lambda c: n_steps - 1 - c)
    wide = pl.BlockSpec((B, RET_CHUNKS * CHUNK, 4 * D_MODEL), lambda c: (0, n_steps - 1 - c, 0))
    proj3 = proj.reshape(B, S, proj.shape[1])
    dr, dgain = pl.pallas_call(
        _after(body, 14, deps),
        name="ret_bwd",
        grid=(n_steps,),
        in_specs=[act, act, qkv(2), qkv(3), qkv(4), qkv(5), rst, rope, rope, dmat, dvec, dvec, hrow, hrow]
        + [ANY_SPEC] * len(deps),
        out_specs=[wide, hrow],
        out_shape=[jax.ShapeDtypeStruct((B, S, 4 * D_MODEL), _MXU), jax.ShapeDtypeStruct((HEADS, 1, DK), F32)],
        scratch_shapes=[pltpu.VMEM((B, HEADS, DK, DK), F32)],
        compiler_params=_params(("arbitrary",)),
    )(dyb.reshape(B, S, D_MODEL), o_pre.reshape(B, S, D_MODEL), proj3, proj3, proj3, proj3, states, cos, sin, dmat_t,
      cd_t, sd_t, gam_t, gain3, *deps)
    return dr.reshape(T, 4 * D_MODEL), dgain


def _mid(ya, yb, proj, x2d, tgt2d, wpa, wpb, wout, g_fin):
    T = x2d.shape[0]
    tm = min(MID_TILE, T)
    n_steps = T // tm
    rows = D_MODEL // (2 * N_CHIPS)

    def body(ya_ref, yb_ref, ma_ref, mb_ref, x_ref, t_ref, gf_ref, wpa_hbm, wpb_hbm, wout_hbm,
             loss_ref, dx2_ref, dya_ref, dyb_ref, dm_ref, dgf_ref, gw_hbm, w_ref, acc_ref, sem):
        i = pl.program_id(0)

        @pl.when(i == 0)
        def _():
            loads = [pltpu.make_async_copy(src, w_ref.at[k], sem.at[k]) for k, src in enumerate((wpa_hbm, wpb_hbm, wout_hbm))]
            for cp in loads:
                cp.start()
            for cp in loads:
                cp.wait()
            acc_ref[...] = jnp.zeros_like(acc_ref)
            loss_ref[...] = jnp.zeros_like(loss_ref)
            dgf_ref[...] = jnp.zeros_like(dgf_ref)

        ya_t, yb_t = ya_ref[...], yb_ref[...]
        out_a = _dot(ya_t, w_ref[0])
        out_b = _dot(yb_t, w_ref[1])
        sa = _sigmoid(ma_ref[...])
        sb = _sigmoid(mb_ref[...])
        mgb = _c(sa * out_a + sb * out_b)
        x2 = x_ref[...] + _dot(mgb, w_ref[2])
        r2 = lax.rsqrt(jnp.mean(x2 * x2, axis=-1, keepdims=True) + EPS)
        nx = x2 * r2
        gf = gf_ref[...]
        err = nx * gf - t_ref[...]
        loss_ref[...] += 0.5 * jnp.sum(jnp.mean(err * err, axis=-1, keepdims=True), axis=0, keepdims=True)
        dy = err * (1.0 / D_MODEL)
        dgf_ref[...] += jnp.sum(dy * nx, axis=0, keepdims=True)
        dyg = dy * gf
        dx2 = r2 * (dyg - nx * jnp.mean(dyg * nx, axis=-1, keepdims=True))
        dx2_ref[...] = dx2
        dx2b = _c(dx2)
        dmg = _dot_nt(dx2b, w_ref[2])
        acc_ref[2] += _dot_tn(mgb, dx2b)
        dm_ref[:, :D_MODEL] = (dmg * out_a * sa * (1.0 - sa)).astype(dm_ref.dtype)
        dm_ref[:, D_MODEL:] = (dmg * out_b * sb * (1.0 - sb)).astype(dm_ref.dtype)
        dab = _c(dmg * sa)
        dbb = _c(dmg * sb)
        dya_ref[...] = _dot_nt(dab, w_ref[0])
        dyb_ref[...] = _dot_nt(dbb, w_ref[1])
        acc_ref[0] += _dot_tn(ya_t, dab)
        acc_ref[1] += _dot_tn(yb_t, dbb)

        @pl.when(i == n_steps - 1)
        def _():
            copies = [pltpu.make_async_copy(acc_ref.at[k, pl.ds((2 * p + hf) * rows, rows), :], gw_hbm.at[p, hf, k],
                                            sem.at[(k * N_CHIPS + p) * 2 + hf])
                      for k in range(3) for p in range(N_CHIPS) for hf in range(2)]
            for cp in copies:
                cp.start()
            for cp in copies:
                cp.wait()

    tile = lambda j: pl.BlockSpec((tm, D_MODEL), lambda i: (i, j))
    one = pl.BlockSpec((1, D_MODEL), lambda i: (0, 0))
    anyspec = pl.BlockSpec(memory_space=pl.ANY)
    return pl.pallas_call(
        body,
        name="mid",
        grid=(n_steps,),
        in_specs=[tile(0), tile(0), tile(6), tile(7), tile(0), tile(0), one, anyspec, anyspec, anyspec],
        out_specs=[pl.BlockSpec((1, 1), lambda i: (0, 0)), tile(0), tile(0), tile(0),
                   pl.BlockSpec((tm, 2 * D_MODEL), lambda i: (i, 0)), one, anyspec],
        out_shape=[
            jax.ShapeDtypeStruct((1, 1), F32),
            jax.ShapeDtypeStruct((T, D_MODEL), F32),
            jax.ShapeDtypeStruct((T, D_MODEL), F32),
            jax.ShapeDtypeStruct((T, D_MODEL), F32),
            jax.ShapeDtypeStruct((T, 2 * D_MODEL), _MXU),
            jax.ShapeDtypeStruct((1, D_MODEL), F32),
            jax.ShapeDtypeStruct((N_CHIPS, 2, 3, rows, D_MODEL), F32),
        ],
        scratch_shapes=[pltpu.VMEM((3, D_MODEL, D_MODEL), _MXU), pltpu.VMEM((3, D_MODEL, D_MODEL), F32),
                        pltpu.SemaphoreType.DMA((3 * N_CHIPS * 2,))],
        compiler_params=_params(("arbitrary",)),
    )(ya, yb, proj, proj, x2d, tgt2d, g_fin, wpa, wpb, wout)


def _inproj_bwd_dx(dparts, w_all, x2d, dx2, g_in, first, count, prev, name, deps=()):
    T = x2d.shape[0]
    tm = min(DX_TILE, T)
    n_d = len(dparts)
    groups = [(a, k) for a, d in enumerate(dparts) for k in range(d.shape[1] // D_MODEL)]
    dg_start = jnp.zeros((1, D_MODEL), F32) if prev is None else prev[1]
    carried = () if prev is None else (prev[0],)

    def body(*refs):
        d_refs = refs[:n_d]
        x_ref, dx2_ref, g_ref, dg0_ref, w_hbm = refs[n_d:n_d + 5]
        dx_ref, dg_ref, w_ref, sem = refs[-4:]

        def load(j):
            part = (j // 2, slice(None), pl.ds((j % 2) * D_MODEL, D_MODEL))
            return pltpu.make_async_copy(w_hbm.at[part], w_ref.at[part], sem.at[j])

        def tile(before_group):
            dh = jnp.zeros((tm, D_MODEL), F32)
            for j, (a, k) in enumerate(groups):
                before_group(j)
                dh = dh + _dot_nt(d_refs[a][:, k * D_MODEL:(k + 1) * D_MODEL],
                                  w_ref[j // 2, :, (j % 2) * D_MODEL:(j % 2 + 1) * D_MODEL])
            x = x_ref[...]
            r = lax.rsqrt(jnp.mean(x * x, axis=-1, keepdims=True) + EPS)
            nx = x * r
            dg_ref[...] += jnp.sum(dh * nx, axis=0, keepdims=True)
            dhg = dh * g_ref[...]
            dx_ref[...] = dx2_ref[...] + r * (dhg - nx * jnp.mean(dhg * nx, axis=-1, keepdims=True))

        first = pl.program_id(0) == 0

        @pl.when(first)
        def _():
            for j in range(len(groups)):
                load(j).start()
            dg_ref[...] = dg0_ref[...]
            tile(lambda j: load(j).wait())

        @pl.when(jnp.logical_not(first))
        def _():
            tile(lambda j: None)

    tile = pl.BlockSpec((tm, D_MODEL), lambda i: (first + i, 0))
    one = pl.BlockSpec((1, D_MODEL), lambda i: (0, 0))
    return pl.pallas_call(
        body,
        name=name,
        grid=(count,),
        in_specs=[pl.BlockSpec((tm, d.shape[1]), lambda i: (first + i, 0)) for d in dparts]
        + [tile, tile, one, one, ANY_SPEC] + [ANY_SPEC] * (len(carried) + len(deps)),
        out_specs=[tile, one],
        out_shape=[jax.ShapeDtypeStruct((T, D_MODEL), F32), jax.ShapeDtypeStruct((1, D_MODEL), F32)],
        input_output_aliases={n_d + 5: 0} if carried else {},
        scratch_shapes=[pltpu.VMEM(w_all.shape, w_all.dtype), pltpu.SemaphoreType.DMA((len(groups),))],
        compiler_params=_params(("arbitrary",)),
    )(*dparts, x2d, dx2, g_in, dg_start, w_all, *carried, *deps)


def _inproj_bwd_dw(ht, dparts, name, deps=()):
    T = ht.shape[1]
    tn = DW_COLS
    half = D_MODEL // 2
    per_chip = 2 * D_MODEL // tn
    n_d = len(dparts)
    tiles = [(a, t) for a, d in enumerate(dparts) for t in range(d.shape[1] // tn)]
    offs = [sum(d.shape[1] // tn for d in dparts[:a]) for a in range(n_d)]

    def body(*refs):
        ht_hbm = refs[0]
        d_refs = refs[1:1 + n_d]
        out_ref, ht_ref, sem = refs[-3:]
        t = pl.program_id(0)

        def load(k):
            cols = pl.ds(k * (T // DW_LOADS), T // DW_LOADS)
            return pltpu.make_async_copy(ht_hbm.at[:, cols], ht_ref.at[:, cols], sem.at[k])

        def store(g):
            out_ref[0, 0] = g[:half]
            out_ref[0, 1] = g[half:]

        @pl.when(t == 0)
        def _():
            for k in range(DW_LOADS):
                load(k).start()
            g = jnp.zeros((D_MODEL, tn), F32)
            for k in range(DW_LOADS):
                load(k).wait()
                tokens = slice(k * (T // DW_LOADS), (k + 1) * (T // DW_LOADS))
                g = g + _dot(ht_ref[:, tokens], d_refs[0][tokens, :])
            store(g)

        for a in range(n_d):
            lo, hi = max(offs[a], 1), offs[a] + dparts[a].shape[1] // tn

            @pl.when((t >= lo) & (t < hi))
            def _(a=a):
                store(_dot(ht_ref[...], d_refs[a][...]))

    def dspec(a):
        n_a = dparts[a].shape[1] // tn
        return pl.BlockSpec((T, tn), lambda t: (0, jnp.clip(t - offs[a], 0, n_a - 1)))

    return pl.pallas_call(
        body,
        name=name,
        grid=(len(tiles),),
        in_specs=[ANY_SPEC] + [dspec(a) for a in range(n_d)] + [ANY_SPEC] * len(deps),
        out_specs=pl.BlockSpec((1, 2, half, tn), lambda t: (t // per_chip, 0, 0, t % per_chip)),
        out_shape=jax.ShapeDtypeStruct((len(tiles) // per_chip, 2, half, 2 * D_MODEL), F32),
        scratch_shapes=[pltpu.VMEM(ht.shape, ht.dtype), pltpu.SemaphoreType.DMA((DW_LOADS,))],
        compiler_params=_params(("arbitrary",)),
    )(ht, *dparts, *deps)


def _coords():
    return lax.axis_index("x"), lax.axis_index("y"), lax.axis_index("c")


def _other_chips(x, y):
    return [(1 - x, y), (x, 1 - y), (1 - x, 1 - y)]


def _chunks(rows, n):
    size = rows // n
    return [pl.ds(q * size, size) for q in range(n)]


HBM_SPEC = pl.BlockSpec(memory_space=pltpu.HBM)
SEM_SPEC = pl.BlockSpec(memory_space=pltpu.SEMAPHORE)
DATAFLOW = pltpu.SideEffectType.DATAFLOW_SIDE_EFFECTING


def _copies_start(bufs, plan, n_copies, name, deps=()):
    n = len(bufs)
    n_deps = len(deps)

    def body(*refs):
        ins = refs[:n]
        send_sems, recv_sems = refs[n + n_deps], refs[n + n_deps + 1]
        token = refs[-1]
        for k, send, _ in plan(ins):
            if send is not None:
                src, dst, dev, pred = send
                cp = pltpu.make_async_remote_copy(src_ref=src, dst_ref=dst, send_sem=send_sems.at[k],
                                                  recv_sem=recv_sems.at[k], device_id=dev, device_id_type=MESH)
                if pred is None:
                    cp.start()
                else:
                    pl.when(pred)(cp.start)
        token[...] = jnp.zeros_like(token)

    hbm = [pltpu.with_memory_space_constraint(b, pltpu.HBM) for b in bufs]
    outs = pl.pallas_call(
        body,
        name=name,
        in_specs=[HBM_SPEC] * n + [ANY_SPEC] * n_deps,
        out_specs=(SEM_SPEC, SEM_SPEC, *([HBM_SPEC] * n), pl.BlockSpec(memory_space=pltpu.VMEM)),
        out_shape=(pltpu.SemaphoreType.DMA((n_copies,)), pltpu.SemaphoreType.DMA((n_copies,)),
                   *[pltpu.HBM(b.shape, b.dtype) for b in bufs], jax.ShapeDtypeStruct((8, 128), F32)),
        input_output_aliases={a: 2 + a for a in range(n)},
        compiler_params=pltpu.CompilerParams(has_side_effects=DATAFLOW),
    )(*hbm, *deps)
    return outs[0], outs[1], list(outs[2:2 + n]), outs[-1]


def _copies_wait(send_sems, recv_sems, bufs, after, plan, name, only=None):
    n = len(bufs)

    def body(*refs):
        ins = refs[:n]
        s_sems, r_sems = refs[n], refs[n + 1]
        for k, send, recv in plan(ins):
            if only is not None and k not in only:
                continue
            if send is not None:
                src, dst, dev, pred = send
                cp = pltpu.make_async_remote_copy(src_ref=src, dst_ref=dst, send_sem=s_sems.at[k],
                                                  recv_sem=r_sems.at[k], device_id=dev, device_id_type=MESH)
                if pred is None:
                    cp.wait_send()
                else:
                    pl.when(pred)(cp.wait_send)
            if recv is not None:
                dst, pred = recv
                cp = pltpu.make_async_remote_copy(src_ref=dst, dst_ref=dst, send_sem=s_sems.at[k],
                                                  recv_sem=r_sems.at[k], device_id=_coords(), device_id_type=MESH)
                if pred is None:
                    cp.wait_recv()
                else:
                    pl.when(pred)(cp.wait_recv)

    outs = pl.pallas_call(
        body,
        name=name,
        in_specs=[HBM_SPEC] * n + [SEM_SPEC, SEM_SPEC, pl.BlockSpec(memory_space=pl.ANY)],
        out_specs=[HBM_SPEC] * n,
        out_shape=[pltpu.HBM(b.shape, b.dtype) for b in bufs],
        input_output_aliases={a: a for a in range(n)},
        compiler_params=pltpu.CompilerParams(has_side_effects=DATAFLOW),
    )(*bufs, send_sems, recv_sems, after)
    return list(outs)


def _gather_plan(n_bufs):
    def plan(refs):
        x, y, c = _coords()
        me = 2 * x + y
        out = []
        for k, (px, py) in enumerate(_other_chips(x, y)):
            for a in range(n_bufs):
                out.append((k * n_bufs + a, (refs[a].at[me], refs[a].at[me], (px, py, c), None),
                            (refs[a].at[2 * px + py], None)))
        return out
    return plan


def _cast_into_slot(ws, name, deps=()):
    n = len(ws)
    nt = 2

    def body(s_ref, *refs):
        outs = refs[len(refs) - n:]
        for a in range(n):
            outs[a][0] = refs[a][...].astype(outs[a].dtype)

    xi, yi, _ = _coords()
    return pl.pallas_call(
        body,
        name=name,
        grid_spec=pltpu.PrefetchScalarGridSpec(
            num_scalar_prefetch=1,
            grid=(2, nt),
            in_specs=[pl.BlockSpec((1, w.shape[1] // nt, w.shape[2]), lambda hf, i, s: (hf, i, 0)) for w in ws]
            + [ANY_SPEC] * len(deps),
            out_specs=[pl.BlockSpec((1, 1, w.shape[1] // nt, w.shape[2]), lambda hf, i, s: (s[0], hf, i, 0)) for w in ws],
        ),
        out_shape=[jax.ShapeDtypeStruct((N_CHIPS,) + w.shape, _MXU) for w in ws],
        compiler_params=_params(("parallel", "parallel")),
    )((2 * xi + yi).reshape(1).astype(jnp.int32), *ws, *deps)


def _chip_gather_plan(stage, n_bufs):
    def plan(refs):
        x, y, c = _coords()
        me = 2 * x + y
        near = [(1 - x, y), (x, 1 - y)]
        slots = [2 * (1 - x) + y, 2 * x + (1 - y), 2 * (1 - x) + (1 - y)]
        sibling = (x, y, 1 - c)
        pass_to = (jnp.where(c == 0, x, 1 - x), jnp.where(c == 0, 1 - y, y), c)
        pass_slot = jnp.where(c == 0, slots[0], slots[1])
        out = []

        def move(src_slot, to, land_slot, land_core, pieces):
            for a, buf in enumerate(refs):
                for rows in _chunks(buf.shape[2], pieces[a]):
                    out.append((len(out), (buf.at[src_slot, c, rows], buf.at[src_slot, c, rows], to, None),
                                (buf.at[land_slot, land_core, rows], None)))

        if stage == "near":
            for k, chip in enumerate(near):
                move(me, (*chip, c), slots[k], c, NEAR_PIECES[:n_bufs])
        elif stage == "pass":
            move(pass_slot, pass_to, slots[2], c, PASS_PIECES[:n_bufs])
            for k in range(2):
                move(slots[k], sibling, slots[k], 1 - c, [1] * n_bufs)
        else:
            move(slots[2], sibling, slots[2], 1 - c, [1] * n_bufs)
        return out
    return plan


NEAR_PIECES = (2, 1)
PASS_PIECES = (2, 1)


def _chip_gather_copies(stage, n_bufs):
    if stage == "near":
        return 2 * sum(NEAR_PIECES[:n_bufs]), None
    if stage == "pass":
        n_pass = sum(PASS_PIECES[:n_bufs])
        return n_pass + 2 * n_bufs, set(range(n_pass))
    return n_bufs, None


def _swap_plan(n_slabs):
    def plan(refs):
        x, y, c = _coords()
        out, k = [], 0
        for i, n in enumerate(n_slabs):
            g, land = refs[2 * i], refs[2 * i + 1]
            for p in range(n):
                out.append((k, (g.at[p, 1 - c], land.at[p], (x, y, 1 - c), None), (land.at[p], None)))
                k += 1
        return out
    return plan


def _is_one_of(chip, dests):
    hit = chip == dests[0]
    for d in dests[1:]:
        hit = hit | (chip == d)
    return hit


def _slab_of(chip, dests):
    return sum(j * (chip == d).astype(jnp.int32) for j, d in enumerate(dests))


def _scatter_plan(dest_sets):
    def plan(refs):
        x, y, c = _coords()
        me = 2 * x + y
        out = []
        for k, (px, py) in enumerate(_other_chips(x, y)):
            peer = 2 * px + py
            for i, dests in enumerate(dest_sets):
                cs, land = refs[2 * i], refs[2 * i + 1]
                everyone = len(dests) == N_CHIPS
                send = (cs.at[_slab_of(peer, dests)], land.at[k], (px, py, c),
                        None if everyone else _is_one_of(peer, dests))
                recv = (land.at[k], None if everyone else _is_one_of(me, dests))
                out.append((k * len(dest_sets) + i, send, recv))
        return out
    return plan


def _join_plan(rows, n_pieces):
    def plan(refs):
        x, y, c = _coords()
        (buf,) = refs
        return [(i, (buf.at[c, piece], buf.at[c, piece], (x, y, 1 - c), None), (buf.at[1 - c, piece], None))
                for i, piece in enumerate(_chunks(rows, n_pieces))]
    return plan


def _join_plans(parts):
    def plan(refs):
        out, b0, k0 = [], 0, 0
        for part_plan, n_bufs, n_copies in parts:
            out += [(k0 + k, send, recv) for k, send, recv in part_plan(refs[b0:b0 + n_bufs])]
            b0 += n_bufs
            k0 += n_copies
        return out
    return plan


def _allgather_plan():
    def plan(refs):
        x, y, c = _coords()
        (land,) = refs
        me = 4 * x + 2 * y + c
        out = []
        for r in range(1, 8):
            px = 1 - x if r & 4 else x
            py = 1 - y if r & 2 else y
            pc = 1 - c if r & 1 else c
            out.append((r - 1, (land.at[me], land.at[me], (px, py, pc), None), (land.at[4 * px + 2 * py + pc], None)))
        return out
    return plan


def _sum_gathered(land, shapes, name):
    m = land.shape[1]

    def body(land_ref, *refs):
        outs, acc_ref = refs[:-1], refs[-1]
        acc = land_ref[0]
        for d in range(1, 8):
            acc = acc + land_ref[d]
        acc_ref[...] = acc
        r = 0
        for o_ref, (n, w) in zip(outs, shapes):
            if w == LANES:
                o_ref[...] = acc_ref[r:r + n, :]
                r += n
            elif w < LANES:
                o_ref[...] = acc_ref[r:r + n, 0:w]
                r += SUBLANES
            else:
                for k in range(n):
                    for q in range(w // LANES):
                        o_ref[k:k + 1, q * LANES:(q + 1) * LANES] = acc_ref[r:r + 1, :]
                        r += 1
        assert r == m, (r, m)

    return pl.pallas_call(
        body,
        name=name,
        out_shape=[jax.ShapeDtypeStruct(s, F32) for s in shapes],
        scratch_shapes=[pltpu.VMEM((m, LANES), F32)],
        compiler_params=_params(),
    )(land)


def _row_tile(rows, cap):
    t = cap
    while rows % t:
        t //= 2
    return t


def _add_my_halves(pairs, steps, name):
    k = len(pairs)

    def body(c_ref, *refs):
        for a in range(k):
            o_ref = refs[2 * k + a]
            o_ref[...] = (refs[2 * a][0] + refs[2 * a + 1][...]).astype(o_ref.dtype)

    in_specs, out_specs = [], []
    for g, _ in pairs:
        n_slabs, _, R, C = g.shape
        per = steps // n_slabs
        tr = R // per
        in_specs += [pl.BlockSpec((1, 1, tr, C), lambda p, c_ref, per=per: (p // per, c_ref[0], p % per, 0)),
                     pl.BlockSpec((1, tr, C), lambda p, c_ref, per=per: (p // per, p % per, 0))]
        out_specs.append(pl.BlockSpec((1, tr, C), lambda p, c_ref, per=per: (p // per, p % per, 0)))
    return pl.pallas_call(
        body,
        name=name,
        grid_spec=pltpu.PrefetchScalarGridSpec(num_scalar_prefetch=1, grid=(steps,), in_specs=in_specs, out_specs=out_specs),
        out_shape=[jax.ShapeDtypeStruct(r.shape, jnp.bfloat16) for _, r in pairs],
        compiler_params=_params(("parallel",)),
    )(lax.axis_index("c").reshape(1).astype(jnp.int32), *[a for pair in pairs for a in pair])


def _add_my_half(g, r, name):
    n_slabs, _, R, _ = g.shape
    steps = n_slabs if n_slabs > 1 else R // _row_tile(R, SUM_ROWS)
    return _add_my_halves([(g, r)], steps, name)[0]


def _sum_slabs(own, got, name, deps=()):
    _, R, C = own.shape
    tr = _row_tile(R, SUM_ROWS)

    def body(s_ref, own_ref, got_ref, *rest):
        rest[-1][0] = ((own_ref[0].astype(F32) + got_ref[0].astype(F32)) + got_ref[1].astype(F32)) + got_ref[2].astype(F32)

    xi, yi, ci = _coords()
    return pl.pallas_call(
        body,
        name=name,
        grid_spec=pltpu.PrefetchScalarGridSpec(
            num_scalar_prefetch=1,
            grid=(R // tr,),
            in_specs=[pl.BlockSpec((1, tr, C), lambda i, s: (s[0], i, 0)),
                      pl.BlockSpec((3, tr, C), lambda i, s: (0, i, 0))] + [ANY_SPEC] * len(deps),
            out_specs=pl.BlockSpec((1, tr, C), lambda i, s: (s[1], i, 0)),
        ),
        out_shape=jax.ShapeDtypeStruct((2, R, C), F32),
        compiler_params=_params(("parallel",)),
    )(jnp.stack([2 * xi + yi, ci]).astype(jnp.int32), own, got, *deps)


def _adamw_math(w, g, m, v):
    m = ADAM_B1 * m + (1.0 - ADAM_B1) * g
    v = ADAM_B2 * v + (1.0 - ADAM_B2) * (g * g)
    m_hat = m / (1.0 - ADAM_B1 ** ADAM_STEP)
    v_hat = v / (1.0 - ADAM_B2 ** ADAM_STEP)
    delta = -ADAM_LR * (m_hat / (jnp.sqrt(v_hat) + ADAM_EPS) + ADAM_WD * w)
    return delta, m, v


def _sum_adamw_own(owns, got, dest_sets, w, m, v, name):
    n = len(owns)
    _, R, C = owns[0].shape
    tr = _row_tile(R, ADAMW_ROWS)
    steps = R // tr

    def body(s_ref, *refs):
        got_ref, w_ref, m_ref, v_ref = refs[n:n + 4]
        half_ref, g_ref, d_ref, mn_ref, vn_ref = refs[len(refs) - 5:]
        total = jnp.zeros((tr, C), F32)
        for i in range(n):
            total = total + jnp.where(s_ref[2 + 2 * i] == 1, refs[i][0].astype(F32), 0.0)
        grad = ((total + got_ref[0].astype(F32)) + got_ref[1].astype(F32)) + got_ref[2].astype(F32)
        d, mn, vn = _adamw_math(w_ref[...], grad, m_ref[...], v_ref[...])
        half_ref[0] = grad
        for o, val in zip((g_ref, d_ref, mn_ref, vn_ref), (grad, d, mn, vn)):
            o[...] = val

    xi, yi, ci = _coords()
    me = 2 * xi + yi
    scalars = [ci, ci]
    for dests in dest_sets:
        scalars += [_is_one_of(me, dests).astype(jnp.int32), _slab_of(me, dests)]
    own_spec = lambda i: pl.BlockSpec((1, tr, C), lambda r, s: (s[3 + 2 * i], r, 0))
    rows = pl.BlockSpec((tr, C), lambda r, s: (s[0] * steps + r, 0))
    outs = pl.pallas_call(
        body,
        name=name,
        grid_spec=pltpu.PrefetchScalarGridSpec(
            num_scalar_prefetch=1,
            grid=(steps,),
            in_specs=[own_spec(i) for i in range(n)] + [pl.BlockSpec((3, tr, C), lambda r, s: (0, r, 0))] + [rows] * 3,
            out_specs=[pl.BlockSpec((1, tr, C), lambda r, s: (s[0], r, 0))] + [rows] * 4,
        ),
        out_shape=[jax.ShapeDtypeStruct((2, R, C), F32)] + [jax.ShapeDtypeStruct((2 * R, C), F32)] * 4,
        compiler_params=_params(("parallel",)),
    )(jnp.stack(scalars).astype(jnp.int32), *owns, got, w, m, v)
    return outs[0], list(outs[1:])


def _adamw_halves(ws, g, ms, vs, half, prev, name, deps=()):
    n = len(ws)
    _, _, R, C = g.shape
    tr = _row_tile(R, ADAMW_ROWS)
    steps = R // tr
    carried = [] if prev is None else [a for four in prev for a in four]
    both = half is None
    which = (lambda i, s: i // steps) if both else (lambda i, s: s[0])
    half = 0 if both else half

    def body(s_ref, *refs):
        w_refs, g_refs, m_refs, v_refs = (refs[k * n:(k + 1) * n] for k in range(4))
        outs = refs[len(refs) - 4 * n:]
        for a in range(n):
            grad = g_refs[a][0, 0]
            d, mn, vn = _adamw_math(w_refs[a][...], grad, m_refs[a][...], v_refs[a][...])
            for o, val in zip(outs[4 * a:4 * a + 4], (grad, d, mn, vn)):
                o[...] = val

    rows = pl.BlockSpec((tr, C), lambda i, s: (which(i, s) * steps + i % steps, 0))
    grad_spec = lambda a: pl.BlockSpec((1, 1, tr, C), lambda i, s: (which(i, s), a, i % steps, 0))
    n_in = 4 * n
    outs = pl.pallas_call(
        body,
        name=name,
        grid_spec=pltpu.PrefetchScalarGridSpec(
            num_scalar_prefetch=1,
            grid=(2 * steps if both else steps,),
            in_specs=[rows] * n + [grad_spec(a) for a in range(n)] + [rows] * (2 * n)
            + [ANY_SPEC] * (len(carried) + len(deps)),
            out_specs=[rows] * (4 * n),
        ),
        out_shape=[jax.ShapeDtypeStruct((2 * R, C), F32)] * (4 * n),
        input_output_aliases={1 + n_in + k: k for k in range(len(carried))},
        compiler_params=_params(("parallel",)),
    )(jnp.reshape(half, (1,)).astype(jnp.int32), *ws, *([g] * n), *ms, *vs, *carried, *deps)
    return [outs[4 * a:4 * a + 4] for a in range(n)]


def _adamw_small(ws, gs, ms, vs, name):
    n = len(ws)

    def body(*refs):
        for a in range(n):
            d, mn, vn = _adamw_math(refs[a][...], refs[n + a][...], refs[2 * n + a][...], refs[3 * n + a][...])
            refs[4 * n + a][...] = d
            refs[5 * n + a][...] = mn
            refs[6 * n + a][...] = vn

    shapes = [jax.ShapeDtypeStruct(w.shape, F32) for w in ws]
    outs = pl.pallas_call(
        body,
        name=name,
        out_shape=shapes * 3,
        compiler_params=_params(),
    )(*ws, *gs, *ms, *vs)
    return outs[:n], outs[n:2 * n], outs[2 * n:]


def _to_blockdiag(w):
    per = CW // LRU_BW
    w4 = w.reshape(N_CT, per, LRU_BW, LRU_BW)
    eye = jnp.eye(per, dtype=w.dtype)
    return (w4[:, :, :, None, :] * eye[None, :, None, :, None]).reshape(N_CT, CW, CW)


def _blocks_from_lanes(g):
    side = LANES // LRU_BW
    g5 = g.reshape(N_CT, CW // LANES, LRU_BW, side, LRU_BW)
    return jnp.transpose(g5, (0, 1, 3, 2, 4)).reshape(LRU_BLOCKS, LRU_BW, LRU_BW)


def _local_grads(x2d, tgt2d, B, S, g_in, in_proj, conv_b, gate_x_w, gate_x_b, gate_a_w, gate_a_b, lam,
                 proj_weights, g_fin, reduce):
    wx_bd = _c(_to_blockdiag(gate_x_w))
    wa_bd = _c(_to_blockdiag(gate_a_w))
    tables = _retention_tables(S)

    proj, ht, w_all, conv_w, gain = in_proj(x2d, g_in, (*tables, wx_bd, wa_bd))
    gain3 = gain.reshape(HEADS, 1, DK)
    hlru, ya = _lru_fwd(proj, conv_w, conv_b, wx_bd, wa_bd, gate_x_b, gate_a_b, lam, B, S)
    o_pre, yb, states = _ret_fwd(proj, tables, gain3, B, S)
    wpa, wpb, wout = proj_weights(yb)
    loss, dx2, dya, dyb, dm, dgf, gw_proj = _mid(ya, yb, proj, x2d, tgt2d, wpa, wpb, wout, g_fin)
    g3 = _inproj_bwd_dw(ht, [dm], "inproj_bwd_dw_m")
    deps = reduce.m_ready(gw_proj, g3)
    dr, dgain = _ret_bwd(dyb, o_pre, proj, states, tables, gain3, B, S, deps)
    deps = reduce.ret_done(dr)
    g12 = _inproj_bwd_dw(ht, [dr], "inproj_bwd_dw_r", deps)
    deps = reduce.r_ready(g12)
    dxa, dga, dcw, dcb, dwx, dwa, dbx, dba, dlam = _lru_bwd(
        dya, proj, hlru, conv_w, conv_b, wx_bd, wa_bd, gate_x_b, gate_a_b, lam, B, S, deps)
    small = dict(conv_w=dcw, conv_b=dcb, gate_x_w=dwx, gate_x_b=dbx, gate_a_w=dwa, gate_a_b=dba, lru_lambda=dlam,
                 gn_gain=dgain.reshape(HEADS, DK), norm_final=dgf)
    deps = reduce.lru_done(dxa, _pack_small(small, loss, reduce.slot()))
    g0 = _inproj_bwd_dw(ht, [dxa, dga], "inproj_bwd_dw_a", deps)
    deps = reduce.a_ready(g0)
    n_tiles = x2d.shape[0] // min(DX_TILE, x2d.shape[0])
    grad_x, dgin = _inproj_bwd_dx([dxa, dga, dr, dm], w_all, x2d, dx2, g_in, 0, n_tiles, None, "inproj_bwd_dx", deps)
    return grad_x, dgin


ALL_CHIPS = (0, 1, 2, 3)


class _GradReduce:
    def __init__(self, proj_done):
        self.pending = {}
        self.proj_done = proj_done
        self.land_in = None

    def _start(self, key, parts, name):
        bufs, plans, shared = [], [], None
        for part_bufs, plan, n_copies, part_shared in parts:
            if part_shared is not None:
                shared = len(bufs) + part_shared
            plans.append((plan, len(part_bufs), n_copies))
            bufs += part_bufs
        plan = _join_plans(plans)
        send_sems, recv_sems, bufs, token = _copies_start(bufs, plan, sum(p[2] for p in plans), name + "_start")
        if shared is not None:
            self.land_in = bufs[shared]
        self.pending[key] = (send_sems, recv_sems, bufs, plan, name + "_wait", shared)
        return (token,)

    def _finish(self, key, after):
        send_sems, recv_sems, bufs, plan, name, shared = self.pending.pop(key)
        if shared is not None:
            bufs[shared] = self.land_in
        bufs = _copies_wait(send_sems, recv_sems, bufs, after, plan, name)
        if shared is not None:
            self.land_in = bufs[shared]
        return bufs

    @staticmethod
    def _swap(pieces):
        bufs = []
        for g in pieces:
            bufs += [g, lax.empty((g.shape[0],) + g.shape[2:], F32)]
        n_slabs = [g.shape[0] for g in pieces]
        return bufs, _swap_plan(n_slabs), sum(n_slabs), None

    def _scatter(self, sums, dest_sets):
        bufs = []
        for cs in sums:
            bufs += [cs, lax.empty((3,) + cs.shape[1:], cs.dtype)]
        if self.land_in is not None:
            bufs[-1] = self.land_in
        return bufs, _scatter_plan(dest_sets), 3 * len(sums), len(bufs) - 1

    @staticmethod
    def slot():
        x, y, c = _coords()
        return 4 * x + 2 * y + c

    def _gather8(self, block):
        land = lax.dynamic_update_slice(lax.empty((8,) + block.shape, F32), block[None], (self.slot(), 0, 0))
        return [land], _allgather_plan(), 7, None

    def m_ready(self, gw_proj, g3):
        rows = gw_proj.shape[2] * gw_proj.shape[3]
        return self._start("m", [self._swap([gw_proj.reshape(N_CHIPS, 2, rows, D_MODEL), g3])], "swap_m")

    def ret_done(self, after):
        proj, land_p, g3, land_3 = self._finish("m", after)
        sums_m = _add_my_halves([(proj, land_p), (g3, land_3)], N_CHIPS, "chip_sum_m")
        return self._start("sm", [self._scatter(sums_m, [ALL_CHIPS, (3,)])], "scatter_m")

    def r_ready(self, g12):
        return self._start("r", [self._swap([g12])], "swap_r")

    def lru_done(self, after, packed):
        g12, land_12 = self._finish("r", after)
        sums_r = [_add_my_half(g12, land_12, "chip_sum_r")]
        return (self._start("sr", [self._scatter(sums_r, [(1, 2)])], "scatter_r")
                + self._start("small", [([packed], _allgather_plan(), 7, None)], "gather_small"))

    def a_ready(self, g0):
        (token,) = self._start("a", [self._swap([g0])], "swap_a")
        csp, gotp, self.cs3, _ = self._finish("sm", token)
        half_proj = _sum_slabs(csp, gotp, "sum_w_proj")
        g0, land_0 = self._finish("a", half_proj)
        join = ([half_proj], _join_plan(half_proj.shape[1], PROJ_JOIN_PIECES), PROJ_JOIN_PIECES, None)
        return self._start("sa", [self._scatter([_add_my_half(g0, land_0, "chip_sum_a")], [(0,)]), join], "scatter_a")

    def finish(self, dgin, w_in_own, w_in_done):
        (small,) = self._finish("small", dgin)
        cs12, _ = self._finish("sr", dgin)
        cs0, _, g_proj = self._finish("sa", dgin)
        half_in, first = w_in_own([self.cs3, cs12, cs0], self.land_in, [(3,), (1, 2), (0,)])
        join = ([half_in], _join_plan(half_in.shape[1], JOIN_PIECES), JOIN_PIECES, None)
        deps = self._start("j", [join, self._gather8(dgin)], "join_w_in")
        g_in, norm_in = self._finish("j", self.proj_done(g_proj, deps))
        w_in_done(g_in, first)
        return _unpack_small(small), _sum_gathered(norm_in, [(1, D_MODEL)], "sum_norm_in_grad")[0]


_SMALL = ("gate_x_w", "gate_a_w", "conv_w", "conv_b", "gate_x_b", "gate_a_b", "lru_lambda", "gn_gain", "norm_final")
_SMALL_SHAPES = dict(gate_x_w=(LRU_BLOCKS, LRU_BW, LRU_BW), gate_a_w=(LRU_BLOCKS, LRU_BW, LRU_BW),
                     norm_in=(1, D_MODEL), conv_w=(CONV, D_MODEL), conv_b=(1, D_MODEL), gate_x_b=(1, D_MODEL),
                     gate_a_b=(1, D_MODEL), lru_lambda=(1, D_MODEL), gn_gain=(HEADS, DK), norm_final=(1, D_MODEL))


def _pack_small(small, loss, slot):
    parts = [small[k] if small[k].ndim == 2 else small[k].reshape(-1, LANES) for k in _SMALL]
    m = sum(p.size for p in parts) // LANES + SUBLANES

    def body(s_ref, *refs):
        o_ref = refs[-1]
        r = 0
        for ref, part in zip(refs, parts):
            if part.shape[1] == LANES:
                o_ref[0, r:r + part.shape[0], :] = ref[...]
                r += part.shape[0]
                continue
            for k in range(part.shape[0]):
                for q in range(part.shape[1] // LANES):
                    o_ref[0, r:r + 1, :] = ref[k:k + 1, q * LANES:(q + 1) * LANES]
                    r += 1
        o_ref[0, r:r + SUBLANES, :] = jnp.broadcast_to(refs[len(parts)][...], (SUBLANES, LANES))

    return pl.pallas_call(
        body,
        name="pack_small_grads",
        grid_spec=pltpu.PrefetchScalarGridSpec(
            num_scalar_prefetch=1,
            grid=(1,),
            in_specs=[pl.BlockSpec(p.shape, lambda i, s: (0, 0)) for p in parts] + [pl.BlockSpec((1, 1), lambda i, s: (0, 0))],
            out_specs=pl.BlockSpec((1, m, LANES), lambda i, s: (s[0], 0, 0)),
        ),
        out_shape=jax.ShapeDtypeStruct((8, m, LANES), F32),
        compiler_params=_params(("arbitrary",)),
    )(jnp.reshape(slot, (1,)).astype(jnp.int32), *parts, loss)


def _unpack_small(land):
    gates = ("gate_x_w", "gate_a_w")
    packed_shape = lambda k: (LRU_BLOCKS * LRU_BW * LRU_BW // LANES, LANES) if k in gates else _SMALL_SHAPES[k]
    *sums, loss = _sum_gathered(land, [packed_shape(k) for k in _SMALL] + [(1, 1)], "sum_small_grads")
    return {k: _blocks_from_lanes(g) if k in gates else g for k, g in zip(_SMALL, sums)}, loss


def kernel(x, norm_in, w_in, conv_w, conv_b, gate_x_w, gate_x_b, gate_a_w, gate_a_b, lru_lambda, gn_gain, w_proj_a, w_proj_b, w_out, norm_final, loss_target, m_norm_in, m_w_in, m_conv_w, m_conv_b, m_gate_x_w, m_gate_x_b, m_gate_a_w, m_gate_a_b, m_lru_lambda, m_gn_gain, m_w_proj_a, m_w_proj_b, m_w_out, m_norm_final, v_norm_in, v_w_in, v_conv_w, v_conv_b, v_gate_x_w, v_gate_x_b, v_gate_a_w, v_gate_a_b, v_lru_lambda, v_gn_gain, v_w_proj_a, v_w_proj_b, v_w_out, v_norm_final):
    B, S, _ = x.shape
    T = B * S
    xi, yi, ci = _coords()
    chip = 2 * xi + yi

    cshard = D_MODEL // N_CHIPS
    mine = _cast_into_slot([w_in[0].reshape(2, D_MODEL // 2, 2 * D_MODEL)], "cast_w_in")
    plan = _gather_plan(3)
    pending_proj = []
    gshard = DK // N_CHIPS
    tiny = jnp.concatenate([conv_w[0], jnp.zeros((4, cshard), F32), jnp.pad(gn_gain[0], ((0, 4), (0, cshard - gshard)))],
                           axis=0).reshape(1, 2, SUBLANES, cshard)
    tiny_buf = lax.dynamic_update_slice(lax.empty((N_CHIPS, 2, SUBLANES, cshard), F32), tiny, (chip, 0, 0, 0))
    near_plan, pass_plan, far_plan = (_chip_gather_plan(stage, 2) for stage in ("near", "pass", "far"))
    (n_near, _), (n_pass, passed_on), (n_far, _) = (_chip_gather_copies(stage, 2) for stage in ("near", "pass", "far"))
    halves = set(range(n_pass)) - passed_on
    near_s, near_r, bufs, near_token = _copies_start([mine[0], tiny_buf], near_plan, n_near, "gather_near_start")

    def in_proj(x2d, g_in, meanwhile):
        as_w = lambda b: b[0].reshape(N_CHIPS, D_MODEL, 2 * D_MODEL)
        slot_x, slot_y, slot_d = 2 * (1 - xi) + yi, 2 * xi + (1 - yi), 2 * (1 - xi) + (1 - yi)
        ids = lambda *chips: jnp.stack(chips).astype(jnp.int32)
        proj, hb, ht = _inproj_first(x2d, g_in, as_w(bufs), ids(chip), "inproj_own", (near_token, *meanwhile))
        got = _copies_wait(near_s, near_r, bufs, proj, near_plan, "gather_near_wait")
        pass_s, pass_r, got, pass_token = _copies_start(got, pass_plan, n_pass, "gather_pass_start")
        mine_proj = _cast_into_slot([w[0].reshape(2, cshard // 2, D_MODEL) for w in (w_proj_a, w_proj_b, w_out)],
                                    "cast_w_proj", (pass_token,))
        got = _copies_wait(pass_s, pass_r, got, mine_proj[0], pass_plan, "gather_pass_wait_halves", only=halves)
        proj = _inproj_more(hb, as_w(got), ids(slot_x, slot_y), proj, "inproj_near")
        got = _copies_wait(pass_s, pass_r, got, proj, pass_plan, "gather_pass_wait_far", only=passed_on)
        far_s, far_r, got, far_token = _copies_start(got, far_plan, n_far, "gather_far_start")
        pending_proj.append(_copies_start(mine_proj, plan, 9, "gather_proj_start", (far_token,)))
        got = _copies_wait(far_s, far_r, got, pending_proj[0][3], far_plan, "gather_far_wait")
        proj = _inproj_more(hb, as_w(got), ids(slot_d), proj, "inproj_far")
        tiny_all = got[1].reshape(N_CHIPS, 2 * SUBLANES, cshard)
        conv_w_full = jnp.transpose(tiny_all[:, 0:CONV, :], (1, 0, 2)).reshape(CONV, D_MODEL)
        gain_full = jnp.transpose(tiny_all[:, 8:8 + HEADS, :gshard], (1, 0, 2)).reshape(HEADS, DK)
        return proj, ht, as_w(got), conv_w_full, gain_full

    def proj_weights(after):
        s_sems, r_sems, pbufs, _ = pending_proj[0]
        got = _copies_wait(s_sems, r_sems, pbufs, after, plan, "gather_proj_wait")
        return [b.reshape(D_MODEL, D_MODEL) for b in got]

    weights = dict(norm_in=norm_in, w_in=w_in, conv_w=conv_w, conv_b=conv_b, gate_x_w=gate_x_w, gate_x_b=gate_x_b,
                   gate_a_w=gate_a_w, gate_a_b=gate_a_b, lru_lambda=lru_lambda, gn_gain=gn_gain, w_proj_a=w_proj_a,
                   w_proj_b=w_proj_b, w_out=w_out, norm_final=norm_final)
    ms = dict(norm_in=m_norm_in, w_in=m_w_in, conv_w=m_conv_w, conv_b=m_conv_b, gate_x_w=m_gate_x_w,
              gate_x_b=m_gate_x_b, gate_a_w=m_gate_a_w, gate_a_b=m_gate_a_b, lru_lambda=m_lru_lambda, gn_gain=m_gn_gain,
              w_proj_a=m_w_proj_a, w_proj_b=m_w_proj_b, w_out=m_w_out, norm_final=m_norm_final)
    vs = dict(norm_in=v_norm_in, w_in=v_w_in, conv_w=v_conv_w, conv_b=v_conv_b, gate_x_w=v_gate_x_w,
              gate_x_b=v_gate_x_b, gate_a_w=v_gate_a_w, gate_a_b=v_gate_a_b, lru_lambda=v_lru_lambda, gn_gain=v_gn_gain,
              w_proj_a=v_w_proj_a, w_proj_b=v_w_proj_b, w_out=v_w_out, norm_final=v_norm_final)
    names = list(weights)
    grads, delta, new_m, new_v = {}, {}, {}, {}

    def update_big(keys, g, half, prev, name, deps=()):
        two = lambda a: a.reshape(a.shape[1], a.shape[2])
        res = _adamw_halves([two(weights[k]) for k in keys], g, [two(ms[k]) for k in keys], [two(vs[k]) for k in keys],
                            half, prev, name, deps)
        for k, (gk, d, mn, vn) in zip(keys, res):
            shp = weights[k].shape
            grads[k], delta[k], new_m[k], new_v[k] = gk.reshape(shp), d.reshape(shp), mn.reshape(shp), vn.reshape(shp)
        return res

    def proj_done(g_proj, deps):
        g4 = g_proj.reshape(2, 3, D_MODEL // (2 * N_CHIPS), D_MODEL)
        return update_big(("w_proj_a", "w_proj_b", "w_out"), g4, None, None, "adamw_proj", deps)[-1][1]

    def w_in_own(owns, got, dest_sets):
        two = lambda a: a.reshape(a.shape[1], a.shape[2])
        return _sum_adamw_own(owns, got, dest_sets, two(w_in), two(m_w_in), two(v_w_in), "adamw_w_in_own")

    def w_in_done(g_in, prev):
        g4 = g_in.reshape(2, 1, D_MODEL // 2, 2 * D_MODEL)
        return update_big(("w_in",), g4, 1 - ci, [prev], "adamw_w_in_other")[0]

    reduce = _GradReduce(proj_done)
    grad_x, dgin = _local_grads(
        x.reshape(T, D_MODEL), loss_target.reshape(T, D_MODEL), B, S, norm_in, in_proj, conv_b,
        gate_x_w[0], gate_x_b, gate_a_w[0], gate_a_b, lru_lambda, proj_weights,
        norm_final.reshape(1, D_MODEL), reduce)

    (gsm, loss), g_norm_in = reduce.finish(dgin.reshape(SUBLANES, LANES), w_in_own, w_in_done)
    loss = loss[0, 0]
    gsm["norm_in"] = g_norm_in
    gsm["conv_w"] = lax.dynamic_slice_in_dim(gsm["conv_w"], chip * cshard, cshard, axis=1)
    gsm["gn_gain"] = lax.dynamic_slice_in_dim(gsm["gn_gain"], chip * gshard, gshard, axis=1)
    smalls = [k for k in names if k not in delta]

    def view(a):
        return a.reshape(1, -1) if a.ndim == 1 else (a.reshape(a.shape[1:]) if a.ndim > 2 else a)

    ds, mns, vns = _adamw_small([view(weights[k]) for k in smalls], [gsm[k].reshape(view(weights[k]).shape) for k in smalls],
                                [view(ms[k]) for k in smalls], [view(vs[k]) for k in smalls], "adamw_small")
    for k, d, mn, vn in zip(smalls, ds, mns, vns):
        shp = weights[k].shape
        grads[k], delta[k], new_m[k], new_v[k] = gsm[k].reshape(shp), d.reshape(shp), mn.reshape(shp), vn.reshape(shp)

    return (loss, grad_x.reshape(B, S, D_MODEL), *[grads[k] for k in names], *[delta[k] for k in names],
            *[new_m[k] for k in names], *[new_v[k] for k in names])
```

```python
import jax
import jax.numpy as jnp
from jax import lax
from jax.experimental import pallas as pl
from jax.experimental.pallas import tpu as pltpu

F32 = jnp.float32
_MXU = jnp.bfloat16

D_MODEL = 1024
N_GROUPS = 8
HEADS = 4
DK = 256
CHUNK = 128
CONV = 4
LRU_BLOCKS = 16
LRU_BW = 64
LRU_C = 8.0
ROPE_THETA = 10000.0
EPS = 1e-6
CW = 256
N_CT = D_MODEL // CW
N_CHIPS = 4
MESH = pl.DeviceIdType.MESH

ADAM_LR = 0.001
ADAM_B1 = 0.9
ADAM_B2 = 0.999
ADAM_EPS = 1e-08
ADAM_WD = 0.01
ADAM_STEP = 10

VMEM_LIMIT = 56 * 1024 * 1024

FIRST_PROJ_TILE = 1024
MORE_PROJ_TILE = 2048
SCAN_TILE = 1024
MID_TILE = 256
DX_TILE = 512
DW_COLS = 512
DW_LOADS = 4
RET_CHUNKS = 2
SUM_ROWS = 256
ADAMW_ROWS = 256
JOIN_PIECES = 1
PROJ_JOIN_PIECES = 4


def _c(v):
    return v.astype(_MXU)


def _dot(a, b):
    return lax.dot_general(a, b, (((1,), (0,)), ((), ())), preferred_element_type=F32)


def _dot_nt(a, b):
    return lax.dot_general(a, b, (((1,), (1,)), ((), ())), preferred_element_type=F32)


def _dot_tn(a, b):
    return lax.dot_general(a, b, (((0,), (0,)), ((), ())), preferred_element_type=F32)


def _sigmoid(z):
    return 0.5 * jnp.tanh(0.5 * z) + 0.5


ANY_SPEC = pl.BlockSpec(memory_space=pl.ANY)


def _after(body, n_in, deps):
    n_deps = len(deps)

    def wrapped(*refs):
        return body(*refs[:n_in], *refs[n_in + n_deps:])

    return wrapped


def _params(sem=None):
    if sem is None:
        return pltpu.CompilerParams(vmem_limit_bytes=VMEM_LIMIT)
    return pltpu.CompilerParams(vmem_limit_bytes=VMEM_LIMIT, dimension_semantics=sem)


def _inproj_first(x2d, g_in, w_all, chips, name, deps=()):
    T = x2d.shape[0]
    tm = min(FIRST_PROJ_TILE, T)
    n_i = T // tm

    def body(s_ref, *refs):
        x_ref, g_ref, w_ref = refs[:3]
        proj_ref, hb_ref, ht_ref, h_all = refs[-4:]
        i = pl.program_id(1)
        rows = pl.ds(pl.multiple_of(i * tm, tm), tm)

        @pl.when(pl.program_id(0) == 0)
        def _():
            x = x_ref[...]
            r = lax.rsqrt(jnp.mean(x * x, axis=-1, keepdims=True) + EPS)
            h = x * r * g_ref[...]
            hb = h.astype(h_all.dtype)
            h_all[rows, :] = hb
            hb_ref[...] = hb
            ht_ref[...] = h.T.astype(ht_ref.dtype)

        proj_ref[...] = _dot(h_all[rows, :], w_ref[0])

    first = lambda j, i: jnp.where(j == 0, i, n_i - 1)
    return pl.pallas_call(
        body,
        name=name,
        grid_spec=pltpu.PrefetchScalarGridSpec(
            num_scalar_prefetch=1,
            grid=(2 * chips.shape[0], n_i),
            in_specs=[
                pl.BlockSpec((tm, D_MODEL), lambda j, i, s: (first(j, i), 0)),
                pl.BlockSpec((1, D_MODEL), lambda j, i, s: (0, 0)),
                pl.BlockSpec((1, D_MODEL, D_MODEL), lambda j, i, s: (s[j // 2], 0, j % 2)),
            ] + [ANY_SPEC] * len(deps),
            out_specs=[
                pl.BlockSpec((tm, D_MODEL), lambda j, i, s: (i, 2 * s[j // 2] + j % 2)),
                pl.BlockSpec((tm, D_MODEL), lambda j, i, s: (first(j, i), 0)),
                pl.BlockSpec((D_MODEL, tm), lambda j, i, s: (0, first(j, i))),
            ],
            scratch_shapes=[pltpu.VMEM((T, D_MODEL), _MXU)],
        ),
        out_shape=[
            jax.ShapeDtypeStruct((T, N_GROUPS * D_MODEL), F32),
            jax.ShapeDtypeStruct((T, D_MODEL), _MXU),
            jax.ShapeDtypeStruct((D_MODEL, T), _MXU),
        ],
        compiler_params=_params(("arbitrary", "arbitrary")),
    )(chips, x2d, g_in, w_all, *deps)


def _inproj_more(hb, w_all, chips, proj, name):
    T = hb.shape[0]
    tm = min(MORE_PROJ_TILE, T)

    def body(s_ref, hb_hbm, w_ref, prev_ref, proj_ref, h_all, sem):
        @pl.when((pl.program_id(0) == 0) & (pl.program_id(1) == 0))
        def _():
            cp = pltpu.make_async_copy(hb_hbm, h_all, sem)
            cp.start()
            cp.wait()

        rows = pl.ds(pl.multiple_of(pl.program_id(1) * tm, tm), tm)
        proj_ref[...] = _dot(h_all[rows, :], w_ref[0])

    return pl.pallas_call(
        body,
        name=name,
        grid_spec=pltpu.PrefetchScalarGridSpec(
            num_scalar_prefetch=1,
            grid=(2 * chips.shape[0], T // tm),
            in_specs=[
                ANY_SPEC,
                pl.BlockSpec((1, D_MODEL, D_MODEL), lambda j, i, s: (s[j // 2], 0, j % 2)),
                ANY_SPEC,
            ],
            out_specs=pl.BlockSpec((tm, D_MODEL), lambda j, i, s: (i, 2 * s[j // 2] + j % 2)),
            scratch_shapes=[pltpu.VMEM((T, D_MODEL), hb.dtype), pltpu.SemaphoreType.DMA],
        ),
        out_shape=jax.ShapeDtypeStruct(proj.shape, F32),
        input_output_aliases={3: 0},
        compiler_params=_params(("arbitrary", "arbitrary")),
    )(chips, hb, w_all, proj)


def _scan_fwd(a, u):
    n = a.shape[0]
    row = lax.broadcasted_iota(jnp.int32, a.shape, 0)
    s = 1
    while s < n:
        m = row >= s
        u = u + a * jnp.where(m, pltpu.roll(u, s, 0), 0.0)
        a = a * jnp.where(m, pltpu.roll(a, s, 0), 1.0)
        s *= 2
    return a, u


def _scan_bwd(b, g):
    n = b.shape[0]
    row = lax.broadcasted_iota(jnp.int32, b.shape, 0)
    s = 1
    while s < n:
        m = row < n - s
        g = g + b * jnp.where(m, pltpu.roll(g, n - s, 0), 0.0)
        b = b * jnp.where(m, pltpu.roll(b, n - s, 0), 1.0)
        s *= 2
    return b, g


LANES = 128
SUBLANES = 8


def _scan_scratch(tc):
    by_lanes = pltpu.VMEM((CW // LANES, tc, LANES), F32)
    return [by_lanes, by_lanes, pltpu.VMEM((tc // SUBLANES, CW), F32), pltpu.VMEM((tc, CW), F32)]


def _scan_tile(a, u, edge, la_ref, lh_ref, c_ref, dst_ref, reverse):
    n, w = a.shape
    groups = n // SUBLANES
    a3 = a.reshape(groups, SUBLANES, w)
    u3 = u.reshape(groups, SUBLANES, w)
    row = lax.broadcasted_iota(jnp.int32, a3.shape, 1)
    for s in (1, 2, 4):
        m = (row < SUBLANES - s) if reverse else (row >= s)
        shift = SUBLANES - s if reverse else s
        u3 = u3 + a3 * jnp.where(m, pltpu.roll(u3, shift, 1), 0.0)
        a3 = a3 * jnp.where(m, pltpu.roll(a3, shift, 1), 1.0)
    al = a3.reshape(n, w)
    hl = u3.reshape(n, w)
    blocks = w // LANES
    for q in range(blocks):
        la_ref[q] = al[:, q * LANES:(q + 1) * LANES]
        lh_ref[q] = hl[:, q * LANES:(q + 1) * LANES]
    ends = pl.ds(0 if reverse else SUBLANES - 1, groups, stride=SUBLANES)
    end_a = jnp.concatenate([la_ref.at[q][ends, :] for q in range(blocks)], axis=-1)
    end_h = jnp.concatenate([lh_ref.at[q][ends, :] for q in range(blocks)], axis=-1)
    prod, part = (_scan_bwd if reverse else _scan_fwd)(end_a, end_h)
    total = part + prod * edge
    g_row = lax.broadcasted_iota(jnp.int32, total.shape, 0)
    if reverse:
        c_ref[...] = jnp.where(g_row == groups - 1, edge, pltpu.roll(total, groups - 1, 0))
    else:
        c_ref[...] = jnp.where(g_row == 0, edge, pltpu.roll(total, 1, 0))
    for g in range(groups):
        rows = slice(g * SUBLANES, (g + 1) * SUBLANES)
        for q in range(blocks):
            cols = slice(q * LANES, (q + 1) * LANES)
            dst_ref[rows, cols] = lh_ref[q, rows, :] + la_ref[q, rows, :] * c_ref[g:g + 1, cols]


def _softplus_neg(lam):
    z = -lam
    return jnp.maximum(z, 0.0) + jnp.log1p(jnp.exp(-jnp.abs(z)))


def _lru_gates(xc, wx_ref, wa_ref, bx_ref, ba_ref, lam_ref):
    xcb = _c(xc)
    i_t = _sigmoid(_dot(xcb, wx_ref[0]) + bx_ref[...])
    r_t = _sigmoid(_dot(xcb, wa_ref[0]) + ba_ref[...])
    sp = _softplus_neg(lam_ref[...])
    log_a = (-LRU_C) * r_t * sp
    a = jnp.exp(log_a)
    mult = jnp.sqrt(1.0 - a * a)
    return xcb, i_t, r_t, sp, a, mult


def _conv_from_ext(ext_ref, xa, cw_ref, cb_ref, tc):
    return (cb_ref[...] + cw_ref[3:4, :] * xa + cw_ref[2:3, :] * ext_ref[7:7 + tc, :]
            + cw_ref[1:2, :] * ext_ref[6:6 + tc, :] + cw_ref[0:1, :] * ext_ref[5:5 + tc, :])


def _lru_fwd(proj, conv_w, conv_b, wx_bd, wa_bd, bx, ba, lam, B, S):
    T = B * S
    tc = min(SCAN_TILE, S)
    nt = S // tc
    h8 = tc // 8

    def body(xa_ref, halo_ref, ga_ref, cw_ref, cb_ref, wx_ref, wa_ref, bx_ref, ba_ref, lam_ref,
             h_ref, ya_ref, ext_ref, carry_ref, la_ref, lh_ref, c_ref):
        t = pl.program_id(2)

        @pl.when(t == 0)
        def _():
            carry_ref[...] = jnp.zeros_like(carry_ref)

        xa = xa_ref[...]
        ext_ref[0:8, :] = jnp.where(t == 0, 0.0, halo_ref[...])
        ext_ref[8:8 + tc, :] = xa
        xc = _conv_from_ext(ext_ref, xa, cw_ref, cb_ref, tc)
        _, i_t, _, _, a, mult = _lru_gates(xc, wx_ref, wa_ref, bx_ref, ba_ref, lam_ref)
        u = mult * (i_t * xc)
        _scan_tile(a, u, carry_ref[7:8, :], la_ref, lh_ref, c_ref, h_ref, False)
        h = h_ref[...]
        carry_ref[...] = h[tc - 8:tc, :]
        ga = ga_ref[...]
        ya_ref[...] = (ga * _sigmoid(ga) * h).astype(ya_ref.dtype)

    row = lambda b, t: b * nt + t
    vec = pl.BlockSpec((1, CW), lambda b, c, t: (0, c))
    mat = pl.BlockSpec((1, CW, CW), lambda b, c, t: (c, 0, 0))
    return pl.pallas_call(
        body,
        name="lru_fwd",
        grid=(B, N_CT, nt),
        in_specs=[
            pl.BlockSpec((tc, CW), lambda b, c, t: (row(b, t), c)),
            pl.BlockSpec((8, CW), lambda b, c, t: (jnp.maximum(row(b, t) * h8 - 1, 0), c)),
            pl.BlockSpec((tc, CW), lambda b, c, t: (row(b, t), N_CT + c)),
            pl.BlockSpec((CONV, CW), lambda b, c, t: (0, c)),
            vec, mat, mat, vec, vec, vec,
        ],
        out_specs=[
            pl.BlockSpec((tc, CW), lambda b, c, t: (row(b, t), c)),
            pl.BlockSpec((tc, CW), lambda b, c, t: (row(b, t), c)),
        ],
        out_shape=[
            jax.ShapeDtypeStruct((T, D_MODEL), F32),
            jax.ShapeDtypeStruct((T, D_MODEL), _MXU),
        ],
        scratch_shapes=[pltpu.VMEM((tc + 8, CW), F32), pltpu.VMEM((8, CW), F32)] + _scan_scratch(tc)[:3],
        compiler_params=_params(("parallel", "parallel", "arbitrary")),
    )(proj, proj, proj, conv_w, conv_b, wx_bd, wa_bd, bx, ba, lam)


def _lru_bwd(dya, proj, hlru, conv_w, conv_b, wx_bd, wa_bd, bx, ba, lam, B, S, deps=()):
    T = B * S
    tc = min(SCAN_TILE, S)
    nt = S // tc
    h8 = tc // 8

    def body(dya_ref, xa_ref, xhalo_ref, ga_ref, h_ref, hhalo_ref, cw_ref, cb_ref, wx_ref, wa_ref, bx_ref, ba_ref,
             lam_ref, dxa_ref, dga_ref, dcw_ref, dcb_ref, dwx_ref, dwa_ref, dbx_ref, dba_ref, dlam_ref,
             ext_ref, ext2_ref, carry_ref, dhalo_ref, la_ref, lh_ref, c_ref, dh_ref, accx_ref, acca_ref):
        b = pl.program_id(1)
        t = pl.program_id(2)
        tt = nt - 1 - t

        @pl.when(t == 0)
        def _():
            carry_ref[...] = jnp.zeros_like(carry_ref)
            dhalo_ref[...] = jnp.zeros_like(dhalo_ref)

        @pl.when((t == 0) & (b == 0))
        def _():
            for r in (dcw_ref, dcb_ref, accx_ref, acca_ref, dbx_ref, dba_ref, dlam_ref):
                r[...] = jnp.zeros_like(r)

        xa = xa_ref[...]
        ext_ref[0:8, :] = jnp.where(tt == 0, 0.0, xhalo_ref[...])
        ext_ref[8:8 + tc, :] = xa
        xc = _conv_from_ext(ext_ref, xa, cw_ref, cb_ref, tc)
        xcb, i_t, r_t, sp, a, mult = _lru_gates(xc, wx_ref, wa_ref, bx_ref, ba_ref, lam_ref)

        h = h_ref[...]
        ga = ga_ref[...]
        dya_t = dya_ref[...]
        sg = _sigmoid(ga)
        dga_ref[...] = (dya_t * h * (sg * (1.0 + ga * (1.0 - sg)))).astype(dga_ref.dtype)
        dlru = dya_t * (ga * sg)

        row = lax.broadcasted_iota(jnp.int32, a.shape, 0)
        coef = jnp.where(row == tc - 1, 1.0, pltpu.roll(a, tc - 1, 0))
        _scan_tile(coef, dlru, carry_ref[0:1, :], la_ref, lh_ref, c_ref, dh_ref, True)
        dh = dh_ref[...]
        ext2_ref[0:tc, :] = a * dh
        carry_ref[...] = ext2_ref[0:8, :]

        ext2_ref[0:8, :] = jnp.where(tt == 0, 0.0, hhalo_ref[...])
        ext2_ref[8:8 + tc, :] = h
        hprev = ext2_ref[7:7 + tc, :]

        da = dh * hprev
        ix = i_t * xc
        dmult = dh * ix
        di = dh * mult * xc
        dxc = dh * mult * i_t
        dlog_a = da * a - dmult * (a * a) / mult
        dr = dlog_a * ((-LRU_C) * sp)
        dlam_ref[...] += jnp.sum(dlog_a * r_t, axis=0, keepdims=True) * (LRU_C * _sigmoid(-lam_ref[...]))
        dza = dr * r_t * (1.0 - r_t)
        dzx = di * i_t * (1.0 - i_t)
        dzab = _c(dza)
        dzxb = _c(dzx)
        dxc = dxc + _dot_nt(dzxb, wx_ref[0]) + _dot_nt(dzab, wa_ref[0])
        accx_ref[...] += _dot_tn(xcb, dzxb)
        acca_ref[...] += _dot_tn(xcb, dzab)
        dbx_ref[...] += jnp.sum(dzx, axis=0, keepdims=True)
        dba_ref[...] += jnp.sum(dza, axis=0, keepdims=True)

        dcb_ref[...] += jnp.sum(dxc, axis=0, keepdims=True)
        dcw_ref[3:4, :] += jnp.sum(dxc * xa, axis=0, keepdims=True)
        dcw_ref[2:3, :] += jnp.sum(dxc * ext_ref[7:7 + tc, :], axis=0, keepdims=True)
        dcw_ref[1:2, :] += jnp.sum(dxc * ext_ref[6:6 + tc, :], axis=0, keepdims=True)
        dcw_ref[0:1, :] += jnp.sum(dxc * ext_ref[5:5 + tc, :], axis=0, keepdims=True)
        ext2_ref[0:tc, :] = dxc
        ext2_ref[tc:tc + 8, :] = dhalo_ref[...]
        dxa = (cw_ref[3:4, :] * dxc + cw_ref[2:3, :] * ext2_ref[1:1 + tc, :]
               + cw_ref[1:2, :] * ext2_ref[2:2 + tc, :] + cw_ref[0:1, :] * ext2_ref[3:3 + tc, :])
        dxa_ref[...] = dxa.astype(dxa_ref.dtype)
        dhalo_ref[...] = ext2_ref[0:8, :]

        @pl.when((b == B - 1) & (t == nt - 1))
        def _():
            lane_block = lax.broadcasted_iota(jnp.int32, (LRU_BW, CW), 1) // LRU_BW
            for acc_ref, out_ref in ((accx_ref, dwx_ref), (acca_ref, dwa_ref)):
                diag = jnp.zeros((LRU_BW, CW), F32)
                for j in range(CW // LRU_BW):
                    diag = jnp.where(lane_block == j, acc_ref[j * LRU_BW:(j + 1) * LRU_BW, :], diag)
                for q in range(CW // LANES):
                    out_ref[0, q] = diag[:, q * LANES:(q + 1) * LANES]

    row_of = lambda b, t: b * nt + (nt - 1 - t)
    tile = lambda off: pl.BlockSpec((tc, CW), lambda c, b, t: (row_of(b, t), off + c))
    halo = pl.BlockSpec((8, CW), lambda c, b, t: (jnp.maximum(row_of(b, t) * h8 - 1, 0), c))
    vec = pl.BlockSpec((1, CW), lambda c, b, t: (0, c))
    mat = pl.BlockSpec((1, CW, CW), lambda c, b, t: (c, 0, 0))
    cwspec = pl.BlockSpec((CONV, CW), lambda c, b, t: (0, c))
    diag = pl.BlockSpec((1, CW // LANES, LRU_BW, LANES), lambda c, b, t: (c, 0, 0, 0))
    return pl.pallas_call(
        _after(body, 13, deps),
        name="lru_bwd",
        grid=(N_CT, B, nt),
        in_specs=[tile(0), tile(0), halo, tile(N_CT), tile(0), halo, cwspec, vec, mat, mat, vec, vec, vec]
        + [ANY_SPEC] * len(deps),
        out_specs=[tile(0), tile(0), cwspec, vec, diag, diag, vec, vec, vec],
        out_shape=[
            jax.ShapeDtypeStruct((T, D_MODEL), _MXU),
            jax.ShapeDtypeStruct((T, D_MODEL), _MXU),
            jax.ShapeDtypeStruct((CONV, D_MODEL), F32),
            jax.ShapeDtypeStruct((1, D_MODEL), F32),
            jax.ShapeDtypeStruct((N_CT, CW // LANES, LRU_BW, LANES), F32),
            jax.ShapeDtypeStruct((N_CT, CW // LANES, LRU_BW, LANES), F32),
            jax.ShapeDtypeStruct((1, D_MODEL), F32),
            jax.ShapeDtypeStruct((1, D_MODEL), F32),
            jax.ShapeDtypeStruct((1, D_MODEL), F32),
        ],
        scratch_shapes=[pltpu.VMEM((tc + 8, CW), F32), pltpu.VMEM((tc + 8, CW), F32),
                        pltpu.VMEM((8, CW), F32), pltpu.VMEM((8, CW), F32)] + _scan_scratch(tc)
        + [pltpu.VMEM((CW, CW), F32), pltpu.VMEM((CW, CW), F32)],
        compiler_params=_params(("parallel", "arbitrary", "arbitrary")),
    )(dya, proj, proj, proj, hlru, hlru, conv_w, conv_b, wx_bd, wa_bd, bx, ba, lam, *deps)


def _retention_tables(S):
    half = DK // 2
    freqs = ROPE_THETA ** (-jnp.arange(half, dtype=F32) / half)
    ang = jnp.arange(S, dtype=F32)[:, None] * freqs[None, :]
    log_g = jnp.log1p(-(2.0 ** (-5.0 - jnp.arange(HEADS, dtype=F32))))
    idx = jnp.arange(CHUNK, dtype=F32)
    diff = idx[:, None] - idx[None, :]
    inner = jnp.where(diff >= 0, jnp.exp(jnp.maximum(diff, 0.0)[None] * log_g[:, None, None]), 0.0)
    cross = jnp.exp((idx[None, :] + 1.0) * log_g[:, None])[:, :, None]
    state = jnp.exp((CHUNK - 1.0 - idx[None, :]) * log_g[:, None])[:, :, None]
    gam = jnp.broadcast_to(jnp.exp(CHUNK * log_g)[:, None, None], (HEADS, 1, DK))
    return jnp.cos(ang), jnp.sin(ang), inner, cross, state, gam


def _rot(x, cos, sin):
    half = DK // 2
    x1, x2 = x[:, :half], x[:, half:]
    return jnp.concatenate([x1 * cos - x2 * sin, x1 * sin + x2 * cos], axis=-1)


def _rot_t(y, cos, sin):
    half = DK // 2
    y1, y2 = y[:, :half], y[:, half:]
    return jnp.concatenate([y1 * cos + y2 * sin, y2 * cos - y1 * sin], axis=-1)


def _groupnorm(o):
    mu = jnp.mean(o, axis=-1, keepdims=True)
    oc = o - mu
    rs = lax.rsqrt(jnp.mean(oc * oc, axis=-1, keepdims=True) + EPS)
    return oc * rs, rs


def _ret_specs(B, chunk_of):
    rows = RET_CHUNKS * CHUNK
    qkv = lambda g: pl.BlockSpec((B, rows, D_MODEL), lambda c: (0, chunk_of(c), g))
    act = pl.BlockSpec((B, rows, D_MODEL), lambda c: (0, chunk_of(c), 0))
    rope = pl.BlockSpec((rows, DK // 2), lambda c: (chunk_of(c), 0))
    dmat = pl.BlockSpec((HEADS, CHUNK, CHUNK), lambda c: (0, 0, 0))
    dvec = pl.BlockSpec((HEADS, CHUNK, 1), lambda c: (0, 0, 0))
    hrow = pl.BlockSpec((HEADS, 1, DK), lambda c: (0, 0, 0))
    rst = pl.BlockSpec((RET_CHUNKS, B, HEADS, DK, DK), lambda c: (chunk_of(c), 0, 0, 0, 0))
    return qkv, act, rope, dmat, dvec, hrow, rst


def _ret_fwd(proj, tables, gain3, B, S):
    T = B * S
    nc = S // CHUNK
    cos, sin, dmat_t, cd_t, sd_t, gam_t = tables

    def body(q_ref, k_ref, v_ref, gb_ref, cos_ref, sin_ref, dm_ref, cd_ref, sd_ref, gam_ref, gain_ref,
             o_ref, yb_ref, rs_ref, state_ref):
        @pl.when(pl.program_id(0) == 0)
        def _():
            state_ref[...] = jnp.zeros_like(state_ref)

        for cc, b, h in [(cc, b, h) for cc in range(RET_CHUNKS) for b in range(B) for h in range(HEADS)]:
            rows = slice(cc * CHUNK, (cc + 1) * CHUNK)
            cos_t, sin_t = cos_ref[rows, :], sin_ref[rows, :]
            cols = slice(h * DK, (h + 1) * DK)
            qb = _c(_rot(q_ref[b, rows, cols], cos_t, sin_t))
            kb = _c(_rot(k_ref[b, rows, cols], cos_t, sin_t) * (DK ** -0.5))
            v = v_ref[b, rows, cols]
            state = state_ref[b, h]
            sb = _c(state)
            rs_ref[cc, b, h] = sb
            scores = _dot_nt(qb, kb) * dm_ref[h]
            o = _dot(_c(scores), _c(v)) + _dot(qb, sb) * cd_ref[h]
            state_ref[b, h] = gam_ref[h] * state + _dot_tn(kb, _c(v * sd_ref[h]))
            o_ref[b, rows, cols] = o
            n, _ = _groupnorm(o)
            gb = gb_ref[b, rows, cols]
            yb_ref[b, rows, cols] = (gb * _sigmoid(gb) * (n * gain_ref[h])).astype(yb_ref.dtype)

    qkv, act, rope, dmat, dvec, hrow, rst = _ret_specs(B, lambda c: c)
    proj3 = proj.reshape(B, S, proj.shape[1])
    o_pre, yb, states = pl.pallas_call(
        body,
        name="ret_fwd",
        grid=(nc // RET_CHUNKS,),
        in_specs=[qkv(2), qkv(3), qkv(4), qkv(5), rope, rope, dmat, dvec, dvec, hrow, hrow],
        out_specs=[act, act, rst],
        out_shape=[
            jax.ShapeDtypeStruct((B, S, D_MODEL), F32),
            jax.ShapeDtypeStruct((B, S, D_MODEL), _MXU),
            jax.ShapeDtypeStruct((nc, B, HEADS, DK, DK), _MXU),
        ],
        scratch_shapes=[pltpu.VMEM((B, HEADS, DK, DK), F32)],
        compiler_params=_params(("arbitrary",)),
    )(proj3, proj3, proj3, proj3, cos, sin, dmat_t, cd_t, sd_t, gam_t, gain3)
    return o_pre.reshape(T, D_MODEL), yb.reshape(T, D_MODEL), states


def _ret_bwd(dyb, o_pre, proj, states, tables, gain3, B, S, deps=()):
    T = B * S
    nc = S // CHUNK
    cos, sin, dmat_t, cd_t, sd_t, gam_t = tables

    def body(dyb_ref, o_ref, q_ref, k_ref, v_ref, gb_ref, rs_ref, cos_ref, sin_ref, dm_ref, cd_ref, sd_ref, gam_ref,
             gain_ref, dr_ref, dgain_ref, dstate_ref):
        @pl.when(pl.program_id(0) == 0)
        def _():
            dstate_ref[...] = jnp.zeros_like(dstate_ref)
            dgain_ref[...] = jnp.zeros_like(dgain_ref)

        for cc, b, h in [(cc, b, h) for cc in reversed(range(RET_CHUNKS)) for b in range(B) for h in range(HEADS)]:
            rows = slice(cc * CHUNK, (cc + 1) * CHUNK)
            cos_t, sin_t = cos_ref[rows, :], sin_ref[rows, :]
            cols = slice(h * DK, (h + 1) * DK)
            gain = gain_ref[h]
            n, rs = _groupnorm(o_ref[b, rows, cols])
            gb = gb_ref[b, rows, cols]
            sg = _sigmoid(gb)
            dy = dyb_ref[b, rows, cols]
            part = lambda g: slice(g * D_MODEL + h * DK, g * D_MODEL + (h + 1) * DK)
            dr_ref[b, rows, part(3)] = (dy * (n * gain) * (sg * (1.0 + gb * (1.0 - sg)))).astype(dr_ref.dtype)
            dgn = dy * (gb * sg)
            dgain_ref[h] += jnp.sum(dgn * n, axis=0, keepdims=True)
            dn = dgn * gain
            do = rs * (dn - jnp.mean(dn, axis=-1, keepdims=True) - n * jnp.mean(dn * n, axis=-1, keepdims=True))

            qb = _c(_rot(q_ref[b, rows, cols], cos_t, sin_t))
            kb = _c(_rot(k_ref[b, rows, cols], cos_t, sin_t) * (DK ** -0.5))
            v = v_ref[b, rows, cols]
            vb = _c(v)
            vsb = _c(v * sd_ref[h])
            dob = _c(do)
            docb = _c(do * cd_ref[h])
            dmat = dm_ref[h]
            dstate = dstate_ref[b, h]
            dsb = _c(dstate)
            pb = _c(_dot_nt(qb, kb) * dmat)
            dsc = _c(_dot_nt(dob, vb) * dmat)
            dq = _dot(dsc, kb) + _dot_nt(docb, rs_ref[cc, b, h])
            dk = _dot_tn(dsc, qb) + _dot_nt(vsb, dsb)
            dv = _dot_tn(pb, dob) + _dot(kb, dsb) * sd_ref[h]
            dstate_ref[b, h] = gam_ref[h] * dstate + _dot_tn(qb, docb)
            dr_ref[b, rows, part(0)] = _rot_t(dq, cos_t, sin_t).astype(dr_ref.dtype)
            dr_ref[b, rows, part(1)] = (_rot_t(dk, cos_t, sin_t) * (DK ** -0.5)).astype(dr_ref.dtype)
            dr_ref[b, rows, part(2)] = dv.astype(dr_ref.dtype)

    n_steps = nc // RET_CHUNKS
    qkv, act, rope, dmat, dvec, hrow, rst = _ret_specs(B, lambda c: n_steps - 1 - c)
    wide = pl.BlockSpec((B, RET_CHUNKS * CHUNK, 4 * D_MODEL), lambda c: (0, n_steps - 1 - c, 0))
    proj3 = proj.reshape(B, S, proj.shape[1])
    dr, dgain = pl.pallas_call(
        _after(body, 14, deps),
        name="ret_bwd",
        grid=(n_steps,),
        in_specs=[act, act, qkv(2), qkv(3), qkv(4), qkv(5), rst, rope, rope, dmat, dvec, dvec, hrow, hrow]
        + [ANY_SPEC] * len(deps),
        out_specs=[wide, hrow],
        out_shape=[jax.ShapeDtypeStruct((B, S, 4 * D_MODEL), _MXU), jax.ShapeDtypeStruct((HEADS, 1, DK), F32)],
        scratch_shapes=[pltpu.VMEM((B, HEADS, DK, DK), F32)],
        compiler_params=_params(("arbitrary",)),
    )(dyb.reshape(B, S, D_MODEL), o_pre.reshape(B, S, D_MODEL), proj3, proj3, proj3, proj3, states, cos, sin, dmat_t,
      cd_t, sd_t, gam_t, gain3, *deps)
    return dr.reshape(T, 4 * D_MODEL), dgain


def _mid(ya, yb, proj, x2d, tgt2d, wpa, wpb, wout, g_fin):
    T = x2d.shape[0]
    tm = min(MID_TILE, T)
    n_steps = T // tm
    rows = D_MODEL // (2 * N_CHIPS)

    def body(ya_ref, yb_ref, ma_ref, mb_ref, x_ref, t_ref, gf_ref, wpa_hbm, wpb_hbm, wout_hbm,
             loss_ref, dx2_ref, dya_ref, dyb_ref, dm_ref, dgf_ref, gw_hbm, w_ref, acc_ref, sem):
        i = pl.program_id(0)

        @pl.when(i == 0)
        def _():
            loads = [pltpu.make_async_copy(src, w_ref.at[k], sem.at[k]) for k, src in enumerate((wpa_hbm, wpb_hbm, wout_hbm))]
            for cp in loads:
                cp.start()
            for cp in loads:
                cp.wait()
            acc_ref[...] = jnp.zeros_like(acc_ref)
            loss_ref[...] = jnp.zeros_like(loss_ref)
            dgf_ref[...] = jnp.zeros_like(dgf_ref)

        ya_t, yb_t = ya_ref[...], yb_ref[...]
        out_a = _dot(ya_t, w_ref[0])
        out_b = _dot(yb_t, w_ref[1])
        sa = _sigmoid(ma_ref[...])
        sb = _sigmoid(mb_ref[...])
        mgb = _c(sa * out_a + sb * out_b)
        x2 = x_ref[...] + _dot(mgb, w_ref[2])
        r2 = lax.rsqrt(jnp.mean(x2 * x2, axis=-1, keepdims=True) + EPS)
        nx = x2 * r2
        gf = gf_ref[...]
        err = nx * gf - t_ref[...]
        loss_ref[...] += 0.5 * jnp.sum(jnp.mean(err * err, axis=-1, keepdims=True), axis=0, keepdims=True)
        dy = err * (1.0 / D_MODEL)
        dgf_ref[...] += jnp.sum(dy * nx, axis=0, keepdims=True)
        dyg = dy * gf
        dx2 = r2 * (dyg - nx * jnp.mean(dyg * nx, axis=-1, keepdims=True))
        dx2_ref[...] = dx2
        dx2b = _c(dx2)
        dmg = _dot_nt(dx2b, w_ref[2])
        acc_ref[2] += _dot_tn(mgb, dx2b)
        dm_ref[:, :D_MODEL] = (dmg * out_a * sa * (1.0 - sa)).astype(dm_ref.dtype)
        dm_ref[:, D_MODEL:] = (dmg * out_b * sb * (1.0 - sb)).astype(dm_ref.dtype)
        dab = _c(dmg * sa)
        dbb = _c(dmg * sb)
        dya_ref[...] = _dot_nt(dab, w_ref[0])
        dyb_ref[...] = _dot_nt(dbb, w_ref[1])
        acc_ref[0] += _dot_tn(ya_t, dab)
        acc_ref[1] += _dot_tn(yb_t, dbb)

        @pl.when(i == n_steps - 1)
        def _():
            copies = [pltpu.make_async_copy(acc_ref.at[k, pl.ds((2 * p + hf) * rows, rows), :], gw_hbm.at[p, hf, k],
                                            sem.at[(k * N_CHIPS + p) * 2 + hf])
                      for k in range(3) for p in range(N_CHIPS) for hf in range(2)]
            for cp in copies:
                cp.start()
            for cp in copies:
                cp.wait()

    tile = lambda j: pl.BlockSpec((tm, D_MODEL), lambda i: (i, j))
    one = pl.BlockSpec((1, D_MODEL), lambda i: (0, 0))
    anyspec = pl.BlockSpec(memory_space=pl.ANY)
    return pl.pallas_call(
        body,
        name="mid",
        grid=(n_steps,),
        in_specs=[tile(0), tile(0), tile(6), tile(7), tile(0), tile(0), one, anyspec, anyspec, anyspec],
        out_specs=[pl.BlockSpec((1, 1), lambda i: (0, 0)), tile(0), tile(0), tile(0),
                   pl.BlockSpec((tm, 2 * D_MODEL), lambda i: (i, 0)), one, anyspec],
        out_shape=[
            jax.ShapeDtypeStruct((1, 1), F32),
            jax.ShapeDtypeStruct((T, D_MODEL), F32),
            jax.ShapeDtypeStruct((T, D_MODEL), F32),
            jax.ShapeDtypeStruct((T, D_MODEL), F32),
            jax.ShapeDtypeStruct((T, 2 * D_MODEL), _MXU),
            jax.ShapeDtypeStruct((1, D_MODEL), F32),
            jax.ShapeDtypeStruct((N_CHIPS, 2, 3, rows, D_MODEL), F32),
        ],
        scratch_shapes=[pltpu.VMEM((3, D_MODEL, D_MODEL), _MXU), pltpu.VMEM((3, D_MODEL, D_MODEL), F32),
                        pltpu.SemaphoreType.DMA((3 * N_CHIPS * 2,))],
        compiler_params=_params(("arbitrary",)),
    )(ya, yb, proj, proj, x2d, tgt2d, g_fin, wpa, wpb, wout)


def _inproj_bwd_dx(dparts, w_all, x2d, dx2, g_in, first, count, prev, name, deps=()):
    T = x2d.shape[0]
    tm = min(DX_TILE, T)
    n_d = len(dparts)
    groups = [(a, k) for a, d in enumerate(dparts) for k in range(d.shape[1] // D_MODEL)]
    dg_start = jnp.zeros((1, D_MODEL), F32) if prev is None else prev[1]
    carried = () if prev is None else (prev[0],)

    def body(*refs):
        d_refs = refs[:n_d]
        x_ref, dx2_ref, g_ref, dg0_ref, w_hbm = refs[n_d:n_d + 5]
        dx_ref, dg_ref, w_ref, sem = refs[-4:]

        def load(j):
            part = (j // 2, slice(None), pl.ds((j % 2) * D_MODEL, D_MODEL))
            return pltpu.make_async_copy(w_hbm.at[part], w_ref.at[part], sem.at[j])

        def tile(before_group):
            dh = jnp.zeros((tm, D_MODEL), F32)
            for j, (a, k) in enumerate(groups):
                before_group(j)
                dh = dh + _dot_nt(d_refs[a][:, k * D_MODEL:(k + 1) * D_MODEL],
                                  w_ref[j // 2, :, (j % 2) * D_MODEL:(j % 2 + 1) * D_MODEL])
            x = x_ref[...]
            r = lax.rsqrt(jnp.mean(x * x, axis=-1, keepdims=True) + EPS)
            nx = x * r
            dg_ref[...] += jnp.sum(dh * nx, axis=0, keepdims=True)
            dhg = dh * g_ref[...]
            dx_ref[...] = dx2_ref[...] + r * (dhg - nx * jnp.mean(dhg * nx, axis=-1, keepdims=True))

        first = pl.program_id(0) == 0

        @pl.when(first)
        def _():
            for j in range(len(groups)):
                load(j).start()
            dg_ref[...] = dg0_ref[...]
            tile(lambda j: load(j).wait())

        @pl.when(jnp.logical_not(first))
        def _():
            tile(lambda j: None)

    tile = pl.BlockSpec((tm, D_MODEL), lambda i: (first + i, 0))
    one = pl.BlockSpec((1, D_MODEL), lambda i: (0, 0))
    return pl.pallas_call(
        body,
        name=name,
        grid=(count,),
        in_specs=[pl.BlockSpec((tm, d.shape[1]), lambda i: (first + i, 0)) for d in dparts]
        + [tile, tile, one, one, ANY_SPEC] + [ANY_SPEC] * (len(carried) + len(deps)),
        out_specs=[tile, one],
        out_shape=[jax.ShapeDtypeStruct((T, D_MODEL), F32), jax.ShapeDtypeStruct((1, D_MODEL), F32)],
        input_output_aliases={n_d + 5: 0} if carried else {},
        scratch_shapes=[pltpu.VMEM(w_all.shape, w_all.dtype), pltpu.SemaphoreType.DMA((len(groups),))],
        compiler_params=_params(("arbitrary",)),
    )(*dparts, x2d, dx2, g_in, dg_start, w_all, *carried, *deps)


def _inproj_bwd_dw(ht, dparts, name, deps=()):
    T = ht.shape[1]
    tn = DW_COLS
    half = D_MODEL // 2
    per_chip = 2 * D_MODEL // tn
    n_d = len(dparts)
    tiles = [(a, t) for a, d in enumerate(dparts) for t in range(d.shape[1] // tn)]
    offs = [sum(d.shape[1] // tn for d in dparts[:a]) for a in range(n_d)]

    def body(*refs):
        ht_hbm = refs[0]
        d_refs = refs[1:1 + n_d]
        out_ref, ht_ref, sem = refs[-3:]
        t = pl.program_id(0)

        def load(k):
            cols = pl.ds(k * (T // DW_LOADS), T // DW_LOADS)
            return pltpu.make_async_copy(ht_hbm.at[:, cols], ht_ref.at[:, cols], sem.at[k])

        def store(g):
            out_ref[0, 0] = g[:half]
            out_ref[0, 1] = g[half:]

        @pl.when(t == 0)
        def _():
            for k in range(DW_LOADS):
                load(k).start()
            g = jnp.zeros((D_MODEL, tn), F32)
            for k in range(DW_LOADS):
                load(k).wait()
                tokens = slice(k * (T // DW_LOADS), (k + 1) * (T // DW_LOADS))
                g = g + _dot(ht_ref[:, tokens], d_refs[0][tokens, :])
            store(g)

        for a in range(n_d):
            lo, hi = max(offs[a], 1), offs[a] + dparts[a].shape[1] // tn

            @pl.when((t >= lo) & (t < hi))
            def _(a=a):
                store(_dot(ht_ref[...], d_refs[a][...]))

    def dspec(a):
        n_a = dparts[a].shape[1] // tn
        return pl.BlockSpec((T, tn), lambda t: (0, jnp.clip(t - offs[a], 0, n_a - 1)))

    return pl.pallas_call(
        body,
        name=name,
        grid=(len(tiles),),
        in_specs=[ANY_SPEC] + [dspec(a) for a in range(n_d)] + [ANY_SPEC] * len(deps),
        out_specs=pl.BlockSpec((1, 2, half, tn), lambda t: (t // per_chip, 0, 0, t % per_chip)),
        out_shape=jax.ShapeDtypeStruct((len(tiles) // per_chip, 2, half, 2 * D_MODEL), F32),
        scratch_shapes=[pltpu.VMEM(ht.shape, ht.dtype), pltpu.SemaphoreType.DMA((DW_LOADS,))],
        compiler_params=_params(("arbitrary",)),
    )(ht, *dparts, *deps)


def _coords():
    return lax.axis_index("x"), lax.axis_index("y"), lax.axis_index("c")


def _other_chips(x, y):
    return [(1 - x, y), (x, 1 - y), (1 - x, 1 - y)]


def _chunks(rows, n):
    size = rows // n
    return [pl.ds(q * size, size) for q in range(n)]


HBM_SPEC = pl.BlockSpec(memory_space=pltpu.HBM)
SEM_SPEC = pl.BlockSpec(memory_space=pltpu.SEMAPHORE)
DATAFLOW = pltpu.SideEffectType.DATAFLOW_SIDE_EFFECTING


def _copies_start(bufs, plan, n_copies, name, deps=()):
    n = len(bufs)
    n_deps = len(deps)

    def body(*refs):
        ins = refs[:n]
        send_sems, recv_sems = refs[n + n_deps], refs[n + n_deps + 1]
        token = refs[-1]
        for k, send, _ in plan(ins):
            if send is not None:
                src, dst, dev, pred = send
                cp = pltpu.make_async_remote_copy(src_ref=src, dst_ref=dst, send_sem=send_sems.at[k],
                                                  recv_sem=recv_sems.at[k], device_id=dev, device_id_type=MESH)
                if pred is None:
                    cp.start()
                else:
                    pl.when(pred)(cp.start)
        token[...] = jnp.zeros_like(token)

    hbm = [pltpu.with_memory_space_constraint(b, pltpu.HBM) for b in bufs]
    outs = pl.pallas_call(
        body,
        name=name,
        in_specs=[HBM_SPEC] * n + [ANY_SPEC] * n_deps,
        out_specs=(SEM_SPEC, SEM_SPEC, *([HBM_SPEC] * n), pl.BlockSpec(memory_space=pltpu.VMEM)),
        out_shape=(pltpu.SemaphoreType.DMA((n_copies,)), pltpu.SemaphoreType.DMA((n_copies,)),
                   *[pltpu.HBM(b.shape, b.dtype) for b in bufs], jax.ShapeDtypeStruct((8, 128), F32)),
        input_output_aliases={a: 2 + a for a in range(n)},
        compiler_params=pltpu.CompilerParams(has_side_effects=DATAFLOW),
    )(*hbm, *deps)
    return outs[0], outs[1], list(outs[2:2 + n]), outs[-1]


def _copies_wait(send_sems, recv_sems, bufs, after, plan, name, only=None):
    n = len(bufs)

    def body(*refs):
        ins = refs[:n]
        s_sems, r_sems = refs[n], refs[n + 1]
        for k, send, recv in plan(ins):
            if only is not None and k not in only:
                continue
            if send is not None:
                src, dst, dev, pred = send
                cp = pltpu.make_async_remote_copy(src_ref=src, dst_ref=dst, send_sem=s_sems.at[k],
                                                  recv_sem=r_sems.at[k], device_id=dev, device_id_type=MESH)
                if pred is None:
                    cp.wait_send()
                else:
                    pl.when(pred)(cp.wait_send)
            if recv is not None:
                dst, pred = recv
                cp = pltpu.make_async_remote_copy(src_ref=dst, dst_ref=dst, send_sem=s_sems.at[k],
                                                  recv_sem=r_sems.at[k], device_id=_coords(), device_id_type=MESH)
                if pred is None:
                    cp.wait_recv()
                else:
                    pl.when(pred)(cp.wait_recv)

    outs = pl.pallas_call(
        body,
        name=name,
        in_specs=[HBM_SPEC] * n + [SEM_SPEC, SEM_SPEC, pl.BlockSpec(memory_space=pl.ANY)],
        out_specs=[HBM_SPEC] * n,
        out_shape=[pltpu.HBM(b.shape, b.dtype) for b in bufs],
        input_output_aliases={a: a for a in range(n)},
        compiler_params=pltpu.CompilerParams(has_side_effects=DATAFLOW),
    )(*bufs, send_sems, recv_sems, after)
    return list(outs)


def _gather_plan(n_bufs):
    def plan(refs):
        x, y, c = _coords()
        me = 2 * x + y
        out = []
        for k, (px, py) in enumerate(_other_chips(x, y)):
            for a in range(n_bufs):
                out.append((k * n_bufs + a, (refs[a].at[me], refs[a].at[me], (px, py, c), None),
                            (refs[a].at[2 * px + py], None)))
        return out
    return plan


def _cast_into_slot(ws, name, deps=()):
    n = len(ws)
    nt = 2

    def body(s_ref, *refs):
        outs = refs[len(refs) - n:]
        for a in range(n):
            outs[a][0] = refs[a][...].astype(outs[a].dtype)

    xi, yi, _ = _coords()
    return pl.pallas_call(
        body,
        name=name,
        grid_spec=pltpu.PrefetchScalarGridSpec(
            num_scalar_prefetch=1,
            grid=(2, nt),
            in_specs=[pl.BlockSpec((1, w.shape[1] // nt, w.shape[2]), lambda hf, i, s: (hf, i, 0)) for w in ws]
            + [ANY_SPEC] * len(deps),
            out_specs=[pl.BlockSpec((1, 1, w.shape[1] // nt, w.shape[2]), lambda hf, i, s: (s[0], hf, i, 0)) for w in ws],
        ),
        out_shape=[jax.ShapeDtypeStruct((N_CHIPS,) + w.shape, _MXU) for w in ws],
        compiler_params=_params(("parallel", "parallel")),
    )((2 * xi + yi).reshape(1).astype(jnp.int32), *ws, *deps)


def _chip_gather_plan(stage, n_bufs):
    def plan(refs):
        x, y, c = _coords()
        me = 2 * x + y
        near = [(1 - x, y), (x, 1 - y)]
        slots = [2 * (1 - x) + y, 2 * x + (1 - y), 2 * (1 - x) + (1 - y)]
        sibling = (x, y, 1 - c)
        pass_to = (jnp.where(c == 0, x, 1 - x), jnp.where(c == 0, 1 - y, y), c)
        pass_slot = jnp.where(c == 0, slots[0], slots[1])
        out = []

        def move(src_slot, to, land_slot, land_core, pieces):
            for a, buf in enumerate(refs):
                for rows in _chunks(buf.shape[2], pieces[a]):
                    out.append((len(out), (buf.at[src_slot, c, rows], buf.at[src_slot, c, rows], to, None),
                                (buf.at[land_slot, land_core, rows], None)))

        if stage == "near":
            for k, chip in enumerate(near):
                move(me, (*chip, c), slots[k], c, NEAR_PIECES[:n_bufs])
        elif stage == "pass":
            move(pass_slot, pass_to, slots[2], c, PASS_PIECES[:n_bufs])
            for k in range(2):
                move(slots[k], sibling, slots[k], 1 - c, [1] * n_bufs)
        else:
            move(slots[2], sibling, slots[2], 1 - c, [1] * n_bufs)
        return out
    return plan


NEAR_PIECES = (2, 1)
PASS_PIECES = (2, 1)


def _chip_gather_copies(stage, n_bufs):
    if stage == "near":
        return 2 * sum(NEAR_PIECES[:n_bufs]), None
    if stage == "pass":
        n_pass = sum(PASS_PIECES[:n_bufs])
        return n_pass + 2 * n_bufs, set(range(n_pass))
    return n_bufs, None


def _swap_plan(n_slabs):
    def plan(refs):
        x, y, c = _coords()
        out, k = [], 0
        for i, n in enumerate(n_slabs):
            g, land = refs[2 * i], refs[2 * i + 1]
            for p in range(n):
                out.append((k, (g.at[p, 1 - c], land.at[p], (x, y, 1 - c), None), (land.at[p], None)))
                k += 1
        return out
    return plan


def _is_one_of(chip, dests):
    hit = chip == dests[0]
    for d in dests[1:]:
        hit = hit | (chip == d)
    return hit


def _slab_of(chip, dests):
    return sum(j * (chip == d).astype(jnp.int32) for j, d in enumerate(dests))


def _scatter_plan(dest_sets):
    def plan(refs):
        x, y, c = _coords()
        me = 2 * x + y
        out = []
        for k, (px, py) in enumerate(_other_chips(x, y)):
            peer = 2 * px + py
            for i, dests in enumerate(dest_sets):
                cs, land = refs[2 * i], refs[2 * i + 1]
                everyone = len(dests) == N_CHIPS
                send = (cs.at[_slab_of(peer, dests)], land.at[k], (px, py, c),
                        None if everyone else _is_one_of(peer, dests))
                recv = (land.at[k], None if everyone else _is_one_of(me, dests))
                out.append((k * len(dest_sets) + i, send, recv))
        return out
    return plan


def _join_plan(rows, n_pieces):
    def plan(refs):
        x, y, c = _coords()
        (buf,) = refs
        return [(i, (buf.at[c, piece], buf.at[c, piece], (x, y, 1 - c), None), (buf.at[1 - c, piece], None))
                for i, piece in enumerate(_chunks(rows, n_pieces))]
    return plan


def _join_plans(parts):
    def plan(refs):
        out, b0, k0 = [], 0, 0
        for part_plan, n_bufs, n_copies in parts:
            out += [(k0 + k, send, recv) for k, send, recv in part_plan(refs[b0:b0 + n_bufs])]
            b0 += n_bufs
            k0 += n_copies
        return out
    return plan


def _allgather_plan():
    def plan(refs):
        x, y, c = _coords()
        (land,) = refs
        me = 4 * x + 2 * y + c
        out = []
        for r in range(1, 8):
            px = 1 - x if r & 4 else x
            py = 1 - y if r & 2 else y
            pc = 1 - c if r & 1 else c
            out.append((r - 1, (land.at[me], land.at[me], (px, py, pc), None), (land.at[4 * px + 2 * py + pc], None)))
        return out
    return plan


def _sum_gathered(land, shapes, name):
    m = land.shape[1]

    def body(land_ref, *refs):
        outs, acc_ref = refs[:-1], refs[-1]
        acc = land_ref[0]
        for d in range(1, 8):
            acc = acc + land_ref[d]
        acc_ref[...] = acc
        r = 0
        for o_ref, shape in zip(outs, shapes):
            if len(shape) == 3:
                n_blocks, bw, _ = shape
                side = LANES // bw
                for blk in range(n_blocks):
                    first = r + (blk // side) * bw
                    o_ref[blk] = acc_ref[first:first + bw, (blk % side) * bw:(blk % side + 1) * bw]
                r += n_blocks * bw // side
                continue
            n, w = shape
            if w == LANES:
                o_ref[...] = acc_ref[r:r + n, :]
                r += n
            elif w < LANES:
                o_ref[...] = acc_ref[r:r + n, 0:w]
                r += SUBLANES
            else:
                for k in range(n):
                    for q in range(w // LANES):
                        o_ref[k:k + 1, q * LANES:(q + 1) * LANES] = acc_ref[r:r + 1, :]
                        r += 1
        assert r == m, (r, m)

    return pl.pallas_call(
        body,
        name=name,
        out_shape=[jax.ShapeDtypeStruct(s, F32) for s in shapes],
        scratch_shapes=[pltpu.VMEM((m, LANES), F32)],
        compiler_params=_params(),
    )(land)


def _row_tile(rows, cap):
    t = cap
    while rows % t:
        t //= 2
    return t


def _add_my_halves(pairs, steps, name):
    k = len(pairs)

    def body(c_ref, *refs):
        for a in range(k):
            o_ref = refs[2 * k + a]
            o_ref[...] = (refs[2 * a][0] + refs[2 * a + 1][...]).astype(o_ref.dtype)

    in_specs, out_specs = [], []
    for g, _ in pairs:
        n_slabs, _, R, C = g.shape
        per = steps // n_slabs
        tr = R // per
        in_specs += [pl.BlockSpec((1, 1, tr, C), lambda p, c_ref, per=per: (p // per, c_ref[0], p % per, 0)),
                     pl.BlockSpec((1, tr, C), lambda p, c_ref, per=per: (p // per, p % per, 0))]
        out_specs.append(pl.BlockSpec((1, tr, C), lambda p, c_ref, per=per: (p // per, p % per, 0)))
    return pl.pallas_call(
        body,
        name=name,
        grid_spec=pltpu.PrefetchScalarGridSpec(num_scalar_prefetch=1, grid=(steps,), in_specs=in_specs, out_specs=out_specs),
        out_shape=[jax.ShapeDtypeStruct(r.shape, jnp.bfloat16) for _, r in pairs],
        compiler_params=_params(("parallel",)),
    )(lax.axis_index("c").reshape(1).astype(jnp.int32), *[a for pair in pairs for a in pair])


def _add_my_half(g, r, name):
    n_slabs, _, R, _ = g.shape
    steps = n_slabs if n_slabs > 1 else R // _row_tile(R, SUM_ROWS)
    return _add_my_halves([(g, r)], steps, name)[0]


def _sum_slabs(own, got, name, deps=()):
    _, R, C = own.shape
    tr = _row_tile(R, SUM_ROWS)

    def body(s_ref, own_ref, got_ref, *rest):
        rest[-1][0] = ((own_ref[0].astype(F32) + got_ref[0].astype(F32)) + got_ref[1].astype(F32)) + got_ref[2].astype(F32)

    xi, yi, ci = _coords()
    return pl.pallas_call(
        body,
        name=name,
        grid_spec=pltpu.PrefetchScalarGridSpec(
            num_scalar_prefetch=1,
            grid=(R // tr,),
            in_specs=[pl.BlockSpec((1, tr, C), lambda i, s: (s[0], i, 0)),
                      pl.BlockSpec((3, tr, C), lambda i, s: (0, i, 0))] + [ANY_SPEC] * len(deps),
            out_specs=pl.BlockSpec((1, tr, C), lambda i, s: (s[1], i, 0)),
        ),
        out_shape=jax.ShapeDtypeStruct((2, R, C), F32),
        compiler_params=_params(("parallel",)),
    )(jnp.stack([2 * xi + yi, ci]).astype(jnp.int32), own, got, *deps)


def _adamw_math(w, g, m, v):
    m = ADAM_B1 * m + (1.0 - ADAM_B1) * g
    v = ADAM_B2 * v + (1.0 - ADAM_B2) * (g * g)
    m_hat = m / (1.0 - ADAM_B1 ** ADAM_STEP)
    v_hat = v / (1.0 - ADAM_B2 ** ADAM_STEP)
    delta = -ADAM_LR * (m_hat / (jnp.sqrt(v_hat) + ADAM_EPS) + ADAM_WD * w)
    return delta, m, v


def _sum_adamw_own(owns, got, dest_sets, w, m, v, name):
    n = len(owns)
    _, R, C = owns[0].shape
    tr = _row_tile(R, ADAMW_ROWS)
    steps = R // tr

    def body(s_ref, *refs):
        got_ref, w_ref, m_ref, v_ref = refs[n:n + 4]
        half_ref, g_ref, d_ref, mn_ref, vn_ref = refs[len(refs) - 5:]
        total = jnp.zeros((tr, C), F32)
        for i in range(n):
            total = total + jnp.where(s_ref[2 + 2 * i] == 1, refs[i][0].astype(F32), 0.0)
        grad = ((total + got_ref[0].astype(F32)) + got_ref[1].astype(F32)) + got_ref[2].astype(F32)
        d, mn, vn = _adamw_math(w_ref[...], grad, m_ref[...], v_ref[...])
        half_ref[0] = grad
        for o, val in zip((g_ref, d_ref, mn_ref, vn_ref), (grad, d, mn, vn)):
            o[...] = val

    xi, yi, ci = _coords()
    me = 2 * xi + yi
    scalars = [ci, ci]
    for dests in dest_sets:
        scalars += [_is_one_of(me, dests).astype(jnp.int32), _slab_of(me, dests)]
    own_spec = lambda i: pl.BlockSpec((1, tr, C), lambda r, s: (s[3 + 2 * i], r, 0))
    rows = pl.BlockSpec((tr, C), lambda r, s: (s[0] * steps + r, 0))
    outs = pl.pallas_call(
        body,
        name=name,
        grid_spec=pltpu.PrefetchScalarGridSpec(
            num_scalar_prefetch=1,
            grid=(steps,),
            in_specs=[own_spec(i) for i in range(n)] + [pl.BlockSpec((3, tr, C), lambda r, s: (0, r, 0))] + [rows] * 3,
            out_specs=[pl.BlockSpec((1, tr, C), lambda r, s: (s[0], r, 0))] + [rows] * 4,
        ),
        out_shape=[jax.ShapeDtypeStruct((2, R, C), F32)] + [jax.ShapeDtypeStruct((2 * R, C), F32)] * 4,
        compiler_params=_params(("parallel",)),
    )(jnp.stack(scalars).astype(jnp.int32), *owns, got, w, m, v)
    return outs[0], list(outs[1:])


def _adamw_halves(ws, g, ms, vs, half, prev, name, deps=()):
    n = len(ws)
    _, _, R, C = g.shape
    tr = _row_tile(R, ADAMW_ROWS)
    steps = R // tr
    carried = [] if prev is None else [a for four in prev for a in four]
    both = half is None
    which = (lambda i, s: i // steps) if both else (lambda i, s: s[0])
    half = 0 if both else half

    def body(s_ref, *refs):
        w_refs, g_refs, m_refs, v_refs = (refs[k * n:(k + 1) * n] for k in range(4))
        outs = refs[len(refs) - 4 * n:]
        for a in range(n):
            grad = g_refs[a][0, 0]
            d, mn, vn = _adamw_math(w_refs[a][...], grad, m_refs[a][...], v_refs[a][...])
            for o, val in zip(outs[4 * a:4 * a + 4], (grad, d, mn, vn)):
                o[...] = val

    rows = pl.BlockSpec((tr, C), lambda i, s: (which(i, s) * steps + i % steps, 0))
    grad_spec = lambda a: pl.BlockSpec((1, 1, tr, C), lambda i, s: (which(i, s), a, i % steps, 0))
    n_in = 4 * n
    outs = pl.pallas_call(
        body,
        name=name,
        grid_spec=pltpu.PrefetchScalarGridSpec(
            num_scalar_prefetch=1,
            grid=(2 * steps if both else steps,),
            in_specs=[rows] * n + [grad_spec(a) for a in range(n)] + [rows] * (2 * n)
            + [ANY_SPEC] * (len(carried) + len(deps)),
            out_specs=[rows] * (4 * n),
        ),
        out_shape=[jax.ShapeDtypeStruct((2 * R, C), F32)] * (4 * n),
        input_output_aliases={1 + n_in + k: k for k in range(len(carried))},
        compiler_params=_params(("parallel",)),
    )(jnp.reshape(half, (1,)).astype(jnp.int32), *ws, *([g] * n), *ms, *vs, *carried, *deps)
    return [outs[4 * a:4 * a + 4] for a in range(n)]


def _adamw_small(ws, gs, ms, vs, name):
    n = len(ws)

    def body(*refs):
        for a in range(n):
            d, mn, vn = _adamw_math(refs[a][...], refs[n + a][...], refs[2 * n + a][...], refs[3 * n + a][...])
            refs[4 * n + a][...] = d
            refs[5 * n + a][...] = mn
            refs[6 * n + a][...] = vn

    shapes = [jax.ShapeDtypeStruct(w.shape, F32) for w in ws]
    outs = pl.pallas_call(
        body,
        name=name,
        out_shape=shapes * 3,
        compiler_params=_params(),
    )(*ws, *gs, *ms, *vs)
    return outs[:n], outs[n:2 * n], outs[2 * n:]


def _to_blockdiag(w):
    per = CW // LRU_BW
    w4 = w.reshape(N_CT, per, LRU_BW, LRU_BW)
    eye = jnp.eye(per, dtype=w.dtype)
    return (w4[:, :, :, None, :] * eye[None, :, None, :, None]).reshape(N_CT, CW, CW)


def _local_grads(x2d, tgt2d, B, S, g_in, in_proj, conv_b, gate_x_w, gate_x_b, gate_a_w, gate_a_b, lam,
                 proj_weights, g_fin, reduce):
    wx_bd = _c(_to_blockdiag(gate_x_w))
    wa_bd = _c(_to_blockdiag(gate_a_w))
    tables = _retention_tables(S)

    proj, ht, w_all, conv_w, gain = in_proj(x2d, g_in, (*tables, wx_bd, wa_bd))
    gain3 = gain.reshape(HEADS, 1, DK)
    hlru, ya = _lru_fwd(proj, conv_w, conv_b, wx_bd, wa_bd, gate_x_b, gate_a_b, lam, B, S)
    o_pre, yb, states = _ret_fwd(proj, tables, gain3, B, S)
    wpa, wpb, wout = proj_weights(yb)
    loss, dx2, dya, dyb, dm, dgf, gw_proj = _mid(ya, yb, proj, x2d, tgt2d, wpa, wpb, wout, g_fin)
    g3 = _inproj_bwd_dw(ht, [dm], "inproj_bwd_dw_m")
    deps = reduce.m_ready(gw_proj, g3)
    dr, dgain = _ret_bwd(dyb, o_pre, proj, states, tables, gain3, B, S, deps)
    deps = reduce.ret_done(dr)
    g12 = _inproj_bwd_dw(ht, [dr], "inproj_bwd_dw_r", deps)
    deps = reduce.r_ready(g12)
    dxa, dga, dcw, dcb, dwx, dwa, dbx, dba, dlam = _lru_bwd(
        dya, proj, hlru, conv_w, conv_b, wx_bd, wa_bd, gate_x_b, gate_a_b, lam, B, S, deps)
    small = dict(conv_w=dcw, conv_b=dcb, gate_x_w=dwx, gate_x_b=dbx, gate_a_w=dwa, gate_a_b=dba, lru_lambda=dlam,
                 gn_gain=dgain.reshape(HEADS, DK), norm_final=dgf)
    deps = reduce.lru_done(dxa, _pack_small(small, loss, reduce.slot()))
    g0 = _inproj_bwd_dw(ht, [dxa, dga], "inproj_bwd_dw_a", deps)
    deps = reduce.a_ready(g0)
    n_tiles = x2d.shape[0] // min(DX_TILE, x2d.shape[0])
    grad_x, dgin = _inproj_bwd_dx([dxa, dga, dr, dm], w_all, x2d, dx2, g_in, 0, n_tiles, None, "inproj_bwd_dx", deps)
    return grad_x, dgin


ALL_CHIPS = (0, 1, 2, 3)


class _GradReduce:
    def __init__(self, proj_done):
        self.pending = {}
        self.proj_done = proj_done
        self.land_in = None

    def _start(self, key, parts, name):
        bufs, plans, shared = [], [], None
        for part_bufs, plan, n_copies, part_shared in parts:
            if part_shared is not None:
                shared = len(bufs) + part_shared
            plans.append((plan, len(part_bufs), n_copies))
            bufs += part_bufs
        plan = _join_plans(plans)
        send_sems, recv_sems, bufs, token = _copies_start(bufs, plan, sum(p[2] for p in plans), name + "_start")
        if shared is not None:
            self.land_in = bufs[shared]
        self.pending[key] = (send_sems, recv_sems, bufs, plan, name + "_wait", shared)
        return (token,)

    def _finish(self, key, after):
        send_sems, recv_sems, bufs, plan, name, shared = self.pending.pop(key)
        if shared is not None:
            bufs[shared] = self.land_in
        bufs = _copies_wait(send_sems, recv_sems, bufs, after, plan, name)
        if shared is not None:
            self.land_in = bufs[shared]
        return bufs

    @staticmethod
    def _swap(pieces):
        bufs = []
        for g in pieces:
            bufs += [g, lax.empty((g.shape[0],) + g.shape[2:], F32)]
        n_slabs = [g.shape[0] for g in pieces]
        return bufs, _swap_plan(n_slabs), sum(n_slabs), None

    def _scatter(self, sums, dest_sets):
        bufs = []
        for cs in sums:
            bufs += [cs, lax.empty((3,) + cs.shape[1:], cs.dtype)]
        if self.land_in is not None:
            bufs[-1] = self.land_in
        return bufs, _scatter_plan(dest_sets), 3 * len(sums), len(bufs) - 1

    @staticmethod
    def slot():
        x, y, c = _coords()
        return 4 * x + 2 * y + c

    def _gather8(self, block):
        land = lax.dynamic_update_slice(lax.empty((8,) + block.shape, F32), block[None], (self.slot(), 0, 0))
        return [land], _allgather_plan(), 7, None

    def m_ready(self, gw_proj, g3):
        rows = gw_proj.shape[2] * gw_proj.shape[3]
        return self._start("m", [self._swap([gw_proj.reshape(N_CHIPS, 2, rows, D_MODEL), g3])], "swap_m")

    def ret_done(self, after):
        proj, land_p, g3, land_3 = self._finish("m", after)
        sums_m = _add_my_halves([(proj, land_p), (g3, land_3)], N_CHIPS, "chip_sum_m")
        return self._start("sm", [self._scatter(sums_m, [ALL_CHIPS, (3,)])], "scatter_m")

    def r_ready(self, g12):
        return self._start("r", [self._swap([g12])], "swap_r")

    def lru_done(self, after, packed):
        g12, land_12 = self._finish("r", after)
        sums_r = [_add_my_half(g12, land_12, "chip_sum_r")]
        return (self._start("sr", [self._scatter(sums_r, [(1, 2)])], "scatter_r")
                + self._start("small", [([packed], _allgather_plan(), 7, None)], "gather_small"))

    def a_ready(self, g0):
        (token,) = self._start("a", [self._swap([g0])], "swap_a")
        csp, gotp, self.cs3, _ = self._finish("sm", token)
        half_proj = _sum_slabs(csp, gotp, "sum_w_proj")
        g0, land_0 = self._finish("a", half_proj)
        join = ([half_proj], _join_plan(half_proj.shape[1], PROJ_JOIN_PIECES), PROJ_JOIN_PIECES, None)
        return self._start("sa", [self._scatter([_add_my_half(g0, land_0, "chip_sum_a")], [(0,)]), join], "scatter_a")

    def finish(self, dgin, w_in_own, w_in_done):
        (small,) = self._finish("small", dgin)
        cs12, _ = self._finish("sr", dgin)
        cs0, _, g_proj = self._finish("sa", dgin)
        half_in, first = w_in_own([self.cs3, cs12, cs0], self.land_in, [(3,), (1, 2), (0,)])
        join = ([half_in], _join_plan(half_in.shape[1], JOIN_PIECES), JOIN_PIECES, None)
        deps = self._start("j", [join, self._gather8(dgin)], "join_w_in")
        g_in, norm_in = self._finish("j", self.proj_done(g_proj, deps))
        w_in_done(g_in, first)
        return _unpack_small(small), _sum_gathered(norm_in, [(1, D_MODEL)], "sum_norm_in_grad")[0]


_SMALL = ("gate_x_w", "gate_a_w", "conv_w", "conv_b", "gate_x_b", "gate_a_b", "lru_lambda", "gn_gain", "norm_final")
_SMALL_SHAPES = dict(gate_x_w=(LRU_BLOCKS, LRU_BW, LRU_BW), gate_a_w=(LRU_BLOCKS, LRU_BW, LRU_BW),
                     norm_in=(1, D_MODEL), conv_w=(CONV, D_MODEL), conv_b=(1, D_MODEL), gate_x_b=(1, D_MODEL),
                     gate_a_b=(1, D_MODEL), lru_lambda=(1, D_MODEL), gn_gain=(HEADS, DK), norm_final=(1, D_MODEL))


def _pack_small(small, loss, slot):
    parts = [small[k] if small[k].ndim == 2 else small[k].reshape(-1, LANES) for k in _SMALL]
    m = sum(p.size for p in parts) // LANES + SUBLANES

    def body(s_ref, *refs):
        o_ref = refs[-1]
        r = 0
        for ref, part in zip(refs, parts):
            if part.shape[1] == LANES:
                o_ref[0, r:r + part.shape[0], :] = ref[...]
                r += part.shape[0]
                continue
            for k in range(part.shape[0]):
                for q in range(part.shape[1] // LANES):
                    o_ref[0, r:r + 1, :] = ref[k:k + 1, q * LANES:(q + 1) * LANES]
                    r += 1
        o_ref[0, r:r + SUBLANES, :] = jnp.broadcast_to(refs[len(parts)][...], (SUBLANES, LANES))

    return pl.pallas_call(
        body,
        name="pack_small_grads",
        grid_spec=pltpu.PrefetchScalarGridSpec(
            num_scalar_prefetch=1,
            grid=(1,),
            in_specs=[pl.BlockSpec(p.shape, lambda i, s: (0, 0)) for p in parts] + [pl.BlockSpec((1, 1), lambda i, s: (0, 0))],
            out_specs=pl.BlockSpec((1, m, LANES), lambda i, s: (s[0], 0, 0)),
        ),
        out_shape=jax.ShapeDtypeStruct((8, m, LANES), F32),
        compiler_params=_params(("arbitrary",)),
    )(jnp.reshape(slot, (1,)).astype(jnp.int32), *parts, loss)


def _unpack_small(land):
    *sums, loss = _sum_gathered(land, [_SMALL_SHAPES[k] for k in _SMALL] + [(1, 1)], "sum_small_grads")
    return dict(zip(_SMALL, sums)), loss


def kernel(x, norm_in, w_in, conv_w, conv_b, gate_x_w, gate_x_b, gate_a_w, gate_a_b, lru_lambda, gn_gain, w_proj_a, w_proj_b, w_out, norm_final, loss_target, m_norm_in, m_w_in, m_conv_w, m_conv_b, m_gate_x_w, m_gate_x_b, m_gate_a_w, m_gate_a_b, m_lru_lambda, m_gn_gain, m_w_proj_a, m_w_proj_b, m_w_out, m_norm_final, v_norm_in, v_w_in, v_conv_w, v_conv_b, v_gate_x_w, v_gate_x_b, v_gate_a_w, v_gate_a_b, v_lru_lambda, v_gn_gain, v_w_proj_a, v_w_proj_b, v_w_out, v_norm_final):
    B, S, _ = x.shape
    T = B * S
    xi, yi, ci = _coords()
    chip = 2 * xi + yi

    cshard = D_MODEL // N_CHIPS
    mine = _cast_into_slot([w_in[0].reshape(2, D_MODEL // 2, 2 * D_MODEL)], "cast_w_in")
    plan = _gather_plan(3)
    pending_proj = []
    gshard = DK // N_CHIPS
    tiny = jnp.concatenate([conv_w[0], jnp.zeros((4, cshard), F32), jnp.pad(gn_gain[0], ((0, 4), (0, cshard - gshard)))],
                           axis=0).reshape(1, 2, SUBLANES, cshard)
    tiny_buf = lax.dynamic_update_slice(lax.empty((N_CHIPS, 2, SUBLANES, cshard), F32), tiny, (chip, 0, 0, 0))
    near_plan, pass_plan, far_plan = (_chip_gather_plan(stage, 2) for stage in ("near", "pass", "far"))
    (n_near, _), (n_pass, passed_on), (n_far, _) = (_chip_gather_copies(stage, 2) for stage in ("near", "pass", "far"))
    halves = set(range(n_pass)) - passed_on
    near_s, near_r, bufs, near_token = _copies_start([mine[0], tiny_buf], near_plan, n_near, "gather_near_start")

    def in_proj(x2d, g_in, meanwhile):
        as_w = lambda b: b[0].reshape(N_CHIPS, D_MODEL, 2 * D_MODEL)
        slot_x, slot_y, slot_d = 2 * (1 - xi) + yi, 2 * xi + (1 - yi), 2 * (1 - xi) + (1 - yi)
        ids = lambda *chips: jnp.stack(chips).astype(jnp.int32)
        proj, hb, ht = _inproj_first(x2d, g_in, as_w(bufs), ids(chip), "inproj_own", (near_token, *meanwhile))
        got = _copies_wait(near_s, near_r, bufs, proj, near_plan, "gather_near_wait")
        pass_s, pass_r, got, pass_token = _copies_start(got, pass_plan, n_pass, "gather_pass_start")
        mine_proj = _cast_into_slot([w[0].reshape(2, cshard // 2, D_MODEL) for w in (w_proj_a, w_proj_b, w_out)],
                                    "cast_w_proj", (pass_token,))
        got = _copies_wait(pass_s, pass_r, got, mine_proj[0], pass_plan, "gather_pass_wait_halves", only=halves)
        proj = _inproj_more(hb, as_w(got), ids(slot_x, slot_y), proj, "inproj_near")
        got = _copies_wait(pass_s, pass_r, got, proj, pass_plan, "gather_pass_wait_far", only=passed_on)
        far_s, far_r, got, far_token = _copies_start(got, far_plan, n_far, "gather_far_start")
        pending_proj.append(_copies_start(mine_proj, plan, 9, "gather_proj_start", (far_token,)))
        got = _copies_wait(far_s, far_r, got, pending_proj[0][3], far_plan, "gather_far_wait")
        proj = _inproj_more(hb, as_w(got), ids(slot_d), proj, "inproj_far")
        tiny_all = got[1].reshape(N_CHIPS, 2 * SUBLANES, cshard)
        conv_w_full = jnp.transpose(tiny_all[:, 0:CONV, :], (1, 0, 2)).reshape(CONV, D_MODEL)
        gain_full = jnp.transpose(tiny_all[:, 8:8 + HEADS, :gshard], (1, 0, 2)).reshape(HEADS, DK)
        return proj, ht, as_w(got), conv_w_full, gain_full

    def proj_weights(after):
        s_sems, r_sems, pbufs, _ = pending_proj[0]
        got = _copies_wait(s_sems, r_sems, pbufs, after, plan, "gather_proj_wait")
        return [b.reshape(D_MODEL, D_MODEL) for b in got]

    weights = dict(norm_in=norm_in, w_in=w_in, conv_w=conv_w, conv_b=conv_b, gate_x_w=gate_x_w, gate_x_b=gate_x_b,
                   gate_a_w=gate_a_w, gate_a_b=gate_a_b, lru_lambda=lru_lambda, gn_gain=gn_gain, w_proj_a=w_proj_a,
                   w_proj_b=w_proj_b, w_out=w_out, norm_final=norm_final)
    ms = dict(norm_in=m_norm_in, w_in=m_w_in, conv_w=m_conv_w, conv_b=m_conv_b, gate_x_w=m_gate_x_w,
              gate_x_b=m_gate_x_b, gate_a_w=m_gate_a_w, gate_a_b=m_gate_a_b, lru_lambda=m_lru_lambda, gn_gain=m_gn_gain,
              w_proj_a=m_w_proj_a, w_proj_b=m_w_proj_b, w_out=m_w_out, norm_final=m_norm_final)
    vs = dict(norm_in=v_norm_in, w_in=v_w_in, conv_w=v_conv_w, conv_b=v_conv_b, gate_x_w=v_gate_x_w,
              gate_x_b=v_gate_x_b, gate_a_w=v_gate_a_w, gate_a_b=v_gate_a_b, lru_lambda=v_lru_lambda, gn_gain=v_gn_gain,
              w_proj_a=v_w_proj_a, w_proj_b=v_w_proj_b, w_out=v_w_out, norm_final=v_norm_final)
    names = list(weights)
    grads, delta, new_m, new_v = {}, {}, {}, {}

    def update_big(keys, g, half, prev, name, deps=()):
        two = lambda a: a.reshape(a.shape[1], a.shape[2])
        res = _adamw_halves([two(weights[k]) for k in keys], g, [two(ms[k]) for k in keys], [two(vs[k]) for k in keys],
                            half, prev, name, deps)
        for k, (gk, d, mn, vn) in zip(keys, res):
            shp = weights[k].shape
            grads[k], delta[k], new_m[k], new_v[k] = gk.reshape(shp), d.reshape(shp), mn.reshape(shp), vn.reshape(shp)
        return res

    def proj_done(g_proj, deps):
        g4 = g_proj.reshape(2, 3, D_MODEL // (2 * N_CHIPS), D_MODEL)
        return update_big(("w_proj_a", "w_proj_b", "w_out"), g4, None, None, "adamw_proj", deps)[-1][1]

    def w_in_own(owns, got, dest_sets):
        two = lambda a: a.reshape(a.shape[1], a.shape[2])
        return _sum_adamw_own(owns, got, dest_sets, two(w_in), two(m_w_in), two(v_w_in), "adamw_w_in_own")

    def w_in_done(g_in, prev):
        g4 = g_in.reshape(2, 1, D_MODEL // 2, 2 * D_MODEL)
        return update_big(("w_in",), g4, 1 - ci, [prev], "adamw_w_in_other")[0]

    reduce = _GradReduce(proj_done)
    grad_x, dgin = _local_grads(
        x.reshape(T, D_MODEL), loss_target.reshape(T, D_MODEL), B, S, norm_in, in_proj, conv_b,
        gate_x_w[0], gate_x_b, gate_a_w[0], gate_a_b, lru_lambda, proj_weights,
        norm_final.reshape(1, D_MODEL), reduce)

    (gsm, loss), g_norm_in = reduce.finish(dgin.reshape(SUBLANES, LANES), w_in_own, w_in_done)
    loss = loss[0, 0]
    gsm["norm_in"] = g_norm_in
    gsm["conv_w"] = lax.dynamic_slice_in_dim(gsm["conv_w"], chip * cshard, cshard, axis=1)
    gsm["gn_gain"] = lax.dynamic_slice_in_dim(gsm["gn_gain"], chip * gshard, gshard, axis=1)
    smalls = [k for k in names if k not in delta]

    def view(a):
        return a.reshape(1, -1) if a.ndim == 1 else (a.reshape(a.shape[1:]) if a.ndim > 2 else a)

    ds, mns, vns = _adamw_small([view(weights[k]) for k in smalls], [gsm[k].reshape(view(weights[k]).shape) for k in smalls],
                                [view(ms[k]) for k in smalls], [view(vs[k]) for k in smalls], "adamw_small")
    for k, d, mn, vn in zip(smalls, ds, mns, vns):
        shp = weights[k].shape
        grads[k], delta[k], new_m[k], new_v[k] = gsm[k].reshape(shp), d.reshape(shp), mn.reshape(shp), vn.reshape(shp)

    return (loss, grad_x.reshape(B, S, D_MODEL), *[grads[k] for k in names], *[delta[k] for k in names],
            *[new_m[k] for k in names], *[new_v[k] for k in names])
```

```python
import jax
import jax.numpy as jnp
from jax import lax
from jax.experimental import pallas as pl
from jax.experimental.pallas import tpu as pltpu

F32 = jnp.float32
_MXU = jnp.bfloat16

D_MODEL = 1024
N_GROUPS = 8
HEADS = 4
DK = 256
CHUNK = 128
CONV = 4
LRU_BLOCKS = 16
LRU_BW = 64
LRU_C = 8.0
ROPE_THETA = 10000.0
EPS = 1e-6
CW = 256
N_CT = D_MODEL // CW
N_CHIPS = 4
MESH = pl.DeviceIdType.MESH

ADAM_LR = 0.001
ADAM_B1 = 0.9
ADAM_B2 = 0.999
ADAM_EPS = 1e-08
ADAM_WD = 0.01
ADAM_STEP = 10

VMEM_LIMIT = 56 * 1024 * 1024

FIRST_PROJ_TILE = 1024
MORE_PROJ_TILE = 2048
SCAN_TILE = 1024
MID_TILE = 256
DX_TILE = 512
DW_COLS = 512
DW_LOADS = 4
RET_CHUNKS = 2
SUM_ROWS = 256
ADAMW_ROWS = 256
JOIN_PIECES = 1
PROJ_JOIN_PIECES = 4


def _c(v):
    return v.astype(_MXU)


def _dot(a, b):
    return lax.dot_general(a, b, (((1,), (0,)), ((), ())), preferred_element_type=F32)


def _dot_nt(a, b):
    return lax.dot_general(a, b, (((1,), (1,)), ((), ())), preferred_element_type=F32)


def _dot_tn(a, b):
    return lax.dot_general(a, b, (((0,), (0,)), ((), ())), preferred_element_type=F32)


def _sigmoid(z):
    return 0.5 * jnp.tanh(0.5 * z) + 0.5


ANY_SPEC = pl.BlockSpec(memory_space=pl.ANY)


def _after(body, n_in, deps):
    n_deps = len(deps)

    def wrapped(*refs):
        return body(*refs[:n_in], *refs[n_in + n_deps:])

    return wrapped


def _params(sem=None):
    if sem is None:
        return pltpu.CompilerParams(vmem_limit_bytes=VMEM_LIMIT)
    return pltpu.CompilerParams(vmem_limit_bytes=VMEM_LIMIT, dimension_semantics=sem)


def _inproj_first(x2d, g_in, w_all, chips, name, deps=()):
    T = x2d.shape[0]
    tm = min(FIRST_PROJ_TILE, T)
    n_i = T // tm

    def body(s_ref, *refs):
        x_ref, g_ref, w_ref = refs[:3]
        proj_ref, hb_ref, ht_ref, h_all = refs[-4:]
        i = pl.program_id(1)
        rows = pl.ds(pl.multiple_of(i * tm, tm), tm)

        @pl.when(pl.program_id(0) == 0)
        def _():
            x = x_ref[...]
            r = lax.rsqrt(jnp.mean(x * x, axis=-1, keepdims=True) + EPS)
            h = x * r * g_ref[...]
            hb = h.astype(h_all.dtype)
            h_all[rows, :] = hb
            hb_ref[...] = hb
            ht_ref[...] = h.T.astype(ht_ref.dtype)

        proj_ref[...] = _dot(h_all[rows, :], w_ref[0])

    first = lambda j, i: jnp.where(j == 0, i, n_i - 1)
    return pl.pallas_call(
        body,
        name=name,
        grid_spec=pltpu.PrefetchScalarGridSpec(
            num_scalar_prefetch=1,
            grid=(2 * chips.shape[0], n_i),
            in_specs=[
                pl.BlockSpec((tm, D_MODEL), lambda j, i, s: (first(j, i), 0)),
                pl.BlockSpec((1, D_MODEL), lambda j, i, s: (0, 0)),
                pl.BlockSpec((1, D_MODEL, D_MODEL), lambda j, i, s: (s[j // 2], 0, j % 2)),
            ] + [ANY_SPEC] * len(deps),
            out_specs=[
                pl.BlockSpec((tm, D_MODEL), lambda j, i, s: (i, 2 * s[j // 2] + j % 2)),
                pl.BlockSpec((tm, D_MODEL), lambda j, i, s: (first(j, i), 0)),
                pl.BlockSpec((D_MODEL, tm), lambda j, i, s: (0, first(j, i))),
            ],
            scratch_shapes=[pltpu.VMEM((T, D_MODEL), _MXU)],
        ),
        out_shape=[
            jax.ShapeDtypeStruct((T, N_GROUPS * D_MODEL), F32),
            jax.ShapeDtypeStruct((T, D_MODEL), _MXU),
            jax.ShapeDtypeStruct((D_MODEL, T), _MXU),
        ],
        compiler_params=_params(("arbitrary", "arbitrary")),
    )(chips, x2d, g_in, w_all, *deps)


def _inproj_more(hb, w_all, chips, proj, name):
    T = hb.shape[0]
    tm = min(MORE_PROJ_TILE, T)

    def body(s_ref, hb_hbm, w_ref, prev_ref, proj_ref, h_all, sem):
        @pl.when((pl.program_id(0) == 0) & (pl.program_id(1) == 0))
        def _():
            cp = pltpu.make_async_copy(hb_hbm, h_all, sem)
            cp.start()
            cp.wait()

        rows = pl.ds(pl.multiple_of(pl.program_id(1) * tm, tm), tm)
        proj_ref[...] = _dot(h_all[rows, :], w_ref[0])

    return pl.pallas_call(
        body,
        name=name,
        grid_spec=pltpu.PrefetchScalarGridSpec(
            num_scalar_prefetch=1,
            grid=(2 * chips.shape[0], T // tm),
            in_specs=[
                ANY_SPEC,
                pl.BlockSpec((1, D_MODEL, D_MODEL), lambda j, i, s: (s[j // 2], 0, j % 2)),
                ANY_SPEC,
            ],
            out_specs=pl.BlockSpec((tm, D_MODEL), lambda j, i, s: (i, 2 * s[j // 2] + j % 2)),
            scratch_shapes=[pltpu.VMEM((T, D_MODEL), hb.dtype), pltpu.SemaphoreType.DMA],
        ),
        out_shape=jax.ShapeDtypeStruct(proj.shape, F32),
        input_output_aliases={3: 0},
        compiler_params=_params(("arbitrary", "arbitrary")),
    )(chips, hb, w_all, proj)


def _scan_fwd(a, u):
    n = a.shape[0]
    row = lax.broadcasted_iota(jnp.int32, a.shape, 0)
    s = 1
    while s < n:
        m = row >= s
        u = u + a * jnp.where(m, pltpu.roll(u, s, 0), 0.0)
        a = a * jnp.where(m, pltpu.roll(a, s, 0), 1.0)
        s *= 2
    return a, u


def _scan_bwd(b, g):
    n = b.shape[0]
    row = lax.broadcasted_iota(jnp.int32, b.shape, 0)
    s = 1
    while s < n:
        m = row < n - s
        g = g + b * jnp.where(m, pltpu.roll(g, n - s, 0), 0.0)
        b = b * jnp.where(m, pltpu.roll(b, n - s, 0), 1.0)
        s *= 2
    return b, g


LANES = 128
SUBLANES = 8


def _scan_scratch(tc):
    by_lanes = pltpu.VMEM((CW // LANES, tc, LANES), F32)
    return [by_lanes, by_lanes, pltpu.VMEM((tc // SUBLANES, CW), F32), pltpu.VMEM((tc, CW), F32)]


def _scan_tile(a, u, edge, la_ref, lh_ref, c_ref, dst_ref, reverse):
    n, w = a.shape
    groups = n // SUBLANES
    a3 = a.reshape(groups, SUBLANES, w)
    u3 = u.reshape(groups, SUBLANES, w)
    row = lax.broadcasted_iota(jnp.int32, a3.shape, 1)
    for s in (1, 2, 4):
        m = (row < SUBLANES - s) if reverse else (row >= s)
        shift = SUBLANES - s if reverse else s
        u3 = u3 + a3 * jnp.where(m, pltpu.roll(u3, shift, 1), 0.0)
        a3 = a3 * jnp.where(m, pltpu.roll(a3, shift, 1), 1.0)
    al = a3.reshape(n, w)
    hl = u3.reshape(n, w)
    blocks = w // LANES
    for q in range(blocks):
        la_ref[q] = al[:, q * LANES:(q + 1) * LANES]
        lh_ref[q] = hl[:, q * LANES:(q + 1) * LANES]
    ends = pl.ds(0 if reverse else SUBLANES - 1, groups, stride=SUBLANES)
    end_a = jnp.concatenate([la_ref.at[q][ends, :] for q in range(blocks)], axis=-1)
    end_h = jnp.concatenate([lh_ref.at[q][ends, :] for q in range(blocks)], axis=-1)
    prod, part = (_scan_bwd if reverse else _scan_fwd)(end_a, end_h)
    total = part + prod * edge
    g_row = lax.broadcasted_iota(jnp.int32, total.shape, 0)
    if reverse:
        c_ref[...] = jnp.where(g_row == groups - 1, edge, pltpu.roll(total, groups - 1, 0))
    else:
        c_ref[...] = jnp.where(g_row == 0, edge, pltpu.roll(total, 1, 0))
    for g in range(groups):
        rows = slice(g * SUBLANES, (g + 1) * SUBLANES)
        for q in range(blocks):
            cols = slice(q * LANES, (q + 1) * LANES)
            dst_ref[rows, cols] = lh_ref[q, rows, :] + la_ref[q, rows, :] * c_ref[g:g + 1, cols]


def _softplus_neg(lam):
    z = -lam
    return jnp.maximum(z, 0.0) + jnp.log1p(jnp.exp(-jnp.abs(z)))


def _lru_gates(xc, wx_ref, wa_ref, bx_ref, ba_ref, lam_ref):
    xcb = _c(xc)
    i_t = _sigmoid(_dot(xcb, wx_ref[0]) + bx_ref[...])
    r_t = _sigmoid(_dot(xcb, wa_ref[0]) + ba_ref[...])
    sp = _softplus_neg(lam_ref[...])
    log_a = (-LRU_C) * r_t * sp
    a = jnp.exp(log_a)
    mult = jnp.sqrt(1.0 - a * a)
    return xcb, i_t, r_t, sp, a, mult


def _conv_from_ext(ext_ref, xa, cw_ref, cb_ref, tc):
    return (cb_ref[...] + cw_ref[3:4, :] * xa + cw_ref[2:3, :] * ext_ref[7:7 + tc, :]
            + cw_ref[1:2, :] * ext_ref[6:6 + tc, :] + cw_ref[0:1, :] * ext_ref[5:5 + tc, :])


def _lru_fwd(proj, conv_w, conv_b, wx_bd, wa_bd, bx, ba, lam, B, S):
    T = B * S
    tc = min(SCAN_TILE, S)
    nt = S // tc
    h8 = tc // 8

    def body(xa_ref, halo_ref, ga_ref, cw_ref, cb_ref, wx_ref, wa_ref, bx_ref, ba_ref, lam_ref,
             h_ref, ya_ref, ext_ref, carry_ref, la_ref, lh_ref, c_ref):
        t = pl.program_id(2)

        @pl.when(t == 0)
        def _():
            carry_ref[...] = jnp.zeros_like(carry_ref)

        xa = xa_ref[...]
        ext_ref[0:8, :] = jnp.where(t == 0, 0.0, halo_ref[...])
        ext_ref[8:8 + tc, :] = xa
        xc = _conv_from_ext(ext_ref, xa, cw_ref, cb_ref, tc)
        _, i_t, _, _, a, mult = _lru_gates(xc, wx_ref, wa_ref, bx_ref, ba_ref, lam_ref)
        u = mult * (i_t * xc)
        _scan_tile(a, u, carry_ref[7:8, :], la_ref, lh_ref, c_ref, h_ref, False)
        h = h_ref[...]
        carry_ref[...] = h[tc - 8:tc, :]
        ga = ga_ref[...]
        ya_ref[...] = (ga * _sigmoid(ga) * h).astype(ya_ref.dtype)

    row = lambda b, t: b * nt + t
    vec = pl.BlockSpec((1, CW), lambda b, c, t: (0, c))
    mat = pl.BlockSpec((1, CW, CW), lambda b, c, t: (c, 0, 0))
    return pl.pallas_call(
        body,
        name="lru_fwd",
        grid=(B, N_CT, nt),
        in_specs=[
            pl.BlockSpec((tc, CW), lambda b, c, t: (row(b, t), c)),
            pl.BlockSpec((8, CW), lambda b, c, t: (jnp.maximum(row(b, t) * h8 - 1, 0), c)),
            pl.BlockSpec((tc, CW), lambda b, c, t: (row(b, t), N_CT + c)),
            pl.BlockSpec((CONV, CW), lambda b, c, t: (0, c)),
            vec, mat, mat, vec, vec, vec,
        ],
        out_specs=[
            pl.BlockSpec((tc, CW), lambda b, c, t: (row(b, t), c)),
            pl.BlockSpec((tc, CW), lambda b, c, t: (row(b, t), c)),
        ],
        out_shape=[
            jax.ShapeDtypeStruct((T, D_MODEL), F32),
            jax.ShapeDtypeStruct((T, D_MODEL), _MXU),
        ],
        scratch_shapes=[pltpu.VMEM((tc + 8, CW), F32), pltpu.VMEM((8, CW), F32)] + _scan_scratch(tc)[:3],
        compiler_params=_params(("parallel", "parallel", "arbitrary")),
    )(proj, proj, proj, conv_w, conv_b, wx_bd, wa_bd, bx, ba, lam)


def _lru_bwd(dya, proj, hlru, conv_w, conv_b, wx_bd, wa_bd, bx, ba, lam, B, S, deps=()):
    T = B * S
    tc = min(SCAN_TILE, S)
    nt = S // tc
    h8 = tc // 8

    def body(dya_ref, xa_ref, xhalo_ref, ga_ref, h_ref, hhalo_ref, cw_ref, cb_ref, wx_ref, wa_ref, bx_ref, ba_ref,
             lam_ref, dxa_ref, dga_ref, dcw_ref, dcb_ref, dwx_ref, dwa_ref, dbx_ref, dba_ref, dlam_ref,
             ext_ref, ext2_ref, carry_ref, dhalo_ref, la_ref, lh_ref, c_ref, dh_ref, accx_ref, acca_ref):
        b = pl.program_id(1)
        t = pl.program_id(2)
        tt = nt - 1 - t

        @pl.when(t == 0)
        def _():
            carry_ref[...] = jnp.zeros_like(carry_ref)
            dhalo_ref[...] = jnp.zeros_like(dhalo_ref)

        @pl.when((t == 0) & (b == 0))
        def _():
            for r in (dcw_ref, dcb_ref, accx_ref, acca_ref, dbx_ref, dba_ref, dlam_ref):
                r[...] = jnp.zeros_like(r)

        xa = xa_ref[...]
        ext_ref[0:8, :] = jnp.where(tt == 0, 0.0, xhalo_ref[...])
        ext_ref[8:8 + tc, :] = xa
        xc = _conv_from_ext(ext_ref, xa, cw_ref, cb_ref, tc)
        xcb, i_t, r_t, sp, a, mult = _lru_gates(xc, wx_ref, wa_ref, bx_ref, ba_ref, lam_ref)

        h = h_ref[...]
        ga = ga_ref[...]
        dya_t = dya_ref[...]
        sg = _sigmoid(ga)
        dga_ref[...] = (dya_t * h * (sg * (1.0 + ga * (1.0 - sg)))).astype(dga_ref.dtype)
        dlru = dya_t * (ga * sg)

        row = lax.broadcasted_iota(jnp.int32, a.shape, 0)
        coef = jnp.where(row == tc - 1, 1.0, pltpu.roll(a, tc - 1, 0))
        _scan_tile(coef, dlru, carry_ref[0:1, :], la_ref, lh_ref, c_ref, dh_ref, True)
        dh = dh_ref[...]
        ext2_ref[0:tc, :] = a * dh
        carry_ref[...] = ext2_ref[0:8, :]

        ext2_ref[0:8, :] = jnp.where(tt == 0, 0.0, hhalo_ref[...])
        ext2_ref[8:8 + tc, :] = h
        hprev = ext2_ref[7:7 + tc, :]

        da = dh * hprev
        ix = i_t * xc
        dmult = dh * ix
        di = dh * mult * xc
        dxc = dh * mult * i_t
        dlog_a = da * a - dmult * (a * a) / mult
        dr = dlog_a * ((-LRU_C) * sp)
        dlam_ref[...] += jnp.sum(dlog_a * r_t, axis=0, keepdims=True) * (LRU_C * _sigmoid(-lam_ref[...]))
        dza = dr * r_t * (1.0 - r_t)
        dzx = di * i_t * (1.0 - i_t)
        dzab = _c(dza)
        dzxb = _c(dzx)
        dxc = dxc + _dot_nt(dzxb, wx_ref[0]) + _dot_nt(dzab, wa_ref[0])
        accx_ref[...] += _dot_tn(xcb, dzxb)
        acca_ref[...] += _dot_tn(xcb, dzab)
        dbx_ref[...] += jnp.sum(dzx, axis=0, keepdims=True)
        dba_ref[...] += jnp.sum(dza, axis=0, keepdims=True)

        dcb_ref[...] += jnp.sum(dxc, axis=0, keepdims=True)
        dcw_ref[3:4, :] += jnp.sum(dxc * xa, axis=0, keepdims=True)
        dcw_ref[2:3, :] += jnp.sum(dxc * ext_ref[7:7 + tc, :], axis=0, keepdims=True)
        dcw_ref[1:2, :] += jnp.sum(dxc * ext_ref[6:6 + tc, :], axis=0, keepdims=True)
        dcw_ref[0:1, :] += jnp.sum(dxc * ext_ref[5:5 + tc, :], axis=0, keepdims=True)
        ext2_ref[0:tc, :] = dxc
        ext2_ref[tc:tc + 8, :] = dhalo_ref[...]
        dxa = (cw_ref[3:4, :] * dxc + cw_ref[2:3, :] * ext2_ref[1:1 + tc, :]
               + cw_ref[1:2, :] * ext2_ref[2:2 + tc, :] + cw_ref[0:1, :] * ext2_ref[3:3 + tc, :])
        dxa_ref[...] = dxa.astype(dxa_ref.dtype)
        dhalo_ref[...] = ext2_ref[0:8, :]

        @pl.when((b == B - 1) & (t == nt - 1))
        def _():
            lane_block = lax.broadcasted_iota(jnp.int32, (LRU_BW, CW), 1) // LRU_BW
            for acc_ref, out_ref in ((accx_ref, dwx_ref), (acca_ref, dwa_ref)):
                diag = jnp.zeros((LRU_BW, CW), F32)
                for j in range(CW // LRU_BW):
                    diag = jnp.where(lane_block == j, acc_ref[j * LRU_BW:(j + 1) * LRU_BW, :], diag)
                for q in range(CW // LANES):
                    out_ref[0, q] = diag[:, q * LANES:(q + 1) * LANES]

    row_of = lambda b, t: b * nt + (nt - 1 - t)
    tile = lambda off: pl.BlockSpec((tc, CW), lambda c, b, t: (row_of(b, t), off + c))
    halo = pl.BlockSpec((8, CW), lambda c, b, t: (jnp.maximum(row_of(b, t) * h8 - 1, 0), c))
    vec = pl.BlockSpec((1, CW), lambda c, b, t: (0, c))
    mat = pl.BlockSpec((1, CW, CW), lambda c, b, t: (c, 0, 0))
    cwspec = pl.BlockSpec((CONV, CW), lambda c, b, t: (0, c))
    diag = pl.BlockSpec((1, CW // LANES, LRU_BW, LANES), lambda c, b, t: (c, 0, 0, 0))
    return pl.pallas_call(
        _after(body, 13, deps),
        name="lru_bwd",
        grid=(N_CT, B, nt),
        in_specs=[tile(0), tile(0), halo, tile(N_CT), tile(0), halo, cwspec, vec, mat, mat, vec, vec, vec]
        + [ANY_SPEC] * len(deps),
        out_specs=[tile(0), tile(0), cwspec, vec, diag, diag, vec, vec, vec],
        out_shape=[
            jax.ShapeDtypeStruct((T, D_MODEL), _MXU),
            jax.ShapeDtypeStruct((T, D_MODEL), _MXU),
            jax.ShapeDtypeStruct((CONV, D_MODEL), F32),
            jax.ShapeDtypeStruct((1, D_MODEL), F32),
            jax.ShapeDtypeStruct((N_CT, CW // LANES, LRU_BW, LANES), F32),
            jax.ShapeDtypeStruct((N_CT, CW // LANES, LRU_BW, LANES), F32),
            jax.ShapeDtypeStruct((1, D_MODEL), F32),
            jax.ShapeDtypeStruct((1, D_MODEL), F32),
            jax.ShapeDtypeStruct((1, D_MODEL), F32),
        ],
        scratch_shapes=[pltpu.VMEM((tc + 8, CW), F32), pltpu.VMEM((tc + 8, CW), F32),
                        pltpu.VMEM((8, CW), F32), pltpu.VMEM((8, CW), F32)] + _scan_scratch(tc)
        + [pltpu.VMEM((CW, CW), F32), pltpu.VMEM((CW, CW), F32)],
        compiler_params=_params(("parallel", "arbitrary", "arbitrary")),
    )(dya, proj, proj, proj, hlru, hlru, conv_w, conv_b, wx_bd, wa_bd, bx, ba, lam, *deps)


def _retention_tables(S):
    half = DK // 2
    freqs = ROPE_THETA ** (-jnp.arange(half, dtype=F32) / half)
    ang = jnp.arange(S, dtype=F32)[:, None] * freqs[None, :]
    log_g = jnp.log1p(-(2.0 ** (-5.0 - jnp.arange(HEADS, dtype=F32))))
    idx = jnp.arange(CHUNK, dtype=F32)
    diff = idx[:, None] - idx[None, :]
    inner = jnp.where(diff >= 0, jnp.exp(jnp.maximum(diff, 0.0)[None] * log_g[:, None, None]), 0.0)
    cross = jnp.exp((idx[None, :] + 1.0) * log_g[:, None])[:, :, None]
    state = jnp.exp((CHUNK - 1.0 - idx[None, :]) * log_g[:, None])[:, :, None]
    gam = jnp.broadcast_to(jnp.exp(CHUNK * log_g)[:, None, None], (HEADS, 1, DK))
    return jnp.cos(ang), jnp.sin(ang), inner, cross, state, gam


def _rot(x, cos, sin):
    half = DK // 2
    x1, x2 = x[:, :half], x[:, half:]
    return jnp.concatenate([x1 * cos - x2 * sin, x1 * sin + x2 * cos], axis=-1)


def _rot_t(y, cos, sin):
    half = DK // 2
    y1, y2 = y[:, :half], y[:, half:]
    return jnp.concatenate([y1 * cos + y2 * sin, y2 * cos - y1 * sin], axis=-1)


def _groupnorm(o):
    mu = jnp.mean(o, axis=-1, keepdims=True)
    oc = o - mu
    rs = lax.rsqrt(jnp.mean(oc * oc, axis=-1, keepdims=True) + EPS)
    return oc * rs, rs


def _ret_specs(B, chunk_of):
    rows = RET_CHUNKS * CHUNK
    qkv = lambda g: pl.BlockSpec((B, rows, D_MODEL), lambda c: (0, chunk_of(c), g))
    act = pl.BlockSpec((B, rows, D_MODEL), lambda c: (0, chunk_of(c), 0))
    rope = pl.BlockSpec((rows, DK // 2), lambda c: (chunk_of(c), 0))
    dmat = pl.BlockSpec((HEADS, CHUNK, CHUNK), lambda c: (0, 0, 0))
    dvec = pl.BlockSpec((HEADS, CHUNK, 1), lambda c: (0, 0, 0))
    hrow = pl.BlockSpec((HEADS, 1, DK), lambda c: (0, 0, 0))
    rst = pl.BlockSpec((RET_CHUNKS, B, HEADS, DK, DK), lambda c: (chunk_of(c), 0, 0, 0, 0))
    return qkv, act, rope, dmat, dvec, hrow, rst


def _ret_fwd(proj, tables, gain3, B, S):
    T = B * S
    nc = S // CHUNK
    cos, sin, dmat_t, cd_t, sd_t, gam_t = tables

    def body(q_ref, k_ref, v_ref, gb_ref, cos_ref, sin_ref, dm_ref, cd_ref, sd_ref, gam_ref, gain_ref,
             o_ref, yb_ref, rs_ref, state_ref):
        @pl.when(pl.program_id(0) == 0)
        def _():
            state_ref[...] = jnp.zeros_like(state_ref)

        for cc, b, h in [(cc, b, h) for cc in range(RET_CHUNKS) for b in range(B) for h in range(HEADS)]:
            rows = slice(cc * CHUNK, (cc + 1) * CHUNK)
            cos_t, sin_t = cos_ref[rows, :], sin_ref[rows, :]
            cols = slice(h * DK, (h + 1) * DK)
            qb = _c(_rot(q_ref[b, rows, cols], cos_t, sin_t))
            kb = _c(_rot(k_ref[b, rows, cols], cos_t, sin_t) * (DK ** -0.5))
            v = v_ref[b, rows, cols]
            state = state_ref[b, h]
            sb = _c(state)
            rs_ref[cc, b, h] = sb
            scores = _dot_nt(qb, kb) * dm_ref[h]
            o = _dot(_c(scores), _c(v)) + _dot(qb, sb) * cd_ref[h]
            state_ref[b, h] = gam_ref[h] * state + _dot_tn(kb, _c(v * sd_ref[h]))
            o_ref[b, rows, cols] = o
            n, _ = _groupnorm(o)
            gb = gb_ref[b, rows, cols]
            yb_ref[b, rows, cols] = (gb * _sigmoid(gb) * (n * gain_ref[h])).astype(yb_ref.dtype)

    qkv, act, rope, dmat, dvec, hrow, rst = _ret_specs(B, lambda c: c)
    proj3 = proj.reshape(B, S, proj.shape[1])
    o_pre, yb, states = pl.pallas_call(
        body,
        name="ret_fwd",
        grid=(nc // RET_CHUNKS,),
        in_specs=[qkv(2), qkv(3), qkv(4), qkv(5), rope, rope, dmat, dvec, dvec, hrow, hrow],
        out_specs=[act, act, rst],
        out_shape=[
            jax.ShapeDtypeStruct((B, S, D_MODEL), F32),
            jax.ShapeDtypeStruct((B, S, D_MODEL), _MXU),
            jax.ShapeDtypeStruct((nc, B, HEADS, DK, DK), _MXU),
        ],
        scratch_shapes=[pltpu.VMEM((B, HEADS, DK, DK), F32)],
        compiler_params=_params(("arbitrary",)),
    )(proj3, proj3, proj3, proj3, cos, sin, dmat_t, cd_t, sd_t, gam_t, gain3)
    return o_pre.reshape(T, D_MODEL), yb.reshape(T, D_MODEL), states


def _ret_bwd(dyb, o_pre, proj, states, tables, gain3, B, S, deps=()):
    T = B * S
    nc = S // CHUNK
    cos, sin, dmat_t, cd_t, sd_t, gam_t = tables

    def body(dyb_ref, o_ref, q_ref, k_ref, v_ref, gb_ref, rs_ref, cos_ref, sin_ref, dm_ref, cd_ref, sd_ref, gam_ref,
             gain_ref, dr_ref, dgain_ref, dstate_ref):
        @pl.when(pl.program_id(0) == 0)
        def _():
            dstate_ref[...] = jnp.zeros_like(dstate_ref)
            dgain_ref[...] = jnp.zeros_like(dgain_ref)

        for cc, b, h in [(cc, b, h) for cc in reversed(range(RET_CHUNKS)) for b in range(B) for h in range(HEADS)]:
            rows = slice(cc * CHUNK, (cc + 1) * CHUNK)
            cos_t, sin_t = cos_ref[rows, :], sin_ref[rows, :]
            cols = slice(h * DK, (h + 1) * DK)
            gain = gain_ref[h]
            n, rs = _groupnorm(o_ref[b, rows, cols])
            gb = gb_ref[b, rows, cols]
            sg = _sigmoid(gb)
            dy = dyb_ref[b, rows, cols]
            part = lambda g: slice(g * D_MODEL + h * DK, g * D_MODEL + (h + 1) * DK)
            dr_ref[b, rows, part(3)] = (dy * (n * gain) * (sg * (1.0 + gb * (1.0 - sg)))).astype(dr_ref.dtype)
            dgn = dy * (gb * sg)
            dgain_ref[h] += jnp.sum(dgn * n, axis=0, keepdims=True)
            dn = dgn * gain
            do = rs * (dn - jnp.mean(dn, axis=-1, keepdims=True) - n * jnp.mean(dn * n, axis=-1, keepdims=True))

            qb = _c(_rot(q_ref[b, rows, cols], cos_t, sin_t))
            kb = _c(_rot(k_ref[b, rows, cols], cos_t, sin_t) * (DK ** -0.5))
            v = v_ref[b, rows, cols]
            vb = _c(v)
            vsb = _c(v * sd_ref[h])
            dob = _c(do)
            docb = _c(do * cd_ref[h])
            dmat = dm_ref[h]
            dstate = dstate_ref[b, h]
            dsb = _c(dstate)
            pb = _c(_dot_nt(qb, kb) * dmat)
            dsc = _c(_dot_nt(dob, vb) * dmat)
            dq = _dot(dsc, kb) + _dot_nt(docb, rs_ref[cc, b, h])
            dk = _dot_tn(dsc, qb) + _dot_nt(vsb, dsb)
            dv = _dot_tn(pb, dob) + _dot(kb, dsb) * sd_ref[h]
            dstate_ref[b, h] = gam_ref[h] * dstate + _dot_tn(qb, docb)
            dr_ref[b, rows, part(0)] = _rot_t(dq, cos_t, sin_t).astype(dr_ref.dtype)
            dr_ref[b, rows, part(1)] = (_rot_t(dk, cos_t, sin_t) * (DK ** -0.5)).astype(dr_ref.dtype)
            dr_ref[b, rows, part(2)] = dv.astype(dr_ref.dtype)

    n_steps = nc // RET_CHUNKS
    qkv, act, rope, dmat, dvec, hrow, rst = _ret_specs(B, lambda c: n_steps - 1 - c)
    wide = pl.BlockSpec((B, RET_CHUNKS * CHUNK, 4 * D_MODEL), lambda c: (0, n_steps - 1 - c, 0))
    proj3 = proj.reshape(B, S, proj.shape[1])
    dr, dgain = pl.pallas_call(
        _after(body, 14, deps),
        name="ret_bwd",
        grid=(n_steps,),
        in_specs=[act, act, qkv(2), qkv(3), qkv(4), qkv(5), rst, rope, rope, dmat, dvec, dvec, hrow, hrow]
        + [ANY_SPEC] * len(deps),
        out_specs=[wide, hrow],
        out_shape=[jax.ShapeDtypeStruct((B, S, 4 * D_MODEL), _MXU), jax.ShapeDtypeStruct((HEADS, 1, DK), F32)],
        scratch_shapes=[pltpu.VMEM((B, HEADS, DK, DK), F32)],
        compiler_params=_params(("arbitrary",)),
    )(dyb.reshape(B, S, D_MODEL), o_pre.reshape(B, S, D_MODEL), proj3, proj3, proj3, proj3, states, cos, sin, dmat_t,
      cd_t, sd_t, gam_t, gain3, *deps)
    return dr.reshape(T, 4 * D_MODEL), dgain


def _mid(ya, yb, proj, x2d, tgt2d, wpa, wpb, wout, g_fin):
    T = x2d.shape[0]
    tm = min(MID_TILE, T)
    n_steps = T // tm
    rows = D_MODEL // (2 * N_CHIPS)

    def body(ya_ref, yb_ref, ma_ref, mb_ref, x_ref, t_ref, gf_ref, wpa_hbm, wpb_hbm, wout_hbm,
             loss_ref, dx2_ref, dya_ref, dyb_ref, dm_ref, dgf_ref, gw_hbm, w_ref, acc_ref, sem):
        i = pl.program_id(0)

        @pl.when(i == 0)
        def _():
            loads = [pltpu.make_async_copy(src, w_ref.at[k], sem.at[k]) for k, src in enumerate((wpa_hbm, wpb_hbm, wout_hbm))]
            for cp in loads:
                cp.start()
            for cp in loads:
                cp.wait()
            acc_ref[...] = jnp.zeros_like(acc_ref)
            loss_ref[...] = jnp.zeros_like(loss_ref)
            dgf_ref[...] = jnp.zeros_like(dgf_ref)

        ya_t, yb_t = ya_ref[...], yb_ref[...]
        out_a = _dot(ya_t, w_ref[0])
        out_b = _dot(yb_t, w_ref[1])
        sa = _sigmoid(ma_ref[...])
        sb = _sigmoid(mb_ref[...])
        mgb = _c(sa * out_a + sb * out_b)
        x2 = x_ref[...] + _dot(mgb, w_ref[2])
        r2 = lax.rsqrt(jnp.mean(x2 * x2, axis=-1, keepdims=True) + EPS)
        nx = x2 * r2
        gf = gf_ref[...]
        err = nx * gf - t_ref[...]
        loss_ref[...] += 0.5 * jnp.sum(jnp.mean(err * err, axis=-1, keepdims=True), axis=0, keepdims=True)
        dy = err * (1.0 / D_MODEL)
        dgf_ref[...] += jnp.sum(dy * nx, axis=0, keepdims=True)
        dyg = dy * gf
        dx2 = r2 * (dyg - nx * jnp.mean(dyg * nx, axis=-1, keepdims=True))
        dx2_ref[...] = dx2
        dx2b = _c(dx2)
        dmg = _dot_nt(dx2b, w_ref[2])
        acc_ref[2] += _dot_tn(mgb, dx2b)
        dm_ref[:, :D_MODEL] = (dmg * out_a * sa * (1.0 - sa)).astype(dm_ref.dtype)
        dm_ref[:, D_MODEL:] = (dmg * out_b * sb * (1.0 - sb)).astype(dm_ref.dtype)
        dab = _c(dmg * sa)
        dbb = _c(dmg * sb)
        dya_ref[...] = _dot_nt(dab, w_ref[0])
        dyb_ref[...] = _dot_nt(dbb, w_ref[1])
        acc_ref[0] += _dot_tn(ya_t, dab)
        acc_ref[1] += _dot_tn(yb_t, dbb)

        @pl.when(i == n_steps - 1)
        def _():
            copies = [pltpu.make_async_copy(acc_ref.at[k, pl.ds((2 * p + hf) * rows, rows), :], gw_hbm.at[p, hf, k],
                                            sem.at[(k * N_CHIPS + p) * 2 + hf])
                      for k in range(3) for p in range(N_CHIPS) for hf in range(2)]
            for cp in copies:
                cp.start()
            for cp in copies:
                cp.wait()

    tile = lambda j: pl.BlockSpec((tm, D_MODEL), lambda i: (i, j))
    one = pl.BlockSpec((1, D_MODEL), lambda i: (0, 0))
    anyspec = pl.BlockSpec(memory_space=pl.ANY)
    return pl.pallas_call(
        body,
        name="mid",
        grid=(n_steps,),
        in_specs=[tile(0), tile(0), tile(6), tile(7), tile(0), tile(0), one, anyspec, anyspec, anyspec],
        out_specs=[pl.BlockSpec((1, 1), lambda i: (0, 0)), tile(0), tile(0), tile(0),
                   pl.BlockSpec((tm, 2 * D_MODEL), lambda i: (i, 0)), one, anyspec],
        out_shape=[
            jax.ShapeDtypeStruct((1, 1), F32),
            jax.ShapeDtypeStruct((T, D_MODEL), F32),
            jax.ShapeDtypeStruct((T, D_MODEL), F32),
            jax.ShapeDtypeStruct((T, D_MODEL), F32),
            jax.ShapeDtypeStruct((T, 2 * D_MODEL), _MXU),
            jax.ShapeDtypeStruct((1, D_MODEL), F32),
            jax.ShapeDtypeStruct((N_CHIPS, 2, 3, rows, D_MODEL), F32),
        ],
        scratch_shapes=[pltpu.VMEM((3, D_MODEL, D_MODEL), _MXU), pltpu.VMEM((3, D_MODEL, D_MODEL), F32),
                        pltpu.SemaphoreType.DMA((3 * N_CHIPS * 2,))],
        compiler_params=_params(("arbitrary",)),
    )(ya, yb, proj, proj, x2d, tgt2d, g_fin, wpa, wpb, wout)


def _inproj_bwd_dx(dparts, w_all, x2d, dx2, g_in, first, count, prev, name, deps=()):
    T = x2d.shape[0]
    tm = min(DX_TILE, T)
    n_d = len(dparts)
    groups = [(a, k) for a, d in enumerate(dparts) for k in range(d.shape[1] // D_MODEL)]
    dg_start = jnp.zeros((1, D_MODEL), F32) if prev is None else prev[1]
    carried = () if prev is None else (prev[0],)

    def body(*refs):
        d_refs = refs[:n_d]
        x_ref, dx2_ref, g_ref, dg0_ref, w_hbm = refs[n_d:n_d + 5]
        dx_ref, dg_ref, w_ref, sem = refs[-4:]

        def load(j):
            part = (j // 2, slice(None), pl.ds((j % 2) * D_MODEL, D_MODEL))
            return pltpu.make_async_copy(w_hbm.at[part], w_ref.at[part], sem.at[j])

        def tile(before_group):
            dh = jnp.zeros((tm, D_MODEL), F32)
            for j, (a, k) in enumerate(groups):
                before_group(j)
                dh = dh + _dot_nt(d_refs[a][:, k * D_MODEL:(k + 1) * D_MODEL],
                                  w_ref[j // 2, :, (j % 2) * D_MODEL:(j % 2 + 1) * D_MODEL])
            x = x_ref[...]
            r = lax.rsqrt(jnp.mean(x * x, axis=-1, keepdims=True) + EPS)
            nx = x * r
            dg_ref[...] += jnp.sum(dh * nx, axis=0, keepdims=True)
            dhg = dh * g_ref[...]
            dx_ref[...] = dx2_ref[...] + r * (dhg - nx * jnp.mean(dhg * nx, axis=-1, keepdims=True))

        first = pl.program_id(0) == 0

        @pl.when(first)
        def _():
            for j in range(len(groups)):
                load(j).start()
            dg_ref[...] = dg0_ref[...]
            tile(lambda j: load(j).wait())

        @pl.when(jnp.logical_not(first))
        def _():
            tile(lambda j: None)

    tile = pl.BlockSpec((tm, D_MODEL), lambda i: (first + i, 0))
    one = pl.BlockSpec((1, D_MODEL), lambda i: (0, 0))
    return pl.pallas_call(
        body,
        name=name,
        grid=(count,),
        in_specs=[pl.BlockSpec((tm, d.shape[1]), lambda i: (first + i, 0)) for d in dparts]
        + [tile, tile, one, one, ANY_SPEC] + [ANY_SPEC] * (len(carried) + len(deps)),
        out_specs=[tile, one],
        out_shape=[jax.ShapeDtypeStruct((T, D_MODEL), F32), jax.ShapeDtypeStruct((1, D_MODEL), F32)],
        input_output_aliases={n_d + 5: 0} if carried else {},
        scratch_shapes=[pltpu.VMEM(w_all.shape, w_all.dtype), pltpu.SemaphoreType.DMA((len(groups),))],
        compiler_params=_params(("arbitrary",)),
    )(*dparts, x2d, dx2, g_in, dg_start, w_all, *carried, *deps)


def _inproj_bwd_dw(ht, dparts, name, deps=()):
    T = ht.shape[1]
    tn = DW_COLS
    half = D_MODEL // 2
    per_chip = 2 * D_MODEL // tn
    n_d = len(dparts)
    tiles = [(a, t) for a, d in enumerate(dparts) for t in range(d.shape[1] // tn)]
    offs = [sum(d.shape[1] // tn for d in dparts[:a]) for a in range(n_d)]

    def body(*refs):
        ht_hbm = refs[0]
        d_refs = refs[1:1 + n_d]
        out_ref, ht_ref, sem = refs[-3:]
        t = pl.program_id(0)

        def load(k):
            cols = pl.ds(k * (T // DW_LOADS), T // DW_LOADS)
            return pltpu.make_async_copy(ht_hbm.at[:, cols], ht_ref.at[:, cols], sem.at[k])

        def store(g):
            out_ref[0, 0] = g[:half]
            out_ref[0, 1] = g[half:]

        @pl.when(t == 0)
        def _():
            for k in range(DW_LOADS):
                load(k).start()
            g = jnp.zeros((D_MODEL, tn), F32)
            for k in range(DW_LOADS):
                load(k).wait()
                tokens = slice(k * (T // DW_LOADS), (k + 1) * (T // DW_LOADS))
                g = g + _dot(ht_ref[:, tokens], d_refs[0][tokens, :])
            store(g)

        for a in range(n_d):
            lo, hi = max(offs[a], 1), offs[a] + dparts[a].shape[1] // tn

            @pl.when((t >= lo) & (t < hi))
            def _(a=a):
                store(_dot(ht_ref[...], d_refs[a][...]))

    def dspec(a):
        n_a = dparts[a].shape[1] // tn
        return pl.BlockSpec((T, tn), lambda t: (0, jnp.clip(t - offs[a], 0, n_a - 1)))

    return pl.pallas_call(
        body,
        name=name,
        grid=(len(tiles),),
        in_specs=[ANY_SPEC] + [dspec(a) for a in range(n_d)] + [ANY_SPEC] * len(deps),
        out_specs=pl.BlockSpec((1, 2, half, tn), lambda t: (t // per_chip, 0, 0, t % per_chip)),
        out_shape=jax.ShapeDtypeStruct((len(tiles) // per_chip, 2, half, 2 * D_MODEL), F32),
        scratch_shapes=[pltpu.VMEM(ht.shape, ht.dtype), pltpu.SemaphoreType.DMA((DW_LOADS,))],
        compiler_params=_params(("arbitrary",)),
    )(ht, *dparts, *deps)


def _coords():
    return lax.axis_index("x"), lax.axis_index("y"), lax.axis_index("c")


def _other_chips(x, y):
    return [(1 - x, y), (x, 1 - y), (1 - x, 1 - y)]


def _chunks(rows, n):
    size = rows // n
    return [pl.ds(q * size, size) for q in range(n)]


HBM_SPEC = pl.BlockSpec(memory_space=pltpu.HBM)
SEM_SPEC = pl.BlockSpec(memory_space=pltpu.SEMAPHORE)
DATAFLOW = pltpu.SideEffectType.DATAFLOW_SIDE_EFFECTING


def _copies_start(bufs, plan, n_copies, name, deps=()):
    n = len(bufs)
    n_deps = len(deps)

    def body(*refs):
        ins = refs[:n]
        send_sems, recv_sems = refs[n + n_deps], refs[n + n_deps + 1]
        token = refs[-1]
        for k, send, _ in plan(ins):
            if send is not None:
                src, dst, dev, pred = send
                cp = pltpu.make_async_remote_copy(src_ref=src, dst_ref=dst, send_sem=send_sems.at[k],
                                                  recv_sem=recv_sems.at[k], device_id=dev, device_id_type=MESH)
                if pred is None:
                    cp.start()
                else:
                    pl.when(pred)(cp.start)
        token[...] = jnp.zeros_like(token)

    hbm = [pltpu.with_memory_space_constraint(b, pltpu.HBM) for b in bufs]
    outs = pl.pallas_call(
        body,
        name=name,
        in_specs=[HBM_SPEC] * n + [ANY_SPEC] * n_deps,
        out_specs=(SEM_SPEC, SEM_SPEC, *([HBM_SPEC] * n), pl.BlockSpec(memory_space=pltpu.VMEM)),
        out_shape=(pltpu.SemaphoreType.DMA((n_copies,)), pltpu.SemaphoreType.DMA((n_copies,)),
                   *[pltpu.HBM(b.shape, b.dtype) for b in bufs], jax.ShapeDtypeStruct((8, 128), F32)),
        input_output_aliases={a: 2 + a for a in range(n)},
        compiler_params=pltpu.CompilerParams(has_side_effects=DATAFLOW),
    )(*hbm, *deps)
    return outs[0], outs[1], list(outs[2:2 + n]), outs[-1]


def _copies_wait(send_sems, recv_sems, bufs, after, plan, name, only=None):
    n = len(bufs)

    def body(*refs):
        ins = refs[:n]
        s_sems, r_sems = refs[n], refs[n + 1]
        for k, send, recv in plan(ins):
            if only is not None and k not in only:
                continue
            if send is not None:
                src, dst, dev, pred = send
                cp = pltpu.make_async_remote_copy(src_ref=src, dst_ref=dst, send_sem=s_sems.at[k],
                                                  recv_sem=r_sems.at[k], device_id=dev, device_id_type=MESH)
                if pred is None:
                    cp.wait_send()
                else:
                    pl.when(pred)(cp.wait_send)
            if recv is not None:
                dst, pred = recv
                cp = pltpu.make_async_remote_copy(src_ref=dst, dst_ref=dst, send_sem=s_sems.at[k],
                                                  recv_sem=r_sems.at[k], device_id=_coords(), device_id_type=MESH)
                if pred is None:
                    cp.wait_recv()
                else:
                    pl.when(pred)(cp.wait_recv)

    outs = pl.pallas_call(
        body,
        name=name,
        in_specs=[HBM_SPEC] * n + [SEM_SPEC, SEM_SPEC, pl.BlockSpec(memory_space=pl.ANY)],
        out_specs=[HBM_SPEC] * n,
        out_shape=[pltpu.HBM(b.shape, b.dtype) for b in bufs],
        input_output_aliases={a: a for a in range(n)},
        compiler_params=pltpu.CompilerParams(has_side_effects=DATAFLOW),
    )(*bufs, send_sems, recv_sems, after)
    return list(outs)


def _gather_plan(n_bufs):
    def plan(refs):
        x, y, c = _coords()
        me = 2 * x + y
        out = []
        for k, (px, py) in enumerate(_other_chips(x, y)):
            for a in range(n_bufs):
                out.append((k * n_bufs + a, (refs[a].at[me], refs[a].at[me], (px, py, c), None),
                            (refs[a].at[2 * px + py], None)))
        return out
    return plan


def _cast_into_slot(ws, name, deps=()):
    n = len(ws)
    nt = 2

    def body(s_ref, *refs):
        outs = refs[len(refs) - n:]
        for a in range(n):
            outs[a][0] = refs[a][...].astype(outs[a].dtype)

    xi, yi, _ = _coords()
    return pl.pallas_call(
        body,
        name=name,
        grid_spec=pltpu.PrefetchScalarGridSpec(
            num_scalar_prefetch=1,
            grid=(2, nt),
            in_specs=[pl.BlockSpec((1, w.shape[1] // nt, w.shape[2]), lambda hf, i, s: (hf, i, 0)) for w in ws]
            + [ANY_SPEC] * len(deps),
            out_specs=[pl.BlockSpec((1, 1, w.shape[1] // nt, w.shape[2]), lambda hf, i, s: (s[0], hf, i, 0)) for w in ws],
        ),
        out_shape=[jax.ShapeDtypeStruct((N_CHIPS,) + w.shape, _MXU) for w in ws],
        compiler_params=_params(("parallel", "parallel")),
    )((2 * xi + yi).reshape(1).astype(jnp.int32), *ws, *deps)


def _chip_gather_plan(stage, n_bufs):
    def plan(refs):
        x, y, c = _coords()
        me = 2 * x + y
        near = [(1 - x, y), (x, 1 - y)]
        slots = [2 * (1 - x) + y, 2 * x + (1 - y), 2 * (1 - x) + (1 - y)]
        sibling = (x, y, 1 - c)
        pass_to = (jnp.where(c == 0, x, 1 - x), jnp.where(c == 0, 1 - y, y), c)
        pass_slot = jnp.where(c == 0, slots[0], slots[1])
        out = []

        def move(src_slot, to, land_slot, land_core, pieces):
            for a, buf in enumerate(refs):
                for rows in _chunks(buf.shape[2], pieces[a]):
                    out.append((len(out), (buf.at[src_slot, c, rows], buf.at[src_slot, c, rows], to, None),
                                (buf.at[land_slot, land_core, rows], None)))

        if stage == "near":
            for k, chip in enumerate(near):
                move(me, (*chip, c), slots[k], c, NEAR_PIECES[:n_bufs])
        elif stage == "pass":
            move(pass_slot, pass_to, slots[2], c, PASS_PIECES[:n_bufs])
            for k in range(2):
                move(slots[k], sibling, slots[k], 1 - c, [1] * n_bufs)
        else:
            move(slots[2], sibling, slots[2], 1 - c, [1] * n_bufs)
        return out
    return plan


NEAR_PIECES = (2, 1)
PASS_PIECES = (2, 1)


def _chip_gather_copies(stage, n_bufs):
    if stage == "near":
        return 2 * sum(NEAR_PIECES[:n_bufs]), None
    if stage == "pass":
        n_pass = sum(PASS_PIECES[:n_bufs])
        return n_pass + 2 * n_bufs, set(range(n_pass))
    return n_bufs, None


def _swap_plan(n_slabs):
    def plan(refs):
        x, y, c = _coords()
        out, k = [], 0
        for i, n in enumerate(n_slabs):
            g, land = refs[2 * i], refs[2 * i + 1]
            for p in range(n):
                out.append((k, (g.at[p, 1 - c], land.at[p], (x, y, 1 - c), None), (land.at[p], None)))
                k += 1
        return out
    return plan


def _is_one_of(chip, dests):
    hit = chip == dests[0]
    for d in dests[1:]:
        hit = hit | (chip == d)
    return hit


def _slab_of(chip, dests):
    return sum(j * (chip == d).astype(jnp.int32) for j, d in enumerate(dests))


def _scatter_plan(dest_sets):
    def plan(refs):
        x, y, c = _coords()
        me = 2 * x + y
        out = []
        for k, (px, py) in enumerate(_other_chips(x, y)):
            peer = 2 * px + py
            for i, dests in enumerate(dest_sets):
                cs, land = refs[2 * i], refs[2 * i + 1]
                everyone = len(dests) == N_CHIPS
                send = (cs.at[_slab_of(peer, dests)], land.at[k], (px, py, c),
                        None if everyone else _is_one_of(peer, dests))
                recv = (land.at[k], None if everyone else _is_one_of(me, dests))
                out.append((k * len(dest_sets) + i, send, recv))
        return out
    return plan


def _join_plan(rows, n_pieces):
    def plan(refs):
        x, y, c = _coords()
        (buf,) = refs
        return [(i, (buf.at[c, piece], buf.at[c, piece], (x, y, 1 - c), None), (buf.at[1 - c, piece], None))
                for i, piece in enumerate(_chunks(rows, n_pieces))]
    return plan


def _join_plans(parts):
    def plan(refs):
        out, b0, k0 = [], 0, 0
        for part_plan, n_bufs, n_copies in parts:
            out += [(k0 + k, send, recv) for k, send, recv in part_plan(refs[b0:b0 + n_bufs])]
            b0 += n_bufs
            k0 += n_copies
        return out
    return plan


def _allgather_plan():
    def plan(refs):
        x, y, c = _coords()
        (land,) = refs
        me = 4 * x + 2 * y + c
        out = []
        for r in range(1, 8):
            px = 1 - x if r & 4 else x
            py = 1 - y if r & 2 else y
            pc = 1 - c if r & 1 else c
            out.append((r - 1, (land.at[me], land.at[me], (px, py, pc), None), (land.at[4 * px + 2 * py + pc], None)))
        return out
    return plan


def _sum_gathered(land, shapes, name):
    m = land.shape[1]

    def body(land_ref, *refs):
        outs, acc_ref = refs[:-1], refs[-1]
        acc = land_ref[0]
        for d in range(1, 8):
            acc = acc + land_ref[d]
        acc_ref[...] = acc
        r = 0
        for o_ref, shape in zip(outs, shapes):
            if len(shape) == 3:
                n_blocks, bw, _ = shape
                side = LANES // bw
                for blk in range(n_blocks):
                    first = r + (blk // side) * bw
                    o_ref[blk] = acc_ref[first:first + bw, (blk % side) * bw:(blk % side + 1) * bw]
                r += n_blocks * bw // side
                continue
            n, w = shape
            if w == LANES:
                o_ref[...] = acc_ref[r:r + n, :]
                r += n
            elif w < LANES:
                o_ref[...] = acc_ref[r:r + n, 0:w]
                r += SUBLANES
            else:
                for k in range(n):
                    for q in range(w // LANES):
                        o_ref[k:k + 1, q * LANES:(q + 1) * LANES] = acc_ref[r:r + 1, :]
                        r += 1
        assert r == m, (r, m)

    return pl.pallas_call(
        body,
        name=name,
        out_shape=[jax.ShapeDtypeStruct(s, F32) for s in shapes],
        scratch_shapes=[pltpu.VMEM((m, LANES), F32)],
        compiler_params=_params(),
    )(land)


def _row_tile(rows, cap):
    t = cap
    while rows % t:
        t //= 2
    return t


def _add_my_halves(pairs, steps, name):
    k = len(pairs)

    def body(c_ref, *refs):
        for a in range(k):
            o_ref = refs[2 * k + a]
            o_ref[...] = (refs[2 * a][0] + refs[2 * a + 1][...]).astype(o_ref.dtype)

    in_specs, out_specs = [], []
    for g, _ in pairs:
        n_slabs, _, R, C = g.shape
        per = steps // n_slabs
        tr = R // per
        in_specs += [pl.BlockSpec((1, 1, tr, C), lambda p, c_ref, per=per: (p // per, c_ref[0], p % per, 0)),
                     pl.BlockSpec((1, tr, C), lambda p, c_ref, per=per: (p // per, p % per, 0))]
        out_specs.append(pl.BlockSpec((1, tr, C), lambda p, c_ref, per=per: (p // per, p % per, 0)))
    return pl.pallas_call(
        body,
        name=name,
        grid_spec=pltpu.PrefetchScalarGridSpec(num_scalar_prefetch=1, grid=(steps,), in_specs=in_specs, out_specs=out_specs),
        out_shape=[jax.ShapeDtypeStruct(r.shape, jnp.bfloat16) for _, r in pairs],
        compiler_params=_params(("parallel",)),
    )(lax.axis_index("c").reshape(1).astype(jnp.int32), *[a for pair in pairs for a in pair])


def _add_my_half(g, r, name):
    n_slabs, _, R, _ = g.shape
    steps = n_slabs if n_slabs > 1 else R // _row_tile(R, SUM_ROWS)
    return _add_my_halves([(g, r)], steps, name)[0]


def _sum_slabs(own, got, name, deps=()):
    _, R, C = own.shape
    tr = _row_tile(R, SUM_ROWS)

    def body(s_ref, own_ref, got_ref, *rest):
        rest[-1][0] = ((own_ref[0].astype(F32) + got_ref[0].astype(F32)) + got_ref[1].astype(F32)) + got_ref[2].astype(F32)

    xi, yi, ci = _coords()
    return pl.pallas_call(
        body,
        name=name,
        grid_spec=pltpu.PrefetchScalarGridSpec(
            num_scalar_prefetch=1,
            grid=(R // tr,),
            in_specs=[pl.BlockSpec((1, tr, C), lambda i, s: (s[0], i, 0)),
                      pl.BlockSpec((3, tr, C), lambda i, s: (0, i, 0))] + [ANY_SPEC] * len(deps),
            out_specs=pl.BlockSpec((1, tr, C), lambda i, s: (s[1], i, 0)),
        ),
        out_shape=jax.ShapeDtypeStruct((2, R, C), F32),
        compiler_params=_params(("parallel",)),
    )(jnp.stack([2 * xi + yi, ci]).astype(jnp.int32), own, got, *deps)


def _adamw_math(w, g, m, v):
    m = ADAM_B1 * m + (1.0 - ADAM_B1) * g
    v = ADAM_B2 * v + (1.0 - ADAM_B2) * (g * g)
    m_hat = m / (1.0 - ADAM_B1 ** ADAM_STEP)
    v_hat = v / (1.0 - ADAM_B2 ** ADAM_STEP)
    delta = -ADAM_LR * (m_hat / (jnp.sqrt(v_hat) + ADAM_EPS) + ADAM_WD * w)
    return delta, m, v


def _sum_adamw_own(owns, got, dest_sets, w, m, v, name):
    n = len(owns)
    _, R, C = owns[0].shape
    tr = _row_tile(R, ADAMW_ROWS)
    steps = R // tr

    def body(s_ref, *refs):
        got_ref, w_ref, m_ref, v_ref = refs[n:n + 4]
        half_ref, g_ref, d_ref, mn_ref, vn_ref = refs[len(refs) - 5:]
        total = jnp.zeros((tr, C), F32)
        for i in range(n):
            total = total + jnp.where(s_ref[2 + 2 * i] == 1, refs[i][0].astype(F32), 0.0)
        grad = ((total + got_ref[0].astype(F32)) + got_ref[1].astype(F32)) + got_ref[2].astype(F32)
        d, mn, vn = _adamw_math(w_ref[...], grad, m_ref[...], v_ref[...])
        half_ref[0] = grad
        for o, val in zip((g_ref, d_ref, mn_ref, vn_ref), (grad, d, mn, vn)):
            o[...] = val

    xi, yi, ci = _coords()
    me = 2 * xi + yi
    scalars = [ci, ci]
    for dests in dest_sets:
        scalars += [_is_one_of(me, dests).astype(jnp.int32), _slab_of(me, dests)]
    own_spec = lambda i: pl.BlockSpec((1, tr, C), lambda r, s: (s[3 + 2 * i], r, 0))
    rows = pl.BlockSpec((tr, C), lambda r, s: (s[0] * steps + r, 0))
    outs = pl.pallas_call(
        body,
        name=name,
        grid_spec=pltpu.PrefetchScalarGridSpec(
            num_scalar_prefetch=1,
            grid=(steps,),
            in_specs=[own_spec(i) for i in range(n)] + [pl.BlockSpec((3, tr, C), lambda r, s: (0, r, 0))] + [rows] * 3,
            out_specs=[pl.BlockSpec((1, tr, C), lambda r, s: (s[0], r, 0))] + [rows] * 4,
        ),
        out_shape=[jax.ShapeDtypeStruct((2, R, C), F32)] + [jax.ShapeDtypeStruct((2 * R, C), F32)] * 4,
        compiler_params=_params(("parallel",)),
    )(jnp.stack(scalars).astype(jnp.int32), *owns, got, w, m, v)
    return outs[0], list(outs[1:])


def _adamw_halves(ws, g, ms, vs, half, prev, name, deps=()):
    n = len(ws)
    _, _, R, C = g.shape
    tr = _row_tile(R, ADAMW_ROWS)
    steps = R // tr
    carried = [] if prev is None else [a for four in prev for a in four]
    both = half is None
    which = (lambda i, s: i // steps) if both else (lambda i, s: s[0])
    half = 0 if both else half

    def body(s_ref, *refs):
        w_refs, g_refs, m_refs, v_refs = (refs[k * n:(k + 1) * n] for k in range(4))
        outs = refs[len(refs) - 4 * n:]
        for a in range(n):
            grad = g_refs[a][0, 0]
            d, mn, vn = _adamw_math(w_refs[a][...], grad, m_refs[a][...], v_refs[a][...])
            for o, val in zip(outs[4 * a:4 * a + 4], (grad, d, mn, vn)):
                o[...] = val

    rows = pl.BlockSpec((tr, C), lambda i, s: (which(i, s) * steps + i % steps, 0))
    grad_spec = lambda a: pl.BlockSpec((1, 1, tr, C), lambda i, s: (which(i, s), a, i % steps, 0))
    n_in = 4 * n
    outs = pl.pallas_call(
        body,
        name=name,
        grid_spec=pltpu.PrefetchScalarGridSpec(
            num_scalar_prefetch=1,
            grid=(2 * steps if both else steps,),
            in_specs=[rows] * n + [grad_spec(a) for a in range(n)] + [rows] * (2 * n)
            + [ANY_SPEC] * (len(carried) + len(deps)),
            out_specs=[rows] * (4 * n),
        ),
        out_shape=[jax.ShapeDtypeStruct((2 * R, C), F32)] * (4 * n),
        input_output_aliases={1 + n_in + k: k for k in range(len(carried))},
        compiler_params=_params(("parallel",)),
    )(jnp.reshape(half, (1,)).astype(jnp.int32), *ws, *([g] * n), *ms, *vs, *carried, *deps)
    return [outs[4 * a:4 * a + 4] for a in range(n)]


def _adamw_small(ws, gs, ms, vs, name):
    n = len(ws)

    def body(*refs):
        for a in range(n):
            g = refs[n + a][...]
            d, mn, vn = _adamw_math(refs[a][...], g, refs[2 * n + a][...], refs[3 * n + a][...])
            refs[4 * n + a][...] = d
            refs[5 * n + a][...] = mn
            refs[6 * n + a][...] = vn
            refs[7 * n + a][...] = g

    shapes = [jax.ShapeDtypeStruct(w.shape, F32) for w in ws]
    outs = pl.pallas_call(
        body,
        name=name,
        out_shape=shapes * 4,
        compiler_params=_params(),
    )(*ws, *gs, *ms, *vs)
    return outs[:n], outs[n:2 * n], outs[2 * n:3 * n], outs[3 * n:]


def _to_blockdiag(w):
    per = CW // LRU_BW
    w4 = w.reshape(N_CT, per, LRU_BW, LRU_BW)
    eye = jnp.eye(per, dtype=w.dtype)
    return (w4[:, :, :, None, :] * eye[None, :, None, :, None]).reshape(N_CT, CW, CW)


def _local_grads(x2d, tgt2d, B, S, g_in, in_proj, conv_b, gate_x_w, gate_x_b, gate_a_w, gate_a_b, lam,
                 proj_weights, g_fin, reduce):
    wx_bd = _c(_to_blockdiag(gate_x_w))
    wa_bd = _c(_to_blockdiag(gate_a_w))
    tables = _retention_tables(S)

    proj, ht, w_all, conv_w, gain = in_proj(x2d, g_in, (*tables, wx_bd, wa_bd))
    gain3 = gain.reshape(HEADS, 1, DK)
    hlru, ya = _lru_fwd(proj, conv_w, conv_b, wx_bd, wa_bd, gate_x_b, gate_a_b, lam, B, S)
    o_pre, yb, states = _ret_fwd(proj, tables, gain3, B, S)
    wpa, wpb, wout = proj_weights(yb)
    loss, dx2, dya, dyb, dm, dgf, gw_proj = _mid(ya, yb, proj, x2d, tgt2d, wpa, wpb, wout, g_fin)
    g3 = _inproj_bwd_dw(ht, [dm], "inproj_bwd_dw_m")
    deps = reduce.m_ready(gw_proj, g3)
    dr, dgain = _ret_bwd(dyb, o_pre, proj, states, tables, gain3, B, S, deps)
    deps = reduce.ret_done(dr)
    g12 = _inproj_bwd_dw(ht, [dr], "inproj_bwd_dw_r", deps)
    deps = reduce.r_ready(g12)
    dxa, dga, dcw, dcb, dwx, dwa, dbx, dba, dlam = _lru_bwd(
        dya, proj, hlru, conv_w, conv_b, wx_bd, wa_bd, gate_x_b, gate_a_b, lam, B, S, deps)
    small = dict(conv_w=dcw, conv_b=dcb, gate_x_w=dwx, gate_x_b=dbx, gate_a_w=dwa, gate_a_b=dba, lru_lambda=dlam,
                 gn_gain=dgain.reshape(HEADS, DK), norm_final=dgf)
    deps = reduce.lru_done(dxa, _pack_small(small, loss, reduce.slot()))
    g0 = _inproj_bwd_dw(ht, [dxa, dga], "inproj_bwd_dw_a", deps)
    deps = reduce.a_ready(g0)
    n_tiles = x2d.shape[0] // min(DX_TILE, x2d.shape[0])
    grad_x, dgin = _inproj_bwd_dx([dxa, dga, dr, dm], w_all, x2d, dx2, g_in, 0, n_tiles, None, "inproj_bwd_dx", deps)
    return grad_x, dgin


ALL_CHIPS = (0, 1, 2, 3)


class _GradReduce:
    def __init__(self, proj_done):
        self.pending = {}
        self.proj_done = proj_done
        self.land_in = None

    def _start(self, key, parts, name):
        bufs, plans, shared = [], [], None
        for part_bufs, plan, n_copies, part_shared in parts:
            if part_shared is not None:
                shared = len(bufs) + part_shared
            plans.append((plan, len(part_bufs), n_copies))
            bufs += part_bufs
        plan = _join_plans(plans)
        send_sems, recv_sems, bufs, token = _copies_start(bufs, plan, sum(p[2] for p in plans), name + "_start")
        if shared is not None:
            self.land_in = bufs[shared]
        self.pending[key] = (send_sems, recv_sems, bufs, plan, name + "_wait", shared)
        return (token,)

    def _finish(self, key, after):
        send_sems, recv_sems, bufs, plan, name, shared = self.pending.pop(key)
        if shared is not None:
            bufs[shared] = self.land_in
        bufs = _copies_wait(send_sems, recv_sems, bufs, after, plan, name)
        if shared is not None:
            self.land_in = bufs[shared]
        return bufs

    @staticmethod
    def _swap(pieces):
        bufs = []
        for g in pieces:
            bufs += [g, lax.empty((g.shape[0],) + g.shape[2:], F32)]
        n_slabs = [g.shape[0] for g in pieces]
        return bufs, _swap_plan(n_slabs), sum(n_slabs), None

    def _scatter(self, sums, dest_sets):
        bufs = []
        for cs in sums:
            bufs += [cs, lax.empty((3,) + cs.shape[1:], cs.dtype)]
        if self.land_in is not None:
            bufs[-1] = self.land_in
        return bufs, _scatter_plan(dest_sets), 3 * len(sums), len(bufs) - 1

    @staticmethod
    def slot():
        x, y, c = _coords()
        return 4 * x + 2 * y + c

    def _gather8(self, block):
        land = lax.dynamic_update_slice(lax.empty((8,) + block.shape, F32), block[None], (self.slot(), 0, 0))
        return [land], _allgather_plan(), 7, None

    def m_ready(self, gw_proj, g3):
        rows = gw_proj.shape[2] * gw_proj.shape[3]
        return self._start("m", [self._swap([gw_proj.reshape(N_CHIPS, 2, rows, D_MODEL), g3])], "swap_m")

    def ret_done(self, after):
        proj, land_p, g3, land_3 = self._finish("m", after)
        sums_m = _add_my_halves([(proj, land_p), (g3, land_3)], N_CHIPS, "chip_sum_m")
        return self._start("sm", [self._scatter(sums_m, [ALL_CHIPS, (3,)])], "scatter_m")

    def r_ready(self, g12):
        return self._start("r", [self._swap([g12])], "swap_r")

    def lru_done(self, after, packed):
        g12, land_12 = self._finish("r", after)
        sums_r = [_add_my_half(g12, land_12, "chip_sum_r")]
        return (self._start("sr", [self._scatter(sums_r, [(1, 2)])], "scatter_r")
                + self._start("small", [([packed], _allgather_plan(), 7, None)], "gather_small"))

    def a_ready(self, g0):
        (token,) = self._start("a", [self._swap([g0])], "swap_a")
        csp, gotp, self.cs3, _ = self._finish("sm", token)
        half_proj = _sum_slabs(csp, gotp, "sum_w_proj")
        g0, land_0 = self._finish("a", half_proj)
        join = ([half_proj], _join_plan(half_proj.shape[1], PROJ_JOIN_PIECES), PROJ_JOIN_PIECES, None)
        return self._start("sa", [self._scatter([_add_my_half(g0, land_0, "chip_sum_a")], [(0,)]), join], "scatter_a")

    def finish(self, dgin, w_in_own, w_in_done):
        (small,) = self._finish("small", dgin)
        cs12, _ = self._finish("sr", dgin)
        cs0, _, g_proj = self._finish("sa", dgin)
        half_in, first = w_in_own([self.cs3, cs12, cs0], self.land_in, [(3,), (1, 2), (0,)])
        join = ([half_in], _join_plan(half_in.shape[1], JOIN_PIECES), JOIN_PIECES, None)
        deps = self._start("j", [join, self._gather8(dgin)], "join_w_in")
        g_in, norm_in = self._finish("j", self.proj_done(g_proj, deps))
        w_in_done(g_in, first)
        return _unpack_small(small), _sum_gathered(norm_in, [(1, D_MODEL)], "sum_norm_in_grad")[0]


_SMALL = ("gate_x_w", "gate_a_w", "conv_w", "conv_b", "gate_x_b", "gate_a_b", "lru_lambda", "gn_gain", "norm_final")
_SMALL_SHAPES = dict(gate_x_w=(LRU_BLOCKS, LRU_BW, LRU_BW), gate_a_w=(LRU_BLOCKS, LRU_BW, LRU_BW),
                     norm_in=(1, D_MODEL), conv_w=(CONV, D_MODEL), conv_b=(1, D_MODEL), gate_x_b=(1, D_MODEL),
                     gate_a_b=(1, D_MODEL), lru_lambda=(1, D_MODEL), gn_gain=(HEADS, DK), norm_final=(1, D_MODEL))


def _pack_small(small, loss, slot):
    parts = [small[k] if small[k].ndim == 2 else small[k].reshape(-1, LANES) for k in _SMALL]
    m = sum(p.size for p in parts) // LANES + SUBLANES

    def body(s_ref, *refs):
        o_ref = refs[-1]
        r = 0
        for ref, part in zip(refs, parts):
            if part.shape[1] == LANES:
                o_ref[0, r:r + part.shape[0], :] = ref[...]
                r += part.shape[0]
                continue
            for k in range(part.shape[0]):
                for q in range(part.shape[1] // LANES):
                    o_ref[0, r:r + 1, :] = ref[k:k + 1, q * LANES:(q + 1) * LANES]
                    r += 1
        o_ref[0, r:r + SUBLANES, :] = jnp.broadcast_to(refs[len(parts)][...], (SUBLANES, LANES))

    return pl.pallas_call(
        body,
        name="pack_small_grads",
        grid_spec=pltpu.PrefetchScalarGridSpec(
            num_scalar_prefetch=1,
            grid=(1,),
            in_specs=[pl.BlockSpec(p.shape, lambda i, s: (0, 0)) for p in parts] + [pl.BlockSpec((1, 1), lambda i, s: (0, 0))],
            out_specs=pl.BlockSpec((1, m, LANES), lambda i, s: (s[0], 0, 0)),
        ),
        out_shape=jax.ShapeDtypeStruct((8, m, LANES), F32),
        compiler_params=_params(("arbitrary",)),
    )(jnp.reshape(slot, (1,)).astype(jnp.int32), *parts, loss)


def _unpack_small(land):
    *sums, loss = _sum_gathered(land, [_SMALL_SHAPES[k] for k in _SMALL] + [(1, 1)], "sum_small_grads")
    return dict(zip(_SMALL, sums)), loss


def kernel(x, norm_in, w_in, conv_w, conv_b, gate_x_w, gate_x_b, gate_a_w, gate_a_b, lru_lambda, gn_gain, w_proj_a, w_proj_b, w_out, norm_final, loss_target, m_norm_in, m_w_in, m_conv_w, m_conv_b, m_gate_x_w, m_gate_x_b, m_gate_a_w, m_gate_a_b, m_lru_lambda, m_gn_gain, m_w_proj_a, m_w_proj_b, m_w_out, m_norm_final, v_norm_in, v_w_in, v_conv_w, v_conv_b, v_gate_x_w, v_gate_x_b, v_gate_a_w, v_gate_a_b, v_lru_lambda, v_gn_gain, v_w_proj_a, v_w_proj_b, v_w_out, v_norm_final):
    B, S, _ = x.shape
    T = B * S
    xi, yi, ci = _coords()
    chip = 2 * xi + yi

    cshard = D_MODEL // N_CHIPS
    mine = _cast_into_slot([w_in[0].reshape(2, D_MODEL // 2, 2 * D_MODEL)], "cast_w_in")
    plan = _gather_plan(3)
    pending_proj = []
    gshard = DK // N_CHIPS
    tiny = jnp.concatenate([conv_w[0], jnp.zeros((4, cshard), F32), jnp.pad(gn_gain[0], ((0, 4), (0, cshard - gshard)))],
                           axis=0).reshape(1, 2, SUBLANES, cshard)
    tiny_buf = lax.dynamic_update_slice(lax.empty((N_CHIPS, 2, SUBLANES, cshard), F32), tiny, (chip, 0, 0, 0))
    near_plan, pass_plan, far_plan = (_chip_gather_plan(stage, 2) for stage in ("near", "pass", "far"))
    (n_near, _), (n_pass, passed_on), (n_far, _) = (_chip_gather_copies(stage, 2) for stage in ("near", "pass", "far"))
    halves = set(range(n_pass)) - passed_on
    near_s, near_r, bufs, near_token = _copies_start([mine[0], tiny_buf], near_plan, n_near, "gather_near_start")

    def in_proj(x2d, g_in, meanwhile):
        as_w = lambda b: b[0].reshape(N_CHIPS, D_MODEL, 2 * D_MODEL)
        slot_x, slot_y, slot_d = 2 * (1 - xi) + yi, 2 * xi + (1 - yi), 2 * (1 - xi) + (1 - yi)
        ids = lambda *chips: jnp.stack(chips).astype(jnp.int32)
        proj, hb, ht = _inproj_first(x2d, g_in, as_w(bufs), ids(chip), "inproj_own", (near_token, *meanwhile))
        got = _copies_wait(near_s, near_r, bufs, proj, near_plan, "gather_near_wait")
        pass_s, pass_r, got, pass_token = _copies_start(got, pass_plan, n_pass, "gather_pass_start")
        mine_proj = _cast_into_slot([w[0].reshape(2, cshard // 2, D_MODEL) for w in (w_proj_a, w_proj_b, w_out)],
                                    "cast_w_proj", (pass_token,))
        got = _copies_wait(pass_s, pass_r, got, mine_proj[0], pass_plan, "gather_pass_wait_halves", only=halves)
        proj = _inproj_more(hb, as_w(got), ids(slot_x, slot_y), proj, "inproj_near")
        got = _copies_wait(pass_s, pass_r, got, proj, pass_plan, "gather_pass_wait_far", only=passed_on)
        far_s, far_r, got, far_token = _copies_start(got, far_plan, n_far, "gather_far_start")
        pending_proj.append(_copies_start(mine_proj, plan, 9, "gather_proj_start", (far_token,)))
        got = _copies_wait(far_s, far_r, got, pending_proj[0][3], far_plan, "gather_far_wait")
        proj = _inproj_more(hb, as_w(got), ids(slot_d), proj, "inproj_far")
        tiny_all = got[1].reshape(N_CHIPS, 2 * SUBLANES, cshard)
        conv_w_full = jnp.transpose(tiny_all[:, 0:CONV, :], (1, 0, 2)).reshape(CONV, D_MODEL)
        gain_full = jnp.transpose(tiny_all[:, 8:8 + HEADS, :gshard], (1, 0, 2)).reshape(HEADS, DK)
        return proj, ht, as_w(got), conv_w_full, gain_full

    def proj_weights(after):
        s_sems, r_sems, pbufs, _ = pending_proj[0]
        got = _copies_wait(s_sems, r_sems, pbufs, after, plan, "gather_proj_wait")
        return [b.reshape(D_MODEL, D_MODEL) for b in got]

    weights = dict(norm_in=norm_in, w_in=w_in, conv_w=conv_w, conv_b=conv_b, gate_x_w=gate_x_w, gate_x_b=gate_x_b,
                   gate_a_w=gate_a_w, gate_a_b=gate_a_b, lru_lambda=lru_lambda, gn_gain=gn_gain, w_proj_a=w_proj_a,
                   w_proj_b=w_proj_b, w_out=w_out, norm_final=norm_final)
    ms = dict(norm_in=m_norm_in, w_in=m_w_in, conv_w=m_conv_w, conv_b=m_conv_b, gate_x_w=m_gate_x_w,
              gate_x_b=m_gate_x_b, gate_a_w=m_gate_a_w, gate_a_b=m_gate_a_b, lru_lambda=m_lru_lambda, gn_gain=m_gn_gain,
              w_proj_a=m_w_proj_a, w_proj_b=m_w_proj_b, w_out=m_w_out, norm_final=m_norm_final)
    vs = dict(norm_in=v_norm_in, w_in=v_w_in, conv_w=v_conv_w, conv_b=v_conv_b, gate_x_w=v_gate_x_w,
              gate_x_b=v_gate_x_b, gate_a_w=v_gate_a_w, gate_a_b=v_gate_a_b, lru_lambda=v_lru_lambda, gn_gain=v_gn_gain,
              w_proj_a=v_w_proj_a, w_proj_b=v_w_proj_b, w_out=v_w_out, norm_final=v_norm_final)
    names = list(weights)
    grads, delta, new_m, new_v = {}, {}, {}, {}

    def update_big(keys, g, half, prev, name, deps=()):
        two = lambda a: a.reshape(a.shape[1], a.shape[2])
        res = _adamw_halves([two(weights[k]) for k in keys], g, [two(ms[k]) for k in keys], [two(vs[k]) for k in keys],
                            half, prev, name, deps)
        for k, (gk, d, mn, vn) in zip(keys, res):
            shp = weights[k].shape
            grads[k], delta[k], new_m[k], new_v[k] = gk.reshape(shp), d.reshape(shp), mn.reshape(shp), vn.reshape(shp)
        return res

    def proj_done(g_proj, deps):
        g4 = g_proj.reshape(2, 3, D_MODEL // (2 * N_CHIPS), D_MODEL)
        return update_big(("w_proj_a", "w_proj_b", "w_out"), g4, None, None, "adamw_proj", deps)[-1][1]

    def w_in_own(owns, got, dest_sets):
        two = lambda a: a.reshape(a.shape[1], a.shape[2])
        return _sum_adamw_own(owns, got, dest_sets, two(w_in), two(m_w_in), two(v_w_in), "adamw_w_in_own")

    def w_in_done(g_in, prev):
        g4 = g_in.reshape(2, 1, D_MODEL // 2, 2 * D_MODEL)
        return update_big(("w_in",), g4, 1 - ci, [prev], "adamw_w_in_other")[0]

    reduce = _GradReduce(proj_done)
    grad_x, dgin = _local_grads(
        x.reshape(T, D_MODEL), loss_target.reshape(T, D_MODEL), B, S, norm_in, in_proj, conv_b,
        gate_x_w[0], gate_x_b, gate_a_w[0], gate_a_b, lru_lambda, proj_weights,
        norm_final.reshape(1, D_MODEL), reduce)

    (gsm, loss), g_norm_in = reduce.finish(dgin.reshape(SUBLANES, LANES), w_in_own, w_in_done)
    loss = loss[0, 0]
    gsm["norm_in"] = g_norm_in
    gsm["conv_w"] = lax.dynamic_slice_in_dim(gsm["conv_w"], chip * cshard, cshard, axis=1)
    gsm["gn_gain"] = lax.dynamic_slice_in_dim(gsm["gn_gain"], chip * gshard, gshard, axis=1)
    smalls = [k for k in names if k not in delta]

    def view(a):
        return a.reshape(1, -1) if a.ndim == 1 else (a.reshape(a.shape[1:]) if a.ndim > 2 else a)

    ds, mns, vns, gs = _adamw_small([view(weights[k]) for k in smalls], [gsm[k].reshape(view(weights[k]).shape) for k in smalls],
                                [view(ms[k]) for k in smalls], [view(vs[k]) for k in smalls], "adamw_small")
    for k, d, mn, vn, gk in zip(smalls, ds, mns, vns, gs):
        shp = weights[k].shape
        grads[k], delta[k], new_m[k], new_v[k] = gk.reshape(shp), d.reshape(shp), mn.reshape(shp), vn.reshape(shp)

    return (loss, grad_x.reshape(B, S, D_MODEL), *[grads[k] for k in names], *[delta[k] for k in names],
            *[new_m[k] for k in names], *[new_v[k] for k in names])
```
